```python
import math
import jax, jax.numpy as jnp
from jax import lax
import numpy as np

D_MODEL = 1024
BATCH = 16
SEQ = 4096
DEPTH = 2

HEAD_DIM = 64
BRANCH_WIDTH = D_MODEL
N_BRANCH = 3
SSM_INNER = BRANCH_WIDTH
SSM_HEAD_DIM = 64
SSM_HEADS = SSM_INNER // SSM_HEAD_DIM
SSM_GROUPS = 4
SSM_HEADS_PER_GROUP = SSM_HEADS // SSM_GROUPS
SSM_STATE = 128
SSM_CONV_DIM = SSM_INNER + 2 * SSM_GROUPS * SSM_STATE
CONV_WIDTH = 4
SSD_CHUNK = 128
SWA_HEADS = BRANCH_WIDTH // HEAD_DIM
SWA_KV_HEADS = 4
SWA_GROUP = SWA_HEADS // SWA_KV_HEADS
SWA_WINDOW = 128
SWA_BLOCK = 128
FOX_HEADS = BRANCH_WIDTH // HEAD_DIM
FOX_BLOCK = 128
ROPE_THETA = 10000.0
NORM_EPS = 1e-6
IN_SIZES = (SSM_CONV_DIM, SSM_INNER, SSM_HEADS,
            SWA_HEADS * HEAD_DIM, SWA_KV_HEADS * HEAD_DIM, SWA_KV_HEADS * HEAD_DIM, BRANCH_WIDTH,
            FOX_HEADS * HEAD_DIM, FOX_HEADS * HEAD_DIM, FOX_HEADS * HEAD_DIM, FOX_HEADS, BRANCH_WIDTH,
            N_BRANCH * D_MODEL)
N_IN = sum(IN_SIZES)

kernel_name = "hybrid_ssd_swa_fox_gated_block"


def rms_norm(x, w):
    xf = x.astype(jnp.float32)
    y = xf * lax.rsqrt(jnp.mean(xf * xf, axis=-1, keepdims=True) + NORM_EPS)
    return (y * w.astype(jnp.float32)).astype(x.dtype)


def grouped_rms_norm(y, w, groups):
    b, s, d = y.shape
    yf = y.astype(jnp.float32).reshape(b, s, groups, d // groups)
    yf = yf * lax.rsqrt(jnp.mean(yf * yf, axis=-1, keepdims=True) + NORM_EPS)
    return (yf.reshape(b, s, d) * w.astype(jnp.float32)).astype(y.dtype)


def rope(x, cos, sin):
    half = x.shape[-1] // 2
    x1, x2 = x[..., :half], x[..., half:]
    c, s = cos[None, :, None, :], sin[None, :, None, :]
    return jnp.concatenate([x1 * c - x2 * s, x2 * c + x1 * s], axis=-1)


def causal_depthwise_conv(u, w, bias):
    c = u.shape[-1]
    out = lax.conv_general_dilated(u, w[:, None, :], window_strides=(1,),
                                   padding=[(CONV_WIDTH - 1, 0)],
                                   dimension_numbers=('NWC', 'WIO', 'NWC'),
                                   feature_group_count=c)
    return out + bias


def ssd_chunked_scan(xh, dt, a, bm, cm):
    b, s, g, r, p = xh.shape
    n = bm.shape[-1]
    nc, l = s // SSD_CHUNK, SSD_CHUNK
    dtype = xh.dtype
    x = xh.reshape(b, nc, l, g, r, p)
    dtc = dt.reshape(b, nc, l, g, r)
    x_dt = x * dtc[..., None]
    bc = bm.reshape(b, nc, l, g, n)
    cc = cm.reshape(b, nc, l, g, n)
    a_dt = jnp.transpose(dtc.astype(jnp.float32) * a.astype(jnp.float32), (0, 3, 4, 1, 2))
    a_cum = jnp.cumsum(a_dt, axis=-1)
    idx = jnp.arange(l)
    causal = idx[:, None] >= idx[None, :]
    seg = a_cum[..., :, None] - a_cum[..., None, :]
    decay = jnp.where(causal, jnp.exp(jnp.where(causal, seg, 0.0)), 0.0).astype(dtype)
    cb = jnp.einsum('bclgn,bcsgn->bgcls', cc, bc)
    y_diag = jnp.einsum('bgrcls,bcsgrp->bclgrp', cb[:, :, None] * decay, x_dt)
    decay_states = jnp.exp(a_cum[..., -1:] - a_cum).astype(dtype)
    states = jnp.einsum('bcsgn,bgrcs,bcsgrp->bcgrpn', bc, decay_states, x_dt)
    chunk_decay = jnp.exp(a_cum[..., -1]).astype(dtype)

    def step(h, inp):
        st, dec = inp
        return h * dec[..., None, None] + st, h

    h0 = jnp.zeros((b, g, r, p, n), dtype)
    _, prev = lax.scan(step, h0, (jnp.moveaxis(states, 1, 0), jnp.moveaxis(chunk_decay, -1, 0)))
    prev = jnp.moveaxis(prev, 0, 1)
    y_off = jnp.einsum('bclgn,bcgrpn,bgrcl->bclgrp', cc, prev, jnp.exp(a_cum).astype(dtype))
    return (y_diag + y_off).reshape(b, s, g, r, p)


def mamba2_branch(xbc, z, dt_raw, conv_w, conv_b, dt_bias, a_log, d_skip, norm_w):
    b, s, _ = xbc.shape
    xbc = jax.nn.silu(causal_depthwise_conv(xbc, conv_w, conv_b))
    gn = SSM_GROUPS * SSM_STATE
    xs, bm, cm = jnp.split(xbc, [SSM_INNER, SSM_INNER + gn], axis=-1)
    xh = xs.reshape(b, s, SSM_GROUPS, SSM_HEADS_PER_GROUP, SSM_HEAD_DIM)
    dt = jax.nn.softplus(dt_raw + dt_bias).reshape(b, s, SSM_GROUPS, SSM_HEADS_PER_GROUP)
    a = -jnp.exp(a_log).reshape(SSM_GROUPS, SSM_HEADS_PER_GROUP)
    y = ssd_chunked_scan(xh, dt, a,
                         bm.reshape(b, s, SSM_GROUPS, SSM_STATE),
                         cm.reshape(b, s, SSM_GROUPS, SSM_STATE))
    y = y + d_skip.reshape(SSM_GROUPS, SSM_HEADS_PER_GROUP)[:, :, None] * xh
    y = y.reshape(b, s, SSM_INNER) * jax.nn.silu(z)
    return grouped_rms_norm(y, norm_w, SSM_GROUPS)


def sliding_window_branch(q, k, v, z, sinks, cos, sin):
    b, s, _ = q.shape
    nb, blk = s // SWA_BLOCK, SWA_BLOCK
    q = rope(q.reshape(b, s, SWA_HEADS, HEAD_DIM), cos, sin)
    k = rope(k.reshape(b, s, SWA_KV_HEADS, HEAD_DIM), cos, sin)
    v = v.reshape(b, s, SWA_KV_HEADS, HEAD_DIM)
    qb = q.reshape(b, nb, blk, SWA_KV_HEADS, SWA_GROUP, HEAD_DIM)
    pad = ((0, 0), (blk, 0), (0, 0), (0, 0))
    kp, vp = jnp.pad(k, pad)[:, :s], jnp.pad(v, pad)[:, :s]
    kb = jnp.concatenate([kp.reshape(b, nb, blk, SWA_KV_HEADS, HEAD_DIM),
                          k.reshape(b, nb, blk, SWA_KV_HEADS, HEAD_DIM)], axis=2)
    vb = jnp.concatenate([vp.reshape(b, nb, blk, SWA_KV_HEADS, HEAD_DIM),
                          v.reshape(b, nb, blk, SWA_KV_HEADS, HEAD_DIM)], axis=2)
    scores = jnp.einsum('bnqkgd,bnskd->bnkgqs', qb, kb).astype(jnp.float32) * (HEAD_DIM ** -0.5)
    qi = jnp.arange(blk)[:, None]
    sj = jnp.arange(2 * blk)[None, :]
    diff = qi + blk - sj
    band = (diff >= 0) & (diff < SWA_WINDOW)
    key_pos = jnp.arange(nb)[:, None, None] * blk - blk + sj[None]
    mask = band[None] & (key_pos >= 0)
    scores = jnp.where(mask[None, :, None, None], scores, -jnp.inf)
    sink = jnp.broadcast_to(sinks.astype(jnp.float32).reshape(1, 1, SWA_KV_HEADS, SWA_GROUP, 1, 1),
                            scores.shape[:-1] + (1,))
    probs = jax.nn.softmax(jnp.concatenate([scores, sink], axis=-1), axis=-1)[..., :-1]
    out = jnp.einsum('bnkgqs,bnskd->bnqkgd', probs.astype(v.dtype), vb)
    return out.reshape(b, s, SWA_HEADS * HEAD_DIM) * jax.nn.silu(z)


def forgetting_attention_branch(q, k, v, f_logit, z, f_bias):
    b, s, _ = q.shape
    q = q.reshape(b, s, FOX_HEADS, HEAD_DIM)
    k = k.reshape(b, s, FOX_HEADS, HEAD_DIM)
    v = v.reshape(b, s, FOX_HEADS, HEAD_DIM)
    log_f = jax.nn.log_sigmoid((f_logit + f_bias).astype(jnp.float32))
    cum = jnp.transpose(jnp.cumsum(log_f, axis=1), (0, 2, 1))
    scale = HEAD_DIM ** -0.5
    outs = []
    for i in range(s // FOX_BLOCK):
        start, end = i * FOX_BLOCK, (i + 1) * FOX_BLOCK
        sc = jnp.einsum('bqhd,bkhd->bhqk', q[:, start:end], k[:, :end]).astype(jnp.float32) * scale
        sc = sc + cum[:, :, start:end, None] - cum[:, :, None, :end]
        causal = (start + jnp.arange(FOX_BLOCK))[:, None] >= jnp.arange(end)[None, :]
        sc = jnp.where(causal[None, None], sc, -jnp.inf)
        probs = jax.nn.softmax(sc, axis=-1).astype(v.dtype)
        outs.append(jnp.einsum('bhqk,bkhd->bqhd', probs, v[:, :end]))
    out = jnp.concatenate(outs, axis=1)
    return out.reshape(b, s, FOX_HEADS * HEAD_DIM) * jax.nn.silu(z)


def _fwd_setup_inputs(seed: int = 0) -> dict:
    key = jax.random.key(seed)
    ks = jax.random.split(key, 16)
    L, D, W = DEPTH, D_MODEL, BRANCH_WIDTH
    x = jax.random.normal(ks[0], (BATCH, SEQ, D), jnp.float32)
    norm_w = 1.0 + 0.1 * jax.random.normal(ks[1], (L, D), jnp.float32)
    w_in = jax.random.normal(ks[2], (L, D, N_IN), jnp.float32) * D ** -0.5
    conv_w = jax.random.normal(ks[3], (L, CONV_WIDTH, SSM_CONV_DIM), jnp.float32) * CONV_WIDTH ** -0.5
    conv_b = 0.01 * jax.random.normal(ks[4], (L, SSM_CONV_DIM), jnp.float32)
    u = jax.random.uniform(ks[5], (L, SSM_HEADS), jnp.float32)
    dt0 = jnp.exp(u * (math.log(0.1) - math.log(0.001)) + math.log(0.001))
    dt_bias = dt0 + jnp.log(-jnp.expm1(-dt0))
    a_log = jnp.log(jax.random.uniform(ks[6], (L, SSM_HEADS), jnp.float32, minval=1.0, maxval=16.0))
    d_skip = 1.0 + 0.1 * jax.random.normal(ks[7], (L, SSM_HEADS), jnp.float32)
    ssm_norm_w = 1.0 + 0.1 * jax.random.normal(ks[8], (L, SSM_INNER), jnp.float32)
    sinks = 0.5 * jax.random.normal(ks[9], (L, SWA_HEADS), jnp.float32)
    f_bias = 3.0 + 0.5 * jax.random.normal(ks[10], (L, FOX_HEADS), jnp.float32)
    gate_bias = 0.1 * jax.random.normal(ks[11], (L, N_BRANCH, D), jnp.float32)
    w_proj = jax.random.normal(ks[12], (L, N_BRANCH, W, D), jnp.float32) * W ** -0.5
    w_out = jax.random.normal(ks[13], (L, D, D), jnp.float32) * D ** -0.5
    final_norm_w = 1.0 + 0.1 * jax.random.normal(ks[14], (D,), jnp.float32)
    return {"x": x, "norm_w": norm_w, "w_in": w_in, "conv_w": conv_w, "conv_b": conv_b,
            "dt_bias": dt_bias, "a_log": a_log, "d_skip": d_skip, "ssm_norm_w": ssm_norm_w,
            "sinks": sinks, "f_bias": f_bias, "gate_bias": gate_bias, "w_proj": w_proj,
            "w_out": w_out, "final_norm_w": final_norm_w}


def _fwd_reference(x, norm_w, w_in, conv_w, conv_b, dt_bias, a_log, d_skip, ssm_norm_w,
              sinks, f_bias, gate_bias, w_proj, w_out, final_norm_w):
    b, s, d = x.shape
    pos = jnp.arange(s, dtype=jnp.float32)
    inv_freq = ROPE_THETA ** (-jnp.arange(0, HEAD_DIM, 2, dtype=jnp.float32) / HEAD_DIM)
    ang = pos[:, None] * inv_freq[None, :]
    cos, sin = jnp.cos(ang).astype(x.dtype), jnp.sin(ang).astype(x.dtype)
    split_at = [int(v) for v in np.cumsum(IN_SIZES)[:-1]]
    for layer in range(DEPTH):
        h = rms_norm(x, norm_w[layer])
        proj = jnp.einsum('bsd,de->bse', h, w_in[layer])
        (a_xbc, a_z, a_dt, b_q, b_k, b_v, b_z,
         c_q, c_k, c_v, c_f, c_z, gates) = jnp.split(proj, split_at, axis=-1)
        y_a = mamba2_branch(a_xbc, a_z, a_dt, conv_w[layer], conv_b[layer], dt_bias[layer],
                            a_log[layer], d_skip[layer], ssm_norm_w[layer])
        y_b = sliding_window_branch(b_q, b_k, b_v, b_z, sinks[layer], cos, sin)
        y_c = forgetting_attention_branch(c_q, c_k, c_v, c_f, c_z, f_bias[layer])
        ys = jnp.stack([y_a, y_b, y_c], axis=2)
        branch = jnp.einsum('bsiw,iwd->bsid', ys, w_proj[layer])
        g = jax.nn.sigmoid(gates.reshape(b, s, N_BRANCH, d) + gate_bias[layer])
        merged = jnp.sum(g * branch, axis=2)
        x = x + jnp.einsum('bsd,de->bse', merged, w_out[layer])
    return rms_norm(x, final_norm_w)


import jax as _jax
import jax.numpy as _jnp

TWIN_FORMAT = 'train_step'
FWD_PARAMS = ['x', 'norm_w', 'w_in', 'conv_w', 'conv_b', 'dt_bias', 'a_log', 'd_skip', 'ssm_norm_w', 'sinks', 'f_bias', 'gate_bias', 'w_proj', 'w_out', 'final_norm_w']
TWIN_WEIGHTS = ['norm_w', 'w_in', 'conv_w', 'conv_b', 'dt_bias', 'a_log', 'd_skip', 'ssm_norm_w', 'sinks', 'f_bias', 'gate_bias', 'w_proj', 'w_out', 'final_norm_w']
TWIN_DIFF_INPUT = 'x'
TWIN_INPUTS = ['x', 'norm_w', 'w_in', 'conv_w', 'conv_b', 'dt_bias', 'a_log', 'd_skip', 'ssm_norm_w', 'sinks', 'f_bias', 'gate_bias', 'w_proj', 'w_out', 'final_norm_w', 'loss_target', 'm_norm_w', 'm_w_in', 'm_conv_w', 'm_conv_b', 'm_dt_bias', 'm_a_log', 'm_d_skip', 'm_ssm_norm_w', 'm_sinks', 'm_f_bias', 'm_gate_bias', 'm_w_proj', 'm_w_out', 'm_final_norm_w', 'v_norm_w', 'v_w_in', 'v_conv_w', 'v_conv_b', 'v_dt_bias', 'v_a_log', 'v_d_skip', 'v_ssm_norm_w', 'v_sinks', 'v_f_bias', 'v_gate_bias', 'v_w_proj', 'v_w_out', 'v_final_norm_w']
TWIN_OUTPUTS = ['loss', 'grad_x', 'grad_norm_w', 'grad_w_in', 'grad_conv_w', 'grad_conv_b', 'grad_dt_bias', 'grad_a_log', 'grad_d_skip', 'grad_ssm_norm_w', 'grad_sinks', 'grad_f_bias', 'grad_gate_bias', 'grad_w_proj', 'grad_w_out', 'grad_final_norm_w', 'delta_norm_w', 'delta_w_in', 'delta_conv_w', 'delta_conv_b', 'delta_dt_bias', 'delta_a_log', 'delta_d_skip', 'delta_ssm_norm_w', 'delta_sinks', 'delta_f_bias', 'delta_gate_bias', 'delta_w_proj', 'delta_w_out', 'delta_final_norm_w', 'new_m_norm_w', 'new_m_w_in', 'new_m_conv_w', 'new_m_conv_b', 'new_m_dt_bias', 'new_m_a_log', 'new_m_d_skip', 'new_m_ssm_norm_w', 'new_m_sinks', 'new_m_f_bias', 'new_m_gate_bias', 'new_m_w_proj', 'new_m_w_out', 'new_m_final_norm_w', 'new_v_norm_w', 'new_v_w_in', 'new_v_conv_w', 'new_v_conv_b', 'new_v_dt_bias', 'new_v_a_log', 'new_v_d_skip', 'new_v_ssm_norm_w', 'new_v_sinks', 'new_v_f_bias', 'new_v_gate_bias', 'new_v_w_proj', 'new_v_w_out', 'new_v_final_norm_w']
TWIN_LEAF_KINDS = {'loss': 'loss', 'grad_x': 'grad_x', 'grad_norm_w': 'grad_w', 'grad_w_in': 'grad_w', 'grad_conv_w': 'grad_w', 'grad_conv_b': 'grad_w', 'grad_dt_bias': 'grad_w', 'grad_a_log': 'grad_w', 'grad_d_skip': 'grad_w', 'grad_ssm_norm_w': 'grad_w', 'grad_sinks': 'grad_w', 'grad_f_bias': 'grad_w', 'grad_gate_bias': 'grad_w', 'grad_w_proj': 'grad_w', 'grad_w_out': 'grad_w', 'grad_final_norm_w': 'grad_w', 'delta_norm_w': 'delta_w', 'delta_w_in': 'delta_w', 'delta_conv_w': 'delta_w', 'delta_conv_b': 'delta_w', 'delta_dt_bias': 'delta_w', 'delta_a_log': 'delta_w', 'delta_d_skip': 'delta_w', 'delta_ssm_norm_w': 'delta_w', 'delta_sinks': 'delta_w', 'delta_f_bias': 'delta_w', 'delta_gate_bias': 'delta_w', 'delta_w_proj': 'delta_w', 'delta_w_out': 'delta_w', 'delta_final_norm_w': 'delta_w', 'new_m_norm_w': 'new_m', 'new_m_w_in': 'new_m', 'new_m_conv_w': 'new_m', 'new_m_conv_b': 'new_m', 'new_m_dt_bias': 'new_m', 'new_m_a_log': 'new_m', 'new_m_d_skip': 'new_m', 'new_m_ssm_norm_w': 'new_m', 'new_m_sinks': 'new_m', 'new_m_f_bias': 'new_m', 'new_m_gate_bias': 'new_m', 'new_m_w_proj': 'new_m', 'new_m_w_out': 'new_m', 'new_m_final_norm_w': 'new_m', 'new_v_norm_w': 'new_v', 'new_v_w_in': 'new_v', 'new_v_conv_w': 'new_v', 'new_v_conv_b': 'new_v', 'new_v_dt_bias': 'new_v', 'new_v_a_log': 'new_v', 'new_v_d_skip': 'new_v', 'new_v_ssm_norm_w': 'new_v', 'new_v_sinks': 'new_v', 'new_v_f_bias': 'new_v', 'new_v_gate_bias': 'new_v', 'new_v_w_proj': 'new_v', 'new_v_w_out': 'new_v', 'new_v_final_norm_w': 'new_v'}


def _forward(args):
    return _fwd_reference(*[args[k] for k in FWD_PARAMS])


def _output_shape():
    out = _jax.eval_shape(lambda: _forward(_fwd_setup_inputs(0)))
    return out.shape, out.dtype

N_MICROBATCH = 1
ADAM_LR = 0.001
ADAM_B1 = 0.9
ADAM_B2 = 0.999
ADAM_EPS = 1e-08
ADAM_WD = 0.01
ADAM_STEP = 10
PER_EXAMPLE_BATCH_AXIS = {'x': 0, 'loss_target': 0}
SHARED_INPUTS = []
_WEIGHT_DTYPES = {'norm_w': _jnp.float32, 'w_in': _jnp.float32, 'conv_w': _jnp.float32, 'conv_b': _jnp.float32, 'dt_bias': _jnp.float32, 'a_log': _jnp.float32, 'd_skip': _jnp.float32, 'ssm_norm_w': _jnp.float32, 'sinks': _jnp.float32, 'f_bias': _jnp.float32, 'gate_bias': _jnp.float32, 'w_proj': _jnp.float32, 'w_out': _jnp.float32, 'final_norm_w': _jnp.float32}
MOMENT_SCALE = {'norm_w': 2.015450e-01, 'w_in': 5.764424e-02, 'conv_w': 9.673484e-02, 'conv_b': 1.365550e-01, 'dt_bias': 2.912411e-01, 'a_log': 5.409992e-01, 'd_skip': 8.556686e-01, 'ssm_norm_w': 1.308874e-01, 'sinks': 1.331509e-02, 'f_bias': 1.056287e-01, 'gate_bias': 3.395263e-02, 'w_proj': 8.104556e-02, 'w_out': 1.444398e-01, 'final_norm_w': 6.442581e+01}


def _to_microbatches(a, axis):
    t = _jnp.moveaxis(a, axis, 0)
    t = t.reshape((N_MICROBATCH, t.shape[0] // N_MICROBATCH) + t.shape[1:])
    return _jnp.moveaxis(t, 1, axis + 1)


def setup_inputs(seed: int = 0) -> dict:
    inp = _fwd_setup_inputs(seed)
    key = _jax.random.fold_in(_jax.random.key(seed), 7919)
    shape, _ = _output_shape()
    out = dict(inp)
    out["loss_target"] = _jax.random.normal(_jax.random.fold_in(key, 0), shape, _jnp.float32)
    for i, name in enumerate(TWIN_WEIGHTS):
        w = inp[name].astype(_jnp.float32)
        if MOMENT_SCALE is None:
            s = _jnp.sqrt(_jnp.mean(_jnp.square(w)) + 1e-30)
        else:
            s = MOMENT_SCALE[name]
        km, kv = _jax.random.split(_jax.random.fold_in(key, i + 1))
        out[name] = w
        out["m_" + name] = s * _jax.random.normal(km, w.shape, _jnp.float32)
        out["v_" + name] = (s * s) * _jax.random.uniform(kv, w.shape, _jnp.float32, 0.5, 1.5)
    if N_MICROBATCH > 1:
        for name, axis in PER_EXAMPLE_BATCH_AXIS.items():
            out[name] = _to_microbatches(out[name], axis)
    return {'x': out['x'], 'norm_w': out['norm_w'], 'w_in': out['w_in'], 'conv_w': out['conv_w'], 'conv_b': out['conv_b'], 'dt_bias': out['dt_bias'], 'a_log': out['a_log'], 'd_skip': out['d_skip'], 'ssm_norm_w': out['ssm_norm_w'], 'sinks': out['sinks'], 'f_bias': out['f_bias'], 'gate_bias': out['gate_bias'], 'w_proj': out['w_proj'], 'w_out': out['w_out'], 'final_norm_w': out['final_norm_w'], 'loss_target': out['loss_target'], 'm_norm_w': out['m_norm_w'], 'm_w_in': out['m_w_in'], 'm_conv_w': out['m_conv_w'], 'm_conv_b': out['m_conv_b'], 'm_dt_bias': out['m_dt_bias'], 'm_a_log': out['m_a_log'], 'm_d_skip': out['m_d_skip'], 'm_ssm_norm_w': out['m_ssm_norm_w'], 'm_sinks': out['m_sinks'], 'm_f_bias': out['m_f_bias'], 'm_gate_bias': out['m_gate_bias'], 'm_w_proj': out['m_w_proj'], 'm_w_out': out['m_w_out'], 'm_final_norm_w': out['m_final_norm_w'], 'v_norm_w': out['v_norm_w'], 'v_w_in': out['v_w_in'], 'v_conv_w': out['v_conv_w'], 'v_conv_b': out['v_conv_b'], 'v_dt_bias': out['v_dt_bias'], 'v_a_log': out['v_a_log'], 'v_d_skip': out['v_d_skip'], 'v_ssm_norm_w': out['v_ssm_norm_w'], 'v_sinks': out['v_sinks'], 'v_f_bias': out['v_f_bias'], 'v_gate_bias': out['v_gate_bias'], 'v_w_proj': out['v_w_proj'], 'v_w_out': out['v_w_out'], 'v_final_norm_w': out['v_final_norm_w']}


def _loss(weights, diff, rest, loss_target):
    with _jax.named_scope("forward"):
        args = {**rest, TWIN_DIFF_INPUT: diff, **{k: w.astype(_WEIGHT_DTYPES[k]) for k, w in weights.items()}}
        y = _forward(args)
    with _jax.named_scope("loss_head"):
        err = _jnp.square(y.astype(_jnp.float32) - loss_target)
        return 0.5 * _jnp.sum(_jnp.mean(err, axis=-1)) if err.ndim else 0.5 * err


def _adamw(w, g, m, v):
    m = ADAM_B1 * m + (1.0 - ADAM_B1) * g
    v = ADAM_B2 * v + (1.0 - ADAM_B2) * _jnp.square(g)
    m_hat = m / (1.0 - ADAM_B1 ** ADAM_STEP)
    v_hat = v / (1.0 - ADAM_B2 ** ADAM_STEP)
    delta = -ADAM_LR * (m_hat / (_jnp.sqrt(v_hat) + ADAM_EPS) + ADAM_WD * w)
    return delta, m, v


def reference(x, norm_w, w_in, conv_w, conv_b, dt_bias, a_log, d_skip, ssm_norm_w, sinks, f_bias, gate_bias, w_proj, w_out, final_norm_w, loss_target, m_norm_w, m_w_in, m_conv_w, m_conv_b, m_dt_bias, m_a_log, m_d_skip, m_ssm_norm_w, m_sinks, m_f_bias, m_gate_bias, m_w_proj, m_w_out, m_final_norm_w, v_norm_w, v_w_in, v_conv_w, v_conv_b, v_dt_bias, v_a_log, v_d_skip, v_ssm_norm_w, v_sinks, v_f_bias, v_gate_bias, v_w_proj, v_w_out, v_final_norm_w):
    given = dict(x=x, norm_w=norm_w, w_in=w_in, conv_w=conv_w, conv_b=conv_b, dt_bias=dt_bias, a_log=a_log, d_skip=d_skip, ssm_norm_w=ssm_norm_w, sinks=sinks, f_bias=f_bias, gate_bias=gate_bias, w_proj=w_proj, w_out=w_out, final_norm_w=final_norm_w, loss_target=loss_target, m_norm_w=m_norm_w, m_w_in=m_w_in, m_conv_w=m_conv_w, m_conv_b=m_conv_b, m_dt_bias=m_dt_bias, m_a_log=m_a_log, m_d_skip=m_d_skip, m_ssm_norm_w=m_ssm_norm_w, m_sinks=m_sinks, m_f_bias=m_f_bias, m_gate_bias=m_gate_bias, m_w_proj=m_w_proj, m_w_out=m_w_out, m_final_norm_w=m_final_norm_w, v_norm_w=v_norm_w, v_w_in=v_w_in, v_conv_w=v_conv_w, v_conv_b=v_conv_b, v_dt_bias=v_dt_bias, v_a_log=v_a_log, v_d_skip=v_d_skip, v_ssm_norm_w=v_ssm_norm_w, v_sinks=v_sinks, v_f_bias=v_f_bias, v_gate_bias=v_gate_bias, v_w_proj=v_w_proj, v_w_out=v_w_out, v_final_norm_w=v_final_norm_w)
    weights = {n: given[n] for n in TWIN_WEIGHTS}
    shared = {n: given[n] for n in SHARED_INPUTS}
    per_example = {n: given[n] for n in ['x']}
    grad_fn = _jax.value_and_grad(_loss, argnums=(0, 1))

    def one_microbatch(ex, loss_target):
        ex = dict(ex)
        diff = ex.pop(TWIN_DIFF_INPUT)
        return grad_fn(weights, diff, {**shared, **ex}, loss_target)

    if N_MICROBATCH == 1:
        loss, (grad_w, grad_x) = one_microbatch(per_example, given["loss_target"])
    else:
        def body(carry, xs):
            loss_sum, grad_sum = carry
            l_k, (gw_k, gx_k) = one_microbatch(xs[0], xs[1])
            with _jax.named_scope("update"):
                return (loss_sum + l_k, _jax.tree.map(_jnp.add, grad_sum, gw_k)), gx_k

        init = (_jnp.zeros((), _jnp.float32), _jax.tree.map(_jnp.zeros_like, weights))
        (loss, grad_w), grad_x = _jax.lax.scan(body, init, (per_example, given["loss_target"]))
    with _jax.named_scope("update"):
        delta_w, new_m, new_v = {}, {}, {}
        for n in TWIN_WEIGHTS:
            delta_w[n], new_m[n], new_v[n] = _adamw(weights[n], grad_w[n], given["m_" + n], given["v_" + n])
    return (loss, grad_x, *[grad_w[n] for n in TWIN_WEIGHTS], *[delta_w[n] for n in TWIN_WEIGHTS],
            *[new_m[n] for n in TWIN_WEIGHTS], *[new_v[n] for n in TWIN_WEIGHTS])
```

```python
import functools
import math

import numpy as np
import jax
import jax.numpy as jnp
from jax import lax
from jax.experimental import pallas as pl
from jax.experimental.pallas import tpu as pltpu

F32 = jnp.float32
BF16 = jnp.bfloat16
HIGHEST = lax.Precision.HIGHEST
MESH = pl.DeviceIdType.MESH

D_MODEL = 1024
HEAD_DIM = 64
N_HEADS = 16
N_GROUPS = 4
SSM_STATE = 128
CHUNK = 128
CONV_WIDTH = 4
CONV_DIM = 2048
ROPE_THETA = 10000.0
NORM_EPS = 1e-6
LANES = 128
N_CHIPS = 4
N_DEV = 8

ADAM_LR = 0.001
ADAM_B1 = 0.9
ADAM_B2 = 0.999
ADAM_EPS = 1e-08
ADAM_WD = 0.01
ADAM_STEP = 10

_REF_COLS = {}
_off = 0
for _n, _s in (("xbc", 2048), ("a_z", 1024), ("a_dt", 16), ("b_q", 1024), ("b_k", 256), ("b_v", 256),
               ("b_z", 1024), ("c_q", 1024), ("c_k", 1024), ("c_v", 1024), ("c_f", 16), ("c_z", 1024),
               ("gates", 3072)):
    _REF_COLS[_n] = (_off, _s)
    _off += _s
N_IN = _off

_PAD_ORDER = (("gates", 3072), ("xbc", 2048), ("a_z", 1024), ("b_q", 1024), ("b_z", 1024), ("c_q", 1024),
              ("c_k", 1024), ("c_v", 1024), ("c_z", 1024), ("b_k", 256), ("b_v", 256), ("a_dt", 512),
              ("c_f", 128))
_PAD_COLS = {}
_off = 0
for _n, _s in _PAD_ORDER:
    _PAD_COLS[_n] = (_off, _s)
    _off += _s
N_USED = _off
N_PAD = 13824


def _cp(sem, vmem_mb=48):
    return pltpu.CompilerParams(dimension_semantics=sem, vmem_limit_bytes=vmem_mb * 1024 * 1024)


def _dot(a, b, dims=((1,), (0,)), precision=None):
    return lax.dot_general(a, b, (dims, ((), ())), preferred_element_type=F32, precision=precision)


def _dot_nt(a, b):
    return _dot(a, b, ((1,), (1,)))


def _dot_tn(a, b):
    return _dot(a, b, ((0,), (0,)))


def _col(v, idx):
    lane = lax.broadcasted_iota(jnp.int32, v.shape, 1)
    return jnp.sum(jnp.where(lane == idx, v, 0.0), axis=1, keepdims=True)


def _row(v, idx):
    row = lax.broadcasted_iota(jnp.int32, v.shape, 0)
    return jnp.sum(jnp.where(row == idx, v, 0.0), axis=0, keepdims=True)


def _iota_col():
    return lax.broadcasted_iota(jnp.int32, (CHUNK, 1), 0)


def _iota_row():
    return lax.broadcasted_iota(jnp.int32, (1, LANES), 1)


def _sigmoid(x):
    return 1.0 / (1.0 + jnp.exp(-x))


def _softplus(x):
    return jnp.maximum(x, 0.0) + jnp.log(1.0 + jnp.exp(-jnp.abs(x)))


def _pad_w_in(w):
    parts = []
    for name, size in _PAD_ORDER:
        s0, sz = _REF_COLS[name]
        seg = w[:, s0:s0 + sz]
        if name == "a_dt":
            seg = jnp.pad(seg.reshape(-1, N_GROUPS, 4), ((0, 0), (0, 0), (0, LANES - 4))).reshape(-1, 512)
        elif name == "c_f":
            seg = jnp.pad(seg, ((0, 0), (0, LANES - 16)))
        parts.append(seg)
    parts.append(jnp.zeros((w.shape[0], N_PAD - N_USED), w.dtype))
    return jnp.concatenate(parts, axis=1)


def _unpad_w_in(wp):
    segs = {}
    for name, _ in _PAD_ORDER:
        p0, psz = _PAD_COLS[name]
        seg = wp[:, p0:p0 + psz]
        if name == "a_dt":
            seg = seg.reshape(-1, N_GROUPS, LANES)[:, :, :4].reshape(-1, 16)
        elif name == "c_f":
            seg = seg[:, :16]
        segs[name] = seg
    order = sorted(_REF_COLS, key=lambda n: _REF_COLS[n][0])
    return jnp.concatenate([segs[n] for n in order], axis=1)


def _group_lanes(v):
    return jnp.pad(v.reshape(N_GROUPS, 1, 4), ((0, 0), (0, 0), (0, LANES - 4)))


def _ungroup_lanes(v):
    return v[:, 0, :4].reshape(16)


def _mm(a, b, *, ta=False, tb=False, tm=512, tn=512, tk=512, out_dtype=F32, add=None, name):
    if ta:
        kdim, m = a.shape
    else:
        m, kdim = a.shape
    if tb:
        n, k2 = b.shape
    else:
        k2, n = b.shape
    assert kdim == k2, (a.shape, b.shape)
    tm, tn, tk = min(tm, m), min(tn, n), min(tk, kdim)
    assert m % tm == 0 and n % tn == 0 and kdim % tk == 0, (m, n, kdim, tm, tn, tk)
    nk = kdim // tk
    a_spec = (pl.BlockSpec((tk, tm), lambda i, j, k: (k, i)) if ta
              else pl.BlockSpec((tm, tk), lambda i, j, k: (i, k)))
    b_spec = (pl.BlockSpec((tn, tk), lambda i, j, k: (j, k)) if tb
              else pl.BlockSpec((tk, tn), lambda i, j, k: (k, j)))
    dims = ((0 if ta else 1,), (1 if tb else 0,))
    has_add = add is not None

    def body(*refs):
        if has_add:
            a_ref, b_ref, add_ref, o_ref, acc_ref = refs
        else:
            a_ref, b_ref, o_ref, acc_ref = refs
        k = pl.program_id(2)
        p = _dot(a_ref[...].astype(BF16), b_ref[...].astype(BF16), dims)

        @pl.when(k == 0)
        def _():
            acc_ref[...] = p

        @pl.when(k > 0)
        def _():
            acc_ref[...] += p

        @pl.when(k == nk - 1)
        def _():
            r = acc_ref[...]
            if has_add:
                r = r + add_ref[...]
            o_ref[...] = r.astype(out_dtype)

    in_specs = [a_spec, b_spec]
    args = [a, b]
    if has_add:
        in_specs.append(pl.BlockSpec((tm, tn), lambda i, j, k: (i, j)))
        args.append(add)
    return pl.pallas_call(
        body, name=name, grid=(m // tm, n // tn, nk),
        in_specs=in_specs, out_specs=pl.BlockSpec((tm, tn), lambda i, j, k: (i, j)),
        out_shape=jax.ShapeDtypeStruct((m, n), out_dtype),
        scratch_shapes=[pltpu.VMEM((tm, tn), F32)],
        compiler_params=_cp(("parallel", "parallel", "arbitrary")),
    )(*args)


def _rms_fwd(x, w, *, name, tm=512):
    t, d = x.shape

    def body(x_ref, w_ref, o_ref):
        xv = x_ref[...]
        r = lax.rsqrt(jnp.mean(xv * xv, axis=1, keepdims=True) + NORM_EPS)
        o_ref[...] = (xv * r * w_ref[...]).astype(BF16)

    return pl.pallas_call(
        body, name=name, grid=(t // tm,),
        in_specs=[pl.BlockSpec((tm, d), lambda i: (i, 0)), pl.BlockSpec((1, d), lambda i: (0, 0))],
        out_specs=pl.BlockSpec((tm, d), lambda i: (i, 0)),
        out_shape=jax.ShapeDtypeStruct((t, d), BF16),
        compiler_params=_cp(("parallel",)),
    )(x, w.reshape(1, d))


def _rms_bwd(x, w, dh, dres, *, name, tm=512):
    t, d = x.shape

    def body(x_ref, w_ref, dh_ref, dres_ref, dx_ref, dw_ref):
        xv = x_ref[...]
        r = lax.rsqrt(jnp.mean(xv * xv, axis=1, keepdims=True) + NORM_EPS)
        xhat = xv * r
        dhv = dh_ref[...]
        dxhat = dhv * w_ref[...]
        dx = r * (dxhat - xhat * jnp.mean(dxhat * xhat, axis=1, keepdims=True))
        dx_ref[...] = dres_ref[...] + dx

        @pl.when(pl.program_id(0) == 0)
        def _():
            dw_ref[...] = jnp.zeros_like(dw_ref)

        dw_ref[...] += jnp.sum(dhv * xhat, axis=0, keepdims=True)

    return pl.pallas_call(
        body, name=name, grid=(t // tm,),
        in_specs=[pl.BlockSpec((tm, d), lambda i: (i, 0)), pl.BlockSpec((1, d), lambda i: (0, 0)),
                  pl.BlockSpec((tm, d), lambda i: (i, 0)), pl.BlockSpec((tm, d), lambda i: (i, 0))],
        out_specs=[pl.BlockSpec((tm, d), lambda i: (i, 0)), pl.BlockSpec((1, d), lambda i: (0, 0))],
        out_shape=[jax.ShapeDtypeStruct((t, d), F32), jax.ShapeDtypeStruct((1, d), F32)],
        compiler_params=_cp(("arbitrary",)),
    )(x, w.reshape(1, d), dh, dres)


def _final_loss(x, w, target, *, name, tm=512):
    t, d = x.shape

    def body(x_ref, w_ref, t_ref, loss_ref, dx_ref, dw_ref):
        xv = x_ref[...]
        wv = w_ref[...]
        r = lax.rsqrt(jnp.mean(xv * xv, axis=1, keepdims=True) + NORM_EPS)
        xhat = xv * r
        err = xhat * wv - t_ref[...]
        dy = err * (1.0 / d)
        dxhat = dy * wv
        dx_ref[...] = r * (dxhat - xhat * jnp.mean(dxhat * xhat, axis=1, keepdims=True))

        @pl.when(pl.program_id(0) == 0)
        def _():
            dw_ref[...] = jnp.zeros_like(dw_ref)
            loss_ref[...] = jnp.zeros_like(loss_ref)

        dw_ref[...] += jnp.sum(dy * xhat, axis=0, keepdims=True)
        part = 0.5 * jnp.sum(jnp.mean(err * err, axis=1, keepdims=True), axis=0, keepdims=True)
        loss_ref[...] += jnp.broadcast_to(part, loss_ref.shape)

    return pl.pallas_call(
        body, name=name, grid=(t // tm,),
        in_specs=[pl.BlockSpec((tm, d), lambda i: (i, 0)), pl.BlockSpec((1, d), lambda i: (0, 0)),
                  pl.BlockSpec((tm, d), lambda i: (i, 0))],
        out_specs=[pl.BlockSpec((8, LANES), lambda i: (0, 0)), pl.BlockSpec((tm, d), lambda i: (i, 0)),
                   pl.BlockSpec((1, d), lambda i: (0, 0))],
        out_shape=[jax.ShapeDtypeStruct((8, LANES), F32), jax.ShapeDtypeStruct((t, d), F32),
                   jax.ShapeDtypeStruct((1, d), F32)],
        compiler_params=_cp(("arbitrary",)),
    )(x, w.reshape(1, d), target)


_CB = 128


def _conv_pre(u, w_ref, b_ref):
    s = u.shape[0]
    row = lax.broadcasted_iota(jnp.int32, u.shape, 0)
    pre = b_ref[...] + w_ref[CONV_WIDTH - 1:CONV_WIDTH, :] * u
    for sh in range(1, CONV_WIDTH):
        shifted = jnp.where(row >= sh, pltpu.roll(u, sh, 0), 0.0)
        pre = pre + w_ref[CONV_WIDTH - 1 - sh:CONV_WIDTH - sh, :] * shifted
    return pre


def _conv_fwd(proj3, cw, cb, *, name):
    b, s, _ = proj3.shape
    c0 = _PAD_COLS["xbc"][0] // _CB

    def body(u_ref, w_ref, b_ref, o_ref):
        pre = _conv_pre(u_ref[...], w_ref, b_ref)
        o_ref[...] = pre * _sigmoid(pre)

    return pl.pallas_call(
        body, name=name, grid=(b, CONV_DIM // _CB),
        in_specs=[pl.BlockSpec((None, s, _CB), lambda i, j: (i, 0, c0 + j)),
                  pl.BlockSpec((CONV_WIDTH, _CB), lambda i, j: (0, j)),
                  pl.BlockSpec((1, _CB), lambda i, j: (0, j))],
        out_specs=pl.BlockSpec((None, s, _CB), lambda i, j: (i, 0, j)),
        out_shape=jax.ShapeDtypeStruct((b, s, CONV_DIM), F32),
        compiler_params=_cp(("parallel", "parallel")),
    )(proj3, cw, cb.reshape(1, CONV_DIM))


def _conv_bwd(proj3, cw, cb, dact, *, name):
    b, s, _ = proj3.shape
    c0 = _PAD_COLS["xbc"][0] // _CB

    def body(u_ref, w_ref, b_ref, da_ref, du_ref, dwb_ref):
        u = u_ref[...]
        pre = _conv_pre(u, w_ref, b_ref)
        sg = _sigmoid(pre)
        dpre = da_ref[...] * (sg * (1.0 + pre * (1.0 - sg)))
        row = lax.broadcasted_iota(jnp.int32, u.shape, 0)
        du = w_ref[CONV_WIDTH - 1:CONV_WIDTH, :] * dpre
        rows = [jnp.sum(dpre * u, axis=0, keepdims=True)]
        for sh in range(1, CONV_WIDTH):
            fwd_shift = jnp.where(row < s - sh, pltpu.roll(dpre, s - sh, 0), 0.0)
            du = du + w_ref[CONV_WIDTH - 1 - sh:CONV_WIDTH - sh, :] * fwd_shift
            ush = jnp.where(row >= sh, pltpu.roll(u, sh, 0), 0.0)
            rows.append(jnp.sum(dpre * ush, axis=0, keepdims=True))
        du_ref[...] = du

        @pl.when(pl.program_id(1) == 0)
        def _():
            dwb_ref[...] = jnp.zeros_like(dwb_ref)

        for sh in range(CONV_WIDTH):
            k = CONV_WIDTH - 1 - sh
            dwb_ref[k:k + 1, :] += rows[sh]
        dwb_ref[CONV_WIDTH:CONV_WIDTH + 1, :] += jnp.sum(dpre, axis=0, keepdims=True)

    return pl.pallas_call(
        body, name=name, grid=(CONV_DIM // _CB, b),
        in_specs=[pl.BlockSpec((None, s, _CB), lambda j, i: (i, 0, c0 + j)),
                  pl.BlockSpec((CONV_WIDTH, _CB), lambda j, i: (0, j)),
                  pl.BlockSpec((1, _CB), lambda j, i: (0, j)),
                  pl.BlockSpec((None, s, _CB), lambda j, i: (i, 0, j))],
        out_specs=[pl.BlockSpec((None, s, _CB), lambda j, i: (i, 0, j)),
                   pl.BlockSpec((8, _CB), lambda j, i: (0, j))],
        out_shape=[jax.ShapeDtypeStruct((b, s, CONV_DIM), F32), jax.ShapeDtypeStruct((8, CONV_DIM), F32)],
        compiler_params=_cp(("parallel", "arbitrary")),
    )(proj3, cw, cb.reshape(1, CONV_DIM), dact)


def _ssd_common(dt_ref, dtb_ref, alog_ref):
    row = lax.broadcasted_iota(jnp.int32, (CHUNK, CHUNK), 0)
    lane = lax.broadcasted_iota(jnp.int32, (CHUNK, CHUNK), 1)
    causal = row >= lane
    tri = causal.astype(F32)
    dtv = _softplus(dt_ref[...] + dtb_ref[...])
    a_row = -jnp.exp(alog_ref[...])
    acum = _dot(tri, dtv * a_row, precision=HIGHEST)
    return row, lane, causal, dtv, a_row, acum, acum.T


def _ssd_pair(pp, x, dtv, acum, acum_t, causal, lane, row):
    lo = lane < HEAD_DIM
    r0, r1 = 2 * pp, 2 * pp + 1
    dtp = jnp.where(lo, _col(dtv, r0), _col(dtv, r1))
    ac0, ac1 = _col(acum, r0), _col(acum, r1)
    ar0, ar1 = _row(acum_t, r0), _row(acum_t, r1)
    d0 = jnp.where(causal, jnp.exp(jnp.where(causal, ac0 - ar0, 0.0)), 0.0)
    d1 = jnp.where(causal, jnp.exp(jnp.where(causal, ac1 - ar1, 0.0)), 0.0)
    al0, al1 = _col(ar0, CHUNK - 1), _col(ar1, CHUNK - 1)
    eac = jnp.where(lo, jnp.exp(ac0), jnp.exp(ac1))
    dsp = jnp.where(lo, jnp.exp(al0 - ac0), jnp.exp(al1 - ac1))
    eal = jnp.where(_iota_col() < HEAD_DIM, jnp.exp(al0), jnp.exp(al1))
    return lo, dtp, x * dtp, d0, d1, al0, al1, eac, dsp, eal


def _ssd_fwd(proj3, xact3, dtb, alog, dsk, nw, *, name):
    b, s, _ = proj3.shape
    nc = s // CHUNK
    dt0 = _PAD_COLS["a_dt"][0] // LANES
    z0 = _PAD_COLS["a_z"][0] // 256

    def body(xs_ref, bm_ref, cm_ref, dt_ref, z_ref, dtb_ref, alog_ref, dsk_ref, nw_ref,
             ya_ref, ypre_ref, hst_ref, h_scr):
        @pl.when(pl.program_id(2) == 0)
        def _():
            h_scr[...] = jnp.zeros_like(h_scr)

        row, lane, causal, dtv, a_row, acum, acum_t = _ssd_common(dt_ref, dtb_ref, alog_ref)
        bb = bm_ref[...].astype(BF16)
        cb = cm_ref[...].astype(BF16)
        cbm = _dot_nt(cb, bb)
        hst_ref[...] = h_scr[...]
        dskv = dsk_ref[...]
        for pp in range(2):
            x = xs_ref[:, LANES * pp:LANES * (pp + 1)]
            lo, dtp, xd, d0, d1, al0, al1, eac, dsp, eal = _ssd_pair(pp, x, dtv, acum, acum_t, causal, lane, row)
            xdb = xd.astype(BF16)
            y = jnp.where(lo, _dot((cbm * d0).astype(BF16), xdb), _dot((cbm * d1).astype(BF16), xdb))
            h = h_scr[pp]
            y = y + eac * _dot_nt(cb, h.astype(BF16))
            h_scr[pp] = h * eal + _dot_tn((xd * dsp).astype(BF16), bb)
            dskp = jnp.where((_iota_row() < HEAD_DIM), _col(dskv, 2 * pp), _col(dskv, 2 * pp + 1))
            ypre_ref[:, LANES * pp:LANES * (pp + 1)] = y + x * dskp
        ypre = ypre_ref[...]
        z = z_ref[...]
        yg = ypre * (z * _sigmoid(z))
        rstd = lax.rsqrt(jnp.sum(yg * yg, axis=1, keepdims=True) * (1.0 / 256.0) + NORM_EPS)
        ya_ref[...] = yg * rstd * nw_ref[...]

    g = N_GROUPS
    par = pl.BlockSpec((None, 1, LANES), lambda i, j, c: (j, 0, 0))
    return pl.pallas_call(
        body, name=name, grid=(b, g, nc),
        in_specs=[pl.BlockSpec((None, CHUNK, 256), lambda i, j, c: (i, c, j)),
                  pl.BlockSpec((None, CHUNK, LANES), lambda i, j, c: (i, c, 8 + j)),
                  pl.BlockSpec((None, CHUNK, LANES), lambda i, j, c: (i, c, 12 + j)),
                  pl.BlockSpec((None, CHUNK, LANES), lambda i, j, c: (i, c, dt0 + j)),
                  pl.BlockSpec((None, CHUNK, 256), lambda i, j, c: (i, c, z0 + j)),
                  par, par, par,
                  pl.BlockSpec((None, 1, 256), lambda i, j, c: (j, 0, 0))],
        out_specs=[pl.BlockSpec((None, CHUNK, 256), lambda i, j, c: (i, c, j)),
                   pl.BlockSpec((None, CHUNK, 256), lambda i, j, c: (i, c, j)),
                   pl.BlockSpec((None, None, None, 2, CHUNK, SSM_STATE), lambda i, j, c: (i, j, c, 0, 0, 0))],
        out_shape=[jax.ShapeDtypeStruct((b, s, D_MODEL), F32), jax.ShapeDtypeStruct((b, s, D_MODEL), F32),
                   jax.ShapeDtypeStruct((b, g, nc, 2, CHUNK, SSM_STATE), F32)],
        scratch_shapes=[pltpu.VMEM((2, CHUNK, SSM_STATE), F32)],
        compiler_params=_cp(("parallel", "parallel", "arbitrary")),
    )(xact3, xact3, xact3, proj3, proj3, dtb, alog, dsk, nw)


def _ssd_bwd(proj3, xact3, dtb, alog, dsk, nw, ypre3, hst, dya3, *, name):
    b, s, _ = proj3.shape
    nc = s // CHUNK
    dt0 = _PAD_COLS["a_dt"][0] // LANES
    z0 = _PAD_COLS["a_z"][0] // 256

    def body(xs_ref, bm_ref, cm_ref, dt_ref, z_ref, dtb_ref, alog_ref, dsk_ref, nw_ref, ypre_ref, hst_ref,
             dya_ref, dxs_ref, dbm_ref, dcm_ref, dz_ref, ddt_ref, ddtb_ref, dalog_ref, ddsk_ref, dnw_ref,
             dh_scr):
        first = jnp.logical_and(pl.program_id(1) == 0, pl.program_id(2) == 0)

        @pl.when(first)
        def _():
            ddtb_ref[...] = jnp.zeros_like(ddtb_ref)
            dalog_ref[...] = jnp.zeros_like(dalog_ref)
            ddsk_ref[...] = jnp.zeros_like(ddsk_ref)
            dnw_ref[...] = jnp.zeros_like(dnw_ref)

        @pl.when(pl.program_id(2) == 0)
        def _():
            dh_scr[...] = jnp.zeros_like(dh_scr)

        row, lane, causal, dtv, a_row, acum, acum_t = _ssd_common(dt_ref, dtb_ref, alog_ref)
        lane1 = _iota_row()
        bb = bm_ref[...].astype(BF16)
        cb = cm_ref[...].astype(BF16)
        cbm = _dot_nt(cb, bb)

        z = z_ref[...]
        ypre = ypre_ref[...]
        dya = dya_ref[...]
        sz = _sigmoid(z)
        silu = z * sz
        yg = ypre * silu
        rstd = lax.rsqrt(jnp.sum(yg * yg, axis=1, keepdims=True) * (1.0 / 256.0) + NORM_EPS)
        dnw_ref[...] += jnp.sum(dya * yg * rstd, axis=0, keepdims=True)
        dn = dya * nw_ref[...]
        dyg = rstd * dn - yg * (rstd * rstd * rstd * (1.0 / 256.0)) * jnp.sum(dn * yg, axis=1, keepdims=True)
        dz_ref[...] = dyg * ypre * (sz * (1.0 + z * (1.0 - sz)))
        dy_all = dyg * silu

        dskv = dsk_ref[...]
        da_cols = jnp.zeros((CHUNK, LANES), F32)
        dxt_cols = jnp.zeros((CHUNK, LANES), F32)
        ddsk_row = jnp.zeros((1, LANES), F32)
        dcb = jnp.zeros((CHUNK, CHUNK), F32)
        dc = jnp.zeros((CHUNK, SSM_STATE), F32)
        db = jnp.zeros((CHUNK, SSM_STATE), F32)
        last = _iota_col() == CHUNK - 1
        for pp in range(2):
            r0, r1 = 2 * pp, 2 * pp + 1
            x = xs_ref[:, LANES * pp:LANES * (pp + 1)]
            dy = dy_all[:, LANES * pp:LANES * (pp + 1)]
            lo, dtp, xd, d0, d1, al0, al1, eac, dsp, eal = _ssd_pair(pp, x, dtv, acum, acum_t, causal, lane, row)
            w0, w1 = cbm * d0, cbm * d1
            w0b, w1b = w0.astype(BF16), w1.astype(BF16)
            xdb = xd.astype(BF16)
            dyb = dy.astype(BF16)
            h = hst_ref[pp]
            dhn = dh_scr[pp]
            hb = h.astype(BF16)
            dhb = dhn.astype(BF16)
            g0 = _dot_nt(jnp.where(lo, dy, 0.0).astype(BF16), xdb)
            g1 = _dot_nt(jnp.where(lo, 0.0, dy).astype(BF16), xdb)
            dcb = dcb + g0 * d0 + g1 * d1
            m0, m1 = g0 * w0, g1 * w1
            bdh = _dot_nt(bb, dhb)
            dxd = jnp.where(lo, _dot_tn(w0b, dyb), _dot_tn(w1b, dyb)) + dsp * bdh
            ch = _dot_nt(cb, hb)
            edy = eac * dy
            edyb = edy.astype(BF16)
            xds = xd * dsp
            dc = dc + _dot(edyb, hb)
            db = db + _dot(xds.astype(BF16), dhb)
            dh_scr[pp] = dhn * eal + _dot_tn(edyb, cb)
            t2 = edy * ch
            t3 = xds * bdh
            r4 = jnp.sum(dhn * h, axis=1, keepdims=True)
            s4_0 = jnp.sum(jnp.where(_iota_col() < HEAD_DIM, r4, 0.0), axis=0, keepdims=True)
            s4_1 = jnp.sum(r4, axis=0, keepdims=True) - s4_0
            t2_0 = jnp.sum(jnp.where(lo, t2, 0.0), axis=1, keepdims=True)
            t2_1 = jnp.sum(t2, axis=1, keepdims=True) - t2_0
            t3_0 = jnp.sum(jnp.where(lo, t3, 0.0), axis=1, keepdims=True)
            t3_1 = jnp.sum(t3, axis=1, keepdims=True) - t3_0
            dal0 = jnp.sum(t3_0, axis=0, keepdims=True) + jnp.exp(al0) * s4_0
            dal1 = jnp.sum(t3_1, axis=0, keepdims=True) + jnp.exp(al1) * s4_1
            dac0 = (jnp.sum(m0, axis=1, keepdims=True) - jnp.sum(m0.T, axis=1, keepdims=True)
                    + t2_0 - t3_0 + jnp.where(last, dal0, 0.0))
            dac1 = (jnp.sum(m1, axis=1, keepdims=True) - jnp.sum(m1.T, axis=1, keepdims=True)
                    + t2_1 - t3_1 + jnp.where(last, dal1, 0.0))
            da_cols = da_cols + jnp.where(lane == r0, dac0, 0.0) + jnp.where(lane == r1, dac1, 0.0)
            xx = dxd * x
            x0 = jnp.sum(jnp.where(lo, xx, 0.0), axis=1, keepdims=True)
            x1 = jnp.sum(xx, axis=1, keepdims=True) - x0
            dxt_cols = dxt_cols + jnp.where(lane == r0, x0, 0.0) + jnp.where(lane == r1, x1, 0.0)
            dskp = jnp.where((_iota_row() < HEAD_DIM), _col(dskv, r0), _col(dskv, r1))
            dxs_ref[:, LANES * pp:LANES * (pp + 1)] = dxd * dtp + dy * dskp
            yx = jnp.sum(dy * x, axis=0, keepdims=True)
            k0 = jnp.sum(jnp.where((_iota_row() < HEAD_DIM), yx, 0.0), axis=1, keepdims=True)
            k1 = jnp.sum(yx, axis=1, keepdims=True) - k0
            ddsk_row = ddsk_row + jnp.where(lane1 == r0, k0, 0.0) + jnp.where(lane1 == r1, k1, 0.0)
        dcbb = dcb.astype(BF16)
        dcm_ref[...] = dc + _dot(dcbb, bb)
        dbm_ref[...] = db + _dot_tn(dcbb, cb)
        tri_t = (row <= lane).astype(F32)
        dadt = _dot(tri_t, da_cols, precision=HIGHEST)
        ddtv = dadt * a_row + dxt_cols
        dalog_ref[...] += jnp.sum(dadt * dtv, axis=0, keepdims=True) * a_row
        ddt_raw = ddtv * _sigmoid(dt_ref[...] + dtb_ref[...])
        ddt_ref[...] = ddt_raw
        ddtb_ref[...] += jnp.sum(ddt_raw, axis=0, keepdims=True)
        ddsk_ref[...] += ddsk_row

    g = N_GROUPS
    rc = lambda c: nc - 1 - c
    par = pl.BlockSpec((None, 1, LANES), lambda j, i, c: (j, 0, 0))
    parw = pl.BlockSpec((None, 1, 256), lambda j, i, c: (j, 0, 0))
    blk256 = pl.BlockSpec((None, CHUNK, 256), lambda j, i, c: (i, rc(c), j))
    blk128 = pl.BlockSpec((None, CHUNK, LANES), lambda j, i, c: (i, rc(c), j))
    return pl.pallas_call(
        body, name=name, grid=(g, b, nc),
        in_specs=[blk256,
                  pl.BlockSpec((None, CHUNK, LANES), lambda j, i, c: (i, rc(c), 8 + j)),
                  pl.BlockSpec((None, CHUNK, LANES), lambda j, i, c: (i, rc(c), 12 + j)),
                  pl.BlockSpec((None, CHUNK, LANES), lambda j, i, c: (i, rc(c), dt0 + j)),
                  pl.BlockSpec((None, CHUNK, 256), lambda j, i, c: (i, rc(c), z0 + j)),
                  par, par, par, parw,
                  blk256,
                  pl.BlockSpec((None, None, None, 2, CHUNK, SSM_STATE), lambda j, i, c: (i, j, rc(c), 0, 0, 0)),
                  blk256],
        out_specs=[blk256, blk128, blk128, blk256, blk128, par, par, par, parw],
        out_shape=[jax.ShapeDtypeStruct((b, s, D_MODEL), F32), jax.ShapeDtypeStruct((b, s, 512), F32),
                   jax.ShapeDtypeStruct((b, s, 512), F32), jax.ShapeDtypeStruct((b, s, D_MODEL), F32),
                   jax.ShapeDtypeStruct((b, s, 512), F32),
                   jax.ShapeDtypeStruct((g, 1, LANES), F32), jax.ShapeDtypeStruct((g, 1, LANES), F32),
                   jax.ShapeDtypeStruct((g, 1, LANES), F32), jax.ShapeDtypeStruct((g, 1, 256), F32)],
        scratch_shapes=[pltpu.VMEM((2, CHUNK, SSM_STATE), F32)],
        compiler_params=_cp(("arbitrary", "arbitrary", "arbitrary")),
    )(xact3, xact3, xact3, proj3, proj3, dtb, alog, dsk, nw, ypre3, hst, dya3)


def _fgate_fwd(proj3, fb, *, name):
    b, s, _ = proj3.shape
    f0 = _PAD_COLS["c_f"][0] // LANES

    def body(f_ref, fb_ref, cum_ref, carry):
        @pl.when(pl.program_id(1) == 0)
        def _():
            carry[...] = jnp.zeros_like(carry)

        row = lax.broadcasted_iota(jnp.int32, (CHUNK, CHUNK), 0)
        lane = lax.broadcasted_iota(jnp.int32, (CHUNK, CHUNK), 1)
        tri = (row >= lane).astype(F32)
        lf = -_softplus(-(f_ref[...] + fb_ref[...]))
        cs = _dot(tri, lf, precision=HIGHEST) + carry[0:1, :]
        cum_ref[...] = cs
        carry[0:1, :] = _row(cs, CHUNK - 1)

    return pl.pallas_call(
        body, name=name, grid=(b, s // CHUNK),
        in_specs=[pl.BlockSpec((None, CHUNK, LANES), lambda i, c: (i, c, f0)),
                  pl.BlockSpec((1, LANES), lambda i, c: (0, 0))],
        out_specs=pl.BlockSpec((None, CHUNK, LANES), lambda i, c: (i, c, 0)),
        out_shape=jax.ShapeDtypeStruct((b, s, LANES), F32),
        scratch_shapes=[pltpu.VMEM((8, LANES), F32)],
        compiler_params=_cp(("parallel", "arbitrary")),
    )(proj3, fb)


def _fgate_bwd(proj3, fb, dcum, *, name):
    b, s, _ = proj3.shape
    nc = s // CHUNK
    f0 = _PAD_COLS["c_f"][0] // LANES

    def body(f_ref, fb_ref, dc_ref, df_ref, dfb_ref, carry):
        first = jnp.logical_and(pl.program_id(0) == 0, pl.program_id(1) == 0)

        @pl.when(first)
        def _():
            dfb_ref[...] = jnp.zeros_like(dfb_ref)

        @pl.when(pl.program_id(1) == 0)
        def _():
            carry[...] = jnp.zeros_like(carry)

        row = lax.broadcasted_iota(jnp.int32, (CHUNK, CHUNK), 0)
        lane = lax.broadcasted_iota(jnp.int32, (CHUNK, CHUNK), 1)
        tri_t = (row <= lane).astype(F32)
        dlf = _dot(tri_t, dc_ref[...], precision=HIGHEST) + carry[0:1, :]
        carry[0:1, :] = _row(dlf, 0)
        df = dlf * _sigmoid(-(f_ref[...] + fb_ref[...]))
        df_ref[...] = df
        dfb_ref[...] += jnp.sum(df, axis=0, keepdims=True)

    return pl.pallas_call(
        body, name=name, grid=(b, nc),
        in_specs=[pl.BlockSpec((None, CHUNK, LANES), lambda i, c: (i, nc - 1 - c, f0)),
                  pl.BlockSpec((1, LANES), lambda i, c: (0, 0)),
                  pl.BlockSpec((None, CHUNK, LANES), lambda i, c: (i, nc - 1 - c, 0))],
        out_specs=[pl.BlockSpec((None, CHUNK, LANES), lambda i, c: (i, nc - 1 - c, 0)),
                   pl.BlockSpec((1, LANES), lambda i, c: (0, 0))],
        out_shape=[jax.ShapeDtypeStruct((b, s, LANES), F32), jax.ShapeDtypeStruct((1, LANES), F32)],
        scratch_shapes=[pltpu.VMEM((8, LANES), F32)],
        compiler_params=_cp(("arbitrary", "arbitrary")),
    )(proj3, fb, dcum)


_SCALE = HEAD_DIM ** -0.5
_NEG = -1e30


def _fox_scores(qm, kb, cq, ck, mask):
    return jnp.where(mask, _dot_nt(qm, kb) + (cq - ck), _NEG)


def _fox_fwd(proj3, cum, cum_t, *, name, tb):
    b, s, _ = proj3.shape
    nq = s // tb
    q0 = _PAD_COLS["c_q"][0] // LANES
    k0 = _PAD_COLS["c_k"][0] // LANES
    v0 = _PAD_COLS["c_v"][0] // LANES
    z0 = _PAD_COLS["c_z"][0] // LANES

    def body(q_ref, k_ref, v_ref, z_ref, cum_ref, cumt_ref, y_ref, o_ref, lse_ref,
             qm_scr, cq_scr, m_scr, l_scr, acc_scr):
        p = pl.program_id(1)
        i = pl.program_id(2)
        j = pl.program_id(3)
        lo = lax.broadcasted_iota(jnp.int32, (tb, LANES), 1) < HEAD_DIM

        @pl.when(j == 0)
        def _():
            q = q_ref[...] * _SCALE
            qm_scr[0] = jnp.where(lo, q, 0.0).astype(BF16)
            qm_scr[1] = jnp.where(lo, 0.0, q).astype(BF16)
            cv = cum_ref[...]
            cq_scr[0] = _col(cv, 2 * p)
            cq_scr[1] = _col(cv, 2 * p + 1)
            m_scr[...] = jnp.full(m_scr.shape, _NEG, F32)
            l_scr[...] = jnp.zeros_like(l_scr)
            acc_scr[...] = jnp.zeros_like(acc_scr)

        @pl.when(j <= i)
        def _():
            kb = k_ref[...].astype(BF16)
            vb = v_ref[...].astype(BF16)
            row = lax.broadcasted_iota(jnp.int32, (tb, tb), 0)
            col = lax.broadcasted_iota(jnp.int32, (tb, tb), 1)
            mask = (i * tb + row) >= (j * tb + col)
            pvs, alphas = [], []
            for hh in range(2):
                sc = _fox_scores(qm_scr[hh], kb, cq_scr[hh], cumt_ref[hh:hh + 1, :], mask)
                m_old = m_scr[hh]
                m_new = jnp.maximum(m_old, jnp.max(sc, axis=1, keepdims=True))
                alpha = jnp.exp(m_old - m_new)
                pr = jnp.exp(sc - m_new)
                l_scr[hh] = alpha * l_scr[hh] + jnp.sum(pr, axis=1, keepdims=True)
                m_scr[hh] = m_new
                pr_hi = pr.astype(BF16)
                pr_lo = (pr - pr_hi.astype(F32)).astype(BF16)
                pvs.append(_dot(pr_hi, vb) + _dot(pr_lo, vb))
                alphas.append(alpha)
            acc = acc_scr[...]
            acc_scr[...] = jnp.where(lo, alphas[0] * acc + pvs[0], alphas[1] * acc + pvs[1])

        @pl.when(j == nq - 1)
        def _():
            o = acc_scr[...] / jnp.where(lo, l_scr[0], l_scr[1])
            o_ref[...] = o
            lse_ref[...] = jnp.where(lo, m_scr[0] + jnp.log(l_scr[0]), m_scr[1] + jnp.log(l_scr[1]))
            z = z_ref[...]
            y_ref[...] = o * (z * _sigmoid(z))

    qspec = lambda c0: pl.BlockSpec((None, tb, LANES), lambda bi, p, i, j: (bi, i, c0 + p))
    kspec = lambda c0: pl.BlockSpec((None, tb, LANES), lambda bi, p, i, j: (bi, jnp.minimum(j, i), c0 + p))
    ospec = pl.BlockSpec((None, tb, LANES), lambda bi, p, i, j: (bi, i, p))
    return pl.pallas_call(
        body, name=name, grid=(b, N_HEADS // 2, nq, nq),
        in_specs=[qspec(q0), kspec(k0), kspec(v0), qspec(z0),
                  pl.BlockSpec((None, tb, LANES), lambda bi, p, i, j: (bi, i, 0)),
                  pl.BlockSpec((None, None, 8, tb), lambda bi, p, i, j: (bi, p, 0, jnp.minimum(j, i)))],
        out_specs=[ospec, ospec, ospec],
        out_shape=[jax.ShapeDtypeStruct((b, s, D_MODEL), F32)] * 3,
        scratch_shapes=[pltpu.VMEM((2, tb, LANES), BF16), pltpu.VMEM((2, tb, 1), F32),
                        pltpu.VMEM((2, tb, 1), F32), pltpu.VMEM((2, tb, 1), F32),
                        pltpu.VMEM((tb, LANES), F32)],
        compiler_params=_cp(("parallel", "parallel", "parallel", "arbitrary")),
    )(proj3, proj3, proj3, proj3, cum, cum_t)


def _fox_bwd_dq(proj3, cum, cum_t, o3, lse3, dy3, *, name, tb):
    b, s, _ = proj3.shape
    nq = s // tb
    q0 = _PAD_COLS["c_q"][0] // LANES
    k0 = _PAD_COLS["c_k"][0] // LANES
    v0 = _PAD_COLS["c_v"][0] // LANES
    z0 = _PAD_COLS["c_z"][0] // LANES

    def body(q_ref, k_ref, v_ref, z_ref, cum_ref, cumt_ref, o_ref, lse_ref, dy_ref,
             dq_ref, dz_ref, do_ref, dl_ref, qm_scr, dom_scr, cq_scr, lse_scr, dl_scr, acc_scr):
        p = pl.program_id(1)
        i = pl.program_id(2)
        j = pl.program_id(3)
        lo = lax.broadcasted_iota(jnp.int32, (tb, LANES), 1) < HEAD_DIM

        @pl.when(j == 0)
        def _():
            q = q_ref[...] * _SCALE
            qm_scr[0] = jnp.where(lo, q, 0.0).astype(BF16)
            qm_scr[1] = jnp.where(lo, 0.0, q).astype(BF16)
            cv = cum_ref[...]
            cq_scr[0] = _col(cv, 2 * p)
            cq_scr[1] = _col(cv, 2 * p + 1)
            z = z_ref[...]
            sz = _sigmoid(z)
            dy = dy_ref[...]
            o = o_ref[...]
            do = dy * (z * sz)
            dz_ref[...] = dy * o * (sz * (1.0 + z * (1.0 - sz)))
            do_ref[...] = do
            dom_scr[0] = jnp.where(lo, do, 0.0).astype(BF16)
            dom_scr[1] = jnp.where(lo, 0.0, do).astype(BF16)
            doo = do.astype(BF16).astype(F32) * o
            d0 = jnp.sum(jnp.where(lo, doo, 0.0), axis=1, keepdims=True)
            d1 = jnp.sum(jnp.where(lo, 0.0, doo), axis=1, keepdims=True)
            dl_scr[0] = d0
            dl_scr[1] = d1
            dl_ref[...] = jnp.where(lo, d0, d1)
            lse = lse_ref[...]
            lse_scr[0] = _col(lse, 0)
            lse_scr[1] = _col(lse, HEAD_DIM)
            acc_scr[...] = jnp.zeros_like(acc_scr)

        @pl.when(j <= i)
        def _():
            kb = k_ref[...].astype(BF16)
            vb = v_ref[...].astype(BF16)
            row = lax.broadcasted_iota(jnp.int32, (tb, tb), 0)
            col = lax.broadcasted_iota(jnp.int32, (tb, tb), 1)
            mask = (i * tb + row) >= (j * tb + col)
            dqs = []
            for hh in range(2):
                sc = _fox_scores(qm_scr[hh], kb, cq_scr[hh], cumt_ref[hh:hh + 1, :], mask)
                pr = jnp.exp(sc - lse_scr[hh])
                dp = _dot_nt(dom_scr[hh], vb)
                ds = pr * (dp - dl_scr[hh])
                dqs.append(_dot(ds.astype(BF16), kb))
            acc_scr[...] += jnp.where(lo, dqs[0], dqs[1])

        @pl.when(j == nq - 1)
        def _():
            dq_ref[...] = acc_scr[...] * _SCALE

    qspec = lambda c0: pl.BlockSpec((None, tb, LANES), lambda bi, p, i, j: (bi, i, c0 + p))
    kspec = lambda c0: pl.BlockSpec((None, tb, LANES), lambda bi, p, i, j: (bi, jnp.minimum(j, i), c0 + p))
    ospec = pl.BlockSpec((None, tb, LANES), lambda bi, p, i, j: (bi, i, p))
    return pl.pallas_call(
        body, name=name, grid=(b, N_HEADS // 2, nq, nq),
        in_specs=[qspec(q0), kspec(k0), kspec(v0), qspec(z0),
                  pl.BlockSpec((None, tb, LANES), lambda bi, p, i, j: (bi, i, 0)),
                  pl.BlockSpec((None, None, 8, tb), lambda bi, p, i, j: (bi, p, 0, jnp.minimum(j, i))),
                  ospec, ospec, ospec],
        out_specs=[ospec, ospec, ospec, ospec],
        out_shape=[jax.ShapeDtypeStruct((b, s, D_MODEL), F32)] * 4,
        scratch_shapes=[pltpu.VMEM((2, tb, LANES), BF16), pltpu.VMEM((2, tb, LANES), BF16),
                        pltpu.VMEM((2, tb, 1), F32), pltpu.VMEM((2, tb, 1), F32), pltpu.VMEM((2, tb, 1), F32),
                        pltpu.VMEM((tb, LANES), F32)],
        compiler_params=_cp(("parallel", "parallel", "parallel", "arbitrary")),
    )(proj3, proj3, proj3, proj3, cum, cum_t, o3, lse3, dy3)


def _fox_bwd_dkv(proj3, cum, cum_t, lse3, do3, dl3, *, name, tb):
    b, s, _ = proj3.shape
    nq = s // tb
    q0 = _PAD_COLS["c_q"][0] // LANES
    k0 = _PAD_COLS["c_k"][0] // LANES
    v0 = _PAD_COLS["c_v"][0] // LANES

    def body(q_ref, k_ref, v_ref, cum_ref, cumt_ref, lse_ref, do_ref, dl_ref,
             dk_ref, dv_ref, cs_ref, dk_scr, dv_scr, cs_scr):
        p = pl.program_id(1)
        j = pl.program_id(2)
        ii = pl.program_id(3)
        lo = lax.broadcasted_iota(jnp.int32, (tb, LANES), 1) < HEAD_DIM

        @pl.when(ii == 0)
        def _():
            dk_scr[...] = jnp.zeros_like(dk_scr)
            dv_scr[...] = jnp.zeros_like(dv_scr)
            cs_scr[...] = jnp.zeros_like(cs_scr)

        @pl.when(ii >= j)
        def _():
            q = q_ref[...] * _SCALE
            do = do_ref[...]
            kb = k_ref[...].astype(BF16)
            vb = v_ref[...].astype(BF16)
            cv = cum_ref[...]
            lse = lse_ref[...]
            dl = dl_ref[...]
            row = lax.broadcasted_iota(jnp.int32, (tb, tb), 0)
            col = lax.broadcasted_iota(jnp.int32, (tb, tb), 1)
            mask = (ii * tb + row) >= (j * tb + col)
            dk = dk_scr[...]
            dv = dv_scr[...]
            for hh in range(2):
                sel = lo if hh == 0 else jnp.logical_not(lo)
                qm = jnp.where(sel, q, 0.0).astype(BF16)
                dom = jnp.where(sel, do, 0.0).astype(BF16)
                sc = _fox_scores(qm, kb, _col(cv, 2 * p + hh), cumt_ref[hh:hh + 1, :], mask)
                pr = jnp.exp(sc - _col(lse, hh * HEAD_DIM))
                dp = _dot_nt(dom, vb)
                ds = pr * (dp - _col(dl, hh * HEAD_DIM))
                dv = dv + _dot_tn(pr.astype(BF16), dom)
                dk = dk + _dot_tn(ds.astype(BF16), qm)
                cs_scr[hh:hh + 1, :] += jnp.sum(ds, axis=0, keepdims=True)
            dk_scr[...] = dk
            dv_scr[...] = dv

        @pl.when(ii == nq - 1)
        def _():
            dk_ref[...] = dk_scr[...]
            dv_ref[...] = dv_scr[...]
            cs_ref[...] = cs_scr[...]

    qspec = lambda c0: pl.BlockSpec((None, tb, LANES), lambda bi, p, j, i: (bi, jnp.maximum(i, j), c0 + p))
    kspec = lambda c0: pl.BlockSpec((None, tb, LANES), lambda bi, p, j, i: (bi, j, c0 + p))
    qo = pl.BlockSpec((None, tb, LANES), lambda bi, p, j, i: (bi, jnp.maximum(i, j), p))
    ko = pl.BlockSpec((None, tb, LANES), lambda bi, p, j, i: (bi, j, p))
    return pl.pallas_call(
        body, name=name, grid=(b, N_HEADS // 2, nq, nq),
        in_specs=[qspec(q0), kspec(k0), kspec(v0),
                  pl.BlockSpec((None, tb, LANES), lambda bi, p, j, i: (bi, jnp.maximum(i, j), 0)),
                  pl.BlockSpec((None, None, 8, tb), lambda bi, p, j, i: (bi, p, 0, j)),
                  qo, qo, qo],
        out_specs=[ko, ko, pl.BlockSpec((None, None, 8, tb), lambda bi, p, j, i: (bi, p, 0, j))],
        out_shape=[jax.ShapeDtypeStruct((b, s, D_MODEL), F32), jax.ShapeDtypeStruct((b, s, D_MODEL), F32),
                   jax.ShapeDtypeStruct((b, N_HEADS // 2, 8, s), F32)],
        scratch_shapes=[pltpu.VMEM((tb, LANES), F32), pltpu.VMEM((tb, LANES), F32), pltpu.VMEM((8, tb), F32)],
        compiler_params=_cp(("parallel", "parallel", "parallel", "arbitrary")),
    )(proj3, proj3, proj3, cum, cum_t, lse3, do3, dl3)


def _rope(x, cos, sin_signed):
    w = x.shape[1]
    lane = lax.broadcasted_iota(jnp.int32, x.shape, 1)
    first = (lane % HEAD_DIM) < (HEAD_DIM // 2)
    rot = jnp.where(first, pltpu.roll(x, w - HEAD_DIM // 2, 1), pltpu.roll(x, HEAD_DIM // 2, 1))
    return x * cos + rot * sin_signed


def _swa_common(i, kc_ref, kp_ref, cq_ref, sq_ref, cp_ref, sp_ref):
    cq, sq, cpv, spv = cq_ref[...], sq_ref[...], cp_ref[...], sp_ref[...]
    kc = _rope(kc_ref[...], cq, sq).astype(BF16)
    kp = _rope(kp_ref[...], cpv, spv).astype(BF16)
    row = lax.broadcasted_iota(jnp.int32, (CHUNK, CHUNK), 0)
    col = lax.broadcasted_iota(jnp.int32, (CHUNK, CHUNK), 1)
    mask_c = col <= row
    mask_p = jnp.logical_and(col > row, i > 0)
    return cq, sq, cpv, spv, kc, kp, mask_c, mask_p


def _swa_fwd(proj3, k2, v2, cos, sin, sinks, *, name):
    b, s, _ = proj3.shape
    nb = s // CHUNK
    q0 = _PAD_COLS["b_q"][0] // 256
    z0 = _PAD_COLS["b_z"][0] // 256

    def body(q_ref, z_ref, kc_ref, kp_ref, vc_ref, vp_ref, cq_ref, sq_ref, cp_ref, sp_ref, sk_ref,
             y_ref, o_ref, lse_ref):
        i = pl.program_id(2)
        cq, sq, cpv, spv, kc, kp, mask_c, mask_p = _swa_common(i, kc_ref, kp_ref, cq_ref, sq_ref, cp_ref, sp_ref)
        vc = vc_ref[...].astype(BF16)
        vp = vp_ref[...].astype(BF16)
        lo = lax.broadcasted_iota(jnp.int32, (CHUNK, LANES), 1) < HEAD_DIM
        skv = sk_ref[...]
        for pp in range(2):
            q = _rope(q_ref[:, LANES * pp:LANES * (pp + 1)], cq, sq) * _SCALE
            os_, lses = [], []
            for hh in range(2):
                qm = jnp.where(lo if hh == 0 else jnp.logical_not(lo), q, 0.0).astype(BF16)
                sc = jnp.where(mask_c, _dot_nt(qm, kc), _NEG)
                sp_ = jnp.where(mask_p, _dot_nt(qm, kp), _NEG)
                sink = _col(skv, 2 * pp + hh)
                m = jnp.maximum(jnp.maximum(jnp.max(sc, axis=1, keepdims=True),
                                            jnp.max(sp_, axis=1, keepdims=True)), sink)
                pc = jnp.exp(sc - m)
                ppv = jnp.exp(sp_ - m)
                l = jnp.sum(pc, axis=1, keepdims=True) + jnp.sum(ppv, axis=1, keepdims=True) + jnp.exp(sink - m)
                os_.append((_dot(pc.astype(BF16), vc) + _dot(ppv.astype(BF16), vp)) / l)
                lses.append(m + jnp.log(l))
            o = jnp.where(lo, os_[0], os_[1])
            z = z_ref[:, LANES * pp:LANES * (pp + 1)]
            o_ref[:, LANES * pp:LANES * (pp + 1)] = o
            lse_ref[:, LANES * pp:LANES * (pp + 1)] = jnp.where(lo, lses[0], lses[1])
            y_ref[:, LANES * pp:LANES * (pp + 1)] = o * (z * _sigmoid(z))

    prev = lambda i: jnp.maximum(i - 1, 0)
    blk = pl.BlockSpec((None, CHUNK, 256), lambda bi, g, i: (bi, i, g))
    kcur = pl.BlockSpec((None, CHUNK, LANES), lambda bi, g, i: (bi, i, g))
    kprev = pl.BlockSpec((None, CHUNK, LANES), lambda bi, g, i: (bi, prev(i), g))
    tcur = pl.BlockSpec((CHUNK, LANES), lambda bi, g, i: (i, 0))
    tprev = pl.BlockSpec((CHUNK, LANES), lambda bi, g, i: (prev(i), 0))
    return pl.pallas_call(
        body, name=name, grid=(b, N_GROUPS, nb),
        in_specs=[pl.BlockSpec((None, CHUNK, 256), lambda bi, g, i: (bi, i, q0 + g)),
                  pl.BlockSpec((None, CHUNK, 256), lambda bi, g, i: (bi, i, z0 + g)),
                  kcur, kprev, kcur, kprev, tcur, tcur, tprev, tprev,
                  pl.BlockSpec((None, 1, LANES), lambda bi, g, i: (g, 0, 0))],
        out_specs=[blk, blk, blk],
        out_shape=[jax.ShapeDtypeStruct((b, s, D_MODEL), F32)] * 3,
        compiler_params=_cp(("parallel", "parallel", "parallel")),
    )(proj3, proj3, k2, k2, v2, v2, cos, sin, cos, sin, sinks)


def _swa_bwd(proj3, k2, v2, cos, sin, sinks, o3, lse3, dy3, *, name):
    b, s, _ = proj3.shape
    nb = s // CHUNK
    q0 = _PAD_COLS["b_q"][0] // 256
    z0 = _PAD_COLS["b_z"][0] // 256

    def body(q_ref, z_ref, kc_ref, kp_ref, vc_ref, vp_ref, cq_ref, sq_ref, cp_ref, sp_ref, sk_ref,
             o_ref, lse_ref, dy_ref, dq_ref, dz_ref, dkc_ref, dkp_ref, dvc_ref, dvp_ref, dsk_ref):
        i = pl.program_id(2)
        first = jnp.logical_and(pl.program_id(1) == 0, i == 0)

        @pl.when(first)
        def _():
            dsk_ref[...] = jnp.zeros_like(dsk_ref)

        cq, sq, cpv, spv, kc, kp, mask_c, mask_p = _swa_common(i, kc_ref, kp_ref, cq_ref, sq_ref, cp_ref, sp_ref)
        vc = vc_ref[...].astype(BF16)
        vp = vp_ref[...].astype(BF16)
        lo = lax.broadcasted_iota(jnp.int32, (CHUNK, LANES), 1) < HEAD_DIM
        lane1 = lax.broadcasted_iota(jnp.int32, (1, LANES), 1)
        skv = sk_ref[...]
        dkc = jnp.zeros((CHUNK, LANES), F32)
        dkp = jnp.zeros((CHUNK, LANES), F32)
        dvc = jnp.zeros((CHUNK, LANES), F32)
        dvp = jnp.zeros((CHUNK, LANES), F32)
        dsk_row = jnp.zeros((1, LANES), F32)
        for pp in range(2):
            sl = slice(LANES * pp, LANES * (pp + 1))
            q = _rope(q_ref[:, sl], cq, sq) * _SCALE
            z = z_ref[:, sl]
            sz = _sigmoid(z)
            dy = dy_ref[:, sl]
            o = o_ref[:, sl]
            lse = lse_ref[:, sl]
            do = dy * (z * sz)
            dz_ref[:, sl] = dy * o * (sz * (1.0 + z * (1.0 - sz)))
            dqs = []
            for hh in range(2):
                sel = lo if hh == 0 else jnp.logical_not(lo)
                qm = jnp.where(sel, q, 0.0).astype(BF16)
                dom = jnp.where(sel, do, 0.0).astype(BF16)
                lse_h = _col(lse, hh * HEAD_DIM)
                sink = _col(skv, 2 * pp + hh)
                pc = jnp.exp(jnp.where(mask_c, _dot_nt(qm, kc), _NEG) - lse_h)
                ppv = jnp.exp(jnp.where(mask_p, _dot_nt(qm, kp), _NEG) - lse_h)
                dpc, dpp = _dot_nt(dom, vc), _dot_nt(dom, vp)
                dl = jnp.sum(pc * dpc, axis=1, keepdims=True) + jnp.sum(ppv * dpp, axis=1, keepdims=True)
                dsc = pc * (dpc - dl)
                dsp = ppv * (dpp - dl)
                dsink = -jnp.sum(jnp.exp(sink - lse_h) * dl, axis=0, keepdims=True)
                dsk_row = dsk_row + jnp.where(lane1 == 2 * pp + hh, dsink, 0.0)
                dscb, dspb = dsc.astype(BF16), dsp.astype(BF16)
                dqs.append(_dot(dscb, kc) + _dot(dspb, kp))
                dkc = dkc + _dot_tn(dscb, qm)
                dkp = dkp + _dot_tn(dspb, qm)
                dvc = dvc + _dot_tn(pc.astype(BF16), dom)
                dvp = dvp + _dot_tn(ppv.astype(BF16), dom)
            dq_ref[:, sl] = _rope(jnp.where(lo, dqs[0], dqs[1]) * _SCALE, cq, -sq)
        dkc = _rope(dkc, cq, -sq)
        dkp = _rope(dkp, cpv, -spv)
        dkc_ref[...] = dkc + pltpu.roll(dkc, HEAD_DIM, 1)
        dkp_ref[...] = dkp + pltpu.roll(dkp, HEAD_DIM, 1)
        dvc_ref[...] = dvc + pltpu.roll(dvc, HEAD_DIM, 1)
        dvp_ref[...] = dvp + pltpu.roll(dvp, HEAD_DIM, 1)
        dsk_ref[...] += dsk_row

    prev = lambda i: jnp.maximum(i - 1, 0)
    blk = pl.BlockSpec((None, CHUNK, 256), lambda g, bi, i: (bi, i, g))
    kcur = pl.BlockSpec((None, CHUNK, LANES), lambda g, bi, i: (bi, i, g))
    kprev = pl.BlockSpec((None, CHUNK, LANES), lambda g, bi, i: (bi, prev(i), g))
    tcur = pl.BlockSpec((CHUNK, LANES), lambda g, bi, i: (i, 0))
    tprev = pl.BlockSpec((CHUNK, LANES), lambda g, bi, i: (prev(i), 0))
    skspec = pl.BlockSpec((None, 1, LANES), lambda g, bi, i: (g, 0, 0))
    kv_shape = jax.ShapeDtypeStruct((b, s, 512), F32)
    return pl.pallas_call(
        body, name=name, grid=(N_GROUPS, b, nb),
        in_specs=[pl.BlockSpec((None, CHUNK, 256), lambda g, bi, i: (bi, i, q0 + g)),
                  pl.BlockSpec((None, CHUNK, 256), lambda g, bi, i: (bi, i, z0 + g)),
                  kcur, kprev, kcur, kprev, tcur, tcur, tprev, tprev, skspec, blk, blk, blk],
        out_specs=[blk, blk, kcur, kcur, kcur, kcur, skspec],
        out_shape=[jax.ShapeDtypeStruct((b, s, D_MODEL), F32), jax.ShapeDtypeStruct((b, s, D_MODEL), F32),
                   kv_shape, kv_shape, kv_shape, kv_shape, jax.ShapeDtypeStruct((N_GROUPS, 1, LANES), F32)],
        compiler_params=_cp(("arbitrary", "arbitrary", "arbitrary")),
    )(proj3, proj3, k2, k2, v2, v2, cos, sin, cos, sin, sinks, o3, lse3, dy3)


def _merge_fwd(proj, br, gb, *, name, tm=256):
    t = proj.shape[0]
    g0 = _PAD_COLS["gates"][0] // D_MODEL

    def body(g_ref, a_ref, b_ref, c_ref, gb_ref, o_ref):
        acc = None
        for i, r in enumerate((a_ref, b_ref, c_ref)):
            gate = _sigmoid(g_ref[:, D_MODEL * i:D_MODEL * (i + 1)] + gb_ref[i:i + 1, :])
            term = gate * r[...]
            acc = term if acc is None else acc + term
        o_ref[...] = acc.astype(BF16)

    row = pl.BlockSpec((tm, D_MODEL), lambda i: (i, 0))
    return pl.pallas_call(
        body, name=name, grid=(t // tm,),
        in_specs=[pl.BlockSpec((tm, 3 * D_MODEL), lambda i: (i, g0)), row, row, row,
                  pl.BlockSpec((3, D_MODEL), lambda i: (0, 0))],
        out_specs=row, out_shape=jax.ShapeDtypeStruct((t, D_MODEL), BF16),
        compiler_params=_cp(("parallel",)),
    )(proj, br[0], br[1], br[2], gb)


def _merge_bwd(proj, br, gb, dm, *, name, tm=256):
    t = proj.shape[0]
    g0 = _PAD_COLS["gates"][0] // D_MODEL

    def body(g_ref, a_ref, b_ref, c_ref, gb_ref, dm_ref, da_ref, db_ref, dc_ref, dg_ref, dgb_ref):
        @pl.when(pl.program_id(0) == 0)
        def _():
            dgb_ref[...] = jnp.zeros_like(dgb_ref)

        dmv = dm_ref[...]
        for i, (r, dr) in enumerate(((a_ref, da_ref), (b_ref, db_ref), (c_ref, dc_ref))):
            gate = _sigmoid(g_ref[:, D_MODEL * i:D_MODEL * (i + 1)] + gb_ref[i:i + 1, :])
            dr[...] = (dmv * gate).astype(BF16)
            dg = dmv * r[...] * gate * (1.0 - gate)
            dg_ref[:, D_MODEL * i:D_MODEL * (i + 1)] = dg
            dgb_ref[i:i + 1, :] += jnp.sum(dg, axis=0, keepdims=True)

    row = pl.BlockSpec((tm, D_MODEL), lambda i: (i, 0))
    rowb = jax.ShapeDtypeStruct((t, D_MODEL), BF16)
    return pl.pallas_call(
        body, name=name, grid=(t // tm,),
        in_specs=[pl.BlockSpec((tm, 3 * D_MODEL), lambda i: (i, g0)), row, row, row,
                  pl.BlockSpec((3, D_MODEL), lambda i: (0, 0)), row],
        out_specs=[row, row, row, pl.BlockSpec((tm, 3 * D_MODEL), lambda i: (i, 0)),
                   pl.BlockSpec((8, D_MODEL), lambda i: (0, 0))],
        out_shape=[rowb, rowb, rowb, jax.ShapeDtypeStruct((t, 3 * D_MODEL), F32),
                   jax.ShapeDtypeStruct((8, D_MODEL), F32)],
        compiler_params=_cp(("arbitrary",)),
    )(proj, br[0], br[1], br[2], gb, dm)


def _rope_tables(s):
    pos = jnp.arange(s, dtype=F32)
    inv_freq = ROPE_THETA ** (-jnp.arange(0, HEAD_DIM, 2, dtype=F32) / HEAD_DIM)
    ang = pos[:, None] * inv_freq[None, :]
    cos, sin = jnp.cos(ang), jnp.sin(ang)
    return jnp.tile(cos, (1, 4)), jnp.tile(jnp.concatenate([-sin, sin], axis=1), (1, 2))


def _dup_kv(proj3, name):
    b, s, _ = proj3.shape
    p0, sz = _PAD_COLS[name]
    kv = proj3[:, :, p0:p0 + sz].reshape(b, s, N_GROUPS, 1, HEAD_DIM)
    return jnp.broadcast_to(kv, (b, s, N_GROUPS, 2, HEAD_DIM)).reshape(b, s, 512)


def _pair_rows(cum):
    b, s, _ = cum.shape
    t = jnp.transpose(cum[:, :, :N_HEADS], (0, 2, 1)).reshape(b, N_HEADS // 2, 2, s)
    return jnp.pad(t, ((0, 0), (0, 0), (0, 6), (0, 0)))


def _layer_params(wl):
    return dict(
        dtb=_group_lanes(wl["dt_bias"]), alog=_group_lanes(wl["a_log"]), dsk=_group_lanes(wl["d_skip"]),
        nw=wl["ssm_norm_w"].reshape(N_GROUPS, 1, 256), sinks=_group_lanes(wl["sinks"]),
        fb=jnp.pad(wl["f_bias"], (0, LANES - N_HEADS)).reshape(1, LANES))


def _layer_fwd(x, wl, tabs, bsz, li, tb):
    t = x.shape[0]
    s = t // bsz
    cos, sin = tabs
    lp = _layer_params(wl)
    n = lambda k: f"l{li}_{k}"
    h = _rms_fwd(x, wl["norm_w"], name=n("rms_fwd"))
    proj = _mm(h, wl["w_in"], tm=1024, tn=768, tk=1024, name=n("mm_proj"))
    proj3 = proj.reshape(bsz, s, N_PAD)
    xact3 = _conv_fwd(proj3, wl["conv_w"], wl["conv_b"], name=n("conv_fwd"))
    ya3, ypre3, hst = _ssd_fwd(proj3, xact3, lp["dtb"], lp["alog"], lp["dsk"], lp["nw"], name=n("ssd_fwd"))
    k2, v2 = _dup_kv(proj3, "b_k"), _dup_kv(proj3, "b_v")
    yb3, ob3, lseb3 = _swa_fwd(proj3, k2, v2, cos, sin, lp["sinks"], name=n("swa_fwd"))
    cum = _fgate_fwd(proj3, lp["fb"], name=n("fgate_fwd"))
    cum_t = _pair_rows(cum)
    yc3, oc3, lsec3 = _fox_fwd(proj3, cum, cum_t, name=n("fox_fwd"), tb=tb)
    ys = [v.reshape(t, D_MODEL) for v in (ya3, yb3, yc3)]
    br = [_mm(ys[i], wl["w_proj"][i], tm=1024, tn=1024, tk=1024, name=n(f"mm_br{i}")) for i in range(3)]
    merged = _merge_fwd(proj, br, wl["gate_bias"], name=n("merge_fwd"))
    x_new = _mm(merged, wl["w_out"], tm=1024, tn=1024, tk=1024, add=x, name=n("mm_out"))
    saved = dict(x=x, h=h, proj=proj, xact3=xact3, ypre3=ypre3, hst=hst, k2=k2, v2=v2, ob3=ob3, lseb3=lseb3,
                 cum=cum, cum_t=cum_t, oc3=oc3, lsec3=lsec3, ys=ys, br=br, merged=merged, lp=lp)
    return x_new, saved


def _layer_bwd(dx, wl, sv, tabs, bsz, li, tb):
    t = dx.shape[0]
    s = t // bsz
    cos, sin = tabs
    lp = sv["lp"]
    n = lambda k: f"l{li}_{k}"
    proj = sv["proj"]
    proj3 = proj.reshape(bsz, s, N_PAD)
    g = {}
    dmerged = _mm(dx, wl["w_out"], tb=True, tm=1024, tn=1024, tk=1024, name=n("mm_dmerged"))
    g["w_out"] = _mm(sv["merged"], dx, ta=True, tm=1024, tn=1024, tk=512, name=n("mm_dwout"))
    dbr0, dbr1, dbr2, dgates, dgb = _merge_bwd(proj, sv["br"], wl["gate_bias"], dmerged, name=n("merge_bwd"))
    g["gate_bias"] = dgb[:3]
    dbr = (dbr0, dbr1, dbr2)
    dys = [_mm(dbr[i], wl["w_proj"][i], tb=True, tm=1024, tn=1024, tk=1024, name=n(f"mm_dy{i}"))
           for i in range(3)]
    g["w_proj"] = jnp.stack([_mm(sv["ys"][i], dbr[i], ta=True, tm=1024, tn=1024, tk=512, name=n(f"mm_dwproj{i}"))
                             for i in range(3)])
    dy3 = [v.reshape(bsz, s, D_MODEL) for v in dys]

    (dxs, dbm, dcm, daz, dadt, ddtb, dalog, ddsk, dnw) = _ssd_bwd(
        proj3, sv["xact3"], lp["dtb"], lp["alog"], lp["dsk"], lp["nw"], sv["ypre3"], sv["hst"], dy3[0],
        name=n("ssd_bwd"))
    g["dt_bias"], g["a_log"], g["d_skip"] = _ungroup_lanes(ddtb), _ungroup_lanes(dalog), _ungroup_lanes(ddsk)
    g["ssm_norm_w"] = dnw.reshape(D_MODEL)
    dact = jnp.concatenate([dxs, dbm, dcm], axis=2)
    dxbc, dwb = _conv_bwd(proj3, wl["conv_w"], wl["conv_b"], dact, name=n("conv_bwd"))
    g["conv_w"], g["conv_b"] = dwb[:CONV_WIDTH], dwb[CONV_WIDTH]

    dbq, dbz, dkc, dkp, dvc, dvp, dsk = _swa_bwd(proj3, sv["k2"], sv["v2"], cos, sin, lp["sinks"], sv["ob3"],
                                                 sv["lseb3"], dy3[1], name=n("swa_bwd"))
    g["sinks"] = _ungroup_lanes(dsk)

    def fold(cur, prv):
        shifted = jnp.concatenate([prv[:, CHUNK:], jnp.zeros_like(prv[:, :CHUNK])], axis=1)
        tot = cur + shifted
        return tot.reshape(bsz, s, N_GROUPS, 2, HEAD_DIM)[:, :, :, 0].reshape(bsz, s, 256)

    dbk, dbv = fold(dkc, dkp), fold(dvc, dvp)

    dcq, dcz, do3, dl3 = _fox_bwd_dq(proj3, sv["cum"], sv["cum_t"], sv["oc3"], sv["lsec3"], dy3[2],
                                     name=n("fox_dq"), tb=tb)
    dck, dcv, csum = _fox_bwd_dkv(proj3, sv["cum"], sv["cum_t"], sv["lsec3"], do3, dl3, name=n("fox_dkv"), tb=tb)
    dcum = -jnp.transpose(csum[:, :, :2].reshape(bsz, N_HEADS, s), (0, 2, 1))
    dcum = jnp.pad(dcum, ((0, 0), (0, 0), (0, LANES - N_HEADS)))
    dcf, dfb = _fgate_bwd(proj3, lp["fb"], dcum, name=n("fgate_bwd"))
    g["f_bias"] = dfb[0, :N_HEADS]

    parts = {"gates": dgates.reshape(bsz, s, 3 * D_MODEL), "xbc": dxbc, "a_z": daz, "b_q": dbq, "b_z": dbz,
             "c_q": dcq, "c_k": dck, "c_v": dcv, "c_z": dcz, "b_k": dbk, "b_v": dbv, "a_dt": dadt, "c_f": dcf}
    dproj = jnp.concatenate([parts[name] for name, _ in _PAD_ORDER]
                            + [jnp.zeros((bsz, s, N_PAD - N_USED), F32)], axis=2).reshape(t, N_PAD)
    dh = _mm(dproj, wl["w_in"], tb=True, tm=1024, tn=1024, tk=768, name=n("mm_dh"))
    g["w_in"] = _unpad_w_in(_mm(sv["h"], dproj, ta=True, tm=1024, tn=768, tk=512, name=n("mm_dwin")))
    dx_in, dnorm = _rms_bwd(sv["x"], wl["norm_w"], dh, dx, name=n("rms_bwd"))
    g["norm_w"] = dnorm[0]
    return dx_in, g


def _local_step(x, target, wls, final_norm_w, tb=256):
    bsz, s, d = x.shape
    t = bsz * s
    tabs = _rope_tables(s)
    xc = x.reshape(t, d)
    saved = []
    for li, wl in enumerate(wls):
        xc, sv = _layer_fwd(xc, wl, tabs, bsz, li, tb)
        saved.append(sv)
    loss, dx, dfw = _final_loss(xc, final_norm_w, target.reshape(t, d), name="final_loss")
    grads = [None] * len(wls)
    for li in reversed(range(len(wls))):
        dx, grads[li] = _layer_bwd(dx, wls[li], saved[li], tabs, bsz, li, tb)
    return loss[0, 0], dx.reshape(bsz, s, d), grads, dfw[0]


_HBM = pl.BlockSpec(memory_space=pltpu.HBM)


def _chip_peers(x, y):
    return [(1 - x, y), (x, 1 - y), (1 - x, 1 - y)]


def _gather_weights(arrs, *, name):
    n = len(arrs)

    def body(*refs):
        ins, outs = refs[:n], refs[n:2 * n]
        ici_send, ici_recv, d2d_send, d2d_recv, loc_sem = refs[2 * n:]
        x, y, c = lax.axis_index("x"), lax.axis_index("y"), lax.axis_index("c")
        me = 2 * x + y
        peers = _chip_peers(x, y)
        sib = (x, y, 1 - c)
        local, sends, fwds = [], [], []
        for a in range(n):
            lc = pltpu.make_async_copy(ins[a], outs[a].at[me], loc_sem.at[a])
            lc.start()
            local.append(lc)
            for k, (px, py) in enumerate(peers):
                cp = pltpu.make_async_remote_copy(
                    src_ref=ins[a].at[c], dst_ref=outs[a].at[me, c], send_sem=ici_send.at[a, k],
                    recv_sem=ici_recv.at[a, k], device_id=(px, py, c), device_id_type=MESH)
                cp.start()
                sends.append(cp)
        for a in range(n):
            for k, (px, py) in enumerate(peers):
                slot = 2 * px + py
                pltpu.make_async_remote_copy(
                    src_ref=ins[a].at[c], dst_ref=outs[a].at[slot, c], send_sem=ici_send.at[a, k],
                    recv_sem=ici_recv.at[a, k], device_id=(px, py, c), device_id_type=MESH).wait_recv()
                fw = pltpu.make_async_remote_copy(
                    src_ref=outs[a].at[slot, c], dst_ref=outs[a].at[slot, c], send_sem=d2d_send.at[a, k],
                    recv_sem=d2d_recv.at[a, k], device_id=sib, device_id_type=MESH)
                fw.start()
                fwds.append(fw)
        for a in range(n):
            for k, (px, py) in enumerate(peers):
                slot = 2 * px + py
                pltpu.make_async_remote_copy(
                    src_ref=outs[a].at[slot, 1 - c], dst_ref=outs[a].at[slot, 1 - c], send_sem=d2d_send.at[a, k],
                    recv_sem=d2d_recv.at[a, k], device_id=sib, device_id_type=MESH).wait_recv()
        for cp in sends + fwds:
            cp.wait_send()
        for lc in local:
            lc.wait()

    out_shape = [jax.ShapeDtypeStruct((N_CHIPS,) + a.shape, a.dtype) for a in arrs]
    return pl.pallas_call(
        body, name=name, out_shape=out_shape, in_specs=[_HBM] * n, out_specs=[_HBM] * n,
        scratch_shapes=[pltpu.SemaphoreType.DMA((n, 3)), pltpu.SemaphoreType.DMA((n, 3)),
                        pltpu.SemaphoreType.DMA((n, 3)), pltpu.SemaphoreType.DMA((n, 3)),
                        pltpu.SemaphoreType.DMA((n,))],
    )(*arrs)


def _pair_exchange(arrs, *, name):
    n = len(arrs)

    def body(*refs):
        ins, outs = refs[:n], refs[n:2 * n]
        send, recv = refs[2 * n:]
        x, y, c = lax.axis_index("x"), lax.axis_index("y"), lax.axis_index("c")
        sib = (x, y, 1 - c)
        cps = []
        for a in range(n):
            for k in range(N_CHIPS):
                cp = pltpu.make_async_remote_copy(
                    src_ref=ins[a].at[k, 1 - c], dst_ref=outs[a].at[k], send_sem=send.at[a, k],
                    recv_sem=recv.at[a, k], device_id=sib, device_id_type=MESH)
                cp.start()
                cps.append(cp)
        for cp in cps:
            cp.wait()

    out_shape = [jax.ShapeDtypeStruct((N_CHIPS,) + a.shape[2:], a.dtype) for a in arrs]
    return pl.pallas_call(
        body, name=name, out_shape=out_shape, in_specs=[_HBM] * n, out_specs=[_HBM] * n,
        scratch_shapes=[pltpu.SemaphoreType.DMA((n, N_CHIPS)), pltpu.SemaphoreType.DMA((n, N_CHIPS))],
    )(*arrs)


def _chip_exchange(arrs, *, name):
    n = len(arrs)

    def body(*refs):
        ins, outs = refs[:n], refs[n:2 * n]
        send, recv, loc_sem = refs[2 * n:]
        x, y, c = lax.axis_index("x"), lax.axis_index("y"), lax.axis_index("c")
        me = 2 * x + y
        peers = _chip_peers(x, y)
        cps, local = [], []
        for a in range(n):
            lc = pltpu.make_async_copy(ins[a].at[me], outs[a].at[me], loc_sem.at[a])
            lc.start()
            local.append(lc)
            for k, (px, py) in enumerate(peers):
                cp = pltpu.make_async_remote_copy(
                    src_ref=ins[a].at[2 * px + py], dst_ref=outs[a].at[me], send_sem=send.at[a, k],
                    recv_sem=recv.at[a, k], device_id=(px, py, c), device_id_type=MESH)
                cp.start()
                cps.append(cp)
        for a in range(n):
            for k, (px, py) in enumerate(peers):
                pltpu.make_async_remote_copy(
                    src_ref=ins[a].at[2 * px + py], dst_ref=outs[a].at[2 * px + py], send_sem=send.at[a, k],
                    recv_sem=recv.at[a, k], device_id=(px, py, c), device_id_type=MESH).wait_recv()
        for cp in cps:
            cp.wait_send()
        for lc in local:
            lc.wait()

    out_shape = [jax.ShapeDtypeStruct(a.shape, a.dtype) for a in arrs]
    return pl.pallas_call(
        body, name=name, out_shape=out_shape, in_specs=[_HBM] * n, out_specs=[_HBM] * n,
        scratch_shapes=[pltpu.SemaphoreType.DMA((n, 3)), pltpu.SemaphoreType.DMA((n, 3)),
                        pltpu.SemaphoreType.DMA((n,))],
    )(*arrs)


def _pair_share(arrs, *, name):
    n = len(arrs)

    def body(*refs):
        ins, outs = refs[:n], refs[n:2 * n]
        send, recv, loc_sem = refs[2 * n:]
        x, y, c = lax.axis_index("x"), lax.axis_index("y"), lax.axis_index("c")
        sib = (x, y, 1 - c)
        cps, local = [], []
        for a in range(n):
            lc = pltpu.make_async_copy(ins[a], outs[a].at[c], loc_sem.at[a])
            lc.start()
            local.append(lc)
            cp = pltpu.make_async_remote_copy(
                src_ref=ins[a], dst_ref=outs[a].at[c], send_sem=send.at[a], recv_sem=recv.at[a],
                device_id=sib, device_id_type=MESH)
            cp.start()
            cps.append(cp)
        for a in range(n):
            pltpu.make_async_remote_copy(
                src_ref=ins[a], dst_ref=outs[a].at[1 - c], send_sem=send.at[a], recv_sem=recv.at[a],
                device_id=sib, device_id_type=MESH).wait_recv()
        for cp in cps:
            cp.wait_send()
        for lc in local:
            lc.wait()

    out_shape = [jax.ShapeDtypeStruct((2,) + a.shape, a.dtype) for a in arrs]
    return pl.pallas_call(
        body, name=name, out_shape=out_shape, in_specs=[_HBM] * n, out_specs=[_HBM] * n,
        scratch_shapes=[pltpu.SemaphoreType.DMA((n,)), pltpu.SemaphoreType.DMA((n,)),
                        pltpu.SemaphoreType.DMA((n,))],
    )(*arrs)


def _allreduce_small(buf, *, name):
    r = buf.shape[0]

    def body(in_ref, out_ref, land, send, recv):
        x, y, c = lax.axis_index("x"), lax.axis_index("y"), lax.axis_index("c")
        me = 4 * x + 2 * y + c
        land[me] = in_ref[...]
        cps = []
        for k in range(1, N_DEV):
            px, py, pc = x ^ ((k >> 2) & 1), y ^ ((k >> 1) & 1), c ^ (k & 1)
            cp = pltpu.make_async_remote_copy(
                src_ref=in_ref, dst_ref=land.at[me], send_sem=send.at[k - 1], recv_sem=recv.at[k - 1],
                device_id=(px, py, pc), device_id_type=MESH)
            cp.start()
            cps.append(cp)
        for k in range(1, N_DEV):
            px, py, pc = x ^ ((k >> 2) & 1), y ^ ((k >> 1) & 1), c ^ (k & 1)
            pltpu.make_async_remote_copy(
                src_ref=in_ref, dst_ref=land.at[4 * px + 2 * py + pc], send_sem=send.at[k - 1],
                recv_sem=recv.at[k - 1], device_id=(px, py, pc), device_id_type=MESH).wait_recv()
        for cp in cps:
            cp.wait_send()
        acc = land[0]
        for k in range(1, N_DEV):
            acc = acc + land[k]
        out_ref[...] = acc

    vm = pl.BlockSpec(memory_space=pltpu.VMEM)
    return pl.pallas_call(
        body, name=name, out_shape=jax.ShapeDtypeStruct((r, LANES), F32), in_specs=[vm], out_specs=vm,
        scratch_shapes=[pltpu.VMEM((N_DEV, r, LANES), F32), pltpu.SemaphoreType.DMA((N_DEV - 1,)),
                        pltpu.SemaphoreType.DMA((N_DEV - 1,))],
    )(buf)


def _rows2d(a):
    return a.reshape(-1, a.shape[-1])


def _row_tile(rows, cols, n_arrays, budget=20 * 1024 * 1024):
    best = 8 if rows % 8 == 0 else rows
    tr = 8
    while tr <= rows:
        if rows % tr == 0 and tr * cols * 4 * n_arrays * 2 <= budget:
            best = tr
        tr *= 2
    return best


def _add_slot_layer(full, other, *, name):
    _, _, r, cdim = full.shape
    tr = _row_tile(r, cdim, 3)

    def body(c_ref, a_ref, b_ref, o_ref):
        o_ref[...] = a_ref[...] + b_ref[...]

    c = lax.axis_index("c").astype(jnp.int32).reshape(1)
    return pl.pallas_call(
        body, name=name,
        grid_spec=pltpu.PrefetchScalarGridSpec(
            num_scalar_prefetch=1, grid=(N_CHIPS, r // tr),
            in_specs=[pl.BlockSpec((None, None, tr, cdim), lambda k, i, c_ref: (k, c_ref[0], i, 0)),
                      pl.BlockSpec((None, tr, cdim), lambda k, i, c_ref: (k, i, 0))],
            out_specs=pl.BlockSpec((None, tr, cdim), lambda k, i, c_ref: (k, i, 0))),
        out_shape=jax.ShapeDtypeStruct((N_CHIPS, r, cdim), F32),
        compiler_params=_cp(("parallel", "parallel")),
    )(c, full, other)


def _sum_slots(parts, *, name):
    _, r, cdim = parts.shape
    tr = _row_tile(r, cdim, 5)

    def body(p_ref, o_ref):
        o_ref[...] = ((p_ref[0] + p_ref[1]) + p_ref[2]) + p_ref[3]

    return pl.pallas_call(
        body, name=name, grid=(r // tr,),
        in_specs=[pl.BlockSpec((N_CHIPS, tr, cdim), lambda i: (0, i, 0))],
        out_specs=pl.BlockSpec((tr, cdim), lambda i: (i, 0)),
        out_shape=jax.ShapeDtypeStruct((r, cdim), F32),
        compiler_params=_cp(("parallel",)),
    )(parts)


def _adamw(w, g, m, v, *, name):
    r, cdim = w.shape
    tr = _row_tile(r, cdim, 7)
    c1 = 1.0 - ADAM_B1 ** ADAM_STEP
    c2 = 1.0 - ADAM_B2 ** ADAM_STEP

    def body(w_ref, g_ref, m_ref, v_ref, d_ref, nm_ref, nv_ref):
        gv = g_ref[...]
        mn = ADAM_B1 * m_ref[...] + (1.0 - ADAM_B1) * gv
        vn = ADAM_B2 * v_ref[...] + (1.0 - ADAM_B2) * (gv * gv)
        nm_ref[...] = mn
        nv_ref[...] = vn
        d_ref[...] = -ADAM_LR * ((mn / c1) / (jnp.sqrt(vn / c2) + ADAM_EPS) + ADAM_WD * w_ref[...])

    blk = pl.BlockSpec((tr, cdim), lambda i: (i, 0))
    sh = jax.ShapeDtypeStruct((r, cdim), F32)
    return pl.pallas_call(
        body, name=name, grid=(r // tr,), in_specs=[blk] * 4, out_specs=[blk] * 3, out_shape=[sh] * 3,
        compiler_params=_cp(("parallel",)),
    )(w, g, m, v)


_SMALL = ("norm_w", "conv_b", "dt_bias", "a_log", "d_skip", "ssm_norm_w", "sinks", "f_bias", "final_norm_w",
          "conv_w", "gate_bias")


def _pack(vals):
    flat = jnp.concatenate([v.reshape(-1) for v in vals])
    rows = -(-flat.shape[0] // LANES)
    rows = -(-rows // 8) * 8
    return jnp.pad(flat, (0, rows * LANES - flat.shape[0])).reshape(rows, LANES)


def _unpack(buf, shapes):
    flat = buf.reshape(-1)
    out, off = [], 0
    for sh in shapes:
        sz = int(np.prod(sh))
        out.append(flat[off:off + sz].reshape(sh))
        off += sz
    return out


def kernel(x, norm_w, w_in, conv_w, conv_b, dt_bias, a_log, d_skip, ssm_norm_w, sinks, f_bias, gate_bias, w_proj, w_out, final_norm_w, loss_target, m_norm_w, m_w_in, m_conv_w, m_conv_b, m_dt_bias, m_a_log, m_d_skip, m_ssm_norm_w, m_sinks, m_f_bias, m_gate_bias, m_w_proj, m_w_out, m_final_norm_w, v_norm_w, v_w_in, v_conv_w, v_conv_b, v_dt_bias, v_a_log, v_d_skip, v_ssm_norm_w, v_sinks, v_f_bias, v_gate_bias, v_w_proj, v_w_out, v_final_norm_w):
    depth = w_in.shape[0]
    chip = 2 * lax.axis_index("x") + lax.axis_index("y")

    g_in, g_proj, g_out, g_conv, g_gb = _gather_weights(
        [w_in.astype(BF16), w_proj.astype(BF16), w_out.astype(BF16), conv_w, gate_bias], name="gather_weights")
    wls = []
    for li in range(depth):
        w_in_full = jnp.concatenate([g_in[k, li] for k in range(N_CHIPS)], axis=1)
        wls.append(dict(
            norm_w=norm_w[li], w_in=_pad_w_in(w_in_full),
            conv_w=jnp.concatenate([g_conv[k, li] for k in range(N_CHIPS)], axis=1),
            conv_b=conv_b[li], dt_bias=dt_bias[li], a_log=a_log[li], d_skip=d_skip[li],
            ssm_norm_w=ssm_norm_w[li], sinks=sinks[li], f_bias=f_bias[li],
            gate_bias=jnp.concatenate([g_gb[k, li] for k in range(N_CHIPS)], axis=1),
            w_proj=jnp.concatenate([g_proj[k, li] for k in range(N_CHIPS)], axis=1),
            w_out=jnp.concatenate([g_out[k, li] for k in range(N_CHIPS)], axis=0)))

    loss_part, grad_x, grads, d_final = _local_step(x, loss_target, wls, final_norm_w)
    loss = lax.psum(loss_part, ("x", "y", "c"))

    c_in = w_in.shape[2]
    r_proj = w_proj.shape[2]
    r_out = w_out.shape[1]
    full_in = jnp.stack([jnp.stack([grads[li]["w_in"][:, k * c_in:(k + 1) * c_in] for li in range(depth)])
                         for k in range(N_CHIPS)])
    full_proj = jnp.stack([jnp.stack([grads[li]["w_proj"][:, k * r_proj:(k + 1) * r_proj].reshape(-1, D_MODEL)
                                      for li in range(depth)]) for k in range(N_CHIPS)])
    full_out = jnp.stack([jnp.stack([grads[li]["w_out"][k * r_out:(k + 1) * r_out] for li in range(depth)])
                          for k in range(N_CHIPS)])
    fulls = [full_in, full_proj, full_out]
    others = _pair_exchange(fulls, name="grad_pair_exchange")
    pair = [_add_slot_layer(f, o, name=f"grad_pair_add{i}") for i, (f, o) in enumerate(zip(fulls, others))]
    parts = _chip_exchange(pair, name="grad_chip_exchange")
    mine = [_sum_slots(p, name=f"grad_slot_sum{i}") for i, p in enumerate(parts)]
    red_in, red_proj, red_out = _pair_share(mine, name="grad_pair_share")
    grad_w_in = red_in
    grad_w_proj = red_proj.reshape(w_proj.shape)
    grad_w_out = red_out

    small_full = {
        "norm_w": jnp.stack([g["norm_w"] for g in grads]), "conv_b": jnp.stack([g["conv_b"] for g in grads]),
        "dt_bias": jnp.stack([g["dt_bias"] for g in grads]), "a_log": jnp.stack([g["a_log"] for g in grads]),
        "d_skip": jnp.stack([g["d_skip"] for g in grads]),
        "ssm_norm_w": jnp.stack([g["ssm_norm_w"] for g in grads]),
        "sinks": jnp.stack([g["sinks"] for g in grads]), "f_bias": jnp.stack([g["f_bias"] for g in grads]),
        "final_norm_w": d_final,
        "conv_w": jnp.stack([g["conv_w"] for g in grads]), "gate_bias": jnp.stack([g["gate_bias"] for g in grads])}
    shapes = [small_full[k].shape for k in _SMALL]
    summed = _unpack(_allreduce_small(_pack([small_full[k] for k in _SMALL]), name="allreduce_small"), shapes)
    gsmall = dict(zip(_SMALL, summed))
    gsmall["conv_w"] = lax.dynamic_slice_in_dim(gsmall["conv_w"], chip * conv_w.shape[2], conv_w.shape[2], axis=2)
    gsmall["gate_bias"] = lax.dynamic_slice_in_dim(gsmall["gate_bias"], chip * gate_bias.shape[2],
                                                   gate_bias.shape[2], axis=2)

    w_small = dict(norm_w=norm_w, conv_b=conv_b, dt_bias=dt_bias, a_log=a_log, d_skip=d_skip,
                   ssm_norm_w=ssm_norm_w, sinks=sinks, f_bias=f_bias, final_norm_w=final_norm_w, conv_w=conv_w,
                   gate_bias=gate_bias)
    m_small = dict(norm_w=m_norm_w, conv_b=m_conv_b, dt_bias=m_dt_bias, a_log=m_a_log, d_skip=m_d_skip,
                   ssm_norm_w=m_ssm_norm_w, sinks=m_sinks, f_bias=m_f_bias, final_norm_w=m_final_norm_w,
                   conv_w=m_conv_w, gate_bias=m_gate_bias)
    v_small = dict(norm_w=v_norm_w, conv_b=v_conv_b, dt_bias=v_dt_bias, a_log=v_a_log, d_skip=v_d_skip,
                   ssm_norm_w=v_ssm_norm_w, sinks=v_sinks, f_bias=v_f_bias, final_norm_w=v_final_norm_w,
                   conv_w=v_conv_w, gate_bias=v_gate_bias)
    sshapes = [w_small[k].shape for k in _SMALL]
    ds, ms, vs = _adamw(_pack([w_small[k] for k in _SMALL]), _pack([gsmall[k] for k in _SMALL]),
                        _pack([m_small[k] for k in _SMALL]), _pack([v_small[k] for k in _SMALL]), name="adamw_small")
    delta = dict(zip(_SMALL, _unpack(ds, sshapes)))
    new_m = dict(zip(_SMALL, _unpack(ms, sshapes)))
    new_v = dict(zip(_SMALL, _unpack(vs, sshapes)))
    grad = dict(gsmall)
    for nm, w, g, m, v in (("w_in", w_in, grad_w_in, m_w_in, v_w_in),
                           ("w_proj", w_proj, grad_w_proj, m_w_proj, v_w_proj),
                           ("w_out", w_out, grad_w_out, m_w_out, v_w_out)):
        d2, m2, v2 = _adamw(_rows2d(w), _rows2d(g), _rows2d(m), _rows2d(v), name=f"adamw_{nm}")
        grad[nm] = g
        delta[nm], new_m[nm], new_v[nm] = d2.reshape(w.shape), m2.reshape(w.shape), v2.reshape(w.shape)

    order = ("norm_w", "w_in", "conv_w", "conv_b", "dt_bias", "a_log", "d_skip", "ssm_norm_w", "sinks", "f_bias",
             "gate_bias", "w_proj", "w_out", "final_norm_w")
    return (loss, grad_x, *[grad[k] for k in order], *[delta[k] for k in order],
            *[new_m[k] for k in order], *[new_v[k] for k in order])
```

```python
import functools
import math

import numpy as np
import jax
import jax.numpy as jnp
from jax import lax
from jax.experimental import pallas as pl
from jax.experimental.pallas import tpu as pltpu

F32 = jnp.float32
BF16 = jnp.bfloat16
HIGHEST = lax.Precision.HIGHEST
MESH = pl.DeviceIdType.MESH

D_MODEL = 1024
HEAD_DIM = 64
N_HEADS = 16
N_GROUPS = 4
SSM_STATE = 128
CHUNK = 128
CONV_WIDTH = 4
CONV_DIM = 2048
ROPE_THETA = 10000.0
NORM_EPS = 1e-6
LANES = 128
N_CHIPS = 4
N_DEV = 8

ADAM_LR = 0.001
ADAM_B1 = 0.9
ADAM_B2 = 0.999
ADAM_EPS = 1e-08
ADAM_WD = 0.01
ADAM_STEP = 10

_REF_COLS = {}
_off = 0
for _n, _s in (("xbc", 2048), ("a_z", 1024), ("a_dt", 16), ("b_q", 1024), ("b_k", 256), ("b_v", 256),
               ("b_z", 1024), ("c_q", 1024), ("c_k", 1024), ("c_v", 1024), ("c_f", 16), ("c_z", 1024),
               ("gates", 3072)):
    _REF_COLS[_n] = (_off, _s)
    _off += _s
N_IN = _off

_PAD_ORDER = (("gates", 3072), ("xbc", 2048), ("a_z", 1024), ("b_q", 1024), ("b_z", 1024), ("c_q", 1024),
              ("c_k", 1024), ("c_v", 1024), ("c_z", 1024), ("b_k", 256), ("b_v", 256), ("a_dt", 512),
              ("c_f", 128))
_PAD_COLS = {}
_off = 0
for _n, _s in _PAD_ORDER:
    _PAD_COLS[_n] = (_off, _s)
    _off += _s
N_USED = _off
N_PAD = 13824


def _cp(sem, vmem_mb=48):
    return pltpu.CompilerParams(dimension_semantics=sem, vmem_limit_bytes=vmem_mb * 1024 * 1024)


def _dot(a, b, dims=((1,), (0,)), precision=None):
    return lax.dot_general(a, b, (dims, ((), ())), preferred_element_type=F32, precision=precision)


def _dot_nt(a, b):
    return _dot(a, b, ((1,), (1,)))


def _dot_tn(a, b):
    return _dot(a, b, ((0,), (0,)))


def _col(v, idx):
    lane = lax.broadcasted_iota(jnp.int32, v.shape, 1)
    return jnp.sum(jnp.where(lane == idx, v, 0.0), axis=1, keepdims=True)


def _row(v, idx):
    row = lax.broadcasted_iota(jnp.int32, v.shape, 0)
    return jnp.sum(jnp.where(row == idx, v, 0.0), axis=0, keepdims=True)


def _iota_col():
    return lax.broadcasted_iota(jnp.int32, (CHUNK, 1), 0)


def _iota_row():
    return lax.broadcasted_iota(jnp.int32, (1, LANES), 1)


def _sigmoid(x):
    return 1.0 / (1.0 + jnp.exp(-x))


def _softplus(x):
    return jnp.maximum(x, 0.0) + jnp.log(1.0 + jnp.exp(-jnp.abs(x)))


def _pad_w_in(w):
    parts = []
    for name, size in _PAD_ORDER:
        s0, sz = _REF_COLS[name]
        seg = w[:, s0:s0 + sz]
        if name == "a_dt":
            seg = jnp.pad(seg.reshape(-1, N_GROUPS, 4), ((0, 0), (0, 0), (0, LANES - 4))).reshape(-1, 512)
        elif name == "c_f":
            seg = jnp.pad(seg, ((0, 0), (0, LANES - 16)))
        parts.append(seg)
    parts.append(jnp.zeros((w.shape[0], N_PAD - N_USED), w.dtype))
    return jnp.concatenate(parts, axis=1)


def _unpad_w_in(wp):
    segs = {}
    for name, _ in _PAD_ORDER:
        p0, psz = _PAD_COLS[name]
        seg = wp[:, p0:p0 + psz]
        if name == "a_dt":
            seg = seg.reshape(-1, N_GROUPS, LANES)[:, :, :4].reshape(-1, 16)
        elif name == "c_f":
            seg = seg[:, :16]
        segs[name] = seg
    order = sorted(_REF_COLS, key=lambda n: _REF_COLS[n][0])
    return jnp.concatenate([segs[n] for n in order], axis=1)


def _group_lanes(v):
    return jnp.pad(v.reshape(N_GROUPS, 1, 4), ((0, 0), (0, 0), (0, LANES - 4)))


def _ungroup_lanes(v):
    return v[:, 0, :4].reshape(16)


def _mm(a, b, *, ta=False, tb=False, tm=512, tn=512, tk=512, out_dtype=F32, add=None, name):
    if ta:
        kdim, m = a.shape
    else:
        m, kdim = a.shape
    if tb:
        n, k2 = b.shape
    else:
        k2, n = b.shape
    assert kdim == k2, (a.shape, b.shape)
    tm, tn, tk = min(tm, m), min(tn, n), min(tk, kdim)
    assert m % tm == 0 and n % tn == 0 and kdim % tk == 0, (m, n, kdim, tm, tn, tk)
    nk = kdim // tk
    a_spec = (pl.BlockSpec((tk, tm), lambda i, j, k: (k, i)) if ta
              else pl.BlockSpec((tm, tk), lambda i, j, k: (i, k)))
    b_spec = (pl.BlockSpec((tn, tk), lambda i, j, k: (j, k)) if tb
              else pl.BlockSpec((tk, tn), lambda i, j, k: (k, j)))
    dims = ((0 if ta else 1,), (1 if tb else 0,))
    has_add = add is not None

    def body(*refs):
        if has_add:
            a_ref, b_ref, add_ref, o_ref, acc_ref = refs
        else:
            a_ref, b_ref, o_ref, acc_ref = refs
        k = pl.program_id(2)
        p = _dot(a_ref[...].astype(BF16), b_ref[...].astype(BF16), dims)

        @pl.when(k == 0)
        def _():
            acc_ref[...] = p

        @pl.when(k > 0)
        def _():
            acc_ref[...] += p

        @pl.when(k == nk - 1)
        def _():
            r = acc_ref[...]
            if has_add:
                r = r + add_ref[...]
            o_ref[...] = r.astype(out_dtype)

    in_specs = [a_spec, b_spec]
    args = [a, b]
    if has_add:
        in_specs.append(pl.BlockSpec((tm, tn), lambda i, j, k: (i, j)))
        args.append(add)
    return pl.pallas_call(
        body, name=name, grid=(m // tm, n // tn, nk),
        in_specs=in_specs, out_specs=pl.BlockSpec((tm, tn), lambda i, j, k: (i, j)),
        out_shape=jax.ShapeDtypeStruct((m, n), out_dtype),
        scratch_shapes=[pltpu.VMEM((tm, tn), F32)],
        compiler_params=_cp(("parallel", "parallel", "arbitrary")),
    )(*args)


def _rms_fwd(x, w, *, name, tm=512):
    t, d = x.shape

    def body(x_ref, w_ref, o_ref):
        xv = x_ref[...]
        r = lax.rsqrt(jnp.mean(xv * xv, axis=1, keepdims=True) + NORM_EPS)
        o_ref[...] = (xv * r * w_ref[...]).astype(BF16)

    return pl.pallas_call(
        body, name=name, grid=(t // tm,),
        in_specs=[pl.BlockSpec((tm, d), lambda i: (i, 0)), pl.BlockSpec((1, d), lambda i: (0, 0))],
        out_specs=pl.BlockSpec((tm, d), lambda i: (i, 0)),
        out_shape=jax.ShapeDtypeStruct((t, d), BF16),
        compiler_params=_cp(("parallel",)),
    )(x, w.reshape(1, d))


def _rms_bwd(x, w, dh, dres, *, name, tm=512):
    t, d = x.shape

    def body(x_ref, w_ref, dh_ref, dres_ref, dx_ref, dw_ref):
        xv = x_ref[...]
        r = lax.rsqrt(jnp.mean(xv * xv, axis=1, keepdims=True) + NORM_EPS)
        xhat = xv * r
        dhv = dh_ref[...]
        dxhat = dhv * w_ref[...]
        dx = r * (dxhat - xhat * jnp.mean(dxhat * xhat, axis=1, keepdims=True))
        dx_ref[...] = dres_ref[...] + dx

        @pl.when(pl.program_id(0) == 0)
        def _():
            dw_ref[...] = jnp.zeros_like(dw_ref)

        dw_ref[...] += jnp.sum(dhv * xhat, axis=0, keepdims=True)

    return pl.pallas_call(
        body, name=name, grid=(t // tm,),
        in_specs=[pl.BlockSpec((tm, d), lambda i: (i, 0)), pl.BlockSpec((1, d), lambda i: (0, 0)),
                  pl.BlockSpec((tm, d), lambda i: (i, 0)), pl.BlockSpec((tm, d), lambda i: (i, 0))],
        out_specs=[pl.BlockSpec((tm, d), lambda i: (i, 0)), pl.BlockSpec((1, d), lambda i: (0, 0))],
        out_shape=[jax.ShapeDtypeStruct((t, d), F32), jax.ShapeDtypeStruct((1, d), F32)],
        compiler_params=_cp(("arbitrary",)),
    )(x, w.reshape(1, d), dh, dres)


def _final_loss(x, w, target, *, name, tm=512):
    t, d = x.shape

    def body(x_ref, w_ref, t_ref, loss_ref, dx_ref, dw_ref):
        xv = x_ref[...]
        wv = w_ref[...]
        r = lax.rsqrt(jnp.mean(xv * xv, axis=1, keepdims=True) + NORM_EPS)
        xhat = xv * r
        err = xhat * wv - t_ref[...]
        dy = err * (1.0 / d)
        dxhat = dy * wv
        dx_ref[...] = r * (dxhat - xhat * jnp.mean(dxhat * xhat, axis=1, keepdims=True))

        @pl.when(pl.program_id(0) == 0)
        def _():
            dw_ref[...] = jnp.zeros_like(dw_ref)
            loss_ref[...] = jnp.zeros_like(loss_ref)

        dw_ref[...] += jnp.sum(dy * xhat, axis=0, keepdims=True)
        part = 0.5 * jnp.sum(jnp.mean(err * err, axis=1, keepdims=True), axis=0, keepdims=True)
        loss_ref[...] += jnp.broadcast_to(part, loss_ref.shape)

    return pl.pallas_call(
        body, name=name, grid=(t // tm,),
        in_specs=[pl.BlockSpec((tm, d), lambda i: (i, 0)), pl.BlockSpec((1, d), lambda i: (0, 0)),
                  pl.BlockSpec((tm, d), lambda i: (i, 0))],
        out_specs=[pl.BlockSpec((8, LANES), lambda i: (0, 0)), pl.BlockSpec((tm, d), lambda i: (i, 0)),
                   pl.BlockSpec((1, d), lambda i: (0, 0))],
        out_shape=[jax.ShapeDtypeStruct((8, LANES), F32), jax.ShapeDtypeStruct((t, d), F32),
                   jax.ShapeDtypeStruct((1, d), F32)],
        compiler_params=_cp(("arbitrary",)),
    )(x, w.reshape(1, d), target)


_CB = 128


def _conv_pre(u, w_ref, b_ref):
    s = u.shape[0]
    row = lax.broadcasted_iota(jnp.int32, u.shape, 0)
    pre = b_ref[...] + w_ref[CONV_WIDTH - 1:CONV_WIDTH, :] * u
    for sh in range(1, CONV_WIDTH):
        shifted = jnp.where(row >= sh, pltpu.roll(u, sh, 0), 0.0)
        pre = pre + w_ref[CONV_WIDTH - 1 - sh:CONV_WIDTH - sh, :] * shifted
    return pre


def _conv_fwd(proj3, cw, cb, *, name):
    b, s, _ = proj3.shape
    c0 = _PAD_COLS["xbc"][0] // _CB

    def body(u_ref, w_ref, b_ref, o_ref):
        pre = _conv_pre(u_ref[...], w_ref, b_ref)
        o_ref[...] = pre * _sigmoid(pre)

    return pl.pallas_call(
        body, name=name, grid=(b, CONV_DIM // _CB),
        in_specs=[pl.BlockSpec((None, s, _CB), lambda i, j: (i, 0, c0 + j)),
                  pl.BlockSpec((CONV_WIDTH, _CB), lambda i, j: (0, j)),
                  pl.BlockSpec((1, _CB), lambda i, j: (0, j))],
        out_specs=pl.BlockSpec((None, s, _CB), lambda i, j: (i, 0, j)),
        out_shape=jax.ShapeDtypeStruct((b, s, CONV_DIM), F32),
        compiler_params=_cp(("parallel", "parallel")),
    )(proj3, cw, cb.reshape(1, CONV_DIM))


def _conv_bwd(proj3, cw, cb, dact, *, name):
    b, s, _ = proj3.shape
    c0 = _PAD_COLS["xbc"][0] // _CB

    def body(u_ref, w_ref, b_ref, da_ref, du_ref, dwb_ref):
        u = u_ref[...]
        pre = _conv_pre(u, w_ref, b_ref)
        sg = _sigmoid(pre)
        dpre = da_ref[...] * (sg * (1.0 + pre * (1.0 - sg)))
        row = lax.broadcasted_iota(jnp.int32, u.shape, 0)
        du = w_ref[CONV_WIDTH - 1:CONV_WIDTH, :] * dpre
        rows = [jnp.sum(dpre * u, axis=0, keepdims=True)]
        for sh in range(1, CONV_WIDTH):
            fwd_shift = jnp.where(row < s - sh, pltpu.roll(dpre, s - sh, 0), 0.0)
            du = du + w_ref[CONV_WIDTH - 1 - sh:CONV_WIDTH - sh, :] * fwd_shift
            ush = jnp.where(row >= sh, pltpu.roll(u, sh, 0), 0.0)
            rows.append(jnp.sum(dpre * ush, axis=0, keepdims=True))
        du_ref[...] = du

        @pl.when(pl.program_id(1) == 0)
        def _():
            dwb_ref[...] = jnp.zeros_like(dwb_ref)

        for sh in range(CONV_WIDTH):
            k = CONV_WIDTH - 1 - sh
            dwb_ref[k:k + 1, :] += rows[sh]
        dwb_ref[CONV_WIDTH:CONV_WIDTH + 1, :] += jnp.sum(dpre, axis=0, keepdims=True)

    return pl.pallas_call(
        body, name=name, grid=(CONV_DIM // _CB, b),
        in_specs=[pl.BlockSpec((None, s, _CB), lambda j, i: (i, 0, c0 + j)),
                  pl.BlockSpec((CONV_WIDTH, _CB), lambda j, i: (0, j)),
                  pl.BlockSpec((1, _CB), lambda j, i: (0, j)),
                  pl.BlockSpec((None, s, _CB), lambda j, i: (i, 0, j))],
        out_specs=[pl.BlockSpec((None, s, _CB), lambda j, i: (i, 0, j)),
                   pl.BlockSpec((8, _CB), lambda j, i: (0, j))],
        out_shape=[jax.ShapeDtypeStruct((b, s, CONV_DIM), F32), jax.ShapeDtypeStruct((8, CONV_DIM), F32)],
        compiler_params=_cp(("parallel", "arbitrary")),
    )(proj3, cw, cb.reshape(1, CONV_DIM), dact)


def _ssd_common(dt_ref, dtb_ref, alog_ref):
    row = lax.broadcasted_iota(jnp.int32, (CHUNK, CHUNK), 0)
    lane = lax.broadcasted_iota(jnp.int32, (CHUNK, CHUNK), 1)
    causal = row >= lane
    tri = causal.astype(F32)
    dtv = _softplus(dt_ref[...] + dtb_ref[...])
    a_row = -jnp.exp(alog_ref[...])
    acum = _dot(tri, dtv * a_row, precision=HIGHEST)
    return row, lane, causal, dtv, a_row, acum, acum.T


def _ssd_pair(pp, x, dtv, acum, acum_t, causal, lane, row):
    lo = lane < HEAD_DIM
    r0, r1 = 2 * pp, 2 * pp + 1
    dtp = jnp.where(lo, _col(dtv, r0), _col(dtv, r1))
    ac0, ac1 = _col(acum, r0), _col(acum, r1)
    ar0, ar1 = _row(acum_t, r0), _row(acum_t, r1)
    d0 = jnp.where(causal, jnp.exp(jnp.where(causal, ac0 - ar0, 0.0)), 0.0)
    d1 = jnp.where(causal, jnp.exp(jnp.where(causal, ac1 - ar1, 0.0)), 0.0)
    al0, al1 = _col(ar0, CHUNK - 1), _col(ar1, CHUNK - 1)
    eac = jnp.where(lo, jnp.exp(ac0), jnp.exp(ac1))
    dsp = jnp.where(lo, jnp.exp(al0 - ac0), jnp.exp(al1 - ac1))
    eal = jnp.where(_iota_col() < HEAD_DIM, jnp.exp(al0), jnp.exp(al1))
    return lo, dtp, x * dtp, d0, d1, al0, al1, eac, dsp, eal


def _ssd_fwd(proj3, xact3, dtb, alog, dsk, nw, *, name):
    b, s, _ = proj3.shape
    nc = s // CHUNK
    dt0 = _PAD_COLS["a_dt"][0] // LANES
    z0 = _PAD_COLS["a_z"][0] // 256

    def body(xs_ref, bm_ref, cm_ref, dt_ref, z_ref, dtb_ref, alog_ref, dsk_ref, nw_ref,
             ya_ref, ypre_ref, hst_ref, h_scr):
        @pl.when(pl.program_id(2) == 0)
        def _():
            h_scr[...] = jnp.zeros_like(h_scr)

        row, lane, causal, dtv, a_row, acum, acum_t = _ssd_common(dt_ref, dtb_ref, alog_ref)
        bb = bm_ref[...].astype(BF16)
        cb = cm_ref[...].astype(BF16)
        cbm = _dot_nt(cb, bb)
        hst_ref[...] = h_scr[...]
        dskv = dsk_ref[...]
        for pp in range(2):
            x = xs_ref[:, LANES * pp:LANES * (pp + 1)]
            lo, dtp, xd, d0, d1, al0, al1, eac, dsp, eal = _ssd_pair(pp, x, dtv, acum, acum_t, causal, lane, row)
            xdb = xd.astype(BF16)
            y = jnp.where(lo, _dot((cbm * d0).astype(BF16), xdb), _dot((cbm * d1).astype(BF16), xdb))
            h = h_scr[pp]
            y = y + eac * _dot_nt(cb, h.astype(BF16))
            h_scr[pp] = h * eal + _dot_tn((xd * dsp).astype(BF16), bb)
            dskp = jnp.where((_iota_row() < HEAD_DIM), _col(dskv, 2 * pp), _col(dskv, 2 * pp + 1))
            ypre_ref[:, LANES * pp:LANES * (pp + 1)] = y + x * dskp
        ypre = ypre_ref[...]
        z = z_ref[...]
        yg = ypre * (z * _sigmoid(z))
        rstd = lax.rsqrt(jnp.sum(yg * yg, axis=1, keepdims=True) * (1.0 / 256.0) + NORM_EPS)
        ya_ref[...] = yg * rstd * nw_ref[...]

    g = N_GROUPS
    par = pl.BlockSpec((None, 1, LANES), lambda i, j, c: (j, 0, 0))
    return pl.pallas_call(
        body, name=name, grid=(b, g, nc),
        in_specs=[pl.BlockSpec((None, CHUNK, 256), lambda i, j, c: (i, c, j)),
                  pl.BlockSpec((None, CHUNK, LANES), lambda i, j, c: (i, c, 8 + j)),
                  pl.BlockSpec((None, CHUNK, LANES), lambda i, j, c: (i, c, 12 + j)),
                  pl.BlockSpec((None, CHUNK, LANES), lambda i, j, c: (i, c, dt0 + j)),
                  pl.BlockSpec((None, CHUNK, 256), lambda i, j, c: (i, c, z0 + j)),
                  par, par, par,
                  pl.BlockSpec((None, 1, 256), lambda i, j, c: (j, 0, 0))],
        out_specs=[pl.BlockSpec((None, CHUNK, 256), lambda i, j, c: (i, c, j)),
                   pl.BlockSpec((None, CHUNK, 256), lambda i, j, c: (i, c, j)),
                   pl.BlockSpec((None, None, None, 2, CHUNK, SSM_STATE), lambda i, j, c: (i, j, c, 0, 0, 0))],
        out_shape=[jax.ShapeDtypeStruct((b, s, D_MODEL), F32), jax.ShapeDtypeStruct((b, s, D_MODEL), F32),
                   jax.ShapeDtypeStruct((b, g, nc, 2, CHUNK, SSM_STATE), F32)],
        scratch_shapes=[pltpu.VMEM((2, CHUNK, SSM_STATE), F32)],
        compiler_params=_cp(("parallel", "parallel", "arbitrary")),
    )(xact3, xact3, xact3, proj3, proj3, dtb, alog, dsk, nw)


def _ssd_bwd(proj3, xact3, dtb, alog, dsk, nw, ypre3, hst, dya3, *, name):
    b, s, _ = proj3.shape
    nc = s // CHUNK
    dt0 = _PAD_COLS["a_dt"][0] // LANES
    z0 = _PAD_COLS["a_z"][0] // 256

    def body(xs_ref, bm_ref, cm_ref, dt_ref, z_ref, dtb_ref, alog_ref, dsk_ref, nw_ref, ypre_ref, hst_ref,
             dya_ref, dxs_ref, dbm_ref, dcm_ref, dz_ref, ddt_ref, ddtb_ref, dalog_ref, ddsk_ref, dnw_ref,
             dh_scr):
        first = jnp.logical_and(pl.program_id(1) == 0, pl.program_id(2) == 0)

        @pl.when(first)
        def _():
            ddtb_ref[...] = jnp.zeros_like(ddtb_ref)
            dalog_ref[...] = jnp.zeros_like(dalog_ref)
            ddsk_ref[...] = jnp.zeros_like(ddsk_ref)
            dnw_ref[...] = jnp.zeros_like(dnw_ref)

        @pl.when(pl.program_id(2) == 0)
        def _():
            dh_scr[...] = jnp.zeros_like(dh_scr)

        row, lane, causal, dtv, a_row, acum, acum_t = _ssd_common(dt_ref, dtb_ref, alog_ref)
        lane1 = _iota_row()
        bb = bm_ref[...].astype(BF16)
        cb = cm_ref[...].astype(BF16)
        cbm = _dot_nt(cb, bb)

        z = z_ref[...]
        ypre = ypre_ref[...]
        dya = dya_ref[...]
        sz = _sigmoid(z)
        silu = z * sz
        yg = ypre * silu
        rstd = lax.rsqrt(jnp.sum(yg * yg, axis=1, keepdims=True) * (1.0 / 256.0) + NORM_EPS)
        dnw_ref[...] += jnp.sum(dya * yg * rstd, axis=0, keepdims=True)
        dn = dya * nw_ref[...]
        dyg = rstd * dn - yg * (rstd * rstd * rstd * (1.0 / 256.0)) * jnp.sum(dn * yg, axis=1, keepdims=True)
        dz_ref[...] = dyg * ypre * (sz * (1.0 + z * (1.0 - sz)))
        dy_all = dyg * silu

        dskv = dsk_ref[...]
        da_cols = jnp.zeros((CHUNK, LANES), F32)
        dxt_cols = jnp.zeros((CHUNK, LANES), F32)
        ddsk_row = jnp.zeros((1, LANES), F32)
        dcb = jnp.zeros((CHUNK, CHUNK), F32)
        dc = jnp.zeros((CHUNK, SSM_STATE), F32)
        db = jnp.zeros((CHUNK, SSM_STATE), F32)
        last = _iota_col() == CHUNK - 1
        for pp in range(2):
            r0, r1 = 2 * pp, 2 * pp + 1
            x = xs_ref[:, LANES * pp:LANES * (pp + 1)]
            dy = dy_all[:, LANES * pp:LANES * (pp + 1)]
            lo, dtp, xd, d0, d1, al0, al1, eac, dsp, eal = _ssd_pair(pp, x, dtv, acum, acum_t, causal, lane, row)
            w0, w1 = cbm * d0, cbm * d1
            w0b, w1b = w0.astype(BF16), w1.astype(BF16)
            xdb = xd.astype(BF16)
            dyb = dy.astype(BF16)
            h = hst_ref[pp]
            dhn = dh_scr[pp]
            hb = h.astype(BF16)
            dhb = dhn.astype(BF16)
            g0 = _dot_nt(jnp.where(lo, dy, 0.0).astype(BF16), xdb)
            g1 = _dot_nt(jnp.where(lo, 0.0, dy).astype(BF16), xdb)
            dcb = dcb + g0 * d0 + g1 * d1
            m0, m1 = g0 * w0, g1 * w1
            bdh = _dot_nt(bb, dhb)
            dxd = jnp.where(lo, _dot_tn(w0b, dyb), _dot_tn(w1b, dyb)) + dsp * bdh
            ch = _dot_nt(cb, hb)
            edy = eac * dy
            edyb = edy.astype(BF16)
            xds = xd * dsp
            dc = dc + _dot(edyb, hb)
            db = db + _dot(xds.astype(BF16), dhb)
            dh_scr[pp] = dhn * eal + _dot_tn(edyb, cb)
            t2 = edy * ch
            t3 = xds * bdh
            r4 = jnp.sum(dhn * h, axis=1, keepdims=True)
            s4_0 = jnp.sum(jnp.where(_iota_col() < HEAD_DIM, r4, 0.0), axis=0, keepdims=True)
            s4_1 = jnp.sum(r4, axis=0, keepdims=True) - s4_0
            t2_0 = jnp.sum(jnp.where(lo, t2, 0.0), axis=1, keepdims=True)
            t2_1 = jnp.sum(t2, axis=1, keepdims=True) - t2_0
            t3_0 = jnp.sum(jnp.where(lo, t3, 0.0), axis=1, keepdims=True)
            t3_1 = jnp.sum(t3, axis=1, keepdims=True) - t3_0
            dal0 = jnp.sum(t3_0, axis=0, keepdims=True) + jnp.exp(al0) * s4_0
            dal1 = jnp.sum(t3_1, axis=0, keepdims=True) + jnp.exp(al1) * s4_1
            dac0 = (jnp.sum(m0, axis=1, keepdims=True) - jnp.sum(m0.T, axis=1, keepdims=True)
                    + t2_0 - t3_0 + jnp.where(last, dal0, 0.0))
            dac1 = (jnp.sum(m1, axis=1, keepdims=True) - jnp.sum(m1.T, axis=1, keepdims=True)
                    + t2_1 - t3_1 + jnp.where(last, dal1, 0.0))
            da_cols = da_cols + jnp.where(lane == r0, dac0, 0.0) + jnp.where(lane == r1, dac1, 0.0)
            xx = dxd * x
            x0 = jnp.sum(jnp.where(lo, xx, 0.0), axis=1, keepdims=True)
            x1 = jnp.sum(xx, axis=1, keepdims=True) - x0
            dxt_cols = dxt_cols + jnp.where(lane == r0, x0, 0.0) + jnp.where(lane == r1, x1, 0.0)
            dskp = jnp.where((_iota_row() < HEAD_DIM), _col(dskv, r0), _col(dskv, r1))
            dxs_ref[:, LANES * pp:LANES * (pp + 1)] = dxd * dtp + dy * dskp
            yx = jnp.sum(dy * x, axis=0, keepdims=True)
            k0 = jnp.sum(jnp.where((_iota_row() < HEAD_DIM), yx, 0.0), axis=1, keepdims=True)
            k1 = jnp.sum(yx, axis=1, keepdims=True) - k0
            ddsk_row = ddsk_row + jnp.where(lane1 == r0, k0, 0.0) + jnp.where(lane1 == r1, k1, 0.0)
        dcbb = dcb.astype(BF16)
        dcm_ref[...] = dc + _dot(dcbb, bb)
        dbm_ref[...] = db + _dot_tn(dcbb, cb)
        tri_t = (row <= lane).astype(F32)
        dadt = _dot(tri_t, da_cols, precision=HIGHEST)
        ddtv = dadt * a_row + dxt_cols
        dalog_ref[...] += jnp.sum(dadt * dtv, axis=0, keepdims=True) * a_row
        ddt_raw = ddtv * _sigmoid(dt_ref[...] + dtb_ref[...])
        ddt_ref[...] = ddt_raw
        ddtb_ref[...] += jnp.sum(ddt_raw, axis=0, keepdims=True)
        ddsk_ref[...] += ddsk_row

    g = N_GROUPS
    rc = lambda c: nc - 1 - c
    par = pl.BlockSpec((None, 1, LANES), lambda j, i, c: (j, 0, 0))
    parw = pl.BlockSpec((None, 1, 256), lambda j, i, c: (j, 0, 0))
    blk256 = pl.BlockSpec((None, CHUNK, 256), lambda j, i, c: (i, rc(c), j))
    blk128 = pl.BlockSpec((None, CHUNK, LANES), lambda j, i, c: (i, rc(c), j))
    return pl.pallas_call(
        body, name=name, grid=(g, b, nc),
        in_specs=[blk256,
                  pl.BlockSpec((None, CHUNK, LANES), lambda j, i, c: (i, rc(c), 8 + j)),
                  pl.BlockSpec((None, CHUNK, LANES), lambda j, i, c: (i, rc(c), 12 + j)),
                  pl.BlockSpec((None, CHUNK, LANES), lambda j, i, c: (i, rc(c), dt0 + j)),
                  pl.BlockSpec((None, CHUNK, 256), lambda j, i, c: (i, rc(c), z0 + j)),
                  par, par, par, parw,
                  blk256,
                  pl.BlockSpec((None, None, None, 2, CHUNK, SSM_STATE), lambda j, i, c: (i, j, rc(c), 0, 0, 0)),
                  blk256],
        out_specs=[blk256, blk128, blk128, blk256, blk128, par, par, par, parw],
        out_shape=[jax.ShapeDtypeStruct((b, s, D_MODEL), F32), jax.ShapeDtypeStruct((b, s, 512), F32),
                   jax.ShapeDtypeStruct((b, s, 512), F32), jax.ShapeDtypeStruct((b, s, D_MODEL), F32),
                   jax.ShapeDtypeStruct((b, s, 512), F32),
                   jax.ShapeDtypeStruct((g, 1, LANES), F32), jax.ShapeDtypeStruct((g, 1, LANES), F32),
                   jax.ShapeDtypeStruct((g, 1, LANES), F32), jax.ShapeDtypeStruct((g, 1, 256), F32)],
        scratch_shapes=[pltpu.VMEM((2, CHUNK, SSM_STATE), F32)],
        compiler_params=_cp(("arbitrary", "arbitrary", "arbitrary")),
    )(xact3, xact3, xact3, proj3, proj3, dtb, alog, dsk, nw, ypre3, hst, dya3)


def _fgate_fwd(proj3, fb, *, name):
    b, s, _ = proj3.shape
    f0 = _PAD_COLS["c_f"][0] // LANES

    def body(f_ref, fb_ref, cum_ref, carry):
        @pl.when(pl.program_id(1) == 0)
        def _():
            carry[...] = jnp.zeros_like(carry)

        row = lax.broadcasted_iota(jnp.int32, (CHUNK, CHUNK), 0)
        lane = lax.broadcasted_iota(jnp.int32, (CHUNK, CHUNK), 1)
        tri = (row >= lane).astype(F32)
        lf = -_softplus(-(f_ref[...] + fb_ref[...]))
        cs = _dot(tri, lf, precision=HIGHEST) + carry[0:1, :]
        cum_ref[...] = cs
        carry[0:1, :] = _row(cs, CHUNK - 1)

    return pl.pallas_call(
        body, name=name, grid=(b, s // CHUNK),
        in_specs=[pl.BlockSpec((None, CHUNK, LANES), lambda i, c: (i, c, f0)),
                  pl.BlockSpec((1, LANES), lambda i, c: (0, 0))],
        out_specs=pl.BlockSpec((None, CHUNK, LANES), lambda i, c: (i, c, 0)),
        out_shape=jax.ShapeDtypeStruct((b, s, LANES), F32),
        scratch_shapes=[pltpu.VMEM((8, LANES), F32)],
        compiler_params=_cp(("parallel", "arbitrary")),
    )(proj3, fb)


def _fgate_bwd(proj3, fb, dcum, *, name):
    b, s, _ = proj3.shape
    nc = s // CHUNK
    f0 = _PAD_COLS["c_f"][0] // LANES

    def body(f_ref, fb_ref, dc_ref, df_ref, dfb_ref, carry):
        first = jnp.logical_and(pl.program_id(0) == 0, pl.program_id(1) == 0)

        @pl.when(first)
        def _():
            dfb_ref[...] = jnp.zeros_like(dfb_ref)

        @pl.when(pl.program_id(1) == 0)
        def _():
            carry[...] = jnp.zeros_like(carry)

        row = lax.broadcasted_iota(jnp.int32, (CHUNK, CHUNK), 0)
        lane = lax.broadcasted_iota(jnp.int32, (CHUNK, CHUNK), 1)
        tri_t = (row <= lane).astype(F32)
        dlf = _dot(tri_t, dc_ref[...], precision=HIGHEST) + carry[0:1, :]
        carry[0:1, :] = _row(dlf, 0)
        df = dlf * _sigmoid(-(f_ref[...] + fb_ref[...]))
        df_ref[...] = df
        dfb_ref[...] += jnp.sum(df, axis=0, keepdims=True)

    return pl.pallas_call(
        body, name=name, grid=(b, nc),
        in_specs=[pl.BlockSpec((None, CHUNK, LANES), lambda i, c: (i, nc - 1 - c, f0)),
                  pl.BlockSpec((1, LANES), lambda i, c: (0, 0)),
                  pl.BlockSpec((None, CHUNK, LANES), lambda i, c: (i, nc - 1 - c, 0))],
        out_specs=[pl.BlockSpec((None, CHUNK, LANES), lambda i, c: (i, nc - 1 - c, 0)),
                   pl.BlockSpec((1, LANES), lambda i, c: (0, 0))],
        out_shape=[jax.ShapeDtypeStruct((b, s, LANES), F32), jax.ShapeDtypeStruct((1, LANES), F32)],
        scratch_shapes=[pltpu.VMEM((8, LANES), F32)],
        compiler_params=_cp(("arbitrary", "arbitrary")),
    )(proj3, fb, dcum)


_SCALE = HEAD_DIM ** -0.5
_NEG = -1e30


def _fox_fwd(proj3, cum_t, *, name, tb):
    b, s, _ = proj3.shape
    nq = s // tb
    q0 = _PAD_COLS["c_q"][0] // LANES
    k0 = _PAD_COLS["c_k"][0] // LANES
    v0 = _PAD_COLS["c_v"][0] // LANES
    z0 = _PAD_COLS["c_z"][0] // LANES

    def body(q_ref, k_ref, v_ref, z_ref, cumt_ref, y_ref, o_ref, lse_ref):
        i = pl.program_id(2)
        lo = lax.broadcasted_iota(jnp.int32, (tb, LANES), 1) < HEAD_DIM
        q = q_ref[...] * _SCALE
        qms = (jnp.where(lo, q, 0.0).astype(BF16), jnp.where(lo, 0.0, q).astype(BF16))

        def block(j, carry, diagonal):
            ks = pl.ds(pl.multiple_of(j * tb, tb), tb)
            kb = k_ref[ks, :].astype(BF16)
            vb = v_ref[ks, :].astype(BF16)
            ckv = cumt_ref[j]
            if diagonal:
                row = lax.broadcasted_iota(jnp.int32, (tb, tb), 0)
                col = lax.broadcasted_iota(jnp.int32, (tb, tb), 1)
                mask = row >= col
            ms, ls, acc = carry
            new_m, new_l, pvs, alphas = [], [], [], []
            for hh in range(2):
                sc = _dot_nt(qms[hh], kb) - ckv[hh:hh + 1, :]
                if diagonal:
                    sc = jnp.where(mask, sc, _NEG)
                m_new = jnp.maximum(ms[hh], jnp.max(sc, axis=1, keepdims=True))
                alpha = jnp.exp(ms[hh] - m_new)
                pr = jnp.exp(sc - m_new)
                new_l.append(alpha * ls[hh] + jnp.sum(pr, axis=1, keepdims=True))
                new_m.append(m_new)
                pr_hi = pr.astype(BF16)
                pr_lo = (pr - pr_hi.astype(F32)).astype(BF16)
                pvs.append(_dot(pr_hi, vb) + _dot(pr_lo, vb))
                alphas.append(alpha)
            acc = jnp.where(lo, alphas[0] * acc + pvs[0], alphas[1] * acc + pvs[1])
            return (tuple(new_m), tuple(new_l), acc)

        neg = jnp.full((tb, 1), _NEG, F32)
        zero = jnp.zeros((tb, 1), F32)
        init = ((neg, neg), (zero, zero), jnp.zeros((tb, LANES), F32))
        carry = lax.fori_loop(0, i, lambda j, c: block(j, c, False), init)
        ms, ls, acc = block(i, carry, True)
        o = acc / jnp.where(lo, ls[0], ls[1])
        o_ref[...] = o
        lse_ref[...] = jnp.where(lo, ms[0] + jnp.log(ls[0]), ms[1] + jnp.log(ls[1]))
        z = z_ref[...]
        y_ref[...] = o * (z * _sigmoid(z))

    qspec = lambda c0: pl.BlockSpec((None, tb, LANES), lambda bi, p, i: (bi, i, c0 + p))
    kspec = lambda c0: pl.BlockSpec((None, s, LANES), lambda bi, p, i: (bi, 0, c0 + p))
    ospec = pl.BlockSpec((None, tb, LANES), lambda bi, p, i: (bi, i, p))
    return pl.pallas_call(
        body, name=name, grid=(b, N_HEADS // 2, nq),
        in_specs=[qspec(q0), kspec(k0), kspec(v0), qspec(z0),
                  pl.BlockSpec((None, None, nq, 8, tb), lambda bi, p, i: (bi, p, 0, 0, 0))],
        out_specs=[ospec, ospec, ospec],
        out_shape=[jax.ShapeDtypeStruct((b, s, D_MODEL), F32)] * 3,
        compiler_params=_cp(("parallel", "parallel", "arbitrary")),
    )(proj3, proj3, proj3, proj3, cum_t)


_ST_LSE, _ST_DELTA = 0, 2


def _fox_prep(proj3, o3, lse3, dy3, *, name, tr=512):
    b, s, _ = proj3.shape
    z0 = _PAD_COLS["c_z"][0] // LANES

    def body(z_ref, o_ref, lse_ref, dy_ref, dz_ref, do_ref, st_ref):
        lane = lax.broadcasted_iota(jnp.int32, (tr, LANES), 1)
        lo = lane < HEAD_DIM
        z = z_ref[...]
        sz = _sigmoid(z)
        dy = dy_ref[...]
        o = o_ref[...]
        do = dy * (z * sz)
        dz_ref[...] = dy * o * (sz * (1.0 + z * (1.0 - sz)))
        do_ref[...] = do
        doo = do.astype(BF16).astype(F32) * o
        lse = lse_ref[...]
        cols = (_col(lse, 0), _col(lse, HEAD_DIM),
                jnp.sum(jnp.where(lo, doo, 0.0), axis=1, keepdims=True),
                jnp.sum(jnp.where(lo, 0.0, doo), axis=1, keepdims=True))
        st = jnp.zeros((tr, LANES), F32)
        for k, cvec in enumerate(cols):
            st = jnp.where(lane == k, cvec, st)
        st_ref[...] = st

    ospec = pl.BlockSpec((None, tr, LANES), lambda bi, p, i: (bi, i, p))
    return pl.pallas_call(
        body, name=name, grid=(b, N_HEADS // 2, s // tr),
        in_specs=[pl.BlockSpec((None, tr, LANES), lambda bi, p, i: (bi, i, z0 + p)), ospec, ospec, ospec],
        out_specs=[ospec, ospec, pl.BlockSpec((None, None, tr, LANES), lambda bi, p, i: (bi, p, i, 0))],
        out_shape=[jax.ShapeDtypeStruct((b, s, D_MODEL), F32), jax.ShapeDtypeStruct((b, s, D_MODEL), F32),
                   jax.ShapeDtypeStruct((b, N_HEADS // 2, s, LANES), F32)],
        compiler_params=_cp(("parallel", "parallel", "parallel")),
    )(proj3, o3, lse3, dy3)


def _fox_bwd(proj3, cum_t, do3, stats, *, name, tb):
    b, s, _ = proj3.shape
    nq = s // tb
    q0 = _PAD_COLS["c_q"][0] // LANES
    k0 = _PAD_COLS["c_k"][0] // LANES
    v0 = _PAD_COLS["c_v"][0] // LANES

    def body(q_ref, do_ref, st_ref, k_ref, v_ref, cumt_ref, dq_ref, dk_ref, dv_ref, cs_ref):
        j = pl.program_id(2)
        lo = lax.broadcasted_iota(jnp.int32, (tb, LANES), 1) < HEAD_DIM

        @pl.when(j == 0)
        def _():
            dq_ref[...] = jnp.zeros_like(dq_ref)

        kb = k_ref[...].astype(BF16)
        vb = v_ref[...].astype(BF16)
        ckv = cumt_ref[...]

        def block(i, carry, diagonal):
            qs = pl.ds(pl.multiple_of(i * tb, tb), tb)
            q = q_ref[qs, :] * _SCALE
            do = do_ref[qs, :]
            st = st_ref[qs, :]
            if diagonal:
                row = lax.broadcasted_iota(jnp.int32, (tb, tb), 0)
                col = lax.broadcasted_iota(jnp.int32, (tb, tb), 1)
                mask = row >= col
            dk, dv, cs = carry
            new_cs, dqs = [], []
            for hh in range(2):
                sel = lo if hh == 0 else jnp.logical_not(lo)
                qm = jnp.where(sel, q, 0.0).astype(BF16)
                dom = jnp.where(sel, do, 0.0).astype(BF16)
                sc = _dot_nt(qm, kb) - ckv[hh:hh + 1, :]
                if diagonal:
                    sc = jnp.where(mask, sc, _NEG)
                pr = jnp.exp(sc - _col(st, _ST_LSE + hh))
                ds = pr * (_dot_nt(dom, vb) - _col(st, _ST_DELTA + hh))
                dsb = ds.astype(BF16)
                dv = dv + _dot_tn(pr.astype(BF16), dom)
                dk = dk + _dot_tn(dsb, qm)
                new_cs.append(cs[hh] + jnp.sum(ds, axis=0, keepdims=True))
                dqs.append(_dot(dsb, kb))
            dq_ref[qs, :] += jnp.where(lo, dqs[0], dqs[1]) * _SCALE
            return (dk, dv, tuple(new_cs))

        zrow = jnp.zeros((1, tb), F32)
        init = (jnp.zeros((tb, LANES), F32), jnp.zeros((tb, LANES), F32), (zrow, zrow))
        carry = block(j, init, True)
        dk, dv, cs = lax.fori_loop(j + 1, nq, lambda i, c: block(i, c, False), carry)
        dk_ref[...] = dk
        dv_ref[...] = dv
        cs_ref[...] = jnp.zeros_like(cs_ref)
        cs_ref[0:1, :] = cs[0]
        cs_ref[1:2, :] = cs[1]

    full = lambda c0: pl.BlockSpec((None, s, LANES), lambda bi, p, j: (bi, 0, c0 + p))
    kspec = lambda c0: pl.BlockSpec((None, tb, LANES), lambda bi, p, j: (bi, j, c0 + p))
    ko = pl.BlockSpec((None, tb, LANES), lambda bi, p, j: (bi, j, p))
    ctspec = pl.BlockSpec((None, None, None, 8, tb), lambda bi, p, j: (bi, p, j, 0, 0))
    return pl.pallas_call(
        body, name=name, grid=(b, N_HEADS // 2, nq),
        in_specs=[full(q0), full(0), pl.BlockSpec((None, None, s, LANES), lambda bi, p, j: (bi, p, 0, 0)),
                  kspec(k0), kspec(v0), ctspec],
        out_specs=[full(0), ko, ko, ctspec],
        out_shape=[jax.ShapeDtypeStruct((b, s, D_MODEL), F32), jax.ShapeDtypeStruct((b, s, D_MODEL), F32),
                   jax.ShapeDtypeStruct((b, s, D_MODEL), F32),
                   jax.ShapeDtypeStruct((b, N_HEADS // 2, nq, 8, tb), F32)],
        compiler_params=_cp(("parallel", "parallel", "arbitrary")),
    )(proj3, do3, stats, proj3, proj3, cum_t)


def _rope(x, cos, sin_signed):
    w = x.shape[1]
    lane = lax.broadcasted_iota(jnp.int32, x.shape, 1)
    first = (lane % HEAD_DIM) < (HEAD_DIM // 2)
    rot = jnp.where(first, pltpu.roll(x, w - HEAD_DIM // 2, 1), pltpu.roll(x, HEAD_DIM // 2, 1))
    return x * cos + rot * sin_signed


def _swa_common(i, kc_ref, kp_ref, cq_ref, sq_ref, cp_ref, sp_ref):
    cq, sq, cpv, spv = cq_ref[...], sq_ref[...], cp_ref[...], sp_ref[...]
    kc = _rope(kc_ref[...], cq, sq).astype(BF16)
    kp = _rope(kp_ref[...], cpv, spv).astype(BF16)
    row = lax.broadcasted_iota(jnp.int32, (CHUNK, CHUNK), 0)
    col = lax.broadcasted_iota(jnp.int32, (CHUNK, CHUNK), 1)
    mask_c = col <= row
    mask_p = jnp.logical_and(col > row, i > 0)
    return cq, sq, cpv, spv, kc, kp, mask_c, mask_p


def _swa_fwd(proj3, k2, v2, cos, sin, sinks, *, name):
    b, s, _ = proj3.shape
    nb = s // CHUNK
    q0 = _PAD_COLS["b_q"][0] // 256
    z0 = _PAD_COLS["b_z"][0] // 256

    def body(q_ref, z_ref, kc_ref, kp_ref, vc_ref, vp_ref, cq_ref, sq_ref, cp_ref, sp_ref, sk_ref,
             y_ref, o_ref, lse_ref):
        i = pl.program_id(2)
        cq, sq, cpv, spv, kc, kp, mask_c, mask_p = _swa_common(i, kc_ref, kp_ref, cq_ref, sq_ref, cp_ref, sp_ref)
        vc = vc_ref[...].astype(BF16)
        vp = vp_ref[...].astype(BF16)
        lo = lax.broadcasted_iota(jnp.int32, (CHUNK, LANES), 1) < HEAD_DIM
        skv = sk_ref[...]
        for pp in range(2):
            q = _rope(q_ref[:, LANES * pp:LANES * (pp + 1)], cq, sq) * _SCALE
            os_, lses = [], []
            for hh in range(2):
                qm = jnp.where(lo if hh == 0 else jnp.logical_not(lo), q, 0.0).astype(BF16)
                sc = jnp.where(mask_c, _dot_nt(qm, kc), _NEG)
                sp_ = jnp.where(mask_p, _dot_nt(qm, kp), _NEG)
                sink = _col(skv, 2 * pp + hh)
                m = jnp.maximum(jnp.maximum(jnp.max(sc, axis=1, keepdims=True),
                                            jnp.max(sp_, axis=1, keepdims=True)), sink)
                pc = jnp.exp(sc - m)
                ppv = jnp.exp(sp_ - m)
                l = jnp.sum(pc, axis=1, keepdims=True) + jnp.sum(ppv, axis=1, keepdims=True) + jnp.exp(sink - m)
                os_.append((_dot(pc.astype(BF16), vc) + _dot(ppv.astype(BF16), vp)) / l)
                lses.append(m + jnp.log(l))
            o = jnp.where(lo, os_[0], os_[1])
            z = z_ref[:, LANES * pp:LANES * (pp + 1)]
            o_ref[:, LANES * pp:LANES * (pp + 1)] = o
            lse_ref[:, LANES * pp:LANES * (pp + 1)] = jnp.where(lo, lses[0], lses[1])
            y_ref[:, LANES * pp:LANES * (pp + 1)] = o * (z * _sigmoid(z))

    prev = lambda i: jnp.maximum(i - 1, 0)
    blk = pl.BlockSpec((None, CHUNK, 256), lambda bi, g, i: (bi, i, g))
    kcur = pl.BlockSpec((None, CHUNK, LANES), lambda bi, g, i: (bi, i, g))
    kprev = pl.BlockSpec((None, CHUNK, LANES), lambda bi, g, i: (bi, prev(i), g))
    tcur = pl.BlockSpec((CHUNK, LANES), lambda bi, g, i: (i, 0))
    tprev = pl.BlockSpec((CHUNK, LANES), lambda bi, g, i: (prev(i), 0))
    return pl.pallas_call(
        body, name=name, grid=(b, N_GROUPS, nb),
        in_specs=[pl.BlockSpec((None, CHUNK, 256), lambda bi, g, i: (bi, i, q0 + g)),
                  pl.BlockSpec((None, CHUNK, 256), lambda bi, g, i: (bi, i, z0 + g)),
                  kcur, kprev, kcur, kprev, tcur, tcur, tprev, tprev,
                  pl.BlockSpec((None, 1, LANES), lambda bi, g, i: (g, 0, 0))],
        out_specs=[blk, blk, blk],
        out_shape=[jax.ShapeDtypeStruct((b, s, D_MODEL), F32)] * 3,
        compiler_params=_cp(("parallel", "parallel", "parallel")),
    )(proj3, proj3, k2, k2, v2, v2, cos, sin, cos, sin, sinks)


def _swa_bwd(proj3, k2, v2, cos, sin, sinks, o3, lse3, dy3, *, name):
    b, s, _ = proj3.shape
    nb = s // CHUNK
    q0 = _PAD_COLS["b_q"][0] // 256
    z0 = _PAD_COLS["b_z"][0] // 256

    def body(q_ref, z_ref, kc_ref, kp_ref, vc_ref, vp_ref, cq_ref, sq_ref, cp_ref, sp_ref, sk_ref,
             o_ref, lse_ref, dy_ref, dq_ref, dz_ref, dkc_ref, dkp_ref, dvc_ref, dvp_ref, dsk_ref):
        i = pl.program_id(2)
        first = jnp.logical_and(pl.program_id(1) == 0, i == 0)

        @pl.when(first)
        def _():
            dsk_ref[...] = jnp.zeros_like(dsk_ref)

        cq, sq, cpv, spv, kc, kp, mask_c, mask_p = _swa_common(i, kc_ref, kp_ref, cq_ref, sq_ref, cp_ref, sp_ref)
        vc = vc_ref[...].astype(BF16)
        vp = vp_ref[...].astype(BF16)
        lo = lax.broadcasted_iota(jnp.int32, (CHUNK, LANES), 1) < HEAD_DIM
        lane1 = lax.broadcasted_iota(jnp.int32, (1, LANES), 1)
        skv = sk_ref[...]
        dkc = jnp.zeros((CHUNK, LANES), F32)
        dkp = jnp.zeros((CHUNK, LANES), F32)
        dvc = jnp.zeros((CHUNK, LANES), F32)
        dvp = jnp.zeros((CHUNK, LANES), F32)
        dsk_row = jnp.zeros((1, LANES), F32)
        for pp in range(2):
            sl = slice(LANES * pp, LANES * (pp + 1))
            q = _rope(q_ref[:, sl], cq, sq) * _SCALE
            z = z_ref[:, sl]
            sz = _sigmoid(z)
            dy = dy_ref[:, sl]
            o = o_ref[:, sl]
            lse = lse_ref[:, sl]
            do = dy * (z * sz)
            dz_ref[:, sl] = dy * o * (sz * (1.0 + z * (1.0 - sz)))
            dqs = []
            for hh in range(2):
                sel = lo if hh == 0 else jnp.logical_not(lo)
                qm = jnp.where(sel, q, 0.0).astype(BF16)
                dom = jnp.where(sel, do, 0.0).astype(BF16)
                lse_h = _col(lse, hh * HEAD_DIM)
                sink = _col(skv, 2 * pp + hh)
                pc = jnp.exp(jnp.where(mask_c, _dot_nt(qm, kc), _NEG) - lse_h)
                ppv = jnp.exp(jnp.where(mask_p, _dot_nt(qm, kp), _NEG) - lse_h)
                dpc, dpp = _dot_nt(dom, vc), _dot_nt(dom, vp)
                dl = jnp.sum(pc * dpc, axis=1, keepdims=True) + jnp.sum(ppv * dpp, axis=1, keepdims=True)
                dsc = pc * (dpc - dl)
                dsp = ppv * (dpp - dl)
                dsink = -jnp.sum(jnp.exp(sink - lse_h) * dl, axis=0, keepdims=True)
                dsk_row = dsk_row + jnp.where(lane1 == 2 * pp + hh, dsink, 0.0)
                dscb, dspb = dsc.astype(BF16), dsp.astype(BF16)
                dqs.append(_dot(dscb, kc) + _dot(dspb, kp))
                dkc = dkc + _dot_tn(dscb, qm)
                dkp = dkp + _dot_tn(dspb, qm)
                dvc = dvc + _dot_tn(pc.astype(BF16), dom)
                dvp = dvp + _dot_tn(ppv.astype(BF16), dom)
            dq_ref[:, sl] = _rope(jnp.where(lo, dqs[0], dqs[1]) * _SCALE, cq, -sq)
        dkc = _rope(dkc, cq, -sq)
        dkp = _rope(dkp, cpv, -spv)
        dkc_ref[...] = dkc + pltpu.roll(dkc, HEAD_DIM, 1)
        dkp_ref[...] = dkp + pltpu.roll(dkp, HEAD_DIM, 1)
        dvc_ref[...] = dvc + pltpu.roll(dvc, HEAD_DIM, 1)
        dvp_ref[...] = dvp + pltpu.roll(dvp, HEAD_DIM, 1)
        dsk_ref[...] += dsk_row

    prev = lambda i: jnp.maximum(i - 1, 0)
    blk = pl.BlockSpec((None, CHUNK, 256), lambda g, bi, i: (bi, i, g))
    kcur = pl.BlockSpec((None, CHUNK, LANES), lambda g, bi, i: (bi, i, g))
    kprev = pl.BlockSpec((None, CHUNK, LANES), lambda g, bi, i: (bi, prev(i), g))
    tcur = pl.BlockSpec((CHUNK, LANES), lambda g, bi, i: (i, 0))
    tprev = pl.BlockSpec((CHUNK, LANES), lambda g, bi, i: (prev(i), 0))
    skspec = pl.BlockSpec((None, 1, LANES), lambda g, bi, i: (g, 0, 0))
    kv_shape = jax.ShapeDtypeStruct((b, s, 512), F32)
    return pl.pallas_call(
        body, name=name, grid=(N_GROUPS, b, nb),
        in_specs=[pl.BlockSpec((None, CHUNK, 256), lambda g, bi, i: (bi, i, q0 + g)),
                  pl.BlockSpec((None, CHUNK, 256), lambda g, bi, i: (bi, i, z0 + g)),
                  kcur, kprev, kcur, kprev, tcur, tcur, tprev, tprev, skspec, blk, blk, blk],
        out_specs=[blk, blk, kcur, kcur, kcur, kcur, skspec],
        out_shape=[jax.ShapeDtypeStruct((b, s, D_MODEL), F32), jax.ShapeDtypeStruct((b, s, D_MODEL), F32),
                   kv_shape, kv_shape, kv_shape, kv_shape, jax.ShapeDtypeStruct((N_GROUPS, 1, LANES), F32)],
        compiler_params=_cp(("arbitrary", "arbitrary", "arbitrary")),
    )(proj3, proj3, k2, k2, v2, v2, cos, sin, cos, sin, sinks, o3, lse3, dy3)


def _merge_fwd(proj, br, gb, *, name, tm=256):
    t = proj.shape[0]
    g0 = _PAD_COLS["gates"][0] // D_MODEL

    def body(g_ref, a_ref, b_ref, c_ref, gb_ref, o_ref):
        acc = None
        for i, r in enumerate((a_ref, b_ref, c_ref)):
            gate = _sigmoid(g_ref[:, D_MODEL * i:D_MODEL * (i + 1)] + gb_ref[i:i + 1, :])
            term = gate * r[...]
            acc = term if acc is None else acc + term
        o_ref[...] = acc.astype(BF16)

    row = pl.BlockSpec((tm, D_MODEL), lambda i: (i, 0))
    return pl.pallas_call(
        body, name=name, grid=(t // tm,),
        in_specs=[pl.BlockSpec((tm, 3 * D_MODEL), lambda i: (i, g0)), row, row, row,
                  pl.BlockSpec((3, D_MODEL), lambda i: (0, 0))],
        out_specs=row, out_shape=jax.ShapeDtypeStruct((t, D_MODEL), BF16),
        compiler_params=_cp(("parallel",)),
    )(proj, br[0], br[1], br[2], gb)


def _merge_bwd(proj, br, gb, dm, *, name, tm=256):
    t = proj.shape[0]
    g0 = _PAD_COLS["gates"][0] // D_MODEL

    def body(g_ref, a_ref, b_ref, c_ref, gb_ref, dm_ref, da_ref, db_ref, dc_ref, dg_ref, dgb_ref):
        @pl.when(pl.program_id(0) == 0)
        def _():
            dgb_ref[...] = jnp.zeros_like(dgb_ref)

        dmv = dm_ref[...]
        for i, (r, dr) in enumerate(((a_ref, da_ref), (b_ref, db_ref), (c_ref, dc_ref))):
            gate = _sigmoid(g_ref[:, D_MODEL * i:D_MODEL * (i + 1)] + gb_ref[i:i + 1, :])
            dr[...] = (dmv * gate).astype(BF16)
            dg = dmv * r[...] * gate * (1.0 - gate)
            dg_ref[:, D_MODEL * i:D_MODEL * (i + 1)] = dg
            dgb_ref[i:i + 1, :] += jnp.sum(dg, axis=0, keepdims=True)

    row = pl.BlockSpec((tm, D_MODEL), lambda i: (i, 0))
    rowb = jax.ShapeDtypeStruct((t, D_MODEL), BF16)
    return pl.pallas_call(
        body, name=name, grid=(t // tm,),
        in_specs=[pl.BlockSpec((tm, 3 * D_MODEL), lambda i: (i, g0)), row, row, row,
                  pl.BlockSpec((3, D_MODEL), lambda i: (0, 0)), row],
        out_specs=[row, row, row, pl.BlockSpec((tm, 3 * D_MODEL), lambda i: (i, 0)),
                   pl.BlockSpec((8, D_MODEL), lambda i: (0, 0))],
        out_shape=[rowb, rowb, rowb, jax.ShapeDtypeStruct((t, 3 * D_MODEL), F32),
                   jax.ShapeDtypeStruct((8, D_MODEL), F32)],
        compiler_params=_cp(("arbitrary",)),
    )(proj, br[0], br[1], br[2], gb, dm)


def _rope_tables(s):
    pos = jnp.arange(s, dtype=F32)
    inv_freq = ROPE_THETA ** (-jnp.arange(0, HEAD_DIM, 2, dtype=F32) / HEAD_DIM)
    ang = pos[:, None] * inv_freq[None, :]
    cos, sin = jnp.cos(ang), jnp.sin(ang)
    return jnp.tile(cos, (1, 4)), jnp.tile(jnp.concatenate([-sin, sin], axis=1), (1, 2))


def _dup_kv(proj3, name):
    b, s, _ = proj3.shape
    p0, sz = _PAD_COLS[name]
    kv = proj3[:, :, p0:p0 + sz].reshape(b, s, N_GROUPS, 1, HEAD_DIM)
    return jnp.broadcast_to(kv, (b, s, N_GROUPS, 2, HEAD_DIM)).reshape(b, s, 512)


def _pair_rows(cum, tb):
    b, s, _ = cum.shape
    t = jnp.transpose(cum[:, :, :N_HEADS], (0, 2, 1)).reshape(b, N_HEADS // 2, 2, s // tb, tb)
    return jnp.pad(jnp.transpose(t, (0, 1, 3, 2, 4)), ((0, 0), (0, 0), (0, 0), (0, 6), (0, 0)))


def _layer_params(wl):
    return dict(
        dtb=_group_lanes(wl["dt_bias"]), alog=_group_lanes(wl["a_log"]), dsk=_group_lanes(wl["d_skip"]),
        nw=wl["ssm_norm_w"].reshape(N_GROUPS, 1, 256), sinks=_group_lanes(wl["sinks"]),
        fb=jnp.pad(wl["f_bias"], (0, LANES - N_HEADS)).reshape(1, LANES))


def _layer_fwd(x, wl, tabs, bsz, li, tb):
    t = x.shape[0]
    s = t // bsz
    cos, sin = tabs
    lp = _layer_params(wl)
    n = lambda k: f"l{li}_{k}"
    h = _rms_fwd(x, wl["norm_w"], name=n("rms_fwd"))
    proj = _mm(h, wl["w_in"], tm=1024, tn=768, tk=1024, name=n("mm_proj"))
    proj3 = proj.reshape(bsz, s, N_PAD)
    xact3 = _conv_fwd(proj3, wl["conv_w"], wl["conv_b"], name=n("conv_fwd"))
    ya3, ypre3, hst = _ssd_fwd(proj3, xact3, lp["dtb"], lp["alog"], lp["dsk"], lp["nw"], name=n("ssd_fwd"))
    k2, v2 = _dup_kv(proj3, "b_k"), _dup_kv(proj3, "b_v")
    yb3, ob3, lseb3 = _swa_fwd(proj3, k2, v2, cos, sin, lp["sinks"], name=n("swa_fwd"))
    cum = _fgate_fwd(proj3, lp["fb"], name=n("fgate_fwd"))
    cum_t = _pair_rows(cum, tb)
    yc3, oc3, lsec3 = _fox_fwd(proj3, cum_t, name=n("fox_fwd"), tb=tb)
    ys = [v.reshape(t, D_MODEL) for v in (ya3, yb3, yc3)]
    br = [_mm(ys[i], wl["w_proj"][i], tm=1024, tn=1024, tk=1024, name=n(f"mm_br{i}")) for i in range(3)]
    merged = _merge_fwd(proj, br, wl["gate_bias"], name=n("merge_fwd"))
    x_new = _mm(merged, wl["w_out"], tm=1024, tn=1024, tk=1024, add=x, name=n("mm_out"))
    saved = dict(x=x, h=h, proj=proj, xact3=xact3, ypre3=ypre3, hst=hst, k2=k2, v2=v2, ob3=ob3, lseb3=lseb3,
                 cum_t=cum_t, oc3=oc3, lsec3=lsec3, ys=ys, br=br, merged=merged, lp=lp)
    return x_new, saved


def _layer_bwd(dx, wl, sv, tabs, bsz, li, tb):
    t = dx.shape[0]
    s = t // bsz
    cos, sin = tabs
    lp = sv["lp"]
    n = lambda k: f"l{li}_{k}"
    proj = sv["proj"]
    proj3 = proj.reshape(bsz, s, N_PAD)
    g = {}
    dmerged = _mm(dx, wl["w_out"], tb=True, tm=1024, tn=1024, tk=1024, name=n("mm_dmerged"))
    g["w_out"] = _mm(sv["merged"], dx, ta=True, tm=1024, tn=1024, tk=512, name=n("mm_dwout"))
    dbr0, dbr1, dbr2, dgates, dgb = _merge_bwd(proj, sv["br"], wl["gate_bias"], dmerged, name=n("merge_bwd"))
    g["gate_bias"] = dgb[:3]
    dbr = (dbr0, dbr1, dbr2)
    dys = [_mm(dbr[i], wl["w_proj"][i], tb=True, tm=1024, tn=1024, tk=1024, name=n(f"mm_dy{i}"))
           for i in range(3)]
    g["w_proj"] = jnp.stack([_mm(sv["ys"][i], dbr[i], ta=True, tm=1024, tn=1024, tk=512, name=n(f"mm_dwproj{i}"))
                             for i in range(3)])
    dy3 = [v.reshape(bsz, s, D_MODEL) for v in dys]

    (dxs, dbm, dcm, daz, dadt, ddtb, dalog, ddsk, dnw) = _ssd_bwd(
        proj3, sv["xact3"], lp["dtb"], lp["alog"], lp["dsk"], lp["nw"], sv["ypre3"], sv["hst"], dy3[0],
        name=n("ssd_bwd"))
    g["dt_bias"], g["a_log"], g["d_skip"] = _ungroup_lanes(ddtb), _ungroup_lanes(dalog), _ungroup_lanes(ddsk)
    g["ssm_norm_w"] = dnw.reshape(D_MODEL)
    dact = jnp.concatenate([dxs, dbm, dcm], axis=2)
    dxbc, dwb = _conv_bwd(proj3, wl["conv_w"], wl["conv_b"], dact, name=n("conv_bwd"))
    g["conv_w"], g["conv_b"] = dwb[:CONV_WIDTH], dwb[CONV_WIDTH]

    dbq, dbz, dkc, dkp, dvc, dvp, dsk = _swa_bwd(proj3, sv["k2"], sv["v2"], cos, sin, lp["sinks"], sv["ob3"],
                                                 sv["lseb3"], dy3[1], name=n("swa_bwd"))
    g["sinks"] = _ungroup_lanes(dsk)

    def fold(cur, prv):
        shifted = jnp.concatenate([prv[:, CHUNK:], jnp.zeros_like(prv[:, :CHUNK])], axis=1)
        tot = cur + shifted
        return tot.reshape(bsz, s, N_GROUPS, 2, HEAD_DIM)[:, :, :, 0].reshape(bsz, s, 256)

    dbk, dbv = fold(dkc, dkp), fold(dvc, dvp)

    dcz, do3, stats = _fox_prep(proj3, sv["oc3"], sv["lsec3"], dy3[2], name=n("fox_prep"))
    dcq, dck, dcv, csum = _fox_bwd(proj3, sv["cum_t"], do3, stats, name=n("fox_bwd"), tb=tb)
    csum = jnp.transpose(csum[:, :, :, :2], (0, 1, 3, 2, 4)).reshape(bsz, N_HEADS, s)
    dcum = -jnp.transpose(csum, (0, 2, 1))
    dcum = jnp.pad(dcum, ((0, 0), (0, 0), (0, LANES - N_HEADS)))
    dcf, dfb = _fgate_bwd(proj3, lp["fb"], dcum, name=n("fgate_bwd"))
    g["f_bias"] = dfb[0, :N_HEADS]

    parts = {"gates": dgates.reshape(bsz, s, 3 * D_MODEL), "xbc": dxbc, "a_z": daz, "b_q": dbq, "b_z": dbz,
             "c_q": dcq, "c_k": dck, "c_v": dcv, "c_z": dcz, "b_k": dbk, "b_v": dbv, "a_dt": dadt, "c_f": dcf}
    dproj = jnp.concatenate([parts[name] for name, _ in _PAD_ORDER]
                            + [jnp.zeros((bsz, s, N_PAD - N_USED), F32)], axis=2).reshape(t, N_PAD)
    dh = _mm(dproj, wl["w_in"], tb=True, tm=1024, tn=1024, tk=768, name=n("mm_dh"))
    g["w_in"] = _unpad_w_in(_mm(sv["h"], dproj, ta=True, tm=1024, tn=768, tk=512, name=n("mm_dwin")))
    dx_in, dnorm = _rms_bwd(sv["x"], wl["norm_w"], dh, dx, name=n("rms_bwd"))
    g["norm_w"] = dnorm[0]
    return dx_in, g


def _local_step(x, target, wls, final_norm_w, tb=512):
    bsz, s, d = x.shape
    t = bsz * s
    tabs = _rope_tables(s)
    xc = x.reshape(t, d)
    saved = []
    for li, wl in enumerate(wls):
        xc, sv = _layer_fwd(xc, wl, tabs, bsz, li, tb)
        saved.append(sv)
    loss, dx, dfw = _final_loss(xc, final_norm_w, target.reshape(t, d), name="final_loss")
    grads = [None] * len(wls)
    for li in reversed(range(len(wls))):
        dx, grads[li] = _layer_bwd(dx, wls[li], saved[li], tabs, bsz, li, tb)
    return loss[0, 0], dx.reshape(bsz, s, d), grads, dfw[0]


_HBM = pl.BlockSpec(memory_space=pltpu.HBM)


def _chip_peers(x, y):
    return [(1 - x, y), (x, 1 - y), (1 - x, 1 - y)]


def _gather_weights(arrs, *, name):
    n = len(arrs)

    def body(*refs):
        ins, outs = refs[:n], refs[n:2 * n]
        ici_send, ici_recv, d2d_send, d2d_recv = refs[2 * n:]
        x, y, c = lax.axis_index("x"), lax.axis_index("y"), lax.axis_index("c")
        me = 2 * x + y
        peers = _chip_peers(x, y)
        sib = (x, y, 1 - c)
        sends, fwds = [], []
        for a in range(n):
            for k, (px, py) in enumerate(peers):
                cp = pltpu.make_async_remote_copy(
                    src_ref=ins[a].at[c], dst_ref=outs[a].at[me, c], send_sem=ici_send.at[a, k],
                    recv_sem=ici_recv.at[a, k], device_id=(px, py, c), device_id_type=MESH)
                cp.start()
                sends.append(cp)
        for a in range(n):
            for k, (px, py) in enumerate(peers):
                slot = 2 * px + py
                pltpu.make_async_remote_copy(
                    src_ref=ins[a].at[c], dst_ref=outs[a].at[slot, c], send_sem=ici_send.at[a, k],
                    recv_sem=ici_recv.at[a, k], device_id=(px, py, c), device_id_type=MESH).wait_recv()
                fw = pltpu.make_async_remote_copy(
                    src_ref=outs[a].at[slot, c], dst_ref=outs[a].at[slot, c], send_sem=d2d_send.at[a, k],
                    recv_sem=d2d_recv.at[a, k], device_id=sib, device_id_type=MESH)
                fw.start()
                fwds.append(fw)
        for a in range(n):
            for k, (px, py) in enumerate(peers):
                slot = 2 * px + py
                pltpu.make_async_remote_copy(
                    src_ref=outs[a].at[slot, 1 - c], dst_ref=outs[a].at[slot, 1 - c], send_sem=d2d_send.at[a, k],
                    recv_sem=d2d_recv.at[a, k], device_id=sib, device_id_type=MESH).wait_recv()
        for cp in sends + fwds:
            cp.wait_send()

    out_shape = [jax.ShapeDtypeStruct((N_CHIPS,) + a.shape, a.dtype) for a in arrs]
    return pl.pallas_call(
        body, name=name, out_shape=out_shape, in_specs=[_HBM] * n, out_specs=[_HBM] * n,
        scratch_shapes=[pltpu.SemaphoreType.DMA((n, 3)), pltpu.SemaphoreType.DMA((n, 3)),
                        pltpu.SemaphoreType.DMA((n, 3)), pltpu.SemaphoreType.DMA((n, 3))],
    )(*arrs)


def _pair_exchange(arrs, *, name):
    n = len(arrs)

    def body(*refs):
        ins, outs = refs[:n], refs[n:2 * n]
        send, recv = refs[2 * n:]
        x, y, c = lax.axis_index("x"), lax.axis_index("y"), lax.axis_index("c")
        sib = (x, y, 1 - c)
        cps = []
        for a in range(n):
            for k in range(N_CHIPS):
                cp = pltpu.make_async_remote_copy(
                    src_ref=ins[a].at[k, 1 - c], dst_ref=outs[a].at[k], send_sem=send.at[a, k],
                    recv_sem=recv.at[a, k], device_id=sib, device_id_type=MESH)
                cp.start()
                cps.append(cp)
        for cp in cps:
            cp.wait()

    out_shape = [jax.ShapeDtypeStruct((N_CHIPS,) + a.shape[2:], a.dtype) for a in arrs]
    return pl.pallas_call(
        body, name=name, out_shape=out_shape, in_specs=[_HBM] * n, out_specs=[_HBM] * n,
        scratch_shapes=[pltpu.SemaphoreType.DMA((n, N_CHIPS)), pltpu.SemaphoreType.DMA((n, N_CHIPS))],
    )(*arrs)


def _chip_exchange(arrs, *, name):
    n = len(arrs)

    def body(*refs):
        ins, outs = refs[:n], refs[n:2 * n]
        send, recv = refs[2 * n:]
        x, y, c = lax.axis_index("x"), lax.axis_index("y"), lax.axis_index("c")
        me = 2 * x + y
        peers = _chip_peers(x, y)
        cps = []
        for a in range(n):
            for k, (px, py) in enumerate(peers):
                cp = pltpu.make_async_remote_copy(
                    src_ref=ins[a].at[2 * px + py], dst_ref=outs[a].at[me], send_sem=send.at[a, k],
                    recv_sem=recv.at[a, k], device_id=(px, py, c), device_id_type=MESH)
                cp.start()
                cps.append(cp)
        for a in range(n):
            for k, (px, py) in enumerate(peers):
                pltpu.make_async_remote_copy(
                    src_ref=ins[a].at[2 * px + py], dst_ref=outs[a].at[2 * px + py], send_sem=send.at[a, k],
                    recv_sem=recv.at[a, k], device_id=(px, py, c), device_id_type=MESH).wait_recv()
        for cp in cps:
            cp.wait_send()

    out_shape = [jax.ShapeDtypeStruct(a.shape, a.dtype) for a in arrs]
    return pl.pallas_call(
        body, name=name, out_shape=out_shape, in_specs=[_HBM] * n, out_specs=[_HBM] * n,
        scratch_shapes=[pltpu.SemaphoreType.DMA((n, 3)), pltpu.SemaphoreType.DMA((n, 3))],
    )(*arrs)


def _pair_share(arrs, *, name):
    n = len(arrs)

    def body(*refs):
        ins, outs = refs[:n], refs[n:2 * n]
        send, recv = refs[2 * n:]
        x, y, c = lax.axis_index("x"), lax.axis_index("y"), lax.axis_index("c")
        sib = (x, y, 1 - c)
        cps = []
        for a in range(n):
            cp = pltpu.make_async_remote_copy(
                src_ref=ins[a], dst_ref=outs[a], send_sem=send.at[a], recv_sem=recv.at[a],
                device_id=sib, device_id_type=MESH)
            cp.start()
            cps.append(cp)
        for cp in cps:
            cp.wait()

    out_shape = [jax.ShapeDtypeStruct(a.shape, a.dtype) for a in arrs]
    return pl.pallas_call(
        body, name=name, out_shape=out_shape, in_specs=[_HBM] * n, out_specs=[_HBM] * n,
        scratch_shapes=[pltpu.SemaphoreType.DMA((n,)), pltpu.SemaphoreType.DMA((n,))],
    )(*arrs)


def _allreduce_small(buf, *, name):
    r = buf.shape[0]

    def body(in_ref, out_ref, land, send, recv):
        x, y, c = lax.axis_index("x"), lax.axis_index("y"), lax.axis_index("c")
        me = 4 * x + 2 * y + c
        land[me] = in_ref[...]
        cps = []
        for k in range(1, N_DEV):
            px, py, pc = x ^ ((k >> 2) & 1), y ^ ((k >> 1) & 1), c ^ (k & 1)
            cp = pltpu.make_async_remote_copy(
                src_ref=in_ref, dst_ref=land.at[me], send_sem=send.at[k - 1], recv_sem=recv.at[k - 1],
                device_id=(px, py, pc), device_id_type=MESH)
            cp.start()
            cps.append(cp)
        for k in range(1, N_DEV):
            px, py, pc = x ^ ((k >> 2) & 1), y ^ ((k >> 1) & 1), c ^ (k & 1)
            pltpu.make_async_remote_copy(
                src_ref=in_ref, dst_ref=land.at[4 * px + 2 * py + pc], send_sem=send.at[k - 1],
                recv_sem=recv.at[k - 1], device_id=(px, py, pc), device_id_type=MESH).wait_recv()
        for cp in cps:
            cp.wait_send()
        acc = land[0]
        for k in range(1, N_DEV):
            acc = acc + land[k]
        out_ref[...] = acc

    vm = pl.BlockSpec(memory_space=pltpu.VMEM)
    return pl.pallas_call(
        body, name=name, out_shape=jax.ShapeDtypeStruct((r, LANES), F32), in_specs=[vm], out_specs=vm,
        scratch_shapes=[pltpu.VMEM((N_DEV, r, LANES), F32), pltpu.SemaphoreType.DMA((N_DEV - 1,)),
                        pltpu.SemaphoreType.DMA((N_DEV - 1,))],
    )(buf)


def _rows2d(a):
    return a.reshape(-1, a.shape[-1])


def _row_tile(rows, cols, n_arrays, budget=20 * 1024 * 1024):
    best = 8 if rows % 8 == 0 else rows
    tr = 8
    while tr <= rows:
        if rows % tr == 0 and tr * cols * 4 * n_arrays * 2 <= budget:
            best = tr
        tr *= 2
    return best


def _add_slot_layer(full, other, *, name):
    _, _, r, cdim = full.shape
    tr = _row_tile(r, cdim, 4)

    def body(c_ref, a_ref, b_ref, o_ref, ob_ref):
        sm = a_ref[...] + b_ref[...]
        o_ref[...] = sm
        ob_ref[...] = sm.astype(BF16)

    c = lax.axis_index("c").astype(jnp.int32).reshape(1)
    blk = pl.BlockSpec((None, tr, cdim), lambda k, i, c_ref: (k, i, 0))
    return pl.pallas_call(
        body, name=name,
        grid_spec=pltpu.PrefetchScalarGridSpec(
            num_scalar_prefetch=1, grid=(N_CHIPS, r // tr),
            in_specs=[pl.BlockSpec((None, None, tr, cdim), lambda k, i, c_ref: (k, c_ref[0], i, 0)), blk],
            out_specs=[blk, blk]),
        out_shape=[jax.ShapeDtypeStruct((N_CHIPS, r, cdim), F32), jax.ShapeDtypeStruct((N_CHIPS, r, cdim), BF16)],
        compiler_params=_cp(("parallel", "parallel")),
    )(c, full, other)


def _sum_slots(parts, pair, *, name):
    _, r, cdim = parts.shape
    tr = _row_tile(r, cdim, 5)

    def body(me_ref, p_ref, own_ref, o_ref):
        me = me_ref[0]
        acc = None
        for k in range(N_CHIPS):
            term = jnp.where(me == k, own_ref[...], p_ref[k].astype(F32))
            acc = term if acc is None else acc + term
        o_ref[...] = acc

    me = (2 * lax.axis_index("x") + lax.axis_index("y")).astype(jnp.int32).reshape(1)
    return pl.pallas_call(
        body, name=name,
        grid_spec=pltpu.PrefetchScalarGridSpec(
            num_scalar_prefetch=1, grid=(r // tr,),
            in_specs=[pl.BlockSpec((N_CHIPS, tr, cdim), lambda i, me_ref: (0, i, 0)),
                      pl.BlockSpec((None, tr, cdim), lambda i, me_ref: (me_ref[0], i, 0))],
            out_specs=pl.BlockSpec((tr, cdim), lambda i, me_ref: (i, 0))),
        out_shape=jax.ShapeDtypeStruct((r, cdim), F32),
        compiler_params=_cp(("parallel",)),
    )(me, parts, pair)


def _adamw(w, g, m, v, *, name):
    r, cdim = w.shape
    tr = _row_tile(r, cdim, 7)
    c1 = 1.0 - ADAM_B1 ** ADAM_STEP
    c2 = 1.0 - ADAM_B2 ** ADAM_STEP

    def body(w_ref, g_ref, m_ref, v_ref, d_ref, nm_ref, nv_ref):
        gv = g_ref[...]
        mn = ADAM_B1 * m_ref[...] + (1.0 - ADAM_B1) * gv
        vn = ADAM_B2 * v_ref[...] + (1.0 - ADAM_B2) * (gv * gv)
        nm_ref[...] = mn
        nv_ref[...] = vn
        d_ref[...] = -ADAM_LR * ((mn / c1) / (jnp.sqrt(vn / c2) + ADAM_EPS) + ADAM_WD * w_ref[...])

    blk = pl.BlockSpec((tr, cdim), lambda i: (i, 0))
    sh = jax.ShapeDtypeStruct((r, cdim), F32)
    return pl.pallas_call(
        body, name=name, grid=(r // tr,), in_specs=[blk] * 4, out_specs=[blk] * 3, out_shape=[sh] * 3,
        compiler_params=_cp(("parallel",)),
    )(w, g, m, v)


_SMALL = ("norm_w", "conv_b", "dt_bias", "a_log", "d_skip", "ssm_norm_w", "sinks", "f_bias", "final_norm_w",
          "conv_w", "gate_bias")


def _pack(vals):
    flat = jnp.concatenate([v.reshape(-1) for v in vals])
    rows = -(-flat.shape[0] // LANES)
    rows = -(-rows // 8) * 8
    return jnp.pad(flat, (0, rows * LANES - flat.shape[0])).reshape(rows, LANES)


def _unpack(buf, shapes):
    flat = buf.reshape(-1)
    out, off = [], 0
    for sh in shapes:
        sz = int(np.prod(sh))
        out.append(flat[off:off + sz].reshape(sh))
        off += sz
    return out


def kernel(x, norm_w, w_in, conv_w, conv_b, dt_bias, a_log, d_skip, ssm_norm_w, sinks, f_bias, gate_bias, w_proj, w_out, final_norm_w, loss_target, m_norm_w, m_w_in, m_conv_w, m_conv_b, m_dt_bias, m_a_log, m_d_skip, m_ssm_norm_w, m_sinks, m_f_bias, m_gate_bias, m_w_proj, m_w_out, m_final_norm_w, v_norm_w, v_w_in, v_conv_w, v_conv_b, v_dt_bias, v_a_log, v_d_skip, v_ssm_norm_w, v_sinks, v_f_bias, v_gate_bias, v_w_proj, v_w_out, v_final_norm_w):
    depth = w_in.shape[0]
    chip = 2 * lax.axis_index("x") + lax.axis_index("y")

    own = [w_in.astype(BF16), w_proj.astype(BF16), w_out.astype(BF16), conv_w, gate_bias]
    gathered = _gather_weights(own, name="gather_weights")

    def whole(a, li, axis):
        return jnp.concatenate([jnp.where(chip == k, own[a][li], gathered[a][k, li]) for k in range(N_CHIPS)],
                               axis=axis)

    wls = []
    for li in range(depth):
        wls.append(dict(
            norm_w=norm_w[li], w_in=_pad_w_in(whole(0, li, 1)),
            conv_w=whole(3, li, 1), conv_b=conv_b[li], dt_bias=dt_bias[li], a_log=a_log[li], d_skip=d_skip[li],
            ssm_norm_w=ssm_norm_w[li], sinks=sinks[li], f_bias=f_bias[li], gate_bias=whole(4, li, 1),
            w_proj=whole(1, li, 1),
            w_out=whole(2, li, 0)))

    loss_part, grad_x, grads, d_final = _local_step(x, loss_target, wls, final_norm_w)
    loss = lax.psum(loss_part, ("x", "y", "c"))

    c_in = w_in.shape[2]
    r_proj = w_proj.shape[2]
    r_out = w_out.shape[1]
    full_in = jnp.stack([jnp.stack([grads[li]["w_in"][:, k * c_in:(k + 1) * c_in] for li in range(depth)])
                         for k in range(N_CHIPS)])
    full_proj = jnp.stack([jnp.stack([grads[li]["w_proj"][:, k * r_proj:(k + 1) * r_proj].reshape(-1, D_MODEL)
                                      for li in range(depth)]) for k in range(N_CHIPS)])
    full_out = jnp.stack([jnp.stack([grads[li]["w_out"][k * r_out:(k + 1) * r_out] for li in range(depth)])
                          for k in range(N_CHIPS)])
    fulls = [full_in, full_proj, full_out]
    others = _pair_exchange(fulls, name="grad_pair_exchange")
    pair = [_add_slot_layer(f, o, name=f"grad_pair_add{i}") for i, (f, o) in enumerate(zip(fulls, others))]
    parts = _chip_exchange([p[1] for p in pair], name="grad_chip_exchange")
    mine = [_sum_slots(p, pr[0], name=f"grad_slot_sum{i}") for i, (p, pr) in enumerate(zip(parts, pair))]
    theirs = _pair_share(mine, name="grad_pair_share")
    core = lax.axis_index("c")
    red_in, red_proj, red_out = [jnp.stack([jnp.where(core == li, m, t) for li in range(depth)])
                                 for m, t in zip(mine, theirs)]
    grad_w_in = red_in
    grad_w_proj = red_proj.reshape(w_proj.shape)
    grad_w_out = red_out

    small_full = {
        "norm_w": jnp.stack([g["norm_w"] for g in grads]), "conv_b": jnp.stack([g["conv_b"] for g in grads]),
        "dt_bias": jnp.stack([g["dt_bias"] for g in grads]), "a_log": jnp.stack([g["a_log"] for g in grads]),
        "d_skip": jnp.stack([g["d_skip"] for g in grads]),
        "ssm_norm_w": jnp.stack([g["ssm_norm_w"] for g in grads]),
        "sinks": jnp.stack([g["sinks"] for g in grads]), "f_bias": jnp.stack([g["f_bias"] for g in grads]),
        "final_norm_w": d_final,
        "conv_w": jnp.stack([g["conv_w"] for g in grads]), "gate_bias": jnp.stack([g["gate_bias"] for g in grads])}
    shapes = [small_full[k].shape for k in _SMALL]
    summed = _unpack(_allreduce_small(_pack([small_full[k] for k in _SMALL]), name="allreduce_small"), shapes)
    gsmall = dict(zip(_SMALL, summed))
    gsmall["conv_w"] = lax.dynamic_slice_in_dim(gsmall["conv_w"], chip * conv_w.shape[2], conv_w.shape[2], axis=2)
    gsmall["gate_bias"] = lax.dynamic_slice_in_dim(gsmall["gate_bias"], chip * gate_bias.shape[2],
                                                   gate_bias.shape[2], axis=2)

    w_small = dict(norm_w=norm_w, conv_b=conv_b, dt_bias=dt_bias, a_log=a_log, d_skip=d_skip,
                   ssm_norm_w=ssm_norm_w, sinks=sinks, f_bias=f_bias, final_norm_w=final_norm_w, conv_w=conv_w,
                   gate_bias=gate_bias)
    m_small = dict(norm_w=m_norm_w, conv_b=m_conv_b, dt_bias=m_dt_bias, a_log=m_a_log, d_skip=m_d_skip,
                   ssm_norm_w=m_ssm_norm_w, sinks=m_sinks, f_bias=m_f_bias, final_norm_w=m_final_norm_w,
                   conv_w=m_conv_w, gate_bias=m_gate_bias)
    v_small = dict(norm_w=v_norm_w, conv_b=v_conv_b, dt_bias=v_dt_bias, a_log=v_a_log, d_skip=v_d_skip,
                   ssm_norm_w=v_ssm_norm_w, sinks=v_sinks, f_bias=v_f_bias, final_norm_w=v_final_norm_w,
                   conv_w=v_conv_w, gate_bias=v_gate_bias)
    sshapes = [w_small[k].shape for k in _SMALL]
    ds, ms, vs = _adamw(_pack([w_small[k] for k in _SMALL]), _pack([gsmall[k] for k in _SMALL]),
                        _pack([m_small[k] for k in _SMALL]), _pack([v_small[k] for k in _SMALL]), name="adamw_small")
    delta = dict(zip(_SMALL, _unpack(ds, sshapes)))
    new_m = dict(zip(_SMALL, _unpack(ms, sshapes)))
    new_v = dict(zip(_SMALL, _unpack(vs, sshapes)))
    grad = dict(gsmall)
    for nm, w, g, m, v in (("w_in", w_in, grad_w_in, m_w_in, v_w_in),
                           ("w_proj", w_proj, grad_w_proj, m_w_proj, v_w_proj),
                           ("w_out", w_out, grad_w_out, m_w_out, v_w_out)):
        d2, m2, v2 = _adamw(_rows2d(w), _rows2d(g), _rows2d(m), _rows2d(v), name=f"adamw_{nm}")
        grad[nm] = g
        delta[nm], new_m[nm], new_v[nm] = d2.reshape(w.shape), m2.reshape(w.shape), v2.reshape(w.shape)

    order = ("norm_w", "w_in", "conv_w", "conv_b", "dt_bias", "a_log", "d_skip", "ssm_norm_w", "sinks", "f_bias",
             "gate_bias", "w_proj", "w_out", "final_norm_w")
    return (loss, grad_x, *[grad[k] for k in order], *[delta[k] for k in order],
            *[new_m[k] for k in order], *[new_v[k] for k in order])
```

```python
import functools
import math

import numpy as np
import jax
import jax.numpy as jnp
from jax import lax
from jax.experimental import pallas as pl
from jax.experimental.pallas import tpu as pltpu

F32 = jnp.float32
BF16 = jnp.bfloat16
HIGHEST = lax.Precision.HIGHEST
MESH = pl.DeviceIdType.MESH

D_MODEL = 1024
HEAD_DIM = 64
N_HEADS = 16
N_GROUPS = 4
SSM_STATE = 128
CHUNK = 128
CONV_WIDTH = 4
CONV_DIM = 2048
ROPE_THETA = 10000.0
NORM_EPS = 1e-6
LANES = 128
N_CHIPS = 4
N_DEV = 8

ADAM_LR = 0.001
ADAM_B1 = 0.9
ADAM_B2 = 0.999
ADAM_EPS = 1e-08
ADAM_WD = 0.01
ADAM_STEP = 10

_REF_COLS = {}
_off = 0
for _n, _s in (("xbc", 2048), ("a_z", 1024), ("a_dt", 16), ("b_q", 1024), ("b_k", 256), ("b_v", 256),
               ("b_z", 1024), ("c_q", 1024), ("c_k", 1024), ("c_v", 1024), ("c_f", 16), ("c_z", 1024),
               ("gates", 3072)):
    _REF_COLS[_n] = (_off, _s)
    _off += _s
N_IN = _off

_PAD_ORDER = (("gates", 3072), ("xbc", 2048), ("a_z", 1024), ("b_q", 1024), ("b_z", 1024), ("c_q", 1024),
              ("c_k", 1024), ("c_v", 1024), ("c_z", 1024), ("b_k", 256), ("b_v", 256), ("a_dt", 512),
              ("c_f", 128))
_PAD_COLS = {}
_off = 0
for _n, _s in _PAD_ORDER:
    _PAD_COLS[_n] = (_off, _s)
    _off += _s
N_USED = _off
N_PAD = 13824


def _cp(sem, vmem_mb=48):
    return pltpu.CompilerParams(dimension_semantics=sem, vmem_limit_bytes=vmem_mb * 1024 * 1024)


def _dot(a, b, dims=((1,), (0,)), precision=None):
    return lax.dot_general(a, b, (dims, ((), ())), preferred_element_type=F32, precision=precision)


def _dot_nt(a, b):
    return _dot(a, b, ((1,), (1,)))


def _dot_tn(a, b):
    return _dot(a, b, ((0,), (0,)))


def _col(v, idx):
    lane = lax.broadcasted_iota(jnp.int32, v.shape, 1)
    return jnp.sum(jnp.where(lane == idx, v, 0.0), axis=1, keepdims=True)


def _row(v, idx):
    row = lax.broadcasted_iota(jnp.int32, v.shape, 0)
    return jnp.sum(jnp.where(row == idx, v, 0.0), axis=0, keepdims=True)


def _iota_col():
    return lax.broadcasted_iota(jnp.int32, (CHUNK, 1), 0)


def _iota_row():
    return lax.broadcasted_iota(jnp.int32, (1, LANES), 1)


def _sigmoid(x):
    return 1.0 / (1.0 + jnp.exp(-x))


def _softplus(x):
    return jnp.maximum(x, 0.0) + jnp.log(1.0 + jnp.exp(-jnp.abs(x)))


def _pad_w_in(w):
    parts = []
    for name, size in _PAD_ORDER:
        s0, sz = _REF_COLS[name]
        seg = w[:, s0:s0 + sz]
        if name == "a_dt":
            seg = jnp.pad(seg.reshape(-1, N_GROUPS, 4), ((0, 0), (0, 0), (0, LANES - 4))).reshape(-1, 512)
        elif name == "c_f":
            seg = jnp.pad(seg, ((0, 0), (0, LANES - 16)))
        parts.append(seg)
    parts.append(jnp.zeros((w.shape[0], N_PAD - N_USED), w.dtype))
    return jnp.concatenate(parts, axis=1)


def _unpad_w_in(wp):
    segs = {}
    for name, _ in _PAD_ORDER:
        p0, psz = _PAD_COLS[name]
        seg = wp[:, p0:p0 + psz]
        if name == "a_dt":
            seg = seg.reshape(-1, N_GROUPS, LANES)[:, :, :4].reshape(-1, 16)
        elif name == "c_f":
            seg = seg[:, :16]
        segs[name] = seg
    order = sorted(_REF_COLS, key=lambda n: _REF_COLS[n][0])
    return jnp.concatenate([segs[n] for n in order], axis=1)


def _group_lanes(v):
    return jnp.pad(v.reshape(N_GROUPS, 1, 4), ((0, 0), (0, 0), (0, LANES - 4)))


def _ungroup_lanes(v):
    return v[:, 0, :4].reshape(16)


def _mm(a, b, *, ta=False, tb=False, tm=512, tn=512, tk=512, out_dtype=F32, add=None, name):
    if ta:
        kdim, m = a.shape
    else:
        m, kdim = a.shape
    if tb:
        n, k2 = b.shape
    else:
        k2, n = b.shape
    assert kdim == k2, (a.shape, b.shape)
    tm, tn, tk = min(tm, m), min(tn, n), min(tk, kdim)
    assert m % tm == 0 and n % tn == 0 and kdim % tk == 0, (m, n, kdim, tm, tn, tk)
    nk = kdim // tk
    a_spec = (pl.BlockSpec((tk, tm), lambda i, j, k: (k, i)) if ta
              else pl.BlockSpec((tm, tk), lambda i, j, k: (i, k)))
    b_spec = (pl.BlockSpec((tn, tk), lambda i, j, k: (j, k)) if tb
              else pl.BlockSpec((tk, tn), lambda i, j, k: (k, j)))
    dims = ((0 if ta else 1,), (1 if tb else 0,))
    has_add = add is not None

    def body(*refs):
        if has_add:
            a_ref, b_ref, add_ref, o_ref, acc_ref = refs
        else:
            a_ref, b_ref, o_ref, acc_ref = refs
        k = pl.program_id(2)
        p = _dot(a_ref[...].astype(BF16), b_ref[...].astype(BF16), dims)

        @pl.when(k == 0)
        def _():
            acc_ref[...] = p

        @pl.when(k > 0)
        def _():
            acc_ref[...] += p

        @pl.when(k == nk - 1)
        def _():
            r = acc_ref[...]
            if has_add:
                r = r + add_ref[...]
            o_ref[...] = r.astype(out_dtype)

    in_specs = [a_spec, b_spec]
    args = [a, b]
    if has_add:
        in_specs.append(pl.BlockSpec((tm, tn), lambda i, j, k: (i, j)))
        args.append(add)
    return pl.pallas_call(
        body, name=name, grid=(m // tm, n // tn, nk),
        in_specs=in_specs, out_specs=pl.BlockSpec((tm, tn), lambda i, j, k: (i, j)),
        out_shape=jax.ShapeDtypeStruct((m, n), out_dtype),
        scratch_shapes=[pltpu.VMEM((tm, tn), F32)],
        compiler_params=_cp(("parallel", "parallel", "arbitrary")),
    )(*args)


def _rms_fwd(x, w, *, name, tm=512):
    t, d = x.shape

    def body(x_ref, w_ref, o_ref, ot_ref):
        xv = x_ref[...]
        r = lax.rsqrt(jnp.mean(xv * xv, axis=1, keepdims=True) + NORM_EPS)
        h = xv * r * w_ref[...]
        o_ref[...] = h.astype(BF16)
        ot_ref[...] = h.T.astype(BF16)

    return pl.pallas_call(
        body, name=name, grid=(t // tm,),
        in_specs=[pl.BlockSpec((tm, d), lambda i: (i, 0)), pl.BlockSpec((1, d), lambda i: (0, 0))],
        out_specs=[pl.BlockSpec((tm, d), lambda i: (i, 0)), pl.BlockSpec((d, tm), lambda i: (0, i))],
        out_shape=[jax.ShapeDtypeStruct((t, d), BF16), jax.ShapeDtypeStruct((d, t), BF16)],
        compiler_params=_cp(("parallel",)),
    )(x, w.reshape(1, d))


def _rms_bwd(x, w, dh, dres, *, name, tm=512):
    t, d = x.shape

    def body(x_ref, w_ref, dh_ref, dres_ref, dx_ref, dw_ref):
        xv = x_ref[...]
        r = lax.rsqrt(jnp.mean(xv * xv, axis=1, keepdims=True) + NORM_EPS)
        xhat = xv * r
        dhv = dh_ref[...]
        dxhat = dhv * w_ref[...]
        dx = r * (dxhat - xhat * jnp.mean(dxhat * xhat, axis=1, keepdims=True))
        dx_ref[...] = dres_ref[...] + dx

        @pl.when(pl.program_id(0) == 0)
        def _():
            dw_ref[...] = jnp.zeros_like(dw_ref)

        dw_ref[...] += jnp.sum(dhv * xhat, axis=0, keepdims=True)

    return pl.pallas_call(
        body, name=name, grid=(t // tm,),
        in_specs=[pl.BlockSpec((tm, d), lambda i: (i, 0)), pl.BlockSpec((1, d), lambda i: (0, 0)),
                  pl.BlockSpec((tm, d), lambda i: (i, 0)), pl.BlockSpec((tm, d), lambda i: (i, 0))],
        out_specs=[pl.BlockSpec((tm, d), lambda i: (i, 0)), pl.BlockSpec((1, d), lambda i: (0, 0))],
        out_shape=[jax.ShapeDtypeStruct((t, d), F32), jax.ShapeDtypeStruct((1, d), F32)],
        compiler_params=_cp(("arbitrary",)),
    )(x, w.reshape(1, d), dh, dres)


def _final_loss(x, w, target, *, name, tm=512):
    t, d = x.shape

    def body(x_ref, w_ref, t_ref, loss_ref, dx_ref, dw_ref):
        xv = x_ref[...]
        wv = w_ref[...]
        r = lax.rsqrt(jnp.mean(xv * xv, axis=1, keepdims=True) + NORM_EPS)
        xhat = xv * r
        err = xhat * wv - t_ref[...]
        dy = err * (1.0 / d)
        dxhat = dy * wv
        dx_ref[...] = r * (dxhat - xhat * jnp.mean(dxhat * xhat, axis=1, keepdims=True))

        @pl.when(pl.program_id(0) == 0)
        def _():
            dw_ref[...] = jnp.zeros_like(dw_ref)
            loss_ref[...] = jnp.zeros_like(loss_ref)

        dw_ref[...] += jnp.sum(dy * xhat, axis=0, keepdims=True)
        part = 0.5 * jnp.sum(jnp.mean(err * err, axis=1, keepdims=True), axis=0, keepdims=True)
        loss_ref[...] += jnp.broadcast_to(part, loss_ref.shape)

    return pl.pallas_call(
        body, name=name, grid=(t // tm,),
        in_specs=[pl.BlockSpec((tm, d), lambda i: (i, 0)), pl.BlockSpec((1, d), lambda i: (0, 0)),
                  pl.BlockSpec((tm, d), lambda i: (i, 0))],
        out_specs=[pl.BlockSpec((8, LANES), lambda i: (0, 0)), pl.BlockSpec((tm, d), lambda i: (i, 0)),
                   pl.BlockSpec((1, d), lambda i: (0, 0))],
        out_shape=[jax.ShapeDtypeStruct((8, LANES), F32), jax.ShapeDtypeStruct((t, d), F32),
                   jax.ShapeDtypeStruct((1, d), F32)],
        compiler_params=_cp(("arbitrary",)),
    )(x, w.reshape(1, d), target)


_CB = 128


def _conv_pre(u, w_ref, b_ref):
    s = u.shape[0]
    row = lax.broadcasted_iota(jnp.int32, u.shape, 0)
    pre = b_ref[...] + w_ref[CONV_WIDTH - 1:CONV_WIDTH, :] * u
    for sh in range(1, CONV_WIDTH):
        shifted = jnp.where(row >= sh, pltpu.roll(u, sh, 0), 0.0)
        pre = pre + w_ref[CONV_WIDTH - 1 - sh:CONV_WIDTH - sh, :] * shifted
    return pre


def _conv_fwd(proj3, cw, cb, *, name):
    b, s, _ = proj3.shape
    c0 = _PAD_COLS["xbc"][0] // _CB

    def body(u_ref, w_ref, b_ref, o_ref):
        pre = _conv_pre(u_ref[...], w_ref, b_ref)
        o_ref[...] = pre * _sigmoid(pre)

    return pl.pallas_call(
        body, name=name, grid=(b, CONV_DIM // _CB),
        in_specs=[pl.BlockSpec((None, s, _CB), lambda i, j: (i, 0, c0 + j)),
                  pl.BlockSpec((CONV_WIDTH, _CB), lambda i, j: (0, j)),
                  pl.BlockSpec((1, _CB), lambda i, j: (0, j))],
        out_specs=pl.BlockSpec((None, s, _CB), lambda i, j: (i, 0, j)),
        out_shape=jax.ShapeDtypeStruct((b, s, CONV_DIM), F32),
        compiler_params=_cp(("parallel", "parallel")),
    )(proj3, cw, cb.reshape(1, CONV_DIM))


def _conv_bwd(proj3, cw, cb, dact, *, name):
    b, s, _ = proj3.shape
    c0 = _PAD_COLS["xbc"][0] // _CB

    def body(u_ref, w_ref, b_ref, da_ref, du_ref, dwb_ref):
        u = u_ref[...]
        pre = _conv_pre(u, w_ref, b_ref)
        sg = _sigmoid(pre)
        dpre = da_ref[...] * (sg * (1.0 + pre * (1.0 - sg)))
        row = lax.broadcasted_iota(jnp.int32, u.shape, 0)
        du = w_ref[CONV_WIDTH - 1:CONV_WIDTH, :] * dpre
        rows = [jnp.sum(dpre * u, axis=0, keepdims=True)]
        for sh in range(1, CONV_WIDTH):
            fwd_shift = jnp.where(row < s - sh, pltpu.roll(dpre, s - sh, 0), 0.0)
            du = du + w_ref[CONV_WIDTH - 1 - sh:CONV_WIDTH - sh, :] * fwd_shift
            ush = jnp.where(row >= sh, pltpu.roll(u, sh, 0), 0.0)
            rows.append(jnp.sum(dpre * ush, axis=0, keepdims=True))
        du_ref[...] = du.astype(BF16)

        @pl.when(pl.program_id(1) == 0)
        def _():
            dwb_ref[...] = jnp.zeros_like(dwb_ref)

        for sh in range(CONV_WIDTH):
            k = CONV_WIDTH - 1 - sh
            dwb_ref[k:k + 1, :] += rows[sh]
        dwb_ref[CONV_WIDTH:CONV_WIDTH + 1, :] += jnp.sum(dpre, axis=0, keepdims=True)

    return pl.pallas_call(
        body, name=name, grid=(CONV_DIM // _CB, b),
        in_specs=[pl.BlockSpec((None, s, _CB), lambda j, i: (i, 0, c0 + j)),
                  pl.BlockSpec((CONV_WIDTH, _CB), lambda j, i: (0, j)),
                  pl.BlockSpec((1, _CB), lambda j, i: (0, j)),
                  pl.BlockSpec((None, s, _CB), lambda j, i: (i, 0, j))],
        out_specs=[pl.BlockSpec((None, s, _CB), lambda j, i: (i, 0, j)),
                   pl.BlockSpec((8, _CB), lambda j, i: (0, j))],
        out_shape=[jax.ShapeDtypeStruct((b, s, CONV_DIM), BF16), jax.ShapeDtypeStruct((8, CONV_DIM), F32)],
        compiler_params=_cp(("parallel", "arbitrary")),
    )(proj3, cw, cb.reshape(1, CONV_DIM), dact)


def _ssd_common(dt_ref, dtb_ref, alog_ref):
    row = lax.broadcasted_iota(jnp.int32, (CHUNK, CHUNK), 0)
    lane = lax.broadcasted_iota(jnp.int32, (CHUNK, CHUNK), 1)
    causal = row >= lane
    tri = causal.astype(F32)
    dtv = _softplus(dt_ref[...] + dtb_ref[...])
    a_row = -jnp.exp(alog_ref[...])
    acum = _dot(tri, dtv * a_row, precision=HIGHEST)
    return row, lane, causal, dtv, a_row, acum, acum.T


def _ssd_pair(pp, x, dtv, acum, acum_t, causal, lane, row):
    lo = lane < HEAD_DIM
    r0, r1 = 2 * pp, 2 * pp + 1
    dtp = jnp.where(lo, _col(dtv, r0), _col(dtv, r1))
    ac0, ac1 = _col(acum, r0), _col(acum, r1)
    ar0, ar1 = _row(acum_t, r0), _row(acum_t, r1)
    d0 = jnp.where(causal, jnp.exp(jnp.where(causal, ac0 - ar0, 0.0)), 0.0)
    d1 = jnp.where(causal, jnp.exp(jnp.where(causal, ac1 - ar1, 0.0)), 0.0)
    al0, al1 = _col(ar0, CHUNK - 1), _col(ar1, CHUNK - 1)
    eac = jnp.where(lo, jnp.exp(ac0), jnp.exp(ac1))
    dsp = jnp.where(lo, jnp.exp(al0 - ac0), jnp.exp(al1 - ac1))
    eal = jnp.where(_iota_col() < HEAD_DIM, jnp.exp(al0), jnp.exp(al1))
    return lo, dtp, x * dtp, d0, d1, al0, al1, eac, dsp, eal


def _ssd_fwd(proj3, xact3, dtb, alog, dsk, nw, *, name):
    b, s, _ = proj3.shape
    nc = s // CHUNK
    dt0 = _PAD_COLS["a_dt"][0] // 512
    z0 = _PAD_COLS["a_z"][0] // D_MODEL

    def body(xs_ref, bm_ref, cm_ref, dt_ref, z_ref, dtb_ref, alog_ref, dsk_ref, nw_ref,
             ya_ref, ypre_ref, hst_ref, h_scr):
        @pl.when(pl.program_id(1) == 0)
        def _():
            h_scr[...] = jnp.zeros_like(h_scr)

        for g in range(N_GROUPS):
            w256 = pl.ds(256 * g, 256)
            w128 = pl.ds(LANES * g, LANES)
            group(xs_ref.at[:, w256], bm_ref.at[:, w128], cm_ref.at[:, w128], dt_ref.at[:, w128],
                  z_ref.at[:, w256], dtb_ref.at[g], alog_ref.at[g], dsk_ref.at[g], nw_ref.at[g],
                  ya_ref.at[:, w256], ypre_ref.at[:, w256], hst_ref.at[g], h_scr.at[g])

    def group(xs_ref, bm_ref, cm_ref, dt_ref, z_ref, dtb_ref, alog_ref, dsk_ref, nw_ref,
              ya_ref, ypre_ref, hst_ref, h_scr):
        row, lane, causal, dtv, a_row, acum, acum_t = _ssd_common(dt_ref, dtb_ref, alog_ref)
        bb = bm_ref[...].astype(BF16)
        cb = cm_ref[...].astype(BF16)
        cbm = _dot_nt(cb, bb)
        hst_ref[...] = h_scr[...]
        dskv = dsk_ref[...]
        for pp in range(2):
            x = xs_ref[:, LANES * pp:LANES * (pp + 1)]
            lo, dtp, xd, d0, d1, al0, al1, eac, dsp, eal = _ssd_pair(pp, x, dtv, acum, acum_t, causal, lane, row)
            xdb = xd.astype(BF16)
            y = jnp.where(lo, _dot((cbm * d0).astype(BF16), xdb), _dot((cbm * d1).astype(BF16), xdb))
            h = h_scr[pp]
            y = y + eac * _dot_nt(cb, h.astype(BF16))
            h_scr[pp] = h * eal + _dot_tn((xd * dsp).astype(BF16), bb)
            dskp = jnp.where((_iota_row() < HEAD_DIM), _col(dskv, 2 * pp), _col(dskv, 2 * pp + 1))
            ypre_ref[:, LANES * pp:LANES * (pp + 1)] = y + x * dskp
        ypre = ypre_ref[...]
        z = z_ref[...]
        yg = ypre * (z * _sigmoid(z))
        rstd = lax.rsqrt(jnp.sum(yg * yg, axis=1, keepdims=True) * (1.0 / 256.0) + NORM_EPS)
        ya_ref[...] = (yg * rstd * nw_ref[...]).astype(BF16)

    g = N_GROUPS
    par = pl.BlockSpec((g, 1, LANES), lambda i, c: (0, 0, 0))
    wide = pl.BlockSpec((None, CHUNK, D_MODEL), lambda i, c: (i, c, 0))
    return pl.pallas_call(
        body, name=name, grid=(b, nc),
        in_specs=[wide,
                  pl.BlockSpec((None, CHUNK, 512), lambda i, c: (i, c, 2)),
                  pl.BlockSpec((None, CHUNK, 512), lambda i, c: (i, c, 3)),
                  pl.BlockSpec((None, CHUNK, 512), lambda i, c: (i, c, dt0)),
                  pl.BlockSpec((None, CHUNK, D_MODEL), lambda i, c: (i, c, z0)),
                  par, par, par,
                  pl.BlockSpec((g, 1, 256), lambda i, c: (0, 0, 0))],
        out_specs=[wide, wide,
                   pl.BlockSpec((None, None, g, 2, CHUNK, SSM_STATE), lambda i, c: (i, c, 0, 0, 0, 0))],
        out_shape=[jax.ShapeDtypeStruct((b, s, D_MODEL), BF16), jax.ShapeDtypeStruct((b, s, D_MODEL), F32),
                   jax.ShapeDtypeStruct((b, nc, g, 2, CHUNK, SSM_STATE), F32)],
        scratch_shapes=[pltpu.VMEM((g, 2, CHUNK, SSM_STATE), F32)],
        compiler_params=_cp(("parallel", "arbitrary")),
    )(xact3, xact3, xact3, proj3, proj3, dtb, alog, dsk, nw)


def _ssd_bwd(proj3, xact3, dtb, alog, dsk, nw, ypre3, hst, dya3, *, name):
    b, s, _ = proj3.shape
    nc = s // CHUNK
    dt0 = _PAD_COLS["a_dt"][0] // 512
    z0 = _PAD_COLS["a_z"][0] // D_MODEL

    def body(xs_ref, bm_ref, cm_ref, dt_ref, z_ref, dtb_ref, alog_ref, dsk_ref, nw_ref, ypre_ref, hst_ref,
             dya_ref, dact_ref, dz_ref, ddt_ref, ddtb_ref, dalog_ref, ddsk_ref, dnw_ref, dh_scr):
        first = jnp.logical_and(pl.program_id(0) == 0, pl.program_id(1) == 0)

        @pl.when(first)
        def _():
            ddtb_ref[...] = jnp.zeros_like(ddtb_ref)
            dalog_ref[...] = jnp.zeros_like(dalog_ref)
            ddsk_ref[...] = jnp.zeros_like(ddsk_ref)
            dnw_ref[...] = jnp.zeros_like(dnw_ref)

        @pl.when(pl.program_id(1) == 0)
        def _():
            dh_scr[...] = jnp.zeros_like(dh_scr)

        for g in range(N_GROUPS):
            w256 = pl.ds(256 * g, 256)
            w128 = pl.ds(LANES * g, LANES)
            group(xs_ref.at[:, w256], bm_ref.at[:, w128], cm_ref.at[:, w128], dt_ref.at[:, w128],
                  z_ref.at[:, w256], dtb_ref.at[g], alog_ref.at[g], dsk_ref.at[g], nw_ref.at[g],
                  ypre_ref.at[:, w256], hst_ref.at[g], dya_ref.at[:, w256],
                  dact_ref.at[:, w256], dact_ref.at[:, pl.ds(D_MODEL + LANES * g, LANES)],
                  dact_ref.at[:, pl.ds(D_MODEL + 512 + LANES * g, LANES)], dz_ref.at[:, w256], ddt_ref.at[:, w128],
                  ddtb_ref.at[g], dalog_ref.at[g], ddsk_ref.at[g], dnw_ref.at[g], dh_scr.at[g])

    def group(xs_ref, bm_ref, cm_ref, dt_ref, z_ref, dtb_ref, alog_ref, dsk_ref, nw_ref, ypre_ref, hst_ref,
              dya_ref, dxs_ref, dbm_ref, dcm_ref, dz_ref, ddt_ref, ddtb_ref, dalog_ref, ddsk_ref, dnw_ref,
              dh_scr):
        row, lane, causal, dtv, a_row, acum, acum_t = _ssd_common(dt_ref, dtb_ref, alog_ref)
        lane1 = _iota_row()
        bb = bm_ref[...].astype(BF16)
        cb = cm_ref[...].astype(BF16)
        cbm = _dot_nt(cb, bb)

        z = z_ref[...]
        ypre = ypre_ref[...]
        dya = dya_ref[...]
        sz = _sigmoid(z)
        silu = z * sz
        yg = ypre * silu
        rstd = lax.rsqrt(jnp.sum(yg * yg, axis=1, keepdims=True) * (1.0 / 256.0) + NORM_EPS)
        dnw_ref[...] += jnp.sum(dya * yg * rstd, axis=0, keepdims=True)
        dn = dya * nw_ref[...]
        dyg = rstd * dn - yg * (rstd * rstd * rstd * (1.0 / 256.0)) * jnp.sum(dn * yg, axis=1, keepdims=True)
        dz_ref[...] = (dyg * ypre * (sz * (1.0 + z * (1.0 - sz)))).astype(BF16)
        dy_all = dyg * silu

        dskv = dsk_ref[...]
        da_cols = jnp.zeros((CHUNK, LANES), F32)
        dxt_cols = jnp.zeros((CHUNK, LANES), F32)
        ddsk_row = jnp.zeros((1, LANES), F32)
        dcb = jnp.zeros((CHUNK, CHUNK), F32)
        dc = jnp.zeros((CHUNK, SSM_STATE), F32)
        db = jnp.zeros((CHUNK, SSM_STATE), F32)
        last = _iota_col() == CHUNK - 1
        for pp in range(2):
            r0, r1 = 2 * pp, 2 * pp + 1
            x = xs_ref[:, LANES * pp:LANES * (pp + 1)]
            dy = dy_all[:, LANES * pp:LANES * (pp + 1)]
            lo, dtp, xd, d0, d1, al0, al1, eac, dsp, eal = _ssd_pair(pp, x, dtv, acum, acum_t, causal, lane, row)
            w0, w1 = cbm * d0, cbm * d1
            w0b, w1b = w0.astype(BF16), w1.astype(BF16)
            xdb = xd.astype(BF16)
            dyb = dy.astype(BF16)
            h = hst_ref[pp]
            dhn = dh_scr[pp]
            hb = h.astype(BF16)
            dhb = dhn.astype(BF16)
            g0 = _dot_nt(jnp.where(lo, dy, 0.0).astype(BF16), xdb)
            g1 = _dot_nt(jnp.where(lo, 0.0, dy).astype(BF16), xdb)
            dcb = dcb + g0 * d0 + g1 * d1
            m0, m1 = g0 * w0, g1 * w1
            bdh = _dot_nt(bb, dhb)
            dxd = jnp.where(lo, _dot_tn(w0b, dyb), _dot_tn(w1b, dyb)) + dsp * bdh
            ch = _dot_nt(cb, hb)
            edy = eac * dy
            edyb = edy.astype(BF16)
            xds = xd * dsp
            dc = dc + _dot(edyb, hb)
            db = db + _dot(xds.astype(BF16), dhb)
            dh_scr[pp] = dhn * eal + _dot_tn(edyb, cb)
            t2 = edy * ch
            t3 = xds * bdh
            r4 = jnp.sum(dhn * h, axis=1, keepdims=True)
            s4_0 = jnp.sum(jnp.where(_iota_col() < HEAD_DIM, r4, 0.0), axis=0, keepdims=True)
            s4_1 = jnp.sum(r4, axis=0, keepdims=True) - s4_0
            t2_0 = jnp.sum(jnp.where(lo, t2, 0.0), axis=1, keepdims=True)
            t2_1 = jnp.sum(t2, axis=1, keepdims=True) - t2_0
            t3_0 = jnp.sum(jnp.where(lo, t3, 0.0), axis=1, keepdims=True)
            t3_1 = jnp.sum(t3, axis=1, keepdims=True) - t3_0
            dal0 = jnp.sum(t3_0, axis=0, keepdims=True) + jnp.exp(al0) * s4_0
            dal1 = jnp.sum(t3_1, axis=0, keepdims=True) + jnp.exp(al1) * s4_1
            dac0 = (jnp.sum(m0, axis=1, keepdims=True) - jnp.sum(m0.T, axis=1, keepdims=True)
                    + t2_0 - t3_0 + jnp.where(last, dal0, 0.0))
            dac1 = (jnp.sum(m1, axis=1, keepdims=True) - jnp.sum(m1.T, axis=1, keepdims=True)
                    + t2_1 - t3_1 + jnp.where(last, dal1, 0.0))
            da_cols = da_cols + jnp.where(lane == r0, dac0, 0.0) + jnp.where(lane == r1, dac1, 0.0)
            xx = dxd * x
            x0 = jnp.sum(jnp.where(lo, xx, 0.0), axis=1, keepdims=True)
            x1 = jnp.sum(xx, axis=1, keepdims=True) - x0
            dxt_cols = dxt_cols + jnp.where(lane == r0, x0, 0.0) + jnp.where(lane == r1, x1, 0.0)
            dskp = jnp.where((_iota_row() < HEAD_DIM), _col(dskv, r0), _col(dskv, r1))
            dxs_ref[:, LANES * pp:LANES * (pp + 1)] = dxd * dtp + dy * dskp
            yx = jnp.sum(dy * x, axis=0, keepdims=True)
            k0 = jnp.sum(jnp.where((_iota_row() < HEAD_DIM), yx, 0.0), axis=1, keepdims=True)
            k1 = jnp.sum(yx, axis=1, keepdims=True) - k0
            ddsk_row = ddsk_row + jnp.where(lane1 == r0, k0, 0.0) + jnp.where(lane1 == r1, k1, 0.0)
        dcbb = dcb.astype(BF16)
        dcm_ref[...] = dc + _dot(dcbb, bb)
        dbm_ref[...] = db + _dot_tn(dcbb, cb)
        tri_t = (row <= lane).astype(F32)
        dadt = _dot(tri_t, da_cols, precision=HIGHEST)
        ddtv = dadt * a_row + dxt_cols
        dalog_ref[...] += jnp.sum(dadt * dtv, axis=0, keepdims=True) * a_row
        ddt_raw = ddtv * _sigmoid(dt_ref[...] + dtb_ref[...])
        ddt_ref[...] = ddt_raw.astype(BF16)
        ddtb_ref[...] += jnp.sum(ddt_raw, axis=0, keepdims=True)
        ddsk_ref[...] += ddsk_row

    g = N_GROUPS
    rc = lambda c: nc - 1 - c
    par = pl.BlockSpec((g, 1, LANES), lambda i, c: (0, 0, 0))
    parw = pl.BlockSpec((g, 1, 256), lambda i, c: (0, 0, 0))
    wide = pl.BlockSpec((None, CHUNK, D_MODEL), lambda i, c: (i, rc(c), 0))
    blk512 = lambda col: pl.BlockSpec((None, CHUNK, 512), lambda i, c: (i, rc(c), col))
    return pl.pallas_call(
        body, name=name, grid=(b, nc),
        in_specs=[wide, blk512(2), blk512(3), blk512(dt0),
                  pl.BlockSpec((None, CHUNK, D_MODEL), lambda i, c: (i, rc(c), z0)),
                  par, par, par, parw,
                  wide,
                  pl.BlockSpec((None, None, g, 2, CHUNK, SSM_STATE), lambda i, c: (i, rc(c), 0, 0, 0, 0)),
                  wide],
        out_specs=[pl.BlockSpec((None, CHUNK, CONV_DIM), lambda i, c: (i, rc(c), 0)), wide, blk512(0),
                   par, par, par, parw],
        out_shape=[jax.ShapeDtypeStruct((b, s, CONV_DIM), F32), jax.ShapeDtypeStruct((b, s, D_MODEL), BF16),
                   jax.ShapeDtypeStruct((b, s, 512), BF16),
                   jax.ShapeDtypeStruct((g, 1, LANES), F32), jax.ShapeDtypeStruct((g, 1, LANES), F32),
                   jax.ShapeDtypeStruct((g, 1, LANES), F32), jax.ShapeDtypeStruct((g, 1, 256), F32)],
        scratch_shapes=[pltpu.VMEM((g, 2, CHUNK, SSM_STATE), F32)],
        compiler_params=_cp(("arbitrary", "arbitrary")),
    )(xact3, xact3, xact3, proj3, proj3, dtb, alog, dsk, nw, ypre3, hst, dya3)


def _fgate_fwd(proj3, fb, *, name):
    b, s, _ = proj3.shape
    f0 = _PAD_COLS["c_f"][0] // LANES

    def body(f_ref, fb_ref, cum_ref, carry):
        @pl.when(pl.program_id(1) == 0)
        def _():
            carry[...] = jnp.zeros_like(carry)

        row = lax.broadcasted_iota(jnp.int32, (CHUNK, CHUNK), 0)
        lane = lax.broadcasted_iota(jnp.int32, (CHUNK, CHUNK), 1)
        tri = (row >= lane).astype(F32)
        lf = -_softplus(-(f_ref[...] + fb_ref[...]))
        cs = _dot(tri, lf, precision=HIGHEST) + carry[0:1, :]
        cum_ref[...] = cs
        carry[0:1, :] = _row(cs, CHUNK - 1)

    return pl.pallas_call(
        body, name=name, grid=(b, s // CHUNK),
        in_specs=[pl.BlockSpec((None, CHUNK, LANES), lambda i, c: (i, c, f0)),
                  pl.BlockSpec((1, LANES), lambda i, c: (0, 0))],
        out_specs=pl.BlockSpec((None, CHUNK, LANES), lambda i, c: (i, c, 0)),
        out_shape=jax.ShapeDtypeStruct((b, s, LANES), F32),
        scratch_shapes=[pltpu.VMEM((8, LANES), F32)],
        compiler_params=_cp(("parallel", "arbitrary")),
    )(proj3, fb)


def _fgate_bwd(proj3, fb, dcum, *, name):
    b, s, _ = proj3.shape
    nc = s // CHUNK
    f0 = _PAD_COLS["c_f"][0] // LANES

    def body(f_ref, fb_ref, dc_ref, df_ref, dfb_ref, carry):
        first = jnp.logical_and(pl.program_id(0) == 0, pl.program_id(1) == 0)

        @pl.when(first)
        def _():
            dfb_ref[...] = jnp.zeros_like(dfb_ref)

        @pl.when(pl.program_id(1) == 0)
        def _():
            carry[...] = jnp.zeros_like(carry)

        row = lax.broadcasted_iota(jnp.int32, (CHUNK, CHUNK), 0)
        lane = lax.broadcasted_iota(jnp.int32, (CHUNK, CHUNK), 1)
        tri_t = (row <= lane).astype(F32)
        dlf = _dot(tri_t, dc_ref[...], precision=HIGHEST) + carry[0:1, :]
        carry[0:1, :] = _row(dlf, 0)
        df = dlf * _sigmoid(-(f_ref[...] + fb_ref[...]))
        df_ref[...] = df.astype(BF16)
        dfb_ref[...] += jnp.sum(df, axis=0, keepdims=True)

    return pl.pallas_call(
        body, name=name, grid=(b, nc),
        in_specs=[pl.BlockSpec((None, CHUNK, LANES), lambda i, c: (i, nc - 1 - c, f0)),
                  pl.BlockSpec((1, LANES), lambda i, c: (0, 0)),
                  pl.BlockSpec((None, CHUNK, LANES), lambda i, c: (i, nc - 1 - c, 0))],
        out_specs=[pl.BlockSpec((None, CHUNK, LANES), lambda i, c: (i, nc - 1 - c, 0)),
                   pl.BlockSpec((1, LANES), lambda i, c: (0, 0))],
        out_shape=[jax.ShapeDtypeStruct((b, s, LANES), BF16), jax.ShapeDtypeStruct((1, LANES), F32)],
        scratch_shapes=[pltpu.VMEM((8, LANES), F32)],
        compiler_params=_cp(("arbitrary", "arbitrary")),
    )(proj3, fb, dcum)


_SCALE = HEAD_DIM ** -0.5
_NEG = -1e30


def _fox_fwd(proj3, cum_t, *, name, tb):
    b, s, _ = proj3.shape
    nq = s // tb
    q0 = _PAD_COLS["c_q"][0] // LANES
    k0 = _PAD_COLS["c_k"][0] // LANES
    v0 = _PAD_COLS["c_v"][0] // LANES
    z0 = _PAD_COLS["c_z"][0] // LANES

    def body(q_ref, k_ref, v_ref, z_ref, cumt_ref, y_ref, o_ref, lse_ref):
        i = pl.program_id(2)
        lo = lax.broadcasted_iota(jnp.int32, (tb, LANES), 1) < HEAD_DIM
        q = q_ref[...] * _SCALE
        qms = (jnp.where(lo, q, 0.0).astype(BF16), jnp.where(lo, 0.0, q).astype(BF16))

        def block(j, carry, diagonal):
            ks = pl.ds(pl.multiple_of(j * tb, tb), tb)
            kb = k_ref[ks, :].astype(BF16)
            vb = v_ref[ks, :].astype(BF16)
            ckv = cumt_ref[j]
            if diagonal:
                row = lax.broadcasted_iota(jnp.int32, (tb, tb), 0)
                col = lax.broadcasted_iota(jnp.int32, (tb, tb), 1)
                mask = row >= col
            ms, ls, acc = carry
            new_m, new_l, pvs, alphas = [], [], [], []
            for hh in range(2):
                sc = _dot_nt(qms[hh], kb) - ckv[hh:hh + 1, :]
                if diagonal:
                    sc = jnp.where(mask, sc, _NEG)
                m_new = jnp.maximum(ms[hh], jnp.max(sc, axis=1, keepdims=True))
                alpha = jnp.exp(ms[hh] - m_new)
                pr = jnp.exp(sc - m_new)
                new_l.append(alpha * ls[hh] + jnp.sum(pr, axis=1, keepdims=True))
                new_m.append(m_new)
                pr_hi = pr.astype(BF16)
                pr_lo = (pr - pr_hi.astype(F32)).astype(BF16)
                pvs.append(_dot(pr_hi, vb) + _dot(pr_lo, vb))
                alphas.append(alpha)
            acc = jnp.where(lo, alphas[0] * acc + pvs[0], alphas[1] * acc + pvs[1])
            return (tuple(new_m), tuple(new_l), acc)

        neg = jnp.full((tb, 1), _NEG, F32)
        zero = jnp.zeros((tb, 1), F32)
        init = ((neg, neg), (zero, zero), jnp.zeros((tb, LANES), F32))
        carry = lax.fori_loop(0, i, lambda j, c: block(j, c, False), init)
        ms, ls, acc = block(i, carry, True)
        o = acc / jnp.where(lo, ls[0], ls[1])
        o_ref[...] = o
        lse_ref[...] = jnp.where(lo, ms[0] + jnp.log(ls[0]), ms[1] + jnp.log(ls[1]))
        z = z_ref[...]
        y_ref[...] = (o * (z * _sigmoid(z))).astype(BF16)

    qspec = lambda c0: pl.BlockSpec((None, tb, LANES), lambda bi, p, i: (bi, i, c0 + p))
    kspec = lambda c0: pl.BlockSpec((None, s, LANES), lambda bi, p, i: (bi, 0, c0 + p))
    ospec = pl.BlockSpec((None, tb, LANES), lambda bi, p, i: (bi, i, p))
    return pl.pallas_call(
        body, name=name, grid=(b, N_HEADS // 2, nq),
        in_specs=[qspec(q0), kspec(k0), kspec(v0), qspec(z0),
                  pl.BlockSpec((None, None, nq, 8, tb), lambda bi, p, i: (bi, p, 0, 0, 0))],
        out_specs=[ospec, ospec, ospec],
        out_shape=[jax.ShapeDtypeStruct((b, s, D_MODEL), BF16)] + [jax.ShapeDtypeStruct((b, s, D_MODEL), F32)] * 2,
        compiler_params=_cp(("parallel", "parallel", "arbitrary")),
    )(proj3, proj3, proj3, proj3, cum_t)


_ST_LSE, _ST_DELTA = 0, 2


def _fox_prep(proj3, o3, lse3, dy3, *, name, tr=512):
    b, s, _ = proj3.shape
    z0 = _PAD_COLS["c_z"][0] // LANES

    def body(z_ref, o_ref, lse_ref, dy_ref, dz_ref, do_ref, st_ref):
        lane = lax.broadcasted_iota(jnp.int32, (tr, LANES), 1)
        lo = lane < HEAD_DIM
        z = z_ref[...]
        sz = _sigmoid(z)
        dy = dy_ref[...]
        o = o_ref[...]
        do = dy * (z * sz)
        dz_ref[...] = (dy * o * (sz * (1.0 + z * (1.0 - sz)))).astype(BF16)
        do_ref[...] = do
        doo = do.astype(BF16).astype(F32) * o
        lse = lse_ref[...]
        cols = (_col(lse, 0), _col(lse, HEAD_DIM),
                jnp.sum(jnp.where(lo, doo, 0.0), axis=1, keepdims=True),
                jnp.sum(jnp.where(lo, 0.0, doo), axis=1, keepdims=True))
        st = jnp.zeros((tr, LANES), F32)
        for k, cvec in enumerate(cols):
            st = jnp.where(lane == k, cvec, st)
        st_ref[...] = st

    ospec = pl.BlockSpec((None, tr, LANES), lambda bi, p, i: (bi, i, p))
    return pl.pallas_call(
        body, name=name, grid=(b, N_HEADS // 2, s // tr),
        in_specs=[pl.BlockSpec((None, tr, LANES), lambda bi, p, i: (bi, i, z0 + p)), ospec, ospec, ospec],
        out_specs=[ospec, ospec, pl.BlockSpec((None, None, tr, LANES), lambda bi, p, i: (bi, p, i, 0))],
        out_shape=[jax.ShapeDtypeStruct((b, s, D_MODEL), BF16), jax.ShapeDtypeStruct((b, s, D_MODEL), F32),
                   jax.ShapeDtypeStruct((b, N_HEADS // 2, s, LANES), F32)],
        compiler_params=_cp(("parallel", "parallel", "parallel")),
    )(proj3, o3, lse3, dy3)


def _fox_bwd(proj3, cum_t, do3, stats, *, name, tb):
    b, s, _ = proj3.shape
    nq = s // tb
    q0 = _PAD_COLS["c_q"][0] // LANES
    k0 = _PAD_COLS["c_k"][0] // LANES
    v0 = _PAD_COLS["c_v"][0] // LANES

    def body(q_ref, do_ref, st_ref, k_ref, v_ref, cumt_ref, dq_ref, dk_ref, dv_ref, cs_ref):
        j = pl.program_id(2)
        lo = lax.broadcasted_iota(jnp.int32, (tb, LANES), 1) < HEAD_DIM

        @pl.when(j == 0)
        def _():
            dq_ref[...] = jnp.zeros_like(dq_ref)

        kb = k_ref[...].astype(BF16)
        vb = v_ref[...].astype(BF16)
        ckv = cumt_ref[...]

        def block(i, carry, diagonal):
            qs = pl.ds(pl.multiple_of(i * tb, tb), tb)
            q = q_ref[qs, :] * _SCALE
            do = do_ref[qs, :]
            st = st_ref[qs, :]
            if diagonal:
                row = lax.broadcasted_iota(jnp.int32, (tb, tb), 0)
                col = lax.broadcasted_iota(jnp.int32, (tb, tb), 1)
                mask = row >= col
            dk, dv, cs = carry
            new_cs, dqs = [], []
            for hh in range(2):
                sel = lo if hh == 0 else jnp.logical_not(lo)
                qm = jnp.where(sel, q, 0.0).astype(BF16)
                dom = jnp.where(sel, do, 0.0).astype(BF16)
                sc = _dot_nt(qm, kb) - ckv[hh:hh + 1, :]
                if diagonal:
                    sc = jnp.where(mask, sc, _NEG)
                pr = jnp.exp(sc - _col(st, _ST_LSE + hh))
                ds = pr * (_dot_nt(dom, vb) - _col(st, _ST_DELTA + hh))
                dsb = ds.astype(BF16)
                dv = dv + _dot_tn(pr.astype(BF16), dom)
                dk = dk + _dot_tn(dsb, qm)
                new_cs.append(cs[hh] + jnp.sum(ds, axis=0, keepdims=True))
                dqs.append(_dot(dsb, kb))
            dq_ref[qs, :] += jnp.where(lo, dqs[0], dqs[1]) * _SCALE
            return (dk, dv, tuple(new_cs))

        zrow = jnp.zeros((1, tb), F32)
        init = (jnp.zeros((tb, LANES), F32), jnp.zeros((tb, LANES), F32), (zrow, zrow))
        carry = block(j, init, True)
        dk, dv, cs = lax.fori_loop(j + 1, nq, lambda i, c: block(i, c, False), carry)
        dk_ref[...] = dk.astype(BF16)
        dv_ref[...] = dv.astype(BF16)
        cs_ref[...] = jnp.zeros_like(cs_ref)
        cs_ref[0:1, :] = cs[0]
        cs_ref[1:2, :] = cs[1]

    full = lambda c0: pl.BlockSpec((None, s, LANES), lambda bi, p, j: (bi, 0, c0 + p))
    kspec = lambda c0: pl.BlockSpec((None, tb, LANES), lambda bi, p, j: (bi, j, c0 + p))
    ko = pl.BlockSpec((None, tb, LANES), lambda bi, p, j: (bi, j, p))
    ctspec = pl.BlockSpec((None, None, None, 8, tb), lambda bi, p, j: (bi, p, j, 0, 0))
    return pl.pallas_call(
        body, name=name, grid=(b, N_HEADS // 2, nq),
        in_specs=[full(q0), full(0), pl.BlockSpec((None, None, s, LANES), lambda bi, p, j: (bi, p, 0, 0)),
                  kspec(k0), kspec(v0), ctspec],
        out_specs=[full(0), ko, ko, ctspec],
        out_shape=[jax.ShapeDtypeStruct((b, s, D_MODEL), F32), jax.ShapeDtypeStruct((b, s, D_MODEL), BF16),
                   jax.ShapeDtypeStruct((b, s, D_MODEL), BF16),
                   jax.ShapeDtypeStruct((b, N_HEADS // 2, nq, 8, tb), F32)],
        compiler_params=_cp(("parallel", "parallel", "arbitrary")),
    )(proj3, do3, stats, proj3, proj3, cum_t)


def _rope(x, cos, sin_signed):
    w = x.shape[1]
    lane = lax.broadcasted_iota(jnp.int32, x.shape, 1)
    first = (lane % HEAD_DIM) < (HEAD_DIM // 2)
    rot = jnp.where(first, pltpu.roll(x, w - HEAD_DIM // 2, 1), pltpu.roll(x, HEAD_DIM // 2, 1))
    return x * cos + rot * sin_signed


_QB = 4
_QROWS = _QB * CHUNK


def _swa_keys(i, kc_ref, kp_ref, vc_ref, vp_ref, cq_ref, sq_ref, cp_ref, sp_ref):
    cq, sq, cpv, spv = cq_ref[...], sq_ref[...], cp_ref[...], sp_ref[...]
    kc = _rope(kc_ref[...], cq, sq).astype(BF16)
    kp = _rope(kp_ref[...], cpv, spv).astype(BF16)
    row = lax.broadcasted_iota(jnp.int32, (CHUNK, CHUNK), 0)
    col = lax.broadcasted_iota(jnp.int32, (CHUNK, CHUNK), 1)
    return cq, sq, cpv, spv, kc, kp, vc_ref[...].astype(BF16), vp_ref[...].astype(BF16), col <= row, col > row


def _swa_specs(order, q0, z0):
    def spec(shape, fn):
        return pl.BlockSpec(shape, lambda *ids: fn(*order(*ids)))

    prev = lambda i: jnp.maximum(_QB * i - 1, 0)
    return dict(
        q=spec((None, _QROWS, 256), lambda bi, g, i: (bi, i, q0 + g)),
        z=spec((None, _QROWS, 256), lambda bi, g, i: (bi, i, z0 + g)),
        blk=spec((None, _QROWS, 256), lambda bi, g, i: (bi, i, g)),
        kcur=spec((None, _QROWS, LANES), lambda bi, g, i: (bi, i, g)),
        kprev=spec((None, CHUNK, LANES), lambda bi, g, i: (bi, prev(i), g)),
        kstep=spec((None, CHUNK, LANES), lambda bi, g, i: (bi, i, g)),
        tcur=spec((_QROWS, LANES), lambda bi, g, i: (i, 0)),
        tprev=spec((CHUNK, LANES), lambda bi, g, i: (prev(i), 0)),
        sk=spec((None, 1, LANES), lambda bi, g, i: (g, 0, 0)))


def _swa_fwd(proj3, k2, v2, cos, sin, sinks, *, name):
    b, s, _ = proj3.shape
    q0 = _PAD_COLS["b_q"][0] // 256
    z0 = _PAD_COLS["b_z"][0] // 256

    def body(q_ref, z_ref, kc_ref, kp_ref, vc_ref, vp_ref, cq_ref, sq_ref, cp_ref, sp_ref, sk_ref,
             y_ref, o_ref, lse_ref):
        i = pl.program_id(2)
        cq_all, sq_all, _, _, kc_all, kp0, vc_all, vp0, mask_c, band = _swa_keys(
            i, kc_ref, kp_ref, vc_ref, vp_ref, cq_ref, sq_ref, cp_ref, sp_ref)
        lo = lax.broadcasted_iota(jnp.int32, (CHUNK, LANES), 1) < HEAD_DIM
        skv = sk_ref[...]
        for u in range(_QB):
            rs = slice(CHUNK * u, CHUNK * (u + 1))
            ps = slice(CHUNK * (u - 1), CHUNK * u)
            kc, vc, cq, sq = kc_all[rs], vc_all[rs], cq_all[rs], sq_all[rs]
            kp, vp = (kp0, vp0) if u == 0 else (kc_all[ps], vc_all[ps])
            mask_p = jnp.logical_and(band, i > 0) if u == 0 else band
            for pp in range(2):
                ls = slice(LANES * pp, LANES * (pp + 1))
                q = _rope(q_ref[rs, ls], cq, sq) * _SCALE
                os_, lses = [], []
                for hh in range(2):
                    qm = jnp.where(lo if hh == 0 else jnp.logical_not(lo), q, 0.0).astype(BF16)
                    sc = jnp.where(mask_c, _dot_nt(qm, kc), _NEG)
                    sp_ = jnp.where(mask_p, _dot_nt(qm, kp), _NEG)
                    sink = _col(skv, 2 * pp + hh)
                    m = jnp.maximum(jnp.maximum(jnp.max(sc, axis=1, keepdims=True),
                                                jnp.max(sp_, axis=1, keepdims=True)), sink)
                    pc = jnp.exp(sc - m)
                    ppv = jnp.exp(sp_ - m)
                    l = (jnp.sum(pc, axis=1, keepdims=True) + jnp.sum(ppv, axis=1, keepdims=True)
                         + jnp.exp(sink - m))
                    os_.append((_dot(pc.astype(BF16), vc) + _dot(ppv.astype(BF16), vp)) / l)
                    lses.append(m + jnp.log(l))
                o = jnp.where(lo, os_[0], os_[1])
                z = z_ref[rs, ls]
                o_ref[rs, ls] = o
                lse_ref[rs, ls] = jnp.where(lo, lses[0], lses[1])
                y_ref[rs, ls] = (o * (z * _sigmoid(z))).astype(BF16)

    sp = _swa_specs(lambda bi, g, i: (bi, g, i), q0, z0)
    return pl.pallas_call(
        body, name=name, grid=(b, N_GROUPS, s // _QROWS),
        in_specs=[sp["q"], sp["z"], sp["kcur"], sp["kprev"], sp["kcur"], sp["kprev"],
                  sp["tcur"], sp["tcur"], sp["tprev"], sp["tprev"], sp["sk"]],
        out_specs=[sp["blk"], sp["blk"], sp["blk"]],
        out_shape=[jax.ShapeDtypeStruct((b, s, D_MODEL), BF16)] + [jax.ShapeDtypeStruct((b, s, D_MODEL), F32)] * 2,
        compiler_params=_cp(("parallel", "parallel", "parallel")),
    )(proj3, proj3, k2, k2, v2, v2, cos, sin, cos, sin, sinks)


def _swa_bwd(proj3, k2, v2, cos, sin, sinks, o3, lse3, dy3, *, name):
    b, s, _ = proj3.shape
    q0 = _PAD_COLS["b_q"][0] // 256
    z0 = _PAD_COLS["b_z"][0] // 256

    def body(q_ref, z_ref, kc_ref, kp_ref, vc_ref, vp_ref, cq_ref, sq_ref, cp_ref, sp_ref, sk_ref,
             o_ref, lse_ref, dy_ref, dq_ref, dz_ref, dkc_ref, dkp_ref, dvc_ref, dvp_ref, dsk_ref):
        i = pl.program_id(2)
        first = jnp.logical_and(pl.program_id(1) == 0, i == 0)

        @pl.when(first)
        def _():
            dsk_ref[...] = jnp.zeros_like(dsk_ref)

        cq_all, sq_all, cpv, spv, kc_all, kp0, vc_all, vp0, mask_c, band = _swa_keys(
            i, kc_ref, kp_ref, vc_ref, vp_ref, cq_ref, sq_ref, cp_ref, sp_ref)
        lo = lax.broadcasted_iota(jnp.int32, (CHUNK, LANES), 1) < HEAD_DIM
        lane1 = lax.broadcasted_iota(jnp.int32, (1, LANES), 1)
        skv = sk_ref[...]
        zero = jnp.zeros((CHUNK, LANES), F32)
        dks = [zero] * (_QB + 1)
        dvs = [zero] * (_QB + 1)
        dsk_row = jnp.zeros((1, LANES), F32)
        for u in range(_QB):
            rs = slice(CHUNK * u, CHUNK * (u + 1))
            ps = slice(CHUNK * (u - 1), CHUNK * u)
            kc, vc, cq, sq = kc_all[rs], vc_all[rs], cq_all[rs], sq_all[rs]
            kp, vp = (kp0, vp0) if u == 0 else (kc_all[ps], vc_all[ps])
            mask_p = jnp.logical_and(band, i > 0) if u == 0 else band
            for pp in range(2):
                ls = slice(LANES * pp, LANES * (pp + 1))
                q = _rope(q_ref[rs, ls], cq, sq) * _SCALE
                z = z_ref[rs, ls]
                sz = _sigmoid(z)
                dy = dy_ref[rs, ls]
                o = o_ref[rs, ls]
                lse = lse_ref[rs, ls]
                do = dy * (z * sz)
                dz_ref[rs, ls] = (dy * o * (sz * (1.0 + z * (1.0 - sz)))).astype(BF16)
                dqs = []
                for hh in range(2):
                    sel = lo if hh == 0 else jnp.logical_not(lo)
                    qm = jnp.where(sel, q, 0.0).astype(BF16)
                    dom = jnp.where(sel, do, 0.0).astype(BF16)
                    lse_h = _col(lse, hh * HEAD_DIM)
                    sink = _col(skv, 2 * pp + hh)
                    pc = jnp.exp(jnp.where(mask_c, _dot_nt(qm, kc), _NEG) - lse_h)
                    ppv = jnp.exp(jnp.where(mask_p, _dot_nt(qm, kp), _NEG) - lse_h)
                    dpc, dpp = _dot_nt(dom, vc), _dot_nt(dom, vp)
                    dl = jnp.sum(pc * dpc, axis=1, keepdims=True) + jnp.sum(ppv * dpp, axis=1, keepdims=True)
                    dsc = pc * (dpc - dl)
                    dsp = ppv * (dpp - dl)
                    dsink = -jnp.sum(jnp.exp(sink - lse_h) * dl, axis=0, keepdims=True)
                    dsk_row = dsk_row + jnp.where(lane1 == 2 * pp + hh, dsink, 0.0)
                    dscb, dspb = dsc.astype(BF16), dsp.astype(BF16)
                    dqs.append(_dot(dscb, kc) + _dot(dspb, kp))
                    dks[u + 1] = dks[u + 1] + _dot_tn(dscb, qm)
                    dks[u] = dks[u] + _dot_tn(dspb, qm)
                    dvs[u + 1] = dvs[u + 1] + _dot_tn(pc.astype(BF16), dom)
                    dvs[u] = dvs[u] + _dot_tn(ppv.astype(BF16), dom)
                dq_ref[rs, ls] = _rope(jnp.where(lo, dqs[0], dqs[1]) * _SCALE, cq, -sq).astype(BF16)
        fold = lambda v: v + pltpu.roll(v, HEAD_DIM, 1)
        dkp_ref[...] = fold(_rope(dks[0], cpv, -spv))
        dvp_ref[...] = fold(dvs[0])
        for u in range(_QB):
            rs = slice(CHUNK * u, CHUNK * (u + 1))
            dkc_ref[rs, :] = fold(_rope(dks[u + 1], cq_all[rs], -sq_all[rs]))
            dvc_ref[rs, :] = fold(dvs[u + 1])
        dsk_ref[...] += dsk_row

    sp = _swa_specs(lambda g, bi, i: (bi, g, i), q0, z0)
    kv_shape = jax.ShapeDtypeStruct((b, s, 512), F32)
    kvp_shape = jax.ShapeDtypeStruct((b, s // _QB, 512), F32)
    return pl.pallas_call(
        body, name=name, grid=(N_GROUPS, b, s // _QROWS),
        in_specs=[sp["q"], sp["z"], sp["kcur"], sp["kprev"], sp["kcur"], sp["kprev"],
                  sp["tcur"], sp["tcur"], sp["tprev"], sp["tprev"], sp["sk"], sp["blk"], sp["blk"], sp["blk"]],
        out_specs=[sp["blk"], sp["blk"], sp["kcur"], sp["kstep"], sp["kcur"], sp["kstep"], sp["sk"]],
        out_shape=[jax.ShapeDtypeStruct((b, s, D_MODEL), BF16), jax.ShapeDtypeStruct((b, s, D_MODEL), BF16),
                   kv_shape, kvp_shape, kv_shape, kvp_shape, jax.ShapeDtypeStruct((N_GROUPS, 1, LANES), F32)],
        compiler_params=_cp(("arbitrary", "arbitrary", "arbitrary")),
    )(proj3, proj3, k2, k2, v2, v2, cos, sin, cos, sin, sinks, o3, lse3, dy3)


def _merge_fwd(proj, br, gb, *, name, tm=256):
    t = proj.shape[0]
    g0 = _PAD_COLS["gates"][0] // D_MODEL

    def body(g_ref, a_ref, b_ref, c_ref, gb_ref, o_ref):
        acc = None
        for i, r in enumerate((a_ref, b_ref, c_ref)):
            gate = _sigmoid(g_ref[:, D_MODEL * i:D_MODEL * (i + 1)] + gb_ref[i:i + 1, :])
            term = gate * r[...]
            acc = term if acc is None else acc + term
        o_ref[...] = acc.astype(BF16)

    row = pl.BlockSpec((tm, D_MODEL), lambda i: (i, 0))
    return pl.pallas_call(
        body, name=name, grid=(t // tm,),
        in_specs=[pl.BlockSpec((tm, 3 * D_MODEL), lambda i: (i, g0)), row, row, row,
                  pl.BlockSpec((3, D_MODEL), lambda i: (0, 0))],
        out_specs=row, out_shape=jax.ShapeDtypeStruct((t, D_MODEL), BF16),
        compiler_params=_cp(("parallel",)),
    )(proj, br[0], br[1], br[2], gb)


def _merge_bwd(proj, br, gb, dm, *, name, tm=256):
    t = proj.shape[0]
    g0 = _PAD_COLS["gates"][0] // D_MODEL

    def body(g_ref, a_ref, b_ref, c_ref, gb_ref, dm_ref, da_ref, db_ref, dc_ref, dg_ref, dgb_ref):
        @pl.when(pl.program_id(0) == 0)
        def _():
            dgb_ref[...] = jnp.zeros_like(dgb_ref)

        dmv = dm_ref[...]
        for i, (r, dr) in enumerate(((a_ref, da_ref), (b_ref, db_ref), (c_ref, dc_ref))):
            gate = _sigmoid(g_ref[:, D_MODEL * i:D_MODEL * (i + 1)] + gb_ref[i:i + 1, :])
            dr[...] = (dmv * gate).astype(BF16)
            dg = dmv * r[...] * gate * (1.0 - gate)
            dg_ref[:, D_MODEL * i:D_MODEL * (i + 1)] = dg.astype(BF16)
            dgb_ref[i:i + 1, :] += jnp.sum(dg, axis=0, keepdims=True)

    row = pl.BlockSpec((tm, D_MODEL), lambda i: (i, 0))
    rowb = jax.ShapeDtypeStruct((t, D_MODEL), BF16)
    return pl.pallas_call(
        body, name=name, grid=(t // tm,),
        in_specs=[pl.BlockSpec((tm, 3 * D_MODEL), lambda i: (i, g0)), row, row, row,
                  pl.BlockSpec((3, D_MODEL), lambda i: (0, 0)), row],
        out_specs=[row, row, row, pl.BlockSpec((tm, 3 * D_MODEL), lambda i: (i, 0)),
                   pl.BlockSpec((8, D_MODEL), lambda i: (0, 0))],
        out_shape=[rowb, rowb, rowb, jax.ShapeDtypeStruct((t, 3 * D_MODEL), BF16),
                   jax.ShapeDtypeStruct((8, D_MODEL), F32)],
        compiler_params=_cp(("arbitrary",)),
    )(proj, br[0], br[1], br[2], gb, dm)


def _rope_tables(s):
    pos = jnp.arange(s, dtype=F32)
    inv_freq = ROPE_THETA ** (-jnp.arange(0, HEAD_DIM, 2, dtype=F32) / HEAD_DIM)
    ang = pos[:, None] * inv_freq[None, :]
    cos, sin = jnp.cos(ang), jnp.sin(ang)
    return jnp.tile(cos, (1, 4)), jnp.tile(jnp.concatenate([-sin, sin], axis=1), (1, 2))


def _dup_kv(proj3, name):
    b, s, _ = proj3.shape
    p0, sz = _PAD_COLS[name]
    kv = proj3[:, :, p0:p0 + sz].reshape(b, s, N_GROUPS, 1, HEAD_DIM)
    return jnp.broadcast_to(kv, (b, s, N_GROUPS, 2, HEAD_DIM)).reshape(b, s, 512)


def _pair_rows(cum, tb):
    b, s, _ = cum.shape
    t = jnp.transpose(cum[:, :, :N_HEADS], (0, 2, 1)).reshape(b, N_HEADS // 2, 2, s // tb, tb)
    return jnp.pad(jnp.transpose(t, (0, 1, 3, 2, 4)), ((0, 0), (0, 0), (0, 0), (0, 6), (0, 0)))


def _layer_params(wl):
    return dict(
        dtb=_group_lanes(wl["dt_bias"]), alog=_group_lanes(wl["a_log"]), dsk=_group_lanes(wl["d_skip"]),
        nw=wl["ssm_norm_w"].reshape(N_GROUPS, 1, 256), sinks=_group_lanes(wl["sinks"]),
        fb=jnp.pad(wl["f_bias"], (0, LANES - N_HEADS)).reshape(1, LANES))


def _layer_fwd(x, wl, tabs, bsz, li, tb):
    t = x.shape[0]
    s = t // bsz
    cos, sin = tabs
    lp = _layer_params(wl)
    n = lambda k: f"l{li}_{k}"
    h, h_t = _rms_fwd(x, wl["norm_w"], name=n("rms_fwd"))
    proj = _mm(h, wl["w_in"], tm=1024, tn=768, tk=1024, name=n("mm_proj"))
    proj3 = proj.reshape(bsz, s, N_PAD)
    xact3 = _conv_fwd(proj3, wl["conv_w"], wl["conv_b"], name=n("conv_fwd"))
    ya3, ypre3, hst = _ssd_fwd(proj3, xact3, lp["dtb"], lp["alog"], lp["dsk"], lp["nw"], name=n("ssd_fwd"))
    k2, v2 = _dup_kv(proj3, "b_k"), _dup_kv(proj3, "b_v")
    yb3, ob3, lseb3 = _swa_fwd(proj3, k2, v2, cos, sin, lp["sinks"], name=n("swa_fwd"))
    cum = _fgate_fwd(proj3, lp["fb"], name=n("fgate_fwd"))
    cum_t = _pair_rows(cum, tb)
    yc3, oc3, lsec3 = _fox_fwd(proj3, cum_t, name=n("fox_fwd"), tb=tb)
    ys = [v.reshape(t, D_MODEL) for v in (ya3, yb3, yc3)]
    br = [_mm(ys[i], wl["w_proj"][i], tm=1024, tn=1024, tk=1024, name=n(f"mm_br{i}")) for i in range(3)]
    merged = _merge_fwd(proj, br, wl["gate_bias"], name=n("merge_fwd"))
    x_new = _mm(merged, wl["w_out"], tm=1024, tn=1024, tk=1024, add=x, name=n("mm_out"))
    saved = dict(x=x, h_t=h_t, proj=proj, xact3=xact3, ypre3=ypre3, hst=hst, k2=k2, v2=v2, ob3=ob3, lseb3=lseb3,
                 cum_t=cum_t, oc3=oc3, lsec3=lsec3, ys=ys, br=br, merged=merged, lp=lp)
    return x_new, saved


def _layer_bwd(dx, wl, sv, tabs, bsz, li, tb):
    t = dx.shape[0]
    s = t // bsz
    cos, sin = tabs
    lp = sv["lp"]
    n = lambda k: f"l{li}_{k}"
    proj = sv["proj"]
    proj3 = proj.reshape(bsz, s, N_PAD)
    g = {}
    dmerged = _mm(dx, wl["w_out"], tb=True, tm=1024, tn=1024, tk=1024, name=n("mm_dmerged"))
    g["w_out"] = _mm(sv["merged"], dx, ta=True, tm=1024, tn=1024, tk=512, name=n("mm_dwout"))
    dbr0, dbr1, dbr2, dgates, dgb = _merge_bwd(proj, sv["br"], wl["gate_bias"], dmerged, name=n("merge_bwd"))
    g["gate_bias"] = dgb[:3]
    dbr = (dbr0, dbr1, dbr2)
    dys = [_mm(dbr[i], wl["w_proj"][i], tb=True, tm=1024, tn=1024, tk=1024, name=n(f"mm_dy{i}"))
           for i in range(3)]
    g["w_proj"] = jnp.stack([_mm(sv["ys"][i], dbr[i], ta=True, tm=1024, tn=1024, tk=512, name=n(f"mm_dwproj{i}"))
                             for i in range(3)])
    dy3 = [v.reshape(bsz, s, D_MODEL) for v in dys]

    (dact, daz, dadt, ddtb, dalog, ddsk, dnw) = _ssd_bwd(
        proj3, sv["xact3"], lp["dtb"], lp["alog"], lp["dsk"], lp["nw"], sv["ypre3"], sv["hst"], dy3[0],
        name=n("ssd_bwd"))
    g["dt_bias"], g["a_log"], g["d_skip"] = _ungroup_lanes(ddtb), _ungroup_lanes(dalog), _ungroup_lanes(ddsk)
    g["ssm_norm_w"] = dnw.reshape(D_MODEL)
    dxbc, dwb = _conv_bwd(proj3, wl["conv_w"], wl["conv_b"], dact, name=n("conv_bwd"))
    g["conv_w"], g["conv_b"] = dwb[:CONV_WIDTH], dwb[CONV_WIDTH]

    dbq, dbz, dkc, dkp, dvc, dvp, dsk = _swa_bwd(proj3, sv["k2"], sv["v2"], cos, sin, lp["sinks"], sv["ob3"],
                                                 sv["lseb3"], dy3[1], name=n("swa_bwd"))
    g["sinks"] = _ungroup_lanes(dsk)

    def fold(cur, prv):
        p4 = prv.reshape(bsz, s // _QROWS, 1, CHUNK, 512)
        tail = jnp.concatenate([p4[:, 1:], jnp.zeros_like(p4[:, :1])], axis=1)
        shifted = jnp.concatenate([jnp.zeros((bsz, s // _QROWS, _QB - 1, CHUNK, 512), F32), tail], axis=2)
        tot = cur + shifted.reshape(bsz, s, 512)
        return tot.reshape(bsz, s, N_GROUPS, 2, HEAD_DIM)[:, :, :, 0].reshape(bsz, s, 256)

    dbk, dbv = fold(dkc, dkp), fold(dvc, dvp)

    dcz, do3, stats = _fox_prep(proj3, sv["oc3"], sv["lsec3"], dy3[2], name=n("fox_prep"))
    dcq, dck, dcv, csum = _fox_bwd(proj3, sv["cum_t"], do3, stats, name=n("fox_bwd"), tb=tb)
    csum = jnp.transpose(csum[:, :, :, :2], (0, 1, 3, 2, 4)).reshape(bsz, N_HEADS, s)
    dcum = -jnp.transpose(csum, (0, 2, 1))
    dcum = jnp.pad(dcum, ((0, 0), (0, 0), (0, LANES - N_HEADS)))
    dcf, dfb = _fgate_bwd(proj3, lp["fb"], dcum, name=n("fgate_bwd"))
    g["f_bias"] = dfb[0, :N_HEADS]

    parts = {"gates": dgates.reshape(bsz, s, 3 * D_MODEL), "xbc": dxbc, "a_z": daz, "b_q": dbq, "b_z": dbz,
             "c_q": dcq, "c_k": dck, "c_v": dcv, "c_z": dcz, "b_k": dbk, "b_v": dbv, "a_dt": dadt, "c_f": dcf}
    dproj = jnp.concatenate([parts[name].astype(BF16) for name, _ in _PAD_ORDER]
                            + [jnp.zeros((bsz, s, N_PAD - N_USED), BF16)], axis=2).reshape(t, N_PAD)
    dh = _mm(dproj, wl["w_in"], tb=True, tm=1024, tn=1024, tk=768, name=n("mm_dh"))
    g["w_in"] = _unpad_w_in(_mm(sv["h_t"], dproj, tm=1024, tn=768, tk=512, name=n("mm_dwin")))
    dx_in, dnorm = _rms_bwd(sv["x"], wl["norm_w"], dh, dx, name=n("rms_bwd"))
    g["norm_w"] = dnorm[0]
    return dx_in, g


def _local_step(x, target, wls, final_norm_w, tb=512):
    bsz, s, d = x.shape
    t = bsz * s
    tabs = _rope_tables(s)
    xc = x.reshape(t, d)
    saved = []
    for li, wl in enumerate(wls):
        xc, sv = _layer_fwd(xc, wl, tabs, bsz, li, tb)
        saved.append(sv)
    loss, dx, dfw = _final_loss(xc, final_norm_w, target.reshape(t, d), name="final_loss")
    grads = [None] * len(wls)
    for li in reversed(range(len(wls))):
        dx, grads[li] = _layer_bwd(dx, wls[li], saved[li], tabs, bsz, li, tb)
    return loss[0, 0], dx.reshape(bsz, s, d), grads, dfw[0]


_HBM = pl.BlockSpec(memory_space=pltpu.HBM)


def _chip_peers(x, y):
    return [(1 - x, y), (x, 1 - y), (1 - x, 1 - y)]


def _gather_weights(arrs, *, name):
    n = len(arrs)

    def body(*refs):
        ins, outs = refs[:n], refs[n:2 * n]
        ici_send, ici_recv, d2d_send, d2d_recv = refs[2 * n:]
        x, y, c = lax.axis_index("x"), lax.axis_index("y"), lax.axis_index("c")
        me = 2 * x + y
        peers = _chip_peers(x, y)
        sib = (x, y, 1 - c)
        sends, fwds = [], []
        for a in range(n):
            for k, (px, py) in enumerate(peers):
                cp = pltpu.make_async_remote_copy(
                    src_ref=ins[a].at[c], dst_ref=outs[a].at[me, c], send_sem=ici_send.at[a, k],
                    recv_sem=ici_recv.at[a, k], device_id=(px, py, c), device_id_type=MESH)
                cp.start()
                sends.append(cp)
        for a in range(n):
            for k, (px, py) in enumerate(peers):
                slot = 2 * px + py
                pltpu.make_async_remote_copy(
                    src_ref=ins[a].at[c], dst_ref=outs[a].at[slot, c], send_sem=ici_send.at[a, k],
                    recv_sem=ici_recv.at[a, k], device_id=(px, py, c), device_id_type=MESH).wait_recv()
                fw = pltpu.make_async_remote_copy(
                    src_ref=outs[a].at[slot, c], dst_ref=outs[a].at[slot, c], send_sem=d2d_send.at[a, k],
                    recv_sem=d2d_recv.at[a, k], device_id=sib, device_id_type=MESH)
                fw.start()
                fwds.append(fw)
        for a in range(n):
            for k, (px, py) in enumerate(peers):
                slot = 2 * px + py
                pltpu.make_async_remote_copy(
                    src_ref=outs[a].at[slot, 1 - c], dst_ref=outs[a].at[slot, 1 - c], send_sem=d2d_send.at[a, k],
                    recv_sem=d2d_recv.at[a, k], device_id=sib, device_id_type=MESH).wait_recv()
        for cp in sends + fwds:
            cp.wait_send()

    out_shape = [jax.ShapeDtypeStruct((N_CHIPS,) + a.shape, a.dtype) for a in arrs]
    return pl.pallas_call(
        body, name=name, out_shape=out_shape, in_specs=[_HBM] * n, out_specs=[_HBM] * n,
        scratch_shapes=[pltpu.SemaphoreType.DMA((n, 3)), pltpu.SemaphoreType.DMA((n, 3)),
                        pltpu.SemaphoreType.DMA((n, 3)), pltpu.SemaphoreType.DMA((n, 3))],
    )(*arrs)


def _pair_exchange(arrs, *, name):
    n = len(arrs)

    def body(*refs):
        ins, outs = refs[:n], refs[n:2 * n]
        send, recv = refs[2 * n:]
        x, y, c = lax.axis_index("x"), lax.axis_index("y"), lax.axis_index("c")
        sib = (x, y, 1 - c)
        cps = []
        for a in range(n):
            for k in range(N_CHIPS):
                cp = pltpu.make_async_remote_copy(
                    src_ref=ins[a].at[k, 1 - c], dst_ref=outs[a].at[k], send_sem=send.at[a, k],
                    recv_sem=recv.at[a, k], device_id=sib, device_id_type=MESH)
                cp.start()
                cps.append(cp)
        for cp in cps:
            cp.wait()

    out_shape = [jax.ShapeDtypeStruct((N_CHIPS,) + a.shape[2:], a.dtype) for a in arrs]
    return pl.pallas_call(
        body, name=name, out_shape=out_shape, in_specs=[_HBM] * n, out_specs=[_HBM] * n,
        scratch_shapes=[pltpu.SemaphoreType.DMA((n, N_CHIPS)), pltpu.SemaphoreType.DMA((n, N_CHIPS))],
    )(*arrs)


def _chip_exchange(arrs, *, name):
    n = len(arrs)

    def body(*refs):
        ins, outs = refs[:n], refs[n:2 * n]
        send, recv = refs[2 * n:]
        x, y, c = lax.axis_index("x"), lax.axis_index("y"), lax.axis_index("c")
        me = 2 * x + y
        peers = _chip_peers(x, y)
        cps = []
        for a in range(n):
            for k, (px, py) in enumerate(peers):
                cp = pltpu.make_async_remote_copy(
                    src_ref=ins[a].at[2 * px + py], dst_ref=outs[a].at[me], send_sem=send.at[a, k],
                    recv_sem=recv.at[a, k], device_id=(px, py, c), device_id_type=MESH)
                cp.start()
                cps.append(cp)
        for a in range(n):
            for k, (px, py) in enumerate(peers):
                pltpu.make_async_remote_copy(
                    src_ref=ins[a].at[2 * px + py], dst_ref=outs[a].at[2 * px + py], send_sem=send.at[a, k],
                    recv_sem=recv.at[a, k], device_id=(px, py, c), device_id_type=MESH).wait_recv()
        for cp in cps:
            cp.wait_send()

    out_shape = [jax.ShapeDtypeStruct(a.shape, a.dtype) for a in arrs]
    return pl.pallas_call(
        body, name=name, out_shape=out_shape, in_specs=[_HBM] * n, out_specs=[_HBM] * n,
        scratch_shapes=[pltpu.SemaphoreType.DMA((n, 3)), pltpu.SemaphoreType.DMA((n, 3))],
    )(*arrs)


def _pair_share(arrs, *, name):
    n = len(arrs)

    def body(*refs):
        ins, outs = refs[:n], refs[n:2 * n]
        send, recv = refs[2 * n:]
        x, y, c = lax.axis_index("x"), lax.axis_index("y"), lax.axis_index("c")
        sib = (x, y, 1 - c)
        cps = []
        for a in range(n):
            cp = pltpu.make_async_remote_copy(
                src_ref=ins[a], dst_ref=outs[a], send_sem=send.at[a], recv_sem=recv.at[a],
                device_id=sib, device_id_type=MESH)
            cp.start()
            cps.append(cp)
        for cp in cps:
            cp.wait()

    out_shape = [jax.ShapeDtypeStruct(a.shape, a.dtype) for a in arrs]
    return pl.pallas_call(
        body, name=name, out_shape=out_shape, in_specs=[_HBM] * n, out_specs=[_HBM] * n,
        scratch_shapes=[pltpu.SemaphoreType.DMA((n,)), pltpu.SemaphoreType.DMA((n,))],
    )(*arrs)


def _allreduce_small(buf, *, name):
    r = buf.shape[0]

    def body(in_ref, out_ref, land, send, recv):
        x, y, c = lax.axis_index("x"), lax.axis_index("y"), lax.axis_index("c")
        me = 4 * x + 2 * y + c
        land[me] = in_ref[...]
        cps = []
        for k in range(1, N_DEV):
            px, py, pc = x ^ ((k >> 2) & 1), y ^ ((k >> 1) & 1), c ^ (k & 1)
            cp = pltpu.make_async_remote_copy(
                src_ref=in_ref, dst_ref=land.at[me], send_sem=send.at[k - 1], recv_sem=recv.at[k - 1],
                device_id=(px, py, pc), device_id_type=MESH)
            cp.start()
            cps.append(cp)
        for k in range(1, N_DEV):
            px, py, pc = x ^ ((k >> 2) & 1), y ^ ((k >> 1) & 1), c ^ (k & 1)
            pltpu.make_async_remote_copy(
                src_ref=in_ref, dst_ref=land.at[4 * px + 2 * py + pc], send_sem=send.at[k - 1],
                recv_sem=recv.at[k - 1], device_id=(px, py, pc), device_id_type=MESH).wait_recv()
        for cp in cps:
            cp.wait_send()
        acc = land[0]
        for k in range(1, N_DEV):
            acc = acc + land[k]
        out_ref[...] = acc

    vm = pl.BlockSpec(memory_space=pltpu.VMEM)
    return pl.pallas_call(
        body, name=name, out_shape=jax.ShapeDtypeStruct((r, LANES), F32), in_specs=[vm], out_specs=vm,
        scratch_shapes=[pltpu.VMEM((N_DEV, r, LANES), F32), pltpu.SemaphoreType.DMA((N_DEV - 1,)),
                        pltpu.SemaphoreType.DMA((N_DEV - 1,))],
    )(buf)


def _rows2d(a):
    return a.reshape(-1, a.shape[-1])


def _row_tile(rows, cols, n_arrays, budget=20 * 1024 * 1024):
    best = 8 if rows % 8 == 0 else rows
    tr = 8
    while tr <= rows:
        if rows % tr == 0 and tr * cols * 4 * n_arrays * 2 <= budget:
            best = tr
        tr *= 2
    return best


def _add_slot_layer(full, other, *, name):
    _, _, r, cdim = full.shape
    tr = _row_tile(r, cdim, 4)

    def body(c_ref, a_ref, b_ref, o_ref, ob_ref):
        sm = a_ref[...] + b_ref[...]
        o_ref[...] = sm
        ob_ref[...] = sm.astype(BF16)

    c = lax.axis_index("c").astype(jnp.int32).reshape(1)
    blk = pl.BlockSpec((None, tr, cdim), lambda k, i, c_ref: (k, i, 0))
    return pl.pallas_call(
        body, name=name,
        grid_spec=pltpu.PrefetchScalarGridSpec(
            num_scalar_prefetch=1, grid=(N_CHIPS, r // tr),
            in_specs=[pl.BlockSpec((None, None, tr, cdim), lambda k, i, c_ref: (k, c_ref[0], i, 0)), blk],
            out_specs=[blk, blk]),
        out_shape=[jax.ShapeDtypeStruct((N_CHIPS, r, cdim), F32), jax.ShapeDtypeStruct((N_CHIPS, r, cdim), BF16)],
        compiler_params=_cp(("parallel", "parallel")),
    )(c, full, other)


def _sum_slots(parts, pair, *, name):
    _, r, cdim = parts.shape
    tr = _row_tile(r, cdim, 5)

    def body(me_ref, p_ref, own_ref, o_ref):
        me = me_ref[0]
        acc = None
        for k in range(N_CHIPS):
            term = jnp.where(me == k, own_ref[...], p_ref[k].astype(F32))
            acc = term if acc is None else acc + term
        o_ref[...] = acc

    me = (2 * lax.axis_index("x") + lax.axis_index("y")).astype(jnp.int32).reshape(1)
    return pl.pallas_call(
        body, name=name,
        grid_spec=pltpu.PrefetchScalarGridSpec(
            num_scalar_prefetch=1, grid=(r // tr,),
            in_specs=[pl.BlockSpec((N_CHIPS, tr, cdim), lambda i, me_ref: (0, i, 0)),
                      pl.BlockSpec((None, tr, cdim), lambda i, me_ref: (me_ref[0], i, 0))],
            out_specs=pl.BlockSpec((tr, cdim), lambda i, me_ref: (i, 0))),
        out_shape=jax.ShapeDtypeStruct((r, cdim), F32),
        compiler_params=_cp(("parallel",)),
    )(me, parts, pair)


def _adamw(w, g, m, v, *, name):
    r, cdim = w.shape
    tr = _row_tile(r, cdim, 7)
    c1 = 1.0 - ADAM_B1 ** ADAM_STEP
    c2 = 1.0 - ADAM_B2 ** ADAM_STEP

    def body(w_ref, g_ref, m_ref, v_ref, d_ref, nm_ref, nv_ref):
        gv = g_ref[...]
        mn = ADAM_B1 * m_ref[...] + (1.0 - ADAM_B1) * gv
        vn = ADAM_B2 * v_ref[...] + (1.0 - ADAM_B2) * (gv * gv)
        nm_ref[...] = mn
        nv_ref[...] = vn
        d_ref[...] = -ADAM_LR * ((mn / c1) / (jnp.sqrt(vn / c2) + ADAM_EPS) + ADAM_WD * w_ref[...])

    blk = pl.BlockSpec((tr, cdim), lambda i: (i, 0))
    sh = jax.ShapeDtypeStruct((r, cdim), F32)
    return pl.pallas_call(
        body, name=name, grid=(r // tr,), in_specs=[blk] * 4, out_specs=[blk] * 3, out_shape=[sh] * 3,
        compiler_params=_cp(("parallel",)),
    )(w, g, m, v)


_SMALL = ("norm_w", "conv_b", "dt_bias", "a_log", "d_skip", "ssm_norm_w", "sinks", "f_bias", "final_norm_w",
          "conv_w", "gate_bias")


def _pack(vals):
    flat = jnp.concatenate([v.reshape(-1) for v in vals])
    rows = -(-flat.shape[0] // LANES)
    rows = -(-rows // 8) * 8
    return jnp.pad(flat, (0, rows * LANES - flat.shape[0])).reshape(rows, LANES)


def _unpack(buf, shapes):
    flat = buf.reshape(-1)
    out, off = [], 0
    for sh in shapes:
        sz = int(np.prod(sh))
        out.append(flat[off:off + sz].reshape(sh))
        off += sz
    return out


def kernel(x, norm_w, w_in, conv_w, conv_b, dt_bias, a_log, d_skip, ssm_norm_w, sinks, f_bias, gate_bias, w_proj, w_out, final_norm_w, loss_target, m_norm_w, m_w_in, m_conv_w, m_conv_b, m_dt_bias, m_a_log, m_d_skip, m_ssm_norm_w, m_sinks, m_f_bias, m_gate_bias, m_w_proj, m_w_out, m_final_norm_w, v_norm_w, v_w_in, v_conv_w, v_conv_b, v_dt_bias, v_a_log, v_d_skip, v_ssm_norm_w, v_sinks, v_f_bias, v_gate_bias, v_w_proj, v_w_out, v_final_norm_w):
    depth = w_in.shape[0]
    chip = 2 * lax.axis_index("x") + lax.axis_index("y")

    own = [w_in.astype(BF16), w_proj.astype(BF16), w_out.astype(BF16), conv_w, gate_bias]
    gathered = _gather_weights(own, name="gather_weights")

    def whole(a, li, axis):
        return jnp.concatenate([jnp.where(chip == k, own[a][li], gathered[a][k, li]) for k in range(N_CHIPS)],
                               axis=axis)

    wls = []
    for li in range(depth):
        wls.append(dict(
            norm_w=norm_w[li], w_in=_pad_w_in(whole(0, li, 1)),
            conv_w=whole(3, li, 1), conv_b=conv_b[li], dt_bias=dt_bias[li], a_log=a_log[li], d_skip=d_skip[li],
            ssm_norm_w=ssm_norm_w[li], sinks=sinks[li], f_bias=f_bias[li], gate_bias=whole(4, li, 1),
            w_proj=whole(1, li, 1),
            w_out=whole(2, li, 0)))

    loss_part, grad_x, grads, d_final = _local_step(x, loss_target, wls, final_norm_w)
    loss = lax.psum(loss_part, ("x", "y", "c"))

    c_in = w_in.shape[2]
    r_proj = w_proj.shape[2]
    r_out = w_out.shape[1]
    full_in = jnp.stack([jnp.stack([grads[li]["w_in"][:, k * c_in:(k + 1) * c_in] for li in range(depth)])
                         for k in range(N_CHIPS)])
    full_proj = jnp.stack([jnp.stack([grads[li]["w_proj"][:, k * r_proj:(k + 1) * r_proj].reshape(-1, D_MODEL)
                                      for li in range(depth)]) for k in range(N_CHIPS)])
    full_out = jnp.stack([jnp.stack([grads[li]["w_out"][k * r_out:(k + 1) * r_out] for li in range(depth)])
                          for k in range(N_CHIPS)])
    fulls = [full_in, full_proj, full_out]
    others = _pair_exchange(fulls, name="grad_pair_exchange")
    pair = [_add_slot_layer(f, o, name=f"grad_pair_add{i}") for i, (f, o) in enumerate(zip(fulls, others))]
    parts = _chip_exchange([p[1] for p in pair], name="grad_chip_exchange")
    mine = [_sum_slots(p, pr[0], name=f"grad_slot_sum{i}") for i, (p, pr) in enumerate(zip(parts, pair))]
    theirs = _pair_share(mine, name="grad_pair_share")
    core = lax.axis_index("c")
    red_in, red_proj, red_out = [jnp.stack([jnp.where(core == li, m, t) for li in range(depth)])
                                 for m, t in zip(mine, theirs)]
    grad_w_in = red_in
    grad_w_proj = red_proj.reshape(w_proj.shape)
    grad_w_out = red_out

    small_full = {
        "norm_w": jnp.stack([g["norm_w"] for g in grads]), "conv_b": jnp.stack([g["conv_b"] for g in grads]),
        "dt_bias": jnp.stack([g["dt_bias"] for g in grads]), "a_log": jnp.stack([g["a_log"] for g in grads]),
        "d_skip": jnp.stack([g["d_skip"] for g in grads]),
        "ssm_norm_w": jnp.stack([g["ssm_norm_w"] for g in grads]),
        "sinks": jnp.stack([g["sinks"] for g in grads]), "f_bias": jnp.stack([g["f_bias"] for g in grads]),
        "final_norm_w": d_final,
        "conv_w": jnp.stack([g["conv_w"] for g in grads]), "gate_bias": jnp.stack([g["gate_bias"] for g in grads])}
    shapes = [small_full[k].shape for k in _SMALL]
    summed = _unpack(_allreduce_small(_pack([small_full[k] for k in _SMALL]), name="allreduce_small"), shapes)
    gsmall = dict(zip(_SMALL, summed))
    gsmall["conv_w"] = lax.dynamic_slice_in_dim(gsmall["conv_w"], chip * conv_w.shape[2], conv_w.shape[2], axis=2)
    gsmall["gate_bias"] = lax.dynamic_slice_in_dim(gsmall["gate_bias"], chip * gate_bias.shape[2],
                                                   gate_bias.shape[2], axis=2)

    w_small = dict(norm_w=norm_w, conv_b=conv_b, dt_bias=dt_bias, a_log=a_log, d_skip=d_skip,
                   ssm_norm_w=ssm_norm_w, sinks=sinks, f_bias=f_bias, final_norm_w=final_norm_w, conv_w=conv_w,
                   gate_bias=gate_bias)
    m_small = dict(norm_w=m_norm_w, conv_b=m_conv_b, dt_bias=m_dt_bias, a_log=m_a_log, d_skip=m_d_skip,
                   ssm_norm_w=m_ssm_norm_w, sinks=m_sinks, f_bias=m_f_bias, final_norm_w=m_final_norm_w,
                   conv_w=m_conv_w, gate_bias=m_gate_bias)
    v_small = dict(norm_w=v_norm_w, conv_b=v_conv_b, dt_bias=v_dt_bias, a_log=v_a_log, d_skip=v_d_skip,
                   ssm_norm_w=v_ssm_norm_w, sinks=v_sinks, f_bias=v_f_bias, final_norm_w=v_final_norm_w,
                   conv_w=v_conv_w, gate_bias=v_gate_bias)
    sshapes = [w_small[k].shape for k in _SMALL]
    ds, ms, vs = _adamw(_pack([w_small[k] for k in _SMALL]), _pack([gsmall[k] for k in _SMALL]),
                        _pack([m_small[k] for k in _SMALL]), _pack([v_small[k] for k in _SMALL]), name="adamw_small")
    delta = dict(zip(_SMALL, _unpack(ds, sshapes)))
    new_m = dict(zip(_SMALL, _unpack(ms, sshapes)))
    new_v = dict(zip(_SMALL, _unpack(vs, sshapes)))
    grad = dict(gsmall)
    for nm, w, g, m, v in (("w_in", w_in, grad_w_in, m_w_in, v_w_in),
                           ("w_proj", w_proj, grad_w_proj, m_w_proj, v_w_proj),
                           ("w_out", w_out, grad_w_out, m_w_out, v_w_out)):
        d2, m2, v2 = _adamw(_rows2d(w), _rows2d(g), _rows2d(m), _rows2d(v), name=f"adamw_{nm}")
        grad[nm] = g
        delta[nm], new_m[nm], new_v[nm] = d2.reshape(w.shape), m2.reshape(w.shape), v2.reshape(w.shape)

    order = ("norm_w", "w_in", "conv_w", "conv_b", "dt_bias", "a_log", "d_skip", "ssm_norm_w", "sinks", "f_bias",
             "gate_bias", "w_proj", "w_out", "final_norm_w")
    return (loss, grad_x, *[grad[k] for k in order], *[delta[k] for k in order],
            *[new_m[k] for k in order], *[new_v[k] for k in order])
```

```python
import functools
import math

import numpy as np
import jax
import jax.numpy as jnp
from jax import lax
from jax.experimental import pallas as pl
from jax.experimental.pallas import tpu as pltpu

F32 = jnp.float32
BF16 = jnp.bfloat16
HIGHEST = lax.Precision.HIGHEST
MESH = pl.DeviceIdType.MESH

D_MODEL = 1024
HEAD_DIM = 64
N_HEADS = 16
N_GROUPS = 4
SSM_STATE = 128
CHUNK = 128
CONV_WIDTH = 4
CONV_DIM = 2048
ROPE_THETA = 10000.0
NORM_EPS = 1e-6
LANES = 128
N_CHIPS = 4
N_DEV = 8

ADAM_LR = 0.001
ADAM_B1 = 0.9
ADAM_B2 = 0.999
ADAM_EPS = 1e-08
ADAM_WD = 0.01
ADAM_STEP = 10

_REF_COLS = {}
_off = 0
for _n, _s in (("xbc", 2048), ("a_z", 1024), ("a_dt", 16), ("b_q", 1024), ("b_k", 256), ("b_v", 256),
               ("b_z", 1024), ("c_q", 1024), ("c_k", 1024), ("c_v", 1024), ("c_f", 16), ("c_z", 1024),
               ("gates", 3072)):
    _REF_COLS[_n] = (_off, _s)
    _off += _s
N_IN = _off

_PAD_ORDER = (("gates", 3072), ("xbc", 2048), ("a_z", 1024), ("b_q", 1024), ("b_z", 1024), ("c_q", 1024),
              ("c_k", 1024), ("c_v", 1024), ("c_z", 1024), ("b_k", 256), ("b_v", 256), ("a_dt", 512),
              ("c_f", 128))
_PAD_COLS = {}
_off = 0
for _n, _s in _PAD_ORDER:
    _PAD_COLS[_n] = (_off, _s)
    _off += _s
N_USED = _off
N_PAD = 13824


def _cp(sem, vmem_mb=48):
    return pltpu.CompilerParams(dimension_semantics=sem, vmem_limit_bytes=vmem_mb * 1024 * 1024)


def _dot(a, b, dims=((1,), (0,)), precision=None):
    return lax.dot_general(a, b, (dims, ((), ())), preferred_element_type=F32, precision=precision)


def _dot_nt(a, b):
    return _dot(a, b, ((1,), (1,)))


def _dot_tn(a, b):
    return _dot(a, b, ((0,), (0,)))


def _col(v, idx):
    lane = lax.broadcasted_iota(jnp.int32, v.shape, 1)
    return jnp.sum(jnp.where(lane == idx, v, 0.0), axis=1, keepdims=True)


def _row(v, idx):
    row = lax.broadcasted_iota(jnp.int32, v.shape, 0)
    return jnp.sum(jnp.where(row == idx, v, 0.0), axis=0, keepdims=True)


def _iota_col():
    return lax.broadcasted_iota(jnp.int32, (CHUNK, 1), 0)


def _iota_row():
    return lax.broadcasted_iota(jnp.int32, (1, LANES), 1)


def _sigmoid(x):
    return 1.0 / (1.0 + jnp.exp(-x))


def _softplus(x):
    return jnp.maximum(x, 0.0) + jnp.log(1.0 + jnp.exp(-jnp.abs(x)))


def _pad_w_in(w):
    parts = []
    for name, size in _PAD_ORDER:
        s0, sz = _REF_COLS[name]
        seg = w[:, s0:s0 + sz]
        if name == "a_dt":
            seg = jnp.pad(seg.reshape(-1, N_GROUPS, 4), ((0, 0), (0, 0), (0, LANES - 4))).reshape(-1, 512)
        elif name == "c_f":
            seg = jnp.pad(seg, ((0, 0), (0, LANES - 16)))
        parts.append(seg)
    parts.append(jnp.zeros((w.shape[0], N_PAD - N_USED), w.dtype))
    return jnp.concatenate(parts, axis=1)


def _unpad_w_in(wp):
    segs = {}
    for name, _ in _PAD_ORDER:
        p0, psz = _PAD_COLS[name]
        seg = wp[:, p0:p0 + psz]
        if name == "a_dt":
            seg = seg.reshape(-1, N_GROUPS, LANES)[:, :, :4].reshape(-1, 16)
        elif name == "c_f":
            seg = seg[:, :16]
        segs[name] = seg
    order = sorted(_REF_COLS, key=lambda n: _REF_COLS[n][0])
    return jnp.concatenate([segs[n] for n in order], axis=1)


def _group_lanes(v):
    return jnp.pad(v.reshape(N_GROUPS, 1, 4), ((0, 0), (0, 0), (0, LANES - 4)))


def _ungroup_lanes(v):
    return v[:, 0, :4].reshape(16)


def _mm(a, b, *, ta=False, tb=False, tm=512, tn=512, tk=512, out_dtype=F32, add=None, name):
    if ta:
        kdim, m = a.shape
    else:
        m, kdim = a.shape
    if tb:
        n, k2 = b.shape
    else:
        k2, n = b.shape
    assert kdim == k2, (a.shape, b.shape)
    tm, tn, tk = min(tm, m), min(tn, n), min(tk, kdim)
    assert m % tm == 0 and n % tn == 0 and kdim % tk == 0, (m, n, kdim, tm, tn, tk)
    nk = kdim // tk
    a_spec = (pl.BlockSpec((tk, tm), lambda i, j, k: (k, i)) if ta
              else pl.BlockSpec((tm, tk), lambda i, j, k: (i, k)))
    b_spec = (pl.BlockSpec((tn, tk), lambda i, j, k: (j, k)) if tb
              else pl.BlockSpec((tk, tn), lambda i, j, k: (k, j)))
    dims = ((0 if ta else 1,), (1 if tb else 0,))
    has_add = add is not None

    def body(*refs):
        if has_add:
            a_ref, b_ref, add_ref, o_ref, acc_ref = refs
        else:
            a_ref, b_ref, o_ref, acc_ref = refs
        k = pl.program_id(2)
        p = _dot(a_ref[...].astype(BF16), b_ref[...].astype(BF16), dims)

        @pl.when(k == 0)
        def _():
            acc_ref[...] = p

        @pl.when(k > 0)
        def _():
            acc_ref[...] += p

        @pl.when(k == nk - 1)
        def _():
            r = acc_ref[...]
            if has_add:
                r = r + add_ref[...]
            o_ref[...] = r.astype(out_dtype)

    in_specs = [a_spec, b_spec]
    args = [a, b]
    if has_add:
        in_specs.append(pl.BlockSpec((tm, tn), lambda i, j, k: (i, j)))
        args.append(add)
    return pl.pallas_call(
        body, name=name, grid=(m // tm, n // tn, nk),
        in_specs=in_specs, out_specs=pl.BlockSpec((tm, tn), lambda i, j, k: (i, j)),
        out_shape=jax.ShapeDtypeStruct((m, n), out_dtype),
        scratch_shapes=[pltpu.VMEM((tm, tn), F32)],
        compiler_params=_cp(("parallel", "parallel", "arbitrary")),
    )(*args)


def _rms_fwd(x, w, *, name, tm=512):
    t, d = x.shape

    def body(x_ref, w_ref, o_ref, ot_ref):
        xv = x_ref[...]
        r = lax.rsqrt(jnp.mean(xv * xv, axis=1, keepdims=True) + NORM_EPS)
        h = xv * r * w_ref[...]
        o_ref[...] = h.astype(BF16)
        ot_ref[...] = h.T.astype(BF16)

    return pl.pallas_call(
        body, name=name, grid=(t // tm,),
        in_specs=[pl.BlockSpec((tm, d), lambda i: (i, 0)), pl.BlockSpec((1, d), lambda i: (0, 0))],
        out_specs=[pl.BlockSpec((tm, d), lambda i: (i, 0)), pl.BlockSpec((d, tm), lambda i: (0, i))],
        out_shape=[jax.ShapeDtypeStruct((t, d), BF16), jax.ShapeDtypeStruct((d, t), BF16)],
        compiler_params=_cp(("parallel",)),
    )(x, w.reshape(1, d))


def _rms_bwd(x, w, dh, dres, *, name, tm=512):
    t, d = x.shape

    def body(x_ref, w_ref, dh_ref, dres_ref, dx_ref, dw_ref):
        xv = x_ref[...]
        r = lax.rsqrt(jnp.mean(xv * xv, axis=1, keepdims=True) + NORM_EPS)
        xhat = xv * r
        dhv = dh_ref[...]
        dxhat = dhv * w_ref[...]
        dx = r * (dxhat - xhat * jnp.mean(dxhat * xhat, axis=1, keepdims=True))
        dx_ref[...] = dres_ref[...] + dx

        @pl.when(pl.program_id(0) == 0)
        def _():
            dw_ref[...] = jnp.zeros_like(dw_ref)

        dw_ref[...] += jnp.sum(dhv * xhat, axis=0, keepdims=True)

    return pl.pallas_call(
        body, name=name, grid=(t // tm,),
        in_specs=[pl.BlockSpec((tm, d), lambda i: (i, 0)), pl.BlockSpec((1, d), lambda i: (0, 0)),
                  pl.BlockSpec((tm, d), lambda i: (i, 0)), pl.BlockSpec((tm, d), lambda i: (i, 0))],
        out_specs=[pl.BlockSpec((tm, d), lambda i: (i, 0)), pl.BlockSpec((1, d), lambda i: (0, 0))],
        out_shape=[jax.ShapeDtypeStruct((t, d), F32), jax.ShapeDtypeStruct((1, d), F32)],
        compiler_params=_cp(("arbitrary",)),
    )(x, w.reshape(1, d), dh, dres)


def _final_loss(x, w, target, *, name, tm=512):
    t, d = x.shape

    def body(x_ref, w_ref, t_ref, loss_ref, dx_ref, dw_ref):
        xv = x_ref[...]
        wv = w_ref[...]
        r = lax.rsqrt(jnp.mean(xv * xv, axis=1, keepdims=True) + NORM_EPS)
        xhat = xv * r
        err = xhat * wv - t_ref[...]
        dy = err * (1.0 / d)
        dxhat = dy * wv
        dx_ref[...] = r * (dxhat - xhat * jnp.mean(dxhat * xhat, axis=1, keepdims=True))

        @pl.when(pl.program_id(0) == 0)
        def _():
            dw_ref[...] = jnp.zeros_like(dw_ref)
            loss_ref[...] = jnp.zeros_like(loss_ref)

        dw_ref[...] += jnp.sum(dy * xhat, axis=0, keepdims=True)
        part = 0.5 * jnp.sum(jnp.mean(err * err, axis=1, keepdims=True), axis=0, keepdims=True)
        loss_ref[...] += jnp.broadcast_to(part, loss_ref.shape)

    return pl.pallas_call(
        body, name=name, grid=(t // tm,),
        in_specs=[pl.BlockSpec((tm, d), lambda i: (i, 0)), pl.BlockSpec((1, d), lambda i: (0, 0)),
                  pl.BlockSpec((tm, d), lambda i: (i, 0))],
        out_specs=[pl.BlockSpec((8, LANES), lambda i: (0, 0)), pl.BlockSpec((tm, d), lambda i: (i, 0)),
                   pl.BlockSpec((1, d), lambda i: (0, 0))],
        out_shape=[jax.ShapeDtypeStruct((8, LANES), F32), jax.ShapeDtypeStruct((t, d), F32),
                   jax.ShapeDtypeStruct((1, d), F32)],
        compiler_params=_cp(("arbitrary",)),
    )(x, w.reshape(1, d), target)


_CB = 128


def _conv_pre(u, w_ref, b_ref):
    s = u.shape[0]
    row = lax.broadcasted_iota(jnp.int32, u.shape, 0)
    pre = b_ref[...] + w_ref[CONV_WIDTH - 1:CONV_WIDTH, :] * u
    for sh in range(1, CONV_WIDTH):
        shifted = jnp.where(row >= sh, pltpu.roll(u, sh, 0), 0.0)
        pre = pre + w_ref[CONV_WIDTH - 1 - sh:CONV_WIDTH - sh, :] * shifted
    return pre


def _conv_fwd(proj3, cw, cb, *, name):
    b, s, _ = proj3.shape
    c0 = _PAD_COLS["xbc"][0] // _CB

    def body(u_ref, w_ref, b_ref, o_ref):
        pre = _conv_pre(u_ref[...], w_ref, b_ref)
        o_ref[...] = pre * _sigmoid(pre)

    return pl.pallas_call(
        body, name=name, grid=(b, CONV_DIM // _CB),
        in_specs=[pl.BlockSpec((None, s, _CB), lambda i, j: (i, 0, c0 + j)),
                  pl.BlockSpec((CONV_WIDTH, _CB), lambda i, j: (0, j)),
                  pl.BlockSpec((1, _CB), lambda i, j: (0, j))],
        out_specs=pl.BlockSpec((None, s, _CB), lambda i, j: (i, 0, j)),
        out_shape=jax.ShapeDtypeStruct((b, s, CONV_DIM), F32),
        compiler_params=_cp(("parallel", "parallel")),
    )(proj3, cw, cb.reshape(1, CONV_DIM))


def _conv_bwd(proj3, cw, cb, dact, *, name):
    b, s, _ = proj3.shape
    c0 = _PAD_COLS["xbc"][0] // _CB

    def body(u_ref, w_ref, b_ref, da_ref, du_ref, dwb_ref):
        u = u_ref[...]
        pre = _conv_pre(u, w_ref, b_ref)
        sg = _sigmoid(pre)
        dpre = da_ref[...] * (sg * (1.0 + pre * (1.0 - sg)))
        row = lax.broadcasted_iota(jnp.int32, u.shape, 0)
        du = w_ref[CONV_WIDTH - 1:CONV_WIDTH, :] * dpre
        rows = [jnp.sum(dpre * u, axis=0, keepdims=True)]
        for sh in range(1, CONV_WIDTH):
            fwd_shift = jnp.where(row < s - sh, pltpu.roll(dpre, s - sh, 0), 0.0)
            du = du + w_ref[CONV_WIDTH - 1 - sh:CONV_WIDTH - sh, :] * fwd_shift
            ush = jnp.where(row >= sh, pltpu.roll(u, sh, 0), 0.0)
            rows.append(jnp.sum(dpre * ush, axis=0, keepdims=True))
        du_ref[...] = du.astype(BF16)

        @pl.when(pl.program_id(1) == 0)
        def _():
            dwb_ref[...] = jnp.zeros_like(dwb_ref)

        for sh in range(CONV_WIDTH):
            k = CONV_WIDTH - 1 - sh
            dwb_ref[k:k + 1, :] += rows[sh]
        dwb_ref[CONV_WIDTH:CONV_WIDTH + 1, :] += jnp.sum(dpre, axis=0, keepdims=True)

    return pl.pallas_call(
        body, name=name, grid=(CONV_DIM // _CB, b),
        in_specs=[pl.BlockSpec((None, s, _CB), lambda j, i: (i, 0, c0 + j)),
                  pl.BlockSpec((CONV_WIDTH, _CB), lambda j, i: (0, j)),
                  pl.BlockSpec((1, _CB), lambda j, i: (0, j)),
                  pl.BlockSpec((None, s, _CB), lambda j, i: (i, 0, j))],
        out_specs=[pl.BlockSpec((None, s, _CB), lambda j, i: (i, 0, j)),
                   pl.BlockSpec((8, _CB), lambda j, i: (0, j))],
        out_shape=[jax.ShapeDtypeStruct((b, s, CONV_DIM), BF16), jax.ShapeDtypeStruct((8, CONV_DIM), F32)],
        compiler_params=_cp(("parallel", "arbitrary")),
    )(proj3, cw, cb.reshape(1, CONV_DIM), dact)


def _ssd_common(dt_ref, dtb_ref, alog_ref):
    row = lax.broadcasted_iota(jnp.int32, (CHUNK, CHUNK), 0)
    lane = lax.broadcasted_iota(jnp.int32, (CHUNK, CHUNK), 1)
    causal = row >= lane
    tri = causal.astype(F32)
    dtv = _softplus(dt_ref[...] + dtb_ref[...])
    a_row = -jnp.exp(alog_ref[...])
    acum = _dot(tri, dtv * a_row, precision=HIGHEST)
    return row, lane, causal, dtv, a_row, acum, acum.T


def _ssd_pair(pp, x, dtv, acum, acum_t, causal, lane, row):
    lo = lane < HEAD_DIM
    r0, r1 = 2 * pp, 2 * pp + 1
    dtp = jnp.where(lo, _col(dtv, r0), _col(dtv, r1))
    ac0, ac1 = _col(acum, r0), _col(acum, r1)
    ar0, ar1 = _row(acum_t, r0), _row(acum_t, r1)
    d0 = jnp.where(causal, jnp.exp(jnp.where(causal, ac0 - ar0, 0.0)), 0.0)
    d1 = jnp.where(causal, jnp.exp(jnp.where(causal, ac1 - ar1, 0.0)), 0.0)
    al0, al1 = _col(ar0, CHUNK - 1), _col(ar1, CHUNK - 1)
    eac = jnp.where(lo, jnp.exp(ac0), jnp.exp(ac1))
    dsp = jnp.where(lo, jnp.exp(al0 - ac0), jnp.exp(al1 - ac1))
    eal = jnp.where(_iota_col() < HEAD_DIM, jnp.exp(al0), jnp.exp(al1))
    return lo, dtp, x * dtp, d0, d1, al0, al1, eac, dsp, eal


def _ssd_fwd(proj3, xact3, dtb, alog, dsk, nw, *, name):
    b, s, _ = proj3.shape
    nc = s // CHUNK
    dt0 = _PAD_COLS["a_dt"][0] // 512
    z0 = _PAD_COLS["a_z"][0] // D_MODEL

    def body(xs_ref, bm_ref, cm_ref, dt_ref, z_ref, dtb_ref, alog_ref, dsk_ref, nw_ref,
             ya_ref, ypre_ref, hst_ref, h_scr):
        @pl.when(pl.program_id(1) == 0)
        def _():
            h_scr[...] = jnp.zeros_like(h_scr)

        for g in range(N_GROUPS):
            w256 = pl.ds(256 * g, 256)
            w128 = pl.ds(LANES * g, LANES)
            group(xs_ref.at[:, w256], bm_ref.at[:, w128], cm_ref.at[:, w128], dt_ref.at[:, w128],
                  z_ref.at[:, w256], dtb_ref.at[g], alog_ref.at[g], dsk_ref.at[g], nw_ref.at[g],
                  ya_ref.at[:, w256], ypre_ref.at[:, w256], hst_ref.at[g], h_scr.at[g])

    def group(xs_ref, bm_ref, cm_ref, dt_ref, z_ref, dtb_ref, alog_ref, dsk_ref, nw_ref,
              ya_ref, ypre_ref, hst_ref, h_scr):
        row, lane, causal, dtv, a_row, acum, acum_t = _ssd_common(dt_ref, dtb_ref, alog_ref)
        bb = bm_ref[...].astype(BF16)
        cb = cm_ref[...].astype(BF16)
        cbm = _dot_nt(cb, bb)
        hst_ref[...] = h_scr[...]
        dskv = dsk_ref[...]
        for pp in range(2):
            x = xs_ref[:, LANES * pp:LANES * (pp + 1)]
            lo, dtp, xd, d0, d1, al0, al1, eac, dsp, eal = _ssd_pair(pp, x, dtv, acum, acum_t, causal, lane, row)
            xdb = xd.astype(BF16)
            y = jnp.where(lo, _dot((cbm * d0).astype(BF16), xdb), _dot((cbm * d1).astype(BF16), xdb))
            h = h_scr[pp]
            y = y + eac * _dot_nt(cb, h.astype(BF16))
            h_scr[pp] = h * eal + _dot_tn((xd * dsp).astype(BF16), bb)
            dskp = jnp.where((_iota_row() < HEAD_DIM), _col(dskv, 2 * pp), _col(dskv, 2 * pp + 1))
            ypre_ref[:, LANES * pp:LANES * (pp + 1)] = y + x * dskp
        ypre = ypre_ref[...]
        z = z_ref[...]
        yg = ypre * (z * _sigmoid(z))
        rstd = lax.rsqrt(jnp.sum(yg * yg, axis=1, keepdims=True) * (1.0 / 256.0) + NORM_EPS)
        ya_ref[...] = (yg * rstd * nw_ref[...]).astype(BF16)

    g = N_GROUPS
    par = pl.BlockSpec((g, 1, LANES), lambda i, c: (0, 0, 0))
    wide = pl.BlockSpec((None, CHUNK, D_MODEL), lambda i, c: (i, c, 0))
    return pl.pallas_call(
        body, name=name, grid=(b, nc),
        in_specs=[wide,
                  pl.BlockSpec((None, CHUNK, 512), lambda i, c: (i, c, 2)),
                  pl.BlockSpec((None, CHUNK, 512), lambda i, c: (i, c, 3)),
                  pl.BlockSpec((None, CHUNK, 512), lambda i, c: (i, c, dt0)),
                  pl.BlockSpec((None, CHUNK, D_MODEL), lambda i, c: (i, c, z0)),
                  par, par, par,
                  pl.BlockSpec((g, 1, 256), lambda i, c: (0, 0, 0))],
        out_specs=[wide, wide,
                   pl.BlockSpec((None, None, g, 2, CHUNK, SSM_STATE), lambda i, c: (i, c, 0, 0, 0, 0))],
        out_shape=[jax.ShapeDtypeStruct((b, s, D_MODEL), BF16), jax.ShapeDtypeStruct((b, s, D_MODEL), F32),
                   jax.ShapeDtypeStruct((b, nc, g, 2, CHUNK, SSM_STATE), F32)],
        scratch_shapes=[pltpu.VMEM((g, 2, CHUNK, SSM_STATE), F32)],
        compiler_params=_cp(("parallel", "arbitrary")),
    )(xact3, xact3, xact3, proj3, proj3, dtb, alog, dsk, nw)


def _ssd_bwd(proj3, xact3, dtb, alog, dsk, nw, ypre3, hst, dya3, *, name):
    b, s, _ = proj3.shape
    nc = s // CHUNK
    dt0 = _PAD_COLS["a_dt"][0] // 512
    z0 = _PAD_COLS["a_z"][0] // D_MODEL

    def body(xs_ref, bm_ref, cm_ref, dt_ref, z_ref, dtb_ref, alog_ref, dsk_ref, nw_ref, ypre_ref, hst_ref,
             dya_ref, dact_ref, dz_ref, ddt_ref, ddtb_ref, dalog_ref, ddsk_ref, dnw_ref, dh_scr):
        first = jnp.logical_and(pl.program_id(0) == 0, pl.program_id(1) == 0)

        @pl.when(first)
        def _():
            ddtb_ref[...] = jnp.zeros_like(ddtb_ref)
            dalog_ref[...] = jnp.zeros_like(dalog_ref)
            ddsk_ref[...] = jnp.zeros_like(ddsk_ref)
            dnw_ref[...] = jnp.zeros_like(dnw_ref)

        @pl.when(pl.program_id(1) == 0)
        def _():
            dh_scr[...] = jnp.zeros_like(dh_scr)

        for g in range(N_GROUPS):
            w256 = pl.ds(256 * g, 256)
            w128 = pl.ds(LANES * g, LANES)
            group(xs_ref.at[:, w256], bm_ref.at[:, w128], cm_ref.at[:, w128], dt_ref.at[:, w128],
                  z_ref.at[:, w256], dtb_ref.at[g], alog_ref.at[g], dsk_ref.at[g], nw_ref.at[g],
                  ypre_ref.at[:, w256], hst_ref.at[g], dya_ref.at[:, w256],
                  dact_ref.at[:, w256], dact_ref.at[:, pl.ds(D_MODEL + LANES * g, LANES)],
                  dact_ref.at[:, pl.ds(D_MODEL + 512 + LANES * g, LANES)], dz_ref.at[:, w256], ddt_ref.at[:, w128],
                  ddtb_ref.at[g], dalog_ref.at[g], ddsk_ref.at[g], dnw_ref.at[g], dh_scr.at[g])

    def group(xs_ref, bm_ref, cm_ref, dt_ref, z_ref, dtb_ref, alog_ref, dsk_ref, nw_ref, ypre_ref, hst_ref,
              dya_ref, dxs_ref, dbm_ref, dcm_ref, dz_ref, ddt_ref, ddtb_ref, dalog_ref, ddsk_ref, dnw_ref,
              dh_scr):
        row, lane, causal, dtv, a_row, acum, acum_t = _ssd_common(dt_ref, dtb_ref, alog_ref)
        lane1 = _iota_row()
        bb = bm_ref[...].astype(BF16)
        cb = cm_ref[...].astype(BF16)
        cbm = _dot_nt(cb, bb)

        z = z_ref[...]
        ypre = ypre_ref[...]
        dya = dya_ref[...]
        sz = _sigmoid(z)
        silu = z * sz
        yg = ypre * silu
        rstd = lax.rsqrt(jnp.sum(yg * yg, axis=1, keepdims=True) * (1.0 / 256.0) + NORM_EPS)
        dnw_ref[...] += jnp.sum(dya * yg * rstd, axis=0, keepdims=True)
        dn = dya * nw_ref[...]
        dyg = rstd * dn - yg * (rstd * rstd * rstd * (1.0 / 256.0)) * jnp.sum(dn * yg, axis=1, keepdims=True)
        dz_ref[...] = (dyg * ypre * (sz * (1.0 + z * (1.0 - sz)))).astype(BF16)
        dy_all = dyg * silu

        dskv = dsk_ref[...]
        da_cols = jnp.zeros((CHUNK, LANES), F32)
        dxt_cols = jnp.zeros((CHUNK, LANES), F32)
        ddsk_row = jnp.zeros((1, LANES), F32)
        dcb = jnp.zeros((CHUNK, CHUNK), F32)
        dc = jnp.zeros((CHUNK, SSM_STATE), F32)
        db = jnp.zeros((CHUNK, SSM_STATE), F32)
        last = _iota_col() == CHUNK - 1
        for pp in range(2):
            r0, r1 = 2 * pp, 2 * pp + 1
            x = xs_ref[:, LANES * pp:LANES * (pp + 1)]
            dy = dy_all[:, LANES * pp:LANES * (pp + 1)]
            lo, dtp, xd, d0, d1, al0, al1, eac, dsp, eal = _ssd_pair(pp, x, dtv, acum, acum_t, causal, lane, row)
            w0, w1 = cbm * d0, cbm * d1
            w0b, w1b = w0.astype(BF16), w1.astype(BF16)
            xdb = xd.astype(BF16)
            dyb = dy.astype(BF16)
            h = hst_ref[pp]
            dhn = dh_scr[pp]
            hb = h.astype(BF16)
            dhb = dhn.astype(BF16)
            g0 = _dot_nt(jnp.where(lo, dy, 0.0).astype(BF16), xdb)
            g1 = _dot_nt(jnp.where(lo, 0.0, dy).astype(BF16), xdb)
            dcb = dcb + g0 * d0 + g1 * d1
            m0, m1 = g0 * w0, g1 * w1
            bdh = _dot_nt(bb, dhb)
            dxd = jnp.where(lo, _dot_tn(w0b, dyb), _dot_tn(w1b, dyb)) + dsp * bdh
            ch = _dot_nt(cb, hb)
            edy = eac * dy
            edyb = edy.astype(BF16)
            xds = xd * dsp
            dc = dc + _dot(edyb, hb)
            db = db + _dot(xds.astype(BF16), dhb)
            dh_scr[pp] = dhn * eal + _dot_tn(edyb, cb)
            t2 = edy * ch
            t3 = xds * bdh
            r4 = jnp.sum(dhn * h, axis=1, keepdims=True)
            s4_0 = jnp.sum(jnp.where(_iota_col() < HEAD_DIM, r4, 0.0), axis=0, keepdims=True)
            s4_1 = jnp.sum(r4, axis=0, keepdims=True) - s4_0
            t2_0 = jnp.sum(jnp.where(lo, t2, 0.0), axis=1, keepdims=True)
            t2_1 = jnp.sum(t2, axis=1, keepdims=True) - t2_0
            t3_0 = jnp.sum(jnp.where(lo, t3, 0.0), axis=1, keepdims=True)
            t3_1 = jnp.sum(t3, axis=1, keepdims=True) - t3_0
            dal0 = jnp.sum(t3_0, axis=0, keepdims=True) + jnp.exp(al0) * s4_0
            dal1 = jnp.sum(t3_1, axis=0, keepdims=True) + jnp.exp(al1) * s4_1
            dac0 = (jnp.sum(m0, axis=1, keepdims=True) - jnp.sum(m0.T, axis=1, keepdims=True)
                    + t2_0 - t3_0 + jnp.where(last, dal0, 0.0))
            dac1 = (jnp.sum(m1, axis=1, keepdims=True) - jnp.sum(m1.T, axis=1, keepdims=True)
                    + t2_1 - t3_1 + jnp.where(last, dal1, 0.0))
            da_cols = da_cols + jnp.where(lane == r0, dac0, 0.0) + jnp.where(lane == r1, dac1, 0.0)
            xx = dxd * x
            x0 = jnp.sum(jnp.where(lo, xx, 0.0), axis=1, keepdims=True)
            x1 = jnp.sum(xx, axis=1, keepdims=True) - x0
            dxt_cols = dxt_cols + jnp.where(lane == r0, x0, 0.0) + jnp.where(lane == r1, x1, 0.0)
            dskp = jnp.where((_iota_row() < HEAD_DIM), _col(dskv, r0), _col(dskv, r1))
            dxs_ref[:, LANES * pp:LANES * (pp + 1)] = dxd * dtp + dy * dskp
            yx = jnp.sum(dy * x, axis=0, keepdims=True)
            k0 = jnp.sum(jnp.where((_iota_row() < HEAD_DIM), yx, 0.0), axis=1, keepdims=True)
            k1 = jnp.sum(yx, axis=1, keepdims=True) - k0
            ddsk_row = ddsk_row + jnp.where(lane1 == r0, k0, 0.0) + jnp.where(lane1 == r1, k1, 0.0)
        dcbb = dcb.astype(BF16)
        dcm_ref[...] = dc + _dot(dcbb, bb)
        dbm_ref[...] = db + _dot_tn(dcbb, cb)
        tri_t = (row <= lane).astype(F32)
        dadt = _dot(tri_t, da_cols, precision=HIGHEST)
        ddtv = dadt * a_row + dxt_cols
        dalog_ref[...] += jnp.sum(dadt * dtv, axis=0, keepdims=True) * a_row
        ddt_raw = ddtv * _sigmoid(dt_ref[...] + dtb_ref[...])
        ddt_ref[...] = ddt_raw.astype(BF16)
        ddtb_ref[...] += jnp.sum(ddt_raw, axis=0, keepdims=True)
        ddsk_ref[...] += ddsk_row

    g = N_GROUPS
    rc = lambda c: nc - 1 - c
    par = pl.BlockSpec((g, 1, LANES), lambda i, c: (0, 0, 0))
    parw = pl.BlockSpec((g, 1, 256), lambda i, c: (0, 0, 0))
    wide = pl.BlockSpec((None, CHUNK, D_MODEL), lambda i, c: (i, rc(c), 0))
    blk512 = lambda col: pl.BlockSpec((None, CHUNK, 512), lambda i, c: (i, rc(c), col))
    return pl.pallas_call(
        body, name=name, grid=(b, nc),
        in_specs=[wide, blk512(2), blk512(3), blk512(dt0),
                  pl.BlockSpec((None, CHUNK, D_MODEL), lambda i, c: (i, rc(c), z0)),
                  par, par, par, parw,
                  wide,
                  pl.BlockSpec((None, None, g, 2, CHUNK, SSM_STATE), lambda i, c: (i, rc(c), 0, 0, 0, 0)),
                  wide],
        out_specs=[pl.BlockSpec((None, CHUNK, CONV_DIM), lambda i, c: (i, rc(c), 0)), wide, blk512(0),
                   par, par, par, parw],
        out_shape=[jax.ShapeDtypeStruct((b, s, CONV_DIM), F32), jax.ShapeDtypeStruct((b, s, D_MODEL), BF16),
                   jax.ShapeDtypeStruct((b, s, 512), BF16),
                   jax.ShapeDtypeStruct((g, 1, LANES), F32), jax.ShapeDtypeStruct((g, 1, LANES), F32),
                   jax.ShapeDtypeStruct((g, 1, LANES), F32), jax.ShapeDtypeStruct((g, 1, 256), F32)],
        scratch_shapes=[pltpu.VMEM((g, 2, CHUNK, SSM_STATE), F32)],
        compiler_params=_cp(("arbitrary", "arbitrary")),
    )(xact3, xact3, xact3, proj3, proj3, dtb, alog, dsk, nw, ypre3, hst, dya3)


def _fgate_fwd(proj3, fb, *, name):
    b, s, _ = proj3.shape
    f0 = _PAD_COLS["c_f"][0] // LANES

    def body(f_ref, fb_ref, cum_ref, carry):
        @pl.when(pl.program_id(1) == 0)
        def _():
            carry[...] = jnp.zeros_like(carry)

        row = lax.broadcasted_iota(jnp.int32, (CHUNK, CHUNK), 0)
        lane = lax.broadcasted_iota(jnp.int32, (CHUNK, CHUNK), 1)
        tri = (row >= lane).astype(F32)
        lf = -_softplus(-(f_ref[...] + fb_ref[...]))
        cs = _dot(tri, lf, precision=HIGHEST) + carry[0:1, :]
        cum_ref[...] = cs
        carry[0:1, :] = _row(cs, CHUNK - 1)

    return pl.pallas_call(
        body, name=name, grid=(b, s // CHUNK),
        in_specs=[pl.BlockSpec((None, CHUNK, LANES), lambda i, c: (i, c, f0)),
                  pl.BlockSpec((1, LANES), lambda i, c: (0, 0))],
        out_specs=pl.BlockSpec((None, CHUNK, LANES), lambda i, c: (i, c, 0)),
        out_shape=jax.ShapeDtypeStruct((b, s, LANES), F32),
        scratch_shapes=[pltpu.VMEM((8, LANES), F32)],
        compiler_params=_cp(("parallel", "arbitrary")),
    )(proj3, fb)


def _fgate_bwd(proj3, fb, dcum, *, name):
    b, s, _ = proj3.shape
    nc = s // CHUNK
    f0 = _PAD_COLS["c_f"][0] // LANES

    def body(f_ref, fb_ref, dc_ref, df_ref, dfb_ref, carry):
        first = jnp.logical_and(pl.program_id(0) == 0, pl.program_id(1) == 0)

        @pl.when(first)
        def _():
            dfb_ref[...] = jnp.zeros_like(dfb_ref)

        @pl.when(pl.program_id(1) == 0)
        def _():
            carry[...] = jnp.zeros_like(carry)

        row = lax.broadcasted_iota(jnp.int32, (CHUNK, CHUNK), 0)
        lane = lax.broadcasted_iota(jnp.int32, (CHUNK, CHUNK), 1)
        tri_t = (row <= lane).astype(F32)
        dlf = _dot(tri_t, dc_ref[...], precision=HIGHEST) + carry[0:1, :]
        carry[0:1, :] = _row(dlf, 0)
        df = dlf * _sigmoid(-(f_ref[...] + fb_ref[...]))
        df_ref[...] = df.astype(BF16)
        dfb_ref[...] += jnp.sum(df, axis=0, keepdims=True)

    return pl.pallas_call(
        body, name=name, grid=(b, nc),
        in_specs=[pl.BlockSpec((None, CHUNK, LANES), lambda i, c: (i, nc - 1 - c, f0)),
                  pl.BlockSpec((1, LANES), lambda i, c: (0, 0)),
                  pl.BlockSpec((None, CHUNK, LANES), lambda i, c: (i, nc - 1 - c, 0))],
        out_specs=[pl.BlockSpec((None, CHUNK, LANES), lambda i, c: (i, nc - 1 - c, 0)),
                   pl.BlockSpec((1, LANES), lambda i, c: (0, 0))],
        out_shape=[jax.ShapeDtypeStruct((b, s, LANES), BF16), jax.ShapeDtypeStruct((1, LANES), F32)],
        scratch_shapes=[pltpu.VMEM((8, LANES), F32)],
        compiler_params=_cp(("arbitrary", "arbitrary")),
    )(proj3, fb, dcum)


_SCALE = HEAD_DIM ** -0.5
_NEG = -1e30


def _fox_fwd(proj3, cum_t, *, name, tb):
    b, s, _ = proj3.shape
    nq = s // tb
    q0 = _PAD_COLS["c_q"][0] // LANES
    k0 = _PAD_COLS["c_k"][0] // LANES
    v0 = _PAD_COLS["c_v"][0] // LANES
    z0 = _PAD_COLS["c_z"][0] // LANES

    def body(q_ref, k_ref, v_ref, z_ref, cumt_ref, y_ref, o_ref, lse_ref):
        i = pl.program_id(2)
        lo = lax.broadcasted_iota(jnp.int32, (tb, LANES), 1) < HEAD_DIM
        q = q_ref[...] * _SCALE
        qms = (jnp.where(lo, q, 0.0).astype(BF16), jnp.where(lo, 0.0, q).astype(BF16))

        def block(j, carry, diagonal):
            ks = pl.ds(pl.multiple_of(j * tb, tb), tb)
            kb = k_ref[ks, :].astype(BF16)
            vb = v_ref[ks, :].astype(BF16)
            ckv = cumt_ref[j]
            if diagonal:
                row = lax.broadcasted_iota(jnp.int32, (tb, tb), 0)
                col = lax.broadcasted_iota(jnp.int32, (tb, tb), 1)
                mask = row >= col
            ms, ls, acc = carry
            new_m, new_l, pvs, alphas = [], [], [], []
            for hh in range(2):
                sc = _dot_nt(qms[hh], kb) - ckv[hh:hh + 1, :]
                if diagonal:
                    sc = jnp.where(mask, sc, _NEG)
                m_new = jnp.maximum(ms[hh], jnp.max(sc, axis=1, keepdims=True))
                alpha = jnp.exp(ms[hh] - m_new)
                pr = jnp.exp(sc - m_new)
                new_l.append(alpha * ls[hh] + jnp.sum(pr, axis=1, keepdims=True))
                new_m.append(m_new)
                pr_hi = pr.astype(BF16)
                pr_lo = (pr - pr_hi.astype(F32)).astype(BF16)
                pvs.append(_dot(pr_hi, vb) + _dot(pr_lo, vb))
                alphas.append(alpha)
            acc = jnp.where(lo, alphas[0] * acc + pvs[0], alphas[1] * acc + pvs[1])
            return (tuple(new_m), tuple(new_l), acc)

        neg = jnp.full((tb, 1), _NEG, F32)
        zero = jnp.zeros((tb, 1), F32)
        init = ((neg, neg), (zero, zero), jnp.zeros((tb, LANES), F32))
        carry = lax.fori_loop(0, i, lambda j, c: block(j, c, False), init)
        ms, ls, acc = block(i, carry, True)
        o = acc / jnp.where(lo, ls[0], ls[1])
        o_ref[...] = o
        lse_ref[...] = jnp.where(lo, ms[0] + jnp.log(ls[0]), ms[1] + jnp.log(ls[1]))
        z = z_ref[...]
        y_ref[...] = (o * (z * _sigmoid(z))).astype(BF16)

    qspec = lambda c0: pl.BlockSpec((None, tb, LANES), lambda bi, p, i: (bi, i, c0 + p))
    kspec = lambda c0: pl.BlockSpec((None, s, LANES), lambda bi, p, i: (bi, 0, c0 + p))
    ospec = pl.BlockSpec((None, tb, LANES), lambda bi, p, i: (bi, i, p))
    return pl.pallas_call(
        body, name=name, grid=(b, N_HEADS // 2, nq),
        in_specs=[qspec(q0), kspec(k0), kspec(v0), qspec(z0),
                  pl.BlockSpec((None, None, nq, 8, tb), lambda bi, p, i: (bi, p, 0, 0, 0))],
        out_specs=[ospec, ospec, ospec],
        out_shape=[jax.ShapeDtypeStruct((b, s, D_MODEL), BF16)] + [jax.ShapeDtypeStruct((b, s, D_MODEL), F32)] * 2,
        compiler_params=_cp(("parallel", "parallel", "arbitrary")),
    )(proj3, proj3, proj3, proj3, cum_t)


_ST_LSE, _ST_DELTA = 0, 2


def _fox_prep(proj3, o3, lse3, dy3, *, name, tr=512):
    b, s, _ = proj3.shape
    z0 = _PAD_COLS["c_z"][0] // LANES

    def body(z_ref, o_ref, lse_ref, dy_ref, dz_ref, do_ref, st_ref):
        lane = lax.broadcasted_iota(jnp.int32, (tr, LANES), 1)
        lo = lane < HEAD_DIM
        z = z_ref[...]
        sz = _sigmoid(z)
        dy = dy_ref[...]
        o = o_ref[...]
        do = dy * (z * sz)
        dz_ref[...] = (dy * o * (sz * (1.0 + z * (1.0 - sz)))).astype(BF16)
        do_ref[...] = do
        doo = do.astype(BF16).astype(F32) * o
        lse = lse_ref[...]
        cols = (_col(lse, 0), _col(lse, HEAD_DIM),
                jnp.sum(jnp.where(lo, doo, 0.0), axis=1, keepdims=True),
                jnp.sum(jnp.where(lo, 0.0, doo), axis=1, keepdims=True))
        st = jnp.zeros((tr, LANES), F32)
        for k, cvec in enumerate(cols):
            st = jnp.where(lane == k, cvec, st)
        st_ref[...] = st

    ospec = pl.BlockSpec((None, tr, LANES), lambda bi, p, i: (bi, i, p))
    return pl.pallas_call(
        body, name=name, grid=(b, N_HEADS // 2, s // tr),
        in_specs=[pl.BlockSpec((None, tr, LANES), lambda bi, p, i: (bi, i, z0 + p)), ospec, ospec, ospec],
        out_specs=[ospec, ospec, pl.BlockSpec((None, None, tr, LANES), lambda bi, p, i: (bi, p, i, 0))],
        out_shape=[jax.ShapeDtypeStruct((b, s, D_MODEL), BF16), jax.ShapeDtypeStruct((b, s, D_MODEL), F32),
                   jax.ShapeDtypeStruct((b, N_HEADS // 2, s, LANES), F32)],
        compiler_params=_cp(("parallel", "parallel", "parallel")),
    )(proj3, o3, lse3, dy3)


def _fox_bwd(proj3, cum_t, do3, stats, *, name, tb):
    b, s, _ = proj3.shape
    nq = s // tb
    q0 = _PAD_COLS["c_q"][0] // LANES
    k0 = _PAD_COLS["c_k"][0] // LANES
    v0 = _PAD_COLS["c_v"][0] // LANES

    def body(q_ref, do_ref, st_ref, k_ref, v_ref, cumt_ref, dq_ref, dk_ref, dv_ref, cs_ref):
        j = pl.program_id(2)
        lo = lax.broadcasted_iota(jnp.int32, (tb, LANES), 1) < HEAD_DIM

        @pl.when(j == 0)
        def _():
            dq_ref[...] = jnp.zeros_like(dq_ref)

        kb = k_ref[...].astype(BF16)
        vb = v_ref[...].astype(BF16)
        ckv = cumt_ref[...]

        def block(i, carry, diagonal):
            qs = pl.ds(pl.multiple_of(i * tb, tb), tb)
            q = q_ref[qs, :] * _SCALE
            do = do_ref[qs, :]
            st = st_ref[qs, :]
            if diagonal:
                row = lax.broadcasted_iota(jnp.int32, (tb, tb), 0)
                col = lax.broadcasted_iota(jnp.int32, (tb, tb), 1)
                mask = row >= col
            dk, dv, cs = carry
            new_cs, dqs = [], []
            for hh in range(2):
                sel = lo if hh == 0 else jnp.logical_not(lo)
                qm = jnp.where(sel, q, 0.0).astype(BF16)
                dom = jnp.where(sel, do, 0.0).astype(BF16)
                sc = _dot_nt(qm, kb) - ckv[hh:hh + 1, :]
                if diagonal:
                    sc = jnp.where(mask, sc, _NEG)
                pr = jnp.exp(sc - _col(st, _ST_LSE + hh))
                ds = pr * (_dot_nt(dom, vb) - _col(st, _ST_DELTA + hh))
                dsb = ds.astype(BF16)
                dv = dv + _dot_tn(pr.astype(BF16), dom)
                dk = dk + _dot_tn(dsb, qm)
                new_cs.append(cs[hh] + jnp.sum(ds, axis=0, keepdims=True))
                dqs.append(_dot(dsb, kb))
            dq_ref[qs, :] += jnp.where(lo, dqs[0], dqs[1]) * _SCALE
            return (dk, dv, tuple(new_cs))

        zrow = jnp.zeros((1, tb), F32)
        init = (jnp.zeros((tb, LANES), F32), jnp.zeros((tb, LANES), F32), (zrow, zrow))
        carry = block(j, init, True)
        dk, dv, cs = lax.fori_loop(j + 1, nq, lambda i, c: block(i, c, False), carry)
        dk_ref[...] = dk.astype(BF16)
        dv_ref[...] = dv.astype(BF16)
        cs_ref[...] = jnp.zeros_like(cs_ref)
        cs_ref[0:1, :] = cs[0]
        cs_ref[1:2, :] = cs[1]

    full = lambda c0: pl.BlockSpec((None, s, LANES), lambda bi, p, j: (bi, 0, c0 + p))
    kspec = lambda c0: pl.BlockSpec((None, tb, LANES), lambda bi, p, j: (bi, j, c0 + p))
    ko = pl.BlockSpec((None, tb, LANES), lambda bi, p, j: (bi, j, p))
    ctspec = pl.BlockSpec((None, None, None, 8, tb), lambda bi, p, j: (bi, p, j, 0, 0))
    return pl.pallas_call(
        body, name=name, grid=(b, N_HEADS // 2, nq),
        in_specs=[full(q0), full(0), pl.BlockSpec((None, None, s, LANES), lambda bi, p, j: (bi, p, 0, 0)),
                  kspec(k0), kspec(v0), ctspec],
        out_specs=[full(0), ko, ko, ctspec],
        out_shape=[jax.ShapeDtypeStruct((b, s, D_MODEL), F32), jax.ShapeDtypeStruct((b, s, D_MODEL), BF16),
                   jax.ShapeDtypeStruct((b, s, D_MODEL), BF16),
                   jax.ShapeDtypeStruct((b, N_HEADS // 2, nq, 8, tb), F32)],
        compiler_params=_cp(("parallel", "parallel", "arbitrary")),
    )(proj3, do3, stats, proj3, proj3, cum_t)


def _rope(x, cos, sin_signed):
    w = x.shape[1]
    lane = lax.broadcasted_iota(jnp.int32, x.shape, 1)
    first = (lane % HEAD_DIM) < (HEAD_DIM // 2)
    rot = jnp.where(first, pltpu.roll(x, w - HEAD_DIM // 2, 1), pltpu.roll(x, HEAD_DIM // 2, 1))
    return x * cos + rot * sin_signed


_QB = 4
_QROWS = _QB * CHUNK


def _swa_keys(kc_ref, kp_ref, vc_ref, vp_ref, cq_ref, sq_ref, cp_ref, sp_ref):
    cq, sq, cpv, spv = cq_ref[...], sq_ref[...], cp_ref[...], sp_ref[...]
    kc = _rope(kc_ref[...], cq, sq).astype(BF16)
    kp = _rope(kp_ref[...], cpv, spv).astype(BF16)
    return cq, sq, cpv, spv, kc, kp, vc_ref[...].astype(BF16), vp_ref[...].astype(BF16)


def _swa_stack(pairs, lo):
    return jnp.concatenate([jnp.where(lo, pairs[0], 0.0), jnp.where(lo, 0.0, pairs[0]),
                            jnp.where(lo, pairs[1], 0.0), jnp.where(lo, 0.0, pairs[1])], axis=0).astype(BF16)


def _swa_mask4(prev_valid):
    r = lax.broadcasted_iota(jnp.int32, (4 * CHUNK, 2 * CHUNK), 0) & (CHUNK - 1)
    c = lax.broadcasted_iota(jnp.int32, (4 * CHUNK, 2 * CHUNK), 1)
    own = jnp.logical_and(c >= CHUNK, c - CHUNK <= r)
    before = jnp.logical_and(c < CHUNK, c > r)
    if prev_valid is True:
        return jnp.logical_or(own, before)
    return jnp.logical_or(own, jnp.logical_and(before, prev_valid))


def _swa_sink4(skv):
    return jnp.concatenate([jnp.broadcast_to(_col(skv, j), (CHUNK, 1)) for j in range(4)], axis=0)


def _swa_specs(order, q0, z0):
    def spec(shape, fn):
        return pl.BlockSpec(shape, lambda *ids: fn(*order(*ids)))

    prev = lambda i: jnp.maximum(_QB * i - 1, 0)
    return dict(
        q=spec((None, _QROWS, 256), lambda bi, g, i: (bi, i, q0 + g)),
        z=spec((None, _QROWS, 256), lambda bi, g, i: (bi, i, z0 + g)),
        blk=spec((None, _QROWS, 256), lambda bi, g, i: (bi, i, g)),
        kcur=spec((None, _QROWS, LANES), lambda bi, g, i: (bi, i, g)),
        kprev=spec((None, CHUNK, LANES), lambda bi, g, i: (bi, prev(i), g)),
        kstep=spec((None, CHUNK, LANES), lambda bi, g, i: (bi, i, g)),
        tcur=spec((_QROWS, LANES), lambda bi, g, i: (i, 0)),
        tprev=spec((CHUNK, LANES), lambda bi, g, i: (prev(i), 0)),
        sk=spec((None, 1, LANES), lambda bi, g, i: (g, 0, 0)))


def _swa_fwd(proj3, k2, v2, cos, sin, sinks, *, name):
    b, s, _ = proj3.shape
    q0 = _PAD_COLS["b_q"][0] // 256
    z0 = _PAD_COLS["b_z"][0] // 256

    def body(q_ref, z_ref, kc_ref, kp_ref, vc_ref, vp_ref, cq_ref, sq_ref, cp_ref, sp_ref, sk_ref,
             y_ref, o_ref, lse_ref):
        i = pl.program_id(2)
        cq_all, sq_all, _, _, kc_all, kp0, vc_all, vp0 = _swa_keys(
            kc_ref, kp_ref, vc_ref, vp_ref, cq_ref, sq_ref, cp_ref, sp_ref)
        lo = lax.broadcasted_iota(jnp.int32, (CHUNK, LANES), 1) < HEAD_DIM
        sink4 = _swa_sink4(sk_ref[...])
        for u in range(_QB):
            rs = slice(CHUNK * u, CHUNK * (u + 1))
            ps = slice(CHUNK * (u - 1), CHUNK * u)
            cq, sq = cq_all[rs], sq_all[rs]
            kp, vp = (kp0, vp0) if u == 0 else (kc_all[ps], vc_all[ps])
            kk = jnp.concatenate([kp, kc_all[rs]], axis=0)
            vv = jnp.concatenate([vp, vc_all[rs]], axis=0)
            q4 = _swa_stack([_rope(q_ref[rs, LANES * pp:LANES * (pp + 1)], cq, sq) * _SCALE for pp in range(2)], lo)
            sc = jnp.where(_swa_mask4(True if u > 0 else i > 0), _dot_nt(q4, kk), _NEG)
            m = jnp.maximum(jnp.max(sc, axis=1, keepdims=True), sink4)
            pr = jnp.exp(sc - m)
            l = jnp.sum(pr, axis=1, keepdims=True) + jnp.exp(sink4 - m)
            o4 = _dot(pr.astype(BF16), vv) / l
            lse4 = m + jnp.log(l)
            for pp in range(2):
                ls = slice(LANES * pp, LANES * (pp + 1))
                h0 = slice(2 * CHUNK * pp, 2 * CHUNK * pp + CHUNK)
                h1 = slice(2 * CHUNK * pp + CHUNK, 2 * CHUNK * (pp + 1))
                o = jnp.where(lo, o4[h0], o4[h1])
                z = z_ref[rs, ls]
                o_ref[rs, ls] = o
                lse_ref[rs, ls] = jnp.where(lo, lse4[h0], lse4[h1])
                y_ref[rs, ls] = (o * (z * _sigmoid(z))).astype(BF16)

    sp = _swa_specs(lambda bi, g, i: (bi, g, i), q0, z0)
    return pl.pallas_call(
        body, name=name, grid=(b, N_GROUPS, s // _QROWS),
        in_specs=[sp["q"], sp["z"], sp["kcur"], sp["kprev"], sp["kcur"], sp["kprev"],
                  sp["tcur"], sp["tcur"], sp["tprev"], sp["tprev"], sp["sk"]],
        out_specs=[sp["blk"], sp["blk"], sp["blk"]],
        out_shape=[jax.ShapeDtypeStruct((b, s, D_MODEL), BF16)] + [jax.ShapeDtypeStruct((b, s, D_MODEL), F32)] * 2,
        compiler_params=_cp(("parallel", "parallel", "parallel")),
    )(proj3, proj3, k2, k2, v2, v2, cos, sin, cos, sin, sinks)


def _swa_bwd(proj3, k2, v2, cos, sin, sinks, o3, lse3, dy3, *, name):
    b, s, _ = proj3.shape
    q0 = _PAD_COLS["b_q"][0] // 256
    z0 = _PAD_COLS["b_z"][0] // 256

    def body(q_ref, z_ref, kc_ref, kp_ref, vc_ref, vp_ref, cq_ref, sq_ref, cp_ref, sp_ref, sk_ref,
             o_ref, lse_ref, dy_ref, dq_ref, dz_ref, dkc_ref, dkp_ref, dvc_ref, dvp_ref, dsk_ref):
        i = pl.program_id(2)
        first = jnp.logical_and(pl.program_id(1) == 0, i == 0)

        @pl.when(first)
        def _():
            dsk_ref[...] = jnp.zeros_like(dsk_ref)

        cq_all, sq_all, cpv, spv, kc_all, kp0, vc_all, vp0 = _swa_keys(
            kc_ref, kp_ref, vc_ref, vp_ref, cq_ref, sq_ref, cp_ref, sp_ref)
        lo = lax.broadcasted_iota(jnp.int32, (CHUNK, LANES), 1) < HEAD_DIM
        lane1 = lax.broadcasted_iota(jnp.int32, (1, LANES), 1)
        sink4 = _swa_sink4(sk_ref[...])
        zero = jnp.zeros((CHUNK, LANES), F32)
        dks = [zero] * (_QB + 1)
        dvs = [zero] * (_QB + 1)
        dsk_row = jnp.zeros((1, LANES), F32)
        for u in range(_QB):
            rs = slice(CHUNK * u, CHUNK * (u + 1))
            ps = slice(CHUNK * (u - 1), CHUNK * u)
            cq, sq = cq_all[rs], sq_all[rs]
            kp, vp = (kp0, vp0) if u == 0 else (kc_all[ps], vc_all[ps])
            kk = jnp.concatenate([kp, kc_all[rs]], axis=0)
            vv = jnp.concatenate([vp, vc_all[rs]], axis=0)
            q4 = _swa_stack([_rope(q_ref[rs, LANES * pp:LANES * (pp + 1)], cq, sq) * _SCALE for pp in range(2)], lo)
            dos, lses = [], []
            for pp in range(2):
                ls = slice(LANES * pp, LANES * (pp + 1))
                z = z_ref[rs, ls]
                sz = _sigmoid(z)
                dy = dy_ref[rs, ls]
                dos.append(dy * (z * sz))
                dz_ref[rs, ls] = (dy * o_ref[rs, ls] * (sz * (1.0 + z * (1.0 - sz)))).astype(BF16)
                lse = lse_ref[rs, ls]
                lses += [_col(lse, 0), _col(lse, HEAD_DIM)]
            do4 = _swa_stack(dos, lo)
            lse4 = jnp.concatenate(lses, axis=0)
            pr = jnp.exp(jnp.where(_swa_mask4(True if u > 0 else i > 0), _dot_nt(q4, kk), _NEG) - lse4)
            dp = _dot_nt(do4, vv)
            dl = jnp.sum(pr * dp, axis=1, keepdims=True)
            ds = (pr * (dp - dl)).astype(BF16)
            dsink = -jnp.exp(sink4 - lse4) * dl
            for j in range(4):
                dsk_row = dsk_row + jnp.where(
                    lane1 == j, jnp.sum(dsink[CHUNK * j:CHUNK * (j + 1)], axis=0, keepdims=True), 0.0)
            dq4 = _dot(ds, kk)
            dkk = _dot_tn(ds, q4)
            dvv = _dot_tn(pr.astype(BF16), do4)
            dks[u], dks[u + 1] = dks[u] + dkk[:CHUNK], dks[u + 1] + dkk[CHUNK:]
            dvs[u], dvs[u + 1] = dvs[u] + dvv[:CHUNK], dvs[u + 1] + dvv[CHUNK:]
            for pp in range(2):
                h0 = slice(2 * CHUNK * pp, 2 * CHUNK * pp + CHUNK)
                h1 = slice(2 * CHUNK * pp + CHUNK, 2 * CHUNK * (pp + 1))
                dq_ref[rs, LANES * pp:LANES * (pp + 1)] = _rope(
                    jnp.where(lo, dq4[h0], dq4[h1]) * _SCALE, cq, -sq).astype(BF16)
        fold = lambda v: v + pltpu.roll(v, HEAD_DIM, 1)
        dkp_ref[...] = fold(_rope(dks[0], cpv, -spv))
        dvp_ref[...] = fold(dvs[0])
        for u in range(_QB):
            rs = slice(CHUNK * u, CHUNK * (u + 1))
            dkc_ref[rs, :] = fold(_rope(dks[u + 1], cq_all[rs], -sq_all[rs]))
            dvc_ref[rs, :] = fold(dvs[u + 1])
        dsk_ref[...] += dsk_row

    sp = _swa_specs(lambda g, bi, i: (bi, g, i), q0, z0)
    kv_shape = jax.ShapeDtypeStruct((b, s, 512), F32)
    kvp_shape = jax.ShapeDtypeStruct((b, s // _QB, 512), F32)
    return pl.pallas_call(
        body, name=name, grid=(N_GROUPS, b, s // _QROWS),
        in_specs=[sp["q"], sp["z"], sp["kcur"], sp["kprev"], sp["kcur"], sp["kprev"],
                  sp["tcur"], sp["tcur"], sp["tprev"], sp["tprev"], sp["sk"], sp["blk"], sp["blk"], sp["blk"]],
        out_specs=[sp["blk"], sp["blk"], sp["kcur"], sp["kstep"], sp["kcur"], sp["kstep"], sp["sk"]],
        out_shape=[jax.ShapeDtypeStruct((b, s, D_MODEL), BF16), jax.ShapeDtypeStruct((b, s, D_MODEL), BF16),
                   kv_shape, kvp_shape, kv_shape, kvp_shape, jax.ShapeDtypeStruct((N_GROUPS, 1, LANES), F32)],
        compiler_params=_cp(("arbitrary", "arbitrary", "arbitrary")),
    )(proj3, proj3, k2, k2, v2, v2, cos, sin, cos, sin, sinks, o3, lse3, dy3)


def _merge_fwd(proj, br, gb, *, name, tm=256):
    t = proj.shape[0]
    g0 = _PAD_COLS["gates"][0] // D_MODEL

    def body(g_ref, a_ref, b_ref, c_ref, gb_ref, o_ref):
        acc = None
        for i, r in enumerate((a_ref, b_ref, c_ref)):
            gate = _sigmoid(g_ref[:, D_MODEL * i:D_MODEL * (i + 1)] + gb_ref[i:i + 1, :])
            term = gate * r[...]
            acc = term if acc is None else acc + term
        o_ref[...] = acc.astype(BF16)

    row = pl.BlockSpec((tm, D_MODEL), lambda i: (i, 0))
    return pl.pallas_call(
        body, name=name, grid=(t // tm,),
        in_specs=[pl.BlockSpec((tm, 3 * D_MODEL), lambda i: (i, g0)), row, row, row,
                  pl.BlockSpec((3, D_MODEL), lambda i: (0, 0))],
        out_specs=row, out_shape=jax.ShapeDtypeStruct((t, D_MODEL), BF16),
        compiler_params=_cp(("parallel",)),
    )(proj, br[0], br[1], br[2], gb)


def _merge_bwd(proj, br, gb, dm, *, name, tm=256):
    t = proj.shape[0]
    g0 = _PAD_COLS["gates"][0] // D_MODEL

    def body(g_ref, a_ref, b_ref, c_ref, gb_ref, dm_ref, da_ref, db_ref, dc_ref, dg_ref, dgb_ref):
        @pl.when(pl.program_id(0) == 0)
        def _():
            dgb_ref[...] = jnp.zeros_like(dgb_ref)

        dmv = dm_ref[...]
        for i, (r, dr) in enumerate(((a_ref, da_ref), (b_ref, db_ref), (c_ref, dc_ref))):
            gate = _sigmoid(g_ref[:, D_MODEL * i:D_MODEL * (i + 1)] + gb_ref[i:i + 1, :])
            dr[...] = (dmv * gate).astype(BF16)
            dg = dmv * r[...] * gate * (1.0 - gate)
            dg_ref[:, D_MODEL * i:D_MODEL * (i + 1)] = dg.astype(BF16)
            dgb_ref[i:i + 1, :] += jnp.sum(dg, axis=0, keepdims=True)

    row = pl.BlockSpec((tm, D_MODEL), lambda i: (i, 0))
    rowb = jax.ShapeDtypeStruct((t, D_MODEL), BF16)
    return pl.pallas_call(
        body, name=name, grid=(t // tm,),
        in_specs=[pl.BlockSpec((tm, 3 * D_MODEL), lambda i: (i, g0)), row, row, row,
                  pl.BlockSpec((3, D_MODEL), lambda i: (0, 0)), row],
        out_specs=[row, row, row, pl.BlockSpec((tm, 3 * D_MODEL), lambda i: (i, 0)),
                   pl.BlockSpec((8, D_MODEL), lambda i: (0, 0))],
        out_shape=[rowb, rowb, rowb, jax.ShapeDtypeStruct((t, 3 * D_MODEL), BF16),
                   jax.ShapeDtypeStruct((8, D_MODEL), F32)],
        compiler_params=_cp(("arbitrary",)),
    )(proj, br[0], br[1], br[2], gb, dm)


def _rope_tables(s):
    pos = jnp.arange(s, dtype=F32)
    inv_freq = ROPE_THETA ** (-jnp.arange(0, HEAD_DIM, 2, dtype=F32) / HEAD_DIM)
    ang = pos[:, None] * inv_freq[None, :]
    cos, sin = jnp.cos(ang), jnp.sin(ang)
    return jnp.tile(cos, (1, 4)), jnp.tile(jnp.concatenate([-sin, sin], axis=1), (1, 2))


def _dup_kv(proj3, name):
    b, s, _ = proj3.shape
    p0, sz = _PAD_COLS[name]
    kv = proj3[:, :, p0:p0 + sz].reshape(b, s, N_GROUPS, 1, HEAD_DIM)
    return jnp.broadcast_to(kv, (b, s, N_GROUPS, 2, HEAD_DIM)).reshape(b, s, 512)


def _pair_rows(cum, tb):
    b, s, _ = cum.shape
    t = jnp.transpose(cum[:, :, :N_HEADS], (0, 2, 1)).reshape(b, N_HEADS // 2, 2, s // tb, tb)
    return jnp.pad(jnp.transpose(t, (0, 1, 3, 2, 4)), ((0, 0), (0, 0), (0, 0), (0, 6), (0, 0)))


def _layer_params(wl):
    return dict(
        dtb=_group_lanes(wl["dt_bias"]), alog=_group_lanes(wl["a_log"]), dsk=_group_lanes(wl["d_skip"]),
        nw=wl["ssm_norm_w"].reshape(N_GROUPS, 1, 256), sinks=_group_lanes(wl["sinks"]),
        fb=jnp.pad(wl["f_bias"], (0, LANES - N_HEADS)).reshape(1, LANES))


def _layer_fwd(x, wl, tabs, bsz, li, tb):
    t = x.shape[0]
    s = t // bsz
    cos, sin = tabs
    lp = _layer_params(wl)
    n = lambda k: f"l{li}_{k}"
    h, h_t = _rms_fwd(x, wl["norm_w"], name=n("rms_fwd"))
    proj = _mm(h, wl["w_in"], tm=1024, tn=1536, tk=1024, name=n("mm_proj"))
    proj3 = proj.reshape(bsz, s, N_PAD)
    xact3 = _conv_fwd(proj3, wl["conv_w"], wl["conv_b"], name=n("conv_fwd"))
    ya3, ypre3, hst = _ssd_fwd(proj3, xact3, lp["dtb"], lp["alog"], lp["dsk"], lp["nw"], name=n("ssd_fwd"))
    k2, v2 = _dup_kv(proj3, "b_k"), _dup_kv(proj3, "b_v")
    yb3, ob3, lseb3 = _swa_fwd(proj3, k2, v2, cos, sin, lp["sinks"], name=n("swa_fwd"))
    cum = _fgate_fwd(proj3, lp["fb"], name=n("fgate_fwd"))
    cum_t = _pair_rows(cum, tb)
    yc3, oc3, lsec3 = _fox_fwd(proj3, cum_t, name=n("fox_fwd"), tb=tb)
    ys = [v.reshape(t, D_MODEL) for v in (ya3, yb3, yc3)]
    br = [_mm(ys[i], wl["w_proj"][i], tm=1024, tn=1024, tk=1024, name=n(f"mm_br{i}")) for i in range(3)]
    merged = _merge_fwd(proj, br, wl["gate_bias"], name=n("merge_fwd"))
    x_new = _mm(merged, wl["w_out"], tm=1024, tn=1024, tk=1024, add=x, name=n("mm_out"))
    saved = dict(x=x, h_t=h_t, proj=proj, xact3=xact3, ypre3=ypre3, hst=hst, k2=k2, v2=v2, ob3=ob3, lseb3=lseb3,
                 cum_t=cum_t, oc3=oc3, lsec3=lsec3, ys=ys, br=br, merged=merged, lp=lp)
    return x_new, saved


def _layer_bwd(dx, wl, sv, tabs, bsz, li, tb):
    t = dx.shape[0]
    s = t // bsz
    cos, sin = tabs
    lp = sv["lp"]
    n = lambda k: f"l{li}_{k}"
    proj = sv["proj"]
    proj3 = proj.reshape(bsz, s, N_PAD)
    g = {}
    dmerged = _mm(dx, wl["w_out"], tb=True, tm=1024, tn=1024, tk=1024, name=n("mm_dmerged"))
    g["w_out"] = _mm(sv["merged"], dx, ta=True, tm=1024, tn=1024, tk=512, name=n("mm_dwout"))
    dbr0, dbr1, dbr2, dgates, dgb = _merge_bwd(proj, sv["br"], wl["gate_bias"], dmerged, name=n("merge_bwd"))
    g["gate_bias"] = dgb[:3]
    dbr = (dbr0, dbr1, dbr2)
    dys = [_mm(dbr[i], wl["w_proj"][i], tb=True, tm=1024, tn=1024, tk=1024, name=n(f"mm_dy{i}"))
           for i in range(3)]
    g["w_proj"] = jnp.stack([_mm(sv["ys"][i], dbr[i], ta=True, tm=1024, tn=1024, tk=512, name=n(f"mm_dwproj{i}"))
                             for i in range(3)])
    dy3 = [v.reshape(bsz, s, D_MODEL) for v in dys]

    (dact, daz, dadt, ddtb, dalog, ddsk, dnw) = _ssd_bwd(
        proj3, sv["xact3"], lp["dtb"], lp["alog"], lp["dsk"], lp["nw"], sv["ypre3"], sv["hst"], dy3[0],
        name=n("ssd_bwd"))
    g["dt_bias"], g["a_log"], g["d_skip"] = _ungroup_lanes(ddtb), _ungroup_lanes(dalog), _ungroup_lanes(ddsk)
    g["ssm_norm_w"] = dnw.reshape(D_MODEL)
    dxbc, dwb = _conv_bwd(proj3, wl["conv_w"], wl["conv_b"], dact, name=n("conv_bwd"))
    g["conv_w"], g["conv_b"] = dwb[:CONV_WIDTH], dwb[CONV_WIDTH]

    dbq, dbz, dkc, dkp, dvc, dvp, dsk = _swa_bwd(proj3, sv["k2"], sv["v2"], cos, sin, lp["sinks"], sv["ob3"],
                                                 sv["lseb3"], dy3[1], name=n("swa_bwd"))
    g["sinks"] = _ungroup_lanes(dsk)

    def fold(cur, prv):
        p4 = prv.reshape(bsz, s // _QROWS, 1, CHUNK, 512)
        tail = jnp.concatenate([p4[:, 1:], jnp.zeros_like(p4[:, :1])], axis=1)
        shifted = jnp.concatenate([jnp.zeros((bsz, s // _QROWS, _QB - 1, CHUNK, 512), F32), tail], axis=2)
        tot = cur + shifted.reshape(bsz, s, 512)
        return tot.reshape(bsz, s, N_GROUPS, 2, HEAD_DIM)[:, :, :, 0].reshape(bsz, s, 256)

    dbk, dbv = fold(dkc, dkp), fold(dvc, dvp)

    dcz, do3, stats = _fox_prep(proj3, sv["oc3"], sv["lsec3"], dy3[2], name=n("fox_prep"))
    dcq, dck, dcv, csum = _fox_bwd(proj3, sv["cum_t"], do3, stats, name=n("fox_bwd"), tb=tb)
    csum = jnp.transpose(csum[:, :, :, :2], (0, 1, 3, 2, 4)).reshape(bsz, N_HEADS, s)
    dcum = -jnp.transpose(csum, (0, 2, 1))
    dcum = jnp.pad(dcum, ((0, 0), (0, 0), (0, LANES - N_HEADS)))
    dcf, dfb = _fgate_bwd(proj3, lp["fb"], dcum, name=n("fgate_bwd"))
    g["f_bias"] = dfb[0, :N_HEADS]

    parts = {"gates": dgates.reshape(bsz, s, 3 * D_MODEL), "xbc": dxbc, "a_z": daz, "b_q": dbq, "b_z": dbz,
             "c_q": dcq, "c_k": dck, "c_v": dcv, "c_z": dcz, "b_k": dbk, "b_v": dbv, "a_dt": dadt, "c_f": dcf}
    dproj = jnp.concatenate([parts[name].astype(BF16) for name, _ in _PAD_ORDER]
                            + [jnp.zeros((bsz, s, N_PAD - N_USED), BF16)], axis=2).reshape(t, N_PAD)
    dh = _mm(dproj, wl["w_in"], tb=True, tm=1024, tn=1024, tk=1536, name=n("mm_dh"))
    g["w_in"] = _unpad_w_in(_mm(sv["h_t"], dproj, tm=1024, tn=768, tk=2048, name=n("mm_dwin")))
    dx_in, dnorm = _rms_bwd(sv["x"], wl["norm_w"], dh, dx, name=n("rms_bwd"))
    g["norm_w"] = dnorm[0]
    return dx_in, g


def _local_step(x, target, wls, final_norm_w, tb=512):
    bsz, s, d = x.shape
    t = bsz * s
    tabs = _rope_tables(s)
    xc = x.reshape(t, d)
    saved = []
    for li, wl in enumerate(wls):
        xc, sv = _layer_fwd(xc, wl, tabs, bsz, li, tb)
        saved.append(sv)
    loss, dx, dfw = _final_loss(xc, final_norm_w, target.reshape(t, d), name="final_loss")
    grads = [None] * len(wls)
    for li in reversed(range(len(wls))):
        dx, grads[li] = _layer_bwd(dx, wls[li], saved[li], tabs, bsz, li, tb)
    return loss[0, 0], dx.reshape(bsz, s, d), grads, dfw[0]


_HBM = pl.BlockSpec(memory_space=pltpu.HBM)


def _chip_peers(x, y):
    return [(1 - x, y), (x, 1 - y), (1 - x, 1 - y)]


def _gather_weights(arrs, *, name):
    n = len(arrs)

    def body(*refs):
        ins, outs = refs[:n], refs[n:2 * n]
        ici_send, ici_recv, d2d_send, d2d_recv = refs[2 * n:]
        x, y, c = lax.axis_index("x"), lax.axis_index("y"), lax.axis_index("c")
        me = 2 * x + y
        peers = _chip_peers(x, y)
        sib = (x, y, 1 - c)
        sends, fwds = [], []
        for a in range(n):
            for k, (px, py) in enumerate(peers):
                cp = pltpu.make_async_remote_copy(
                    src_ref=ins[a].at[c], dst_ref=outs[a].at[me, c], send_sem=ici_send.at[a, k],
                    recv_sem=ici_recv.at[a, k], device_id=(px, py, c), device_id_type=MESH)
                cp.start()
                sends.append(cp)
        for a in range(n):
            for k, (px, py) in enumerate(peers):
                slot = 2 * px + py
                pltpu.make_async_remote_copy(
                    src_ref=ins[a].at[c], dst_ref=outs[a].at[slot, c], send_sem=ici_send.at[a, k],
                    recv_sem=ici_recv.at[a, k], device_id=(px, py, c), device_id_type=MESH).wait_recv()
                fw = pltpu.make_async_remote_copy(
                    src_ref=outs[a].at[slot, c], dst_ref=outs[a].at[slot, c], send_sem=d2d_send.at[a, k],
                    recv_sem=d2d_recv.at[a, k], device_id=sib, device_id_type=MESH)
                fw.start()
                fwds.append(fw)
        for a in range(n):
            for k, (px, py) in enumerate(peers):
                slot = 2 * px + py
                pltpu.make_async_remote_copy(
                    src_ref=outs[a].at[slot, 1 - c], dst_ref=outs[a].at[slot, 1 - c], send_sem=d2d_send.at[a, k],
                    recv_sem=d2d_recv.at[a, k], device_id=sib, device_id_type=MESH).wait_recv()
        for cp in sends + fwds:
            cp.wait_send()

    out_shape = [jax.ShapeDtypeStruct((N_CHIPS,) + a.shape, a.dtype) for a in arrs]
    return pl.pallas_call(
        body, name=name, out_shape=out_shape, in_specs=[_HBM] * n, out_specs=[_HBM] * n,
        scratch_shapes=[pltpu.SemaphoreType.DMA((n, 3)), pltpu.SemaphoreType.DMA((n, 3)),
                        pltpu.SemaphoreType.DMA((n, 3)), pltpu.SemaphoreType.DMA((n, 3))],
    )(*arrs)


def _pair_exchange(arrs, *, name):
    n = len(arrs)

    def body(*refs):
        ins, outs = refs[:n], refs[n:2 * n]
        send, recv = refs[2 * n:]
        x, y, c = lax.axis_index("x"), lax.axis_index("y"), lax.axis_index("c")
        sib = (x, y, 1 - c)
        cps = []
        for a in range(n):
            for k in range(N_CHIPS):
                cp = pltpu.make_async_remote_copy(
                    src_ref=ins[a].at[k, 1 - c], dst_ref=outs[a].at[k], send_sem=send.at[a, k],
                    recv_sem=recv.at[a, k], device_id=sib, device_id_type=MESH)
                cp.start()
                cps.append(cp)
        for cp in cps:
            cp.wait()

    out_shape = [jax.ShapeDtypeStruct((N_CHIPS,) + a.shape[2:], a.dtype) for a in arrs]
    return pl.pallas_call(
        body, name=name, out_shape=out_shape, in_specs=[_HBM] * n, out_specs=[_HBM] * n,
        scratch_shapes=[pltpu.SemaphoreType.DMA((n, N_CHIPS)), pltpu.SemaphoreType.DMA((n, N_CHIPS))],
    )(*arrs)


def _chip_exchange(arrs, *, name):
    n = len(arrs)

    def body(*refs):
        ins, outs = refs[:n], refs[n:2 * n]
        send, recv = refs[2 * n:]
        x, y, c = lax.axis_index("x"), lax.axis_index("y"), lax.axis_index("c")
        me = 2 * x + y
        peers = _chip_peers(x, y)
        cps = []
        for a in range(n):
            for k, (px, py) in enumerate(peers):
                cp = pltpu.make_async_remote_copy(
                    src_ref=ins[a].at[2 * px + py], dst_ref=outs[a].at[me], send_sem=send.at[a, k],
                    recv_sem=recv.at[a, k], device_id=(px, py, c), device_id_type=MESH)
                cp.start()
                cps.append(cp)
        for a in range(n):
            for k, (px, py) in enumerate(peers):
                pltpu.make_async_remote_copy(
                    src_ref=ins[a].at[2 * px + py], dst_ref=outs[a].at[2 * px + py], send_sem=send.at[a, k],
                    recv_sem=recv.at[a, k], device_id=(px, py, c), device_id_type=MESH).wait_recv()
        for cp in cps:
            cp.wait_send()

    out_shape = [jax.ShapeDtypeStruct(a.shape, a.dtype) for a in arrs]
    return pl.pallas_call(
        body, name=name, out_shape=out_shape, in_specs=[_HBM] * n, out_specs=[_HBM] * n,
        scratch_shapes=[pltpu.SemaphoreType.DMA((n, 3)), pltpu.SemaphoreType.DMA((n, 3))],
    )(*arrs)


def _pair_share(arrs, *, name):
    n = len(arrs)

    def body(*refs):
        ins, outs = refs[:n], refs[n:2 * n]
        send, recv = refs[2 * n:]
        x, y, c = lax.axis_index("x"), lax.axis_index("y"), lax.axis_index("c")
        sib = (x, y, 1 - c)
        cps = []
        for a in range(n):
            cp = pltpu.make_async_remote_copy(
                src_ref=ins[a], dst_ref=outs[a], send_sem=send.at[a], recv_sem=recv.at[a],
                device_id=sib, device_id_type=MESH)
            cp.start()
            cps.append(cp)
        for cp in cps:
            cp.wait()

    out_shape = [jax.ShapeDtypeStruct(a.shape, a.dtype) for a in arrs]
    return pl.pallas_call(
        body, name=name, out_shape=out_shape, in_specs=[_HBM] * n, out_specs=[_HBM] * n,
        scratch_shapes=[pltpu.SemaphoreType.DMA((n,)), pltpu.SemaphoreType.DMA((n,))],
    )(*arrs)


def _allreduce_small(buf, *, name):
    r = buf.shape[0]

    def body(in_ref, out_ref, land, send, recv):
        x, y, c = lax.axis_index("x"), lax.axis_index("y"), lax.axis_index("c")
        me = 4 * x + 2 * y + c
        land[me] = in_ref[...]
        cps = []
        for k in range(1, N_DEV):
            px, py, pc = x ^ ((k >> 2) & 1), y ^ ((k >> 1) & 1), c ^ (k & 1)
            cp = pltpu.make_async_remote_copy(
                src_ref=in_ref, dst_ref=land.at[me], send_sem=send.at[k - 1], recv_sem=recv.at[k - 1],
                device_id=(px, py, pc), device_id_type=MESH)
            cp.start()
            cps.append(cp)
        for k in range(1, N_DEV):
            px, py, pc = x ^ ((k >> 2) & 1), y ^ ((k >> 1) & 1), c ^ (k & 1)
            pltpu.make_async_remote_copy(
                src_ref=in_ref, dst_ref=land.at[4 * px + 2 * py + pc], send_sem=send.at[k - 1],
                recv_sem=recv.at[k - 1], device_id=(px, py, pc), device_id_type=MESH).wait_recv()
        for cp in cps:
            cp.wait_send()
        acc = land[0]
        for k in range(1, N_DEV):
            acc = acc + land[k]
        out_ref[...] = acc

    vm = pl.BlockSpec(memory_space=pltpu.VMEM)
    return pl.pallas_call(
        body, name=name, out_shape=jax.ShapeDtypeStruct((r, LANES), F32), in_specs=[vm], out_specs=vm,
        scratch_shapes=[pltpu.VMEM((N_DEV, r, LANES), F32), pltpu.SemaphoreType.DMA((N_DEV - 1,)),
                        pltpu.SemaphoreType.DMA((N_DEV - 1,))],
    )(buf)


def _rows2d(a):
    return a.reshape(-1, a.shape[-1])


def _row_tile(rows, cols, n_arrays, budget=20 * 1024 * 1024):
    best = 8 if rows % 8 == 0 else rows
    tr = 8
    while tr <= rows:
        if rows % tr == 0 and tr * cols * 4 * n_arrays * 2 <= budget:
            best = tr
        tr *= 2
    return best


def _add_slot_layer(full, other, *, name):
    _, _, r, cdim = full.shape
    tr = _row_tile(r, cdim, 4)

    def body(c_ref, a_ref, b_ref, o_ref, ob_ref):
        sm = a_ref[...] + b_ref[...]
        o_ref[...] = sm
        ob_ref[...] = sm.astype(BF16)

    c = lax.axis_index("c").astype(jnp.int32).reshape(1)
    blk = pl.BlockSpec((None, tr, cdim), lambda k, i, c_ref: (k, i, 0))
    return pl.pallas_call(
        body, name=name,
        grid_spec=pltpu.PrefetchScalarGridSpec(
            num_scalar_prefetch=1, grid=(N_CHIPS, r // tr),
            in_specs=[pl.BlockSpec((None, None, tr, cdim), lambda k, i, c_ref: (k, c_ref[0], i, 0)), blk],
            out_specs=[blk, blk]),
        out_shape=[jax.ShapeDtypeStruct((N_CHIPS, r, cdim), F32), jax.ShapeDtypeStruct((N_CHIPS, r, cdim), BF16)],
        compiler_params=_cp(("parallel", "parallel")),
    )(c, full, other)


def _sum_slots(parts, pair, *, name):
    _, r, cdim = parts.shape
    tr = _row_tile(r, cdim, 5)

    def body(me_ref, p_ref, own_ref, o_ref):
        me = me_ref[0]
        acc = None
        for k in range(N_CHIPS):
            term = jnp.where(me == k, own_ref[...], p_ref[k].astype(F32))
            acc = term if acc is None else acc + term
        o_ref[...] = acc

    me = (2 * lax.axis_index("x") + lax.axis_index("y")).astype(jnp.int32).reshape(1)
    return pl.pallas_call(
        body, name=name,
        grid_spec=pltpu.PrefetchScalarGridSpec(
            num_scalar_prefetch=1, grid=(r // tr,),
            in_specs=[pl.BlockSpec((N_CHIPS, tr, cdim), lambda i, me_ref: (0, i, 0)),
                      pl.BlockSpec((None, tr, cdim), lambda i, me_ref: (me_ref[0], i, 0))],
            out_specs=pl.BlockSpec((tr, cdim), lambda i, me_ref: (i, 0))),
        out_shape=jax.ShapeDtypeStruct((r, cdim), F32),
        compiler_params=_cp(("parallel",)),
    )(me, parts, pair)


def _adamw(w, g, m, v, *, name):
    r, cdim = w.shape
    tr = _row_tile(r, cdim, 7)
    c1 = 1.0 - ADAM_B1 ** ADAM_STEP
    c2 = 1.0 - ADAM_B2 ** ADAM_STEP

    def body(w_ref, g_ref, m_ref, v_ref, d_ref, nm_ref, nv_ref):
        gv = g_ref[...]
        mn = ADAM_B1 * m_ref[...] + (1.0 - ADAM_B1) * gv
        vn = ADAM_B2 * v_ref[...] + (1.0 - ADAM_B2) * (gv * gv)
        nm_ref[...] = mn
        nv_ref[...] = vn
        d_ref[...] = -ADAM_LR * ((mn / c1) / (jnp.sqrt(vn / c2) + ADAM_EPS) + ADAM_WD * w_ref[...])

    blk = pl.BlockSpec((tr, cdim), lambda i: (i, 0))
    sh = jax.ShapeDtypeStruct((r, cdim), F32)
    return pl.pallas_call(
        body, name=name, grid=(r // tr,), in_specs=[blk] * 4, out_specs=[blk] * 3, out_shape=[sh] * 3,
        compiler_params=_cp(("parallel",)),
    )(w, g, m, v)


_SMALL = ("norm_w", "conv_b", "dt_bias", "a_log", "d_skip", "ssm_norm_w", "sinks", "f_bias", "final_norm_w",
          "conv_w", "gate_bias")


def _pack(vals):
    flat = jnp.concatenate([v.reshape(-1) for v in vals])
    rows = -(-flat.shape[0] // LANES)
    rows = -(-rows // 8) * 8
    return jnp.pad(flat, (0, rows * LANES - flat.shape[0])).reshape(rows, LANES)


def _unpack(buf, shapes):
    flat = buf.reshape(-1)
    out, off = [], 0
    for sh in shapes:
        sz = int(np.prod(sh))
        out.append(flat[off:off + sz].reshape(sh))
        off += sz
    return out


def kernel(x, norm_w, w_in, conv_w, conv_b, dt_bias, a_log, d_skip, ssm_norm_w, sinks, f_bias, gate_bias, w_proj, w_out, final_norm_w, loss_target, m_norm_w, m_w_in, m_conv_w, m_conv_b, m_dt_bias, m_a_log, m_d_skip, m_ssm_norm_w, m_sinks, m_f_bias, m_gate_bias, m_w_proj, m_w_out, m_final_norm_w, v_norm_w, v_w_in, v_conv_w, v_conv_b, v_dt_bias, v_a_log, v_d_skip, v_ssm_norm_w, v_sinks, v_f_bias, v_gate_bias, v_w_proj, v_w_out, v_final_norm_w):
    depth = w_in.shape[0]
    chip = 2 * lax.axis_index("x") + lax.axis_index("y")

    own = [w_in.astype(BF16), w_proj.astype(BF16), w_out.astype(BF16), conv_w, gate_bias]
    gathered = _gather_weights(own, name="gather_weights")

    def whole(a, li, axis):
        return jnp.concatenate([jnp.where(chip == k, own[a][li], gathered[a][k, li]) for k in range(N_CHIPS)],
                               axis=axis)

    wls = []
    for li in range(depth):
        wls.append(dict(
            norm_w=norm_w[li], w_in=_pad_w_in(whole(0, li, 1)),
            conv_w=whole(3, li, 1), conv_b=conv_b[li], dt_bias=dt_bias[li], a_log=a_log[li], d_skip=d_skip[li],
            ssm_norm_w=ssm_norm_w[li], sinks=sinks[li], f_bias=f_bias[li], gate_bias=whole(4, li, 1),
            w_proj=whole(1, li, 1),
            w_out=whole(2, li, 0)))

    loss_part, grad_x, grads, d_final = _local_step(x, loss_target, wls, final_norm_w)
    loss = lax.psum(loss_part, ("x", "y", "c"))

    c_in = w_in.shape[2]
    r_proj = w_proj.shape[2]
    r_out = w_out.shape[1]
    full_in = jnp.stack([jnp.stack([grads[li]["w_in"][:, k * c_in:(k + 1) * c_in] for li in range(depth)])
                         for k in range(N_CHIPS)])
    full_proj = jnp.stack([jnp.stack([grads[li]["w_proj"][:, k * r_proj:(k + 1) * r_proj].reshape(-1, D_MODEL)
                                      for li in range(depth)]) for k in range(N_CHIPS)])
    full_out = jnp.stack([jnp.stack([grads[li]["w_out"][k * r_out:(k + 1) * r_out] for li in range(depth)])
                          for k in range(N_CHIPS)])
    fulls = [full_in, full_proj, full_out]
    others = _pair_exchange(fulls, name="grad_pair_exchange")
    pair = [_add_slot_layer(f, o, name=f"grad_pair_add{i}") for i, (f, o) in enumerate(zip(fulls, others))]
    parts = _chip_exchange([p[1] for p in pair], name="grad_chip_exchange")
    mine = [_sum_slots(p, pr[0], name=f"grad_slot_sum{i}") for i, (p, pr) in enumerate(zip(parts, pair))]
    theirs = _pair_share(mine, name="grad_pair_share")
    core = lax.axis_index("c")
    red_in, red_proj, red_out = [jnp.stack([jnp.where(core == li, m, t) for li in range(depth)])
                                 for m, t in zip(mine, theirs)]
    grad_w_in = red_in
    grad_w_proj = red_proj.reshape(w_proj.shape)
    grad_w_out = red_out

    small_full = {
        "norm_w": jnp.stack([g["norm_w"] for g in grads]), "conv_b": jnp.stack([g["conv_b"] for g in grads]),
        "dt_bias": jnp.stack([g["dt_bias"] for g in grads]), "a_log": jnp.stack([g["a_log"] for g in grads]),
        "d_skip": jnp.stack([g["d_skip"] for g in grads]),
        "ssm_norm_w": jnp.stack([g["ssm_norm_w"] for g in grads]),
        "sinks": jnp.stack([g["sinks"] for g in grads]), "f_bias": jnp.stack([g["f_bias"] for g in grads]),
        "final_norm_w": d_final,
        "conv_w": jnp.stack([g["conv_w"] for g in grads]), "gate_bias": jnp.stack([g["gate_bias"] for g in grads])}
    shapes = [small_full[k].shape for k in _SMALL]
    summed = _unpack(_allreduce_small(_pack([small_full[k] for k in _SMALL]), name="allreduce_small"), shapes)
    gsmall = dict(zip(_SMALL, summed))
    gsmall["conv_w"] = lax.dynamic_slice_in_dim(gsmall["conv_w"], chip * conv_w.shape[2], conv_w.shape[2], axis=2)
    gsmall["gate_bias"] = lax.dynamic_slice_in_dim(gsmall["gate_bias"], chip * gate_bias.shape[2],
                                                   gate_bias.shape[2], axis=2)

    w_small = dict(norm_w=norm_w, conv_b=conv_b, dt_bias=dt_bias, a_log=a_log, d_skip=d_skip,
                   ssm_norm_w=ssm_norm_w, sinks=sinks, f_bias=f_bias, final_norm_w=final_norm_w, conv_w=conv_w,
                   gate_bias=gate_bias)
    m_small = dict(norm_w=m_norm_w, conv_b=m_conv_b, dt_bias=m_dt_bias, a_log=m_a_log, d_skip=m_d_skip,
                   ssm_norm_w=m_ssm_norm_w, sinks=m_sinks, f_bias=m_f_bias, final_norm_w=m_final_norm_w,
                   conv_w=m_conv_w, gate_bias=m_gate_bias)
    v_small = dict(norm_w=v_norm_w, conv_b=v_conv_b, dt_bias=v_dt_bias, a_log=v_a_log, d_skip=v_d_skip,
                   ssm_norm_w=v_ssm_norm_w, sinks=v_sinks, f_bias=v_f_bias, final_norm_w=v_final_norm_w,
                   conv_w=v_conv_w, gate_bias=v_gate_bias)
    sshapes = [w_small[k].shape for k in _SMALL]
    ds, ms, vs = _adamw(_pack([w_small[k] for k in _SMALL]), _pack([gsmall[k] for k in _SMALL]),
                        _pack([m_small[k] for k in _SMALL]), _pack([v_small[k] for k in _SMALL]), name="adamw_small")
    delta = dict(zip(_SMALL, _unpack(ds, sshapes)))
    new_m = dict(zip(_SMALL, _unpack(ms, sshapes)))
    new_v = dict(zip(_SMALL, _unpack(vs, sshapes)))
    grad = dict(gsmall)
    for nm, w, g, m, v in (("w_in", w_in, grad_w_in, m_w_in, v_w_in),
                           ("w_proj", w_proj, grad_w_proj, m_w_proj, v_w_proj),
                           ("w_out", w_out, grad_w_out, m_w_out, v_w_out)):
        d2, m2, v2 = _adamw(_rows2d(w), _rows2d(g), _rows2d(m), _rows2d(v), name=f"adamw_{nm}")
        grad[nm] = g
        delta[nm], new_m[nm], new_v[nm] = d2.reshape(w.shape), m2.reshape(w.shape), v2.reshape(w.shape)

    order = ("norm_w", "w_in", "conv_w", "conv_b", "dt_bias", "a_log", "d_skip", "ssm_norm_w", "sinks", "f_bias",
             "gate_bias", "w_proj", "w_out", "final_norm_w")
    return (loss, grad_x, *[grad[k] for k in order], *[delta[k] for k in order],
            *[new_m[k] for k in order], *[new_v[k] for k in order])
```

```python
import functools
import math

import numpy as np
import jax
import jax.numpy as jnp
from jax import lax
from jax.experimental import pallas as pl
from jax.experimental.pallas import tpu as pltpu

F32 = jnp.float32
BF16 = jnp.bfloat16
HIGHEST = lax.Precision.HIGHEST
MESH = pl.DeviceIdType.MESH

D_MODEL = 1024
HEAD_DIM = 64
N_HEADS = 16
N_GROUPS = 4
SSM_STATE = 128
CHUNK = 128
CONV_WIDTH = 4
CONV_DIM = 2048
ROPE_THETA = 10000.0
NORM_EPS = 1e-6
LANES = 128
N_CHIPS = 4
N_DEV = 8

ADAM_LR = 0.001
ADAM_B1 = 0.9
ADAM_B2 = 0.999
ADAM_EPS = 1e-08
ADAM_WD = 0.01
ADAM_STEP = 10

_REF_COLS = {}
_off = 0
for _n, _s in (("xbc", 2048), ("a_z", 1024), ("a_dt", 16), ("b_q", 1024), ("b_k", 256), ("b_v", 256),
               ("b_z", 1024), ("c_q", 1024), ("c_k", 1024), ("c_v", 1024), ("c_f", 16), ("c_z", 1024),
               ("gates", 3072)):
    _REF_COLS[_n] = (_off, _s)
    _off += _s
N_IN = _off

_PAD_ORDER = (("gates", 3072), ("xbc", 2048), ("a_z", 1024), ("b_q", 1024), ("b_z", 1024), ("c_q", 1024),
              ("c_k", 1024), ("c_v", 1024), ("c_z", 1024), ("b_k", 256), ("b_v", 256), ("a_dt", 512),
              ("c_f", 128))
_PAD_COLS = {}
_off = 0
for _n, _s in _PAD_ORDER:
    _PAD_COLS[_n] = (_off, _s)
    _off += _s
N_USED = _off
N_PAD = 13824


def _cp(sem, vmem_mb=48):
    return pltpu.CompilerParams(dimension_semantics=sem, vmem_limit_bytes=vmem_mb * 1024 * 1024)


def _dot(a, b, dims=((1,), (0,)), precision=None):
    return lax.dot_general(a, b, (dims, ((), ())), preferred_element_type=F32, precision=precision)


def _dot_nt(a, b):
    return _dot(a, b, ((1,), (1,)))


def _dot_tn(a, b):
    return _dot(a, b, ((0,), (0,)))


def _col(v, idx):
    lane = lax.broadcasted_iota(jnp.int32, v.shape, 1)
    return jnp.sum(jnp.where(lane == idx, v, 0.0), axis=1, keepdims=True)


def _row(v, idx):
    row = lax.broadcasted_iota(jnp.int32, v.shape, 0)
    return jnp.sum(jnp.where(row == idx, v, 0.0), axis=0, keepdims=True)


def _iota_col():
    return lax.broadcasted_iota(jnp.int32, (CHUNK, 1), 0)


def _iota_row():
    return lax.broadcasted_iota(jnp.int32, (1, LANES), 1)


def _sigmoid(x):
    return 1.0 / (1.0 + jnp.exp(-x))


def _softplus(x):
    return jnp.maximum(x, 0.0) + jnp.log(1.0 + jnp.exp(-jnp.abs(x)))


def _pad_w_in(w):
    parts = []
    for name, size in _PAD_ORDER:
        s0, sz = _REF_COLS[name]
        seg = w[:, s0:s0 + sz]
        if name == "a_dt":
            seg = jnp.pad(seg.reshape(-1, N_GROUPS, 4), ((0, 0), (0, 0), (0, LANES - 4))).reshape(-1, 512)
        elif name == "c_f":
            seg = jnp.pad(seg, ((0, 0), (0, LANES - 16)))
        parts.append(seg)
    parts.append(jnp.zeros((w.shape[0], N_PAD - N_USED), w.dtype))
    return jnp.concatenate(parts, axis=1)


def _unpad_w_in(wp):
    segs = {}
    for name, _ in _PAD_ORDER:
        p0, psz = _PAD_COLS[name]
        seg = wp[:, p0:p0 + psz]
        if name == "a_dt":
            seg = seg.reshape(-1, N_GROUPS, LANES)[:, :, :4].reshape(-1, 16)
        elif name == "c_f":
            seg = seg[:, :16]
        segs[name] = seg
    order = sorted(_REF_COLS, key=lambda n: _REF_COLS[n][0])
    return jnp.concatenate([segs[n] for n in order], axis=1)


def _group_lanes(v):
    return jnp.pad(v.reshape(N_GROUPS, 1, 4), ((0, 0), (0, 0), (0, LANES - 4)))


def _ungroup_lanes(v):
    return v[:, 0, :4].reshape(16)


def _mm(a, b, *, ta=False, tb=False, tm=512, tn=512, tk=512, out_dtype=F32, add=None, name):
    if ta:
        kdim, m = a.shape
    else:
        m, kdim = a.shape
    if tb:
        n, k2 = b.shape
    else:
        k2, n = b.shape
    assert kdim == k2, (a.shape, b.shape)
    tm, tn, tk = min(tm, m), min(tn, n), min(tk, kdim)
    assert m % tm == 0 and n % tn == 0 and kdim % tk == 0, (m, n, kdim, tm, tn, tk)
    nk = kdim // tk
    a_spec = (pl.BlockSpec((tk, tm), lambda i, j, k: (k, i)) if ta
              else pl.BlockSpec((tm, tk), lambda i, j, k: (i, k)))
    b_spec = (pl.BlockSpec((tn, tk), lambda i, j, k: (j, k)) if tb
              else pl.BlockSpec((tk, tn), lambda i, j, k: (k, j)))
    dims = ((0 if ta else 1,), (1 if tb else 0,))
    has_add = add is not None

    def body(*refs):
        if has_add:
            a_ref, b_ref, add_ref, o_ref, acc_ref = refs
        else:
            a_ref, b_ref, o_ref, acc_ref = refs
        k = pl.program_id(2)
        p = _dot(a_ref[...].astype(BF16), b_ref[...].astype(BF16), dims)

        @pl.when(k == 0)
        def _():
            acc_ref[...] = p

        @pl.when(k > 0)
        def _():
            acc_ref[...] += p

        @pl.when(k == nk - 1)
        def _():
            r = acc_ref[...]
            if has_add:
                r = r + add_ref[...]
            o_ref[...] = r.astype(out_dtype)

    in_specs = [a_spec, b_spec]
    args = [a, b]
    if has_add:
        in_specs.append(pl.BlockSpec((tm, tn), lambda i, j, k: (i, j)))
        args.append(add)
    return pl.pallas_call(
        body, name=name, grid=(m // tm, n // tn, nk),
        in_specs=in_specs, out_specs=pl.BlockSpec((tm, tn), lambda i, j, k: (i, j)),
        out_shape=jax.ShapeDtypeStruct((m, n), out_dtype),
        scratch_shapes=[pltpu.VMEM((tm, tn), F32)],
        compiler_params=_cp(("parallel", "parallel", "arbitrary")),
    )(*args)


def _rms_fwd(x, w, *, name, tm=512):
    t, d = x.shape

    def body(x_ref, w_ref, o_ref, ot_ref):
        xv = x_ref[...]
        r = lax.rsqrt(jnp.mean(xv * xv, axis=1, keepdims=True) + NORM_EPS)
        h = xv * r * w_ref[...]
        o_ref[...] = h.astype(BF16)
        ot_ref[...] = h.T.astype(BF16)

    return pl.pallas_call(
        body, name=name, grid=(t // tm,),
        in_specs=[pl.BlockSpec((tm, d), lambda i: (i, 0)), pl.BlockSpec((1, d), lambda i: (0, 0))],
        out_specs=[pl.BlockSpec((tm, d), lambda i: (i, 0)), pl.BlockSpec((d, tm), lambda i: (0, i))],
        out_shape=[jax.ShapeDtypeStruct((t, d), BF16), jax.ShapeDtypeStruct((d, t), BF16)],
        compiler_params=_cp(("parallel",)),
    )(x, w.reshape(1, d))


def _rms_bwd(x, w, dh, dres, *, name, tm=512):
    t, d = x.shape

    def body(x_ref, w_ref, dh_ref, dres_ref, dx_ref, dw_ref):
        xv = x_ref[...]
        r = lax.rsqrt(jnp.mean(xv * xv, axis=1, keepdims=True) + NORM_EPS)
        xhat = xv * r
        dhv = dh_ref[...]
        dxhat = dhv * w_ref[...]
        dx = r * (dxhat - xhat * jnp.mean(dxhat * xhat, axis=1, keepdims=True))
        dx_ref[...] = dres_ref[...] + dx

        @pl.when(pl.program_id(0) == 0)
        def _():
            dw_ref[...] = jnp.zeros_like(dw_ref)

        dw_ref[...] += jnp.sum(dhv * xhat, axis=0, keepdims=True)

    return pl.pallas_call(
        body, name=name, grid=(t // tm,),
        in_specs=[pl.BlockSpec((tm, d), lambda i: (i, 0)), pl.BlockSpec((1, d), lambda i: (0, 0)),
                  pl.BlockSpec((tm, d), lambda i: (i, 0)), pl.BlockSpec((tm, d), lambda i: (i, 0))],
        out_specs=[pl.BlockSpec((tm, d), lambda i: (i, 0)), pl.BlockSpec((1, d), lambda i: (0, 0))],
        out_shape=[jax.ShapeDtypeStruct((t, d), F32), jax.ShapeDtypeStruct((1, d), F32)],
        compiler_params=_cp(("arbitrary",)),
    )(x, w.reshape(1, d), dh, dres)


def _final_loss(x, w, target, *, name, tm=512):
    t, d = x.shape

    def body(x_ref, w_ref, t_ref, loss_ref, dx_ref, dw_ref):
        xv = x_ref[...]
        wv = w_ref[...]
        r = lax.rsqrt(jnp.mean(xv * xv, axis=1, keepdims=True) + NORM_EPS)
        xhat = xv * r
        err = xhat * wv - t_ref[...]
        dy = err * (1.0 / d)
        dxhat = dy * wv
        dx_ref[...] = r * (dxhat - xhat * jnp.mean(dxhat * xhat, axis=1, keepdims=True))

        @pl.when(pl.program_id(0) == 0)
        def _():
            dw_ref[...] = jnp.zeros_like(dw_ref)
            loss_ref[...] = jnp.zeros_like(loss_ref)

        dw_ref[...] += jnp.sum(dy * xhat, axis=0, keepdims=True)
        part = 0.5 * jnp.sum(jnp.mean(err * err, axis=1, keepdims=True), axis=0, keepdims=True)
        loss_ref[...] += jnp.broadcast_to(part, loss_ref.shape)

    return pl.pallas_call(
        body, name=name, grid=(t // tm,),
        in_specs=[pl.BlockSpec((tm, d), lambda i: (i, 0)), pl.BlockSpec((1, d), lambda i: (0, 0)),
                  pl.BlockSpec((tm, d), lambda i: (i, 0))],
        out_specs=[pl.BlockSpec((8, LANES), lambda i: (0, 0)), pl.BlockSpec((tm, d), lambda i: (i, 0)),
                   pl.BlockSpec((1, d), lambda i: (0, 0))],
        out_shape=[jax.ShapeDtypeStruct((8, LANES), F32), jax.ShapeDtypeStruct((t, d), F32),
                   jax.ShapeDtypeStruct((1, d), F32)],
        compiler_params=_cp(("arbitrary",)),
    )(x, w.reshape(1, d), target)


_CB = 128


def _conv_pre(u, w_ref, b_ref):
    s = u.shape[0]
    row = lax.broadcasted_iota(jnp.int32, u.shape, 0)
    pre = b_ref[...] + w_ref[CONV_WIDTH - 1:CONV_WIDTH, :] * u
    for sh in range(1, CONV_WIDTH):
        shifted = jnp.where(row >= sh, pltpu.roll(u, sh, 0), 0.0)
        pre = pre + w_ref[CONV_WIDTH - 1 - sh:CONV_WIDTH - sh, :] * shifted
    return pre


def _conv_fwd(proj3, cw, cb, *, name):
    b, s, _ = proj3.shape
    c0 = _PAD_COLS["xbc"][0] // _CB

    def body(u_ref, w_ref, b_ref, o_ref):
        pre = _conv_pre(u_ref[...], w_ref, b_ref)
        o_ref[...] = pre * _sigmoid(pre)

    return pl.pallas_call(
        body, name=name, grid=(b, CONV_DIM // _CB),
        in_specs=[pl.BlockSpec((None, s, _CB), lambda i, j: (i, 0, c0 + j)),
                  pl.BlockSpec((CONV_WIDTH, _CB), lambda i, j: (0, j)),
                  pl.BlockSpec((1, _CB), lambda i, j: (0, j))],
        out_specs=pl.BlockSpec((None, s, _CB), lambda i, j: (i, 0, j)),
        out_shape=jax.ShapeDtypeStruct((b, s, CONV_DIM), F32),
        compiler_params=_cp(("parallel", "parallel")),
    )(proj3, cw, cb.reshape(1, CONV_DIM))


def _conv_bwd(proj3, cw, cb, dact, *, name):
    b, s, _ = proj3.shape
    c0 = _PAD_COLS["xbc"][0] // _CB

    def body(u_ref, w_ref, b_ref, da_ref, du_ref, dwb_ref):
        u = u_ref[...]
        pre = _conv_pre(u, w_ref, b_ref)
        sg = _sigmoid(pre)
        dpre = da_ref[...] * (sg * (1.0 + pre * (1.0 - sg)))
        row = lax.broadcasted_iota(jnp.int32, u.shape, 0)
        du = w_ref[CONV_WIDTH - 1:CONV_WIDTH, :] * dpre
        rows = [jnp.sum(dpre * u, axis=0, keepdims=True)]
        for sh in range(1, CONV_WIDTH):
            fwd_shift = jnp.where(row < s - sh, pltpu.roll(dpre, s - sh, 0), 0.0)
            du = du + w_ref[CONV_WIDTH - 1 - sh:CONV_WIDTH - sh, :] * fwd_shift
            ush = jnp.where(row >= sh, pltpu.roll(u, sh, 0), 0.0)
            rows.append(jnp.sum(dpre * ush, axis=0, keepdims=True))
        du_ref[...] = du.astype(BF16)

        @pl.when(pl.program_id(1) == 0)
        def _():
            dwb_ref[...] = jnp.zeros_like(dwb_ref)

        for sh in range(CONV_WIDTH):
            k = CONV_WIDTH - 1 - sh
            dwb_ref[k:k + 1, :] += rows[sh]
        dwb_ref[CONV_WIDTH:CONV_WIDTH + 1, :] += jnp.sum(dpre, axis=0, keepdims=True)

    return pl.pallas_call(
        body, name=name, grid=(CONV_DIM // _CB, b),
        in_specs=[pl.BlockSpec((None, s, _CB), lambda j, i: (i, 0, c0 + j)),
                  pl.BlockSpec((CONV_WIDTH, _CB), lambda j, i: (0, j)),
                  pl.BlockSpec((1, _CB), lambda j, i: (0, j)),
                  pl.BlockSpec((None, s, _CB), lambda j, i: (i, 0, j))],
        out_specs=[pl.BlockSpec((None, s, _CB), lambda j, i: (i, 0, j)),
                   pl.BlockSpec((8, _CB), lambda j, i: (0, j))],
        out_shape=[jax.ShapeDtypeStruct((b, s, CONV_DIM), BF16), jax.ShapeDtypeStruct((8, CONV_DIM), F32)],
        compiler_params=_cp(("parallel", "arbitrary")),
    )(proj3, cw, cb.reshape(1, CONV_DIM), dact)


def _ssd_common(dt_ref, dtb_ref, alog_ref):
    row = lax.broadcasted_iota(jnp.int32, (CHUNK, CHUNK), 0)
    lane = lax.broadcasted_iota(jnp.int32, (CHUNK, CHUNK), 1)
    causal = row >= lane
    tri = causal.astype(F32)
    dtv = _softplus(dt_ref[...] + dtb_ref[...])
    a_row = -jnp.exp(alog_ref[...])
    acum = _dot(tri, dtv * a_row, precision=HIGHEST)
    return row, lane, causal, dtv, a_row, acum, acum.T


def _ssd_pair(pp, x, dtv, acum, acum_t, causal, lane, row):
    lo = lane < HEAD_DIM
    r0, r1 = 2 * pp, 2 * pp + 1
    dtp = jnp.where(lo, _col(dtv, r0), _col(dtv, r1))
    ac0, ac1 = _col(acum, r0), _col(acum, r1)
    ar0, ar1 = _row(acum_t, r0), _row(acum_t, r1)
    d0 = jnp.where(causal, jnp.exp(jnp.where(causal, ac0 - ar0, 0.0)), 0.0)
    d1 = jnp.where(causal, jnp.exp(jnp.where(causal, ac1 - ar1, 0.0)), 0.0)
    al0, al1 = _col(ar0, CHUNK - 1), _col(ar1, CHUNK - 1)
    eac = jnp.where(lo, jnp.exp(ac0), jnp.exp(ac1))
    dsp = jnp.where(lo, jnp.exp(al0 - ac0), jnp.exp(al1 - ac1))
    eal = jnp.where(_iota_col() < HEAD_DIM, jnp.exp(al0), jnp.exp(al1))
    return lo, dtp, x * dtp, d0, d1, al0, al1, eac, dsp, eal


def _ssd_fwd(proj3, xact3, dtb, alog, dsk, nw, *, name):
    b, s, _ = proj3.shape
    nc = s // CHUNK
    dt0 = _PAD_COLS["a_dt"][0] // 512
    z0 = _PAD_COLS["a_z"][0] // D_MODEL

    def body(xs_ref, bm_ref, cm_ref, dt_ref, z_ref, dtb_ref, alog_ref, dsk_ref, nw_ref,
             ya_ref, ypre_ref, hst_ref, h_scr):
        @pl.when(pl.program_id(1) == 0)
        def _():
            h_scr[...] = jnp.zeros_like(h_scr)

        for g in range(N_GROUPS):
            w256 = pl.ds(256 * g, 256)
            w128 = pl.ds(LANES * g, LANES)
            group(xs_ref.at[:, w256], bm_ref.at[:, w128], cm_ref.at[:, w128], dt_ref.at[:, w128],
                  z_ref.at[:, w256], dtb_ref.at[g], alog_ref.at[g], dsk_ref.at[g], nw_ref.at[g],
                  ya_ref.at[:, w256], ypre_ref.at[:, w256], hst_ref.at[g], h_scr.at[g])

    def group(xs_ref, bm_ref, cm_ref, dt_ref, z_ref, dtb_ref, alog_ref, dsk_ref, nw_ref,
              ya_ref, ypre_ref, hst_ref, h_scr):
        row, lane, causal, dtv, a_row, acum, acum_t = _ssd_common(dt_ref, dtb_ref, alog_ref)
        bb = bm_ref[...].astype(BF16)
        cb = cm_ref[...].astype(BF16)
        cbm = _dot_nt(cb, bb)
        hst_ref[...] = h_scr[...]
        dskv = dsk_ref[...]
        for pp in range(2):
            x = xs_ref[:, LANES * pp:LANES * (pp + 1)]
            lo, dtp, xd, d0, d1, al0, al1, eac, dsp, eal = _ssd_pair(pp, x, dtv, acum, acum_t, causal, lane, row)
            xdb = xd.astype(BF16)
            y = jnp.where(lo, _dot((cbm * d0).astype(BF16), xdb), _dot((cbm * d1).astype(BF16), xdb))
            h = h_scr[pp]
            y = y + eac * _dot_nt(cb, h.astype(BF16))
            h_scr[pp] = h * eal + _dot_tn((xd * dsp).astype(BF16), bb)
            dskp = jnp.where((_iota_row() < HEAD_DIM), _col(dskv, 2 * pp), _col(dskv, 2 * pp + 1))
            ypre_ref[:, LANES * pp:LANES * (pp + 1)] = y + x * dskp
        ypre = ypre_ref[...]
        z = z_ref[...]
        yg = ypre * (z * _sigmoid(z))
        rstd = lax.rsqrt(jnp.sum(yg * yg, axis=1, keepdims=True) * (1.0 / 256.0) + NORM_EPS)
        ya_ref[...] = (yg * rstd * nw_ref[...]).astype(BF16)

    g = N_GROUPS
    par = pl.BlockSpec((g, 1, LANES), lambda i, c: (0, 0, 0))
    wide = pl.BlockSpec((None, CHUNK, D_MODEL), lambda i, c: (i, c, 0))
    return pl.pallas_call(
        body, name=name, grid=(b, nc),
        in_specs=[wide,
                  pl.BlockSpec((None, CHUNK, 512), lambda i, c: (i, c, 2)),
                  pl.BlockSpec((None, CHUNK, 512), lambda i, c: (i, c, 3)),
                  pl.BlockSpec((None, CHUNK, 512), lambda i, c: (i, c, dt0)),
                  pl.BlockSpec((None, CHUNK, D_MODEL), lambda i, c: (i, c, z0)),
                  par, par, par,
                  pl.BlockSpec((g, 1, 256), lambda i, c: (0, 0, 0))],
        out_specs=[wide, wide,
                   pl.BlockSpec((None, None, g, 2, CHUNK, SSM_STATE), lambda i, c: (i, c, 0, 0, 0, 0))],
        out_shape=[jax.ShapeDtypeStruct((b, s, D_MODEL), BF16), jax.ShapeDtypeStruct((b, s, D_MODEL), F32),
                   jax.ShapeDtypeStruct((b, nc, g, 2, CHUNK, SSM_STATE), F32)],
        scratch_shapes=[pltpu.VMEM((g, 2, CHUNK, SSM_STATE), F32)],
        compiler_params=_cp(("parallel", "arbitrary")),
    )(xact3, xact3, xact3, proj3, proj3, dtb, alog, dsk, nw)


def _ssd_bwd(proj3, xact3, dtb, alog, dsk, nw, ypre3, hst, dya3, *, name):
    b, s, _ = proj3.shape
    nc = s // CHUNK
    dt0 = _PAD_COLS["a_dt"][0] // 512
    z0 = _PAD_COLS["a_z"][0] // D_MODEL

    def body(xs_ref, bm_ref, cm_ref, dt_ref, z_ref, dtb_ref, alog_ref, dsk_ref, nw_ref, ypre_ref, hst_ref,
             dya_ref, dact_ref, dz_ref, ddt_ref, ddtb_ref, dalog_ref, ddsk_ref, dnw_ref, dh_scr):
        first = jnp.logical_and(pl.program_id(0) == 0, pl.program_id(1) == 0)

        @pl.when(first)
        def _():
            ddtb_ref[...] = jnp.zeros_like(ddtb_ref)
            dalog_ref[...] = jnp.zeros_like(dalog_ref)
            ddsk_ref[...] = jnp.zeros_like(ddsk_ref)
            dnw_ref[...] = jnp.zeros_like(dnw_ref)

        @pl.when(pl.program_id(1) == 0)
        def _():
            dh_scr[...] = jnp.zeros_like(dh_scr)

        for g in range(N_GROUPS):
            w256 = pl.ds(256 * g, 256)
            w128 = pl.ds(LANES * g, LANES)
            group(xs_ref.at[:, w256], bm_ref.at[:, w128], cm_ref.at[:, w128], dt_ref.at[:, w128],
                  z_ref.at[:, w256], dtb_ref.at[g], alog_ref.at[g], dsk_ref.at[g], nw_ref.at[g],
                  ypre_ref.at[:, w256], hst_ref.at[g], dya_ref.at[:, w256],
                  dact_ref.at[:, w256], dact_ref.at[:, pl.ds(D_MODEL + LANES * g, LANES)],
                  dact_ref.at[:, pl.ds(D_MODEL + 512 + LANES * g, LANES)], dz_ref.at[:, w256], ddt_ref.at[:, w128],
                  ddtb_ref.at[g], dalog_ref.at[g], ddsk_ref.at[g], dnw_ref.at[g], dh_scr.at[g])

    def group(xs_ref, bm_ref, cm_ref, dt_ref, z_ref, dtb_ref, alog_ref, dsk_ref, nw_ref, ypre_ref, hst_ref,
              dya_ref, dxs_ref, dbm_ref, dcm_ref, dz_ref, ddt_ref, ddtb_ref, dalog_ref, ddsk_ref, dnw_ref,
              dh_scr):
        row, lane, causal, dtv, a_row, acum, acum_t = _ssd_common(dt_ref, dtb_ref, alog_ref)
        lane1 = _iota_row()
        bb = bm_ref[...].astype(BF16)
        cb = cm_ref[...].astype(BF16)
        cbm = _dot_nt(cb, bb)

        z = z_ref[...]
        ypre = ypre_ref[...]
        dya = dya_ref[...]
        sz = _sigmoid(z)
        silu = z * sz
        yg = ypre * silu
        rstd = lax.rsqrt(jnp.sum(yg * yg, axis=1, keepdims=True) * (1.0 / 256.0) + NORM_EPS)
        dnw_ref[...] += jnp.sum(dya * yg * rstd, axis=0, keepdims=True)
        dn = dya * nw_ref[...]
        dyg = rstd * dn - yg * (rstd * rstd * rstd * (1.0 / 256.0)) * jnp.sum(dn * yg, axis=1, keepdims=True)
        dz_ref[...] = (dyg * ypre * (sz * (1.0 + z * (1.0 - sz)))).astype(BF16)
        dy_all = dyg * silu

        dskv = dsk_ref[...]
        da_cols = jnp.zeros((CHUNK, LANES), F32)
        dxt_cols = jnp.zeros((CHUNK, LANES), F32)
        ddsk_row = jnp.zeros((1, LANES), F32)
        dcb = jnp.zeros((CHUNK, CHUNK), F32)
        dc = jnp.zeros((CHUNK, SSM_STATE), F32)
        db = jnp.zeros((CHUNK, SSM_STATE), F32)
        last = _iota_col() == CHUNK - 1
        for pp in range(2):
            r0, r1 = 2 * pp, 2 * pp + 1
            x = xs_ref[:, LANES * pp:LANES * (pp + 1)]
            dy = dy_all[:, LANES * pp:LANES * (pp + 1)]
            lo, dtp, xd, d0, d1, al0, al1, eac, dsp, eal = _ssd_pair(pp, x, dtv, acum, acum_t, causal, lane, row)
            w0, w1 = cbm * d0, cbm * d1
            w0b, w1b = w0.astype(BF16), w1.astype(BF16)
            xdb = xd.astype(BF16)
            dyb = dy.astype(BF16)
            h = hst_ref[pp]
            dhn = dh_scr[pp]
            hb = h.astype(BF16)
            dhb = dhn.astype(BF16)
            g0 = _dot_nt(jnp.where(lo, dy, 0.0).astype(BF16), xdb)
            g1 = _dot_nt(jnp.where(lo, 0.0, dy).astype(BF16), xdb)
            dcb = dcb + g0 * d0 + g1 * d1
            m0, m1 = g0 * w0, g1 * w1
            bdh = _dot_nt(bb, dhb)
            dxd = jnp.where(lo, _dot_tn(w0b, dyb), _dot_tn(w1b, dyb)) + dsp * bdh
            ch = _dot_nt(cb, hb)
            edy = eac * dy
            edyb = edy.astype(BF16)
            xds = xd * dsp
            dc = dc + _dot(edyb, hb)
            db = db + _dot(xds.astype(BF16), dhb)
            dh_scr[pp] = dhn * eal + _dot_tn(edyb, cb)
            t2 = edy * ch
            t3 = xds * bdh
            r4 = jnp.sum(dhn * h, axis=1, keepdims=True)
            s4_0 = jnp.sum(jnp.where(_iota_col() < HEAD_DIM, r4, 0.0), axis=0, keepdims=True)
            s4_1 = jnp.sum(r4, axis=0, keepdims=True) - s4_0
            t2_0 = jnp.sum(jnp.where(lo, t2, 0.0), axis=1, keepdims=True)
            t2_1 = jnp.sum(t2, axis=1, keepdims=True) - t2_0
            t3_0 = jnp.sum(jnp.where(lo, t3, 0.0), axis=1, keepdims=True)
            t3_1 = jnp.sum(t3, axis=1, keepdims=True) - t3_0
            dal0 = jnp.sum(t3_0, axis=0, keepdims=True) + jnp.exp(al0) * s4_0
            dal1 = jnp.sum(t3_1, axis=0, keepdims=True) + jnp.exp(al1) * s4_1
            dac0 = (jnp.sum(m0, axis=1, keepdims=True) - jnp.sum(m0.T, axis=1, keepdims=True)
                    + t2_0 - t3_0 + jnp.where(last, dal0, 0.0))
            dac1 = (jnp.sum(m1, axis=1, keepdims=True) - jnp.sum(m1.T, axis=1, keepdims=True)
                    + t2_1 - t3_1 + jnp.where(last, dal1, 0.0))
            da_cols = da_cols + jnp.where(lane == r0, dac0, 0.0) + jnp.where(lane == r1, dac1, 0.0)
            xx = dxd * x
            x0 = jnp.sum(jnp.where(lo, xx, 0.0), axis=1, keepdims=True)
            x1 = jnp.sum(xx, axis=1, keepdims=True) - x0
            dxt_cols = dxt_cols + jnp.where(lane == r0, x0, 0.0) + jnp.where(lane == r1, x1, 0.0)
            dskp = jnp.where((_iota_row() < HEAD_DIM), _col(dskv, r0), _col(dskv, r1))
            dxs_ref[:, LANES * pp:LANES * (pp + 1)] = dxd * dtp + dy * dskp
            yx = jnp.sum(dy * x, axis=0, keepdims=True)
            k0 = jnp.sum(jnp.where((_iota_row() < HEAD_DIM), yx, 0.0), axis=1, keepdims=True)
            k1 = jnp.sum(yx, axis=1, keepdims=True) - k0
            ddsk_row = ddsk_row + jnp.where(lane1 == r0, k0, 0.0) + jnp.where(lane1 == r1, k1, 0.0)
        dcbb = dcb.astype(BF16)
        dcm_ref[...] = dc + _dot(dcbb, bb)
        dbm_ref[...] = db + _dot_tn(dcbb, cb)
        tri_t = (row <= lane).astype(F32)
        dadt = _dot(tri_t, da_cols, precision=HIGHEST)
        ddtv = dadt * a_row + dxt_cols
        dalog_ref[...] += jnp.sum(dadt * dtv, axis=0, keepdims=True) * a_row
        ddt_raw = ddtv * _sigmoid(dt_ref[...] + dtb_ref[...])
        ddt_ref[...] = ddt_raw.astype(BF16)
        ddtb_ref[...] += jnp.sum(ddt_raw, axis=0, keepdims=True)
        ddsk_ref[...] += ddsk_row

    g = N_GROUPS
    rc = lambda c: nc - 1 - c
    par = pl.BlockSpec((g, 1, LANES), lambda i, c: (0, 0, 0))
    parw = pl.BlockSpec((g, 1, 256), lambda i, c: (0, 0, 0))
    wide = pl.BlockSpec((None, CHUNK, D_MODEL), lambda i, c: (i, rc(c), 0))
    blk512 = lambda col: pl.BlockSpec((None, CHUNK, 512), lambda i, c: (i, rc(c), col))
    return pl.pallas_call(
        body, name=name, grid=(b, nc),
        in_specs=[wide, blk512(2), blk512(3), blk512(dt0),
                  pl.BlockSpec((None, CHUNK, D_MODEL), lambda i, c: (i, rc(c), z0)),
                  par, par, par, parw,
                  wide,
                  pl.BlockSpec((None, None, g, 2, CHUNK, SSM_STATE), lambda i, c: (i, rc(c), 0, 0, 0, 0)),
                  wide],
        out_specs=[pl.BlockSpec((None, CHUNK, CONV_DIM), lambda i, c: (i, rc(c), 0)), wide, blk512(0),
                   par, par, par, parw],
        out_shape=[jax.ShapeDtypeStruct((b, s, CONV_DIM), F32), jax.ShapeDtypeStruct((b, s, D_MODEL), BF16),
                   jax.ShapeDtypeStruct((b, s, 512), BF16),
                   jax.ShapeDtypeStruct((g, 1, LANES), F32), jax.ShapeDtypeStruct((g, 1, LANES), F32),
                   jax.ShapeDtypeStruct((g, 1, LANES), F32), jax.ShapeDtypeStruct((g, 1, 256), F32)],
        scratch_shapes=[pltpu.VMEM((g, 2, CHUNK, SSM_STATE), F32)],
        compiler_params=_cp(("arbitrary", "arbitrary")),
    )(xact3, xact3, xact3, proj3, proj3, dtb, alog, dsk, nw, ypre3, hst, dya3)


def _fgate_fwd(proj3, fb, *, name):
    b, s, _ = proj3.shape
    f0 = _PAD_COLS["c_f"][0] // LANES

    def body(f_ref, fb_ref, cum_ref, carry):
        @pl.when(pl.program_id(1) == 0)
        def _():
            carry[...] = jnp.zeros_like(carry)

        row = lax.broadcasted_iota(jnp.int32, (CHUNK, CHUNK), 0)
        lane = lax.broadcasted_iota(jnp.int32, (CHUNK, CHUNK), 1)
        tri = (row >= lane).astype(F32)
        lf = -_softplus(-(f_ref[...] + fb_ref[...]))
        cs = _dot(tri, lf, precision=HIGHEST) + carry[0:1, :]
        cum_ref[...] = cs
        carry[0:1, :] = _row(cs, CHUNK - 1)

    return pl.pallas_call(
        body, name=name, grid=(b, s // CHUNK),
        in_specs=[pl.BlockSpec((None, CHUNK, LANES), lambda i, c: (i, c, f0)),
                  pl.BlockSpec((1, LANES), lambda i, c: (0, 0))],
        out_specs=pl.BlockSpec((None, CHUNK, LANES), lambda i, c: (i, c, 0)),
        out_shape=jax.ShapeDtypeStruct((b, s, LANES), F32),
        scratch_shapes=[pltpu.VMEM((8, LANES), F32)],
        compiler_params=_cp(("parallel", "arbitrary")),
    )(proj3, fb)


def _fgate_bwd(proj3, fb, dcum, *, name):
    b, s, _ = proj3.shape
    nc = s // CHUNK
    f0 = _PAD_COLS["c_f"][0] // LANES

    def body(f_ref, fb_ref, dc_ref, df_ref, dfb_ref, carry):
        first = jnp.logical_and(pl.program_id(0) == 0, pl.program_id(1) == 0)

        @pl.when(first)
        def _():
            dfb_ref[...] = jnp.zeros_like(dfb_ref)

        @pl.when(pl.program_id(1) == 0)
        def _():
            carry[...] = jnp.zeros_like(carry)

        row = lax.broadcasted_iota(jnp.int32, (CHUNK, CHUNK), 0)
        lane = lax.broadcasted_iota(jnp.int32, (CHUNK, CHUNK), 1)
        tri_t = (row <= lane).astype(F32)
        dlf = _dot(tri_t, dc_ref[...], precision=HIGHEST) + carry[0:1, :]
        carry[0:1, :] = _row(dlf, 0)
        df = dlf * _sigmoid(-(f_ref[...] + fb_ref[...]))
        df_ref[...] = df.astype(BF16)
        dfb_ref[...] += jnp.sum(df, axis=0, keepdims=True)

    return pl.pallas_call(
        body, name=name, grid=(b, nc),
        in_specs=[pl.BlockSpec((None, CHUNK, LANES), lambda i, c: (i, nc - 1 - c, f0)),
                  pl.BlockSpec((1, LANES), lambda i, c: (0, 0)),
                  pl.BlockSpec((None, CHUNK, LANES), lambda i, c: (i, nc - 1 - c, 0))],
        out_specs=[pl.BlockSpec((None, CHUNK, LANES), lambda i, c: (i, nc - 1 - c, 0)),
                   pl.BlockSpec((1, LANES), lambda i, c: (0, 0))],
        out_shape=[jax.ShapeDtypeStruct((b, s, LANES), BF16), jax.ShapeDtypeStruct((1, LANES), F32)],
        scratch_shapes=[pltpu.VMEM((8, LANES), F32)],
        compiler_params=_cp(("arbitrary", "arbitrary")),
    )(proj3, fb, dcum)


_SCALE = HEAD_DIM ** -0.5
_NEG = -1e30


def _fox_fwd(proj3, cum_t, *, name, tb):
    b, s, _ = proj3.shape
    nq = s // tb
    assert _ST_MJ + 2 * nq <= LANES
    q0 = _PAD_COLS["c_q"][0] // LANES
    k0 = _PAD_COLS["c_k"][0] // LANES
    v0 = _PAD_COLS["c_v"][0] // LANES
    z0 = _PAD_COLS["c_z"][0] // LANES

    def body(q_ref, k_ref, v_ref, z_ref, cumt_ref, y_ref, o_ref, st_ref):
        i = pl.program_id(2)
        lane = lax.broadcasted_iota(jnp.int32, (tb, LANES), 1)
        lo = lane < HEAD_DIM
        q = q_ref[...] * _SCALE
        qms = (jnp.where(lo, q, 0.0).astype(BF16), jnp.where(lo, 0.0, q).astype(BF16))
        ones_at = (HEAD_DIM, 0)

        def block(j, carry, diagonal):
            ks = pl.ds(pl.multiple_of(j * tb, tb), tb)
            kb = k_ref[ks, :].astype(BF16)
            v = v_ref[ks, :].astype(F32)
            vhs = (jnp.where(lo, v, jnp.where(lane == ones_at[0], 1.0, 0.0)).astype(BF16),
                   jnp.where(lo, jnp.where(lane == ones_at[1], 1.0, 0.0), v).astype(BF16))
            ckv = cumt_ref[j]
            if diagonal:
                row = lax.broadcasted_iota(jnp.int32, (tb, tb), 0)
                col = lax.broadcasted_iota(jnp.int32, (tb, tb), 1)
                mask = row >= col
            ms, ls, acc, st = carry
            new_m, new_l, pvs, alphas = [], [], [], []
            for hh in range(2):
                sc = _dot_nt(qms[hh], kb) - ckv[hh:hh + 1, :]
                if diagonal:
                    sc = jnp.where(mask, sc, _NEG)
                m_new = jnp.maximum(ms[hh], jnp.max(sc, axis=1, keepdims=True))
                alpha = jnp.exp(ms[hh] - m_new)
                pv = _dot(jnp.exp(sc - m_new).astype(BF16), vhs[hh])
                new_l.append(alpha * ls[hh] + _col(pv, ones_at[hh]))
                new_m.append(m_new)
                pvs.append(pv)
                alphas.append(alpha)
                st = jnp.where(lane == _ST_MJ + 2 * j + hh, m_new, st)
            acc = jnp.where(lo, alphas[0] * acc + pvs[0], alphas[1] * acc + pvs[1])
            return (tuple(new_m), tuple(new_l), acc, st)

        neg = jnp.full((tb, 1), _NEG, F32)
        zero = jnp.zeros((tb, 1), F32)
        init = ((neg, neg), (zero, zero), jnp.zeros((tb, LANES), F32), jnp.zeros((tb, LANES), F32))
        carry = lax.fori_loop(0, i, lambda j, c: block(j, c, False), init)
        ms, ls, acc, st = block(i, carry, True)
        o = acc / jnp.where(lo, ls[0], ls[1])
        o_ref[...] = o
        st = jnp.where(lane == _ST_LSE, ms[0] + jnp.log(ls[0]), st)
        st_ref[...] = jnp.where(lane == _ST_LSE + 1, ms[1] + jnp.log(ls[1]), st)
        z = z_ref[...]
        y_ref[...] = (o * (z * _sigmoid(z))).astype(BF16)

    qspec = lambda c0: pl.BlockSpec((None, tb, LANES), lambda bi, p, i: (bi, i, c0 + p))
    kspec = lambda c0: pl.BlockSpec((None, s, LANES), lambda bi, p, i: (bi, 0, c0 + p))
    ospec = pl.BlockSpec((None, tb, LANES), lambda bi, p, i: (bi, i, p))
    return pl.pallas_call(
        body, name=name, grid=(b, N_HEADS // 2, nq),
        in_specs=[qspec(q0), kspec(k0), kspec(v0), qspec(z0),
                  pl.BlockSpec((None, None, nq, 8, tb), lambda bi, p, i: (bi, p, 0, 0, 0))],
        out_specs=[ospec, ospec, ospec],
        out_shape=[jax.ShapeDtypeStruct((b, s, D_MODEL), BF16)] + [jax.ShapeDtypeStruct((b, s, D_MODEL), F32)] * 2,
        compiler_params=_cp(("parallel", "parallel", "arbitrary")),
    )(proj3, proj3, proj3, proj3, cum_t)


_ST_LSE, _ST_DELTA, _ST_MJ = 0, 2, 8


def _fox_prep(proj3, o3, stat3, dy3, *, name, tr=512):
    b, s, _ = proj3.shape
    z0 = _PAD_COLS["c_z"][0] // LANES

    def body(z_ref, o_ref, fst_ref, dy_ref, dz_ref, do_ref, st_ref):
        lane = lax.broadcasted_iota(jnp.int32, (tr, LANES), 1)
        lo = lane < HEAD_DIM
        z = z_ref[...]
        sz = _sigmoid(z)
        dy = dy_ref[...]
        o = o_ref[...]
        do = dy * (z * sz)
        dz_ref[...] = (dy * o * (sz * (1.0 + z * (1.0 - sz)))).astype(BF16)
        do_ref[...] = do
        doo = do.astype(BF16).astype(F32) * o
        st = jnp.where(lane == _ST_DELTA, jnp.sum(jnp.where(lo, doo, 0.0), axis=1, keepdims=True), fst_ref[...])
        st_ref[...] = jnp.where(lane == _ST_DELTA + 1, jnp.sum(jnp.where(lo, 0.0, doo), axis=1, keepdims=True), st)

    ospec = pl.BlockSpec((None, tr, LANES), lambda bi, p, i: (bi, i, p))
    return pl.pallas_call(
        body, name=name, grid=(b, N_HEADS // 2, s // tr),
        in_specs=[pl.BlockSpec((None, tr, LANES), lambda bi, p, i: (bi, i, z0 + p)), ospec, ospec, ospec],
        out_specs=[ospec, ospec, pl.BlockSpec((None, None, tr, LANES), lambda bi, p, i: (bi, p, i, 0))],
        out_shape=[jax.ShapeDtypeStruct((b, s, D_MODEL), BF16), jax.ShapeDtypeStruct((b, s, D_MODEL), F32),
                   jax.ShapeDtypeStruct((b, N_HEADS // 2, s, LANES), F32)],
        compiler_params=_cp(("parallel", "parallel", "parallel")),
    )(proj3, o3, stat3, dy3)


def _fox_bwd(proj3, cum_t, do3, stats, *, name, tb):
    b, s, _ = proj3.shape
    nq = s // tb
    q0 = _PAD_COLS["c_q"][0] // LANES
    k0 = _PAD_COLS["c_k"][0] // LANES
    v0 = _PAD_COLS["c_v"][0] // LANES

    def body(q_ref, do_ref, st_ref, k_ref, v_ref, cumt_ref, dq_ref, dk_ref, dv_ref, cs_ref):
        j = pl.program_id(2)
        lo = lax.broadcasted_iota(jnp.int32, (tb, LANES), 1) < HEAD_DIM

        @pl.when(j == 0)
        def _():
            dq_ref[...] = jnp.zeros_like(dq_ref)

        kb = k_ref[...].astype(BF16)
        vb = v_ref[...].astype(BF16)
        ckv = cumt_ref[...]

        def block(i, carry, diagonal):
            qs = pl.ds(pl.multiple_of(i * tb, tb), tb)
            q = q_ref[qs, :] * _SCALE
            do = do_ref[qs, :]
            st = st_ref[qs, :]
            if diagonal:
                row = lax.broadcasted_iota(jnp.int32, (tb, tb), 0)
                col = lax.broadcasted_iota(jnp.int32, (tb, tb), 1)
                mask = row >= col
            dk, dv, cs = carry
            new_cs, dqs = [], []
            for hh in range(2):
                sel = lo if hh == 0 else jnp.logical_not(lo)
                qm = jnp.where(sel, q, 0.0).astype(BF16)
                dom = jnp.where(sel, do, 0.0)
                sc = _dot_nt(qm, kb) - ckv[hh:hh + 1, :]
                if diagonal:
                    sc = jnp.where(mask, sc, _NEG)
                mj = _col(st, _ST_MJ + 2 * j + hh)
                w = jnp.exp(mj - _col(st, _ST_LSE + hh))
                pb = jnp.exp(sc - mj).astype(BF16)
                ds = (pb.astype(F32) * w) * (_dot_nt(dom.astype(BF16), vb) - _col(st, _ST_DELTA + hh))
                dsb = ds.astype(BF16)
                dv = dv + _dot_tn(pb, (dom * w).astype(BF16))
                dk = dk + _dot_tn(dsb, qm)
                new_cs.append(cs[hh] + jnp.sum(ds, axis=0, keepdims=True))
                dqs.append(_dot(dsb, kb))
            dq_ref[qs, :] += jnp.where(lo, dqs[0], dqs[1]) * _SCALE
            return (dk, dv, tuple(new_cs))

        zrow = jnp.zeros((1, tb), F32)
        init = (jnp.zeros((tb, LANES), F32), jnp.zeros((tb, LANES), F32), (zrow, zrow))
        carry = block(j, init, True)
        dk, dv, cs = lax.fori_loop(j + 1, nq, lambda i, c: block(i, c, False), carry)
        dk_ref[...] = dk.astype(BF16)
        dv_ref[...] = dv.astype(BF16)
        cs_ref[...] = jnp.zeros_like(cs_ref)
        cs_ref[0:1, :] = cs[0]
        cs_ref[1:2, :] = cs[1]

    full = lambda c0: pl.BlockSpec((None, s, LANES), lambda bi, p, j: (bi, 0, c0 + p))
    kspec = lambda c0: pl.BlockSpec((None, tb, LANES), lambda bi, p, j: (bi, j, c0 + p))
    ko = pl.BlockSpec((None, tb, LANES), lambda bi, p, j: (bi, j, p))
    ctspec = pl.BlockSpec((None, None, None, 8, tb), lambda bi, p, j: (bi, p, j, 0, 0))
    return pl.pallas_call(
        body, name=name, grid=(b, N_HEADS // 2, nq),
        in_specs=[full(q0), full(0), pl.BlockSpec((None, None, s, LANES), lambda bi, p, j: (bi, p, 0, 0)),
                  kspec(k0), kspec(v0), ctspec],
        out_specs=[full(0), ko, ko, ctspec],
        out_shape=[jax.ShapeDtypeStruct((b, s, D_MODEL), F32), jax.ShapeDtypeStruct((b, s, D_MODEL), BF16),
                   jax.ShapeDtypeStruct((b, s, D_MODEL), BF16),
                   jax.ShapeDtypeStruct((b, N_HEADS // 2, nq, 8, tb), F32)],
        compiler_params=_cp(("parallel", "parallel", "arbitrary")),
    )(proj3, do3, stats, proj3, proj3, cum_t)


def _rope(x, cos, sin_signed):
    w = x.shape[1]
    lane = lax.broadcasted_iota(jnp.int32, x.shape, 1)
    first = (lane % HEAD_DIM) < (HEAD_DIM // 2)
    rot = jnp.where(first, pltpu.roll(x, w - HEAD_DIM // 2, 1), pltpu.roll(x, HEAD_DIM // 2, 1))
    return x * cos + rot * sin_signed


_QB = 4
_QROWS = _QB * CHUNK


def _swa_keys(kc_ref, kp_ref, vc_ref, vp_ref, cq_ref, sq_ref, cp_ref, sp_ref):
    cq, sq, cpv, spv = cq_ref[...], sq_ref[...], cp_ref[...], sp_ref[...]
    kc = _rope(kc_ref[...], cq, sq).astype(BF16)
    kp = _rope(kp_ref[...], cpv, spv).astype(BF16)
    return cq, sq, cpv, spv, kc, kp, vc_ref[...].astype(BF16), vp_ref[...].astype(BF16)


def _swa_stack(pairs, lo):
    return jnp.concatenate([jnp.where(lo, pairs[0], 0.0), jnp.where(lo, 0.0, pairs[0]),
                            jnp.where(lo, pairs[1], 0.0), jnp.where(lo, 0.0, pairs[1])], axis=0).astype(BF16)


def _swa_mask4(prev_valid):
    r = lax.broadcasted_iota(jnp.int32, (4 * CHUNK, 2 * CHUNK), 0) & (CHUNK - 1)
    c = lax.broadcasted_iota(jnp.int32, (4 * CHUNK, 2 * CHUNK), 1)
    own = jnp.logical_and(c >= CHUNK, c - CHUNK <= r)
    before = jnp.logical_and(c < CHUNK, c > r)
    if prev_valid is True:
        return jnp.logical_or(own, before)
    return jnp.logical_or(own, jnp.logical_and(before, prev_valid))


def _swa_sink4(skv):
    return jnp.concatenate([jnp.broadcast_to(_col(skv, j), (CHUNK, 1)) for j in range(4)], axis=0)


def _swa_specs(order, q0, z0):
    def spec(shape, fn):
        return pl.BlockSpec(shape, lambda *ids: fn(*order(*ids)))

    prev = lambda i: jnp.maximum(_QB * i - 1, 0)
    return dict(
        q=spec((None, _QROWS, 256), lambda bi, g, i: (bi, i, q0 + g)),
        z=spec((None, _QROWS, 256), lambda bi, g, i: (bi, i, z0 + g)),
        blk=spec((None, _QROWS, 256), lambda bi, g, i: (bi, i, g)),
        kcur=spec((None, _QROWS, LANES), lambda bi, g, i: (bi, i, g)),
        kprev=spec((None, CHUNK, LANES), lambda bi, g, i: (bi, prev(i), g)),
        kstep=spec((None, CHUNK, LANES), lambda bi, g, i: (bi, i, g)),
        tcur=spec((_QROWS, LANES), lambda bi, g, i: (i, 0)),
        tprev=spec((CHUNK, LANES), lambda bi, g, i: (prev(i), 0)),
        sk=spec((None, 1, LANES), lambda bi, g, i: (g, 0, 0)))


def _swa_fwd(proj3, k2, v2, cos, sin, sinks, *, name):
    b, s, _ = proj3.shape
    q0 = _PAD_COLS["b_q"][0] // 256
    z0 = _PAD_COLS["b_z"][0] // 256

    def body(q_ref, z_ref, kc_ref, kp_ref, vc_ref, vp_ref, cq_ref, sq_ref, cp_ref, sp_ref, sk_ref,
             y_ref, o_ref, lse_ref):
        i = pl.program_id(2)
        cq_all, sq_all, _, _, kc_all, kp0, vc_all, vp0 = _swa_keys(
            kc_ref, kp_ref, vc_ref, vp_ref, cq_ref, sq_ref, cp_ref, sp_ref)
        lo = lax.broadcasted_iota(jnp.int32, (CHUNK, LANES), 1) < HEAD_DIM
        sink4 = _swa_sink4(sk_ref[...])
        for u in range(_QB):
            rs = slice(CHUNK * u, CHUNK * (u + 1))
            ps = slice(CHUNK * (u - 1), CHUNK * u)
            cq, sq = cq_all[rs], sq_all[rs]
            kp, vp = (kp0, vp0) if u == 0 else (kc_all[ps], vc_all[ps])
            kk = jnp.concatenate([kp, kc_all[rs]], axis=0)
            vv = jnp.concatenate([vp, vc_all[rs]], axis=0)
            q4 = _swa_stack([_rope(q_ref[rs, LANES * pp:LANES * (pp + 1)], cq, sq) * _SCALE for pp in range(2)], lo)
            sc = jnp.where(_swa_mask4(True if u > 0 else i > 0), _dot_nt(q4, kk), _NEG)
            m = jnp.maximum(jnp.max(sc, axis=1, keepdims=True), sink4)
            pr = jnp.exp(sc - m)
            l = jnp.sum(pr, axis=1, keepdims=True) + jnp.exp(sink4 - m)
            o4 = _dot(pr.astype(BF16), vv) / l
            lse4 = m + jnp.log(l)
            for pp in range(2):
                ls = slice(LANES * pp, LANES * (pp + 1))
                h0 = slice(2 * CHUNK * pp, 2 * CHUNK * pp + CHUNK)
                h1 = slice(2 * CHUNK * pp + CHUNK, 2 * CHUNK * (pp + 1))
                o = jnp.where(lo, o4[h0], o4[h1])
                z = z_ref[rs, ls]
                o_ref[rs, ls] = o
                lse_ref[rs, ls] = jnp.where(lo, lse4[h0], lse4[h1])
                y_ref[rs, ls] = (o * (z * _sigmoid(z))).astype(BF16)

    sp = _swa_specs(lambda bi, g, i: (bi, g, i), q0, z0)
    return pl.pallas_call(
        body, name=name, grid=(b, N_GROUPS, s // _QROWS),
        in_specs=[sp["q"], sp["z"], sp["kcur"], sp["kprev"], sp["kcur"], sp["kprev"],
                  sp["tcur"], sp["tcur"], sp["tprev"], sp["tprev"], sp["sk"]],
        out_specs=[sp["blk"], sp["blk"], sp["blk"]],
        out_shape=[jax.ShapeDtypeStruct((b, s, D_MODEL), BF16)] + [jax.ShapeDtypeStruct((b, s, D_MODEL), F32)] * 2,
        compiler_params=_cp(("parallel", "parallel", "parallel")),
    )(proj3, proj3, k2, k2, v2, v2, cos, sin, cos, sin, sinks)


def _swa_bwd(proj3, k2, v2, cos, sin, sinks, o3, lse3, dy3, *, name):
    b, s, _ = proj3.shape
    q0 = _PAD_COLS["b_q"][0] // 256
    z0 = _PAD_COLS["b_z"][0] // 256

    def body(q_ref, z_ref, kc_ref, kp_ref, vc_ref, vp_ref, cq_ref, sq_ref, cp_ref, sp_ref, sk_ref,
             o_ref, lse_ref, dy_ref, dq_ref, dz_ref, dkc_ref, dkp_ref, dvc_ref, dvp_ref, dsk_ref):
        i = pl.program_id(2)
        first = jnp.logical_and(pl.program_id(1) == 0, i == 0)

        @pl.when(first)
        def _():
            dsk_ref[...] = jnp.zeros_like(dsk_ref)

        cq_all, sq_all, cpv, spv, kc_all, kp0, vc_all, vp0 = _swa_keys(
            kc_ref, kp_ref, vc_ref, vp_ref, cq_ref, sq_ref, cp_ref, sp_ref)
        lo = lax.broadcasted_iota(jnp.int32, (CHUNK, LANES), 1) < HEAD_DIM
        lane1 = lax.broadcasted_iota(jnp.int32, (1, LANES), 1)
        sink4 = _swa_sink4(sk_ref[...])
        zero = jnp.zeros((CHUNK, LANES), F32)
        dks = [zero] * (_QB + 1)
        dvs = [zero] * (_QB + 1)
        dsk_row = jnp.zeros((1, LANES), F32)
        for u in range(_QB):
            rs = slice(CHUNK * u, CHUNK * (u + 1))
            ps = slice(CHUNK * (u - 1), CHUNK * u)
            cq, sq = cq_all[rs], sq_all[rs]
            kp, vp = (kp0, vp0) if u == 0 else (kc_all[ps], vc_all[ps])
            kk = jnp.concatenate([kp, kc_all[rs]], axis=0)
            vv = jnp.concatenate([vp, vc_all[rs]], axis=0)
            q4 = _swa_stack([_rope(q_ref[rs, LANES * pp:LANES * (pp + 1)], cq, sq) * _SCALE for pp in range(2)], lo)
            dos, lses = [], []
            for pp in range(2):
                ls = slice(LANES * pp, LANES * (pp + 1))
                z = z_ref[rs, ls]
                sz = _sigmoid(z)
                dy = dy_ref[rs, ls]
                dos.append(dy * (z * sz))
                dz_ref[rs, ls] = (dy * o_ref[rs, ls] * (sz * (1.0 + z * (1.0 - sz)))).astype(BF16)
                lse = lse_ref[rs, ls]
                lses += [_col(lse, 0), _col(lse, HEAD_DIM)]
            do4 = _swa_stack(dos, lo)
            lse4 = jnp.concatenate(lses, axis=0)
            pr = jnp.exp(jnp.where(_swa_mask4(True if u > 0 else i > 0), _dot_nt(q4, kk), _NEG) - lse4)
            dp = _dot_nt(do4, vv)
            dl = jnp.sum(pr * dp, axis=1, keepdims=True)
            ds = (pr * (dp - dl)).astype(BF16)
            dsink = -jnp.exp(sink4 - lse4) * dl
            for j in range(4):
                dsk_row = dsk_row + jnp.where(
                    lane1 == j, jnp.sum(dsink[CHUNK * j:CHUNK * (j + 1)], axis=0, keepdims=True), 0.0)
            dq4 = _dot(ds, kk)
            dkk = _dot_tn(ds, q4)
            dvv = _dot_tn(pr.astype(BF16), do4)
            dks[u], dks[u + 1] = dks[u] + dkk[:CHUNK], dks[u + 1] + dkk[CHUNK:]
            dvs[u], dvs[u + 1] = dvs[u] + dvv[:CHUNK], dvs[u + 1] + dvv[CHUNK:]
            for pp in range(2):
                h0 = slice(2 * CHUNK * pp, 2 * CHUNK * pp + CHUNK)
                h1 = slice(2 * CHUNK * pp + CHUNK, 2 * CHUNK * (pp + 1))
                dq_ref[rs, LANES * pp:LANES * (pp + 1)] = _rope(
                    jnp.where(lo, dq4[h0], dq4[h1]) * _SCALE, cq, -sq).astype(BF16)
        fold = lambda v: v + pltpu.roll(v, HEAD_DIM, 1)
        dkp_ref[...] = fold(_rope(dks[0], cpv, -spv))
        dvp_ref[...] = fold(dvs[0])
        for u in range(_QB):
            rs = slice(CHUNK * u, CHUNK * (u + 1))
            dkc_ref[rs, :] = fold(_rope(dks[u + 1], cq_all[rs], -sq_all[rs]))
            dvc_ref[rs, :] = fold(dvs[u + 1])
        dsk_ref[...] += dsk_row

    sp = _swa_specs(lambda g, bi, i: (bi, g, i), q0, z0)
    kv_shape = jax.ShapeDtypeStruct((b, s, 512), F32)
    kvp_shape = jax.ShapeDtypeStruct((b, s // _QB, 512), F32)
    return pl.pallas_call(
        body, name=name, grid=(N_GROUPS, b, s // _QROWS),
        in_specs=[sp["q"], sp["z"], sp["kcur"], sp["kprev"], sp["kcur"], sp["kprev"],
                  sp["tcur"], sp["tcur"], sp["tprev"], sp["tprev"], sp["sk"], sp["blk"], sp["blk"], sp["blk"]],
        out_specs=[sp["blk"], sp["blk"], sp["kcur"], sp["kstep"], sp["kcur"], sp["kstep"], sp["sk"]],
        out_shape=[jax.ShapeDtypeStruct((b, s, D_MODEL), BF16), jax.ShapeDtypeStruct((b, s, D_MODEL), BF16),
                   kv_shape, kvp_shape, kv_shape, kvp_shape, jax.ShapeDtypeStruct((N_GROUPS, 1, LANES), F32)],
        compiler_params=_cp(("arbitrary", "arbitrary", "arbitrary")),
    )(proj3, proj3, k2, k2, v2, v2, cos, sin, cos, sin, sinks, o3, lse3, dy3)


def _merge_fwd(proj, br, gb, *, name, tm=256):
    t = proj.shape[0]
    g0 = _PAD_COLS["gates"][0] // D_MODEL

    def body(g_ref, a_ref, b_ref, c_ref, gb_ref, o_ref):
        acc = None
        for i, r in enumerate((a_ref, b_ref, c_ref)):
            gate = _sigmoid(g_ref[:, D_MODEL * i:D_MODEL * (i + 1)] + gb_ref[i:i + 1, :])
            term = gate * r[...]
            acc = term if acc is None else acc + term
        o_ref[...] = acc.astype(BF16)

    row = pl.BlockSpec((tm, D_MODEL), lambda i: (i, 0))
    return pl.pallas_call(
        body, name=name, grid=(t // tm,),
        in_specs=[pl.BlockSpec((tm, 3 * D_MODEL), lambda i: (i, g0)), row, row, row,
                  pl.BlockSpec((3, D_MODEL), lambda i: (0, 0))],
        out_specs=row, out_shape=jax.ShapeDtypeStruct((t, D_MODEL), BF16),
        compiler_params=_cp(("parallel",)),
    )(proj, br[0], br[1], br[2], gb)


def _merge_bwd(proj, br, gb, dm, *, name, tm=256):
    t = proj.shape[0]
    g0 = _PAD_COLS["gates"][0] // D_MODEL

    def body(g_ref, a_ref, b_ref, c_ref, gb_ref, dm_ref, da_ref, db_ref, dc_ref, dg_ref, dgb_ref):
        @pl.when(pl.program_id(0) == 0)
        def _():
            dgb_ref[...] = jnp.zeros_like(dgb_ref)

        dmv = dm_ref[...]
        for i, (r, dr) in enumerate(((a_ref, da_ref), (b_ref, db_ref), (c_ref, dc_ref))):
            gate = _sigmoid(g_ref[:, D_MODEL * i:D_MODEL * (i + 1)] + gb_ref[i:i + 1, :])
            dr[...] = (dmv * gate).astype(BF16)
            dg = dmv * r[...] * gate * (1.0 - gate)
            dg_ref[:, D_MODEL * i:D_MODEL * (i + 1)] = dg.astype(BF16)
            dgb_ref[i:i + 1, :] += jnp.sum(dg, axis=0, keepdims=True)

    row = pl.BlockSpec((tm, D_MODEL), lambda i: (i, 0))
    rowb = jax.ShapeDtypeStruct((t, D_MODEL), BF16)
    return pl.pallas_call(
        body, name=name, grid=(t // tm,),
        in_specs=[pl.BlockSpec((tm, 3 * D_MODEL), lambda i: (i, g0)), row, row, row,
                  pl.BlockSpec((3, D_MODEL), lambda i: (0, 0)), row],
        out_specs=[row, row, row, pl.BlockSpec((tm, 3 * D_MODEL), lambda i: (i, 0)),
                   pl.BlockSpec((8, D_MODEL), lambda i: (0, 0))],
        out_shape=[rowb, rowb, rowb, jax.ShapeDtypeStruct((t, 3 * D_MODEL), BF16),
                   jax.ShapeDtypeStruct((8, D_MODEL), F32)],
        compiler_params=_cp(("arbitrary",)),
    )(proj, br[0], br[1], br[2], gb, dm)


def _rope_tables(s):
    pos = jnp.arange(s, dtype=F32)
    inv_freq = ROPE_THETA ** (-jnp.arange(0, HEAD_DIM, 2, dtype=F32) / HEAD_DIM)
    ang = pos[:, None] * inv_freq[None, :]
    cos, sin = jnp.cos(ang), jnp.sin(ang)
    return jnp.tile(cos, (1, 4)), jnp.tile(jnp.concatenate([-sin, sin], axis=1), (1, 2))


def _dup_kv(proj3, name):
    b, s, _ = proj3.shape
    p0, sz = _PAD_COLS[name]
    kv = proj3[:, :, p0:p0 + sz].reshape(b, s, N_GROUPS, 1, HEAD_DIM)
    return jnp.broadcast_to(kv, (b, s, N_GROUPS, 2, HEAD_DIM)).reshape(b, s, 512)


def _pair_rows(cum, tb):
    b, s, _ = cum.shape
    t = jnp.transpose(cum[:, :, :N_HEADS], (0, 2, 1)).reshape(b, N_HEADS // 2, 2, s // tb, tb)
    return jnp.pad(jnp.transpose(t, (0, 1, 3, 2, 4)), ((0, 0), (0, 0), (0, 0), (0, 6), (0, 0)))


def _layer_params(wl):
    return dict(
        dtb=_group_lanes(wl["dt_bias"]), alog=_group_lanes(wl["a_log"]), dsk=_group_lanes(wl["d_skip"]),
        nw=wl["ssm_norm_w"].reshape(N_GROUPS, 1, 256), sinks=_group_lanes(wl["sinks"]),
        fb=jnp.pad(wl["f_bias"], (0, LANES - N_HEADS)).reshape(1, LANES))


def _layer_fwd(x, wl, tabs, bsz, li, tb):
    t = x.shape[0]
    s = t // bsz
    cos, sin = tabs
    lp = _layer_params(wl)
    n = lambda k: f"l{li}_{k}"
    h, h_t = _rms_fwd(x, wl["norm_w"], name=n("rms_fwd"))
    proj = _mm(h, wl["w_in"], tm=1024, tn=1536, tk=1024, name=n("mm_proj"))
    proj3 = proj.reshape(bsz, s, N_PAD)
    xact3 = _conv_fwd(proj3, wl["conv_w"], wl["conv_b"], name=n("conv_fwd"))
    ya3, ypre3, hst = _ssd_fwd(proj3, xact3, lp["dtb"], lp["alog"], lp["dsk"], lp["nw"], name=n("ssd_fwd"))
    k2, v2 = _dup_kv(proj3, "b_k"), _dup_kv(proj3, "b_v")
    yb3, ob3, lseb3 = _swa_fwd(proj3, k2, v2, cos, sin, lp["sinks"], name=n("swa_fwd"))
    cum = _fgate_fwd(proj3, lp["fb"], name=n("fgate_fwd"))
    cum_t = _pair_rows(cum, tb)
    yc3, oc3, statc3 = _fox_fwd(proj3, cum_t, name=n("fox_fwd"), tb=tb)
    ys = [v.reshape(t, D_MODEL) for v in (ya3, yb3, yc3)]
    br = [_mm(ys[i], wl["w_proj"][i], tm=1024, tn=1024, tk=1024, name=n(f"mm_br{i}")) for i in range(3)]
    merged = _merge_fwd(proj, br, wl["gate_bias"], name=n("merge_fwd"))
    x_new = _mm(merged, wl["w_out"], tm=1024, tn=1024, tk=1024, add=x, name=n("mm_out"))
    saved = dict(x=x, h_t=h_t, proj=proj, xact3=xact3, ypre3=ypre3, hst=hst, k2=k2, v2=v2, ob3=ob3, lseb3=lseb3,
                 cum_t=cum_t, oc3=oc3, statc3=statc3, ys=ys, br=br, merged=merged, lp=lp)
    return x_new, saved


def _layer_bwd(dx, wl, sv, tabs, bsz, li, tb):
    t = dx.shape[0]
    s = t // bsz
    cos, sin = tabs
    lp = sv["lp"]
    n = lambda k: f"l{li}_{k}"
    proj = sv["proj"]
    proj3 = proj.reshape(bsz, s, N_PAD)
    g = {}
    dmerged = _mm(dx, wl["w_out"], tb=True, tm=1024, tn=1024, tk=1024, name=n("mm_dmerged"))
    g["w_out"] = _mm(sv["merged"], dx, ta=True, tm=1024, tn=1024, tk=512, name=n("mm_dwout"))
    dbr0, dbr1, dbr2, dgates, dgb = _merge_bwd(proj, sv["br"], wl["gate_bias"], dmerged, name=n("merge_bwd"))
    g["gate_bias"] = dgb[:3]
    dbr = (dbr0, dbr1, dbr2)
    dys = [_mm(dbr[i], wl["w_proj"][i], tb=True, tm=1024, tn=1024, tk=1024, name=n(f"mm_dy{i}"))
           for i in range(3)]
    g["w_proj"] = jnp.stack([_mm(sv["ys"][i], dbr[i], ta=True, tm=1024, tn=1024, tk=512, name=n(f"mm_dwproj{i}"))
                             for i in range(3)])
    dy3 = [v.reshape(bsz, s, D_MODEL) for v in dys]

    (dact, daz, dadt, ddtb, dalog, ddsk, dnw) = _ssd_bwd(
        proj3, sv["xact3"], lp["dtb"], lp["alog"], lp["dsk"], lp["nw"], sv["ypre3"], sv["hst"], dy3[0],
        name=n("ssd_bwd"))
    g["dt_bias"], g["a_log"], g["d_skip"] = _ungroup_lanes(ddtb), _ungroup_lanes(dalog), _ungroup_lanes(ddsk)
    g["ssm_norm_w"] = dnw.reshape(D_MODEL)
    dxbc, dwb = _conv_bwd(proj3, wl["conv_w"], wl["conv_b"], dact, name=n("conv_bwd"))
    g["conv_w"], g["conv_b"] = dwb[:CONV_WIDTH], dwb[CONV_WIDTH]

    dbq, dbz, dkc, dkp, dvc, dvp, dsk = _swa_bwd(proj3, sv["k2"], sv["v2"], cos, sin, lp["sinks"], sv["ob3"],
                                                 sv["lseb3"], dy3[1], name=n("swa_bwd"))
    g["sinks"] = _ungroup_lanes(dsk)

    def fold(cur, prv):
        p4 = prv.reshape(bsz, s // _QROWS, 1, CHUNK, 512)
        tail = jnp.concatenate([p4[:, 1:], jnp.zeros_like(p4[:, :1])], axis=1)
        shifted = jnp.concatenate([jnp.zeros((bsz, s // _QROWS, _QB - 1, CHUNK, 512), F32), tail], axis=2)
        tot = cur + shifted.reshape(bsz, s, 512)
        return tot.reshape(bsz, s, N_GROUPS, 2, HEAD_DIM)[:, :, :, 0].reshape(bsz, s, 256)

    dbk, dbv = fold(dkc, dkp), fold(dvc, dvp)

    dcz, do3, stats = _fox_prep(proj3, sv["oc3"], sv["statc3"], dy3[2], name=n("fox_prep"))
    dcq, dck, dcv, csum = _fox_bwd(proj3, sv["cum_t"], do3, stats, name=n("fox_bwd"), tb=tb)
    csum = jnp.transpose(csum[:, :, :, :2], (0, 1, 3, 2, 4)).reshape(bsz, N_HEADS, s)
    dcum = -jnp.transpose(csum, (0, 2, 1))
    dcum = jnp.pad(dcum, ((0, 0), (0, 0), (0, LANES - N_HEADS)))
    dcf, dfb = _fgate_bwd(proj3, lp["fb"], dcum, name=n("fgate_bwd"))
    g["f_bias"] = dfb[0, :N_HEADS]

    parts = {"gates": dgates.reshape(bsz, s, 3 * D_MODEL), "xbc": dxbc, "a_z": daz, "b_q": dbq, "b_z": dbz,
             "c_q": dcq, "c_k": dck, "c_v": dcv, "c_z": dcz, "b_k": dbk, "b_v": dbv, "a_dt": dadt, "c_f": dcf}
    dproj = jnp.concatenate([parts[name].astype(BF16) for name, _ in _PAD_ORDER]
                            + [jnp.zeros((bsz, s, N_PAD - N_USED), BF16)], axis=2).reshape(t, N_PAD)
    dh = _mm(dproj, wl["w_in"], tb=True, tm=1024, tn=1024, tk=1536, name=n("mm_dh"))
    g["w_in"] = _unpad_w_in(_mm(sv["h_t"], dproj, tm=1024, tn=768, tk=2048, name=n("mm_dwin")))
    dx_in, dnorm = _rms_bwd(sv["x"], wl["norm_w"], dh, dx, name=n("rms_bwd"))
    g["norm_w"] = dnorm[0]
    return dx_in, g


def _local_step(x, target, wls, final_norm_w, tb=512):
    bsz, s, d = x.shape
    t = bsz * s
    tabs = _rope_tables(s)
    xc = x.reshape(t, d)
    saved = []
    for li, wl in enumerate(wls):
        xc, sv = _layer_fwd(xc, wl, tabs, bsz, li, tb)
        saved.append(sv)
    loss, dx, dfw = _final_loss(xc, final_norm_w, target.reshape(t, d), name="final_loss")
    grads = [None] * len(wls)
    for li in reversed(range(len(wls))):
        dx, grads[li] = _layer_bwd(dx, wls[li], saved[li], tabs, bsz, li, tb)
    return loss[0, 0], dx.reshape(bsz, s, d), grads, dfw[0]


_HBM = pl.BlockSpec(memory_space=pltpu.HBM)


def _chip_peers(x, y):
    return [(1 - x, y), (x, 1 - y), (1 - x, 1 - y)]


def _gather_weights(arrs, *, name):
    n = len(arrs)

    def body(*refs):
        ins, outs = refs[:n], refs[n:2 * n]
        ici_send, ici_recv, d2d_send, d2d_recv = refs[2 * n:]
        x, y, c = lax.axis_index("x"), lax.axis_index("y"), lax.axis_index("c")
        me = 2 * x + y
        peers = _chip_peers(x, y)
        sib = (x, y, 1 - c)
        sends, fwds = [], []
        for a in range(n):
            for k, (px, py) in enumerate(peers):
                cp = pltpu.make_async_remote_copy(
                    src_ref=ins[a].at[c], dst_ref=outs[a].at[me, c], send_sem=ici_send.at[a, k],
                    recv_sem=ici_recv.at[a, k], device_id=(px, py, c), device_id_type=MESH)
                cp.start()
                sends.append(cp)
        for a in range(n):
            for k, (px, py) in enumerate(peers):
                slot = 2 * px + py
                pltpu.make_async_remote_copy(
                    src_ref=ins[a].at[c], dst_ref=outs[a].at[slot, c], send_sem=ici_send.at[a, k],
                    recv_sem=ici_recv.at[a, k], device_id=(px, py, c), device_id_type=MESH).wait_recv()
                fw = pltpu.make_async_remote_copy(
                    src_ref=outs[a].at[slot, c], dst_ref=outs[a].at[slot, c], send_sem=d2d_send.at[a, k],
                    recv_sem=d2d_recv.at[a, k], device_id=sib, device_id_type=MESH)
                fw.start()
                fwds.append(fw)
        for a in range(n):
            for k, (px, py) in enumerate(peers):
                slot = 2 * px + py
                pltpu.make_async_remote_copy(
                    src_ref=outs[a].at[slot, 1 - c], dst_ref=outs[a].at[slot, 1 - c], send_sem=d2d_send.at[a, k],
                    recv_sem=d2d_recv.at[a, k], device_id=sib, device_id_type=MESH).wait_recv()
        for cp in sends + fwds:
            cp.wait_send()

    out_shape = [jax.ShapeDtypeStruct((N_CHIPS,) + a.shape, a.dtype) for a in arrs]
    return pl.pallas_call(
        body, name=name, out_shape=out_shape, in_specs=[_HBM] * n, out_specs=[_HBM] * n,
        scratch_shapes=[pltpu.SemaphoreType.DMA((n, 3)), pltpu.SemaphoreType.DMA((n, 3)),
                        pltpu.SemaphoreType.DMA((n, 3)), pltpu.SemaphoreType.DMA((n, 3))],
    )(*arrs)


def _pair_exchange(arrs, *, name):
    n = len(arrs)

    def body(*refs):
        ins, outs = refs[:n], refs[n:2 * n]
        send, recv = refs[2 * n:]
        x, y, c = lax.axis_index("x"), lax.axis_index("y"), lax.axis_index("c")
        sib = (x, y, 1 - c)
        cps = []
        for a in range(n):
            for k in range(N_CHIPS):
                cp = pltpu.make_async_remote_copy(
                    src_ref=ins[a].at[k, 1 - c], dst_ref=outs[a].at[k], send_sem=send.at[a, k],
                    recv_sem=recv.at[a, k], device_id=sib, device_id_type=MESH)
                cp.start()
                cps.append(cp)
        for cp in cps:
            cp.wait()

    out_shape = [jax.ShapeDtypeStruct((N_CHIPS,) + a.shape[2:], a.dtype) for a in arrs]
    return pl.pallas_call(
        body, name=name, out_shape=out_shape, in_specs=[_HBM] * n, out_specs=[_HBM] * n,
        scratch_shapes=[pltpu.SemaphoreType.DMA((n, N_CHIPS)), pltpu.SemaphoreType.DMA((n, N_CHIPS))],
    )(*arrs)


def _chip_exchange(arrs, *, name):
    n = len(arrs)

    def body(*refs):
        ins, outs = refs[:n], refs[n:2 * n]
        send, recv = refs[2 * n:]
        x, y, c = lax.axis_index("x"), lax.axis_index("y"), lax.axis_index("c")
        me = 2 * x + y
        peers = _chip_peers(x, y)
        cps = []
        for a in range(n):
            for k, (px, py) in enumerate(peers):
                cp = pltpu.make_async_remote_copy(
                    src_ref=ins[a].at[2 * px + py], dst_ref=outs[a].at[me], send_sem=send.at[a, k],
                    recv_sem=recv.at[a, k], device_id=(px, py, c), device_id_type=MESH)
                cp.start()
                cps.append(cp)
        for a in range(n):
            for k, (px, py) in enumerate(peers):
                pltpu.make_async_remote_copy(
                    src_ref=ins[a].at[2 * px + py], dst_ref=outs[a].at[2 * px + py], send_sem=send.at[a, k],
                    recv_sem=recv.at[a, k], device_id=(px, py, c), device_id_type=MESH).wait_recv()
        for cp in cps:
            cp.wait_send()

    out_shape = [jax.ShapeDtypeStruct(a.shape, a.dtype) for a in arrs]
    return pl.pallas_call(
        body, name=name, out_shape=out_shape, in_specs=[_HBM] * n, out_specs=[_HBM] * n,
        scratch_shapes=[pltpu.SemaphoreType.DMA((n, 3)), pltpu.SemaphoreType.DMA((n, 3))],
    )(*arrs)


def _pair_share(arrs, *, name):
    n = len(arrs)

    def body(*refs):
        ins, outs = refs[:n], refs[n:2 * n]
        send, recv = refs[2 * n:]
        x, y, c = lax.axis_index("x"), lax.axis_index("y"), lax.axis_index("c")
        sib = (x, y, 1 - c)
        cps = []
        for a in range(n):
            cp = pltpu.make_async_remote_copy(
                src_ref=ins[a], dst_ref=outs[a], send_sem=send.at[a], recv_sem=recv.at[a],
                device_id=sib, device_id_type=MESH)
            cp.start()
            cps.append(cp)
        for cp in cps:
            cp.wait()

    out_shape = [jax.ShapeDtypeStruct(a.shape, a.dtype) for a in arrs]
    return pl.pallas_call(
        body, name=name, out_shape=out_shape, in_specs=[_HBM] * n, out_specs=[_HBM] * n,
        scratch_shapes=[pltpu.SemaphoreType.DMA((n,)), pltpu.SemaphoreType.DMA((n,))],
    )(*arrs)


def _allreduce_small(buf, *, name):
    r = buf.shape[0]

    def body(in_ref, out_ref, land, send, recv):
        x, y, c = lax.axis_index("x"), lax.axis_index("y"), lax.axis_index("c")
        me = 4 * x + 2 * y + c
        land[me] = in_ref[...]
        cps = []
        for k in range(1, N_DEV):
            px, py, pc = x ^ ((k >> 2) & 1), y ^ ((k >> 1) & 1), c ^ (k & 1)
            cp = pltpu.make_async_remote_copy(
                src_ref=in_ref, dst_ref=land.at[me], send_sem=send.at[k - 1], recv_sem=recv.at[k - 1],
                device_id=(px, py, pc), device_id_type=MESH)
            cp.start()
            cps.append(cp)
        for k in range(1, N_DEV):
            px, py, pc = x ^ ((k >> 2) & 1), y ^ ((k >> 1) & 1), c ^ (k & 1)
            pltpu.make_async_remote_copy(
                src_ref=in_ref, dst_ref=land.at[4 * px + 2 * py + pc], send_sem=send.at[k - 1],
                recv_sem=recv.at[k - 1], device_id=(px, py, pc), device_id_type=MESH).wait_recv()
        for cp in cps:
            cp.wait_send()
        acc = land[0]
        for k in range(1, N_DEV):
            acc = acc + land[k]
        out_ref[...] = acc

    vm = pl.BlockSpec(memory_space=pltpu.VMEM)
    return pl.pallas_call(
        body, name=name, out_shape=jax.ShapeDtypeStruct((r, LANES), F32), in_specs=[vm], out_specs=vm,
        scratch_shapes=[pltpu.VMEM((N_DEV, r, LANES), F32), pltpu.SemaphoreType.DMA((N_DEV - 1,)),
                        pltpu.SemaphoreType.DMA((N_DEV - 1,))],
    )(buf)


def _rows2d(a):
    return a.reshape(-1, a.shape[-1])


def _row_tile(rows, cols, n_arrays, budget=20 * 1024 * 1024):
    best = 8 if rows % 8 == 0 else rows
    tr = 8
    while tr <= rows:
        if rows % tr == 0 and tr * cols * 4 * n_arrays * 2 <= budget:
            best = tr
        tr *= 2
    return best


def _add_slot_layer(full, other, *, name):
    _, _, r, cdim = full.shape
    tr = _row_tile(r, cdim, 4)

    def body(c_ref, a_ref, b_ref, o_ref, ob_ref):
        sm = a_ref[...] + b_ref[...]
        o_ref[...] = sm
        ob_ref[...] = sm.astype(BF16)

    c = lax.axis_index("c").astype(jnp.int32).reshape(1)
    blk = pl.BlockSpec((None, tr, cdim), lambda k, i, c_ref: (k, i, 0))
    return pl.pallas_call(
        body, name=name,
        grid_spec=pltpu.PrefetchScalarGridSpec(
            num_scalar_prefetch=1, grid=(N_CHIPS, r // tr),
            in_specs=[pl.BlockSpec((None, None, tr, cdim), lambda k, i, c_ref: (k, c_ref[0], i, 0)), blk],
            out_specs=[blk, blk]),
        out_shape=[jax.ShapeDtypeStruct((N_CHIPS, r, cdim), F32), jax.ShapeDtypeStruct((N_CHIPS, r, cdim), BF16)],
        compiler_params=_cp(("parallel", "parallel")),
    )(c, full, other)


def _sum_slots(parts, pair, *, name):
    _, r, cdim = parts.shape
    tr = _row_tile(r, cdim, 5)

    def body(me_ref, p_ref, own_ref, o_ref):
        me = me_ref[0]
        acc = None
        for k in range(N_CHIPS):
            term = jnp.where(me == k, own_ref[...], p_ref[k].astype(F32))
            acc = term if acc is None else acc + term
        o_ref[...] = acc

    me = (2 * lax.axis_index("x") + lax.axis_index("y")).astype(jnp.int32).reshape(1)
    return pl.pallas_call(
        body, name=name,
        grid_spec=pltpu.PrefetchScalarGridSpec(
            num_scalar_prefetch=1, grid=(r // tr,),
            in_specs=[pl.BlockSpec((N_CHIPS, tr, cdim), lambda i, me_ref: (0, i, 0)),
                      pl.BlockSpec((None, tr, cdim), lambda i, me_ref: (me_ref[0], i, 0))],
            out_specs=pl.BlockSpec((tr, cdim), lambda i, me_ref: (i, 0))),
        out_shape=jax.ShapeDtypeStruct((r, cdim), F32),
        compiler_params=_cp(("parallel",)),
    )(me, parts, pair)


def _adamw(w, g, m, v, *, name):
    r, cdim = w.shape
    tr = _row_tile(r, cdim, 7)
    c1 = 1.0 - ADAM_B1 ** ADAM_STEP
    c2 = 1.0 - ADAM_B2 ** ADAM_STEP

    def body(w_ref, g_ref, m_ref, v_ref, d_ref, nm_ref, nv_ref):
        gv = g_ref[...]
        mn = ADAM_B1 * m_ref[...] + (1.0 - ADAM_B1) * gv
        vn = ADAM_B2 * v_ref[...] + (1.0 - ADAM_B2) * (gv * gv)
        nm_ref[...] = mn
        nv_ref[...] = vn
        d_ref[...] = -ADAM_LR * ((mn / c1) / (jnp.sqrt(vn / c2) + ADAM_EPS) + ADAM_WD * w_ref[...])

    blk = pl.BlockSpec((tr, cdim), lambda i: (i, 0))
    sh = jax.ShapeDtypeStruct((r, cdim), F32)
    return pl.pallas_call(
        body, name=name, grid=(r // tr,), in_specs=[blk] * 4, out_specs=[blk] * 3, out_shape=[sh] * 3,
        compiler_params=_cp(("parallel",)),
    )(w, g, m, v)


_SMALL = ("norm_w", "conv_b", "dt_bias", "a_log", "d_skip", "ssm_norm_w", "sinks", "f_bias", "final_norm_w",
          "conv_w", "gate_bias")


def _pack(vals):
    flat = jnp.concatenate([v.reshape(-1) for v in vals])
    rows = -(-flat.shape[0] // LANES)
    rows = -(-rows // 8) * 8
    return jnp.pad(flat, (0, rows * LANES - flat.shape[0])).reshape(rows, LANES)


def _unpack(buf, shapes):
    flat = buf.reshape(-1)
    out, off = [], 0
    for sh in shapes:
        sz = int(np.prod(sh))
        out.append(flat[off:off + sz].reshape(sh))
        off += sz
    return out


def kernel(x, norm_w, w_in, conv_w, conv_b, dt_bias, a_log, d_skip, ssm_norm_w, sinks, f_bias, gate_bias, w_proj, w_out, final_norm_w, loss_target, m_norm_w, m_w_in, m_conv_w, m_conv_b, m_dt_bias, m_a_log, m_d_skip, m_ssm_norm_w, m_sinks, m_f_bias, m_gate_bias, m_w_proj, m_w_out, m_final_norm_w, v_norm_w, v_w_in, v_conv_w, v_conv_b, v_dt_bias, v_a_log, v_d_skip, v_ssm_norm_w, v_sinks, v_f_bias, v_gate_bias, v_w_proj, v_w_out, v_final_norm_w):
    depth = w_in.shape[0]
    chip = 2 * lax.axis_index("x") + lax.axis_index("y")

    own = [w_in.astype(BF16), w_proj.astype(BF16), w_out.astype(BF16), conv_w, gate_bias]
    gathered = _gather_weights(own, name="gather_weights")

    def whole(a, li, axis):
        return jnp.concatenate([jnp.where(chip == k, own[a][li], gathered[a][k, li]) for k in range(N_CHIPS)],
                               axis=axis)

    wls = []
    for li in range(depth):
        wls.append(dict(
            norm_w=norm_w[li], w_in=_pad_w_in(whole(0, li, 1)),
            conv_w=whole(3, li, 1), conv_b=conv_b[li], dt_bias=dt_bias[li], a_log=a_log[li], d_skip=d_skip[li],
            ssm_norm_w=ssm_norm_w[li], sinks=sinks[li], f_bias=f_bias[li], gate_bias=whole(4, li, 1),
            w_proj=whole(1, li, 1),
            w_out=whole(2, li, 0)))

    loss_part, grad_x, grads, d_final = _local_step(x, loss_target, wls, final_norm_w)
    loss = lax.psum(loss_part, ("x", "y", "c"))

    c_in = w_in.shape[2]
    r_proj = w_proj.shape[2]
    r_out = w_out.shape[1]
    full_in = jnp.stack([jnp.stack([grads[li]["w_in"][:, k * c_in:(k + 1) * c_in] for li in range(depth)])
                         for k in range(N_CHIPS)])
    full_proj = jnp.stack([jnp.stack([grads[li]["w_proj"][:, k * r_proj:(k + 1) * r_proj].reshape(-1, D_MODEL)
                                      for li in range(depth)]) for k in range(N_CHIPS)])
    full_out = jnp.stack([jnp.stack([grads[li]["w_out"][k * r_out:(k + 1) * r_out] for li in range(depth)])
                          for k in range(N_CHIPS)])
    fulls = [full_in, full_proj, full_out]
    others = _pair_exchange(fulls, name="grad_pair_exchange")
    pair = [_add_slot_layer(f, o, name=f"grad_pair_add{i}") for i, (f, o) in enumerate(zip(fulls, others))]
    parts = _chip_exchange([p[1] for p in pair], name="grad_chip_exchange")
    mine = [_sum_slots(p, pr[0], name=f"grad_slot_sum{i}") for i, (p, pr) in enumerate(zip(parts, pair))]
    theirs = _pair_share(mine, name="grad_pair_share")
    core = lax.axis_index("c")
    red_in, red_proj, red_out = [jnp.stack([jnp.where(core == li, m, t) for li in range(depth)])
                                 for m, t in zip(mine, theirs)]
    grad_w_in = red_in
    grad_w_proj = red_proj.reshape(w_proj.shape)
    grad_w_out = red_out

    small_full = {
        "norm_w": jnp.stack([g["norm_w"] for g in grads]), "conv_b": jnp.stack([g["conv_b"] for g in grads]),
        "dt_bias": jnp.stack([g["dt_bias"] for g in grads]), "a_log": jnp.stack([g["a_log"] for g in grads]),
        "d_skip": jnp.stack([g["d_skip"] for g in grads]),
        "ssm_norm_w": jnp.stack([g["ssm_norm_w"] for g in grads]),
        "sinks": jnp.stack([g["sinks"] for g in grads]), "f_bias": jnp.stack([g["f_bias"] for g in grads]),
        "final_norm_w": d_final,
        "conv_w": jnp.stack([g["conv_w"] for g in grads]), "gate_bias": jnp.stack([g["gate_bias"] for g in grads])}
    shapes = [small_full[k].shape for k in _SMALL]
    summed = _unpack(_allreduce_small(_pack([small_full[k] for k in _SMALL]), name="allreduce_small"), shapes)
    gsmall = dict(zip(_SMALL, summed))
    gsmall["conv_w"] = lax.dynamic_slice_in_dim(gsmall["conv_w"], chip * conv_w.shape[2], conv_w.shape[2], axis=2)
    gsmall["gate_bias"] = lax.dynamic_slice_in_dim(gsmall["gate_bias"], chip * gate_bias.shape[2],
                                                   gate_bias.shape[2], axis=2)

    w_small = dict(norm_w=norm_w, conv_b=conv_b, dt_bias=dt_bias, a_log=a_log, d_skip=d_skip,
                   ssm_norm_w=ssm_norm_w, sinks=sinks, f_bias=f_bias, final_norm_w=final_norm_w, conv_w=conv_w,
                   gate_bias=gate_bias)
    m_small = dict(norm_w=m_norm_w, conv_b=m_conv_b, dt_bias=m_dt_bias, a_log=m_a_log, d_skip=m_d_skip,
                   ssm_norm_w=m_ssm_norm_w, sinks=m_sinks, f_bias=m_f_bias, final_norm_w=m_final_norm_w,
                   conv_w=m_conv_w, gate_bias=m_gate_bias)
    v_small = dict(norm_w=v_norm_w, conv_b=v_conv_b, dt_bias=v_dt_bias, a_log=v_a_log, d_skip=v_d_skip,
                   ssm_norm_w=v_ssm_norm_w, sinks=v_sinks, f_bias=v_f_bias, final_norm_w=v_final_norm_w,
                   conv_w=v_conv_w, gate_bias=v_gate_bias)
    sshapes = [w_small[k].shape for k in _SMALL]
    ds, ms, vs = _adamw(_pack([w_small[k] for k in _SMALL]), _pack([gsmall[k] for k in _SMALL]),
                        _pack([m_small[k] for k in _SMALL]), _pack([v_small[k] for k in _SMALL]), name="adamw_small")
    delta = dict(zip(_SMALL, _unpack(ds, sshapes)))
    new_m = dict(zip(_SMALL, _unpack(ms, sshapes)))
    new_v = dict(zip(_SMALL, _unpack(vs, sshapes)))
    grad = dict(gsmall)
    for nm, w, g, m, v in (("w_in", w_in, grad_w_in, m_w_in, v_w_in),
                           ("w_proj", w_proj, grad_w_proj, m_w_proj, v_w_proj),
                           ("w_out", w_out, grad_w_out, m_w_out, v_w_out)):
        d2, m2, v2 = _adamw(_rows2d(w), _rows2d(g), _rows2d(m), _rows2d(v), name=f"adamw_{nm}")
        grad[nm] = g
        delta[nm], new_m[nm], new_v[nm] = d2.reshape(w.shape), m2.reshape(w.shape), v2.reshape(w.shape)

    order = ("norm_w", "w_in", "conv_w", "conv_b", "dt_bias", "a_log", "d_skip", "ssm_norm_w", "sinks", "f_bias",
             "gate_bias", "w_proj", "w_out", "final_norm_w")
    return (loss, grad_x, *[grad[k] for k in order], *[delta[k] for k in order],
            *[new_m[k] for k in order], *[new_v[k] for k in order])
```

```python
import functools
import math

import numpy as np
import jax
import jax.numpy as jnp
from jax import lax
from jax.experimental import pallas as pl
from jax.experimental.pallas import tpu as pltpu

F32 = jnp.float32
BF16 = jnp.bfloat16
HIGHEST = lax.Precision.HIGHEST
MESH = pl.DeviceIdType.MESH

D_MODEL = 1024
HEAD_DIM = 64
N_HEADS = 16
N_GROUPS = 4
SSM_STATE = 128
CHUNK = 128
CONV_WIDTH = 4
CONV_DIM = 2048
ROPE_THETA = 10000.0
NORM_EPS = 1e-6
LANES = 128
N_CHIPS = 4
N_DEV = 8

ADAM_LR = 0.001
ADAM_B1 = 0.9
ADAM_B2 = 0.999
ADAM_EPS = 1e-08
ADAM_WD = 0.01
ADAM_STEP = 10

_REF_COLS = {}
_off = 0
for _n, _s in (("xbc", 2048), ("a_z", 1024), ("a_dt", 16), ("b_q", 1024), ("b_k", 256), ("b_v", 256),
               ("b_z", 1024), ("c_q", 1024), ("c_k", 1024), ("c_v", 1024), ("c_f", 16), ("c_z", 1024),
               ("gates", 3072)):
    _REF_COLS[_n] = (_off, _s)
    _off += _s
N_IN = _off

_PAD_ORDER = (("gates", 3072), ("xbc", 2048), ("a_z", 1024), ("b_q", 1024), ("b_z", 1024), ("c_q", 1024),
              ("c_k", 1024), ("c_v", 1024), ("c_z", 1024), ("b_k", 256), ("b_v", 256), ("a_dt", 512),
              ("c_f", 128))
_PAD_COLS = {}
_off = 0
for _n, _s in _PAD_ORDER:
    _PAD_COLS[_n] = (_off, _s)
    _off += _s
N_USED = _off
N_PAD = 13824


def _cp(sem, vmem_mb=48):
    return pltpu.CompilerParams(dimension_semantics=sem, vmem_limit_bytes=vmem_mb * 1024 * 1024)


def _dot(a, b, dims=((1,), (0,)), precision=None):
    return lax.dot_general(a, b, (dims, ((), ())), preferred_element_type=F32, precision=precision)


def _dot_nt(a, b):
    return _dot(a, b, ((1,), (1,)))


def _dot_tn(a, b):
    return _dot(a, b, ((0,), (0,)))


def _col(v, idx):
    lane = lax.broadcasted_iota(jnp.int32, v.shape, 1)
    return jnp.sum(jnp.where(lane == idx, v, 0.0), axis=1, keepdims=True)


def _row(v, idx):
    row = lax.broadcasted_iota(jnp.int32, v.shape, 0)
    return jnp.sum(jnp.where(row == idx, v, 0.0), axis=0, keepdims=True)


def _iota_col():
    return lax.broadcasted_iota(jnp.int32, (CHUNK, 1), 0)


def _iota_row():
    return lax.broadcasted_iota(jnp.int32, (1, LANES), 1)


def _sigmoid(x):
    return 1.0 / (1.0 + jnp.exp(-x))


def _softplus(x):
    return jnp.maximum(x, 0.0) + jnp.log(1.0 + jnp.exp(-jnp.abs(x)))


def _pad_w_in(w):
    parts = []
    for name, size in _PAD_ORDER:
        s0, sz = _REF_COLS[name]
        seg = w[:, s0:s0 + sz]
        if name == "a_dt":
            seg = jnp.pad(seg.reshape(-1, N_GROUPS, 4), ((0, 0), (0, 0), (0, LANES - 4))).reshape(-1, 512)
        elif name == "c_f":
            seg = jnp.pad(seg, ((0, 0), (0, LANES - 16)))
        parts.append(seg)
    parts.append(jnp.zeros((w.shape[0], N_PAD - N_USED), w.dtype))
    return jnp.concatenate(parts, axis=1)


def _unpad_w_in(wp):
    segs = {}
    for name, _ in _PAD_ORDER:
        p0, psz = _PAD_COLS[name]
        seg = wp[:, p0:p0 + psz]
        if name == "a_dt":
            seg = seg.reshape(-1, N_GROUPS, LANES)[:, :, :4].reshape(-1, 16)
        elif name == "c_f":
            seg = seg[:, :16]
        segs[name] = seg
    order = sorted(_REF_COLS, key=lambda n: _REF_COLS[n][0])
    return jnp.concatenate([segs[n] for n in order], axis=1)


def _group_lanes(v):
    return jnp.pad(v.reshape(N_GROUPS, 1, 4), ((0, 0), (0, 0), (0, LANES - 4)))


def _ungroup_lanes(v):
    return v[:, 0, :4].reshape(16)


def _mm(a, b, *, ta=False, tb=False, tm=512, tn=512, tk=512, out_dtype=F32, add=None, name):
    if ta:
        kdim, m = a.shape
    else:
        m, kdim = a.shape
    if tb:
        n, k2 = b.shape
    else:
        k2, n = b.shape
    assert kdim == k2, (a.shape, b.shape)
    tm, tn, tk = min(tm, m), min(tn, n), min(tk, kdim)
    assert m % tm == 0 and n % tn == 0 and kdim % tk == 0, (m, n, kdim, tm, tn, tk)
    nk = kdim // tk
    a_spec = (pl.BlockSpec((tk, tm), lambda i, j, k: (k, i)) if ta
              else pl.BlockSpec((tm, tk), lambda i, j, k: (i, k)))
    b_spec = (pl.BlockSpec((tn, tk), lambda i, j, k: (j, k)) if tb
              else pl.BlockSpec((tk, tn), lambda i, j, k: (k, j)))
    dims = ((0 if ta else 1,), (1 if tb else 0,))
    has_add = add is not None

    def body(*refs):
        if has_add:
            a_ref, b_ref, add_ref, o_ref, acc_ref = refs
        else:
            a_ref, b_ref, o_ref, acc_ref = refs
        k = pl.program_id(2)
        p = _dot(a_ref[...].astype(BF16), b_ref[...].astype(BF16), dims)

        @pl.when(k == 0)
        def _():
            acc_ref[...] = p

        @pl.when(k > 0)
        def _():
            acc_ref[...] += p

        @pl.when(k == nk - 1)
        def _():
            r = acc_ref[...]
            if has_add:
                r = r + add_ref[...]
            o_ref[...] = r.astype(out_dtype)

    in_specs = [a_spec, b_spec]
    args = [a, b]
    if has_add:
        in_specs.append(pl.BlockSpec((tm, tn), lambda i, j, k: (i, j)))
        args.append(add)
    return pl.pallas_call(
        body, name=name, grid=(m // tm, n // tn, nk),
        in_specs=in_specs, out_specs=pl.BlockSpec((tm, tn), lambda i, j, k: (i, j)),
        out_shape=jax.ShapeDtypeStruct((m, n), out_dtype),
        scratch_shapes=[pltpu.VMEM((tm, tn), F32)],
        compiler_params=_cp(("parallel", "parallel", "arbitrary")),
    )(*args)


def _rms_fwd(x, w, *, name, tm=512):
    t, d = x.shape

    def body(x_ref, w_ref, o_ref, ot_ref):
        xv = x_ref[...]
        r = lax.rsqrt(jnp.mean(xv * xv, axis=1, keepdims=True) + NORM_EPS)
        h = xv * r * w_ref[...]
        o_ref[...] = h.astype(BF16)
        ot_ref[...] = h.T.astype(BF16)

    return pl.pallas_call(
        body, name=name, grid=(t // tm,),
        in_specs=[pl.BlockSpec((tm, d), lambda i: (i, 0)), pl.BlockSpec((1, d), lambda i: (0, 0))],
        out_specs=[pl.BlockSpec((tm, d), lambda i: (i, 0)), pl.BlockSpec((d, tm), lambda i: (0, i))],
        out_shape=[jax.ShapeDtypeStruct((t, d), BF16), jax.ShapeDtypeStruct((d, t), BF16)],
        compiler_params=_cp(("parallel",)),
    )(x, w.reshape(1, d))


def _rms_bwd(x, w, dh, dres, *, name, tm=512):
    t, d = x.shape

    def body(x_ref, w_ref, dh_ref, dres_ref, dx_ref, dw_ref):
        xv = x_ref[...]
        r = lax.rsqrt(jnp.mean(xv * xv, axis=1, keepdims=True) + NORM_EPS)
        xhat = xv * r
        dhv = dh_ref[...]
        dxhat = dhv * w_ref[...]
        dx = r * (dxhat - xhat * jnp.mean(dxhat * xhat, axis=1, keepdims=True))
        dx_ref[...] = dres_ref[...] + dx

        @pl.when(pl.program_id(0) == 0)
        def _():
            dw_ref[...] = jnp.zeros_like(dw_ref)

        dw_ref[...] += jnp.sum(dhv * xhat, axis=0, keepdims=True)

    return pl.pallas_call(
        body, name=name, grid=(t // tm,),
        in_specs=[pl.BlockSpec((tm, d), lambda i: (i, 0)), pl.BlockSpec((1, d), lambda i: (0, 0)),
                  pl.BlockSpec((tm, d), lambda i: (i, 0)), pl.BlockSpec((tm, d), lambda i: (i, 0))],
        out_specs=[pl.BlockSpec((tm, d), lambda i: (i, 0)), pl.BlockSpec((1, d), lambda i: (0, 0))],
        out_shape=[jax.ShapeDtypeStruct((t, d), F32), jax.ShapeDtypeStruct((1, d), F32)],
        compiler_params=_cp(("arbitrary",)),
    )(x, w.reshape(1, d), dh, dres)


def _final_loss(x, w, target, *, name, tm=512):
    t, d = x.shape

    def body(x_ref, w_ref, t_ref, loss_ref, dx_ref, dw_ref):
        xv = x_ref[...]
        wv = w_ref[...]
        r = lax.rsqrt(jnp.mean(xv * xv, axis=1, keepdims=True) + NORM_EPS)
        xhat = xv * r
        err = xhat * wv - t_ref[...]
        dy = err * (1.0 / d)
        dxhat = dy * wv
        dx_ref[...] = r * (dxhat - xhat * jnp.mean(dxhat * xhat, axis=1, keepdims=True))

        @pl.when(pl.program_id(0) == 0)
        def _():
            dw_ref[...] = jnp.zeros_like(dw_ref)
            loss_ref[...] = jnp.zeros_like(loss_ref)

        dw_ref[...] += jnp.sum(dy * xhat, axis=0, keepdims=True)
        part = 0.5 * jnp.sum(jnp.mean(err * err, axis=1, keepdims=True), axis=0, keepdims=True)
        loss_ref[...] += jnp.broadcast_to(part, loss_ref.shape)

    return pl.pallas_call(
        body, name=name, grid=(t // tm,),
        in_specs=[pl.BlockSpec((tm, d), lambda i: (i, 0)), pl.BlockSpec((1, d), lambda i: (0, 0)),
                  pl.BlockSpec((tm, d), lambda i: (i, 0))],
        out_specs=[pl.BlockSpec((8, LANES), lambda i: (0, 0)), pl.BlockSpec((tm, d), lambda i: (i, 0)),
                   pl.BlockSpec((1, d), lambda i: (0, 0))],
        out_shape=[jax.ShapeDtypeStruct((8, LANES), F32), jax.ShapeDtypeStruct((t, d), F32),
                   jax.ShapeDtypeStruct((1, d), F32)],
        compiler_params=_cp(("arbitrary",)),
    )(x, w.reshape(1, d), target)


_CB = 128


def _conv_pre(u, w_ref, b_ref):
    s = u.shape[0]
    row = lax.broadcasted_iota(jnp.int32, u.shape, 0)
    pre = b_ref[...] + w_ref[CONV_WIDTH - 1:CONV_WIDTH, :] * u
    for sh in range(1, CONV_WIDTH):
        shifted = jnp.where(row >= sh, pltpu.roll(u, sh, 0), 0.0)
        pre = pre + w_ref[CONV_WIDTH - 1 - sh:CONV_WIDTH - sh, :] * shifted
    return pre


def _conv_fwd(proj3, cw, cb, *, name):
    b, s, _ = proj3.shape
    c0 = _PAD_COLS["xbc"][0] // _CB

    def body(u_ref, w_ref, b_ref, o_ref):
        pre = _conv_pre(u_ref[...], w_ref, b_ref)
        o_ref[...] = pre * _sigmoid(pre)

    return pl.pallas_call(
        body, name=name, grid=(b, CONV_DIM // _CB),
        in_specs=[pl.BlockSpec((None, s, _CB), lambda i, j: (i, 0, c0 + j)),
                  pl.BlockSpec((CONV_WIDTH, _CB), lambda i, j: (0, j)),
                  pl.BlockSpec((1, _CB), lambda i, j: (0, j))],
        out_specs=pl.BlockSpec((None, s, _CB), lambda i, j: (i, 0, j)),
        out_shape=jax.ShapeDtypeStruct((b, s, CONV_DIM), F32),
        compiler_params=_cp(("parallel", "parallel")),
    )(proj3, cw, cb.reshape(1, CONV_DIM))


def _conv_bwd(proj3, cw, cb, dact, *, name):
    b, s, _ = proj3.shape
    c0 = _PAD_COLS["xbc"][0] // _CB

    def body(u_ref, w_ref, b_ref, da_ref, du_ref, dwb_ref):
        u = u_ref[...]
        pre = _conv_pre(u, w_ref, b_ref)
        sg = _sigmoid(pre)
        dpre = da_ref[...] * (sg * (1.0 + pre * (1.0 - sg)))
        row = lax.broadcasted_iota(jnp.int32, u.shape, 0)
        du = w_ref[CONV_WIDTH - 1:CONV_WIDTH, :] * dpre
        rows = [jnp.sum(dpre * u, axis=0, keepdims=True)]
        for sh in range(1, CONV_WIDTH):
            fwd_shift = jnp.where(row < s - sh, pltpu.roll(dpre, s - sh, 0), 0.0)
            du = du + w_ref[CONV_WIDTH - 1 - sh:CONV_WIDTH - sh, :] * fwd_shift
            ush = jnp.where(row >= sh, pltpu.roll(u, sh, 0), 0.0)
            rows.append(jnp.sum(dpre * ush, axis=0, keepdims=True))
        du_ref[...] = du.astype(BF16)

        @pl.when(pl.program_id(1) == 0)
        def _():
            dwb_ref[...] = jnp.zeros_like(dwb_ref)

        for sh in range(CONV_WIDTH):
            k = CONV_WIDTH - 1 - sh
            dwb_ref[k:k + 1, :] += rows[sh]
        dwb_ref[CONV_WIDTH:CONV_WIDTH + 1, :] += jnp.sum(dpre, axis=0, keepdims=True)

    return pl.pallas_call(
        body, name=name, grid=(CONV_DIM // _CB, b),
        in_specs=[pl.BlockSpec((None, s, _CB), lambda j, i: (i, 0, c0 + j)),
                  pl.BlockSpec((CONV_WIDTH, _CB), lambda j, i: (0, j)),
                  pl.BlockSpec((1, _CB), lambda j, i: (0, j)),
                  pl.BlockSpec((None, s, _CB), lambda j, i: (i, 0, j))],
        out_specs=[pl.BlockSpec((None, s, _CB), lambda j, i: (i, 0, j)),
                   pl.BlockSpec((8, _CB), lambda j, i: (0, j))],
        out_shape=[jax.ShapeDtypeStruct((b, s, CONV_DIM), BF16), jax.ShapeDtypeStruct((8, CONV_DIM), F32)],
        compiler_params=_cp(("parallel", "arbitrary")),
    )(proj3, cw, cb.reshape(1, CONV_DIM), dact)


def _ssd_common(dt_ref, dtb_ref, alog_ref):
    row = lax.broadcasted_iota(jnp.int32, (CHUNK, CHUNK), 0)
    lane = lax.broadcasted_iota(jnp.int32, (CHUNK, CHUNK), 1)
    causal = row >= lane
    tri = causal.astype(F32)
    dtv = _softplus(dt_ref[...] + dtb_ref[...])
    a_row = -jnp.exp(alog_ref[...])
    acum = _dot(tri, dtv * a_row, precision=HIGHEST)
    return row, lane, causal, dtv, a_row, acum, acum.T


def _ssd_pair(pp, x, dtv, acum, acum_t, causal, lane, row):
    lo = lane < HEAD_DIM
    r0, r1 = 2 * pp, 2 * pp + 1
    dtp = jnp.where(lo, _col(dtv, r0), _col(dtv, r1))
    ac0, ac1 = _col(acum, r0), _col(acum, r1)
    ar0, ar1 = _row(acum_t, r0), _row(acum_t, r1)
    d0 = jnp.where(causal, jnp.exp(jnp.where(causal, ac0 - ar0, 0.0)), 0.0)
    d1 = jnp.where(causal, jnp.exp(jnp.where(causal, ac1 - ar1, 0.0)), 0.0)
    al0, al1 = _col(ar0, CHUNK - 1), _col(ar1, CHUNK - 1)
    eac = jnp.where(lo, jnp.exp(ac0), jnp.exp(ac1))
    dsp = jnp.where(lo, jnp.exp(al0 - ac0), jnp.exp(al1 - ac1))
    eal = jnp.where(_iota_col() < HEAD_DIM, jnp.exp(al0), jnp.exp(al1))
    return lo, dtp, x * dtp, d0, d1, al0, al1, eac, dsp, eal


def _ssd_fwd(proj3, xact3, dtb, alog, dsk, nw, *, name):
    b, s, _ = proj3.shape
    nc = s // CHUNK
    dt0 = _PAD_COLS["a_dt"][0] // 512
    z0 = _PAD_COLS["a_z"][0] // D_MODEL

    def body(xs_ref, bm_ref, cm_ref, dt_ref, z_ref, dtb_ref, alog_ref, dsk_ref, nw_ref,
             ya_ref, ypre_ref, hst_ref, h_scr):
        @pl.when(pl.program_id(1) == 0)
        def _():
            h_scr[...] = jnp.zeros_like(h_scr)

        for g in range(N_GROUPS):
            w256 = pl.ds(256 * g, 256)
            w128 = pl.ds(LANES * g, LANES)
            group(xs_ref.at[:, w256], bm_ref.at[:, w128], cm_ref.at[:, w128], dt_ref.at[:, w128],
                  z_ref.at[:, w256], dtb_ref.at[g], alog_ref.at[g], dsk_ref.at[g], nw_ref.at[g],
                  ya_ref.at[:, w256], ypre_ref.at[:, w256], hst_ref.at[g], h_scr.at[g])

    def group(xs_ref, bm_ref, cm_ref, dt_ref, z_ref, dtb_ref, alog_ref, dsk_ref, nw_ref,
              ya_ref, ypre_ref, hst_ref, h_scr):
        row, lane, causal, dtv, a_row, acum, acum_t = _ssd_common(dt_ref, dtb_ref, alog_ref)
        bb = bm_ref[...].astype(BF16)
        cb = cm_ref[...].astype(BF16)
        cbm = _dot_nt(cb, bb)
        hst_ref[...] = h_scr[...]
        dskv = dsk_ref[...]
        for pp in range(2):
            x = xs_ref[:, LANES * pp:LANES * (pp + 1)]
            lo, dtp, xd, d0, d1, al0, al1, eac, dsp, eal = _ssd_pair(pp, x, dtv, acum, acum_t, causal, lane, row)
            xdb = xd.astype(BF16)
            y = jnp.where(lo, _dot((cbm * d0).astype(BF16), xdb), _dot((cbm * d1).astype(BF16), xdb))
            h = h_scr[pp]
            y = y + eac * _dot_nt(cb, h.astype(BF16))
            h_scr[pp] = h * eal + _dot_tn((xd * dsp).astype(BF16), bb)
            dskp = jnp.where((_iota_row() < HEAD_DIM), _col(dskv, 2 * pp), _col(dskv, 2 * pp + 1))
            ypre_ref[:, LANES * pp:LANES * (pp + 1)] = y + x * dskp
        ypre = ypre_ref[...]
        z = z_ref[...]
        yg = ypre * (z * _sigmoid(z))
        rstd = lax.rsqrt(jnp.sum(yg * yg, axis=1, keepdims=True) * (1.0 / 256.0) + NORM_EPS)
        ya_ref[...] = (yg * rstd * nw_ref[...]).astype(BF16)

    g = N_GROUPS
    par = pl.BlockSpec((g, 1, LANES), lambda i, c: (0, 0, 0))
    wide = pl.BlockSpec((None, CHUNK, D_MODEL), lambda i, c: (i, c, 0))
    return pl.pallas_call(
        body, name=name, grid=(b, nc),
        in_specs=[wide,
                  pl.BlockSpec((None, CHUNK, 512), lambda i, c: (i, c, 2)),
                  pl.BlockSpec((None, CHUNK, 512), lambda i, c: (i, c, 3)),
                  pl.BlockSpec((None, CHUNK, 512), lambda i, c: (i, c, dt0)),
                  pl.BlockSpec((None, CHUNK, D_MODEL), lambda i, c: (i, c, z0)),
                  par, par, par,
                  pl.BlockSpec((g, 1, 256), lambda i, c: (0, 0, 0))],
        out_specs=[wide, wide,
                   pl.BlockSpec((None, None, g, 2, CHUNK, SSM_STATE), lambda i, c: (i, c, 0, 0, 0, 0))],
        out_shape=[jax.ShapeDtypeStruct((b, s, D_MODEL), BF16), jax.ShapeDtypeStruct((b, s, D_MODEL), F32),
                   jax.ShapeDtypeStruct((b, nc, g, 2, CHUNK, SSM_STATE), F32)],
        scratch_shapes=[pltpu.VMEM((g, 2, CHUNK, SSM_STATE), F32)],
        compiler_params=_cp(("parallel", "arbitrary")),
    )(xact3, xact3, xact3, proj3, proj3, dtb, alog, dsk, nw)


def _ssd_bwd(proj3, xact3, dtb, alog, dsk, nw, ypre3, hst, dya3, *, name):
    b, s, _ = proj3.shape
    nc = s // CHUNK
    dt0 = _PAD_COLS["a_dt"][0] // 512
    z0 = _PAD_COLS["a_z"][0] // D_MODEL

    def body(xs_ref, bm_ref, cm_ref, dt_ref, z_ref, dtb_ref, alog_ref, dsk_ref, nw_ref, ypre_ref, hst_ref,
             dya_ref, dact_ref, dz_ref, ddt_ref, ddtb_ref, dalog_ref, ddsk_ref, dnw_ref, dh_scr):
        first = jnp.logical_and(pl.program_id(0) == 0, pl.program_id(1) == 0)

        @pl.when(first)
        def _():
            ddtb_ref[...] = jnp.zeros_like(ddtb_ref)
            dalog_ref[...] = jnp.zeros_like(dalog_ref)
            ddsk_ref[...] = jnp.zeros_like(ddsk_ref)
            dnw_ref[...] = jnp.zeros_like(dnw_ref)

        @pl.when(pl.program_id(1) == 0)
        def _():
            dh_scr[...] = jnp.zeros_like(dh_scr)

        for g in range(N_GROUPS):
            w256 = pl.ds(256 * g, 256)
            w128 = pl.ds(LANES * g, LANES)
            group(xs_ref.at[:, w256], bm_ref.at[:, w128], cm_ref.at[:, w128], dt_ref.at[:, w128],
                  z_ref.at[:, w256], dtb_ref.at[g], alog_ref.at[g], dsk_ref.at[g], nw_ref.at[g],
                  ypre_ref.at[:, w256], hst_ref.at[g], dya_ref.at[:, w256],
                  dact_ref.at[:, w256], dact_ref.at[:, pl.ds(D_MODEL + LANES * g, LANES)],
                  dact_ref.at[:, pl.ds(D_MODEL + 512 + LANES * g, LANES)], dz_ref.at[:, w256], ddt_ref.at[:, w128],
                  ddtb_ref.at[g], dalog_ref.at[g], ddsk_ref.at[g], dnw_ref.at[g], dh_scr.at[g])

    def group(xs_ref, bm_ref, cm_ref, dt_ref, z_ref, dtb_ref, alog_ref, dsk_ref, nw_ref, ypre_ref, hst_ref,
              dya_ref, dxs_ref, dbm_ref, dcm_ref, dz_ref, ddt_ref, ddtb_ref, dalog_ref, ddsk_ref, dnw_ref,
              dh_scr):
        row, lane, causal, dtv, a_row, acum, acum_t = _ssd_common(dt_ref, dtb_ref, alog_ref)
        lane1 = _iota_row()
        bb = bm_ref[...].astype(BF16)
        cb = cm_ref[...].astype(BF16)
        cbm = _dot_nt(cb, bb)

        z = z_ref[...]
        ypre = ypre_ref[...]
        dya = dya_ref[...]
        sz = _sigmoid(z)
        silu = z * sz
        yg = ypre * silu
        rstd = lax.rsqrt(jnp.sum(yg * yg, axis=1, keepdims=True) * (1.0 / 256.0) + NORM_EPS)
        dnw_ref[...] += jnp.sum(dya * yg * rstd, axis=0, keepdims=True)
        dn = dya * nw_ref[...]
        dyg = rstd * dn - yg * (rstd * rstd * rstd * (1.0 / 256.0)) * jnp.sum(dn * yg, axis=1, keepdims=True)
        dz_ref[...] = (dyg * ypre * (sz * (1.0 + z * (1.0 - sz)))).astype(BF16)
        dy_all = dyg * silu

        dskv = dsk_ref[...]
        da_cols = jnp.zeros((CHUNK, LANES), F32)
        dxt_cols = jnp.zeros((CHUNK, LANES), F32)
        ddsk_row = jnp.zeros((1, LANES), F32)
        dcb = jnp.zeros((CHUNK, CHUNK), F32)
        dc = jnp.zeros((CHUNK, SSM_STATE), F32)
        db = jnp.zeros((CHUNK, SSM_STATE), F32)
        last = _iota_col() == CHUNK - 1
        for pp in range(2):
            r0, r1 = 2 * pp, 2 * pp + 1
            x = xs_ref[:, LANES * pp:LANES * (pp + 1)]
            dy = dy_all[:, LANES * pp:LANES * (pp + 1)]
            lo, dtp, xd, d0, d1, al0, al1, eac, dsp, eal = _ssd_pair(pp, x, dtv, acum, acum_t, causal, lane, row)
            w0, w1 = cbm * d0, cbm * d1
            w0b, w1b = w0.astype(BF16), w1.astype(BF16)
            xdb = xd.astype(BF16)
            dyb = dy.astype(BF16)
            h = hst_ref[pp]
            dhn = dh_scr[pp]
            hb = h.astype(BF16)
            dhb = dhn.astype(BF16)
            g0 = _dot_nt(jnp.where(lo, dy, 0.0).astype(BF16), xdb)
            g1 = _dot_nt(jnp.where(lo, 0.0, dy).astype(BF16), xdb)
            dcb = dcb + g0 * d0 + g1 * d1
            m0, m1 = g0 * w0, g1 * w1
            bdh = _dot_nt(bb, dhb)
            dxd = jnp.where(lo, _dot_tn(w0b, dyb), _dot_tn(w1b, dyb)) + dsp * bdh
            ch = _dot_nt(cb, hb)
            edy = eac * dy
            edyb = edy.astype(BF16)
            xds = xd * dsp
            dc = dc + _dot(edyb, hb)
            db = db + _dot(xds.astype(BF16), dhb)
            dh_scr[pp] = dhn * eal + _dot_tn(edyb, cb)
            t2 = edy * ch
            t3 = xds * bdh
            r4 = jnp.sum(dhn * h, axis=1, keepdims=True)
            s4_0 = jnp.sum(jnp.where(_iota_col() < HEAD_DIM, r4, 0.0), axis=0, keepdims=True)
            s4_1 = jnp.sum(r4, axis=0, keepdims=True) - s4_0
            t2_0 = jnp.sum(jnp.where(lo, t2, 0.0), axis=1, keepdims=True)
            t2_1 = jnp.sum(t2, axis=1, keepdims=True) - t2_0
            t3_0 = jnp.sum(jnp.where(lo, t3, 0.0), axis=1, keepdims=True)
            t3_1 = jnp.sum(t3, axis=1, keepdims=True) - t3_0
            dal0 = jnp.sum(t3_0, axis=0, keepdims=True) + jnp.exp(al0) * s4_0
            dal1 = jnp.sum(t3_1, axis=0, keepdims=True) + jnp.exp(al1) * s4_1
            dac0 = (jnp.sum(m0, axis=1, keepdims=True) - jnp.sum(m0.T, axis=1, keepdims=True)
                    + t2_0 - t3_0 + jnp.where(last, dal0, 0.0))
            dac1 = (jnp.sum(m1, axis=1, keepdims=True) - jnp.sum(m1.T, axis=1, keepdims=True)
                    + t2_1 - t3_1 + jnp.where(last, dal1, 0.0))
            da_cols = da_cols + jnp.where(lane == r0, dac0, 0.0) + jnp.where(lane == r1, dac1, 0.0)
            xx = dxd * x
            x0 = jnp.sum(jnp.where(lo, xx, 0.0), axis=1, keepdims=True)
            x1 = jnp.sum(xx, axis=1, keepdims=True) - x0
            dxt_cols = dxt_cols + jnp.where(lane == r0, x0, 0.0) + jnp.where(lane == r1, x1, 0.0)
            dskp = jnp.where((_iota_row() < HEAD_DIM), _col(dskv, r0), _col(dskv, r1))
            dxs_ref[:, LANES * pp:LANES * (pp + 1)] = dxd * dtp + dy * dskp
            yx = jnp.sum(dy * x, axis=0, keepdims=True)
            k0 = jnp.sum(jnp.where((_iota_row() < HEAD_DIM), yx, 0.0), axis=1, keepdims=True)
            k1 = jnp.sum(yx, axis=1, keepdims=True) - k0
            ddsk_row = ddsk_row + jnp.where(lane1 == r0, k0, 0.0) + jnp.where(lane1 == r1, k1, 0.0)
        dcbb = dcb.astype(BF16)
        dcm_ref[...] = dc + _dot(dcbb, bb)
        dbm_ref[...] = db + _dot_tn(dcbb, cb)
        tri_t = (row <= lane).astype(F32)
        dadt = _dot(tri_t, da_cols, precision=HIGHEST)
        ddtv = dadt * a_row + dxt_cols
        dalog_ref[...] += jnp.sum(dadt * dtv, axis=0, keepdims=True) * a_row
        ddt_raw = ddtv * _sigmoid(dt_ref[...] + dtb_ref[...])
        ddt_ref[...] = ddt_raw.astype(BF16)
        ddtb_ref[...] += jnp.sum(ddt_raw, axis=0, keepdims=True)
        ddsk_ref[...] += ddsk_row

    g = N_GROUPS
    rc = lambda c: nc - 1 - c
    par = pl.BlockSpec((g, 1, LANES), lambda i, c: (0, 0, 0))
    parw = pl.BlockSpec((g, 1, 256), lambda i, c: (0, 0, 0))
    wide = pl.BlockSpec((None, CHUNK, D_MODEL), lambda i, c: (i, rc(c), 0))
    blk512 = lambda col: pl.BlockSpec((None, CHUNK, 512), lambda i, c: (i, rc(c), col))
    return pl.pallas_call(
        body, name=name, grid=(b, nc),
        in_specs=[wide, blk512(2), blk512(3), blk512(dt0),
                  pl.BlockSpec((None, CHUNK, D_MODEL), lambda i, c: (i, rc(c), z0)),
                  par, par, par, parw,
                  wide,
                  pl.BlockSpec((None, None, g, 2, CHUNK, SSM_STATE), lambda i, c: (i, rc(c), 0, 0, 0, 0)),
                  wide],
        out_specs=[pl.BlockSpec((None, CHUNK, CONV_DIM), lambda i, c: (i, rc(c), 0)), wide, blk512(0),
                   par, par, par, parw],
        out_shape=[jax.ShapeDtypeStruct((b, s, CONV_DIM), F32), jax.ShapeDtypeStruct((b, s, D_MODEL), BF16),
                   jax.ShapeDtypeStruct((b, s, 512), BF16),
                   jax.ShapeDtypeStruct((g, 1, LANES), F32), jax.ShapeDtypeStruct((g, 1, LANES), F32),
                   jax.ShapeDtypeStruct((g, 1, LANES), F32), jax.ShapeDtypeStruct((g, 1, 256), F32)],
        scratch_shapes=[pltpu.VMEM((g, 2, CHUNK, SSM_STATE), F32)],
        compiler_params=_cp(("arbitrary", "arbitrary")),
    )(xact3, xact3, xact3, proj3, proj3, dtb, alog, dsk, nw, ypre3, hst, dya3)


def _fgate_fwd(proj3, fb, *, name):
    b, s, _ = proj3.shape
    f0 = _PAD_COLS["c_f"][0] // LANES

    def body(f_ref, fb_ref, cum_ref, carry):
        @pl.when(pl.program_id(1) == 0)
        def _():
            carry[...] = jnp.zeros_like(carry)

        row = lax.broadcasted_iota(jnp.int32, (CHUNK, CHUNK), 0)
        lane = lax.broadcasted_iota(jnp.int32, (CHUNK, CHUNK), 1)
        tri = (row >= lane).astype(F32)
        lf = -_softplus(-(f_ref[...] + fb_ref[...]))
        cs = _dot(tri, lf, precision=HIGHEST) + carry[0:1, :]
        cum_ref[...] = cs
        carry[0:1, :] = _row(cs, CHUNK - 1)

    return pl.pallas_call(
        body, name=name, grid=(b, s // CHUNK),
        in_specs=[pl.BlockSpec((None, CHUNK, LANES), lambda i, c: (i, c, f0)),
                  pl.BlockSpec((1, LANES), lambda i, c: (0, 0))],
        out_specs=pl.BlockSpec((None, CHUNK, LANES), lambda i, c: (i, c, 0)),
        out_shape=jax.ShapeDtypeStruct((b, s, LANES), F32),
        scratch_shapes=[pltpu.VMEM((8, LANES), F32)],
        compiler_params=_cp(("parallel", "arbitrary")),
    )(proj3, fb)


def _fgate_bwd(proj3, fb, dcum, *, name):
    b, s, _ = proj3.shape
    nc = s // CHUNK
    f0 = _PAD_COLS["c_f"][0] // LANES

    def body(f_ref, fb_ref, dc_ref, df_ref, dfb_ref, carry):
        first = jnp.logical_and(pl.program_id(0) == 0, pl.program_id(1) == 0)

        @pl.when(first)
        def _():
            dfb_ref[...] = jnp.zeros_like(dfb_ref)

        @pl.when(pl.program_id(1) == 0)
        def _():
            carry[...] = jnp.zeros_like(carry)

        row = lax.broadcasted_iota(jnp.int32, (CHUNK, CHUNK), 0)
        lane = lax.broadcasted_iota(jnp.int32, (CHUNK, CHUNK), 1)
        tri_t = (row <= lane).astype(F32)
        dlf = _dot(tri_t, dc_ref[...], precision=HIGHEST) + carry[0:1, :]
        carry[0:1, :] = _row(dlf, 0)
        df = dlf * _sigmoid(-(f_ref[...] + fb_ref[...]))
        df_ref[...] = df.astype(BF16)
        dfb_ref[...] += jnp.sum(df, axis=0, keepdims=True)

    return pl.pallas_call(
        body, name=name, grid=(b, nc),
        in_specs=[pl.BlockSpec((None, CHUNK, LANES), lambda i, c: (i, nc - 1 - c, f0)),
                  pl.BlockSpec((1, LANES), lambda i, c: (0, 0)),
                  pl.BlockSpec((None, CHUNK, LANES), lambda i, c: (i, nc - 1 - c, 0))],
        out_specs=[pl.BlockSpec((None, CHUNK, LANES), lambda i, c: (i, nc - 1 - c, 0)),
                   pl.BlockSpec((1, LANES), lambda i, c: (0, 0))],
        out_shape=[jax.ShapeDtypeStruct((b, s, LANES), BF16), jax.ShapeDtypeStruct((1, LANES), F32)],
        scratch_shapes=[pltpu.VMEM((8, LANES), F32)],
        compiler_params=_cp(("arbitrary", "arbitrary")),
    )(proj3, fb, dcum)


_SCALE = HEAD_DIM ** -0.5
_NEG = -1e30


def _fox_fwd(proj3, cum_t, *, name, tb):
    b, s, _ = proj3.shape
    nq = s // tb
    assert _ST_MJ + 2 * nq <= LANES
    q0 = _PAD_COLS["c_q"][0] // LANES
    k0 = _PAD_COLS["c_k"][0] // LANES
    v0 = _PAD_COLS["c_v"][0] // LANES
    z0 = _PAD_COLS["c_z"][0] // LANES

    def body(q_ref, k_ref, v_ref, z_ref, cumt_ref, y_ref, o_ref, st_ref):
        i = pl.program_id(2)
        lane = lax.broadcasted_iota(jnp.int32, (tb, LANES), 1)
        lo = lane < HEAD_DIM
        q = q_ref[...] * _SCALE
        qms = (jnp.where(lo, q, 0.0).astype(BF16), jnp.where(lo, 0.0, q).astype(BF16))
        ones_at = (HEAD_DIM, 0)

        def block(j, carry, diagonal):
            ks = pl.ds(pl.multiple_of(j * tb, tb), tb)
            kb = k_ref[ks, :].astype(BF16)
            v = v_ref[ks, :].astype(F32)
            vhs = (jnp.where(lo, v, jnp.where(lane == ones_at[0], 1.0, 0.0)).astype(BF16),
                   jnp.where(lo, jnp.where(lane == ones_at[1], 1.0, 0.0), v).astype(BF16))
            ckv = cumt_ref[j]
            if diagonal:
                row = lax.broadcasted_iota(jnp.int32, (tb, tb), 0)
                col = lax.broadcasted_iota(jnp.int32, (tb, tb), 1)
                mask = row >= col
            ms, ls, acc, st = carry
            new_m, new_l, pvs, alphas = [], [], [], []
            for hh in range(2):
                sc = _dot_nt(qms[hh], kb) - ckv[hh:hh + 1, :]
                if diagonal:
                    sc = jnp.where(mask, sc, _NEG)
                m_new = jnp.maximum(ms[hh], jnp.max(sc, axis=1, keepdims=True))
                alpha = jnp.exp(ms[hh] - m_new)
                pv = _dot(jnp.exp(sc - m_new).astype(BF16), vhs[hh])
                new_l.append(alpha * ls[hh] + _col(pv, ones_at[hh]))
                new_m.append(m_new)
                pvs.append(pv)
                alphas.append(alpha)
                st = jnp.where(lane == _ST_MJ + 2 * j + hh, m_new, st)
            acc = jnp.where(lo, alphas[0] * acc + pvs[0], alphas[1] * acc + pvs[1])
            return (tuple(new_m), tuple(new_l), acc, st)

        neg = jnp.full((tb, 1), _NEG, F32)
        zero = jnp.zeros((tb, 1), F32)
        init = ((neg, neg), (zero, zero), jnp.zeros((tb, LANES), F32), jnp.zeros((tb, LANES), F32))
        carry = lax.fori_loop(0, i, lambda j, c: block(j, c, False), init)
        ms, ls, acc, st = block(i, carry, True)
        o = acc / jnp.where(lo, ls[0], ls[1])
        o_ref[...] = o
        st = jnp.where(lane == _ST_LSE, ms[0] + jnp.log(ls[0]), st)
        st_ref[...] = jnp.where(lane == _ST_LSE + 1, ms[1] + jnp.log(ls[1]), st)
        z = z_ref[...]
        y_ref[...] = (o * (z * _sigmoid(z))).astype(BF16)

    qspec = lambda c0: pl.BlockSpec((None, tb, LANES), lambda bi, p, i: (bi, i, c0 + p))
    kspec = lambda c0: pl.BlockSpec((None, s, LANES), lambda bi, p, i: (bi, 0, c0 + p))
    ospec = pl.BlockSpec((None, tb, LANES), lambda bi, p, i: (bi, i, p))
    return pl.pallas_call(
        body, name=name, grid=(b, N_HEADS // 2, nq),
        in_specs=[qspec(q0), kspec(k0), kspec(v0), qspec(z0),
                  pl.BlockSpec((None, None, nq, 8, tb), lambda bi, p, i: (bi, p, 0, 0, 0))],
        out_specs=[ospec, ospec, ospec],
        out_shape=[jax.ShapeDtypeStruct((b, s, D_MODEL), BF16)] + [jax.ShapeDtypeStruct((b, s, D_MODEL), F32)] * 2,
        compiler_params=_cp(("parallel", "parallel", "arbitrary")),
    )(proj3, proj3, proj3, proj3, cum_t)


_ST_LSE, _ST_DELTA, _ST_MJ = 0, 2, 8


def _fox_prep(proj3, o3, stat3, dy3, *, name, tr=512):
    b, s, _ = proj3.shape
    z0 = _PAD_COLS["c_z"][0] // LANES

    def body(z_ref, o_ref, fst_ref, dy_ref, dz_ref, do_ref, st_ref):
        lane = lax.broadcasted_iota(jnp.int32, (tr, LANES), 1)
        lo = lane < HEAD_DIM
        z = z_ref[...]
        sz = _sigmoid(z)
        dy = dy_ref[...]
        o = o_ref[...]
        do = dy * (z * sz)
        dz_ref[...] = (dy * o * (sz * (1.0 + z * (1.0 - sz)))).astype(BF16)
        do_ref[...] = do
        doo = do.astype(BF16).astype(F32) * o
        st = jnp.where(lane == _ST_DELTA, jnp.sum(jnp.where(lo, doo, 0.0), axis=1, keepdims=True), fst_ref[...])
        st_ref[...] = jnp.where(lane == _ST_DELTA + 1, jnp.sum(jnp.where(lo, 0.0, doo), axis=1, keepdims=True), st)

    ospec = pl.BlockSpec((None, tr, LANES), lambda bi, p, i: (bi, i, p))
    return pl.pallas_call(
        body, name=name, grid=(b, N_HEADS // 2, s // tr),
        in_specs=[pl.BlockSpec((None, tr, LANES), lambda bi, p, i: (bi, i, z0 + p)), ospec, ospec, ospec],
        out_specs=[ospec, ospec, pl.BlockSpec((None, None, tr, LANES), lambda bi, p, i: (bi, p, i, 0))],
        out_shape=[jax.ShapeDtypeStruct((b, s, D_MODEL), BF16), jax.ShapeDtypeStruct((b, s, D_MODEL), F32),
                   jax.ShapeDtypeStruct((b, N_HEADS // 2, s, LANES), F32)],
        compiler_params=_cp(("parallel", "parallel", "parallel")),
    )(proj3, o3, stat3, dy3)


def _fox_bwd(proj3, cum_t, do3, stats, *, name, tb):
    b, s, _ = proj3.shape
    nq = s // tb
    q0 = _PAD_COLS["c_q"][0] // LANES
    k0 = _PAD_COLS["c_k"][0] // LANES
    v0 = _PAD_COLS["c_v"][0] // LANES

    def body(q_ref, do_ref, st_ref, k_ref, v_ref, cumt_ref, dq_ref, dk_ref, dv_ref, cs_ref):
        j = pl.program_id(2)
        lo = lax.broadcasted_iota(jnp.int32, (tb, LANES), 1) < HEAD_DIM

        @pl.when(j == 0)
        def _():
            dq_ref[...] = jnp.zeros_like(dq_ref)

        kb = k_ref[...].astype(BF16)
        vb = v_ref[...].astype(BF16)
        ckv = cumt_ref[...]

        def block(i, carry, diagonal):
            qs = pl.ds(pl.multiple_of(i * tb, tb), tb)
            q = q_ref[qs, :] * _SCALE
            do = do_ref[qs, :]
            st = st_ref[qs, :]
            if diagonal:
                row = lax.broadcasted_iota(jnp.int32, (tb, tb), 0)
                col = lax.broadcasted_iota(jnp.int32, (tb, tb), 1)
                mask = row >= col
            dk, dv, cs = carry
            new_cs, dqs = [], []
            for hh in range(2):
                sel = lo if hh == 0 else jnp.logical_not(lo)
                qm = jnp.where(sel, q, 0.0).astype(BF16)
                dom = jnp.where(sel, do, 0.0)
                sc = _dot_nt(qm, kb) - ckv[hh:hh + 1, :]
                if diagonal:
                    sc = jnp.where(mask, sc, _NEG)
                mj = _col(st, _ST_MJ + 2 * j + hh)
                w = jnp.exp(mj - _col(st, _ST_LSE + hh))
                pb = jnp.exp(sc - mj).astype(BF16)
                ds = (pb.astype(F32) * w) * (_dot_nt(dom.astype(BF16), vb) - _col(st, _ST_DELTA + hh))
                dsb = ds.astype(BF16)
                dv = dv + _dot_tn(pb, (dom * w).astype(BF16))
                dk = dk + _dot_tn(dsb, qm)
                new_cs.append(cs[hh] + jnp.sum(ds, axis=0, keepdims=True))
                dqs.append(_dot(dsb, kb))
            dq_ref[qs, :] += jnp.where(lo, dqs[0], dqs[1]) * _SCALE
            return (dk, dv, tuple(new_cs))

        zrow = jnp.zeros((1, tb), F32)
        init = (jnp.zeros((tb, LANES), F32), jnp.zeros((tb, LANES), F32), (zrow, zrow))
        carry = block(j, init, True)
        dk, dv, cs = lax.fori_loop(j + 1, nq, lambda i, c: block(i, c, False), carry)
        dk_ref[...] = dk.astype(BF16)
        dv_ref[...] = dv.astype(BF16)
        cs_ref[...] = jnp.zeros_like(cs_ref)
        cs_ref[0:1, :] = cs[0]
        cs_ref[1:2, :] = cs[1]

    full = lambda c0: pl.BlockSpec((None, s, LANES), lambda bi, p, j: (bi, 0, c0 + p))
    kspec = lambda c0: pl.BlockSpec((None, tb, LANES), lambda bi, p, j: (bi, j, c0 + p))
    ko = pl.BlockSpec((None, tb, LANES), lambda bi, p, j: (bi, j, p))
    ctspec = pl.BlockSpec((None, None, None, 8, tb), lambda bi, p, j: (bi, p, j, 0, 0))
    return pl.pallas_call(
        body, name=name, grid=(b, N_HEADS // 2, nq),
        in_specs=[full(q0), full(0), pl.BlockSpec((None, None, s, LANES), lambda bi, p, j: (bi, p, 0, 0)),
                  kspec(k0), kspec(v0), ctspec],
        out_specs=[full(0), ko, ko, ctspec],
        out_shape=[jax.ShapeDtypeStruct((b, s, D_MODEL), F32), jax.ShapeDtypeStruct((b, s, D_MODEL), BF16),
                   jax.ShapeDtypeStruct((b, s, D_MODEL), BF16),
                   jax.ShapeDtypeStruct((b, N_HEADS // 2, nq, 8, tb), F32)],
        compiler_params=_cp(("parallel", "parallel", "arbitrary")),
    )(proj3, do3, stats, proj3, proj3, cum_t)


_SR = 40


def _ck_rep(cum):
    b, s, _ = cum.shape
    t = jnp.transpose(cum[:, :, :N_HEADS], (0, 2, 1)).reshape(b, N_HEADS // 2, 2, s, 1)
    return jnp.broadcast_to(t, (b, N_HEADS // 2, 2, s, LANES))


def _foxt_fwd(proj3, ckrep, *, name, tb):
    b, s, _ = proj3.shape
    nq = s // tb
    assert _ST_MJ + 2 * nq <= _SR
    q0 = _PAD_COLS["c_q"][0] // LANES
    k0 = _PAD_COLS["c_k"][0] // LANES
    v0 = _PAD_COLS["c_v"][0] // LANES
    z0 = _PAD_COLS["c_z"][0] // LANES
    rep = tb // LANES

    def body(q_ref, k_ref, v_ref, z_ref, ck_ref, y_ref, o_ref, st_ref):
        i = pl.program_id(2)
        lane = lax.broadcasted_iota(jnp.int32, (tb, LANES), 1)
        lo = lane < HEAD_DIM
        lo_r = lax.broadcasted_iota(jnp.int32, (LANES, tb), 0) < HEAD_DIM
        srow = lax.broadcasted_iota(jnp.int32, (_SR, tb), 0)
        q = q_ref[...] * _SCALE
        qms = (jnp.where(lo, q, 0.0).astype(BF16), jnp.where(lo, 0.0, q).astype(BF16))
        ones_at = (HEAD_DIM, 0)

        def block(j, carry, diagonal):
            ks = pl.ds(pl.multiple_of(j * tb, tb), tb)
            kb = k_ref[ks, :].astype(BF16)
            v = v_ref[ks, :].astype(F32)
            vts = (jnp.where(lo, v, jnp.where(lane == ones_at[0], 1.0, 0.0)).T.astype(BF16),
                   jnp.where(lo, jnp.where(lane == ones_at[1], 1.0, 0.0), v).T.astype(BF16))
            if diagonal:
                key = lax.broadcasted_iota(jnp.int32, (tb, tb), 0)
                qry = lax.broadcasted_iota(jnp.int32, (tb, tb), 1)
                mask = key <= qry
            ms, ls, acc, st = carry
            new_m, new_l, pvs, alphas = [], [], [], []
            for hh in range(2):
                sc = _dot_nt(kb, qms[hh]) - jnp.tile(ck_ref[hh, ks, :], (1, rep))
                if diagonal:
                    sc = jnp.where(mask, sc, _NEG)
                m_new = jnp.maximum(ms[hh], jnp.max(sc, axis=0, keepdims=True))
                alpha = jnp.exp(ms[hh] - m_new)
                pv = _dot(vts[hh], jnp.exp(sc - m_new).astype(BF16))
                rs = _row(pv[ones_at[hh]:ones_at[hh] + 8, :], 0)
                new_l.append(alpha * ls[hh] + rs)
                new_m.append(m_new)
                pvs.append(pv)
                alphas.append(alpha)
                st = jnp.where(srow == _ST_MJ + 2 * j + hh, m_new, st)
            acc = jnp.where(lo_r, alphas[0] * acc + pvs[0], alphas[1] * acc + pvs[1])
            return (tuple(new_m), tuple(new_l), acc, st)

        neg = jnp.full((1, tb), _NEG, F32)
        zero = jnp.zeros((1, tb), F32)
        init = ((neg, neg), (zero, zero), jnp.zeros((LANES, tb), F32), jnp.zeros((_SR, tb), F32))
        carry = lax.fori_loop(0, i, lambda j, c: block(j, c, False), init)
        ms, ls, acc, st = block(i, carry, True)
        o = (acc / jnp.where(lo_r, ls[0], ls[1])).T
        o_ref[...] = o
        st = jnp.where(srow == _ST_LSE, ms[0] + jnp.log(ls[0]), st)
        st_ref[...] = jnp.where(srow == _ST_LSE + 1, ms[1] + jnp.log(ls[1]), st)
        z = z_ref[...]
        y_ref[...] = (o * (z * _sigmoid(z))).astype(BF16)

    qspec = lambda c0: pl.BlockSpec((None, tb, LANES), lambda bi, p, i: (bi, i, c0 + p))
    kspec = lambda c0: pl.BlockSpec((None, s, LANES), lambda bi, p, i: (bi, 0, c0 + p))
    ospec = pl.BlockSpec((None, tb, LANES), lambda bi, p, i: (bi, i, p))
    return pl.pallas_call(
        body, name=name, grid=(b, N_HEADS // 2, nq),
        in_specs=[qspec(q0), kspec(k0), kspec(v0), qspec(z0),
                  pl.BlockSpec((None, None, 2, s, LANES), lambda bi, p, i: (bi, p, 0, 0, 0))],
        out_specs=[ospec, ospec, pl.BlockSpec((None, None, None, _SR, tb), lambda bi, p, i: (bi, p, i, 0, 0))],
        out_shape=[jax.ShapeDtypeStruct((b, s, D_MODEL), BF16), jax.ShapeDtypeStruct((b, s, D_MODEL), F32),
                   jax.ShapeDtypeStruct((b, N_HEADS // 2, nq, _SR, tb), F32)],
        compiler_params=_cp(("parallel", "parallel", "arbitrary")),
    )(proj3, proj3, proj3, proj3, ckrep)


def _foxt_prep(proj3, o3, stat, dy3, *, name, tb):
    b, s, _ = proj3.shape
    nq = s // tb
    z0 = _PAD_COLS["c_z"][0] // LANES

    def body(z_ref, o_ref, fst_ref, dy_ref, dz_ref, do_ref, st_ref):
        z = z_ref[...]
        sz = _sigmoid(z)
        dy = dy_ref[...]
        o = o_ref[...]
        do = dy * (z * sz)
        dz_ref[...] = (dy * o * (sz * (1.0 + z * (1.0 - sz)))).astype(BF16)
        do_ref[...] = do
        doo = do.astype(BF16).astype(F32) * o
        r8 = lax.broadcasted_iota(jnp.int32, (8, LANES), 0)
        l8 = lax.broadcasted_iota(jnp.int32, (8, LANES), 1)
        pick = jnp.logical_or(jnp.logical_and(r8 == 0, l8 < HEAD_DIM),
                              jnp.logical_and(r8 == 1, l8 >= HEAD_DIM)).astype(F32)
        d8 = _dot(pick, doo, ((1,), (1,)), precision=HIGHEST)
        srow = lax.broadcasted_iota(jnp.int32, (_SR, tb), 0)
        st = jnp.where(srow == _ST_DELTA, _row(d8, 0), fst_ref[...])
        st_ref[...] = jnp.where(srow == _ST_DELTA + 1, _row(d8, 1), st)

    ospec = pl.BlockSpec((None, tb, LANES), lambda bi, p, i: (bi, i, p))
    sspec = pl.BlockSpec((None, None, None, _SR, tb), lambda bi, p, i: (bi, p, i, 0, 0))
    return pl.pallas_call(
        body, name=name, grid=(b, N_HEADS // 2, nq),
        in_specs=[pl.BlockSpec((None, tb, LANES), lambda bi, p, i: (bi, i, z0 + p)), ospec, sspec, ospec],
        out_specs=[ospec, ospec, sspec],
        out_shape=[jax.ShapeDtypeStruct((b, s, D_MODEL), BF16), jax.ShapeDtypeStruct((b, s, D_MODEL), F32),
                   jax.ShapeDtypeStruct((b, N_HEADS // 2, nq, _SR, tb), F32)],
        compiler_params=_cp(("parallel", "parallel", "parallel")),
    )(proj3, o3, stat, dy3)


def _foxt_bwd(proj3, ckrep, do3, stats, *, name, tb):
    b, s, _ = proj3.shape
    nq = s // tb
    q0 = _PAD_COLS["c_q"][0] // LANES
    k0 = _PAD_COLS["c_k"][0] // LANES
    v0 = _PAD_COLS["c_v"][0] // LANES
    rep = tb // LANES

    def body(q_ref, do_ref, st_ref, k_ref, v_ref, ck_ref, dq_ref, dk_ref, dv_ref, cs_ref):
        j = pl.program_id(2)
        lane = lax.broadcasted_iota(jnp.int32, (tb, LANES), 1)
        lo = lane < HEAD_DIM
        lo_r = lax.broadcasted_iota(jnp.int32, (LANES, tb), 0) < HEAD_DIM

        @pl.when(j == 0)
        def _():
            dq_ref[...] = jnp.zeros_like(dq_ref)

        kf = k_ref[...].astype(F32)
        kb = kf.astype(BF16)
        kt = kf.T.astype(BF16)
        vb = v_ref[...].astype(BF16)
        cks = (jnp.tile(ck_ref[0], (1, rep)), jnp.tile(ck_ref[1], (1, rep)))

        def block(i, carry, diagonal):
            qs = pl.ds(pl.multiple_of(i * tb, tb), tb)
            q = q_ref[qs, :] * _SCALE
            do = do_ref[qs, :]
            st = st_ref[i]
            if diagonal:
                key = lax.broadcasted_iota(jnp.int32, (tb, tb), 0)
                qry = lax.broadcasted_iota(jnp.int32, (tb, tb), 1)
                mask = key <= qry
            dk, dv, cs = carry
            new_cs, dqs = [], []
            for hh in range(2):
                sel = lo if hh == 0 else jnp.logical_not(lo)
                qm = jnp.where(sel, q, 0.0).astype(BF16)
                dom = jnp.where(sel, do, 0.0).astype(BF16)
                sc = _dot_nt(kb, qm) - cks[hh]
                if diagonal:
                    sc = jnp.where(mask, sc, _NEG)
                mj = _row(st, _ST_MJ + 2 * j + hh)
                w = jnp.exp(mj - _row(st, _ST_LSE + hh))
                ph = jnp.exp(sc - mj).astype(BF16).astype(F32) * w
                ds = ph * (_dot_nt(vb, dom) - _row(st, _ST_DELTA + hh))
                dsb = ds.astype(BF16)
                dv = dv + _dot(ph.astype(BF16), dom)
                dk = dk + _dot(dsb, qm)
                new_cs.append(cs[hh] + jnp.sum(ds, axis=1, keepdims=True))
                dqs.append(_dot(kt, dsb))
            dq_ref[i] += jnp.where(lo_r, dqs[0], dqs[1]) * _SCALE
            return (dk, dv, tuple(new_cs))

        zcol = jnp.zeros((tb, 1), F32)
        init = (jnp.zeros((tb, LANES), F32), jnp.zeros((tb, LANES), F32), (zcol, zcol))
        carry = block(j, init, True)
        dk, dv, cs = lax.fori_loop(j + 1, nq, lambda i, c: block(i, c, False), carry)
        dk_ref[...] = dk.astype(BF16)
        dv_ref[...] = dv.astype(BF16)
        cs_ref[...] = jnp.where(lane == 0, cs[0], jnp.where(lane == 1, cs[1], 0.0))

    full = lambda c0: pl.BlockSpec((None, s, LANES), lambda bi, p, j: (bi, 0, c0 + p))
    kspec = lambda c0: pl.BlockSpec((None, tb, LANES), lambda bi, p, j: (bi, j, c0 + p))
    ko = pl.BlockSpec((None, tb, LANES), lambda bi, p, j: (bi, j, p))
    sall = pl.BlockSpec((None, None, nq, _SR, tb), lambda bi, p, j: (bi, p, 0, 0, 0))
    dqspec = pl.BlockSpec((None, None, nq, LANES, tb), lambda bi, p, j: (bi, p, 0, 0, 0))
    return pl.pallas_call(
        body, name=name, grid=(b, N_HEADS // 2, nq),
        in_specs=[full(q0), full(0), sall, kspec(k0), kspec(v0),
                  pl.BlockSpec((None, None, 2, tb, LANES), lambda bi, p, j: (bi, p, 0, j, 0))],
        out_specs=[dqspec, ko, ko, pl.BlockSpec((None, None, tb, LANES), lambda bi, p, j: (bi, p, j, 0))],
        out_shape=[jax.ShapeDtypeStruct((b, N_HEADS // 2, nq, LANES, tb), F32),
                   jax.ShapeDtypeStruct((b, s, D_MODEL), BF16), jax.ShapeDtypeStruct((b, s, D_MODEL), BF16),
                   jax.ShapeDtypeStruct((b, N_HEADS // 2, s, LANES), F32)],
        compiler_params=_cp(("parallel", "parallel", "arbitrary")),
    )(proj3, do3, stats, proj3, proj3, ckrep)


def _rope(x, cos, sin_signed):
    w = x.shape[1]
    lane = lax.broadcasted_iota(jnp.int32, x.shape, 1)
    first = (lane % HEAD_DIM) < (HEAD_DIM // 2)
    rot = jnp.where(first, pltpu.roll(x, w - HEAD_DIM // 2, 1), pltpu.roll(x, HEAD_DIM // 2, 1))
    return x * cos + rot * sin_signed


_QB = 4
_QROWS = _QB * CHUNK


def _swa_keys(kc_ref, kp_ref, vc_ref, vp_ref, cq_ref, sq_ref, cp_ref, sp_ref):
    cq, sq, cpv, spv = cq_ref[...], sq_ref[...], cp_ref[...], sp_ref[...]
    kc = _rope(kc_ref[...], cq, sq).astype(BF16)
    kp = _rope(kp_ref[...], cpv, spv).astype(BF16)
    return cq, sq, cpv, spv, kc, kp, vc_ref[...].astype(BF16), vp_ref[...].astype(BF16)


def _swa_stack(pairs, lo):
    return jnp.concatenate([jnp.where(lo, pairs[0], 0.0), jnp.where(lo, 0.0, pairs[0]),
                            jnp.where(lo, pairs[1], 0.0), jnp.where(lo, 0.0, pairs[1])], axis=0).astype(BF16)


def _swa_mask4(prev_valid):
    r = lax.broadcasted_iota(jnp.int32, (4 * CHUNK, 2 * CHUNK), 0) & (CHUNK - 1)
    c = lax.broadcasted_iota(jnp.int32, (4 * CHUNK, 2 * CHUNK), 1)
    own = jnp.logical_and(c >= CHUNK, c - CHUNK <= r)
    before = jnp.logical_and(c < CHUNK, c > r)
    if prev_valid is True:
        return jnp.logical_or(own, before)
    return jnp.logical_or(own, jnp.logical_and(before, prev_valid))


def _swa_sink4(skv):
    return jnp.concatenate([jnp.broadcast_to(_col(skv, j), (CHUNK, 1)) for j in range(4)], axis=0)


def _swa_specs(order, q0, z0):
    def spec(shape, fn):
        return pl.BlockSpec(shape, lambda *ids: fn(*order(*ids)))

    prev = lambda i: jnp.maximum(_QB * i - 1, 0)
    return dict(
        q=spec((None, _QROWS, 256), lambda bi, g, i: (bi, i, q0 + g)),
        z=spec((None, _QROWS, 256), lambda bi, g, i: (bi, i, z0 + g)),
        blk=spec((None, _QROWS, 256), lambda bi, g, i: (bi, i, g)),
        kcur=spec((None, _QROWS, LANES), lambda bi, g, i: (bi, i, g)),
        kprev=spec((None, CHUNK, LANES), lambda bi, g, i: (bi, prev(i), g)),
        kstep=spec((None, CHUNK, LANES), lambda bi, g, i: (bi, i, g)),
        tcur=spec((_QROWS, LANES), lambda bi, g, i: (i, 0)),
        tprev=spec((CHUNK, LANES), lambda bi, g, i: (prev(i), 0)),
        sk=spec((None, 1, LANES), lambda bi, g, i: (g, 0, 0)))


def _swa_fwd(proj3, k2, v2, cos, sin, sinks, *, name):
    b, s, _ = proj3.shape
    q0 = _PAD_COLS["b_q"][0] // 256
    z0 = _PAD_COLS["b_z"][0] // 256

    def body(q_ref, z_ref, kc_ref, kp_ref, vc_ref, vp_ref, cq_ref, sq_ref, cp_ref, sp_ref, sk_ref,
             y_ref, o_ref, lse_ref):
        i = pl.program_id(2)
        cq_all, sq_all, _, _, kc_all, kp0, vc_all, vp0 = _swa_keys(
            kc_ref, kp_ref, vc_ref, vp_ref, cq_ref, sq_ref, cp_ref, sp_ref)
        lo = lax.broadcasted_iota(jnp.int32, (CHUNK, LANES), 1) < HEAD_DIM
        sink4 = _swa_sink4(sk_ref[...])
        for u in range(_QB):
            rs = slice(CHUNK * u, CHUNK * (u + 1))
            ps = slice(CHUNK * (u - 1), CHUNK * u)
            cq, sq = cq_all[rs], sq_all[rs]
            kp, vp = (kp0, vp0) if u == 0 else (kc_all[ps], vc_all[ps])
            kk = jnp.concatenate([kp, kc_all[rs]], axis=0)
            vv = jnp.concatenate([vp, vc_all[rs]], axis=0)
            q4 = _swa_stack([_rope(q_ref[rs, LANES * pp:LANES * (pp + 1)], cq, sq) * _SCALE for pp in range(2)], lo)
            sc = jnp.where(_swa_mask4(True if u > 0 else i > 0), _dot_nt(q4, kk), _NEG)
            m = jnp.maximum(jnp.max(sc, axis=1, keepdims=True), sink4)
            pr = jnp.exp(sc - m)
            l = jnp.sum(pr, axis=1, keepdims=True) + jnp.exp(sink4 - m)
            o4 = _dot(pr.astype(BF16), vv) / l
            lse4 = m + jnp.log(l)
            for pp in range(2):
                ls = slice(LANES * pp, LANES * (pp + 1))
                h0 = slice(2 * CHUNK * pp, 2 * CHUNK * pp + CHUNK)
                h1 = slice(2 * CHUNK * pp + CHUNK, 2 * CHUNK * (pp + 1))
                o = jnp.where(lo, o4[h0], o4[h1])
                z = z_ref[rs, ls]
                o_ref[rs, ls] = o
                lse_ref[rs, ls] = jnp.where(lo, lse4[h0], lse4[h1])
                y_ref[rs, ls] = (o * (z * _sigmoid(z))).astype(BF16)

    sp = _swa_specs(lambda bi, g, i: (bi, g, i), q0, z0)
    return pl.pallas_call(
        body, name=name, grid=(b, N_GROUPS, s // _QROWS),
        in_specs=[sp["q"], sp["z"], sp["kcur"], sp["kprev"], sp["kcur"], sp["kprev"],
                  sp["tcur"], sp["tcur"], sp["tprev"], sp["tprev"], sp["sk"]],
        out_specs=[sp["blk"], sp["blk"], sp["blk"]],
        out_shape=[jax.ShapeDtypeStruct((b, s, D_MODEL), BF16)] + [jax.ShapeDtypeStruct((b, s, D_MODEL), F32)] * 2,
        compiler_params=_cp(("parallel", "parallel", "parallel")),
    )(proj3, proj3, k2, k2, v2, v2, cos, sin, cos, sin, sinks)


def _swa_bwd(proj3, k2, v2, cos, sin, sinks, o3, lse3, dy3, *, name):
    b, s, _ = proj3.shape
    q0 = _PAD_COLS["b_q"][0] // 256
    z0 = _PAD_COLS["b_z"][0] // 256

    def body(q_ref, z_ref, kc_ref, kp_ref, vc_ref, vp_ref, cq_ref, sq_ref, cp_ref, sp_ref, sk_ref,
             o_ref, lse_ref, dy_ref, dq_ref, dz_ref, dkc_ref, dkp_ref, dvc_ref, dvp_ref, dsk_ref):
        i = pl.program_id(2)
        first = jnp.logical_and(pl.program_id(1) == 0, i == 0)

        @pl.when(first)
        def _():
            dsk_ref[...] = jnp.zeros_like(dsk_ref)

        cq_all, sq_all, cpv, spv, kc_all, kp0, vc_all, vp0 = _swa_keys(
            kc_ref, kp_ref, vc_ref, vp_ref, cq_ref, sq_ref, cp_ref, sp_ref)
        lo = lax.broadcasted_iota(jnp.int32, (CHUNK, LANES), 1) < HEAD_DIM
        lane1 = lax.broadcasted_iota(jnp.int32, (1, LANES), 1)
        sink4 = _swa_sink4(sk_ref[...])
        zero = jnp.zeros((CHUNK, LANES), F32)
        dks = [zero] * (_QB + 1)
        dvs = [zero] * (_QB + 1)
        dsk_row = jnp.zeros((1, LANES), F32)
        for u in range(_QB):
            rs = slice(CHUNK * u, CHUNK * (u + 1))
            ps = slice(CHUNK * (u - 1), CHUNK * u)
            cq, sq = cq_all[rs], sq_all[rs]
            kp, vp = (kp0, vp0) if u == 0 else (kc_all[ps], vc_all[ps])
            kk = jnp.concatenate([kp, kc_all[rs]], axis=0)
            vv = jnp.concatenate([vp, vc_all[rs]], axis=0)
            q4 = _swa_stack([_rope(q_ref[rs, LANES * pp:LANES * (pp + 1)], cq, sq) * _SCALE for pp in range(2)], lo)
            dos, lses = [], []
            for pp in range(2):
                ls = slice(LANES * pp, LANES * (pp + 1))
                z = z_ref[rs, ls]
                sz = _sigmoid(z)
                dy = dy_ref[rs, ls]
                dos.append(dy * (z * sz))
                dz_ref[rs, ls] = (dy * o_ref[rs, ls] * (sz * (1.0 + z * (1.0 - sz)))).astype(BF16)
                lse = lse_ref[rs, ls]
                lses += [_col(lse, 0), _col(lse, HEAD_DIM)]
            do4 = _swa_stack(dos, lo)
            lse4 = jnp.concatenate(lses, axis=0)
            pr = jnp.exp(jnp.where(_swa_mask4(True if u > 0 else i > 0), _dot_nt(q4, kk), _NEG) - lse4)
            dp = _dot_nt(do4, vv)
            dl = jnp.sum(pr * dp, axis=1, keepdims=True)
            ds = (pr * (dp - dl)).astype(BF16)
            dsink = -jnp.exp(sink4 - lse4) * dl
            for j in range(4):
                dsk_row = dsk_row + jnp.where(
                    lane1 == j, jnp.sum(dsink[CHUNK * j:CHUNK * (j + 1)], axis=0, keepdims=True), 0.0)
            dq4 = _dot(ds, kk)
            dkk = _dot_tn(ds, q4)
            dvv = _dot_tn(pr.astype(BF16), do4)
            dks[u], dks[u + 1] = dks[u] + dkk[:CHUNK], dks[u + 1] + dkk[CHUNK:]
            dvs[u], dvs[u + 1] = dvs[u] + dvv[:CHUNK], dvs[u + 1] + dvv[CHUNK:]
            for pp in range(2):
                h0 = slice(2 * CHUNK * pp, 2 * CHUNK * pp + CHUNK)
                h1 = slice(2 * CHUNK * pp + CHUNK, 2 * CHUNK * (pp + 1))
                dq_ref[rs, LANES * pp:LANES * (pp + 1)] = _rope(
                    jnp.where(lo, dq4[h0], dq4[h1]) * _SCALE, cq, -sq).astype(BF16)
        fold = lambda v: v + pltpu.roll(v, HEAD_DIM, 1)
        dkp_ref[...] = fold(_rope(dks[0], cpv, -spv))
        dvp_ref[...] = fold(dvs[0])
        for u in range(_QB):
            rs = slice(CHUNK * u, CHUNK * (u + 1))
            dkc_ref[rs, :] = fold(_rope(dks[u + 1], cq_all[rs], -sq_all[rs]))
            dvc_ref[rs, :] = fold(dvs[u + 1])
        dsk_ref[...] += dsk_row

    sp = _swa_specs(lambda g, bi, i: (bi, g, i), q0, z0)
    kv_shape = jax.ShapeDtypeStruct((b, s, 512), F32)
    kvp_shape = jax.ShapeDtypeStruct((b, s // _QB, 512), F32)
    return pl.pallas_call(
        body, name=name, grid=(N_GROUPS, b, s // _QROWS),
        in_specs=[sp["q"], sp["z"], sp["kcur"], sp["kprev"], sp["kcur"], sp["kprev"],
                  sp["tcur"], sp["tcur"], sp["tprev"], sp["tprev"], sp["sk"], sp["blk"], sp["blk"], sp["blk"]],
        out_specs=[sp["blk"], sp["blk"], sp["kcur"], sp["kstep"], sp["kcur"], sp["kstep"], sp["sk"]],
        out_shape=[jax.ShapeDtypeStruct((b, s, D_MODEL), BF16), jax.ShapeDtypeStruct((b, s, D_MODEL), BF16),
                   kv_shape, kvp_shape, kv_shape, kvp_shape, jax.ShapeDtypeStruct((N_GROUPS, 1, LANES), F32)],
        compiler_params=_cp(("arbitrary", "arbitrary", "arbitrary")),
    )(proj3, proj3, k2, k2, v2, v2, cos, sin, cos, sin, sinks, o3, lse3, dy3)


def _merge_fwd(proj, br, gb, *, name, tm=256):
    t = proj.shape[0]
    g0 = _PAD_COLS["gates"][0] // D_MODEL

    def body(g_ref, a_ref, b_ref, c_ref, gb_ref, o_ref):
        acc = None
        for i, r in enumerate((a_ref, b_ref, c_ref)):
            gate = _sigmoid(g_ref[:, D_MODEL * i:D_MODEL * (i + 1)] + gb_ref[i:i + 1, :])
            term = gate * r[...]
            acc = term if acc is None else acc + term
        o_ref[...] = acc.astype(BF16)

    row = pl.BlockSpec((tm, D_MODEL), lambda i: (i, 0))
    return pl.pallas_call(
        body, name=name, grid=(t // tm,),
        in_specs=[pl.BlockSpec((tm, 3 * D_MODEL), lambda i: (i, g0)), row, row, row,
                  pl.BlockSpec((3, D_MODEL), lambda i: (0, 0))],
        out_specs=row, out_shape=jax.ShapeDtypeStruct((t, D_MODEL), BF16),
        compiler_params=_cp(("parallel",)),
    )(proj, br[0], br[1], br[2], gb)


def _merge_bwd(proj, br, gb, dm, *, name, tm=256):
    t = proj.shape[0]
    g0 = _PAD_COLS["gates"][0] // D_MODEL

    def body(g_ref, a_ref, b_ref, c_ref, gb_ref, dm_ref, da_ref, db_ref, dc_ref, dg_ref, dgb_ref):
        @pl.when(pl.program_id(0) == 0)
        def _():
            dgb_ref[...] = jnp.zeros_like(dgb_ref)

        dmv = dm_ref[...]
        for i, (r, dr) in enumerate(((a_ref, da_ref), (b_ref, db_ref), (c_ref, dc_ref))):
            gate = _sigmoid(g_ref[:, D_MODEL * i:D_MODEL * (i + 1)] + gb_ref[i:i + 1, :])
            dr[...] = (dmv * gate).astype(BF16)
            dg = dmv * r[...] * gate * (1.0 - gate)
            dg_ref[:, D_MODEL * i:D_MODEL * (i + 1)] = dg.astype(BF16)
            dgb_ref[i:i + 1, :] += jnp.sum(dg, axis=0, keepdims=True)

    row = pl.BlockSpec((tm, D_MODEL), lambda i: (i, 0))
    rowb = jax.ShapeDtypeStruct((t, D_MODEL), BF16)
    return pl.pallas_call(
        body, name=name, grid=(t // tm,),
        in_specs=[pl.BlockSpec((tm, 3 * D_MODEL), lambda i: (i, g0)), row, row, row,
                  pl.BlockSpec((3, D_MODEL), lambda i: (0, 0)), row],
        out_specs=[row, row, row, pl.BlockSpec((tm, 3 * D_MODEL), lambda i: (i, 0)),
                   pl.BlockSpec((8, D_MODEL), lambda i: (0, 0))],
        out_shape=[rowb, rowb, rowb, jax.ShapeDtypeStruct((t, 3 * D_MODEL), BF16),
                   jax.ShapeDtypeStruct((8, D_MODEL), F32)],
        compiler_params=_cp(("arbitrary",)),
    )(proj, br[0], br[1], br[2], gb, dm)


def _rope_tables(s):
    pos = jnp.arange(s, dtype=F32)
    inv_freq = ROPE_THETA ** (-jnp.arange(0, HEAD_DIM, 2, dtype=F32) / HEAD_DIM)
    ang = pos[:, None] * inv_freq[None, :]
    cos, sin = jnp.cos(ang), jnp.sin(ang)
    return jnp.tile(cos, (1, 4)), jnp.tile(jnp.concatenate([-sin, sin], axis=1), (1, 2))


def _dup_kv(proj3, name):
    b, s, _ = proj3.shape
    p0, sz = _PAD_COLS[name]
    kv = proj3[:, :, p0:p0 + sz].reshape(b, s, N_GROUPS, 1, HEAD_DIM)
    return jnp.broadcast_to(kv, (b, s, N_GROUPS, 2, HEAD_DIM)).reshape(b, s, 512)


def _pair_rows(cum, tb):
    b, s, _ = cum.shape
    t = jnp.transpose(cum[:, :, :N_HEADS], (0, 2, 1)).reshape(b, N_HEADS // 2, 2, s // tb, tb)
    return jnp.pad(jnp.transpose(t, (0, 1, 3, 2, 4)), ((0, 0), (0, 0), (0, 0), (0, 6), (0, 0)))


def _layer_params(wl):
    return dict(
        dtb=_group_lanes(wl["dt_bias"]), alog=_group_lanes(wl["a_log"]), dsk=_group_lanes(wl["d_skip"]),
        nw=wl["ssm_norm_w"].reshape(N_GROUPS, 1, 256), sinks=_group_lanes(wl["sinks"]),
        fb=jnp.pad(wl["f_bias"], (0, LANES - N_HEADS)).reshape(1, LANES))


def _layer_fwd(x, wl, tabs, bsz, li, tb):
    t = x.shape[0]
    s = t // bsz
    cos, sin = tabs
    lp = _layer_params(wl)
    n = lambda k: f"l{li}_{k}"
    h, h_t = _rms_fwd(x, wl["norm_w"], name=n("rms_fwd"))
    proj = _mm(h, wl["w_in"], tm=1024, tn=1536, tk=1024, name=n("mm_proj"))
    proj3 = proj.reshape(bsz, s, N_PAD)
    xact3 = _conv_fwd(proj3, wl["conv_w"], wl["conv_b"], name=n("conv_fwd"))
    ya3, ypre3, hst = _ssd_fwd(proj3, xact3, lp["dtb"], lp["alog"], lp["dsk"], lp["nw"], name=n("ssd_fwd"))
    k2, v2 = _dup_kv(proj3, "b_k"), _dup_kv(proj3, "b_v")
    yb3, ob3, lseb3 = _swa_fwd(proj3, k2, v2, cos, sin, lp["sinks"], name=n("swa_fwd"))
    cum = _fgate_fwd(proj3, lp["fb"], name=n("fgate_fwd"))
    cum_t = _ck_rep(cum)
    yc3, oc3, statc3 = _foxt_fwd(proj3, cum_t, name=n("fox_fwd"), tb=tb)
    ys = [v.reshape(t, D_MODEL) for v in (ya3, yb3, yc3)]
    br = [_mm(ys[i], wl["w_proj"][i], tm=1024, tn=1024, tk=1024, name=n(f"mm_br{i}")) for i in range(3)]
    merged = _merge_fwd(proj, br, wl["gate_bias"], name=n("merge_fwd"))
    x_new = _mm(merged, wl["w_out"], tm=1024, tn=1024, tk=1024, add=x, name=n("mm_out"))
    saved = dict(x=x, h_t=h_t, proj=proj, xact3=xact3, ypre3=ypre3, hst=hst, k2=k2, v2=v2, ob3=ob3, lseb3=lseb3,
                 cum_t=cum_t, oc3=oc3, statc3=statc3, ys=ys, br=br, merged=merged, lp=lp)
    return x_new, saved


def _layer_bwd(dx, wl, sv, tabs, bsz, li, tb):
    t = dx.shape[0]
    s = t // bsz
    cos, sin = tabs
    lp = sv["lp"]
    n = lambda k: f"l{li}_{k}"
    proj = sv["proj"]
    proj3 = proj.reshape(bsz, s, N_PAD)
    g = {}
    dmerged = _mm(dx, wl["w_out"], tb=True, tm=1024, tn=1024, tk=1024, name=n("mm_dmerged"))
    g["w_out"] = _mm(sv["merged"], dx, ta=True, tm=1024, tn=1024, tk=512, name=n("mm_dwout"))
    dbr0, dbr1, dbr2, dgates, dgb = _merge_bwd(proj, sv["br"], wl["gate_bias"], dmerged, name=n("merge_bwd"))
    g["gate_bias"] = dgb[:3]
    dbr = (dbr0, dbr1, dbr2)
    dys = [_mm(dbr[i], wl["w_proj"][i], tb=True, tm=1024, tn=1024, tk=1024, name=n(f"mm_dy{i}"))
           for i in range(3)]
    g["w_proj"] = jnp.stack([_mm(sv["ys"][i], dbr[i], ta=True, tm=1024, tn=1024, tk=512, name=n(f"mm_dwproj{i}"))
                             for i in range(3)])
    dy3 = [v.reshape(bsz, s, D_MODEL) for v in dys]

    (dact, daz, dadt, ddtb, dalog, ddsk, dnw) = _ssd_bwd(
        proj3, sv["xact3"], lp["dtb"], lp["alog"], lp["dsk"], lp["nw"], sv["ypre3"], sv["hst"], dy3[0],
        name=n("ssd_bwd"))
    g["dt_bias"], g["a_log"], g["d_skip"] = _ungroup_lanes(ddtb), _ungroup_lanes(dalog), _ungroup_lanes(ddsk)
    g["ssm_norm_w"] = dnw.reshape(D_MODEL)
    dxbc, dwb = _conv_bwd(proj3, wl["conv_w"], wl["conv_b"], dact, name=n("conv_bwd"))
    g["conv_w"], g["conv_b"] = dwb[:CONV_WIDTH], dwb[CONV_WIDTH]

    dbq, dbz, dkc, dkp, dvc, dvp, dsk = _swa_bwd(proj3, sv["k2"], sv["v2"], cos, sin, lp["sinks"], sv["ob3"],
                                                 sv["lseb3"], dy3[1], name=n("swa_bwd"))
    g["sinks"] = _ungroup_lanes(dsk)

    def fold(cur, prv):
        p4 = prv.reshape(bsz, s // _QROWS, 1, CHUNK, 512)
        tail = jnp.concatenate([p4[:, 1:], jnp.zeros_like(p4[:, :1])], axis=1)
        shifted = jnp.concatenate([jnp.zeros((bsz, s // _QROWS, _QB - 1, CHUNK, 512), F32), tail], axis=2)
        tot = cur + shifted.reshape(bsz, s, 512)
        return tot.reshape(bsz, s, N_GROUPS, 2, HEAD_DIM)[:, :, :, 0].reshape(bsz, s, 256)

    dbk, dbv = fold(dkc, dkp), fold(dvc, dvp)

    dcz, do3, stats = _foxt_prep(proj3, sv["oc3"], sv["statc3"], dy3[2], name=n("fox_prep"), tb=tb)
    dqt, dck, dcv, csum = _foxt_bwd(proj3, sv["cum_t"], do3, stats, name=n("fox_bwd"), tb=tb)
    dcq = jnp.transpose(dqt, (0, 2, 4, 1, 3)).reshape(bsz, s, D_MODEL)
    dcum = -jnp.transpose(csum[:, :, :, :2], (0, 2, 1, 3)).reshape(bsz, s, N_HEADS)
    dcum = jnp.pad(dcum, ((0, 0), (0, 0), (0, LANES - N_HEADS)))
    dcf, dfb = _fgate_bwd(proj3, lp["fb"], dcum, name=n("fgate_bwd"))
    g["f_bias"] = dfb[0, :N_HEADS]

    parts = {"gates": dgates.reshape(bsz, s, 3 * D_MODEL), "xbc": dxbc, "a_z": daz, "b_q": dbq, "b_z": dbz,
             "c_q": dcq, "c_k": dck, "c_v": dcv, "c_z": dcz, "b_k": dbk, "b_v": dbv, "a_dt": dadt, "c_f": dcf}
    dproj = jnp.concatenate([parts[name].astype(BF16) for name, _ in _PAD_ORDER]
                            + [jnp.zeros((bsz, s, N_PAD - N_USED), BF16)], axis=2).reshape(t, N_PAD)
    dh = _mm(dproj, wl["w_in"], tb=True, tm=1024, tn=1024, tk=1536, name=n("mm_dh"))
    g["w_in"] = _unpad_w_in(_mm(sv["h_t"], dproj, tm=1024, tn=768, tk=2048, name=n("mm_dwin")))
    dx_in, dnorm = _rms_bwd(sv["x"], wl["norm_w"], dh, dx, name=n("rms_bwd"))
    g["norm_w"] = dnorm[0]
    return dx_in, g


def _local_step(x, target, wls, final_norm_w, tb=1024):
    bsz, s, d = x.shape
    t = bsz * s
    tabs = _rope_tables(s)
    xc = x.reshape(t, d)
    saved = []
    for li, wl in enumerate(wls):
        xc, sv = _layer_fwd(xc, wl, tabs, bsz, li, tb)
        saved.append(sv)
    loss, dx, dfw = _final_loss(xc, final_norm_w, target.reshape(t, d), name="final_loss")
    grads = [None] * len(wls)
    for li in reversed(range(len(wls))):
        dx, grads[li] = _layer_bwd(dx, wls[li], saved[li], tabs, bsz, li, tb)
    return loss[0, 0], dx.reshape(bsz, s, d), grads, dfw[0]


_HBM = pl.BlockSpec(memory_space=pltpu.HBM)


def _chip_peers(x, y):
    return [(1 - x, y), (x, 1 - y), (1 - x, 1 - y)]


def _gather_weights(arrs, *, name):
    n = len(arrs)

    def body(*refs):
        ins, outs = refs[:n], refs[n:2 * n]
        ici_send, ici_recv, d2d_send, d2d_recv = refs[2 * n:]
        x, y, c = lax.axis_index("x"), lax.axis_index("y"), lax.axis_index("c")
        me = 2 * x + y
        peers = _chip_peers(x, y)
        sib = (x, y, 1 - c)
        sends, fwds = [], []
        for a in range(n):
            for k, (px, py) in enumerate(peers):
                cp = pltpu.make_async_remote_copy(
                    src_ref=ins[a].at[c], dst_ref=outs[a].at[me, c], send_sem=ici_send.at[a, k],
                    recv_sem=ici_recv.at[a, k], device_id=(px, py, c), device_id_type=MESH)
                cp.start()
                sends.append(cp)
        for a in range(n):
            for k, (px, py) in enumerate(peers):
                slot = 2 * px + py
                pltpu.make_async_remote_copy(
                    src_ref=ins[a].at[c], dst_ref=outs[a].at[slot, c], send_sem=ici_send.at[a, k],
                    recv_sem=ici_recv.at[a, k], device_id=(px, py, c), device_id_type=MESH).wait_recv()
                fw = pltpu.make_async_remote_copy(
                    src_ref=outs[a].at[slot, c], dst_ref=outs[a].at[slot, c], send_sem=d2d_send.at[a, k],
                    recv_sem=d2d_recv.at[a, k], device_id=sib, device_id_type=MESH)
                fw.start()
                fwds.append(fw)
        for a in range(n):
            for k, (px, py) in enumerate(peers):
                slot = 2 * px + py
                pltpu.make_async_remote_copy(
                    src_ref=outs[a].at[slot, 1 - c], dst_ref=outs[a].at[slot, 1 - c], send_sem=d2d_send.at[a, k],
                    recv_sem=d2d_recv.at[a, k], device_id=sib, device_id_type=MESH).wait_recv()
        for cp in sends + fwds:
            cp.wait_send()

    out_shape = [jax.ShapeDtypeStruct((N_CHIPS,) + a.shape, a.dtype) for a in arrs]
    return pl.pallas_call(
        body, name=name, out_shape=out_shape, in_specs=[_HBM] * n, out_specs=[_HBM] * n,
        scratch_shapes=[pltpu.SemaphoreType.DMA((n, 3)), pltpu.SemaphoreType.DMA((n, 3)),
                        pltpu.SemaphoreType.DMA((n, 3)), pltpu.SemaphoreType.DMA((n, 3))],
    )(*arrs)


def _pair_exchange(arrs, *, name):
    n = len(arrs)

    def body(*refs):
        ins, outs = refs[:n], refs[n:2 * n]
        send, recv = refs[2 * n:]
        x, y, c = lax.axis_index("x"), lax.axis_index("y"), lax.axis_index("c")
        sib = (x, y, 1 - c)
        cps = []
        for a in range(n):
            for k in range(N_CHIPS):
                cp = pltpu.make_async_remote_copy(
                    src_ref=ins[a].at[k, 1 - c], dst_ref=outs[a].at[k], send_sem=send.at[a, k],
                    recv_sem=recv.at[a, k], device_id=sib, device_id_type=MESH)
                cp.start()
                cps.append(cp)
        for cp in cps:
            cp.wait()

    out_shape = [jax.ShapeDtypeStruct((N_CHIPS,) + a.shape[2:], a.dtype) for a in arrs]
    return pl.pallas_call(
        body, name=name, out_shape=out_shape, in_specs=[_HBM] * n, out_specs=[_HBM] * n,
        scratch_shapes=[pltpu.SemaphoreType.DMA((n, N_CHIPS)), pltpu.SemaphoreType.DMA((n, N_CHIPS))],
    )(*arrs)


def _chip_exchange(arrs, *, name):
    n = len(arrs)

    def body(*refs):
        ins, outs = refs[:n], refs[n:2 * n]
        send, recv = refs[2 * n:]
        x, y, c = lax.axis_index("x"), lax.axis_index("y"), lax.axis_index("c")
        me = 2 * x + y
        peers = _chip_peers(x, y)
        cps = []
        for a in range(n):
            for k, (px, py) in enumerate(peers):
                cp = pltpu.make_async_remote_copy(
                    src_ref=ins[a].at[2 * px + py], dst_ref=outs[a].at[me], send_sem=send.at[a, k],
                    recv_sem=recv.at[a, k], device_id=(px, py, c), device_id_type=MESH)
                cp.start()
                cps.append(cp)
        for a in range(n):
            for k, (px, py) in enumerate(peers):
                pltpu.make_async_remote_copy(
                    src_ref=ins[a].at[2 * px + py], dst_ref=outs[a].at[2 * px + py], send_sem=send.at[a, k],
                    recv_sem=recv.at[a, k], device_id=(px, py, c), device_id_type=MESH).wait_recv()
        for cp in cps:
            cp.wait_send()

    out_shape = [jax.ShapeDtypeStruct(a.shape, a.dtype) for a in arrs]
    return pl.pallas_call(
        body, name=name, out_shape=out_shape, in_specs=[_HBM] * n, out_specs=[_HBM] * n,
        scratch_shapes=[pltpu.SemaphoreType.DMA((n, 3)), pltpu.SemaphoreType.DMA((n, 3))],
    )(*arrs)


def _pair_share(arrs, *, name):
    n = len(arrs)

    def body(*refs):
        ins, outs = refs[:n], refs[n:2 * n]
        send, recv = refs[2 * n:]
        x, y, c = lax.axis_index("x"), lax.axis_index("y"), lax.axis_index("c")
        sib = (x, y, 1 - c)
        cps = []
        for a in range(n):
            cp = pltpu.make_async_remote_copy(
                src_ref=ins[a], dst_ref=outs[a], send_sem=send.at[a], recv_sem=recv.at[a],
                device_id=sib, device_id_type=MESH)
            cp.start()
            cps.append(cp)
        for cp in cps:
            cp.wait()

    out_shape = [jax.ShapeDtypeStruct(a.shape, a.dtype) for a in arrs]
    return pl.pallas_call(
        body, name=name, out_shape=out_shape, in_specs=[_HBM] * n, out_specs=[_HBM] * n,
        scratch_shapes=[pltpu.SemaphoreType.DMA((n,)), pltpu.SemaphoreType.DMA((n,))],
    )(*arrs)


def _allreduce_small(buf, *, name):
    r = buf.shape[0]

    def body(in_ref, out_ref, land, send, recv):
        x, y, c = lax.axis_index("x"), lax.axis_index("y"), lax.axis_index("c")
        me = 4 * x + 2 * y + c
        land[me] = in_ref[...]
        cps = []
        for k in range(1, N_DEV):
            px, py, pc = x ^ ((k >> 2) & 1), y ^ ((k >> 1) & 1), c ^ (k & 1)
            cp = pltpu.make_async_remote_copy(
                src_ref=in_ref, dst_ref=land.at[me], send_sem=send.at[k - 1], recv_sem=recv.at[k - 1],
                device_id=(px, py, pc), device_id_type=MESH)
            cp.start()
            cps.append(cp)
        for k in range(1, N_DEV):
            px, py, pc = x ^ ((k >> 2) & 1), y ^ ((k >> 1) & 1), c ^ (k & 1)
            pltpu.make_async_remote_copy(
                src_ref=in_ref, dst_ref=land.at[4 * px + 2 * py + pc], send_sem=send.at[k - 1],
                recv_sem=recv.at[k - 1], device_id=(px, py, pc), device_id_type=MESH).wait_recv()
        for cp in cps:
            cp.wait_send()
        acc = land[0]
        for k in range(1, N_DEV):
            acc = acc + land[k]
        out_ref[...] = acc

    vm = pl.BlockSpec(memory_space=pltpu.VMEM)
    return pl.pallas_call(
        body, name=name, out_shape=jax.ShapeDtypeStruct((r, LANES), F32), in_specs=[vm], out_specs=vm,
        scratch_shapes=[pltpu.VMEM((N_DEV, r, LANES), F32), pltpu.SemaphoreType.DMA((N_DEV - 1,)),
                        pltpu.SemaphoreType.DMA((N_DEV - 1,))],
    )(buf)


def _rows2d(a):
    return a.reshape(-1, a.shape[-1])


def _row_tile(rows, cols, n_arrays, budget=20 * 1024 * 1024):
    best = 8 if rows % 8 == 0 else rows
    tr = 8
    while tr <= rows:
        if rows % tr == 0 and tr * cols * 4 * n_arrays * 2 <= budget:
            best = tr
        tr *= 2
    return best


def _add_slot_layer(full, other, *, name):
    _, _, r, cdim = full.shape
    tr = _row_tile(r, cdim, 4)

    def body(c_ref, a_ref, b_ref, o_ref, ob_ref):
        sm = a_ref[...] + b_ref[...]
        o_ref[...] = sm
        ob_ref[...] = sm.astype(BF16)

    c = lax.axis_index("c").astype(jnp.int32).reshape(1)
    blk = pl.BlockSpec((None, tr, cdim), lambda k, i, c_ref: (k, i, 0))
    return pl.pallas_call(
        body, name=name,
        grid_spec=pltpu.PrefetchScalarGridSpec(
            num_scalar_prefetch=1, grid=(N_CHIPS, r // tr),
            in_specs=[pl.BlockSpec((None, None, tr, cdim), lambda k, i, c_ref: (k, c_ref[0], i, 0)), blk],
            out_specs=[blk, blk]),
        out_shape=[jax.ShapeDtypeStruct((N_CHIPS, r, cdim), F32), jax.ShapeDtypeStruct((N_CHIPS, r, cdim), BF16)],
        compiler_params=_cp(("parallel", "parallel")),
    )(c, full, other)


def _sum_slots(parts, pair, *, name):
    _, r, cdim = parts.shape
    tr = _row_tile(r, cdim, 5)

    def body(me_ref, p_ref, own_ref, o_ref):
        me = me_ref[0]
        acc = None
        for k in range(N_CHIPS):
            term = jnp.where(me == k, own_ref[...], p_ref[k].astype(F32))
            acc = term if acc is None else acc + term
        o_ref[...] = acc

    me = (2 * lax.axis_index("x") + lax.axis_index("y")).astype(jnp.int32).reshape(1)
    return pl.pallas_call(
        body, name=name,
        grid_spec=pltpu.PrefetchScalarGridSpec(
            num_scalar_prefetch=1, grid=(r // tr,),
            in_specs=[pl.BlockSpec((N_CHIPS, tr, cdim), lambda i, me_ref: (0, i, 0)),
                      pl.BlockSpec((None, tr, cdim), lambda i, me_ref: (me_ref[0], i, 0))],
            out_specs=pl.BlockSpec((tr, cdim), lambda i, me_ref: (i, 0))),
        out_shape=jax.ShapeDtypeStruct((r, cdim), F32),
        compiler_params=_cp(("parallel",)),
    )(me, parts, pair)


def _adamw(w, g, m, v, *, name):
    r, cdim = w.shape
    tr = _row_tile(r, cdim, 7)
    c1 = 1.0 - ADAM_B1 ** ADAM_STEP
    c2 = 1.0 - ADAM_B2 ** ADAM_STEP

    def body(w_ref, g_ref, m_ref, v_ref, d_ref, nm_ref, nv_ref):
        gv = g_ref[...]
        mn = ADAM_B1 * m_ref[...] + (1.0 - ADAM_B1) * gv
        vn = ADAM_B2 * v_ref[...] + (1.0 - ADAM_B2) * (gv * gv)
        nm_ref[...] = mn
        nv_ref[...] = vn
        d_ref[...] = -ADAM_LR * ((mn / c1) / (jnp.sqrt(vn / c2) + ADAM_EPS) + ADAM_WD * w_ref[...])

    blk = pl.BlockSpec((tr, cdim), lambda i: (i, 0))
    sh = jax.ShapeDtypeStruct((r, cdim), F32)
    return pl.pallas_call(
        body, name=name, grid=(r // tr,), in_specs=[blk] * 4, out_specs=[blk] * 3, out_shape=[sh] * 3,
        compiler_params=_cp(("parallel",)),
    )(w, g, m, v)


_SMALL = ("norm_w", "conv_b", "dt_bias", "a_log", "d_skip", "ssm_norm_w", "sinks", "f_bias", "final_norm_w",
          "conv_w", "gate_bias")


def _pack(vals):
    flat = jnp.concatenate([v.reshape(-1) for v in vals])
    rows = -(-flat.shape[0] // LANES)
    rows = -(-rows // 8) * 8
    return jnp.pad(flat, (0, rows * LANES - flat.shape[0])).reshape(rows, LANES)


def _unpack(buf, shapes):
    flat = buf.reshape(-1)
    out, off = [], 0
    for sh in shapes:
        sz = int(np.prod(sh))
        out.append(flat[off:off + sz].reshape(sh))
        off += sz
    return out


def kernel(x, norm_w, w_in, conv_w, conv_b, dt_bias, a_log, d_skip, ssm_norm_w, sinks, f_bias, gate_bias, w_proj, w_out, final_norm_w, loss_target, m_norm_w, m_w_in, m_conv_w, m_conv_b, m_dt_bias, m_a_log, m_d_skip, m_ssm_norm_w, m_sinks, m_f_bias, m_gate_bias, m_w_proj, m_w_out, m_final_norm_w, v_norm_w, v_w_in, v_conv_w, v_conv_b, v_dt_bias, v_a_log, v_d_skip, v_ssm_norm_w, v_sinks, v_f_bias, v_gate_bias, v_w_proj, v_w_out, v_final_norm_w):
    depth = w_in.shape[0]
    chip = 2 * lax.axis_index("x") + lax.axis_index("y")

    own = [w_in.astype(BF16), w_proj.astype(BF16), w_out.astype(BF16), conv_w, gate_bias]
    gathered = _gather_weights(own, name="gather_weights")

    def whole(a, li, axis):
        return jnp.concatenate([jnp.where(chip == k, own[a][li], gathered[a][k, li]) for k in range(N_CHIPS)],
                               axis=axis)

    wls = []
    for li in range(depth):
        wls.append(dict(
            norm_w=norm_w[li], w_in=_pad_w_in(whole(0, li, 1)),
            conv_w=whole(3, li, 1), conv_b=conv_b[li], dt_bias=dt_bias[li], a_log=a_log[li], d_skip=d_skip[li],
            ssm_norm_w=ssm_norm_w[li], sinks=sinks[li], f_bias=f_bias[li], gate_bias=whole(4, li, 1),
            w_proj=whole(1, li, 1),
            w_out=whole(2, li, 0)))

    loss_part, grad_x, grads, d_final = _local_step(x, loss_target, wls, final_norm_w)
    loss = lax.psum(loss_part, ("x", "y", "c"))

    c_in = w_in.shape[2]
    r_proj = w_proj.shape[2]
    r_out = w_out.shape[1]
    full_in = jnp.stack([jnp.stack([grads[li]["w_in"][:, k * c_in:(k + 1) * c_in] for li in range(depth)])
                         for k in range(N_CHIPS)])
    full_proj = jnp.stack([jnp.stack([grads[li]["w_proj"][:, k * r_proj:(k + 1) * r_proj].reshape(-1, D_MODEL)
                                      for li in range(depth)]) for k in range(N_CHIPS)])
    full_out = jnp.stack([jnp.stack([grads[li]["w_out"][k * r_out:(k + 1) * r_out] for li in range(depth)])
                          for k in range(N_CHIPS)])
    fulls = [full_in, full_proj, full_out]
    others = _pair_exchange(fulls, name="grad_pair_exchange")
    pair = [_add_slot_layer(f, o, name=f"grad_pair_add{i}") for i, (f, o) in enumerate(zip(fulls, others))]
    parts = _chip_exchange([p[1] for p in pair], name="grad_chip_exchange")
    mine = [_sum_slots(p, pr[0], name=f"grad_slot_sum{i}") for i, (p, pr) in enumerate(zip(parts, pair))]
    theirs = _pair_share(mine, name="grad_pair_share")
    core = lax.axis_index("c")
    red_in, red_proj, red_out = [jnp.stack([jnp.where(core == li, m, t) for li in range(depth)])
                                 for m, t in zip(mine, theirs)]
    grad_w_in = red_in
    grad_w_proj = red_proj.reshape(w_proj.shape)
    grad_w_out = red_out

    small_full = {
        "norm_w": jnp.stack([g["norm_w"] for g in grads]), "conv_b": jnp.stack([g["conv_b"] for g in grads]),
        "dt_bias": jnp.stack([g["dt_bias"] for g in grads]), "a_log": jnp.stack([g["a_log"] for g in grads]),
        "d_skip": jnp.stack([g["d_skip"] for g in grads]),
        "ssm_norm_w": jnp.stack([g["ssm_norm_w"] for g in grads]),
        "sinks": jnp.stack([g["sinks"] for g in grads]), "f_bias": jnp.stack([g["f_bias"] for g in grads]),
        "final_norm_w": d_final,
        "conv_w": jnp.stack([g["conv_w"] for g in grads]), "gate_bias": jnp.stack([g["gate_bias"] for g in grads])}
    shapes = [small_full[k].shape for k in _SMALL]
    summed = _unpack(_allreduce_small(_pack([small_full[k] for k in _SMALL]), name="allreduce_small"), shapes)
    gsmall = dict(zip(_SMALL, summed))
    gsmall["conv_w"] = lax.dynamic_slice_in_dim(gsmall["conv_w"], chip * conv_w.shape[2], conv_w.shape[2], axis=2)
    gsmall["gate_bias"] = lax.dynamic_slice_in_dim(gsmall["gate_bias"], chip * gate_bias.shape[2],
                                                   gate_bias.shape[2], axis=2)

    w_small = dict(norm_w=norm_w, conv_b=conv_b, dt_bias=dt_bias, a_log=a_log, d_skip=d_skip,
                   ssm_norm_w=ssm_norm_w, sinks=sinks, f_bias=f_bias, final_norm_w=final_norm_w, conv_w=conv_w,
                   gate_bias=gate_bias)
    m_small = dict(norm_w=m_norm_w, conv_b=m_conv_b, dt_bias=m_dt_bias, a_log=m_a_log, d_skip=m_d_skip,
                   ssm_norm_w=m_ssm_norm_w, sinks=m_sinks, f_bias=m_f_bias, final_norm_w=m_final_norm_w,
                   conv_w=m_conv_w, gate_bias=m_gate_bias)
    v_small = dict(norm_w=v_norm_w, conv_b=v_conv_b, dt_bias=v_dt_bias, a_log=v_a_log, d_skip=v_d_skip,
                   ssm_norm_w=v_ssm_norm_w, sinks=v_sinks, f_bias=v_f_bias, final_norm_w=v_final_norm_w,
                   conv_w=v_conv_w, gate_bias=v_gate_bias)
    sshapes = [w_small[k].shape for k in _SMALL]
    ds, ms, vs = _adamw(_pack([w_small[k] for k in _SMALL]), _pack([gsmall[k] for k in _SMALL]),
                        _pack([m_small[k] for k in _SMALL]), _pack([v_small[k] for k in _SMALL]), name="adamw_small")
    delta = dict(zip(_SMALL, _unpack(ds, sshapes)))
    new_m = dict(zip(_SMALL, _unpack(ms, sshapes)))
    new_v = dict(zip(_SMALL, _unpack(vs, sshapes)))
    grad = dict(gsmall)
    for nm, w, g, m, v in (("w_in", w_in, grad_w_in, m_w_in, v_w_in),
                           ("w_proj", w_proj, grad_w_proj, m_w_proj, v_w_proj),
                           ("w_out", w_out, grad_w_out, m_w_out, v_w_out)):
        d2, m2, v2 = _adamw(_rows2d(w), _rows2d(g), _rows2d(m), _rows2d(v), name=f"adamw_{nm}")
        grad[nm] = g
        delta[nm], new_m[nm], new_v[nm] = d2.reshape(w.shape), m2.reshape(w.shape), v2.reshape(w.shape)

    order = ("norm_w", "w_in", "conv_w", "conv_b", "dt_bias", "a_log", "d_skip", "ssm_norm_w", "sinks", "f_bias",
             "gate_bias", "w_proj", "w_out", "final_norm_w")
    return (loss, grad_x, *[grad[k] for k in order], *[delta[k] for k in order],
            *[new_m[k] for k in order], *[new_v[k] for k in order])
```

```python
import functools
import math

import numpy as np
import jax
import jax.numpy as jnp
from jax import lax
from jax.experimental import pallas as pl
from jax.experimental.pallas import tpu as pltpu

F32 = jnp.float32
BF16 = jnp.bfloat16
HIGHEST = lax.Precision.HIGHEST
MESH = pl.DeviceIdType.MESH

D_MODEL = 1024
HEAD_DIM = 64
N_HEADS = 16
N_GROUPS = 4
SSM_STATE = 128
CHUNK = 128
CONV_WIDTH = 4
CONV_DIM = 2048
ROPE_THETA = 10000.0
NORM_EPS = 1e-6
LANES = 128
N_CHIPS = 4
N_DEV = 8

ADAM_LR = 0.001
ADAM_B1 = 0.9
ADAM_B2 = 0.999
ADAM_EPS = 1e-08
ADAM_WD = 0.01
ADAM_STEP = 10

_REF_COLS = {}
_off = 0
for _n, _s in (("xbc", 2048), ("a_z", 1024), ("a_dt", 16), ("b_q", 1024), ("b_k", 256), ("b_v", 256),
               ("b_z", 1024), ("c_q", 1024), ("c_k", 1024), ("c_v", 1024), ("c_f", 16), ("c_z", 1024),
               ("gates", 3072)):
    _REF_COLS[_n] = (_off, _s)
    _off += _s
N_IN = _off

_PAD_ORDER = (("gates", 3072), ("xbc", 2048), ("a_z", 1024), ("b_q", 1024), ("b_z", 1024), ("c_q", 1024),
              ("c_k", 1024), ("c_v", 1024), ("c_z", 1024), ("b_k", 256), ("b_v", 256), ("a_dt", 512),
              ("c_f", 128))
_PAD_COLS = {}
_off = 0
for _n, _s in _PAD_ORDER:
    _PAD_COLS[_n] = (_off, _s)
    _off += _s
N_USED = _off
N_PAD = 13824


def _cp(sem, vmem_mb=48):
    return pltpu.CompilerParams(dimension_semantics=sem, vmem_limit_bytes=vmem_mb * 1024 * 1024)


def _dot(a, b, dims=((1,), (0,)), precision=None):
    return lax.dot_general(a, b, (dims, ((), ())), preferred_element_type=F32, precision=precision)


def _dot_nt(a, b):
    return _dot(a, b, ((1,), (1,)))


def _dot_tn(a, b):
    return _dot(a, b, ((0,), (0,)))


def _col(v, idx):
    lane = lax.broadcasted_iota(jnp.int32, v.shape, 1)
    return jnp.sum(jnp.where(lane == idx, v, 0.0), axis=1, keepdims=True)


def _row(v, idx):
    row = lax.broadcasted_iota(jnp.int32, v.shape, 0)
    return jnp.sum(jnp.where(row == idx, v, 0.0), axis=0, keepdims=True)


def _iota_col():
    return lax.broadcasted_iota(jnp.int32, (CHUNK, 1), 0)


def _iota_row():
    return lax.broadcasted_iota(jnp.int32, (1, LANES), 1)


def _sigmoid(x):
    return 1.0 / (1.0 + jnp.exp(-x))


def _softplus(x):
    return jnp.maximum(x, 0.0) + jnp.log(1.0 + jnp.exp(-jnp.abs(x)))


def _pad_w_in(w):
    parts = []
    for name, size in _PAD_ORDER:
        s0, sz = _REF_COLS[name]
        seg = w[:, s0:s0 + sz]
        if name == "a_dt":
            seg = jnp.pad(seg.reshape(-1, N_GROUPS, 4), ((0, 0), (0, 0), (0, LANES - 4))).reshape(-1, 512)
        elif name == "c_f":
            seg = jnp.pad(seg, ((0, 0), (0, LANES - 16)))
        parts.append(seg)
    parts.append(jnp.zeros((w.shape[0], N_PAD - N_USED), w.dtype))
    return jnp.concatenate(parts, axis=1)


def _unpad_w_in(wp):
    segs = {}
    for name, _ in _PAD_ORDER:
        p0, psz = _PAD_COLS[name]
        seg = wp[:, p0:p0 + psz]
        if name == "a_dt":
            seg = seg.reshape(-1, N_GROUPS, LANES)[:, :, :4].reshape(-1, 16)
        elif name == "c_f":
            seg = seg[:, :16]
        segs[name] = seg
    order = sorted(_REF_COLS, key=lambda n: _REF_COLS[n][0])
    return jnp.concatenate([segs[n] for n in order], axis=1)


def _group_lanes(v):
    return jnp.pad(v.reshape(N_GROUPS, 1, 4), ((0, 0), (0, 0), (0, LANES - 4)))


def _ungroup_lanes(v):
    return v[:, 0, :4].reshape(16)


def _mm(a, b, *, ta=False, tb=False, tm=512, tn=512, tk=512, out_dtype=F32, add=None, name):
    if ta:
        kdim, m = a.shape
    else:
        m, kdim = a.shape
    if tb:
        n, k2 = b.shape
    else:
        k2, n = b.shape
    assert kdim == k2, (a.shape, b.shape)
    tm, tn, tk = min(tm, m), min(tn, n), min(tk, kdim)
    assert m % tm == 0 and n % tn == 0 and kdim % tk == 0, (m, n, kdim, tm, tn, tk)
    nk = kdim // tk
    a_spec = (pl.BlockSpec((tk, tm), lambda i, j, k: (k, i)) if ta
              else pl.BlockSpec((tm, tk), lambda i, j, k: (i, k)))
    b_spec = (pl.BlockSpec((tn, tk), lambda i, j, k: (j, k)) if tb
              else pl.BlockSpec((tk, tn), lambda i, j, k: (k, j)))
    dims = ((0 if ta else 1,), (1 if tb else 0,))
    has_add = add is not None

    def body(*refs):
        if has_add:
            a_ref, b_ref, add_ref, o_ref, acc_ref = refs
        else:
            a_ref, b_ref, o_ref, acc_ref = refs
        k = pl.program_id(2)
        p = _dot(a_ref[...].astype(BF16), b_ref[...].astype(BF16), dims)

        @pl.when(k == 0)
        def _():
            acc_ref[...] = p

        @pl.when(k > 0)
        def _():
            acc_ref[...] += p

        @pl.when(k == nk - 1)
        def _():
            r = acc_ref[...]
            if has_add:
                r = r + add_ref[...]
            o_ref[...] = r.astype(out_dtype)

    in_specs = [a_spec, b_spec]
    args = [a, b]
    if has_add:
        in_specs.append(pl.BlockSpec((tm, tn), lambda i, j, k: (i, j)))
        args.append(add)
    return pl.pallas_call(
        body, name=name, grid=(m // tm, n // tn, nk),
        in_specs=in_specs, out_specs=pl.BlockSpec((tm, tn), lambda i, j, k: (i, j)),
        out_shape=jax.ShapeDtypeStruct((m, n), out_dtype),
        scratch_shapes=[pltpu.VMEM((tm, tn), F32)],
        compiler_params=_cp(("parallel", "parallel", "arbitrary")),
    )(*args)


def _rms_fwd(x, w, *, name, tm=512):
    t, d = x.shape

    def body(x_ref, w_ref, o_ref, ot_ref):
        xv = x_ref[...]
        r = lax.rsqrt(jnp.mean(xv * xv, axis=1, keepdims=True) + NORM_EPS)
        h = xv * r * w_ref[...]
        o_ref[...] = h.astype(BF16)
        ot_ref[...] = h.T.astype(BF16)

    return pl.pallas_call(
        body, name=name, grid=(t // tm,),
        in_specs=[pl.BlockSpec((tm, d), lambda i: (i, 0)), pl.BlockSpec((1, d), lambda i: (0, 0))],
        out_specs=[pl.BlockSpec((tm, d), lambda i: (i, 0)), pl.BlockSpec((d, tm), lambda i: (0, i))],
        out_shape=[jax.ShapeDtypeStruct((t, d), BF16), jax.ShapeDtypeStruct((d, t), BF16)],
        compiler_params=_cp(("parallel",)),
    )(x, w.reshape(1, d))


def _rms_bwd(x, w, dh, dres, *, name, tm=512):
    t, d = x.shape

    def body(x_ref, w_ref, dh_ref, dres_ref, dx_ref, dw_ref):
        xv = x_ref[...]
        r = lax.rsqrt(jnp.mean(xv * xv, axis=1, keepdims=True) + NORM_EPS)
        xhat = xv * r
        dhv = dh_ref[...]
        dxhat = dhv * w_ref[...]
        dx = r * (dxhat - xhat * jnp.mean(dxhat * xhat, axis=1, keepdims=True))
        dx_ref[...] = dres_ref[...] + dx

        @pl.when(pl.program_id(0) == 0)
        def _():
            dw_ref[...] = jnp.zeros_like(dw_ref)

        dw_ref[...] += jnp.sum(dhv * xhat, axis=0, keepdims=True)

    return pl.pallas_call(
        body, name=name, grid=(t // tm,),
        in_specs=[pl.BlockSpec((tm, d), lambda i: (i, 0)), pl.BlockSpec((1, d), lambda i: (0, 0)),
                  pl.BlockSpec((tm, d), lambda i: (i, 0)), pl.BlockSpec((tm, d), lambda i: (i, 0))],
        out_specs=[pl.BlockSpec((tm, d), lambda i: (i, 0)), pl.BlockSpec((1, d), lambda i: (0, 0))],
        out_shape=[jax.ShapeDtypeStruct((t, d), F32), jax.ShapeDtypeStruct((1, d), F32)],
        compiler_params=_cp(("arbitrary",)),
    )(x, w.reshape(1, d), dh, dres)


def _final_loss(x, w, target, *, name, tm=512):
    t, d = x.shape

    def body(x_ref, w_ref, t_ref, loss_ref, dx_ref, dw_ref):
        xv = x_ref[...]
        wv = w_ref[...]
        r = lax.rsqrt(jnp.mean(xv * xv, axis=1, keepdims=True) + NORM_EPS)
        xhat = xv * r
        err = xhat * wv - t_ref[...]
        dy = err * (1.0 / d)
        dxhat = dy * wv
        dx_ref[...] = r * (dxhat - xhat * jnp.mean(dxhat * xhat, axis=1, keepdims=True))

        @pl.when(pl.program_id(0) == 0)
        def _():
            dw_ref[...] = jnp.zeros_like(dw_ref)
            loss_ref[...] = jnp.zeros_like(loss_ref)

        dw_ref[...] += jnp.sum(dy * xhat, axis=0, keepdims=True)
        part = 0.5 * jnp.sum(jnp.mean(err * err, axis=1, keepdims=True), axis=0, keepdims=True)
        loss_ref[...] += jnp.broadcast_to(part, loss_ref.shape)

    return pl.pallas_call(
        body, name=name, grid=(t // tm,),
        in_specs=[pl.BlockSpec((tm, d), lambda i: (i, 0)), pl.BlockSpec((1, d), lambda i: (0, 0)),
                  pl.BlockSpec((tm, d), lambda i: (i, 0))],
        out_specs=[pl.BlockSpec((8, LANES), lambda i: (0, 0)), pl.BlockSpec((tm, d), lambda i: (i, 0)),
                   pl.BlockSpec((1, d), lambda i: (0, 0))],
        out_shape=[jax.ShapeDtypeStruct((8, LANES), F32), jax.ShapeDtypeStruct((t, d), F32),
                   jax.ShapeDtypeStruct((1, d), F32)],
        compiler_params=_cp(("arbitrary",)),
    )(x, w.reshape(1, d), target)


_CB = 128


def _conv_pre(u, w_ref, b_ref):
    s = u.shape[0]
    row = lax.broadcasted_iota(jnp.int32, u.shape, 0)
    pre = b_ref[...] + w_ref[CONV_WIDTH - 1:CONV_WIDTH, :] * u
    for sh in range(1, CONV_WIDTH):
        shifted = jnp.where(row >= sh, pltpu.roll(u, sh, 0), 0.0)
        pre = pre + w_ref[CONV_WIDTH - 1 - sh:CONV_WIDTH - sh, :] * shifted
    return pre


def _conv_fwd(proj3, cw, cb, *, name):
    b, s, _ = proj3.shape
    c0 = _PAD_COLS["xbc"][0] // _CB

    def body(u_ref, w_ref, b_ref, o_ref):
        pre = _conv_pre(u_ref[...], w_ref, b_ref)
        o_ref[...] = pre * _sigmoid(pre)

    return pl.pallas_call(
        body, name=name, grid=(b, CONV_DIM // _CB),
        in_specs=[pl.BlockSpec((None, s, _CB), lambda i, j: (i, 0, c0 + j)),
                  pl.BlockSpec((CONV_WIDTH, _CB), lambda i, j: (0, j)),
                  pl.BlockSpec((1, _CB), lambda i, j: (0, j))],
        out_specs=pl.BlockSpec((None, s, _CB), lambda i, j: (i, 0, j)),
        out_shape=jax.ShapeDtypeStruct((b, s, CONV_DIM), F32),
        compiler_params=_cp(("parallel", "parallel")),
    )(proj3, cw, cb.reshape(1, CONV_DIM))


def _conv_bwd(proj3, cw, cb, dact, *, name):
    b, s, _ = proj3.shape
    c0 = _PAD_COLS["xbc"][0] // _CB

    def body(u_ref, w_ref, b_ref, da_ref, du_ref, dwb_ref):
        u = u_ref[...]
        pre = _conv_pre(u, w_ref, b_ref)
        sg = _sigmoid(pre)
        dpre = da_ref[...] * (sg * (1.0 + pre * (1.0 - sg)))
        row = lax.broadcasted_iota(jnp.int32, u.shape, 0)
        du = w_ref[CONV_WIDTH - 1:CONV_WIDTH, :] * dpre
        rows = [jnp.sum(dpre * u, axis=0, keepdims=True)]
        for sh in range(1, CONV_WIDTH):
            fwd_shift = jnp.where(row < s - sh, pltpu.roll(dpre, s - sh, 0), 0.0)
            du = du + w_ref[CONV_WIDTH - 1 - sh:CONV_WIDTH - sh, :] * fwd_shift
            ush = jnp.where(row >= sh, pltpu.roll(u, sh, 0), 0.0)
            rows.append(jnp.sum(dpre * ush, axis=0, keepdims=True))
        du_ref[...] = du.astype(BF16)

        @pl.when(pl.program_id(1) == 0)
        def _():
            dwb_ref[...] = jnp.zeros_like(dwb_ref)

        for sh in range(CONV_WIDTH):
            k = CONV_WIDTH - 1 - sh
            dwb_ref[k:k + 1, :] += rows[sh]
        dwb_ref[CONV_WIDTH:CONV_WIDTH + 1, :] += jnp.sum(dpre, axis=0, keepdims=True)

    return pl.pallas_call(
        body, name=name, grid=(CONV_DIM // _CB, b),
        in_specs=[pl.BlockSpec((None, s, _CB), lambda j, i: (i, 0, c0 + j)),
                  pl.BlockSpec((CONV_WIDTH, _CB), lambda j, i: (0, j)),
                  pl.BlockSpec((1, _CB), lambda j, i: (0, j)),
                  pl.BlockSpec((None, s, _CB), lambda j, i: (i, 0, j))],
        out_specs=[pl.BlockSpec((None, s, _CB), lambda j, i: (i, 0, j)),
                   pl.BlockSpec((8, _CB), lambda j, i: (0, j))],
        out_shape=[jax.ShapeDtypeStruct((b, s, CONV_DIM), BF16), jax.ShapeDtypeStruct((8, CONV_DIM), F32)],
        compiler_params=_cp(("parallel", "arbitrary")),
    )(proj3, cw, cb.reshape(1, CONV_DIM), dact)


def _ssd_common(dt_ref, dtb_ref, alog_ref):
    row = lax.broadcasted_iota(jnp.int32, (CHUNK, CHUNK), 0)
    lane = lax.broadcasted_iota(jnp.int32, (CHUNK, CHUNK), 1)
    causal = row >= lane
    tri = causal.astype(F32)
    dtv = _softplus(dt_ref[...] + dtb_ref[...])
    a_row = -jnp.exp(alog_ref[...])
    acum = _dot(tri, dtv * a_row, precision=HIGHEST)
    return row, lane, causal, dtv, a_row, acum, acum.T


def _ssd_pair(pp, x, dtv, acum, acum_t, causal, lane, row):
    lo = lane < HEAD_DIM
    r0, r1 = 2 * pp, 2 * pp + 1
    dtp = jnp.where(lo, _col(dtv, r0), _col(dtv, r1))
    ac0, ac1 = _col(acum, r0), _col(acum, r1)
    ar0, ar1 = _row(acum_t, r0), _row(acum_t, r1)
    d0 = jnp.where(causal, jnp.exp(jnp.where(causal, ac0 - ar0, 0.0)), 0.0)
    d1 = jnp.where(causal, jnp.exp(jnp.where(causal, ac1 - ar1, 0.0)), 0.0)
    al0, al1 = _col(ar0, CHUNK - 1), _col(ar1, CHUNK - 1)
    eac = jnp.where(lo, jnp.exp(ac0), jnp.exp(ac1))
    dsp = jnp.where(lo, jnp.exp(al0 - ac0), jnp.exp(al1 - ac1))
    eal = jnp.where(_iota_col() < HEAD_DIM, jnp.exp(al0), jnp.exp(al1))
    return lo, dtp, x * dtp, d0, d1, al0, al1, eac, dsp, eal


def _ssd_fwd(proj3, xact3, dtb, alog, dsk, nw, *, name):
    b, s, _ = proj3.shape
    nc = s // CHUNK
    dt0 = _PAD_COLS["a_dt"][0] // 512
    z0 = _PAD_COLS["a_z"][0] // D_MODEL

    def body(xs_ref, bm_ref, cm_ref, dt_ref, z_ref, dtb_ref, alog_ref, dsk_ref, nw_ref,
             ya_ref, ypre_ref, hst_ref, h_scr):
        @pl.when(pl.program_id(1) == 0)
        def _():
            h_scr[...] = jnp.zeros_like(h_scr)

        for g in range(N_GROUPS):
            w256 = pl.ds(256 * g, 256)
            w128 = pl.ds(LANES * g, LANES)
            group(xs_ref.at[:, w256], bm_ref.at[:, w128], cm_ref.at[:, w128], dt_ref.at[:, w128],
                  z_ref.at[:, w256], dtb_ref.at[g], alog_ref.at[g], dsk_ref.at[g], nw_ref.at[g],
                  ya_ref.at[:, w256], ypre_ref.at[:, w256], hst_ref.at[g], h_scr.at[g])

    def group(xs_ref, bm_ref, cm_ref, dt_ref, z_ref, dtb_ref, alog_ref, dsk_ref, nw_ref,
              ya_ref, ypre_ref, hst_ref, h_scr):
        row, lane, causal, dtv, a_row, acum, acum_t = _ssd_common(dt_ref, dtb_ref, alog_ref)
        bb = bm_ref[...].astype(BF16)
        cb = cm_ref[...].astype(BF16)
        cbm = _dot_nt(cb, bb)
        hst_ref[...] = h_scr[...]
        dskv = dsk_ref[...]
        for pp in range(2):
            x = xs_ref[:, LANES * pp:LANES * (pp + 1)]
            lo, dtp, xd, d0, d1, al0, al1, eac, dsp, eal = _ssd_pair(pp, x, dtv, acum, acum_t, causal, lane, row)
            xdb = xd.astype(BF16)
            y = jnp.where(lo, _dot((cbm * d0).astype(BF16), xdb), _dot((cbm * d1).astype(BF16), xdb))
            h = h_scr[pp]
            y = y + eac * _dot_nt(cb, h.astype(BF16))
            h_scr[pp] = h * eal + _dot_tn((xd * dsp).astype(BF16), bb)
            dskp = jnp.where((_iota_row() < HEAD_DIM), _col(dskv, 2 * pp), _col(dskv, 2 * pp + 1))
            ypre_ref[:, LANES * pp:LANES * (pp + 1)] = y + x * dskp
        ypre = ypre_ref[...]
        z = z_ref[...]
        yg = ypre * (z * _sigmoid(z))
        rstd = lax.rsqrt(jnp.sum(yg * yg, axis=1, keepdims=True) * (1.0 / 256.0) + NORM_EPS)
        ya_ref[...] = (yg * rstd * nw_ref[...]).astype(BF16)

    g = N_GROUPS
    par = pl.BlockSpec((g, 1, LANES), lambda i, c: (0, 0, 0))
    wide = pl.BlockSpec((None, CHUNK, D_MODEL), lambda i, c: (i, c, 0))
    return pl.pallas_call(
        body, name=name, grid=(b, nc),
        in_specs=[wide,
                  pl.BlockSpec((None, CHUNK, 512), lambda i, c: (i, c, 2)),
                  pl.BlockSpec((None, CHUNK, 512), lambda i, c: (i, c, 3)),
                  pl.BlockSpec((None, CHUNK, 512), lambda i, c: (i, c, dt0)),
                  pl.BlockSpec((None, CHUNK, D_MODEL), lambda i, c: (i, c, z0)),
                  par, par, par,
                  pl.BlockSpec((g, 1, 256), lambda i, c: (0, 0, 0))],
        out_specs=[wide, wide,
                   pl.BlockSpec((None, None, g, 2, CHUNK, SSM_STATE), lambda i, c: (i, c, 0, 0, 0, 0))],
        out_shape=[jax.ShapeDtypeStruct((b, s, D_MODEL), BF16), jax.ShapeDtypeStruct((b, s, D_MODEL), F32),
                   jax.ShapeDtypeStruct((b, nc, g, 2, CHUNK, SSM_STATE), F32)],
        scratch_shapes=[pltpu.VMEM((g, 2, CHUNK, SSM_STATE), F32)],
        compiler_params=_cp(("parallel", "arbitrary")),
    )(xact3, xact3, xact3, proj3, proj3, dtb, alog, dsk, nw)


def _ssd_bwd(proj3, xact3, dtb, alog, dsk, nw, ypre3, hst, dya3, *, name):
    b, s, _ = proj3.shape
    nc = s // CHUNK
    dt0 = _PAD_COLS["a_dt"][0] // 512
    z0 = _PAD_COLS["a_z"][0] // D_MODEL

    def body(xs_ref, bm_ref, cm_ref, dt_ref, z_ref, dtb_ref, alog_ref, dsk_ref, nw_ref, ypre_ref, hst_ref,
             dya_ref, dact_ref, dz_ref, ddt_ref, ddtb_ref, dalog_ref, ddsk_ref, dnw_ref, dh_scr):
        first = jnp.logical_and(pl.program_id(0) == 0, pl.program_id(1) == 0)

        @pl.when(first)
        def _():
            ddtb_ref[...] = jnp.zeros_like(ddtb_ref)
            dalog_ref[...] = jnp.zeros_like(dalog_ref)
            ddsk_ref[...] = jnp.zeros_like(ddsk_ref)
            dnw_ref[...] = jnp.zeros_like(dnw_ref)

        @pl.when(pl.program_id(1) == 0)
        def _():
            dh_scr[...] = jnp.zeros_like(dh_scr)

        for g in range(N_GROUPS):
            w256 = pl.ds(256 * g, 256)
            w128 = pl.ds(LANES * g, LANES)
            group(xs_ref.at[:, w256], bm_ref.at[:, w128], cm_ref.at[:, w128], dt_ref.at[:, w128],
                  z_ref.at[:, w256], dtb_ref.at[g], alog_ref.at[g], dsk_ref.at[g], nw_ref.at[g],
                  ypre_ref.at[:, w256], hst_ref.at[g], dya_ref.at[:, w256],
                  dact_ref.at[:, w256], dact_ref.at[:, pl.ds(D_MODEL + LANES * g, LANES)],
                  dact_ref.at[:, pl.ds(D_MODEL + 512 + LANES * g, LANES)], dz_ref.at[:, w256], ddt_ref.at[:, w128],
                  ddtb_ref.at[g], dalog_ref.at[g], ddsk_ref.at[g], dnw_ref.at[g], dh_scr.at[g])

    def group(xs_ref, bm_ref, cm_ref, dt_ref, z_ref, dtb_ref, alog_ref, dsk_ref, nw_ref, ypre_ref, hst_ref,
              dya_ref, dxs_ref, dbm_ref, dcm_ref, dz_ref, ddt_ref, ddtb_ref, dalog_ref, ddsk_ref, dnw_ref,
              dh_scr):
        row, lane, causal, dtv, a_row, acum, acum_t = _ssd_common(dt_ref, dtb_ref, alog_ref)
        lane1 = _iota_row()
        bb = bm_ref[...].astype(BF16)
        cb = cm_ref[...].astype(BF16)
        cbm = _dot_nt(cb, bb)

        z = z_ref[...]
        ypre = ypre_ref[...]
        dya = dya_ref[...]
        sz = _sigmoid(z)
        silu = z * sz
        yg = ypre * silu
        rstd = lax.rsqrt(jnp.sum(yg * yg, axis=1, keepdims=True) * (1.0 / 256.0) + NORM_EPS)
        dnw_ref[...] += jnp.sum(dya * yg * rstd, axis=0, keepdims=True)
        dn = dya * nw_ref[...]
        dyg = rstd * dn - yg * (rstd * rstd * rstd * (1.0 / 256.0)) * jnp.sum(dn * yg, axis=1, keepdims=True)
        dz_ref[...] = (dyg * ypre * (sz * (1.0 + z * (1.0 - sz)))).astype(BF16)
        dy_all = dyg * silu

        dskv = dsk_ref[...]
        da_cols = jnp.zeros((CHUNK, LANES), F32)
        dxt_cols = jnp.zeros((CHUNK, LANES), F32)
        ddsk_row = jnp.zeros((1, LANES), F32)
        dcb = jnp.zeros((CHUNK, CHUNK), F32)
        dc = jnp.zeros((CHUNK, SSM_STATE), F32)
        db = jnp.zeros((CHUNK, SSM_STATE), F32)
        last = _iota_col() == CHUNK - 1
        for pp in range(2):
            r0, r1 = 2 * pp, 2 * pp + 1
            x = xs_ref[:, LANES * pp:LANES * (pp + 1)]
            dy = dy_all[:, LANES * pp:LANES * (pp + 1)]
            lo, dtp, xd, d0, d1, al0, al1, eac, dsp, eal = _ssd_pair(pp, x, dtv, acum, acum_t, causal, lane, row)
            w0, w1 = cbm * d0, cbm * d1
            w0b, w1b = w0.astype(BF16), w1.astype(BF16)
            xdb = xd.astype(BF16)
            dyb = dy.astype(BF16)
            h = hst_ref[pp]
            dhn = dh_scr[pp]
            hb = h.astype(BF16)
            dhb = dhn.astype(BF16)
            g0 = _dot_nt(jnp.where(lo, dy, 0.0).astype(BF16), xdb)
            g1 = _dot_nt(jnp.where(lo, 0.0, dy).astype(BF16), xdb)
            dcb = dcb + g0 * d0 + g1 * d1
            m0, m1 = g0 * w0, g1 * w1
            bdh = _dot_nt(bb, dhb)
            dxd = jnp.where(lo, _dot_tn(w0b, dyb), _dot_tn(w1b, dyb)) + dsp * bdh
            ch = _dot_nt(cb, hb)
            edy = eac * dy
            edyb = edy.astype(BF16)
            xds = xd * dsp
            dc = dc + _dot(edyb, hb)
            db = db + _dot(xds.astype(BF16), dhb)
            dh_scr[pp] = dhn * eal + _dot_tn(edyb, cb)
            t2 = edy * ch
            t3 = xds * bdh
            r4 = jnp.sum(dhn * h, axis=1, keepdims=True)
            s4_0 = jnp.sum(jnp.where(_iota_col() < HEAD_DIM, r4, 0.0), axis=0, keepdims=True)
            s4_1 = jnp.sum(r4, axis=0, keepdims=True) - s4_0
            t2_0 = jnp.sum(jnp.where(lo, t2, 0.0), axis=1, keepdims=True)
            t2_1 = jnp.sum(t2, axis=1, keepdims=True) - t2_0
            t3_0 = jnp.sum(jnp.where(lo, t3, 0.0), axis=1, keepdims=True)
            t3_1 = jnp.sum(t3, axis=1, keepdims=True) - t3_0
            dal0 = jnp.sum(t3_0, axis=0, keepdims=True) + jnp.exp(al0) * s4_0
            dal1 = jnp.sum(t3_1, axis=0, keepdims=True) + jnp.exp(al1) * s4_1
            dac0 = (jnp.sum(m0, axis=1, keepdims=True) - jnp.sum(m0.T, axis=1, keepdims=True)
                    + t2_0 - t3_0 + jnp.where(last, dal0, 0.0))
            dac1 = (jnp.sum(m1, axis=1, keepdims=True) - jnp.sum(m1.T, axis=1, keepdims=True)
                    + t2_1 - t3_1 + jnp.where(last, dal1, 0.0))
            da_cols = da_cols + jnp.where(lane == r0, dac0, 0.0) + jnp.where(lane == r1, dac1, 0.0)
            xx = dxd * x
            x0 = jnp.sum(jnp.where(lo, xx, 0.0), axis=1, keepdims=True)
            x1 = jnp.sum(xx, axis=1, keepdims=True) - x0
            dxt_cols = dxt_cols + jnp.where(lane == r0, x0, 0.0) + jnp.where(lane == r1, x1, 0.0)
            dskp = jnp.where((_iota_row() < HEAD_DIM), _col(dskv, r0), _col(dskv, r1))
            dxs_ref[:, LANES * pp:LANES * (pp + 1)] = dxd * dtp + dy * dskp
            yx = jnp.sum(dy * x, axis=0, keepdims=True)
            k0 = jnp.sum(jnp.where((_iota_row() < HEAD_DIM), yx, 0.0), axis=1, keepdims=True)
            k1 = jnp.sum(yx, axis=1, keepdims=True) - k0
            ddsk_row = ddsk_row + jnp.where(lane1 == r0, k0, 0.0) + jnp.where(lane1 == r1, k1, 0.0)
        dcbb = dcb.astype(BF16)
        dcm_ref[...] = dc + _dot(dcbb, bb)
        dbm_ref[...] = db + _dot_tn(dcbb, cb)
        tri_t = (row <= lane).astype(F32)
        dadt = _dot(tri_t, da_cols, precision=HIGHEST)
        ddtv = dadt * a_row + dxt_cols
        dalog_ref[...] += jnp.sum(dadt * dtv, axis=0, keepdims=True) * a_row
        ddt_raw = ddtv * _sigmoid(dt_ref[...] + dtb_ref[...])
        ddt_ref[...] = ddt_raw.astype(BF16)
        ddtb_ref[...] += jnp.sum(ddt_raw, axis=0, keepdims=True)
        ddsk_ref[...] += ddsk_row

    g = N_GROUPS
    rc = lambda c: nc - 1 - c
    par = pl.BlockSpec((g, 1, LANES), lambda i, c: (0, 0, 0))
    parw = pl.BlockSpec((g, 1, 256), lambda i, c: (0, 0, 0))
    wide = pl.BlockSpec((None, CHUNK, D_MODEL), lambda i, c: (i, rc(c), 0))
    blk512 = lambda col: pl.BlockSpec((None, CHUNK, 512), lambda i, c: (i, rc(c), col))
    return pl.pallas_call(
        body, name=name, grid=(b, nc),
        in_specs=[wide, blk512(2), blk512(3), blk512(dt0),
                  pl.BlockSpec((None, CHUNK, D_MODEL), lambda i, c: (i, rc(c), z0)),
                  par, par, par, parw,
                  wide,
                  pl.BlockSpec((None, None, g, 2, CHUNK, SSM_STATE), lambda i, c: (i, rc(c), 0, 0, 0, 0)),
                  wide],
        out_specs=[pl.BlockSpec((None, CHUNK, CONV_DIM), lambda i, c: (i, rc(c), 0)), wide, blk512(0),
                   par, par, par, parw],
        out_shape=[jax.ShapeDtypeStruct((b, s, CONV_DIM), F32), jax.ShapeDtypeStruct((b, s, D_MODEL), BF16),
                   jax.ShapeDtypeStruct((b, s, 512), BF16),
                   jax.ShapeDtypeStruct((g, 1, LANES), F32), jax.ShapeDtypeStruct((g, 1, LANES), F32),
                   jax.ShapeDtypeStruct((g, 1, LANES), F32), jax.ShapeDtypeStruct((g, 1, 256), F32)],
        scratch_shapes=[pltpu.VMEM((g, 2, CHUNK, SSM_STATE), F32)],
        compiler_params=_cp(("arbitrary", "arbitrary")),
    )(xact3, xact3, xact3, proj3, proj3, dtb, alog, dsk, nw, ypre3, hst, dya3)


def _fgate_fwd(proj3, fb, *, name):
    b, s, _ = proj3.shape
    f0 = _PAD_COLS["c_f"][0] // LANES

    def body(f_ref, fb_ref, cum_ref, carry):
        @pl.when(pl.program_id(1) == 0)
        def _():
            carry[...] = jnp.zeros_like(carry)

        row = lax.broadcasted_iota(jnp.int32, (CHUNK, CHUNK), 0)
        lane = lax.broadcasted_iota(jnp.int32, (CHUNK, CHUNK), 1)
        tri = (row >= lane).astype(F32)
        lf = -_softplus(-(f_ref[...] + fb_ref[...]))
        cs = _dot(tri, lf, precision=HIGHEST) + carry[0:1, :]
        cum_ref[...] = cs
        carry[0:1, :] = _row(cs, CHUNK - 1)

    return pl.pallas_call(
        body, name=name, grid=(b, s // CHUNK),
        in_specs=[pl.BlockSpec((None, CHUNK, LANES), lambda i, c: (i, c, f0)),
                  pl.BlockSpec((1, LANES), lambda i, c: (0, 0))],
        out_specs=pl.BlockSpec((None, CHUNK, LANES), lambda i, c: (i, c, 0)),
        out_shape=jax.ShapeDtypeStruct((b, s, LANES), F32),
        scratch_shapes=[pltpu.VMEM((8, LANES), F32)],
        compiler_params=_cp(("parallel", "arbitrary")),
    )(proj3, fb)


def _fgate_bwd(proj3, fb, dcum, *, name):
    b, s, _ = proj3.shape
    nc = s // CHUNK
    f0 = _PAD_COLS["c_f"][0] // LANES
    npair = dcum.shape[1]

    def body(f_ref, fb_ref, dc_ref, df_ref, dfb_ref, carry):
        first = jnp.logical_and(pl.program_id(0) == 0, pl.program_id(1) == 0)

        @pl.when(first)
        def _():
            dfb_ref[...] = jnp.zeros_like(dfb_ref)

        @pl.when(pl.program_id(1) == 0)
        def _():
            carry[...] = jnp.zeros_like(carry)

        row = lax.broadcasted_iota(jnp.int32, (CHUNK, CHUNK), 0)
        lane = lax.broadcasted_iota(jnp.int32, (CHUNK, CHUNK), 1)
        tri_t = (row <= lane).astype(F32)
        dc = -jnp.sum(dc_ref[...], axis=0)
        dlf = _dot(tri_t, dc, precision=HIGHEST) + carry[0:1, :]
        carry[0:1, :] = _row(dlf, 0)
        df = dlf * _sigmoid(-(f_ref[...] + fb_ref[...]))
        df_ref[...] = df.astype(BF16)
        dfb_ref[...] += jnp.sum(df, axis=0, keepdims=True)

    return pl.pallas_call(
        body, name=name, grid=(b, nc),
        in_specs=[pl.BlockSpec((None, CHUNK, LANES), lambda i, c: (i, nc - 1 - c, f0)),
                  pl.BlockSpec((1, LANES), lambda i, c: (0, 0)),
                  pl.BlockSpec((None, npair, CHUNK, LANES), lambda i, c: (i, 0, nc - 1 - c, 0))],
        out_specs=[pl.BlockSpec((None, CHUNK, LANES), lambda i, c: (i, nc - 1 - c, 0)),
                   pl.BlockSpec((1, LANES), lambda i, c: (0, 0))],
        out_shape=[jax.ShapeDtypeStruct((b, s, LANES), BF16), jax.ShapeDtypeStruct((1, LANES), F32)],
        scratch_shapes=[pltpu.VMEM((8, LANES), F32)],
        compiler_params=_cp(("arbitrary", "arbitrary")),
    )(proj3, fb, dcum)


_SCALE = HEAD_DIM ** -0.5
_NEG = -1e30


_ST_LSE, _ST_DELTA, _ST_MJ = 0, 2, 8


_SR = 40


def _ck_rep(cum):
    b, s, _ = cum.shape
    t = jnp.transpose(cum[:, :, :N_HEADS], (0, 2, 1)).reshape(b, N_HEADS // 2, 2, s, 1)
    return jnp.broadcast_to(t, (b, N_HEADS // 2, 2, s, LANES))


def _foxt_fwd(proj3, ckrep, *, name, tb):
    b, s, _ = proj3.shape
    nq = s // tb
    assert _ST_MJ + 2 * nq <= _SR
    q0 = _PAD_COLS["c_q"][0] // LANES
    k0 = _PAD_COLS["c_k"][0] // LANES
    v0 = _PAD_COLS["c_v"][0] // LANES
    z0 = _PAD_COLS["c_z"][0] // LANES
    rep = tb // LANES

    def body(q_ref, k_ref, v_ref, z_ref, ck_ref, y_ref, o_ref, st_ref):
        i = pl.program_id(2)
        lane = lax.broadcasted_iota(jnp.int32, (tb, LANES), 1)
        lo = lane < HEAD_DIM
        lo_r = lax.broadcasted_iota(jnp.int32, (LANES, tb), 0) < HEAD_DIM
        srow = lax.broadcasted_iota(jnp.int32, (_SR, tb), 0)
        q = q_ref[...] * _SCALE
        qms = (jnp.where(lo, q, 0.0).astype(BF16), jnp.where(lo, 0.0, q).astype(BF16))
        ones_at = (HEAD_DIM, 0)

        def block(j, carry, diagonal):
            ks = pl.ds(pl.multiple_of(j * tb, tb), tb)
            kb = k_ref[ks, :].astype(BF16)
            v = v_ref[ks, :].astype(F32)
            vts = (jnp.where(lo, v, jnp.where(lane == ones_at[0], 1.0, 0.0)).T.astype(BF16),
                   jnp.where(lo, jnp.where(lane == ones_at[1], 1.0, 0.0), v).T.astype(BF16))
            if diagonal:
                key = lax.broadcasted_iota(jnp.int32, (tb, tb), 0)
                qry = lax.broadcasted_iota(jnp.int32, (tb, tb), 1)
                mask = key <= qry
            ms, ls, acc, st = carry
            new_m, new_l, pvs, alphas = [], [], [], []
            for hh in range(2):
                sc = _dot_nt(kb, qms[hh]) - jnp.tile(ck_ref[hh, ks, :], (1, rep))
                if diagonal:
                    sc = jnp.where(mask, sc, _NEG)
                m_new = jnp.maximum(ms[hh], jnp.max(sc, axis=0, keepdims=True))
                alpha = jnp.exp(ms[hh] - m_new)
                pv = _dot(vts[hh], jnp.exp(sc - m_new).astype(BF16))
                rs = _row(pv[ones_at[hh]:ones_at[hh] + 8, :], 0)
                new_l.append(alpha * ls[hh] + rs)
                new_m.append(m_new)
                pvs.append(pv)
                alphas.append(alpha)
                st = jnp.where(srow == _ST_MJ + 2 * j + hh, m_new, st)
            acc = jnp.where(lo_r, alphas[0] * acc + pvs[0], alphas[1] * acc + pvs[1])
            return (tuple(new_m), tuple(new_l), acc, st)

        neg = jnp.full((1, tb), _NEG, F32)
        zero = jnp.zeros((1, tb), F32)
        init = ((neg, neg), (zero, zero), jnp.zeros((LANES, tb), F32), jnp.zeros((_SR, tb), F32))
        carry = lax.fori_loop(0, i, lambda j, c: block(j, c, False), init)
        ms, ls, acc, st = block(i, carry, True)
        o = (acc / jnp.where(lo_r, ls[0], ls[1])).T
        o_ref[...] = o
        st = jnp.where(srow == _ST_LSE, ms[0] + jnp.log(ls[0]), st)
        st_ref[...] = jnp.where(srow == _ST_LSE + 1, ms[1] + jnp.log(ls[1]), st)
        z = z_ref[...]
        y_ref[...] = (o * (z * _sigmoid(z))).astype(BF16)

    qspec = lambda c0: pl.BlockSpec((None, tb, LANES), lambda bi, p, i: (bi, i, c0 + p))
    kspec = lambda c0: pl.BlockSpec((None, s, LANES), lambda bi, p, i: (bi, 0, c0 + p))
    ospec = pl.BlockSpec((None, tb, LANES), lambda bi, p, i: (bi, i, p))
    return pl.pallas_call(
        body, name=name, grid=(b, N_HEADS // 2, nq),
        in_specs=[qspec(q0), kspec(k0), kspec(v0), qspec(z0),
                  pl.BlockSpec((None, None, 2, s, LANES), lambda bi, p, i: (bi, p, 0, 0, 0))],
        out_specs=[ospec, ospec, pl.BlockSpec((None, None, None, _SR, tb), lambda bi, p, i: (bi, p, i, 0, 0))],
        out_shape=[jax.ShapeDtypeStruct((b, s, D_MODEL), BF16), jax.ShapeDtypeStruct((b, s, D_MODEL), F32),
                   jax.ShapeDtypeStruct((b, N_HEADS // 2, nq, _SR, tb), F32)],
        compiler_params=_cp(("parallel", "parallel", "arbitrary")),
    )(proj3, proj3, proj3, proj3, ckrep)


def _foxt_prep(proj3, o3, stat, dy3, *, name, tb):
    b, s, _ = proj3.shape
    nq = s // tb
    z0 = _PAD_COLS["c_z"][0] // LANES

    def body(z_ref, o_ref, fst_ref, dy_ref, dz_ref, do_ref, st_ref):
        z = z_ref[...]
        sz = _sigmoid(z)
        dy = dy_ref[...]
        o = o_ref[...]
        do = dy * (z * sz)
        dz_ref[...] = (dy * o * (sz * (1.0 + z * (1.0 - sz)))).astype(BF16)
        do_ref[...] = do
        doo = do.astype(BF16).astype(F32) * o
        r8 = lax.broadcasted_iota(jnp.int32, (8, LANES), 0)
        l8 = lax.broadcasted_iota(jnp.int32, (8, LANES), 1)
        pick = jnp.logical_or(jnp.logical_and(r8 == 0, l8 < HEAD_DIM),
                              jnp.logical_and(r8 == 1, l8 >= HEAD_DIM)).astype(F32)
        d8 = _dot(pick, doo, ((1,), (1,)), precision=HIGHEST)
        srow = lax.broadcasted_iota(jnp.int32, (_SR, tb), 0)
        st = jnp.where(srow == _ST_DELTA, _row(d8, 0), fst_ref[...])
        st_ref[...] = jnp.where(srow == _ST_DELTA + 1, _row(d8, 1), st)

    ospec = pl.BlockSpec((None, tb, LANES), lambda bi, p, i: (bi, i, p))
    sspec = pl.BlockSpec((None, None, None, _SR, tb), lambda bi, p, i: (bi, p, i, 0, 0))
    return pl.pallas_call(
        body, name=name, grid=(b, N_HEADS // 2, nq),
        in_specs=[pl.BlockSpec((None, tb, LANES), lambda bi, p, i: (bi, i, z0 + p)), ospec, sspec, ospec],
        out_specs=[ospec, ospec, sspec],
        out_shape=[jax.ShapeDtypeStruct((b, s, D_MODEL), BF16), jax.ShapeDtypeStruct((b, s, D_MODEL), F32),
                   jax.ShapeDtypeStruct((b, N_HEADS // 2, nq, _SR, tb), F32)],
        compiler_params=_cp(("parallel", "parallel", "parallel")),
    )(proj3, o3, stat, dy3)


def _foxt_bwd(proj3, ckrep, do3, stats, *, name, tb):
    b, s, _ = proj3.shape
    nq = s // tb
    q0 = _PAD_COLS["c_q"][0] // LANES
    k0 = _PAD_COLS["c_k"][0] // LANES
    v0 = _PAD_COLS["c_v"][0] // LANES
    rep = tb // LANES

    def body(q_ref, do_ref, st_ref, k_ref, v_ref, ck_ref, dq_ref, dk_ref, dv_ref, cs_ref):
        j = pl.program_id(2)
        lane = lax.broadcasted_iota(jnp.int32, (tb, LANES), 1)
        lo = lane < HEAD_DIM
        lo_r = lax.broadcasted_iota(jnp.int32, (LANES, tb), 0) < HEAD_DIM

        @pl.when(j == 0)
        def _():
            dq_ref[...] = jnp.zeros_like(dq_ref)

        kf = k_ref[...].astype(F32)
        kb = kf.astype(BF16)
        kt = kf.T.astype(BF16)
        vb = v_ref[...].astype(BF16)
        cks = (jnp.tile(ck_ref[0], (1, rep)), jnp.tile(ck_ref[1], (1, rep)))

        def block(i, carry, diagonal):
            qs = pl.ds(pl.multiple_of(i * tb, tb), tb)
            q = q_ref[qs, :] * _SCALE
            do = do_ref[qs, :]
            st = st_ref[i]
            if diagonal:
                key = lax.broadcasted_iota(jnp.int32, (tb, tb), 0)
                qry = lax.broadcasted_iota(jnp.int32, (tb, tb), 1)
                mask = key <= qry
            dk, dv, cs = carry
            new_cs, dqs = [], []
            for hh in range(2):
                sel = lo if hh == 0 else jnp.logical_not(lo)
                qm = jnp.where(sel, q, 0.0).astype(BF16)
                dom = jnp.where(sel, do, 0.0).astype(BF16)
                sc = _dot_nt(kb, qm) - cks[hh]
                if diagonal:
                    sc = jnp.where(mask, sc, _NEG)
                mj = _row(st, _ST_MJ + 2 * j + hh)
                w = jnp.exp(mj - _row(st, _ST_LSE + hh))
                ph = jnp.exp(sc - mj).astype(BF16).astype(F32) * w
                ds = ph * (_dot_nt(vb, dom) - _row(st, _ST_DELTA + hh))
                dsb = ds.astype(BF16)
                dv = dv + _dot(ph.astype(BF16), dom)
                dk = dk + _dot(dsb, qm)
                new_cs.append(cs[hh] + jnp.sum(ds, axis=1, keepdims=True))
                dqs.append(_dot(kt, dsb))
            dq_ref[i] += jnp.where(lo_r, dqs[0], dqs[1]) * _SCALE
            return (dk, dv, tuple(new_cs))

        zcol = jnp.zeros((tb, 1), F32)
        init = (jnp.zeros((tb, LANES), F32), jnp.zeros((tb, LANES), F32), (zcol, zcol))
        carry = block(j, init, True)
        dk, dv, cs = lax.fori_loop(j + 1, nq, lambda i, c: block(i, c, False), carry)
        dk_ref[...] = dk.astype(BF16)
        dv_ref[...] = dv.astype(BF16)
        p2 = 2 * pl.program_id(1)
        cs_ref[...] = jnp.where(lane == p2, cs[0], jnp.where(lane == p2 + 1, cs[1], 0.0))

    full = lambda c0: pl.BlockSpec((None, s, LANES), lambda bi, p, j: (bi, 0, c0 + p))
    kspec = lambda c0: pl.BlockSpec((None, tb, LANES), lambda bi, p, j: (bi, j, c0 + p))
    ko = pl.BlockSpec((None, tb, LANES), lambda bi, p, j: (bi, j, p))
    sall = pl.BlockSpec((None, None, nq, _SR, tb), lambda bi, p, j: (bi, p, 0, 0, 0))
    dqspec = pl.BlockSpec((None, None, nq, LANES, tb), lambda bi, p, j: (bi, p, 0, 0, 0))
    return pl.pallas_call(
        body, name=name, grid=(b, N_HEADS // 2, nq),
        in_specs=[full(q0), full(0), sall, kspec(k0), kspec(v0),
                  pl.BlockSpec((None, None, 2, tb, LANES), lambda bi, p, j: (bi, p, 0, j, 0))],
        out_specs=[dqspec, ko, ko, pl.BlockSpec((None, None, tb, LANES), lambda bi, p, j: (bi, p, j, 0))],
        out_shape=[jax.ShapeDtypeStruct((b, N_HEADS // 2, nq, LANES, tb), F32),
                   jax.ShapeDtypeStruct((b, s, D_MODEL), BF16), jax.ShapeDtypeStruct((b, s, D_MODEL), BF16),
                   jax.ShapeDtypeStruct((b, N_HEADS // 2, s, LANES), F32)],
        compiler_params=_cp(("parallel", "parallel", "arbitrary")),
    )(proj3, do3, stats, proj3, proj3, ckrep)


def _rope(x, cos, sin_signed):
    w = x.shape[1]
    lane = lax.broadcasted_iota(jnp.int32, x.shape, 1)
    first = (lane % HEAD_DIM) < (HEAD_DIM // 2)
    rot = jnp.where(first, pltpu.roll(x, w - HEAD_DIM // 2, 1), pltpu.roll(x, HEAD_DIM // 2, 1))
    return x * cos + rot * sin_signed


_QB = 4
_QROWS = _QB * CHUNK


def _swa_keys(g, kc_ref, kp_ref, vc_ref, vp_ref, cq_ref, sq_ref, cp_ref, sp_ref):
    def both_halves(x):
        lane = lax.broadcasted_iota(jnp.int32, x.shape, 1)
        keep = (lane // HEAD_DIM) == (g % 2)
        return jnp.where(keep, x, pltpu.roll(x, HEAD_DIM, 1))

    cq, sq, cpv, spv = cq_ref[...], sq_ref[...], cp_ref[...], sp_ref[...]
    kc = _rope(both_halves(kc_ref[...]), cq, sq).astype(BF16)
    kp = _rope(both_halves(kp_ref[...]), cpv, spv).astype(BF16)
    return cq, sq, cpv, spv, kc, kp, both_halves(vc_ref[...]).astype(BF16), both_halves(vp_ref[...]).astype(BF16)


def _swa_stack(pairs, lo):
    return jnp.concatenate([jnp.where(lo, pairs[0], 0.0), jnp.where(lo, 0.0, pairs[0]),
                            jnp.where(lo, pairs[1], 0.0), jnp.where(lo, 0.0, pairs[1])], axis=0).astype(BF16)


def _swa_mask4(prev_valid):
    r = lax.broadcasted_iota(jnp.int32, (4 * CHUNK, 2 * CHUNK), 0) & (CHUNK - 1)
    c = lax.broadcasted_iota(jnp.int32, (4 * CHUNK, 2 * CHUNK), 1)
    own = jnp.logical_and(c >= CHUNK, c - CHUNK <= r)
    before = jnp.logical_and(c < CHUNK, c > r)
    if prev_valid is True:
        return jnp.logical_or(own, before)
    return jnp.logical_or(own, jnp.logical_and(before, prev_valid))


def _swa_sink4(skv):
    return jnp.concatenate([jnp.broadcast_to(_col(skv, j), (CHUNK, 1)) for j in range(4)], axis=0)


def _swa_specs(order):
    def spec(shape, fn):
        return pl.BlockSpec(shape, lambda *ids: fn(*order(*ids)))

    q0 = _PAD_COLS["b_q"][0] // 256
    z0 = _PAD_COLS["b_z"][0] // 256
    k0 = _PAD_COLS["b_k"][0] // LANES
    v0 = _PAD_COLS["b_v"][0] // LANES
    prev = lambda i: jnp.maximum(_QB * i - 1, 0)
    return dict(
        kc=spec((None, _QROWS, LANES), lambda bi, g, i: (bi, i, k0 + g // 2)),
        kp=spec((None, CHUNK, LANES), lambda bi, g, i: (bi, prev(i), k0 + g // 2)),
        vc=spec((None, _QROWS, LANES), lambda bi, g, i: (bi, i, v0 + g // 2)),
        vp=spec((None, CHUNK, LANES), lambda bi, g, i: (bi, prev(i), v0 + g // 2)),
        q=spec((None, _QROWS, 256), lambda bi, g, i: (bi, i, q0 + g)),
        z=spec((None, _QROWS, 256), lambda bi, g, i: (bi, i, z0 + g)),
        blk=spec((None, _QROWS, 256), lambda bi, g, i: (bi, i, g)),
        kcur=spec((None, _QROWS, LANES), lambda bi, g, i: (bi, i, g)),
        kstep=spec((None, CHUNK, LANES), lambda bi, g, i: (bi, i, g)),
        tcur=spec((_QROWS, LANES), lambda bi, g, i: (i, 0)),
        tprev=spec((CHUNK, LANES), lambda bi, g, i: (prev(i), 0)),
        sk=spec((None, 1, LANES), lambda bi, g, i: (g, 0, 0)))


def _swa_fwd(proj3, cos, sin, sinks, *, name):
    b, s, _ = proj3.shape

    def body(q_ref, z_ref, kc_ref, kp_ref, vc_ref, vp_ref, cq_ref, sq_ref, cp_ref, sp_ref, sk_ref,
             y_ref, o_ref, lse_ref):
        i = pl.program_id(2)
        cq_all, sq_all, _, _, kc_all, kp0, vc_all, vp0 = _swa_keys(
            pl.program_id(1), kc_ref, kp_ref, vc_ref, vp_ref, cq_ref, sq_ref, cp_ref, sp_ref)
        lo = lax.broadcasted_iota(jnp.int32, (CHUNK, LANES), 1) < HEAD_DIM
        sink4 = _swa_sink4(sk_ref[...])
        for u in range(_QB):
            rs = slice(CHUNK * u, CHUNK * (u + 1))
            ps = slice(CHUNK * (u - 1), CHUNK * u)
            cq, sq = cq_all[rs], sq_all[rs]
            kp, vp = (kp0, vp0) if u == 0 else (kc_all[ps], vc_all[ps])
            kk = jnp.concatenate([kp, kc_all[rs]], axis=0)
            vv = jnp.concatenate([vp, vc_all[rs]], axis=0)
            q4 = _swa_stack([_rope(q_ref[rs, LANES * pp:LANES * (pp + 1)], cq, sq) * _SCALE for pp in range(2)], lo)
            sc = jnp.where(_swa_mask4(True if u > 0 else i > 0), _dot_nt(q4, kk), _NEG)
            m = jnp.maximum(jnp.max(sc, axis=1, keepdims=True), sink4)
            pr = jnp.exp(sc - m)
            l = jnp.sum(pr, axis=1, keepdims=True) + jnp.exp(sink4 - m)
            o4 = _dot(pr.astype(BF16), vv) / l
            lse4 = m + jnp.log(l)
            for pp in range(2):
                ls = slice(LANES * pp, LANES * (pp + 1))
                h0 = slice(2 * CHUNK * pp, 2 * CHUNK * pp + CHUNK)
                h1 = slice(2 * CHUNK * pp + CHUNK, 2 * CHUNK * (pp + 1))
                o = jnp.where(lo, o4[h0], o4[h1])
                z = z_ref[rs, ls]
                o_ref[rs, ls] = o
                lse_ref[rs, ls] = jnp.where(lo, lse4[h0], lse4[h1])
                y_ref[rs, ls] = (o * (z * _sigmoid(z))).astype(BF16)

    sp = _swa_specs(lambda bi, g, i: (bi, g, i))
    return pl.pallas_call(
        body, name=name, grid=(b, N_GROUPS, s // _QROWS),
        in_specs=[sp["q"], sp["z"], sp["kc"], sp["kp"], sp["vc"], sp["vp"],
                  sp["tcur"], sp["tcur"], sp["tprev"], sp["tprev"], sp["sk"]],
        out_specs=[sp["blk"], sp["blk"], sp["blk"]],
        out_shape=[jax.ShapeDtypeStruct((b, s, D_MODEL), BF16)] + [jax.ShapeDtypeStruct((b, s, D_MODEL), F32)] * 2,
        compiler_params=_cp(("parallel", "parallel", "parallel")),
    )(proj3, proj3, proj3, proj3, proj3, proj3, cos, sin, cos, sin, sinks)


def _swa_bwd(proj3, cos, sin, sinks, o3, lse3, dy3, *, name):
    b, s, _ = proj3.shape

    def body(q_ref, z_ref, kc_ref, kp_ref, vc_ref, vp_ref, cq_ref, sq_ref, cp_ref, sp_ref, sk_ref,
             o_ref, lse_ref, dy_ref, dq_ref, dz_ref, dkc_ref, dkp_ref, dvc_ref, dvp_ref, dsk_ref):
        i = pl.program_id(2)
        first = jnp.logical_and(pl.program_id(1) == 0, i == 0)

        @pl.when(first)
        def _():
            dsk_ref[...] = jnp.zeros_like(dsk_ref)

        cq_all, sq_all, cpv, spv, kc_all, kp0, vc_all, vp0 = _swa_keys(
            pl.program_id(0), kc_ref, kp_ref, vc_ref, vp_ref, cq_ref, sq_ref, cp_ref, sp_ref)
        lo = lax.broadcasted_iota(jnp.int32, (CHUNK, LANES), 1) < HEAD_DIM
        lane1 = lax.broadcasted_iota(jnp.int32, (1, LANES), 1)
        sink4 = _swa_sink4(sk_ref[...])
        zero = jnp.zeros((CHUNK, LANES), F32)
        dks = [zero] * (_QB + 1)
        dvs = [zero] * (_QB + 1)
        dsk_row = jnp.zeros((1, LANES), F32)
        for u in range(_QB):
            rs = slice(CHUNK * u, CHUNK * (u + 1))
            ps = slice(CHUNK * (u - 1), CHUNK * u)
            cq, sq = cq_all[rs], sq_all[rs]
            kp, vp = (kp0, vp0) if u == 0 else (kc_all[ps], vc_all[ps])
            kk = jnp.concatenate([kp, kc_all[rs]], axis=0)
            vv = jnp.concatenate([vp, vc_all[rs]], axis=0)
            q4 = _swa_stack([_rope(q_ref[rs, LANES * pp:LANES * (pp + 1)], cq, sq) * _SCALE for pp in range(2)], lo)
            dos, lses = [], []
            for pp in range(2):
                ls = slice(LANES * pp, LANES * (pp + 1))
                z = z_ref[rs, ls]
                sz = _sigmoid(z)
                dy = dy_ref[rs, ls]
                dos.append(dy * (z * sz))
                dz_ref[rs, ls] = (dy * o_ref[rs, ls] * (sz * (1.0 + z * (1.0 - sz)))).astype(BF16)
                lse = lse_ref[rs, ls]
                lses += [_col(lse, 0), _col(lse, HEAD_DIM)]
            do4 = _swa_stack(dos, lo)
            lse4 = jnp.concatenate(lses, axis=0)
            pr = jnp.exp(jnp.where(_swa_mask4(True if u > 0 else i > 0), _dot_nt(q4, kk), _NEG) - lse4)
            dp = _dot_nt(do4, vv)
            dl = jnp.sum(pr * dp, axis=1, keepdims=True)
            ds = (pr * (dp - dl)).astype(BF16)
            dsink = -jnp.exp(sink4 - lse4) * dl
            for j in range(4):
                dsk_row = dsk_row + jnp.where(
                    lane1 == j, jnp.sum(dsink[CHUNK * j:CHUNK * (j + 1)], axis=0, keepdims=True), 0.0)
            dq4 = _dot(ds, kk)
            dkk = _dot_tn(ds, q4)
            dvv = _dot_tn(pr.astype(BF16), do4)
            dks[u], dks[u + 1] = dks[u] + dkk[:CHUNK], dks[u + 1] + dkk[CHUNK:]
            dvs[u], dvs[u + 1] = dvs[u] + dvv[:CHUNK], dvs[u + 1] + dvv[CHUNK:]
            for pp in range(2):
                h0 = slice(2 * CHUNK * pp, 2 * CHUNK * pp + CHUNK)
                h1 = slice(2 * CHUNK * pp + CHUNK, 2 * CHUNK * (pp + 1))
                dq_ref[rs, LANES * pp:LANES * (pp + 1)] = _rope(
                    jnp.where(lo, dq4[h0], dq4[h1]) * _SCALE, cq, -sq).astype(BF16)
        fold = lambda v: v + pltpu.roll(v, HEAD_DIM, 1)
        dkp_ref[...] = fold(_rope(dks[0], cpv, -spv))
        dvp_ref[...] = fold(dvs[0])
        for u in range(_QB):
            rs = slice(CHUNK * u, CHUNK * (u + 1))
            dkc_ref[rs, :] = fold(_rope(dks[u + 1], cq_all[rs], -sq_all[rs]))
            dvc_ref[rs, :] = fold(dvs[u + 1])
        dsk_ref[...] += dsk_row

    sp = _swa_specs(lambda g, bi, i: (bi, g, i))
    kv_shape = jax.ShapeDtypeStruct((b, s, 512), F32)
    kvp_shape = jax.ShapeDtypeStruct((b, s // _QB, 512), F32)
    return pl.pallas_call(
        body, name=name, grid=(N_GROUPS, b, s // _QROWS),
        in_specs=[sp["q"], sp["z"], sp["kc"], sp["kp"], sp["vc"], sp["vp"],
                  sp["tcur"], sp["tcur"], sp["tprev"], sp["tprev"], sp["sk"], sp["blk"], sp["blk"], sp["blk"]],
        out_specs=[sp["blk"], sp["blk"], sp["kcur"], sp["kstep"], sp["kcur"], sp["kstep"], sp["sk"]],
        out_shape=[jax.ShapeDtypeStruct((b, s, D_MODEL), BF16), jax.ShapeDtypeStruct((b, s, D_MODEL), BF16),
                   kv_shape, kvp_shape, kv_shape, kvp_shape, jax.ShapeDtypeStruct((N_GROUPS, 1, LANES), F32)],
        compiler_params=_cp(("arbitrary", "arbitrary", "arbitrary")),
    )(proj3, proj3, proj3, proj3, proj3, proj3, cos, sin, cos, sin, sinks, o3, lse3, dy3)


def _merge_fwd(proj, br, gb, *, name, tm=256):
    t = proj.shape[0]
    g0 = _PAD_COLS["gates"][0] // D_MODEL

    def body(g_ref, a_ref, b_ref, c_ref, gb_ref, o_ref):
        acc = None
        for i, r in enumerate((a_ref, b_ref, c_ref)):
            gate = _sigmoid(g_ref[:, D_MODEL * i:D_MODEL * (i + 1)] + gb_ref[i:i + 1, :])
            term = gate * r[...]
            acc = term if acc is None else acc + term
        o_ref[...] = acc.astype(BF16)

    row = pl.BlockSpec((tm, D_MODEL), lambda i: (i, 0))
    return pl.pallas_call(
        body, name=name, grid=(t // tm,),
        in_specs=[pl.BlockSpec((tm, 3 * D_MODEL), lambda i: (i, g0)), row, row, row,
                  pl.BlockSpec((3, D_MODEL), lambda i: (0, 0))],
        out_specs=row, out_shape=jax.ShapeDtypeStruct((t, D_MODEL), BF16),
        compiler_params=_cp(("parallel",)),
    )(proj, br[0], br[1], br[2], gb)


def _merge_bwd(proj, br, gb, dm, *, name, tm=256):
    t = proj.shape[0]
    g0 = _PAD_COLS["gates"][0] // D_MODEL

    def body(g_ref, a_ref, b_ref, c_ref, gb_ref, dm_ref, da_ref, db_ref, dc_ref, dg_ref, dgb_ref):
        @pl.when(pl.program_id(0) == 0)
        def _():
            dgb_ref[...] = jnp.zeros_like(dgb_ref)

        dmv = dm_ref[...]
        for i, (r, dr) in enumerate(((a_ref, da_ref), (b_ref, db_ref), (c_ref, dc_ref))):
            gate = _sigmoid(g_ref[:, D_MODEL * i:D_MODEL * (i + 1)] + gb_ref[i:i + 1, :])
            dr[...] = (dmv * gate).astype(BF16)
            dg = dmv * r[...] * gate * (1.0 - gate)
            dg_ref[:, D_MODEL * i:D_MODEL * (i + 1)] = dg.astype(BF16)
            dgb_ref[i:i + 1, :] += jnp.sum(dg, axis=0, keepdims=True)

    row = pl.BlockSpec((tm, D_MODEL), lambda i: (i, 0))
    rowb = jax.ShapeDtypeStruct((t, D_MODEL), BF16)
    return pl.pallas_call(
        body, name=name, grid=(t // tm,),
        in_specs=[pl.BlockSpec((tm, 3 * D_MODEL), lambda i: (i, g0)), row, row, row,
                  pl.BlockSpec((3, D_MODEL), lambda i: (0, 0)), row],
        out_specs=[row, row, row, pl.BlockSpec((tm, 3 * D_MODEL), lambda i: (i, 0)),
                   pl.BlockSpec((8, D_MODEL), lambda i: (0, 0))],
        out_shape=[rowb, rowb, rowb, jax.ShapeDtypeStruct((t, 3 * D_MODEL), BF16),
                   jax.ShapeDtypeStruct((8, D_MODEL), F32)],
        compiler_params=_cp(("arbitrary",)),
    )(proj, br[0], br[1], br[2], gb, dm)


def _rope_tables(s):
    pos = jnp.arange(s, dtype=F32)
    inv_freq = ROPE_THETA ** (-jnp.arange(0, HEAD_DIM, 2, dtype=F32) / HEAD_DIM)
    ang = pos[:, None] * inv_freq[None, :]
    cos, sin = jnp.cos(ang), jnp.sin(ang)
    return jnp.tile(cos, (1, 4)), jnp.tile(jnp.concatenate([-sin, sin], axis=1), (1, 2))


def _layer_params(wl):
    return dict(
        dtb=_group_lanes(wl["dt_bias"]), alog=_group_lanes(wl["a_log"]), dsk=_group_lanes(wl["d_skip"]),
        nw=wl["ssm_norm_w"].reshape(N_GROUPS, 1, 256), sinks=_group_lanes(wl["sinks"]),
        fb=jnp.pad(wl["f_bias"], (0, LANES - N_HEADS)).reshape(1, LANES))


def _layer_fwd(x, wl, tabs, bsz, li, tb):
    t = x.shape[0]
    s = t // bsz
    cos, sin = tabs
    lp = _layer_params(wl)
    n = lambda k: f"l{li}_{k}"
    h, h_t = _rms_fwd(x, wl["norm_w"], name=n("rms_fwd"))
    proj = _mm(h, wl["w_in"], tm=1024, tn=1536, tk=1024, name=n("mm_proj"))
    proj3 = proj.reshape(bsz, s, N_PAD)
    xact3 = _conv_fwd(proj3, wl["conv_w"], wl["conv_b"], name=n("conv_fwd"))
    ya3, ypre3, hst = _ssd_fwd(proj3, xact3, lp["dtb"], lp["alog"], lp["dsk"], lp["nw"], name=n("ssd_fwd"))
    yb3, ob3, lseb3 = _swa_fwd(proj3, cos, sin, lp["sinks"], name=n("swa_fwd"))
    cum = _fgate_fwd(proj3, lp["fb"], name=n("fgate_fwd"))
    cum_t = _ck_rep(cum)
    yc3, oc3, statc3 = _foxt_fwd(proj3, cum_t, name=n("fox_fwd"), tb=tb)
    ys = [v.reshape(t, D_MODEL) for v in (ya3, yb3, yc3)]
    br = [_mm(ys[i], wl["w_proj"][i], tm=1024, tn=1024, tk=1024, name=n(f"mm_br{i}")) for i in range(3)]
    merged = _merge_fwd(proj, br, wl["gate_bias"], name=n("merge_fwd"))
    x_new = _mm(merged, wl["w_out"], tm=1024, tn=1024, tk=1024, add=x, name=n("mm_out"))
    saved = dict(x=x, h_t=h_t, proj=proj, xact3=xact3, ypre3=ypre3, hst=hst, ob3=ob3, lseb3=lseb3,
                 cum_t=cum_t, oc3=oc3, statc3=statc3, ys=ys, br=br, merged=merged, lp=lp)
    return x_new, saved


def _layer_bwd(dx, wl, sv, tabs, bsz, li, tb):
    t = dx.shape[0]
    s = t // bsz
    cos, sin = tabs
    lp = sv["lp"]
    n = lambda k: f"l{li}_{k}"
    proj = sv["proj"]
    proj3 = proj.reshape(bsz, s, N_PAD)
    g = {}
    dmerged = _mm(dx, wl["w_out"], tb=True, tm=1024, tn=1024, tk=1024, name=n("mm_dmerged"))
    g["w_out"] = _mm(sv["merged"], dx, ta=True, tm=1024, tn=1024, tk=512, name=n("mm_dwout"))
    dbr0, dbr1, dbr2, dgates, dgb = _merge_bwd(proj, sv["br"], wl["gate_bias"], dmerged, name=n("merge_bwd"))
    g["gate_bias"] = dgb[:3]
    dbr = (dbr0, dbr1, dbr2)
    dys = [_mm(dbr[i], wl["w_proj"][i], tb=True, tm=1024, tn=1024, tk=1024, name=n(f"mm_dy{i}"))
           for i in range(3)]
    g["w_proj"] = jnp.stack([_mm(sv["ys"][i], dbr[i], ta=True, tm=1024, tn=1024, tk=512, name=n(f"mm_dwproj{i}"))
                             for i in range(3)])
    dy3 = [v.reshape(bsz, s, D_MODEL) for v in dys]

    (dact, daz, dadt, ddtb, dalog, ddsk, dnw) = _ssd_bwd(
        proj3, sv["xact3"], lp["dtb"], lp["alog"], lp["dsk"], lp["nw"], sv["ypre3"], sv["hst"], dy3[0],
        name=n("ssd_bwd"))
    g["dt_bias"], g["a_log"], g["d_skip"] = _ungroup_lanes(ddtb), _ungroup_lanes(dalog), _ungroup_lanes(ddsk)
    g["ssm_norm_w"] = dnw.reshape(D_MODEL)
    dxbc, dwb = _conv_bwd(proj3, wl["conv_w"], wl["conv_b"], dact, name=n("conv_bwd"))
    g["conv_w"], g["conv_b"] = dwb[:CONV_WIDTH], dwb[CONV_WIDTH]

    dbq, dbz, dkc, dkp, dvc, dvp, dsk = _swa_bwd(proj3, cos, sin, lp["sinks"], sv["ob3"],
                                                 sv["lseb3"], dy3[1], name=n("swa_bwd"))
    g["sinks"] = _ungroup_lanes(dsk)

    def fold(cur, prv):
        p4 = prv.reshape(bsz, s // _QROWS, 1, CHUNK, 512)
        tail = jnp.concatenate([p4[:, 1:], jnp.zeros_like(p4[:, :1])], axis=1)
        shifted = jnp.concatenate([jnp.zeros((bsz, s // _QROWS, _QB - 1, CHUNK, 512), F32), tail], axis=2)
        tot = cur + shifted.reshape(bsz, s, 512)
        return tot.reshape(bsz, s, N_GROUPS, 2, HEAD_DIM)[:, :, :, 0].reshape(bsz, s, 256)

    dbk, dbv = fold(dkc, dkp), fold(dvc, dvp)

    dcz, do3, stats = _foxt_prep(proj3, sv["oc3"], sv["statc3"], dy3[2], name=n("fox_prep"), tb=tb)
    dqt, dck, dcv, csum = _foxt_bwd(proj3, sv["cum_t"], do3, stats, name=n("fox_bwd"), tb=tb)
    dcq = jnp.transpose(dqt, (0, 2, 4, 1, 3)).reshape(bsz, s, D_MODEL)
    dcf, dfb = _fgate_bwd(proj3, lp["fb"], csum, name=n("fgate_bwd"))
    g["f_bias"] = dfb[0, :N_HEADS]

    parts = {"gates": dgates.reshape(bsz, s, 3 * D_MODEL), "xbc": dxbc, "a_z": daz, "b_q": dbq, "b_z": dbz,
             "c_q": dcq, "c_k": dck, "c_v": dcv, "c_z": dcz, "b_k": dbk, "b_v": dbv, "a_dt": dadt, "c_f": dcf}
    dproj = jnp.concatenate([parts[name].astype(BF16) for name, _ in _PAD_ORDER]
                            + [jnp.zeros((bsz, s, N_PAD - N_USED), BF16)], axis=2).reshape(t, N_PAD)
    dh = _mm(dproj, wl["w_in"], tb=True, tm=1024, tn=1024, tk=1536, name=n("mm_dh"))
    g["w_in"] = _unpad_w_in(_mm(sv["h_t"], dproj, tm=1024, tn=768, tk=2048, name=n("mm_dwin")))
    dx_in, dnorm = _rms_bwd(sv["x"], wl["norm_w"], dh, dx, name=n("rms_bwd"))
    g["norm_w"] = dnorm[0]
    return dx_in, g


def _local_step(x, target, wls, final_norm_w, tb=1024):
    bsz, s, d = x.shape
    t = bsz * s
    tabs = _rope_tables(s)
    xc = x.reshape(t, d)
    saved = []
    for li, wl in enumerate(wls):
        xc, sv = _layer_fwd(xc, wl, tabs, bsz, li, tb)
        saved.append(sv)
    loss, dx, dfw = _final_loss(xc, final_norm_w, target.reshape(t, d), name="final_loss")
    grads = [None] * len(wls)
    for li in reversed(range(len(wls))):
        dx, grads[li] = _layer_bwd(dx, wls[li], saved[li], tabs, bsz, li, tb)
    return loss[0, 0], dx.reshape(bsz, s, d), grads, dfw[0]


_HBM = pl.BlockSpec(memory_space=pltpu.HBM)


def _chip_peers(x, y):
    return [(1 - x, y), (x, 1 - y), (1 - x, 1 - y)]


def _gather_weights(arrs, *, name):
    n = len(arrs)

    def body(*refs):
        ins, outs = refs[:n], refs[n:2 * n]
        ici_send, ici_recv, d2d_send, d2d_recv = refs[2 * n:]
        x, y, c = lax.axis_index("x"), lax.axis_index("y"), lax.axis_index("c")
        me = 2 * x + y
        peers = _chip_peers(x, y)
        sib = (x, y, 1 - c)
        sends, fwds = [], []
        for a in range(n):
            for k, (px, py) in enumerate(peers):
                cp = pltpu.make_async_remote_copy(
                    src_ref=ins[a].at[c], dst_ref=outs[a].at[me, c], send_sem=ici_send.at[a, k],
                    recv_sem=ici_recv.at[a, k], device_id=(px, py, c), device_id_type=MESH)
                cp.start()
                sends.append(cp)
        for a in range(n):
            for k, (px, py) in enumerate(peers):
                slot = 2 * px + py
                pltpu.make_async_remote_copy(
                    src_ref=ins[a].at[c], dst_ref=outs[a].at[slot, c], send_sem=ici_send.at[a, k],
                    recv_sem=ici_recv.at[a, k], device_id=(px, py, c), device_id_type=MESH).wait_recv()
                fw = pltpu.make_async_remote_copy(
                    src_ref=outs[a].at[slot, c], dst_ref=outs[a].at[slot, c], send_sem=d2d_send.at[a, k],
                    recv_sem=d2d_recv.at[a, k], device_id=sib, device_id_type=MESH)
                fw.start()
                fwds.append(fw)
        for a in range(n):
            for k, (px, py) in enumerate(peers):
                slot = 2 * px + py
                pltpu.make_async_remote_copy(
                    src_ref=outs[a].at[slot, 1 - c], dst_ref=outs[a].at[slot, 1 - c], send_sem=d2d_send.at[a, k],
                    recv_sem=d2d_recv.at[a, k], device_id=sib, device_id_type=MESH).wait_recv()
        for cp in sends + fwds:
            cp.wait_send()

    out_shape = [jax.ShapeDtypeStruct((N_CHIPS,) + a.shape, a.dtype) for a in arrs]
    return pl.pallas_call(
        body, name=name, out_shape=out_shape, in_specs=[_HBM] * n, out_specs=[_HBM] * n,
        scratch_shapes=[pltpu.SemaphoreType.DMA((n, 3)), pltpu.SemaphoreType.DMA((n, 3)),
                        pltpu.SemaphoreType.DMA((n, 3)), pltpu.SemaphoreType.DMA((n, 3))],
    )(*arrs)


def _pair_exchange(arrs, *, name):
    n = len(arrs)

    def body(*refs):
        ins, outs = refs[:n], refs[n:2 * n]
        send, recv = refs[2 * n:]
        x, y, c = lax.axis_index("x"), lax.axis_index("y"), lax.axis_index("c")
        sib = (x, y, 1 - c)
        cps = []
        for a in range(n):
            for k in range(N_CHIPS):
                cp = pltpu.make_async_remote_copy(
                    src_ref=ins[a].at[k, 1 - c], dst_ref=outs[a].at[k], send_sem=send.at[a, k],
                    recv_sem=recv.at[a, k], device_id=sib, device_id_type=MESH)
                cp.start()
                cps.append(cp)
        for cp in cps:
            cp.wait()

    out_shape = [jax.ShapeDtypeStruct((N_CHIPS,) + a.shape[2:], a.dtype) for a in arrs]
    return pl.pallas_call(
        body, name=name, out_shape=out_shape, in_specs=[_HBM] * n, out_specs=[_HBM] * n,
        scratch_shapes=[pltpu.SemaphoreType.DMA((n, N_CHIPS)), pltpu.SemaphoreType.DMA((n, N_CHIPS))],
    )(*arrs)


def _chip_exchange(arrs, *, name):
    n = len(arrs)

    def body(*refs):
        ins, outs = refs[:n], refs[n:2 * n]
        send, recv = refs[2 * n:]
        x, y, c = lax.axis_index("x"), lax.axis_index("y"), lax.axis_index("c")
        me = 2 * x + y
        peers = _chip_peers(x, y)
        cps = []
        for a in range(n):
            for k, (px, py) in enumerate(peers):
                cp = pltpu.make_async_remote_copy(
                    src_ref=ins[a].at[2 * px + py], dst_ref=outs[a].at[me], send_sem=send.at[a, k],
                    recv_sem=recv.at[a, k], device_id=(px, py, c), device_id_type=MESH)
                cp.start()
                cps.append(cp)
        for a in range(n):
            for k, (px, py) in enumerate(peers):
                pltpu.make_async_remote_copy(
                    src_ref=ins[a].at[2 * px + py], dst_ref=outs[a].at[2 * px + py], send_sem=send.at[a, k],
                    recv_sem=recv.at[a, k], device_id=(px, py, c), device_id_type=MESH).wait_recv()
        for cp in cps:
            cp.wait_send()

    out_shape = [jax.ShapeDtypeStruct(a.shape, a.dtype) for a in arrs]
    return pl.pallas_call(
        body, name=name, out_shape=out_shape, in_specs=[_HBM] * n, out_specs=[_HBM] * n,
        scratch_shapes=[pltpu.SemaphoreType.DMA((n, 3)), pltpu.SemaphoreType.DMA((n, 3))],
    )(*arrs)


def _pair_share(arrs, *, name):
    n = len(arrs)

    def body(*refs):
        ins, outs = refs[:n], refs[n:2 * n]
        send, recv = refs[2 * n:]
        x, y, c = lax.axis_index("x"), lax.axis_index("y"), lax.axis_index("c")
        sib = (x, y, 1 - c)
        cps = []
        for a in range(n):
            cp = pltpu.make_async_remote_copy(
                src_ref=ins[a], dst_ref=outs[a], send_sem=send.at[a], recv_sem=recv.at[a],
                device_id=sib, device_id_type=MESH)
            cp.start()
            cps.append(cp)
        for cp in cps:
            cp.wait()

    out_shape = [jax.ShapeDtypeStruct(a.shape, a.dtype) for a in arrs]
    return pl.pallas_call(
        body, name=name, out_shape=out_shape, in_specs=[_HBM] * n, out_specs=[_HBM] * n,
        scratch_shapes=[pltpu.SemaphoreType.DMA((n,)), pltpu.SemaphoreType.DMA((n,))],
    )(*arrs)


def _allreduce_small(buf, *, name):
    r = buf.shape[0]

    def body(in_ref, out_ref, land, send, recv):
        x, y, c = lax.axis_index("x"), lax.axis_index("y"), lax.axis_index("c")
        me = 4 * x + 2 * y + c
        land[me] = in_ref[...]
        cps = []
        for k in range(1, N_DEV):
            px, py, pc = x ^ ((k >> 2) & 1), y ^ ((k >> 1) & 1), c ^ (k & 1)
            cp = pltpu.make_async_remote_copy(
                src_ref=in_ref, dst_ref=land.at[me], send_sem=send.at[k - 1], recv_sem=recv.at[k - 1],
                device_id=(px, py, pc), device_id_type=MESH)
            cp.start()
            cps.append(cp)
        for k in range(1, N_DEV):
            px, py, pc = x ^ ((k >> 2) & 1), y ^ ((k >> 1) & 1), c ^ (k & 1)
            pltpu.make_async_remote_copy(
                src_ref=in_ref, dst_ref=land.at[4 * px + 2 * py + pc], send_sem=send.at[k - 1],
                recv_sem=recv.at[k - 1], device_id=(px, py, pc), device_id_type=MESH).wait_recv()
        for cp in cps:
            cp.wait_send()
        acc = land[0]
        for k in range(1, N_DEV):
            acc = acc + land[k]
        out_ref[...] = acc

    vm = pl.BlockSpec(memory_space=pltpu.VMEM)
    return pl.pallas_call(
        body, name=name, out_shape=jax.ShapeDtypeStruct((r, LANES), F32), in_specs=[vm], out_specs=vm,
        scratch_shapes=[pltpu.VMEM((N_DEV, r, LANES), F32), pltpu.SemaphoreType.DMA((N_DEV - 1,)),
                        pltpu.SemaphoreType.DMA((N_DEV - 1,))],
    )(buf)


def _row_tile(rows, cols, n_arrays, budget=20 * 1024 * 1024):
    best = 8 if rows % 8 == 0 else rows
    tr = 8
    while tr <= rows:
        if rows % tr == 0 and tr * cols * 4 * n_arrays * 2 <= budget:
            best = tr
        tr *= 2
    return best


def _add_slot_layer(full, other, *, name):
    _, _, r, cdim = full.shape
    tr = _row_tile(r, cdim, 4)

    def body(c_ref, a_ref, b_ref, o_ref, ob_ref):
        sm = a_ref[...] + b_ref[...]
        o_ref[...] = sm
        ob_ref[...] = sm.astype(BF16)

    c = lax.axis_index("c").astype(jnp.int32).reshape(1)
    blk = pl.BlockSpec((None, tr, cdim), lambda k, i, c_ref: (k, i, 0))
    return pl.pallas_call(
        body, name=name,
        grid_spec=pltpu.PrefetchScalarGridSpec(
            num_scalar_prefetch=1, grid=(N_CHIPS, r // tr),
            in_specs=[pl.BlockSpec((None, None, tr, cdim), lambda k, i, c_ref: (k, c_ref[0], i, 0)), blk],
            out_specs=[blk, blk]),
        out_shape=[jax.ShapeDtypeStruct((N_CHIPS, r, cdim), F32), jax.ShapeDtypeStruct((N_CHIPS, r, cdim), BF16)],
        compiler_params=_cp(("parallel", "parallel")),
    )(c, full, other)


def _sum_slots(parts, pair, *, name):
    _, r, cdim = parts.shape
    tr = _row_tile(r, cdim, 5)

    def body(me_ref, p_ref, own_ref, o_ref):
        me = me_ref[0]
        acc = None
        for k in range(N_CHIPS):
            term = jnp.where(me == k, own_ref[...], p_ref[k].astype(F32))
            acc = term if acc is None else acc + term
        o_ref[...] = acc

    me = (2 * lax.axis_index("x") + lax.axis_index("y")).astype(jnp.int32).reshape(1)
    return pl.pallas_call(
        body, name=name,
        grid_spec=pltpu.PrefetchScalarGridSpec(
            num_scalar_prefetch=1, grid=(r // tr,),
            in_specs=[pl.BlockSpec((N_CHIPS, tr, cdim), lambda i, me_ref: (0, i, 0)),
                      pl.BlockSpec((None, tr, cdim), lambda i, me_ref: (me_ref[0], i, 0))],
            out_specs=pl.BlockSpec((tr, cdim), lambda i, me_ref: (i, 0))),
        out_shape=jax.ShapeDtypeStruct((r, cdim), F32),
        compiler_params=_cp(("parallel",)),
    )(me, parts, pair)


def _adamw(w, g, m, v, *, name):
    lead, (r, cdim) = w.shape[:-2], w.shape[-2:]
    nl = len(lead)
    tr = _row_tile(r, cdim, 7)
    c1 = 1.0 - ADAM_B1 ** ADAM_STEP
    c2 = 1.0 - ADAM_B2 ** ADAM_STEP

    def body(w_ref, g_ref, m_ref, v_ref, d_ref, nm_ref, nv_ref):
        gv = g_ref[...]
        mn = ADAM_B1 * m_ref[...] + (1.0 - ADAM_B1) * gv
        vn = ADAM_B2 * v_ref[...] + (1.0 - ADAM_B2) * (gv * gv)
        nm_ref[...] = mn
        nv_ref[...] = vn
        d_ref[...] = -ADAM_LR * ((mn / c1) / (jnp.sqrt(vn / c2) + ADAM_EPS) + ADAM_WD * w_ref[...])

    blk = pl.BlockSpec((None,) * nl + (tr, cdim), lambda *ids: ids[:nl] + (ids[nl], 0))
    sh = jax.ShapeDtypeStruct(w.shape, F32)
    return pl.pallas_call(
        body, name=name, grid=lead + (r // tr,), in_specs=[blk] * 4, out_specs=[blk] * 3, out_shape=[sh] * 3,
        compiler_params=_cp(("parallel",) * (nl + 1)),
    )(w, g, m, v)


_SMALL = ("norm_w", "conv_b", "dt_bias", "a_log", "d_skip", "ssm_norm_w", "sinks", "f_bias", "final_norm_w",
          "conv_w", "gate_bias")


def _pack(vals):
    flat = jnp.concatenate([v.reshape(-1) for v in vals])
    rows = -(-flat.shape[0] // LANES)
    rows = -(-rows // 8) * 8
    return jnp.pad(flat, (0, rows * LANES - flat.shape[0])).reshape(rows, LANES)


def _unpack(buf, shapes):
    flat = buf.reshape(-1)
    out, off = [], 0
    for sh in shapes:
        sz = int(np.prod(sh))
        out.append(flat[off:off + sz].reshape(sh))
        off += sz
    return out


def kernel(x, norm_w, w_in, conv_w, conv_b, dt_bias, a_log, d_skip, ssm_norm_w, sinks, f_bias, gate_bias, w_proj, w_out, final_norm_w, loss_target, m_norm_w, m_w_in, m_conv_w, m_conv_b, m_dt_bias, m_a_log, m_d_skip, m_ssm_norm_w, m_sinks, m_f_bias, m_gate_bias, m_w_proj, m_w_out, m_final_norm_w, v_norm_w, v_w_in, v_conv_w, v_conv_b, v_dt_bias, v_a_log, v_d_skip, v_ssm_norm_w, v_sinks, v_f_bias, v_gate_bias, v_w_proj, v_w_out, v_final_norm_w):
    depth = w_in.shape[0]
    chip = 2 * lax.axis_index("x") + lax.axis_index("y")

    own = [w_in.astype(BF16), w_proj.astype(BF16), w_out.astype(BF16), conv_w, gate_bias]
    gathered = _gather_weights(own, name="gather_weights")

    def whole(a, li, axis):
        return jnp.concatenate([jnp.where(chip == k, own[a][li], gathered[a][k, li]) for k in range(N_CHIPS)],
                               axis=axis)

    wls = []
    for li in range(depth):
        wls.append(dict(
            norm_w=norm_w[li], w_in=_pad_w_in(whole(0, li, 1)),
            conv_w=whole(3, li, 1), conv_b=conv_b[li], dt_bias=dt_bias[li], a_log=a_log[li], d_skip=d_skip[li],
            ssm_norm_w=ssm_norm_w[li], sinks=sinks[li], f_bias=f_bias[li], gate_bias=whole(4, li, 1),
            w_proj=whole(1, li, 1),
            w_out=whole(2, li, 0)))

    loss_part, grad_x, grads, d_final = _local_step(x, loss_target, wls, final_norm_w)
    loss = lax.psum(loss_part, ("x", "y", "c"))

    c_in = w_in.shape[2]
    r_proj = w_proj.shape[2]
    r_out = w_out.shape[1]
    full_in = jnp.stack([jnp.stack([grads[li]["w_in"][:, k * c_in:(k + 1) * c_in] for li in range(depth)])
                         for k in range(N_CHIPS)])
    full_proj = jnp.stack([jnp.stack([grads[li]["w_proj"][:, k * r_proj:(k + 1) * r_proj].reshape(-1, D_MODEL)
                                      for li in range(depth)]) for k in range(N_CHIPS)])
    full_out = jnp.stack([jnp.stack([grads[li]["w_out"][k * r_out:(k + 1) * r_out] for li in range(depth)])
                          for k in range(N_CHIPS)])
    fulls = [full_in, full_proj, full_out]
    others = _pair_exchange(fulls, name="grad_pair_exchange")
    pair = [_add_slot_layer(f, o, name=f"grad_pair_add{i}") for i, (f, o) in enumerate(zip(fulls, others))]
    parts = _chip_exchange([p[1] for p in pair], name="grad_chip_exchange")
    mine = [_sum_slots(p, pr[0], name=f"grad_slot_sum{i}") for i, (p, pr) in enumerate(zip(parts, pair))]
    theirs = _pair_share(mine, name="grad_pair_share")
    core = lax.axis_index("c")
    red_in, red_proj, red_out = [jnp.stack([jnp.where(core == li, m, t) for li in range(depth)])
                                 for m, t in zip(mine, theirs)]
    grad_w_in = red_in
    grad_w_proj = red_proj.reshape(w_proj.shape)
    grad_w_out = red_out

    small_full = {
        "norm_w": jnp.stack([g["norm_w"] for g in grads]), "conv_b": jnp.stack([g["conv_b"] for g in grads]),
        "dt_bias": jnp.stack([g["dt_bias"] for g in grads]), "a_log": jnp.stack([g["a_log"] for g in grads]),
        "d_skip": jnp.stack([g["d_skip"] for g in grads]),
        "ssm_norm_w": jnp.stack([g["ssm_norm_w"] for g in grads]),
        "sinks": jnp.stack([g["sinks"] for g in grads]), "f_bias": jnp.stack([g["f_bias"] for g in grads]),
        "final_norm_w": d_final,
        "conv_w": jnp.stack([g["conv_w"] for g in grads]), "gate_bias": jnp.stack([g["gate_bias"] for g in grads])}
    shapes = [small_full[k].shape for k in _SMALL]
    summed = _unpack(_allreduce_small(_pack([small_full[k] for k in _SMALL]), name="allreduce_small"), shapes)
    gsmall = dict(zip(_SMALL, summed))
    gsmall["conv_w"] = lax.dynamic_slice_in_dim(gsmall["conv_w"], chip * conv_w.shape[2], conv_w.shape[2], axis=2)
    gsmall["gate_bias"] = lax.dynamic_slice_in_dim(gsmall["gate_bias"], chip * gate_bias.shape[2],
                                                   gate_bias.shape[2], axis=2)

    w_small = dict(norm_w=norm_w, conv_b=conv_b, dt_bias=dt_bias, a_log=a_log, d_skip=d_skip,
                   ssm_norm_w=ssm_norm_w, sinks=sinks, f_bias=f_bias, final_norm_w=final_norm_w, conv_w=conv_w,
                   gate_bias=gate_bias)
    m_small = dict(norm_w=m_norm_w, conv_b=m_conv_b, dt_bias=m_dt_bias, a_log=m_a_log, d_skip=m_d_skip,
                   ssm_norm_w=m_ssm_norm_w, sinks=m_sinks, f_bias=m_f_bias, final_norm_w=m_final_norm_w,
                   conv_w=m_conv_w, gate_bias=m_gate_bias)
    v_small = dict(norm_w=v_norm_w, conv_b=v_conv_b, dt_bias=v_dt_bias, a_log=v_a_log, d_skip=v_d_skip,
                   ssm_norm_w=v_ssm_norm_w, sinks=v_sinks, f_bias=v_f_bias, final_norm_w=v_final_norm_w,
                   conv_w=v_conv_w, gate_bias=v_gate_bias)
    sshapes = [w_small[k].shape for k in _SMALL]
    ds, ms, vs = _adamw(_pack([w_small[k] for k in _SMALL]), _pack([gsmall[k] for k in _SMALL]),
                        _pack([m_small[k] for k in _SMALL]), _pack([v_small[k] for k in _SMALL]), name="adamw_small")
    delta = dict(zip(_SMALL, _unpack(ds, sshapes)))
    new_m = dict(zip(_SMALL, _unpack(ms, sshapes)))
    new_v = dict(zip(_SMALL, _unpack(vs, sshapes)))
    grad = dict(gsmall)
    for nm, w, g, m, v in (("w_in", w_in, grad_w_in, m_w_in, v_w_in),
                           ("w_proj", w_proj, grad_w_proj, m_w_proj, v_w_proj),
                           ("w_out", w_out, grad_w_out, m_w_out, v_w_out)):
        grad[nm] = g
        delta[nm], new_m[nm], new_v[nm] = _adamw(w, g, m, v, name=f"adamw_{nm}")

    order = ("norm_w", "w_in", "conv_w", "conv_b", "dt_bias", "a_log", "d_skip", "ssm_norm_w", "sinks", "f_bias",
             "gate_bias", "w_proj", "w_out", "final_norm_w")
    return (loss, grad_x, *[grad[k] for k in order], *[delta[k] for k in order],
            *[new_m[k] for k in order], *[new_v[k] for k in order])
```

```python
import functools
import math

import numpy as np
import jax
import jax.numpy as jnp
from jax import lax
from jax.experimental import pallas as pl
from jax.experimental.pallas import tpu as pltpu

F32 = jnp.float32
BF16 = jnp.bfloat16
HIGHEST = lax.Precision.HIGHEST
MESH = pl.DeviceIdType.MESH

D_MODEL = 1024
HEAD_DIM = 64
N_HEADS = 16
N_GROUPS = 4
SSM_STATE = 128
CHUNK = 128
CONV_WIDTH = 4
CONV_DIM = 2048
ROPE_THETA = 10000.0
NORM_EPS = 1e-6
LANES = 128
N_CHIPS = 4
N_DEV = 8

ADAM_LR = 0.001
ADAM_B1 = 0.9
ADAM_B2 = 0.999
ADAM_EPS = 1e-08
ADAM_WD = 0.01
ADAM_STEP = 10

_REF_COLS = {}
_off = 0
for _n, _s in (("xbc", 2048), ("a_z", 1024), ("a_dt", 16), ("b_q", 1024), ("b_k", 256), ("b_v", 256),
               ("b_z", 1024), ("c_q", 1024), ("c_k", 1024), ("c_v", 1024), ("c_f", 16), ("c_z", 1024),
               ("gates", 3072)):
    _REF_COLS[_n] = (_off, _s)
    _off += _s
N_IN = _off

_PAD_ORDER = (("gates", 3072), ("xbc", 2048), ("a_z", 1024), ("b_q", 1024), ("b_z", 1024), ("c_q", 1024),
              ("c_k", 1024), ("c_v", 1024), ("c_z", 1024), ("b_k", 256), ("b_v", 256), ("a_dt", 512),
              ("c_f", 128))
_PAD_COLS = {}
_off = 0
for _n, _s in _PAD_ORDER:
    _PAD_COLS[_n] = (_off, _s)
    _off += _s
N_USED = _off
N_PAD = 13824


def _cp(sem, vmem_mb=48):
    return pltpu.CompilerParams(dimension_semantics=sem, vmem_limit_bytes=vmem_mb * 1024 * 1024)


def _dot(a, b, dims=((1,), (0,)), precision=None):
    return lax.dot_general(a, b, (dims, ((), ())), preferred_element_type=F32, precision=precision)


def _dot_nt(a, b):
    return _dot(a, b, ((1,), (1,)))


def _dot_tn(a, b):
    return _dot(a, b, ((0,), (0,)))


def _col(v, idx):
    lane = lax.broadcasted_iota(jnp.int32, v.shape, 1)
    return jnp.sum(jnp.where(lane == idx, v, 0.0), axis=1, keepdims=True)


def _row(v, idx):
    row = lax.broadcasted_iota(jnp.int32, v.shape, 0)
    return jnp.sum(jnp.where(row == idx, v, 0.0), axis=0, keepdims=True)


def _iota_col():
    return lax.broadcasted_iota(jnp.int32, (CHUNK, 1), 0)


def _iota_row():
    return lax.broadcasted_iota(jnp.int32, (1, LANES), 1)


def _sigmoid(x):
    return 1.0 / (1.0 + jnp.exp(-x))


def _softplus(x):
    return jnp.maximum(x, 0.0) + jnp.log(1.0 + jnp.exp(-jnp.abs(x)))


def _pad_w_in(w):
    parts = []
    for name, size in _PAD_ORDER:
        s0, sz = _REF_COLS[name]
        seg = w[:, s0:s0 + sz]
        if name == "a_dt":
            seg = jnp.pad(seg.reshape(-1, N_GROUPS, 4), ((0, 0), (0, 0), (0, LANES - 4))).reshape(-1, 512)
        elif name == "c_f":
            seg = jnp.pad(seg, ((0, 0), (0, LANES - 16)))
        parts.append(seg)
    parts.append(jnp.zeros((w.shape[0], N_PAD - N_USED), w.dtype))
    return jnp.concatenate(parts, axis=1)


def _unpad_w_in(wp):
    segs = {}
    for name, _ in _PAD_ORDER:
        p0, psz = _PAD_COLS[name]
        seg = wp[:, p0:p0 + psz]
        if name == "a_dt":
            seg = seg.reshape(-1, N_GROUPS, LANES)[:, :, :4].reshape(-1, 16)
        elif name == "c_f":
            seg = seg[:, :16]
        segs[name] = seg
    order = sorted(_REF_COLS, key=lambda n: _REF_COLS[n][0])
    return jnp.concatenate([segs[n] for n in order], axis=1)


def _group_lanes(v):
    return jnp.pad(v.reshape(N_GROUPS, 1, 4), ((0, 0), (0, 0), (0, LANES - 4)))


def _ungroup_lanes(v):
    return v[:, 0, :4].reshape(16)


def _mm(a, b, *, ta=False, tb=False, tm=512, tn=512, tk=512, out_dtype=F32, add=None, name):
    if ta:
        kdim, m = a.shape
    else:
        m, kdim = a.shape
    if tb:
        n, k2 = b.shape
    else:
        k2, n = b.shape
    assert kdim == k2, (a.shape, b.shape)
    tm, tn, tk = min(tm, m), min(tn, n), min(tk, kdim)
    assert m % tm == 0 and n % tn == 0 and kdim % tk == 0, (m, n, kdim, tm, tn, tk)
    nk = kdim // tk
    a_spec = (pl.BlockSpec((tk, tm), lambda i, j, k: (k, i)) if ta
              else pl.BlockSpec((tm, tk), lambda i, j, k: (i, k)))
    b_spec = (pl.BlockSpec((tn, tk), lambda i, j, k: (j, k)) if tb
              else pl.BlockSpec((tk, tn), lambda i, j, k: (k, j)))
    dims = ((0 if ta else 1,), (1 if tb else 0,))
    has_add = add is not None

    def body(*refs):
        if has_add:
            a_ref, b_ref, add_ref, o_ref, acc_ref = refs
        else:
            a_ref, b_ref, o_ref, acc_ref = refs
        k = pl.program_id(2)
        p = _dot(a_ref[...].astype(BF16), b_ref[...].astype(BF16), dims)

        @pl.when(k == 0)
        def _():
            acc_ref[...] = p

        @pl.when(k > 0)
        def _():
            acc_ref[...] += p

        @pl.when(k == nk - 1)
        def _():
            r = acc_ref[...]
            if has_add:
                r = r + add_ref[...]
            o_ref[...] = r.astype(out_dtype)

    in_specs = [a_spec, b_spec]
    args = [a, b]
    if has_add:
        in_specs.append(pl.BlockSpec((tm, tn), lambda i, j, k: (i, j)))
        args.append(add)
    return pl.pallas_call(
        body, name=name, grid=(m // tm, n // tn, nk),
        in_specs=in_specs, out_specs=pl.BlockSpec((tm, tn), lambda i, j, k: (i, j)),
        out_shape=jax.ShapeDtypeStruct((m, n), out_dtype),
        scratch_shapes=[pltpu.VMEM((tm, tn), F32)],
        compiler_params=_cp(("parallel", "parallel", "arbitrary")),
    )(*args)


def _rms_fwd(x, w, *, name, tm=512):
    t, d = x.shape

    def body(x_ref, w_ref, o_ref, ot_ref):
        xv = x_ref[...]
        r = lax.rsqrt(jnp.mean(xv * xv, axis=1, keepdims=True) + NORM_EPS)
        h = xv * r * w_ref[...]
        o_ref[...] = h.astype(BF16)
        ot_ref[...] = h.T.astype(BF16)

    return pl.pallas_call(
        body, name=name, grid=(t // tm,),
        in_specs=[pl.BlockSpec((tm, d), lambda i: (i, 0)), pl.BlockSpec((1, d), lambda i: (0, 0))],
        out_specs=[pl.BlockSpec((tm, d), lambda i: (i, 0)), pl.BlockSpec((d, tm), lambda i: (0, i))],
        out_shape=[jax.ShapeDtypeStruct((t, d), BF16), jax.ShapeDtypeStruct((d, t), BF16)],
        compiler_params=_cp(("parallel",)),
    )(x, w.reshape(1, d))


def _rms_bwd(x, w, dh, dres, *, name, tm=512):
    t, d = x.shape

    def body(x_ref, w_ref, dh_ref, dres_ref, dx_ref, dw_ref):
        xv = x_ref[...]
        r = lax.rsqrt(jnp.mean(xv * xv, axis=1, keepdims=True) + NORM_EPS)
        xhat = xv * r
        dhv = dh_ref[...]
        dxhat = dhv * w_ref[...]
        dx = r * (dxhat - xhat * jnp.mean(dxhat * xhat, axis=1, keepdims=True))
        dx_ref[...] = dres_ref[...] + dx

        @pl.when(pl.program_id(0) == 0)
        def _():
            dw_ref[...] = jnp.zeros_like(dw_ref)

        dw_ref[...] += jnp.sum(dhv * xhat, axis=0, keepdims=True)

    return pl.pallas_call(
        body, name=name, grid=(t // tm,),
        in_specs=[pl.BlockSpec((tm, d), lambda i: (i, 0)), pl.BlockSpec((1, d), lambda i: (0, 0)),
                  pl.BlockSpec((tm, d), lambda i: (i, 0)), pl.BlockSpec((tm, d), lambda i: (i, 0))],
        out_specs=[pl.BlockSpec((tm, d), lambda i: (i, 0)), pl.BlockSpec((1, d), lambda i: (0, 0))],
        out_shape=[jax.ShapeDtypeStruct((t, d), F32), jax.ShapeDtypeStruct((1, d), F32)],
        compiler_params=_cp(("arbitrary",)),
    )(x, w.reshape(1, d), dh, dres)


def _final_loss(x, w, target, *, name, tm=512):
    t, d = x.shape

    def body(x_ref, w_ref, t_ref, loss_ref, dx_ref, dw_ref):
        xv = x_ref[...]
        wv = w_ref[...]
        r = lax.rsqrt(jnp.mean(xv * xv, axis=1, keepdims=True) + NORM_EPS)
        xhat = xv * r
        err = xhat * wv - t_ref[...]
        dy = err * (1.0 / d)
        dxhat = dy * wv
        dx_ref[...] = r * (dxhat - xhat * jnp.mean(dxhat * xhat, axis=1, keepdims=True))

        @pl.when(pl.program_id(0) == 0)
        def _():
            dw_ref[...] = jnp.zeros_like(dw_ref)
            loss_ref[...] = jnp.zeros_like(loss_ref)

        dw_ref[...] += jnp.sum(dy * xhat, axis=0, keepdims=True)
        part = 0.5 * jnp.sum(jnp.mean(err * err, axis=1, keepdims=True), axis=0, keepdims=True)
        loss_ref[...] += jnp.broadcast_to(part, loss_ref.shape)

    return pl.pallas_call(
        body, name=name, grid=(t // tm,),
        in_specs=[pl.BlockSpec((tm, d), lambda i: (i, 0)), pl.BlockSpec((1, d), lambda i: (0, 0)),
                  pl.BlockSpec((tm, d), lambda i: (i, 0))],
        out_specs=[pl.BlockSpec((8, LANES), lambda i: (0, 0)), pl.BlockSpec((tm, d), lambda i: (i, 0)),
                   pl.BlockSpec((1, d), lambda i: (0, 0))],
        out_shape=[jax.ShapeDtypeStruct((8, LANES), F32), jax.ShapeDtypeStruct((t, d), F32),
                   jax.ShapeDtypeStruct((1, d), F32)],
        compiler_params=_cp(("arbitrary",)),
    )(x, w.reshape(1, d), target)


_CB = 128


def _conv_pre(u, w_ref, b_ref):
    s = u.shape[0]
    row = lax.broadcasted_iota(jnp.int32, u.shape, 0)
    pre = b_ref[...] + w_ref[CONV_WIDTH - 1:CONV_WIDTH, :] * u
    for sh in range(1, CONV_WIDTH):
        shifted = jnp.where(row >= sh, pltpu.roll(u, sh, 0), 0.0)
        pre = pre + w_ref[CONV_WIDTH - 1 - sh:CONV_WIDTH - sh, :] * shifted
    return pre


def _conv_fwd(proj3, cw, cb, *, name):
    b, s, _ = proj3.shape
    c0 = _PAD_COLS["xbc"][0] // _CB

    def body(u_ref, w_ref, b_ref, o_ref):
        pre = _conv_pre(u_ref[...].astype(F32), w_ref, b_ref)
        o_ref[...] = pre * _sigmoid(pre)

    return pl.pallas_call(
        body, name=name, grid=(b, CONV_DIM // _CB),
        in_specs=[pl.BlockSpec((None, s, _CB), lambda i, j: (i, 0, c0 + j)),
                  pl.BlockSpec((CONV_WIDTH, _CB), lambda i, j: (0, j)),
                  pl.BlockSpec((1, _CB), lambda i, j: (0, j))],
        out_specs=pl.BlockSpec((None, s, _CB), lambda i, j: (i, 0, j)),
        out_shape=jax.ShapeDtypeStruct((b, s, CONV_DIM), F32),
        compiler_params=_cp(("parallel", "parallel")),
    )(proj3, cw, cb.reshape(1, CONV_DIM))


def _conv_bwd(proj3, cw, cb, dact, *, name):
    b, s, _ = proj3.shape
    c0 = _PAD_COLS["xbc"][0] // _CB

    def body(u_ref, w_ref, b_ref, da_ref, du_ref, dwb_ref):
        u = u_ref[...].astype(F32)
        pre = _conv_pre(u, w_ref, b_ref)
        sg = _sigmoid(pre)
        dpre = da_ref[...] * (sg * (1.0 + pre * (1.0 - sg)))
        row = lax.broadcasted_iota(jnp.int32, u.shape, 0)
        du = w_ref[CONV_WIDTH - 1:CONV_WIDTH, :] * dpre
        rows = [jnp.sum(dpre * u, axis=0, keepdims=True)]
        for sh in range(1, CONV_WIDTH):
            fwd_shift = jnp.where(row < s - sh, pltpu.roll(dpre, s - sh, 0), 0.0)
            du = du + w_ref[CONV_WIDTH - 1 - sh:CONV_WIDTH - sh, :] * fwd_shift
            ush = jnp.where(row >= sh, pltpu.roll(u, sh, 0), 0.0)
            rows.append(jnp.sum(dpre * ush, axis=0, keepdims=True))
        du_ref[...] = du.astype(BF16)

        @pl.when(pl.program_id(1) == 0)
        def _():
            dwb_ref[...] = jnp.zeros_like(dwb_ref)

        for sh in range(CONV_WIDTH):
            k = CONV_WIDTH - 1 - sh
            dwb_ref[k:k + 1, :] += rows[sh]
        dwb_ref[CONV_WIDTH:CONV_WIDTH + 1, :] += jnp.sum(dpre, axis=0, keepdims=True)

    return pl.pallas_call(
        body, name=name, grid=(CONV_DIM // _CB, b),
        in_specs=[pl.BlockSpec((None, s, _CB), lambda j, i: (i, 0, c0 + j)),
                  pl.BlockSpec((CONV_WIDTH, _CB), lambda j, i: (0, j)),
                  pl.BlockSpec((1, _CB), lambda j, i: (0, j)),
                  pl.BlockSpec((None, s, _CB), lambda j, i: (i, 0, j))],
        out_specs=[pl.BlockSpec((None, s, _CB), lambda j, i: (i, 0, j)),
                   pl.BlockSpec((8, _CB), lambda j, i: (0, j))],
        out_shape=[jax.ShapeDtypeStruct((b, s, CONV_DIM), BF16), jax.ShapeDtypeStruct((8, CONV_DIM), F32)],
        compiler_params=_cp(("parallel", "arbitrary")),
    )(proj3, cw, cb.reshape(1, CONV_DIM), dact)


def _ssd_common(dt_ref, dtb_ref, alog_ref):
    row = lax.broadcasted_iota(jnp.int32, (CHUNK, CHUNK), 0)
    lane = lax.broadcasted_iota(jnp.int32, (CHUNK, CHUNK), 1)
    causal = row >= lane
    tri = causal.astype(F32)
    dtv = _softplus(dt_ref[...] + dtb_ref[...])
    a_row = -jnp.exp(alog_ref[...])
    acum = _dot(tri, dtv * a_row, precision=HIGHEST)
    return row, lane, causal, dtv, a_row, acum, acum.T


def _ssd_pair(pp, x, dtv, acum, acum_t, causal, lane, row):
    lo = lane < HEAD_DIM
    r0, r1 = 2 * pp, 2 * pp + 1
    dtp = jnp.where(lo, _col(dtv, r0), _col(dtv, r1))
    ac0, ac1 = _col(acum, r0), _col(acum, r1)
    ar0, ar1 = _row(acum_t, r0), _row(acum_t, r1)
    d0 = jnp.where(causal, jnp.exp(jnp.where(causal, ac0 - ar0, 0.0)), 0.0)
    d1 = jnp.where(causal, jnp.exp(jnp.where(causal, ac1 - ar1, 0.0)), 0.0)
    al0, al1 = _col(ar0, CHUNK - 1), _col(ar1, CHUNK - 1)
    eac = jnp.where(lo, jnp.exp(ac0), jnp.exp(ac1))
    dsp = jnp.where(lo, jnp.exp(al0 - ac0), jnp.exp(al1 - ac1))
    eal = jnp.where(_iota_col() < HEAD_DIM, jnp.exp(al0), jnp.exp(al1))
    return lo, dtp, x * dtp, d0, d1, al0, al1, eac, dsp, eal


def _ssd_fwd(proj3, gates3, xact3, dtb, alog, dsk, nw, *, name):
    b, s, _ = proj3.shape
    nc = s // CHUNK
    dt0 = 0
    z0 = _PAD_COLS["a_z"][0] // D_MODEL

    def body(xs_ref, bm_ref, cm_ref, dt_ref, z_ref, dtb_ref, alog_ref, dsk_ref, nw_ref,
             ya_ref, ypre_ref, hst_ref, h_scr):
        @pl.when(pl.program_id(1) == 0)
        def _():
            h_scr[...] = jnp.zeros_like(h_scr)

        for g in range(N_GROUPS):
            w256 = pl.ds(256 * g, 256)
            w128 = pl.ds(LANES * g, LANES)
            group(xs_ref.at[:, w256], bm_ref.at[:, w128], cm_ref.at[:, w128], dt_ref.at[:, w128],
                  z_ref.at[:, w256], dtb_ref.at[g], alog_ref.at[g], dsk_ref.at[g], nw_ref.at[g],
                  ya_ref.at[:, w256], ypre_ref.at[:, w256], hst_ref.at[g], h_scr.at[g])

    def group(xs_ref, bm_ref, cm_ref, dt_ref, z_ref, dtb_ref, alog_ref, dsk_ref, nw_ref,
              ya_ref, ypre_ref, hst_ref, h_scr):
        row, lane, causal, dtv, a_row, acum, acum_t = _ssd_common(dt_ref, dtb_ref, alog_ref)
        bb = bm_ref[...].astype(BF16)
        cb = cm_ref[...].astype(BF16)
        cbm = _dot_nt(cb, bb)
        hst_ref[...] = h_scr[...]
        dskv = dsk_ref[...]
        for pp in range(2):
            x = xs_ref[:, LANES * pp:LANES * (pp + 1)]
            lo, dtp, xd, d0, d1, al0, al1, eac, dsp, eal = _ssd_pair(pp, x, dtv, acum, acum_t, causal, lane, row)
            xdb = xd.astype(BF16)
            y = jnp.where(lo, _dot((cbm * d0).astype(BF16), xdb), _dot((cbm * d1).astype(BF16), xdb))
            h = h_scr[pp]
            y = y + eac * _dot_nt(cb, h.astype(BF16))
            h_scr[pp] = h * eal + _dot_tn((xd * dsp).astype(BF16), bb)
            dskp = jnp.where((_iota_row() < HEAD_DIM), _col(dskv, 2 * pp), _col(dskv, 2 * pp + 1))
            ypre_ref[:, LANES * pp:LANES * (pp + 1)] = y + x * dskp
        ypre = ypre_ref[...]
        z = z_ref[...].astype(F32)
        yg = ypre * (z * _sigmoid(z))
        rstd = lax.rsqrt(jnp.sum(yg * yg, axis=1, keepdims=True) * (1.0 / 256.0) + NORM_EPS)
        ya_ref[...] = (yg * rstd * nw_ref[...]).astype(BF16)

    g = N_GROUPS
    par = pl.BlockSpec((g, 1, LANES), lambda i, c: (0, 0, 0))
    wide = pl.BlockSpec((None, CHUNK, D_MODEL), lambda i, c: (i, c, 0))
    return pl.pallas_call(
        body, name=name, grid=(b, nc),
        in_specs=[wide,
                  pl.BlockSpec((None, CHUNK, 512), lambda i, c: (i, c, 2)),
                  pl.BlockSpec((None, CHUNK, 512), lambda i, c: (i, c, 3)),
                  pl.BlockSpec((None, CHUNK, 512), lambda i, c: (i, c, dt0)),
                  pl.BlockSpec((None, CHUNK, D_MODEL), lambda i, c: (i, c, z0)),
                  par, par, par,
                  pl.BlockSpec((g, 1, 256), lambda i, c: (0, 0, 0))],
        out_specs=[wide, wide,
                   pl.BlockSpec((None, None, g, 2, CHUNK, SSM_STATE), lambda i, c: (i, c, 0, 0, 0, 0))],
        out_shape=[jax.ShapeDtypeStruct((b, s, D_MODEL), BF16), jax.ShapeDtypeStruct((b, s, D_MODEL), F32),
                   jax.ShapeDtypeStruct((b, nc, g, 2, CHUNK, SSM_STATE), F32)],
        scratch_shapes=[pltpu.VMEM((g, 2, CHUNK, SSM_STATE), F32)],
        compiler_params=_cp(("parallel", "arbitrary")),
    )(xact3, xact3, xact3, gates3, proj3, dtb, alog, dsk, nw)


def _ssd_bwd(proj3, gates3, xact3, dtb, alog, dsk, nw, ypre3, hst, dya3, *, name):
    b, s, _ = proj3.shape
    nc = s // CHUNK
    dt0 = 0
    z0 = _PAD_COLS["a_z"][0] // D_MODEL

    def body(xs_ref, bm_ref, cm_ref, dt_ref, z_ref, dtb_ref, alog_ref, dsk_ref, nw_ref, ypre_ref, hst_ref,
             dya_ref, dact_ref, dz_ref, ddt_ref, ddtb_ref, dalog_ref, ddsk_ref, dnw_ref, dh_scr):
        first = jnp.logical_and(pl.program_id(0) == 0, pl.program_id(1) == 0)

        @pl.when(first)
        def _():
            ddtb_ref[...] = jnp.zeros_like(ddtb_ref)
            dalog_ref[...] = jnp.zeros_like(dalog_ref)
            ddsk_ref[...] = jnp.zeros_like(ddsk_ref)
            dnw_ref[...] = jnp.zeros_like(dnw_ref)

        @pl.when(pl.program_id(1) == 0)
        def _():
            dh_scr[...] = jnp.zeros_like(dh_scr)

        for g in range(N_GROUPS):
            w256 = pl.ds(256 * g, 256)
            w128 = pl.ds(LANES * g, LANES)
            group(xs_ref.at[:, w256], bm_ref.at[:, w128], cm_ref.at[:, w128], dt_ref.at[:, w128],
                  z_ref.at[:, w256], dtb_ref.at[g], alog_ref.at[g], dsk_ref.at[g], nw_ref.at[g],
                  ypre_ref.at[:, w256], hst_ref.at[g], dya_ref.at[:, w256],
                  dact_ref.at[:, w256], dact_ref.at[:, pl.ds(D_MODEL + LANES * g, LANES)],
                  dact_ref.at[:, pl.ds(D_MODEL + 512 + LANES * g, LANES)], dz_ref.at[:, w256], ddt_ref.at[:, w128],
                  ddtb_ref.at[g], dalog_ref.at[g], ddsk_ref.at[g], dnw_ref.at[g], dh_scr.at[g])

    def group(xs_ref, bm_ref, cm_ref, dt_ref, z_ref, dtb_ref, alog_ref, dsk_ref, nw_ref, ypre_ref, hst_ref,
              dya_ref, dxs_ref, dbm_ref, dcm_ref, dz_ref, ddt_ref, ddtb_ref, dalog_ref, ddsk_ref, dnw_ref,
              dh_scr):
        row, lane, causal, dtv, a_row, acum, acum_t = _ssd_common(dt_ref, dtb_ref, alog_ref)
        lane1 = _iota_row()
        bb = bm_ref[...].astype(BF16)
        cb = cm_ref[...].astype(BF16)
        cbm = _dot_nt(cb, bb)

        z = z_ref[...].astype(F32)
        ypre = ypre_ref[...]
        dya = dya_ref[...]
        sz = _sigmoid(z)
        silu = z * sz
        yg = ypre * silu
        rstd = lax.rsqrt(jnp.sum(yg * yg, axis=1, keepdims=True) * (1.0 / 256.0) + NORM_EPS)
        dnw_ref[...] += jnp.sum(dya * yg * rstd, axis=0, keepdims=True)
        dn = dya * nw_ref[...]
        dyg = rstd * dn - yg * (rstd * rstd * rstd * (1.0 / 256.0)) * jnp.sum(dn * yg, axis=1, keepdims=True)
        dz_ref[...] = (dyg * ypre * (sz * (1.0 + z * (1.0 - sz)))).astype(BF16)
        dy_all = dyg * silu

        dskv = dsk_ref[...]
        da_cols = jnp.zeros((CHUNK, LANES), F32)
        dxt_cols = jnp.zeros((CHUNK, LANES), F32)
        ddsk_row = jnp.zeros((1, LANES), F32)
        dcb = jnp.zeros((CHUNK, CHUNK), F32)
        dc = jnp.zeros((CHUNK, SSM_STATE), F32)
        db = jnp.zeros((CHUNK, SSM_STATE), F32)
        last = _iota_col() == CHUNK - 1
        for pp in range(2):
            r0, r1 = 2 * pp, 2 * pp + 1
            x = xs_ref[:, LANES * pp:LANES * (pp + 1)]
            dy = dy_all[:, LANES * pp:LANES * (pp + 1)]
            lo, dtp, xd, d0, d1, al0, al1, eac, dsp, eal = _ssd_pair(pp, x, dtv, acum, acum_t, causal, lane, row)
            w0, w1 = cbm * d0, cbm * d1
            w0b, w1b = w0.astype(BF16), w1.astype(BF16)
            xdb = xd.astype(BF16)
            dyb = dy.astype(BF16)
            h = hst_ref[pp]
            dhn = dh_scr[pp]
            hb = h.astype(BF16)
            dhb = dhn.astype(BF16)
            g0 = _dot_nt(jnp.where(lo, dy, 0.0).astype(BF16), xdb)
            g1 = _dot_nt(jnp.where(lo, 0.0, dy).astype(BF16), xdb)
            dcb = dcb + g0 * d0 + g1 * d1
            m0, m1 = g0 * w0, g1 * w1
            bdh = _dot_nt(bb, dhb)
            dxd = jnp.where(lo, _dot_tn(w0b, dyb), _dot_tn(w1b, dyb)) + dsp * bdh
            ch = _dot_nt(cb, hb)
            edy = eac * dy
            edyb = edy.astype(BF16)
            xds = xd * dsp
            dc = dc + _dot(edyb, hb)
            db = db + _dot(xds.astype(BF16), dhb)
            dh_scr[pp] = dhn * eal + _dot_tn(edyb, cb)
            t2 = edy * ch
            t3 = xds * bdh
            r4 = jnp.sum(dhn * h, axis=1, keepdims=True)
            s4_0 = jnp.sum(jnp.where(_iota_col() < HEAD_DIM, r4, 0.0), axis=0, keepdims=True)
            s4_1 = jnp.sum(r4, axis=0, keepdims=True) - s4_0
            t2_0 = jnp.sum(jnp.where(lo, t2, 0.0), axis=1, keepdims=True)
            t2_1 = jnp.sum(t2, axis=1, keepdims=True) - t2_0
            t3_0 = jnp.sum(jnp.where(lo, t3, 0.0), axis=1, keepdims=True)
            t3_1 = jnp.sum(t3, axis=1, keepdims=True) - t3_0
            dal0 = jnp.sum(t3_0, axis=0, keepdims=True) + jnp.exp(al0) * s4_0
            dal1 = jnp.sum(t3_1, axis=0, keepdims=True) + jnp.exp(al1) * s4_1
            dac0 = (jnp.sum(m0, axis=1, keepdims=True) - jnp.sum(m0.T, axis=1, keepdims=True)
                    + t2_0 - t3_0 + jnp.where(last, dal0, 0.0))
            dac1 = (jnp.sum(m1, axis=1, keepdims=True) - jnp.sum(m1.T, axis=1, keepdims=True)
                    + t2_1 - t3_1 + jnp.where(last, dal1, 0.0))
            da_cols = da_cols + jnp.where(lane == r0, dac0, 0.0) + jnp.where(lane == r1, dac1, 0.0)
            xx = dxd * x
            x0 = jnp.sum(jnp.where(lo, xx, 0.0), axis=1, keepdims=True)
            x1 = jnp.sum(xx, axis=1, keepdims=True) - x0
            dxt_cols = dxt_cols + jnp.where(lane == r0, x0, 0.0) + jnp.where(lane == r1, x1, 0.0)
            dskp = jnp.where((_iota_row() < HEAD_DIM), _col(dskv, r0), _col(dskv, r1))
            dxs_ref[:, LANES * pp:LANES * (pp + 1)] = dxd * dtp + dy * dskp
            yx = jnp.sum(dy * x, axis=0, keepdims=True)
            k0 = jnp.sum(jnp.where((_iota_row() < HEAD_DIM), yx, 0.0), axis=1, keepdims=True)
            k1 = jnp.sum(yx, axis=1, keepdims=True) - k0
            ddsk_row = ddsk_row + jnp.where(lane1 == r0, k0, 0.0) + jnp.where(lane1 == r1, k1, 0.0)
        dcbb = dcb.astype(BF16)
        dcm_ref[...] = dc + _dot(dcbb, bb)
        dbm_ref[...] = db + _dot_tn(dcbb, cb)
        tri_t = (row <= lane).astype(F32)
        dadt = _dot(tri_t, da_cols, precision=HIGHEST)
        ddtv = dadt * a_row + dxt_cols
        dalog_ref[...] += jnp.sum(dadt * dtv, axis=0, keepdims=True) * a_row
        ddt_raw = ddtv * _sigmoid(dt_ref[...] + dtb_ref[...])
        ddt_ref[...] = ddt_raw.astype(BF16)
        ddtb_ref[...] += jnp.sum(ddt_raw, axis=0, keepdims=True)
        ddsk_ref[...] += ddsk_row

    g = N_GROUPS
    rc = lambda c: nc - 1 - c
    par = pl.BlockSpec((g, 1, LANES), lambda i, c: (0, 0, 0))
    parw = pl.BlockSpec((g, 1, 256), lambda i, c: (0, 0, 0))
    wide = pl.BlockSpec((None, CHUNK, D_MODEL), lambda i, c: (i, rc(c), 0))
    blk512 = lambda col: pl.BlockSpec((None, CHUNK, 512), lambda i, c: (i, rc(c), col))
    return pl.pallas_call(
        body, name=name, grid=(b, nc),
        in_specs=[wide, blk512(2), blk512(3), blk512(dt0),
                  pl.BlockSpec((None, CHUNK, D_MODEL), lambda i, c: (i, rc(c), z0)),
                  par, par, par, parw,
                  wide,
                  pl.BlockSpec((None, None, g, 2, CHUNK, SSM_STATE), lambda i, c: (i, rc(c), 0, 0, 0, 0)),
                  wide],
        out_specs=[pl.BlockSpec((None, CHUNK, CONV_DIM), lambda i, c: (i, rc(c), 0)), wide, blk512(0),
                   par, par, par, parw],
        out_shape=[jax.ShapeDtypeStruct((b, s, CONV_DIM), F32), jax.ShapeDtypeStruct((b, s, D_MODEL), BF16),
                   jax.ShapeDtypeStruct((b, s, 512), BF16),
                   jax.ShapeDtypeStruct((g, 1, LANES), F32), jax.ShapeDtypeStruct((g, 1, LANES), F32),
                   jax.ShapeDtypeStruct((g, 1, LANES), F32), jax.ShapeDtypeStruct((g, 1, 256), F32)],
        scratch_shapes=[pltpu.VMEM((g, 2, CHUNK, SSM_STATE), F32)],
        compiler_params=_cp(("arbitrary", "arbitrary")),
    )(xact3, xact3, xact3, gates3, proj3, dtb, alog, dsk, nw, ypre3, hst, dya3)


def _fgate_fwd(gates3, fb, *, name):
    b, s, _ = gates3.shape
    f0 = _PAD_COLS["a_dt"][1] // LANES

    def body(f_ref, fb_ref, cum_ref, carry):
        @pl.when(pl.program_id(1) == 0)
        def _():
            carry[...] = jnp.zeros_like(carry)

        row = lax.broadcasted_iota(jnp.int32, (CHUNK, CHUNK), 0)
        lane = lax.broadcasted_iota(jnp.int32, (CHUNK, CHUNK), 1)
        tri = (row >= lane).astype(F32)
        lf = -_softplus(-(f_ref[...] + fb_ref[...]))
        cs = _dot(tri, lf, precision=HIGHEST) + carry[0:1, :]
        cum_ref[...] = cs
        carry[0:1, :] = _row(cs, CHUNK - 1)

    return pl.pallas_call(
        body, name=name, grid=(b, s // CHUNK),
        in_specs=[pl.BlockSpec((None, CHUNK, LANES), lambda i, c: (i, c, f0)),
                  pl.BlockSpec((1, LANES), lambda i, c: (0, 0))],
        out_specs=pl.BlockSpec((None, CHUNK, LANES), lambda i, c: (i, c, 0)),
        out_shape=jax.ShapeDtypeStruct((b, s, LANES), F32),
        scratch_shapes=[pltpu.VMEM((8, LANES), F32)],
        compiler_params=_cp(("parallel", "arbitrary")),
    )(gates3, fb)


def _fgate_bwd(gates3, fb, dcum, *, name):
    b, s, _ = gates3.shape
    nc = s // CHUNK
    f0 = _PAD_COLS["a_dt"][1] // LANES
    npair = dcum.shape[1]

    def body(f_ref, fb_ref, dc_ref, df_ref, dfb_ref, carry):
        first = jnp.logical_and(pl.program_id(0) == 0, pl.program_id(1) == 0)

        @pl.when(first)
        def _():
            dfb_ref[...] = jnp.zeros_like(dfb_ref)

        @pl.when(pl.program_id(1) == 0)
        def _():
            carry[...] = jnp.zeros_like(carry)

        row = lax.broadcasted_iota(jnp.int32, (CHUNK, CHUNK), 0)
        lane = lax.broadcasted_iota(jnp.int32, (CHUNK, CHUNK), 1)
        tri_t = (row <= lane).astype(F32)
        dc = -jnp.sum(dc_ref[...], axis=0)
        dlf = _dot(tri_t, dc, precision=HIGHEST) + carry[0:1, :]
        carry[0:1, :] = _row(dlf, 0)
        df = dlf * _sigmoid(-(f_ref[...] + fb_ref[...]))
        df_ref[...] = df.astype(BF16)
        dfb_ref[...] += jnp.sum(df, axis=0, keepdims=True)

    return pl.pallas_call(
        body, name=name, grid=(b, nc),
        in_specs=[pl.BlockSpec((None, CHUNK, LANES), lambda i, c: (i, nc - 1 - c, f0)),
                  pl.BlockSpec((1, LANES), lambda i, c: (0, 0)),
                  pl.BlockSpec((None, npair, CHUNK, LANES), lambda i, c: (i, 0, nc - 1 - c, 0))],
        out_specs=[pl.BlockSpec((None, CHUNK, LANES), lambda i, c: (i, nc - 1 - c, 0)),
                   pl.BlockSpec((1, LANES), lambda i, c: (0, 0))],
        out_shape=[jax.ShapeDtypeStruct((b, s, LANES), BF16), jax.ShapeDtypeStruct((1, LANES), F32)],
        scratch_shapes=[pltpu.VMEM((8, LANES), F32)],
        compiler_params=_cp(("arbitrary", "arbitrary")),
    )(gates3, fb, dcum)


_SCALE = HEAD_DIM ** -0.5
_NEG = -1e30


_ST_LSE, _ST_DELTA, _ST_MJ = 0, 2, 8


_SR = 40


def _ck_rep(cum):
    b, s, _ = cum.shape
    t = jnp.transpose(cum[:, :, :N_HEADS], (0, 2, 1)).reshape(b, N_HEADS // 2, 2, s, 1)
    return jnp.broadcast_to(t, (b, N_HEADS // 2, 2, s, LANES))


def _foxt_fwd(proj3, ckrep, *, name, tb):
    b, s, _ = proj3.shape
    nq = s // tb
    assert _ST_MJ + 2 * nq <= _SR
    q0 = _PAD_COLS["c_q"][0] // LANES
    k0 = _PAD_COLS["c_k"][0] // LANES
    v0 = _PAD_COLS["c_v"][0] // LANES
    z0 = _PAD_COLS["c_z"][0] // LANES
    rep = tb // LANES

    def body(q_ref, k_ref, v_ref, z_ref, ck_ref, y_ref, o_ref, st_ref):
        i = pl.program_id(2)
        lane = lax.broadcasted_iota(jnp.int32, (tb, LANES), 1)
        lo = lane < HEAD_DIM
        lo_r = lax.broadcasted_iota(jnp.int32, (LANES, tb), 0) < HEAD_DIM
        srow = lax.broadcasted_iota(jnp.int32, (_SR, tb), 0)
        q = q_ref[...].astype(F32) * _SCALE
        qms = (jnp.where(lo, q, 0.0).astype(BF16), jnp.where(lo, 0.0, q).astype(BF16))
        ones_at = (HEAD_DIM, 0)

        def block(j, carry, diagonal):
            ks = pl.ds(pl.multiple_of(j * tb, tb), tb)
            kb = k_ref[ks, :].astype(BF16)
            v = v_ref[ks, :].astype(F32)
            vts = (jnp.where(lo, v, jnp.where(lane == ones_at[0], 1.0, 0.0)).T.astype(BF16),
                   jnp.where(lo, jnp.where(lane == ones_at[1], 1.0, 0.0), v).T.astype(BF16))
            if diagonal:
                key = lax.broadcasted_iota(jnp.int32, (tb, tb), 0)
                qry = lax.broadcasted_iota(jnp.int32, (tb, tb), 1)
                mask = key <= qry
            ms, ls, acc, st = carry
            new_m, new_l, pvs, alphas = [], [], [], []
            for hh in range(2):
                sc = _dot_nt(kb, qms[hh]) - jnp.tile(ck_ref[hh, ks, :], (1, rep))
                if diagonal:
                    sc = jnp.where(mask, sc, _NEG)
                m_new = jnp.maximum(ms[hh], jnp.max(sc, axis=0, keepdims=True))
                alpha = jnp.exp(ms[hh] - m_new)
                pv = _dot(vts[hh], jnp.exp(sc - m_new).astype(BF16))
                rs = _row(pv[ones_at[hh]:ones_at[hh] + 8, :], 0)
                new_l.append(alpha * ls[hh] + rs)
                new_m.append(m_new)
                pvs.append(pv)
                alphas.append(alpha)
                st = jnp.where(srow == _ST_MJ + 2 * j + hh, m_new, st)
            acc = jnp.where(lo_r, alphas[0] * acc + pvs[0], alphas[1] * acc + pvs[1])
            return (tuple(new_m), tuple(new_l), acc, st)

        neg = jnp.full((1, tb), _NEG, F32)
        zero = jnp.zeros((1, tb), F32)
        init = ((neg, neg), (zero, zero), jnp.zeros((LANES, tb), F32), jnp.zeros((_SR, tb), F32))
        carry = lax.fori_loop(0, i, lambda j, c: block(j, c, False), init)
        ms, ls, acc, st = block(i, carry, True)
        o = (acc / jnp.where(lo_r, ls[0], ls[1])).T
        o_ref[...] = o
        st = jnp.where(srow == _ST_LSE, ms[0] + jnp.log(ls[0]), st)
        st_ref[...] = jnp.where(srow == _ST_LSE + 1, ms[1] + jnp.log(ls[1]), st)
        z = z_ref[...].astype(F32)
        y_ref[...] = (o * (z * _sigmoid(z))).astype(BF16)

    qspec = lambda c0: pl.BlockSpec((None, tb, LANES), lambda bi, p, i: (bi, i, c0 + p))
    kspec = lambda c0: pl.BlockSpec((None, s, LANES), lambda bi, p, i: (bi, 0, c0 + p))
    ospec = pl.BlockSpec((None, tb, LANES), lambda bi, p, i: (bi, i, p))
    return pl.pallas_call(
        body, name=name, grid=(b, N_HEADS // 2, nq),
        in_specs=[qspec(q0), kspec(k0), kspec(v0), qspec(z0),
                  pl.BlockSpec((None, None, 2, s, LANES), lambda bi, p, i: (bi, p, 0, 0, 0))],
        out_specs=[ospec, ospec, pl.BlockSpec((None, None, None, _SR, tb), lambda bi, p, i: (bi, p, i, 0, 0))],
        out_shape=[jax.ShapeDtypeStruct((b, s, D_MODEL), BF16), jax.ShapeDtypeStruct((b, s, D_MODEL), F32),
                   jax.ShapeDtypeStruct((b, N_HEADS // 2, nq, _SR, tb), F32)],
        compiler_params=_cp(("parallel", "parallel", "arbitrary")),
    )(proj3, proj3, proj3, proj3, ckrep)


def _foxt_prep(proj3, o3, stat, dy3, *, name, tb):
    b, s, _ = proj3.shape
    nq = s // tb
    z0 = _PAD_COLS["c_z"][0] // LANES

    def body(z_ref, o_ref, fst_ref, dy_ref, dz_ref, do_ref, st_ref):
        z = z_ref[...].astype(F32)
        sz = _sigmoid(z)
        dy = dy_ref[...]
        o = o_ref[...]
        do = dy * (z * sz)
        dz_ref[...] = (dy * o * (sz * (1.0 + z * (1.0 - sz)))).astype(BF16)
        do_ref[...] = do
        doo = do.astype(BF16).astype(F32) * o
        r8 = lax.broadcasted_iota(jnp.int32, (8, LANES), 0)
        l8 = lax.broadcasted_iota(jnp.int32, (8, LANES), 1)
        pick = jnp.logical_or(jnp.logical_and(r8 == 0, l8 < HEAD_DIM),
                              jnp.logical_and(r8 == 1, l8 >= HEAD_DIM)).astype(F32)
        d8 = _dot(pick, doo, ((1,), (1,)), precision=HIGHEST)
        srow = lax.broadcasted_iota(jnp.int32, (_SR, tb), 0)
        st = jnp.where(srow == _ST_DELTA, _row(d8, 0), fst_ref[...])
        st_ref[...] = jnp.where(srow == _ST_DELTA + 1, _row(d8, 1), st)

    ospec = pl.BlockSpec((None, tb, LANES), lambda bi, p, i: (bi, i, p))
    sspec = pl.BlockSpec((None, None, None, _SR, tb), lambda bi, p, i: (bi, p, i, 0, 0))
    return pl.pallas_call(
        body, name=name, grid=(b, N_HEADS // 2, nq),
        in_specs=[pl.BlockSpec((None, tb, LANES), lambda bi, p, i: (bi, i, z0 + p)), ospec, sspec, ospec],
        out_specs=[ospec, ospec, sspec],
        out_shape=[jax.ShapeDtypeStruct((b, s, D_MODEL), BF16), jax.ShapeDtypeStruct((b, s, D_MODEL), F32),
                   jax.ShapeDtypeStruct((b, N_HEADS // 2, nq, _SR, tb), F32)],
        compiler_params=_cp(("parallel", "parallel", "parallel")),
    )(proj3, o3, stat, dy3)


def _foxt_bwd(proj3, ckrep, do3, stats, *, name, tb):
    b, s, _ = proj3.shape
    nq = s // tb
    q0 = _PAD_COLS["c_q"][0] // LANES
    k0 = _PAD_COLS["c_k"][0] // LANES
    v0 = _PAD_COLS["c_v"][0] // LANES
    rep = tb // LANES

    def body(q_ref, do_ref, st_ref, k_ref, v_ref, ck_ref, dq_ref, dk_ref, dv_ref, cs_ref):
        j = pl.program_id(2)
        lane = lax.broadcasted_iota(jnp.int32, (tb, LANES), 1)
        lo = lane < HEAD_DIM
        lo_r = lax.broadcasted_iota(jnp.int32, (LANES, tb), 0) < HEAD_DIM

        @pl.when(j == 0)
        def _():
            dq_ref[...] = jnp.zeros_like(dq_ref)

        kf = k_ref[...].astype(F32)
        kb = kf.astype(BF16)
        kt = kf.T.astype(BF16)
        vb = v_ref[...].astype(BF16)
        cks = (jnp.tile(ck_ref[0], (1, rep)), jnp.tile(ck_ref[1], (1, rep)))

        def block(i, carry, diagonal):
            qs = pl.ds(pl.multiple_of(i * tb, tb), tb)
            q = q_ref[qs, :].astype(F32) * _SCALE
            do = do_ref[qs, :]
            st = st_ref[i]
            if diagonal:
                key = lax.broadcasted_iota(jnp.int32, (tb, tb), 0)
                qry = lax.broadcasted_iota(jnp.int32, (tb, tb), 1)
                mask = key <= qry
            dk, dv, cs = carry
            new_cs, dqs = [], []
            for hh in range(2):
                sel = lo if hh == 0 else jnp.logical_not(lo)
                qm = jnp.where(sel, q, 0.0).astype(BF16)
                dom = jnp.where(sel, do, 0.0).astype(BF16)
                sc = _dot_nt(kb, qm) - cks[hh]
                if diagonal:
                    sc = jnp.where(mask, sc, _NEG)
                mj = _row(st, _ST_MJ + 2 * j + hh)
                w = jnp.exp(mj - _row(st, _ST_LSE + hh))
                ph = jnp.exp(sc - mj).astype(BF16).astype(F32) * w
                ds = ph * (_dot_nt(vb, dom) - _row(st, _ST_DELTA + hh))
                dsb = ds.astype(BF16)
                dv = dv + _dot(ph.astype(BF16), dom)
                dk = dk + _dot(dsb, qm)
                new_cs.append(cs[hh] + jnp.sum(ds, axis=1, keepdims=True))
                dqs.append(_dot(kt, dsb))
            dq_ref[i] += jnp.where(lo_r, dqs[0], dqs[1]) * _SCALE
            return (dk, dv, tuple(new_cs))

        zcol = jnp.zeros((tb, 1), F32)
        init = (jnp.zeros((tb, LANES), F32), jnp.zeros((tb, LANES), F32), (zcol, zcol))
        carry = block(j, init, True)
        dk, dv, cs = lax.fori_loop(j + 1, nq, lambda i, c: block(i, c, False), carry)
        dk_ref[...] = dk.astype(BF16)
        dv_ref[...] = dv.astype(BF16)
        p2 = 2 * pl.program_id(1)
        cs_ref[...] = jnp.where(lane == p2, cs[0], jnp.where(lane == p2 + 1, cs[1], 0.0))

    full = lambda c0: pl.BlockSpec((None, s, LANES), lambda bi, p, j: (bi, 0, c0 + p))
    kspec = lambda c0: pl.BlockSpec((None, tb, LANES), lambda bi, p, j: (bi, j, c0 + p))
    ko = pl.BlockSpec((None, tb, LANES), lambda bi, p, j: (bi, j, p))
    sall = pl.BlockSpec((None, None, nq, _SR, tb), lambda bi, p, j: (bi, p, 0, 0, 0))
    dqspec = pl.BlockSpec((None, None, nq, LANES, tb), lambda bi, p, j: (bi, p, 0, 0, 0))
    return pl.pallas_call(
        body, name=name, grid=(b, N_HEADS // 2, nq),
        in_specs=[full(q0), full(0), sall, kspec(k0), kspec(v0),
                  pl.BlockSpec((None, None, 2, tb, LANES), lambda bi, p, j: (bi, p, 0, j, 0))],
        out_specs=[dqspec, ko, ko, pl.BlockSpec((None, None, tb, LANES), lambda bi, p, j: (bi, p, j, 0))],
        out_shape=[jax.ShapeDtypeStruct((b, N_HEADS // 2, nq, LANES, tb), F32),
                   jax.ShapeDtypeStruct((b, s, D_MODEL), BF16), jax.ShapeDtypeStruct((b, s, D_MODEL), BF16),
                   jax.ShapeDtypeStruct((b, N_HEADS // 2, s, LANES), F32)],
        compiler_params=_cp(("parallel", "parallel", "arbitrary")),
    )(proj3, do3, stats, proj3, proj3, ckrep)


def _rope(x, cos, sin_signed):
    w = x.shape[1]
    lane = lax.broadcasted_iota(jnp.int32, x.shape, 1)
    first = (lane % HEAD_DIM) < (HEAD_DIM // 2)
    rot = jnp.where(first, pltpu.roll(x, w - HEAD_DIM // 2, 1), pltpu.roll(x, HEAD_DIM // 2, 1))
    return x * cos + rot * sin_signed


_QB = 4
_QROWS = _QB * CHUNK


def _swa_keys(g, kc_ref, kp_ref, vc_ref, vp_ref, cq_ref, sq_ref, cp_ref, sp_ref):
    def both_halves(x):
        x = x.astype(F32)
        lane = lax.broadcasted_iota(jnp.int32, x.shape, 1)
        keep = (lane // HEAD_DIM) == (g % 2)
        return jnp.where(keep, x, pltpu.roll(x, HEAD_DIM, 1))

    cq, sq, cpv, spv = cq_ref[...], sq_ref[...], cp_ref[...], sp_ref[...]
    kc = _rope(both_halves(kc_ref[...]), cq, sq).astype(BF16)
    kp = _rope(both_halves(kp_ref[...]), cpv, spv).astype(BF16)
    return cq, sq, cpv, spv, kc, kp, both_halves(vc_ref[...]).astype(BF16), both_halves(vp_ref[...]).astype(BF16)


def _swa_stack(pairs, lo):
    return jnp.concatenate([jnp.where(lo, pairs[0], 0.0), jnp.where(lo, 0.0, pairs[0]),
                            jnp.where(lo, pairs[1], 0.0), jnp.where(lo, 0.0, pairs[1])], axis=0).astype(BF16)


def _swa_mask4(prev_valid):
    r = lax.broadcasted_iota(jnp.int32, (4 * CHUNK, 2 * CHUNK), 0) & (CHUNK - 1)
    c = lax.broadcasted_iota(jnp.int32, (4 * CHUNK, 2 * CHUNK), 1)
    own = jnp.logical_and(c >= CHUNK, c - CHUNK <= r)
    before = jnp.logical_and(c < CHUNK, c > r)
    if prev_valid is True:
        return jnp.logical_or(own, before)
    return jnp.logical_or(own, jnp.logical_and(before, prev_valid))


def _swa_sink4(skv):
    return jnp.concatenate([jnp.broadcast_to(_col(skv, j), (CHUNK, 1)) for j in range(4)], axis=0)


def _swa_specs(order):
    def spec(shape, fn):
        return pl.BlockSpec(shape, lambda *ids: fn(*order(*ids)))

    q0 = _PAD_COLS["b_q"][0] // 256
    z0 = _PAD_COLS["b_z"][0] // 256
    k0 = _PAD_COLS["b_k"][0] // LANES
    v0 = _PAD_COLS["b_v"][0] // LANES
    prev = lambda i: jnp.maximum(_QB * i - 1, 0)
    return dict(
        kc=spec((None, _QROWS, LANES), lambda bi, g, i: (bi, i, k0 + g // 2)),
        kp=spec((None, CHUNK, LANES), lambda bi, g, i: (bi, prev(i), k0 + g // 2)),
        vc=spec((None, _QROWS, LANES), lambda bi, g, i: (bi, i, v0 + g // 2)),
        vp=spec((None, CHUNK, LANES), lambda bi, g, i: (bi, prev(i), v0 + g // 2)),
        q=spec((None, _QROWS, 256), lambda bi, g, i: (bi, i, q0 + g)),
        z=spec((None, _QROWS, 256), lambda bi, g, i: (bi, i, z0 + g)),
        blk=spec((None, _QROWS, 256), lambda bi, g, i: (bi, i, g)),
        kcur=spec((None, _QROWS, LANES), lambda bi, g, i: (bi, i, g)),
        kstep=spec((None, CHUNK, LANES), lambda bi, g, i: (bi, i, g)),
        tcur=spec((_QROWS, LANES), lambda bi, g, i: (i, 0)),
        tprev=spec((CHUNK, LANES), lambda bi, g, i: (prev(i), 0)),
        sk=spec((None, 1, LANES), lambda bi, g, i: (g, 0, 0)))


def _swa_fwd(proj3, cos, sin, sinks, *, name):
    b, s, _ = proj3.shape

    def body(q_ref, z_ref, kc_ref, kp_ref, vc_ref, vp_ref, cq_ref, sq_ref, cp_ref, sp_ref, sk_ref,
             y_ref, o_ref, lse_ref):
        i = pl.program_id(2)
        cq_all, sq_all, _, _, kc_all, kp0, vc_all, vp0 = _swa_keys(
            pl.program_id(1), kc_ref, kp_ref, vc_ref, vp_ref, cq_ref, sq_ref, cp_ref, sp_ref)
        lo = lax.broadcasted_iota(jnp.int32, (CHUNK, LANES), 1) < HEAD_DIM
        sink4 = _swa_sink4(sk_ref[...])
        for u in range(_QB):
            rs = slice(CHUNK * u, CHUNK * (u + 1))
            ps = slice(CHUNK * (u - 1), CHUNK * u)
            cq, sq = cq_all[rs], sq_all[rs]
            kp, vp = (kp0, vp0) if u == 0 else (kc_all[ps], vc_all[ps])
            kk = jnp.concatenate([kp, kc_all[rs]], axis=0)
            vv = jnp.concatenate([vp, vc_all[rs]], axis=0)
            q4 = _swa_stack([_rope(q_ref[rs, LANES * pp:LANES * (pp + 1)].astype(F32), cq, sq) * _SCALE
                             for pp in range(2)], lo)
            sc = jnp.where(_swa_mask4(True if u > 0 else i > 0), _dot_nt(q4, kk), _NEG)
            m = jnp.maximum(jnp.max(sc, axis=1, keepdims=True), sink4)
            pr = jnp.exp(sc - m)
            l = jnp.sum(pr, axis=1, keepdims=True) + jnp.exp(sink4 - m)
            o4 = _dot(pr.astype(BF16), vv) / l
            lse4 = m + jnp.log(l)
            for pp in range(2):
                ls = slice(LANES * pp, LANES * (pp + 1))
                h0 = slice(2 * CHUNK * pp, 2 * CHUNK * pp + CHUNK)
                h1 = slice(2 * CHUNK * pp + CHUNK, 2 * CHUNK * (pp + 1))
                o = jnp.where(lo, o4[h0], o4[h1])
                z = z_ref[rs, ls].astype(F32)
                o_ref[rs, ls] = o
                lse_ref[rs, ls] = jnp.where(lo, lse4[h0], lse4[h1])
                y_ref[rs, ls] = (o * (z * _sigmoid(z))).astype(BF16)

    sp = _swa_specs(lambda bi, g, i: (bi, g, i))
    return pl.pallas_call(
        body, name=name, grid=(b, N_GROUPS, s // _QROWS),
        in_specs=[sp["q"], sp["z"], sp["kc"], sp["kp"], sp["vc"], sp["vp"],
                  sp["tcur"], sp["tcur"], sp["tprev"], sp["tprev"], sp["sk"]],
        out_specs=[sp["blk"], sp["blk"], sp["blk"]],
        out_shape=[jax.ShapeDtypeStruct((b, s, D_MODEL), BF16)] + [jax.ShapeDtypeStruct((b, s, D_MODEL), F32)] * 2,
        compiler_params=_cp(("parallel", "parallel", "parallel")),
    )(proj3, proj3, proj3, proj3, proj3, proj3, cos, sin, cos, sin, sinks)


def _swa_bwd(proj3, cos, sin, sinks, o3, lse3, dy3, *, name):
    b, s, _ = proj3.shape

    def body(q_ref, z_ref, kc_ref, kp_ref, vc_ref, vp_ref, cq_ref, sq_ref, cp_ref, sp_ref, sk_ref,
             o_ref, lse_ref, dy_ref, dq_ref, dz_ref, dkc_ref, dkp_ref, dvc_ref, dvp_ref, dsk_ref):
        i = pl.program_id(2)
        first = jnp.logical_and(pl.program_id(1) == 0, i == 0)

        @pl.when(first)
        def _():
            dsk_ref[...] = jnp.zeros_like(dsk_ref)

        cq_all, sq_all, cpv, spv, kc_all, kp0, vc_all, vp0 = _swa_keys(
            pl.program_id(0), kc_ref, kp_ref, vc_ref, vp_ref, cq_ref, sq_ref, cp_ref, sp_ref)
        lo = lax.broadcasted_iota(jnp.int32, (CHUNK, LANES), 1) < HEAD_DIM
        lane1 = lax.broadcasted_iota(jnp.int32, (1, LANES), 1)
        sink4 = _swa_sink4(sk_ref[...])
        zero = jnp.zeros((CHUNK, LANES), F32)
        dks = [zero] * (_QB + 1)
        dvs = [zero] * (_QB + 1)
        dsk_row = jnp.zeros((1, LANES), F32)
        for u in range(_QB):
            rs = slice(CHUNK * u, CHUNK * (u + 1))
            ps = slice(CHUNK * (u - 1), CHUNK * u)
            cq, sq = cq_all[rs], sq_all[rs]
            kp, vp = (kp0, vp0) if u == 0 else (kc_all[ps], vc_all[ps])
            kk = jnp.concatenate([kp, kc_all[rs]], axis=0)
            vv = jnp.concatenate([vp, vc_all[rs]], axis=0)
            q4 = _swa_stack([_rope(q_ref[rs, LANES * pp:LANES * (pp + 1)].astype(F32), cq, sq) * _SCALE
                             for pp in range(2)], lo)
            dos, lses = [], []
            for pp in range(2):
                ls = slice(LANES * pp, LANES * (pp + 1))
                z = z_ref[rs, ls].astype(F32)
                sz = _sigmoid(z)
                dy = dy_ref[rs, ls]
                dos.append(dy * (z * sz))
                dz_ref[rs, ls] = (dy * o_ref[rs, ls] * (sz * (1.0 + z * (1.0 - sz)))).astype(BF16)
                lse = lse_ref[rs, ls]
                lses += [_col(lse, 0), _col(lse, HEAD_DIM)]
            do4 = _swa_stack(dos, lo)
            lse4 = jnp.concatenate(lses, axis=0)
            pr = jnp.exp(jnp.where(_swa_mask4(True if u > 0 else i > 0), _dot_nt(q4, kk), _NEG) - lse4)
            dp = _dot_nt(do4, vv)
            dl = jnp.sum(pr * dp, axis=1, keepdims=True)
            ds = (pr * (dp - dl)).astype(BF16)
            dsink = -jnp.exp(sink4 - lse4) * dl
            for j in range(4):
                dsk_row = dsk_row + jnp.where(
                    lane1 == j, jnp.sum(dsink[CHUNK * j:CHUNK * (j + 1)], axis=0, keepdims=True), 0.0)
            dq4 = _dot(ds, kk)
            dkk = _dot_tn(ds, q4)
            dvv = _dot_tn(pr.astype(BF16), do4)
            dks[u], dks[u + 1] = dks[u] + dkk[:CHUNK], dks[u + 1] + dkk[CHUNK:]
            dvs[u], dvs[u + 1] = dvs[u] + dvv[:CHUNK], dvs[u + 1] + dvv[CHUNK:]
            for pp in range(2):
                h0 = slice(2 * CHUNK * pp, 2 * CHUNK * pp + CHUNK)
                h1 = slice(2 * CHUNK * pp + CHUNK, 2 * CHUNK * (pp + 1))
                dq_ref[rs, LANES * pp:LANES * (pp + 1)] = _rope(
                    jnp.where(lo, dq4[h0], dq4[h1]) * _SCALE, cq, -sq).astype(BF16)
        fold = lambda v: v + pltpu.roll(v, HEAD_DIM, 1)
        dkp_ref[...] = fold(_rope(dks[0], cpv, -spv))
        dvp_ref[...] = fold(dvs[0])
        for u in range(_QB):
            rs = slice(CHUNK * u, CHUNK * (u + 1))
            dkc_ref[rs, :] = fold(_rope(dks[u + 1], cq_all[rs], -sq_all[rs]))
            dvc_ref[rs, :] = fold(dvs[u + 1])
        dsk_ref[...] += dsk_row

    sp = _swa_specs(lambda g, bi, i: (bi, g, i))
    kv_shape = jax.ShapeDtypeStruct((b, s, 512), F32)
    kvp_shape = jax.ShapeDtypeStruct((b, s // _QB, 512), F32)
    return pl.pallas_call(
        body, name=name, grid=(N_GROUPS, b, s // _QROWS),
        in_specs=[sp["q"], sp["z"], sp["kc"], sp["kp"], sp["vc"], sp["vp"],
                  sp["tcur"], sp["tcur"], sp["tprev"], sp["tprev"], sp["sk"], sp["blk"], sp["blk"], sp["blk"]],
        out_specs=[sp["blk"], sp["blk"], sp["kcur"], sp["kstep"], sp["kcur"], sp["kstep"], sp["sk"]],
        out_shape=[jax.ShapeDtypeStruct((b, s, D_MODEL), BF16), jax.ShapeDtypeStruct((b, s, D_MODEL), BF16),
                   kv_shape, kvp_shape, kv_shape, kvp_shape, jax.ShapeDtypeStruct((N_GROUPS, 1, LANES), F32)],
        compiler_params=_cp(("arbitrary", "arbitrary", "arbitrary")),
    )(proj3, proj3, proj3, proj3, proj3, proj3, cos, sin, cos, sin, sinks, o3, lse3, dy3)


def _merge_fwd(proj, br, gb, *, name, tm=256):
    t = proj.shape[0]
    g0 = _PAD_COLS["gates"][0] // D_MODEL

    def body(g_ref, a_ref, b_ref, c_ref, gb_ref, o_ref):
        acc = None
        for i, r in enumerate((a_ref, b_ref, c_ref)):
            gate = _sigmoid(g_ref[:, D_MODEL * i:D_MODEL * (i + 1)].astype(F32) + gb_ref[i:i + 1, :])
            term = gate * r[...]
            acc = term if acc is None else acc + term
        o_ref[...] = acc.astype(BF16)

    row = pl.BlockSpec((tm, D_MODEL), lambda i: (i, 0))
    return pl.pallas_call(
        body, name=name, grid=(t // tm,),
        in_specs=[pl.BlockSpec((tm, 3 * D_MODEL), lambda i: (i, g0)), row, row, row,
                  pl.BlockSpec((3, D_MODEL), lambda i: (0, 0))],
        out_specs=row, out_shape=jax.ShapeDtypeStruct((t, D_MODEL), BF16),
        compiler_params=_cp(("parallel",)),
    )(proj, br[0], br[1], br[2], gb)


def _merge_bwd(proj, br, gb, dm, *, name, tm=256):
    t = proj.shape[0]
    g0 = _PAD_COLS["gates"][0] // D_MODEL

    def body(g_ref, a_ref, b_ref, c_ref, gb_ref, dm_ref, da_ref, db_ref, dc_ref, dg_ref, dgb_ref):
        @pl.when(pl.program_id(0) == 0)
        def _():
            dgb_ref[...] = jnp.zeros_like(dgb_ref)

        dmv = dm_ref[...]
        for i, (r, dr) in enumerate(((a_ref, da_ref), (b_ref, db_ref), (c_ref, dc_ref))):
            gate = _sigmoid(g_ref[:, D_MODEL * i:D_MODEL * (i + 1)].astype(F32) + gb_ref[i:i + 1, :])
            dr[...] = (dmv * gate).astype(BF16)
            dg = dmv * r[...] * gate * (1.0 - gate)
            dg_ref[:, D_MODEL * i:D_MODEL * (i + 1)] = dg.astype(BF16)
            dgb_ref[i:i + 1, :] += jnp.sum(dg, axis=0, keepdims=True)

    row = pl.BlockSpec((tm, D_MODEL), lambda i: (i, 0))
    rowb = jax.ShapeDtypeStruct((t, D_MODEL), BF16)
    return pl.pallas_call(
        body, name=name, grid=(t // tm,),
        in_specs=[pl.BlockSpec((tm, 3 * D_MODEL), lambda i: (i, g0)), row, row, row,
                  pl.BlockSpec((3, D_MODEL), lambda i: (0, 0)), row],
        out_specs=[row, row, row, pl.BlockSpec((tm, 3 * D_MODEL), lambda i: (i, 0)),
                   pl.BlockSpec((8, D_MODEL), lambda i: (0, 0))],
        out_shape=[rowb, rowb, rowb, jax.ShapeDtypeStruct((t, 3 * D_MODEL), BF16),
                   jax.ShapeDtypeStruct((8, D_MODEL), F32)],
        compiler_params=_cp(("arbitrary",)),
    )(proj, br[0], br[1], br[2], gb, dm)


def _rope_tables(s):
    pos = jnp.arange(s, dtype=F32)
    inv_freq = ROPE_THETA ** (-jnp.arange(0, HEAD_DIM, 2, dtype=F32) / HEAD_DIM)
    ang = pos[:, None] * inv_freq[None, :]
    cos, sin = jnp.cos(ang), jnp.sin(ang)
    return jnp.tile(cos, (1, 4)), jnp.tile(jnp.concatenate([-sin, sin], axis=1), (1, 2))


def _layer_params(wl):
    return dict(
        dtb=_group_lanes(wl["dt_bias"]), alog=_group_lanes(wl["a_log"]), dsk=_group_lanes(wl["d_skip"]),
        nw=wl["ssm_norm_w"].reshape(N_GROUPS, 1, 256), sinks=_group_lanes(wl["sinks"]),
        fb=jnp.pad(wl["f_bias"], (0, LANES - N_HEADS)).reshape(1, LANES))


def _layer_fwd(x, wl, tabs, bsz, li, tb):
    t = x.shape[0]
    s = t // bsz
    cos, sin = tabs
    lp = _layer_params(wl)
    n = lambda k: f"l{li}_{k}"
    h, h_t = _rms_fwd(x, wl["norm_w"], name=n("rms_fwd"))
    proj = _mm(h, wl["w_in"], tm=1024, tn=1536, tk=1024, out_dtype=BF16, name=n("mm_proj"))
    proj3 = proj.reshape(bsz, s, N_PAD)
    g0, gw = _PAD_COLS["a_dt"][0], _PAD_COLS["a_dt"][1] + _PAD_COLS["c_f"][1]
    gates3 = _mm(h, wl["w_in"][:, g0:g0 + gw], tm=1024, tn=gw, tk=1024, name=n("mm_gates")).reshape(bsz, s, gw)
    xact3 = _conv_fwd(proj3, wl["conv_w"], wl["conv_b"], name=n("conv_fwd"))
    ya3, ypre3, hst = _ssd_fwd(proj3, gates3, xact3, lp["dtb"], lp["alog"], lp["dsk"], lp["nw"], name=n("ssd_fwd"))
    yb3, ob3, lseb3 = _swa_fwd(proj3, cos, sin, lp["sinks"], name=n("swa_fwd"))
    cum = _fgate_fwd(gates3, lp["fb"], name=n("fgate_fwd"))
    cum_t = _ck_rep(cum)
    yc3, oc3, statc3 = _foxt_fwd(proj3, cum_t, name=n("fox_fwd"), tb=tb)
    ys = [v.reshape(t, D_MODEL) for v in (ya3, yb3, yc3)]
    br = [_mm(ys[i], wl["w_proj"][i], tm=1024, tn=1024, tk=1024, name=n(f"mm_br{i}")) for i in range(3)]
    merged = _merge_fwd(proj, br, wl["gate_bias"], name=n("merge_fwd"))
    x_new = _mm(merged, wl["w_out"], tm=1024, tn=1024, tk=1024, add=x, name=n("mm_out"))
    saved = dict(x=x, h_t=h_t, proj=proj, gates3=gates3, xact3=xact3, ypre3=ypre3, hst=hst, ob3=ob3, lseb3=lseb3,
                 cum_t=cum_t, oc3=oc3, statc3=statc3, ys=ys, br=br, merged=merged, lp=lp)
    return x_new, saved


def _layer_bwd(dx, wl, sv, tabs, bsz, li, tb):
    t = dx.shape[0]
    s = t // bsz
    cos, sin = tabs
    lp = sv["lp"]
    n = lambda k: f"l{li}_{k}"
    proj = sv["proj"]
    proj3 = proj.reshape(bsz, s, N_PAD)
    g = {}
    dmerged = _mm(dx, wl["w_out"], tb=True, tm=1024, tn=1024, tk=1024, name=n("mm_dmerged"))
    g["w_out"] = _mm(sv["merged"], dx, ta=True, tm=1024, tn=1024, tk=512, name=n("mm_dwout"))
    dbr0, dbr1, dbr2, dgates, dgb = _merge_bwd(proj, sv["br"], wl["gate_bias"], dmerged, name=n("merge_bwd"))
    g["gate_bias"] = dgb[:3]
    dbr = (dbr0, dbr1, dbr2)
    dys = [_mm(dbr[i], wl["w_proj"][i], tb=True, tm=1024, tn=1024, tk=1024, name=n(f"mm_dy{i}"))
           for i in range(3)]
    g["w_proj"] = jnp.stack([_mm(sv["ys"][i], dbr[i], ta=True, tm=1024, tn=1024, tk=512, name=n(f"mm_dwproj{i}"))
                             for i in range(3)])
    dy3 = [v.reshape(bsz, s, D_MODEL) for v in dys]

    (dact, daz, dadt, ddtb, dalog, ddsk, dnw) = _ssd_bwd(
        proj3, sv["gates3"], sv["xact3"], lp["dtb"], lp["alog"], lp["dsk"], lp["nw"], sv["ypre3"], sv["hst"], dy3[0],
        name=n("ssd_bwd"))
    g["dt_bias"], g["a_log"], g["d_skip"] = _ungroup_lanes(ddtb), _ungroup_lanes(dalog), _ungroup_lanes(ddsk)
    g["ssm_norm_w"] = dnw.reshape(D_MODEL)
    dxbc, dwb = _conv_bwd(proj3, wl["conv_w"], wl["conv_b"], dact, name=n("conv_bwd"))
    g["conv_w"], g["conv_b"] = dwb[:CONV_WIDTH], dwb[CONV_WIDTH]

    dbq, dbz, dkc, dkp, dvc, dvp, dsk = _swa_bwd(proj3, cos, sin, lp["sinks"], sv["ob3"],
                                                 sv["lseb3"], dy3[1], name=n("swa_bwd"))
    g["sinks"] = _ungroup_lanes(dsk)

    def fold(cur, prv):
        p4 = prv.reshape(bsz, s // _QROWS, 1, CHUNK, 512)
        tail = jnp.concatenate([p4[:, 1:], jnp.zeros_like(p4[:, :1])], axis=1)
        shifted = jnp.concatenate([jnp.zeros((bsz, s // _QROWS, _QB - 1, CHUNK, 512), F32), tail], axis=2)
        tot = cur + shifted.reshape(bsz, s, 512)
        return tot.reshape(bsz, s, N_GROUPS, 2, HEAD_DIM)[:, :, :, 0].reshape(bsz, s, 256)

    dbk, dbv = fold(dkc, dkp), fold(dvc, dvp)

    dcz, do3, stats = _foxt_prep(proj3, sv["oc3"], sv["statc3"], dy3[2], name=n("fox_prep"), tb=tb)
    dqt, dck, dcv, csum = _foxt_bwd(proj3, sv["cum_t"], do3, stats, name=n("fox_bwd"), tb=tb)
    dcq = jnp.transpose(dqt, (0, 2, 4, 1, 3)).reshape(bsz, s, D_MODEL)
    dcf, dfb = _fgate_bwd(sv["gates3"], lp["fb"], csum, name=n("fgate_bwd"))
    g["f_bias"] = dfb[0, :N_HEADS]

    parts = {"gates": dgates.reshape(bsz, s, 3 * D_MODEL), "xbc": dxbc, "a_z": daz, "b_q": dbq, "b_z": dbz,
             "c_q": dcq, "c_k": dck, "c_v": dcv, "c_z": dcz, "b_k": dbk, "b_v": dbv, "a_dt": dadt, "c_f": dcf}
    dproj = jnp.concatenate([parts[name].astype(BF16) for name, _ in _PAD_ORDER]
                            + [jnp.zeros((bsz, s, N_PAD - N_USED), BF16)], axis=2).reshape(t, N_PAD)
    dh = _mm(dproj, wl["w_in"], tb=True, tm=1024, tn=1024, tk=1536, name=n("mm_dh"))
    g["w_in"] = _unpad_w_in(_mm(sv["h_t"], dproj, tm=1024, tn=768, tk=2048, name=n("mm_dwin")))
    dx_in, dnorm = _rms_bwd(sv["x"], wl["norm_w"], dh, dx, name=n("rms_bwd"))
    g["norm_w"] = dnorm[0]
    return dx_in, g


def _local_step(x, target, wls, final_norm_w, tb=1024):
    bsz, s, d = x.shape
    t = bsz * s
    tabs = _rope_tables(s)
    xc = x.reshape(t, d)
    saved = []
    for li, wl in enumerate(wls):
        xc, sv = _layer_fwd(xc, wl, tabs, bsz, li, tb)
        saved.append(sv)
    loss, dx, dfw = _final_loss(xc, final_norm_w, target.reshape(t, d), name="final_loss")
    grads = [None] * len(wls)
    for li in reversed(range(len(wls))):
        dx, grads[li] = _layer_bwd(dx, wls[li], saved[li], tabs, bsz, li, tb)
    return loss[0, 0], dx.reshape(bsz, s, d), grads, dfw[0]


_HBM = pl.BlockSpec(memory_space=pltpu.HBM)


def _chip_peers(x, y):
    return [(1 - x, y), (x, 1 - y), (1 - x, 1 - y)]


def _gather_weights(arrs, *, name):
    n = len(arrs)

    def body(*refs):
        ins, outs = refs[:n], refs[n:2 * n]
        ici_send, ici_recv, d2d_send, d2d_recv = refs[2 * n:]
        x, y, c = lax.axis_index("x"), lax.axis_index("y"), lax.axis_index("c")
        me = 2 * x + y
        peers = _chip_peers(x, y)
        sib = (x, y, 1 - c)
        sends, fwds = [], []
        for a in range(n):
            for k, (px, py) in enumerate(peers):
                cp = pltpu.make_async_remote_copy(
                    src_ref=ins[a].at[c], dst_ref=outs[a].at[me, c], send_sem=ici_send.at[a, k],
                    recv_sem=ici_recv.at[a, k], device_id=(px, py, c), device_id_type=MESH)
                cp.start()
                sends.append(cp)
        for a in range(n):
            for k, (px, py) in enumerate(peers):
                slot = 2 * px + py
                pltpu.make_async_remote_copy(
                    src_ref=ins[a].at[c], dst_ref=outs[a].at[slot, c], send_sem=ici_send.at[a, k],
                    recv_sem=ici_recv.at[a, k], device_id=(px, py, c), device_id_type=MESH).wait_recv()
                fw = pltpu.make_async_remote_copy(
                    src_ref=outs[a].at[slot, c], dst_ref=outs[a].at[slot, c], send_sem=d2d_send.at[a, k],
                    recv_sem=d2d_recv.at[a, k], device_id=sib, device_id_type=MESH)
                fw.start()
                fwds.append(fw)
        for a in range(n):
            for k, (px, py) in enumerate(peers):
                slot = 2 * px + py
                pltpu.make_async_remote_copy(
                    src_ref=outs[a].at[slot, 1 - c], dst_ref=outs[a].at[slot, 1 - c], send_sem=d2d_send.at[a, k],
                    recv_sem=d2d_recv.at[a, k], device_id=sib, device_id_type=MESH).wait_recv()
        for cp in sends + fwds:
            cp.wait_send()

    out_shape = [jax.ShapeDtypeStruct((N_CHIPS,) + a.shape, a.dtype) for a in arrs]
    return pl.pallas_call(
        body, name=name, out_shape=out_shape, in_specs=[_HBM] * n, out_specs=[_HBM] * n,
        scratch_shapes=[pltpu.SemaphoreType.DMA((n, 3)), pltpu.SemaphoreType.DMA((n, 3)),
                        pltpu.SemaphoreType.DMA((n, 3)), pltpu.SemaphoreType.DMA((n, 3))],
    )(*arrs)


def _pair_exchange(arrs, *, name):
    n = len(arrs)

    def body(*refs):
        ins, outs = refs[:n], refs[n:2 * n]
        send, recv = refs[2 * n:]
        x, y, c = lax.axis_index("x"), lax.axis_index("y"), lax.axis_index("c")
        sib = (x, y, 1 - c)
        cps = []
        for a in range(n):
            for k in range(N_CHIPS):
                cp = pltpu.make_async_remote_copy(
                    src_ref=ins[a].at[k, 1 - c], dst_ref=outs[a].at[k], send_sem=send.at[a, k],
                    recv_sem=recv.at[a, k], device_id=sib, device_id_type=MESH)
                cp.start()
                cps.append(cp)
        for cp in cps:
            cp.wait()

    out_shape = [jax.ShapeDtypeStruct((N_CHIPS,) + a.shape[2:], a.dtype) for a in arrs]
    return pl.pallas_call(
        body, name=name, out_shape=out_shape, in_specs=[_HBM] * n, out_specs=[_HBM] * n,
        scratch_shapes=[pltpu.SemaphoreType.DMA((n, N_CHIPS)), pltpu.SemaphoreType.DMA((n, N_CHIPS))],
    )(*arrs)


def _chip_exchange(arrs, *, name):
    n = len(arrs)

    def body(*refs):
        ins, outs = refs[:n], refs[n:2 * n]
        send, recv = refs[2 * n:]
        x, y, c = lax.axis_index("x"), lax.axis_index("y"), lax.axis_index("c")
        me = 2 * x + y
        peers = _chip_peers(x, y)
        cps = []
        for a in range(n):
            for k, (px, py) in enumerate(peers):
                cp = pltpu.make_async_remote_copy(
                    src_ref=ins[a].at[2 * px + py], dst_ref=outs[a].at[me], send_sem=send.at[a, k],
                    recv_sem=recv.at[a, k], device_id=(px, py, c), device_id_type=MESH)
                cp.start()
                cps.append(cp)
        for a in range(n):
            for k, (px, py) in enumerate(peers):
                pltpu.make_async_remote_copy(
                    src_ref=ins[a].at[2 * px + py], dst_ref=outs[a].at[2 * px + py], send_sem=send.at[a, k],
                    recv_sem=recv.at[a, k], device_id=(px, py, c), device_id_type=MESH).wait_recv()
        for cp in cps:
            cp.wait_send()

    out_shape = [jax.ShapeDtypeStruct(a.shape, a.dtype) for a in arrs]
    return pl.pallas_call(
        body, name=name, out_shape=out_shape, in_specs=[_HBM] * n, out_specs=[_HBM] * n,
        scratch_shapes=[pltpu.SemaphoreType.DMA((n, 3)), pltpu.SemaphoreType.DMA((n, 3))],
    )(*arrs)


def _pair_share(arrs, *, name):
    n = len(arrs)

    def body(*refs):
        ins, outs = refs[:n], refs[n:2 * n]
        send, recv = refs[2 * n:]
        x, y, c = lax.axis_index("x"), lax.axis_index("y"), lax.axis_index("c")
        sib = (x, y, 1 - c)
        cps = []
        for a in range(n):
            cp = pltpu.make_async_remote_copy(
                src_ref=ins[a], dst_ref=outs[a], send_sem=send.at[a], recv_sem=recv.at[a],
                device_id=sib, device_id_type=MESH)
            cp.start()
            cps.append(cp)
        for cp in cps:
            cp.wait()

    out_shape = [jax.ShapeDtypeStruct(a.shape, a.dtype) for a in arrs]
    return pl.pallas_call(
        body, name=name, out_shape=out_shape, in_specs=[_HBM] * n, out_specs=[_HBM] * n,
        scratch_shapes=[pltpu.SemaphoreType.DMA((n,)), pltpu.SemaphoreType.DMA((n,))],
    )(*arrs)


def _allreduce_small(buf, *, name):
    r = buf.shape[0]

    def body(in_ref, out_ref, land, send, recv):
        x, y, c = lax.axis_index("x"), lax.axis_index("y"), lax.axis_index("c")
        me = 4 * x + 2 * y + c
        land[me] = in_ref[...]
        cps = []
        for k in range(1, N_DEV):
            px, py, pc = x ^ ((k >> 2) & 1), y ^ ((k >> 1) & 1), c ^ (k & 1)
            cp = pltpu.make_async_remote_copy(
                src_ref=in_ref, dst_ref=land.at[me], send_sem=send.at[k - 1], recv_sem=recv.at[k - 1],
                device_id=(px, py, pc), device_id_type=MESH)
            cp.start()
            cps.append(cp)
        for k in range(1, N_DEV):
            px, py, pc = x ^ ((k >> 2) & 1), y ^ ((k >> 1) & 1), c ^ (k & 1)
            pltpu.make_async_remote_copy(
                src_ref=in_ref, dst_ref=land.at[4 * px + 2 * py + pc], send_sem=send.at[k - 1],
                recv_sem=recv.at[k - 1], device_id=(px, py, pc), device_id_type=MESH).wait_recv()
        for cp in cps:
            cp.wait_send()
        acc = land[0]
        for k in range(1, N_DEV):
            acc = acc + land[k]
        out_ref[...] = acc

    vm = pl.BlockSpec(memory_space=pltpu.VMEM)
    return pl.pallas_call(
        body, name=name, out_shape=jax.ShapeDtypeStruct((r, LANES), F32), in_specs=[vm], out_specs=vm,
        scratch_shapes=[pltpu.VMEM((N_DEV, r, LANES), F32), pltpu.SemaphoreType.DMA((N_DEV - 1,)),
                        pltpu.SemaphoreType.DMA((N_DEV - 1,))],
    )(buf)


def _row_tile(rows, cols, n_arrays, budget=20 * 1024 * 1024):
    best = 8 if rows % 8 == 0 else rows
    tr = 8
    while tr <= rows:
        if rows % tr == 0 and tr * cols * 4 * n_arrays * 2 <= budget:
            best = tr
        tr *= 2
    return best


def _add_slot_layer(full, other, *, name):
    _, _, r, cdim = full.shape
    tr = _row_tile(r, cdim, 4)

    def body(c_ref, a_ref, b_ref, o_ref, ob_ref):
        sm = a_ref[...] + b_ref[...]
        o_ref[...] = sm
        ob_ref[...] = sm.astype(BF16)

    c = lax.axis_index("c").astype(jnp.int32).reshape(1)
    blk = pl.BlockSpec((None, tr, cdim), lambda k, i, c_ref: (k, i, 0))
    return pl.pallas_call(
        body, name=name,
        grid_spec=pltpu.PrefetchScalarGridSpec(
            num_scalar_prefetch=1, grid=(N_CHIPS, r // tr),
            in_specs=[pl.BlockSpec((None, None, tr, cdim), lambda k, i, c_ref: (k, c_ref[0], i, 0)), blk],
            out_specs=[blk, blk]),
        out_shape=[jax.ShapeDtypeStruct((N_CHIPS, r, cdim), F32), jax.ShapeDtypeStruct((N_CHIPS, r, cdim), BF16)],
        compiler_params=_cp(("parallel", "parallel")),
    )(c, full, other)


def _sum_slots(parts, pair, *, name):
    _, r, cdim = parts.shape
    tr = _row_tile(r, cdim, 5)

    def body(me_ref, p_ref, own_ref, o_ref):
        me = me_ref[0]
        acc = None
        for k in range(N_CHIPS):
            term = jnp.where(me == k, own_ref[...], p_ref[k].astype(F32))
            acc = term if acc is None else acc + term
        o_ref[...] = acc

    me = (2 * lax.axis_index("x") + lax.axis_index("y")).astype(jnp.int32).reshape(1)
    return pl.pallas_call(
        body, name=name,
        grid_spec=pltpu.PrefetchScalarGridSpec(
            num_scalar_prefetch=1, grid=(r // tr,),
            in_specs=[pl.BlockSpec((N_CHIPS, tr, cdim), lambda i, me_ref: (0, i, 0)),
                      pl.BlockSpec((None, tr, cdim), lambda i, me_ref: (me_ref[0], i, 0))],
            out_specs=pl.BlockSpec((tr, cdim), lambda i, me_ref: (i, 0))),
        out_shape=jax.ShapeDtypeStruct((r, cdim), F32),
        compiler_params=_cp(("parallel",)),
    )(me, parts, pair)


def _adamw(w, g, m, v, *, name):
    lead, (r, cdim) = w.shape[:-2], w.shape[-2:]
    nl = len(lead)
    tr = _row_tile(r, cdim, 7)
    c1 = 1.0 - ADAM_B1 ** ADAM_STEP
    c2 = 1.0 - ADAM_B2 ** ADAM_STEP

    def body(w_ref, g_ref, m_ref, v_ref, d_ref, nm_ref, nv_ref):
        gv = g_ref[...]
        mn = ADAM_B1 * m_ref[...] + (1.0 - ADAM_B1) * gv
        vn = ADAM_B2 * v_ref[...] + (1.0 - ADAM_B2) * (gv * gv)
        nm_ref[...] = mn
        nv_ref[...] = vn
        d_ref[...] = -ADAM_LR * ((mn / c1) / (jnp.sqrt(vn / c2) + ADAM_EPS) + ADAM_WD * w_ref[...])

    blk = pl.BlockSpec((None,) * nl + (tr, cdim), lambda *ids: ids[:nl] + (ids[nl], 0))
    sh = jax.ShapeDtypeStruct(w.shape, F32)
    return pl.pallas_call(
        body, name=name, grid=lead + (r // tr,), in_specs=[blk] * 4, out_specs=[blk] * 3, out_shape=[sh] * 3,
        compiler_params=_cp(("parallel",) * (nl + 1)),
    )(w, g, m, v)


_SMALL = ("norm_w", "conv_b", "dt_bias", "a_log", "d_skip", "ssm_norm_w", "sinks", "f_bias", "final_norm_w",
          "conv_w", "gate_bias")


def _pack(vals):
    flat = jnp.concatenate([v.reshape(-1) for v in vals])
    rows = -(-flat.shape[0] // LANES)
    rows = -(-rows // 8) * 8
    return jnp.pad(flat, (0, rows * LANES - flat.shape[0])).reshape(rows, LANES)


def _unpack(buf, shapes):
    flat = buf.reshape(-1)
    out, off = [], 0
    for sh in shapes:
        sz = int(np.prod(sh))
        out.append(flat[off:off + sz].reshape(sh))
        off += sz
    return out


def kernel(x, norm_w, w_in, conv_w, conv_b, dt_bias, a_log, d_skip, ssm_norm_w, sinks, f_bias, gate_bias, w_proj, w_out, final_norm_w, loss_target, m_norm_w, m_w_in, m_conv_w, m_conv_b, m_dt_bias, m_a_log, m_d_skip, m_ssm_norm_w, m_sinks, m_f_bias, m_gate_bias, m_w_proj, m_w_out, m_final_norm_w, v_norm_w, v_w_in, v_conv_w, v_conv_b, v_dt_bias, v_a_log, v_d_skip, v_ssm_norm_w, v_sinks, v_f_bias, v_gate_bias, v_w_proj, v_w_out, v_final_norm_w):
    depth = w_in.shape[0]
    chip = 2 * lax.axis_index("x") + lax.axis_index("y")

    own = [w_in.astype(BF16), w_proj.astype(BF16), w_out.astype(BF16), conv_w, gate_bias]
    gathered = _gather_weights(own, name="gather_weights")

    def whole(a, li, axis):
        return jnp.concatenate([jnp.where(chip == k, own[a][li], gathered[a][k, li]) for k in range(N_CHIPS)],
                               axis=axis)

    wls = []
    for li in range(depth):
        wls.append(dict(
            norm_w=norm_w[li], w_in=_pad_w_in(whole(0, li, 1)),
            conv_w=whole(3, li, 1), conv_b=conv_b[li], dt_bias=dt_bias[li], a_log=a_log[li], d_skip=d_skip[li],
            ssm_norm_w=ssm_norm_w[li], sinks=sinks[li], f_bias=f_bias[li], gate_bias=whole(4, li, 1),
            w_proj=whole(1, li, 1),
            w_out=whole(2, li, 0)))

    loss_part, grad_x, grads, d_final = _local_step(x, loss_target, wls, final_norm_w)
    loss = lax.psum(loss_part, ("x", "y", "c"))

    c_in = w_in.shape[2]
    r_proj = w_proj.shape[2]
    r_out = w_out.shape[1]
    full_in = jnp.stack([jnp.stack([grads[li]["w_in"][:, k * c_in:(k + 1) * c_in] for li in range(depth)])
                         for k in range(N_CHIPS)])
    full_proj = jnp.stack([jnp.stack([grads[li]["w_proj"][:, k * r_proj:(k + 1) * r_proj].reshape(-1, D_MODEL)
                                      for li in range(depth)]) for k in range(N_CHIPS)])
    full_out = jnp.stack([jnp.stack([grads[li]["w_out"][k * r_out:(k + 1) * r_out] for li in range(depth)])
                          for k in range(N_CHIPS)])
    fulls = [full_in, full_proj, full_out]
    others = _pair_exchange(fulls, name="grad_pair_exchange")
    pair = [_add_slot_layer(f, o, name=f"grad_pair_add{i}") for i, (f, o) in enumerate(zip(fulls, others))]
    parts = _chip_exchange([p[1] for p in pair], name="grad_chip_exchange")
    mine = [_sum_slots(p, pr[0], name=f"grad_slot_sum{i}") for i, (p, pr) in enumerate(zip(parts, pair))]
    theirs = _pair_share(mine, name="grad_pair_share")
    core = lax.axis_index("c")
    red_in, red_proj, red_out = [jnp.stack([jnp.where(core == li, m, t) for li in range(depth)])
                                 for m, t in zip(mine, theirs)]
    grad_w_in = red_in
    grad_w_proj = red_proj.reshape(w_proj.shape)
    grad_w_out = red_out

    small_full = {
        "norm_w": jnp.stack([g["norm_w"] for g in grads]), "conv_b": jnp.stack([g["conv_b"] for g in grads]),
        "dt_bias": jnp.stack([g["dt_bias"] for g in grads]), "a_log": jnp.stack([g["a_log"] for g in grads]),
        "d_skip": jnp.stack([g["d_skip"] for g in grads]),
        "ssm_norm_w": jnp.stack([g["ssm_norm_w"] for g in grads]),
        "sinks": jnp.stack([g["sinks"] for g in grads]), "f_bias": jnp.stack([g["f_bias"] for g in grads]),
        "final_norm_w": d_final,
        "conv_w": jnp.stack([g["conv_w"] for g in grads]), "gate_bias": jnp.stack([g["gate_bias"] for g in grads])}
    shapes = [small_full[k].shape for k in _SMALL]
    summed = _unpack(_allreduce_small(_pack([small_full[k] for k in _SMALL]), name="allreduce_small"), shapes)
    gsmall = dict(zip(_SMALL, summed))
    gsmall["conv_w"] = lax.dynamic_slice_in_dim(gsmall["conv_w"], chip * conv_w.shape[2], conv_w.shape[2], axis=2)
    gsmall["gate_bias"] = lax.dynamic_slice_in_dim(gsmall["gate_bias"], chip * gate_bias.shape[2],
                                                   gate_bias.shape[2], axis=2)

    w_small = dict(norm_w=norm_w, conv_b=conv_b, dt_bias=dt_bias, a_log=a_log, d_skip=d_skip,
                   ssm_norm_w=ssm_norm_w, sinks=sinks, f_bias=f_bias, final_norm_w=final_norm_w, conv_w=conv_w,
                   gate_bias=gate_bias)
    m_small = dict(norm_w=m_norm_w, conv_b=m_conv_b, dt_bias=m_dt_bias, a_log=m_a_log, d_skip=m_d_skip,
                   ssm_norm_w=m_ssm_norm_w, sinks=m_sinks, f_bias=m_f_bias, final_norm_w=m_final_norm_w,
                   conv_w=m_conv_w, gate_bias=m_gate_bias)
    v_small = dict(norm_w=v_norm_w, conv_b=v_conv_b, dt_bias=v_dt_bias, a_log=v_a_log, d_skip=v_d_skip,
                   ssm_norm_w=v_ssm_norm_w, sinks=v_sinks, f_bias=v_f_bias, final_norm_w=v_final_norm_w,
                   conv_w=v_conv_w, gate_bias=v_gate_bias)
    sshapes = [w_small[k].shape for k in _SMALL]
    ds, ms, vs = _adamw(_pack([w_small[k] for k in _SMALL]), _pack([gsmall[k] for k in _SMALL]),
                        _pack([m_small[k] for k in _SMALL]), _pack([v_small[k] for k in _SMALL]), name="adamw_small")
    delta = dict(zip(_SMALL, _unpack(ds, sshapes)))
    new_m = dict(zip(_SMALL, _unpack(ms, sshapes)))
    new_v = dict(zip(_SMALL, _unpack(vs, sshapes)))
    grad = dict(gsmall)
    for nm, w, g, m, v in (("w_in", w_in, grad_w_in, m_w_in, v_w_in),
                           ("w_proj", w_proj, grad_w_proj, m_w_proj, v_w_proj),
                           ("w_out", w_out, grad_w_out, m_w_out, v_w_out)):
        grad[nm] = g
        delta[nm], new_m[nm], new_v[nm] = _adamw(w, g, m, v, name=f"adamw_{nm}")

    order = ("norm_w", "w_in", "conv_w", "conv_b", "dt_bias", "a_log", "d_skip", "ssm_norm_w", "sinks", "f_bias",
             "gate_bias", "w_proj", "w_out", "final_norm_w")
    return (loss, grad_x, *[grad[k] for k in order], *[delta[k] for k in order],
            *[new_m[k] for k in order], *[new_v[k] for k in order])
```

```python
import functools
import math

import numpy as np
import jax
import jax.numpy as jnp
from jax import lax
from jax.experimental import pallas as pl
from jax.experimental.pallas import tpu as pltpu

F32 = jnp.float32
BF16 = jnp.bfloat16
HIGHEST = lax.Precision.HIGHEST
MESH = pl.DeviceIdType.MESH

D_MODEL = 1024
HEAD_DIM = 64
N_HEADS = 16
N_GROUPS = 4
SSM_STATE = 128
CHUNK = 128
CONV_WIDTH = 4
CONV_DIM = 2048
ROPE_THETA = 10000.0
NORM_EPS = 1e-6
LANES = 128
N_CHIPS = 4
N_DEV = 8

ADAM_LR = 0.001
ADAM_B1 = 0.9
ADAM_B2 = 0.999
ADAM_EPS = 1e-08
ADAM_WD = 0.01
ADAM_STEP = 10

_REF_COLS = {}
_off = 0
for _n, _s in (("xbc", 2048), ("a_z", 1024), ("a_dt", 16), ("b_q", 1024), ("b_k", 256), ("b_v", 256),
               ("b_z", 1024), ("c_q", 1024), ("c_k", 1024), ("c_v", 1024), ("c_f", 16), ("c_z", 1024),
               ("gates", 3072)):
    _REF_COLS[_n] = (_off, _s)
    _off += _s
N_IN = _off

_PAD_ORDER = (("gates", 3072), ("xbc", 2048), ("a_z", 1024), ("b_q", 1024), ("b_z", 1024), ("c_q", 1024),
              ("c_k", 1024), ("c_v", 1024), ("c_z", 1024), ("b_k", 256), ("b_v", 256), ("a_dt", 512),
              ("c_f", 128))
_PAD_COLS = {}
_off = 0
for _n, _s in _PAD_ORDER:
    _PAD_COLS[_n] = (_off, _s)
    _off += _s
N_USED = _off
N_PAD = 13824


def _cp(sem, vmem_mb=48):
    return pltpu.CompilerParams(dimension_semantics=sem, vmem_limit_bytes=vmem_mb * 1024 * 1024)


def _dot(a, b, dims=((1,), (0,)), precision=None):
    return lax.dot_general(a, b, (dims, ((), ())), preferred_element_type=F32, precision=precision)


def _dot_nt(a, b):
    return _dot(a, b, ((1,), (1,)))


def _dot_tn(a, b):
    return _dot(a, b, ((0,), (0,)))


def _col(v, idx):
    lane = lax.broadcasted_iota(jnp.int32, v.shape, 1)
    return jnp.sum(jnp.where(lane == idx, v, 0.0), axis=1, keepdims=True)


def _row(v, idx):
    row = lax.broadcasted_iota(jnp.int32, v.shape, 0)
    return jnp.sum(jnp.where(row == idx, v, 0.0), axis=0, keepdims=True)


def _iota_col():
    return lax.broadcasted_iota(jnp.int32, (CHUNK, 1), 0)


def _iota_row():
    return lax.broadcasted_iota(jnp.int32, (1, LANES), 1)


def _sigmoid(x):
    return 1.0 / (1.0 + jnp.exp(-x))


def _softplus(x):
    return jnp.maximum(x, 0.0) + jnp.log(1.0 + jnp.exp(-jnp.abs(x)))


def _pad_w_in(w):
    parts = []
    for name, size in _PAD_ORDER:
        s0, sz = _REF_COLS[name]
        seg = w[:, s0:s0 + sz]
        if name == "a_dt":
            seg = jnp.pad(seg.reshape(-1, N_GROUPS, 4), ((0, 0), (0, 0), (0, LANES - 4))).reshape(-1, 512)
        elif name == "c_f":
            seg = jnp.pad(seg, ((0, 0), (0, LANES - 16)))
        parts.append(seg)
    parts.append(jnp.zeros((w.shape[0], N_PAD - N_USED), w.dtype))
    return jnp.concatenate(parts, axis=1)


def _unpad_w_in(wp):
    segs = {}
    for name, _ in _PAD_ORDER:
        p0, psz = _PAD_COLS[name]
        seg = wp[:, p0:p0 + psz]
        if name == "a_dt":
            seg = seg.reshape(-1, N_GROUPS, LANES)[:, :, :4].reshape(-1, 16)
        elif name == "c_f":
            seg = seg[:, :16]
        segs[name] = seg
    order = sorted(_REF_COLS, key=lambda n: _REF_COLS[n][0])
    return jnp.concatenate([segs[n] for n in order], axis=1)


def _group_lanes(v):
    return jnp.pad(v.reshape(N_GROUPS, 1, 4), ((0, 0), (0, 0), (0, LANES - 4)))


def _ungroup_lanes(v):
    return v[:, 0, :4].reshape(16)


def _mm(a, b, *, ta=False, tb=False, tm=512, tn=512, tk=512, out_dtype=F32, name):
    if ta:
        kdim, m = a.shape
    else:
        m, kdim = a.shape
    if tb:
        n, k2 = b.shape
    else:
        k2, n = b.shape
    assert kdim == k2, (a.shape, b.shape)
    tm, tn, tk = min(tm, m), min(tn, n), min(tk, kdim)
    assert m % tm == 0 and n % tn == 0 and kdim % tk == 0, (m, n, kdim, tm, tn, tk)
    nk = kdim // tk
    a_spec = (pl.BlockSpec((tk, tm), lambda i, j, k: (k, i)) if ta
              else pl.BlockSpec((tm, tk), lambda i, j, k: (i, k)))
    b_spec = (pl.BlockSpec((tn, tk), lambda i, j, k: (j, k)) if tb
              else pl.BlockSpec((tk, tn), lambda i, j, k: (k, j)))
    dims = ((0 if ta else 1,), (1 if tb else 0,))

    def body(a_ref, b_ref, o_ref, acc_ref):
        k = pl.program_id(2)
        p = _dot(a_ref[...].astype(BF16), b_ref[...].astype(BF16), dims)

        @pl.when(k == 0)
        def _():
            acc_ref[...] = p

        @pl.when(k > 0)
        def _():
            acc_ref[...] += p

        @pl.when(k == nk - 1)
        def _():
            o_ref[...] = acc_ref[...].astype(out_dtype)

    return pl.pallas_call(
        body, name=name, grid=(m // tm, n // tn, nk),
        in_specs=[a_spec, b_spec], out_specs=pl.BlockSpec((tm, tn), lambda i, j, k: (i, j)),
        out_shape=jax.ShapeDtypeStruct((m, n), out_dtype),
        scratch_shapes=[pltpu.VMEM((tm, tn), F32)],
        compiler_params=_cp(("parallel", "parallel", "arbitrary")),
    )(a, b)


def _rms_fwd(x, w, *, name, tm=512):
    t, d = x.shape

    def body(x_ref, w_ref, o_ref, ot_ref):
        xv = x_ref[...]
        r = lax.rsqrt(jnp.mean(xv * xv, axis=1, keepdims=True) + NORM_EPS)
        h = xv * r * w_ref[...]
        o_ref[...] = h.astype(BF16)
        ot_ref[...] = h.T.astype(BF16)

    return pl.pallas_call(
        body, name=name, grid=(t // tm,),
        in_specs=[pl.BlockSpec((tm, d), lambda i: (i, 0)), pl.BlockSpec((1, d), lambda i: (0, 0))],
        out_specs=[pl.BlockSpec((tm, d), lambda i: (i, 0)), pl.BlockSpec((d, tm), lambda i: (0, i))],
        out_shape=[jax.ShapeDtypeStruct((t, d), BF16), jax.ShapeDtypeStruct((d, t), BF16)],
        compiler_params=_cp(("parallel",)),
    )(x, w.reshape(1, d))


def _rms_bwd(x, w, dh, dres, *, name, tm=512):
    t, d = x.shape

    def body(x_ref, w_ref, dh_ref, dres_ref, dx_ref, dw_ref):
        xv = x_ref[...]
        r = lax.rsqrt(jnp.mean(xv * xv, axis=1, keepdims=True) + NORM_EPS)
        xhat = xv * r
        dhv = dh_ref[...]
        dxhat = dhv * w_ref[...]
        dx = r * (dxhat - xhat * jnp.mean(dxhat * xhat, axis=1, keepdims=True))
        dx_ref[...] = dres_ref[...] + dx

        @pl.when(pl.program_id(0) == 0)
        def _():
            dw_ref[...] = jnp.zeros_like(dw_ref)

        dw_ref[...] += jnp.sum(dhv * xhat, axis=0, keepdims=True)

    return pl.pallas_call(
        body, name=name, grid=(t // tm,),
        in_specs=[pl.BlockSpec((tm, d), lambda i: (i, 0)), pl.BlockSpec((1, d), lambda i: (0, 0)),
                  pl.BlockSpec((tm, d), lambda i: (i, 0)), pl.BlockSpec((tm, d), lambda i: (i, 0))],
        out_specs=[pl.BlockSpec((tm, d), lambda i: (i, 0)), pl.BlockSpec((1, d), lambda i: (0, 0))],
        out_shape=[jax.ShapeDtypeStruct((t, d), F32), jax.ShapeDtypeStruct((1, d), F32)],
        compiler_params=_cp(("arbitrary",)),
    )(x, w.reshape(1, d), dh, dres)


def _final_loss(x, w, target, *, name, tm=512):
    t, d = x.shape

    def body(x_ref, w_ref, t_ref, loss_ref, dx_ref, dw_ref):
        xv = x_ref[...]
        wv = w_ref[...]
        r = lax.rsqrt(jnp.mean(xv * xv, axis=1, keepdims=True) + NORM_EPS)
        xhat = xv * r
        err = xhat * wv - t_ref[...]
        dy = err * (1.0 / d)
        dxhat = dy * wv
        dx_ref[...] = r * (dxhat - xhat * jnp.mean(dxhat * xhat, axis=1, keepdims=True))

        @pl.when(pl.program_id(0) == 0)
        def _():
            dw_ref[...] = jnp.zeros_like(dw_ref)
            loss_ref[...] = jnp.zeros_like(loss_ref)

        dw_ref[...] += jnp.sum(dy * xhat, axis=0, keepdims=True)
        part = 0.5 * jnp.sum(jnp.mean(err * err, axis=1, keepdims=True), axis=0, keepdims=True)
        loss_ref[...] += jnp.broadcast_to(part, loss_ref.shape)

    return pl.pallas_call(
        body, name=name, grid=(t // tm,),
        in_specs=[pl.BlockSpec((tm, d), lambda i: (i, 0)), pl.BlockSpec((1, d), lambda i: (0, 0)),
                  pl.BlockSpec((tm, d), lambda i: (i, 0))],
        out_specs=[pl.BlockSpec((8, LANES), lambda i: (0, 0)), pl.BlockSpec((tm, d), lambda i: (i, 0)),
                   pl.BlockSpec((1, d), lambda i: (0, 0))],
        out_shape=[jax.ShapeDtypeStruct((8, LANES), F32), jax.ShapeDtypeStruct((t, d), F32),
                   jax.ShapeDtypeStruct((1, d), F32)],
        compiler_params=_cp(("arbitrary",)),
    )(x, w.reshape(1, d), target)


_CB = 128


def _conv_pre(u, w_ref, b_ref):
    s = u.shape[0]
    row = lax.broadcasted_iota(jnp.int32, u.shape, 0)
    pre = b_ref[...] + w_ref[CONV_WIDTH - 1:CONV_WIDTH, :] * u
    for sh in range(1, CONV_WIDTH):
        shifted = jnp.where(row >= sh, pltpu.roll(u, sh, 0), 0.0)
        pre = pre + w_ref[CONV_WIDTH - 1 - sh:CONV_WIDTH - sh, :] * shifted
    return pre


def _conv_fwd(proj3, cw, cb, *, name):
    b, s, _ = proj3.shape
    c0 = _PAD_COLS["xbc"][0] // _CB

    def body(u_ref, w_ref, b_ref, o_ref):
        pre = _conv_pre(u_ref[...].astype(F32), w_ref, b_ref)
        o_ref[...] = pre * _sigmoid(pre)

    return pl.pallas_call(
        body, name=name, grid=(b, CONV_DIM // _CB),
        in_specs=[pl.BlockSpec((None, s, _CB), lambda i, j: (i, 0, c0 + j)),
                  pl.BlockSpec((CONV_WIDTH, _CB), lambda i, j: (0, j)),
                  pl.BlockSpec((1, _CB), lambda i, j: (0, j))],
        out_specs=pl.BlockSpec((None, s, _CB), lambda i, j: (i, 0, j)),
        out_shape=jax.ShapeDtypeStruct((b, s, CONV_DIM), F32),
        compiler_params=_cp(("parallel", "parallel")),
    )(proj3, cw, cb.reshape(1, CONV_DIM))


def _conv_bwd(proj3, cw, cb, dact, *, name):
    b, s, _ = proj3.shape
    c0 = _PAD_COLS["xbc"][0] // _CB

    def body(u_ref, w_ref, b_ref, da_ref, du_ref, dwb_ref):
        u = u_ref[...].astype(F32)
        pre = _conv_pre(u, w_ref, b_ref)
        sg = _sigmoid(pre)
        dpre = da_ref[...] * (sg * (1.0 + pre * (1.0 - sg)))
        row = lax.broadcasted_iota(jnp.int32, u.shape, 0)
        du = w_ref[CONV_WIDTH - 1:CONV_WIDTH, :] * dpre
        rows = [jnp.sum(dpre * u, axis=0, keepdims=True)]
        for sh in range(1, CONV_WIDTH):
            fwd_shift = jnp.where(row < s - sh, pltpu.roll(dpre, s - sh, 0), 0.0)
            du = du + w_ref[CONV_WIDTH - 1 - sh:CONV_WIDTH - sh, :] * fwd_shift
            ush = jnp.where(row >= sh, pltpu.roll(u, sh, 0), 0.0)
            rows.append(jnp.sum(dpre * ush, axis=0, keepdims=True))
        du_ref[...] = du.astype(BF16)

        @pl.when(pl.program_id(1) == 0)
        def _():
            dwb_ref[...] = jnp.zeros_like(dwb_ref)

        for sh in range(CONV_WIDTH):
            k = CONV_WIDTH - 1 - sh
            dwb_ref[k:k + 1, :] += rows[sh]
        dwb_ref[CONV_WIDTH:CONV_WIDTH + 1, :] += jnp.sum(dpre, axis=0, keepdims=True)

    return pl.pallas_call(
        body, name=name, grid=(CONV_DIM // _CB, b),
        in_specs=[pl.BlockSpec((None, s, _CB), lambda j, i: (i, 0, c0 + j)),
                  pl.BlockSpec((CONV_WIDTH, _CB), lambda j, i: (0, j)),
                  pl.BlockSpec((1, _CB), lambda j, i: (0, j)),
                  pl.BlockSpec((None, s, _CB), lambda j, i: (i, 0, j))],
        out_specs=[pl.BlockSpec((None, s, _CB), lambda j, i: (i, 0, j)),
                   pl.BlockSpec((8, _CB), lambda j, i: (0, j))],
        out_shape=[jax.ShapeDtypeStruct((b, s, CONV_DIM), BF16), jax.ShapeDtypeStruct((8, CONV_DIM), F32)],
        compiler_params=_cp(("parallel", "arbitrary")),
    )(proj3, cw, cb.reshape(1, CONV_DIM), dact)


def _ssd_common(dt_ref, dtb_ref, alog_ref):
    row = lax.broadcasted_iota(jnp.int32, (CHUNK, CHUNK), 0)
    lane = lax.broadcasted_iota(jnp.int32, (CHUNK, CHUNK), 1)
    causal = row >= lane
    tri = causal.astype(F32)
    dtv = _softplus(dt_ref[...] + dtb_ref[...])
    a_row = -jnp.exp(alog_ref[...])
    acum = _dot(tri, dtv * a_row, precision=HIGHEST)
    return row, lane, causal, dtv, a_row, acum, acum.T


def _ssd_pair(pp, x, dtv, acum, acum_t, causal, lane, row):
    lo = lane < HEAD_DIM
    r0, r1 = 2 * pp, 2 * pp + 1
    dtp = jnp.where(lo, _col(dtv, r0), _col(dtv, r1))
    ac0, ac1 = _col(acum, r0), _col(acum, r1)
    ar0, ar1 = _row(acum_t, r0), _row(acum_t, r1)
    d0 = jnp.where(causal, jnp.exp(jnp.where(causal, ac0 - ar0, 0.0)), 0.0)
    d1 = jnp.where(causal, jnp.exp(jnp.where(causal, ac1 - ar1, 0.0)), 0.0)
    al0, al1 = _col(ar0, CHUNK - 1), _col(ar1, CHUNK - 1)
    eac = jnp.where(lo, jnp.exp(ac0), jnp.exp(ac1))
    dsp = jnp.where(lo, jnp.exp(al0 - ac0), jnp.exp(al1 - ac1))
    eal = jnp.where(_iota_col() < HEAD_DIM, jnp.exp(al0), jnp.exp(al1))
    return lo, dtp, x * dtp, d0, d1, al0, al1, eac, dsp, eal


def _ssd_fwd(proj3, gates3, xact3, dtb, alog, dsk, nw, *, name):
    b, s, _ = proj3.shape
    nc = s // CHUNK
    dt0 = 0
    z0 = _PAD_COLS["a_z"][0] // D_MODEL

    def body(xs_ref, bm_ref, cm_ref, dt_ref, z_ref, dtb_ref, alog_ref, dsk_ref, nw_ref,
             ya_ref, ypre_ref, hst_ref, h_scr):
        @pl.when(pl.program_id(1) == 0)
        def _():
            h_scr[...] = jnp.zeros_like(h_scr)

        for g in range(N_GROUPS):
            w256 = pl.ds(256 * g, 256)
            w128 = pl.ds(LANES * g, LANES)
            group(xs_ref.at[:, w256], bm_ref.at[:, w128], cm_ref.at[:, w128], dt_ref.at[:, w128],
                  z_ref.at[:, w256], dtb_ref.at[g], alog_ref.at[g], dsk_ref.at[g], nw_ref.at[g],
                  ya_ref.at[:, w256], ypre_ref.at[:, w256], hst_ref.at[g], h_scr.at[g])

    def group(xs_ref, bm_ref, cm_ref, dt_ref, z_ref, dtb_ref, alog_ref, dsk_ref, nw_ref,
              ya_ref, ypre_ref, hst_ref, h_scr):
        row, lane, causal, dtv, a_row, acum, acum_t = _ssd_common(dt_ref, dtb_ref, alog_ref)
        bb = bm_ref[...].astype(BF16)
        cb = cm_ref[...].astype(BF16)
        cbm = _dot_nt(cb, bb)
        hst_ref[...] = h_scr[...]
        dskv = dsk_ref[...]
        for pp in range(2):
            x = xs_ref[:, LANES * pp:LANES * (pp + 1)]
            lo, dtp, xd, d0, d1, al0, al1, eac, dsp, eal = _ssd_pair(pp, x, dtv, acum, acum_t, causal, lane, row)
            xdb = xd.astype(BF16)
            y = jnp.where(lo, _dot((cbm * d0).astype(BF16), xdb), _dot((cbm * d1).astype(BF16), xdb))
            h = h_scr[pp]
            y = y + eac * _dot_nt(cb, h.astype(BF16))
            h_scr[pp] = h * eal + _dot_tn((xd * dsp).astype(BF16), bb)
            dskp = jnp.where((_iota_row() < HEAD_DIM), _col(dskv, 2 * pp), _col(dskv, 2 * pp + 1))
            ypre_ref[:, LANES * pp:LANES * (pp + 1)] = y + x * dskp
        ypre = ypre_ref[...]
        z = z_ref[...].astype(F32)
        yg = ypre * (z * _sigmoid(z))
        rstd = lax.rsqrt(jnp.sum(yg * yg, axis=1, keepdims=True) * (1.0 / 256.0) + NORM_EPS)
        ya_ref[...] = (yg * rstd * nw_ref[...]).astype(BF16)

    g = N_GROUPS
    par = pl.BlockSpec((g, 1, LANES), lambda i, c: (0, 0, 0))
    wide = pl.BlockSpec((None, CHUNK, D_MODEL), lambda i, c: (i, c, 0))
    return pl.pallas_call(
        body, name=name, grid=(b, nc),
        in_specs=[wide,
                  pl.BlockSpec((None, CHUNK, 512), lambda i, c: (i, c, 2)),
                  pl.BlockSpec((None, CHUNK, 512), lambda i, c: (i, c, 3)),
                  pl.BlockSpec((None, CHUNK, 512), lambda i, c: (i, c, dt0)),
                  pl.BlockSpec((None, CHUNK, D_MODEL), lambda i, c: (i, c, z0)),
                  par, par, par,
                  pl.BlockSpec((g, 1, 256), lambda i, c: (0, 0, 0))],
        out_specs=[wide, wide,
                   pl.BlockSpec((None, None, g, 2, CHUNK, SSM_STATE), lambda i, c: (i, c, 0, 0, 0, 0))],
        out_shape=[jax.ShapeDtypeStruct((b, s, D_MODEL), BF16), jax.ShapeDtypeStruct((b, s, D_MODEL), F32),
                   jax.ShapeDtypeStruct((b, nc, g, 2, CHUNK, SSM_STATE), F32)],
        scratch_shapes=[pltpu.VMEM((g, 2, CHUNK, SSM_STATE), F32)],
        compiler_params=_cp(("parallel", "arbitrary")),
    )(xact3, xact3, xact3, gates3, proj3, dtb, alog, dsk, nw)


def _ssd_bwd(proj3, gates3, xact3, dtb, alog, dsk, nw, ypre3, hst, dya3, *, name):
    b, s, _ = proj3.shape
    nc = s // CHUNK
    dt0 = 0
    z0 = _PAD_COLS["a_z"][0] // D_MODEL

    def body(xs_ref, bm_ref, cm_ref, dt_ref, z_ref, dtb_ref, alog_ref, dsk_ref, nw_ref, ypre_ref, hst_ref,
             dya_ref, dact_ref, dz_ref, ddt_ref, ddtb_ref, dalog_ref, ddsk_ref, dnw_ref, dh_scr):
        first = jnp.logical_and(pl.program_id(0) == 0, pl.program_id(1) == 0)

        @pl.when(first)
        def _():
            ddtb_ref[...] = jnp.zeros_like(ddtb_ref)
            dalog_ref[...] = jnp.zeros_like(dalog_ref)
            ddsk_ref[...] = jnp.zeros_like(ddsk_ref)
            dnw_ref[...] = jnp.zeros_like(dnw_ref)

        @pl.when(pl.program_id(1) == 0)
        def _():
            dh_scr[...] = jnp.zeros_like(dh_scr)

        for g in range(N_GROUPS):
            w256 = pl.ds(256 * g, 256)
            w128 = pl.ds(LANES * g, LANES)
            group(xs_ref.at[:, w256], bm_ref.at[:, w128], cm_ref.at[:, w128], dt_ref.at[:, w128],
                  z_ref.at[:, w256], dtb_ref.at[g], alog_ref.at[g], dsk_ref.at[g], nw_ref.at[g],
                  ypre_ref.at[:, w256], hst_ref.at[g], dya_ref.at[:, w256],
                  dact_ref.at[:, w256], dact_ref.at[:, pl.ds(D_MODEL + LANES * g, LANES)],
                  dact_ref.at[:, pl.ds(D_MODEL + 512 + LANES * g, LANES)], dz_ref.at[:, w256], ddt_ref.at[:, w128],
                  ddtb_ref.at[g], dalog_ref.at[g], ddsk_ref.at[g], dnw_ref.at[g], dh_scr.at[g])

    def group(xs_ref, bm_ref, cm_ref, dt_ref, z_ref, dtb_ref, alog_ref, dsk_ref, nw_ref, ypre_ref, hst_ref,
              dya_ref, dxs_ref, dbm_ref, dcm_ref, dz_ref, ddt_ref, ddtb_ref, dalog_ref, ddsk_ref, dnw_ref,
              dh_scr):
        row, lane, causal, dtv, a_row, acum, acum_t = _ssd_common(dt_ref, dtb_ref, alog_ref)
        lane1 = _iota_row()
        bb = bm_ref[...].astype(BF16)
        cb = cm_ref[...].astype(BF16)
        cbm = _dot_nt(cb, bb)

        z = z_ref[...].astype(F32)
        ypre = ypre_ref[...]
        dya = dya_ref[...]
        sz = _sigmoid(z)
        silu = z * sz
        yg = ypre * silu
        rstd = lax.rsqrt(jnp.sum(yg * yg, axis=1, keepdims=True) * (1.0 / 256.0) + NORM_EPS)
        dnw_ref[...] += jnp.sum(dya * yg * rstd, axis=0, keepdims=True)
        dn = dya * nw_ref[...]
        dyg = rstd * dn - yg * (rstd * rstd * rstd * (1.0 / 256.0)) * jnp.sum(dn * yg, axis=1, keepdims=True)
        dz_ref[...] = (dyg * ypre * (sz * (1.0 + z * (1.0 - sz)))).astype(BF16)
        dy_all = dyg * silu

        dskv = dsk_ref[...]
        da_cols = jnp.zeros((CHUNK, LANES), F32)
        dxt_cols = jnp.zeros((CHUNK, LANES), F32)
        ddsk_row = jnp.zeros((1, LANES), F32)
        dcb = jnp.zeros((CHUNK, CHUNK), F32)
        dc = jnp.zeros((CHUNK, SSM_STATE), F32)
        db = jnp.zeros((CHUNK, SSM_STATE), F32)
        last = _iota_col() == CHUNK - 1
        for pp in range(2):
            r0, r1 = 2 * pp, 2 * pp + 1
            x = xs_ref[:, LANES * pp:LANES * (pp + 1)]
            dy = dy_all[:, LANES * pp:LANES * (pp + 1)]
            lo, dtp, xd, d0, d1, al0, al1, eac, dsp, eal = _ssd_pair(pp, x, dtv, acum, acum_t, causal, lane, row)
            w0, w1 = cbm * d0, cbm * d1
            w0b, w1b = w0.astype(BF16), w1.astype(BF16)
            xdb = xd.astype(BF16)
            dyb = dy.astype(BF16)
            h = hst_ref[pp]
            dhn = dh_scr[pp]
            hb = h.astype(BF16)
            dhb = dhn.astype(BF16)
            g0 = _dot_nt(jnp.where(lo, dy, 0.0).astype(BF16), xdb)
            g1 = _dot_nt(jnp.where(lo, 0.0, dy).astype(BF16), xdb)
            dcb = dcb + g0 * d0 + g1 * d1
            m0, m1 = g0 * w0, g1 * w1
            bdh = _dot_nt(bb, dhb)
            dxd = jnp.where(lo, _dot_tn(w0b, dyb), _dot_tn(w1b, dyb)) + dsp * bdh
            ch = _dot_nt(cb, hb)
            edy = eac * dy
            edyb = edy.astype(BF16)
            xds = xd * dsp
            dc = dc + _dot(edyb, hb)
            db = db + _dot(xds.astype(BF16), dhb)
            dh_scr[pp] = dhn * eal + _dot_tn(edyb, cb)
            t2 = edy * ch
            t3 = xds * bdh
            r4 = jnp.sum(dhn * h, axis=1, keepdims=True)
            s4_0 = jnp.sum(jnp.where(_iota_col() < HEAD_DIM, r4, 0.0), axis=0, keepdims=True)
            s4_1 = jnp.sum(r4, axis=0, keepdims=True) - s4_0
            t2_0 = jnp.sum(jnp.where(lo, t2, 0.0), axis=1, keepdims=True)
            t2_1 = jnp.sum(t2, axis=1, keepdims=True) - t2_0
            t3_0 = jnp.sum(jnp.where(lo, t3, 0.0), axis=1, keepdims=True)
            t3_1 = jnp.sum(t3, axis=1, keepdims=True) - t3_0
            dal0 = jnp.sum(t3_0, axis=0, keepdims=True) + jnp.exp(al0) * s4_0
            dal1 = jnp.sum(t3_1, axis=0, keepdims=True) + jnp.exp(al1) * s4_1
            dac0 = (jnp.sum(m0, axis=1, keepdims=True) - jnp.sum(m0.T, axis=1, keepdims=True)
                    + t2_0 - t3_0 + jnp.where(last, dal0, 0.0))
            dac1 = (jnp.sum(m1, axis=1, keepdims=True) - jnp.sum(m1.T, axis=1, keepdims=True)
                    + t2_1 - t3_1 + jnp.where(last, dal1, 0.0))
            da_cols = da_cols + jnp.where(lane == r0, dac0, 0.0) + jnp.where(lane == r1, dac1, 0.0)
            xx = dxd * x
            x0 = jnp.sum(jnp.where(lo, xx, 0.0), axis=1, keepdims=True)
            x1 = jnp.sum(xx, axis=1, keepdims=True) - x0
            dxt_cols = dxt_cols + jnp.where(lane == r0, x0, 0.0) + jnp.where(lane == r1, x1, 0.0)
            dskp = jnp.where((_iota_row() < HEAD_DIM), _col(dskv, r0), _col(dskv, r1))
            dxs_ref[:, LANES * pp:LANES * (pp + 1)] = dxd * dtp + dy * dskp
            yx = jnp.sum(dy * x, axis=0, keepdims=True)
            k0 = jnp.sum(jnp.where((_iota_row() < HEAD_DIM), yx, 0.0), axis=1, keepdims=True)
            k1 = jnp.sum(yx, axis=1, keepdims=True) - k0
            ddsk_row = ddsk_row + jnp.where(lane1 == r0, k0, 0.0) + jnp.where(lane1 == r1, k1, 0.0)
        dcbb = dcb.astype(BF16)
        dcm_ref[...] = dc + _dot(dcbb, bb)
        dbm_ref[...] = db + _dot_tn(dcbb, cb)
        tri_t = (row <= lane).astype(F32)
        dadt = _dot(tri_t, da_cols, precision=HIGHEST)
        ddtv = dadt * a_row + dxt_cols
        dalog_ref[...] += jnp.sum(dadt * dtv, axis=0, keepdims=True) * a_row
        ddt_raw = ddtv * _sigmoid(dt_ref[...] + dtb_ref[...])
        ddt_ref[...] = ddt_raw.astype(BF16)
        ddtb_ref[...] += jnp.sum(ddt_raw, axis=0, keepdims=True)
        ddsk_ref[...] += ddsk_row

    g = N_GROUPS
    rc = lambda c: nc - 1 - c
    par = pl.BlockSpec((g, 1, LANES), lambda i, c: (0, 0, 0))
    parw = pl.BlockSpec((g, 1, 256), lambda i, c: (0, 0, 0))
    wide = pl.BlockSpec((None, CHUNK, D_MODEL), lambda i, c: (i, rc(c), 0))
    blk512 = lambda col: pl.BlockSpec((None, CHUNK, 512), lambda i, c: (i, rc(c), col))
    return pl.pallas_call(
        body, name=name, grid=(b, nc),
        in_specs=[wide, blk512(2), blk512(3), blk512(dt0),
                  pl.BlockSpec((None, CHUNK, D_MODEL), lambda i, c: (i, rc(c), z0)),
                  par, par, par, parw,
                  wide,
                  pl.BlockSpec((None, None, g, 2, CHUNK, SSM_STATE), lambda i, c: (i, rc(c), 0, 0, 0, 0)),
                  wide],
        out_specs=[pl.BlockSpec((None, CHUNK, CONV_DIM), lambda i, c: (i, rc(c), 0)), wide, blk512(0),
                   par, par, par, parw],
        out_shape=[jax.ShapeDtypeStruct((b, s, CONV_DIM), F32), jax.ShapeDtypeStruct((b, s, D_MODEL), BF16),
                   jax.ShapeDtypeStruct((b, s, 512), BF16),
                   jax.ShapeDtypeStruct((g, 1, LANES), F32), jax.ShapeDtypeStruct((g, 1, LANES), F32),
                   jax.ShapeDtypeStruct((g, 1, LANES), F32), jax.ShapeDtypeStruct((g, 1, 256), F32)],
        scratch_shapes=[pltpu.VMEM((g, 2, CHUNK, SSM_STATE), F32)],
        compiler_params=_cp(("arbitrary", "arbitrary")),
    )(xact3, xact3, xact3, gates3, proj3, dtb, alog, dsk, nw, ypre3, hst, dya3)


def _fgate_fwd(gates3, fb, *, name):
    b, s, _ = gates3.shape
    f0 = _PAD_COLS["a_dt"][1] // LANES

    def body(f_ref, fb_ref, cum_ref, carry):
        @pl.when(pl.program_id(1) == 0)
        def _():
            carry[...] = jnp.zeros_like(carry)

        row = lax.broadcasted_iota(jnp.int32, (CHUNK, CHUNK), 0)
        lane = lax.broadcasted_iota(jnp.int32, (CHUNK, CHUNK), 1)
        tri = (row >= lane).astype(F32)
        lf = -_softplus(-(f_ref[...] + fb_ref[...]))
        cs = _dot(tri, lf, precision=HIGHEST) + carry[0:1, :]
        cum_ref[...] = cs
        carry[0:1, :] = _row(cs, CHUNK - 1)

    return pl.pallas_call(
        body, name=name, grid=(b, s // CHUNK),
        in_specs=[pl.BlockSpec((None, CHUNK, LANES), lambda i, c: (i, c, f0)),
                  pl.BlockSpec((1, LANES), lambda i, c: (0, 0))],
        out_specs=pl.BlockSpec((None, CHUNK, LANES), lambda i, c: (i, c, 0)),
        out_shape=jax.ShapeDtypeStruct((b, s, LANES), F32),
        scratch_shapes=[pltpu.VMEM((8, LANES), F32)],
        compiler_params=_cp(("parallel", "arbitrary")),
    )(gates3, fb)


def _fgate_bwd(gates3, fb, dcum, *, name):
    b, s, _ = gates3.shape
    nc = s // CHUNK
    f0 = _PAD_COLS["a_dt"][1] // LANES
    npair = dcum.shape[1]

    def body(f_ref, fb_ref, dc_ref, df_ref, dfb_ref, carry):
        first = jnp.logical_and(pl.program_id(0) == 0, pl.program_id(1) == 0)

        @pl.when(first)
        def _():
            dfb_ref[...] = jnp.zeros_like(dfb_ref)

        @pl.when(pl.program_id(1) == 0)
        def _():
            carry[...] = jnp.zeros_like(carry)

        row = lax.broadcasted_iota(jnp.int32, (CHUNK, CHUNK), 0)
        lane = lax.broadcasted_iota(jnp.int32, (CHUNK, CHUNK), 1)
        tri_t = (row <= lane).astype(F32)
        dc = -jnp.sum(dc_ref[...], axis=0)
        dlf = _dot(tri_t, dc, precision=HIGHEST) + carry[0:1, :]
        carry[0:1, :] = _row(dlf, 0)
        df = dlf * _sigmoid(-(f_ref[...] + fb_ref[...]))
        df_ref[...] = df.astype(BF16)
        dfb_ref[...] += jnp.sum(df, axis=0, keepdims=True)

    return pl.pallas_call(
        body, name=name, grid=(b, nc),
        in_specs=[pl.BlockSpec((None, CHUNK, LANES), lambda i, c: (i, nc - 1 - c, f0)),
                  pl.BlockSpec((1, LANES), lambda i, c: (0, 0)),
                  pl.BlockSpec((None, npair, CHUNK, LANES), lambda i, c: (i, 0, nc - 1 - c, 0))],
        out_specs=[pl.BlockSpec((None, CHUNK, LANES), lambda i, c: (i, nc - 1 - c, 0)),
                   pl.BlockSpec((1, LANES), lambda i, c: (0, 0))],
        out_shape=[jax.ShapeDtypeStruct((b, s, LANES), BF16), jax.ShapeDtypeStruct((1, LANES), F32)],
        scratch_shapes=[pltpu.VMEM((8, LANES), F32)],
        compiler_params=_cp(("arbitrary", "arbitrary")),
    )(gates3, fb, dcum)


_SCALE = HEAD_DIM ** -0.5
_NEG = -1e30


_ST_LSE, _ST_DELTA, _ST_MJ = 0, 2, 8


_SR = 40


def _ck_rep(cum):
    b, s, _ = cum.shape
    t = jnp.transpose(cum[:, :, :N_HEADS], (0, 2, 1)).reshape(b, N_HEADS // 2, 2, s, 1)
    return jnp.broadcast_to(t, (b, N_HEADS // 2, 2, s, LANES))


def _foxt_fwd(proj3, ckrep, *, name, tb):
    b, s, _ = proj3.shape
    nq = s // tb
    assert _ST_MJ + 2 * nq <= _SR
    q0 = _PAD_COLS["c_q"][0] // LANES
    k0 = _PAD_COLS["c_k"][0] // LANES
    v0 = _PAD_COLS["c_v"][0] // LANES
    z0 = _PAD_COLS["c_z"][0] // LANES
    rep = tb // LANES

    def body(q_ref, k_ref, v_ref, z_ref, ck_ref, y_ref, o_ref, st_ref):
        i = pl.program_id(2)
        lane = lax.broadcasted_iota(jnp.int32, (tb, LANES), 1)
        lo = lane < HEAD_DIM
        lo_r = lax.broadcasted_iota(jnp.int32, (LANES, tb), 0) < HEAD_DIM
        srow = lax.broadcasted_iota(jnp.int32, (_SR, tb), 0)
        q = q_ref[...].astype(F32) * _SCALE
        qms = (jnp.where(lo, q, 0.0).astype(BF16), jnp.where(lo, 0.0, q).astype(BF16))
        ones_at = (HEAD_DIM, 0)

        def block(j, carry, diagonal):
            ks = pl.ds(pl.multiple_of(j * tb, tb), tb)
            kb = k_ref[ks, :].astype(BF16)
            v = v_ref[ks, :].astype(F32)
            vts = (jnp.where(lo, v, jnp.where(lane == ones_at[0], 1.0, 0.0)).T.astype(BF16),
                   jnp.where(lo, jnp.where(lane == ones_at[1], 1.0, 0.0), v).T.astype(BF16))
            if diagonal:
                key = lax.broadcasted_iota(jnp.int32, (tb, tb), 0)
                qry = lax.broadcasted_iota(jnp.int32, (tb, tb), 1)
                mask = key <= qry
            ms, ls, acc, st = carry
            new_m, new_l, pvs, alphas = [], [], [], []
            for hh in range(2):
                sc = _dot_nt(kb, qms[hh]) - jnp.tile(ck_ref[hh, ks, :], (1, rep))
                if diagonal:
                    sc = jnp.where(mask, sc, _NEG)
                m_new = jnp.maximum(ms[hh], jnp.max(sc, axis=0, keepdims=True))
                alpha = jnp.exp(ms[hh] - m_new)
                pv = _dot(vts[hh], jnp.exp(sc - m_new).astype(BF16))
                rs = _row(pv[ones_at[hh]:ones_at[hh] + 8, :], 0)
                new_l.append(alpha * ls[hh] + rs)
                new_m.append(m_new)
                pvs.append(pv)
                alphas.append(alpha)
                st = jnp.where(srow == _ST_MJ + 2 * j + hh, m_new, st)
            acc = jnp.where(lo_r, alphas[0] * acc + pvs[0], alphas[1] * acc + pvs[1])
            return (tuple(new_m), tuple(new_l), acc, st)

        neg = jnp.full((1, tb), _NEG, F32)
        zero = jnp.zeros((1, tb), F32)
        init = ((neg, neg), (zero, zero), jnp.zeros((LANES, tb), F32), jnp.zeros((_SR, tb), F32))
        carry = lax.fori_loop(0, i, lambda j, c: block(j, c, False), init)
        ms, ls, acc, st = block(i, carry, True)
        o = (acc / jnp.where(lo_r, ls[0], ls[1])).T
        o_ref[...] = o
        st = jnp.where(srow == _ST_LSE, ms[0] + jnp.log(ls[0]), st)
        st_ref[...] = jnp.where(srow == _ST_LSE + 1, ms[1] + jnp.log(ls[1]), st)
        z = z_ref[...].astype(F32)
        y_ref[...] = (o * (z * _sigmoid(z))).astype(BF16)

    qspec = lambda c0: pl.BlockSpec((None, tb, LANES), lambda bi, p, i: (bi, i, c0 + p))
    kspec = lambda c0: pl.BlockSpec((None, s, LANES), lambda bi, p, i: (bi, 0, c0 + p))
    ospec = pl.BlockSpec((None, tb, LANES), lambda bi, p, i: (bi, i, p))
    return pl.pallas_call(
        body, name=name, grid=(b, N_HEADS // 2, nq),
        in_specs=[qspec(q0), kspec(k0), kspec(v0), qspec(z0),
                  pl.BlockSpec((None, None, 2, s, LANES), lambda bi, p, i: (bi, p, 0, 0, 0))],
        out_specs=[ospec, ospec, pl.BlockSpec((None, None, None, _SR, tb), lambda bi, p, i: (bi, p, i, 0, 0))],
        out_shape=[jax.ShapeDtypeStruct((b, s, D_MODEL), BF16), jax.ShapeDtypeStruct((b, s, D_MODEL), F32),
                   jax.ShapeDtypeStruct((b, N_HEADS // 2, nq, _SR, tb), F32)],
        compiler_params=_cp(("parallel", "parallel", "arbitrary")),
    )(proj3, proj3, proj3, proj3, ckrep)


def _foxt_prep(proj3, o3, stat, dy3, *, name, tb):
    b, s, _ = proj3.shape
    nq = s // tb
    z0 = _PAD_COLS["c_z"][0] // LANES

    def body(z_ref, o_ref, fst_ref, dy_ref, dz_ref, do_ref, st_ref):
        z = z_ref[...].astype(F32)
        sz = _sigmoid(z)
        dy = dy_ref[...]
        o = o_ref[...]
        do = dy * (z * sz)
        dz_ref[...] = (dy * o * (sz * (1.0 + z * (1.0 - sz)))).astype(BF16)
        do_ref[...] = do
        doo = do.astype(BF16).astype(F32) * o
        r8 = lax.broadcasted_iota(jnp.int32, (8, LANES), 0)
        l8 = lax.broadcasted_iota(jnp.int32, (8, LANES), 1)
        pick = jnp.logical_or(jnp.logical_and(r8 == 0, l8 < HEAD_DIM),
                              jnp.logical_and(r8 == 1, l8 >= HEAD_DIM)).astype(F32)
        d8 = _dot(pick, doo, ((1,), (1,)), precision=HIGHEST)
        srow = lax.broadcasted_iota(jnp.int32, (_SR, tb), 0)
        st = jnp.where(srow == _ST_DELTA, _row(d8, 0), fst_ref[...])
        st_ref[...] = jnp.where(srow == _ST_DELTA + 1, _row(d8, 1), st)

    ospec = pl.BlockSpec((None, tb, LANES), lambda bi, p, i: (bi, i, p))
    sspec = pl.BlockSpec((None, None, None, _SR, tb), lambda bi, p, i: (bi, p, i, 0, 0))
    return pl.pallas_call(
        body, name=name, grid=(b, N_HEADS // 2, nq),
        in_specs=[pl.BlockSpec((None, tb, LANES), lambda bi, p, i: (bi, i, z0 + p)), ospec, sspec, ospec],
        out_specs=[ospec, ospec, sspec],
        out_shape=[jax.ShapeDtypeStruct((b, s, D_MODEL), BF16), jax.ShapeDtypeStruct((b, s, D_MODEL), F32),
                   jax.ShapeDtypeStruct((b, N_HEADS // 2, nq, _SR, tb), F32)],
        compiler_params=_cp(("parallel", "parallel", "parallel")),
    )(proj3, o3, stat, dy3)


def _foxt_bwd(proj3, ckrep, do3, stats, *, name, tb):
    b, s, _ = proj3.shape
    nq = s // tb
    q0 = _PAD_COLS["c_q"][0] // LANES
    k0 = _PAD_COLS["c_k"][0] // LANES
    v0 = _PAD_COLS["c_v"][0] // LANES
    rep = tb // LANES

    def body(q_ref, do_ref, st_ref, k_ref, v_ref, ck_ref, dq_ref, dk_ref, dv_ref, cs_ref):
        j = pl.program_id(2)
        lane = lax.broadcasted_iota(jnp.int32, (tb, LANES), 1)
        lo = lane < HEAD_DIM
        lo_r = lax.broadcasted_iota(jnp.int32, (LANES, tb), 0) < HEAD_DIM

        @pl.when(j == 0)
        def _():
            dq_ref[...] = jnp.zeros_like(dq_ref)

        kf = k_ref[...].astype(F32)
        kb = kf.astype(BF16)
        kt = kf.T.astype(BF16)
        vb = v_ref[...].astype(BF16)
        cks = (jnp.tile(ck_ref[0], (1, rep)), jnp.tile(ck_ref[1], (1, rep)))

        def block(i, carry, diagonal):
            qs = pl.ds(pl.multiple_of(i * tb, tb), tb)
            q = q_ref[qs, :].astype(F32) * _SCALE
            do = do_ref[qs, :]
            st = st_ref[i]
            if diagonal:
                key = lax.broadcasted_iota(jnp.int32, (tb, tb), 0)
                qry = lax.broadcasted_iota(jnp.int32, (tb, tb), 1)
                mask = key <= qry
            dk, dv, cs = carry
            new_cs, dqs = [], []
            for hh in range(2):
                sel = lo if hh == 0 else jnp.logical_not(lo)
                qm = jnp.where(sel, q, 0.0).astype(BF16)
                dom = jnp.where(sel, do, 0.0).astype(BF16)
                sc = _dot_nt(kb, qm) - cks[hh]
                if diagonal:
                    sc = jnp.where(mask, sc, _NEG)
                mj = _row(st, _ST_MJ + 2 * j + hh)
                w = jnp.exp(mj - _row(st, _ST_LSE + hh))
                ph = jnp.exp(sc - mj).astype(BF16).astype(F32) * w
                ds = ph * (_dot_nt(vb, dom) - _row(st, _ST_DELTA + hh))
                dsb = ds.astype(BF16)
                dv = dv + _dot(ph.astype(BF16), dom)
                dk = dk + _dot(dsb, qm)
                new_cs.append(cs[hh] + jnp.sum(ds, axis=1, keepdims=True))
                dqs.append(_dot(kt, dsb))
            dq_ref[i] += jnp.where(lo_r, dqs[0], dqs[1]) * _SCALE
            return (dk, dv, tuple(new_cs))

        zcol = jnp.zeros((tb, 1), F32)
        init = (jnp.zeros((tb, LANES), F32), jnp.zeros((tb, LANES), F32), (zcol, zcol))
        carry = block(j, init, True)
        dk, dv, cs = lax.fori_loop(j + 1, nq, lambda i, c: block(i, c, False), carry)
        dk_ref[...] = dk.astype(BF16)
        dv_ref[...] = dv.astype(BF16)
        p2 = 2 * pl.program_id(1)
        cs_ref[...] = jnp.where(lane == p2, cs[0], jnp.where(lane == p2 + 1, cs[1], 0.0))

    full = lambda c0: pl.BlockSpec((None, s, LANES), lambda bi, p, j: (bi, 0, c0 + p))
    kspec = lambda c0: pl.BlockSpec((None, tb, LANES), lambda bi, p, j: (bi, j, c0 + p))
    ko = pl.BlockSpec((None, tb, LANES), lambda bi, p, j: (bi, j, p))
    sall = pl.BlockSpec((None, None, nq, _SR, tb), lambda bi, p, j: (bi, p, 0, 0, 0))
    dqspec = pl.BlockSpec((None, None, nq, LANES, tb), lambda bi, p, j: (bi, p, 0, 0, 0))
    return pl.pallas_call(
        body, name=name, grid=(b, N_HEADS // 2, nq),
        in_specs=[full(q0), full(0), sall, kspec(k0), kspec(v0),
                  pl.BlockSpec((None, None, 2, tb, LANES), lambda bi, p, j: (bi, p, 0, j, 0))],
        out_specs=[dqspec, ko, ko, pl.BlockSpec((None, None, tb, LANES), lambda bi, p, j: (bi, p, j, 0))],
        out_shape=[jax.ShapeDtypeStruct((b, N_HEADS // 2, nq, LANES, tb), F32),
                   jax.ShapeDtypeStruct((b, s, D_MODEL), BF16), jax.ShapeDtypeStruct((b, s, D_MODEL), BF16),
                   jax.ShapeDtypeStruct((b, N_HEADS // 2, s, LANES), F32)],
        compiler_params=_cp(("parallel", "parallel", "arbitrary")),
    )(proj3, do3, stats, proj3, proj3, ckrep)


def _rope(x, cos, sin_signed):
    w = x.shape[1]
    lane = lax.broadcasted_iota(jnp.int32, x.shape, 1)
    first = (lane % HEAD_DIM) < (HEAD_DIM // 2)
    rot = jnp.where(first, pltpu.roll(x, w - HEAD_DIM // 2, 1), pltpu.roll(x, HEAD_DIM // 2, 1))
    return x * cos + rot * sin_signed


_QB = 4
_QROWS = _QB * CHUNK


def _swa_keys(g, kc_ref, kp_ref, vc_ref, vp_ref, cq_ref, sq_ref, cp_ref, sp_ref):
    def both_halves(x):
        x = x.astype(F32)
        lane = lax.broadcasted_iota(jnp.int32, x.shape, 1)
        keep = (lane // HEAD_DIM) == (g % 2)
        return jnp.where(keep, x, pltpu.roll(x, HEAD_DIM, 1))

    cq, sq, cpv, spv = cq_ref[...], sq_ref[...], cp_ref[...], sp_ref[...]
    kc = _rope(both_halves(kc_ref[...]), cq, sq).astype(BF16)
    kp = _rope(both_halves(kp_ref[...]), cpv, spv).astype(BF16)
    return cq, sq, cpv, spv, kc, kp, both_halves(vc_ref[...]).astype(BF16), both_halves(vp_ref[...]).astype(BF16)


def _swa_stack(pairs, lo):
    return jnp.concatenate([jnp.where(lo, pairs[0], 0.0), jnp.where(lo, 0.0, pairs[0]),
                            jnp.where(lo, pairs[1], 0.0), jnp.where(lo, 0.0, pairs[1])], axis=0).astype(BF16)


def _swa_mask4(prev_valid):
    r = lax.broadcasted_iota(jnp.int32, (4 * CHUNK, 2 * CHUNK), 0) & (CHUNK - 1)
    c = lax.broadcasted_iota(jnp.int32, (4 * CHUNK, 2 * CHUNK), 1)
    own = jnp.logical_and(c >= CHUNK, c - CHUNK <= r)
    before = jnp.logical_and(c < CHUNK, c > r)
    if prev_valid is True:
        return jnp.logical_or(own, before)
    return jnp.logical_or(own, jnp.logical_and(before, prev_valid))


def _swa_sink4(skv):
    return jnp.concatenate([jnp.broadcast_to(_col(skv, j), (CHUNK, 1)) for j in range(4)], axis=0)


def _swa_specs(order):
    def spec(shape, fn):
        return pl.BlockSpec(shape, lambda *ids: fn(*order(*ids)))

    q0 = _PAD_COLS["b_q"][0] // 256
    z0 = _PAD_COLS["b_z"][0] // 256
    k0 = _PAD_COLS["b_k"][0] // LANES
    v0 = _PAD_COLS["b_v"][0] // LANES
    prev = lambda i: jnp.maximum(_QB * i - 1, 0)
    return dict(
        kc=spec((None, _QROWS, LANES), lambda bi, g, i: (bi, i, k0 + g // 2)),
        kp=spec((None, CHUNK, LANES), lambda bi, g, i: (bi, prev(i), k0 + g // 2)),
        vc=spec((None, _QROWS, LANES), lambda bi, g, i: (bi, i, v0 + g // 2)),
        vp=spec((None, CHUNK, LANES), lambda bi, g, i: (bi, prev(i), v0 + g // 2)),
        q=spec((None, _QROWS, 256), lambda bi, g, i: (bi, i, q0 + g)),
        z=spec((None, _QROWS, 256), lambda bi, g, i: (bi, i, z0 + g)),
        blk=spec((None, _QROWS, 256), lambda bi, g, i: (bi, i, g)),
        kcur=spec((None, _QROWS, LANES), lambda bi, g, i: (bi, i, g)),
        kstep=spec((None, CHUNK, LANES), lambda bi, g, i: (bi, i, g)),
        tcur=spec((_QROWS, LANES), lambda bi, g, i: (i, 0)),
        tprev=spec((CHUNK, LANES), lambda bi, g, i: (prev(i), 0)),
        sk=spec((None, 1, LANES), lambda bi, g, i: (g, 0, 0)))


def _swa_fwd(proj3, cos, sin, sinks, *, name):
    b, s, _ = proj3.shape

    def body(q_ref, z_ref, kc_ref, kp_ref, vc_ref, vp_ref, cq_ref, sq_ref, cp_ref, sp_ref, sk_ref,
             y_ref, o_ref, lse_ref):
        i = pl.program_id(2)
        cq_all, sq_all, _, _, kc_all, kp0, vc_all, vp0 = _swa_keys(
            pl.program_id(1), kc_ref, kp_ref, vc_ref, vp_ref, cq_ref, sq_ref, cp_ref, sp_ref)
        lo = lax.broadcasted_iota(jnp.int32, (CHUNK, LANES), 1) < HEAD_DIM
        sink4 = _swa_sink4(sk_ref[...])
        for u in range(_QB):
            rs = slice(CHUNK * u, CHUNK * (u + 1))
            ps = slice(CHUNK * (u - 1), CHUNK * u)
            cq, sq = cq_all[rs], sq_all[rs]
            kp, vp = (kp0, vp0) if u == 0 else (kc_all[ps], vc_all[ps])
            kk = jnp.concatenate([kp, kc_all[rs]], axis=0)
            vv = jnp.concatenate([vp, vc_all[rs]], axis=0)
            q4 = _swa_stack([_rope(q_ref[rs, LANES * pp:LANES * (pp + 1)].astype(F32), cq, sq) * _SCALE
                             for pp in range(2)], lo)
            sc = jnp.where(_swa_mask4(True if u > 0 else i > 0), _dot_nt(q4, kk), _NEG)
            m = jnp.maximum(jnp.max(sc, axis=1, keepdims=True), sink4)
            pr = jnp.exp(sc - m)
            l = jnp.sum(pr, axis=1, keepdims=True) + jnp.exp(sink4 - m)
            o4 = _dot(pr.astype(BF16), vv) / l
            lse4 = m + jnp.log(l)
            for pp in range(2):
                ls = slice(LANES * pp, LANES * (pp + 1))
                h0 = slice(2 * CHUNK * pp, 2 * CHUNK * pp + CHUNK)
                h1 = slice(2 * CHUNK * pp + CHUNK, 2 * CHUNK * (pp + 1))
                o = jnp.where(lo, o4[h0], o4[h1])
                z = z_ref[rs, ls].astype(F32)
                o_ref[rs, ls] = o
                lse_ref[rs, ls] = jnp.where(lo, lse4[h0], lse4[h1])
                y_ref[rs, ls] = (o * (z * _sigmoid(z))).astype(BF16)

    sp = _swa_specs(lambda bi, g, i: (bi, g, i))
    return pl.pallas_call(
        body, name=name, grid=(b, N_GROUPS, s // _QROWS),
        in_specs=[sp["q"], sp["z"], sp["kc"], sp["kp"], sp["vc"], sp["vp"],
                  sp["tcur"], sp["tcur"], sp["tprev"], sp["tprev"], sp["sk"]],
        out_specs=[sp["blk"], sp["blk"], sp["blk"]],
        out_shape=[jax.ShapeDtypeStruct((b, s, D_MODEL), BF16)] + [jax.ShapeDtypeStruct((b, s, D_MODEL), F32)] * 2,
        compiler_params=_cp(("parallel", "parallel", "parallel")),
    )(proj3, proj3, proj3, proj3, proj3, proj3, cos, sin, cos, sin, sinks)


def _swa_bwd(proj3, cos, sin, sinks, o3, lse3, dy3, *, name):
    b, s, _ = proj3.shape

    def body(q_ref, z_ref, kc_ref, kp_ref, vc_ref, vp_ref, cq_ref, sq_ref, cp_ref, sp_ref, sk_ref,
             o_ref, lse_ref, dy_ref, dq_ref, dz_ref, dkc_ref, dkp_ref, dvc_ref, dvp_ref, dsk_ref):
        i = pl.program_id(2)
        first = jnp.logical_and(pl.program_id(1) == 0, i == 0)

        @pl.when(first)
        def _():
            dsk_ref[...] = jnp.zeros_like(dsk_ref)

        cq_all, sq_all, cpv, spv, kc_all, kp0, vc_all, vp0 = _swa_keys(
            pl.program_id(0), kc_ref, kp_ref, vc_ref, vp_ref, cq_ref, sq_ref, cp_ref, sp_ref)
        lo = lax.broadcasted_iota(jnp.int32, (CHUNK, LANES), 1) < HEAD_DIM
        lane1 = lax.broadcasted_iota(jnp.int32, (1, LANES), 1)
        sink4 = _swa_sink4(sk_ref[...])
        zero = jnp.zeros((CHUNK, LANES), F32)
        dks = [zero] * (_QB + 1)
        dvs = [zero] * (_QB + 1)
        dsk_row = jnp.zeros((1, LANES), F32)
        for u in range(_QB):
            rs = slice(CHUNK * u, CHUNK * (u + 1))
            ps = slice(CHUNK * (u - 1), CHUNK * u)
            cq, sq = cq_all[rs], sq_all[rs]
            kp, vp = (kp0, vp0) if u == 0 else (kc_all[ps], vc_all[ps])
            kk = jnp.concatenate([kp, kc_all[rs]], axis=0)
            vv = jnp.concatenate([vp, vc_all[rs]], axis=0)
            q4 = _swa_stack([_rope(q_ref[rs, LANES * pp:LANES * (pp + 1)].astype(F32), cq, sq) * _SCALE
                             for pp in range(2)], lo)
            dos, lses = [], []
            for pp in range(2):
                ls = slice(LANES * pp, LANES * (pp + 1))
                z = z_ref[rs, ls].astype(F32)
                sz = _sigmoid(z)
                dy = dy_ref[rs, ls]
                dos.append(dy * (z * sz))
                dz_ref[rs, ls] = (dy * o_ref[rs, ls] * (sz * (1.0 + z * (1.0 - sz)))).astype(BF16)
                lse = lse_ref[rs, ls]
                lses += [_col(lse, 0), _col(lse, HEAD_DIM)]
            do4 = _swa_stack(dos, lo)
            lse4 = jnp.concatenate(lses, axis=0)
            pr = jnp.exp(jnp.where(_swa_mask4(True if u > 0 else i > 0), _dot_nt(q4, kk), _NEG) - lse4)
            dp = _dot_nt(do4, vv)
            dl = jnp.sum(pr * dp, axis=1, keepdims=True)
            ds = (pr * (dp - dl)).astype(BF16)
            dsink = -jnp.exp(sink4 - lse4) * dl
            for j in range(4):
                dsk_row = dsk_row + jnp.where(
                    lane1 == j, jnp.sum(dsink[CHUNK * j:CHUNK * (j + 1)], axis=0, keepdims=True), 0.0)
            dq4 = _dot(ds, kk)
            dkk = _dot_tn(ds, q4)
            dvv = _dot_tn(pr.astype(BF16), do4)
            dks[u], dks[u + 1] = dks[u] + dkk[:CHUNK], dks[u + 1] + dkk[CHUNK:]
            dvs[u], dvs[u + 1] = dvs[u] + dvv[:CHUNK], dvs[u + 1] + dvv[CHUNK:]
            for pp in range(2):
                h0 = slice(2 * CHUNK * pp, 2 * CHUNK * pp + CHUNK)
                h1 = slice(2 * CHUNK * pp + CHUNK, 2 * CHUNK * (pp + 1))
                dq_ref[rs, LANES * pp:LANES * (pp + 1)] = _rope(
                    jnp.where(lo, dq4[h0], dq4[h1]) * _SCALE, cq, -sq).astype(BF16)
        fold = lambda v: v + pltpu.roll(v, HEAD_DIM, 1)
        dkp_ref[...] = fold(_rope(dks[0], cpv, -spv))
        dvp_ref[...] = fold(dvs[0])
        for u in range(_QB):
            rs = slice(CHUNK * u, CHUNK * (u + 1))
            dkc_ref[rs, :] = fold(_rope(dks[u + 1], cq_all[rs], -sq_all[rs]))
            dvc_ref[rs, :] = fold(dvs[u + 1])
        dsk_ref[...] += dsk_row

    sp = _swa_specs(lambda g, bi, i: (bi, g, i))
    kv_shape = jax.ShapeDtypeStruct((b, s, 512), F32)
    kvp_shape = jax.ShapeDtypeStruct((b, s // _QB, 512), F32)
    return pl.pallas_call(
        body, name=name, grid=(N_GROUPS, b, s // _QROWS),
        in_specs=[sp["q"], sp["z"], sp["kc"], sp["kp"], sp["vc"], sp["vp"],
                  sp["tcur"], sp["tcur"], sp["tprev"], sp["tprev"], sp["sk"], sp["blk"], sp["blk"], sp["blk"]],
        out_specs=[sp["blk"], sp["blk"], sp["kcur"], sp["kstep"], sp["kcur"], sp["kstep"], sp["sk"]],
        out_shape=[jax.ShapeDtypeStruct((b, s, D_MODEL), BF16), jax.ShapeDtypeStruct((b, s, D_MODEL), BF16),
                   kv_shape, kvp_shape, kv_shape, kvp_shape, jax.ShapeDtypeStruct((N_GROUPS, 1, LANES), F32)],
        compiler_params=_cp(("arbitrary", "arbitrary", "arbitrary")),
    )(proj3, proj3, proj3, proj3, proj3, proj3, cos, sin, cos, sin, sinks, o3, lse3, dy3)


def _branch_fwd(ys, proj, gb, wp, wo, x, *, name, tm=256):
    t = proj.shape[0]
    g0 = _PAD_COLS["gates"][0] // D_MODEL

    def body(g_ref, a_ref, b_ref, c_ref, gb_ref, wp_ref, wo_ref, x_ref, ba_ref, bb_ref, bc_ref, m_ref, xn_ref):
        acc = None
        for i, (y, br) in enumerate(((a_ref, ba_ref), (b_ref, bb_ref), (c_ref, bc_ref))):
            bri = _dot(y[...], wp_ref[i])
            br[...] = bri
            gate = _sigmoid(g_ref[:, D_MODEL * i:D_MODEL * (i + 1)].astype(F32) + gb_ref[i:i + 1, :])
            acc = gate * bri if acc is None else acc + gate * bri
        mb = acc.astype(BF16)
        m_ref[...] = mb
        xn_ref[...] = x_ref[...] + _dot(mb, wo_ref[...])

    row = pl.BlockSpec((tm, D_MODEL), lambda i: (i, 0))
    rowf = jax.ShapeDtypeStruct((t, D_MODEL), F32)
    outs = pl.pallas_call(
        body, name=name, grid=(t // tm,),
        in_specs=[pl.BlockSpec((tm, 3 * D_MODEL), lambda i: (i, g0)), row, row, row,
                  pl.BlockSpec((3, D_MODEL), lambda i: (0, 0)),
                  pl.BlockSpec((3, D_MODEL, D_MODEL), lambda i: (0, 0, 0)),
                  pl.BlockSpec((D_MODEL, D_MODEL), lambda i: (0, 0)), row],
        out_specs=[row, row, row, row, row],
        out_shape=[rowf, rowf, rowf, jax.ShapeDtypeStruct((t, D_MODEL), BF16), rowf],
        compiler_params=_cp(("parallel",)),
    )(proj, ys[0], ys[1], ys[2], gb, wp, wo, x)
    return outs[:3], outs[3], outs[4]


def _branch_bwd(dx, proj, br, gb, wp, wo, *, name, tm=256):
    t = proj.shape[0]
    g0 = _PAD_COLS["gates"][0] // D_MODEL

    def body(g_ref, a_ref, b_ref, c_ref, gb_ref, wp_ref, wo_ref, dx_ref,
             da_ref, db_ref, dc_ref, dg_ref, dgb_ref, ya_ref, yb_ref, yc_ref):
        @pl.when(pl.program_id(0) == 0)
        def _():
            dgb_ref[...] = jnp.zeros_like(dgb_ref)

        dmv = _dot_nt(dx_ref[...].astype(BF16), wo_ref[...])
        for i, (r, dr, dy) in enumerate(((a_ref, da_ref, ya_ref), (b_ref, db_ref, yb_ref), (c_ref, dc_ref, yc_ref))):
            gate = _sigmoid(g_ref[:, D_MODEL * i:D_MODEL * (i + 1)].astype(F32) + gb_ref[i:i + 1, :])
            dbr = (dmv * gate).astype(BF16)
            dr[...] = dbr
            dg = dmv * r[...] * gate * (1.0 - gate)
            dg_ref[:, D_MODEL * i:D_MODEL * (i + 1)] = dg.astype(BF16)
            dgb_ref[i:i + 1, :] += jnp.sum(dg, axis=0, keepdims=True)
            dy[...] = _dot_nt(dbr, wp_ref[i])

    row = pl.BlockSpec((tm, D_MODEL), lambda i: (i, 0))
    rowb = jax.ShapeDtypeStruct((t, D_MODEL), BF16)
    rowf = jax.ShapeDtypeStruct((t, D_MODEL), F32)
    outs = pl.pallas_call(
        body, name=name, grid=(t // tm,),
        in_specs=[pl.BlockSpec((tm, 3 * D_MODEL), lambda i: (i, g0)), row, row, row,
                  pl.BlockSpec((3, D_MODEL), lambda i: (0, 0)),
                  pl.BlockSpec((3, D_MODEL, D_MODEL), lambda i: (0, 0, 0)),
                  pl.BlockSpec((D_MODEL, D_MODEL), lambda i: (0, 0)), row],
        out_specs=[row, row, row, pl.BlockSpec((tm, 3 * D_MODEL), lambda i: (i, 0)),
                   pl.BlockSpec((8, D_MODEL), lambda i: (0, 0)), row, row, row],
        out_shape=[rowb, rowb, rowb, jax.ShapeDtypeStruct((t, 3 * D_MODEL), BF16),
                   jax.ShapeDtypeStruct((8, D_MODEL), F32), rowf, rowf, rowf],
        compiler_params=_cp(("arbitrary",)),
    )(proj, br[0], br[1], br[2], gb, wp, wo, dx)
    return outs[:3], outs[3], outs[4], outs[5:]


def _rope_tables(s):
    pos = jnp.arange(s, dtype=F32)
    inv_freq = ROPE_THETA ** (-jnp.arange(0, HEAD_DIM, 2, dtype=F32) / HEAD_DIM)
    ang = pos[:, None] * inv_freq[None, :]
    cos, sin = jnp.cos(ang), jnp.sin(ang)
    return jnp.tile(cos, (1, 4)), jnp.tile(jnp.concatenate([-sin, sin], axis=1), (1, 2))


def _layer_params(wl):
    return dict(
        dtb=_group_lanes(wl["dt_bias"]), alog=_group_lanes(wl["a_log"]), dsk=_group_lanes(wl["d_skip"]),
        nw=wl["ssm_norm_w"].reshape(N_GROUPS, 1, 256), sinks=_group_lanes(wl["sinks"]),
        fb=jnp.pad(wl["f_bias"], (0, LANES - N_HEADS)).reshape(1, LANES))


def _layer_fwd(x, wl, tabs, bsz, li, tb):
    t = x.shape[0]
    s = t // bsz
    cos, sin = tabs
    lp = _layer_params(wl)
    n = lambda k: f"l{li}_{k}"
    h, h_t = _rms_fwd(x, wl["norm_w"], name=n("rms_fwd"))
    proj = _mm(h, wl["w_in"], tm=1024, tn=1536, tk=1024, out_dtype=BF16, name=n("mm_proj"))
    proj3 = proj.reshape(bsz, s, N_PAD)
    g0, gw = _PAD_COLS["a_dt"][0], _PAD_COLS["a_dt"][1] + _PAD_COLS["c_f"][1]
    gates3 = _mm(h, wl["w_in"][:, g0:g0 + gw], tm=1024, tn=gw, tk=1024, name=n("mm_gates")).reshape(bsz, s, gw)
    xact3 = _conv_fwd(proj3, wl["conv_w"], wl["conv_b"], name=n("conv_fwd"))
    ya3, ypre3, hst = _ssd_fwd(proj3, gates3, xact3, lp["dtb"], lp["alog"], lp["dsk"], lp["nw"], name=n("ssd_fwd"))
    yb3, ob3, lseb3 = _swa_fwd(proj3, cos, sin, lp["sinks"], name=n("swa_fwd"))
    cum = _fgate_fwd(gates3, lp["fb"], name=n("fgate_fwd"))
    cum_t = _ck_rep(cum)
    yc3, oc3, statc3 = _foxt_fwd(proj3, cum_t, name=n("fox_fwd"), tb=tb)
    ys = [v.reshape(t, D_MODEL) for v in (ya3, yb3, yc3)]
    br, merged, x_new = _branch_fwd(ys, proj, wl["gate_bias"], wl["w_proj"], wl["w_out"], x, name=n("branch_fwd"))
    saved = dict(x=x, h_t=h_t, proj=proj, gates3=gates3, xact3=xact3, ypre3=ypre3, hst=hst, ob3=ob3, lseb3=lseb3,
                 cum_t=cum_t, oc3=oc3, statc3=statc3, ys=ys, br=br, merged=merged, lp=lp)
    return x_new, saved


def _layer_bwd(dx, wl, sv, tabs, bsz, li, tb):
    t = dx.shape[0]
    s = t // bsz
    cos, sin = tabs
    lp = sv["lp"]
    n = lambda k: f"l{li}_{k}"
    proj = sv["proj"]
    proj3 = proj.reshape(bsz, s, N_PAD)
    g = {}
    g["w_out"] = _mm(sv["merged"], dx, ta=True, tm=1024, tn=1024, tk=512, name=n("mm_dwout"))
    dbr, dgates, dgb, dys = _branch_bwd(dx, proj, sv["br"], wl["gate_bias"], wl["w_proj"], wl["w_out"],
                                        name=n("branch_bwd"))
    g["gate_bias"] = dgb[:3]
    g["w_proj"] = jnp.stack([_mm(sv["ys"][i], dbr[i], ta=True, tm=1024, tn=1024, tk=512, name=n(f"mm_dwproj{i}"))
                             for i in range(3)])
    dy3 = [v.reshape(bsz, s, D_MODEL) for v in dys]

    (dact, daz, dadt, ddtb, dalog, ddsk, dnw) = _ssd_bwd(
        proj3, sv["gates3"], sv["xact3"], lp["dtb"], lp["alog"], lp["dsk"], lp["nw"], sv["ypre3"], sv["hst"], dy3[0],
        name=n("ssd_bwd"))
    g["dt_bias"], g["a_log"], g["d_skip"] = _ungroup_lanes(ddtb), _ungroup_lanes(dalog), _ungroup_lanes(ddsk)
    g["ssm_norm_w"] = dnw.reshape(D_MODEL)
    dxbc, dwb = _conv_bwd(proj3, wl["conv_w"], wl["conv_b"], dact, name=n("conv_bwd"))
    g["conv_w"], g["conv_b"] = dwb[:CONV_WIDTH], dwb[CONV_WIDTH]

    dbq, dbz, dkc, dkp, dvc, dvp, dsk = _swa_bwd(proj3, cos, sin, lp["sinks"], sv["ob3"],
                                                 sv["lseb3"], dy3[1], name=n("swa_bwd"))
    g["sinks"] = _ungroup_lanes(dsk)

    def fold(cur, prv):
        p4 = prv.reshape(bsz, s // _QROWS, 1, CHUNK, 512)
        tail = jnp.concatenate([p4[:, 1:], jnp.zeros_like(p4[:, :1])], axis=1)
        shifted = jnp.concatenate([jnp.zeros((bsz, s // _QROWS, _QB - 1, CHUNK, 512), F32), tail], axis=2)
        tot = cur + shifted.reshape(bsz, s, 512)
        return tot.reshape(bsz, s, N_GROUPS, 2, HEAD_DIM)[:, :, :, 0].reshape(bsz, s, 256)

    dbk, dbv = fold(dkc, dkp), fold(dvc, dvp)

    dcz, do3, stats = _foxt_prep(proj3, sv["oc3"], sv["statc3"], dy3[2], name=n("fox_prep"), tb=tb)
    dqt, dck, dcv, csum = _foxt_bwd(proj3, sv["cum_t"], do3, stats, name=n("fox_bwd"), tb=tb)
    dcq = jnp.transpose(dqt, (0, 2, 4, 1, 3)).reshape(bsz, s, D_MODEL)
    dcf, dfb = _fgate_bwd(sv["gates3"], lp["fb"], csum, name=n("fgate_bwd"))
    g["f_bias"] = dfb[0, :N_HEADS]

    parts = {"gates": dgates.reshape(bsz, s, 3 * D_MODEL), "xbc": dxbc, "a_z": daz, "b_q": dbq, "b_z": dbz,
             "c_q": dcq, "c_k": dck, "c_v": dcv, "c_z": dcz, "b_k": dbk, "b_v": dbv, "a_dt": dadt, "c_f": dcf}
    dproj = jnp.concatenate([parts[name].astype(BF16) for name, _ in _PAD_ORDER]
                            + [jnp.zeros((bsz, s, N_PAD - N_USED), BF16)], axis=2).reshape(t, N_PAD)
    dh = _mm(dproj, wl["w_in"], tb=True, tm=1024, tn=1024, tk=1536, name=n("mm_dh"))
    g["w_in"] = _unpad_w_in(_mm(sv["h_t"], dproj, tm=1024, tn=768, tk=2048, name=n("mm_dwin")))
    dx_in, dnorm = _rms_bwd(sv["x"], wl["norm_w"], dh, dx, name=n("rms_bwd"))
    g["norm_w"] = dnorm[0]
    return dx_in, g


def _local_step(x, target, wls, final_norm_w, tb=1024):
    bsz, s, d = x.shape
    t = bsz * s
    tabs = _rope_tables(s)
    xc = x.reshape(t, d)
    saved = []
    for li, wl in enumerate(wls):
        xc, sv = _layer_fwd(xc, wl, tabs, bsz, li, tb)
        saved.append(sv)
    loss, dx, dfw = _final_loss(xc, final_norm_w, target.reshape(t, d), name="final_loss")
    grads = [None] * len(wls)
    for li in reversed(range(len(wls))):
        dx, grads[li] = _layer_bwd(dx, wls[li], saved[li], tabs, bsz, li, tb)
    return loss[0, 0], dx.reshape(bsz, s, d), grads, dfw[0]


_HBM = pl.BlockSpec(memory_space=pltpu.HBM)


def _chip_peers(x, y):
    return [(1 - x, y), (x, 1 - y), (1 - x, 1 - y)]


def _gather_weights(arrs, *, name):
    n = len(arrs)

    def body(*refs):
        ins, outs = refs[:n], refs[n:2 * n]
        ici_send, ici_recv, d2d_send, d2d_recv = refs[2 * n:]
        x, y, c = lax.axis_index("x"), lax.axis_index("y"), lax.axis_index("c")
        me = 2 * x + y
        peers = _chip_peers(x, y)
        sib = (x, y, 1 - c)
        sends, fwds = [], []
        for a in range(n):
            for k, (px, py) in enumerate(peers):
                cp = pltpu.make_async_remote_copy(
                    src_ref=ins[a].at[c], dst_ref=outs[a].at[me, c], send_sem=ici_send.at[a, k],
                    recv_sem=ici_recv.at[a, k], device_id=(px, py, c), device_id_type=MESH)
                cp.start()
                sends.append(cp)
        for a in range(n):
            for k, (px, py) in enumerate(peers):
                slot = 2 * px + py
                pltpu.make_async_remote_copy(
                    src_ref=ins[a].at[c], dst_ref=outs[a].at[slot, c], send_sem=ici_send.at[a, k],
                    recv_sem=ici_recv.at[a, k], device_id=(px, py, c), device_id_type=MESH).wait_recv()
                fw = pltpu.make_async_remote_copy(
                    src_ref=outs[a].at[slot, c], dst_ref=outs[a].at[slot, c], send_sem=d2d_send.at[a, k],
                    recv_sem=d2d_recv.at[a, k], device_id=sib, device_id_type=MESH)
                fw.start()
                fwds.append(fw)
        for a in range(n):
            for k, (px, py) in enumerate(peers):
                slot = 2 * px + py
                pltpu.make_async_remote_copy(
                    src_ref=outs[a].at[slot, 1 - c], dst_ref=outs[a].at[slot, 1 - c], send_sem=d2d_send.at[a, k],
                    recv_sem=d2d_recv.at[a, k], device_id=sib, device_id_type=MESH).wait_recv()
        for cp in sends + fwds:
            cp.wait_send()

    out_shape = [jax.ShapeDtypeStruct((N_CHIPS,) + a.shape, a.dtype) for a in arrs]
    return pl.pallas_call(
        body, name=name, out_shape=out_shape, in_specs=[_HBM] * n, out_specs=[_HBM] * n,
        scratch_shapes=[pltpu.SemaphoreType.DMA((n, 3)), pltpu.SemaphoreType.DMA((n, 3)),
                        pltpu.SemaphoreType.DMA((n, 3)), pltpu.SemaphoreType.DMA((n, 3))],
    )(*arrs)


def _pair_exchange(arrs, *, name):
    n = len(arrs)

    def body(*refs):
        ins, outs = refs[:n], refs[n:2 * n]
        send, recv = refs[2 * n:]
        x, y, c = lax.axis_index("x"), lax.axis_index("y"), lax.axis_index("c")
        sib = (x, y, 1 - c)
        cps = []
        for a in range(n):
            for k in range(N_CHIPS):
                cp = pltpu.make_async_remote_copy(
                    src_ref=ins[a].at[k, 1 - c], dst_ref=outs[a].at[k], send_sem=send.at[a, k],
                    recv_sem=recv.at[a, k], device_id=sib, device_id_type=MESH)
                cp.start()
                cps.append(cp)
        for cp in cps:
            cp.wait()

    out_shape = [jax.ShapeDtypeStruct((N_CHIPS,) + a.shape[2:], a.dtype) for a in arrs]
    return pl.pallas_call(
        body, name=name, out_shape=out_shape, in_specs=[_HBM] * n, out_specs=[_HBM] * n,
        scratch_shapes=[pltpu.SemaphoreType.DMA((n, N_CHIPS)), pltpu.SemaphoreType.DMA((n, N_CHIPS))],
    )(*arrs)


def _chip_exchange(arrs, *, name):
    n = len(arrs)

    def body(*refs):
        ins, outs = refs[:n], refs[n:2 * n]
        send, recv = refs[2 * n:]
        x, y, c = lax.axis_index("x"), lax.axis_index("y"), lax.axis_index("c")
        me = 2 * x + y
        peers = _chip_peers(x, y)
        cps = []
        for a in range(n):
            for k, (px, py) in enumerate(peers):
                cp = pltpu.make_async_remote_copy(
                    src_ref=ins[a].at[2 * px + py], dst_ref=outs[a].at[me], send_sem=send.at[a, k],
                    recv_sem=recv.at[a, k], device_id=(px, py, c), device_id_type=MESH)
                cp.start()
                cps.append(cp)
        for a in range(n):
            for k, (px, py) in enumerate(peers):
                pltpu.make_async_remote_copy(
                    src_ref=ins[a].at[2 * px + py], dst_ref=outs[a].at[2 * px + py], send_sem=send.at[a, k],
                    recv_sem=recv.at[a, k], device_id=(px, py, c), device_id_type=MESH).wait_recv()
        for cp in cps:
            cp.wait_send()

    out_shape = [jax.ShapeDtypeStruct(a.shape, a.dtype) for a in arrs]
    return pl.pallas_call(
        body, name=name, out_shape=out_shape, in_specs=[_HBM] * n, out_specs=[_HBM] * n,
        scratch_shapes=[pltpu.SemaphoreType.DMA((n, 3)), pltpu.SemaphoreType.DMA((n, 3))],
    )(*arrs)


def _pair_share(arrs, *, name):
    n = len(arrs)

    def body(*refs):
        ins, outs = refs[:n], refs[n:2 * n]
        send, recv = refs[2 * n:]
        x, y, c = lax.axis_index("x"), lax.axis_index("y"), lax.axis_index("c")
        sib = (x, y, 1 - c)
        cps = []
        for a in range(n):
            cp = pltpu.make_async_remote_copy(
                src_ref=ins[a], dst_ref=outs[a], send_sem=send.at[a], recv_sem=recv.at[a],
                device_id=sib, device_id_type=MESH)
            cp.start()
            cps.append(cp)
        for cp in cps:
            cp.wait()

    out_shape = [jax.ShapeDtypeStruct(a.shape, a.dtype) for a in arrs]
    return pl.pallas_call(
        body, name=name, out_shape=out_shape, in_specs=[_HBM] * n, out_specs=[_HBM] * n,
        scratch_shapes=[pltpu.SemaphoreType.DMA((n,)), pltpu.SemaphoreType.DMA((n,))],
    )(*arrs)


def _allreduce_small(buf, *, name):
    r = buf.shape[0]

    def body(in_ref, out_ref, land, send, recv):
        x, y, c = lax.axis_index("x"), lax.axis_index("y"), lax.axis_index("c")
        me = 4 * x + 2 * y + c
        land[me] = in_ref[...]
        cps = []
        for k in range(1, N_DEV):
            px, py, pc = x ^ ((k >> 2) & 1), y ^ ((k >> 1) & 1), c ^ (k & 1)
            cp = pltpu.make_async_remote_copy(
                src_ref=in_ref, dst_ref=land.at[me], send_sem=send.at[k - 1], recv_sem=recv.at[k - 1],
                device_id=(px, py, pc), device_id_type=MESH)
            cp.start()
            cps.append(cp)
        for k in range(1, N_DEV):
            px, py, pc = x ^ ((k >> 2) & 1), y ^ ((k >> 1) & 1), c ^ (k & 1)
            pltpu.make_async_remote_copy(
                src_ref=in_ref, dst_ref=land.at[4 * px + 2 * py + pc], send_sem=send.at[k - 1],
                recv_sem=recv.at[k - 1], device_id=(px, py, pc), device_id_type=MESH).wait_recv()
        for cp in cps:
            cp.wait_send()
        acc = land[0]
        for k in range(1, N_DEV):
            acc = acc + land[k]
        out_ref[...] = acc

    vm = pl.BlockSpec(memory_space=pltpu.VMEM)
    return pl.pallas_call(
        body, name=name, out_shape=jax.ShapeDtypeStruct((r, LANES), F32), in_specs=[vm], out_specs=vm,
        scratch_shapes=[pltpu.VMEM((N_DEV, r, LANES), F32), pltpu.SemaphoreType.DMA((N_DEV - 1,)),
                        pltpu.SemaphoreType.DMA((N_DEV - 1,))],
    )(buf)


def _row_tile(rows, cols, n_arrays, budget=20 * 1024 * 1024):
    best = 8 if rows % 8 == 0 else rows
    tr = 8
    while tr <= rows:
        if rows % tr == 0 and tr * cols * 4 * n_arrays * 2 <= budget:
            best = tr
        tr *= 2
    return best


def _add_slot_layer(full, other, *, name):
    _, _, r, cdim = full.shape
    tr = _row_tile(r, cdim, 4)

    def body(c_ref, a_ref, b_ref, o_ref, ob_ref):
        sm = a_ref[...] + b_ref[...]
        o_ref[...] = sm
        ob_ref[...] = sm.astype(BF16)

    c = lax.axis_index("c").astype(jnp.int32).reshape(1)
    blk = pl.BlockSpec((None, tr, cdim), lambda k, i, c_ref: (k, i, 0))
    return pl.pallas_call(
        body, name=name,
        grid_spec=pltpu.PrefetchScalarGridSpec(
            num_scalar_prefetch=1, grid=(N_CHIPS, r // tr),
            in_specs=[pl.BlockSpec((None, None, tr, cdim), lambda k, i, c_ref: (k, c_ref[0], i, 0)), blk],
            out_specs=[blk, blk]),
        out_shape=[jax.ShapeDtypeStruct((N_CHIPS, r, cdim), F32), jax.ShapeDtypeStruct((N_CHIPS, r, cdim), BF16)],
        compiler_params=_cp(("parallel", "parallel")),
    )(c, full, other)


def _sum_slots(parts, pair, *, name):
    _, r, cdim = parts.shape
    tr = _row_tile(r, cdim, 5)

    def body(me_ref, p_ref, own_ref, o_ref):
        me = me_ref[0]
        acc = None
        for k in range(N_CHIPS):
            term = jnp.where(me == k, own_ref[...], p_ref[k].astype(F32))
            acc = term if acc is None else acc + term
        o_ref[...] = acc

    me = (2 * lax.axis_index("x") + lax.axis_index("y")).astype(jnp.int32).reshape(1)
    return pl.pallas_call(
        body, name=name,
        grid_spec=pltpu.PrefetchScalarGridSpec(
            num_scalar_prefetch=1, grid=(r // tr,),
            in_specs=[pl.BlockSpec((N_CHIPS, tr, cdim), lambda i, me_ref: (0, i, 0)),
                      pl.BlockSpec((None, tr, cdim), lambda i, me_ref: (me_ref[0], i, 0))],
            out_specs=pl.BlockSpec((tr, cdim), lambda i, me_ref: (i, 0))),
        out_shape=jax.ShapeDtypeStruct((r, cdim), F32),
        compiler_params=_cp(("parallel",)),
    )(me, parts, pair)


def _adamw(w, g, m, v, *, name):
    lead, (r, cdim) = w.shape[:-2], w.shape[-2:]
    nl = len(lead)
    tr = _row_tile(r, cdim, 7)
    c1 = 1.0 - ADAM_B1 ** ADAM_STEP
    c2 = 1.0 - ADAM_B2 ** ADAM_STEP

    def body(w_ref, g_ref, m_ref, v_ref, d_ref, nm_ref, nv_ref):
        gv = g_ref[...]
        mn = ADAM_B1 * m_ref[...] + (1.0 - ADAM_B1) * gv
        vn = ADAM_B2 * v_ref[...] + (1.0 - ADAM_B2) * (gv * gv)
        nm_ref[...] = mn
        nv_ref[...] = vn
        d_ref[...] = -ADAM_LR * ((mn / c1) / (jnp.sqrt(vn / c2) + ADAM_EPS) + ADAM_WD * w_ref[...])

    blk = pl.BlockSpec((None,) * nl + (tr, cdim), lambda *ids: ids[:nl] + (ids[nl], 0))
    sh = jax.ShapeDtypeStruct(w.shape, F32)
    return pl.pallas_call(
        body, name=name, grid=lead + (r // tr,), in_specs=[blk] * 4, out_specs=[blk] * 3, out_shape=[sh] * 3,
        compiler_params=_cp(("parallel",) * (nl + 1)),
    )(w, g, m, v)


_SMALL = ("norm_w", "conv_b", "dt_bias", "a_log", "d_skip", "ssm_norm_w", "sinks", "f_bias", "final_norm_w",
          "conv_w", "gate_bias")


def _pack(vals):
    flat = jnp.concatenate([v.reshape(-1) for v in vals])
    rows = -(-flat.shape[0] // LANES)
    rows = -(-rows // 8) * 8
    return jnp.pad(flat, (0, rows * LANES - flat.shape[0])).reshape(rows, LANES)


def _unpack(buf, shapes):
    flat = buf.reshape(-1)
    out, off = [], 0
    for sh in shapes:
        sz = int(np.prod(sh))
        out.append(flat[off:off + sz].reshape(sh))
        off += sz
    return out


def kernel(x, norm_w, w_in, conv_w, conv_b, dt_bias, a_log, d_skip, ssm_norm_w, sinks, f_bias, gate_bias, w_proj, w_out, final_norm_w, loss_target, m_norm_w, m_w_in, m_conv_w, m_conv_b, m_dt_bias, m_a_log, m_d_skip, m_ssm_norm_w, m_sinks, m_f_bias, m_gate_bias, m_w_proj, m_w_out, m_final_norm_w, v_norm_w, v_w_in, v_conv_w, v_conv_b, v_dt_bias, v_a_log, v_d_skip, v_ssm_norm_w, v_sinks, v_f_bias, v_gate_bias, v_w_proj, v_w_out, v_final_norm_w):
    depth = w_in.shape[0]
    chip = 2 * lax.axis_index("x") + lax.axis_index("y")

    own = [w_in.astype(BF16), w_proj.astype(BF16), w_out.astype(BF16), conv_w, gate_bias]
    gathered = _gather_weights(own, name="gather_weights")

    def whole(a, li, axis):
        return jnp.concatenate([jnp.where(chip == k, own[a][li], gathered[a][k, li]) for k in range(N_CHIPS)],
                               axis=axis)

    wls = []
    for li in range(depth):
        wls.append(dict(
            norm_w=norm_w[li], w_in=_pad_w_in(whole(0, li, 1)),
            conv_w=whole(3, li, 1), conv_b=conv_b[li], dt_bias=dt_bias[li], a_log=a_log[li], d_skip=d_skip[li],
            ssm_norm_w=ssm_norm_w[li], sinks=sinks[li], f_bias=f_bias[li], gate_bias=whole(4, li, 1),
            w_proj=whole(1, li, 1),
            w_out=whole(2, li, 0)))

    loss_part, grad_x, grads, d_final = _local_step(x, loss_target, wls, final_norm_w)
    loss = lax.psum(loss_part, ("x", "y", "c"))

    c_in = w_in.shape[2]
    r_proj = w_proj.shape[2]
    r_out = w_out.shape[1]
    full_in = jnp.stack([jnp.stack([grads[li]["w_in"][:, k * c_in:(k + 1) * c_in] for li in range(depth)])
                         for k in range(N_CHIPS)])
    full_proj = jnp.stack([jnp.stack([grads[li]["w_proj"][:, k * r_proj:(k + 1) * r_proj].reshape(-1, D_MODEL)
                                      for li in range(depth)]) for k in range(N_CHIPS)])
    full_out = jnp.stack([jnp.stack([grads[li]["w_out"][k * r_out:(k + 1) * r_out] for li in range(depth)])
                          for k in range(N_CHIPS)])
    fulls = [full_in, full_proj, full_out]
    others = _pair_exchange(fulls, name="grad_pair_exchange")
    pair = [_add_slot_layer(f, o, name=f"grad_pair_add{i}") for i, (f, o) in enumerate(zip(fulls, others))]
    parts = _chip_exchange([p[1] for p in pair], name="grad_chip_exchange")
    mine = [_sum_slots(p, pr[0], name=f"grad_slot_sum{i}") for i, (p, pr) in enumerate(zip(parts, pair))]
    theirs = _pair_share(mine, name="grad_pair_share")
    core = lax.axis_index("c")
    red_in, red_proj, red_out = [jnp.stack([jnp.where(core == li, m, t) for li in range(depth)])
                                 for m, t in zip(mine, theirs)]
    grad_w_in = red_in
    grad_w_proj = red_proj.reshape(w_proj.shape)
    grad_w_out = red_out

    small_full = {
        "norm_w": jnp.stack([g["norm_w"] for g in grads]), "conv_b": jnp.stack([g["conv_b"] for g in grads]),
        "dt_bias": jnp.stack([g["dt_bias"] for g in grads]), "a_log": jnp.stack([g["a_log"] for g in grads]),
        "d_skip": jnp.stack([g["d_skip"] for g in grads]),
        "ssm_norm_w": jnp.stack([g["ssm_norm_w"] for g in grads]),
        "sinks": jnp.stack([g["sinks"] for g in grads]), "f_bias": jnp.stack([g["f_bias"] for g in grads]),
        "final_norm_w": d_final,
        "conv_w": jnp.stack([g["conv_w"] for g in grads]), "gate_bias": jnp.stack([g["gate_bias"] for g in grads])}
    shapes = [small_full[k].shape for k in _SMALL]
    summed = _unpack(_allreduce_small(_pack([small_full[k] for k in _SMALL]), name="allreduce_small"), shapes)
    gsmall = dict(zip(_SMALL, summed))
    gsmall["conv_w"] = lax.dynamic_slice_in_dim(gsmall["conv_w"], chip * conv_w.shape[2], conv_w.shape[2], axis=2)
    gsmall["gate_bias"] = lax.dynamic_slice_in_dim(gsmall["gate_bias"], chip * gate_bias.shape[2],
                                                   gate_bias.shape[2], axis=2)

    w_small = dict(norm_w=norm_w, conv_b=conv_b, dt_bias=dt_bias, a_log=a_log, d_skip=d_skip,
                   ssm_norm_w=ssm_norm_w, sinks=sinks, f_bias=f_bias, final_norm_w=final_norm_w, conv_w=conv_w,
                   gate_bias=gate_bias)
    m_small = dict(norm_w=m_norm_w, conv_b=m_conv_b, dt_bias=m_dt_bias, a_log=m_a_log, d_skip=m_d_skip,
                   ssm_norm_w=m_ssm_norm_w, sinks=m_sinks, f_bias=m_f_bias, final_norm_w=m_final_norm_w,
                   conv_w=m_conv_w, gate_bias=m_gate_bias)
    v_small = dict(norm_w=v_norm_w, conv_b=v_conv_b, dt_bias=v_dt_bias, a_log=v_a_log, d_skip=v_d_skip,
                   ssm_norm_w=v_ssm_norm_w, sinks=v_sinks, f_bias=v_f_bias, final_norm_w=v_final_norm_w,
                   conv_w=v_conv_w, gate_bias=v_gate_bias)
    sshapes = [w_small[k].shape for k in _SMALL]
    ds, ms, vs = _adamw(_pack([w_small[k] for k in _SMALL]), _pack([gsmall[k] for k in _SMALL]),
                        _pack([m_small[k] for k in _SMALL]), _pack([v_small[k] for k in _SMALL]), name="adamw_small")
    delta = dict(zip(_SMALL, _unpack(ds, sshapes)))
    new_m = dict(zip(_SMALL, _unpack(ms, sshapes)))
    new_v = dict(zip(_SMALL, _unpack(vs, sshapes)))
    grad = dict(gsmall)
    for nm, w, g, m, v in (("w_in", w_in, grad_w_in, m_w_in, v_w_in),
                           ("w_proj", w_proj, grad_w_proj, m_w_proj, v_w_proj),
                           ("w_out", w_out, grad_w_out, m_w_out, v_w_out)):
        grad[nm] = g
        delta[nm], new_m[nm], new_v[nm] = _adamw(w, g, m, v, name=f"adamw_{nm}")

    order = ("norm_w", "w_in", "conv_w", "conv_b", "dt_bias", "a_log", "d_skip", "ssm_norm_w", "sinks", "f_bias",
             "gate_bias", "w_proj", "w_out", "final_norm_w")
    return (loss, grad_x, *[grad[k] for k in order], *[delta[k] for k in order],
            *[new_m[k] for k in order], *[new_v[k] for k in order])
```

```python
import functools
import math

import numpy as np
import jax
import jax.numpy as jnp
from jax import lax
from jax.experimental import pallas as pl
from jax.experimental.pallas import tpu as pltpu

F32 = jnp.float32
BF16 = jnp.bfloat16
HIGHEST = lax.Precision.HIGHEST
MESH = pl.DeviceIdType.MESH

D_MODEL = 1024
HEAD_DIM = 64
N_HEADS = 16
N_GROUPS = 4
SSM_STATE = 128
CHUNK = 128
CONV_WIDTH = 4
CONV_DIM = 2048
ROPE_THETA = 10000.0
NORM_EPS = 1e-6
LANES = 128
N_CHIPS = 4
N_DEV = 8

ADAM_LR = 0.001
ADAM_B1 = 0.9
ADAM_B2 = 0.999
ADAM_EPS = 1e-08
ADAM_WD = 0.01
ADAM_STEP = 10

_REF_COLS = {}
_off = 0
for _n, _s in (("xbc", 2048), ("a_z", 1024), ("a_dt", 16), ("b_q", 1024), ("b_k", 256), ("b_v", 256),
               ("b_z", 1024), ("c_q", 1024), ("c_k", 1024), ("c_v", 1024), ("c_f", 16), ("c_z", 1024),
               ("gates", 3072)):
    _REF_COLS[_n] = (_off, _s)
    _off += _s
N_IN = _off

_PAD_ORDER = (("gates", 3072), ("xbc", 2048), ("a_z", 1024), ("b_q", 1024), ("b_z", 1024), ("c_q", 1024),
              ("c_k", 1024), ("c_v", 1024), ("c_z", 1024), ("b_k", 256), ("b_v", 256), ("a_dt", 512),
              ("c_f", 128))
_PAD_COLS = {}
_off = 0
for _n, _s in _PAD_ORDER:
    _PAD_COLS[_n] = (_off, _s)
    _off += _s
N_USED = _off
N_PAD = 13824


def _cp(sem, vmem_mb=48):
    return pltpu.CompilerParams(dimension_semantics=sem, vmem_limit_bytes=vmem_mb * 1024 * 1024)


def _dot(a, b, dims=((1,), (0,)), precision=None):
    return lax.dot_general(a, b, (dims, ((), ())), preferred_element_type=F32, precision=precision)


def _dot_nt(a, b):
    return _dot(a, b, ((1,), (1,)))


def _dot_tn(a, b):
    return _dot(a, b, ((0,), (0,)))


def _col(v, idx):
    lane = lax.broadcasted_iota(jnp.int32, v.shape, 1)
    return jnp.sum(jnp.where(lane == idx, v, 0.0), axis=1, keepdims=True)


def _row(v, idx):
    row = lax.broadcasted_iota(jnp.int32, v.shape, 0)
    return jnp.sum(jnp.where(row == idx, v, 0.0), axis=0, keepdims=True)


def _iota_col():
    return lax.broadcasted_iota(jnp.int32, (CHUNK, 1), 0)


def _iota_row():
    return lax.broadcasted_iota(jnp.int32, (1, LANES), 1)


def _sigmoid(x):
    return 1.0 / (1.0 + jnp.exp(-x))


def _softplus(x):
    return jnp.maximum(x, 0.0) + jnp.log(1.0 + jnp.exp(-jnp.abs(x)))


def _pad_w_in(w):
    parts = []
    for name, size in _PAD_ORDER:
        s0, sz = _REF_COLS[name]
        seg = w[:, s0:s0 + sz]
        if name == "a_dt":
            seg = jnp.pad(seg.reshape(-1, N_GROUPS, 4), ((0, 0), (0, 0), (0, LANES - 4))).reshape(-1, 512)
        elif name == "c_f":
            seg = jnp.pad(seg, ((0, 0), (0, LANES - 16)))
        parts.append(seg)
    parts.append(jnp.zeros((w.shape[0], N_PAD - N_USED), w.dtype))
    return jnp.concatenate(parts, axis=1)


def _unpad_w_in(wp):
    segs = {}
    for name, _ in _PAD_ORDER:
        p0, psz = _PAD_COLS[name]
        seg = wp[:, p0:p0 + psz]
        if name == "a_dt":
            seg = seg.reshape(-1, N_GROUPS, LANES)[:, :, :4].reshape(-1, 16)
        elif name == "c_f":
            seg = seg[:, :16]
        segs[name] = seg
    order = sorted(_REF_COLS, key=lambda n: _REF_COLS[n][0])
    return jnp.concatenate([segs[n] for n in order], axis=1)


def _group_lanes(v):
    return jnp.pad(v.reshape(N_GROUPS, 1, 4), ((0, 0), (0, 0), (0, LANES - 4)))


def _ungroup_lanes(v):
    return v[:, 0, :4].reshape(16)


def _mm(a, b, *, ta=False, tb=False, tm=512, tn=512, tk=512, out_dtype=F32, name):
    if ta:
        kdim, m = a.shape
    else:
        m, kdim = a.shape
    if tb:
        n, k2 = b.shape
    else:
        k2, n = b.shape
    assert kdim == k2, (a.shape, b.shape)
    tm, tn, tk = min(tm, m), min(tn, n), min(tk, kdim)
    assert m % tm == 0 and n % tn == 0 and kdim % tk == 0, (m, n, kdim, tm, tn, tk)
    nk = kdim // tk
    a_spec = (pl.BlockSpec((tk, tm), lambda i, j, k: (k, i)) if ta
              else pl.BlockSpec((tm, tk), lambda i, j, k: (i, k)))
    b_spec = (pl.BlockSpec((tn, tk), lambda i, j, k: (j, k)) if tb
              else pl.BlockSpec((tk, tn), lambda i, j, k: (k, j)))
    dims = ((0 if ta else 1,), (1 if tb else 0,))

    def body(a_ref, b_ref, o_ref, acc_ref):
        k = pl.program_id(2)
        p = _dot(a_ref[...].astype(BF16), b_ref[...].astype(BF16), dims)

        @pl.when(k == 0)
        def _():
            acc_ref[...] = p

        @pl.when(k > 0)
        def _():
            acc_ref[...] += p

        @pl.when(k == nk - 1)
        def _():
            o_ref[...] = acc_ref[...].astype(out_dtype)

    return pl.pallas_call(
        body, name=name, grid=(m // tm, n // tn, nk),
        in_specs=[a_spec, b_spec], out_specs=pl.BlockSpec((tm, tn), lambda i, j, k: (i, j)),
        out_shape=jax.ShapeDtypeStruct((m, n), out_dtype),
        scratch_shapes=[pltpu.VMEM((tm, tn), F32)],
        compiler_params=_cp(("parallel", "parallel", "arbitrary")),
    )(a, b)


def _rms_fwd(x, w, *, name, tm=512):
    t, d = x.shape

    def body(x_ref, w_ref, o_ref, ot_ref):
        xv = x_ref[...]
        r = lax.rsqrt(jnp.mean(xv * xv, axis=1, keepdims=True) + NORM_EPS)
        h = xv * r * w_ref[...]
        o_ref[...] = h.astype(BF16)
        ot_ref[...] = h.T.astype(BF16)

    return pl.pallas_call(
        body, name=name, grid=(t // tm,),
        in_specs=[pl.BlockSpec((tm, d), lambda i: (i, 0)), pl.BlockSpec((1, d), lambda i: (0, 0))],
        out_specs=[pl.BlockSpec((tm, d), lambda i: (i, 0)), pl.BlockSpec((d, tm), lambda i: (0, i))],
        out_shape=[jax.ShapeDtypeStruct((t, d), BF16), jax.ShapeDtypeStruct((d, t), BF16)],
        compiler_params=_cp(("parallel",)),
    )(x, w.reshape(1, d))


def _rms_bwd(x, w, dh, dres, *, name, tm=512):
    t, d = x.shape

    def body(x_ref, w_ref, dh_ref, dres_ref, dx_ref, dw_ref):
        xv = x_ref[...]
        r = lax.rsqrt(jnp.mean(xv * xv, axis=1, keepdims=True) + NORM_EPS)
        xhat = xv * r
        dhv = dh_ref[...]
        dxhat = dhv * w_ref[...]
        dx = r * (dxhat - xhat * jnp.mean(dxhat * xhat, axis=1, keepdims=True))
        dx_ref[...] = dres_ref[...] + dx

        @pl.when(pl.program_id(0) == 0)
        def _():
            dw_ref[...] = jnp.zeros_like(dw_ref)

        dw_ref[...] += jnp.sum(dhv * xhat, axis=0, keepdims=True)

    return pl.pallas_call(
        body, name=name, grid=(t // tm,),
        in_specs=[pl.BlockSpec((tm, d), lambda i: (i, 0)), pl.BlockSpec((1, d), lambda i: (0, 0)),
                  pl.BlockSpec((tm, d), lambda i: (i, 0)), pl.BlockSpec((tm, d), lambda i: (i, 0))],
        out_specs=[pl.BlockSpec((tm, d), lambda i: (i, 0)), pl.BlockSpec((1, d), lambda i: (0, 0))],
        out_shape=[jax.ShapeDtypeStruct((t, d), F32), jax.ShapeDtypeStruct((1, d), F32)],
        compiler_params=_cp(("arbitrary",)),
    )(x, w.reshape(1, d), dh, dres)


def _final_loss(x, w, target, *, name, tm=512):
    t, d = x.shape

    def body(x_ref, w_ref, t_ref, loss_ref, dx_ref, dw_ref):
        xv = x_ref[...]
        wv = w_ref[...]
        r = lax.rsqrt(jnp.mean(xv * xv, axis=1, keepdims=True) + NORM_EPS)
        xhat = xv * r
        err = xhat * wv - t_ref[...]
        dy = err * (1.0 / d)
        dxhat = dy * wv
        dx_ref[...] = r * (dxhat - xhat * jnp.mean(dxhat * xhat, axis=1, keepdims=True))

        @pl.when(pl.program_id(0) == 0)
        def _():
            dw_ref[...] = jnp.zeros_like(dw_ref)
            loss_ref[...] = jnp.zeros_like(loss_ref)

        dw_ref[...] += jnp.sum(dy * xhat, axis=0, keepdims=True)
        part = 0.5 * jnp.sum(jnp.mean(err * err, axis=1, keepdims=True), axis=0, keepdims=True)
        loss_ref[...] += jnp.broadcast_to(part, loss_ref.shape)

    return pl.pallas_call(
        body, name=name, grid=(t // tm,),
        in_specs=[pl.BlockSpec((tm, d), lambda i: (i, 0)), pl.BlockSpec((1, d), lambda i: (0, 0)),
                  pl.BlockSpec((tm, d), lambda i: (i, 0))],
        out_specs=[pl.BlockSpec((8, LANES), lambda i: (0, 0)), pl.BlockSpec((tm, d), lambda i: (i, 0)),
                   pl.BlockSpec((1, d), lambda i: (0, 0))],
        out_shape=[jax.ShapeDtypeStruct((8, LANES), F32), jax.ShapeDtypeStruct((t, d), F32),
                   jax.ShapeDtypeStruct((1, d), F32)],
        compiler_params=_cp(("arbitrary",)),
    )(x, w.reshape(1, d), target)


_CB = 128


def _conv_pre(u, w_ref, b_ref):
    s = u.shape[0]
    row = lax.broadcasted_iota(jnp.int32, u.shape, 0)
    pre = b_ref[...] + w_ref[CONV_WIDTH - 1:CONV_WIDTH, :] * u
    for sh in range(1, CONV_WIDTH):
        shifted = jnp.where(row >= sh, pltpu.roll(u, sh, 0), 0.0)
        pre = pre + w_ref[CONV_WIDTH - 1 - sh:CONV_WIDTH - sh, :] * shifted
    return pre


def _conv_fwd(proj3, cw, cb, *, name):
    b, s, _ = proj3.shape
    c0 = _PAD_COLS["xbc"][0] // _CB

    def body(u_ref, w_ref, b_ref, o_ref):
        pre = _conv_pre(u_ref[...].astype(F32), w_ref, b_ref)
        o_ref[...] = pre * _sigmoid(pre)

    return pl.pallas_call(
        body, name=name, grid=(b, CONV_DIM // _CB),
        in_specs=[pl.BlockSpec((None, s, _CB), lambda i, j: (i, 0, c0 + j)),
                  pl.BlockSpec((CONV_WIDTH, _CB), lambda i, j: (0, j)),
                  pl.BlockSpec((1, _CB), lambda i, j: (0, j))],
        out_specs=pl.BlockSpec((None, s, _CB), lambda i, j: (i, 0, j)),
        out_shape=jax.ShapeDtypeStruct((b, s, CONV_DIM), F32),
        compiler_params=_cp(("parallel", "parallel")),
    )(proj3, cw, cb.reshape(1, CONV_DIM))


def _conv_bwd(proj3, cw, cb, dact, *, name):
    b, s, _ = proj3.shape
    c0 = _PAD_COLS["xbc"][0] // _CB

    def body(u_ref, w_ref, b_ref, da_ref, du_ref, dwb_ref):
        u = u_ref[...].astype(F32)
        pre = _conv_pre(u, w_ref, b_ref)
        sg = _sigmoid(pre)
        dpre = da_ref[...] * (sg * (1.0 + pre * (1.0 - sg)))
        row = lax.broadcasted_iota(jnp.int32, u.shape, 0)
        du = w_ref[CONV_WIDTH - 1:CONV_WIDTH, :] * dpre
        rows = [jnp.sum(dpre * u, axis=0, keepdims=True)]
        for sh in range(1, CONV_WIDTH):
            fwd_shift = jnp.where(row < s - sh, pltpu.roll(dpre, s - sh, 0), 0.0)
            du = du + w_ref[CONV_WIDTH - 1 - sh:CONV_WIDTH - sh, :] * fwd_shift
            ush = jnp.where(row >= sh, pltpu.roll(u, sh, 0), 0.0)
            rows.append(jnp.sum(dpre * ush, axis=0, keepdims=True))
        du_ref[...] = du.astype(BF16)

        @pl.when(pl.program_id(1) == 0)
        def _():
            dwb_ref[...] = jnp.zeros_like(dwb_ref)

        for sh in range(CONV_WIDTH):
            k = CONV_WIDTH - 1 - sh
            dwb_ref[k:k + 1, :] += rows[sh]
        dwb_ref[CONV_WIDTH:CONV_WIDTH + 1, :] += jnp.sum(dpre, axis=0, keepdims=True)

    return pl.pallas_call(
        body, name=name, grid=(CONV_DIM // _CB, b),
        in_specs=[pl.BlockSpec((None, s, _CB), lambda j, i: (i, 0, c0 + j)),
                  pl.BlockSpec((CONV_WIDTH, _CB), lambda j, i: (0, j)),
                  pl.BlockSpec((1, _CB), lambda j, i: (0, j)),
                  pl.BlockSpec((None, s, _CB), lambda j, i: (i, 0, j))],
        out_specs=[pl.BlockSpec((None, s, _CB), lambda j, i: (i, 0, j)),
                   pl.BlockSpec((8, _CB), lambda j, i: (0, j))],
        out_shape=[jax.ShapeDtypeStruct((b, s, CONV_DIM), BF16), jax.ShapeDtypeStruct((8, CONV_DIM), F32)],
        compiler_params=_cp(("parallel", "arbitrary")),
    )(proj3, cw, cb.reshape(1, CONV_DIM), dact)


def _ssd_common(dt_ref, dtb_ref, alog_ref):
    row = lax.broadcasted_iota(jnp.int32, (CHUNK, CHUNK), 0)
    lane = lax.broadcasted_iota(jnp.int32, (CHUNK, CHUNK), 1)
    causal = row >= lane
    tri = causal.astype(F32)
    dtv = _softplus(dt_ref[...] + dtb_ref[...])
    a_row = -jnp.exp(alog_ref[...])
    acum = _dot(tri, dtv * a_row, precision=HIGHEST)
    return row, lane, causal, dtv, a_row, acum, acum.T


def _ssd_pair(pp, x, dtv, acum, acum_t, causal, lane, row):
    lo = lane < HEAD_DIM
    r0, r1 = 2 * pp, 2 * pp + 1
    dtp = jnp.where(lo, _col(dtv, r0), _col(dtv, r1))
    ac0, ac1 = _col(acum, r0), _col(acum, r1)
    ar0, ar1 = _row(acum_t, r0), _row(acum_t, r1)
    d0 = jnp.where(causal, jnp.exp(jnp.where(causal, ac0 - ar0, 0.0)), 0.0)
    d1 = jnp.where(causal, jnp.exp(jnp.where(causal, ac1 - ar1, 0.0)), 0.0)
    al0, al1 = _col(ar0, CHUNK - 1), _col(ar1, CHUNK - 1)
    eac = jnp.where(lo, jnp.exp(ac0), jnp.exp(ac1))
    dsp = jnp.where(lo, jnp.exp(al0 - ac0), jnp.exp(al1 - ac1))
    eal = jnp.where(_iota_col() < HEAD_DIM, jnp.exp(al0), jnp.exp(al1))
    return lo, dtp, x * dtp, d0, d1, al0, al1, eac, dsp, eal


def _ssd_fwd(proj3, gates3, xact3, dtb, alog, dsk, nw, *, name):
    b, s, _ = proj3.shape
    nc = s // CHUNK
    dt0 = 0
    z0 = _PAD_COLS["a_z"][0] // D_MODEL

    def body(xs_ref, bm_ref, cm_ref, dt_ref, z_ref, dtb_ref, alog_ref, dsk_ref, nw_ref,
             ya_ref, ypre_ref, hst_ref, h_scr):
        @pl.when(pl.program_id(1) == 0)
        def _():
            h_scr[...] = jnp.zeros_like(h_scr)

        for g in range(N_GROUPS):
            w256 = pl.ds(256 * g, 256)
            w128 = pl.ds(LANES * g, LANES)
            group(xs_ref.at[:, w256], bm_ref.at[:, w128], cm_ref.at[:, w128], dt_ref.at[:, w128],
                  z_ref.at[:, w256], dtb_ref.at[g], alog_ref.at[g], dsk_ref.at[g], nw_ref.at[g],
                  ya_ref.at[:, w256], ypre_ref.at[:, w256], hst_ref.at[g], h_scr.at[g])

    def group(xs_ref, bm_ref, cm_ref, dt_ref, z_ref, dtb_ref, alog_ref, dsk_ref, nw_ref,
              ya_ref, ypre_ref, hst_ref, h_scr):
        row, lane, causal, dtv, a_row, acum, acum_t = _ssd_common(dt_ref, dtb_ref, alog_ref)
        bb = bm_ref[...].astype(BF16)
        cb = cm_ref[...].astype(BF16)
        cbm = _dot_nt(cb, bb)
        hst_ref[...] = h_scr[...]
        dskv = dsk_ref[...]
        for pp in range(2):
            x = xs_ref[:, LANES * pp:LANES * (pp + 1)]
            lo, dtp, xd, d0, d1, al0, al1, eac, dsp, eal = _ssd_pair(pp, x, dtv, acum, acum_t, causal, lane, row)
            xdb = xd.astype(BF16)
            y = jnp.where(lo, _dot((cbm * d0).astype(BF16), xdb), _dot((cbm * d1).astype(BF16), xdb))
            h = h_scr[pp]
            y = y + eac * _dot_nt(cb, h.astype(BF16))
            h_scr[pp] = h * eal + _dot_tn((xd * dsp).astype(BF16), bb)
            dskp = jnp.where((_iota_row() < HEAD_DIM), _col(dskv, 2 * pp), _col(dskv, 2 * pp + 1))
            ypre_ref[:, LANES * pp:LANES * (pp + 1)] = y + x * dskp
        ypre = ypre_ref[...]
        z = z_ref[...].astype(F32)
        yg = ypre * (z * _sigmoid(z))
        rstd = lax.rsqrt(jnp.sum(yg * yg, axis=1, keepdims=True) * (1.0 / 256.0) + NORM_EPS)
        ya_ref[...] = (yg * rstd * nw_ref[...]).astype(BF16)

    g = N_GROUPS
    par = pl.BlockSpec((g, 1, LANES), lambda i, c: (0, 0, 0))
    wide = pl.BlockSpec((None, CHUNK, D_MODEL), lambda i, c: (i, c, 0))
    return pl.pallas_call(
        body, name=name, grid=(b, nc),
        in_specs=[wide,
                  pl.BlockSpec((None, CHUNK, 512), lambda i, c: (i, c, 2)),
                  pl.BlockSpec((None, CHUNK, 512), lambda i, c: (i, c, 3)),
                  pl.BlockSpec((None, CHUNK, 512), lambda i, c: (i, c, dt0)),
                  pl.BlockSpec((None, CHUNK, D_MODEL), lambda i, c: (i, c, z0)),
                  par, par, par,
                  pl.BlockSpec((g, 1, 256), lambda i, c: (0, 0, 0))],
        out_specs=[wide, wide,
                   pl.BlockSpec((None, None, g, 2, CHUNK, SSM_STATE), lambda i, c: (i, c, 0, 0, 0, 0))],
        out_shape=[jax.ShapeDtypeStruct((b, s, D_MODEL), BF16), jax.ShapeDtypeStruct((b, s, D_MODEL), F32),
                   jax.ShapeDtypeStruct((b, nc, g, 2, CHUNK, SSM_STATE), F32)],
        scratch_shapes=[pltpu.VMEM((g, 2, CHUNK, SSM_STATE), F32)],
        compiler_params=_cp(("parallel", "arbitrary")),
    )(xact3, xact3, xact3, gates3, proj3, dtb, alog, dsk, nw)


def _ssd_bwd(proj3, gates3, xact3, dtb, alog, dsk, nw, ypre3, hst, dya3, *, name):
    b, s, _ = proj3.shape
    nc = s // CHUNK
    dt0 = 0
    z0 = _PAD_COLS["a_z"][0] // D_MODEL

    def body(xs_ref, bm_ref, cm_ref, dt_ref, z_ref, dtb_ref, alog_ref, dsk_ref, nw_ref, ypre_ref, hst_ref,
             dya_ref, dact_ref, dz_ref, ddt_ref, ddtb_ref, dalog_ref, ddsk_ref, dnw_ref, dh_scr):
        first = jnp.logical_and(pl.program_id(0) == 0, pl.program_id(1) == 0)

        @pl.when(first)
        def _():
            ddtb_ref[...] = jnp.zeros_like(ddtb_ref)
            dalog_ref[...] = jnp.zeros_like(dalog_ref)
            ddsk_ref[...] = jnp.zeros_like(ddsk_ref)
            dnw_ref[...] = jnp.zeros_like(dnw_ref)

        @pl.when(pl.program_id(1) == 0)
        def _():
            dh_scr[...] = jnp.zeros_like(dh_scr)

        for g in range(N_GROUPS):
            w256 = pl.ds(256 * g, 256)
            w128 = pl.ds(LANES * g, LANES)
            group(xs_ref.at[:, w256], bm_ref.at[:, w128], cm_ref.at[:, w128], dt_ref.at[:, w128],
                  z_ref.at[:, w256], dtb_ref.at[g], alog_ref.at[g], dsk_ref.at[g], nw_ref.at[g],
                  ypre_ref.at[:, w256], hst_ref.at[g], dya_ref.at[:, w256],
                  dact_ref.at[:, w256], dact_ref.at[:, pl.ds(D_MODEL + LANES * g, LANES)],
                  dact_ref.at[:, pl.ds(D_MODEL + 512 + LANES * g, LANES)], dz_ref.at[:, w256], ddt_ref.at[:, w128],
                  ddtb_ref.at[g], dalog_ref.at[g], ddsk_ref.at[g], dnw_ref.at[g], dh_scr.at[g])

    def group(xs_ref, bm_ref, cm_ref, dt_ref, z_ref, dtb_ref, alog_ref, dsk_ref, nw_ref, ypre_ref, hst_ref,
              dya_ref, dxs_ref, dbm_ref, dcm_ref, dz_ref, ddt_ref, ddtb_ref, dalog_ref, ddsk_ref, dnw_ref,
              dh_scr):
        row, lane, causal, dtv, a_row, acum, acum_t = _ssd_common(dt_ref, dtb_ref, alog_ref)
        lane1 = _iota_row()
        bb = bm_ref[...].astype(BF16)
        cb = cm_ref[...].astype(BF16)
        cbm = _dot_nt(cb, bb)

        z = z_ref[...].astype(F32)
        ypre = ypre_ref[...]
        dya = dya_ref[...]
        sz = _sigmoid(z)
        silu = z * sz
        yg = ypre * silu
        rstd = lax.rsqrt(jnp.sum(yg * yg, axis=1, keepdims=True) * (1.0 / 256.0) + NORM_EPS)
        dnw_ref[...] += jnp.sum(dya * yg * rstd, axis=0, keepdims=True)
        dn = dya * nw_ref[...]
        dyg = rstd * dn - yg * (rstd * rstd * rstd * (1.0 / 256.0)) * jnp.sum(dn * yg, axis=1, keepdims=True)
        dz_ref[...] = (dyg * ypre * (sz * (1.0 + z * (1.0 - sz)))).astype(BF16)
        dy_all = dyg * silu

        dskv = dsk_ref[...]
        da_cols = jnp.zeros((CHUNK, LANES), F32)
        dxt_cols = jnp.zeros((CHUNK, LANES), F32)
        ddsk_row = jnp.zeros((1, LANES), F32)
        dcb = jnp.zeros((CHUNK, CHUNK), F32)
        dc = jnp.zeros((CHUNK, SSM_STATE), F32)
        db = jnp.zeros((CHUNK, SSM_STATE), F32)
        last = _iota_col() == CHUNK - 1
        for pp in range(2):
            r0, r1 = 2 * pp, 2 * pp + 1
            x = xs_ref[:, LANES * pp:LANES * (pp + 1)]
            dy = dy_all[:, LANES * pp:LANES * (pp + 1)]
            lo, dtp, xd, d0, d1, al0, al1, eac, dsp, eal = _ssd_pair(pp, x, dtv, acum, acum_t, causal, lane, row)
            w0, w1 = cbm * d0, cbm * d1
            w0b, w1b = w0.astype(BF16), w1.astype(BF16)
            xdb = xd.astype(BF16)
            dyb = dy.astype(BF16)
            h = hst_ref[pp]
            dhn = dh_scr[pp]
            hb = h.astype(BF16)
            dhb = dhn.astype(BF16)
            g0 = _dot_nt(jnp.where(lo, dy, 0.0).astype(BF16), xdb)
            g1 = _dot_nt(jnp.where(lo, 0.0, dy).astype(BF16), xdb)
            dcb = dcb + g0 * d0 + g1 * d1
            m0, m1 = g0 * w0, g1 * w1
            bdh = _dot_nt(bb, dhb)
            dxd = jnp.where(lo, _dot_tn(w0b, dyb), _dot_tn(w1b, dyb)) + dsp * bdh
            ch = _dot_nt(cb, hb)
            edy = eac * dy
            edyb = edy.astype(BF16)
            xds = xd * dsp
            dc = dc + _dot(edyb, hb)
            db = db + _dot(xds.astype(BF16), dhb)
            dh_scr[pp] = dhn * eal + _dot_tn(edyb, cb)
            t2 = edy * ch
            t3 = xds * bdh
            dhh = dhn * h
            s4_0 = jnp.sum(jnp.sum(jnp.where(row < HEAD_DIM, dhh, 0.0), axis=0, keepdims=True), axis=1, keepdims=True)
            s4_1 = jnp.sum(jnp.sum(dhh, axis=0, keepdims=True), axis=1, keepdims=True) - s4_0
            t23 = t2 - t3
            t23_0 = jnp.sum(jnp.where(lo, t23, 0.0), axis=1, keepdims=True)
            t23_1 = jnp.sum(t23, axis=1, keepdims=True) - t23_0
            c3 = jnp.sum(t3, axis=0, keepdims=True)
            c3_0 = jnp.sum(jnp.where(_iota_row() < HEAD_DIM, c3, 0.0), axis=1, keepdims=True)
            c3_1 = jnp.sum(c3, axis=1, keepdims=True) - c3_0
            dal0 = c3_0 + jnp.exp(al0) * s4_0
            dal1 = c3_1 + jnp.exp(al1) * s4_1
            dac0 = jnp.sum(m0 - m0.T, axis=1, keepdims=True) + t23_0 + jnp.where(last, dal0, 0.0)
            dac1 = jnp.sum(m1 - m1.T, axis=1, keepdims=True) + t23_1 + jnp.where(last, dal1, 0.0)
            da_cols = da_cols + jnp.where(lane == r0, dac0, 0.0) + jnp.where(lane == r1, dac1, 0.0)
            xx = dxd * x
            x0 = jnp.sum(jnp.where(lo, xx, 0.0), axis=1, keepdims=True)
            x1 = jnp.sum(xx, axis=1, keepdims=True) - x0
            dxt_cols = dxt_cols + jnp.where(lane == r0, x0, 0.0) + jnp.where(lane == r1, x1, 0.0)
            dskp = jnp.where((_iota_row() < HEAD_DIM), _col(dskv, r0), _col(dskv, r1))
            dxs_ref[:, LANES * pp:LANES * (pp + 1)] = dxd * dtp + dy * dskp
            yx = jnp.sum(dy * x, axis=0, keepdims=True)
            k0 = jnp.sum(jnp.where((_iota_row() < HEAD_DIM), yx, 0.0), axis=1, keepdims=True)
            k1 = jnp.sum(yx, axis=1, keepdims=True) - k0
            ddsk_row = ddsk_row + jnp.where(lane1 == r0, k0, 0.0) + jnp.where(lane1 == r1, k1, 0.0)
        dcbb = dcb.astype(BF16)
        dcm_ref[...] = dc + _dot(dcbb, bb)
        dbm_ref[...] = db + _dot_tn(dcbb, cb)
        tri_t = (row <= lane).astype(F32)
        dadt = _dot(tri_t, da_cols, precision=HIGHEST)
        ddtv = dadt * a_row + dxt_cols
        dalog_ref[...] += jnp.sum(dadt * dtv, axis=0, keepdims=True) * a_row
        ddt_raw = ddtv * _sigmoid(dt_ref[...] + dtb_ref[...])
        ddt_ref[...] = ddt_raw.astype(BF16)
        ddtb_ref[...] += jnp.sum(ddt_raw, axis=0, keepdims=True)
        ddsk_ref[...] += ddsk_row

    g = N_GROUPS
    rc = lambda c: nc - 1 - c
    par = pl.BlockSpec((g, 1, LANES), lambda i, c: (0, 0, 0))
    parw = pl.BlockSpec((g, 1, 256), lambda i, c: (0, 0, 0))
    wide = pl.BlockSpec((None, CHUNK, D_MODEL), lambda i, c: (i, rc(c), 0))
    blk512 = lambda col: pl.BlockSpec((None, CHUNK, 512), lambda i, c: (i, rc(c), col))
    return pl.pallas_call(
        body, name=name, grid=(b, nc),
        in_specs=[wide, blk512(2), blk512(3), blk512(dt0),
                  pl.BlockSpec((None, CHUNK, D_MODEL), lambda i, c: (i, rc(c), z0)),
                  par, par, par, parw,
                  wide,
                  pl.BlockSpec((None, None, g, 2, CHUNK, SSM_STATE), lambda i, c: (i, rc(c), 0, 0, 0, 0)),
                  wide],
        out_specs=[pl.BlockSpec((None, CHUNK, CONV_DIM), lambda i, c: (i, rc(c), 0)), wide, blk512(0),
                   par, par, par, parw],
        out_shape=[jax.ShapeDtypeStruct((b, s, CONV_DIM), F32), jax.ShapeDtypeStruct((b, s, D_MODEL), BF16),
                   jax.ShapeDtypeStruct((b, s, 512), BF16),
                   jax.ShapeDtypeStruct((g, 1, LANES), F32), jax.ShapeDtypeStruct((g, 1, LANES), F32),
                   jax.ShapeDtypeStruct((g, 1, LANES), F32), jax.ShapeDtypeStruct((g, 1, 256), F32)],
        scratch_shapes=[pltpu.VMEM((g, 2, CHUNK, SSM_STATE), F32)],
        compiler_params=_cp(("arbitrary", "arbitrary")),
    )(xact3, xact3, xact3, gates3, proj3, dtb, alog, dsk, nw, ypre3, hst, dya3)


_FGATE_ROWS = 512


def _fgate_fwd(gates3, fb, *, name):
    b, s, _ = gates3.shape
    rows = min(_FGATE_ROWS, s)
    f0 = _PAD_COLS["a_dt"][1] // LANES

    def body(f_ref, fb_ref, cum_ref, carry):
        @pl.when(pl.program_id(1) == 0)
        def _():
            carry[...] = jnp.zeros_like(carry)

        row = lax.broadcasted_iota(jnp.int32, (rows, rows), 0)
        lane = lax.broadcasted_iota(jnp.int32, (rows, rows), 1)
        tri = (row >= lane).astype(F32)
        lf = -_softplus(-(f_ref[...] + fb_ref[...]))
        cs = _dot(tri, lf, precision=HIGHEST) + carry[0:1, :]
        cum_ref[...] = cs
        carry[0:1, :] = _row(cs, rows - 1)

    return pl.pallas_call(
        body, name=name, grid=(b, s // rows),
        in_specs=[pl.BlockSpec((None, rows, LANES), lambda i, c: (i, c, f0)),
                  pl.BlockSpec((1, LANES), lambda i, c: (0, 0))],
        out_specs=pl.BlockSpec((None, rows, LANES), lambda i, c: (i, c, 0)),
        out_shape=jax.ShapeDtypeStruct((b, s, LANES), F32),
        scratch_shapes=[pltpu.VMEM((8, LANES), F32)],
        compiler_params=_cp(("parallel", "arbitrary")),
    )(gates3, fb)


def _fgate_bwd(gates3, fb, dcum, *, name):
    b, s, _ = gates3.shape
    rows = min(_FGATE_ROWS, s)
    nc = s // rows
    f0 = _PAD_COLS["a_dt"][1] // LANES
    npair = dcum.shape[1]

    def body(f_ref, fb_ref, dc_ref, df_ref, dfb_ref, carry):
        first = jnp.logical_and(pl.program_id(0) == 0, pl.program_id(1) == 0)

        @pl.when(first)
        def _():
            dfb_ref[...] = jnp.zeros_like(dfb_ref)

        @pl.when(pl.program_id(1) == 0)
        def _():
            carry[...] = jnp.zeros_like(carry)

        row = lax.broadcasted_iota(jnp.int32, (rows, rows), 0)
        lane = lax.broadcasted_iota(jnp.int32, (rows, rows), 1)
        tri_t = (row <= lane).astype(F32)
        dc = -jnp.sum(dc_ref[...], axis=0)
        dlf = _dot(tri_t, dc, precision=HIGHEST) + carry[0:1, :]
        carry[0:1, :] = _row(dlf, 0)
        df = dlf * _sigmoid(-(f_ref[...] + fb_ref[...]))
        df_ref[...] = df.astype(BF16)
        dfb_ref[...] += jnp.sum(df, axis=0, keepdims=True)

    return pl.pallas_call(
        body, name=name, grid=(b, nc),
        in_specs=[pl.BlockSpec((None, rows, LANES), lambda i, c: (i, nc - 1 - c, f0)),
                  pl.BlockSpec((1, LANES), lambda i, c: (0, 0)),
                  pl.BlockSpec((None, npair, rows, LANES), lambda i, c: (i, 0, nc - 1 - c, 0))],
        out_specs=[pl.BlockSpec((None, rows, LANES), lambda i, c: (i, nc - 1 - c, 0)),
                   pl.BlockSpec((1, LANES), lambda i, c: (0, 0))],
        out_shape=[jax.ShapeDtypeStruct((b, s, LANES), BF16), jax.ShapeDtypeStruct((1, LANES), F32)],
        scratch_shapes=[pltpu.VMEM((8, LANES), F32)],
        compiler_params=_cp(("arbitrary", "arbitrary")),
    )(gates3, fb, dcum)


_SCALE = HEAD_DIM ** -0.5
_NEG = -1e30


_ST_LSE, _ST_DELTA, _ST_MJ = 0, 2, 8


_SR = 40


def _ck_rep(cum):
    b, s, _ = cum.shape
    t = jnp.transpose(cum[:, :, :N_HEADS], (0, 2, 1)).reshape(b, N_HEADS // 2, 2, s, 1)
    return jnp.broadcast_to(t, (b, N_HEADS // 2, 2, s, LANES))


def _foxt_fwd(proj3, ckrep, *, name, tb):
    b, s, _ = proj3.shape
    nq = s // tb
    assert _ST_MJ + 2 * nq <= _SR
    q0 = _PAD_COLS["c_q"][0] // LANES
    k0 = _PAD_COLS["c_k"][0] // LANES
    v0 = _PAD_COLS["c_v"][0] // LANES
    z0 = _PAD_COLS["c_z"][0] // LANES
    rep = tb // LANES

    def body(q_ref, k_ref, v_ref, z_ref, ck_ref, y_ref, o_ref, st_ref):
        i = pl.program_id(2)
        lane = lax.broadcasted_iota(jnp.int32, (tb, LANES), 1)
        lo = lane < HEAD_DIM
        lo_r = lax.broadcasted_iota(jnp.int32, (LANES, tb), 0) < HEAD_DIM
        srow = lax.broadcasted_iota(jnp.int32, (_SR, tb), 0)
        q = q_ref[...].astype(F32) * _SCALE
        qms = (jnp.where(lo, q, 0.0).astype(BF16), jnp.where(lo, 0.0, q).astype(BF16))
        ones_at = (HEAD_DIM, 0)

        def block(j, carry, diagonal):
            ks = pl.ds(pl.multiple_of(j * tb, tb), tb)
            kb = k_ref[ks, :].astype(BF16)
            v = v_ref[ks, :].astype(F32)
            vts = (jnp.where(lo, v, jnp.where(lane == ones_at[0], 1.0, 0.0)).T.astype(BF16),
                   jnp.where(lo, jnp.where(lane == ones_at[1], 1.0, 0.0), v).T.astype(BF16))
            if diagonal:
                key = lax.broadcasted_iota(jnp.int32, (tb, tb), 0)
                qry = lax.broadcasted_iota(jnp.int32, (tb, tb), 1)
                mask = key <= qry
            ms, ls, acc, st = carry
            new_m, new_l, pvs, alphas = [], [], [], []
            for hh in range(2):
                sc = _dot_nt(kb, qms[hh]) - jnp.tile(ck_ref[hh, ks, :], (1, rep))
                if diagonal:
                    sc = jnp.where(mask, sc, _NEG)
                m_new = jnp.maximum(ms[hh], jnp.max(sc, axis=0, keepdims=True))
                alpha = jnp.exp(ms[hh] - m_new)
                pv = _dot(vts[hh], jnp.exp(sc - m_new).astype(BF16))
                rs = _row(pv[ones_at[hh]:ones_at[hh] + 8, :], 0)
                new_l.append(alpha * ls[hh] + rs)
                new_m.append(m_new)
                pvs.append(pv)
                alphas.append(alpha)
                st = jnp.where(srow == _ST_MJ + 2 * j + hh, m_new, st)
            acc = jnp.where(lo_r, alphas[0] * acc + pvs[0], alphas[1] * acc + pvs[1])
            return (tuple(new_m), tuple(new_l), acc, st)

        neg = jnp.full((1, tb), _NEG, F32)
        zero = jnp.zeros((1, tb), F32)
        init = ((neg, neg), (zero, zero), jnp.zeros((LANES, tb), F32), jnp.zeros((_SR, tb), F32))
        carry = lax.fori_loop(0, i, lambda j, c: block(j, c, False), init)
        ms, ls, acc, st = block(i, carry, True)
        o = (acc / jnp.where(lo_r, ls[0], ls[1])).T
        o_ref[...] = o
        st = jnp.where(srow == _ST_LSE, ms[0] + jnp.log(ls[0]), st)
        st_ref[...] = jnp.where(srow == _ST_LSE + 1, ms[1] + jnp.log(ls[1]), st)
        z = z_ref[...].astype(F32)
        y_ref[...] = (o * (z * _sigmoid(z))).astype(BF16)

    qspec = lambda c0: pl.BlockSpec((None, tb, LANES), lambda bi, p, i: (bi, i, c0 + p))
    kspec = lambda c0: pl.BlockSpec((None, s, LANES), lambda bi, p, i: (bi, 0, c0 + p))
    ospec = pl.BlockSpec((None, tb, LANES), lambda bi, p, i: (bi, i, p))
    return pl.pallas_call(
        body, name=name, grid=(b, N_HEADS // 2, nq),
        in_specs=[qspec(q0), kspec(k0), kspec(v0), qspec(z0),
                  pl.BlockSpec((None, None, 2, s, LANES), lambda bi, p, i: (bi, p, 0, 0, 0))],
        out_specs=[ospec, ospec, pl.BlockSpec((None, None, None, _SR, tb), lambda bi, p, i: (bi, p, i, 0, 0))],
        out_shape=[jax.ShapeDtypeStruct((b, s, D_MODEL), BF16), jax.ShapeDtypeStruct((b, s, D_MODEL), F32),
                   jax.ShapeDtypeStruct((b, N_HEADS // 2, nq, _SR, tb), F32)],
        compiler_params=_cp(("parallel", "parallel", "arbitrary")),
    )(proj3, proj3, proj3, proj3, ckrep)


def _foxt_prep(proj3, o3, stat, dy3, *, name, tb):
    b, s, _ = proj3.shape
    nq = s // tb
    z0 = _PAD_COLS["c_z"][0] // LANES

    def body(z_ref, o_ref, fst_ref, dy_ref, dz_ref, do_ref, st_ref):
        z = z_ref[...].astype(F32)
        sz = _sigmoid(z)
        dy = dy_ref[...]
        o = o_ref[...]
        do = dy * (z * sz)
        dz_ref[...] = (dy * o * (sz * (1.0 + z * (1.0 - sz)))).astype(BF16)
        do_ref[...] = do
        doo = do.astype(BF16).astype(F32) * o
        r8 = lax.broadcasted_iota(jnp.int32, (8, LANES), 0)
        l8 = lax.broadcasted_iota(jnp.int32, (8, LANES), 1)
        pick = jnp.logical_or(jnp.logical_and(r8 == 0, l8 < HEAD_DIM),
                              jnp.logical_and(r8 == 1, l8 >= HEAD_DIM)).astype(F32)
        d8 = _dot(pick, doo, ((1,), (1,)), precision=HIGHEST)
        srow = lax.broadcasted_iota(jnp.int32, (_SR, tb), 0)
        st = jnp.where(srow == _ST_DELTA, _row(d8, 0), fst_ref[...])
        st_ref[...] = jnp.where(srow == _ST_DELTA + 1, _row(d8, 1), st)

    ospec = pl.BlockSpec((None, tb, LANES), lambda bi, p, i: (bi, i, p))
    sspec = pl.BlockSpec((None, None, None, _SR, tb), lambda bi, p, i: (bi, p, i, 0, 0))
    return pl.pallas_call(
        body, name=name, grid=(b, N_HEADS // 2, nq),
        in_specs=[pl.BlockSpec((None, tb, LANES), lambda bi, p, i: (bi, i, z0 + p)), ospec, sspec, ospec],
        out_specs=[ospec, ospec, sspec],
        out_shape=[jax.ShapeDtypeStruct((b, s, D_MODEL), BF16), jax.ShapeDtypeStruct((b, s, D_MODEL), F32),
                   jax.ShapeDtypeStruct((b, N_HEADS // 2, nq, _SR, tb), F32)],
        compiler_params=_cp(("parallel", "parallel", "parallel")),
    )(proj3, o3, stat, dy3)


def _foxt_bwd(proj3, ckrep, do3, stats, *, name, tb):
    b, s, _ = proj3.shape
    nq = s // tb
    q0 = _PAD_COLS["c_q"][0] // LANES
    k0 = _PAD_COLS["c_k"][0] // LANES
    v0 = _PAD_COLS["c_v"][0] // LANES
    rep = tb // LANES

    def body(q_ref, do_ref, st_ref, k_ref, v_ref, ck_ref, dq_ref, dk_ref, dv_ref, cs_ref):
        j = pl.program_id(2)
        lane = lax.broadcasted_iota(jnp.int32, (tb, LANES), 1)
        lo = lane < HEAD_DIM
        lo_r = lax.broadcasted_iota(jnp.int32, (LANES, tb), 0) < HEAD_DIM

        @pl.when(j == 0)
        def _():
            dq_ref[...] = jnp.zeros_like(dq_ref)

        kf = k_ref[...].astype(F32)
        kb = kf.astype(BF16)
        kt = kf.T.astype(BF16)
        vb = v_ref[...].astype(BF16)
        cks = (jnp.tile(ck_ref[0], (1, rep)), jnp.tile(ck_ref[1], (1, rep)))

        def block(i, carry, diagonal):
            qs = pl.ds(pl.multiple_of(i * tb, tb), tb)
            q = q_ref[qs, :].astype(F32) * _SCALE
            do = do_ref[qs, :]
            st = st_ref[i]
            if diagonal:
                key = lax.broadcasted_iota(jnp.int32, (tb, tb), 0)
                qry = lax.broadcasted_iota(jnp.int32, (tb, tb), 1)
                mask = key <= qry
            dk, dv, cs = carry
            new_cs, dqs = [], []
            for hh in range(2):
                sel = lo if hh == 0 else jnp.logical_not(lo)
                qm = jnp.where(sel, q, 0.0).astype(BF16)
                dom = jnp.where(sel, do, 0.0).astype(BF16)
                sc = _dot_nt(kb, qm) - cks[hh]
                if diagonal:
                    sc = jnp.where(mask, sc, _NEG)
                mj = _row(st, _ST_MJ + 2 * j + hh)
                w = jnp.exp(mj - _row(st, _ST_LSE + hh))
                ph = jnp.exp(sc - mj).astype(BF16).astype(F32) * w
                ds = ph * (_dot_nt(vb, dom) - _row(st, _ST_DELTA + hh))
                dsb = ds.astype(BF16)
                dv = dv + _dot(ph.astype(BF16), dom)
                dk = dk + _dot(dsb, qm)
                new_cs.append(cs[hh] + jnp.sum(ds, axis=1, keepdims=True))
                dqs.append(_dot(kt, dsb))
            dq_ref[i] += jnp.where(lo_r, dqs[0], dqs[1]) * _SCALE
            return (dk, dv, tuple(new_cs))

        zcol = jnp.zeros((tb, 1), F32)
        init = (jnp.zeros((tb, LANES), F32), jnp.zeros((tb, LANES), F32), (zcol, zcol))
        carry = block(j, init, True)
        dk, dv, cs = lax.fori_loop(j + 1, nq, lambda i, c: block(i, c, False), carry)
        dk_ref[...] = dk.astype(BF16)
        dv_ref[...] = dv.astype(BF16)
        p2 = 2 * pl.program_id(1)
        cs_ref[...] = jnp.where(lane == p2, cs[0], jnp.where(lane == p2 + 1, cs[1], 0.0))

    full = lambda c0: pl.BlockSpec((None, s, LANES), lambda bi, p, j: (bi, 0, c0 + p))
    kspec = lambda c0: pl.BlockSpec((None, tb, LANES), lambda bi, p, j: (bi, j, c0 + p))
    ko = pl.BlockSpec((None, tb, LANES), lambda bi, p, j: (bi, j, p))
    sall = pl.BlockSpec((None, None, nq, _SR, tb), lambda bi, p, j: (bi, p, 0, 0, 0))
    dqspec = pl.BlockSpec((None, None, nq, LANES, tb), lambda bi, p, j: (bi, p, 0, 0, 0))
    return pl.pallas_call(
        body, name=name, grid=(b, N_HEADS // 2, nq),
        in_specs=[full(q0), full(0), sall, kspec(k0), kspec(v0),
                  pl.BlockSpec((None, None, 2, tb, LANES), lambda bi, p, j: (bi, p, 0, j, 0))],
        out_specs=[dqspec, ko, ko, pl.BlockSpec((None, None, tb, LANES), lambda bi, p, j: (bi, p, j, 0))],
        out_shape=[jax.ShapeDtypeStruct((b, N_HEADS // 2, nq, LANES, tb), F32),
                   jax.ShapeDtypeStruct((b, s, D_MODEL), BF16), jax.ShapeDtypeStruct((b, s, D_MODEL), BF16),
                   jax.ShapeDtypeStruct((b, N_HEADS // 2, s, LANES), F32)],
        compiler_params=_cp(("parallel", "parallel", "arbitrary")),
    )(proj3, do3, stats, proj3, proj3, ckrep)


def _rope(x, cos, sin_signed):
    w = x.shape[1]
    lane = lax.broadcasted_iota(jnp.int32, x.shape, 1)
    first = (lane % HEAD_DIM) < (HEAD_DIM // 2)
    rot = jnp.where(first, pltpu.roll(x, w - HEAD_DIM // 2, 1), pltpu.roll(x, HEAD_DIM // 2, 1))
    return x * cos + rot * sin_signed


_QB = 4
_QROWS = _QB * CHUNK


def _swa_keys(g, kc_ref, kp_ref, vc_ref, vp_ref, cq_ref, sq_ref, cp_ref, sp_ref):
    def both_halves(x):
        x = x.astype(F32)
        lane = lax.broadcasted_iota(jnp.int32, x.shape, 1)
        keep = (lane // HEAD_DIM) == (g % 2)
        return jnp.where(keep, x, pltpu.roll(x, HEAD_DIM, 1))

    cq, sq, cpv, spv = cq_ref[...], sq_ref[...], cp_ref[...], sp_ref[...]
    kc = _rope(both_halves(kc_ref[...]), cq, sq).astype(BF16)
    kp = _rope(both_halves(kp_ref[...]), cpv, spv).astype(BF16)
    return cq, sq, cpv, spv, kc, kp, both_halves(vc_ref[...]).astype(BF16), both_halves(vp_ref[...]).astype(BF16)


def _swa_stack(pairs, lo):
    return jnp.concatenate([jnp.where(lo, pairs[0], 0.0), jnp.where(lo, 0.0, pairs[0]),
                            jnp.where(lo, pairs[1], 0.0), jnp.where(lo, 0.0, pairs[1])], axis=0).astype(BF16)


def _swa_mask4(prev_valid):
    r = lax.broadcasted_iota(jnp.int32, (4 * CHUNK, 2 * CHUNK), 0) & (CHUNK - 1)
    c = lax.broadcasted_iota(jnp.int32, (4 * CHUNK, 2 * CHUNK), 1)
    own = jnp.logical_and(c >= CHUNK, c - CHUNK <= r)
    before = jnp.logical_and(c < CHUNK, c > r)
    if prev_valid is True:
        return jnp.logical_or(own, before)
    return jnp.logical_or(own, jnp.logical_and(before, prev_valid))


def _swa_sink4(skv):
    return jnp.concatenate([jnp.broadcast_to(_col(skv, j), (CHUNK, 1)) for j in range(4)], axis=0)


def _swa_specs(order):
    def spec(shape, fn):
        return pl.BlockSpec(shape, lambda *ids: fn(*order(*ids)))

    q0 = _PAD_COLS["b_q"][0] // 256
    z0 = _PAD_COLS["b_z"][0] // 256
    k0 = _PAD_COLS["b_k"][0] // LANES
    v0 = _PAD_COLS["b_v"][0] // LANES
    prev = lambda i: jnp.maximum(_QB * i - 1, 0)
    return dict(
        kc=spec((None, _QROWS, LANES), lambda bi, g, i: (bi, i, k0 + g // 2)),
        kp=spec((None, CHUNK, LANES), lambda bi, g, i: (bi, prev(i), k0 + g // 2)),
        vc=spec((None, _QROWS, LANES), lambda bi, g, i: (bi, i, v0 + g // 2)),
        vp=spec((None, CHUNK, LANES), lambda bi, g, i: (bi, prev(i), v0 + g // 2)),
        q=spec((None, _QROWS, 256), lambda bi, g, i: (bi, i, q0 + g)),
        z=spec((None, _QROWS, 256), lambda bi, g, i: (bi, i, z0 + g)),
        blk=spec((None, _QROWS, 256), lambda bi, g, i: (bi, i, g)),
        kcur=spec((None, _QROWS, LANES), lambda bi, g, i: (bi, i, g)),
        kstep=spec((None, CHUNK, LANES), lambda bi, g, i: (bi, i, g)),
        tcur=spec((_QROWS, LANES), lambda bi, g, i: (i, 0)),
        tprev=spec((CHUNK, LANES), lambda bi, g, i: (prev(i), 0)),
        sk=spec((None, 1, LANES), lambda bi, g, i: (g, 0, 0)))


def _swa_fwd(proj3, cos, sin, sinks, *, name):
    b, s, _ = proj3.shape

    def body(q_ref, z_ref, kc_ref, kp_ref, vc_ref, vp_ref, cq_ref, sq_ref, cp_ref, sp_ref, sk_ref,
             y_ref, o_ref, lse_ref):
        i = pl.program_id(2)
        cq_all, sq_all, _, _, kc_all, kp0, vc_all, vp0 = _swa_keys(
            pl.program_id(1), kc_ref, kp_ref, vc_ref, vp_ref, cq_ref, sq_ref, cp_ref, sp_ref)
        lo = lax.broadcasted_iota(jnp.int32, (CHUNK, LANES), 1) < HEAD_DIM
        sink4 = _swa_sink4(sk_ref[...])
        for u in range(_QB):
            rs = slice(CHUNK * u, CHUNK * (u + 1))
            ps = slice(CHUNK * (u - 1), CHUNK * u)
            cq, sq = cq_all[rs], sq_all[rs]
            kp, vp = (kp0, vp0) if u == 0 else (kc_all[ps], vc_all[ps])
            kk = jnp.concatenate([kp, kc_all[rs]], axis=0)
            vv = jnp.concatenate([vp, vc_all[rs]], axis=0)
            q4 = _swa_stack([_rope(q_ref[rs, LANES * pp:LANES * (pp + 1)].astype(F32), cq, sq) * _SCALE
                             for pp in range(2)], lo)
            sc = jnp.where(_swa_mask4(True if u > 0 else i > 0), _dot_nt(q4, kk), _NEG)
            m = jnp.maximum(jnp.max(sc, axis=1, keepdims=True), sink4)
            pr = jnp.exp(sc - m)
            l = jnp.sum(pr, axis=1, keepdims=True) + jnp.exp(sink4 - m)
            o4 = _dot(pr.astype(BF16), vv) / l
            lse4 = m + jnp.log(l)
            for pp in range(2):
                ls = slice(LANES * pp, LANES * (pp + 1))
                h0 = slice(2 * CHUNK * pp, 2 * CHUNK * pp + CHUNK)
                h1 = slice(2 * CHUNK * pp + CHUNK, 2 * CHUNK * (pp + 1))
                o = jnp.where(lo, o4[h0], o4[h1])
                z = z_ref[rs, ls].astype(F32)
                o_ref[rs, ls] = o
                lse_ref[rs, ls] = jnp.where(lo, lse4[h0], lse4[h1])
                y_ref[rs, ls] = (o * (z * _sigmoid(z))).astype(BF16)

    sp = _swa_specs(lambda bi, g, i: (bi, g, i))
    return pl.pallas_call(
        body, name=name, grid=(b, N_GROUPS, s // _QROWS),
        in_specs=[sp["q"], sp["z"], sp["kc"], sp["kp"], sp["vc"], sp["vp"],
                  sp["tcur"], sp["tcur"], sp["tprev"], sp["tprev"], sp["sk"]],
        out_specs=[sp["blk"], sp["blk"], sp["blk"]],
        out_shape=[jax.ShapeDtypeStruct((b, s, D_MODEL), BF16)] + [jax.ShapeDtypeStruct((b, s, D_MODEL), F32)] * 2,
        compiler_params=_cp(("parallel", "parallel", "parallel")),
    )(proj3, proj3, proj3, proj3, proj3, proj3, cos, sin, cos, sin, sinks)


def _swa_bwd(proj3, cos, sin, sinks, o3, lse3, dy3, *, name):
    b, s, _ = proj3.shape

    def body(q_ref, z_ref, kc_ref, kp_ref, vc_ref, vp_ref, cq_ref, sq_ref, cp_ref, sp_ref, sk_ref,
             o_ref, lse_ref, dy_ref, dq_ref, dz_ref, dkc_ref, dkp_ref, dvc_ref, dvp_ref, dsk_ref):
        i = pl.program_id(2)
        first = jnp.logical_and(pl.program_id(1) == 0, i == 0)

        @pl.when(first)
        def _():
            dsk_ref[...] = jnp.zeros_like(dsk_ref)

        cq_all, sq_all, cpv, spv, kc_all, kp0, vc_all, vp0 = _swa_keys(
            pl.program_id(0), kc_ref, kp_ref, vc_ref, vp_ref, cq_ref, sq_ref, cp_ref, sp_ref)
        lo = lax.broadcasted_iota(jnp.int32, (CHUNK, LANES), 1) < HEAD_DIM
        lane1 = lax.broadcasted_iota(jnp.int32, (1, LANES), 1)
        sink4 = _swa_sink4(sk_ref[...])
        zero = jnp.zeros((CHUNK, LANES), F32)
        dks = [zero] * (_QB + 1)
        dvs = [zero] * (_QB + 1)
        dsk_row = jnp.zeros((1, LANES), F32)
        for u in range(_QB):
            rs = slice(CHUNK * u, CHUNK * (u + 1))
            ps = slice(CHUNK * (u - 1), CHUNK * u)
            cq, sq = cq_all[rs], sq_all[rs]
            kp, vp = (kp0, vp0) if u == 0 else (kc_all[ps], vc_all[ps])
            kk = jnp.concatenate([kp, kc_all[rs]], axis=0)
            vv = jnp.concatenate([vp, vc_all[rs]], axis=0)
            q4 = _swa_stack([_rope(q_ref[rs, LANES * pp:LANES * (pp + 1)].astype(F32), cq, sq) * _SCALE
                             for pp in range(2)], lo)
            dos, lses = [], []
            for pp in range(2):
                ls = slice(LANES * pp, LANES * (pp + 1))
                z = z_ref[rs, ls].astype(F32)
                sz = _sigmoid(z)
                dy = dy_ref[rs, ls]
                dos.append(dy * (z * sz))
                dz_ref[rs, ls] = (dy * o_ref[rs, ls] * (sz * (1.0 + z * (1.0 - sz)))).astype(BF16)
                lse = lse_ref[rs, ls]
                lses += [_col(lse, 0), _col(lse, HEAD_DIM)]
            do4 = _swa_stack(dos, lo)
            lse4 = jnp.concatenate(lses, axis=0)
            pr = jnp.exp(jnp.where(_swa_mask4(True if u > 0 else i > 0), _dot_nt(q4, kk), _NEG) - lse4)
            dp = _dot_nt(do4, vv)
            dl = jnp.sum(pr * dp, axis=1, keepdims=True)
            ds = (pr * (dp - dl)).astype(BF16)
            dsink = -jnp.exp(sink4 - lse4) * dl
            for j in range(4):
                dsk_row = dsk_row + jnp.where(
                    lane1 == j, jnp.sum(dsink[CHUNK * j:CHUNK * (j + 1)], axis=0, keepdims=True), 0.0)
            dq4 = _dot(ds, kk)
            dkk = _dot_tn(ds, q4)
            dvv = _dot_tn(pr.astype(BF16), do4)
            dks[u], dks[u + 1] = dks[u] + dkk[:CHUNK], dks[u + 1] + dkk[CHUNK:]
            dvs[u], dvs[u + 1] = dvs[u] + dvv[:CHUNK], dvs[u + 1] + dvv[CHUNK:]
            for pp in range(2):
                h0 = slice(2 * CHUNK * pp, 2 * CHUNK * pp + CHUNK)
                h1 = slice(2 * CHUNK * pp + CHUNK, 2 * CHUNK * (pp + 1))
                dq_ref[rs, LANES * pp:LANES * (pp + 1)] = _rope(
                    jnp.where(lo, dq4[h0], dq4[h1]) * _SCALE, cq, -sq).astype(BF16)
        fold = lambda v: v + pltpu.roll(v, HEAD_DIM, 1)
        dkp_ref[...] = fold(_rope(dks[0], cpv, -spv))
        dvp_ref[...] = fold(dvs[0])
        for u in range(_QB):
            rs = slice(CHUNK * u, CHUNK * (u + 1))
            dkc_ref[rs, :] = fold(_rope(dks[u + 1], cq_all[rs], -sq_all[rs]))
            dvc_ref[rs, :] = fold(dvs[u + 1])
        dsk_ref[...] += dsk_row

    sp = _swa_specs(lambda g, bi, i: (bi, g, i))
    kv_shape = jax.ShapeDtypeStruct((b, s, 512), F32)
    kvp_shape = jax.ShapeDtypeStruct((b, s // _QB, 512), F32)
    return pl.pallas_call(
        body, name=name, grid=(N_GROUPS, b, s // _QROWS),
        in_specs=[sp["q"], sp["z"], sp["kc"], sp["kp"], sp["vc"], sp["vp"],
                  sp["tcur"], sp["tcur"], sp["tprev"], sp["tprev"], sp["sk"], sp["blk"], sp["blk"], sp["blk"]],
        out_specs=[sp["blk"], sp["blk"], sp["kcur"], sp["kstep"], sp["kcur"], sp["kstep"], sp["sk"]],
        out_shape=[jax.ShapeDtypeStruct((b, s, D_MODEL), BF16), jax.ShapeDtypeStruct((b, s, D_MODEL), BF16),
                   kv_shape, kvp_shape, kv_shape, kvp_shape, jax.ShapeDtypeStruct((N_GROUPS, 1, LANES), F32)],
        compiler_params=_cp(("arbitrary", "arbitrary", "arbitrary")),
    )(proj3, proj3, proj3, proj3, proj3, proj3, cos, sin, cos, sin, sinks, o3, lse3, dy3)


def _branch_fwd(ys, proj, gb, wp, wo, x, *, name, tm=256):
    t = proj.shape[0]
    g0 = _PAD_COLS["gates"][0] // D_MODEL

    def body(g_ref, a_ref, b_ref, c_ref, gb_ref, wp_ref, wo_ref, x_ref, ba_ref, bb_ref, bc_ref, m_ref, xn_ref):
        acc = None
        for i, (y, br) in enumerate(((a_ref, ba_ref), (b_ref, bb_ref), (c_ref, bc_ref))):
            bri = _dot(y[...], wp_ref[i])
            br[...] = bri
            gate = _sigmoid(g_ref[:, D_MODEL * i:D_MODEL * (i + 1)].astype(F32) + gb_ref[i:i + 1, :])
            acc = gate * bri if acc is None else acc + gate * bri
        mb = acc.astype(BF16)
        m_ref[...] = mb
        xn_ref[...] = x_ref[...] + _dot(mb, wo_ref[...])

    row = pl.BlockSpec((tm, D_MODEL), lambda i: (i, 0))
    rowf = jax.ShapeDtypeStruct((t, D_MODEL), F32)
    outs = pl.pallas_call(
        body, name=name, grid=(t // tm,),
        in_specs=[pl.BlockSpec((tm, 3 * D_MODEL), lambda i: (i, g0)), row, row, row,
                  pl.BlockSpec((3, D_MODEL), lambda i: (0, 0)),
                  pl.BlockSpec((3, D_MODEL, D_MODEL), lambda i: (0, 0, 0)),
                  pl.BlockSpec((D_MODEL, D_MODEL), lambda i: (0, 0)), row],
        out_specs=[row, row, row, row, row],
        out_shape=[rowf, rowf, rowf, jax.ShapeDtypeStruct((t, D_MODEL), BF16), rowf],
        compiler_params=_cp(("parallel",)),
    )(proj, ys[0], ys[1], ys[2], gb, wp, wo, x)
    return outs[:3], outs[3], outs[4]


def _branch_bwd(dx, proj, br, gb, wp, wo, *, name, tm=256):
    t = proj.shape[0]
    g0 = _PAD_COLS["gates"][0] // D_MODEL

    def body(g_ref, a_ref, b_ref, c_ref, gb_ref, wp_ref, wo_ref, dx_ref,
             da_ref, db_ref, dc_ref, dg_ref, dgb_ref, ya_ref, yb_ref, yc_ref):
        @pl.when(pl.program_id(0) == 0)
        def _():
            dgb_ref[...] = jnp.zeros_like(dgb_ref)

        dmv = _dot_nt(dx_ref[...].astype(BF16), wo_ref[...])
        for i, (r, dr, dy) in enumerate(((a_ref, da_ref, ya_ref), (b_ref, db_ref, yb_ref), (c_ref, dc_ref, yc_ref))):
            gate = _sigmoid(g_ref[:, D_MODEL * i:D_MODEL * (i + 1)].astype(F32) + gb_ref[i:i + 1, :])
            dbr = (dmv * gate).astype(BF16)
            dr[...] = dbr
            dg = dmv * r[...] * gate * (1.0 - gate)
            dg_ref[:, D_MODEL * i:D_MODEL * (i + 1)] = dg.astype(BF16)
            dgb_ref[i:i + 1, :] += jnp.sum(dg, axis=0, keepdims=True)
            dy[...] = _dot_nt(dbr, wp_ref[i])

    row = pl.BlockSpec((tm, D_MODEL), lambda i: (i, 0))
    rowb = jax.ShapeDtypeStruct((t, D_MODEL), BF16)
    rowf = jax.ShapeDtypeStruct((t, D_MODEL), F32)
    outs = pl.pallas_call(
        body, name=name, grid=(t // tm,),
        in_specs=[pl.BlockSpec((tm, 3 * D_MODEL), lambda i: (i, g0)), row, row, row,
                  pl.BlockSpec((3, D_MODEL), lambda i: (0, 0)),
                  pl.BlockSpec((3, D_MODEL, D_MODEL), lambda i: (0, 0, 0)),
                  pl.BlockSpec((D_MODEL, D_MODEL), lambda i: (0, 0)), row],
        out_specs=[row, row, row, pl.BlockSpec((tm, 3 * D_MODEL), lambda i: (i, 0)),
                   pl.BlockSpec((8, D_MODEL), lambda i: (0, 0)), row, row, row],
        out_shape=[rowb, rowb, rowb, jax.ShapeDtypeStruct((t, 3 * D_MODEL), BF16),
                   jax.ShapeDtypeStruct((8, D_MODEL), F32), rowf, rowf, rowf],
        compiler_params=_cp(("arbitrary",)),
    )(proj, br[0], br[1], br[2], gb, wp, wo, dx)
    return outs[:3], outs[3], outs[4], outs[5:]


def _rope_tables(s):
    pos = jnp.arange(s, dtype=F32)
    inv_freq = ROPE_THETA ** (-jnp.arange(0, HEAD_DIM, 2, dtype=F32) / HEAD_DIM)
    ang = pos[:, None] * inv_freq[None, :]
    cos, sin = jnp.cos(ang), jnp.sin(ang)
    return jnp.tile(cos, (1, 4)), jnp.tile(jnp.concatenate([-sin, sin], axis=1), (1, 2))


def _layer_params(wl):
    return dict(
        dtb=_group_lanes(wl["dt_bias"]), alog=_group_lanes(wl["a_log"]), dsk=_group_lanes(wl["d_skip"]),
        nw=wl["ssm_norm_w"].reshape(N_GROUPS, 1, 256), sinks=_group_lanes(wl["sinks"]),
        fb=jnp.pad(wl["f_bias"], (0, LANES - N_HEADS)).reshape(1, LANES))


def _layer_fwd(x, wl, tabs, bsz, li, tb):
    t = x.shape[0]
    s = t // bsz
    cos, sin = tabs
    lp = _layer_params(wl)
    n = lambda k: f"l{li}_{k}"
    h, h_t = _rms_fwd(x, wl["norm_w"], name=n("rms_fwd"))
    proj = _mm(h, wl["w_in"], tm=1024, tn=1536, tk=1024, out_dtype=BF16, name=n("mm_proj"))
    proj3 = proj.reshape(bsz, s, N_PAD)
    g0, gw = _PAD_COLS["a_dt"][0], _PAD_COLS["a_dt"][1] + _PAD_COLS["c_f"][1]
    gates3 = _mm(h, wl["w_in"][:, g0:g0 + gw], tm=1024, tn=gw, tk=1024, name=n("mm_gates")).reshape(bsz, s, gw)
    xact3 = _conv_fwd(proj3, wl["conv_w"], wl["conv_b"], name=n("conv_fwd"))
    ya3, ypre3, hst = _ssd_fwd(proj3, gates3, xact3, lp["dtb"], lp["alog"], lp["dsk"], lp["nw"], name=n("ssd_fwd"))
    yb3, ob3, lseb3 = _swa_fwd(proj3, cos, sin, lp["sinks"], name=n("swa_fwd"))
    cum = _fgate_fwd(gates3, lp["fb"], name=n("fgate_fwd"))
    cum_t = _ck_rep(cum)
    yc3, oc3, statc3 = _foxt_fwd(proj3, cum_t, name=n("fox_fwd"), tb=tb)
    ys = [v.reshape(t, D_MODEL) for v in (ya3, yb3, yc3)]
    br, merged, x_new = _branch_fwd(ys, proj, wl["gate_bias"], wl["w_proj"], wl["w_out"], x, name=n("branch_fwd"))
    saved = dict(x=x, h_t=h_t, proj=proj, gates3=gates3, xact3=xact3, ypre3=ypre3, hst=hst, ob3=ob3, lseb3=lseb3,
                 cum_t=cum_t, oc3=oc3, statc3=statc3, ys=ys, br=br, merged=merged, lp=lp)
    return x_new, saved


def _layer_bwd(dx, wl, sv, tabs, bsz, li, tb):
    t = dx.shape[0]
    s = t // bsz
    cos, sin = tabs
    lp = sv["lp"]
    n = lambda k: f"l{li}_{k}"
    proj = sv["proj"]
    proj3 = proj.reshape(bsz, s, N_PAD)
    g = {}
    g["w_out"] = _mm(sv["merged"], dx, ta=True, tm=1024, tn=1024, tk=512, name=n("mm_dwout"))
    dbr, dgates, dgb, dys = _branch_bwd(dx, proj, sv["br"], wl["gate_bias"], wl["w_proj"], wl["w_out"],
                                        name=n("branch_bwd"))
    g["gate_bias"] = dgb[:3]
    g["w_proj"] = jnp.stack([_mm(sv["ys"][i], dbr[i], ta=True, tm=1024, tn=1024, tk=512, name=n(f"mm_dwproj{i}"))
                             for i in range(3)])
    dy3 = [v.reshape(bsz, s, D_MODEL) for v in dys]

    (dact, daz, dadt, ddtb, dalog, ddsk, dnw) = _ssd_bwd(
        proj3, sv["gates3"], sv["xact3"], lp["dtb"], lp["alog"], lp["dsk"], lp["nw"], sv["ypre3"], sv["hst"], dy3[0],
        name=n("ssd_bwd"))
    g["dt_bias"], g["a_log"], g["d_skip"] = _ungroup_lanes(ddtb), _ungroup_lanes(dalog), _ungroup_lanes(ddsk)
    g["ssm_norm_w"] = dnw.reshape(D_MODEL)
    dxbc, dwb = _conv_bwd(proj3, wl["conv_w"], wl["conv_b"], dact, name=n("conv_bwd"))
    g["conv_w"], g["conv_b"] = dwb[:CONV_WIDTH], dwb[CONV_WIDTH]

    dbq, dbz, dkc, dkp, dvc, dvp, dsk = _swa_bwd(proj3, cos, sin, lp["sinks"], sv["ob3"],
                                                 sv["lseb3"], dy3[1], name=n("swa_bwd"))
    g["sinks"] = _ungroup_lanes(dsk)

    def fold(cur, prv):
        p4 = prv.reshape(bsz, s // _QROWS, 1, CHUNK, 512)
        tail = jnp.concatenate([p4[:, 1:], jnp.zeros_like(p4[:, :1])], axis=1)
        shifted = jnp.concatenate([jnp.zeros((bsz, s // _QROWS, _QB - 1, CHUNK, 512), F32), tail], axis=2)
        tot = cur + shifted.reshape(bsz, s, 512)
        return tot.reshape(bsz, s, N_GROUPS, 2, HEAD_DIM)[:, :, :, 0].reshape(bsz, s, 256)

    dbk, dbv = fold(dkc, dkp), fold(dvc, dvp)

    dcz, do3, stats = _foxt_prep(proj3, sv["oc3"], sv["statc3"], dy3[2], name=n("fox_prep"), tb=tb)
    dqt, dck, dcv, csum = _foxt_bwd(proj3, sv["cum_t"], do3, stats, name=n("fox_bwd"), tb=tb)
    dcq = jnp.transpose(dqt, (0, 2, 4, 1, 3)).reshape(bsz, s, D_MODEL)
    dcf, dfb = _fgate_bwd(sv["gates3"], lp["fb"], csum, name=n("fgate_bwd"))
    g["f_bias"] = dfb[0, :N_HEADS]

    parts = {"gates": dgates.reshape(bsz, s, 3 * D_MODEL), "xbc": dxbc, "a_z": daz, "b_q": dbq, "b_z": dbz,
             "c_q": dcq, "c_k": dck, "c_v": dcv, "c_z": dcz, "b_k": dbk, "b_v": dbv, "a_dt": dadt, "c_f": dcf}
    dproj = jnp.concatenate([parts[name].astype(BF16) for name, _ in _PAD_ORDER]
                            + [jnp.zeros((bsz, s, N_PAD - N_USED), BF16)], axis=2).reshape(t, N_PAD)
    dh = _mm(dproj, wl["w_in"], tb=True, tm=1024, tn=1024, tk=1536, name=n("mm_dh"))
    g["w_in"] = _unpad_w_in(_mm(sv["h_t"], dproj, tm=1024, tn=768, tk=2048, name=n("mm_dwin")))
    dx_in, dnorm = _rms_bwd(sv["x"], wl["norm_w"], dh, dx, name=n("rms_bwd"))
    g["norm_w"] = dnorm[0]
    return dx_in, g


def _local_step(x, target, wls, final_norm_w, tb=1024):
    bsz, s, d = x.shape
    t = bsz * s
    tabs = _rope_tables(s)
    xc = x.reshape(t, d)
    saved = []
    for li, wl in enumerate(wls):
        xc, sv = _layer_fwd(xc, wl, tabs, bsz, li, tb)
        saved.append(sv)
    loss, dx, dfw = _final_loss(xc, final_norm_w, target.reshape(t, d), name="final_loss")
    grads = [None] * len(wls)
    for li in reversed(range(len(wls))):
        dx, grads[li] = _layer_bwd(dx, wls[li], saved[li], tabs, bsz, li, tb)
    return loss[0, 0], dx.reshape(bsz, s, d), grads, dfw[0]


_HBM = pl.BlockSpec(memory_space=pltpu.HBM)


def _chip_peers(x, y):
    return [(1 - x, y), (x, 1 - y), (1 - x, 1 - y)]


def _gather_weights(arrs, *, name):
    n = len(arrs)

    def body(*refs):
        ins, outs = refs[:n], refs[n:2 * n]
        ici_send, ici_recv, d2d_send, d2d_recv = refs[2 * n:]
        x, y, c = lax.axis_index("x"), lax.axis_index("y"), lax.axis_index("c")
        me = 2 * x + y
        peers = _chip_peers(x, y)
        sib = (x, y, 1 - c)
        sends, fwds = [], []
        for a in range(n):
            for k, (px, py) in enumerate(peers):
                cp = pltpu.make_async_remote_copy(
                    src_ref=ins[a].at[c], dst_ref=outs[a].at[me, c], send_sem=ici_send.at[a, k],
                    recv_sem=ici_recv.at[a, k], device_id=(px, py, c), device_id_type=MESH)
                cp.start()
                sends.append(cp)
        for a in range(n):
            for k, (px, py) in enumerate(peers):
                slot = 2 * px + py
                pltpu.make_async_remote_copy(
                    src_ref=ins[a].at[c], dst_ref=outs[a].at[slot, c], send_sem=ici_send.at[a, k],
                    recv_sem=ici_recv.at[a, k], device_id=(px, py, c), device_id_type=MESH).wait_recv()
                fw = pltpu.make_async_remote_copy(
                    src_ref=outs[a].at[slot, c], dst_ref=outs[a].at[slot, c], send_sem=d2d_send.at[a, k],
                    recv_sem=d2d_recv.at[a, k], device_id=sib, device_id_type=MESH)
                fw.start()
                fwds.append(fw)
        for a in range(n):
            for k, (px, py) in enumerate(peers):
                slot = 2 * px + py
                pltpu.make_async_remote_copy(
                    src_ref=outs[a].at[slot, 1 - c], dst_ref=outs[a].at[slot, 1 - c], send_sem=d2d_send.at[a, k],
                    recv_sem=d2d_recv.at[a, k], device_id=sib, device_id_type=MESH).wait_recv()
        for cp in sends + fwds:
            cp.wait_send()

    out_shape = [jax.ShapeDtypeStruct((N_CHIPS,) + a.shape, a.dtype) for a in arrs]
    return pl.pallas_call(
        body, name=name, out_shape=out_shape, in_specs=[_HBM] * n, out_specs=[_HBM] * n,
        scratch_shapes=[pltpu.SemaphoreType.DMA((n, 3)), pltpu.SemaphoreType.DMA((n, 3)),
                        pltpu.SemaphoreType.DMA((n, 3)), pltpu.SemaphoreType.DMA((n, 3))],
    )(*arrs)


def _pair_exchange(arrs, *, name):
    n = len(arrs)

    def body(*refs):
        ins, outs = refs[:n], refs[n:2 * n]
        send, recv = refs[2 * n:]
        x, y, c = lax.axis_index("x"), lax.axis_index("y"), lax.axis_index("c")
        sib = (x, y, 1 - c)
        cps = []
        for a in range(n):
            for k in range(N_CHIPS):
                cp = pltpu.make_async_remote_copy(
                    src_ref=ins[a].at[k, 1 - c], dst_ref=outs[a].at[k], send_sem=send.at[a, k],
                    recv_sem=recv.at[a, k], device_id=sib, device_id_type=MESH)
                cp.start()
                cps.append(cp)
        for cp in cps:
            cp.wait()

    out_shape = [jax.ShapeDtypeStruct((N_CHIPS,) + a.shape[2:], a.dtype) for a in arrs]
    return pl.pallas_call(
        body, name=name, out_shape=out_shape, in_specs=[_HBM] * n, out_specs=[_HBM] * n,
        scratch_shapes=[pltpu.SemaphoreType.DMA((n, N_CHIPS)), pltpu.SemaphoreType.DMA((n, N_CHIPS))],
    )(*arrs)


def _chip_exchange(arrs, *, name):
    n = len(arrs)

    def body(*refs):
        ins, outs = refs[:n], refs[n:2 * n]
        send, recv = refs[2 * n:]
        x, y, c = lax.axis_index("x"), lax.axis_index("y"), lax.axis_index("c")
        me = 2 * x + y
        peers = _chip_peers(x, y)
        cps = []
        for a in range(n):
            for k, (px, py) in enumerate(peers):
                cp = pltpu.make_async_remote_copy(
                    src_ref=ins[a].at[2 * px + py], dst_ref=outs[a].at[me], send_sem=send.at[a, k],
                    recv_sem=recv.at[a, k], device_id=(px, py, c), device_id_type=MESH)
                cp.start()
                cps.append(cp)
        for a in range(n):
            for k, (px, py) in enumerate(peers):
                pltpu.make_async_remote_copy(
                    src_ref=ins[a].at[2 * px + py], dst_ref=outs[a].at[2 * px + py], send_sem=send.at[a, k],
                    recv_sem=recv.at[a, k], device_id=(px, py, c), device_id_type=MESH).wait_recv()
        for cp in cps:
            cp.wait_send()

    out_shape = [jax.ShapeDtypeStruct(a.shape, a.dtype) for a in arrs]
    return pl.pallas_call(
        body, name=name, out_shape=out_shape, in_specs=[_HBM] * n, out_specs=[_HBM] * n,
        scratch_shapes=[pltpu.SemaphoreType.DMA((n, 3)), pltpu.SemaphoreType.DMA((n, 3))],
    )(*arrs)


def _pair_share(arrs, *, name):
    n = len(arrs)

    def body(*refs):
        ins, outs = refs[:n], refs[n:2 * n]
        send, recv = refs[2 * n:]
        x, y, c = lax.axis_index("x"), lax.axis_index("y"), lax.axis_index("c")
        sib = (x, y, 1 - c)
        cps = []
        for a in range(n):
            cp = pltpu.make_async_remote_copy(
                src_ref=ins[a], dst_ref=outs[a], send_sem=send.at[a], recv_sem=recv.at[a],
                device_id=sib, device_id_type=MESH)
            cp.start()
            cps.append(cp)
        for cp in cps:
            cp.wait()

    out_shape = [jax.ShapeDtypeStruct(a.shape, a.dtype) for a in arrs]
    return pl.pallas_call(
        body, name=name, out_shape=out_shape, in_specs=[_HBM] * n, out_specs=[_HBM] * n,
        scratch_shapes=[pltpu.SemaphoreType.DMA((n,)), pltpu.SemaphoreType.DMA((n,))],
    )(*arrs)


def _allreduce_small(buf, *, name):
    r = buf.shape[0]

    def body(in_ref, out_ref, land, send, recv):
        x, y, c = lax.axis_index("x"), lax.axis_index("y"), lax.axis_index("c")
        me = 4 * x + 2 * y + c
        land[me] = in_ref[...]
        cps = []
        for k in range(1, N_DEV):
            px, py, pc = x ^ ((k >> 2) & 1), y ^ ((k >> 1) & 1), c ^ (k & 1)
            cp = pltpu.make_async_remote_copy(
                src_ref=in_ref, dst_ref=land.at[me], send_sem=send.at[k - 1], recv_sem=recv.at[k - 1],
                device_id=(px, py, pc), device_id_type=MESH)
            cp.start()
            cps.append(cp)
        for k in range(1, N_DEV):
            px, py, pc = x ^ ((k >> 2) & 1), y ^ ((k >> 1) & 1), c ^ (k & 1)
            pltpu.make_async_remote_copy(
                src_ref=in_ref, dst_ref=land.at[4 * px + 2 * py + pc], send_sem=send.at[k - 1],
                recv_sem=recv.at[k - 1], device_id=(px, py, pc), device_id_type=MESH).wait_recv()
        for cp in cps:
            cp.wait_send()
        acc = land[0]
        for k in range(1, N_DEV):
            acc = acc + land[k]
        out_ref[...] = acc

    vm = pl.BlockSpec(memory_space=pltpu.VMEM)
    return pl.pallas_call(
        body, name=name, out_shape=jax.ShapeDtypeStruct((r, LANES), F32), in_specs=[vm], out_specs=vm,
        scratch_shapes=[pltpu.VMEM((N_DEV, r, LANES), F32), pltpu.SemaphoreType.DMA((N_DEV - 1,)),
                        pltpu.SemaphoreType.DMA((N_DEV - 1,))],
    )(buf)


def _row_tile(rows, cols, n_arrays, budget=20 * 1024 * 1024):
    best = 8 if rows % 8 == 0 else rows
    tr = 8
    while tr <= rows:
        if rows % tr == 0 and tr * cols * 4 * n_arrays * 2 <= budget:
            best = tr
        tr *= 2
    return best


def _add_slot_layer(full, other, *, name):
    _, _, r, cdim = full.shape
    tr = _row_tile(r, cdim, 4)

    def body(c_ref, a_ref, b_ref, o_ref, ob_ref):
        sm = a_ref[...] + b_ref[...]
        o_ref[...] = sm
        ob_ref[...] = sm.astype(BF16)

    c = lax.axis_index("c").astype(jnp.int32).reshape(1)
    blk = pl.BlockSpec((None, tr, cdim), lambda k, i, c_ref: (k, i, 0))
    return pl.pallas_call(
        body, name=name,
        grid_spec=pltpu.PrefetchScalarGridSpec(
            num_scalar_prefetch=1, grid=(N_CHIPS, r // tr),
            in_specs=[pl.BlockSpec((None, None, tr, cdim), lambda k, i, c_ref: (k, c_ref[0], i, 0)), blk],
            out_specs=[blk, blk]),
        out_shape=[jax.ShapeDtypeStruct((N_CHIPS, r, cdim), F32), jax.ShapeDtypeStruct((N_CHIPS, r, cdim), BF16)],
        compiler_params=_cp(("parallel", "parallel")),
    )(c, full, other)


def _sum_slots(parts, pair, *, name):
    _, r, cdim = parts.shape
    tr = _row_tile(r, cdim, 5)

    def body(me_ref, p_ref, own_ref, o_ref):
        me = me_ref[0]
        acc = None
        for k in range(N_CHIPS):
            term = jnp.where(me == k, own_ref[...], p_ref[k].astype(F32))
            acc = term if acc is None else acc + term
        o_ref[...] = acc

    me = (2 * lax.axis_index("x") + lax.axis_index("y")).astype(jnp.int32).reshape(1)
    return pl.pallas_call(
        body, name=name,
        grid_spec=pltpu.PrefetchScalarGridSpec(
            num_scalar_prefetch=1, grid=(r // tr,),
            in_specs=[pl.BlockSpec((N_CHIPS, tr, cdim), lambda i, me_ref: (0, i, 0)),
                      pl.BlockSpec((None, tr, cdim), lambda i, me_ref: (me_ref[0], i, 0))],
            out_specs=pl.BlockSpec((tr, cdim), lambda i, me_ref: (i, 0))),
        out_shape=jax.ShapeDtypeStruct((r, cdim), F32),
        compiler_params=_cp(("parallel",)),
    )(me, parts, pair)


def _adamw(w, g, m, v, *, name):
    lead, (r, cdim) = w.shape[:-2], w.shape[-2:]
    nl = len(lead)
    tr = _row_tile(r, cdim, 7)
    c1 = 1.0 - ADAM_B1 ** ADAM_STEP
    c2 = 1.0 - ADAM_B2 ** ADAM_STEP

    def body(w_ref, g_ref, m_ref, v_ref, d_ref, nm_ref, nv_ref):
        gv = g_ref[...]
        mn = ADAM_B1 * m_ref[...] + (1.0 - ADAM_B1) * gv
        vn = ADAM_B2 * v_ref[...] + (1.0 - ADAM_B2) * (gv * gv)
        nm_ref[...] = mn
        nv_ref[...] = vn
        d_ref[...] = -ADAM_LR * ((mn / c1) / (jnp.sqrt(vn / c2) + ADAM_EPS) + ADAM_WD * w_ref[...])

    blk = pl.BlockSpec((None,) * nl + (tr, cdim), lambda *ids: ids[:nl] + (ids[nl], 0))
    sh = jax.ShapeDtypeStruct(w.shape, F32)
    return pl.pallas_call(
        body, name=name, grid=lead + (r // tr,), in_specs=[blk] * 4, out_specs=[blk] * 3, out_shape=[sh] * 3,
        compiler_params=_cp(("parallel",) * (nl + 1)),
    )(w, g, m, v)


_SMALL = ("norm_w", "conv_b", "dt_bias", "a_log", "d_skip", "ssm_norm_w", "sinks", "f_bias", "final_norm_w",
          "conv_w", "gate_bias")


def _pack(vals):
    flat = jnp.concatenate([v.reshape(-1) for v in vals])
    rows = -(-flat.shape[0] // LANES)
    rows = -(-rows // 8) * 8
    return jnp.pad(flat, (0, rows * LANES - flat.shape[0])).reshape(rows, LANES)


def _unpack(buf, shapes):
    flat = buf.reshape(-1)
    out, off = [], 0
    for sh in shapes:
        sz = int(np.prod(sh))
        out.append(flat[off:off + sz].reshape(sh))
        off += sz
    return out


def kernel(x, norm_w, w_in, conv_w, conv_b, dt_bias, a_log, d_skip, ssm_norm_w, sinks, f_bias, gate_bias, w_proj, w_out, final_norm_w, loss_target, m_norm_w, m_w_in, m_conv_w, m_conv_b, m_dt_bias, m_a_log, m_d_skip, m_ssm_norm_w, m_sinks, m_f_bias, m_gate_bias, m_w_proj, m_w_out, m_final_norm_w, v_norm_w, v_w_in, v_conv_w, v_conv_b, v_dt_bias, v_a_log, v_d_skip, v_ssm_norm_w, v_sinks, v_f_bias, v_gate_bias, v_w_proj, v_w_out, v_final_norm_w):
    depth = w_in.shape[0]
    chip = 2 * lax.axis_index("x") + lax.axis_index("y")

    own = [w_in.astype(BF16), w_proj.astype(BF16), w_out.astype(BF16), conv_w, gate_bias]
    gathered = _gather_weights(own, name="gather_weights")

    def whole(a, li, axis):
        return jnp.concatenate([jnp.where(chip == k, own[a][li], gathered[a][k, li]) for k in range(N_CHIPS)],
                               axis=axis)

    wls = []
    for li in range(depth):
        wls.append(dict(
            norm_w=norm_w[li], w_in=_pad_w_in(whole(0, li, 1)),
            conv_w=whole(3, li, 1), conv_b=conv_b[li], dt_bias=dt_bias[li], a_log=a_log[li], d_skip=d_skip[li],
            ssm_norm_w=ssm_norm_w[li], sinks=sinks[li], f_bias=f_bias[li], gate_bias=whole(4, li, 1),
            w_proj=whole(1, li, 1),
            w_out=whole(2, li, 0)))

    loss_part, grad_x, grads, d_final = _local_step(x, loss_target, wls, final_norm_w)
    loss = lax.psum(loss_part, ("x", "y", "c"))

    c_in = w_in.shape[2]
    r_proj = w_proj.shape[2]
    r_out = w_out.shape[1]
    full_in = jnp.stack([jnp.stack([grads[li]["w_in"][:, k * c_in:(k + 1) * c_in] for li in range(depth)])
                         for k in range(N_CHIPS)])
    full_proj = jnp.stack([jnp.stack([grads[li]["w_proj"][:, k * r_proj:(k + 1) * r_proj].reshape(-1, D_MODEL)
                                      for li in range(depth)]) for k in range(N_CHIPS)])
    full_out = jnp.stack([jnp.stack([grads[li]["w_out"][k * r_out:(k + 1) * r_out] for li in range(depth)])
                          for k in range(N_CHIPS)])
    fulls = [full_in, full_proj, full_out]
    others = _pair_exchange(fulls, name="grad_pair_exchange")
    pair = [_add_slot_layer(f, o, name=f"grad_pair_add{i}") for i, (f, o) in enumerate(zip(fulls, others))]
    parts = _chip_exchange([p[1] for p in pair], name="grad_chip_exchange")
    mine = [_sum_slots(p, pr[0], name=f"grad_slot_sum{i}") for i, (p, pr) in enumerate(zip(parts, pair))]
    theirs = _pair_share(mine, name="grad_pair_share")
    core = lax.axis_index("c")
    red_in, red_proj, red_out = [jnp.stack([jnp.where(core == li, m, t) for li in range(depth)])
                                 for m, t in zip(mine, theirs)]
    grad_w_in = red_in
    grad_w_proj = red_proj.reshape(w_proj.shape)
    grad_w_out = red_out

    small_full = {
        "norm_w": jnp.stack([g["norm_w"] for g in grads]), "conv_b": jnp.stack([g["conv_b"] for g in grads]),
        "dt_bias": jnp.stack([g["dt_bias"] for g in grads]), "a_log": jnp.stack([g["a_log"] for g in grads]),
        "d_skip": jnp.stack([g["d_skip"] for g in grads]),
        "ssm_norm_w": jnp.stack([g["ssm_norm_w"] for g in grads]),
        "sinks": jnp.stack([g["sinks"] for g in grads]), "f_bias": jnp.stack([g["f_bias"] for g in grads]),
        "final_norm_w": d_final,
        "conv_w": jnp.stack([g["conv_w"] for g in grads]), "gate_bias": jnp.stack([g["gate_bias"] for g in grads])}
    shapes = [small_full[k].shape for k in _SMALL]
    summed = _unpack(_allreduce_small(_pack([small_full[k] for k in _SMALL]), name="allreduce_small"), shapes)
    gsmall = dict(zip(_SMALL, summed))
    gsmall["conv_w"] = lax.dynamic_slice_in_dim(gsmall["conv_w"], chip * conv_w.shape[2], conv_w.shape[2], axis=2)
    gsmall["gate_bias"] = lax.dynamic_slice_in_dim(gsmall["gate_bias"], chip * gate_bias.shape[2],
                                                   gate_bias.shape[2], axis=2)

    w_small = dict(norm_w=norm_w, conv_b=conv_b, dt_bias=dt_bias, a_log=a_log, d_skip=d_skip,
                   ssm_norm_w=ssm_norm_w, sinks=sinks, f_bias=f_bias, final_norm_w=final_norm_w, conv_w=conv_w,
                   gate_bias=gate_bias)
    m_small = dict(norm_w=m_norm_w, conv_b=m_conv_b, dt_bias=m_dt_bias, a_log=m_a_log, d_skip=m_d_skip,
                   ssm_norm_w=m_ssm_norm_w, sinks=m_sinks, f_bias=m_f_bias, final_norm_w=m_final_norm_w,
                   conv_w=m_conv_w, gate_bias=m_gate_bias)
    v_small = dict(norm_w=v_norm_w, conv_b=v_conv_b, dt_bias=v_dt_bias, a_log=v_a_log, d_skip=v_d_skip,
                   ssm_norm_w=v_ssm_norm_w, sinks=v_sinks, f_bias=v_f_bias, final_norm_w=v_final_norm_w,
                   conv_w=v_conv_w, gate_bias=v_gate_bias)
    sshapes = [w_small[k].shape for k in _SMALL]
    ds, ms, vs = _adamw(_pack([w_small[k] for k in _SMALL]), _pack([gsmall[k] for k in _SMALL]),
                        _pack([m_small[k] for k in _SMALL]), _pack([v_small[k] for k in _SMALL]), name="adamw_small")
    delta = dict(zip(_SMALL, _unpack(ds, sshapes)))
    new_m = dict(zip(_SMALL, _unpack(ms, sshapes)))
    new_v = dict(zip(_SMALL, _unpack(vs, sshapes)))
    grad = dict(gsmall)
    for nm, w, g, m, v in (("w_in", w_in, grad_w_in, m_w_in, v_w_in),
                           ("w_proj", w_proj, grad_w_proj, m_w_proj, v_w_proj),
                           ("w_out", w_out, grad_w_out, m_w_out, v_w_out)):
        grad[nm] = g
        delta[nm], new_m[nm], new_v[nm] = _adamw(w, g, m, v, name=f"adamw_{nm}")

    order = ("norm_w", "w_in", "conv_w", "conv_b", "dt_bias", "a_log", "d_skip", "ssm_norm_w", "sinks", "f_bias",
             "gate_bias", "w_proj", "w_out", "final_norm_w")
    return (loss, grad_x, *[grad[k] for k in order], *[delta[k] for k in order],
            *[new_m[k] for k in order], *[new_v[k] for k in order])
```

```python
import functools
import math

import numpy as np
import jax
import jax.numpy as jnp
from jax import lax
from jax.experimental import pallas as pl
from jax.experimental.pallas import tpu as pltpu

F32 = jnp.float32
BF16 = jnp.bfloat16
HIGHEST = lax.Precision.HIGHEST
MESH = pl.DeviceIdType.MESH

D_MODEL = 1024
HEAD_DIM = 64
N_HEADS = 16
N_GROUPS = 4
SSM_STATE = 128
CHUNK = 128
CONV_WIDTH = 4
CONV_DIM = 2048
ROPE_THETA = 10000.0
NORM_EPS = 1e-6
LANES = 128
N_CHIPS = 4
N_DEV = 8

ADAM_LR = 0.001
ADAM_B1 = 0.9
ADAM_B2 = 0.999
ADAM_EPS = 1e-08
ADAM_WD = 0.01
ADAM_STEP = 10

_REF_COLS = {}
_off = 0
for _n, _s in (("xbc", 2048), ("a_z", 1024), ("a_dt", 16), ("b_q", 1024), ("b_k", 256), ("b_v", 256),
               ("b_z", 1024), ("c_q", 1024), ("c_k", 1024), ("c_v", 1024), ("c_f", 16), ("c_z", 1024),
               ("gates", 3072)):
    _REF_COLS[_n] = (_off, _s)
    _off += _s
N_IN = _off

_PAD_ORDER = (("gates", 3072), ("xbc", 2048), ("a_z", 1024), ("b_q", 1024), ("b_z", 1024), ("c_q", 1024),
              ("c_k", 1024), ("c_v", 1024), ("c_z", 1024), ("b_k", 256), ("b_v", 256), ("a_dt", 512),
              ("c_f", 128))
_PAD_COLS = {}
_off = 0
for _n, _s in _PAD_ORDER:
    _PAD_COLS[_n] = (_off, _s)
    _off += _s
N_USED = _off
N_PAD = 13824


def _cp(sem, vmem_mb=48):
    return pltpu.CompilerParams(dimension_semantics=sem, vmem_limit_bytes=vmem_mb * 1024 * 1024)


def _dot(a, b, dims=((1,), (0,)), precision=None):
    return lax.dot_general(a, b, (dims, ((), ())), preferred_element_type=F32, precision=precision)


def _dot_nt(a, b):
    return _dot(a, b, ((1,), (1,)))


def _dot_tn(a, b):
    return _dot(a, b, ((0,), (0,)))


def _col(v, idx):
    lane = lax.broadcasted_iota(jnp.int32, v.shape, 1)
    return jnp.sum(jnp.where(lane == idx, v, 0.0), axis=1, keepdims=True)


def _row(v, idx):
    row = lax.broadcasted_iota(jnp.int32, v.shape, 0)
    return jnp.sum(jnp.where(row == idx, v, 0.0), axis=0, keepdims=True)


def _iota_col():
    return lax.broadcasted_iota(jnp.int32, (CHUNK, 1), 0)


def _iota_row():
    return lax.broadcasted_iota(jnp.int32, (1, LANES), 1)


def _sigmoid(x):
    return 1.0 / (1.0 + jnp.exp(-x))


def _softplus(x):
    return jnp.maximum(x, 0.0) + jnp.log(1.0 + jnp.exp(-jnp.abs(x)))


def _pad_w_in(w):
    parts = []
    for name, size in _PAD_ORDER:
        s0, sz = _REF_COLS[name]
        seg = w[:, s0:s0 + sz]
        if name == "a_dt":
            seg = jnp.pad(seg.reshape(-1, N_GROUPS, 4), ((0, 0), (0, 0), (0, LANES - 4))).reshape(-1, 512)
        elif name == "c_f":
            seg = jnp.pad(seg, ((0, 0), (0, LANES - 16)))
        parts.append(seg)
    parts.append(jnp.zeros((w.shape[0], N_PAD - N_USED), w.dtype))
    return jnp.concatenate(parts, axis=1)


def _unpad_w_in(wp):
    segs = {}
    for name, _ in _PAD_ORDER:
        p0, psz = _PAD_COLS[name]
        seg = wp[:, p0:p0 + psz]
        if name == "a_dt":
            seg = seg.reshape(-1, N_GROUPS, LANES)[:, :, :4].reshape(-1, 16)
        elif name == "c_f":
            seg = seg[:, :16]
        segs[name] = seg
    order = sorted(_REF_COLS, key=lambda n: _REF_COLS[n][0])
    return jnp.concatenate([segs[n] for n in order], axis=1)


def _group_lanes(v):
    return jnp.pad(v.reshape(N_GROUPS, 1, 4), ((0, 0), (0, 0), (0, LANES - 4)))


def _ungroup_lanes(v):
    return v[:, 0, :4].reshape(16)


def _mm(a, b, *, ta=False, tb=False, tm=512, tn=512, tk=512, out_dtype=F32, name):
    if ta:
        kdim, m = a.shape
    else:
        m, kdim = a.shape
    if tb:
        n, k2 = b.shape
    else:
        k2, n = b.shape
    assert kdim == k2, (a.shape, b.shape)
    tm, tn, tk = min(tm, m), min(tn, n), min(tk, kdim)
    assert m % tm == 0 and n % tn == 0 and kdim % tk == 0, (m, n, kdim, tm, tn, tk)
    nk = kdim // tk
    a_spec = (pl.BlockSpec((tk, tm), lambda i, j, k: (k, i)) if ta
              else pl.BlockSpec((tm, tk), lambda i, j, k: (i, k)))
    b_spec = (pl.BlockSpec((tn, tk), lambda i, j, k: (j, k)) if tb
              else pl.BlockSpec((tk, tn), lambda i, j, k: (k, j)))
    dims = ((0 if ta else 1,), (1 if tb else 0,))

    def body(a_ref, b_ref, o_ref, acc_ref):
        k = pl.program_id(2)
        p = _dot(a_ref[...].astype(BF16), b_ref[...].astype(BF16), dims)

        @pl.when(k == 0)
        def _():
            acc_ref[...] = p

        @pl.when(k > 0)
        def _():
            acc_ref[...] += p

        @pl.when(k == nk - 1)
        def _():
            o_ref[...] = acc_ref[...].astype(out_dtype)

    return pl.pallas_call(
        body, name=name, grid=(m // tm, n // tn, nk),
        in_specs=[a_spec, b_spec], out_specs=pl.BlockSpec((tm, tn), lambda i, j, k: (i, j)),
        out_shape=jax.ShapeDtypeStruct((m, n), out_dtype),
        scratch_shapes=[pltpu.VMEM((tm, tn), F32)],
        compiler_params=_cp(("parallel", "parallel", "arbitrary")),
    )(a, b)


def _rms_fwd(x, w, *, name, tm=512):
    t, d = x.shape

    def body(x_ref, w_ref, o_ref, ot_ref):
        xv = x_ref[...]
        r = lax.rsqrt(jnp.mean(xv * xv, axis=1, keepdims=True) + NORM_EPS)
        h = xv * r * w_ref[...]
        o_ref[...] = h.astype(BF16)
        ot_ref[...] = h.T.astype(BF16)

    return pl.pallas_call(
        body, name=name, grid=(t // tm,),
        in_specs=[pl.BlockSpec((tm, d), lambda i: (i, 0)), pl.BlockSpec((1, d), lambda i: (0, 0))],
        out_specs=[pl.BlockSpec((tm, d), lambda i: (i, 0)), pl.BlockSpec((d, tm), lambda i: (0, i))],
        out_shape=[jax.ShapeDtypeStruct((t, d), BF16), jax.ShapeDtypeStruct((d, t), BF16)],
        compiler_params=_cp(("parallel",)),
    )(x, w.reshape(1, d))


def _rms_bwd(x, w, dh, dres, *, name, tm=512):
    t, d = x.shape

    def body(x_ref, w_ref, dh_ref, dres_ref, dx_ref, dw_ref):
        xv = x_ref[...]
        r = lax.rsqrt(jnp.mean(xv * xv, axis=1, keepdims=True) + NORM_EPS)
        xhat = xv * r
        dhv = dh_ref[...]
        dxhat = dhv * w_ref[...]
        dx = r * (dxhat - xhat * jnp.mean(dxhat * xhat, axis=1, keepdims=True))
        dx_ref[...] = dres_ref[...] + dx

        @pl.when(pl.program_id(0) == 0)
        def _():
            dw_ref[...] = jnp.zeros_like(dw_ref)

        dw_ref[...] += jnp.sum(dhv * xhat, axis=0, keepdims=True)

    return pl.pallas_call(
        body, name=name, grid=(t // tm,),
        in_specs=[pl.BlockSpec((tm, d), lambda i: (i, 0)), pl.BlockSpec((1, d), lambda i: (0, 0)),
                  pl.BlockSpec((tm, d), lambda i: (i, 0)), pl.BlockSpec((tm, d), lambda i: (i, 0))],
        out_specs=[pl.BlockSpec((tm, d), lambda i: (i, 0)), pl.BlockSpec((1, d), lambda i: (0, 0))],
        out_shape=[jax.ShapeDtypeStruct((t, d), F32), jax.ShapeDtypeStruct((1, d), F32)],
        compiler_params=_cp(("arbitrary",)),
    )(x, w.reshape(1, d), dh, dres)


def _final_loss(x, w, target, *, name, tm=512):
    t, d = x.shape

    def body(x_ref, w_ref, t_ref, loss_ref, dx_ref, dw_ref):
        xv = x_ref[...]
        wv = w_ref[...]
        r = lax.rsqrt(jnp.mean(xv * xv, axis=1, keepdims=True) + NORM_EPS)
        xhat = xv * r
        err = xhat * wv - t_ref[...]
        dy = err * (1.0 / d)
        dxhat = dy * wv
        dx_ref[...] = r * (dxhat - xhat * jnp.mean(dxhat * xhat, axis=1, keepdims=True))

        @pl.when(pl.program_id(0) == 0)
        def _():
            dw_ref[...] = jnp.zeros_like(dw_ref)
            loss_ref[...] = jnp.zeros_like(loss_ref)

        dw_ref[...] += jnp.sum(dy * xhat, axis=0, keepdims=True)
        part = 0.5 * jnp.sum(jnp.mean(err * err, axis=1, keepdims=True), axis=0, keepdims=True)
        loss_ref[...] += jnp.broadcast_to(part, loss_ref.shape)

    return pl.pallas_call(
        body, name=name, grid=(t // tm,),
        in_specs=[pl.BlockSpec((tm, d), lambda i: (i, 0)), pl.BlockSpec((1, d), lambda i: (0, 0)),
                  pl.BlockSpec((tm, d), lambda i: (i, 0))],
        out_specs=[pl.BlockSpec((8, LANES), lambda i: (0, 0)), pl.BlockSpec((tm, d), lambda i: (i, 0)),
                   pl.BlockSpec((1, d), lambda i: (0, 0))],
        out_shape=[jax.ShapeDtypeStruct((8, LANES), F32), jax.ShapeDtypeStruct((t, d), F32),
                   jax.ShapeDtypeStruct((1, d), F32)],
        compiler_params=_cp(("arbitrary",)),
    )(x, w.reshape(1, d), target)


_CB = 128


def _conv_pre(u, w_ref, b_ref):
    s = u.shape[0]
    row = lax.broadcasted_iota(jnp.int32, u.shape, 0)
    pre = b_ref[...] + w_ref[CONV_WIDTH - 1:CONV_WIDTH, :] * u
    for sh in range(1, CONV_WIDTH):
        shifted = jnp.where(row >= sh, pltpu.roll(u, sh, 0), 0.0)
        pre = pre + w_ref[CONV_WIDTH - 1 - sh:CONV_WIDTH - sh, :] * shifted
    return pre


def _conv_fwd(proj3, cw, cb, *, name):
    b, s, _ = proj3.shape
    c0 = _PAD_COLS["xbc"][0] // _CB

    def body(u_ref, w_ref, b_ref, o_ref):
        pre = _conv_pre(u_ref[...].astype(F32), w_ref, b_ref)
        o_ref[...] = pre * _sigmoid(pre)

    return pl.pallas_call(
        body, name=name, grid=(b, CONV_DIM // _CB),
        in_specs=[pl.BlockSpec((None, s, _CB), lambda i, j: (i, 0, c0 + j)),
                  pl.BlockSpec((CONV_WIDTH, _CB), lambda i, j: (0, j)),
                  pl.BlockSpec((1, _CB), lambda i, j: (0, j))],
        out_specs=pl.BlockSpec((None, s, _CB), lambda i, j: (i, 0, j)),
        out_shape=jax.ShapeDtypeStruct((b, s, CONV_DIM), F32),
        compiler_params=_cp(("parallel", "parallel")),
    )(proj3, cw, cb.reshape(1, CONV_DIM))


def _conv_bwd(proj3, cw, cb, dact, *, name):
    b, s, _ = proj3.shape
    c0 = _PAD_COLS["xbc"][0] // _CB

    def body(u_ref, w_ref, b_ref, da_ref, du_ref, dwb_ref):
        u = u_ref[...].astype(F32)
        pre = _conv_pre(u, w_ref, b_ref)
        sg = _sigmoid(pre)
        dpre = da_ref[...] * (sg * (1.0 + pre * (1.0 - sg)))
        row = lax.broadcasted_iota(jnp.int32, u.shape, 0)
        du = w_ref[CONV_WIDTH - 1:CONV_WIDTH, :] * dpre
        rows = [jnp.sum(dpre * u, axis=0, keepdims=True)]
        for sh in range(1, CONV_WIDTH):
            fwd_shift = jnp.where(row < s - sh, pltpu.roll(dpre, s - sh, 0), 0.0)
            du = du + w_ref[CONV_WIDTH - 1 - sh:CONV_WIDTH - sh, :] * fwd_shift
            ush = jnp.where(row >= sh, pltpu.roll(u, sh, 0), 0.0)
            rows.append(jnp.sum(dpre * ush, axis=0, keepdims=True))
        du_ref[...] = du.astype(BF16)

        @pl.when(pl.program_id(1) == 0)
        def _():
            dwb_ref[...] = jnp.zeros_like(dwb_ref)

        for sh in range(CONV_WIDTH):
            k = CONV_WIDTH - 1 - sh
            dwb_ref[k:k + 1, :] += rows[sh]
        dwb_ref[CONV_WIDTH:CONV_WIDTH + 1, :] += jnp.sum(dpre, axis=0, keepdims=True)

    return pl.pallas_call(
        body, name=name, grid=(CONV_DIM // _CB, b),
        in_specs=[pl.BlockSpec((None, s, _CB), lambda j, i: (i, 0, c0 + j)),
                  pl.BlockSpec((CONV_WIDTH, _CB), lambda j, i: (0, j)),
                  pl.BlockSpec((1, _CB), lambda j, i: (0, j)),
                  pl.BlockSpec((None, s, _CB), lambda j, i: (i, 0, j))],
        out_specs=[pl.BlockSpec((None, s, _CB), lambda j, i: (i, 0, j)),
                   pl.BlockSpec((8, _CB), lambda j, i: (0, j))],
        out_shape=[jax.ShapeDtypeStruct((b, s, CONV_DIM), BF16), jax.ShapeDtypeStruct((8, CONV_DIM), F32)],
        compiler_params=_cp(("parallel", "arbitrary")),
    )(proj3, cw, cb.reshape(1, CONV_DIM), dact)


def _ssd_common(dt_ref, dtb_ref, alog_ref):
    row = lax.broadcasted_iota(jnp.int32, (CHUNK, CHUNK), 0)
    lane = lax.broadcasted_iota(jnp.int32, (CHUNK, CHUNK), 1)
    causal = row >= lane
    tri = causal.astype(F32)
    dtv = _softplus(dt_ref[...] + dtb_ref[...])
    a_row = -jnp.exp(alog_ref[...])
    acum = _dot(tri, dtv * a_row, precision=HIGHEST)
    return row, lane, causal, dtv, a_row, acum, acum.T


def _ssd_pair(pp, x, dtv, acum, acum_t, causal, lane, row):
    lo = lane < HEAD_DIM
    r0, r1 = 2 * pp, 2 * pp + 1
    dtp = jnp.where(lo, _col(dtv, r0), _col(dtv, r1))
    ac0, ac1 = _col(acum, r0), _col(acum, r1)
    ar0, ar1 = _row(acum_t, r0), _row(acum_t, r1)
    d0 = jnp.where(causal, jnp.exp(jnp.where(causal, ac0 - ar0, 0.0)), 0.0)
    d1 = jnp.where(causal, jnp.exp(jnp.where(causal, ac1 - ar1, 0.0)), 0.0)
    al0, al1 = _col(ar0, CHUNK - 1), _col(ar1, CHUNK - 1)
    eac = jnp.where(lo, jnp.exp(ac0), jnp.exp(ac1))
    dsp = jnp.where(lo, jnp.exp(al0 - ac0), jnp.exp(al1 - ac1))
    eal = jnp.where(_iota_col() < HEAD_DIM, jnp.exp(al0), jnp.exp(al1))
    return lo, dtp, x * dtp, d0, d1, al0, al1, eac, dsp, eal


def _ssd_fwd(proj3, gates3, xact3, dtb, alog, dsk, nw, *, name):
    b, s, _ = proj3.shape
    nc = s // CHUNK
    dt0 = 0
    z0 = _PAD_COLS["a_z"][0] // D_MODEL

    def body(xs_ref, bm_ref, cm_ref, dt_ref, z_ref, dtb_ref, alog_ref, dsk_ref, nw_ref,
             ya_ref, ypre_ref, hst_ref, h_scr):
        @pl.when(pl.program_id(1) == 0)
        def _():
            h_scr[...] = jnp.zeros_like(h_scr)

        for g in range(N_GROUPS):
            w256 = pl.ds(256 * g, 256)
            w128 = pl.ds(LANES * g, LANES)
            group(xs_ref.at[:, w256], bm_ref.at[:, w128], cm_ref.at[:, w128], dt_ref.at[:, w128],
                  z_ref.at[:, w256], dtb_ref.at[g], alog_ref.at[g], dsk_ref.at[g], nw_ref.at[g],
                  ya_ref.at[:, w256], ypre_ref.at[:, w256], hst_ref.at[g], h_scr.at[g])

    def group(xs_ref, bm_ref, cm_ref, dt_ref, z_ref, dtb_ref, alog_ref, dsk_ref, nw_ref,
              ya_ref, ypre_ref, hst_ref, h_scr):
        row, lane, causal, dtv, a_row, acum, acum_t = _ssd_common(dt_ref, dtb_ref, alog_ref)
        bb = bm_ref[...].astype(BF16)
        cb = cm_ref[...].astype(BF16)
        cbm = _dot_nt(cb, bb)
        hst_ref[...] = h_scr[...]
        dskv = dsk_ref[...]
        for pp in range(2):
            x = xs_ref[:, LANES * pp:LANES * (pp + 1)]
            lo, dtp, xd, d0, d1, al0, al1, eac, dsp, eal = _ssd_pair(pp, x, dtv, acum, acum_t, causal, lane, row)
            xdb = xd.astype(BF16)
            y = jnp.where(lo, _dot((cbm * d0).astype(BF16), xdb), _dot((cbm * d1).astype(BF16), xdb))
            h = h_scr[pp]
            y = y + eac * _dot_nt(cb, h.astype(BF16))
            h_scr[pp] = h * eal + _dot_tn((xd * dsp).astype(BF16), bb)
            dskp = jnp.where((_iota_row() < HEAD_DIM), _col(dskv, 2 * pp), _col(dskv, 2 * pp + 1))
            ypre_ref[:, LANES * pp:LANES * (pp + 1)] = y + x * dskp
        ypre = ypre_ref[...]
        z = z_ref[...].astype(F32)
        yg = ypre * (z * _sigmoid(z))
        rstd = lax.rsqrt(jnp.sum(yg * yg, axis=1, keepdims=True) * (1.0 / 256.0) + NORM_EPS)
        ya_ref[...] = (yg * rstd * nw_ref[...]).astype(BF16)

    g = N_GROUPS
    par = pl.BlockSpec((g, 1, LANES), lambda i, c: (0, 0, 0))
    wide = pl.BlockSpec((None, CHUNK, D_MODEL), lambda i, c: (i, c, 0))
    return pl.pallas_call(
        body, name=name, grid=(b, nc),
        in_specs=[wide,
                  pl.BlockSpec((None, CHUNK, 512), lambda i, c: (i, c, 2)),
                  pl.BlockSpec((None, CHUNK, 512), lambda i, c: (i, c, 3)),
                  pl.BlockSpec((None, CHUNK, 512), lambda i, c: (i, c, dt0)),
                  pl.BlockSpec((None, CHUNK, D_MODEL), lambda i, c: (i, c, z0)),
                  par, par, par,
                  pl.BlockSpec((g, 1, 256), lambda i, c: (0, 0, 0))],
        out_specs=[wide, wide,
                   pl.BlockSpec((None, None, g, 2, CHUNK, SSM_STATE), lambda i, c: (i, c, 0, 0, 0, 0))],
        out_shape=[jax.ShapeDtypeStruct((b, s, D_MODEL), BF16), jax.ShapeDtypeStruct((b, s, D_MODEL), F32),
                   jax.ShapeDtypeStruct((b, nc, g, 2, CHUNK, SSM_STATE), F32)],
        scratch_shapes=[pltpu.VMEM((g, 2, CHUNK, SSM_STATE), F32)],
        compiler_params=_cp(("parallel", "arbitrary")),
    )(xact3, xact3, xact3, gates3, proj3, dtb, alog, dsk, nw)


def _ssd_bwd(proj3, gates3, xact3, dtb, alog, dsk, nw, ypre3, hst, dya3, *, name):
    b, s, _ = proj3.shape
    nc = s // CHUNK
    dt0 = 0
    z0 = _PAD_COLS["a_z"][0] // D_MODEL

    def body(xs_ref, bm_ref, cm_ref, dt_ref, z_ref, dtb_ref, alog_ref, dsk_ref, nw_ref, ypre_ref, hst_ref,
             dya_ref, dact_ref, dz_ref, ddt_ref, ddtb_ref, dalog_ref, ddsk_ref, dnw_ref, dh_scr):
        first = jnp.logical_and(pl.program_id(0) == 0, pl.program_id(1) == 0)

        @pl.when(first)
        def _():
            ddtb_ref[...] = jnp.zeros_like(ddtb_ref)
            dalog_ref[...] = jnp.zeros_like(dalog_ref)
            ddsk_ref[...] = jnp.zeros_like(ddsk_ref)
            dnw_ref[...] = jnp.zeros_like(dnw_ref)

        @pl.when(pl.program_id(1) == 0)
        def _():
            dh_scr[...] = jnp.zeros_like(dh_scr)

        for g in range(N_GROUPS):
            w256 = pl.ds(256 * g, 256)
            w128 = pl.ds(LANES * g, LANES)
            group(xs_ref.at[:, w256], bm_ref.at[:, w128], cm_ref.at[:, w128], dt_ref.at[:, w128],
                  z_ref.at[:, w256], dtb_ref.at[g], alog_ref.at[g], dsk_ref.at[g], nw_ref.at[g],
                  ypre_ref.at[:, w256], hst_ref.at[g], dya_ref.at[:, w256],
                  dact_ref.at[:, w256], dact_ref.at[:, pl.ds(D_MODEL + LANES * g, LANES)],
                  dact_ref.at[:, pl.ds(D_MODEL + 512 + LANES * g, LANES)], dz_ref.at[:, w256], ddt_ref.at[:, w128],
                  ddtb_ref.at[g], dalog_ref.at[g], ddsk_ref.at[g], dnw_ref.at[g], dh_scr.at[g])

    def group(xs_ref, bm_ref, cm_ref, dt_ref, z_ref, dtb_ref, alog_ref, dsk_ref, nw_ref, ypre_ref, hst_ref,
              dya_ref, dxs_ref, dbm_ref, dcm_ref, dz_ref, ddt_ref, ddtb_ref, dalog_ref, ddsk_ref, dnw_ref,
              dh_scr):
        row, lane, causal, dtv, a_row, acum, acum_t = _ssd_common(dt_ref, dtb_ref, alog_ref)
        lane1 = _iota_row()
        bb = bm_ref[...].astype(BF16)
        cb = cm_ref[...].astype(BF16)
        cbm = _dot_nt(cb, bb)

        z = z_ref[...].astype(F32)
        ypre = ypre_ref[...]
        dya = dya_ref[...]
        sz = _sigmoid(z)
        silu = z * sz
        yg = ypre * silu
        rstd = lax.rsqrt(jnp.sum(yg * yg, axis=1, keepdims=True) * (1.0 / 256.0) + NORM_EPS)
        dnw_ref[...] += jnp.sum(dya * yg * rstd, axis=0, keepdims=True)
        dn = dya * nw_ref[...]
        dyg = rstd * dn - yg * (rstd * rstd * rstd * (1.0 / 256.0)) * jnp.sum(dn * yg, axis=1, keepdims=True)
        dz_ref[...] = (dyg * ypre * (sz * (1.0 + z * (1.0 - sz)))).astype(BF16)
        dy_all = dyg * silu

        dskv = dsk_ref[...]
        da_cols = jnp.zeros((CHUNK, LANES), F32)
        dxt_cols = jnp.zeros((CHUNK, LANES), F32)
        ddsk_row = jnp.zeros((1, LANES), F32)
        dcb = jnp.zeros((CHUNK, CHUNK), F32)
        dc = jnp.zeros((CHUNK, SSM_STATE), F32)
        db = jnp.zeros((CHUNK, SSM_STATE), F32)
        last = _iota_col() == CHUNK - 1
        for pp in range(2):
            r0, r1 = 2 * pp, 2 * pp + 1
            x = xs_ref[:, LANES * pp:LANES * (pp + 1)]
            dy = dy_all[:, LANES * pp:LANES * (pp + 1)]
            lo, dtp, xd, d0, d1, al0, al1, eac, dsp, eal = _ssd_pair(pp, x, dtv, acum, acum_t, causal, lane, row)
            w0, w1 = cbm * d0, cbm * d1
            w0b, w1b = w0.astype(BF16), w1.astype(BF16)
            xdb = xd.astype(BF16)
            dyb = dy.astype(BF16)
            h = hst_ref[pp]
            dhn = dh_scr[pp]
            hb = h.astype(BF16)
            dhb = dhn.astype(BF16)
            g0 = _dot_nt(jnp.where(lo, dy, 0.0).astype(BF16), xdb)
            g1 = _dot_nt(jnp.where(lo, 0.0, dy).astype(BF16), xdb)
            dcb = dcb + g0 * d0 + g1 * d1
            m0, m1 = g0 * w0, g1 * w1
            bdh = _dot_nt(bb, dhb)
            dxd = jnp.where(lo, _dot_tn(w0b, dyb), _dot_tn(w1b, dyb)) + dsp * bdh
            ch = _dot_nt(cb, hb)
            edy = eac * dy
            edyb = edy.astype(BF16)
            xds = xd * dsp
            dc = dc + _dot(edyb, hb)
            db = db + _dot(xds.astype(BF16), dhb)
            dh_scr[pp] = dhn * eal + _dot_tn(edyb, cb)
            t2 = edy * ch
            t3 = xds * bdh
            dhh = dhn * h
            s4_0 = jnp.sum(jnp.sum(jnp.where(row < HEAD_DIM, dhh, 0.0), axis=0, keepdims=True), axis=1, keepdims=True)
            s4_1 = jnp.sum(jnp.sum(dhh, axis=0, keepdims=True), axis=1, keepdims=True) - s4_0
            t23 = t2 - t3
            t23_0 = jnp.sum(jnp.where(lo, t23, 0.0), axis=1, keepdims=True)
            t23_1 = jnp.sum(t23, axis=1, keepdims=True) - t23_0
            c3 = jnp.sum(t3, axis=0, keepdims=True)
            c3_0 = jnp.sum(jnp.where(_iota_row() < HEAD_DIM, c3, 0.0), axis=1, keepdims=True)
            c3_1 = jnp.sum(c3, axis=1, keepdims=True) - c3_0
            dal0 = c3_0 + jnp.exp(al0) * s4_0
            dal1 = c3_1 + jnp.exp(al1) * s4_1
            dac0 = jnp.sum(m0 - m0.T, axis=1, keepdims=True) + t23_0 + jnp.where(last, dal0, 0.0)
            dac1 = jnp.sum(m1 - m1.T, axis=1, keepdims=True) + t23_1 + jnp.where(last, dal1, 0.0)
            da_cols = da_cols + jnp.where(lane == r0, dac0, 0.0) + jnp.where(lane == r1, dac1, 0.0)
            xx = dxd * x
            x0 = jnp.sum(jnp.where(lo, xx, 0.0), axis=1, keepdims=True)
            x1 = jnp.sum(xx, axis=1, keepdims=True) - x0
            dxt_cols = dxt_cols + jnp.where(lane == r0, x0, 0.0) + jnp.where(lane == r1, x1, 0.0)
            dskp = jnp.where((_iota_row() < HEAD_DIM), _col(dskv, r0), _col(dskv, r1))
            dxs_ref[:, LANES * pp:LANES * (pp + 1)] = dxd * dtp + dy * dskp
            yx = jnp.sum(dy * x, axis=0, keepdims=True)
            k0 = jnp.sum(jnp.where((_iota_row() < HEAD_DIM), yx, 0.0), axis=1, keepdims=True)
            k1 = jnp.sum(yx, axis=1, keepdims=True) - k0
            ddsk_row = ddsk_row + jnp.where(lane1 == r0, k0, 0.0) + jnp.where(lane1 == r1, k1, 0.0)
        dcbb = dcb.astype(BF16)
        dcm_ref[...] = dc + _dot(dcbb, bb)
        dbm_ref[...] = db + _dot_tn(dcbb, cb)
        tri_t = (row <= lane).astype(F32)
        dadt = _dot(tri_t, da_cols, precision=HIGHEST)
        ddtv = dadt * a_row + dxt_cols
        dalog_ref[...] += jnp.sum(dadt * dtv, axis=0, keepdims=True) * a_row
        ddt_raw = ddtv * _sigmoid(dt_ref[...] + dtb_ref[...])
        ddt_ref[...] = ddt_raw.astype(BF16)
        ddtb_ref[...] += jnp.sum(ddt_raw, axis=0, keepdims=True)
        ddsk_ref[...] += ddsk_row

    g = N_GROUPS
    rc = lambda c: nc - 1 - c
    par = pl.BlockSpec((g, 1, LANES), lambda i, c: (0, 0, 0))
    parw = pl.BlockSpec((g, 1, 256), lambda i, c: (0, 0, 0))
    wide = pl.BlockSpec((None, CHUNK, D_MODEL), lambda i, c: (i, rc(c), 0))
    blk512 = lambda col: pl.BlockSpec((None, CHUNK, 512), lambda i, c: (i, rc(c), col))
    return pl.pallas_call(
        body, name=name, grid=(b, nc),
        in_specs=[wide, blk512(2), blk512(3), blk512(dt0),
                  pl.BlockSpec((None, CHUNK, D_MODEL), lambda i, c: (i, rc(c), z0)),
                  par, par, par, parw,
                  wide,
                  pl.BlockSpec((None, None, g, 2, CHUNK, SSM_STATE), lambda i, c: (i, rc(c), 0, 0, 0, 0)),
                  wide],
        out_specs=[pl.BlockSpec((None, CHUNK, CONV_DIM), lambda i, c: (i, rc(c), 0)), wide, blk512(0),
                   par, par, par, parw],
        out_shape=[jax.ShapeDtypeStruct((b, s, CONV_DIM), F32), jax.ShapeDtypeStruct((b, s, D_MODEL), BF16),
                   jax.ShapeDtypeStruct((b, s, 512), BF16),
                   jax.ShapeDtypeStruct((g, 1, LANES), F32), jax.ShapeDtypeStruct((g, 1, LANES), F32),
                   jax.ShapeDtypeStruct((g, 1, LANES), F32), jax.ShapeDtypeStruct((g, 1, 256), F32)],
        scratch_shapes=[pltpu.VMEM((g, 2, CHUNK, SSM_STATE), F32)],
        compiler_params=_cp(("arbitrary", "arbitrary")),
    )(xact3, xact3, xact3, gates3, proj3, dtb, alog, dsk, nw, ypre3, hst, dya3)


_FGATE_ROWS = 512


def _fgate_fwd(gates3, fb, *, name):
    b, s, _ = gates3.shape
    rows = min(_FGATE_ROWS, s)
    f0 = _PAD_COLS["a_dt"][1] // LANES

    def body(f_ref, fb_ref, cum_ref, carry):
        @pl.when(pl.program_id(1) == 0)
        def _():
            carry[...] = jnp.zeros_like(carry)

        row = lax.broadcasted_iota(jnp.int32, (rows, rows), 0)
        lane = lax.broadcasted_iota(jnp.int32, (rows, rows), 1)
        tri = (row >= lane).astype(F32)
        lf = -_softplus(-(f_ref[...] + fb_ref[...]))
        cs = _dot(tri, lf, precision=HIGHEST) + carry[0:1, :]
        cum_ref[...] = cs
        carry[0:1, :] = _row(cs, rows - 1)

    return pl.pallas_call(
        body, name=name, grid=(b, s // rows),
        in_specs=[pl.BlockSpec((None, rows, LANES), lambda i, c: (i, c, f0)),
                  pl.BlockSpec((1, LANES), lambda i, c: (0, 0))],
        out_specs=pl.BlockSpec((None, rows, LANES), lambda i, c: (i, c, 0)),
        out_shape=jax.ShapeDtypeStruct((b, s, LANES), F32),
        scratch_shapes=[pltpu.VMEM((8, LANES), F32)],
        compiler_params=_cp(("parallel", "arbitrary")),
    )(gates3, fb)


def _fgate_bwd(gates3, fb, dcum, *, name):
    b, s, _ = gates3.shape
    rows = min(_FGATE_ROWS, s)
    nc = s // rows
    f0 = _PAD_COLS["a_dt"][1] // LANES
    npair = dcum.shape[1]

    def body(f_ref, fb_ref, dc_ref, df_ref, dfb_ref, carry):
        first = jnp.logical_and(pl.program_id(0) == 0, pl.program_id(1) == 0)

        @pl.when(first)
        def _():
            dfb_ref[...] = jnp.zeros_like(dfb_ref)

        @pl.when(pl.program_id(1) == 0)
        def _():
            carry[...] = jnp.zeros_like(carry)

        row = lax.broadcasted_iota(jnp.int32, (rows, rows), 0)
        lane = lax.broadcasted_iota(jnp.int32, (rows, rows), 1)
        tri_t = (row <= lane).astype(F32)
        dc = -jnp.sum(dc_ref[...], axis=0)
        dlf = _dot(tri_t, dc, precision=HIGHEST) + carry[0:1, :]
        carry[0:1, :] = _row(dlf, 0)
        df = dlf * _sigmoid(-(f_ref[...] + fb_ref[...]))
        df_ref[...] = df.astype(BF16)
        dfb_ref[...] += jnp.sum(df, axis=0, keepdims=True)

    return pl.pallas_call(
        body, name=name, grid=(b, nc),
        in_specs=[pl.BlockSpec((None, rows, LANES), lambda i, c: (i, nc - 1 - c, f0)),
                  pl.BlockSpec((1, LANES), lambda i, c: (0, 0)),
                  pl.BlockSpec((None, npair, rows, LANES), lambda i, c: (i, 0, nc - 1 - c, 0))],
        out_specs=[pl.BlockSpec((None, rows, LANES), lambda i, c: (i, nc - 1 - c, 0)),
                   pl.BlockSpec((1, LANES), lambda i, c: (0, 0))],
        out_shape=[jax.ShapeDtypeStruct((b, s, LANES), BF16), jax.ShapeDtypeStruct((1, LANES), F32)],
        scratch_shapes=[pltpu.VMEM((8, LANES), F32)],
        compiler_params=_cp(("arbitrary", "arbitrary")),
    )(gates3, fb, dcum)


_SCALE = HEAD_DIM ** -0.5
_NEG = -1e30


_ST_LSE, _ST_DELTA, _ST_MJ = 0, 2, 8


_SR = 40


def _ck_rep(cum):
    b, s, _ = cum.shape
    t = jnp.transpose(cum[:, :, :N_HEADS], (0, 2, 1)).reshape(b, N_HEADS // 2, 2, s, 1)
    return jnp.broadcast_to(t, (b, N_HEADS // 2, 2, s, LANES))


def _foxt_fwd(proj3, ckrep, *, name, tb):
    b, s, _ = proj3.shape
    nq = s // tb
    assert _ST_MJ + 2 * nq <= _SR
    q0 = _PAD_COLS["c_q"][0] // LANES
    k0 = _PAD_COLS["c_k"][0] // LANES
    v0 = _PAD_COLS["c_v"][0] // LANES
    z0 = _PAD_COLS["c_z"][0] // LANES
    rep = tb // LANES

    def body(q_ref, k_ref, v_ref, z_ref, ck_ref, y_ref, o_ref, st_ref):
        i = pl.program_id(2)
        lane = lax.broadcasted_iota(jnp.int32, (tb, LANES), 1)
        lo = lane < HEAD_DIM
        lo_r = lax.broadcasted_iota(jnp.int32, (LANES, tb), 0) < HEAD_DIM
        srow = lax.broadcasted_iota(jnp.int32, (_SR, tb), 0)
        q = q_ref[...].astype(F32) * _SCALE
        qms = (jnp.where(lo, q, 0.0).astype(BF16), jnp.where(lo, 0.0, q).astype(BF16))
        ones_at = (HEAD_DIM, 0)

        def block(j, carry, diagonal):
            ks = pl.ds(pl.multiple_of(j * tb, tb), tb)
            kb = k_ref[ks, :].astype(BF16)
            v = v_ref[ks, :].astype(F32)
            vts = (jnp.where(lo, v, jnp.where(lane == ones_at[0], 1.0, 0.0)).T.astype(BF16),
                   jnp.where(lo, jnp.where(lane == ones_at[1], 1.0, 0.0), v).T.astype(BF16))
            if diagonal:
                key = lax.broadcasted_iota(jnp.int32, (tb, tb), 0)
                qry = lax.broadcasted_iota(jnp.int32, (tb, tb), 1)
                mask = key <= qry
            ms, ls, acc, st = carry
            new_m, new_l, pvs, alphas = [], [], [], []
            for hh in range(2):
                sc = _dot_nt(kb, qms[hh]) - jnp.tile(ck_ref[hh, ks, :], (1, rep))
                if diagonal:
                    sc = jnp.where(mask, sc, _NEG)
                m_new = jnp.maximum(ms[hh], jnp.max(sc, axis=0, keepdims=True))
                alpha = jnp.exp(ms[hh] - m_new)
                pv = _dot(vts[hh], jnp.exp(sc - m_new).astype(BF16))
                rs = _row(pv[ones_at[hh]:ones_at[hh] + 8, :], 0)
                new_l.append(alpha * ls[hh] + rs)
                new_m.append(m_new)
                pvs.append(pv)
                alphas.append(alpha)
                st = jnp.where(srow == _ST_MJ + 2 * j + hh, m_new, st)
            acc = jnp.where(lo_r, alphas[0] * acc + pvs[0], alphas[1] * acc + pvs[1])
            return (tuple(new_m), tuple(new_l), acc, st)

        neg = jnp.full((1, tb), _NEG, F32)
        zero = jnp.zeros((1, tb), F32)
        init = ((neg, neg), (zero, zero), jnp.zeros((LANES, tb), F32), jnp.zeros((_SR, tb), F32))
        carry = lax.fori_loop(0, i, lambda j, c: block(j, c, False), init)
        ms, ls, acc, st = block(i, carry, True)
        o = (acc / jnp.where(lo_r, ls[0], ls[1])).T
        o_ref[...] = o
        st = jnp.where(srow == _ST_LSE, ms[0] + jnp.log(ls[0]), st)
        st_ref[...] = jnp.where(srow == _ST_LSE + 1, ms[1] + jnp.log(ls[1]), st)
        z = z_ref[...].astype(F32)
        y_ref[...] = (o * (z * _sigmoid(z))).astype(BF16)

    qspec = lambda c0: pl.BlockSpec((None, tb, LANES), lambda bi, p, i: (bi, i, c0 + p))
    kspec = lambda c0: pl.BlockSpec((None, s, LANES), lambda bi, p, i: (bi, 0, c0 + p))
    ospec = pl.BlockSpec((None, tb, LANES), lambda bi, p, i: (bi, i, p))
    return pl.pallas_call(
        body, name=name, grid=(b, N_HEADS // 2, nq),
        in_specs=[qspec(q0), kspec(k0), kspec(v0), qspec(z0),
                  pl.BlockSpec((None, None, 2, s, LANES), lambda bi, p, i: (bi, p, 0, 0, 0))],
        out_specs=[ospec, ospec, pl.BlockSpec((None, None, None, _SR, tb), lambda bi, p, i: (bi, p, i, 0, 0))],
        out_shape=[jax.ShapeDtypeStruct((b, s, D_MODEL), BF16), jax.ShapeDtypeStruct((b, s, D_MODEL), F32),
                   jax.ShapeDtypeStruct((b, N_HEADS // 2, nq, _SR, tb), F32)],
        compiler_params=_cp(("parallel", "parallel", "arbitrary")),
    )(proj3, proj3, proj3, proj3, ckrep)


def _foxt_prep(proj3, o3, stat, dy3, *, name, tb):
    b, s, _ = proj3.shape
    nq = s // tb
    z0 = _PAD_COLS["c_z"][0] // LANES

    def body(z_ref, o_ref, fst_ref, dy_ref, dz_ref, do_ref, st_ref):
        z = z_ref[...].astype(F32)
        sz = _sigmoid(z)
        dy = dy_ref[...]
        o = o_ref[...]
        do = dy * (z * sz)
        dz_ref[...] = (dy * o * (sz * (1.0 + z * (1.0 - sz)))).astype(BF16)
        do_ref[...] = do
        doo = do.astype(BF16).astype(F32) * o
        r8 = lax.broadcasted_iota(jnp.int32, (8, LANES), 0)
        l8 = lax.broadcasted_iota(jnp.int32, (8, LANES), 1)
        pick = jnp.logical_or(jnp.logical_and(r8 == 0, l8 < HEAD_DIM),
                              jnp.logical_and(r8 == 1, l8 >= HEAD_DIM)).astype(F32)
        d8 = _dot(pick, doo, ((1,), (1,)), precision=HIGHEST)
        srow = lax.broadcasted_iota(jnp.int32, (_SR, tb), 0)
        st = jnp.where(srow == _ST_DELTA, _row(d8, 0), fst_ref[...])
        st_ref[...] = jnp.where(srow == _ST_DELTA + 1, _row(d8, 1), st)

    ospec = pl.BlockSpec((None, tb, LANES), lambda bi, p, i: (bi, i, p))
    sspec = pl.BlockSpec((None, None, None, _SR, tb), lambda bi, p, i: (bi, p, i, 0, 0))
    return pl.pallas_call(
        body, name=name, grid=(b, N_HEADS // 2, nq),
        in_specs=[pl.BlockSpec((None, tb, LANES), lambda bi, p, i: (bi, i, z0 + p)), ospec, sspec, ospec],
        out_specs=[ospec, ospec, sspec],
        out_shape=[jax.ShapeDtypeStruct((b, s, D_MODEL), BF16), jax.ShapeDtypeStruct((b, s, D_MODEL), F32),
                   jax.ShapeDtypeStruct((b, N_HEADS // 2, nq, _SR, tb), F32)],
        compiler_params=_cp(("parallel", "parallel", "parallel")),
    )(proj3, o3, stat, dy3)


def _foxt_bwd(proj3, ckrep, do3, stats, *, name, tb):
    b, s, _ = proj3.shape
    nq = s // tb
    q0 = _PAD_COLS["c_q"][0] // LANES
    k0 = _PAD_COLS["c_k"][0] // LANES
    v0 = _PAD_COLS["c_v"][0] // LANES
    rep = tb // LANES

    def body(q_ref, do_ref, st_ref, k_ref, v_ref, ck_ref, dq_ref, dk_ref, dv_ref, cs_ref):
        j = pl.program_id(2)
        lane = lax.broadcasted_iota(jnp.int32, (tb, LANES), 1)
        lo = lane < HEAD_DIM
        lo_r = lax.broadcasted_iota(jnp.int32, (LANES, tb), 0) < HEAD_DIM

        @pl.when(j == 0)
        def _():
            dq_ref[...] = jnp.zeros_like(dq_ref)

        kf = k_ref[...].astype(F32)
        kb = kf.astype(BF16)
        kt = kf.T.astype(BF16)
        vb = v_ref[...].astype(BF16)
        cks = (jnp.tile(ck_ref[0], (1, rep)), jnp.tile(ck_ref[1], (1, rep)))

        def block(i, carry, diagonal):
            qs = pl.ds(pl.multiple_of(i * tb, tb), tb)
            q = q_ref[qs, :].astype(F32) * _SCALE
            do = do_ref[qs, :]
            st = st_ref[i]
            if diagonal:
                key = lax.broadcasted_iota(jnp.int32, (tb, tb), 0)
                qry = lax.broadcasted_iota(jnp.int32, (tb, tb), 1)
                mask = key <= qry
            dk, dv, cs = carry
            new_cs, dqs = [], []
            for hh in range(2):
                sel = lo if hh == 0 else jnp.logical_not(lo)
                qm = jnp.where(sel, q, 0.0).astype(BF16)
                dom = jnp.where(sel, do, 0.0).astype(BF16)
                sc = _dot_nt(kb, qm) - cks[hh]
                if diagonal:
                    sc = jnp.where(mask, sc, _NEG)
                mj = _row(st, _ST_MJ + 2 * j + hh)
                w = jnp.exp(mj - _row(st, _ST_LSE + hh))
                ph = jnp.exp(sc - mj).astype(BF16).astype(F32) * w
                ds = ph * (_dot_nt(vb, dom) - _row(st, _ST_DELTA + hh))
                dsb = ds.astype(BF16)
                dv = dv + _dot(ph.astype(BF16), dom)
                dk = dk + _dot(dsb, qm)
                new_cs.append(cs[hh] + jnp.sum(ds, axis=1, keepdims=True))
                dqs.append(_dot(kt, dsb))
            dq_ref[i] += jnp.where(lo_r, dqs[0], dqs[1]) * _SCALE
            return (dk, dv, tuple(new_cs))

        zcol = jnp.zeros((tb, 1), F32)
        init = (jnp.zeros((tb, LANES), F32), jnp.zeros((tb, LANES), F32), (zcol, zcol))
        carry = block(j, init, True)
        dk, dv, cs = lax.fori_loop(j + 1, nq, lambda i, c: block(i, c, False), carry)
        dk_ref[...] = dk.astype(BF16)
        dv_ref[...] = dv.astype(BF16)
        p2 = 2 * pl.program_id(1)
        cs_ref[...] = jnp.where(lane == p2, cs[0], jnp.where(lane == p2 + 1, cs[1], 0.0))

    full = lambda c0: pl.BlockSpec((None, s, LANES), lambda bi, p, j: (bi, 0, c0 + p))
    kspec = lambda c0: pl.BlockSpec((None, tb, LANES), lambda bi, p, j: (bi, j, c0 + p))
    ko = pl.BlockSpec((None, tb, LANES), lambda bi, p, j: (bi, j, p))
    sall = pl.BlockSpec((None, None, nq, _SR, tb), lambda bi, p, j: (bi, p, 0, 0, 0))
    dqspec = pl.BlockSpec((None, None, nq, LANES, tb), lambda bi, p, j: (bi, p, 0, 0, 0))
    return pl.pallas_call(
        body, name=name, grid=(b, N_HEADS // 2, nq),
        in_specs=[full(q0), full(0), sall, kspec(k0), kspec(v0),
                  pl.BlockSpec((None, None, 2, tb, LANES), lambda bi, p, j: (bi, p, 0, j, 0))],
        out_specs=[dqspec, ko, ko, pl.BlockSpec((None, None, tb, LANES), lambda bi, p, j: (bi, p, j, 0))],
        out_shape=[jax.ShapeDtypeStruct((b, N_HEADS // 2, nq, LANES, tb), F32),
                   jax.ShapeDtypeStruct((b, s, D_MODEL), BF16), jax.ShapeDtypeStruct((b, s, D_MODEL), BF16),
                   jax.ShapeDtypeStruct((b, N_HEADS // 2, s, LANES), F32)],
        compiler_params=_cp(("parallel", "parallel", "arbitrary")),
    )(proj3, do3, stats, proj3, proj3, ckrep)


def _rope(x, cos, sin_signed):
    w = x.shape[1]
    lane = lax.broadcasted_iota(jnp.int32, x.shape, 1)
    first = (lane % HEAD_DIM) < (HEAD_DIM // 2)
    rot = jnp.where(first, pltpu.roll(x, w - HEAD_DIM // 2, 1), pltpu.roll(x, HEAD_DIM // 2, 1))
    return x * cos + rot * sin_signed


_QB = 4
_QROWS = _QB * CHUNK


def _swa_keys(g, kc_ref, kp_ref, vc_ref, vp_ref, cq_ref, sq_ref, cp_ref, sp_ref):
    def both_halves(x):
        x = x.astype(F32)
        lane = lax.broadcasted_iota(jnp.int32, x.shape, 1)
        keep = (lane // HEAD_DIM) == (g % 2)
        return jnp.where(keep, x, pltpu.roll(x, HEAD_DIM, 1))

    cq, sq, cpv, spv = cq_ref[...], sq_ref[...], cp_ref[...], sp_ref[...]
    kc = _rope(both_halves(kc_ref[...]), cq, sq).astype(BF16)
    kp = _rope(both_halves(kp_ref[...]), cpv, spv).astype(BF16)
    return cq, sq, cpv, spv, kc, kp, both_halves(vc_ref[...]).astype(BF16), both_halves(vp_ref[...]).astype(BF16)


def _swa_stack(pairs, lo):
    return jnp.concatenate([jnp.where(lo, pairs[0], 0.0), jnp.where(lo, 0.0, pairs[0]),
                            jnp.where(lo, pairs[1], 0.0), jnp.where(lo, 0.0, pairs[1])], axis=0).astype(BF16)


def _swa_mask4(prev_valid):
    r = lax.broadcasted_iota(jnp.int32, (4 * CHUNK, 2 * CHUNK), 0) & (CHUNK - 1)
    c = lax.broadcasted_iota(jnp.int32, (4 * CHUNK, 2 * CHUNK), 1)
    own = jnp.logical_and(c >= CHUNK, c - CHUNK <= r)
    before = jnp.logical_and(c < CHUNK, c > r)
    if prev_valid is True:
        return jnp.logical_or(own, before)
    return jnp.logical_or(own, jnp.logical_and(before, prev_valid))


def _swa_sink4(skv):
    return jnp.concatenate([jnp.broadcast_to(_col(skv, j), (CHUNK, 1)) for j in range(4)], axis=0)


def _swa_specs(order):
    def spec(shape, fn):
        return pl.BlockSpec(shape, lambda *ids: fn(*order(*ids)))

    q0 = _PAD_COLS["b_q"][0] // 256
    z0 = _PAD_COLS["b_z"][0] // 256
    k0 = _PAD_COLS["b_k"][0] // LANES
    v0 = _PAD_COLS["b_v"][0] // LANES
    prev = lambda i: jnp.maximum(_QB * i - 1, 0)
    return dict(
        kc=spec((None, _QROWS, LANES), lambda bi, g, i: (bi, i, k0 + g // 2)),
        kp=spec((None, CHUNK, LANES), lambda bi, g, i: (bi, prev(i), k0 + g // 2)),
        vc=spec((None, _QROWS, LANES), lambda bi, g, i: (bi, i, v0 + g // 2)),
        vp=spec((None, CHUNK, LANES), lambda bi, g, i: (bi, prev(i), v0 + g // 2)),
        q=spec((None, _QROWS, 256), lambda bi, g, i: (bi, i, q0 + g)),
        z=spec((None, _QROWS, 256), lambda bi, g, i: (bi, i, z0 + g)),
        blk=spec((None, _QROWS, 256), lambda bi, g, i: (bi, i, g)),
        kcur=spec((None, _QROWS, LANES), lambda bi, g, i: (bi, i, g)),
        kstep=spec((None, CHUNK, LANES), lambda bi, g, i: (bi, i, g)),
        tcur=spec((_QROWS, LANES), lambda bi, g, i: (i, 0)),
        tprev=spec((CHUNK, LANES), lambda bi, g, i: (prev(i), 0)),
        sk=spec((None, 1, LANES), lambda bi, g, i: (g, 0, 0)))


def _swa_fwd(proj3, cos, sin, sinks, *, name):
    b, s, _ = proj3.shape

    def body(q_ref, z_ref, kc_ref, kp_ref, vc_ref, vp_ref, cq_ref, sq_ref, cp_ref, sp_ref, sk_ref,
             y_ref, o_ref, lse_ref):
        i = pl.program_id(2)
        cq_all, sq_all, _, _, kc_all, kp0, vc_all, vp0 = _swa_keys(
            pl.program_id(1), kc_ref, kp_ref, vc_ref, vp_ref, cq_ref, sq_ref, cp_ref, sp_ref)
        lo = lax.broadcasted_iota(jnp.int32, (CHUNK, LANES), 1) < HEAD_DIM
        sink4 = _swa_sink4(sk_ref[...])
        for u in range(_QB):
            rs = slice(CHUNK * u, CHUNK * (u + 1))
            ps = slice(CHUNK * (u - 1), CHUNK * u)
            cq, sq = cq_all[rs], sq_all[rs]
            kp, vp = (kp0, vp0) if u == 0 else (kc_all[ps], vc_all[ps])
            kk = jnp.concatenate([kp, kc_all[rs]], axis=0)
            vv = jnp.concatenate([vp, vc_all[rs]], axis=0)
            q4 = _swa_stack([_rope(q_ref[rs, LANES * pp:LANES * (pp + 1)].astype(F32), cq, sq) * _SCALE
                             for pp in range(2)], lo)
            sc = jnp.where(_swa_mask4(True if u > 0 else i > 0), _dot_nt(q4, kk), _NEG)
            m = jnp.maximum(jnp.max(sc, axis=1, keepdims=True), sink4)
            pr = jnp.exp(sc - m)
            l = jnp.sum(pr, axis=1, keepdims=True) + jnp.exp(sink4 - m)
            o4 = _dot(pr.astype(BF16), vv) / l
            lse4 = m + jnp.log(l)
            for pp in range(2):
                ls = slice(LANES * pp, LANES * (pp + 1))
                h0 = slice(2 * CHUNK * pp, 2 * CHUNK * pp + CHUNK)
                h1 = slice(2 * CHUNK * pp + CHUNK, 2 * CHUNK * (pp + 1))
                o = jnp.where(lo, o4[h0], o4[h1])
                z = z_ref[rs, ls].astype(F32)
                o_ref[rs, ls] = o
                lse_ref[rs, ls] = jnp.where(lo, lse4[h0], lse4[h1])
                y_ref[rs, ls] = (o * (z * _sigmoid(z))).astype(BF16)

    sp = _swa_specs(lambda bi, g, i: (bi, g, i))
    return pl.pallas_call(
        body, name=name, grid=(b, N_GROUPS, s // _QROWS),
        in_specs=[sp["q"], sp["z"], sp["kc"], sp["kp"], sp["vc"], sp["vp"],
                  sp["tcur"], sp["tcur"], sp["tprev"], sp["tprev"], sp["sk"]],
        out_specs=[sp["blk"], sp["blk"], sp["blk"]],
        out_shape=[jax.ShapeDtypeStruct((b, s, D_MODEL), BF16)] + [jax.ShapeDtypeStruct((b, s, D_MODEL), F32)] * 2,
        compiler_params=_cp(("parallel", "parallel", "parallel")),
    )(proj3, proj3, proj3, proj3, proj3, proj3, cos, sin, cos, sin, sinks)


def _swa_bwd(proj3, cos, sin, sinks, o3, lse3, dy3, *, name):
    b, s, _ = proj3.shape

    def body(q_ref, z_ref, kc_ref, kp_ref, vc_ref, vp_ref, cq_ref, sq_ref, cp_ref, sp_ref, sk_ref,
             o_ref, lse_ref, dy_ref, dq_ref, dz_ref, dkc_ref, dkp_ref, dvc_ref, dvp_ref, dsk_ref):
        i = pl.program_id(2)
        first = jnp.logical_and(pl.program_id(1) == 0, i == 0)

        @pl.when(first)
        def _():
            dsk_ref[...] = jnp.zeros_like(dsk_ref)

        cq_all, sq_all, cpv, spv, kc_all, kp0, vc_all, vp0 = _swa_keys(
            pl.program_id(0), kc_ref, kp_ref, vc_ref, vp_ref, cq_ref, sq_ref, cp_ref, sp_ref)
        lo = lax.broadcasted_iota(jnp.int32, (CHUNK, LANES), 1) < HEAD_DIM
        lane1 = lax.broadcasted_iota(jnp.int32, (1, LANES), 1)
        sink4 = _swa_sink4(sk_ref[...])
        zero = jnp.zeros((CHUNK, LANES), F32)
        dks = [zero] * (_QB + 1)
        dvs = [zero] * (_QB + 1)
        dsk_row = jnp.zeros((1, LANES), F32)
        for u in range(_QB):
            rs = slice(CHUNK * u, CHUNK * (u + 1))
            ps = slice(CHUNK * (u - 1), CHUNK * u)
            cq, sq = cq_all[rs], sq_all[rs]
            kp, vp = (kp0, vp0) if u == 0 else (kc_all[ps], vc_all[ps])
            kk = jnp.concatenate([kp, kc_all[rs]], axis=0)
            vv = jnp.concatenate([vp, vc_all[rs]], axis=0)
            q4 = _swa_stack([_rope(q_ref[rs, LANES * pp:LANES * (pp + 1)].astype(F32), cq, sq) * _SCALE
                             for pp in range(2)], lo)
            dos, lses = [], []
            for pp in range(2):
                ls = slice(LANES * pp, LANES * (pp + 1))
                z = z_ref[rs, ls].astype(F32)
                sz = _sigmoid(z)
                dy = dy_ref[rs, ls]
                dos.append(dy * (z * sz))
                dz_ref[rs, ls] = (dy * o_ref[rs, ls] * (sz * (1.0 + z * (1.0 - sz)))).astype(BF16)
                lse = lse_ref[rs, ls]
                lses += [_col(lse, 0), _col(lse, HEAD_DIM)]
            do4 = _swa_stack(dos, lo)
            lse4 = jnp.concatenate(lses, axis=0)
            pr = jnp.exp(jnp.where(_swa_mask4(True if u > 0 else i > 0), _dot_nt(q4, kk), _NEG) - lse4)
            dp = _dot_nt(do4, vv)
            dl = jnp.sum(pr * dp, axis=1, keepdims=True)
            ds = (pr * (dp - dl)).astype(BF16)
            dsink = -jnp.exp(sink4 - lse4) * dl
            for j in range(4):
                dsk_row = dsk_row + jnp.where(
                    lane1 == j, jnp.sum(dsink[CHUNK * j:CHUNK * (j + 1)], axis=0, keepdims=True), 0.0)
            dq4 = _dot(ds, kk)
            dkk = _dot_tn(ds, q4)
            dvv = _dot_tn(pr.astype(BF16), do4)
            dks[u], dks[u + 1] = dks[u] + dkk[:CHUNK], dks[u + 1] + dkk[CHUNK:]
            dvs[u], dvs[u + 1] = dvs[u] + dvv[:CHUNK], dvs[u + 1] + dvv[CHUNK:]
            for pp in range(2):
                h0 = slice(2 * CHUNK * pp, 2 * CHUNK * pp + CHUNK)
                h1 = slice(2 * CHUNK * pp + CHUNK, 2 * CHUNK * (pp + 1))
                dq_ref[rs, LANES * pp:LANES * (pp + 1)] = _rope(
                    jnp.where(lo, dq4[h0], dq4[h1]) * _SCALE, cq, -sq).astype(BF16)
        fold = lambda v: v + pltpu.roll(v, HEAD_DIM, 1)
        dkp_ref[...] = fold(_rope(dks[0], cpv, -spv))
        dvp_ref[...] = fold(dvs[0])
        for u in range(_QB):
            rs = slice(CHUNK * u, CHUNK * (u + 1))
            dkc_ref[rs, :] = fold(_rope(dks[u + 1], cq_all[rs], -sq_all[rs]))
            dvc_ref[rs, :] = fold(dvs[u + 1])
        dsk_ref[...] += dsk_row

    sp = _swa_specs(lambda g, bi, i: (bi, g, i))
    kv_shape = jax.ShapeDtypeStruct((b, s, 512), F32)
    kvp_shape = jax.ShapeDtypeStruct((b, s // _QB, 512), F32)
    return pl.pallas_call(
        body, name=name, grid=(N_GROUPS, b, s // _QROWS),
        in_specs=[sp["q"], sp["z"], sp["kc"], sp["kp"], sp["vc"], sp["vp"],
                  sp["tcur"], sp["tcur"], sp["tprev"], sp["tprev"], sp["sk"], sp["blk"], sp["blk"], sp["blk"]],
        out_specs=[sp["blk"], sp["blk"], sp["kcur"], sp["kstep"], sp["kcur"], sp["kstep"], sp["sk"]],
        out_shape=[jax.ShapeDtypeStruct((b, s, D_MODEL), BF16), jax.ShapeDtypeStruct((b, s, D_MODEL), BF16),
                   kv_shape, kvp_shape, kv_shape, kvp_shape, jax.ShapeDtypeStruct((N_GROUPS, 1, LANES), F32)],
        compiler_params=_cp(("arbitrary", "arbitrary", "arbitrary")),
    )(proj3, proj3, proj3, proj3, proj3, proj3, cos, sin, cos, sin, sinks, o3, lse3, dy3)


def _swa_fold(dkc, dkp, dvc, dvp, *, name):
    b, s, _ = dkc.shape
    ns = s // _QROWS

    def body(kc_ref, kp_ref, vc_ref, vp_ref, dk_ref, dv_ref):
        has_next = pl.program_id(1) < ns - 1
        lo = lax.broadcasted_iota(jnp.int32, (_QROWS, LANES), 1) < HEAD_DIM
        row = lax.broadcasted_iota(jnp.int32, (_QROWS, 512), 0)
        last_block = jnp.logical_and(row >= _QROWS - CHUNK, has_next)
        for cur, nxt, out in ((kc_ref, kp_ref, dk_ref), (vc_ref, vp_ref, dv_ref)):
            tot = cur[...] + jnp.where(last_block, jnp.tile(nxt[...], (_QB, 1)), 0.0)
            for j in range(2):
                out[:, LANES * j:LANES * (j + 1)] = jnp.where(
                    lo, tot[:, 256 * j:256 * j + LANES], tot[:, 256 * j + LANES:256 * (j + 1)]).astype(BF16)

    cur = pl.BlockSpec((None, _QROWS, 512), lambda bi, i: (bi, i, 0))
    nxt = pl.BlockSpec((None, CHUNK, 512), lambda bi, i: (bi, jnp.minimum(i + 1, ns - 1), 0))
    out = pl.BlockSpec((None, _QROWS, 256), lambda bi, i: (bi, i, 0))
    sh = jax.ShapeDtypeStruct((b, s, 256), BF16)
    return pl.pallas_call(
        body, name=name, grid=(b, ns), in_specs=[cur, nxt, cur, nxt], out_specs=[out, out], out_shape=[sh, sh],
        compiler_params=_cp(("parallel", "parallel")),
    )(dkc, dkp, dvc, dvp)


def _branch_fwd(ys, proj, gb, wp, wo, x, *, name, tm=256):
    t = proj.shape[0]
    g0 = _PAD_COLS["gates"][0] // D_MODEL

    def body(g_ref, a_ref, b_ref, c_ref, gb_ref, wp_ref, wo_ref, x_ref, ba_ref, bb_ref, bc_ref, m_ref, xn_ref):
        acc = None
        for i, (y, br) in enumerate(((a_ref, ba_ref), (b_ref, bb_ref), (c_ref, bc_ref))):
            bri = _dot(y[...], wp_ref[i])
            br[...] = bri
            gate = _sigmoid(g_ref[:, D_MODEL * i:D_MODEL * (i + 1)].astype(F32) + gb_ref[i:i + 1, :])
            acc = gate * bri if acc is None else acc + gate * bri
        mb = acc.astype(BF16)
        m_ref[...] = mb
        xn_ref[...] = x_ref[...] + _dot(mb, wo_ref[...])

    row = pl.BlockSpec((tm, D_MODEL), lambda i: (i, 0))
    rowf = jax.ShapeDtypeStruct((t, D_MODEL), F32)
    outs = pl.pallas_call(
        body, name=name, grid=(t // tm,),
        in_specs=[pl.BlockSpec((tm, 3 * D_MODEL), lambda i: (i, g0)), row, row, row,
                  pl.BlockSpec((3, D_MODEL), lambda i: (0, 0)),
                  pl.BlockSpec((3, D_MODEL, D_MODEL), lambda i: (0, 0, 0)),
                  pl.BlockSpec((D_MODEL, D_MODEL), lambda i: (0, 0)), row],
        out_specs=[row, row, row, row, row],
        out_shape=[rowf, rowf, rowf, jax.ShapeDtypeStruct((t, D_MODEL), BF16), rowf],
        compiler_params=_cp(("parallel",)),
    )(proj, ys[0], ys[1], ys[2], gb, wp, wo, x)
    return outs[:3], outs[3], outs[4]


def _branch_bwd(dx, proj, br, gb, wp, wo, *, name, tm=256):
    t = proj.shape[0]
    g0 = _PAD_COLS["gates"][0] // D_MODEL

    def body(g_ref, a_ref, b_ref, c_ref, gb_ref, wp_ref, wo_ref, dx_ref,
             da_ref, db_ref, dc_ref, dg_ref, dgb_ref, ya_ref, yb_ref, yc_ref):
        @pl.when(pl.program_id(0) == 0)
        def _():
            dgb_ref[...] = jnp.zeros_like(dgb_ref)

        dmv = _dot_nt(dx_ref[...].astype(BF16), wo_ref[...])
        for i, (r, dr, dy) in enumerate(((a_ref, da_ref, ya_ref), (b_ref, db_ref, yb_ref), (c_ref, dc_ref, yc_ref))):
            gate = _sigmoid(g_ref[:, D_MODEL * i:D_MODEL * (i + 1)].astype(F32) + gb_ref[i:i + 1, :])
            dbr = (dmv * gate).astype(BF16)
            dr[...] = dbr
            dg = dmv * r[...] * gate * (1.0 - gate)
            dg_ref[:, D_MODEL * i:D_MODEL * (i + 1)] = dg.astype(BF16)
            dgb_ref[i:i + 1, :] += jnp.sum(dg, axis=0, keepdims=True)
            dy[...] = _dot_nt(dbr, wp_ref[i])

    row = pl.BlockSpec((tm, D_MODEL), lambda i: (i, 0))
    rowb = jax.ShapeDtypeStruct((t, D_MODEL), BF16)
    rowf = jax.ShapeDtypeStruct((t, D_MODEL), F32)
    outs = pl.pallas_call(
        body, name=name, grid=(t // tm,),
        in_specs=[pl.BlockSpec((tm, 3 * D_MODEL), lambda i: (i, g0)), row, row, row,
                  pl.BlockSpec((3, D_MODEL), lambda i: (0, 0)),
                  pl.BlockSpec((3, D_MODEL, D_MODEL), lambda i: (0, 0, 0)),
                  pl.BlockSpec((D_MODEL, D_MODEL), lambda i: (0, 0)), row],
        out_specs=[row, row, row, pl.BlockSpec((tm, 3 * D_MODEL), lambda i: (i, 0)),
                   pl.BlockSpec((8, D_MODEL), lambda i: (0, 0)), row, row, row],
        out_shape=[rowb, rowb, rowb, jax.ShapeDtypeStruct((t, 3 * D_MODEL), BF16),
                   jax.ShapeDtypeStruct((8, D_MODEL), F32), rowf, rowf, rowf],
        compiler_params=_cp(("arbitrary",)),
    )(proj, br[0], br[1], br[2], gb, wp, wo, dx)
    return outs[:3], outs[3], outs[4], outs[5:]


def _rope_tables(s):
    pos = jnp.arange(s, dtype=F32)
    inv_freq = ROPE_THETA ** (-jnp.arange(0, HEAD_DIM, 2, dtype=F32) / HEAD_DIM)
    ang = pos[:, None] * inv_freq[None, :]
    cos, sin = jnp.cos(ang), jnp.sin(ang)
    return jnp.tile(cos, (1, 4)), jnp.tile(jnp.concatenate([-sin, sin], axis=1), (1, 2))


def _layer_params(wl):
    return dict(
        dtb=_group_lanes(wl["dt_bias"]), alog=_group_lanes(wl["a_log"]), dsk=_group_lanes(wl["d_skip"]),
        nw=wl["ssm_norm_w"].reshape(N_GROUPS, 1, 256), sinks=_group_lanes(wl["sinks"]),
        fb=jnp.pad(wl["f_bias"], (0, LANES - N_HEADS)).reshape(1, LANES))


def _layer_fwd(x, wl, tabs, bsz, li, tb):
    t = x.shape[0]
    s = t // bsz
    cos, sin = tabs
    lp = _layer_params(wl)
    n = lambda k: f"l{li}_{k}"
    h, h_t = _rms_fwd(x, wl["norm_w"], name=n("rms_fwd"))
    proj = _mm(h, wl["w_in"], tm=1024, tn=1536, tk=1024, out_dtype=BF16, name=n("mm_proj"))
    proj3 = proj.reshape(bsz, s, N_PAD)
    g0, gw = _PAD_COLS["a_dt"][0], _PAD_COLS["a_dt"][1] + _PAD_COLS["c_f"][1]
    gates3 = _mm(h, wl["w_in"][:, g0:g0 + gw], tm=1024, tn=gw, tk=1024, name=n("mm_gates")).reshape(bsz, s, gw)
    xact3 = _conv_fwd(proj3, wl["conv_w"], wl["conv_b"], name=n("conv_fwd"))
    ya3, ypre3, hst = _ssd_fwd(proj3, gates3, xact3, lp["dtb"], lp["alog"], lp["dsk"], lp["nw"], name=n("ssd_fwd"))
    yb3, ob3, lseb3 = _swa_fwd(proj3, cos, sin, lp["sinks"], name=n("swa_fwd"))
    cum = _fgate_fwd(gates3, lp["fb"], name=n("fgate_fwd"))
    cum_t = _ck_rep(cum)
    yc3, oc3, statc3 = _foxt_fwd(proj3, cum_t, name=n("fox_fwd"), tb=tb)
    ys = [v.reshape(t, D_MODEL) for v in (ya3, yb3, yc3)]
    br, merged, x_new = _branch_fwd(ys, proj, wl["gate_bias"], wl["w_proj"], wl["w_out"], x, name=n("branch_fwd"))
    saved = dict(x=x, h_t=h_t, proj=proj, gates3=gates3, xact3=xact3, ypre3=ypre3, hst=hst, ob3=ob3, lseb3=lseb3,
                 cum_t=cum_t, oc3=oc3, statc3=statc3, ys=ys, br=br, merged=merged, lp=lp)
    return x_new, saved


def _layer_bwd(dx, wl, sv, tabs, bsz, li, tb):
    t = dx.shape[0]
    s = t // bsz
    cos, sin = tabs
    lp = sv["lp"]
    n = lambda k: f"l{li}_{k}"
    proj = sv["proj"]
    proj3 = proj.reshape(bsz, s, N_PAD)
    g = {}
    g["w_out"] = _mm(sv["merged"], dx, ta=True, tm=1024, tn=1024, tk=512, name=n("mm_dwout"))
    dbr, dgates, dgb, dys = _branch_bwd(dx, proj, sv["br"], wl["gate_bias"], wl["w_proj"], wl["w_out"],
                                        name=n("branch_bwd"))
    g["gate_bias"] = dgb[:3]
    g["w_proj"] = jnp.stack([_mm(sv["ys"][i], dbr[i], ta=True, tm=1024, tn=1024, tk=512, name=n(f"mm_dwproj{i}"))
                             for i in range(3)])
    dy3 = [v.reshape(bsz, s, D_MODEL) for v in dys]

    (dact, daz, dadt, ddtb, dalog, ddsk, dnw) = _ssd_bwd(
        proj3, sv["gates3"], sv["xact3"], lp["dtb"], lp["alog"], lp["dsk"], lp["nw"], sv["ypre3"], sv["hst"], dy3[0],
        name=n("ssd_bwd"))
    g["dt_bias"], g["a_log"], g["d_skip"] = _ungroup_lanes(ddtb), _ungroup_lanes(dalog), _ungroup_lanes(ddsk)
    g["ssm_norm_w"] = dnw.reshape(D_MODEL)
    dxbc, dwb = _conv_bwd(proj3, wl["conv_w"], wl["conv_b"], dact, name=n("conv_bwd"))
    g["conv_w"], g["conv_b"] = dwb[:CONV_WIDTH], dwb[CONV_WIDTH]

    dbq, dbz, dkc, dkp, dvc, dvp, dsk = _swa_bwd(proj3, cos, sin, lp["sinks"], sv["ob3"],
                                                 sv["lseb3"], dy3[1], name=n("swa_bwd"))
    g["sinks"] = _ungroup_lanes(dsk)

    dbk, dbv = _swa_fold(dkc, dkp, dvc, dvp, name=n("swa_fold"))

    dcz, do3, stats = _foxt_prep(proj3, sv["oc3"], sv["statc3"], dy3[2], name=n("fox_prep"), tb=tb)
    dqt, dck, dcv, csum = _foxt_bwd(proj3, sv["cum_t"], do3, stats, name=n("fox_bwd"), tb=tb)
    dcq = jnp.transpose(dqt, (0, 2, 4, 1, 3)).reshape(bsz, s, D_MODEL)
    dcf, dfb = _fgate_bwd(sv["gates3"], lp["fb"], csum, name=n("fgate_bwd"))
    g["f_bias"] = dfb[0, :N_HEADS]

    parts = {"gates": dgates.reshape(bsz, s, 3 * D_MODEL), "xbc": dxbc, "a_z": daz, "b_q": dbq, "b_z": dbz,
             "c_q": dcq, "c_k": dck, "c_v": dcv, "c_z": dcz, "b_k": dbk, "b_v": dbv, "a_dt": dadt, "c_f": dcf}
    dproj = jnp.concatenate([parts[name].astype(BF16) for name, _ in _PAD_ORDER]
                            + [jnp.zeros((bsz, s, N_PAD - N_USED), BF16)], axis=2).reshape(t, N_PAD)
    dh = _mm(dproj, wl["w_in"], tb=True, tm=1024, tn=1024, tk=1536, name=n("mm_dh"))
    g["w_in"] = _unpad_w_in(_mm(sv["h_t"], dproj, tm=1024, tn=768, tk=2048, name=n("mm_dwin")))
    dx_in, dnorm = _rms_bwd(sv["x"], wl["norm_w"], dh, dx, name=n("rms_bwd"))
    g["norm_w"] = dnorm[0]
    return dx_in, g


def _local_step(x, target, wls, final_norm_w, tb=1024):
    bsz, s, d = x.shape
    t = bsz * s
    tabs = _rope_tables(s)
    xc = x.reshape(t, d)
    saved = []
    for li, wl in enumerate(wls):
        xc, sv = _layer_fwd(xc, wl, tabs, bsz, li, tb)
        saved.append(sv)
    loss, dx, dfw = _final_loss(xc, final_norm_w, target.reshape(t, d), name="final_loss")
    grads = [None] * len(wls)
    for li in reversed(range(len(wls))):
        dx, grads[li] = _layer_bwd(dx, wls[li], saved[li], tabs, bsz, li, tb)
    return loss[0, 0], dx.reshape(bsz, s, d), grads, dfw[0]


_HBM = pl.BlockSpec(memory_space=pltpu.HBM)


def _chip_peers(x, y):
    return [(1 - x, y), (x, 1 - y), (1 - x, 1 - y)]


def _gather_weights(arrs, *, name):
    n = len(arrs)

    def body(*refs):
        ins, outs = refs[:n], refs[n:2 * n]
        ici_send, ici_recv, d2d_send, d2d_recv = refs[2 * n:]
        x, y, c = lax.axis_index("x"), lax.axis_index("y"), lax.axis_index("c")
        me = 2 * x + y
        peers = _chip_peers(x, y)
        sib = (x, y, 1 - c)
        sends, fwds = [], []
        for a in range(n):
            for k, (px, py) in enumerate(peers):
                cp = pltpu.make_async_remote_copy(
                    src_ref=ins[a].at[c], dst_ref=outs[a].at[me, c], send_sem=ici_send.at[a, k],
                    recv_sem=ici_recv.at[a, k], device_id=(px, py, c), device_id_type=MESH)
                cp.start()
                sends.append(cp)
        for a in range(n):
            for k, (px, py) in enumerate(peers):
                slot = 2 * px + py
                pltpu.make_async_remote_copy(
                    src_ref=ins[a].at[c], dst_ref=outs[a].at[slot, c], send_sem=ici_send.at[a, k],
                    recv_sem=ici_recv.at[a, k], device_id=(px, py, c), device_id_type=MESH).wait_recv()
                fw = pltpu.make_async_remote_copy(
                    src_ref=outs[a].at[slot, c], dst_ref=outs[a].at[slot, c], send_sem=d2d_send.at[a, k],
                    recv_sem=d2d_recv.at[a, k], device_id=sib, device_id_type=MESH)
                fw.start()
                fwds.append(fw)
        for a in range(n):
            for k, (px, py) in enumerate(peers):
                slot = 2 * px + py
                pltpu.make_async_remote_copy(
                    src_ref=outs[a].at[slot, 1 - c], dst_ref=outs[a].at[slot, 1 - c], send_sem=d2d_send.at[a, k],
                    recv_sem=d2d_recv.at[a, k], device_id=sib, device_id_type=MESH).wait_recv()
        for cp in sends + fwds:
            cp.wait_send()

    out_shape = [jax.ShapeDtypeStruct((N_CHIPS,) + a.shape, a.dtype) for a in arrs]
    return pl.pallas_call(
        body, name=name, out_shape=out_shape, in_specs=[_HBM] * n, out_specs=[_HBM] * n,
        scratch_shapes=[pltpu.SemaphoreType.DMA((n, 3)), pltpu.SemaphoreType.DMA((n, 3)),
                        pltpu.SemaphoreType.DMA((n, 3)), pltpu.SemaphoreType.DMA((n, 3))],
    )(*arrs)


def _pair_exchange(arrs, *, name):
    n = len(arrs)

    def body(*refs):
        ins, outs = refs[:n], refs[n:2 * n]
        send, recv = refs[2 * n:]
        x, y, c = lax.axis_index("x"), lax.axis_index("y"), lax.axis_index("c")
        sib = (x, y, 1 - c)
        cps = []
        for a in range(n):
            for k in range(N_CHIPS):
                cp = pltpu.make_async_remote_copy(
                    src_ref=ins[a].at[k, 1 - c], dst_ref=outs[a].at[k], send_sem=send.at[a, k],
                    recv_sem=recv.at[a, k], device_id=sib, device_id_type=MESH)
                cp.start()
                cps.append(cp)
        for cp in cps:
            cp.wait()

    out_shape = [jax.ShapeDtypeStruct((N_CHIPS,) + a.shape[2:], a.dtype) for a in arrs]
    return pl.pallas_call(
        body, name=name, out_shape=out_shape, in_specs=[_HBM] * n, out_specs=[_HBM] * n,
        scratch_shapes=[pltpu.SemaphoreType.DMA((n, N_CHIPS)), pltpu.SemaphoreType.DMA((n, N_CHIPS))],
    )(*arrs)


def _chip_exchange(arrs, *, name):
    n = len(arrs)

    def body(*refs):
        ins, outs = refs[:n], refs[n:2 * n]
        send, recv = refs[2 * n:]
        x, y, c = lax.axis_index("x"), lax.axis_index("y"), lax.axis_index("c")
        me = 2 * x + y
        peers = _chip_peers(x, y)
        cps = []
        for a in range(n):
            for k, (px, py) in enumerate(peers):
                cp = pltpu.make_async_remote_copy(
                    src_ref=ins[a].at[2 * px + py], dst_ref=outs[a].at[me], send_sem=send.at[a, k],
                    recv_sem=recv.at[a, k], device_id=(px, py, c), device_id_type=MESH)
                cp.start()
                cps.append(cp)
        for a in range(n):
            for k, (px, py) in enumerate(peers):
                pltpu.make_async_remote_copy(
                    src_ref=ins[a].at[2 * px + py], dst_ref=outs[a].at[2 * px + py], send_sem=send.at[a, k],
                    recv_sem=recv.at[a, k], device_id=(px, py, c), device_id_type=MESH).wait_recv()
        for cp in cps:
            cp.wait_send()

    out_shape = [jax.ShapeDtypeStruct(a.shape, a.dtype) for a in arrs]
    return pl.pallas_call(
        body, name=name, out_shape=out_shape, in_specs=[_HBM] * n, out_specs=[_HBM] * n,
        scratch_shapes=[pltpu.SemaphoreType.DMA((n, 3)), pltpu.SemaphoreType.DMA((n, 3))],
    )(*arrs)


def _pair_share(arrs, *, name):
    n = len(arrs)

    def body(*refs):
        ins, outs = refs[:n], refs[n:2 * n]
        send, recv = refs[2 * n:]
        x, y, c = lax.axis_index("x"), lax.axis_index("y"), lax.axis_index("c")
        sib = (x, y, 1 - c)
        cps = []
        for a in range(n):
            cp = pltpu.make_async_remote_copy(
                src_ref=ins[a], dst_ref=outs[a], send_sem=send.at[a], recv_sem=recv.at[a],
                device_id=sib, device_id_type=MESH)
            cp.start()
            cps.append(cp)
        for cp in cps:
            cp.wait()

    out_shape = [jax.ShapeDtypeStruct(a.shape, a.dtype) for a in arrs]
    return pl.pallas_call(
        body, name=name, out_shape=out_shape, in_specs=[_HBM] * n, out_specs=[_HBM] * n,
        scratch_shapes=[pltpu.SemaphoreType.DMA((n,)), pltpu.SemaphoreType.DMA((n,))],
    )(*arrs)


def _allreduce_small(buf, *, name):
    r = buf.shape[0]

    def body(in_ref, out_ref, land, send, recv):
        x, y, c = lax.axis_index("x"), lax.axis_index("y"), lax.axis_index("c")
        me = 4 * x + 2 * y + c
        land[me] = in_ref[...]
        cps = []
        for k in range(1, N_DEV):
            px, py, pc = x ^ ((k >> 2) & 1), y ^ ((k >> 1) & 1), c ^ (k & 1)
            cp = pltpu.make_async_remote_copy(
                src_ref=in_ref, dst_ref=land.at[me], send_sem=send.at[k - 1], recv_sem=recv.at[k - 1],
                device_id=(px, py, pc), device_id_type=MESH)
            cp.start()
            cps.append(cp)
        for k in range(1, N_DEV):
            px, py, pc = x ^ ((k >> 2) & 1), y ^ ((k >> 1) & 1), c ^ (k & 1)
            pltpu.make_async_remote_copy(
                src_ref=in_ref, dst_ref=land.at[4 * px + 2 * py + pc], send_sem=send.at[k - 1],
                recv_sem=recv.at[k - 1], device_id=(px, py, pc), device_id_type=MESH).wait_recv()
        for cp in cps:
            cp.wait_send()
        acc = land[0]
        for k in range(1, N_DEV):
            acc = acc + land[k]
        out_ref[...] = acc

    vm = pl.BlockSpec(memory_space=pltpu.VMEM)
    return pl.pallas_call(
        body, name=name, out_shape=jax.ShapeDtypeStruct((r, LANES), F32), in_specs=[vm], out_specs=vm,
        scratch_shapes=[pltpu.VMEM((N_DEV, r, LANES), F32), pltpu.SemaphoreType.DMA((N_DEV - 1,)),
                        pltpu.SemaphoreType.DMA((N_DEV - 1,))],
    )(buf)


def _row_tile(rows, cols, n_arrays, budget=20 * 1024 * 1024):
    best = 8 if rows % 8 == 0 else rows
    tr = 8
    while tr <= rows:
        if rows % tr == 0 and tr * cols * 4 * n_arrays * 2 <= budget:
            best = tr
        tr *= 2
    return best


def _add_slot_layer(full, other, *, name):
    _, _, r, cdim = full.shape
    tr = _row_tile(r, cdim, 4)

    def body(c_ref, a_ref, b_ref, o_ref, ob_ref):
        sm = a_ref[...] + b_ref[...]
        o_ref[...] = sm
        ob_ref[...] = sm.astype(BF16)

    c = lax.axis_index("c").astype(jnp.int32).reshape(1)
    blk = pl.BlockSpec((None, tr, cdim), lambda k, i, c_ref: (k, i, 0))
    return pl.pallas_call(
        body, name=name,
        grid_spec=pltpu.PrefetchScalarGridSpec(
            num_scalar_prefetch=1, grid=(N_CHIPS, r // tr),
            in_specs=[pl.BlockSpec((None, None, tr, cdim), lambda k, i, c_ref: (k, c_ref[0], i, 0)), blk],
            out_specs=[blk, blk]),
        out_shape=[jax.ShapeDtypeStruct((N_CHIPS, r, cdim), F32), jax.ShapeDtypeStruct((N_CHIPS, r, cdim), BF16)],
        compiler_params=_cp(("parallel", "parallel")),
    )(c, full, other)


def _sum_slots(parts, pair, *, name):
    _, r, cdim = parts.shape
    tr = _row_tile(r, cdim, 5)

    def body(me_ref, p_ref, own_ref, o_ref):
        me = me_ref[0]
        acc = None
        for k in range(N_CHIPS):
            term = jnp.where(me == k, own_ref[...], p_ref[k].astype(F32))
            acc = term if acc is None else acc + term
        o_ref[...] = acc

    me = (2 * lax.axis_index("x") + lax.axis_index("y")).astype(jnp.int32).reshape(1)
    return pl.pallas_call(
        body, name=name,
        grid_spec=pltpu.PrefetchScalarGridSpec(
            num_scalar_prefetch=1, grid=(r // tr,),
            in_specs=[pl.BlockSpec((N_CHIPS, tr, cdim), lambda i, me_ref: (0, i, 0)),
                      pl.BlockSpec((None, tr, cdim), lambda i, me_ref: (me_ref[0], i, 0))],
            out_specs=pl.BlockSpec((tr, cdim), lambda i, me_ref: (i, 0))),
        out_shape=jax.ShapeDtypeStruct((r, cdim), F32),
        compiler_params=_cp(("parallel",)),
    )(me, parts, pair)


def _adamw(w, g, m, v, *, name):
    lead, (r, cdim) = w.shape[:-2], w.shape[-2:]
    nl = len(lead)
    tr = _row_tile(r, cdim, 7)
    c1 = 1.0 - ADAM_B1 ** ADAM_STEP
    c2 = 1.0 - ADAM_B2 ** ADAM_STEP

    def body(w_ref, g_ref, m_ref, v_ref, d_ref, nm_ref, nv_ref):
        gv = g_ref[...]
        mn = ADAM_B1 * m_ref[...] + (1.0 - ADAM_B1) * gv
        vn = ADAM_B2 * v_ref[...] + (1.0 - ADAM_B2) * (gv * gv)
        nm_ref[...] = mn
        nv_ref[...] = vn
        d_ref[...] = -ADAM_LR * ((mn / c1) / (jnp.sqrt(vn / c2) + ADAM_EPS) + ADAM_WD * w_ref[...])

    blk = pl.BlockSpec((None,) * nl + (tr, cdim), lambda *ids: ids[:nl] + (ids[nl], 0))
    sh = jax.ShapeDtypeStruct(w.shape, F32)
    return pl.pallas_call(
        body, name=name, grid=lead + (r // tr,), in_specs=[blk] * 4, out_specs=[blk] * 3, out_shape=[sh] * 3,
        compiler_params=_cp(("parallel",) * (nl + 1)),
    )(w, g, m, v)


_SMALL = ("norm_w", "conv_b", "dt_bias", "a_log", "d_skip", "ssm_norm_w", "sinks", "f_bias", "final_norm_w",
          "conv_w", "gate_bias")


def _pack(vals):
    flat = jnp.concatenate([v.reshape(-1) for v in vals])
    rows = -(-flat.shape[0] // LANES)
    rows = -(-rows // 8) * 8
    return jnp.pad(flat, (0, rows * LANES - flat.shape[0])).reshape(rows, LANES)


def _unpack(buf, shapes):
    flat = buf.reshape(-1)
    out, off = [], 0
    for sh in shapes:
        sz = int(np.prod(sh))
        out.append(flat[off:off + sz].reshape(sh))
        off += sz
    return out


def kernel(x, norm_w, w_in, conv_w, conv_b, dt_bias, a_log, d_skip, ssm_norm_w, sinks, f_bias, gate_bias, w_proj, w_out, final_norm_w, loss_target, m_norm_w, m_w_in, m_conv_w, m_conv_b, m_dt_bias, m_a_log, m_d_skip, m_ssm_norm_w, m_sinks, m_f_bias, m_gate_bias, m_w_proj, m_w_out, m_final_norm_w, v_norm_w, v_w_in, v_conv_w, v_conv_b, v_dt_bias, v_a_log, v_d_skip, v_ssm_norm_w, v_sinks, v_f_bias, v_gate_bias, v_w_proj, v_w_out, v_final_norm_w):
    depth = w_in.shape[0]
    chip = 2 * lax.axis_index("x") + lax.axis_index("y")

    own = [w_in.astype(BF16), w_proj.astype(BF16), w_out.astype(BF16), conv_w, gate_bias]
    gathered = _gather_weights(own, name="gather_weights")

    def whole(a, li, axis):
        return jnp.concatenate([jnp.where(chip == k, own[a][li], gathered[a][k, li]) for k in range(N_CHIPS)],
                               axis=axis)

    wls = []
    for li in range(depth):
        wls.append(dict(
            norm_w=norm_w[li], w_in=_pad_w_in(whole(0, li, 1)),
            conv_w=whole(3, li, 1), conv_b=conv_b[li], dt_bias=dt_bias[li], a_log=a_log[li], d_skip=d_skip[li],
            ssm_norm_w=ssm_norm_w[li], sinks=sinks[li], f_bias=f_bias[li], gate_bias=whole(4, li, 1),
            w_proj=whole(1, li, 1),
            w_out=whole(2, li, 0)))

    loss_part, grad_x, grads, d_final = _local_step(x, loss_target, wls, final_norm_w)
    loss = lax.psum(loss_part, ("x", "y", "c"))

    c_in = w_in.shape[2]
    r_proj = w_proj.shape[2]
    r_out = w_out.shape[1]
    full_in = jnp.stack([jnp.stack([grads[li]["w_in"][:, k * c_in:(k + 1) * c_in] for li in range(depth)])
                         for k in range(N_CHIPS)])
    full_proj = jnp.stack([jnp.stack([grads[li]["w_proj"][:, k * r_proj:(k + 1) * r_proj].reshape(-1, D_MODEL)
                                      for li in range(depth)]) for k in range(N_CHIPS)])
    full_out = jnp.stack([jnp.stack([grads[li]["w_out"][k * r_out:(k + 1) * r_out] for li in range(depth)])
                          for k in range(N_CHIPS)])
    fulls = [full_in, full_proj, full_out]
    others = _pair_exchange(fulls, name="grad_pair_exchange")
    pair = [_add_slot_layer(f, o, name=f"grad_pair_add{i}") for i, (f, o) in enumerate(zip(fulls, others))]
    parts = _chip_exchange([p[1] for p in pair], name="grad_chip_exchange")
    mine = [_sum_slots(p, pr[0], name=f"grad_slot_sum{i}") for i, (p, pr) in enumerate(zip(parts, pair))]
    theirs = _pair_share(mine, name="grad_pair_share")
    core = lax.axis_index("c")
    red_in, red_proj, red_out = [jnp.stack([jnp.where(core == li, m, t) for li in range(depth)])
                                 for m, t in zip(mine, theirs)]
    grad_w_in = red_in
    grad_w_proj = red_proj.reshape(w_proj.shape)
    grad_w_out = red_out

    small_full = {
        "norm_w": jnp.stack([g["norm_w"] for g in grads]), "conv_b": jnp.stack([g["conv_b"] for g in grads]),
        "dt_bias": jnp.stack([g["dt_bias"] for g in grads]), "a_log": jnp.stack([g["a_log"] for g in grads]),
        "d_skip": jnp.stack([g["d_skip"] for g in grads]),
        "ssm_norm_w": jnp.stack([g["ssm_norm_w"] for g in grads]),
        "sinks": jnp.stack([g["sinks"] for g in grads]), "f_bias": jnp.stack([g["f_bias"] for g in grads]),
        "final_norm_w": d_final,
        "conv_w": jnp.stack([g["conv_w"] for g in grads]), "gate_bias": jnp.stack([g["gate_bias"] for g in grads])}
    shapes = [small_full[k].shape for k in _SMALL]
    summed = _unpack(_allreduce_small(_pack([small_full[k] for k in _SMALL]), name="allreduce_small"), shapes)
    gsmall = dict(zip(_SMALL, summed))
    gsmall["conv_w"] = lax.dynamic_slice_in_dim(gsmall["conv_w"], chip * conv_w.shape[2], conv_w.shape[2], axis=2)
    gsmall["gate_bias"] = lax.dynamic_slice_in_dim(gsmall["gate_bias"], chip * gate_bias.shape[2],
                                                   gate_bias.shape[2], axis=2)

    w_small = dict(norm_w=norm_w, conv_b=conv_b, dt_bias=dt_bias, a_log=a_log, d_skip=d_skip,
                   ssm_norm_w=ssm_norm_w, sinks=sinks, f_bias=f_bias, final_norm_w=final_norm_w, conv_w=conv_w,
                   gate_bias=gate_bias)
    m_small = dict(norm_w=m_norm_w, conv_b=m_conv_b, dt_bias=m_dt_bias, a_log=m_a_log, d_skip=m_d_skip,
                   ssm_norm_w=m_ssm_norm_w, sinks=m_sinks, f_bias=m_f_bias, final_norm_w=m_final_norm_w,
                   conv_w=m_conv_w, gate_bias=m_gate_bias)
    v_small = dict(norm_w=v_norm_w, conv_b=v_conv_b, dt_bias=v_dt_bias, a_log=v_a_log, d_skip=v_d_skip,
                   ssm_norm_w=v_ssm_norm_w, sinks=v_sinks, f_bias=v_f_bias, final_norm_w=v_final_norm_w,
                   conv_w=v_conv_w, gate_bias=v_gate_bias)
    sshapes = [w_small[k].shape for k in _SMALL]
    ds, ms, vs = _adamw(_pack([w_small[k] for k in _SMALL]), _pack([gsmall[k] for k in _SMALL]),
                        _pack([m_small[k] for k in _SMALL]), _pack([v_small[k] for k in _SMALL]), name="adamw_small")
    delta = dict(zip(_SMALL, _unpack(ds, sshapes)))
    new_m = dict(zip(_SMALL, _unpack(ms, sshapes)))
    new_v = dict(zip(_SMALL, _unpack(vs, sshapes)))
    grad = dict(gsmall)
    for nm, w, g, m, v in (("w_in", w_in, grad_w_in, m_w_in, v_w_in),
                           ("w_proj", w_proj, grad_w_proj, m_w_proj, v_w_proj),
                           ("w_out", w_out, grad_w_out, m_w_out, v_w_out)):
        grad[nm] = g
        delta[nm], new_m[nm], new_v[nm] = _adamw(w, g, m, v, name=f"adamw_{nm}")

    order = ("norm_w", "w_in", "conv_w", "conv_b", "dt_bias", "a_log", "d_skip", "ssm_norm_w", "sinks", "f_bias",
             "gate_bias", "w_proj", "w_out", "final_norm_w")
    return (loss, grad_x, *[grad[k] for k in order], *[delta[k] for k in order],
            *[new_m[k] for k in order], *[new_v[k] for k in order])
```

```python
import functools
import math

import numpy as np
import jax
import jax.numpy as jnp
from jax import lax
from jax.experimental import pallas as pl
from jax.experimental.pallas import tpu as pltpu

F32 = jnp.float32
BF16 = jnp.bfloat16
HIGHEST = lax.Precision.HIGHEST
MESH = pl.DeviceIdType.MESH

D_MODEL = 1024
HEAD_DIM = 64
N_HEADS = 16
N_GROUPS = 4
SSM_STATE = 128
CHUNK = 128
CONV_WIDTH = 4
CONV_DIM = 2048
ROPE_THETA = 10000.0
NORM_EPS = 1e-6
LANES = 128
N_CHIPS = 4
N_DEV = 8

ADAM_LR = 0.001
ADAM_B1 = 0.9
ADAM_B2 = 0.999
ADAM_EPS = 1e-08
ADAM_WD = 0.01
ADAM_STEP = 10

_REF_COLS = {}
_off = 0
for _n, _s in (("xbc", 2048), ("a_z", 1024), ("a_dt", 16), ("b_q", 1024), ("b_k", 256), ("b_v", 256),
               ("b_z", 1024), ("c_q", 1024), ("c_k", 1024), ("c_v", 1024), ("c_f", 16), ("c_z", 1024),
               ("gates", 3072)):
    _REF_COLS[_n] = (_off, _s)
    _off += _s
N_IN = _off

_PAD_ORDER = (("gates", 3072), ("xbc", 2048), ("a_z", 1024), ("b_q", 1024), ("b_z", 1024), ("c_q", 1024),
              ("c_k", 1024), ("c_v", 1024), ("c_z", 1024), ("b_k", 256), ("b_v", 256), ("a_dt", 512),
              ("c_f", 128))
_PAD_COLS = {}
_off = 0
for _n, _s in _PAD_ORDER:
    _PAD_COLS[_n] = (_off, _s)
    _off += _s
N_USED = _off
N_PAD = 13824


def _cp(sem, vmem_mb=48):
    return pltpu.CompilerParams(dimension_semantics=sem, vmem_limit_bytes=vmem_mb * 1024 * 1024)


def _dot(a, b, dims=((1,), (0,)), precision=None):
    return lax.dot_general(a, b, (dims, ((), ())), preferred_element_type=F32, precision=precision)


def _dot_nt(a, b):
    return _dot(a, b, ((1,), (1,)))


def _dot_tn(a, b):
    return _dot(a, b, ((0,), (0,)))


def _col(v, idx):
    lane = lax.broadcasted_iota(jnp.int32, v.shape, 1)
    return jnp.sum(jnp.where(lane == idx, v, 0.0), axis=1, keepdims=True)


def _row(v, idx):
    row = lax.broadcasted_iota(jnp.int32, v.shape, 0)
    return jnp.sum(jnp.where(row == idx, v, 0.0), axis=0, keepdims=True)


def _iota_col():
    return lax.broadcasted_iota(jnp.int32, (CHUNK, 1), 0)


def _iota_row():
    return lax.broadcasted_iota(jnp.int32, (1, LANES), 1)


def _sigmoid(x):
    return 1.0 / (1.0 + jnp.exp(-x))


def _softplus(x):
    return jnp.maximum(x, 0.0) + jnp.log(1.0 + jnp.exp(-jnp.abs(x)))


def _pad_w_in(w):
    parts = []
    for name, size in _PAD_ORDER:
        s0, sz = _REF_COLS[name]
        seg = w[:, s0:s0 + sz]
        if name == "a_dt":
            seg = jnp.pad(seg.reshape(-1, N_GROUPS, 4), ((0, 0), (0, 0), (0, LANES - 4))).reshape(-1, 512)
        elif name == "c_f":
            seg = jnp.pad(seg, ((0, 0), (0, LANES - 16)))
        parts.append(seg)
    parts.append(jnp.zeros((w.shape[0], N_PAD - N_USED), w.dtype))
    return jnp.concatenate(parts, axis=1)


def _unpad_w_in(wp):
    segs = {}
    for name, _ in _PAD_ORDER:
        p0, psz = _PAD_COLS[name]
        seg = wp[:, p0:p0 + psz]
        if name == "a_dt":
            seg = seg.reshape(-1, N_GROUPS, LANES)[:, :, :4].reshape(-1, 16)
        elif name == "c_f":
            seg = seg[:, :16]
        segs[name] = seg
    order = sorted(_REF_COLS, key=lambda n: _REF_COLS[n][0])
    return jnp.concatenate([segs[n] for n in order], axis=1)


def _group_lanes(v):
    return jnp.pad(v.reshape(N_GROUPS, 1, 4), ((0, 0), (0, 0), (0, LANES - 4)))


def _ungroup_lanes(v):
    return v[:, 0, :4].reshape(16)


def _mm(a, b, *, ta=False, tb=False, tm=512, tn=512, tk=512, out_dtype=F32, name):
    if ta:
        kdim, m = a.shape
    else:
        m, kdim = a.shape
    if tb:
        n, k2 = b.shape
    else:
        k2, n = b.shape
    assert kdim == k2, (a.shape, b.shape)
    tm, tn, tk = min(tm, m), min(tn, n), min(tk, kdim)
    assert m % tm == 0 and n % tn == 0 and kdim % tk == 0, (m, n, kdim, tm, tn, tk)
    nk = kdim // tk
    a_spec = (pl.BlockSpec((tk, tm), lambda i, j, k: (k, i)) if ta
              else pl.BlockSpec((tm, tk), lambda i, j, k: (i, k)))
    b_spec = (pl.BlockSpec((tn, tk), lambda i, j, k: (j, k)) if tb
              else pl.BlockSpec((tk, tn), lambda i, j, k: (k, j)))
    dims = ((0 if ta else 1,), (1 if tb else 0,))

    def body(a_ref, b_ref, o_ref, acc_ref):
        k = pl.program_id(2)
        p = _dot(a_ref[...].astype(BF16), b_ref[...].astype(BF16), dims)

        @pl.when(k == 0)
        def _():
            acc_ref[...] = p

        @pl.when(k > 0)
        def _():
            acc_ref[...] += p

        @pl.when(k == nk - 1)
        def _():
            o_ref[...] = acc_ref[...].astype(out_dtype)

    return pl.pallas_call(
        body, name=name, grid=(m // tm, n // tn, nk),
        in_specs=[a_spec, b_spec], out_specs=pl.BlockSpec((tm, tn), lambda i, j, k: (i, j)),
        out_shape=jax.ShapeDtypeStruct((m, n), out_dtype),
        scratch_shapes=[pltpu.VMEM((tm, tn), F32)],
        compiler_params=_cp(("parallel", "parallel", "arbitrary")),
    )(a, b)


def _rms_fwd(x, w, *, name, tm=512):
    t, d = x.shape

    def body(x_ref, w_ref, o_ref, ot_ref):
        xv = x_ref[...]
        r = lax.rsqrt(jnp.mean(xv * xv, axis=1, keepdims=True) + NORM_EPS)
        h = xv * r * w_ref[...]
        o_ref[...] = h.astype(BF16)
        ot_ref[...] = h.T.astype(BF16)

    return pl.pallas_call(
        body, name=name, grid=(t // tm,),
        in_specs=[pl.BlockSpec((tm, d), lambda i: (i, 0)), pl.BlockSpec((1, d), lambda i: (0, 0))],
        out_specs=[pl.BlockSpec((tm, d), lambda i: (i, 0)), pl.BlockSpec((d, tm), lambda i: (0, i))],
        out_shape=[jax.ShapeDtypeStruct((t, d), BF16), jax.ShapeDtypeStruct((d, t), BF16)],
        compiler_params=_cp(("parallel",)),
    )(x, w.reshape(1, d))


def _rms_bwd(x, w, dh, dres, *, name, tm=512):
    t, d = x.shape

    def body(x_ref, w_ref, dh_ref, dres_ref, dx_ref, dw_ref):
        xv = x_ref[...]
        r = lax.rsqrt(jnp.mean(xv * xv, axis=1, keepdims=True) + NORM_EPS)
        xhat = xv * r
        dhv = dh_ref[...]
        dxhat = dhv * w_ref[...]
        dx = r * (dxhat - xhat * jnp.mean(dxhat * xhat, axis=1, keepdims=True))
        dx_ref[...] = dres_ref[...] + dx

        @pl.when(pl.program_id(0) == 0)
        def _():
            dw_ref[...] = jnp.zeros_like(dw_ref)

        dw_ref[...] += jnp.sum(dhv * xhat, axis=0, keepdims=True)

    return pl.pallas_call(
        body, name=name, grid=(t // tm,),
        in_specs=[pl.BlockSpec((tm, d), lambda i: (i, 0)), pl.BlockSpec((1, d), lambda i: (0, 0)),
                  pl.BlockSpec((tm, d), lambda i: (i, 0)), pl.BlockSpec((tm, d), lambda i: (i, 0))],
        out_specs=[pl.BlockSpec((tm, d), lambda i: (i, 0)), pl.BlockSpec((1, d), lambda i: (0, 0))],
        out_shape=[jax.ShapeDtypeStruct((t, d), F32), jax.ShapeDtypeStruct((1, d), F32)],
        compiler_params=_cp(("arbitrary",)),
    )(x, w.reshape(1, d), dh, dres)


def _final_loss(x, w, target, *, name, tm=512):
    t, d = x.shape

    def body(x_ref, w_ref, t_ref, loss_ref, dx_ref, dw_ref):
        xv = x_ref[...]
        wv = w_ref[...]
        r = lax.rsqrt(jnp.mean(xv * xv, axis=1, keepdims=True) + NORM_EPS)
        xhat = xv * r
        err = xhat * wv - t_ref[...]
        dy = err * (1.0 / d)
        dxhat = dy * wv
        dx_ref[...] = r * (dxhat - xhat * jnp.mean(dxhat * xhat, axis=1, keepdims=True))

        @pl.when(pl.program_id(0) == 0)
        def _():
            dw_ref[...] = jnp.zeros_like(dw_ref)
            loss_ref[...] = jnp.zeros_like(loss_ref)

        dw_ref[...] += jnp.sum(dy * xhat, axis=0, keepdims=True)
        part = 0.5 * jnp.sum(jnp.mean(err * err, axis=1, keepdims=True), axis=0, keepdims=True)
        loss_ref[...] += jnp.broadcast_to(part, loss_ref.shape)

    return pl.pallas_call(
        body, name=name, grid=(t // tm,),
        in_specs=[pl.BlockSpec((tm, d), lambda i: (i, 0)), pl.BlockSpec((1, d), lambda i: (0, 0)),
                  pl.BlockSpec((tm, d), lambda i: (i, 0))],
        out_specs=[pl.BlockSpec((8, LANES), lambda i: (0, 0)), pl.BlockSpec((tm, d), lambda i: (i, 0)),
                   pl.BlockSpec((1, d), lambda i: (0, 0))],
        out_shape=[jax.ShapeDtypeStruct((8, LANES), F32), jax.ShapeDtypeStruct((t, d), F32),
                   jax.ShapeDtypeStruct((1, d), F32)],
        compiler_params=_cp(("arbitrary",)),
    )(x, w.reshape(1, d), target)


_CB = 128


def _conv_pre(u, w_ref, b_ref):
    s = u.shape[0]
    row = lax.broadcasted_iota(jnp.int32, u.shape, 0)
    pre = b_ref[...] + w_ref[CONV_WIDTH - 1:CONV_WIDTH, :] * u
    for sh in range(1, CONV_WIDTH):
        shifted = jnp.where(row >= sh, pltpu.roll(u, sh, 0), 0.0)
        pre = pre + w_ref[CONV_WIDTH - 1 - sh:CONV_WIDTH - sh, :] * shifted
    return pre


def _conv_fwd(proj3, cw, cb, *, name):
    b, s, _ = proj3.shape
    c0 = _PAD_COLS["xbc"][0] // _CB

    def body(u_ref, w_ref, b_ref, o_ref):
        pre = _conv_pre(u_ref[...].astype(F32), w_ref, b_ref)
        o_ref[...] = pre * _sigmoid(pre)

    return pl.pallas_call(
        body, name=name, grid=(b, CONV_DIM // _CB),
        in_specs=[pl.BlockSpec((None, s, _CB), lambda i, j: (i, 0, c0 + j)),
                  pl.BlockSpec((CONV_WIDTH, _CB), lambda i, j: (0, j)),
                  pl.BlockSpec((1, _CB), lambda i, j: (0, j))],
        out_specs=pl.BlockSpec((None, s, _CB), lambda i, j: (i, 0, j)),
        out_shape=jax.ShapeDtypeStruct((b, s, CONV_DIM), F32),
        compiler_params=_cp(("parallel", "parallel")),
    )(proj3, cw, cb.reshape(1, CONV_DIM))


def _conv_bwd(proj3, cw, cb, dact, *, name):
    b, s, _ = proj3.shape
    c0 = _PAD_COLS["xbc"][0] // _CB

    def body(u_ref, w_ref, b_ref, da_ref, du_ref, dwb_ref):
        u = u_ref[...].astype(F32)
        pre = _conv_pre(u, w_ref, b_ref)
        sg = _sigmoid(pre)
        dpre = da_ref[...] * (sg * (1.0 + pre * (1.0 - sg)))
        row = lax.broadcasted_iota(jnp.int32, u.shape, 0)
        du = w_ref[CONV_WIDTH - 1:CONV_WIDTH, :] * dpre
        rows = [jnp.sum(dpre * u, axis=0, keepdims=True)]
        for sh in range(1, CONV_WIDTH):
            fwd_shift = jnp.where(row < s - sh, pltpu.roll(dpre, s - sh, 0), 0.0)
            du = du + w_ref[CONV_WIDTH - 1 - sh:CONV_WIDTH - sh, :] * fwd_shift
            ush = jnp.where(row >= sh, pltpu.roll(u, sh, 0), 0.0)
            rows.append(jnp.sum(dpre * ush, axis=0, keepdims=True))
        du_ref[...] = du.astype(BF16)

        @pl.when(pl.program_id(1) == 0)
        def _():
            dwb_ref[...] = jnp.zeros_like(dwb_ref)

        for sh in range(CONV_WIDTH):
            k = CONV_WIDTH - 1 - sh
            dwb_ref[k:k + 1, :] += rows[sh]
        dwb_ref[CONV_WIDTH:CONV_WIDTH + 1, :] += jnp.sum(dpre, axis=0, keepdims=True)

    return pl.pallas_call(
        body, name=name, grid=(CONV_DIM // _CB, b),
        in_specs=[pl.BlockSpec((None, s, _CB), lambda j, i: (i, 0, c0 + j)),
                  pl.BlockSpec((CONV_WIDTH, _CB), lambda j, i: (0, j)),
                  pl.BlockSpec((1, _CB), lambda j, i: (0, j)),
                  pl.BlockSpec((None, s, _CB), lambda j, i: (i, 0, j))],
        out_specs=[pl.BlockSpec((None, s, _CB), lambda j, i: (i, 0, j)),
                   pl.BlockSpec((8, _CB), lambda j, i: (0, j))],
        out_shape=[jax.ShapeDtypeStruct((b, s, CONV_DIM), BF16), jax.ShapeDtypeStruct((8, CONV_DIM), F32)],
        compiler_params=_cp(("parallel", "arbitrary")),
    )(proj3, cw, cb.reshape(1, CONV_DIM), dact)


def _ssd_common(dt_ref, dtb_ref, alog_ref):
    row = lax.broadcasted_iota(jnp.int32, (CHUNK, CHUNK), 0)
    lane = lax.broadcasted_iota(jnp.int32, (CHUNK, CHUNK), 1)
    causal = row >= lane
    tri = causal.astype(F32)
    dtv = _softplus(dt_ref[...] + dtb_ref[...])
    a_row = -jnp.exp(alog_ref[...])
    acum = _dot(tri, dtv * a_row, precision=HIGHEST)
    return row, lane, causal, dtv, a_row, acum, acum.T


def _ssd_pair(pp, x, dtv, acum, acum_t, causal, lane, row):
    lo = lane < HEAD_DIM
    r0, r1 = 2 * pp, 2 * pp + 1
    dtp = jnp.where(lo, _col(dtv, r0), _col(dtv, r1))
    ac0, ac1 = _col(acum, r0), _col(acum, r1)
    ar0, ar1 = _row(acum_t, r0), _row(acum_t, r1)
    d0 = jnp.where(causal, jnp.exp(jnp.where(causal, ac0 - ar0, 0.0)), 0.0)
    d1 = jnp.where(causal, jnp.exp(jnp.where(causal, ac1 - ar1, 0.0)), 0.0)
    al0, al1 = _col(ar0, CHUNK - 1), _col(ar1, CHUNK - 1)
    eac = jnp.where(lo, jnp.exp(ac0), jnp.exp(ac1))
    dsp = jnp.where(lo, jnp.exp(al0 - ac0), jnp.exp(al1 - ac1))
    eal = jnp.where(_iota_col() < HEAD_DIM, jnp.exp(al0), jnp.exp(al1))
    return lo, dtp, x * dtp, d0, d1, al0, al1, eac, dsp, eal


def _ssd_fwd(proj3, gates3, xact3, dtb, alog, dsk, nw, *, name):
    b, s, _ = proj3.shape
    nc = s // CHUNK
    dt0 = 0
    z0 = _PAD_COLS["a_z"][0] // D_MODEL

    def body(xs_ref, bm_ref, cm_ref, dt_ref, z_ref, dtb_ref, alog_ref, dsk_ref, nw_ref,
             ya_ref, ypre_ref, hst_ref, h_scr):
        @pl.when(pl.program_id(1) == 0)
        def _():
            h_scr[...] = jnp.zeros_like(h_scr)

        for g in range(N_GROUPS):
            w256 = pl.ds(256 * g, 256)
            w128 = pl.ds(LANES * g, LANES)
            group(xs_ref.at[:, w256], bm_ref.at[:, w128], cm_ref.at[:, w128], dt_ref.at[:, w128],
                  z_ref.at[:, w256], dtb_ref.at[g], alog_ref.at[g], dsk_ref.at[g], nw_ref.at[g],
                  ya_ref.at[:, w256], ypre_ref.at[:, w256], hst_ref.at[g], h_scr.at[g])

    def group(xs_ref, bm_ref, cm_ref, dt_ref, z_ref, dtb_ref, alog_ref, dsk_ref, nw_ref,
              ya_ref, ypre_ref, hst_ref, h_scr):
        row, lane, causal, dtv, a_row, acum, acum_t = _ssd_common(dt_ref, dtb_ref, alog_ref)
        bb = bm_ref[...].astype(BF16)
        cb = cm_ref[...].astype(BF16)
        cbm = _dot_nt(cb, bb)
        hst_ref[...] = h_scr[...]
        dskv = dsk_ref[...]
        for pp in range(2):
            x = xs_ref[:, LANES * pp:LANES * (pp + 1)]
            lo, dtp, xd, d0, d1, al0, al1, eac, dsp, eal = _ssd_pair(pp, x, dtv, acum, acum_t, causal, lane, row)
            xdb = xd.astype(BF16)
            y = jnp.where(lo, _dot((cbm * d0).astype(BF16), xdb), _dot((cbm * d1).astype(BF16), xdb))
            h = h_scr[pp]
            y = y + eac * _dot_nt(cb, h.astype(BF16))
            h_scr[pp] = h * eal + _dot_tn((xd * dsp).astype(BF16), bb)
            dskp = jnp.where((_iota_row() < HEAD_DIM), _col(dskv, 2 * pp), _col(dskv, 2 * pp + 1))
            ypre_ref[:, LANES * pp:LANES * (pp + 1)] = y + x * dskp
        ypre = ypre_ref[...]
        z = z_ref[...].astype(F32)
        yg = ypre * (z * _sigmoid(z))
        rstd = lax.rsqrt(jnp.sum(yg * yg, axis=1, keepdims=True) * (1.0 / 256.0) + NORM_EPS)
        ya_ref[...] = (yg * rstd * nw_ref[...]).astype(BF16)

    g = N_GROUPS
    par = pl.BlockSpec((g, 1, LANES), lambda i, c: (0, 0, 0))
    wide = pl.BlockSpec((None, CHUNK, D_MODEL), lambda i, c: (i, c, 0))
    return pl.pallas_call(
        body, name=name, grid=(b, nc),
        in_specs=[wide,
                  pl.BlockSpec((None, CHUNK, 512), lambda i, c: (i, c, 2)),
                  pl.BlockSpec((None, CHUNK, 512), lambda i, c: (i, c, 3)),
                  pl.BlockSpec((None, CHUNK, 512), lambda i, c: (i, c, dt0)),
                  pl.BlockSpec((None, CHUNK, D_MODEL), lambda i, c: (i, c, z0)),
                  par, par, par,
                  pl.BlockSpec((g, 1, 256), lambda i, c: (0, 0, 0))],
        out_specs=[wide, wide,
                   pl.BlockSpec((None, None, g, 2, CHUNK, SSM_STATE), lambda i, c: (i, c, 0, 0, 0, 0))],
        out_shape=[jax.ShapeDtypeStruct((b, s, D_MODEL), BF16), jax.ShapeDtypeStruct((b, s, D_MODEL), F32),
                   jax.ShapeDtypeStruct((b, nc, g, 2, CHUNK, SSM_STATE), F32)],
        scratch_shapes=[pltpu.VMEM((g, 2, CHUNK, SSM_STATE), F32)],
        compiler_params=_cp(("parallel", "arbitrary")),
    )(xact3, xact3, xact3, gates3, proj3, dtb, alog, dsk, nw)


def _ssd_bwd(proj3, gates3, xact3, dtb, alog, dsk, nw, ypre3, hst, dya3, *, name):
    b, s, _ = proj3.shape
    nc = s // CHUNK
    dt0 = 0
    z0 = _PAD_COLS["a_z"][0] // D_MODEL

    def body(xs_ref, bm_ref, cm_ref, dt_ref, z_ref, dtb_ref, alog_ref, dsk_ref, nw_ref, ypre_ref, hst_ref,
             dya_ref, dact_ref, dz_ref, ddt_ref, ddtb_ref, dalog_ref, ddsk_ref, dnw_ref, dh_scr):
        first = jnp.logical_and(pl.program_id(0) == 0, pl.program_id(1) == 0)

        @pl.when(first)
        def _():
            ddtb_ref[...] = jnp.zeros_like(ddtb_ref)
            dalog_ref[...] = jnp.zeros_like(dalog_ref)
            ddsk_ref[...] = jnp.zeros_like(ddsk_ref)
            dnw_ref[...] = jnp.zeros_like(dnw_ref)

        @pl.when(pl.program_id(1) == 0)
        def _():
            dh_scr[...] = jnp.zeros_like(dh_scr)

        for g in range(N_GROUPS):
            w256 = pl.ds(256 * g, 256)
            w128 = pl.ds(LANES * g, LANES)
            group(xs_ref.at[:, w256], bm_ref.at[:, w128], cm_ref.at[:, w128], dt_ref.at[:, w128],
                  z_ref.at[:, w256], dtb_ref.at[g], alog_ref.at[g], dsk_ref.at[g], nw_ref.at[g],
                  ypre_ref.at[:, w256], hst_ref.at[g], dya_ref.at[:, w256],
                  dact_ref.at[:, w256], dact_ref.at[:, pl.ds(D_MODEL + LANES * g, LANES)],
                  dact_ref.at[:, pl.ds(D_MODEL + 512 + LANES * g, LANES)], dz_ref.at[:, w256], ddt_ref.at[:, w128],
                  ddtb_ref.at[g], dalog_ref.at[g], ddsk_ref.at[g], dnw_ref.at[g], dh_scr.at[g])

    def group(xs_ref, bm_ref, cm_ref, dt_ref, z_ref, dtb_ref, alog_ref, dsk_ref, nw_ref, ypre_ref, hst_ref,
              dya_ref, dxs_ref, dbm_ref, dcm_ref, dz_ref, ddt_ref, ddtb_ref, dalog_ref, ddsk_ref, dnw_ref,
              dh_scr):
        row, lane, causal, dtv, a_row, acum, acum_t = _ssd_common(dt_ref, dtb_ref, alog_ref)
        lane1 = _iota_row()
        bb = bm_ref[...].astype(BF16)
        cb = cm_ref[...].astype(BF16)
        cbm = _dot_nt(cb, bb)

        z = z_ref[...].astype(F32)
        ypre = ypre_ref[...]
        dya = dya_ref[...]
        sz = _sigmoid(z)
        silu = z * sz
        yg = ypre * silu
        rstd = lax.rsqrt(jnp.sum(yg * yg, axis=1, keepdims=True) * (1.0 / 256.0) + NORM_EPS)
        dnw_ref[...] += jnp.sum(dya * yg * rstd, axis=0, keepdims=True)
        dn = dya * nw_ref[...]
        dyg = rstd * dn - yg * (rstd * rstd * rstd * (1.0 / 256.0)) * jnp.sum(dn * yg, axis=1, keepdims=True)
        dz_ref[...] = (dyg * ypre * (sz * (1.0 + z * (1.0 - sz)))).astype(BF16)
        dy_all = dyg * silu

        dskv = dsk_ref[...]
        da_cols = jnp.zeros((CHUNK, LANES), F32)
        dxt_cols = jnp.zeros((CHUNK, LANES), F32)
        ddsk_row = jnp.zeros((1, LANES), F32)
        dcb = jnp.zeros((CHUNK, CHUNK), F32)
        dc = jnp.zeros((CHUNK, SSM_STATE), F32)
        db = jnp.zeros((CHUNK, SSM_STATE), F32)
        last = _iota_col() == CHUNK - 1
        for pp in range(2):
            r0, r1 = 2 * pp, 2 * pp + 1
            x = xs_ref[:, LANES * pp:LANES * (pp + 1)]
            dy = dy_all[:, LANES * pp:LANES * (pp + 1)]
            lo, dtp, xd, d0, d1, al0, al1, eac, dsp, eal = _ssd_pair(pp, x, dtv, acum, acum_t, causal, lane, row)
            w0, w1 = cbm * d0, cbm * d1
            w0b, w1b = w0.astype(BF16), w1.astype(BF16)
            xdb = xd.astype(BF16)
            dyb = dy.astype(BF16)
            h = hst_ref[pp]
            dhn = dh_scr[pp]
            hb = h.astype(BF16)
            dhb = dhn.astype(BF16)
            g0 = _dot_nt(jnp.where(lo, dy, 0.0).astype(BF16), xdb)
            g1 = _dot_nt(jnp.where(lo, 0.0, dy).astype(BF16), xdb)
            dcb = dcb + g0 * d0 + g1 * d1
            m0, m1 = g0 * w0, g1 * w1
            bdh = _dot_nt(bb, dhb)
            dxd = jnp.where(lo, _dot_tn(w0b, dyb), _dot_tn(w1b, dyb)) + dsp * bdh
            ch = _dot_nt(cb, hb)
            edy = eac * dy
            edyb = edy.astype(BF16)
            xds = xd * dsp
            dc = dc + _dot(edyb, hb)
            db = db + _dot(xds.astype(BF16), dhb)
            dh_scr[pp] = dhn * eal + _dot_tn(edyb, cb)
            t2 = edy * ch
            t3 = xds * bdh
            dhh = dhn * h
            s4_0 = jnp.sum(jnp.sum(jnp.where(row < HEAD_DIM, dhh, 0.0), axis=0, keepdims=True), axis=1, keepdims=True)
            s4_1 = jnp.sum(jnp.sum(dhh, axis=0, keepdims=True), axis=1, keepdims=True) - s4_0
            t23 = t2 - t3
            t23_0 = jnp.sum(jnp.where(lo, t23, 0.0), axis=1, keepdims=True)
            t23_1 = jnp.sum(t23, axis=1, keepdims=True) - t23_0
            c3 = jnp.sum(t3, axis=0, keepdims=True)
            c3_0 = jnp.sum(jnp.where(_iota_row() < HEAD_DIM, c3, 0.0), axis=1, keepdims=True)
            c3_1 = jnp.sum(c3, axis=1, keepdims=True) - c3_0
            dal0 = c3_0 + jnp.exp(al0) * s4_0
            dal1 = c3_1 + jnp.exp(al1) * s4_1
            dac0 = jnp.sum(m0 - m0.T, axis=1, keepdims=True) + t23_0 + jnp.where(last, dal0, 0.0)
            dac1 = jnp.sum(m1 - m1.T, axis=1, keepdims=True) + t23_1 + jnp.where(last, dal1, 0.0)
            da_cols = da_cols + jnp.where(lane == r0, dac0, 0.0) + jnp.where(lane == r1, dac1, 0.0)
            xx = dxd * x
            x0 = jnp.sum(jnp.where(lo, xx, 0.0), axis=1, keepdims=True)
            x1 = jnp.sum(xx, axis=1, keepdims=True) - x0
            dxt_cols = dxt_cols + jnp.where(lane == r0, x0, 0.0) + jnp.where(lane == r1, x1, 0.0)
            dskp = jnp.where((_iota_row() < HEAD_DIM), _col(dskv, r0), _col(dskv, r1))
            dxs_ref[:, LANES * pp:LANES * (pp + 1)] = dxd * dtp + dy * dskp
            yx = jnp.sum(dy * x, axis=0, keepdims=True)
            k0 = jnp.sum(jnp.where((_iota_row() < HEAD_DIM), yx, 0.0), axis=1, keepdims=True)
            k1 = jnp.sum(yx, axis=1, keepdims=True) - k0
            ddsk_row = ddsk_row + jnp.where(lane1 == r0, k0, 0.0) + jnp.where(lane1 == r1, k1, 0.0)
        dcbb = dcb.astype(BF16)
        dcm_ref[...] = dc + _dot(dcbb, bb)
        dbm_ref[...] = db + _dot_tn(dcbb, cb)
        tri_t = (row <= lane).astype(F32)
        dadt = _dot(tri_t, da_cols, precision=HIGHEST)
        ddtv = dadt * a_row + dxt_cols
        dalog_ref[...] += jnp.sum(dadt * dtv, axis=0, keepdims=True) * a_row
        ddt_raw = ddtv * _sigmoid(dt_ref[...] + dtb_ref[...])
        ddt_ref[...] = ddt_raw.astype(BF16)
        ddtb_ref[...] += jnp.sum(ddt_raw, axis=0, keepdims=True)
        ddsk_ref[...] += ddsk_row

    g = N_GROUPS
    rc = lambda c: nc - 1 - c
    par = pl.BlockSpec((g, 1, LANES), lambda i, c: (0, 0, 0))
    parw = pl.BlockSpec((g, 1, 256), lambda i, c: (0, 0, 0))
    wide = pl.BlockSpec((None, CHUNK, D_MODEL), lambda i, c: (i, rc(c), 0))
    blk512 = lambda col: pl.BlockSpec((None, CHUNK, 512), lambda i, c: (i, rc(c), col))
    return pl.pallas_call(
        body, name=name, grid=(b, nc),
        in_specs=[wide, blk512(2), blk512(3), blk512(dt0),
                  pl.BlockSpec((None, CHUNK, D_MODEL), lambda i, c: (i, rc(c), z0)),
                  par, par, par, parw,
                  wide,
                  pl.BlockSpec((None, None, g, 2, CHUNK, SSM_STATE), lambda i, c: (i, rc(c), 0, 0, 0, 0)),
                  wide],
        out_specs=[pl.BlockSpec((None, CHUNK, CONV_DIM), lambda i, c: (i, rc(c), 0)), wide, blk512(0),
                   par, par, par, parw],
        out_shape=[jax.ShapeDtypeStruct((b, s, CONV_DIM), F32), jax.ShapeDtypeStruct((b, s, D_MODEL), BF16),
                   jax.ShapeDtypeStruct((b, s, 512), BF16),
                   jax.ShapeDtypeStruct((g, 1, LANES), F32), jax.ShapeDtypeStruct((g, 1, LANES), F32),
                   jax.ShapeDtypeStruct((g, 1, LANES), F32), jax.ShapeDtypeStruct((g, 1, 256), F32)],
        scratch_shapes=[pltpu.VMEM((g, 2, CHUNK, SSM_STATE), F32)],
        compiler_params=_cp(("arbitrary", "arbitrary")),
    )(xact3, xact3, xact3, gates3, proj3, dtb, alog, dsk, nw, ypre3, hst, dya3)


_FGATE_ROWS = 512


def _fgate_fwd(gates3, fb, *, name):
    b, s, _ = gates3.shape
    rows = min(_FGATE_ROWS, s)
    f0 = _PAD_COLS["a_dt"][1] // LANES

    def body(f_ref, fb_ref, cum_ref, carry):
        @pl.when(pl.program_id(1) == 0)
        def _():
            carry[...] = jnp.zeros_like(carry)

        row = lax.broadcasted_iota(jnp.int32, (rows, rows), 0)
        lane = lax.broadcasted_iota(jnp.int32, (rows, rows), 1)
        tri = (row >= lane).astype(F32)
        lf = -_softplus(-(f_ref[...] + fb_ref[...]))
        cs = _dot(tri, lf, precision=HIGHEST) + carry[0:1, :]
        cum_ref[...] = cs
        carry[0:1, :] = _row(cs, rows - 1)

    return pl.pallas_call(
        body, name=name, grid=(b, s // rows),
        in_specs=[pl.BlockSpec((None, rows, LANES), lambda i, c: (i, c, f0)),
                  pl.BlockSpec((1, LANES), lambda i, c: (0, 0))],
        out_specs=pl.BlockSpec((None, rows, LANES), lambda i, c: (i, c, 0)),
        out_shape=jax.ShapeDtypeStruct((b, s, LANES), F32),
        scratch_shapes=[pltpu.VMEM((8, LANES), F32)],
        compiler_params=_cp(("parallel", "arbitrary")),
    )(gates3, fb)


def _fgate_bwd(gates3, fb, dcum, *, name):
    b, s, _ = gates3.shape
    rows = min(_FGATE_ROWS, s)
    nc = s // rows
    f0 = _PAD_COLS["a_dt"][1] // LANES
    npair = dcum.shape[1]

    def body(f_ref, fb_ref, dc_ref, df_ref, dfb_ref, carry):
        first = jnp.logical_and(pl.program_id(0) == 0, pl.program_id(1) == 0)

        @pl.when(first)
        def _():
            dfb_ref[...] = jnp.zeros_like(dfb_ref)

        @pl.when(pl.program_id(1) == 0)
        def _():
            carry[...] = jnp.zeros_like(carry)

        row = lax.broadcasted_iota(jnp.int32, (rows, rows), 0)
        lane = lax.broadcasted_iota(jnp.int32, (rows, rows), 1)
        tri_t = (row <= lane).astype(F32)
        dc = -jnp.sum(dc_ref[...], axis=0)
        dlf = _dot(tri_t, dc, precision=HIGHEST) + carry[0:1, :]
        carry[0:1, :] = _row(dlf, 0)
        df = dlf * _sigmoid(-(f_ref[...] + fb_ref[...]))
        df_ref[...] = df.astype(BF16)
        dfb_ref[...] += jnp.sum(df, axis=0, keepdims=True)

    return pl.pallas_call(
        body, name=name, grid=(b, nc),
        in_specs=[pl.BlockSpec((None, rows, LANES), lambda i, c: (i, nc - 1 - c, f0)),
                  pl.BlockSpec((1, LANES), lambda i, c: (0, 0)),
                  pl.BlockSpec((None, npair, rows, LANES), lambda i, c: (i, 0, nc - 1 - c, 0))],
        out_specs=[pl.BlockSpec((None, rows, LANES), lambda i, c: (i, nc - 1 - c, 0)),
                   pl.BlockSpec((1, LANES), lambda i, c: (0, 0))],
        out_shape=[jax.ShapeDtypeStruct((b, s, LANES), BF16), jax.ShapeDtypeStruct((1, LANES), F32)],
        scratch_shapes=[pltpu.VMEM((8, LANES), F32)],
        compiler_params=_cp(("arbitrary", "arbitrary")),
    )(gates3, fb, dcum)


_SCALE = HEAD_DIM ** -0.5
_NEG = -1e30


_ST_LSE, _ST_DELTA, _ST_MJ = 0, 2, 8


_SR = 40


def _ck_rep(cum):
    b, s, _ = cum.shape
    t = jnp.transpose(cum[:, :, :N_HEADS], (0, 2, 1)).reshape(b, N_HEADS // 2, 2, s, 1)
    return jnp.broadcast_to(t, (b, N_HEADS // 2, 2, s, LANES))


def _foxt_fwd(proj3, ckrep, *, name, tb):
    b, s, _ = proj3.shape
    nq = s // tb
    assert _ST_MJ + 2 * nq <= _SR
    q0 = _PAD_COLS["c_q"][0] // LANES
    k0 = _PAD_COLS["c_k"][0] // LANES
    v0 = _PAD_COLS["c_v"][0] // LANES
    z0 = _PAD_COLS["c_z"][0] // LANES
    rep = tb // LANES

    def body(q_ref, k_ref, v_ref, z_ref, ck_ref, y_ref, o_ref, st_ref):
        i = pl.program_id(2)
        lane = lax.broadcasted_iota(jnp.int32, (tb, LANES), 1)
        lo = lane < HEAD_DIM
        lo_r = lax.broadcasted_iota(jnp.int32, (LANES, tb), 0) < HEAD_DIM
        srow = lax.broadcasted_iota(jnp.int32, (_SR, tb), 0)
        q = q_ref[...].astype(F32) * _SCALE
        qms = (jnp.where(lo, q, 0.0).astype(BF16), jnp.where(lo, 0.0, q).astype(BF16))
        ones_at = (HEAD_DIM, 0)

        def block(j, carry, diagonal):
            ks = pl.ds(pl.multiple_of(j * tb, tb), tb)
            kb = k_ref[ks, :].astype(BF16)
            v = v_ref[ks, :].astype(F32)
            vts = (jnp.where(lo, v, jnp.where(lane == ones_at[0], 1.0, 0.0)).T.astype(BF16),
                   jnp.where(lo, jnp.where(lane == ones_at[1], 1.0, 0.0), v).T.astype(BF16))
            if diagonal:
                key = lax.broadcasted_iota(jnp.int32, (tb, tb), 0)
                qry = lax.broadcasted_iota(jnp.int32, (tb, tb), 1)
                mask = key <= qry
            ms, ls, acc, st = carry
            new_m, new_l, pvs, alphas = [], [], [], []
            for hh in range(2):
                sc = _dot_nt(kb, qms[hh]) - jnp.tile(ck_ref[hh, ks, :], (1, rep))
                if diagonal:
                    sc = jnp.where(mask, sc, _NEG)
                m_new = jnp.maximum(ms[hh], jnp.max(sc, axis=0, keepdims=True))
                alpha = jnp.exp(ms[hh] - m_new)
                pv = _dot(vts[hh], jnp.exp(sc - m_new).astype(BF16))
                rs = _row(pv[ones_at[hh]:ones_at[hh] + 8, :], 0)
                new_l.append(alpha * ls[hh] + rs)
                new_m.append(m_new)
                pvs.append(pv)
                alphas.append(alpha)
                st = jnp.where(srow == _ST_MJ + 2 * j + hh, m_new, st)
            acc = jnp.where(lo_r, alphas[0] * acc + pvs[0], alphas[1] * acc + pvs[1])
            return (tuple(new_m), tuple(new_l), acc, st)

        neg = jnp.full((1, tb), _NEG, F32)
        zero = jnp.zeros((1, tb), F32)
        init = ((neg, neg), (zero, zero), jnp.zeros((LANES, tb), F32), jnp.zeros((_SR, tb), F32))
        carry = lax.fori_loop(0, i, lambda j, c: block(j, c, False), init)
        ms, ls, acc, st = block(i, carry, True)
        o = (acc / jnp.where(lo_r, ls[0], ls[1])).T
        o_ref[...] = o
        st = jnp.where(srow == _ST_LSE, ms[0] + jnp.log(ls[0]), st)
        st_ref[...] = jnp.where(srow == _ST_LSE + 1, ms[1] + jnp.log(ls[1]), st)
        z = z_ref[...].astype(F32)
        y_ref[...] = (o * (z * _sigmoid(z))).astype(BF16)

    qspec = lambda c0: pl.BlockSpec((None, tb, LANES), lambda bi, p, i: (bi, i, c0 + p))
    kspec = lambda c0: pl.BlockSpec((None, s, LANES), lambda bi, p, i: (bi, 0, c0 + p))
    ospec = pl.BlockSpec((None, tb, LANES), lambda bi, p, i: (bi, i, p))
    return pl.pallas_call(
        body, name=name, grid=(b, N_HEADS // 2, nq),
        in_specs=[qspec(q0), kspec(k0), kspec(v0), qspec(z0),
                  pl.BlockSpec((None, None, 2, s, LANES), lambda bi, p, i: (bi, p, 0, 0, 0))],
        out_specs=[ospec, ospec, pl.BlockSpec((None, None, None, _SR, tb), lambda bi, p, i: (bi, p, i, 0, 0))],
        out_shape=[jax.ShapeDtypeStruct((b, s, D_MODEL), BF16), jax.ShapeDtypeStruct((b, s, D_MODEL), F32),
                   jax.ShapeDtypeStruct((b, N_HEADS // 2, nq, _SR, tb), F32)],
        compiler_params=_cp(("parallel", "parallel", "arbitrary")),
    )(proj3, proj3, proj3, proj3, ckrep)


def _foxt_prep(proj3, o3, stat, dy3, *, name, tb):
    b, s, _ = proj3.shape
    nq = s // tb
    z0 = _PAD_COLS["c_z"][0] // LANES

    def body(z_ref, o_ref, fst_ref, dy_ref, dz_ref, do_ref, st_ref):
        z = z_ref[...].astype(F32)
        sz = _sigmoid(z)
        dy = dy_ref[...]
        o = o_ref[...]
        do = dy * (z * sz)
        dz_ref[...] = (dy * o * (sz * (1.0 + z * (1.0 - sz)))).astype(BF16)
        do_ref[...] = do
        doo = do.astype(BF16).astype(F32) * o
        r8 = lax.broadcasted_iota(jnp.int32, (8, LANES), 0)
        l8 = lax.broadcasted_iota(jnp.int32, (8, LANES), 1)
        pick = jnp.logical_or(jnp.logical_and(r8 == 0, l8 < HEAD_DIM),
                              jnp.logical_and(r8 == 1, l8 >= HEAD_DIM)).astype(F32)
        d8 = _dot(pick, doo, ((1,), (1,)), precision=HIGHEST)
        srow = lax.broadcasted_iota(jnp.int32, (_SR, tb), 0)
        st = jnp.where(srow == _ST_DELTA, _row(d8, 0), fst_ref[...])
        st_ref[...] = jnp.where(srow == _ST_DELTA + 1, _row(d8, 1), st)

    ospec = pl.BlockSpec((None, tb, LANES), lambda bi, p, i: (bi, i, p))
    sspec = pl.BlockSpec((None, None, None, _SR, tb), lambda bi, p, i: (bi, p, i, 0, 0))
    return pl.pallas_call(
        body, name=name, grid=(b, N_HEADS // 2, nq),
        in_specs=[pl.BlockSpec((None, tb, LANES), lambda bi, p, i: (bi, i, z0 + p)), ospec, sspec, ospec],
        out_specs=[ospec, ospec, sspec],
        out_shape=[jax.ShapeDtypeStruct((b, s, D_MODEL), BF16), jax.ShapeDtypeStruct((b, s, D_MODEL), F32),
                   jax.ShapeDtypeStruct((b, N_HEADS // 2, nq, _SR, tb), F32)],
        compiler_params=_cp(("parallel", "parallel", "parallel")),
    )(proj3, o3, stat, dy3)


def _foxt_bwd(proj3, ckrep, do3, stats, *, name, tb):
    b, s, _ = proj3.shape
    nq = s // tb
    q0 = _PAD_COLS["c_q"][0] // LANES
    k0 = _PAD_COLS["c_k"][0] // LANES
    v0 = _PAD_COLS["c_v"][0] // LANES
    rep = tb // LANES

    def body(q_ref, do_ref, st_ref, k_ref, v_ref, ck_ref, dq_ref, dk_ref, dv_ref, cs_ref):
        j = pl.program_id(2)
        lane = lax.broadcasted_iota(jnp.int32, (tb, LANES), 1)
        lo = lane < HEAD_DIM
        lo_r = lax.broadcasted_iota(jnp.int32, (LANES, tb), 0) < HEAD_DIM

        @pl.when(j == 0)
        def _():
            dq_ref[...] = jnp.zeros_like(dq_ref)

        kf = k_ref[...].astype(F32)
        kb = kf.astype(BF16)
        kt = kf.T.astype(BF16)
        vb = v_ref[...].astype(BF16)
        cks = (jnp.tile(ck_ref[0], (1, rep)), jnp.tile(ck_ref[1], (1, rep)))

        def block(i, carry, diagonal):
            qs = pl.ds(pl.multiple_of(i * tb, tb), tb)
            q = q_ref[qs, :].astype(F32) * _SCALE
            do = do_ref[qs, :]
            st = st_ref[i]
            if diagonal:
                key = lax.broadcasted_iota(jnp.int32, (tb, tb), 0)
                qry = lax.broadcasted_iota(jnp.int32, (tb, tb), 1)
                mask = key <= qry
            dk, dv, cs = carry
            new_cs, dqs = [], []
            for hh in range(2):
                sel = lo if hh == 0 else jnp.logical_not(lo)
                qm = jnp.where(sel, q, 0.0).astype(BF16)
                dom = jnp.where(sel, do, 0.0).astype(BF16)
                sc = _dot_nt(kb, qm) - cks[hh]
                if diagonal:
                    sc = jnp.where(mask, sc, _NEG)
                mj = _row(st, _ST_MJ + 2 * j + hh)
                w = jnp.exp(mj - _row(st, _ST_LSE + hh))
                ph = jnp.exp(sc - mj).astype(BF16).astype(F32) * w
                ds = ph * (_dot_nt(vb, dom) - _row(st, _ST_DELTA + hh))
                dsb = ds.astype(BF16)
                dv = dv + _dot(ph.astype(BF16), dom)
                dk = dk + _dot(dsb, qm)
                new_cs.append(cs[hh] + jnp.sum(ds, axis=1, keepdims=True))
                dqs.append(_dot(kt, dsb))
            dq_ref[i] += jnp.where(lo_r, dqs[0], dqs[1]) * _SCALE
            return (dk, dv, tuple(new_cs))

        zcol = jnp.zeros((tb, 1), F32)
        init = (jnp.zeros((tb, LANES), F32), jnp.zeros((tb, LANES), F32), (zcol, zcol))
        carry = block(j, init, True)
        dk, dv, cs = lax.fori_loop(j + 1, nq, lambda i, c: block(i, c, False), carry)
        dk_ref[...] = dk.astype(BF16)
        dv_ref[...] = dv.astype(BF16)
        p2 = 2 * pl.program_id(1)
        cs_ref[...] = jnp.where(lane == p2, cs[0], jnp.where(lane == p2 + 1, cs[1], 0.0))

    full = lambda c0: pl.BlockSpec((None, s, LANES), lambda bi, p, j: (bi, 0, c0 + p))
    kspec = lambda c0: pl.BlockSpec((None, tb, LANES), lambda bi, p, j: (bi, j, c0 + p))
    ko = pl.BlockSpec((None, tb, LANES), lambda bi, p, j: (bi, j, p))
    sall = pl.BlockSpec((None, None, nq, _SR, tb), lambda bi, p, j: (bi, p, 0, 0, 0))
    dqspec = pl.BlockSpec((None, None, nq, LANES, tb), lambda bi, p, j: (bi, p, 0, 0, 0))
    return pl.pallas_call(
        body, name=name, grid=(b, N_HEADS // 2, nq),
        in_specs=[full(q0), full(0), sall, kspec(k0), kspec(v0),
                  pl.BlockSpec((None, None, 2, tb, LANES), lambda bi, p, j: (bi, p, 0, j, 0))],
        out_specs=[dqspec, ko, ko, pl.BlockSpec((None, None, tb, LANES), lambda bi, p, j: (bi, p, j, 0))],
        out_shape=[jax.ShapeDtypeStruct((b, N_HEADS // 2, nq, LANES, tb), F32),
                   jax.ShapeDtypeStruct((b, s, D_MODEL), BF16), jax.ShapeDtypeStruct((b, s, D_MODEL), BF16),
                   jax.ShapeDtypeStruct((b, N_HEADS // 2, s, LANES), F32)],
        compiler_params=_cp(("parallel", "parallel", "arbitrary")),
    )(proj3, do3, stats, proj3, proj3, ckrep)


def _rope(x, cos, sin_signed):
    w = x.shape[1]
    lane = lax.broadcasted_iota(jnp.int32, x.shape, 1)
    first = (lane % HEAD_DIM) < (HEAD_DIM // 2)
    rot = jnp.where(first, pltpu.roll(x, w - HEAD_DIM // 2, 1), pltpu.roll(x, HEAD_DIM // 2, 1))
    return x * cos + rot * sin_signed


_QB = 4
_QROWS = _QB * CHUNK


def _swa_keys(g, kc_ref, kp_ref, vc_ref, vp_ref, cq_ref, sq_ref, cp_ref, sp_ref):
    def both_halves(x):
        x = x.astype(F32)
        lane = lax.broadcasted_iota(jnp.int32, x.shape, 1)
        keep = (lane // HEAD_DIM) == (g % 2)
        return jnp.where(keep, x, pltpu.roll(x, HEAD_DIM, 1))

    cq, sq, cpv, spv = cq_ref[...], sq_ref[...], cp_ref[...], sp_ref[...]
    kc = _rope(both_halves(kc_ref[...]), cq, sq).astype(BF16)
    kp = _rope(both_halves(kp_ref[...]), cpv, spv).astype(BF16)
    return cq, sq, cpv, spv, kc, kp, both_halves(vc_ref[...]).astype(BF16), both_halves(vp_ref[...]).astype(BF16)


def _swa_stack(pairs, lo):
    return jnp.concatenate([jnp.where(lo, pairs[0], 0.0), jnp.where(lo, 0.0, pairs[0]),
                            jnp.where(lo, pairs[1], 0.0), jnp.where(lo, 0.0, pairs[1])], axis=0).astype(BF16)


def _swa_mask4(prev_valid):
    r = lax.broadcasted_iota(jnp.int32, (4 * CHUNK, 2 * CHUNK), 0) & (CHUNK - 1)
    c = lax.broadcasted_iota(jnp.int32, (4 * CHUNK, 2 * CHUNK), 1)
    own = jnp.logical_and(c >= CHUNK, c - CHUNK <= r)
    before = jnp.logical_and(c < CHUNK, c > r)
    if prev_valid is True:
        return jnp.logical_or(own, before)
    return jnp.logical_or(own, jnp.logical_and(before, prev_valid))


def _swa_sink4(skv):
    return jnp.concatenate([jnp.broadcast_to(_col(skv, j), (CHUNK, 1)) for j in range(4)], axis=0)


def _swa_specs(order):
    def spec(shape, fn):
        return pl.BlockSpec(shape, lambda *ids: fn(*order(*ids)))

    q0 = _PAD_COLS["b_q"][0] // 256
    z0 = _PAD_COLS["b_z"][0] // 256
    k0 = _PAD_COLS["b_k"][0] // LANES
    v0 = _PAD_COLS["b_v"][0] // LANES
    prev = lambda i: jnp.maximum(_QB * i - 1, 0)
    return dict(
        kc=spec((None, _QROWS, LANES), lambda bi, g, i: (bi, i, k0 + g // 2)),
        kp=spec((None, CHUNK, LANES), lambda bi, g, i: (bi, prev(i), k0 + g // 2)),
        vc=spec((None, _QROWS, LANES), lambda bi, g, i: (bi, i, v0 + g // 2)),
        vp=spec((None, CHUNK, LANES), lambda bi, g, i: (bi, prev(i), v0 + g // 2)),
        q=spec((None, _QROWS, 256), lambda bi, g, i: (bi, i, q0 + g)),
        z=spec((None, _QROWS, 256), lambda bi, g, i: (bi, i, z0 + g)),
        blk=spec((None, _QROWS, 256), lambda bi, g, i: (bi, i, g)),
        kcur=spec((None, _QROWS, LANES), lambda bi, g, i: (bi, i, g)),
        kstep=spec((None, CHUNK, LANES), lambda bi, g, i: (bi, i, g)),
        tcur=spec((_QROWS, LANES), lambda bi, g, i: (i, 0)),
        tprev=spec((CHUNK, LANES), lambda bi, g, i: (prev(i), 0)),
        sk=spec((None, 1, LANES), lambda bi, g, i: (g, 0, 0)))


def _swa_fwd(proj3, cos, sin, sinks, *, name):
    b, s, _ = proj3.shape

    def body(q_ref, z_ref, kc_ref, kp_ref, vc_ref, vp_ref, cq_ref, sq_ref, cp_ref, sp_ref, sk_ref,
             y_ref, o_ref, lse_ref):
        i = pl.program_id(2)
        cq_all, sq_all, _, _, kc_all, kp0, vc_all, vp0 = _swa_keys(
            pl.program_id(1), kc_ref, kp_ref, vc_ref, vp_ref, cq_ref, sq_ref, cp_ref, sp_ref)
        lo = lax.broadcasted_iota(jnp.int32, (CHUNK, LANES), 1) < HEAD_DIM
        sink4 = _swa_sink4(sk_ref[...])
        for u in range(_QB):
            rs = slice(CHUNK * u, CHUNK * (u + 1))
            ps = slice(CHUNK * (u - 1), CHUNK * u)
            cq, sq = cq_all[rs], sq_all[rs]
            kp, vp = (kp0, vp0) if u == 0 else (kc_all[ps], vc_all[ps])
            kk = jnp.concatenate([kp, kc_all[rs]], axis=0)
            vv = jnp.concatenate([vp, vc_all[rs]], axis=0)
            q4 = _swa_stack([_rope(q_ref[rs, LANES * pp:LANES * (pp + 1)].astype(F32), cq, sq) * _SCALE
                             for pp in range(2)], lo)
            sc = jnp.where(_swa_mask4(True if u > 0 else i > 0), _dot_nt(q4, kk), _NEG)
            m = jnp.maximum(jnp.max(sc, axis=1, keepdims=True), sink4)
            pr = jnp.exp(sc - m)
            l = jnp.sum(pr, axis=1, keepdims=True) + jnp.exp(sink4 - m)
            o4 = _dot(pr.astype(BF16), vv) / l
            lse4 = m + jnp.log(l)
            for pp in range(2):
                ls = slice(LANES * pp, LANES * (pp + 1))
                h0 = slice(2 * CHUNK * pp, 2 * CHUNK * pp + CHUNK)
                h1 = slice(2 * CHUNK * pp + CHUNK, 2 * CHUNK * (pp + 1))
                o = jnp.where(lo, o4[h0], o4[h1])
                z = z_ref[rs, ls].astype(F32)
                o_ref[rs, ls] = o
                lse_ref[rs, ls] = jnp.where(lo, lse4[h0], lse4[h1])
                y_ref[rs, ls] = (o * (z * _sigmoid(z))).astype(BF16)

    sp = _swa_specs(lambda bi, g, i: (bi, g, i))
    return pl.pallas_call(
        body, name=name, grid=(b, N_GROUPS, s // _QROWS),
        in_specs=[sp["q"], sp["z"], sp["kc"], sp["kp"], sp["vc"], sp["vp"],
                  sp["tcur"], sp["tcur"], sp["tprev"], sp["tprev"], sp["sk"]],
        out_specs=[sp["blk"], sp["blk"], sp["blk"]],
        out_shape=[jax.ShapeDtypeStruct((b, s, D_MODEL), BF16)] + [jax.ShapeDtypeStruct((b, s, D_MODEL), F32)] * 2,
        compiler_params=_cp(("parallel", "parallel", "parallel")),
    )(proj3, proj3, proj3, proj3, proj3, proj3, cos, sin, cos, sin, sinks)


def _swa_bwd(proj3, cos, sin, sinks, o3, lse3, dy3, *, name):
    b, s, _ = proj3.shape

    def body(q_ref, z_ref, kc_ref, kp_ref, vc_ref, vp_ref, cq_ref, sq_ref, cp_ref, sp_ref, sk_ref,
             o_ref, lse_ref, dy_ref, dq_ref, dz_ref, dkc_ref, dkp_ref, dvc_ref, dvp_ref, dsk_ref):
        i = pl.program_id(2)
        first = jnp.logical_and(pl.program_id(1) == 0, i == 0)

        @pl.when(first)
        def _():
            dsk_ref[...] = jnp.zeros_like(dsk_ref)

        cq_all, sq_all, cpv, spv, kc_all, kp0, vc_all, vp0 = _swa_keys(
            pl.program_id(0), kc_ref, kp_ref, vc_ref, vp_ref, cq_ref, sq_ref, cp_ref, sp_ref)
        lo = lax.broadcasted_iota(jnp.int32, (CHUNK, LANES), 1) < HEAD_DIM
        lane1 = lax.broadcasted_iota(jnp.int32, (1, LANES), 1)
        sink4 = _swa_sink4(sk_ref[...])
        zero = jnp.zeros((CHUNK, LANES), F32)
        dks = [zero] * (_QB + 1)
        dvs = [zero] * (_QB + 1)
        dsk_row = jnp.zeros((1, LANES), F32)
        for u in range(_QB):
            rs = slice(CHUNK * u, CHUNK * (u + 1))
            ps = slice(CHUNK * (u - 1), CHUNK * u)
            cq, sq = cq_all[rs], sq_all[rs]
            kp, vp = (kp0, vp0) if u == 0 else (kc_all[ps], vc_all[ps])
            kk = jnp.concatenate([kp, kc_all[rs]], axis=0)
            vv = jnp.concatenate([vp, vc_all[rs]], axis=0)
            q4 = _swa_stack([_rope(q_ref[rs, LANES * pp:LANES * (pp + 1)].astype(F32), cq, sq) * _SCALE
                             for pp in range(2)], lo)
            dos, lses = [], []
            for pp in range(2):
                ls = slice(LANES * pp, LANES * (pp + 1))
                z = z_ref[rs, ls].astype(F32)
                sz = _sigmoid(z)
                dy = dy_ref[rs, ls]
                dos.append(dy * (z * sz))
                dz_ref[rs, ls] = (dy * o_ref[rs, ls] * (sz * (1.0 + z * (1.0 - sz)))).astype(BF16)
                lse = lse_ref[rs, ls]
                lses += [_col(lse, 0), _col(lse, HEAD_DIM)]
            do4 = _swa_stack(dos, lo)
            lse4 = jnp.concatenate(lses, axis=0)
            pr = jnp.exp(jnp.where(_swa_mask4(True if u > 0 else i > 0), _dot_nt(q4, kk), _NEG) - lse4)
            dp = _dot_nt(do4, vv)
            dl = jnp.sum(pr * dp, axis=1, keepdims=True)
            ds = (pr * (dp - dl)).astype(BF16)
            dsink = -jnp.exp(sink4 - lse4) * dl
            for j in range(4):
                dsk_row = dsk_row + jnp.where(
                    lane1 == j, jnp.sum(dsink[CHUNK * j:CHUNK * (j + 1)], axis=0, keepdims=True), 0.0)
            dq4 = _dot(ds, kk)
            dkk = _dot_tn(ds, q4)
            dvv = _dot_tn(pr.astype(BF16), do4)
            dks[u], dks[u + 1] = dks[u] + dkk[:CHUNK], dks[u + 1] + dkk[CHUNK:]
            dvs[u], dvs[u + 1] = dvs[u] + dvv[:CHUNK], dvs[u + 1] + dvv[CHUNK:]
            for pp in range(2):
                h0 = slice(2 * CHUNK * pp, 2 * CHUNK * pp + CHUNK)
                h1 = slice(2 * CHUNK * pp + CHUNK, 2 * CHUNK * (pp + 1))
                dq_ref[rs, LANES * pp:LANES * (pp + 1)] = _rope(
                    jnp.where(lo, dq4[h0], dq4[h1]) * _SCALE, cq, -sq).astype(BF16)
        fold = lambda v: v + pltpu.roll(v, HEAD_DIM, 1)
        dkp_ref[...] = fold(_rope(dks[0], cpv, -spv))
        dvp_ref[...] = fold(dvs[0])
        for u in range(_QB):
            rs = slice(CHUNK * u, CHUNK * (u + 1))
            dkc_ref[rs, :] = fold(_rope(dks[u + 1], cq_all[rs], -sq_all[rs]))
            dvc_ref[rs, :] = fold(dvs[u + 1])
        dsk_ref[...] += dsk_row

    sp = _swa_specs(lambda g, bi, i: (bi, g, i))
    kv_shape = jax.ShapeDtypeStruct((b, s, 512), F32)
    kvp_shape = jax.ShapeDtypeStruct((b, s // _QB, 512), F32)
    return pl.pallas_call(
        body, name=name, grid=(N_GROUPS, b, s // _QROWS),
        in_specs=[sp["q"], sp["z"], sp["kc"], sp["kp"], sp["vc"], sp["vp"],
                  sp["tcur"], sp["tcur"], sp["tprev"], sp["tprev"], sp["sk"], sp["blk"], sp["blk"], sp["blk"]],
        out_specs=[sp["blk"], sp["blk"], sp["kcur"], sp["kstep"], sp["kcur"], sp["kstep"], sp["sk"]],
        out_shape=[jax.ShapeDtypeStruct((b, s, D_MODEL), BF16), jax.ShapeDtypeStruct((b, s, D_MODEL), BF16),
                   kv_shape, kvp_shape, kv_shape, kvp_shape, jax.ShapeDtypeStruct((N_GROUPS, 1, LANES), F32)],
        compiler_params=_cp(("arbitrary", "arbitrary", "arbitrary")),
    )(proj3, proj3, proj3, proj3, proj3, proj3, cos, sin, cos, sin, sinks, o3, lse3, dy3)


def _swa_fold(dkc, dkp, dvc, dvp, *, name):
    b, s, _ = dkc.shape
    ns = s // _QROWS

    def body(kc_ref, kp_ref, vc_ref, vp_ref, dk_ref, dv_ref):
        has_next = pl.program_id(1) < ns - 1
        lo = lax.broadcasted_iota(jnp.int32, (_QROWS, LANES), 1) < HEAD_DIM
        row = lax.broadcasted_iota(jnp.int32, (_QROWS, 512), 0)
        last_block = jnp.logical_and(row >= _QROWS - CHUNK, has_next)
        for cur, nxt, out in ((kc_ref, kp_ref, dk_ref), (vc_ref, vp_ref, dv_ref)):
            tot = cur[...] + jnp.where(last_block, jnp.tile(nxt[...], (_QB, 1)), 0.0)
            for j in range(2):
                out[:, LANES * j:LANES * (j + 1)] = jnp.where(
                    lo, tot[:, 256 * j:256 * j + LANES], tot[:, 256 * j + LANES:256 * (j + 1)]).astype(BF16)

    cur = pl.BlockSpec((None, _QROWS, 512), lambda bi, i: (bi, i, 0))
    nxt = pl.BlockSpec((None, CHUNK, 512), lambda bi, i: (bi, jnp.minimum(i + 1, ns - 1), 0))
    out = pl.BlockSpec((None, _QROWS, 256), lambda bi, i: (bi, i, 0))
    sh = jax.ShapeDtypeStruct((b, s, 256), BF16)
    return pl.pallas_call(
        body, name=name, grid=(b, ns), in_specs=[cur, nxt, cur, nxt], out_specs=[out, out], out_shape=[sh, sh],
        compiler_params=_cp(("parallel", "parallel")),
    )(dkc, dkp, dvc, dvp)


def _branch_fwd(ys, proj, gb, wp, wo, x, *, name, tm=256):
    t = proj.shape[0]
    g0 = _PAD_COLS["gates"][0] // D_MODEL

    def body(g_ref, a_ref, b_ref, c_ref, gb_ref, wp_ref, wo_ref, x_ref, ba_ref, bb_ref, bc_ref, m_ref, xn_ref):
        acc = None
        for i, (y, br) in enumerate(((a_ref, ba_ref), (b_ref, bb_ref), (c_ref, bc_ref))):
            bri = _dot(y[...], wp_ref[i])
            br[...] = bri
            gate = _sigmoid(g_ref[:, D_MODEL * i:D_MODEL * (i + 1)].astype(F32) + gb_ref[i:i + 1, :])
            acc = gate * bri if acc is None else acc + gate * bri
        mb = acc.astype(BF16)
        m_ref[...] = mb
        xn_ref[...] = x_ref[...] + _dot(mb, wo_ref[...])

    row = pl.BlockSpec((tm, D_MODEL), lambda i: (i, 0))
    rowf = jax.ShapeDtypeStruct((t, D_MODEL), F32)
    outs = pl.pallas_call(
        body, name=name, grid=(t // tm,),
        in_specs=[pl.BlockSpec((tm, 3 * D_MODEL), lambda i: (i, g0)), row, row, row,
                  pl.BlockSpec((3, D_MODEL), lambda i: (0, 0)),
                  pl.BlockSpec((3, D_MODEL, D_MODEL), lambda i: (0, 0, 0)),
                  pl.BlockSpec((D_MODEL, D_MODEL), lambda i: (0, 0)), row],
        out_specs=[row, row, row, row, row],
        out_shape=[rowf, rowf, rowf, jax.ShapeDtypeStruct((t, D_MODEL), BF16), rowf],
        compiler_params=_cp(("parallel",)),
    )(proj, ys[0], ys[1], ys[2], gb, wp, wo, x)
    return outs[:3], outs[3], outs[4]


def _branch_bwd(dx, proj, br, gb, wp, wo, *, name, tm=256):
    t = proj.shape[0]
    g0 = _PAD_COLS["gates"][0] // D_MODEL

    def body(g_ref, a_ref, b_ref, c_ref, gb_ref, wp_ref, wo_ref, dx_ref,
             da_ref, db_ref, dc_ref, dg_ref, dgb_ref, ya_ref, yb_ref, yc_ref):
        @pl.when(pl.program_id(0) == 0)
        def _():
            dgb_ref[...] = jnp.zeros_like(dgb_ref)

        dmv = _dot_nt(dx_ref[...].astype(BF16), wo_ref[...])
        for i, (r, dr, dy) in enumerate(((a_ref, da_ref, ya_ref), (b_ref, db_ref, yb_ref), (c_ref, dc_ref, yc_ref))):
            gate = _sigmoid(g_ref[:, D_MODEL * i:D_MODEL * (i + 1)].astype(F32) + gb_ref[i:i + 1, :])
            dbr = (dmv * gate).astype(BF16)
            dr[...] = dbr
            dg = dmv * r[...] * gate * (1.0 - gate)
            dg_ref[:, D_MODEL * i:D_MODEL * (i + 1)] = dg.astype(BF16)
            dgb_ref[i:i + 1, :] += jnp.sum(dg, axis=0, keepdims=True)
            dy[...] = _dot_nt(dbr, wp_ref[i])

    row = pl.BlockSpec((tm, D_MODEL), lambda i: (i, 0))
    rowb = jax.ShapeDtypeStruct((t, D_MODEL), BF16)
    rowf = jax.ShapeDtypeStruct((t, D_MODEL), F32)
    outs = pl.pallas_call(
        body, name=name, grid=(t // tm,),
        in_specs=[pl.BlockSpec((tm, 3 * D_MODEL), lambda i: (i, g0)), row, row, row,
                  pl.BlockSpec((3, D_MODEL), lambda i: (0, 0)),
                  pl.BlockSpec((3, D_MODEL, D_MODEL), lambda i: (0, 0, 0)),
                  pl.BlockSpec((D_MODEL, D_MODEL), lambda i: (0, 0)), row],
        out_specs=[row, row, row, pl.BlockSpec((tm, 3 * D_MODEL), lambda i: (i, 0)),
                   pl.BlockSpec((8, D_MODEL), lambda i: (0, 0)), row, row, row],
        out_shape=[rowb, rowb, rowb, jax.ShapeDtypeStruct((t, 3 * D_MODEL), BF16),
                   jax.ShapeDtypeStruct((8, D_MODEL), F32), rowf, rowf, rowf],
        compiler_params=_cp(("arbitrary",)),
    )(proj, br[0], br[1], br[2], gb, wp, wo, dx)
    return outs[:3], outs[3], outs[4], outs[5:]


def _rope_tables(s):
    pos = jnp.arange(s, dtype=F32)
    inv_freq = ROPE_THETA ** (-jnp.arange(0, HEAD_DIM, 2, dtype=F32) / HEAD_DIM)
    ang = pos[:, None] * inv_freq[None, :]
    cos, sin = jnp.cos(ang), jnp.sin(ang)
    return jnp.tile(cos, (1, 4)), jnp.tile(jnp.concatenate([-sin, sin], axis=1), (1, 2))


def _layer_params(wl):
    return dict(
        dtb=_group_lanes(wl["dt_bias"]), alog=_group_lanes(wl["a_log"]), dsk=_group_lanes(wl["d_skip"]),
        nw=wl["ssm_norm_w"].reshape(N_GROUPS, 1, 256), sinks=_group_lanes(wl["sinks"]),
        fb=jnp.pad(wl["f_bias"], (0, LANES - N_HEADS)).reshape(1, LANES))


def _layer_fwd(x, wl, tabs, bsz, li, tb):
    t = x.shape[0]
    s = t // bsz
    cos, sin = tabs
    lp = _layer_params(wl)
    n = lambda k: f"l{li}_{k}"
    h, h_t = _rms_fwd(x, wl["norm_w"], name=n("rms_fwd"))
    proj = _mm(h, wl["w_in"], tm=1024, tn=1536, tk=1024, out_dtype=BF16, name=n("mm_proj"))
    proj3 = proj.reshape(bsz, s, N_PAD)
    g0, gw = _PAD_COLS["a_dt"][0], _PAD_COLS["a_dt"][1] + _PAD_COLS["c_f"][1]
    gates3 = _mm(h, wl["w_in"][:, g0:g0 + gw], tm=1024, tn=gw, tk=1024, name=n("mm_gates")).reshape(bsz, s, gw)
    xact3 = _conv_fwd(proj3, wl["conv_w"], wl["conv_b"], name=n("conv_fwd"))
    ya3, ypre3, hst = _ssd_fwd(proj3, gates3, xact3, lp["dtb"], lp["alog"], lp["dsk"], lp["nw"], name=n("ssd_fwd"))
    yb3, ob3, lseb3 = _swa_fwd(proj3, cos, sin, lp["sinks"], name=n("swa_fwd"))
    cum = _fgate_fwd(gates3, lp["fb"], name=n("fgate_fwd"))
    cum_t = _ck_rep(cum)
    yc3, oc3, statc3 = _foxt_fwd(proj3, cum_t, name=n("fox_fwd"), tb=tb)
    ys = [v.reshape(t, D_MODEL) for v in (ya3, yb3, yc3)]
    br, merged, x_new = _branch_fwd(ys, proj, wl["gate_bias"], wl["w_proj"], wl["w_out"], x, name=n("branch_fwd"))
    saved = dict(x=x, h_t=h_t, proj=proj, gates3=gates3, xact3=xact3, ypre3=ypre3, hst=hst, ob3=ob3, lseb3=lseb3,
                 cum_t=cum_t, oc3=oc3, statc3=statc3, ys=ys, br=br, merged=merged, lp=lp)
    return x_new, saved


def _layer_bwd(dx, wl, sv, tabs, bsz, li, tb):
    t = dx.shape[0]
    s = t // bsz
    cos, sin = tabs
    lp = sv["lp"]
    n = lambda k: f"l{li}_{k}"
    proj = sv["proj"]
    proj3 = proj.reshape(bsz, s, N_PAD)
    g = {}
    g["w_out"] = _mm(sv["merged"], dx, ta=True, tm=1024, tn=1024, tk=512, name=n("mm_dwout"))
    dbr, dgates, dgb, dys = _branch_bwd(dx, proj, sv["br"], wl["gate_bias"], wl["w_proj"], wl["w_out"],
                                        name=n("branch_bwd"))
    g["gate_bias"] = dgb[:3]
    g["w_proj"] = jnp.stack([_mm(sv["ys"][i], dbr[i], ta=True, tm=1024, tn=1024, tk=512, name=n(f"mm_dwproj{i}"))
                             for i in range(3)])
    dy3 = [v.reshape(bsz, s, D_MODEL) for v in dys]

    (dact, daz, dadt, ddtb, dalog, ddsk, dnw) = _ssd_bwd(
        proj3, sv["gates3"], sv["xact3"], lp["dtb"], lp["alog"], lp["dsk"], lp["nw"], sv["ypre3"], sv["hst"], dy3[0],
        name=n("ssd_bwd"))
    g["dt_bias"], g["a_log"], g["d_skip"] = _ungroup_lanes(ddtb), _ungroup_lanes(dalog), _ungroup_lanes(ddsk)
    g["ssm_norm_w"] = dnw.reshape(D_MODEL)
    dxbc, dwb = _conv_bwd(proj3, wl["conv_w"], wl["conv_b"], dact, name=n("conv_bwd"))
    g["conv_w"], g["conv_b"] = dwb[:CONV_WIDTH], dwb[CONV_WIDTH]

    dbq, dbz, dkc, dkp, dvc, dvp, dsk = _swa_bwd(proj3, cos, sin, lp["sinks"], sv["ob3"],
                                                 sv["lseb3"], dy3[1], name=n("swa_bwd"))
    g["sinks"] = _ungroup_lanes(dsk)

    dbk, dbv = _swa_fold(dkc, dkp, dvc, dvp, name=n("swa_fold"))

    dcz, do3, stats = _foxt_prep(proj3, sv["oc3"], sv["statc3"], dy3[2], name=n("fox_prep"), tb=tb)
    dqt, dck, dcv, csum = _foxt_bwd(proj3, sv["cum_t"], do3, stats, name=n("fox_bwd"), tb=tb)
    dcq = jnp.transpose(dqt, (0, 2, 4, 1, 3)).reshape(bsz, s, D_MODEL)
    dcf, dfb = _fgate_bwd(sv["gates3"], lp["fb"], csum, name=n("fgate_bwd"))
    g["f_bias"] = dfb[0, :N_HEADS]

    parts = {"gates": dgates.reshape(bsz, s, 3 * D_MODEL), "xbc": dxbc, "a_z": daz, "b_q": dbq, "b_z": dbz,
             "c_q": dcq, "c_k": dck, "c_v": dcv, "c_z": dcz, "b_k": dbk, "b_v": dbv, "a_dt": dadt, "c_f": dcf}
    dproj = jnp.concatenate([parts[name].astype(BF16) for name, _ in _PAD_ORDER]
                            + [jnp.zeros((bsz, s, N_PAD - N_USED), BF16)], axis=2).reshape(t, N_PAD)
    dh = _mm(dproj, wl["w_in"], tb=True, tm=1024, tn=1024, tk=1536, name=n("mm_dh"))
    g["w_in"] = _unpad_w_in(_mm(sv["h_t"], dproj, tm=1024, tn=768, tk=2048, name=n("mm_dwin")))
    dx_in, dnorm = _rms_bwd(sv["x"], wl["norm_w"], dh, dx, name=n("rms_bwd"))
    g["norm_w"] = dnorm[0]
    return dx_in, g


def _local_step(x, target, wls, final_norm_w, tb=1024):
    bsz, s, d = x.shape
    t = bsz * s
    tabs = _rope_tables(s)
    xc = x.reshape(t, d)
    saved = []
    for li, wl in enumerate(wls):
        xc, sv = _layer_fwd(xc, wl, tabs, bsz, li, tb)
        saved.append(sv)
    loss, dx, dfw = _final_loss(xc, final_norm_w, target.reshape(t, d), name="final_loss")
    grads = [None] * len(wls)
    for li in reversed(range(len(wls))):
        dx, grads[li] = _layer_bwd(dx, wls[li], saved[li], tabs, bsz, li, tb)
    return loss[0, 0], dx.reshape(bsz, s, d), grads, dfw[0]


_HBM = pl.BlockSpec(memory_space=pltpu.HBM)


def _chip_peers(x, y):
    return [(1 - x, y), (x, 1 - y), (1 - x, 1 - y)]


def _gather_weights(arrs, *, name):
    n = len(arrs)

    def body(*refs):
        ins, outs = refs[:n], refs[n:2 * n]
        ici_send, ici_recv, d2d_send, d2d_recv = refs[2 * n:]
        x, y, c = lax.axis_index("x"), lax.axis_index("y"), lax.axis_index("c")
        me = 2 * x + y
        peers = _chip_peers(x, y)
        sib = (x, y, 1 - c)
        sends, fwds = [], []
        for a in range(n):
            for k, (px, py) in enumerate(peers):
                cp = pltpu.make_async_remote_copy(
                    src_ref=ins[a].at[c], dst_ref=outs[a].at[me, c], send_sem=ici_send.at[a, k],
                    recv_sem=ici_recv.at[a, k], device_id=(px, py, c), device_id_type=MESH)
                cp.start()
                sends.append(cp)
        for a in range(n):
            for k, (px, py) in enumerate(peers):
                slot = 2 * px + py
                pltpu.make_async_remote_copy(
                    src_ref=ins[a].at[c], dst_ref=outs[a].at[slot, c], send_sem=ici_send.at[a, k],
                    recv_sem=ici_recv.at[a, k], device_id=(px, py, c), device_id_type=MESH).wait_recv()
                fw = pltpu.make_async_remote_copy(
                    src_ref=outs[a].at[slot, c], dst_ref=outs[a].at[slot, c], send_sem=d2d_send.at[a, k],
                    recv_sem=d2d_recv.at[a, k], device_id=sib, device_id_type=MESH)
                fw.start()
                fwds.append(fw)
        for a in range(n):
            for k, (px, py) in enumerate(peers):
                slot = 2 * px + py
                pltpu.make_async_remote_copy(
                    src_ref=outs[a].at[slot, 1 - c], dst_ref=outs[a].at[slot, 1 - c], send_sem=d2d_send.at[a, k],
                    recv_sem=d2d_recv.at[a, k], device_id=sib, device_id_type=MESH).wait_recv()
        for cp in sends + fwds:
            cp.wait_send()

    out_shape = [jax.ShapeDtypeStruct((N_CHIPS,) + a.shape, a.dtype) for a in arrs]
    return pl.pallas_call(
        body, name=name, out_shape=out_shape, in_specs=[_HBM] * n, out_specs=[_HBM] * n,
        scratch_shapes=[pltpu.SemaphoreType.DMA((n, 3)), pltpu.SemaphoreType.DMA((n, 3)),
                        pltpu.SemaphoreType.DMA((n, 3)), pltpu.SemaphoreType.DMA((n, 3))],
    )(*arrs)


def _pair_exchange(arrs, *, name):
    n = len(arrs)

    def body(*refs):
        ins, outs = refs[:n], refs[n:2 * n]
        send, recv = refs[2 * n:]
        x, y, c = lax.axis_index("x"), lax.axis_index("y"), lax.axis_index("c")
        sib = (x, y, 1 - c)
        cps = []
        for a in range(n):
            for k in range(N_CHIPS):
                cp = pltpu.make_async_remote_copy(
                    src_ref=ins[a].at[k, 1 - c], dst_ref=outs[a].at[k], send_sem=send.at[a, k],
                    recv_sem=recv.at[a, k], device_id=sib, device_id_type=MESH)
                cp.start()
                cps.append(cp)
        for cp in cps:
            cp.wait()

    out_shape = [jax.ShapeDtypeStruct((N_CHIPS,) + a.shape[2:], a.dtype) for a in arrs]
    return pl.pallas_call(
        body, name=name, out_shape=out_shape, in_specs=[_HBM] * n, out_specs=[_HBM] * n,
        scratch_shapes=[pltpu.SemaphoreType.DMA((n, N_CHIPS)), pltpu.SemaphoreType.DMA((n, N_CHIPS))],
    )(*arrs)


def _chip_exchange(arrs, *, name):
    n = len(arrs)

    def body(*refs):
        ins, outs = refs[:n], refs[n:2 * n]
        send, recv = refs[2 * n:]
        x, y, c = lax.axis_index("x"), lax.axis_index("y"), lax.axis_index("c")
        me = 2 * x + y
        peers = _chip_peers(x, y)
        cps = []
        for a in range(n):
            for k, (px, py) in enumerate(peers):
                cp = pltpu.make_async_remote_copy(
                    src_ref=ins[a].at[2 * px + py], dst_ref=outs[a].at[me], send_sem=send.at[a, k],
                    recv_sem=recv.at[a, k], device_id=(px, py, c), device_id_type=MESH)
                cp.start()
                cps.append(cp)
        for a in range(n):
            for k, (px, py) in enumerate(peers):
                pltpu.make_async_remote_copy(
                    src_ref=ins[a].at[2 * px + py], dst_ref=outs[a].at[2 * px + py], send_sem=send.at[a, k],
                    recv_sem=recv.at[a, k], device_id=(px, py, c), device_id_type=MESH).wait_recv()
        for cp in cps:
            cp.wait_send()

    out_shape = [jax.ShapeDtypeStruct(a.shape, a.dtype) for a in arrs]
    return pl.pallas_call(
        body, name=name, out_shape=out_shape, in_specs=[_HBM] * n, out_specs=[_HBM] * n,
        scratch_shapes=[pltpu.SemaphoreType.DMA((n, 3)), pltpu.SemaphoreType.DMA((n, 3))],
    )(*arrs)


def _pair_share(arrs, *, name):
    n = len(arrs)

    def body(*refs):
        ins, outs = refs[:n], refs[n:2 * n]
        send, recv = refs[2 * n:]
        x, y, c = lax.axis_index("x"), lax.axis_index("y"), lax.axis_index("c")
        sib = (x, y, 1 - c)
        cps = []
        for a in range(n):
            cp = pltpu.make_async_remote_copy(
                src_ref=ins[a], dst_ref=outs[a], send_sem=send.at[a], recv_sem=recv.at[a],
                device_id=sib, device_id_type=MESH)
            cp.start()
            cps.append(cp)
        for cp in cps:
            cp.wait()

    out_shape = [jax.ShapeDtypeStruct(a.shape, a.dtype) for a in arrs]
    return pl.pallas_call(
        body, name=name, out_shape=out_shape, in_specs=[_HBM] * n, out_specs=[_HBM] * n,
        scratch_shapes=[pltpu.SemaphoreType.DMA((n,)), pltpu.SemaphoreType.DMA((n,))],
    )(*arrs)


def _allreduce_small(buf, *, name):
    r = buf.shape[0]

    def body(in_ref, out_ref, land, send, recv):
        x, y, c = lax.axis_index("x"), lax.axis_index("y"), lax.axis_index("c")
        me = 4 * x + 2 * y + c
        land[me] = in_ref[...]
        cps = []
        for k in range(1, N_DEV):
            px, py, pc = x ^ ((k >> 2) & 1), y ^ ((k >> 1) & 1), c ^ (k & 1)
            cp = pltpu.make_async_remote_copy(
                src_ref=in_ref, dst_ref=land.at[me], send_sem=send.at[k - 1], recv_sem=recv.at[k - 1],
                device_id=(px, py, pc), device_id_type=MESH)
            cp.start()
            cps.append(cp)
        for k in range(1, N_DEV):
            px, py, pc = x ^ ((k >> 2) & 1), y ^ ((k >> 1) & 1), c ^ (k & 1)
            pltpu.make_async_remote_copy(
                src_ref=in_ref, dst_ref=land.at[4 * px + 2 * py + pc], send_sem=send.at[k - 1],
                recv_sem=recv.at[k - 1], device_id=(px, py, pc), device_id_type=MESH).wait_recv()
        for cp in cps:
            cp.wait_send()
        acc = land[0]
        for k in range(1, N_DEV):
            acc = acc + land[k]
        out_ref[...] = acc

    vm = pl.BlockSpec(memory_space=pltpu.VMEM)
    return pl.pallas_call(
        body, name=name, out_shape=jax.ShapeDtypeStruct((r, LANES), F32), in_specs=[vm], out_specs=vm,
        scratch_shapes=[pltpu.VMEM((N_DEV, r, LANES), F32), pltpu.SemaphoreType.DMA((N_DEV - 1,)),
                        pltpu.SemaphoreType.DMA((N_DEV - 1,))],
    )(buf)


def _row_tile(rows, cols, n_arrays, budget=20 * 1024 * 1024):
    best = 8 if rows % 8 == 0 else rows
    tr = 8
    while tr <= rows:
        if rows % tr == 0 and tr * cols * 4 * n_arrays * 2 <= budget:
            best = tr
        tr *= 2
    return best


def _add_slot_layer(full, other, *, name):
    _, _, r, cdim = full.shape
    tr = _row_tile(r, cdim, 4)

    def body(c_ref, a_ref, b_ref, o_ref, ob_ref):
        sm = a_ref[...] + b_ref[...]
        o_ref[...] = sm
        ob_ref[...] = sm.astype(BF16)

    c = lax.axis_index("c").astype(jnp.int32).reshape(1)
    blk = pl.BlockSpec((None, tr, cdim), lambda k, i, c_ref: (k, i, 0))
    return pl.pallas_call(
        body, name=name,
        grid_spec=pltpu.PrefetchScalarGridSpec(
            num_scalar_prefetch=1, grid=(N_CHIPS, r // tr),
            in_specs=[pl.BlockSpec((None, None, tr, cdim), lambda k, i, c_ref: (k, c_ref[0], i, 0)), blk],
            out_specs=[blk, blk]),
        out_shape=[jax.ShapeDtypeStruct((N_CHIPS, r, cdim), F32), jax.ShapeDtypeStruct((N_CHIPS, r, cdim), BF16)],
        compiler_params=_cp(("parallel", "parallel")),
    )(c, full, other)


def _sum_slots(parts, pair, *, name):
    _, r, cdim = parts.shape
    tr = _row_tile(r, cdim, 5)

    def body(me_ref, p_ref, own_ref, o_ref):
        me = me_ref[0]
        acc = None
        for k in range(N_CHIPS):
            term = jnp.where(me == k, own_ref[...], p_ref[k].astype(F32))
            acc = term if acc is None else acc + term
        o_ref[...] = acc

    me = (2 * lax.axis_index("x") + lax.axis_index("y")).astype(jnp.int32).reshape(1)
    return pl.pallas_call(
        body, name=name,
        grid_spec=pltpu.PrefetchScalarGridSpec(
            num_scalar_prefetch=1, grid=(r // tr,),
            in_specs=[pl.BlockSpec((N_CHIPS, tr, cdim), lambda i, me_ref: (0, i, 0)),
                      pl.BlockSpec((None, tr, cdim), lambda i, me_ref: (me_ref[0], i, 0))],
            out_specs=pl.BlockSpec((tr, cdim), lambda i, me_ref: (i, 0))),
        out_shape=jax.ShapeDtypeStruct((r, cdim), F32),
        compiler_params=_cp(("parallel",)),
    )(me, parts, pair)


def _adamw(w, g, m, v, *, name):
    lead, (r, cdim) = w.shape[:-2], w.shape[-2:]
    nl = len(lead)
    tr = _row_tile(r, cdim, 7)
    tc = cdim
    if tr < 64 < r and cdim % LANES == 0:
        tr, tc = r, LANES
    c1 = 1.0 - ADAM_B1 ** ADAM_STEP
    c2 = 1.0 - ADAM_B2 ** ADAM_STEP

    def body(w_ref, g_ref, m_ref, v_ref, d_ref, nm_ref, nv_ref):
        gv = g_ref[...]
        mn = ADAM_B1 * m_ref[...] + (1.0 - ADAM_B1) * gv
        vn = ADAM_B2 * v_ref[...] + (1.0 - ADAM_B2) * (gv * gv)
        nm_ref[...] = mn
        nv_ref[...] = vn
        d_ref[...] = -ADAM_LR * ((mn / c1) / (jnp.sqrt(vn / c2) + ADAM_EPS) + ADAM_WD * w_ref[...])

    blk = pl.BlockSpec((None,) * nl + (tr, tc), lambda *ids: ids[:nl] + (ids[nl], ids[nl + 1]))
    sh = jax.ShapeDtypeStruct(w.shape, F32)
    return pl.pallas_call(
        body, name=name, grid=lead + (r // tr, cdim // tc), in_specs=[blk] * 4, out_specs=[blk] * 3,
        out_shape=[sh] * 3, compiler_params=_cp(("parallel",) * (nl + 2)),
    )(w, g, m, v)


_SMALL = ("norm_w", "conv_b", "dt_bias", "a_log", "d_skip", "ssm_norm_w", "sinks", "f_bias", "final_norm_w",
          "conv_w", "gate_bias")


def _pack(vals):
    flat = jnp.concatenate([v.reshape(-1) for v in vals])
    rows = -(-flat.shape[0] // LANES)
    rows = -(-rows // 8) * 8
    return jnp.pad(flat, (0, rows * LANES - flat.shape[0])).reshape(rows, LANES)


def _unpack(buf, shapes):
    flat = buf.reshape(-1)
    out, off = [], 0
    for sh in shapes:
        sz = int(np.prod(sh))
        out.append(flat[off:off + sz].reshape(sh))
        off += sz
    return out


def kernel(x, norm_w, w_in, conv_w, conv_b, dt_bias, a_log, d_skip, ssm_norm_w, sinks, f_bias, gate_bias, w_proj, w_out, final_norm_w, loss_target, m_norm_w, m_w_in, m_conv_w, m_conv_b, m_dt_bias, m_a_log, m_d_skip, m_ssm_norm_w, m_sinks, m_f_bias, m_gate_bias, m_w_proj, m_w_out, m_final_norm_w, v_norm_w, v_w_in, v_conv_w, v_conv_b, v_dt_bias, v_a_log, v_d_skip, v_ssm_norm_w, v_sinks, v_f_bias, v_gate_bias, v_w_proj, v_w_out, v_final_norm_w):
    depth = w_in.shape[0]
    chip = 2 * lax.axis_index("x") + lax.axis_index("y")

    own = [w_in.astype(BF16), w_proj.astype(BF16), w_out.astype(BF16), conv_w, gate_bias]
    gathered = _gather_weights(own, name="gather_weights")

    def whole(a, li, axis):
        slots = gathered[a][:, li]
        mine = (jnp.arange(N_CHIPS) == chip).reshape((N_CHIPS,) + (1,) * (slots.ndim - 1))
        parts = jnp.moveaxis(jnp.where(mine, own[a][li][None], slots), 0, axis)
        return parts.reshape(parts.shape[:axis] + (-1,) + parts.shape[axis + 2:])

    wls = []
    for li in range(depth):
        wls.append(dict(
            norm_w=norm_w[li], w_in=_pad_w_in(whole(0, li, 1)),
            conv_w=whole(3, li, 1), conv_b=conv_b[li], dt_bias=dt_bias[li], a_log=a_log[li], d_skip=d_skip[li],
            ssm_norm_w=ssm_norm_w[li], sinks=sinks[li], f_bias=f_bias[li], gate_bias=whole(4, li, 1),
            w_proj=whole(1, li, 1),
            w_out=whole(2, li, 0)))

    loss_part, grad_x, grads, d_final = _local_step(x, loss_target, wls, final_norm_w)
    loss = lax.psum(loss_part, ("x", "y", "c"))

    c_in = w_in.shape[2]
    r_proj = w_proj.shape[2]
    r_out = w_out.shape[1]
    full_in = jnp.stack([jnp.stack([grads[li]["w_in"][:, k * c_in:(k + 1) * c_in] for li in range(depth)])
                         for k in range(N_CHIPS)])
    full_proj = jnp.stack([jnp.stack([grads[li]["w_proj"][:, k * r_proj:(k + 1) * r_proj].reshape(-1, D_MODEL)
                                      for li in range(depth)]) for k in range(N_CHIPS)])
    full_out = jnp.stack([jnp.stack([grads[li]["w_out"][k * r_out:(k + 1) * r_out] for li in range(depth)])
                          for k in range(N_CHIPS)])
    fulls = [full_in, full_proj, full_out]
    others = _pair_exchange(fulls, name="grad_pair_exchange")
    pair = [_add_slot_layer(f, o, name=f"grad_pair_add{i}") for i, (f, o) in enumerate(zip(fulls, others))]
    parts = _chip_exchange([p[1] for p in pair], name="grad_chip_exchange")
    mine = [_sum_slots(p, pr[0], name=f"grad_slot_sum{i}") for i, (p, pr) in enumerate(zip(parts, pair))]
    theirs = _pair_share(mine, name="grad_pair_share")
    core = lax.axis_index("c")
    red_in, red_proj, red_out = [jnp.stack([jnp.where(core == li, m, t) for li in range(depth)])
                                 for m, t in zip(mine, theirs)]
    grad_w_in = red_in
    grad_w_proj = red_proj.reshape(w_proj.shape)
    grad_w_out = red_out

    small_full = {
        "norm_w": jnp.stack([g["norm_w"] for g in grads]), "conv_b": jnp.stack([g["conv_b"] for g in grads]),
        "dt_bias": jnp.stack([g["dt_bias"] for g in grads]), "a_log": jnp.stack([g["a_log"] for g in grads]),
        "d_skip": jnp.stack([g["d_skip"] for g in grads]),
        "ssm_norm_w": jnp.stack([g["ssm_norm_w"] for g in grads]),
        "sinks": jnp.stack([g["sinks"] for g in grads]), "f_bias": jnp.stack([g["f_bias"] for g in grads]),
        "final_norm_w": d_final,
        "conv_w": jnp.stack([g["conv_w"] for g in grads]), "gate_bias": jnp.stack([g["gate_bias"] for g in grads])}
    shapes = [small_full[k].shape for k in _SMALL]
    summed = _unpack(_allreduce_small(_pack([small_full[k] for k in _SMALL]), name="allreduce_small"), shapes)
    gsmall = dict(zip(_SMALL, summed))
    gsmall["conv_w"] = lax.dynamic_slice_in_dim(gsmall["conv_w"], chip * conv_w.shape[2], conv_w.shape[2], axis=2)
    gsmall["gate_bias"] = lax.dynamic_slice_in_dim(gsmall["gate_bias"], chip * gate_bias.shape[2],
                                                   gate_bias.shape[2], axis=2)

    w_small = dict(norm_w=norm_w, conv_b=conv_b, dt_bias=dt_bias, a_log=a_log, d_skip=d_skip,
                   ssm_norm_w=ssm_norm_w, sinks=sinks, f_bias=f_bias, final_norm_w=final_norm_w, conv_w=conv_w,
                   gate_bias=gate_bias)
    m_small = dict(norm_w=m_norm_w, conv_b=m_conv_b, dt_bias=m_dt_bias, a_log=m_a_log, d_skip=m_d_skip,
                   ssm_norm_w=m_ssm_norm_w, sinks=m_sinks, f_bias=m_f_bias, final_norm_w=m_final_norm_w,
                   conv_w=m_conv_w, gate_bias=m_gate_bias)
    v_small = dict(norm_w=v_norm_w, conv_b=v_conv_b, dt_bias=v_dt_bias, a_log=v_a_log, d_skip=v_d_skip,
                   ssm_norm_w=v_ssm_norm_w, sinks=v_sinks, f_bias=v_f_bias, final_norm_w=v_final_norm_w,
                   conv_w=v_conv_w, gate_bias=v_gate_bias)
    sshapes = [w_small[k].shape for k in _SMALL]
    ds, ms, vs = _adamw(_pack([w_small[k] for k in _SMALL]), _pack([gsmall[k] for k in _SMALL]),
                        _pack([m_small[k] for k in _SMALL]), _pack([v_small[k] for k in _SMALL]), name="adamw_small")
    delta = dict(zip(_SMALL, _unpack(ds, sshapes)))
    new_m = dict(zip(_SMALL, _unpack(ms, sshapes)))
    new_v = dict(zip(_SMALL, _unpack(vs, sshapes)))
    grad = dict(gsmall)
    for nm, w, g, m, v in (("w_proj", w_proj, grad_w_proj, m_w_proj, v_w_proj),
                           ("w_out", w_out, grad_w_out, m_w_out, v_w_out)):
        grad[nm] = g
        delta[nm], new_m[nm], new_v[nm] = _adamw(w, g, m, v, name=f"adamw_{nm}")
    tview = lambda a: jnp.transpose(a, (0, 2, 1))
    grad["w_in"] = grad_w_in
    delta["w_in"], new_m["w_in"], new_v["w_in"] = [
        tview(a) for a in _adamw(tview(w_in), tview(grad_w_in), tview(m_w_in), tview(v_w_in), name="adamw_w_in")]

    order = ("norm_w", "w_in", "conv_w", "conv_b", "dt_bias", "a_log", "d_skip", "ssm_norm_w", "sinks", "f_bias",
             "gate_bias", "w_proj", "w_out", "final_norm_w")
    return (loss, grad_x, *[grad[k] for k in order], *[delta[k] for k in order],
            *[new_m[k] for k in order], *[new_v[k] for k in order])
```

```python
import functools
import math

import numpy as np
import jax
import jax.numpy as jnp
from jax import lax
from jax.experimental import pallas as pl
from jax.experimental.pallas import tpu as pltpu

F32 = jnp.float32
BF16 = jnp.bfloat16
HIGHEST = lax.Precision.HIGHEST
MESH = pl.DeviceIdType.MESH

D_MODEL = 1024
HEAD_DIM = 64
N_HEADS = 16
N_GROUPS = 4
SSM_STATE = 128
CHUNK = 128
CONV_WIDTH = 4
CONV_DIM = 2048
ROPE_THETA = 10000.0
NORM_EPS = 1e-6
LANES = 128
N_CHIPS = 4
N_DEV = 8

ADAM_LR = 0.001
ADAM_B1 = 0.9
ADAM_B2 = 0.999
ADAM_EPS = 1e-08
ADAM_WD = 0.01
ADAM_STEP = 10

_REF_COLS = {}
_off = 0
for _n, _s in (("xbc", 2048), ("a_z", 1024), ("a_dt", 16), ("b_q", 1024), ("b_k", 256), ("b_v", 256),
               ("b_z", 1024), ("c_q", 1024), ("c_k", 1024), ("c_v", 1024), ("c_f", 16), ("c_z", 1024),
               ("gates", 3072)):
    _REF_COLS[_n] = (_off, _s)
    _off += _s
N_IN = _off

_PAD_ORDER = (("gates", 3072), ("xbc", 2048), ("a_z", 1024), ("b_q", 1024), ("b_z", 1024), ("c_q", 1024),
              ("c_k", 1024), ("c_v", 1024), ("c_z", 1024), ("b_k", 256), ("b_v", 256), ("a_dt", 512),
              ("c_f", 128))
_PAD_COLS = {}
_off = 0
for _n, _s in _PAD_ORDER:
    _PAD_COLS[_n] = (_off, _s)
    _off += _s
N_USED = _off
N_PAD = 13824


def _cp(sem, vmem_mb=48):
    return pltpu.CompilerParams(dimension_semantics=sem, vmem_limit_bytes=vmem_mb * 1024 * 1024)


def _dot(a, b, dims=((1,), (0,)), precision=None):
    return lax.dot_general(a, b, (dims, ((), ())), preferred_element_type=F32, precision=precision)


def _dot_nt(a, b):
    return _dot(a, b, ((1,), (1,)))


def _dot_tn(a, b):
    return _dot(a, b, ((0,), (0,)))


def _col(v, idx):
    lane = lax.broadcasted_iota(jnp.int32, v.shape, 1)
    return jnp.sum(jnp.where(lane == idx, v, 0.0), axis=1, keepdims=True)


def _row(v, idx):
    row = lax.broadcasted_iota(jnp.int32, v.shape, 0)
    return jnp.sum(jnp.where(row == idx, v, 0.0), axis=0, keepdims=True)


def _iota_col():
    return lax.broadcasted_iota(jnp.int32, (CHUNK, 1), 0)


def _iota_row():
    return lax.broadcasted_iota(jnp.int32, (1, LANES), 1)


def _sigmoid(x):
    return 1.0 / (1.0 + jnp.exp(-x))


def _softplus(x):
    return jnp.maximum(x, 0.0) + jnp.log(1.0 + jnp.exp(-jnp.abs(x)))


def _pad_w_in(w):
    parts = []
    for name, size in _PAD_ORDER:
        s0, sz = _REF_COLS[name]
        seg = w[:, s0:s0 + sz]
        if name == "a_dt":
            seg = jnp.pad(seg.reshape(-1, N_GROUPS, 4), ((0, 0), (0, 0), (0, LANES - 4))).reshape(-1, 512)
        elif name == "c_f":
            seg = jnp.pad(seg, ((0, 0), (0, LANES - 16)))
        parts.append(seg)
    parts.append(jnp.zeros((w.shape[0], N_PAD - N_USED), w.dtype))
    return jnp.concatenate(parts, axis=1)


def _unpad_w_in(wp):
    segs = {}
    for name, _ in _PAD_ORDER:
        p0, psz = _PAD_COLS[name]
        seg = wp[:, p0:p0 + psz]
        if name == "a_dt":
            seg = seg.reshape(-1, N_GROUPS, LANES)[:, :, :4].reshape(-1, 16)
        elif name == "c_f":
            seg = seg[:, :16]
        segs[name] = seg
    order = sorted(_REF_COLS, key=lambda n: _REF_COLS[n][0])
    return jnp.concatenate([segs[n] for n in order], axis=1)


def _group_lanes(v):
    return jnp.pad(v.reshape(N_GROUPS, 1, 4), ((0, 0), (0, 0), (0, LANES - 4)))


def _ungroup_lanes(v):
    return v[:, 0, :4].reshape(16)


def _mm(a, b, *, ta=False, tb=False, tm=512, tn=512, tk=512, out_dtype=F32, name):
    if ta:
        kdim, m = a.shape
    else:
        m, kdim = a.shape
    if tb:
        n, k2 = b.shape
    else:
        k2, n = b.shape
    assert kdim == k2, (a.shape, b.shape)
    tm, tn, tk = min(tm, m), min(tn, n), min(tk, kdim)
    assert m % tm == 0 and n % tn == 0 and kdim % tk == 0, (m, n, kdim, tm, tn, tk)
    nk = kdim // tk
    a_spec = (pl.BlockSpec((tk, tm), lambda i, j, k: (k, i)) if ta
              else pl.BlockSpec((tm, tk), lambda i, j, k: (i, k)))
    b_spec = (pl.BlockSpec((tn, tk), lambda i, j, k: (j, k)) if tb
              else pl.BlockSpec((tk, tn), lambda i, j, k: (k, j)))
    dims = ((0 if ta else 1,), (1 if tb else 0,))

    def body(a_ref, b_ref, o_ref, acc_ref):
        k = pl.program_id(2)
        p = _dot(a_ref[...].astype(BF16), b_ref[...].astype(BF16), dims)

        @pl.when(k == 0)
        def _():
            acc_ref[...] = p

        @pl.when(k > 0)
        def _():
            acc_ref[...] += p

        @pl.when(k == nk - 1)
        def _():
            o_ref[...] = acc_ref[...].astype(out_dtype)

    return pl.pallas_call(
        body, name=name, grid=(m // tm, n // tn, nk),
        in_specs=[a_spec, b_spec], out_specs=pl.BlockSpec((tm, tn), lambda i, j, k: (i, j)),
        out_shape=jax.ShapeDtypeStruct((m, n), out_dtype),
        scratch_shapes=[pltpu.VMEM((tm, tn), F32)],
        compiler_params=_cp(("parallel", "parallel", "arbitrary")),
    )(a, b)


def _rms_fwd(x, w, *, name, tm=512):
    t, d = x.shape

    def body(x_ref, w_ref, o_ref, ot_ref):
        xv = x_ref[...]
        r = lax.rsqrt(jnp.mean(xv * xv, axis=1, keepdims=True) + NORM_EPS)
        h = xv * r * w_ref[...]
        o_ref[...] = h.astype(BF16)
        ot_ref[...] = h.T.astype(BF16)

    return pl.pallas_call(
        body, name=name, grid=(t // tm,),
        in_specs=[pl.BlockSpec((tm, d), lambda i: (i, 0)), pl.BlockSpec((1, d), lambda i: (0, 0))],
        out_specs=[pl.BlockSpec((tm, d), lambda i: (i, 0)), pl.BlockSpec((d, tm), lambda i: (0, i))],
        out_shape=[jax.ShapeDtypeStruct((t, d), BF16), jax.ShapeDtypeStruct((d, t), BF16)],
        compiler_params=_cp(("parallel",)),
    )(x, w.reshape(1, d))


def _rms_bwd(x, w, dh, dres, *, name, tm=512):
    t, d = x.shape

    def body(x_ref, w_ref, dh_ref, dres_ref, dx_ref, dw_ref):
        xv = x_ref[...]
        r = lax.rsqrt(jnp.mean(xv * xv, axis=1, keepdims=True) + NORM_EPS)
        xhat = xv * r
        dhv = dh_ref[...]
        dxhat = dhv * w_ref[...]
        dx = r * (dxhat - xhat * jnp.mean(dxhat * xhat, axis=1, keepdims=True))
        dx_ref[...] = dres_ref[...] + dx

        @pl.when(pl.program_id(0) == 0)
        def _():
            dw_ref[...] = jnp.zeros_like(dw_ref)

        dw_ref[...] += jnp.sum(dhv * xhat, axis=0, keepdims=True)

    return pl.pallas_call(
        body, name=name, grid=(t // tm,),
        in_specs=[pl.BlockSpec((tm, d), lambda i: (i, 0)), pl.BlockSpec((1, d), lambda i: (0, 0)),
                  pl.BlockSpec((tm, d), lambda i: (i, 0)), pl.BlockSpec((tm, d), lambda i: (i, 0))],
        out_specs=[pl.BlockSpec((tm, d), lambda i: (i, 0)), pl.BlockSpec((1, d), lambda i: (0, 0))],
        out_shape=[jax.ShapeDtypeStruct((t, d), F32), jax.ShapeDtypeStruct((1, d), F32)],
        compiler_params=_cp(("arbitrary",)),
    )(x, w.reshape(1, d), dh, dres)


def _final_loss(x, w, target, *, name, tm=512):
    t, d = x.shape

    def body(x_ref, w_ref, t_ref, loss_ref, dx_ref, dw_ref):
        xv = x_ref[...]
        wv = w_ref[...]
        r = lax.rsqrt(jnp.mean(xv * xv, axis=1, keepdims=True) + NORM_EPS)
        xhat = xv * r
        err = xhat * wv - t_ref[...]
        dy = err * (1.0 / d)
        dxhat = dy * wv
        dx_ref[...] = r * (dxhat - xhat * jnp.mean(dxhat * xhat, axis=1, keepdims=True))

        @pl.when(pl.program_id(0) == 0)
        def _():
            dw_ref[...] = jnp.zeros_like(dw_ref)
            loss_ref[...] = jnp.zeros_like(loss_ref)

        dw_ref[...] += jnp.sum(dy * xhat, axis=0, keepdims=True)
        part = 0.5 * jnp.sum(jnp.mean(err * err, axis=1, keepdims=True), axis=0, keepdims=True)
        loss_ref[...] += jnp.broadcast_to(part, loss_ref.shape)

    return pl.pallas_call(
        body, name=name, grid=(t // tm,),
        in_specs=[pl.BlockSpec((tm, d), lambda i: (i, 0)), pl.BlockSpec((1, d), lambda i: (0, 0)),
                  pl.BlockSpec((tm, d), lambda i: (i, 0))],
        out_specs=[pl.BlockSpec((8, LANES), lambda i: (0, 0)), pl.BlockSpec((tm, d), lambda i: (i, 0)),
                   pl.BlockSpec((1, d), lambda i: (0, 0))],
        out_shape=[jax.ShapeDtypeStruct((8, LANES), F32), jax.ShapeDtypeStruct((t, d), F32),
                   jax.ShapeDtypeStruct((1, d), F32)],
        compiler_params=_cp(("arbitrary",)),
    )(x, w.reshape(1, d), target)


_CB = 128


def _conv_pre(u, w_ref, b_ref):
    s = u.shape[0]
    row = lax.broadcasted_iota(jnp.int32, u.shape, 0)
    pre = b_ref[...] + w_ref[CONV_WIDTH - 1:CONV_WIDTH, :] * u
    for sh in range(1, CONV_WIDTH):
        shifted = jnp.where(row >= sh, pltpu.roll(u, sh, 0), 0.0)
        pre = pre + w_ref[CONV_WIDTH - 1 - sh:CONV_WIDTH - sh, :] * shifted
    return pre


def _conv_fwd(proj3, cw, cb, *, name):
    b, s, _ = proj3.shape
    c0 = _PAD_COLS["xbc"][0] // _CB

    def body(u_ref, w_ref, b_ref, o_ref):
        pre = _conv_pre(u_ref[...].astype(F32), w_ref, b_ref)
        o_ref[...] = pre * _sigmoid(pre)

    return pl.pallas_call(
        body, name=name, grid=(b, CONV_DIM // _CB),
        in_specs=[pl.BlockSpec((None, s, _CB), lambda i, j: (i, 0, c0 + j)),
                  pl.BlockSpec((CONV_WIDTH, _CB), lambda i, j: (0, j)),
                  pl.BlockSpec((1, _CB), lambda i, j: (0, j))],
        out_specs=pl.BlockSpec((None, s, _CB), lambda i, j: (i, 0, j)),
        out_shape=jax.ShapeDtypeStruct((b, s, CONV_DIM), F32),
        compiler_params=_cp(("parallel", "parallel")),
    )(proj3, cw, cb.reshape(1, CONV_DIM))


def _conv_bwd(proj3, cw, cb, dact, *, name):
    b, s, _ = proj3.shape
    c0 = _PAD_COLS["xbc"][0] // _CB

    def body(u_ref, w_ref, b_ref, da_ref, du_ref, dwb_ref):
        u = u_ref[...].astype(F32)
        pre = _conv_pre(u, w_ref, b_ref)
        sg = _sigmoid(pre)
        dpre = da_ref[...] * (sg * (1.0 + pre * (1.0 - sg)))
        row = lax.broadcasted_iota(jnp.int32, u.shape, 0)
        du = w_ref[CONV_WIDTH - 1:CONV_WIDTH, :] * dpre
        rows = [jnp.sum(dpre * u, axis=0, keepdims=True)]
        for sh in range(1, CONV_WIDTH):
            fwd_shift = jnp.where(row < s - sh, pltpu.roll(dpre, s - sh, 0), 0.0)
            du = du + w_ref[CONV_WIDTH - 1 - sh:CONV_WIDTH - sh, :] * fwd_shift
            ush = jnp.where(row >= sh, pltpu.roll(u, sh, 0), 0.0)
            rows.append(jnp.sum(dpre * ush, axis=0, keepdims=True))
        du_ref[...] = du.astype(BF16)

        @pl.when(pl.program_id(1) == 0)
        def _():
            dwb_ref[...] = jnp.zeros_like(dwb_ref)

        for sh in range(CONV_WIDTH):
            k = CONV_WIDTH - 1 - sh
            dwb_ref[k:k + 1, :] += rows[sh]
        dwb_ref[CONV_WIDTH:CONV_WIDTH + 1, :] += jnp.sum(dpre, axis=0, keepdims=True)

    return pl.pallas_call(
        body, name=name, grid=(CONV_DIM // _CB, b),
        in_specs=[pl.BlockSpec((None, s, _CB), lambda j, i: (i, 0, c0 + j)),
                  pl.BlockSpec((CONV_WIDTH, _CB), lambda j, i: (0, j)),
                  pl.BlockSpec((1, _CB), lambda j, i: (0, j)),
                  pl.BlockSpec((None, s, _CB), lambda j, i: (i, 0, j))],
        out_specs=[pl.BlockSpec((None, s, _CB), lambda j, i: (i, 0, j)),
                   pl.BlockSpec((8, _CB), lambda j, i: (0, j))],
        out_shape=[jax.ShapeDtypeStruct((b, s, CONV_DIM), BF16), jax.ShapeDtypeStruct((8, CONV_DIM), F32)],
        compiler_params=_cp(("parallel", "arbitrary")),
    )(proj3, cw, cb.reshape(1, CONV_DIM), dact)


def _ssd_common(dt_ref, dtb_ref, alog_ref):
    row = lax.broadcasted_iota(jnp.int32, (CHUNK, CHUNK), 0)
    lane = lax.broadcasted_iota(jnp.int32, (CHUNK, CHUNK), 1)
    causal = row >= lane
    tri = causal.astype(F32)
    dtv = _softplus(dt_ref[...] + dtb_ref[...])
    a_row = -jnp.exp(alog_ref[...])
    acum = _dot(tri, dtv * a_row, precision=HIGHEST)
    return row, lane, causal, dtv, a_row, acum, acum.T


def _ssd_pair(pp, x, dtv, acum, acum_t, causal, lane, row):
    lo = lane < HEAD_DIM
    r0, r1 = 2 * pp, 2 * pp + 1
    dtp = jnp.where(lo, _col(dtv, r0), _col(dtv, r1))
    ac0, ac1 = _col(acum, r0), _col(acum, r1)
    ar0, ar1 = _row(acum_t, r0), _row(acum_t, r1)
    d0 = jnp.where(causal, jnp.exp(jnp.where(causal, ac0 - ar0, 0.0)), 0.0)
    d1 = jnp.where(causal, jnp.exp(jnp.where(causal, ac1 - ar1, 0.0)), 0.0)
    al0, al1 = _col(ar0, CHUNK - 1), _col(ar1, CHUNK - 1)
    eac = jnp.where(lo, jnp.exp(ac0), jnp.exp(ac1))
    dsp = jnp.where(lo, jnp.exp(al0 - ac0), jnp.exp(al1 - ac1))
    eal = jnp.where(_iota_col() < HEAD_DIM, jnp.exp(al0), jnp.exp(al1))
    return lo, dtp, x * dtp, d0, d1, al0, al1, eac, dsp, eal


def _ssd_fwd(proj3, gates3, xact3, dtb, alog, dsk, nw, *, name):
    b, s, _ = proj3.shape
    nc = s // CHUNK
    dt0 = 0
    z0 = _PAD_COLS["a_z"][0] // D_MODEL

    def body(xs_ref, bm_ref, cm_ref, dt_ref, z_ref, dtb_ref, alog_ref, dsk_ref, nw_ref,
             ya_ref, ypre_ref, hst_ref, h_scr):
        @pl.when(pl.program_id(1) == 0)
        def _():
            h_scr[...] = jnp.zeros_like(h_scr)

        for g in range(N_GROUPS):
            w256 = pl.ds(256 * g, 256)
            w128 = pl.ds(LANES * g, LANES)
            group(xs_ref.at[:, w256], bm_ref.at[:, w128], cm_ref.at[:, w128], dt_ref.at[:, w128],
                  z_ref.at[:, w256], dtb_ref.at[g], alog_ref.at[g], dsk_ref.at[g], nw_ref.at[g],
                  ya_ref.at[:, w256], ypre_ref.at[:, w256], hst_ref.at[g], h_scr.at[g])

    def group(xs_ref, bm_ref, cm_ref, dt_ref, z_ref, dtb_ref, alog_ref, dsk_ref, nw_ref,
              ya_ref, ypre_ref, hst_ref, h_scr):
        row, lane, causal, dtv, a_row, acum, acum_t = _ssd_common(dt_ref, dtb_ref, alog_ref)
        bb = bm_ref[...].astype(BF16)
        cb = cm_ref[...].astype(BF16)
        cbm = _dot_nt(cb, bb)
        hst_ref[...] = h_scr[...]
        dskv = dsk_ref[...]
        for pp in range(2):
            x = xs_ref[:, LANES * pp:LANES * (pp + 1)]
            lo, dtp, xd, d0, d1, al0, al1, eac, dsp, eal = _ssd_pair(pp, x, dtv, acum, acum_t, causal, lane, row)
            xdb = xd.astype(BF16)
            y = jnp.where(lo, _dot((cbm * d0).astype(BF16), xdb), _dot((cbm * d1).astype(BF16), xdb))
            h = h_scr[pp]
            y = y + eac * _dot_nt(cb, h.astype(BF16))
            h_scr[pp] = h * eal + _dot_tn((xd * dsp).astype(BF16), bb)
            dskp = jnp.where((_iota_row() < HEAD_DIM), _col(dskv, 2 * pp), _col(dskv, 2 * pp + 1))
            ypre_ref[:, LANES * pp:LANES * (pp + 1)] = y + x * dskp
        ypre = ypre_ref[...]
        z = z_ref[...].astype(F32)
        yg = ypre * (z * _sigmoid(z))
        rstd = lax.rsqrt(jnp.sum(yg * yg, axis=1, keepdims=True) * (1.0 / 256.0) + NORM_EPS)
        ya_ref[...] = (yg * rstd * nw_ref[...]).astype(BF16)

    g = N_GROUPS
    par = pl.BlockSpec((g, 1, LANES), lambda i, c: (0, 0, 0))
    wide = pl.BlockSpec((None, CHUNK, D_MODEL), lambda i, c: (i, c, 0))
    return pl.pallas_call(
        body, name=name, grid=(b, nc),
        in_specs=[wide,
                  pl.BlockSpec((None, CHUNK, 512), lambda i, c: (i, c, 2)),
                  pl.BlockSpec((None, CHUNK, 512), lambda i, c: (i, c, 3)),
                  pl.BlockSpec((None, CHUNK, 512), lambda i, c: (i, c, dt0)),
                  pl.BlockSpec((None, CHUNK, D_MODEL), lambda i, c: (i, c, z0)),
                  par, par, par,
                  pl.BlockSpec((g, 1, 256), lambda i, c: (0, 0, 0))],
        out_specs=[wide, wide,
                   pl.BlockSpec((None, None, g, 2, CHUNK, SSM_STATE), lambda i, c: (i, c, 0, 0, 0, 0))],
        out_shape=[jax.ShapeDtypeStruct((b, s, D_MODEL), BF16), jax.ShapeDtypeStruct((b, s, D_MODEL), F32),
                   jax.ShapeDtypeStruct((b, nc, g, 2, CHUNK, SSM_STATE), F32)],
        scratch_shapes=[pltpu.VMEM((g, 2, CHUNK, SSM_STATE), F32)],
        compiler_params=_cp(("parallel", "arbitrary")),
    )(xact3, xact3, xact3, gates3, proj3, dtb, alog, dsk, nw)


def _ssd_bwd(proj3, gates3, xact3, dtb, alog, dsk, nw, ypre3, hst, dya3, *, name):
    b, s, _ = proj3.shape
    nc = s // CHUNK
    dt0 = 0
    z0 = _PAD_COLS["a_z"][0] // D_MODEL

    def body(xs_ref, bm_ref, cm_ref, dt_ref, z_ref, dtb_ref, alog_ref, dsk_ref, nw_ref, ypre_ref, hst_ref,
             dya_ref, dact_ref, dz_ref, ddt_ref, ddtb_ref, dalog_ref, ddsk_ref, dnw_ref, dh_scr):
        first = jnp.logical_and(pl.program_id(0) == 0, pl.program_id(1) == 0)

        @pl.when(first)
        def _():
            ddtb_ref[...] = jnp.zeros_like(ddtb_ref)
            dalog_ref[...] = jnp.zeros_like(dalog_ref)
            ddsk_ref[...] = jnp.zeros_like(ddsk_ref)
            dnw_ref[...] = jnp.zeros_like(dnw_ref)

        @pl.when(pl.program_id(1) == 0)
        def _():
            dh_scr[...] = jnp.zeros_like(dh_scr)

        for g in range(N_GROUPS):
            w256 = pl.ds(256 * g, 256)
            w128 = pl.ds(LANES * g, LANES)
            group(xs_ref.at[:, w256], bm_ref.at[:, w128], cm_ref.at[:, w128], dt_ref.at[:, w128],
                  z_ref.at[:, w256], dtb_ref.at[g], alog_ref.at[g], dsk_ref.at[g], nw_ref.at[g],
                  ypre_ref.at[:, w256], hst_ref.at[g], dya_ref.at[:, w256],
                  dact_ref.at[:, w256], dact_ref.at[:, pl.ds(D_MODEL + LANES * g, LANES)],
                  dact_ref.at[:, pl.ds(D_MODEL + 512 + LANES * g, LANES)], dz_ref.at[:, w256], ddt_ref.at[:, w128],
                  ddtb_ref.at[g], dalog_ref.at[g], ddsk_ref.at[g], dnw_ref.at[g], dh_scr.at[g])

    def group(xs_ref, bm_ref, cm_ref, dt_ref, z_ref, dtb_ref, alog_ref, dsk_ref, nw_ref, ypre_ref, hst_ref,
              dya_ref, dxs_ref, dbm_ref, dcm_ref, dz_ref, ddt_ref, ddtb_ref, dalog_ref, ddsk_ref, dnw_ref,
              dh_scr):
        row, lane, causal, dtv, a_row, acum, acum_t = _ssd_common(dt_ref, dtb_ref, alog_ref)
        lane1 = _iota_row()
        bb = bm_ref[...].astype(BF16)
        cb = cm_ref[...].astype(BF16)
        cbm = _dot_nt(cb, bb)

        z = z_ref[...].astype(F32)
        ypre = ypre_ref[...]
        dya = dya_ref[...]
        sz = _sigmoid(z)
        silu = z * sz
        yg = ypre * silu
        rstd = lax.rsqrt(jnp.sum(yg * yg, axis=1, keepdims=True) * (1.0 / 256.0) + NORM_EPS)
        dnw_ref[...] += jnp.sum(dya * yg * rstd, axis=0, keepdims=True)
        dn = dya * nw_ref[...]
        dyg = rstd * dn - yg * (rstd * rstd * rstd * (1.0 / 256.0)) * jnp.sum(dn * yg, axis=1, keepdims=True)
        dz_ref[...] = (dyg * ypre * (sz * (1.0 + z * (1.0 - sz)))).astype(BF16)
        dy_all = dyg * silu

        dskv = dsk_ref[...]
        da_cols = jnp.zeros((CHUNK, LANES), F32)
        dxt_cols = jnp.zeros((CHUNK, LANES), F32)
        ddsk_row = jnp.zeros((1, LANES), F32)
        dcb = jnp.zeros((CHUNK, CHUNK), F32)
        dc = jnp.zeros((CHUNK, SSM_STATE), F32)
        db = jnp.zeros((CHUNK, SSM_STATE), F32)
        last = _iota_col() == CHUNK - 1
        for pp in range(2):
            r0, r1 = 2 * pp, 2 * pp + 1
            x = xs_ref[:, LANES * pp:LANES * (pp + 1)]
            dy = dy_all[:, LANES * pp:LANES * (pp + 1)]
            lo, dtp, xd, d0, d1, al0, al1, eac, dsp, eal = _ssd_pair(pp, x, dtv, acum, acum_t, causal, lane, row)
            w0, w1 = cbm * d0, cbm * d1
            w0b, w1b = w0.astype(BF16), w1.astype(BF16)
            xdb = xd.astype(BF16)
            dyb = dy.astype(BF16)
            h = hst_ref[pp]
            dhn = dh_scr[pp]
            hb = h.astype(BF16)
            dhb = dhn.astype(BF16)
            g0 = _dot_nt(jnp.where(lo, dy, 0.0).astype(BF16), xdb)
            g1 = _dot_nt(jnp.where(lo, 0.0, dy).astype(BF16), xdb)
            dcb = dcb + g0 * d0 + g1 * d1
            m0, m1 = g0 * w0, g1 * w1
            bdh = _dot_nt(bb, dhb)
            dxd = jnp.where(lo, _dot_tn(w0b, dyb), _dot_tn(w1b, dyb)) + dsp * bdh
            ch = _dot_nt(cb, hb)
            edy = eac * dy
            edyb = edy.astype(BF16)
            xds = xd * dsp
            dc = dc + _dot(edyb, hb)
            db = db + _dot(xds.astype(BF16), dhb)
            dh_scr[pp] = dhn * eal + _dot_tn(edyb, cb)
            t2 = edy * ch
            t3 = xds * bdh
            dhh = dhn * h
            s4_0 = jnp.sum(jnp.sum(jnp.where(row < HEAD_DIM, dhh, 0.0), axis=0, keepdims=True), axis=1, keepdims=True)
            s4_1 = jnp.sum(jnp.sum(dhh, axis=0, keepdims=True), axis=1, keepdims=True) - s4_0
            t23 = t2 - t3
            t23_0 = jnp.sum(jnp.where(lo, t23, 0.0), axis=1, keepdims=True)
            t23_1 = jnp.sum(t23, axis=1, keepdims=True) - t23_0
            c3 = jnp.sum(t3, axis=0, keepdims=True)
            c3_0 = jnp.sum(jnp.where(_iota_row() < HEAD_DIM, c3, 0.0), axis=1, keepdims=True)
            c3_1 = jnp.sum(c3, axis=1, keepdims=True) - c3_0
            dal0 = c3_0 + jnp.exp(al0) * s4_0
            dal1 = c3_1 + jnp.exp(al1) * s4_1
            dac0 = jnp.sum(m0 - m0.T, axis=1, keepdims=True) + t23_0 + jnp.where(last, dal0, 0.0)
            dac1 = jnp.sum(m1 - m1.T, axis=1, keepdims=True) + t23_1 + jnp.where(last, dal1, 0.0)
            da_cols = da_cols + jnp.where(lane == r0, dac0, 0.0) + jnp.where(lane == r1, dac1, 0.0)
            xx = dxd * x
            x0 = jnp.sum(jnp.where(lo, xx, 0.0), axis=1, keepdims=True)
            x1 = jnp.sum(xx, axis=1, keepdims=True) - x0
            dxt_cols = dxt_cols + jnp.where(lane == r0, x0, 0.0) + jnp.where(lane == r1, x1, 0.0)
            dskp = jnp.where((_iota_row() < HEAD_DIM), _col(dskv, r0), _col(dskv, r1))
            dxs_ref[:, LANES * pp:LANES * (pp + 1)] = dxd * dtp + dy * dskp
            yx = jnp.sum(dy * x, axis=0, keepdims=True)
            k0 = jnp.sum(jnp.where((_iota_row() < HEAD_DIM), yx, 0.0), axis=1, keepdims=True)
            k1 = jnp.sum(yx, axis=1, keepdims=True) - k0
            ddsk_row = ddsk_row + jnp.where(lane1 == r0, k0, 0.0) + jnp.where(lane1 == r1, k1, 0.0)
        dcbb = dcb.astype(BF16)
        dcm_ref[...] = dc + _dot(dcbb, bb)
        dbm_ref[...] = db + _dot_tn(dcbb, cb)
        tri_t = (row <= lane).astype(F32)
        dadt = _dot(tri_t, da_cols, precision=HIGHEST)
        ddtv = dadt * a_row + dxt_cols
        dalog_ref[...] += jnp.sum(dadt * dtv, axis=0, keepdims=True) * a_row
        ddt_raw = ddtv * _sigmoid(dt_ref[...] + dtb_ref[...])
        ddt_ref[...] = ddt_raw.astype(BF16)
        ddtb_ref[...] += jnp.sum(ddt_raw, axis=0, keepdims=True)
        ddsk_ref[...] += ddsk_row

    g = N_GROUPS
    rc = lambda c: nc - 1 - c
    par = pl.BlockSpec((g, 1, LANES), lambda i, c: (0, 0, 0))
    parw = pl.BlockSpec((g, 1, 256), lambda i, c: (0, 0, 0))
    wide = pl.BlockSpec((None, CHUNK, D_MODEL), lambda i, c: (i, rc(c), 0))
    blk512 = lambda col: pl.BlockSpec((None, CHUNK, 512), lambda i, c: (i, rc(c), col))
    return pl.pallas_call(
        body, name=name, grid=(b, nc),
        in_specs=[wide, blk512(2), blk512(3), blk512(dt0),
                  pl.BlockSpec((None, CHUNK, D_MODEL), lambda i, c: (i, rc(c), z0)),
                  par, par, par, parw,
                  wide,
                  pl.BlockSpec((None, None, g, 2, CHUNK, SSM_STATE), lambda i, c: (i, rc(c), 0, 0, 0, 0)),
                  wide],
        out_specs=[pl.BlockSpec((None, CHUNK, CONV_DIM), lambda i, c: (i, rc(c), 0)), wide, blk512(0),
                   par, par, par, parw],
        out_shape=[jax.ShapeDtypeStruct((b, s, CONV_DIM), F32), jax.ShapeDtypeStruct((b, s, D_MODEL), BF16),
                   jax.ShapeDtypeStruct((b, s, 512), BF16),
                   jax.ShapeDtypeStruct((g, 1, LANES), F32), jax.ShapeDtypeStruct((g, 1, LANES), F32),
                   jax.ShapeDtypeStruct((g, 1, LANES), F32), jax.ShapeDtypeStruct((g, 1, 256), F32)],
        scratch_shapes=[pltpu.VMEM((g, 2, CHUNK, SSM_STATE), F32)],
        compiler_params=_cp(("arbitrary", "arbitrary")),
    )(xact3, xact3, xact3, gates3, proj3, dtb, alog, dsk, nw, ypre3, hst, dya3)


_FGATE_ROWS = 512


def _fgate_fwd(gates3, fb, *, name):
    b, s, _ = gates3.shape
    rows = min(_FGATE_ROWS, s)
    f0 = _PAD_COLS["a_dt"][1] // LANES

    def body(f_ref, fb_ref, cum_ref, carry):
        @pl.when(pl.program_id(1) == 0)
        def _():
            carry[...] = jnp.zeros_like(carry)

        row = lax.broadcasted_iota(jnp.int32, (rows, rows), 0)
        lane = lax.broadcasted_iota(jnp.int32, (rows, rows), 1)
        tri = (row >= lane).astype(F32)
        lf = -_softplus(-(f_ref[...] + fb_ref[...]))
        cs = _dot(tri, lf, precision=HIGHEST) + carry[0:1, :]
        cum_ref[...] = cs
        carry[0:1, :] = _row(cs, rows - 1)

    return pl.pallas_call(
        body, name=name, grid=(b, s // rows),
        in_specs=[pl.BlockSpec((None, rows, LANES), lambda i, c: (i, c, f0)),
                  pl.BlockSpec((1, LANES), lambda i, c: (0, 0))],
        out_specs=pl.BlockSpec((None, rows, LANES), lambda i, c: (i, c, 0)),
        out_shape=jax.ShapeDtypeStruct((b, s, LANES), F32),
        scratch_shapes=[pltpu.VMEM((8, LANES), F32)],
        compiler_params=_cp(("parallel", "arbitrary")),
    )(gates3, fb)


def _fgate_bwd(gates3, fb, dcum, *, name):
    b, s, _ = gates3.shape
    rows = min(_FGATE_ROWS, s)
    nc = s // rows
    f0 = _PAD_COLS["a_dt"][1] // LANES
    npair = dcum.shape[1]

    def body(f_ref, fb_ref, dc_ref, df_ref, dfb_ref, carry):
        first = jnp.logical_and(pl.program_id(0) == 0, pl.program_id(1) == 0)

        @pl.when(first)
        def _():
            dfb_ref[...] = jnp.zeros_like(dfb_ref)

        @pl.when(pl.program_id(1) == 0)
        def _():
            carry[...] = jnp.zeros_like(carry)

        row = lax.broadcasted_iota(jnp.int32, (rows, rows), 0)
        lane = lax.broadcasted_iota(jnp.int32, (rows, rows), 1)
        tri_t = (row <= lane).astype(F32)
        dc = -jnp.sum(dc_ref[...], axis=0)
        dlf = _dot(tri_t, dc, precision=HIGHEST) + carry[0:1, :]
        carry[0:1, :] = _row(dlf, 0)
        df = dlf * _sigmoid(-(f_ref[...] + fb_ref[...]))
        df_ref[...] = df.astype(BF16)
        dfb_ref[...] += jnp.sum(df, axis=0, keepdims=True)

    return pl.pallas_call(
        body, name=name, grid=(b, nc),
        in_specs=[pl.BlockSpec((None, rows, LANES), lambda i, c: (i, nc - 1 - c, f0)),
                  pl.BlockSpec((1, LANES), lambda i, c: (0, 0)),
                  pl.BlockSpec((None, npair, rows, LANES), lambda i, c: (i, 0, nc - 1 - c, 0))],
        out_specs=[pl.BlockSpec((None, rows, LANES), lambda i, c: (i, nc - 1 - c, 0)),
                   pl.BlockSpec((1, LANES), lambda i, c: (0, 0))],
        out_shape=[jax.ShapeDtypeStruct((b, s, LANES), BF16), jax.ShapeDtypeStruct((1, LANES), F32)],
        scratch_shapes=[pltpu.VMEM((8, LANES), F32)],
        compiler_params=_cp(("arbitrary", "arbitrary")),
    )(gates3, fb, dcum)


_SCALE = HEAD_DIM ** -0.5
_NEG = -1e30


_ST_LSE, _ST_DELTA, _ST_MJ = 0, 2, 8


_SR = 40


def _ck_rep(cum):
    b, s, _ = cum.shape
    t = jnp.transpose(cum[:, :, :N_HEADS], (0, 2, 1)).reshape(b, N_HEADS // 2, 2, s, 1)
    return jnp.broadcast_to(t, (b, N_HEADS // 2, 2, s, LANES))


def _foxt_fwd(proj3, ckrep, *, name, tb):
    b, s, _ = proj3.shape
    nq = s // tb
    assert _ST_MJ + 2 * nq <= _SR
    q0 = _PAD_COLS["c_q"][0] // LANES
    k0 = _PAD_COLS["c_k"][0] // LANES
    v0 = _PAD_COLS["c_v"][0] // LANES
    z0 = _PAD_COLS["c_z"][0] // LANES
    rep = tb // LANES

    def body(q_ref, k_ref, v_ref, z_ref, ck_ref, y_ref, o_ref, st_ref):
        i = pl.program_id(2)
        lane = lax.broadcasted_iota(jnp.int32, (tb, LANES), 1)
        lo = lane < HEAD_DIM
        lo_r = lax.broadcasted_iota(jnp.int32, (LANES, tb), 0) < HEAD_DIM
        srow = lax.broadcasted_iota(jnp.int32, (_SR, tb), 0)
        q = q_ref[...].astype(F32) * _SCALE
        qms = (jnp.where(lo, q, 0.0).astype(BF16), jnp.where(lo, 0.0, q).astype(BF16))
        ones_at = (HEAD_DIM, 0)

        def block(j, carry, diagonal):
            ks = pl.ds(pl.multiple_of(j * tb, tb), tb)
            kb = k_ref[ks, :].astype(BF16)
            v = v_ref[ks, :].astype(F32)
            vts = (jnp.where(lo, v, jnp.where(lane == ones_at[0], 1.0, 0.0)).T.astype(BF16),
                   jnp.where(lo, jnp.where(lane == ones_at[1], 1.0, 0.0), v).T.astype(BF16))
            if diagonal:
                key = lax.broadcasted_iota(jnp.int32, (tb, tb), 0)
                qry = lax.broadcasted_iota(jnp.int32, (tb, tb), 1)
                mask = key <= qry
            ms, ls, acc, st = carry
            new_m, new_l, pvs, alphas = [], [], [], []
            for hh in range(2):
                sc = _dot_nt(kb, qms[hh]) - jnp.tile(ck_ref[hh, ks, :], (1, rep))
                if diagonal:
                    sc = jnp.where(mask, sc, _NEG)
                m_new = jnp.maximum(ms[hh], jnp.max(sc, axis=0, keepdims=True))
                alpha = jnp.exp(ms[hh] - m_new)
                pv = _dot(vts[hh], jnp.exp(sc - m_new).astype(BF16))
                rs = _row(pv[ones_at[hh]:ones_at[hh] + 8, :], 0)
                new_l.append(alpha * ls[hh] + rs)
                new_m.append(m_new)
                pvs.append(pv)
                alphas.append(alpha)
                st = jnp.where(srow == _ST_MJ + 2 * j + hh, m_new, st)
            acc = jnp.where(lo_r, alphas[0] * acc + pvs[0], alphas[1] * acc + pvs[1])
            return (tuple(new_m), tuple(new_l), acc, st)

        neg = jnp.full((1, tb), _NEG, F32)
        zero = jnp.zeros((1, tb), F32)
        init = ((neg, neg), (zero, zero), jnp.zeros((LANES, tb), F32), jnp.zeros((_SR, tb), F32))
        carry = lax.fori_loop(0, i, lambda j, c: block(j, c, False), init)
        ms, ls, acc, st = block(i, carry, True)
        o = (acc / jnp.where(lo_r, ls[0], ls[1])).T
        o_ref[...] = o
        st = jnp.where(srow == _ST_LSE, ms[0] + jnp.log(ls[0]), st)
        st_ref[...] = jnp.where(srow == _ST_LSE + 1, ms[1] + jnp.log(ls[1]), st)
        z = z_ref[...].astype(F32)
        y_ref[...] = (o * (z * _sigmoid(z))).astype(BF16)

    qspec = lambda c0: pl.BlockSpec((None, tb, LANES), lambda bi, p, i: (bi, i, c0 + p))
    kspec = lambda c0: pl.BlockSpec((None, s, LANES), lambda bi, p, i: (bi, 0, c0 + p))
    ospec = pl.BlockSpec((None, tb, LANES), lambda bi, p, i: (bi, i, p))
    return pl.pallas_call(
        body, name=name, grid=(b, N_HEADS // 2, nq),
        in_specs=[qspec(q0), kspec(k0), kspec(v0), qspec(z0),
                  pl.BlockSpec((None, None, 2, s, LANES), lambda bi, p, i: (bi, p, 0, 0, 0))],
        out_specs=[ospec, ospec, pl.BlockSpec((None, None, None, _SR, tb), lambda bi, p, i: (bi, p, i, 0, 0))],
        out_shape=[jax.ShapeDtypeStruct((b, s, D_MODEL), BF16), jax.ShapeDtypeStruct((b, s, D_MODEL), F32),
                   jax.ShapeDtypeStruct((b, N_HEADS // 2, nq, _SR, tb), F32)],
        compiler_params=_cp(("parallel", "parallel", "arbitrary")),
    )(proj3, proj3, proj3, proj3, ckrep)


def _foxt_prep(proj3, o3, stat, dy3, *, name, tb):
    b, s, _ = proj3.shape
    nq = s // tb
    z0 = _PAD_COLS["c_z"][0] // LANES

    def body(z_ref, o_ref, fst_ref, dy_ref, dz_ref, do_ref, st_ref):
        z = z_ref[...].astype(F32)
        sz = _sigmoid(z)
        dy = dy_ref[...]
        o = o_ref[...]
        do = dy * (z * sz)
        dz_ref[...] = (dy * o * (sz * (1.0 + z * (1.0 - sz)))).astype(BF16)
        do_ref[...] = do
        doo = do.astype(BF16).astype(F32) * o
        r8 = lax.broadcasted_iota(jnp.int32, (8, LANES), 0)
        l8 = lax.broadcasted_iota(jnp.int32, (8, LANES), 1)
        pick = jnp.logical_or(jnp.logical_and(r8 == 0, l8 < HEAD_DIM),
                              jnp.logical_and(r8 == 1, l8 >= HEAD_DIM)).astype(F32)
        d8 = _dot(pick, doo, ((1,), (1,)), precision=HIGHEST)
        srow = lax.broadcasted_iota(jnp.int32, (_SR, tb), 0)
        st = jnp.where(srow == _ST_DELTA, _row(d8, 0), fst_ref[...])
        st_ref[...] = jnp.where(srow == _ST_DELTA + 1, _row(d8, 1), st)

    ospec = pl.BlockSpec((None, tb, LANES), lambda bi, p, i: (bi, i, p))
    sspec = pl.BlockSpec((None, None, None, _SR, tb), lambda bi, p, i: (bi, p, i, 0, 0))
    return pl.pallas_call(
        body, name=name, grid=(b, N_HEADS // 2, nq),
        in_specs=[pl.BlockSpec((None, tb, LANES), lambda bi, p, i: (bi, i, z0 + p)), ospec, sspec, ospec],
        out_specs=[ospec, ospec, sspec],
        out_shape=[jax.ShapeDtypeStruct((b, s, D_MODEL), BF16), jax.ShapeDtypeStruct((b, s, D_MODEL), F32),
                   jax.ShapeDtypeStruct((b, N_HEADS // 2, nq, _SR, tb), F32)],
        compiler_params=_cp(("parallel", "parallel", "parallel")),
    )(proj3, o3, stat, dy3)


def _foxt_bwd(proj3, ckrep, do3, stats, *, name, tb):
    b, s, _ = proj3.shape
    nq = s // tb
    q0 = _PAD_COLS["c_q"][0] // LANES
    k0 = _PAD_COLS["c_k"][0] // LANES
    v0 = _PAD_COLS["c_v"][0] // LANES
    rep = tb // LANES

    def body(q_ref, do_ref, st_ref, k_ref, v_ref, ck_ref, dq_ref, dk_ref, dv_ref, cs_ref):
        j = pl.program_id(2)
        lane = lax.broadcasted_iota(jnp.int32, (tb, LANES), 1)
        lo = lane < HEAD_DIM
        lo_r = lax.broadcasted_iota(jnp.int32, (LANES, tb), 0) < HEAD_DIM

        @pl.when(j == 0)
        def _():
            dq_ref[...] = jnp.zeros_like(dq_ref)

        kf = k_ref[...].astype(F32)
        kb = kf.astype(BF16)
        kt = kf.T.astype(BF16)
        vb = v_ref[...].astype(BF16)
        cks = (jnp.tile(ck_ref[0], (1, rep)), jnp.tile(ck_ref[1], (1, rep)))

        def block(i, carry, diagonal):
            qs = pl.ds(pl.multiple_of(i * tb, tb), tb)
            q = q_ref[qs, :].astype(F32) * _SCALE
            do = do_ref[qs, :]
            st = st_ref[i]
            if diagonal:
                key = lax.broadcasted_iota(jnp.int32, (tb, tb), 0)
                qry = lax.broadcasted_iota(jnp.int32, (tb, tb), 1)
                mask = key <= qry
            dk, dv, cs = carry
            new_cs, dqs = [], []
            for hh in range(2):
                sel = lo if hh == 0 else jnp.logical_not(lo)
                qm = jnp.where(sel, q, 0.0).astype(BF16)
                dom = jnp.where(sel, do, 0.0).astype(BF16)
                sc = _dot_nt(kb, qm) - cks[hh]
                if diagonal:
                    sc = jnp.where(mask, sc, _NEG)
                mj = _row(st, _ST_MJ + 2 * j + hh)
                w = jnp.exp(mj - _row(st, _ST_LSE + hh))
                ph = jnp.exp(sc - mj).astype(BF16).astype(F32) * w
                ds = ph * (_dot_nt(vb, dom) - _row(st, _ST_DELTA + hh))
                dsb = ds.astype(BF16)
                dv = dv + _dot(ph.astype(BF16), dom)
                dk = dk + _dot(dsb, qm)
                new_cs.append(cs[hh] + jnp.sum(ds, axis=1, keepdims=True))
                dqs.append(_dot(kt, dsb))
            dq_ref[i] += jnp.where(lo_r, dqs[0], dqs[1]) * _SCALE
            return (dk, dv, tuple(new_cs))

        zcol = jnp.zeros((tb, 1), F32)
        init = (jnp.zeros((tb, LANES), F32), jnp.zeros((tb, LANES), F32), (zcol, zcol))
        carry = block(j, init, True)
        dk, dv, cs = lax.fori_loop(j + 1, nq, lambda i, c: block(i, c, False), carry)
        dk_ref[...] = dk.astype(BF16)
        dv_ref[...] = dv.astype(BF16)
        p2 = 2 * pl.program_id(1)
        cs_ref[...] = jnp.where(lane == p2, cs[0], jnp.where(lane == p2 + 1, cs[1], 0.0))

    full = lambda c0: pl.BlockSpec((None, s, LANES), lambda bi, p, j: (bi, 0, c0 + p))
    kspec = lambda c0: pl.BlockSpec((None, tb, LANES), lambda bi, p, j: (bi, j, c0 + p))
    ko = pl.BlockSpec((None, tb, LANES), lambda bi, p, j: (bi, j, p))
    sall = pl.BlockSpec((None, None, nq, _SR, tb), lambda bi, p, j: (bi, p, 0, 0, 0))
    dqspec = pl.BlockSpec((None, None, nq, LANES, tb), lambda bi, p, j: (bi, p, 0, 0, 0))
    return pl.pallas_call(
        body, name=name, grid=(b, N_HEADS // 2, nq),
        in_specs=[full(q0), full(0), sall, kspec(k0), kspec(v0),
                  pl.BlockSpec((None, None, 2, tb, LANES), lambda bi, p, j: (bi, p, 0, j, 0))],
        out_specs=[dqspec, ko, ko, pl.BlockSpec((None, None, tb, LANES), lambda bi, p, j: (bi, p, j, 0))],
        out_shape=[jax.ShapeDtypeStruct((b, N_HEADS // 2, nq, LANES, tb), F32),
                   jax.ShapeDtypeStruct((b, s, D_MODEL), BF16), jax.ShapeDtypeStruct((b, s, D_MODEL), BF16),
                   jax.ShapeDtypeStruct((b, N_HEADS // 2, s, LANES), F32)],
        compiler_params=_cp(("parallel", "parallel", "arbitrary")),
    )(proj3, do3, stats, proj3, proj3, ckrep)


def _rope(x, cos, sin_signed):
    w = x.shape[1]
    lane = lax.broadcasted_iota(jnp.int32, x.shape, 1)
    first = (lane % HEAD_DIM) < (HEAD_DIM // 2)
    rot = jnp.where(first, pltpu.roll(x, w - HEAD_DIM // 2, 1), pltpu.roll(x, HEAD_DIM // 2, 1))
    return x * cos + rot * sin_signed


_QB = 4
_QROWS = _QB * CHUNK


def _swa_keys(g, kc_ref, kp_ref, vc_ref, vp_ref, cq_ref, sq_ref, cp_ref, sp_ref):
    def both_halves(x):
        x = x.astype(F32)
        lane = lax.broadcasted_iota(jnp.int32, x.shape, 1)
        keep = (lane // HEAD_DIM) == (g % 2)
        return jnp.where(keep, x, pltpu.roll(x, HEAD_DIM, 1))

    cq, sq, cpv, spv = cq_ref[...], sq_ref[...], cp_ref[...], sp_ref[...]
    kc = _rope(both_halves(kc_ref[...]), cq, sq).astype(BF16)
    kp = _rope(both_halves(kp_ref[...]), cpv, spv).astype(BF16)
    return cq, sq, cpv, spv, kc, kp, both_halves(vc_ref[...]).astype(BF16), both_halves(vp_ref[...]).astype(BF16)


def _swa_stack(pairs, lo):
    return jnp.concatenate([jnp.where(lo, pairs[0], 0.0), jnp.where(lo, 0.0, pairs[0]),
                            jnp.where(lo, pairs[1], 0.0), jnp.where(lo, 0.0, pairs[1])], axis=0).astype(BF16)


def _swa_mask4(prev_valid):
    r = lax.broadcasted_iota(jnp.int32, (4 * CHUNK, 2 * CHUNK), 0) & (CHUNK - 1)
    c = lax.broadcasted_iota(jnp.int32, (4 * CHUNK, 2 * CHUNK), 1)
    own = jnp.logical_and(c >= CHUNK, c - CHUNK <= r)
    before = jnp.logical_and(c < CHUNK, c > r)
    if prev_valid is True:
        return jnp.logical_or(own, before)
    return jnp.logical_or(own, jnp.logical_and(before, prev_valid))


def _swa_sink4(skv):
    return jnp.concatenate([jnp.broadcast_to(_col(skv, j), (CHUNK, 1)) for j in range(4)], axis=0)


def _swa_specs(order):
    def spec(shape, fn):
        return pl.BlockSpec(shape, lambda *ids: fn(*order(*ids)))

    q0 = _PAD_COLS["b_q"][0] // 256
    z0 = _PAD_COLS["b_z"][0] // 256
    k0 = _PAD_COLS["b_k"][0] // LANES
    v0 = _PAD_COLS["b_v"][0] // LANES
    prev = lambda i: jnp.maximum(_QB * i - 1, 0)
    return dict(
        kc=spec((None, _QROWS, LANES), lambda bi, g, i: (bi, i, k0 + g // 2)),
        kp=spec((None, CHUNK, LANES), lambda bi, g, i: (bi, prev(i), k0 + g // 2)),
        vc=spec((None, _QROWS, LANES), lambda bi, g, i: (bi, i, v0 + g // 2)),
        vp=spec((None, CHUNK, LANES), lambda bi, g, i: (bi, prev(i), v0 + g // 2)),
        q=spec((None, _QROWS, 256), lambda bi, g, i: (bi, i, q0 + g)),
        z=spec((None, _QROWS, 256), lambda bi, g, i: (bi, i, z0 + g)),
        blk=spec((None, _QROWS, 256), lambda bi, g, i: (bi, i, g)),
        kcur=spec((None, _QROWS, LANES), lambda bi, g, i: (bi, i, g)),
        kstep=spec((None, CHUNK, LANES), lambda bi, g, i: (bi, i, g)),
        tcur=spec((_QROWS, LANES), lambda bi, g, i: (i, 0)),
        tprev=spec((CHUNK, LANES), lambda bi, g, i: (prev(i), 0)),
        sk=spec((None, 1, LANES), lambda bi, g, i: (g, 0, 0)))


def _swa_fwd(proj3, cos, sin, sinks, *, name):
    b, s, _ = proj3.shape

    def body(q_ref, z_ref, kc_ref, kp_ref, vc_ref, vp_ref, cq_ref, sq_ref, cp_ref, sp_ref, sk_ref,
             y_ref, o_ref, lse_ref):
        i = pl.program_id(2)
        cq_all, sq_all, _, _, kc_all, kp0, vc_all, vp0 = _swa_keys(
            pl.program_id(1), kc_ref, kp_ref, vc_ref, vp_ref, cq_ref, sq_ref, cp_ref, sp_ref)
        lo = lax.broadcasted_iota(jnp.int32, (CHUNK, LANES), 1) < HEAD_DIM
        sink4 = _swa_sink4(sk_ref[...])
        for u in range(_QB):
            rs = slice(CHUNK * u, CHUNK * (u + 1))
            ps = slice(CHUNK * (u - 1), CHUNK * u)
            cq, sq = cq_all[rs], sq_all[rs]
            kp, vp = (kp0, vp0) if u == 0 else (kc_all[ps], vc_all[ps])
            kk = jnp.concatenate([kp, kc_all[rs]], axis=0)
            vv = jnp.concatenate([vp, vc_all[rs]], axis=0)
            q4 = _swa_stack([_rope(q_ref[rs, LANES * pp:LANES * (pp + 1)].astype(F32), cq, sq) * _SCALE
                             for pp in range(2)], lo)
            sc = jnp.where(_swa_mask4(True if u > 0 else i > 0), _dot_nt(q4, kk), _NEG)
            m = jnp.maximum(jnp.max(sc, axis=1, keepdims=True), sink4)
            pr = jnp.exp(sc - m)
            l = jnp.sum(pr, axis=1, keepdims=True) + jnp.exp(sink4 - m)
            o4 = _dot(pr.astype(BF16), vv) / l
            lse4 = m + jnp.log(l)
            for pp in range(2):
                ls = slice(LANES * pp, LANES * (pp + 1))
                h0 = slice(2 * CHUNK * pp, 2 * CHUNK * pp + CHUNK)
                h1 = slice(2 * CHUNK * pp + CHUNK, 2 * CHUNK * (pp + 1))
                o = jnp.where(lo, o4[h0], o4[h1])
                z = z_ref[rs, ls].astype(F32)
                o_ref[rs, ls] = o
                lse_ref[rs, ls] = jnp.where(lo, lse4[h0], lse4[h1])
                y_ref[rs, ls] = (o * (z * _sigmoid(z))).astype(BF16)

    sp = _swa_specs(lambda bi, g, i: (bi, g, i))
    return pl.pallas_call(
        body, name=name, grid=(b, N_GROUPS, s // _QROWS),
        in_specs=[sp["q"], sp["z"], sp["kc"], sp["kp"], sp["vc"], sp["vp"],
                  sp["tcur"], sp["tcur"], sp["tprev"], sp["tprev"], sp["sk"]],
        out_specs=[sp["blk"], sp["blk"], sp["blk"]],
        out_shape=[jax.ShapeDtypeStruct((b, s, D_MODEL), BF16)] + [jax.ShapeDtypeStruct((b, s, D_MODEL), F32)] * 2,
        compiler_params=_cp(("parallel", "parallel", "parallel")),
    )(proj3, proj3, proj3, proj3, proj3, proj3, cos, sin, cos, sin, sinks)


def _swa_bwd(proj3, cos, sin, sinks, o3, lse3, dy3, *, name):
    b, s, _ = proj3.shape

    def body(q_ref, z_ref, kc_ref, kp_ref, vc_ref, vp_ref, cq_ref, sq_ref, cp_ref, sp_ref, sk_ref,
             o_ref, lse_ref, dy_ref, dq_ref, dz_ref, dkc_ref, dkp_ref, dvc_ref, dvp_ref, dsk_ref):
        i = pl.program_id(2)
        first = jnp.logical_and(pl.program_id(1) == 0, i == 0)

        @pl.when(first)
        def _():
            dsk_ref[...] = jnp.zeros_like(dsk_ref)

        cq_all, sq_all, cpv, spv, kc_all, kp0, vc_all, vp0 = _swa_keys(
            pl.program_id(0), kc_ref, kp_ref, vc_ref, vp_ref, cq_ref, sq_ref, cp_ref, sp_ref)
        lo = lax.broadcasted_iota(jnp.int32, (CHUNK, LANES), 1) < HEAD_DIM
        lane1 = lax.broadcasted_iota(jnp.int32, (1, LANES), 1)
        sink4 = _swa_sink4(sk_ref[...])
        zero = jnp.zeros((CHUNK, LANES), F32)
        dks = [zero] * (_QB + 1)
        dvs = [zero] * (_QB + 1)
        dsk_row = jnp.zeros((1, LANES), F32)
        for u in range(_QB):
            rs = slice(CHUNK * u, CHUNK * (u + 1))
            ps = slice(CHUNK * (u - 1), CHUNK * u)
            cq, sq = cq_all[rs], sq_all[rs]
            kp, vp = (kp0, vp0) if u == 0 else (kc_all[ps], vc_all[ps])
            kk = jnp.concatenate([kp, kc_all[rs]], axis=0)
            vv = jnp.concatenate([vp, vc_all[rs]], axis=0)
            q4 = _swa_stack([_rope(q_ref[rs, LANES * pp:LANES * (pp + 1)].astype(F32), cq, sq) * _SCALE
                             for pp in range(2)], lo)
            dos, lses = [], []
            for pp in range(2):
                ls = slice(LANES * pp, LANES * (pp + 1))
                z = z_ref[rs, ls].astype(F32)
                sz = _sigmoid(z)
                dy = dy_ref[rs, ls]
                dos.append(dy * (z * sz))
                dz_ref[rs, ls] = (dy * o_ref[rs, ls] * (sz * (1.0 + z * (1.0 - sz)))).astype(BF16)
                lse = lse_ref[rs, ls]
                lses += [_col(lse, 0), _col(lse, HEAD_DIM)]
            do4 = _swa_stack(dos, lo)
            lse4 = jnp.concatenate(lses, axis=0)
            pr = jnp.exp(jnp.where(_swa_mask4(True if u > 0 else i > 0), _dot_nt(q4, kk), _NEG) - lse4)
            dp = _dot_nt(do4, vv)
            dl = jnp.sum(pr * dp, axis=1, keepdims=True)
            ds = (pr * (dp - dl)).astype(BF16)
            dsink = -jnp.exp(sink4 - lse4) * dl
            for j in range(4):
                dsk_row = dsk_row + jnp.where(
                    lane1 == j, jnp.sum(dsink[CHUNK * j:CHUNK * (j + 1)], axis=0, keepdims=True), 0.0)
            dq4 = _dot(ds, kk)
            dkk = _dot_tn(ds, q4)
            dvv = _dot_tn(pr.astype(BF16), do4)
            dks[u], dks[u + 1] = dks[u] + dkk[:CHUNK], dks[u + 1] + dkk[CHUNK:]
            dvs[u], dvs[u + 1] = dvs[u] + dvv[:CHUNK], dvs[u + 1] + dvv[CHUNK:]
            for pp in range(2):
                h0 = slice(2 * CHUNK * pp, 2 * CHUNK * pp + CHUNK)
                h1 = slice(2 * CHUNK * pp + CHUNK, 2 * CHUNK * (pp + 1))
                dq_ref[rs, LANES * pp:LANES * (pp + 1)] = _rope(
                    jnp.where(lo, dq4[h0], dq4[h1]) * _SCALE, cq, -sq).astype(BF16)
        fold = lambda v: v + pltpu.roll(v, HEAD_DIM, 1)
        dkp_ref[...] = fold(_rope(dks[0], cpv, -spv))
        dvp_ref[...] = fold(dvs[0])
        for u in range(_QB):
            rs = slice(CHUNK * u, CHUNK * (u + 1))
            dkc_ref[rs, :] = fold(_rope(dks[u + 1], cq_all[rs], -sq_all[rs]))
            dvc_ref[rs, :] = fold(dvs[u + 1])
        dsk_ref[...] += dsk_row

    sp = _swa_specs(lambda g, bi, i: (bi, g, i))
    kv_shape = jax.ShapeDtypeStruct((b, s, 512), F32)
    kvp_shape = jax.ShapeDtypeStruct((b, s // _QB, 512), F32)
    return pl.pallas_call(
        body, name=name, grid=(N_GROUPS, b, s // _QROWS),
        in_specs=[sp["q"], sp["z"], sp["kc"], sp["kp"], sp["vc"], sp["vp"],
                  sp["tcur"], sp["tcur"], sp["tprev"], sp["tprev"], sp["sk"], sp["blk"], sp["blk"], sp["blk"]],
        out_specs=[sp["blk"], sp["blk"], sp["kcur"], sp["kstep"], sp["kcur"], sp["kstep"], sp["sk"]],
        out_shape=[jax.ShapeDtypeStruct((b, s, D_MODEL), BF16), jax.ShapeDtypeStruct((b, s, D_MODEL), BF16),
                   kv_shape, kvp_shape, kv_shape, kvp_shape, jax.ShapeDtypeStruct((N_GROUPS, 1, LANES), F32)],
        compiler_params=_cp(("arbitrary", "arbitrary", "arbitrary")),
    )(proj3, proj3, proj3, proj3, proj3, proj3, cos, sin, cos, sin, sinks, o3, lse3, dy3)


def _swa_fold(dkc, dkp, dvc, dvp, *, name):
    b, s, _ = dkc.shape
    ns = s // _QROWS

    def body(kc_ref, kp_ref, vc_ref, vp_ref, dk_ref, dv_ref):
        has_next = pl.program_id(1) < ns - 1
        lo = lax.broadcasted_iota(jnp.int32, (_QROWS, LANES), 1) < HEAD_DIM
        row = lax.broadcasted_iota(jnp.int32, (_QROWS, 512), 0)
        last_block = jnp.logical_and(row >= _QROWS - CHUNK, has_next)
        for cur, nxt, out in ((kc_ref, kp_ref, dk_ref), (vc_ref, vp_ref, dv_ref)):
            tot = cur[...] + jnp.where(last_block, jnp.tile(nxt[...], (_QB, 1)), 0.0)
            for j in range(2):
                out[:, LANES * j:LANES * (j + 1)] = jnp.where(
                    lo, tot[:, 256 * j:256 * j + LANES], tot[:, 256 * j + LANES:256 * (j + 1)]).astype(BF16)

    cur = pl.BlockSpec((None, _QROWS, 512), lambda bi, i: (bi, i, 0))
    nxt = pl.BlockSpec((None, CHUNK, 512), lambda bi, i: (bi, jnp.minimum(i + 1, ns - 1), 0))
    out = pl.BlockSpec((None, _QROWS, 256), lambda bi, i: (bi, i, 0))
    sh = jax.ShapeDtypeStruct((b, s, 256), BF16)
    return pl.pallas_call(
        body, name=name, grid=(b, ns), in_specs=[cur, nxt, cur, nxt], out_specs=[out, out], out_shape=[sh, sh],
        compiler_params=_cp(("parallel", "parallel")),
    )(dkc, dkp, dvc, dvp)


def _branch_fwd(ys, proj, gb, wp, wo, x, *, name, tm=256):
    t = proj.shape[0]
    g0 = _PAD_COLS["gates"][0] // D_MODEL

    def body(g_ref, a_ref, b_ref, c_ref, gb_ref, wp_ref, wo_ref, x_ref, ba_ref, bb_ref, bc_ref, m_ref, xn_ref):
        acc = None
        for i, (y, br) in enumerate(((a_ref, ba_ref), (b_ref, bb_ref), (c_ref, bc_ref))):
            bri = _dot(y[...], wp_ref[i])
            br[...] = bri
            gate = _sigmoid(g_ref[:, D_MODEL * i:D_MODEL * (i + 1)].astype(F32) + gb_ref[i:i + 1, :])
            acc = gate * bri if acc is None else acc + gate * bri
        mb = acc.astype(BF16)
        m_ref[...] = mb
        xn_ref[...] = x_ref[...] + _dot(mb, wo_ref[...])

    row = pl.BlockSpec((tm, D_MODEL), lambda i: (i, 0))
    rowf = jax.ShapeDtypeStruct((t, D_MODEL), F32)
    outs = pl.pallas_call(
        body, name=name, grid=(t // tm,),
        in_specs=[pl.BlockSpec((tm, 3 * D_MODEL), lambda i: (i, g0)), row, row, row,
                  pl.BlockSpec((3, D_MODEL), lambda i: (0, 0)),
                  pl.BlockSpec((3, D_MODEL, D_MODEL), lambda i: (0, 0, 0)),
                  pl.BlockSpec((D_MODEL, D_MODEL), lambda i: (0, 0)), row],
        out_specs=[row, row, row, row, row],
        out_shape=[rowf, rowf, rowf, jax.ShapeDtypeStruct((t, D_MODEL), BF16), rowf],
        compiler_params=_cp(("parallel",)),
    )(proj, ys[0], ys[1], ys[2], gb, wp, wo, x)
    return outs[:3], outs[3], outs[4]


def _branch_bwd(dx, proj, br, gb, wp, wo, *, name, tm=256):
    t = proj.shape[0]
    g0 = _PAD_COLS["gates"][0] // D_MODEL

    def body(g_ref, a_ref, b_ref, c_ref, gb_ref, wp_ref, wo_ref, dx_ref,
             da_ref, db_ref, dc_ref, dg_ref, dgb_ref, ya_ref, yb_ref, yc_ref):
        @pl.when(pl.program_id(0) == 0)
        def _():
            dgb_ref[...] = jnp.zeros_like(dgb_ref)

        dmv = _dot_nt(dx_ref[...].astype(BF16), wo_ref[...])
        for i, (r, dr, dy) in enumerate(((a_ref, da_ref, ya_ref), (b_ref, db_ref, yb_ref), (c_ref, dc_ref, yc_ref))):
            gate = _sigmoid(g_ref[:, D_MODEL * i:D_MODEL * (i + 1)].astype(F32) + gb_ref[i:i + 1, :])
            dbr = (dmv * gate).astype(BF16)
            dr[...] = dbr
            dg = dmv * r[...] * gate * (1.0 - gate)
            dg_ref[:, D_MODEL * i:D_MODEL * (i + 1)] = dg.astype(BF16)
            dgb_ref[i:i + 1, :] += jnp.sum(dg, axis=0, keepdims=True)
            dy[...] = _dot_nt(dbr, wp_ref[i])

    row = pl.BlockSpec((tm, D_MODEL), lambda i: (i, 0))
    rowb = jax.ShapeDtypeStruct((t, D_MODEL), BF16)
    rowf = jax.ShapeDtypeStruct((t, D_MODEL), F32)
    outs = pl.pallas_call(
        body, name=name, grid=(t // tm,),
        in_specs=[pl.BlockSpec((tm, 3 * D_MODEL), lambda i: (i, g0)), row, row, row,
                  pl.BlockSpec((3, D_MODEL), lambda i: (0, 0)),
                  pl.BlockSpec((3, D_MODEL, D_MODEL), lambda i: (0, 0, 0)),
                  pl.BlockSpec((D_MODEL, D_MODEL), lambda i: (0, 0)), row],
        out_specs=[row, row, row, pl.BlockSpec((tm, 3 * D_MODEL), lambda i: (i, 0)),
                   pl.BlockSpec((8, D_MODEL), lambda i: (0, 0)), row, row, row],
        out_shape=[rowb, rowb, rowb, jax.ShapeDtypeStruct((t, 3 * D_MODEL), BF16),
                   jax.ShapeDtypeStruct((8, D_MODEL), F32), rowf, rowf, rowf],
        compiler_params=_cp(("arbitrary",)),
    )(proj, br[0], br[1], br[2], gb, wp, wo, dx)
    return outs[:3], outs[3], outs[4], outs[5:]


def _rope_tables(s):
    pos = jnp.arange(s, dtype=F32)
    inv_freq = ROPE_THETA ** (-jnp.arange(0, HEAD_DIM, 2, dtype=F32) / HEAD_DIM)
    ang = pos[:, None] * inv_freq[None, :]
    cos, sin = jnp.cos(ang), jnp.sin(ang)
    return jnp.tile(cos, (1, 4)), jnp.tile(jnp.concatenate([-sin, sin], axis=1), (1, 2))


def _layer_params(wl):
    return dict(
        dtb=_group_lanes(wl["dt_bias"]), alog=_group_lanes(wl["a_log"]), dsk=_group_lanes(wl["d_skip"]),
        nw=wl["ssm_norm_w"].reshape(N_GROUPS, 1, 256), sinks=_group_lanes(wl["sinks"]),
        fb=jnp.pad(wl["f_bias"], (0, LANES - N_HEADS)).reshape(1, LANES))


def _layer_fwd(x, wl, tabs, bsz, li, tb):
    t = x.shape[0]
    s = t // bsz
    cos, sin = tabs
    lp = _layer_params(wl)
    n = lambda k: f"l{li}_{k}"
    h, h_t = _rms_fwd(x, wl["norm_w"], name=n("rms_fwd"))
    proj = _mm(h, wl["w_in"], tm=1024, tn=1536, tk=1024, out_dtype=BF16, name=n("mm_proj"))
    proj3 = proj.reshape(bsz, s, N_PAD)
    g0, gw = _PAD_COLS["a_dt"][0], _PAD_COLS["a_dt"][1] + _PAD_COLS["c_f"][1]
    gates3 = _mm(h, wl["w_in"][:, g0:g0 + gw], tm=1024, tn=gw, tk=1024, name=n("mm_gates")).reshape(bsz, s, gw)
    xact3 = _conv_fwd(proj3, wl["conv_w"], wl["conv_b"], name=n("conv_fwd"))
    ya3, ypre3, hst = _ssd_fwd(proj3, gates3, xact3, lp["dtb"], lp["alog"], lp["dsk"], lp["nw"], name=n("ssd_fwd"))
    yb3, ob3, lseb3 = _swa_fwd(proj3, cos, sin, lp["sinks"], name=n("swa_fwd"))
    cum = _fgate_fwd(gates3, lp["fb"], name=n("fgate_fwd"))
    cum_t = _ck_rep(cum)
    yc3, oc3, statc3 = _foxt_fwd(proj3, cum_t, name=n("fox_fwd"), tb=tb)
    ys = [v.reshape(t, D_MODEL) for v in (ya3, yb3, yc3)]
    br, merged, x_new = _branch_fwd(ys, proj, wl["gate_bias"], wl["w_proj"], wl["w_out"], x, name=n("branch_fwd"))
    saved = dict(x=x, h_t=h_t, proj=proj, gates3=gates3, xact3=xact3, ypre3=ypre3, hst=hst, ob3=ob3, lseb3=lseb3,
                 cum_t=cum_t, oc3=oc3, statc3=statc3, ys=ys, br=br, merged=merged, lp=lp)
    return x_new, saved


def _layer_bwd(dx, wl, sv, tabs, bsz, li, tb):
    t = dx.shape[0]
    s = t // bsz
    cos, sin = tabs
    lp = sv["lp"]
    n = lambda k: f"l{li}_{k}"
    proj = sv["proj"]
    proj3 = proj.reshape(bsz, s, N_PAD)
    g = {}
    g["w_out"] = _mm(sv["merged"], dx, ta=True, tm=1024, tn=1024, tk=512, name=n("mm_dwout"))
    dbr, dgates, dgb, dys = _branch_bwd(dx, proj, sv["br"], wl["gate_bias"], wl["w_proj"], wl["w_out"],
                                        name=n("branch_bwd"))
    g["gate_bias"] = dgb[:3]
    g["w_proj"] = jnp.stack([_mm(sv["ys"][i], dbr[i], ta=True, tm=1024, tn=1024, tk=512, name=n(f"mm_dwproj{i}"))
                             for i in range(3)])
    dy3 = [v.reshape(bsz, s, D_MODEL) for v in dys]

    (dact, daz, dadt, ddtb, dalog, ddsk, dnw) = _ssd_bwd(
        proj3, sv["gates3"], sv["xact3"], lp["dtb"], lp["alog"], lp["dsk"], lp["nw"], sv["ypre3"], sv["hst"], dy3[0],
        name=n("ssd_bwd"))
    g["dt_bias"], g["a_log"], g["d_skip"] = _ungroup_lanes(ddtb), _ungroup_lanes(dalog), _ungroup_lanes(ddsk)
    g["ssm_norm_w"] = dnw.reshape(D_MODEL)
    dxbc, dwb = _conv_bwd(proj3, wl["conv_w"], wl["conv_b"], dact, name=n("conv_bwd"))
    g["conv_w"], g["conv_b"] = dwb[:CONV_WIDTH], dwb[CONV_WIDTH]

    dbq, dbz, dkc, dkp, dvc, dvp, dsk = _swa_bwd(proj3, cos, sin, lp["sinks"], sv["ob3"],
                                                 sv["lseb3"], dy3[1], name=n("swa_bwd"))
    g["sinks"] = _ungroup_lanes(dsk)

    dbk, dbv = _swa_fold(dkc, dkp, dvc, dvp, name=n("swa_fold"))

    dcz, do3, stats = _foxt_prep(proj3, sv["oc3"], sv["statc3"], dy3[2], name=n("fox_prep"), tb=tb)
    dqt, dck, dcv, csum = _foxt_bwd(proj3, sv["cum_t"], do3, stats, name=n("fox_bwd"), tb=tb)
    dcq = jnp.transpose(dqt, (0, 2, 4, 1, 3)).reshape(bsz, s, D_MODEL)
    dcf, dfb = _fgate_bwd(sv["gates3"], lp["fb"], csum, name=n("fgate_bwd"))
    g["f_bias"] = dfb[0, :N_HEADS]

    parts = {"gates": dgates.reshape(bsz, s, 3 * D_MODEL), "xbc": dxbc, "a_z": daz, "b_q": dbq, "b_z": dbz,
             "c_q": dcq, "c_k": dck, "c_v": dcv, "c_z": dcz, "b_k": dbk, "b_v": dbv, "a_dt": dadt, "c_f": dcf}
    dproj = jnp.concatenate([parts[name].astype(BF16) for name, _ in _PAD_ORDER]
                            + [jnp.zeros((bsz, s, N_PAD - N_USED), BF16)], axis=2).reshape(t, N_PAD)
    dh = _mm(dproj, wl["w_in"], tb=True, tm=1024, tn=1024, tk=1536, name=n("mm_dh"))
    g["w_in"] = _unpad_w_in(_mm(sv["h_t"], dproj, tm=1024, tn=768, tk=2048, name=n("mm_dwin")))
    dx_in, dnorm = _rms_bwd(sv["x"], wl["norm_w"], dh, dx, name=n("rms_bwd"))
    g["norm_w"] = dnorm[0]
    return dx_in, g


def _local_step(x, target, wls, final_norm_w, tb=1024):
    bsz, s, d = x.shape
    t = bsz * s
    tabs = _rope_tables(s)
    xc = x.reshape(t, d)
    saved = []
    for li, wl in enumerate(wls):
        xc, sv = _layer_fwd(xc, wl, tabs, bsz, li, tb)
        saved.append(sv)
    loss, dx, dfw = _final_loss(xc, final_norm_w, target.reshape(t, d), name="final_loss")
    grads = [None] * len(wls)
    for li in reversed(range(len(wls))):
        dx, grads[li] = _layer_bwd(dx, wls[li], saved[li], tabs, bsz, li, tb)
    return loss[0, 0], dx.reshape(bsz, s, d), grads, dfw[0]


_HBM = pl.BlockSpec(memory_space=pltpu.HBM)


def _chip_peers(x, y):
    return [(1 - x, y), (x, 1 - y), (1 - x, 1 - y)]


def _gather_weights(arrs, *, name):
    n = len(arrs)

    def body(*refs):
        ins, outs = refs[:n], refs[n:2 * n]
        ici_send, ici_recv, d2d_send, d2d_recv = refs[2 * n:]
        x, y, c = lax.axis_index("x"), lax.axis_index("y"), lax.axis_index("c")
        me = 2 * x + y
        peers = _chip_peers(x, y)
        sib = (x, y, 1 - c)
        sends, fwds = [], []
        for a in range(n):
            for k, (px, py) in enumerate(peers):
                cp = pltpu.make_async_remote_copy(
                    src_ref=ins[a].at[c], dst_ref=outs[a].at[me, c], send_sem=ici_send.at[a, k],
                    recv_sem=ici_recv.at[a, k], device_id=(px, py, c), device_id_type=MESH)
                cp.start()
                sends.append(cp)
        for a in range(n):
            for k, (px, py) in enumerate(peers):
                slot = 2 * px + py
                pltpu.make_async_remote_copy(
                    src_ref=ins[a].at[c], dst_ref=outs[a].at[slot, c], send_sem=ici_send.at[a, k],
                    recv_sem=ici_recv.at[a, k], device_id=(px, py, c), device_id_type=MESH).wait_recv()
                fw = pltpu.make_async_remote_copy(
                    src_ref=outs[a].at[slot, c], dst_ref=outs[a].at[slot, c], send_sem=d2d_send.at[a, k],
                    recv_sem=d2d_recv.at[a, k], device_id=sib, device_id_type=MESH)
                fw.start()
                fwds.append(fw)
        for a in range(n):
            for k, (px, py) in enumerate(peers):
                slot = 2 * px + py
                pltpu.make_async_remote_copy(
                    src_ref=outs[a].at[slot, 1 - c], dst_ref=outs[a].at[slot, 1 - c], send_sem=d2d_send.at[a, k],
                    recv_sem=d2d_recv.at[a, k], device_id=sib, device_id_type=MESH).wait_recv()
        for cp in sends + fwds:
            cp.wait_send()

    out_shape = [jax.ShapeDtypeStruct((N_CHIPS,) + a.shape, a.dtype) for a in arrs]
    return pl.pallas_call(
        body, name=name, out_shape=out_shape, in_specs=[_HBM] * n, out_specs=[_HBM] * n,
        scratch_shapes=[pltpu.SemaphoreType.DMA((n, 3)), pltpu.SemaphoreType.DMA((n, 3)),
                        pltpu.SemaphoreType.DMA((n, 3)), pltpu.SemaphoreType.DMA((n, 3))],
    )(*arrs)


def _pair_exchange(arrs, *, name):
    n = len(arrs)

    def body(*refs):
        ins, outs = refs[:n], refs[n:2 * n]
        send, recv = refs[2 * n:]
        x, y, c = lax.axis_index("x"), lax.axis_index("y"), lax.axis_index("c")
        sib = (x, y, 1 - c)
        cps = []
        for a in range(n):
            for k in range(N_CHIPS):
                cp = pltpu.make_async_remote_copy(
                    src_ref=ins[a].at[k, 1 - c], dst_ref=outs[a].at[k], send_sem=send.at[a, k],
                    recv_sem=recv.at[a, k], device_id=sib, device_id_type=MESH)
                cp.start()
                cps.append(cp)
        for cp in cps:
            cp.wait()

    out_shape = [jax.ShapeDtypeStruct((N_CHIPS,) + a.shape[2:], a.dtype) for a in arrs]
    return pl.pallas_call(
        body, name=name, out_shape=out_shape, in_specs=[_HBM] * n, out_specs=[_HBM] * n,
        scratch_shapes=[pltpu.SemaphoreType.DMA((n, N_CHIPS)), pltpu.SemaphoreType.DMA((n, N_CHIPS))],
    )(*arrs)


def _chip_exchange(arrs, *, name):
    n = len(arrs)

    def body(*refs):
        ins, outs = refs[:n], refs[n:2 * n]
        send, recv = refs[2 * n:]
        x, y, c = lax.axis_index("x"), lax.axis_index("y"), lax.axis_index("c")
        me = 2 * x + y
        peers = _chip_peers(x, y)
        cps = []
        for a in range(n):
            for k, (px, py) in enumerate(peers):
                cp = pltpu.make_async_remote_copy(
                    src_ref=ins[a].at[2 * px + py], dst_ref=outs[a].at[me], send_sem=send.at[a, k],
                    recv_sem=recv.at[a, k], device_id=(px, py, c), device_id_type=MESH)
                cp.start()
                cps.append(cp)
        for a in range(n):
            for k, (px, py) in enumerate(peers):
                pltpu.make_async_remote_copy(
                    src_ref=ins[a].at[2 * px + py], dst_ref=outs[a].at[2 * px + py], send_sem=send.at[a, k],
                    recv_sem=recv.at[a, k], device_id=(px, py, c), device_id_type=MESH).wait_recv()
        for cp in cps:
            cp.wait_send()

    out_shape = [jax.ShapeDtypeStruct(a.shape, a.dtype) for a in arrs]
    return pl.pallas_call(
        body, name=name, out_shape=out_shape, in_specs=[_HBM] * n, out_specs=[_HBM] * n,
        scratch_shapes=[pltpu.SemaphoreType.DMA((n, 3)), pltpu.SemaphoreType.DMA((n, 3))],
    )(*arrs)


def _pair_share(arrs, *, name):
    n = len(arrs)

    def body(*refs):
        ins, outs = refs[:n], refs[n:2 * n]
        send, recv = refs[2 * n:]
        x, y, c = lax.axis_index("x"), lax.axis_index("y"), lax.axis_index("c")
        sib = (x, y, 1 - c)
        cps = []
        for a in range(n):
            cp = pltpu.make_async_remote_copy(
                src_ref=ins[a], dst_ref=outs[a], send_sem=send.at[a], recv_sem=recv.at[a],
                device_id=sib, device_id_type=MESH)
            cp.start()
            cps.append(cp)
        for cp in cps:
            cp.wait()

    out_shape = [jax.ShapeDtypeStruct(a.shape, a.dtype) for a in arrs]
    return pl.pallas_call(
        body, name=name, out_shape=out_shape, in_specs=[_HBM] * n, out_specs=[_HBM] * n,
        scratch_shapes=[pltpu.SemaphoreType.DMA((n,)), pltpu.SemaphoreType.DMA((n,))],
    )(*arrs)


def _allreduce_small(buf, *, name):
    r = buf.shape[0]

    def body(in_ref, out_ref, land, send, recv):
        x, y, c = lax.axis_index("x"), lax.axis_index("y"), lax.axis_index("c")
        me = 4 * x + 2 * y + c
        land[me] = in_ref[...]
        cps = []
        for k in range(1, N_DEV):
            px, py, pc = x ^ ((k >> 2) & 1), y ^ ((k >> 1) & 1), c ^ (k & 1)
            cp = pltpu.make_async_remote_copy(
                src_ref=in_ref, dst_ref=land.at[me], send_sem=send.at[k - 1], recv_sem=recv.at[k - 1],
                device_id=(px, py, pc), device_id_type=MESH)
            cp.start()
            cps.append(cp)
        for k in range(1, N_DEV):
            px, py, pc = x ^ ((k >> 2) & 1), y ^ ((k >> 1) & 1), c ^ (k & 1)
            pltpu.make_async_remote_copy(
                src_ref=in_ref, dst_ref=land.at[4 * px + 2 * py + pc], send_sem=send.at[k - 1],
                recv_sem=recv.at[k - 1], device_id=(px, py, pc), device_id_type=MESH).wait_recv()
        for cp in cps:
            cp.wait_send()
        acc = land[0]
        for k in range(1, N_DEV):
            acc = acc + land[k]
        out_ref[...] = acc

    vm = pl.BlockSpec(memory_space=pltpu.VMEM)
    return pl.pallas_call(
        body, name=name, out_shape=jax.ShapeDtypeStruct((r, LANES), F32), in_specs=[vm], out_specs=vm,
        scratch_shapes=[pltpu.VMEM((N_DEV, r, LANES), F32), pltpu.SemaphoreType.DMA((N_DEV - 1,)),
                        pltpu.SemaphoreType.DMA((N_DEV - 1,))],
    )(buf)


def _row_tile(rows, cols, n_arrays, budget=20 * 1024 * 1024):
    best = 8 if rows % 8 == 0 else rows
    tr = 8
    while tr <= rows:
        if rows % tr == 0 and tr * cols * 4 * n_arrays * 2 <= budget:
            best = tr
        tr *= 2
    return best


def _add_slot_layer(full, other, *, name):
    _, _, r, cdim = full.shape
    tr = _row_tile(r, cdim, 4)

    def body(c_ref, a_ref, b_ref, o_ref, ob_ref):
        sm = a_ref[...] + b_ref[...]
        o_ref[...] = sm
        ob_ref[...] = sm.astype(BF16)

    c = lax.axis_index("c").astype(jnp.int32).reshape(1)
    blk = pl.BlockSpec((None, tr, cdim), lambda k, i, c_ref: (k, i, 0))
    return pl.pallas_call(
        body, name=name,
        grid_spec=pltpu.PrefetchScalarGridSpec(
            num_scalar_prefetch=1, grid=(N_CHIPS, r // tr),
            in_specs=[pl.BlockSpec((None, None, tr, cdim), lambda k, i, c_ref: (k, c_ref[0], i, 0)), blk],
            out_specs=[blk, blk]),
        out_shape=[jax.ShapeDtypeStruct((N_CHIPS, r, cdim), F32), jax.ShapeDtypeStruct((N_CHIPS, r, cdim), BF16)],
        compiler_params=_cp(("parallel", "parallel")),
    )(c, full, other)


def _sum_slots(parts, pair, *, name):
    _, r, cdim = parts.shape
    tr = _row_tile(r, cdim, 5)

    def body(me_ref, p_ref, own_ref, o_ref):
        me = me_ref[0]
        acc = None
        for k in range(N_CHIPS):
            term = jnp.where(me == k, own_ref[...], p_ref[k].astype(F32))
            acc = term if acc is None else acc + term
        o_ref[...] = acc

    me = (2 * lax.axis_index("x") + lax.axis_index("y")).astype(jnp.int32).reshape(1)
    return pl.pallas_call(
        body, name=name,
        grid_spec=pltpu.PrefetchScalarGridSpec(
            num_scalar_prefetch=1, grid=(r // tr,),
            in_specs=[pl.BlockSpec((N_CHIPS, tr, cdim), lambda i, me_ref: (0, i, 0)),
                      pl.BlockSpec((None, tr, cdim), lambda i, me_ref: (me_ref[0], i, 0))],
            out_specs=pl.BlockSpec((tr, cdim), lambda i, me_ref: (i, 0))),
        out_shape=jax.ShapeDtypeStruct((r, cdim), F32),
        compiler_params=_cp(("parallel",)),
    )(me, parts, pair)


def _adamw(w, g, m, v, *, name):
    lead, (r, cdim) = w.shape[:-2], w.shape[-2:]
    nl = len(lead)
    tr = _row_tile(r, cdim, 7)
    tc = cdim
    if tr < 64 < r and cdim % LANES == 0:
        tr, tc = r, LANES
    c1 = 1.0 - ADAM_B1 ** ADAM_STEP
    c2 = 1.0 - ADAM_B2 ** ADAM_STEP

    def body(w_ref, g_ref, m_ref, v_ref, d_ref, nm_ref, nv_ref):
        gv = g_ref[...]
        mn = ADAM_B1 * m_ref[...] + (1.0 - ADAM_B1) * gv
        vn = ADAM_B2 * v_ref[...] + (1.0 - ADAM_B2) * (gv * gv)
        nm_ref[...] = mn
        nv_ref[...] = vn
        d_ref[...] = -ADAM_LR * ((mn / c1) / (jnp.sqrt(vn / c2) + ADAM_EPS) + ADAM_WD * w_ref[...])

    blk = pl.BlockSpec((None,) * nl + (tr, tc), lambda *ids: ids[:nl] + (ids[nl], ids[nl + 1]))
    sh = jax.ShapeDtypeStruct(w.shape, F32)
    return pl.pallas_call(
        body, name=name, grid=lead + (r // tr, cdim // tc), in_specs=[blk] * 4, out_specs=[blk] * 3,
        out_shape=[sh] * 3, compiler_params=_cp(("parallel",) * (nl + 2)),
    )(w, g, m, v)


_SMALL = ("norm_w", "conv_b", "dt_bias", "a_log", "d_skip", "ssm_norm_w", "sinks", "f_bias", "final_norm_w",
          "conv_w", "gate_bias")


def _pack(vals):
    flat = jnp.concatenate([v.reshape(-1) for v in vals])
    rows = -(-flat.shape[0] // LANES)
    rows = -(-rows // 8) * 8
    return jnp.pad(flat, (0, rows * LANES - flat.shape[0])).reshape(rows, LANES)


def _unpack(buf, shapes):
    flat = buf.reshape(-1)
    out, off = [], 0
    for sh in shapes:
        sz = int(np.prod(sh))
        out.append(flat[off:off + sz].reshape(sh))
        off += sz
    return out


def kernel(x, norm_w, w_in, conv_w, conv_b, dt_bias, a_log, d_skip, ssm_norm_w, sinks, f_bias, gate_bias, w_proj, w_out, final_norm_w, loss_target, m_norm_w, m_w_in, m_conv_w, m_conv_b, m_dt_bias, m_a_log, m_d_skip, m_ssm_norm_w, m_sinks, m_f_bias, m_gate_bias, m_w_proj, m_w_out, m_final_norm_w, v_norm_w, v_w_in, v_conv_w, v_conv_b, v_dt_bias, v_a_log, v_d_skip, v_ssm_norm_w, v_sinks, v_f_bias, v_gate_bias, v_w_proj, v_w_out, v_final_norm_w):
    depth = w_in.shape[0]
    chip = 2 * lax.axis_index("x") + lax.axis_index("y")

    own = [w_in.astype(BF16), w_proj.astype(BF16), w_out.astype(BF16), conv_w, gate_bias]
    gathered = _gather_weights(own, name="gather_weights")

    def whole(a, li, axis):
        return jnp.concatenate([jnp.where(chip == k, own[a][li], gathered[a][k, li]) for k in range(N_CHIPS)],
                               axis=axis)

    wls = []
    for li in range(depth):
        wls.append(dict(
            norm_w=norm_w[li], w_in=_pad_w_in(whole(0, li, 1)),
            conv_w=whole(3, li, 1), conv_b=conv_b[li], dt_bias=dt_bias[li], a_log=a_log[li], d_skip=d_skip[li],
            ssm_norm_w=ssm_norm_w[li], sinks=sinks[li], f_bias=f_bias[li], gate_bias=whole(4, li, 1),
            w_proj=whole(1, li, 1),
            w_out=whole(2, li, 0)))

    loss_part, grad_x, grads, d_final = _local_step(x, loss_target, wls, final_norm_w)
    loss = lax.psum(loss_part, ("x", "y", "c"))

    c_in = w_in.shape[2]
    r_proj = w_proj.shape[2]
    r_out = w_out.shape[1]
    full_in = jnp.stack([jnp.stack([grads[li]["w_in"][:, k * c_in:(k + 1) * c_in] for li in range(depth)])
                         for k in range(N_CHIPS)])
    full_proj = jnp.stack([jnp.stack([grads[li]["w_proj"][:, k * r_proj:(k + 1) * r_proj].reshape(-1, D_MODEL)
                                      for li in range(depth)]) for k in range(N_CHIPS)])
    full_out = jnp.stack([jnp.stack([grads[li]["w_out"][k * r_out:(k + 1) * r_out] for li in range(depth)])
                          for k in range(N_CHIPS)])
    fulls = [full_in, full_proj, full_out]
    others = _pair_exchange(fulls, name="grad_pair_exchange")
    pair = [_add_slot_layer(f, o, name=f"grad_pair_add{i}") for i, (f, o) in enumerate(zip(fulls, others))]
    parts = _chip_exchange([p[1] for p in pair], name="grad_chip_exchange")
    mine = [_sum_slots(p, pr[0], name=f"grad_slot_sum{i}") for i, (p, pr) in enumerate(zip(parts, pair))]
    theirs = _pair_share(mine, name="grad_pair_share")
    core = lax.axis_index("c")
    red_in, red_proj, red_out = [jnp.stack([jnp.where(core == li, m, t) for li in range(depth)])
                                 for m, t in zip(mine, theirs)]
    grad_w_in = red_in
    grad_w_proj = red_proj.reshape(w_proj.shape)
    grad_w_out = red_out

    small_full = {
        "norm_w": jnp.stack([g["norm_w"] for g in grads]), "conv_b": jnp.stack([g["conv_b"] for g in grads]),
        "dt_bias": jnp.stack([g["dt_bias"] for g in grads]), "a_log": jnp.stack([g["a_log"] for g in grads]),
        "d_skip": jnp.stack([g["d_skip"] for g in grads]),
        "ssm_norm_w": jnp.stack([g["ssm_norm_w"] for g in grads]),
        "sinks": jnp.stack([g["sinks"] for g in grads]), "f_bias": jnp.stack([g["f_bias"] for g in grads]),
        "final_norm_w": d_final,
        "conv_w": jnp.stack([g["conv_w"] for g in grads]), "gate_bias": jnp.stack([g["gate_bias"] for g in grads])}
    shapes = [small_full[k].shape for k in _SMALL]
    summed = _unpack(_allreduce_small(_pack([small_full[k] for k in _SMALL]), name="allreduce_small"), shapes)
    gsmall = dict(zip(_SMALL, summed))
    gsmall["conv_w"] = lax.dynamic_slice_in_dim(gsmall["conv_w"], chip * conv_w.shape[2], conv_w.shape[2], axis=2)
    gsmall["gate_bias"] = lax.dynamic_slice_in_dim(gsmall["gate_bias"], chip * gate_bias.shape[2],
                                                   gate_bias.shape[2], axis=2)

    w_small = dict(norm_w=norm_w, conv_b=conv_b, dt_bias=dt_bias, a_log=a_log, d_skip=d_skip,
                   ssm_norm_w=ssm_norm_w, sinks=sinks, f_bias=f_bias, final_norm_w=final_norm_w, conv_w=conv_w,
                   gate_bias=gate_bias)
    m_small = dict(norm_w=m_norm_w, conv_b=m_conv_b, dt_bias=m_dt_bias, a_log=m_a_log, d_skip=m_d_skip,
                   ssm_norm_w=m_ssm_norm_w, sinks=m_sinks, f_bias=m_f_bias, final_norm_w=m_final_norm_w,
                   conv_w=m_conv_w, gate_bias=m_gate_bias)
    v_small = dict(norm_w=v_norm_w, conv_b=v_conv_b, dt_bias=v_dt_bias, a_log=v_a_log, d_skip=v_d_skip,
                   ssm_norm_w=v_ssm_norm_w, sinks=v_sinks, f_bias=v_f_bias, final_norm_w=v_final_norm_w,
                   conv_w=v_conv_w, gate_bias=v_gate_bias)
    sshapes = [w_small[k].shape for k in _SMALL]
    ds, ms, vs = _adamw(_pack([w_small[k] for k in _SMALL]), _pack([gsmall[k] for k in _SMALL]),
                        _pack([m_small[k] for k in _SMALL]), _pack([v_small[k] for k in _SMALL]), name="adamw_small")
    delta = dict(zip(_SMALL, _unpack(ds, sshapes)))
    new_m = dict(zip(_SMALL, _unpack(ms, sshapes)))
    new_v = dict(zip(_SMALL, _unpack(vs, sshapes)))
    grad = dict(gsmall)
    for nm, w, g, m, v in (("w_proj", w_proj, grad_w_proj, m_w_proj, v_w_proj),
                           ("w_out", w_out, grad_w_out, m_w_out, v_w_out)):
        grad[nm] = g
        delta[nm], new_m[nm], new_v[nm] = _adamw(w, g, m, v, name=f"adamw_{nm}")
    tview = lambda a: jnp.transpose(a, (0, 2, 1))
    grad["w_in"] = grad_w_in
    delta["w_in"], new_m["w_in"], new_v["w_in"] = [
        tview(a) for a in _adamw(tview(w_in), tview(grad_w_in), tview(m_w_in), tview(v_w_in), name="adamw_w_in")]

    order = ("norm_w", "w_in", "conv_w", "conv_b", "dt_bias", "a_log", "d_skip", "ssm_norm_w", "sinks", "f_bias",
             "gate_bias", "w_proj", "w_out", "final_norm_w")
    return (loss, grad_x, *[grad[k] for k in order], *[delta[k] for k in order],
            *[new_m[k] for k in order], *[new_v[k] for k in order])
```

```python
import numpy as np
import jax
import jax.numpy as jnp
from jax import lax
from jax.experimental import pallas as pl
from jax.experimental.pallas import tpu as pltpu

F32 = jnp.float32
BF16 = jnp.bfloat16
HIGHEST = lax.Precision.HIGHEST
MESH = pl.DeviceIdType.MESH

D_MODEL = 1024
HEAD_DIM = 64
N_HEADS = 16
N_GROUPS = 4
SSM_STATE = 128
CHUNK = 128
CONV_WIDTH = 4
CONV_DIM = 2048
ROPE_THETA = 10000.0
NORM_EPS = 1e-6
LANES = 128
N_CHIPS = 4
N_DEV = 8

ADAM_LR = 0.001
ADAM_B1 = 0.9
ADAM_B2 = 0.999
ADAM_EPS = 1e-08
ADAM_WD = 0.01
ADAM_STEP = 10

_REF_COLS = {}
_off = 0
for _n, _s in (("xbc", 2048), ("a_z", 1024), ("a_dt", 16), ("b_q", 1024), ("b_k", 256), ("b_v", 256),
               ("b_z", 1024), ("c_q", 1024), ("c_k", 1024), ("c_v", 1024), ("c_f", 16), ("c_z", 1024),
               ("gates", 3072)):
    _REF_COLS[_n] = (_off, _s)
    _off += _s

_PAD_ORDER = (("gates", 3072), ("xbc", 2048), ("a_z", 1024), ("b_q", 1024), ("b_z", 1024), ("c_q", 1024),
              ("c_k", 1024), ("c_v", 1024), ("c_z", 1024), ("b_k", 256), ("b_v", 256), ("a_dt", 512),
              ("c_f", 128))
_PAD_COLS = {}
_off = 0
for _n, _s in _PAD_ORDER:
    _PAD_COLS[_n] = (_off, _s)
    _off += _s
N_USED = _off
N_PAD = 13824


def _cp(sem, vmem_mb=48):
    return pltpu.CompilerParams(dimension_semantics=sem, vmem_limit_bytes=vmem_mb * 1024 * 1024)


def _dot(a, b, dims=((1,), (0,)), precision=None):
    return lax.dot_general(a, b, (dims, ((), ())), preferred_element_type=F32, precision=precision)


def _dot_nt(a, b):
    return _dot(a, b, ((1,), (1,)))


def _dot_tn(a, b):
    return _dot(a, b, ((0,), (0,)))


def _col(v, idx):
    lane = lax.broadcasted_iota(jnp.int32, v.shape, 1)
    return jnp.sum(jnp.where(lane == idx, v, 0.0), axis=1, keepdims=True)


def _row(v, idx):
    row = lax.broadcasted_iota(jnp.int32, v.shape, 0)
    return jnp.sum(jnp.where(row == idx, v, 0.0), axis=0, keepdims=True)


def _iota_col():
    return lax.broadcasted_iota(jnp.int32, (CHUNK, 1), 0)


def _iota_row():
    return lax.broadcasted_iota(jnp.int32, (1, LANES), 1)


def _sigmoid(x):
    return 1.0 / (1.0 + jnp.exp(-x))


def _softplus(x):
    return jnp.maximum(x, 0.0) + jnp.log(1.0 + jnp.exp(-jnp.abs(x)))


def _pad_w_in(w):
    parts = []
    for name, size in _PAD_ORDER:
        s0, sz = _REF_COLS[name]
        seg = w[:, s0:s0 + sz]
        if name == "a_dt":
            seg = jnp.pad(seg.reshape(-1, N_GROUPS, 4), ((0, 0), (0, 0), (0, LANES - 4))).reshape(-1, 512)
        elif name == "c_f":
            seg = jnp.pad(seg, ((0, 0), (0, LANES - 16)))
        parts.append(seg)
    parts.append(jnp.zeros((w.shape[0], N_PAD - N_USED), w.dtype))
    return jnp.concatenate(parts, axis=1)


def _unpad_w_in(wp):
    segs = {}
    for name, _ in _PAD_ORDER:
        p0, psz = _PAD_COLS[name]
        seg = wp[:, p0:p0 + psz]
        if name == "a_dt":
            seg = seg.reshape(-1, N_GROUPS, LANES)[:, :, :4].reshape(-1, 16)
        elif name == "c_f":
            seg = seg[:, :16]
        segs[name] = seg
    order = sorted(_REF_COLS, key=lambda n: _REF_COLS[n][0])
    return jnp.concatenate([segs[n] for n in order], axis=1)


def _group_lanes(v):
    return jnp.pad(v.reshape(N_GROUPS, 1, 4), ((0, 0), (0, 0), (0, LANES - 4)))


def _ungroup_lanes(v):
    return v[:, 0, :4].reshape(16)


def _mm(a, b, *, ta=False, tb=False, tm=512, tn=512, tk=512, out_dtype=F32, name):
    if ta:
        kdim, m = a.shape
    else:
        m, kdim = a.shape
    if tb:
        n, k2 = b.shape
    else:
        k2, n = b.shape
    assert kdim == k2, (a.shape, b.shape)
    tm, tn, tk = min(tm, m), min(tn, n), min(tk, kdim)
    assert m % tm == 0 and n % tn == 0 and kdim % tk == 0, (m, n, kdim, tm, tn, tk)
    nk = kdim // tk
    a_spec = (pl.BlockSpec((tk, tm), lambda i, j, k: (k, i)) if ta
              else pl.BlockSpec((tm, tk), lambda i, j, k: (i, k)))
    b_spec = (pl.BlockSpec((tn, tk), lambda i, j, k: (j, k)) if tb
              else pl.BlockSpec((tk, tn), lambda i, j, k: (k, j)))
    dims = ((0 if ta else 1,), (1 if tb else 0,))

    def body(a_ref, b_ref, o_ref, acc_ref):
        k = pl.program_id(2)
        p = _dot(a_ref[...].astype(BF16), b_ref[...].astype(BF16), dims)

        @pl.when(k == 0)
        def _():
            acc_ref[...] = p

        @pl.when(k > 0)
        def _():
            acc_ref[...] += p

        @pl.when(k == nk - 1)
        def _():
            o_ref[...] = acc_ref[...].astype(out_dtype)

    return pl.pallas_call(
        body, name=name, grid=(m // tm, n // tn, nk),
        in_specs=[a_spec, b_spec], out_specs=pl.BlockSpec((tm, tn), lambda i, j, k: (i, j)),
        out_shape=jax.ShapeDtypeStruct((m, n), out_dtype),
        scratch_shapes=[pltpu.VMEM((tm, tn), F32)],
        compiler_params=_cp(("parallel", "parallel", "arbitrary")),
    )(a, b)


def _rms_fwd(x, w, *, name, tm=512):
    t, d = x.shape

    def body(x_ref, w_ref, o_ref, ot_ref):
        xv = x_ref[...]
        r = lax.rsqrt(jnp.mean(xv * xv, axis=1, keepdims=True) + NORM_EPS)
        h = xv * r * w_ref[...]
        o_ref[...] = h.astype(BF16)
        ot_ref[...] = h.T.astype(BF16)

    return pl.pallas_call(
        body, name=name, grid=(t // tm,),
        in_specs=[pl.BlockSpec((tm, d), lambda i: (i, 0)), pl.BlockSpec((1, d), lambda i: (0, 0))],
        out_specs=[pl.BlockSpec((tm, d), lambda i: (i, 0)), pl.BlockSpec((d, tm), lambda i: (0, i))],
        out_shape=[jax.ShapeDtypeStruct((t, d), BF16), jax.ShapeDtypeStruct((d, t), BF16)],
        compiler_params=_cp(("parallel",)),
    )(x, w.reshape(1, d))


def _rms_bwd(x, w, dh, dres, *, name, tm=512):
    t, d = x.shape

    def body(x_ref, w_ref, dh_ref, dres_ref, dx_ref, dw_ref):
        xv = x_ref[...]
        r = lax.rsqrt(jnp.mean(xv * xv, axis=1, keepdims=True) + NORM_EPS)
        xhat = xv * r
        dhv = dh_ref[...]
        dxhat = dhv * w_ref[...]
        dx = r * (dxhat - xhat * jnp.mean(dxhat * xhat, axis=1, keepdims=True))
        dx_ref[...] = dres_ref[...] + dx

        @pl.when(pl.program_id(0) == 0)
        def _():
            dw_ref[...] = jnp.zeros_like(dw_ref)

        dw_ref[...] += jnp.sum(dhv * xhat, axis=0, keepdims=True)

    return pl.pallas_call(
        body, name=name, grid=(t // tm,),
        in_specs=[pl.BlockSpec((tm, d), lambda i: (i, 0)), pl.BlockSpec((1, d), lambda i: (0, 0)),
                  pl.BlockSpec((tm, d), lambda i: (i, 0)), pl.BlockSpec((tm, d), lambda i: (i, 0))],
        out_specs=[pl.BlockSpec((tm, d), lambda i: (i, 0)), pl.BlockSpec((1, d), lambda i: (0, 0))],
        out_shape=[jax.ShapeDtypeStruct((t, d), F32), jax.ShapeDtypeStruct((1, d), F32)],
        compiler_params=_cp(("arbitrary",)),
    )(x, w.reshape(1, d), dh, dres)


def _final_loss(x, w, target, *, name, tm=512):
    t, d = x.shape

    def body(x_ref, w_ref, t_ref, loss_ref, dx_ref, dw_ref):
        xv = x_ref[...]
        wv = w_ref[...]
        r = lax.rsqrt(jnp.mean(xv * xv, axis=1, keepdims=True) + NORM_EPS)
        xhat = xv * r
        err = xhat * wv - t_ref[...]
        dy = err * (1.0 / d)
        dxhat = dy * wv
        dx_ref[...] = r * (dxhat - xhat * jnp.mean(dxhat * xhat, axis=1, keepdims=True))

        @pl.when(pl.program_id(0) == 0)
        def _():
            dw_ref[...] = jnp.zeros_like(dw_ref)
            loss_ref[...] = jnp.zeros_like(loss_ref)

        dw_ref[...] += jnp.sum(dy * xhat, axis=0, keepdims=True)
        part = 0.5 * jnp.sum(jnp.mean(err * err, axis=1, keepdims=True), axis=0, keepdims=True)
        loss_ref[...] += jnp.broadcast_to(part, loss_ref.shape)

    return pl.pallas_call(
        body, name=name, grid=(t // tm,),
        in_specs=[pl.BlockSpec((tm, d), lambda i: (i, 0)), pl.BlockSpec((1, d), lambda i: (0, 0)),
                  pl.BlockSpec((tm, d), lambda i: (i, 0))],
        out_specs=[pl.BlockSpec((8, LANES), lambda i: (0, 0)), pl.BlockSpec((tm, d), lambda i: (i, 0)),
                   pl.BlockSpec((1, d), lambda i: (0, 0))],
        out_shape=[jax.ShapeDtypeStruct((8, LANES), F32), jax.ShapeDtypeStruct((t, d), F32),
                   jax.ShapeDtypeStruct((1, d), F32)],
        compiler_params=_cp(("arbitrary",)),
    )(x, w.reshape(1, d), target)


_CB = 128


def _conv_pre(u, w_ref, b_ref):
    s = u.shape[0]
    row = lax.broadcasted_iota(jnp.int32, u.shape, 0)
    pre = b_ref[...] + w_ref[CONV_WIDTH - 1:CONV_WIDTH, :] * u
    for sh in range(1, CONV_WIDTH):
        shifted = jnp.where(row >= sh, pltpu.roll(u, sh, 0), 0.0)
        pre = pre + w_ref[CONV_WIDTH - 1 - sh:CONV_WIDTH - sh, :] * shifted
    return pre


def _conv_fwd(proj3, cw, cb, *, name):
    b, s, _ = proj3.shape
    c0 = _PAD_COLS["xbc"][0] // _CB

    def body(u_ref, w_ref, b_ref, o_ref):
        pre = _conv_pre(u_ref[...].astype(F32), w_ref, b_ref)
        o_ref[...] = pre * _sigmoid(pre)

    return pl.pallas_call(
        body, name=name, grid=(b, CONV_DIM // _CB),
        in_specs=[pl.BlockSpec((None, s, _CB), lambda i, j: (i, 0, c0 + j)),
                  pl.BlockSpec((CONV_WIDTH, _CB), lambda i, j: (0, j)),
                  pl.BlockSpec((1, _CB), lambda i, j: (0, j))],
        out_specs=pl.BlockSpec((None, s, _CB), lambda i, j: (i, 0, j)),
        out_shape=jax.ShapeDtypeStruct((b, s, CONV_DIM), F32),
        compiler_params=_cp(("parallel", "parallel")),
    )(proj3, cw, cb.reshape(1, CONV_DIM))


def _conv_bwd(proj3, cw, cb, dact, *, name):
    b, s, _ = proj3.shape
    c0 = _PAD_COLS["xbc"][0] // _CB

    def body(u_ref, w_ref, b_ref, da_ref, du_ref, dwb_ref):
        u = u_ref[...].astype(F32)
        pre = _conv_pre(u, w_ref, b_ref)
        sg = _sigmoid(pre)
        dpre = da_ref[...] * (sg * (1.0 + pre * (1.0 - sg)))
        row = lax.broadcasted_iota(jnp.int32, u.shape, 0)
        du = w_ref[CONV_WIDTH - 1:CONV_WIDTH, :] * dpre
        rows = [jnp.sum(dpre * u, axis=0, keepdims=True)]
        for sh in range(1, CONV_WIDTH):
            fwd_shift = jnp.where(row < s - sh, pltpu.roll(dpre, s - sh, 0), 0.0)
            du = du + w_ref[CONV_WIDTH - 1 - sh:CONV_WIDTH - sh, :] * fwd_shift
            ush = jnp.where(row >= sh, pltpu.roll(u, sh, 0), 0.0)
            rows.append(jnp.sum(dpre * ush, axis=0, keepdims=True))
        du_ref[...] = du.astype(BF16)

        @pl.when(pl.program_id(1) == 0)
        def _():
            dwb_ref[...] = jnp.zeros_like(dwb_ref)

        for sh in range(CONV_WIDTH):
            k = CONV_WIDTH - 1 - sh
            dwb_ref[k:k + 1, :] += rows[sh]
        dwb_ref[CONV_WIDTH:CONV_WIDTH + 1, :] += jnp.sum(dpre, axis=0, keepdims=True)

    return pl.pallas_call(
        body, name=name, grid=(CONV_DIM // _CB, b),
        in_specs=[pl.BlockSpec((None, s, _CB), lambda j, i: (i, 0, c0 + j)),
                  pl.BlockSpec((CONV_WIDTH, _CB), lambda j, i: (0, j)),
                  pl.BlockSpec((1, _CB), lambda j, i: (0, j)),
                  pl.BlockSpec((None, s, _CB), lambda j, i: (i, 0, j))],
        out_specs=[pl.BlockSpec((None, s, _CB), lambda j, i: (i, 0, j)),
                   pl.BlockSpec((8, _CB), lambda j, i: (0, j))],
        out_shape=[jax.ShapeDtypeStruct((b, s, CONV_DIM), BF16), jax.ShapeDtypeStruct((8, CONV_DIM), F32)],
        compiler_params=_cp(("parallel", "arbitrary")),
    )(proj3, cw, cb.reshape(1, CONV_DIM), dact)


def _ssd_common(dt_ref, dtb_ref, alog_ref):
    row = lax.broadcasted_iota(jnp.int32, (CHUNK, CHUNK), 0)
    lane = lax.broadcasted_iota(jnp.int32, (CHUNK, CHUNK), 1)
    causal = row >= lane
    tri = causal.astype(F32)
    dtv = _softplus(dt_ref[...] + dtb_ref[...])
    a_row = -jnp.exp(alog_ref[...])
    acum = _dot(tri, dtv * a_row, precision=HIGHEST)
    return row, lane, causal, dtv, a_row, acum, acum.T


def _ssd_pair(pp, x, dtv, acum, acum_t, causal, lane, row):
    lo = lane < HEAD_DIM
    r0, r1 = 2 * pp, 2 * pp + 1
    dtp = jnp.where(lo, _col(dtv, r0), _col(dtv, r1))
    ac0, ac1 = _col(acum, r0), _col(acum, r1)
    ar0, ar1 = _row(acum_t, r0), _row(acum_t, r1)
    d0 = jnp.where(causal, jnp.exp(jnp.where(causal, ac0 - ar0, 0.0)), 0.0)
    d1 = jnp.where(causal, jnp.exp(jnp.where(causal, ac1 - ar1, 0.0)), 0.0)
    al0, al1 = _col(ar0, CHUNK - 1), _col(ar1, CHUNK - 1)
    eac = jnp.where(lo, jnp.exp(ac0), jnp.exp(ac1))
    dsp = jnp.where(lo, jnp.exp(al0 - ac0), jnp.exp(al1 - ac1))
    eal = jnp.where(_iota_col() < HEAD_DIM, jnp.exp(al0), jnp.exp(al1))
    return lo, dtp, x * dtp, d0, d1, al0, al1, eac, dsp, eal


def _ssd_fwd(proj3, gates3, xact3, dtb, alog, dsk, nw, *, name):
    b, s, _ = proj3.shape
    nc = s // CHUNK
    dt0 = 0
    z0 = _PAD_COLS["a_z"][0] // D_MODEL

    def body(xs_ref, bm_ref, cm_ref, dt_ref, z_ref, dtb_ref, alog_ref, dsk_ref, nw_ref,
             ya_ref, ypre_ref, hst_ref, h_scr):
        @pl.when(pl.program_id(1) == 0)
        def _():
            h_scr[...] = jnp.zeros_like(h_scr)

        for g in range(N_GROUPS):
            w256 = pl.ds(256 * g, 256)
            w128 = pl.ds(LANES * g, LANES)
            group(xs_ref.at[:, w256], bm_ref.at[:, w128], cm_ref.at[:, w128], dt_ref.at[:, w128],
                  z_ref.at[:, w256], dtb_ref.at[g], alog_ref.at[g], dsk_ref.at[g], nw_ref.at[g],
                  ya_ref.at[:, w256], ypre_ref.at[:, w256], hst_ref.at[g], h_scr.at[g])

    def group(xs_ref, bm_ref, cm_ref, dt_ref, z_ref, dtb_ref, alog_ref, dsk_ref, nw_ref,
              ya_ref, ypre_ref, hst_ref, h_scr):
        row, lane, causal, dtv, a_row, acum, acum_t = _ssd_common(dt_ref, dtb_ref, alog_ref)
        bb = bm_ref[...].astype(BF16)
        cb = cm_ref[...].astype(BF16)
        cbm = _dot_nt(cb, bb)
        hst_ref[...] = h_scr[...]
        dskv = dsk_ref[...]
        for pp in range(2):
            x = xs_ref[:, LANES * pp:LANES * (pp + 1)]
            lo, dtp, xd, d0, d1, al0, al1, eac, dsp, eal = _ssd_pair(pp, x, dtv, acum, acum_t, causal, lane, row)
            xdb = xd.astype(BF16)
            y = jnp.where(lo, _dot((cbm * d0).astype(BF16), xdb), _dot((cbm * d1).astype(BF16), xdb))
            h = h_scr[pp]
            y = y + eac * _dot_nt(cb, h.astype(BF16))
            h_scr[pp] = h * eal + _dot_tn((xd * dsp).astype(BF16), bb)
            dskp = jnp.where((_iota_row() < HEAD_DIM), _col(dskv, 2 * pp), _col(dskv, 2 * pp + 1))
            ypre_ref[:, LANES * pp:LANES * (pp + 1)] = y + x * dskp
        ypre = ypre_ref[...]
        z = z_ref[...].astype(F32)
        yg = ypre * (z * _sigmoid(z))
        rstd = lax.rsqrt(jnp.sum(yg * yg, axis=1, keepdims=True) * (1.0 / 256.0) + NORM_EPS)
        ya_ref[...] = (yg * rstd * nw_ref[...]).astype(BF16)

    g = N_GROUPS
    par = pl.BlockSpec((g, 1, LANES), lambda i, c: (0, 0, 0))
    wide = pl.BlockSpec((None, CHUNK, D_MODEL), lambda i, c: (i, c, 0))
    return pl.pallas_call(
        body, name=name, grid=(b, nc),
        in_specs=[wide,
                  pl.BlockSpec((None, CHUNK, 512), lambda i, c: (i, c, 2)),
                  pl.BlockSpec((None, CHUNK, 512), lambda i, c: (i, c, 3)),
                  pl.BlockSpec((None, CHUNK, 512), lambda i, c: (i, c, dt0)),
                  pl.BlockSpec((None, CHUNK, D_MODEL), lambda i, c: (i, c, z0)),
                  par, par, par,
                  pl.BlockSpec((g, 1, 256), lambda i, c: (0, 0, 0))],
        out_specs=[wide, wide,
                   pl.BlockSpec((None, None, g, 2, CHUNK, SSM_STATE), lambda i, c: (i, c, 0, 0, 0, 0))],
        out_shape=[jax.ShapeDtypeStruct((b, s, D_MODEL), BF16), jax.ShapeDtypeStruct((b, s, D_MODEL), F32),
                   jax.ShapeDtypeStruct((b, nc, g, 2, CHUNK, SSM_STATE), F32)],
        scratch_shapes=[pltpu.VMEM((g, 2, CHUNK, SSM_STATE), F32)],
        compiler_params=_cp(("parallel", "arbitrary")),
    )(xact3, xact3, xact3, gates3, proj3, dtb, alog, dsk, nw)


def _ssd_bwd(proj3, gates3, xact3, dtb, alog, dsk, nw, ypre3, hst, dya3, *, name):
    b, s, _ = proj3.shape
    nc = s // CHUNK
    dt0 = 0
    z0 = _PAD_COLS["a_z"][0] // D_MODEL

    def body(xs_ref, bm_ref, cm_ref, dt_ref, z_ref, dtb_ref, alog_ref, dsk_ref, nw_ref, ypre_ref, hst_ref,
             dya_ref, dact_ref, dz_ref, ddt_ref, ddtb_ref, dalog_ref, ddsk_ref, dnw_ref, dh_scr):
        first = jnp.logical_and(pl.program_id(0) == 0, pl.program_id(1) == 0)

        @pl.when(first)
        def _():
            ddtb_ref[...] = jnp.zeros_like(ddtb_ref)
            dalog_ref[...] = jnp.zeros_like(dalog_ref)
            ddsk_ref[...] = jnp.zeros_like(ddsk_ref)
            dnw_ref[...] = jnp.zeros_like(dnw_ref)

        @pl.when(pl.program_id(1) == 0)
        def _():
            dh_scr[...] = jnp.zeros_like(dh_scr)

        for g in range(N_GROUPS):
            w256 = pl.ds(256 * g, 256)
            w128 = pl.ds(LANES * g, LANES)
            group(xs_ref.at[:, w256], bm_ref.at[:, w128], cm_ref.at[:, w128], dt_ref.at[:, w128],
                  z_ref.at[:, w256], dtb_ref.at[g], alog_ref.at[g], dsk_ref.at[g], nw_ref.at[g],
                  ypre_ref.at[:, w256], hst_ref.at[g], dya_ref.at[:, w256],
                  dact_ref.at[:, w256], dact_ref.at[:, pl.ds(D_MODEL + LANES * g, LANES)],
                  dact_ref.at[:, pl.ds(D_MODEL + 512 + LANES * g, LANES)], dz_ref.at[:, w256], ddt_ref.at[:, w128],
                  ddtb_ref.at[g], dalog_ref.at[g], ddsk_ref.at[g], dnw_ref.at[g], dh_scr.at[g])

    def group(xs_ref, bm_ref, cm_ref, dt_ref, z_ref, dtb_ref, alog_ref, dsk_ref, nw_ref, ypre_ref, hst_ref,
              dya_ref, dxs_ref, dbm_ref, dcm_ref, dz_ref, ddt_ref, ddtb_ref, dalog_ref, ddsk_ref, dnw_ref,
              dh_scr):
        row, lane, causal, dtv, a_row, acum, acum_t = _ssd_common(dt_ref, dtb_ref, alog_ref)
        lane1 = _iota_row()
        bb = bm_ref[...].astype(BF16)
        cb = cm_ref[...].astype(BF16)
        cbm = _dot_nt(cb, bb)

        z = z_ref[...].astype(F32)
        ypre = ypre_ref[...]
        dya = dya_ref[...]
        sz = _sigmoid(z)
        silu = z * sz
        yg = ypre * silu
        rstd = lax.rsqrt(jnp.sum(yg * yg, axis=1, keepdims=True) * (1.0 / 256.0) + NORM_EPS)
        dnw_ref[...] += jnp.sum(dya * yg * rstd, axis=0, keepdims=True)
        dn = dya * nw_ref[...]
        dyg = rstd * dn - yg * (rstd * rstd * rstd * (1.0 / 256.0)) * jnp.sum(dn * yg, axis=1, keepdims=True)
        dz_ref[...] = (dyg * ypre * (sz * (1.0 + z * (1.0 - sz)))).astype(BF16)
        dy_all = dyg * silu

        dskv = dsk_ref[...]
        da_cols = jnp.zeros((CHUNK, LANES), F32)
        dxt_cols = jnp.zeros((CHUNK, LANES), F32)
        ddsk_row = jnp.zeros((1, LANES), F32)
        dcb = jnp.zeros((CHUNK, CHUNK), F32)
        dc = jnp.zeros((CHUNK, SSM_STATE), F32)
        db = jnp.zeros((CHUNK, SSM_STATE), F32)
        last = _iota_col() == CHUNK - 1
        for pp in range(2):
            r0, r1 = 2 * pp, 2 * pp + 1
            x = xs_ref[:, LANES * pp:LANES * (pp + 1)]
            dy = dy_all[:, LANES * pp:LANES * (pp + 1)]
            lo, dtp, xd, d0, d1, al0, al1, eac, dsp, eal = _ssd_pair(pp, x, dtv, acum, acum_t, causal, lane, row)
            w0, w1 = cbm * d0, cbm * d1
            w0b, w1b = w0.astype(BF16), w1.astype(BF16)
            xdb = xd.astype(BF16)
            dyb = dy.astype(BF16)
            h = hst_ref[pp]
            dhn = dh_scr[pp]
            hb = h.astype(BF16)
            dhb = dhn.astype(BF16)
            g0 = _dot_nt(jnp.where(lo, dy, 0.0).astype(BF16), xdb)
            g1 = _dot_nt(jnp.where(lo, 0.0, dy).astype(BF16), xdb)
            dcb = dcb + g0 * d0 + g1 * d1
            m0, m1 = g0 * w0, g1 * w1
            bdh = _dot_nt(bb, dhb)
            dxd = jnp.where(lo, _dot_tn(w0b, dyb), _dot_tn(w1b, dyb)) + dsp * bdh
            ch = _dot_nt(cb, hb)
            edy = eac * dy
            edyb = edy.astype(BF16)
            xds = xd * dsp
            dc = dc + _dot(edyb, hb)
            db = db + _dot(xds.astype(BF16), dhb)
            dh_scr[pp] = dhn * eal + _dot_tn(edyb, cb)
            t2 = edy * ch
            t3 = xds * bdh
            dhh = dhn * h
            s4_0 = jnp.sum(jnp.sum(jnp.where(row < HEAD_DIM, dhh, 0.0), axis=0, keepdims=True), axis=1, keepdims=True)
            s4_1 = jnp.sum(jnp.sum(dhh, axis=0, keepdims=True), axis=1, keepdims=True) - s4_0
            t23 = t2 - t3
            t23_0 = jnp.sum(jnp.where(lo, t23, 0.0), axis=1, keepdims=True)
            t23_1 = jnp.sum(t23, axis=1, keepdims=True) - t23_0
            c3 = jnp.sum(t3, axis=0, keepdims=True)
            c3_0 = jnp.sum(jnp.where(_iota_row() < HEAD_DIM, c3, 0.0), axis=1, keepdims=True)
            c3_1 = jnp.sum(c3, axis=1, keepdims=True) - c3_0
            dal0 = c3_0 + jnp.exp(al0) * s4_0
            dal1 = c3_1 + jnp.exp(al1) * s4_1
            dac0 = jnp.sum(m0 - m0.T, axis=1, keepdims=True) + t23_0 + jnp.where(last, dal0, 0.0)
            dac1 = jnp.sum(m1 - m1.T, axis=1, keepdims=True) + t23_1 + jnp.where(last, dal1, 0.0)
            da_cols = da_cols + jnp.where(lane == r0, dac0, 0.0) + jnp.where(lane == r1, dac1, 0.0)
            xx = dxd * x
            x0 = jnp.sum(jnp.where(lo, xx, 0.0), axis=1, keepdims=True)
            x1 = jnp.sum(xx, axis=1, keepdims=True) - x0
            dxt_cols = dxt_cols + jnp.where(lane == r0, x0, 0.0) + jnp.where(lane == r1, x1, 0.0)
            dskp = jnp.where((_iota_row() < HEAD_DIM), _col(dskv, r0), _col(dskv, r1))
            dxs_ref[:, LANES * pp:LANES * (pp + 1)] = dxd * dtp + dy * dskp
            yx = jnp.sum(dy * x, axis=0, keepdims=True)
            k0 = jnp.sum(jnp.where((_iota_row() < HEAD_DIM), yx, 0.0), axis=1, keepdims=True)
            k1 = jnp.sum(yx, axis=1, keepdims=True) - k0
            ddsk_row = ddsk_row + jnp.where(lane1 == r0, k0, 0.0) + jnp.where(lane1 == r1, k1, 0.0)
        dcbb = dcb.astype(BF16)
        dcm_ref[...] = dc + _dot(dcbb, bb)
        dbm_ref[...] = db + _dot_tn(dcbb, cb)
        tri_t = (row <= lane).astype(F32)
        dadt = _dot(tri_t, da_cols, precision=HIGHEST)
        ddtv = dadt * a_row + dxt_cols
        dalog_ref[...] += jnp.sum(dadt * dtv, axis=0, keepdims=True) * a_row
        ddt_raw = ddtv * _sigmoid(dt_ref[...] + dtb_ref[...])
        ddt_ref[...] = ddt_raw.astype(BF16)
        ddtb_ref[...] += jnp.sum(ddt_raw, axis=0, keepdims=True)
        ddsk_ref[...] += ddsk_row

    g = N_GROUPS
    rc = lambda c: nc - 1 - c
    par = pl.BlockSpec((g, 1, LANES), lambda i, c: (0, 0, 0))
    parw = pl.BlockSpec((g, 1, 256), lambda i, c: (0, 0, 0))
    wide = pl.BlockSpec((None, CHUNK, D_MODEL), lambda i, c: (i, rc(c), 0))
    blk512 = lambda col: pl.BlockSpec((None, CHUNK, 512), lambda i, c: (i, rc(c), col))
    return pl.pallas_call(
        body, name=name, grid=(b, nc),
        in_specs=[wide, blk512(2), blk512(3), blk512(dt0),
                  pl.BlockSpec((None, CHUNK, D_MODEL), lambda i, c: (i, rc(c), z0)),
                  par, par, par, parw,
                  wide,
                  pl.BlockSpec((None, None, g, 2, CHUNK, SSM_STATE), lambda i, c: (i, rc(c), 0, 0, 0, 0)),
                  wide],
        out_specs=[pl.BlockSpec((None, CHUNK, CONV_DIM), lambda i, c: (i, rc(c), 0)), wide, blk512(0),
                   par, par, par, parw],
        out_shape=[jax.ShapeDtypeStruct((b, s, CONV_DIM), F32), jax.ShapeDtypeStruct((b, s, D_MODEL), BF16),
                   jax.ShapeDtypeStruct((b, s, 512), BF16),
                   jax.ShapeDtypeStruct((g, 1, LANES), F32), jax.ShapeDtypeStruct((g, 1, LANES), F32),
                   jax.ShapeDtypeStruct((g, 1, LANES), F32), jax.ShapeDtypeStruct((g, 1, 256), F32)],
        scratch_shapes=[pltpu.VMEM((g, 2, CHUNK, SSM_STATE), F32)],
        compiler_params=_cp(("arbitrary", "arbitrary")),
    )(xact3, xact3, xact3, gates3, proj3, dtb, alog, dsk, nw, ypre3, hst, dya3)


_FGATE_ROWS = 512


def _fgate_fwd(gates3, fb, *, name):
    b, s, _ = gates3.shape
    rows = min(_FGATE_ROWS, s)
    f0 = _PAD_COLS["a_dt"][1] // LANES

    def body(f_ref, fb_ref, cum_ref, carry):
        @pl.when(pl.program_id(1) == 0)
        def _():
            carry[...] = jnp.zeros_like(carry)

        row = lax.broadcasted_iota(jnp.int32, (rows, rows), 0)
        lane = lax.broadcasted_iota(jnp.int32, (rows, rows), 1)
        tri = (row >= lane).astype(F32)
        lf = -_softplus(-(f_ref[...] + fb_ref[...]))
        cs = _dot(tri, lf, precision=HIGHEST) + carry[0:1, :]
        cum_ref[...] = cs
        carry[0:1, :] = _row(cs, rows - 1)

    return pl.pallas_call(
        body, name=name, grid=(b, s // rows),
        in_specs=[pl.BlockSpec((None, rows, LANES), lambda i, c: (i, c, f0)),
                  pl.BlockSpec((1, LANES), lambda i, c: (0, 0))],
        out_specs=pl.BlockSpec((None, rows, LANES), lambda i, c: (i, c, 0)),
        out_shape=jax.ShapeDtypeStruct((b, s, LANES), F32),
        scratch_shapes=[pltpu.VMEM((8, LANES), F32)],
        compiler_params=_cp(("parallel", "arbitrary")),
    )(gates3, fb)


def _fgate_bwd(gates3, fb, dcum, *, name):
    b, s, _ = gates3.shape
    rows = min(_FGATE_ROWS, s)
    nc = s // rows
    f0 = _PAD_COLS["a_dt"][1] // LANES
    npair = dcum.shape[1]

    def body(f_ref, fb_ref, dc_ref, df_ref, dfb_ref, carry):
        first = jnp.logical_and(pl.program_id(0) == 0, pl.program_id(1) == 0)

        @pl.when(first)
        def _():
            dfb_ref[...] = jnp.zeros_like(dfb_ref)

        @pl.when(pl.program_id(1) == 0)
        def _():
            carry[...] = jnp.zeros_like(carry)

        row = lax.broadcasted_iota(jnp.int32, (rows, rows), 0)
        lane = lax.broadcasted_iota(jnp.int32, (rows, rows), 1)
        tri_t = (row <= lane).astype(F32)
        dc = -jnp.sum(dc_ref[...], axis=0)
        dlf = _dot(tri_t, dc, precision=HIGHEST) + carry[0:1, :]
        carry[0:1, :] = _row(dlf, 0)
        df = dlf * _sigmoid(-(f_ref[...] + fb_ref[...]))
        df_ref[...] = df.astype(BF16)
        dfb_ref[...] += jnp.sum(df, axis=0, keepdims=True)

    return pl.pallas_call(
        body, name=name, grid=(b, nc),
        in_specs=[pl.BlockSpec((None, rows, LANES), lambda i, c: (i, nc - 1 - c, f0)),
                  pl.BlockSpec((1, LANES), lambda i, c: (0, 0)),
                  pl.BlockSpec((None, npair, rows, LANES), lambda i, c: (i, 0, nc - 1 - c, 0))],
        out_specs=[pl.BlockSpec((None, rows, LANES), lambda i, c: (i, nc - 1 - c, 0)),
                   pl.BlockSpec((1, LANES), lambda i, c: (0, 0))],
        out_shape=[jax.ShapeDtypeStruct((b, s, LANES), BF16), jax.ShapeDtypeStruct((1, LANES), F32)],
        scratch_shapes=[pltpu.VMEM((8, LANES), F32)],
        compiler_params=_cp(("arbitrary", "arbitrary")),
    )(gates3, fb, dcum)


_SCALE = HEAD_DIM ** -0.5
_NEG = -1e30


_ST_LSE, _ST_DELTA, _ST_MJ = 0, 2, 8


_SR = 40


def _ck_rep(cum):
    b, s, _ = cum.shape
    t = jnp.transpose(cum[:, :, :N_HEADS], (0, 2, 1)).reshape(b, N_HEADS // 2, 2, s, 1)
    return jnp.broadcast_to(t, (b, N_HEADS // 2, 2, s, LANES))


def _foxt_fwd(proj3, ckrep, *, name, tb):
    b, s, _ = proj3.shape
    nq = s // tb
    assert _ST_MJ + 2 * nq <= _SR
    q0 = _PAD_COLS["c_q"][0] // LANES
    k0 = _PAD_COLS["c_k"][0] // LANES
    v0 = _PAD_COLS["c_v"][0] // LANES
    z0 = _PAD_COLS["c_z"][0] // LANES
    rep = tb // LANES

    def body(q_ref, k_ref, v_ref, z_ref, ck_ref, y_ref, o_ref, st_ref):
        i = pl.program_id(2)
        lane = lax.broadcasted_iota(jnp.int32, (tb, LANES), 1)
        lo = lane < HEAD_DIM
        lo_r = lax.broadcasted_iota(jnp.int32, (LANES, tb), 0) < HEAD_DIM
        srow = lax.broadcasted_iota(jnp.int32, (_SR, tb), 0)
        q = q_ref[...].astype(F32) * _SCALE
        qms = (jnp.where(lo, q, 0.0).astype(BF16), jnp.where(lo, 0.0, q).astype(BF16))
        ones_at = (HEAD_DIM, 0)

        def block(j, carry, diagonal):
            ks = pl.ds(pl.multiple_of(j * tb, tb), tb)
            kb = k_ref[ks, :].astype(BF16)
            v = v_ref[ks, :].astype(F32)
            vts = (jnp.where(lo, v, jnp.where(lane == ones_at[0], 1.0, 0.0)).T.astype(BF16),
                   jnp.where(lo, jnp.where(lane == ones_at[1], 1.0, 0.0), v).T.astype(BF16))
            if diagonal:
                key = lax.broadcasted_iota(jnp.int32, (tb, tb), 0)
                qry = lax.broadcasted_iota(jnp.int32, (tb, tb), 1)
                mask = key <= qry
            ms, ls, acc, st = carry
            new_m, new_l, pvs, alphas = [], [], [], []
            for hh in range(2):
                sc = _dot_nt(kb, qms[hh]) - jnp.tile(ck_ref[hh, ks, :], (1, rep))
                if diagonal:
                    sc = jnp.where(mask, sc, _NEG)
                m_new = jnp.maximum(ms[hh], jnp.max(sc, axis=0, keepdims=True))
                alpha = jnp.exp(ms[hh] - m_new)
                pv = _dot(vts[hh], jnp.exp(sc - m_new).astype(BF16))
                rs = _row(pv[ones_at[hh]:ones_at[hh] + 8, :], 0)
                new_l.append(alpha * ls[hh] + rs)
                new_m.append(m_new)
                pvs.append(pv)
                alphas.append(alpha)
                st = jnp.where(srow == _ST_MJ + 2 * j + hh, m_new, st)
            acc = jnp.where(lo_r, alphas[0] * acc + pvs[0], alphas[1] * acc + pvs[1])
            return (tuple(new_m), tuple(new_l), acc, st)

        neg = jnp.full((1, tb), _NEG, F32)
        zero = jnp.zeros((1, tb), F32)
        init = ((neg, neg), (zero, zero), jnp.zeros((LANES, tb), F32), jnp.zeros((_SR, tb), F32))
        carry = lax.fori_loop(0, i, lambda j, c: block(j, c, False), init)
        ms, ls, acc, st = block(i, carry, True)
        o = (acc / jnp.where(lo_r, ls[0], ls[1])).T
        o_ref[...] = o
        st = jnp.where(srow == _ST_LSE, ms[0] + jnp.log(ls[0]), st)
        st_ref[...] = jnp.where(srow == _ST_LSE + 1, ms[1] + jnp.log(ls[1]), st)
        z = z_ref[...].astype(F32)
        y_ref[...] = (o * (z * _sigmoid(z))).astype(BF16)

    qspec = lambda c0: pl.BlockSpec((None, tb, LANES), lambda bi, p, i: (bi, i, c0 + p))
    kspec = lambda c0: pl.BlockSpec((None, s, LANES), lambda bi, p, i: (bi, 0, c0 + p))
    ospec = pl.BlockSpec((None, tb, LANES), lambda bi, p, i: (bi, i, p))
    return pl.pallas_call(
        body, name=name, grid=(b, N_HEADS // 2, nq),
        in_specs=[qspec(q0), kspec(k0), kspec(v0), qspec(z0),
                  pl.BlockSpec((None, None, 2, s, LANES), lambda bi, p, i: (bi, p, 0, 0, 0))],
        out_specs=[ospec, ospec, pl.BlockSpec((None, None, None, _SR, tb), lambda bi, p, i: (bi, p, i, 0, 0))],
        out_shape=[jax.ShapeDtypeStruct((b, s, D_MODEL), BF16), jax.ShapeDtypeStruct((b, s, D_MODEL), F32),
                   jax.ShapeDtypeStruct((b, N_HEADS // 2, nq, _SR, tb), F32)],
        compiler_params=_cp(("parallel", "parallel", "arbitrary")),
    )(proj3, proj3, proj3, proj3, ckrep)


def _foxt_prep(proj3, o3, stat, dy3, *, name, tb):
    b, s, _ = proj3.shape
    nq = s // tb
    z0 = _PAD_COLS["c_z"][0] // 256

    def body(z_ref, o_ref, fst_ref, dy_ref, dz_ref, do_ref, st_ref):
        z = z_ref[...].astype(F32)
        sz = _sigmoid(z)
        dy = dy_ref[...]
        o = o_ref[...]
        do = dy * (z * sz)
        dz_ref[...] = (dy * o * (sz * (1.0 + z * (1.0 - sz)))).astype(BF16)
        do_ref[...] = do
        doo = do.astype(BF16).astype(F32) * o
        r8 = lax.broadcasted_iota(jnp.int32, (8, LANES), 0)
        l8 = lax.broadcasted_iota(jnp.int32, (8, LANES), 1)
        pick = jnp.logical_or(jnp.logical_and(r8 == 0, l8 < HEAD_DIM),
                              jnp.logical_and(r8 == 1, l8 >= HEAD_DIM)).astype(F32)
        srow = lax.broadcasted_iota(jnp.int32, (_SR, tb), 0)
        for pp in range(2):
            d8 = _dot(pick, doo[:, LANES * pp:LANES * (pp + 1)], ((1,), (1,)), precision=HIGHEST)
            st = jnp.where(srow == _ST_DELTA, _row(d8, 0), fst_ref[pp])
            st_ref[pp] = jnp.where(srow == _ST_DELTA + 1, _row(d8, 1), st)

    ospec = pl.BlockSpec((None, tb, 256), lambda bi, p, i: (bi, i, p))
    sspec = pl.BlockSpec((None, 2, None, _SR, tb), lambda bi, p, i: (bi, p, i, 0, 0))
    return pl.pallas_call(
        body, name=name, grid=(b, N_HEADS // 4, nq),
        in_specs=[pl.BlockSpec((None, tb, 256), lambda bi, p, i: (bi, i, z0 + p)), ospec, sspec, ospec],
        out_specs=[ospec, ospec, sspec],
        out_shape=[jax.ShapeDtypeStruct((b, s, D_MODEL), BF16), jax.ShapeDtypeStruct((b, s, D_MODEL), F32),
                   jax.ShapeDtypeStruct((b, N_HEADS // 2, nq, _SR, tb), F32)],
        compiler_params=_cp(("parallel", "parallel", "parallel")),
    )(proj3, o3, stat, dy3)


def _foxt_bwd(proj3, ckrep, do3, stats, *, name, tb):
    b, s, _ = proj3.shape
    nq = s // tb
    q0 = _PAD_COLS["c_q"][0] // LANES
    k0 = _PAD_COLS["c_k"][0] // LANES
    v0 = _PAD_COLS["c_v"][0] // LANES
    rep = tb // LANES

    def body(q_ref, do_ref, st_ref, k_ref, v_ref, ck_ref, dq_ref, dk_ref, dv_ref, cs_ref):
        j = pl.program_id(2)
        lane = lax.broadcasted_iota(jnp.int32, (tb, LANES), 1)
        lo = lane < HEAD_DIM
        lo_r = lax.broadcasted_iota(jnp.int32, (LANES, tb), 0) < HEAD_DIM

        @pl.when(j == 0)
        def _():
            dq_ref[...] = jnp.zeros_like(dq_ref)

        kf = k_ref[...].astype(F32)
        kb = kf.astype(BF16)
        kt = kf.T.astype(BF16)
        vb = v_ref[...].astype(BF16)
        cks = (jnp.tile(ck_ref[0], (1, rep)), jnp.tile(ck_ref[1], (1, rep)))

        def block(i, carry, diagonal):
            qs = pl.ds(pl.multiple_of(i * tb, tb), tb)
            q = q_ref[qs, :].astype(F32) * _SCALE
            do = do_ref[qs, :]
            st = st_ref[i]
            if diagonal:
                key = lax.broadcasted_iota(jnp.int32, (tb, tb), 0)
                qry = lax.broadcasted_iota(jnp.int32, (tb, tb), 1)
                mask = key <= qry
            dk, dv, cs = carry
            new_cs, dqs = [], []
            for hh in range(2):
                sel = lo if hh == 0 else jnp.logical_not(lo)
                qm = jnp.where(sel, q, 0.0).astype(BF16)
                dom = jnp.where(sel, do, 0.0).astype(BF16)
                sc = _dot_nt(kb, qm) - cks[hh]
                if diagonal:
                    sc = jnp.where(mask, sc, _NEG)
                mj = _row(st, _ST_MJ + 2 * j + hh)
                w = jnp.exp(mj - _row(st, _ST_LSE + hh))
                ph = jnp.exp(sc - mj).astype(BF16).astype(F32) * w
                ds = ph * (_dot_nt(vb, dom) - _row(st, _ST_DELTA + hh))
                dsb = ds.astype(BF16)
                dv = dv + _dot(ph.astype(BF16), dom)
                dk = dk + _dot(dsb, qm)
                new_cs.append(cs[hh] + jnp.sum(ds, axis=1, keepdims=True))
                dqs.append(_dot(kt, dsb))
            dq_ref[i] += jnp.where(lo_r, dqs[0], dqs[1]) * _SCALE
            return (dk, dv, tuple(new_cs))

        zcol = jnp.zeros((tb, 1), F32)
        init = (jnp.zeros((tb, LANES), F32), jnp.zeros((tb, LANES), F32), (zcol, zcol))
        carry = block(j, init, True)
        dk, dv, cs = lax.fori_loop(j + 1, nq, lambda i, c: block(i, c, False), carry)
        dk_ref[...] = dk.astype(BF16)
        dv_ref[...] = dv.astype(BF16)
        p2 = 2 * pl.program_id(1)
        cs_ref[...] = jnp.where(lane == p2, cs[0], jnp.where(lane == p2 + 1, cs[1], 0.0))

    full = lambda c0: pl.BlockSpec((None, s, LANES), lambda bi, p, j: (bi, 0, c0 + p))
    kspec = lambda c0: pl.BlockSpec((None, tb, LANES), lambda bi, p, j: (bi, j, c0 + p))
    ko = pl.BlockSpec((None, tb, LANES), lambda bi, p, j: (bi, j, p))
    sall = pl.BlockSpec((None, None, nq, _SR, tb), lambda bi, p, j: (bi, p, 0, 0, 0))
    dqspec = pl.BlockSpec((None, None, nq, LANES, tb), lambda bi, p, j: (bi, p, 0, 0, 0))
    return pl.pallas_call(
        body, name=name, grid=(b, N_HEADS // 2, nq),
        in_specs=[full(q0), full(0), sall, kspec(k0), kspec(v0),
                  pl.BlockSpec((None, None, 2, tb, LANES), lambda bi, p, j: (bi, p, 0, j, 0))],
        out_specs=[dqspec, ko, ko, pl.BlockSpec((None, None, tb, LANES), lambda bi, p, j: (bi, p, j, 0))],
        out_shape=[jax.ShapeDtypeStruct((b, N_HEADS // 2, nq, LANES, tb), F32),
                   jax.ShapeDtypeStruct((b, s, D_MODEL), BF16), jax.ShapeDtypeStruct((b, s, D_MODEL), BF16),
                   jax.ShapeDtypeStruct((b, N_HEADS // 2, s, LANES), F32)],
        compiler_params=_cp(("parallel", "parallel", "arbitrary")),
    )(proj3, do3, stats, proj3, proj3, ckrep)


def _rope(x, cos, sin_signed):
    w = x.shape[1]
    lane = lax.broadcasted_iota(jnp.int32, x.shape, 1)
    first = (lane % HEAD_DIM) < (HEAD_DIM // 2)
    rot = jnp.where(first, pltpu.roll(x, w - HEAD_DIM // 2, 1), pltpu.roll(x, HEAD_DIM // 2, 1))
    return x * cos + rot * sin_signed


_QB = 4
_QROWS = _QB * CHUNK


def _swa_keys(g, kc_ref, kp_ref, vc_ref, vp_ref, cq_ref, sq_ref, cp_ref, sp_ref):
    def both_halves(x):
        x = x.astype(F32)
        lane = lax.broadcasted_iota(jnp.int32, x.shape, 1)
        keep = (lane // HEAD_DIM) == (g % 2)
        return jnp.where(keep, x, pltpu.roll(x, HEAD_DIM, 1))

    cq, sq, cpv, spv = cq_ref[...], sq_ref[...], cp_ref[...], sp_ref[...]
    kc = _rope(both_halves(kc_ref[...]), cq, sq).astype(BF16)
    kp = _rope(both_halves(kp_ref[...]), cpv, spv).astype(BF16)
    return cq, sq, cpv, spv, kc, kp, both_halves(vc_ref[...]).astype(BF16), both_halves(vp_ref[...]).astype(BF16)


def _swa_stack(pairs, lo):
    return jnp.concatenate([jnp.where(lo, pairs[0], 0.0), jnp.where(lo, 0.0, pairs[0]),
                            jnp.where(lo, pairs[1], 0.0), jnp.where(lo, 0.0, pairs[1])], axis=0).astype(BF16)


def _swa_mask4(prev_valid):
    r = lax.broadcasted_iota(jnp.int32, (4 * CHUNK, 2 * CHUNK), 0) & (CHUNK - 1)
    c = lax.broadcasted_iota(jnp.int32, (4 * CHUNK, 2 * CHUNK), 1)
    own = jnp.logical_and(c >= CHUNK, c - CHUNK <= r)
    before = jnp.logical_and(c < CHUNK, c > r)
    if prev_valid is True:
        return jnp.logical_or(own, before)
    return jnp.logical_or(own, jnp.logical_and(before, prev_valid))


def _swa_sink4(skv):
    return jnp.concatenate([jnp.broadcast_to(_col(skv, j), (CHUNK, 1)) for j in range(4)], axis=0)


def _swa_specs(order):
    def spec(shape, fn):
        return pl.BlockSpec(shape, lambda *ids: fn(*order(*ids)))

    q0 = _PAD_COLS["b_q"][0] // 256
    z0 = _PAD_COLS["b_z"][0] // 256
    k0 = _PAD_COLS["b_k"][0] // LANES
    v0 = _PAD_COLS["b_v"][0] // LANES
    prev = lambda i: jnp.maximum(_QB * i - 1, 0)
    return dict(
        kc=spec((None, _QROWS, LANES), lambda bi, g, i: (bi, i, k0 + g // 2)),
        kp=spec((None, CHUNK, LANES), lambda bi, g, i: (bi, prev(i), k0 + g // 2)),
        vc=spec((None, _QROWS, LANES), lambda bi, g, i: (bi, i, v0 + g // 2)),
        vp=spec((None, CHUNK, LANES), lambda bi, g, i: (bi, prev(i), v0 + g // 2)),
        q=spec((None, _QROWS, 256), lambda bi, g, i: (bi, i, q0 + g)),
        z=spec((None, _QROWS, 256), lambda bi, g, i: (bi, i, z0 + g)),
        blk=spec((None, _QROWS, 256), lambda bi, g, i: (bi, i, g)),
        kcur=spec((None, _QROWS, LANES), lambda bi, g, i: (bi, i, g)),
        kstep=spec((None, CHUNK, LANES), lambda bi, g, i: (bi, i, g)),
        tcur=spec((_QROWS, LANES), lambda bi, g, i: (i, 0)),
        tprev=spec((CHUNK, LANES), lambda bi, g, i: (prev(i), 0)),
        sk=spec((None, 1, LANES), lambda bi, g, i: (g, 0, 0)))


def _swa_fwd(proj3, cos, sin, sinks, *, name):
    b, s, _ = proj3.shape

    def body(q_ref, z_ref, kc_ref, kp_ref, vc_ref, vp_ref, cq_ref, sq_ref, cp_ref, sp_ref, sk_ref,
             y_ref, o_ref, lse_ref):
        i = pl.program_id(2)
        cq_all, sq_all, _, _, kc_all, kp0, vc_all, vp0 = _swa_keys(
            pl.program_id(1), kc_ref, kp_ref, vc_ref, vp_ref, cq_ref, sq_ref, cp_ref, sp_ref)
        lo = lax.broadcasted_iota(jnp.int32, (CHUNK, LANES), 1) < HEAD_DIM
        sink4 = _swa_sink4(sk_ref[...])
        for u in range(_QB):
            rs = slice(CHUNK * u, CHUNK * (u + 1))
            ps = slice(CHUNK * (u - 1), CHUNK * u)
            cq, sq = cq_all[rs], sq_all[rs]
            kp, vp = (kp0, vp0) if u == 0 else (kc_all[ps], vc_all[ps])
            kk = jnp.concatenate([kp, kc_all[rs]], axis=0)
            vv = jnp.concatenate([vp, vc_all[rs]], axis=0)
            q4 = _swa_stack([_rope(q_ref[rs, LANES * pp:LANES * (pp + 1)].astype(F32), cq, sq) * _SCALE
                             for pp in range(2)], lo)
            sc = jnp.where(_swa_mask4(True if u > 0 else i > 0), _dot_nt(q4, kk), _NEG)
            m = jnp.maximum(jnp.max(sc, axis=1, keepdims=True), sink4)
            pr = jnp.exp(sc - m)
            l = jnp.sum(pr, axis=1, keepdims=True) + jnp.exp(sink4 - m)
            o4 = _dot(pr.astype(BF16), vv) / l
            lse4 = m + jnp.log(l)
            for pp in range(2):
                ls = slice(LANES * pp, LANES * (pp + 1))
                h0 = slice(2 * CHUNK * pp, 2 * CHUNK * pp + CHUNK)
                h1 = slice(2 * CHUNK * pp + CHUNK, 2 * CHUNK * (pp + 1))
                o = jnp.where(lo, o4[h0], o4[h1])
                z = z_ref[rs, ls].astype(F32)
                o_ref[rs, ls] = o
                lse_ref[rs, ls] = jnp.where(lo, lse4[h0], lse4[h1])
                y_ref[rs, ls] = (o * (z * _sigmoid(z))).astype(BF16)

    sp = _swa_specs(lambda bi, g, i: (bi, g, i))
    return pl.pallas_call(
        body, name=name, grid=(b, N_GROUPS, s // _QROWS),
        in_specs=[sp["q"], sp["z"], sp["kc"], sp["kp"], sp["vc"], sp["vp"],
                  sp["tcur"], sp["tcur"], sp["tprev"], sp["tprev"], sp["sk"]],
        out_specs=[sp["blk"], sp["blk"], sp["blk"]],
        out_shape=[jax.ShapeDtypeStruct((b, s, D_MODEL), BF16)] + [jax.ShapeDtypeStruct((b, s, D_MODEL), F32)] * 2,
        compiler_params=_cp(("parallel", "parallel", "parallel")),
    )(proj3, proj3, proj3, proj3, proj3, proj3, cos, sin, cos, sin, sinks)


def _swa_bwd(proj3, cos, sin, sinks, o3, lse3, dy3, *, name):
    b, s, _ = proj3.shape

    def body(q_ref, z_ref, kc_ref, kp_ref, vc_ref, vp_ref, cq_ref, sq_ref, cp_ref, sp_ref, sk_ref,
             o_ref, lse_ref, dy_ref, dq_ref, dz_ref, dkc_ref, dkp_ref, dvc_ref, dvp_ref, dsk_ref):
        i = pl.program_id(2)
        first = jnp.logical_and(pl.program_id(1) == 0, i == 0)

        @pl.when(first)
        def _():
            dsk_ref[...] = jnp.zeros_like(dsk_ref)

        cq_all, sq_all, cpv, spv, kc_all, kp0, vc_all, vp0 = _swa_keys(
            pl.program_id(0), kc_ref, kp_ref, vc_ref, vp_ref, cq_ref, sq_ref, cp_ref, sp_ref)
        lo = lax.broadcasted_iota(jnp.int32, (CHUNK, LANES), 1) < HEAD_DIM
        lane1 = lax.broadcasted_iota(jnp.int32, (1, LANES), 1)
        sink4 = _swa_sink4(sk_ref[...])
        zero = jnp.zeros((CHUNK, LANES), F32)
        dks = [zero] * (_QB + 1)
        dvs = [zero] * (_QB + 1)
        dsk_row = jnp.zeros((1, LANES), F32)
        for u in range(_QB):
            rs = slice(CHUNK * u, CHUNK * (u + 1))
            ps = slice(CHUNK * (u - 1), CHUNK * u)
            cq, sq = cq_all[rs], sq_all[rs]
            kp, vp = (kp0, vp0) if u == 0 else (kc_all[ps], vc_all[ps])
            kk = jnp.concatenate([kp, kc_all[rs]], axis=0)
            vv = jnp.concatenate([vp, vc_all[rs]], axis=0)
            q4 = _swa_stack([_rope(q_ref[rs, LANES * pp:LANES * (pp + 1)].astype(F32), cq, sq) * _SCALE
                             for pp in range(2)], lo)
            dos, lses = [], []
            for pp in range(2):
                ls = slice(LANES * pp, LANES * (pp + 1))
                z = z_ref[rs, ls].astype(F32)
                sz = _sigmoid(z)
                dy = dy_ref[rs, ls]
                dos.append(dy * (z * sz))
                dz_ref[rs, ls] = (dy * o_ref[rs, ls] * (sz * (1.0 + z * (1.0 - sz)))).astype(BF16)
                lse = lse_ref[rs, ls]
                lses += [_col(lse, 0), _col(lse, HEAD_DIM)]
            do4 = _swa_stack(dos, lo)
            lse4 = jnp.concatenate(lses, axis=0)
            pr = jnp.exp(jnp.where(_swa_mask4(True if u > 0 else i > 0), _dot_nt(q4, kk), _NEG) - lse4)
            dp = _dot_nt(do4, vv)
            dl = jnp.sum(pr * dp, axis=1, keepdims=True)
            ds = (pr * (dp - dl)).astype(BF16)
            dsink = -jnp.exp(sink4 - lse4) * dl
            for j in range(4):
                dsk_row = dsk_row + jnp.where(
                    lane1 == j, jnp.sum(dsink[CHUNK * j:CHUNK * (j + 1)], axis=0, keepdims=True), 0.0)
            dq4 = _dot(ds, kk)
            dkk = _dot_tn(ds, q4)
            dvv = _dot_tn(pr.astype(BF16), do4)
            dks[u], dks[u + 1] = dks[u] + dkk[:CHUNK], dks[u + 1] + dkk[CHUNK:]
            dvs[u], dvs[u + 1] = dvs[u] + dvv[:CHUNK], dvs[u + 1] + dvv[CHUNK:]
            for pp in range(2):
                h0 = slice(2 * CHUNK * pp, 2 * CHUNK * pp + CHUNK)
                h1 = slice(2 * CHUNK * pp + CHUNK, 2 * CHUNK * (pp + 1))
                dq_ref[rs, LANES * pp:LANES * (pp + 1)] = _rope(
                    jnp.where(lo, dq4[h0], dq4[h1]) * _SCALE, cq, -sq).astype(BF16)
        fold = lambda v: v + pltpu.roll(v, HEAD_DIM, 1)
        dkp_ref[...] = fold(_rope(dks[0], cpv, -spv))
        dvp_ref[...] = fold(dvs[0])
        for u in range(_QB):
            rs = slice(CHUNK * u, CHUNK * (u + 1))
            dkc_ref[rs, :] = fold(_rope(dks[u + 1], cq_all[rs], -sq_all[rs]))
            dvc_ref[rs, :] = fold(dvs[u + 1])
        dsk_ref[...] += dsk_row

    sp = _swa_specs(lambda g, bi, i: (bi, g, i))
    kv_shape = jax.ShapeDtypeStruct((b, s, 512), F32)
    kvp_shape = jax.ShapeDtypeStruct((b, s // _QB, 512), F32)
    return pl.pallas_call(
        body, name=name, grid=(N_GROUPS, b, s // _QROWS),
        in_specs=[sp["q"], sp["z"], sp["kc"], sp["kp"], sp["vc"], sp["vp"],
                  sp["tcur"], sp["tcur"], sp["tprev"], sp["tprev"], sp["sk"], sp["blk"], sp["blk"], sp["blk"]],
        out_specs=[sp["blk"], sp["blk"], sp["kcur"], sp["kstep"], sp["kcur"], sp["kstep"], sp["sk"]],
        out_shape=[jax.ShapeDtypeStruct((b, s, D_MODEL), BF16), jax.ShapeDtypeStruct((b, s, D_MODEL), BF16),
                   kv_shape, kvp_shape, kv_shape, kvp_shape, jax.ShapeDtypeStruct((N_GROUPS, 1, LANES), F32)],
        compiler_params=_cp(("arbitrary", "arbitrary", "arbitrary")),
    )(proj3, proj3, proj3, proj3, proj3, proj3, cos, sin, cos, sin, sinks, o3, lse3, dy3)


def _swa_fold(dkc, dkp, dvc, dvp, *, name):
    b, s, _ = dkc.shape
    ns = s // _QROWS

    def body(kc_ref, kp_ref, vc_ref, vp_ref, dk_ref, dv_ref):
        has_next = pl.program_id(1) < ns - 1
        lo = lax.broadcasted_iota(jnp.int32, (_QROWS, LANES), 1) < HEAD_DIM
        row = lax.broadcasted_iota(jnp.int32, (_QROWS, 512), 0)
        last_block = jnp.logical_and(row >= _QROWS - CHUNK, has_next)
        for cur, nxt, out in ((kc_ref, kp_ref, dk_ref), (vc_ref, vp_ref, dv_ref)):
            tot = cur[...] + jnp.where(last_block, jnp.tile(nxt[...], (_QB, 1)), 0.0)
            for j in range(2):
                out[:, LANES * j:LANES * (j + 1)] = jnp.where(
                    lo, tot[:, 256 * j:256 * j + LANES], tot[:, 256 * j + LANES:256 * (j + 1)]).astype(BF16)

    cur = pl.BlockSpec((None, _QROWS, 512), lambda bi, i: (bi, i, 0))
    nxt = pl.BlockSpec((None, CHUNK, 512), lambda bi, i: (bi, jnp.minimum(i + 1, ns - 1), 0))
    out = pl.BlockSpec((None, _QROWS, 256), lambda bi, i: (bi, i, 0))
    sh = jax.ShapeDtypeStruct((b, s, 256), BF16)
    return pl.pallas_call(
        body, name=name, grid=(b, ns), in_specs=[cur, nxt, cur, nxt], out_specs=[out, out], out_shape=[sh, sh],
        compiler_params=_cp(("parallel", "parallel")),
    )(dkc, dkp, dvc, dvp)


def _branch_fwd(ys, proj, gb, wp, wo, x, *, name, tm=256):
    t = proj.shape[0]
    g0 = _PAD_COLS["gates"][0] // D_MODEL

    def body(g_ref, a_ref, b_ref, c_ref, gb_ref, wp_ref, wo_ref, x_ref, ba_ref, bb_ref, bc_ref, m_ref, xn_ref):
        acc = None
        for i, (y, br) in enumerate(((a_ref, ba_ref), (b_ref, bb_ref), (c_ref, bc_ref))):
            bri = _dot(y[...], wp_ref[i])
            br[...] = bri
            gate = _sigmoid(g_ref[:, D_MODEL * i:D_MODEL * (i + 1)].astype(F32) + gb_ref[i:i + 1, :])
            acc = gate * bri if acc is None else acc + gate * bri
        mb = acc.astype(BF16)
        m_ref[...] = mb
        xn_ref[...] = x_ref[...] + _dot(mb, wo_ref[...])

    row = pl.BlockSpec((tm, D_MODEL), lambda i: (i, 0))
    rowf = jax.ShapeDtypeStruct((t, D_MODEL), F32)
    outs = pl.pallas_call(
        body, name=name, grid=(t // tm,),
        in_specs=[pl.BlockSpec((tm, 3 * D_MODEL), lambda i: (i, g0)), row, row, row,
                  pl.BlockSpec((3, D_MODEL), lambda i: (0, 0)),
                  pl.BlockSpec((3, D_MODEL, D_MODEL), lambda i: (0, 0, 0)),
                  pl.BlockSpec((D_MODEL, D_MODEL), lambda i: (0, 0)), row],
        out_specs=[row, row, row, row, row],
        out_shape=[rowf, rowf, rowf, jax.ShapeDtypeStruct((t, D_MODEL), BF16), rowf],
        compiler_params=_cp(("parallel",)),
    )(proj, ys[0], ys[1], ys[2], gb, wp, wo, x)
    return outs[:3], outs[3], outs[4]


def _branch_bwd(dx, proj, br, gb, wp, wo, *, name, tm=256):
    t = proj.shape[0]
    g0 = _PAD_COLS["gates"][0] // D_MODEL

    def body(g_ref, a_ref, b_ref, c_ref, gb_ref, wp_ref, wo_ref, dx_ref,
             da_ref, db_ref, dc_ref, dg_ref, dgb_ref, ya_ref, yb_ref, yc_ref):
        @pl.when(pl.program_id(0) == 0)
        def _():
            dgb_ref[...] = jnp.zeros_like(dgb_ref)

        dmv = _dot_nt(dx_ref[...].astype(BF16), wo_ref[...])
        for i, (r, dr, dy) in enumerate(((a_ref, da_ref, ya_ref), (b_ref, db_ref, yb_ref), (c_ref, dc_ref, yc_ref))):
            gate = _sigmoid(g_ref[:, D_MODEL * i:D_MODEL * (i + 1)].astype(F32) + gb_ref[i:i + 1, :])
            dbr = (dmv * gate).astype(BF16)
            dr[...] = dbr
            dg = dmv * r[...] * gate * (1.0 - gate)
            dg_ref[:, D_MODEL * i:D_MODEL * (i + 1)] = dg.astype(BF16)
            dgb_ref[i:i + 1, :] += jnp.sum(dg, axis=0, keepdims=True)
            dy[...] = _dot_nt(dbr, wp_ref[i])

    row = pl.BlockSpec((tm, D_MODEL), lambda i: (i, 0))
    rowb = jax.ShapeDtypeStruct((t, D_MODEL), BF16)
    rowf = jax.ShapeDtypeStruct((t, D_MODEL), F32)
    outs = pl.pallas_call(
        body, name=name, grid=(t // tm,),
        in_specs=[pl.BlockSpec((tm, 3 * D_MODEL), lambda i: (i, g0)), row, row, row,
                  pl.BlockSpec((3, D_MODEL), lambda i: (0, 0)),
                  pl.BlockSpec((3, D_MODEL, D_MODEL), lambda i: (0, 0, 0)),
                  pl.BlockSpec((D_MODEL, D_MODEL), lambda i: (0, 0)), row],
        out_specs=[row, row, row, pl.BlockSpec((tm, 3 * D_MODEL), lambda i: (i, 0)),
                   pl.BlockSpec((8, D_MODEL), lambda i: (0, 0)), row, row, row],
        out_shape=[rowb, rowb, rowb, jax.ShapeDtypeStruct((t, 3 * D_MODEL), BF16),
                   jax.ShapeDtypeStruct((8, D_MODEL), F32), rowf, rowf, rowf],
        compiler_params=_cp(("arbitrary",)),
    )(proj, br[0], br[1], br[2], gb, wp, wo, dx)
    return outs[:3], outs[3], outs[4], outs[5:]


def _rope_tables(s):
    pos = jnp.arange(s, dtype=F32)
    inv_freq = ROPE_THETA ** (-jnp.arange(0, HEAD_DIM, 2, dtype=F32) / HEAD_DIM)
    ang = pos[:, None] * inv_freq[None, :]
    cos, sin = jnp.cos(ang), jnp.sin(ang)
    return jnp.tile(cos, (1, 4)), jnp.tile(jnp.concatenate([-sin, sin], axis=1), (1, 2))


def _layer_params(wl):
    return dict(
        dtb=_group_lanes(wl["dt_bias"]), alog=_group_lanes(wl["a_log"]), dsk=_group_lanes(wl["d_skip"]),
        nw=wl["ssm_norm_w"].reshape(N_GROUPS, 1, 256), sinks=_group_lanes(wl["sinks"]),
        fb=jnp.pad(wl["f_bias"], (0, LANES - N_HEADS)).reshape(1, LANES))


def _layer_fwd(x, wl, tabs, bsz, li, tb):
    t = x.shape[0]
    s = t // bsz
    cos, sin = tabs
    lp = _layer_params(wl)
    n = lambda k: f"l{li}_{k}"
    h, h_t = _rms_fwd(x, wl["norm_w"], name=n("rms_fwd"))
    proj = _mm(h, wl["w_in"], tm=1024, tn=1536, tk=1024, out_dtype=BF16, name=n("mm_proj"))
    proj3 = proj.reshape(bsz, s, N_PAD)
    g0, gw = _PAD_COLS["a_dt"][0], _PAD_COLS["a_dt"][1] + _PAD_COLS["c_f"][1]
    gates3 = _mm(h, wl["w_in"][:, g0:g0 + gw], tm=1024, tn=gw, tk=1024, name=n("mm_gates")).reshape(bsz, s, gw)
    xact3 = _conv_fwd(proj3, wl["conv_w"], wl["conv_b"], name=n("conv_fwd"))
    ya3, ypre3, hst = _ssd_fwd(proj3, gates3, xact3, lp["dtb"], lp["alog"], lp["dsk"], lp["nw"], name=n("ssd_fwd"))
    yb3, ob3, lseb3 = _swa_fwd(proj3, cos, sin, lp["sinks"], name=n("swa_fwd"))
    cum = _fgate_fwd(gates3, lp["fb"], name=n("fgate_fwd"))
    cum_t = _ck_rep(cum)
    yc3, oc3, statc3 = _foxt_fwd(proj3, cum_t, name=n("fox_fwd"), tb=tb)
    ys = [v.reshape(t, D_MODEL) for v in (ya3, yb3, yc3)]
    br, merged, x_new = _branch_fwd(ys, proj, wl["gate_bias"], wl["w_proj"], wl["w_out"], x, name=n("branch_fwd"))
    saved = dict(x=x, h_t=h_t, proj=proj, gates3=gates3, xact3=xact3, ypre3=ypre3, hst=hst, ob3=ob3, lseb3=lseb3,
                 cum_t=cum_t, oc3=oc3, statc3=statc3, ys=ys, br=br, merged=merged, lp=lp)
    return x_new, saved


def _layer_bwd(dx, wl, sv, tabs, bsz, li, tb):
    t = dx.shape[0]
    s = t // bsz
    cos, sin = tabs
    lp = sv["lp"]
    n = lambda k: f"l{li}_{k}"
    proj = sv["proj"]
    proj3 = proj.reshape(bsz, s, N_PAD)
    g = {}
    g["w_out"] = _mm(sv["merged"], dx, ta=True, tm=1024, tn=1024, tk=512, name=n("mm_dwout"))
    dbr, dgates, dgb, dys = _branch_bwd(dx, proj, sv["br"], wl["gate_bias"], wl["w_proj"], wl["w_out"],
                                        name=n("branch_bwd"))
    g["gate_bias"] = dgb[:3]
    g["w_proj"] = jnp.stack([_mm(sv["ys"][i], dbr[i], ta=True, tm=1024, tn=1024, tk=512, name=n(f"mm_dwproj{i}"))
                             for i in range(3)])
    dy3 = [v.reshape(bsz, s, D_MODEL) for v in dys]

    (dact, daz, dadt, ddtb, dalog, ddsk, dnw) = _ssd_bwd(
        proj3, sv["gates3"], sv["xact3"], lp["dtb"], lp["alog"], lp["dsk"], lp["nw"], sv["ypre3"], sv["hst"], dy3[0],
        name=n("ssd_bwd"))
    g["dt_bias"], g["a_log"], g["d_skip"] = _ungroup_lanes(ddtb), _ungroup_lanes(dalog), _ungroup_lanes(ddsk)
    g["ssm_norm_w"] = dnw.reshape(D_MODEL)
    dxbc, dwb = _conv_bwd(proj3, wl["conv_w"], wl["conv_b"], dact, name=n("conv_bwd"))
    g["conv_w"], g["conv_b"] = dwb[:CONV_WIDTH], dwb[CONV_WIDTH]

    dbq, dbz, dkc, dkp, dvc, dvp, dsk = _swa_bwd(proj3, cos, sin, lp["sinks"], sv["ob3"],
                                                 sv["lseb3"], dy3[1], name=n("swa_bwd"))
    g["sinks"] = _ungroup_lanes(dsk)

    dbk, dbv = _swa_fold(dkc, dkp, dvc, dvp, name=n("swa_fold"))

    dcz, do3, stats = _foxt_prep(proj3, sv["oc3"], sv["statc3"], dy3[2], name=n("fox_prep"), tb=tb)
    dqt, dck, dcv, csum = _foxt_bwd(proj3, sv["cum_t"], do3, stats, name=n("fox_bwd"), tb=tb)
    dcq = jnp.transpose(dqt, (0, 2, 4, 1, 3)).reshape(bsz, s, D_MODEL)
    dcf, dfb = _fgate_bwd(sv["gates3"], lp["fb"], csum, name=n("fgate_bwd"))
    g["f_bias"] = dfb[0, :N_HEADS]

    parts = {"gates": dgates.reshape(bsz, s, 3 * D_MODEL), "xbc": dxbc, "a_z": daz, "b_q": dbq, "b_z": dbz,
             "c_q": dcq, "c_k": dck, "c_v": dcv, "c_z": dcz, "b_k": dbk, "b_v": dbv, "a_dt": dadt, "c_f": dcf}
    dproj = jnp.concatenate([parts[name].astype(BF16) for name, _ in _PAD_ORDER]
                            + [jnp.zeros((bsz, s, N_PAD - N_USED), BF16)], axis=2).reshape(t, N_PAD)
    dh = _mm(dproj, wl["w_in"], tb=True, tm=1024, tn=1024, tk=1536, name=n("mm_dh"))
    g["w_in"] = _unpad_w_in(_mm(sv["h_t"], dproj, tm=1024, tn=768, tk=2048, name=n("mm_dwin")))
    dx_in, dnorm = _rms_bwd(sv["x"], wl["norm_w"], dh, dx, name=n("rms_bwd"))
    g["norm_w"] = dnorm[0]
    return dx_in, g


def _local_step(x, target, wls, final_norm_w, tb=1024):
    bsz, s, d = x.shape
    t = bsz * s
    tabs = _rope_tables(s)
    xc = x.reshape(t, d)
    saved = []
    for li, wl in enumerate(wls):
        xc, sv = _layer_fwd(xc, wl, tabs, bsz, li, tb)
        saved.append(sv)
    loss, dx, dfw = _final_loss(xc, final_norm_w, target.reshape(t, d), name="final_loss")
    grads = [None] * len(wls)
    for li in reversed(range(len(wls))):
        dx, grads[li] = _layer_bwd(dx, wls[li], saved[li], tabs, bsz, li, tb)
    return loss[0, 0], dx.reshape(bsz, s, d), grads, dfw[0]


_HBM = pl.BlockSpec(memory_space=pltpu.HBM)


def _chip_peers(x, y):
    return [(1 - x, y), (x, 1 - y), (1 - x, 1 - y)]


def _gather_weights(arrs, *, name):
    n = len(arrs)

    def body(*refs):
        ins, outs = refs[:n], refs[n:2 * n]
        ici_send, ici_recv, d2d_send, d2d_recv = refs[2 * n:]
        x, y, c = lax.axis_index("x"), lax.axis_index("y"), lax.axis_index("c")
        me = 2 * x + y
        peers = _chip_peers(x, y)
        sib = (x, y, 1 - c)
        sends, fwds = [], []
        for a in range(n):
            for k, (px, py) in enumerate(peers):
                cp = pltpu.make_async_remote_copy(
                    src_ref=ins[a].at[c], dst_ref=outs[a].at[me, c], send_sem=ici_send.at[a, k],
                    recv_sem=ici_recv.at[a, k], device_id=(px, py, c), device_id_type=MESH)
                cp.start()
                sends.append(cp)
        for a in range(n):
            for k, (px, py) in enumerate(peers):
                slot = 2 * px + py
                pltpu.make_async_remote_copy(
                    src_ref=ins[a].at[c], dst_ref=outs[a].at[slot, c], send_sem=ici_send.at[a, k],
                    recv_sem=ici_recv.at[a, k], device_id=(px, py, c), device_id_type=MESH).wait_recv()
                fw = pltpu.make_async_remote_copy(
                    src_ref=outs[a].at[slot, c], dst_ref=outs[a].at[slot, c], send_sem=d2d_send.at[a, k],
                    recv_sem=d2d_recv.at[a, k], device_id=sib, device_id_type=MESH)
                fw.start()
                fwds.append(fw)
        for a in range(n):
            for k, (px, py) in enumerate(peers):
                slot = 2 * px + py
                pltpu.make_async_remote_copy(
                    src_ref=outs[a].at[slot, 1 - c], dst_ref=outs[a].at[slot, 1 - c], send_sem=d2d_send.at[a, k],
                    recv_sem=d2d_recv.at[a, k], device_id=sib, device_id_type=MESH).wait_recv()
        for cp in sends + fwds:
            cp.wait_send()

    out_shape = [jax.ShapeDtypeStruct((N_CHIPS,) + a.shape, a.dtype) for a in arrs]
    return pl.pallas_call(
        body, name=name, out_shape=out_shape, in_specs=[_HBM] * n, out_specs=[_HBM] * n,
        scratch_shapes=[pltpu.SemaphoreType.DMA((n, 3)), pltpu.SemaphoreType.DMA((n, 3)),
                        pltpu.SemaphoreType.DMA((n, 3)), pltpu.SemaphoreType.DMA((n, 3))],
    )(*arrs)


def _pair_exchange(arrs, *, name):
    n = len(arrs)

    def body(*refs):
        ins, outs = refs[:n], refs[n:2 * n]
        send, recv = refs[2 * n:]
        x, y, c = lax.axis_index("x"), lax.axis_index("y"), lax.axis_index("c")
        sib = (x, y, 1 - c)
        cps = []
        for a in range(n):
            for k in range(N_CHIPS):
                cp = pltpu.make_async_remote_copy(
                    src_ref=ins[a].at[k, 1 - c], dst_ref=outs[a].at[k], send_sem=send.at[a, k],
                    recv_sem=recv.at[a, k], device_id=sib, device_id_type=MESH)
                cp.start()
                cps.append(cp)
        for cp in cps:
            cp.wait()

    out_shape = [jax.ShapeDtypeStruct((N_CHIPS,) + a.shape[2:], a.dtype) for a in arrs]
    return pl.pallas_call(
        body, name=name, out_shape=out_shape, in_specs=[_HBM] * n, out_specs=[_HBM] * n,
        scratch_shapes=[pltpu.SemaphoreType.DMA((n, N_CHIPS)), pltpu.SemaphoreType.DMA((n, N_CHIPS))],
    )(*arrs)


def _chip_exchange(arrs, *, name):
    n = len(arrs)

    def body(*refs):
        ins, outs = refs[:n], refs[n:2 * n]
        send, recv = refs[2 * n:]
        x, y, c = lax.axis_index("x"), lax.axis_index("y"), lax.axis_index("c")
        me = 2 * x + y
        peers = _chip_peers(x, y)
        cps = []
        for a in range(n):
            for k, (px, py) in enumerate(peers):
                cp = pltpu.make_async_remote_copy(
                    src_ref=ins[a].at[2 * px + py], dst_ref=outs[a].at[me], send_sem=send.at[a, k],
                    recv_sem=recv.at[a, k], device_id=(px, py, c), device_id_type=MESH)
                cp.start()
                cps.append(cp)
        for a in range(n):
            for k, (px, py) in enumerate(peers):
                pltpu.make_async_remote_copy(
                    src_ref=ins[a].at[2 * px + py], dst_ref=outs[a].at[2 * px + py], send_sem=send.at[a, k],
                    recv_sem=recv.at[a, k], device_id=(px, py, c), device_id_type=MESH).wait_recv()
        for cp in cps:
            cp.wait_send()

    out_shape = [jax.ShapeDtypeStruct(a.shape, a.dtype) for a in arrs]
    return pl.pallas_call(
        body, name=name, out_shape=out_shape, in_specs=[_HBM] * n, out_specs=[_HBM] * n,
        scratch_shapes=[pltpu.SemaphoreType.DMA((n, 3)), pltpu.SemaphoreType.DMA((n, 3))],
    )(*arrs)


def _pair_share(arrs, *, name):
    n = len(arrs)

    def body(*refs):
        ins, outs = refs[:n], refs[n:2 * n]
        send, recv = refs[2 * n:]
        x, y, c = lax.axis_index("x"), lax.axis_index("y"), lax.axis_index("c")
        sib = (x, y, 1 - c)
        cps = []
        for a in range(n):
            cp = pltpu.make_async_remote_copy(
                src_ref=ins[a], dst_ref=outs[a], send_sem=send.at[a], recv_sem=recv.at[a],
                device_id=sib, device_id_type=MESH)
            cp.start()
            cps.append(cp)
        for cp in cps:
            cp.wait()

    out_shape = [jax.ShapeDtypeStruct(a.shape, a.dtype) for a in arrs]
    return pl.pallas_call(
        body, name=name, out_shape=out_shape, in_specs=[_HBM] * n, out_specs=[_HBM] * n,
        scratch_shapes=[pltpu.SemaphoreType.DMA((n,)), pltpu.SemaphoreType.DMA((n,))],
    )(*arrs)


def _allreduce_small(buf, *, name):
    r = buf.shape[0]

    def body(in_ref, out_ref, land, send, recv):
        x, y, c = lax.axis_index("x"), lax.axis_index("y"), lax.axis_index("c")
        me = 4 * x + 2 * y + c
        land[me] = in_ref[...]
        cps = []
        for k in range(1, N_DEV):
            px, py, pc = x ^ ((k >> 2) & 1), y ^ ((k >> 1) & 1), c ^ (k & 1)
            cp = pltpu.make_async_remote_copy(
                src_ref=in_ref, dst_ref=land.at[me], send_sem=send.at[k - 1], recv_sem=recv.at[k - 1],
                device_id=(px, py, pc), device_id_type=MESH)
            cp.start()
            cps.append(cp)
        for k in range(1, N_DEV):
            px, py, pc = x ^ ((k >> 2) & 1), y ^ ((k >> 1) & 1), c ^ (k & 1)
            pltpu.make_async_remote_copy(
                src_ref=in_ref, dst_ref=land.at[4 * px + 2 * py + pc], send_sem=send.at[k - 1],
                recv_sem=recv.at[k - 1], device_id=(px, py, pc), device_id_type=MESH).wait_recv()
        for cp in cps:
            cp.wait_send()
        acc = land[0]
        for k in range(1, N_DEV):
            acc = acc + land[k]
        out_ref[...] = acc

    vm = pl.BlockSpec(memory_space=pltpu.VMEM)
    return pl.pallas_call(
        body, name=name, out_shape=jax.ShapeDtypeStruct((r, LANES), F32), in_specs=[vm], out_specs=vm,
        scratch_shapes=[pltpu.VMEM((N_DEV, r, LANES), F32), pltpu.SemaphoreType.DMA((N_DEV - 1,)),
                        pltpu.SemaphoreType.DMA((N_DEV - 1,))],
    )(buf)


def _row_tile(rows, cols, n_arrays, budget=20 * 1024 * 1024):
    best = 8 if rows % 8 == 0 else rows
    tr = 8
    while tr <= rows:
        if rows % tr == 0 and tr * cols * 4 * n_arrays * 2 <= budget:
            best = tr
        tr *= 2
    return best


def _add_slot_layer(full, other, *, name):
    _, _, r, cdim = full.shape
    tr = _row_tile(r, cdim, 4)

    def body(c_ref, a_ref, b_ref, o_ref, ob_ref):
        sm = a_ref[...] + b_ref[...]
        o_ref[...] = sm
        ob_ref[...] = sm.astype(BF16)

    c = lax.axis_index("c").astype(jnp.int32).reshape(1)
    blk = pl.BlockSpec((None, tr, cdim), lambda k, i, c_ref: (k, i, 0))
    return pl.pallas_call(
        body, name=name,
        grid_spec=pltpu.PrefetchScalarGridSpec(
            num_scalar_prefetch=1, grid=(N_CHIPS, r // tr),
            in_specs=[pl.BlockSpec((None, None, tr, cdim), lambda k, i, c_ref: (k, c_ref[0], i, 0)), blk],
            out_specs=[blk, blk]),
        out_shape=[jax.ShapeDtypeStruct((N_CHIPS, r, cdim), F32), jax.ShapeDtypeStruct((N_CHIPS, r, cdim), BF16)],
        compiler_params=_cp(("parallel", "parallel")),
    )(c, full, other)


def _sum_slots(parts, pair, *, name):
    _, r, cdim = parts.shape
    tr = _row_tile(r, cdim, 5)

    def body(me_ref, p_ref, own_ref, o_ref):
        me = me_ref[0]
        acc = None
        for k in range(N_CHIPS):
            term = jnp.where(me == k, own_ref[...], p_ref[k].astype(F32))
            acc = term if acc is None else acc + term
        o_ref[...] = acc

    me = (2 * lax.axis_index("x") + lax.axis_index("y")).astype(jnp.int32).reshape(1)
    return pl.pallas_call(
        body, name=name,
        grid_spec=pltpu.PrefetchScalarGridSpec(
            num_scalar_prefetch=1, grid=(r // tr,),
            in_specs=[pl.BlockSpec((N_CHIPS, tr, cdim), lambda i, me_ref: (0, i, 0)),
                      pl.BlockSpec((None, tr, cdim), lambda i, me_ref: (me_ref[0], i, 0))],
            out_specs=pl.BlockSpec((tr, cdim), lambda i, me_ref: (i, 0))),
        out_shape=jax.ShapeDtypeStruct((r, cdim), F32),
        compiler_params=_cp(("parallel",)),
    )(me, parts, pair)


def _adamw(w, g, m, v, *, name):
    lead, (r, cdim) = w.shape[:-2], w.shape[-2:]
    nl = len(lead)
    tr = _row_tile(r, cdim, 7)
    tc = cdim
    if tr < 64 < r and cdim % LANES == 0:
        tr, tc = r, LANES
    c1 = 1.0 - ADAM_B1 ** ADAM_STEP
    c2 = 1.0 - ADAM_B2 ** ADAM_STEP

    def body(w_ref, g_ref, m_ref, v_ref, d_ref, nm_ref, nv_ref):
        gv = g_ref[...]
        mn = ADAM_B1 * m_ref[...] + (1.0 - ADAM_B1) * gv
        vn = ADAM_B2 * v_ref[...] + (1.0 - ADAM_B2) * (gv * gv)
        nm_ref[...] = mn
        nv_ref[...] = vn
        d_ref[...] = -ADAM_LR * ((mn / c1) / (jnp.sqrt(vn / c2) + ADAM_EPS) + ADAM_WD * w_ref[...])

    blk = pl.BlockSpec((None,) * nl + (tr, tc), lambda *ids: ids[:nl] + (ids[nl], ids[nl + 1]))
    sh = jax.ShapeDtypeStruct(w.shape, F32)
    return pl.pallas_call(
        body, name=name, grid=lead + (r // tr, cdim // tc), in_specs=[blk] * 4, out_specs=[blk] * 3,
        out_shape=[sh] * 3, compiler_params=_cp(("parallel",) * (nl + 2)),
    )(w, g, m, v)


_SMALL = ("norm_w", "conv_b", "dt_bias", "a_log", "d_skip", "ssm_norm_w", "sinks", "f_bias", "final_norm_w",
          "conv_w", "gate_bias")


def _pack(vals):
    flat = jnp.concatenate([v.reshape(-1) for v in vals])
    rows = -(-flat.shape[0] // LANES)
    rows = -(-rows // 8) * 8
    return jnp.pad(flat, (0, rows * LANES - flat.shape[0])).reshape(rows, LANES)


def _unpack(buf, shapes):
    flat = buf.reshape(-1)
    out, off = [], 0
    for sh in shapes:
        sz = int(np.prod(sh))
        out.append(flat[off:off + sz].reshape(sh))
        off += sz
    return out


def kernel(x, norm_w, w_in, conv_w, conv_b, dt_bias, a_log, d_skip, ssm_norm_w, sinks, f_bias, gate_bias, w_proj, w_out, final_norm_w, loss_target, m_norm_w, m_w_in, m_conv_w, m_conv_b, m_dt_bias, m_a_log, m_d_skip, m_ssm_norm_w, m_sinks, m_f_bias, m_gate_bias, m_w_proj, m_w_out, m_final_norm_w, v_norm_w, v_w_in, v_conv_w, v_conv_b, v_dt_bias, v_a_log, v_d_skip, v_ssm_norm_w, v_sinks, v_f_bias, v_gate_bias, v_w_proj, v_w_out, v_final_norm_w):
    depth = w_in.shape[0]
    chip = 2 * lax.axis_index("x") + lax.axis_index("y")

    own = [w_in.astype(BF16), w_proj.astype(BF16), w_out.astype(BF16), conv_w, gate_bias]
    gathered = _gather_weights(own, name="gather_weights")

    def whole(a, li, axis):
        return jnp.concatenate([jnp.where(chip == k, own[a][li], gathered[a][k, li]) for k in range(N_CHIPS)],
                               axis=axis)

    wls = []
    for li in range(depth):
        wls.append(dict(
            norm_w=norm_w[li], w_in=_pad_w_in(whole(0, li, 1)),
            conv_w=whole(3, li, 1), conv_b=conv_b[li], dt_bias=dt_bias[li], a_log=a_log[li], d_skip=d_skip[li],
            ssm_norm_w=ssm_norm_w[li], sinks=sinks[li], f_bias=f_bias[li], gate_bias=whole(4, li, 1),
            w_proj=whole(1, li, 1),
            w_out=whole(2, li, 0)))

    loss_part, grad_x, grads, d_final = _local_step(x, loss_target, wls, final_norm_w)
    loss = lax.psum(loss_part, ("x", "y", "c"))

    c_in = w_in.shape[2]
    r_proj = w_proj.shape[2]
    r_out = w_out.shape[1]
    full_in = jnp.stack([jnp.stack([grads[li]["w_in"][:, k * c_in:(k + 1) * c_in] for li in range(depth)])
                         for k in range(N_CHIPS)])
    full_proj = jnp.stack([jnp.stack([grads[li]["w_proj"][:, k * r_proj:(k + 1) * r_proj].reshape(-1, D_MODEL)
                                      for li in range(depth)]) for k in range(N_CHIPS)])
    full_out = jnp.stack([jnp.stack([grads[li]["w_out"][k * r_out:(k + 1) * r_out] for li in range(depth)])
                          for k in range(N_CHIPS)])
    fulls = [full_in, full_proj, full_out]
    others = _pair_exchange(fulls, name="grad_pair_exchange")
    pair = [_add_slot_layer(f, o, name=f"grad_pair_add{i}") for i, (f, o) in enumerate(zip(fulls, others))]
    parts = _chip_exchange([p[1] for p in pair], name="grad_chip_exchange")
    mine = [_sum_slots(p, pr[0], name=f"grad_slot_sum{i}") for i, (p, pr) in enumerate(zip(parts, pair))]
    theirs = _pair_share(mine, name="grad_pair_share")
    core = lax.axis_index("c")
    red_in, red_proj, red_out = [jnp.stack([jnp.where(core == li, m, t) for li in range(depth)])
                                 for m, t in zip(mine, theirs)]
    grad_w_in = red_in
    grad_w_proj = red_proj.reshape(w_proj.shape)
    grad_w_out = red_out

    small_full = {
        "norm_w": jnp.stack([g["norm_w"] for g in grads]), "conv_b": jnp.stack([g["conv_b"] for g in grads]),
        "dt_bias": jnp.stack([g["dt_bias"] for g in grads]), "a_log": jnp.stack([g["a_log"] for g in grads]),
        "d_skip": jnp.stack([g["d_skip"] for g in grads]),
        "ssm_norm_w": jnp.stack([g["ssm_norm_w"] for g in grads]),
        "sinks": jnp.stack([g["sinks"] for g in grads]), "f_bias": jnp.stack([g["f_bias"] for g in grads]),
        "final_norm_w": d_final,
        "conv_w": jnp.stack([g["conv_w"] for g in grads]), "gate_bias": jnp.stack([g["gate_bias"] for g in grads])}
    shapes = [small_full[k].shape for k in _SMALL]
    summed = _unpack(_allreduce_small(_pack([small_full[k] for k in _SMALL]), name="allreduce_small"), shapes)
    gsmall = dict(zip(_SMALL, summed))
    gsmall["conv_w"] = lax.dynamic_slice_in_dim(gsmall["conv_w"], chip * conv_w.shape[2], conv_w.shape[2], axis=2)
    gsmall["gate_bias"] = lax.dynamic_slice_in_dim(gsmall["gate_bias"], chip * gate_bias.shape[2],
                                                   gate_bias.shape[2], axis=2)

    w_small = dict(norm_w=norm_w, conv_b=conv_b, dt_bias=dt_bias, a_log=a_log, d_skip=d_skip,
                   ssm_norm_w=ssm_norm_w, sinks=sinks, f_bias=f_bias, final_norm_w=final_norm_w, conv_w=conv_w,
                   gate_bias=gate_bias)
    m_small = dict(norm_w=m_norm_w, conv_b=m_conv_b, dt_bias=m_dt_bias, a_log=m_a_log, d_skip=m_d_skip,
                   ssm_norm_w=m_ssm_norm_w, sinks=m_sinks, f_bias=m_f_bias, final_norm_w=m_final_norm_w,
                   conv_w=m_conv_w, gate_bias=m_gate_bias)
    v_small = dict(norm_w=v_norm_w, conv_b=v_conv_b, dt_bias=v_dt_bias, a_log=v_a_log, d_skip=v_d_skip,
                   ssm_norm_w=v_ssm_norm_w, sinks=v_sinks, f_bias=v_f_bias, final_norm_w=v_final_norm_w,
                   conv_w=v_conv_w, gate_bias=v_gate_bias)
    sshapes = [w_small[k].shape for k in _SMALL]
    ds, ms, vs = _adamw(_pack([w_small[k] for k in _SMALL]), _pack([gsmall[k] for k in _SMALL]),
                        _pack([m_small[k] for k in _SMALL]), _pack([v_small[k] for k in _SMALL]), name="adamw_small")
    delta = dict(zip(_SMALL, _unpack(ds, sshapes)))
    new_m = dict(zip(_SMALL, _unpack(ms, sshapes)))
    new_v = dict(zip(_SMALL, _unpack(vs, sshapes)))
    grad = dict(gsmall)
    for nm, w, g, m, v in (("w_proj", w_proj, grad_w_proj, m_w_proj, v_w_proj),
                           ("w_out", w_out, grad_w_out, m_w_out, v_w_out)):
        grad[nm] = g
        delta[nm], new_m[nm], new_v[nm] = _adamw(w, g, m, v, name=f"adamw_{nm}")
    tview = lambda a: jnp.transpose(a, (0, 2, 1))
    grad["w_in"] = grad_w_in
    delta["w_in"], new_m["w_in"], new_v["w_in"] = [
        tview(a) for a in _adamw(tview(w_in), tview(grad_w_in), tview(m_w_in), tview(v_w_in), name="adamw_w_in")]

    order = ("norm_w", "w_in", "conv_w", "conv_b", "dt_bias", "a_log", "d_skip", "ssm_norm_w", "sinks", "f_bias",
             "gate_bias", "w_proj", "w_out", "final_norm_w")
    return (loss, grad_x, *[grad[k] for k in order], *[delta[k] for k in order],
            *[new_m[k] for k in order], *[new_v[k] for k in order])
```

```python
import numpy as np
import jax
import jax.numpy as jnp
from jax import lax
from jax.experimental import pallas as pl
from jax.experimental.pallas import tpu as pltpu

F32 = jnp.float32
BF16 = jnp.bfloat16
HIGHEST = lax.Precision.HIGHEST
MESH = pl.DeviceIdType.MESH

D_MODEL = 1024
HEAD_DIM = 64
N_HEADS = 16
N_GROUPS = 4
SSM_STATE = 128
CHUNK = 128
CONV_WIDTH = 4
CONV_DIM = 2048
ROPE_THETA = 10000.0
NORM_EPS = 1e-6
LANES = 128
N_CHIPS = 4
N_DEV = 8

ADAM_LR = 0.001
ADAM_B1 = 0.9
ADAM_B2 = 0.999
ADAM_EPS = 1e-08
ADAM_WD = 0.01
ADAM_STEP = 10

_REF_COLS = {}
_off = 0
for _n, _s in (("xbc", 2048), ("a_z", 1024), ("a_dt", 16), ("b_q", 1024), ("b_k", 256), ("b_v", 256),
               ("b_z", 1024), ("c_q", 1024), ("c_k", 1024), ("c_v", 1024), ("c_f", 16), ("c_z", 1024),
               ("gates", 3072)):
    _REF_COLS[_n] = (_off, _s)
    _off += _s

_PAD_ORDER = (("gates", 3072), ("xbc", 2048), ("a_z", 1024), ("b_q", 1024), ("b_z", 1024), ("c_q", 1024),
              ("c_k", 1024), ("c_v", 1024), ("c_z", 1024), ("b_k", 256), ("b_v", 256), ("a_dt", 512),
              ("c_f", 128))
_PAD_COLS = {}
_off = 0
for _n, _s in _PAD_ORDER:
    _PAD_COLS[_n] = (_off, _s)
    _off += _s
N_USED = _off
N_PAD = 13824


def _cp(sem, vmem_mb=48):
    return pltpu.CompilerParams(dimension_semantics=sem, vmem_limit_bytes=vmem_mb * 1024 * 1024)


def _dot(a, b, dims=((1,), (0,)), precision=None):
    return lax.dot_general(a, b, (dims, ((), ())), preferred_element_type=F32, precision=precision)


def _dot_nt(a, b):
    return _dot(a, b, ((1,), (1,)))


def _dot_tn(a, b):
    return _dot(a, b, ((0,), (0,)))


def _col(v, idx):
    lane = lax.broadcasted_iota(jnp.int32, v.shape, 1)
    return jnp.sum(jnp.where(lane == idx, v, 0.0), axis=1, keepdims=True)


def _row(v, idx):
    row = lax.broadcasted_iota(jnp.int32, v.shape, 0)
    return jnp.sum(jnp.where(row == idx, v, 0.0), axis=0, keepdims=True)


def _iota_col():
    return lax.broadcasted_iota(jnp.int32, (CHUNK, 1), 0)


def _iota_row():
    return lax.broadcasted_iota(jnp.int32, (1, LANES), 1)


def _sigmoid(x):
    return 1.0 / (1.0 + jnp.exp(-x))


def _softplus(x):
    return jnp.maximum(x, 0.0) + jnp.log(1.0 + jnp.exp(-jnp.abs(x)))


def _pad_w_in(w):
    parts = []
    for name, size in _PAD_ORDER:
        s0, sz = _REF_COLS[name]
        seg = w[:, s0:s0 + sz]
        if name == "a_dt":
            seg = jnp.pad(seg.reshape(-1, N_GROUPS, 4), ((0, 0), (0, 0), (0, LANES - 4))).reshape(-1, 512)
        elif name == "c_f":
            seg = jnp.pad(seg, ((0, 0), (0, LANES - 16)))
        parts.append(seg)
    parts.append(jnp.zeros((w.shape[0], N_PAD - N_USED), w.dtype))
    return jnp.concatenate(parts, axis=1)


def _unpad_w_in(wp):
    segs = {}
    for name, _ in _PAD_ORDER:
        p0, psz = _PAD_COLS[name]
        seg = wp[:, p0:p0 + psz]
        if name == "a_dt":
            seg = seg.reshape(-1, N_GROUPS, LANES)[:, :, :4].reshape(-1, 16)
        elif name == "c_f":
            seg = seg[:, :16]
        segs[name] = seg
    order = sorted(_REF_COLS, key=lambda n: _REF_COLS[n][0])
    return jnp.concatenate([segs[n] for n in order], axis=1)


def _group_lanes(v):
    return jnp.pad(v.reshape(N_GROUPS, 1, 4), ((0, 0), (0, 0), (0, LANES - 4)))


def _ungroup_lanes(v):
    return v[:, 0, :4].reshape(16)


def _mm(a, b, *, ta=False, tb=False, tm=512, tn=512, tk=512, out_dtype=F32, name):
    if ta:
        kdim, m = a.shape
    else:
        m, kdim = a.shape
    if tb:
        n, k2 = b.shape
    else:
        k2, n = b.shape
    assert kdim == k2, (a.shape, b.shape)
    tm, tn, tk = min(tm, m), min(tn, n), min(tk, kdim)
    assert m % tm == 0 and n % tn == 0 and kdim % tk == 0, (m, n, kdim, tm, tn, tk)
    nk = kdim // tk
    a_spec = (pl.BlockSpec((tk, tm), lambda i, j, k: (k, i)) if ta
              else pl.BlockSpec((tm, tk), lambda i, j, k: (i, k)))
    b_spec = (pl.BlockSpec((tn, tk), lambda i, j, k: (j, k)) if tb
              else pl.BlockSpec((tk, tn), lambda i, j, k: (k, j)))
    dims = ((0 if ta else 1,), (1 if tb else 0,))

    def body(a_ref, b_ref, o_ref, acc_ref):
        k = pl.program_id(2)
        p = _dot(a_ref[...].astype(BF16), b_ref[...].astype(BF16), dims)

        @pl.when(k == 0)
        def _():
            acc_ref[...] = p

        @pl.when(k > 0)
        def _():
            acc_ref[...] += p

        @pl.when(k == nk - 1)
        def _():
            o_ref[...] = acc_ref[...].astype(out_dtype)

    return pl.pallas_call(
        body, name=name, grid=(m // tm, n // tn, nk),
        in_specs=[a_spec, b_spec], out_specs=pl.BlockSpec((tm, tn), lambda i, j, k: (i, j)),
        out_shape=jax.ShapeDtypeStruct((m, n), out_dtype),
        scratch_shapes=[pltpu.VMEM((tm, tn), F32)],
        compiler_params=_cp(("parallel", "parallel", "arbitrary")),
    )(a, b)


def _rms_fwd(x, w, *, name, tm=512):
    t, d = x.shape

    def body(x_ref, w_ref, o_ref, ot_ref):
        xv = x_ref[...]
        r = lax.rsqrt(jnp.mean(xv * xv, axis=1, keepdims=True) + NORM_EPS)
        h = xv * r * w_ref[...]
        o_ref[...] = h.astype(BF16)
        ot_ref[...] = h.T.astype(BF16)

    return pl.pallas_call(
        body, name=name, grid=(t // tm,),
        in_specs=[pl.BlockSpec((tm, d), lambda i: (i, 0)), pl.BlockSpec((1, d), lambda i: (0, 0))],
        out_specs=[pl.BlockSpec((tm, d), lambda i: (i, 0)), pl.BlockSpec((d, tm), lambda i: (0, i))],
        out_shape=[jax.ShapeDtypeStruct((t, d), BF16), jax.ShapeDtypeStruct((d, t), BF16)],
        compiler_params=_cp(("parallel",)),
    )(x, w.reshape(1, d))


def _rms_bwd(x, w, dh, dres, *, name, tm=512):
    t, d = x.shape

    def body(x_ref, w_ref, dh_ref, dres_ref, dx_ref, dw_ref):
        xv = x_ref[...]
        r = lax.rsqrt(jnp.mean(xv * xv, axis=1, keepdims=True) + NORM_EPS)
        xhat = xv * r
        dhv = dh_ref[...]
        dxhat = dhv * w_ref[...]
        dx = r * (dxhat - xhat * jnp.mean(dxhat * xhat, axis=1, keepdims=True))
        dx_ref[...] = dres_ref[...] + dx

        @pl.when(pl.program_id(0) == 0)
        def _():
            dw_ref[...] = jnp.zeros_like(dw_ref)

        dw_ref[...] += jnp.sum(dhv * xhat, axis=0, keepdims=True)

    return pl.pallas_call(
        body, name=name, grid=(t // tm,),
        in_specs=[pl.BlockSpec((tm, d), lambda i: (i, 0)), pl.BlockSpec((1, d), lambda i: (0, 0)),
                  pl.BlockSpec((tm, d), lambda i: (i, 0)), pl.BlockSpec((tm, d), lambda i: (i, 0))],
        out_specs=[pl.BlockSpec((tm, d), lambda i: (i, 0)), pl.BlockSpec((1, d), lambda i: (0, 0))],
        out_shape=[jax.ShapeDtypeStruct((t, d), F32), jax.ShapeDtypeStruct((1, d), F32)],
        compiler_params=_cp(("arbitrary",)),
    )(x, w.reshape(1, d), dh, dres)


def _final_loss(x, w, target, *, name, tm=512):
    t, d = x.shape

    def body(x_ref, w_ref, t_ref, loss_ref, dx_ref, dw_ref):
        xv = x_ref[...]
        wv = w_ref[...]
        r = lax.rsqrt(jnp.mean(xv * xv, axis=1, keepdims=True) + NORM_EPS)
        xhat = xv * r
        err = xhat * wv - t_ref[...]
        dy = err * (1.0 / d)
        dxhat = dy * wv
        dx_ref[...] = r * (dxhat - xhat * jnp.mean(dxhat * xhat, axis=1, keepdims=True))

        @pl.when(pl.program_id(0) == 0)
        def _():
            dw_ref[...] = jnp.zeros_like(dw_ref)
            loss_ref[...] = jnp.zeros_like(loss_ref)

        dw_ref[...] += jnp.sum(dy * xhat, axis=0, keepdims=True)
        part = 0.5 * jnp.sum(jnp.mean(err * err, axis=1, keepdims=True), axis=0, keepdims=True)
        loss_ref[...] += jnp.broadcast_to(part, loss_ref.shape)

    return pl.pallas_call(
        body, name=name, grid=(t // tm,),
        in_specs=[pl.BlockSpec((tm, d), lambda i: (i, 0)), pl.BlockSpec((1, d), lambda i: (0, 0)),
                  pl.BlockSpec((tm, d), lambda i: (i, 0))],
        out_specs=[pl.BlockSpec((8, LANES), lambda i: (0, 0)), pl.BlockSpec((tm, d), lambda i: (i, 0)),
                   pl.BlockSpec((1, d), lambda i: (0, 0))],
        out_shape=[jax.ShapeDtypeStruct((8, LANES), F32), jax.ShapeDtypeStruct((t, d), F32),
                   jax.ShapeDtypeStruct((1, d), F32)],
        compiler_params=_cp(("arbitrary",)),
    )(x, w.reshape(1, d), target)


_CB = 128


def _conv_pre(u, w_ref, b_ref):
    s = u.shape[0]
    row = lax.broadcasted_iota(jnp.int32, u.shape, 0)
    pre = b_ref[...] + w_ref[CONV_WIDTH - 1:CONV_WIDTH, :] * u
    for sh in range(1, CONV_WIDTH):
        shifted = jnp.where(row >= sh, pltpu.roll(u, sh, 0), 0.0)
        pre = pre + w_ref[CONV_WIDTH - 1 - sh:CONV_WIDTH - sh, :] * shifted
    return pre


def _conv_fwd(proj3, cw, cb, *, name):
    b, s, _ = proj3.shape
    c0 = _PAD_COLS["xbc"][0] // _CB

    def body(u_ref, w_ref, b_ref, o_ref):
        pre = _conv_pre(u_ref[...].astype(F32), w_ref, b_ref)
        o_ref[...] = pre * _sigmoid(pre)

    return pl.pallas_call(
        body, name=name, grid=(b, CONV_DIM // _CB),
        in_specs=[pl.BlockSpec((None, s, _CB), lambda i, j: (i, 0, c0 + j)),
                  pl.BlockSpec((CONV_WIDTH, _CB), lambda i, j: (0, j)),
                  pl.BlockSpec((1, _CB), lambda i, j: (0, j))],
        out_specs=pl.BlockSpec((None, s, _CB), lambda i, j: (i, 0, j)),
        out_shape=jax.ShapeDtypeStruct((b, s, CONV_DIM), F32),
        compiler_params=_cp(("parallel", "parallel")),
    )(proj3, cw, cb.reshape(1, CONV_DIM))


def _conv_bwd(proj3, cw, cb, dact, *, name):
    b, s, _ = proj3.shape
    c0 = _PAD_COLS["xbc"][0] // _CB

    def body(u_ref, w_ref, b_ref, da_ref, du_ref, dwb_ref):
        u = u_ref[...].astype(F32)
        pre = _conv_pre(u, w_ref, b_ref)
        sg = _sigmoid(pre)
        dpre = da_ref[...] * (sg * (1.0 + pre * (1.0 - sg)))
        row = lax.broadcasted_iota(jnp.int32, u.shape, 0)
        du = w_ref[CONV_WIDTH - 1:CONV_WIDTH, :] * dpre
        rows = [jnp.sum(dpre * u, axis=0, keepdims=True)]
        for sh in range(1, CONV_WIDTH):
            fwd_shift = jnp.where(row < s - sh, pltpu.roll(dpre, s - sh, 0), 0.0)
            du = du + w_ref[CONV_WIDTH - 1 - sh:CONV_WIDTH - sh, :] * fwd_shift
            ush = jnp.where(row >= sh, pltpu.roll(u, sh, 0), 0.0)
            rows.append(jnp.sum(dpre * ush, axis=0, keepdims=True))
        du_ref[...] = du.astype(BF16)

        @pl.when(pl.program_id(1) == 0)
        def _():
            dwb_ref[...] = jnp.zeros_like(dwb_ref)

        for sh in range(CONV_WIDTH):
            k = CONV_WIDTH - 1 - sh
            dwb_ref[k:k + 1, :] += rows[sh]
        dwb_ref[CONV_WIDTH:CONV_WIDTH + 1, :] += jnp.sum(dpre, axis=0, keepdims=True)

    return pl.pallas_call(
        body, name=name, grid=(CONV_DIM // _CB, b),
        in_specs=[pl.BlockSpec((None, s, _CB), lambda j, i: (i, 0, c0 + j)),
                  pl.BlockSpec((CONV_WIDTH, _CB), lambda j, i: (0, j)),
                  pl.BlockSpec((1, _CB), lambda j, i: (0, j)),
                  pl.BlockSpec((None, s, _CB), lambda j, i: (i, 0, j))],
        out_specs=[pl.BlockSpec((None, s, _CB), lambda j, i: (i, 0, j)),
                   pl.BlockSpec((8, _CB), lambda j, i: (0, j))],
        out_shape=[jax.ShapeDtypeStruct((b, s, CONV_DIM), BF16), jax.ShapeDtypeStruct((8, CONV_DIM), F32)],
        compiler_params=_cp(("parallel", "arbitrary")),
    )(proj3, cw, cb.reshape(1, CONV_DIM), dact)


def _ssd_common(dt_ref, dtb_ref, alog_ref):
    row = lax.broadcasted_iota(jnp.int32, (CHUNK, CHUNK), 0)
    lane = lax.broadcasted_iota(jnp.int32, (CHUNK, CHUNK), 1)
    causal = row >= lane
    tri = causal.astype(F32)
    dtv = _softplus(dt_ref[...] + dtb_ref[...])
    a_row = -jnp.exp(alog_ref[...])
    acum = _dot(tri, dtv * a_row, precision=HIGHEST)
    return row, lane, causal, dtv, a_row, acum, acum.T


def _ssd_pair(pp, x, dtv, acum, acum_t, causal, lane, row):
    lo = lane < HEAD_DIM
    r0, r1 = 2 * pp, 2 * pp + 1
    dtp = jnp.where(lo, _col(dtv, r0), _col(dtv, r1))
    ac0, ac1 = _col(acum, r0), _col(acum, r1)
    ar0, ar1 = _row(acum_t, r0), _row(acum_t, r1)
    d0 = jnp.where(causal, jnp.exp(jnp.where(causal, ac0 - ar0, 0.0)), 0.0)
    d1 = jnp.where(causal, jnp.exp(jnp.where(causal, ac1 - ar1, 0.0)), 0.0)
    al0, al1 = _col(ar0, CHUNK - 1), _col(ar1, CHUNK - 1)
    eac = jnp.where(lo, jnp.exp(ac0), jnp.exp(ac1))
    dsp = jnp.where(lo, jnp.exp(al0 - ac0), jnp.exp(al1 - ac1))
    eal = jnp.where(_iota_col() < HEAD_DIM, jnp.exp(al0), jnp.exp(al1))
    return lo, dtp, x * dtp, d0, d1, al0, al1, eac, dsp, eal


def _ssd_fwd(proj3, gates3, xact3, dtb, alog, dsk, nw, *, name):
    b, s, _ = proj3.shape
    nc = s // CHUNK
    dt0 = 0
    z0 = _PAD_COLS["a_z"][0] // D_MODEL

    def body(xs_ref, bm_ref, cm_ref, dt_ref, z_ref, dtb_ref, alog_ref, dsk_ref, nw_ref,
             ya_ref, ypre_ref, hst_ref, h_scr):
        @pl.when(pl.program_id(1) == 0)
        def _():
            h_scr[...] = jnp.zeros_like(h_scr)

        for g in range(N_GROUPS):
            w256 = pl.ds(256 * g, 256)
            w128 = pl.ds(LANES * g, LANES)
            group(xs_ref.at[:, w256], bm_ref.at[:, w128], cm_ref.at[:, w128], dt_ref.at[:, w128],
                  z_ref.at[:, w256], dtb_ref.at[g], alog_ref.at[g], dsk_ref.at[g], nw_ref.at[g],
                  ya_ref.at[:, w256], ypre_ref.at[:, w256], hst_ref.at[g], h_scr.at[g])

    def group(xs_ref, bm_ref, cm_ref, dt_ref, z_ref, dtb_ref, alog_ref, dsk_ref, nw_ref,
              ya_ref, ypre_ref, hst_ref, h_scr):
        row, lane, causal, dtv, a_row, acum, acum_t = _ssd_common(dt_ref, dtb_ref, alog_ref)
        bb = bm_ref[...].astype(BF16)
        cb = cm_ref[...].astype(BF16)
        cbm = _dot_nt(cb, bb)
        hst_ref[...] = h_scr[...]
        dskv = dsk_ref[...]
        for pp in range(2):
            x = xs_ref[:, LANES * pp:LANES * (pp + 1)]
            lo, dtp, xd, d0, d1, al0, al1, eac, dsp, eal = _ssd_pair(pp, x, dtv, acum, acum_t, causal, lane, row)
            xdb = xd.astype(BF16)
            y = jnp.where(lo, _dot((cbm * d0).astype(BF16), xdb), _dot((cbm * d1).astype(BF16), xdb))
            h = h_scr[pp]
            y = y + eac * _dot_nt(cb, h.astype(BF16))
            h_scr[pp] = h * eal + _dot_tn((xd * dsp).astype(BF16), bb)
            dskp = jnp.where((_iota_row() < HEAD_DIM), _col(dskv, 2 * pp), _col(dskv, 2 * pp + 1))
            ypre_ref[:, LANES * pp:LANES * (pp + 1)] = y + x * dskp
        ypre = ypre_ref[...]
        z = z_ref[...].astype(F32)
        yg = ypre * (z * _sigmoid(z))
        rstd = lax.rsqrt(jnp.sum(yg * yg, axis=1, keepdims=True) * (1.0 / 256.0) + NORM_EPS)
        ya_ref[...] = (yg * rstd * nw_ref[...]).astype(BF16)

    g = N_GROUPS
    par = pl.BlockSpec((g, 1, LANES), lambda i, c: (0, 0, 0))
    wide = pl.BlockSpec((None, CHUNK, D_MODEL), lambda i, c: (i, c, 0))
    return pl.pallas_call(
        body, name=name, grid=(b, nc),
        in_specs=[wide,
                  pl.BlockSpec((None, CHUNK, 512), lambda i, c: (i, c, 2)),
                  pl.BlockSpec((None, CHUNK, 512), lambda i, c: (i, c, 3)),
                  pl.BlockSpec((None, CHUNK, 512), lambda i, c: (i, c, dt0)),
                  pl.BlockSpec((None, CHUNK, D_MODEL), lambda i, c: (i, c, z0)),
                  par, par, par,
                  pl.BlockSpec((g, 1, 256), lambda i, c: (0, 0, 0))],
        out_specs=[wide, wide,
                   pl.BlockSpec((None, None, g, 2, CHUNK, SSM_STATE), lambda i, c: (i, c, 0, 0, 0, 0))],
        out_shape=[jax.ShapeDtypeStruct((b, s, D_MODEL), BF16), jax.ShapeDtypeStruct((b, s, D_MODEL), F32),
                   jax.ShapeDtypeStruct((b, nc, g, 2, CHUNK, SSM_STATE), F32)],
        scratch_shapes=[pltpu.VMEM((g, 2, CHUNK, SSM_STATE), F32)],
        compiler_params=_cp(("parallel", "arbitrary")),
    )(xact3, xact3, xact3, gates3, proj3, dtb, alog, dsk, nw)


def _ssd_bwd(proj3, gates3, xact3, dtb, alog, dsk, nw, ypre3, hst, dya3, *, name):
    b, s, _ = proj3.shape
    nc = s // CHUNK
    dt0 = 0
    z0 = _PAD_COLS["a_z"][0] // D_MODEL

    def body(xs_ref, bm_ref, cm_ref, dt_ref, z_ref, dtb_ref, alog_ref, dsk_ref, nw_ref, ypre_ref, hst_ref,
             dya_ref, dact_ref, dz_ref, ddt_ref, ddtb_ref, dalog_ref, ddsk_ref, dnw_ref, dh_scr):
        first = jnp.logical_and(pl.program_id(0) == 0, pl.program_id(1) == 0)

        @pl.when(first)
        def _():
            ddtb_ref[...] = jnp.zeros_like(ddtb_ref)
            dalog_ref[...] = jnp.zeros_like(dalog_ref)
            ddsk_ref[...] = jnp.zeros_like(ddsk_ref)
            dnw_ref[...] = jnp.zeros_like(dnw_ref)

        @pl.when(pl.program_id(1) == 0)
        def _():
            dh_scr[...] = jnp.zeros_like(dh_scr)

        for g in range(N_GROUPS):
            w256 = pl.ds(256 * g, 256)
            w128 = pl.ds(LANES * g, LANES)
            group(xs_ref.at[:, w256], bm_ref.at[:, w128], cm_ref.at[:, w128], dt_ref.at[:, w128],
                  z_ref.at[:, w256], dtb_ref.at[g], alog_ref.at[g], dsk_ref.at[g], nw_ref.at[g],
                  ypre_ref.at[:, w256], hst_ref.at[g], dya_ref.at[:, w256],
                  dact_ref.at[:, w256], dact_ref.at[:, pl.ds(D_MODEL + LANES * g, LANES)],
                  dact_ref.at[:, pl.ds(D_MODEL + 512 + LANES * g, LANES)], dz_ref.at[:, w256], ddt_ref.at[:, w128],
                  ddtb_ref.at[g], dalog_ref.at[g], ddsk_ref.at[g], dnw_ref.at[g], dh_scr.at[g])

    def group(xs_ref, bm_ref, cm_ref, dt_ref, z_ref, dtb_ref, alog_ref, dsk_ref, nw_ref, ypre_ref, hst_ref,
              dya_ref, dxs_ref, dbm_ref, dcm_ref, dz_ref, ddt_ref, ddtb_ref, dalog_ref, ddsk_ref, dnw_ref,
              dh_scr):
        row, lane, causal, dtv, a_row, acum, acum_t = _ssd_common(dt_ref, dtb_ref, alog_ref)
        lane1 = _iota_row()
        bb = bm_ref[...].astype(BF16)
        cb = cm_ref[...].astype(BF16)
        cbm = _dot_nt(cb, bb)

        z = z_ref[...].astype(F32)
        ypre = ypre_ref[...]
        dya = dya_ref[...]
        sz = _sigmoid(z)
        silu = z * sz
        yg = ypre * silu
        rstd = lax.rsqrt(jnp.sum(yg * yg, axis=1, keepdims=True) * (1.0 / 256.0) + NORM_EPS)
        dnw_ref[...] += jnp.sum(dya * yg * rstd, axis=0, keepdims=True)
        dn = dya * nw_ref[...]
        dyg = rstd * dn - yg * (rstd * rstd * rstd * (1.0 / 256.0)) * jnp.sum(dn * yg, axis=1, keepdims=True)
        dz_ref[...] = (dyg * ypre * (sz * (1.0 + z * (1.0 - sz)))).astype(BF16)
        dy_all = dyg * silu

        dskv = dsk_ref[...]
        da_cols = jnp.zeros((CHUNK, LANES), F32)
        dxt_cols = jnp.zeros((CHUNK, LANES), F32)
        ddsk_row = jnp.zeros((1, LANES), F32)
        dcb = jnp.zeros((CHUNK, CHUNK), F32)
        dc = jnp.zeros((CHUNK, SSM_STATE), F32)
        db = jnp.zeros((CHUNK, SSM_STATE), F32)
        last = _iota_col() == CHUNK - 1
        for pp in range(2):
            r0, r1 = 2 * pp, 2 * pp + 1
            x = xs_ref[:, LANES * pp:LANES * (pp + 1)]
            dy = dy_all[:, LANES * pp:LANES * (pp + 1)]
            lo, dtp, xd, d0, d1, al0, al1, eac, dsp, eal = _ssd_pair(pp, x, dtv, acum, acum_t, causal, lane, row)
            w0, w1 = cbm * d0, cbm * d1
            w0b, w1b = w0.astype(BF16), w1.astype(BF16)
            xdb = xd.astype(BF16)
            dyb = dy.astype(BF16)
            h = hst_ref[pp]
            dhn = dh_scr[pp]
            hb = h.astype(BF16)
            dhb = dhn.astype(BF16)
            g0 = _dot_nt(jnp.where(lo, dy, 0.0).astype(BF16), xdb)
            g1 = _dot_nt(jnp.where(lo, 0.0, dy).astype(BF16), xdb)
            dcb = dcb + g0 * d0 + g1 * d1
            m0, m1 = g0 * w0, g1 * w1
            bdh = _dot_nt(bb, dhb)
            dxd = jnp.where(lo, _dot_tn(w0b, dyb), _dot_tn(w1b, dyb)) + dsp * bdh
            ch = _dot_nt(cb, hb)
            edy = eac * dy
            edyb = edy.astype(BF16)
            xds = xd * dsp
            dc = dc + _dot(edyb, hb)
            db = db + _dot(xds.astype(BF16), dhb)
            dh_scr[pp] = dhn * eal + _dot_tn(edyb, cb)
            t2 = edy * ch
            t3 = xds * bdh
            dhh = dhn * h
            s4_0 = jnp.sum(jnp.sum(jnp.where(row < HEAD_DIM, dhh, 0.0), axis=0, keepdims=True), axis=1, keepdims=True)
            s4_1 = jnp.sum(jnp.sum(dhh, axis=0, keepdims=True), axis=1, keepdims=True) - s4_0
            t23 = t2 - t3
            t23_0 = jnp.sum(jnp.where(lo, t23, 0.0), axis=1, keepdims=True)
            t23_1 = jnp.sum(t23, axis=1, keepdims=True) - t23_0
            c3 = jnp.sum(t3, axis=0, keepdims=True)
            c3_0 = jnp.sum(jnp.where(_iota_row() < HEAD_DIM, c3, 0.0), axis=1, keepdims=True)
            c3_1 = jnp.sum(c3, axis=1, keepdims=True) - c3_0
            dal0 = c3_0 + jnp.exp(al0) * s4_0
            dal1 = c3_1 + jnp.exp(al1) * s4_1
            dac0 = jnp.sum(m0 - m0.T, axis=1, keepdims=True) + t23_0 + jnp.where(last, dal0, 0.0)
            dac1 = jnp.sum(m1 - m1.T, axis=1, keepdims=True) + t23_1 + jnp.where(last, dal1, 0.0)
            da_cols = da_cols + jnp.where(lane == r0, dac0, 0.0) + jnp.where(lane == r1, dac1, 0.0)
            xx = dxd * x
            x0 = jnp.sum(jnp.where(lo, xx, 0.0), axis=1, keepdims=True)
            x1 = jnp.sum(xx, axis=1, keepdims=True) - x0
            dxt_cols = dxt_cols + jnp.where(lane == r0, x0, 0.0) + jnp.where(lane == r1, x1, 0.0)
            dskp = jnp.where((_iota_row() < HEAD_DIM), _col(dskv, r0), _col(dskv, r1))
            dxs_ref[:, LANES * pp:LANES * (pp + 1)] = dxd * dtp + dy * dskp
            yx = jnp.sum(dy * x, axis=0, keepdims=True)
            k0 = jnp.sum(jnp.where((_iota_row() < HEAD_DIM), yx, 0.0), axis=1, keepdims=True)
            k1 = jnp.sum(yx, axis=1, keepdims=True) - k0
            ddsk_row = ddsk_row + jnp.where(lane1 == r0, k0, 0.0) + jnp.where(lane1 == r1, k1, 0.0)
        dcbb = dcb.astype(BF16)
        dcm_ref[...] = dc + _dot(dcbb, bb)
        dbm_ref[...] = db + _dot_tn(dcbb, cb)
        tri_t = (row <= lane).astype(F32)
        dadt = _dot(tri_t, da_cols, precision=HIGHEST)
        ddtv = dadt * a_row + dxt_cols
        dalog_ref[...] += jnp.sum(dadt * dtv, axis=0, keepdims=True) * a_row
        ddt_raw = ddtv * _sigmoid(dt_ref[...] + dtb_ref[...])
        ddt_ref[...] = ddt_raw.astype(BF16)
        ddtb_ref[...] += jnp.sum(ddt_raw, axis=0, keepdims=True)
        ddsk_ref[...] += ddsk_row

    g = N_GROUPS
    rc = lambda c: nc - 1 - c
    par = pl.BlockSpec((g, 1, LANES), lambda i, c: (0, 0, 0))
    parw = pl.BlockSpec((g, 1, 256), lambda i, c: (0, 0, 0))
    wide = pl.BlockSpec((None, CHUNK, D_MODEL), lambda i, c: (i, rc(c), 0))
    blk512 = lambda col: pl.BlockSpec((None, CHUNK, 512), lambda i, c: (i, rc(c), col))
    return pl.pallas_call(
        body, name=name, grid=(b, nc),
        in_specs=[wide, blk512(2), blk512(3), blk512(dt0),
                  pl.BlockSpec((None, CHUNK, D_MODEL), lambda i, c: (i, rc(c), z0)),
                  par, par, par, parw,
                  wide,
                  pl.BlockSpec((None, None, g, 2, CHUNK, SSM_STATE), lambda i, c: (i, rc(c), 0, 0, 0, 0)),
                  wide],
        out_specs=[pl.BlockSpec((None, CHUNK, CONV_DIM), lambda i, c: (i, rc(c), 0)), wide, blk512(0),
                   par, par, par, parw],
        out_shape=[jax.ShapeDtypeStruct((b, s, CONV_DIM), F32), jax.ShapeDtypeStruct((b, s, D_MODEL), BF16),
                   jax.ShapeDtypeStruct((b, s, 512), BF16),
                   jax.ShapeDtypeStruct((g, 1, LANES), F32), jax.ShapeDtypeStruct((g, 1, LANES), F32),
                   jax.ShapeDtypeStruct((g, 1, LANES), F32), jax.ShapeDtypeStruct((g, 1, 256), F32)],
        scratch_shapes=[pltpu.VMEM((g, 2, CHUNK, SSM_STATE), F32)],
        compiler_params=_cp(("arbitrary", "arbitrary")),
    )(xact3, xact3, xact3, gates3, proj3, dtb, alog, dsk, nw, ypre3, hst, dya3)


_FGATE_ROWS = 512


def _fgate_fwd(gates3, fb, *, name):
    b, s, _ = gates3.shape
    rows = min(_FGATE_ROWS, s)
    f0 = _PAD_COLS["a_dt"][1] // LANES

    def body(f_ref, fb_ref, cum_ref, carry):
        @pl.when(pl.program_id(1) == 0)
        def _():
            carry[...] = jnp.zeros_like(carry)

        row = lax.broadcasted_iota(jnp.int32, (rows, rows), 0)
        lane = lax.broadcasted_iota(jnp.int32, (rows, rows), 1)
        tri = (row >= lane).astype(F32)
        lf = -_softplus(-(f_ref[...] + fb_ref[...]))
        cs = _dot(tri, lf, precision=HIGHEST) + carry[0:1, :]
        cum_ref[...] = cs
        carry[0:1, :] = _row(cs, rows - 1)

    return pl.pallas_call(
        body, name=name, grid=(b, s // rows),
        in_specs=[pl.BlockSpec((None, rows, LANES), lambda i, c: (i, c, f0)),
                  pl.BlockSpec((1, LANES), lambda i, c: (0, 0))],
        out_specs=pl.BlockSpec((None, rows, LANES), lambda i, c: (i, c, 0)),
        out_shape=jax.ShapeDtypeStruct((b, s, LANES), F32),
        scratch_shapes=[pltpu.VMEM((8, LANES), F32)],
        compiler_params=_cp(("parallel", "arbitrary")),
    )(gates3, fb)


def _fgate_bwd(gates3, fb, dcum, *, name):
    b, s, _ = gates3.shape
    rows = min(_FGATE_ROWS, s)
    nc = s // rows
    f0 = _PAD_COLS["a_dt"][1] // LANES
    npair = dcum.shape[1]

    def body(f_ref, fb_ref, dc_ref, df_ref, dfb_ref, carry):
        first = jnp.logical_and(pl.program_id(0) == 0, pl.program_id(1) == 0)

        @pl.when(first)
        def _():
            dfb_ref[...] = jnp.zeros_like(dfb_ref)

        @pl.when(pl.program_id(1) == 0)
        def _():
            carry[...] = jnp.zeros_like(carry)

        row = lax.broadcasted_iota(jnp.int32, (rows, rows), 0)
        lane = lax.broadcasted_iota(jnp.int32, (rows, rows), 1)
        tri_t = (row <= lane).astype(F32)
        dc = -jnp.sum(dc_ref[...], axis=0)
        dlf = _dot(tri_t, dc, precision=HIGHEST) + carry[0:1, :]
        carry[0:1, :] = _row(dlf, 0)
        df = dlf * _sigmoid(-(f_ref[...] + fb_ref[...]))
        df_ref[...] = df.astype(BF16)
        dfb_ref[...] += jnp.sum(df, axis=0, keepdims=True)

    return pl.pallas_call(
        body, name=name, grid=(b, nc),
        in_specs=[pl.BlockSpec((None, rows, LANES), lambda i, c: (i, nc - 1 - c, f0)),
                  pl.BlockSpec((1, LANES), lambda i, c: (0, 0)),
                  pl.BlockSpec((None, npair, rows, LANES), lambda i, c: (i, 0, nc - 1 - c, 0))],
        out_specs=[pl.BlockSpec((None, rows, LANES), lambda i, c: (i, nc - 1 - c, 0)),
                   pl.BlockSpec((1, LANES), lambda i, c: (0, 0))],
        out_shape=[jax.ShapeDtypeStruct((b, s, LANES), BF16), jax.ShapeDtypeStruct((1, LANES), F32)],
        scratch_shapes=[pltpu.VMEM((8, LANES), F32)],
        compiler_params=_cp(("arbitrary", "arbitrary")),
    )(gates3, fb, dcum)


_SCALE = HEAD_DIM ** -0.5
_NEG = -1e30


_ST_LSE, _ST_DELTA, _ST_MJ = 0, 2, 8


_SR = 40


def _ck_rep(cum):
    b, s, _ = cum.shape
    t = jnp.transpose(cum[:, :, :N_HEADS], (0, 2, 1)).reshape(b, N_HEADS // 2, 2, s, 1)
    return jnp.broadcast_to(t, (b, N_HEADS // 2, 2, s, LANES))


def _foxt_fwd(proj3, ckrep, *, name, tb):
    b, s, _ = proj3.shape
    nq = s // tb
    assert _ST_MJ + 2 * nq <= _SR
    q0 = _PAD_COLS["c_q"][0] // LANES
    k0 = _PAD_COLS["c_k"][0] // LANES
    v0 = _PAD_COLS["c_v"][0] // LANES
    z0 = _PAD_COLS["c_z"][0] // LANES
    rep = tb // LANES

    def body(q_ref, k_ref, v_ref, z_ref, ck_ref, y_ref, o_ref, st_ref):
        i = pl.program_id(2)
        lane = lax.broadcasted_iota(jnp.int32, (tb, LANES), 1)
        lo = lane < HEAD_DIM
        lo_r = lax.broadcasted_iota(jnp.int32, (LANES, tb), 0) < HEAD_DIM
        srow = lax.broadcasted_iota(jnp.int32, (_SR, tb), 0)
        q = q_ref[...].astype(F32) * _SCALE
        qms = (jnp.where(lo, q, 0.0).astype(BF16), jnp.where(lo, 0.0, q).astype(BF16))
        ones_at = (HEAD_DIM, 0)

        def block(j, carry, diagonal):
            ks = pl.ds(pl.multiple_of(j * tb, tb), tb)
            kb = k_ref[ks, :].astype(BF16)
            v = v_ref[ks, :].astype(F32)
            vts = (jnp.where(lo, v, jnp.where(lane == ones_at[0], 1.0, 0.0)).T.astype(BF16),
                   jnp.where(lo, jnp.where(lane == ones_at[1], 1.0, 0.0), v).T.astype(BF16))
            if diagonal:
                key = lax.broadcasted_iota(jnp.int32, (tb, tb), 0)
                qry = lax.broadcasted_iota(jnp.int32, (tb, tb), 1)
                mask = key <= qry
            ms, ls, acc, st = carry
            new_m, new_l, pvs, alphas = [], [], [], []
            for hh in range(2):
                sc = _dot_nt(kb, qms[hh]) - jnp.tile(ck_ref[hh, ks, :], (1, rep))
                if diagonal:
                    sc = jnp.where(mask, sc, _NEG)
                m_new = jnp.maximum(ms[hh], jnp.max(sc, axis=0, keepdims=True))
                alpha = jnp.exp(ms[hh] - m_new)
                pv = _dot(vts[hh], jnp.exp(sc - m_new).astype(BF16))
                rs = _row(pv[ones_at[hh]:ones_at[hh] + 8, :], 0)
                new_l.append(alpha * ls[hh] + rs)
                new_m.append(m_new)
                pvs.append(pv)
                alphas.append(alpha)
                st = jnp.where(srow == _ST_MJ + 2 * j + hh, m_new, st)
            acc = jnp.where(lo_r, alphas[0] * acc + pvs[0], alphas[1] * acc + pvs[1])
            return (tuple(new_m), tuple(new_l), acc, st)

        neg = jnp.full((1, tb), _NEG, F32)
        zero = jnp.zeros((1, tb), F32)
        init = ((neg, neg), (zero, zero), jnp.zeros((LANES, tb), F32), jnp.zeros((_SR, tb), F32))
        carry = lax.fori_loop(0, i, lambda j, c: block(j, c, False), init)
        ms, ls, acc, st = block(i, carry, True)
        o = (acc / jnp.where(lo_r, ls[0], ls[1])).T
        o_ref[...] = o
        st = jnp.where(srow == _ST_LSE, ms[0] + jnp.log(ls[0]), st)
        st_ref[...] = jnp.where(srow == _ST_LSE + 1, ms[1] + jnp.log(ls[1]), st)
        z = z_ref[...].astype(F32)
        y_ref[...] = (o * (z * _sigmoid(z))).astype(BF16)

    qspec = lambda c0: pl.BlockSpec((None, tb, LANES), lambda bi, p, i: (bi, i, c0 + p))
    kspec = lambda c0: pl.BlockSpec((None, s, LANES), lambda bi, p, i: (bi, 0, c0 + p))
    ospec = pl.BlockSpec((None, tb, LANES), lambda bi, p, i: (bi, i, p))
    return pl.pallas_call(
        body, name=name, grid=(b, N_HEADS // 2, nq),
        in_specs=[qspec(q0), kspec(k0), kspec(v0), qspec(z0),
                  pl.BlockSpec((None, None, 2, s, LANES), lambda bi, p, i: (bi, p, 0, 0, 0))],
        out_specs=[ospec, ospec, pl.BlockSpec((None, None, None, _SR, tb), lambda bi, p, i: (bi, p, i, 0, 0))],
        out_shape=[jax.ShapeDtypeStruct((b, s, D_MODEL), BF16), jax.ShapeDtypeStruct((b, s, D_MODEL), F32),
                   jax.ShapeDtypeStruct((b, N_HEADS // 2, nq, _SR, tb), F32)],
        compiler_params=_cp(("parallel", "parallel", "arbitrary")),
    )(proj3, proj3, proj3, proj3, ckrep)


def _foxt_prep(proj3, o3, stat, dy3, *, name, tb):
    b, s, _ = proj3.shape
    nq = s // tb
    z0 = _PAD_COLS["c_z"][0] // 256

    def body(z_ref, o_ref, fst_ref, dy_ref, dz_ref, do_ref, st_ref):
        z = z_ref[...].astype(F32)
        sz = _sigmoid(z)
        dy = dy_ref[...]
        o = o_ref[...]
        do = dy * (z * sz)
        dz_ref[...] = (dy * o * (sz * (1.0 + z * (1.0 - sz)))).astype(BF16)
        do_ref[...] = do
        doo = do.astype(BF16).astype(F32) * o
        r8 = lax.broadcasted_iota(jnp.int32, (8, LANES), 0)
        l8 = lax.broadcasted_iota(jnp.int32, (8, LANES), 1)
        pick = jnp.logical_or(jnp.logical_and(r8 == 0, l8 < HEAD_DIM),
                              jnp.logical_and(r8 == 1, l8 >= HEAD_DIM)).astype(F32)
        srow = lax.broadcasted_iota(jnp.int32, (_SR, tb), 0)
        for pp in range(2):
            d8 = _dot(pick, doo[:, LANES * pp:LANES * (pp + 1)], ((1,), (1,)), precision=HIGHEST)
            st = jnp.where(srow == _ST_DELTA, _row(d8, 0), fst_ref[pp])
            st_ref[pp] = jnp.where(srow == _ST_DELTA + 1, _row(d8, 1), st)

    ospec = pl.BlockSpec((None, tb, 256), lambda bi, p, i: (bi, i, p))
    sspec = pl.BlockSpec((None, 2, None, _SR, tb), lambda bi, p, i: (bi, p, i, 0, 0))
    return pl.pallas_call(
        body, name=name, grid=(b, N_HEADS // 4, nq),
        in_specs=[pl.BlockSpec((None, tb, 256), lambda bi, p, i: (bi, i, z0 + p)), ospec, sspec, ospec],
        out_specs=[ospec, ospec, sspec],
        out_shape=[jax.ShapeDtypeStruct((b, s, D_MODEL), BF16), jax.ShapeDtypeStruct((b, s, D_MODEL), F32),
                   jax.ShapeDtypeStruct((b, N_HEADS // 2, nq, _SR, tb), F32)],
        compiler_params=_cp(("parallel", "parallel", "parallel")),
    )(proj3, o3, stat, dy3)


def _foxt_bwd(proj3, ckrep, do3, stats, *, name, tb):
    b, s, _ = proj3.shape
    nq = s // tb
    q0 = _PAD_COLS["c_q"][0] // LANES
    k0 = _PAD_COLS["c_k"][0] // LANES
    v0 = _PAD_COLS["c_v"][0] // LANES
    rep = tb // LANES

    def body(q_ref, do_ref, st_ref, k_ref, v_ref, ck_ref, dq_ref, dk_ref, dv_ref, cs_ref):
        j = pl.program_id(2)
        lane = lax.broadcasted_iota(jnp.int32, (tb, LANES), 1)
        lo = lane < HEAD_DIM
        lo_r = lax.broadcasted_iota(jnp.int32, (LANES, tb), 0) < HEAD_DIM

        @pl.when(j == 0)
        def _():
            dq_ref[...] = jnp.zeros_like(dq_ref)

        kf = k_ref[...].astype(F32)
        kb = kf.astype(BF16)
        kt = kf.T.astype(BF16)
        vb = v_ref[...].astype(BF16)
        cks = (jnp.tile(ck_ref[0], (1, rep)), jnp.tile(ck_ref[1], (1, rep)))

        def block(i, carry, diagonal):
            qs = pl.ds(pl.multiple_of(i * tb, tb), tb)
            q = q_ref[qs, :].astype(F32) * _SCALE
            do = do_ref[qs, :]
            st = st_ref[i]
            if diagonal:
                key = lax.broadcasted_iota(jnp.int32, (tb, tb), 0)
                qry = lax.broadcasted_iota(jnp.int32, (tb, tb), 1)
                mask = key <= qry
            dk, dv, cs = carry
            new_cs, dqs = [], []
            for hh in range(2):
                sel = lo if hh == 0 else jnp.logical_not(lo)
                qm = jnp.where(sel, q, 0.0).astype(BF16)
                dom = jnp.where(sel, do, 0.0).astype(BF16)
                sc = _dot_nt(kb, qm) - cks[hh]
                if diagonal:
                    sc = jnp.where(mask, sc, _NEG)
                mj = _row(st, _ST_MJ + 2 * j + hh)
                w = jnp.exp(mj - _row(st, _ST_LSE + hh))
                ph = jnp.exp(sc - mj).astype(BF16).astype(F32) * w
                ds = ph * (_dot_nt(vb, dom) - _row(st, _ST_DELTA + hh))
                dsb = ds.astype(BF16)
                dv = dv + _dot(ph.astype(BF16), dom)
                dk = dk + _dot(dsb, qm)
                new_cs.append(cs[hh] + jnp.sum(ds, axis=1, keepdims=True))
                dqs.append(_dot(kt, dsb))
            dq_ref[i] += jnp.where(lo_r, dqs[0], dqs[1]) * _SCALE
            return (dk, dv, tuple(new_cs))

        zcol = jnp.zeros((tb, 1), F32)
        init = (jnp.zeros((tb, LANES), F32), jnp.zeros((tb, LANES), F32), (zcol, zcol))
        carry = block(j, init, True)
        dk, dv, cs = lax.fori_loop(j + 1, nq, lambda i, c: block(i, c, False), carry)
        dk_ref[...] = dk.astype(BF16)
        dv_ref[...] = dv.astype(BF16)
        p2 = 2 * pl.program_id(1)
        cs_ref[...] = jnp.where(lane == p2, cs[0], jnp.where(lane == p2 + 1, cs[1], 0.0))

    full = lambda c0: pl.BlockSpec((None, s, LANES), lambda bi, p, j: (bi, 0, c0 + p))
    kspec = lambda c0: pl.BlockSpec((None, tb, LANES), lambda bi, p, j: (bi, j, c0 + p))
    ko = pl.BlockSpec((None, tb, LANES), lambda bi, p, j: (bi, j, p))
    sall = pl.BlockSpec((None, None, nq, _SR, tb), lambda bi, p, j: (bi, p, 0, 0, 0))
    dqspec = pl.BlockSpec((None, None, nq, LANES, tb), lambda bi, p, j: (bi, p, 0, 0, 0))
    return pl.pallas_call(
        body, name=name, grid=(b, N_HEADS // 2, nq),
        in_specs=[full(q0), full(0), sall, kspec(k0), kspec(v0),
                  pl.BlockSpec((None, None, 2, tb, LANES), lambda bi, p, j: (bi, p, 0, j, 0))],
        out_specs=[dqspec, ko, ko, pl.BlockSpec((None, None, tb, LANES), lambda bi, p, j: (bi, p, j, 0))],
        out_shape=[jax.ShapeDtypeStruct((b, N_HEADS // 2, nq, LANES, tb), F32),
                   jax.ShapeDtypeStruct((b, s, D_MODEL), BF16), jax.ShapeDtypeStruct((b, s, D_MODEL), BF16),
                   jax.ShapeDtypeStruct((b, N_HEADS // 2, s, LANES), F32)],
        compiler_params=_cp(("parallel", "parallel", "arbitrary")),
    )(proj3, do3, stats, proj3, proj3, ckrep)


def _rope(x, cos, sin_signed):
    w = x.shape[1]
    lane = lax.broadcasted_iota(jnp.int32, x.shape, 1)
    first = (lane % HEAD_DIM) < (HEAD_DIM // 2)
    rot = jnp.where(first, pltpu.roll(x, w - HEAD_DIM // 2, 1), pltpu.roll(x, HEAD_DIM // 2, 1))
    return x * cos + rot * sin_signed


_QB = 8
_QROWS = _QB * CHUNK


def _swa_keys(g, kc_ref, kp_ref, vc_ref, vp_ref, cq_ref, sq_ref, cp_ref, sp_ref):
    def both_halves(x):
        x = x.astype(F32)
        lane = lax.broadcasted_iota(jnp.int32, x.shape, 1)
        keep = (lane // HEAD_DIM) == (g % 2)
        return jnp.where(keep, x, pltpu.roll(x, HEAD_DIM, 1))

    cq, sq, cpv, spv = cq_ref[...], sq_ref[...], cp_ref[...], sp_ref[...]
    kc = _rope(both_halves(kc_ref[...]), cq, sq).astype(BF16)
    kp = _rope(both_halves(kp_ref[...]), cpv, spv).astype(BF16)
    return cq, sq, cpv, spv, kc, kp, both_halves(vc_ref[...]).astype(BF16), both_halves(vp_ref[...]).astype(BF16)


def _swa_stack(pairs, lo):
    return jnp.concatenate([jnp.where(lo, pairs[0], 0.0), jnp.where(lo, 0.0, pairs[0]),
                            jnp.where(lo, pairs[1], 0.0), jnp.where(lo, 0.0, pairs[1])], axis=0).astype(BF16)


def _swa_mask4(prev_valid):
    r = lax.broadcasted_iota(jnp.int32, (4 * CHUNK, 2 * CHUNK), 0) & (CHUNK - 1)
    c = lax.broadcasted_iota(jnp.int32, (4 * CHUNK, 2 * CHUNK), 1)
    own = jnp.logical_and(c >= CHUNK, c - CHUNK <= r)
    before = jnp.logical_and(c < CHUNK, c > r)
    if prev_valid is True:
        return jnp.logical_or(own, before)
    return jnp.logical_or(own, jnp.logical_and(before, prev_valid))


def _swa_sink4(skv):
    return jnp.concatenate([jnp.broadcast_to(_col(skv, j), (CHUNK, 1)) for j in range(4)], axis=0)


def _swa_specs(order):
    def spec(shape, fn):
        return pl.BlockSpec(shape, lambda *ids: fn(*order(*ids)))

    q0 = _PAD_COLS["b_q"][0] // 256
    z0 = _PAD_COLS["b_z"][0] // 256
    k0 = _PAD_COLS["b_k"][0] // LANES
    v0 = _PAD_COLS["b_v"][0] // LANES
    prev = lambda i: jnp.maximum(_QB * i - 1, 0)
    return dict(
        kc=spec((None, _QROWS, LANES), lambda bi, g, i: (bi, i, k0 + g // 2)),
        kp=spec((None, CHUNK, LANES), lambda bi, g, i: (bi, prev(i), k0 + g // 2)),
        vc=spec((None, _QROWS, LANES), lambda bi, g, i: (bi, i, v0 + g // 2)),
        vp=spec((None, CHUNK, LANES), lambda bi, g, i: (bi, prev(i), v0 + g // 2)),
        q=spec((None, _QROWS, 256), lambda bi, g, i: (bi, i, q0 + g)),
        z=spec((None, _QROWS, 256), lambda bi, g, i: (bi, i, z0 + g)),
        blk=spec((None, _QROWS, 256), lambda bi, g, i: (bi, i, g)),
        kcur=spec((None, _QROWS, LANES), lambda bi, g, i: (bi, i, g)),
        kstep=spec((None, CHUNK, LANES), lambda bi, g, i: (bi, i, g)),
        tcur=spec((_QROWS, LANES), lambda bi, g, i: (i, 0)),
        tprev=spec((CHUNK, LANES), lambda bi, g, i: (prev(i), 0)),
        sk=spec((None, 1, LANES), lambda bi, g, i: (g, 0, 0)))


def _swa_fwd(proj3, cos, sin, sinks, *, name):
    b, s, _ = proj3.shape

    def body(q_ref, z_ref, kc_ref, kp_ref, vc_ref, vp_ref, cq_ref, sq_ref, cp_ref, sp_ref, sk_ref,
             y_ref, o_ref, lse_ref):
        i = pl.program_id(2)
        cq_all, sq_all, _, _, kc_all, kp0, vc_all, vp0 = _swa_keys(
            pl.program_id(1), kc_ref, kp_ref, vc_ref, vp_ref, cq_ref, sq_ref, cp_ref, sp_ref)
        lo = lax.broadcasted_iota(jnp.int32, (CHUNK, LANES), 1) < HEAD_DIM
        sink4 = _swa_sink4(sk_ref[...])
        for u in range(_QB):
            rs = slice(CHUNK * u, CHUNK * (u + 1))
            ps = slice(CHUNK * (u - 1), CHUNK * u)
            cq, sq = cq_all[rs], sq_all[rs]
            kp, vp = (kp0, vp0) if u == 0 else (kc_all[ps], vc_all[ps])
            kk = jnp.concatenate([kp, kc_all[rs]], axis=0)
            vv = jnp.concatenate([vp, vc_all[rs]], axis=0)
            q4 = _swa_stack([_rope(q_ref[rs, LANES * pp:LANES * (pp + 1)].astype(F32), cq, sq) * _SCALE
                             for pp in range(2)], lo)
            sc = jnp.where(_swa_mask4(True if u > 0 else i > 0), _dot_nt(q4, kk), _NEG)
            m = jnp.maximum(jnp.max(sc, axis=1, keepdims=True), sink4)
            pr = jnp.exp(sc - m)
            l = jnp.sum(pr, axis=1, keepdims=True) + jnp.exp(sink4 - m)
            o4 = _dot(pr.astype(BF16), vv) / l
            lse4 = m + jnp.log(l)
            for pp in range(2):
                ls = slice(LANES * pp, LANES * (pp + 1))
                h0 = slice(2 * CHUNK * pp, 2 * CHUNK * pp + CHUNK)
                h1 = slice(2 * CHUNK * pp + CHUNK, 2 * CHUNK * (pp + 1))
                o = jnp.where(lo, o4[h0], o4[h1])
                z = z_ref[rs, ls].astype(F32)
                o_ref[rs, ls] = o
                lse_ref[rs, ls] = jnp.where(lo, lse4[h0], lse4[h1])
                y_ref[rs, ls] = (o * (z * _sigmoid(z))).astype(BF16)

    sp = _swa_specs(lambda bi, g, i: (bi, g, i))
    return pl.pallas_call(
        body, name=name, grid=(b, N_GROUPS, s // _QROWS),
        in_specs=[sp["q"], sp["z"], sp["kc"], sp["kp"], sp["vc"], sp["vp"],
                  sp["tcur"], sp["tcur"], sp["tprev"], sp["tprev"], sp["sk"]],
        out_specs=[sp["blk"], sp["blk"], sp["blk"]],
        out_shape=[jax.ShapeDtypeStruct((b, s, D_MODEL), BF16)] + [jax.ShapeDtypeStruct((b, s, D_MODEL), F32)] * 2,
        compiler_params=_cp(("parallel", "parallel", "parallel")),
    )(proj3, proj3, proj3, proj3, proj3, proj3, cos, sin, cos, sin, sinks)


def _swa_bwd(proj3, cos, sin, sinks, o3, lse3, dy3, *, name):
    b, s, _ = proj3.shape

    def body(q_ref, z_ref, kc_ref, kp_ref, vc_ref, vp_ref, cq_ref, sq_ref, cp_ref, sp_ref, sk_ref,
             o_ref, lse_ref, dy_ref, dq_ref, dz_ref, dkc_ref, dkp_ref, dvc_ref, dvp_ref, dsk_ref):
        i = pl.program_id(2)
        first = jnp.logical_and(pl.program_id(1) == 0, i == 0)

        @pl.when(first)
        def _():
            dsk_ref[...] = jnp.zeros_like(dsk_ref)

        cq_all, sq_all, cpv, spv, kc_all, kp0, vc_all, vp0 = _swa_keys(
            pl.program_id(0), kc_ref, kp_ref, vc_ref, vp_ref, cq_ref, sq_ref, cp_ref, sp_ref)
        lo = lax.broadcasted_iota(jnp.int32, (CHUNK, LANES), 1) < HEAD_DIM
        lane1 = lax.broadcasted_iota(jnp.int32, (1, LANES), 1)
        sink4 = _swa_sink4(sk_ref[...])
        zero = jnp.zeros((CHUNK, LANES), F32)
        dks = [zero] * (_QB + 1)
        dvs = [zero] * (_QB + 1)
        dsk_row = jnp.zeros((1, LANES), F32)
        for u in range(_QB):
            rs = slice(CHUNK * u, CHUNK * (u + 1))
            ps = slice(CHUNK * (u - 1), CHUNK * u)
            cq, sq = cq_all[rs], sq_all[rs]
            kp, vp = (kp0, vp0) if u == 0 else (kc_all[ps], vc_all[ps])
            kk = jnp.concatenate([kp, kc_all[rs]], axis=0)
            vv = jnp.concatenate([vp, vc_all[rs]], axis=0)
            q4 = _swa_stack([_rope(q_ref[rs, LANES * pp:LANES * (pp + 1)].astype(F32), cq, sq) * _SCALE
                             for pp in range(2)], lo)
            dos, lses = [], []
            for pp in range(2):
                ls = slice(LANES * pp, LANES * (pp + 1))
                z = z_ref[rs, ls].astype(F32)
                sz = _sigmoid(z)
                dy = dy_ref[rs, ls]
                dos.append(dy * (z * sz))
                dz_ref[rs, ls] = (dy * o_ref[rs, ls] * (sz * (1.0 + z * (1.0 - sz)))).astype(BF16)
                lse = lse_ref[rs, ls]
                lses += [_col(lse, 0), _col(lse, HEAD_DIM)]
            do4 = _swa_stack(dos, lo)
            lse4 = jnp.concatenate(lses, axis=0)
            pr = jnp.exp(jnp.where(_swa_mask4(True if u > 0 else i > 0), _dot_nt(q4, kk), _NEG) - lse4)
            dp = _dot_nt(do4, vv)
            dl = jnp.sum(pr * dp, axis=1, keepdims=True)
            ds = (pr * (dp - dl)).astype(BF16)
            dsink = -jnp.exp(sink4 - lse4) * dl
            for j in range(4):
                dsk_row = dsk_row + jnp.where(
                    lane1 == j, jnp.sum(dsink[CHUNK * j:CHUNK * (j + 1)], axis=0, keepdims=True), 0.0)
            dq4 = _dot(ds, kk)
            dkk = _dot_tn(ds, q4)
            dvv = _dot_tn(pr.astype(BF16), do4)
            dks[u], dks[u + 1] = dks[u] + dkk[:CHUNK], dks[u + 1] + dkk[CHUNK:]
            dvs[u], dvs[u + 1] = dvs[u] + dvv[:CHUNK], dvs[u + 1] + dvv[CHUNK:]
            for pp in range(2):
                h0 = slice(2 * CHUNK * pp, 2 * CHUNK * pp + CHUNK)
                h1 = slice(2 * CHUNK * pp + CHUNK, 2 * CHUNK * (pp + 1))
                dq_ref[rs, LANES * pp:LANES * (pp + 1)] = _rope(
                    jnp.where(lo, dq4[h0], dq4[h1]) * _SCALE, cq, -sq).astype(BF16)
        fold = lambda v: v + pltpu.roll(v, HEAD_DIM, 1)
        dkp_ref[...] = fold(_rope(dks[0], cpv, -spv))
        dvp_ref[...] = fold(dvs[0])
        for u in range(_QB):
            rs = slice(CHUNK * u, CHUNK * (u + 1))
            dkc_ref[rs, :] = fold(_rope(dks[u + 1], cq_all[rs], -sq_all[rs]))
            dvc_ref[rs, :] = fold(dvs[u + 1])
        dsk_ref[...] += dsk_row

    sp = _swa_specs(lambda g, bi, i: (bi, g, i))
    kv_shape = jax.ShapeDtypeStruct((b, s, 512), F32)
    kvp_shape = jax.ShapeDtypeStruct((b, s // _QB, 512), F32)
    return pl.pallas_call(
        body, name=name, grid=(N_GROUPS, b, s // _QROWS),
        in_specs=[sp["q"], sp["z"], sp["kc"], sp["kp"], sp["vc"], sp["vp"],
                  sp["tcur"], sp["tcur"], sp["tprev"], sp["tprev"], sp["sk"], sp["blk"], sp["blk"], sp["blk"]],
        out_specs=[sp["blk"], sp["blk"], sp["kcur"], sp["kstep"], sp["kcur"], sp["kstep"], sp["sk"]],
        out_shape=[jax.ShapeDtypeStruct((b, s, D_MODEL), BF16), jax.ShapeDtypeStruct((b, s, D_MODEL), BF16),
                   kv_shape, kvp_shape, kv_shape, kvp_shape, jax.ShapeDtypeStruct((N_GROUPS, 1, LANES), F32)],
        compiler_params=_cp(("arbitrary", "arbitrary", "arbitrary")),
    )(proj3, proj3, proj3, proj3, proj3, proj3, cos, sin, cos, sin, sinks, o3, lse3, dy3)


def _swa_fold(dkc, dkp, dvc, dvp, *, name):
    b, s, _ = dkc.shape
    ns = s // _QROWS

    def body(kc_ref, kp_ref, vc_ref, vp_ref, dk_ref, dv_ref):
        has_next = pl.program_id(1) < ns - 1
        lo = lax.broadcasted_iota(jnp.int32, (_QROWS, LANES), 1) < HEAD_DIM
        row = lax.broadcasted_iota(jnp.int32, (_QROWS, 512), 0)
        last_block = jnp.logical_and(row >= _QROWS - CHUNK, has_next)
        for cur, nxt, out in ((kc_ref, kp_ref, dk_ref), (vc_ref, vp_ref, dv_ref)):
            tot = cur[...] + jnp.where(last_block, jnp.tile(nxt[...], (_QB, 1)), 0.0)
            for j in range(2):
                out[:, LANES * j:LANES * (j + 1)] = jnp.where(
                    lo, tot[:, 256 * j:256 * j + LANES], tot[:, 256 * j + LANES:256 * (j + 1)]).astype(BF16)

    cur = pl.BlockSpec((None, _QROWS, 512), lambda bi, i: (bi, i, 0))
    nxt = pl.BlockSpec((None, CHUNK, 512), lambda bi, i: (bi, jnp.minimum(i + 1, ns - 1), 0))
    out = pl.BlockSpec((None, _QROWS, 256), lambda bi, i: (bi, i, 0))
    sh = jax.ShapeDtypeStruct((b, s, 256), BF16)
    return pl.pallas_call(
        body, name=name, grid=(b, ns), in_specs=[cur, nxt, cur, nxt], out_specs=[out, out], out_shape=[sh, sh],
        compiler_params=_cp(("parallel", "parallel")),
    )(dkc, dkp, dvc, dvp)


def _branch_fwd(ys, proj, gb, wp, wo, x, *, name, tm=256):
    t = proj.shape[0]
    g0 = _PAD_COLS["gates"][0] // D_MODEL

    def body(g_ref, a_ref, b_ref, c_ref, gb_ref, wp_ref, wo_ref, x_ref, ba_ref, bb_ref, bc_ref, m_ref, xn_ref):
        acc = None
        for i, (y, br) in enumerate(((a_ref, ba_ref), (b_ref, bb_ref), (c_ref, bc_ref))):
            bri = _dot(y[...], wp_ref[i])
            br[...] = bri
            gate = _sigmoid(g_ref[:, D_MODEL * i:D_MODEL * (i + 1)].astype(F32) + gb_ref[i:i + 1, :])
            acc = gate * bri if acc is None else acc + gate * bri
        mb = acc.astype(BF16)
        m_ref[...] = mb
        xn_ref[...] = x_ref[...] + _dot(mb, wo_ref[...])

    row = pl.BlockSpec((tm, D_MODEL), lambda i: (i, 0))
    rowf = jax.ShapeDtypeStruct((t, D_MODEL), F32)
    outs = pl.pallas_call(
        body, name=name, grid=(t // tm,),
        in_specs=[pl.BlockSpec((tm, 3 * D_MODEL), lambda i: (i, g0)), row, row, row,
                  pl.BlockSpec((3, D_MODEL), lambda i: (0, 0)),
                  pl.BlockSpec((3, D_MODEL, D_MODEL), lambda i: (0, 0, 0)),
                  pl.BlockSpec((D_MODEL, D_MODEL), lambda i: (0, 0)), row],
        out_specs=[row, row, row, row, row],
        out_shape=[rowf, rowf, rowf, jax.ShapeDtypeStruct((t, D_MODEL), BF16), rowf],
        compiler_params=_cp(("parallel",)),
    )(proj, ys[0], ys[1], ys[2], gb, wp, wo, x)
    return outs[:3], outs[3], outs[4]


def _branch_bwd(dx, proj, br, gb, wp, wo, *, name, tm=256):
    t = proj.shape[0]
    g0 = _PAD_COLS["gates"][0] // D_MODEL

    def body(g_ref, a_ref, b_ref, c_ref, gb_ref, wp_ref, wo_ref, dx_ref,
             da_ref, db_ref, dc_ref, dg_ref, dgb_ref, ya_ref, yb_ref, yc_ref):
        @pl.when(pl.program_id(0) == 0)
        def _():
            dgb_ref[...] = jnp.zeros_like(dgb_ref)

        dmv = _dot_nt(dx_ref[...].astype(BF16), wo_ref[...])
        for i, (r, dr, dy) in enumerate(((a_ref, da_ref, ya_ref), (b_ref, db_ref, yb_ref), (c_ref, dc_ref, yc_ref))):
            gate = _sigmoid(g_ref[:, D_MODEL * i:D_MODEL * (i + 1)].astype(F32) + gb_ref[i:i + 1, :])
            dbr = (dmv * gate).astype(BF16)
            dr[...] = dbr
            dg = dmv * r[...] * gate * (1.0 - gate)
            dg_ref[:, D_MODEL * i:D_MODEL * (i + 1)] = dg.astype(BF16)
            dgb_ref[i:i + 1, :] += jnp.sum(dg, axis=0, keepdims=True)
            dy[...] = _dot_nt(dbr, wp_ref[i])

    row = pl.BlockSpec((tm, D_MODEL), lambda i: (i, 0))
    rowb = jax.ShapeDtypeStruct((t, D_MODEL), BF16)
    rowf = jax.ShapeDtypeStruct((t, D_MODEL), F32)
    outs = pl.pallas_call(
        body, name=name, grid=(t // tm,),
        in_specs=[pl.BlockSpec((tm, 3 * D_MODEL), lambda i: (i, g0)), row, row, row,
                  pl.BlockSpec((3, D_MODEL), lambda i: (0, 0)),
                  pl.BlockSpec((3, D_MODEL, D_MODEL), lambda i: (0, 0, 0)),
                  pl.BlockSpec((D_MODEL, D_MODEL), lambda i: (0, 0)), row],
        out_specs=[row, row, row, pl.BlockSpec((tm, 3 * D_MODEL), lambda i: (i, 0)),
                   pl.BlockSpec((8, D_MODEL), lambda i: (0, 0)), row, row, row],
        out_shape=[rowb, rowb, rowb, jax.ShapeDtypeStruct((t, 3 * D_MODEL), BF16),
                   jax.ShapeDtypeStruct((8, D_MODEL), F32), rowf, rowf, rowf],
        compiler_params=_cp(("arbitrary",)),
    )(proj, br[0], br[1], br[2], gb, wp, wo, dx)
    return outs[:3], outs[3], outs[4], outs[5:]


def _rope_tables(s):
    pos = jnp.arange(s, dtype=F32)
    inv_freq = ROPE_THETA ** (-jnp.arange(0, HEAD_DIM, 2, dtype=F32) / HEAD_DIM)
    ang = pos[:, None] * inv_freq[None, :]
    cos, sin = jnp.cos(ang), jnp.sin(ang)
    return jnp.tile(cos, (1, 4)), jnp.tile(jnp.concatenate([-sin, sin], axis=1), (1, 2))


def _layer_params(wl):
    return dict(
        dtb=_group_lanes(wl["dt_bias"]), alog=_group_lanes(wl["a_log"]), dsk=_group_lanes(wl["d_skip"]),
        nw=wl["ssm_norm_w"].reshape(N_GROUPS, 1, 256), sinks=_group_lanes(wl["sinks"]),
        fb=jnp.pad(wl["f_bias"], (0, LANES - N_HEADS)).reshape(1, LANES))


def _layer_fwd(x, wl, tabs, bsz, li, tb):
    t = x.shape[0]
    s = t // bsz
    cos, sin = tabs
    lp = _layer_params(wl)
    n = lambda k: f"l{li}_{k}"
    h, h_t = _rms_fwd(x, wl["norm_w"], name=n("rms_fwd"))
    proj = _mm(h, wl["w_in"], tm=1024, tn=1536, tk=1024, out_dtype=BF16, name=n("mm_proj"))
    proj3 = proj.reshape(bsz, s, N_PAD)
    g0, gw = _PAD_COLS["a_dt"][0], _PAD_COLS["a_dt"][1] + _PAD_COLS["c_f"][1]
    gates3 = _mm(h, wl["w_in"][:, g0:g0 + gw], tm=1024, tn=gw, tk=1024, name=n("mm_gates")).reshape(bsz, s, gw)
    xact3 = _conv_fwd(proj3, wl["conv_w"], wl["conv_b"], name=n("conv_fwd"))
    ya3, ypre3, hst = _ssd_fwd(proj3, gates3, xact3, lp["dtb"], lp["alog"], lp["dsk"], lp["nw"], name=n("ssd_fwd"))
    yb3, ob3, lseb3 = _swa_fwd(proj3, cos, sin, lp["sinks"], name=n("swa_fwd"))
    cum = _fgate_fwd(gates3, lp["fb"], name=n("fgate_fwd"))
    cum_t = _ck_rep(cum)
    yc3, oc3, statc3 = _foxt_fwd(proj3, cum_t, name=n("fox_fwd"), tb=tb)
    ys = [v.reshape(t, D_MODEL) for v in (ya3, yb3, yc3)]
    br, merged, x_new = _branch_fwd(ys, proj, wl["gate_bias"], wl["w_proj"], wl["w_out"], x, name=n("branch_fwd"))
    saved = dict(x=x, h_t=h_t, proj=proj, gates3=gates3, xact3=xact3, ypre3=ypre3, hst=hst, ob3=ob3, lseb3=lseb3,
                 cum_t=cum_t, oc3=oc3, statc3=statc3, ys=ys, br=br, merged=merged, lp=lp)
    return x_new, saved


def _layer_bwd(dx, wl, sv, tabs, bsz, li, tb):
    t = dx.shape[0]
    s = t // bsz
    cos, sin = tabs
    lp = sv["lp"]
    n = lambda k: f"l{li}_{k}"
    proj = sv["proj"]
    proj3 = proj.reshape(bsz, s, N_PAD)
    g = {}
    g["w_out"] = _mm(sv["merged"], dx, ta=True, tm=1024, tn=1024, tk=512, name=n("mm_dwout"))
    dbr, dgates, dgb, dys = _branch_bwd(dx, proj, sv["br"], wl["gate_bias"], wl["w_proj"], wl["w_out"],
                                        name=n("branch_bwd"))
    g["gate_bias"] = dgb[:3]
    g["w_proj"] = jnp.stack([_mm(sv["ys"][i], dbr[i], ta=True, tm=1024, tn=1024, tk=512, name=n(f"mm_dwproj{i}"))
                             for i in range(3)])
    dy3 = [v.reshape(bsz, s, D_MODEL) for v in dys]

    (dact, daz, dadt, ddtb, dalog, ddsk, dnw) = _ssd_bwd(
        proj3, sv["gates3"], sv["xact3"], lp["dtb"], lp["alog"], lp["dsk"], lp["nw"], sv["ypre3"], sv["hst"], dy3[0],
        name=n("ssd_bwd"))
    g["dt_bias"], g["a_log"], g["d_skip"] = _ungroup_lanes(ddtb), _ungroup_lanes(dalog), _ungroup_lanes(ddsk)
    g["ssm_norm_w"] = dnw.reshape(D_MODEL)
    dxbc, dwb = _conv_bwd(proj3, wl["conv_w"], wl["conv_b"], dact, name=n("conv_bwd"))
    g["conv_w"], g["conv_b"] = dwb[:CONV_WIDTH], dwb[CONV_WIDTH]

    dbq, dbz, dkc, dkp, dvc, dvp, dsk = _swa_bwd(proj3, cos, sin, lp["sinks"], sv["ob3"],
                                                 sv["lseb3"], dy3[1], name=n("swa_bwd"))
    g["sinks"] = _ungroup_lanes(dsk)

    dbk, dbv = _swa_fold(dkc, dkp, dvc, dvp, name=n("swa_fold"))

    dcz, do3, stats = _foxt_prep(proj3, sv["oc3"], sv["statc3"], dy3[2], name=n("fox_prep"), tb=tb)
    dqt, dck, dcv, csum = _foxt_bwd(proj3, sv["cum_t"], do3, stats, name=n("fox_bwd"), tb=tb)
    dcq = jnp.transpose(dqt, (0, 2, 4, 1, 3)).reshape(bsz, s, D_MODEL)
    dcf, dfb = _fgate_bwd(sv["gates3"], lp["fb"], csum, name=n("fgate_bwd"))
    g["f_bias"] = dfb[0, :N_HEADS]

    parts = {"gates": dgates.reshape(bsz, s, 3 * D_MODEL), "xbc": dxbc, "a_z": daz, "b_q": dbq, "b_z": dbz,
             "c_q": dcq, "c_k": dck, "c_v": dcv, "c_z": dcz, "b_k": dbk, "b_v": dbv, "a_dt": dadt, "c_f": dcf}
    dproj = jnp.concatenate([parts[name].astype(BF16) for name, _ in _PAD_ORDER]
                            + [jnp.zeros((bsz, s, N_PAD - N_USED), BF16)], axis=2).reshape(t, N_PAD)
    dh = _mm(dproj, wl["w_in"], tb=True, tm=1024, tn=1024, tk=1536, name=n("mm_dh"))
    g["w_in"] = _unpad_w_in(_mm(sv["h_t"], dproj, tm=1024, tn=768, tk=2048, name=n("mm_dwin")))
    dx_in, dnorm = _rms_bwd(sv["x"], wl["norm_w"], dh, dx, name=n("rms_bwd"))
    g["norm_w"] = dnorm[0]
    return dx_in, g


def _local_step(x, target, wls, final_norm_w, tb=1024):
    bsz, s, d = x.shape
    t = bsz * s
    tabs = _rope_tables(s)
    xc = x.reshape(t, d)
    saved = []
    for li, wl in enumerate(wls):
        xc, sv = _layer_fwd(xc, wl, tabs, bsz, li, tb)
        saved.append(sv)
    loss, dx, dfw = _final_loss(xc, final_norm_w, target.reshape(t, d), name="final_loss")
    grads = [None] * len(wls)
    for li in reversed(range(len(wls))):
        dx, grads[li] = _layer_bwd(dx, wls[li], saved[li], tabs, bsz, li, tb)
    return loss[0, 0], dx.reshape(bsz, s, d), grads, dfw[0]


_HBM = pl.BlockSpec(memory_space=pltpu.HBM)


def _chip_peers(x, y):
    return [(1 - x, y), (x, 1 - y), (1 - x, 1 - y)]


def _gather_weights(arrs, *, name):
    n = len(arrs)

    def body(*refs):
        ins, outs = refs[:n], refs[n:2 * n]
        ici_send, ici_recv, d2d_send, d2d_recv = refs[2 * n:]
        x, y, c = lax.axis_index("x"), lax.axis_index("y"), lax.axis_index("c")
        me = 2 * x + y
        peers = _chip_peers(x, y)
        sib = (x, y, 1 - c)
        sends, fwds = [], []
        for a in range(n):
            for k, (px, py) in enumerate(peers):
                cp = pltpu.make_async_remote_copy(
                    src_ref=ins[a].at[c], dst_ref=outs[a].at[me, c], send_sem=ici_send.at[a, k],
                    recv_sem=ici_recv.at[a, k], device_id=(px, py, c), device_id_type=MESH)
                cp.start()
                sends.append(cp)
        for a in range(n):
            for k, (px, py) in enumerate(peers):
                slot = 2 * px + py
                pltpu.make_async_remote_copy(
                    src_ref=ins[a].at[c], dst_ref=outs[a].at[slot, c], send_sem=ici_send.at[a, k],
                    recv_sem=ici_recv.at[a, k], device_id=(px, py, c), device_id_type=MESH).wait_recv()
                fw = pltpu.make_async_remote_copy(
                    src_ref=outs[a].at[slot, c], dst_ref=outs[a].at[slot, c], send_sem=d2d_send.at[a, k],
                    recv_sem=d2d_recv.at[a, k], device_id=sib, device_id_type=MESH)
                fw.start()
                fwds.append(fw)
        for a in range(n):
            for k, (px, py) in enumerate(peers):
                slot = 2 * px + py
                pltpu.make_async_remote_copy(
                    src_ref=outs[a].at[slot, 1 - c], dst_ref=outs[a].at[slot, 1 - c], send_sem=d2d_send.at[a, k],
                    recv_sem=d2d_recv.at[a, k], device_id=sib, device_id_type=MESH).wait_recv()
        for cp in sends + fwds:
            cp.wait_send()

    out_shape = [jax.ShapeDtypeStruct((N_CHIPS,) + a.shape, a.dtype) for a in arrs]
    return pl.pallas_call(
        body, name=name, out_shape=out_shape, in_specs=[_HBM] * n, out_specs=[_HBM] * n,
        scratch_shapes=[pltpu.SemaphoreType.DMA((n, 3)), pltpu.SemaphoreType.DMA((n, 3)),
                        pltpu.SemaphoreType.DMA((n, 3)), pltpu.SemaphoreType.DMA((n, 3))],
    )(*arrs)


def _pair_exchange(arrs, *, name):
    n = len(arrs)

    def body(*refs):
        ins, outs = refs[:n], refs[n:2 * n]
        send, recv = refs[2 * n:]
        x, y, c = lax.axis_index("x"), lax.axis_index("y"), lax.axis_index("c")
        sib = (x, y, 1 - c)
        cps = []
        for a in range(n):
            for k in range(N_CHIPS):
                cp = pltpu.make_async_remote_copy(
                    src_ref=ins[a].at[k, 1 - c], dst_ref=outs[a].at[k], send_sem=send.at[a, k],
                    recv_sem=recv.at[a, k], device_id=sib, device_id_type=MESH)
                cp.start()
                cps.append(cp)
        for cp in cps:
            cp.wait()

    out_shape = [jax.ShapeDtypeStruct((N_CHIPS,) + a.shape[2:], a.dtype) for a in arrs]
    return pl.pallas_call(
        body, name=name, out_shape=out_shape, in_specs=[_HBM] * n, out_specs=[_HBM] * n,
        scratch_shapes=[pltpu.SemaphoreType.DMA((n, N_CHIPS)), pltpu.SemaphoreType.DMA((n, N_CHIPS))],
    )(*arrs)


def _chip_exchange(arrs, *, name):
    n = len(arrs)

    def body(*refs):
        ins, outs = refs[:n], refs[n:2 * n]
        send, recv = refs[2 * n:]
        x, y, c = lax.axis_index("x"), lax.axis_index("y"), lax.axis_index("c")
        me = 2 * x + y
        peers = _chip_peers(x, y)
        cps = []
        for a in range(n):
            for k, (px, py) in enumerate(peers):
                cp = pltpu.make_async_remote_copy(
                    src_ref=ins[a].at[2 * px + py], dst_ref=outs[a].at[me], send_sem=send.at[a, k],
                    recv_sem=recv.at[a, k], device_id=(px, py, c), device_id_type=MESH)
                cp.start()
                cps.append(cp)
        for a in range(n):
            for k, (px, py) in enumerate(peers):
                pltpu.make_async_remote_copy(
                    src_ref=ins[a].at[2 * px + py], dst_ref=outs[a].at[2 * px + py], send_sem=send.at[a, k],
                    recv_sem=recv.at[a, k], device_id=(px, py, c), device_id_type=MESH).wait_recv()
        for cp in cps:
            cp.wait_send()

    out_shape = [jax.ShapeDtypeStruct(a.shape, a.dtype) for a in arrs]
    return pl.pallas_call(
        body, name=name, out_shape=out_shape, in_specs=[_HBM] * n, out_specs=[_HBM] * n,
        scratch_shapes=[pltpu.SemaphoreType.DMA((n, 3)), pltpu.SemaphoreType.DMA((n, 3))],
    )(*arrs)


def _pair_share(arrs, *, name):
    n = len(arrs)

    def body(*refs):
        ins, outs = refs[:n], refs[n:2 * n]
        send, recv = refs[2 * n:]
        x, y, c = lax.axis_index("x"), lax.axis_index("y"), lax.axis_index("c")
        sib = (x, y, 1 - c)
        cps = []
        for a in range(n):
            cp = pltpu.make_async_remote_copy(
                src_ref=ins[a], dst_ref=outs[a], send_sem=send.at[a], recv_sem=recv.at[a],
                device_id=sib, device_id_type=MESH)
            cp.start()
            cps.append(cp)
        for cp in cps:
            cp.wait()

    out_shape = [jax.ShapeDtypeStruct(a.shape, a.dtype) for a in arrs]
    return pl.pallas_call(
        body, name=name, out_shape=out_shape, in_specs=[_HBM] * n, out_specs=[_HBM] * n,
        scratch_shapes=[pltpu.SemaphoreType.DMA((n,)), pltpu.SemaphoreType.DMA((n,))],
    )(*arrs)


def _allreduce_small(buf, *, name):
    r = buf.shape[0]

    def body(in_ref, out_ref, land, send, recv):
        x, y, c = lax.axis_index("x"), lax.axis_index("y"), lax.axis_index("c")
        me = 4 * x + 2 * y + c
        land[me] = in_ref[...]
        cps = []
        for k in range(1, N_DEV):
            px, py, pc = x ^ ((k >> 2) & 1), y ^ ((k >> 1) & 1), c ^ (k & 1)
            cp = pltpu.make_async_remote_copy(
                src_ref=in_ref, dst_ref=land.at[me], send_sem=send.at[k - 1], recv_sem=recv.at[k - 1],
                device_id=(px, py, pc), device_id_type=MESH)
            cp.start()
            cps.append(cp)
        for k in range(1, N_DEV):
            px, py, pc = x ^ ((k >> 2) & 1), y ^ ((k >> 1) & 1), c ^ (k & 1)
            pltpu.make_async_remote_copy(
                src_ref=in_ref, dst_ref=land.at[4 * px + 2 * py + pc], send_sem=send.at[k - 1],
                recv_sem=recv.at[k - 1], device_id=(px, py, pc), device_id_type=MESH).wait_recv()
        for cp in cps:
            cp.wait_send()
        acc = land[0]
        for k in range(1, N_DEV):
            acc = acc + land[k]
        out_ref[...] = acc

    vm = pl.BlockSpec(memory_space=pltpu.VMEM)
    return pl.pallas_call(
        body, name=name, out_shape=jax.ShapeDtypeStruct((r, LANES), F32), in_specs=[vm], out_specs=vm,
        scratch_shapes=[pltpu.VMEM((N_DEV, r, LANES), F32), pltpu.SemaphoreType.DMA((N_DEV - 1,)),
                        pltpu.SemaphoreType.DMA((N_DEV - 1,))],
    )(buf)


def _row_tile(rows, cols, n_arrays, budget=20 * 1024 * 1024):
    best = 8 if rows % 8 == 0 else rows
    tr = 8
    while tr <= rows:
        if rows % tr == 0 and tr * cols * 4 * n_arrays * 2 <= budget:
            best = tr
        tr *= 2
    return best


def _add_slot_layer(full, other, *, name):
    _, _, r, cdim = full.shape
    tr = _row_tile(r, cdim, 4)

    def body(c_ref, a_ref, b_ref, o_ref, ob_ref):
        sm = a_ref[...] + b_ref[...]
        o_ref[...] = sm
        ob_ref[...] = sm.astype(BF16)

    c = lax.axis_index("c").astype(jnp.int32).reshape(1)
    blk = pl.BlockSpec((None, tr, cdim), lambda k, i, c_ref: (k, i, 0))
    return pl.pallas_call(
        body, name=name,
        grid_spec=pltpu.PrefetchScalarGridSpec(
            num_scalar_prefetch=1, grid=(N_CHIPS, r // tr),
            in_specs=[pl.BlockSpec((None, None, tr, cdim), lambda k, i, c_ref: (k, c_ref[0], i, 0)), blk],
            out_specs=[blk, blk]),
        out_shape=[jax.ShapeDtypeStruct((N_CHIPS, r, cdim), F32), jax.ShapeDtypeStruct((N_CHIPS, r, cdim), BF16)],
        compiler_params=_cp(("parallel", "parallel")),
    )(c, full, other)


def _sum_slots(parts, pair, *, name):
    _, r, cdim = parts.shape
    tr = _row_tile(r, cdim, 5)

    def body(me_ref, p_ref, own_ref, o_ref):
        me = me_ref[0]
        acc = None
        for k in range(N_CHIPS):
            term = jnp.where(me == k, own_ref[...], p_ref[k].astype(F32))
            acc = term if acc is None else acc + term
        o_ref[...] = acc

    me = (2 * lax.axis_index("x") + lax.axis_index("y")).astype(jnp.int32).reshape(1)
    return pl.pallas_call(
        body, name=name,
        grid_spec=pltpu.PrefetchScalarGridSpec(
            num_scalar_prefetch=1, grid=(r // tr,),
            in_specs=[pl.BlockSpec((N_CHIPS, tr, cdim), lambda i, me_ref: (0, i, 0)),
                      pl.BlockSpec((None, tr, cdim), lambda i, me_ref: (me_ref[0], i, 0))],
            out_specs=pl.BlockSpec((tr, cdim), lambda i, me_ref: (i, 0))),
        out_shape=jax.ShapeDtypeStruct((r, cdim), F32),
        compiler_params=_cp(("parallel",)),
    )(me, parts, pair)


def _adamw(w, g, m, v, *, name):
    lead, (r, cdim) = w.shape[:-2], w.shape[-2:]
    nl = len(lead)
    tr = _row_tile(r, cdim, 7)
    tc = cdim
    if tr < 64 < r and cdim % LANES == 0:
        tr, tc = r, LANES
    c1 = 1.0 - ADAM_B1 ** ADAM_STEP
    c2 = 1.0 - ADAM_B2 ** ADAM_STEP

    def body(w_ref, g_ref, m_ref, v_ref, d_ref, nm_ref, nv_ref):
        gv = g_ref[...]
        mn = ADAM_B1 * m_ref[...] + (1.0 - ADAM_B1) * gv
        vn = ADAM_B2 * v_ref[...] + (1.0 - ADAM_B2) * (gv * gv)
        nm_ref[...] = mn
        nv_ref[...] = vn
        d_ref[...] = -ADAM_LR * ((mn / c1) / (jnp.sqrt(vn / c2) + ADAM_EPS) + ADAM_WD * w_ref[...])

    blk = pl.BlockSpec((None,) * nl + (tr, tc), lambda *ids: ids[:nl] + (ids[nl], ids[nl + 1]))
    sh = jax.ShapeDtypeStruct(w.shape, F32)
    return pl.pallas_call(
        body, name=name, grid=lead + (r // tr, cdim // tc), in_specs=[blk] * 4, out_specs=[blk] * 3,
        out_shape=[sh] * 3, compiler_params=_cp(("parallel",) * (nl + 2)),
    )(w, g, m, v)


_SMALL = ("norm_w", "conv_b", "dt_bias", "a_log", "d_skip", "ssm_norm_w", "sinks", "f_bias", "final_norm_w",
          "conv_w", "gate_bias")


def _pack(vals):
    flat = jnp.concatenate([v.reshape(-1) for v in vals])
    rows = -(-flat.shape[0] // LANES)
    rows = -(-rows // 8) * 8
    return jnp.pad(flat, (0, rows * LANES - flat.shape[0])).reshape(rows, LANES)


def _unpack(buf, shapes):
    flat = buf.reshape(-1)
    out, off = [], 0
    for sh in shapes:
        sz = int(np.prod(sh))
        out.append(flat[off:off + sz].reshape(sh))
        off += sz
    return out


def kernel(x, norm_w, w_in, conv_w, conv_b, dt_bias, a_log, d_skip, ssm_norm_w, sinks, f_bias, gate_bias, w_proj, w_out, final_norm_w, loss_target, m_norm_w, m_w_in, m_conv_w, m_conv_b, m_dt_bias, m_a_log, m_d_skip, m_ssm_norm_w, m_sinks, m_f_bias, m_gate_bias, m_w_proj, m_w_out, m_final_norm_w, v_norm_w, v_w_in, v_conv_w, v_conv_b, v_dt_bias, v_a_log, v_d_skip, v_ssm_norm_w, v_sinks, v_f_bias, v_gate_bias, v_w_proj, v_w_out, v_final_norm_w):
    depth = w_in.shape[0]
    chip = 2 * lax.axis_index("x") + lax.axis_index("y")

    own = [w_in.astype(BF16), w_proj.astype(BF16), w_out.astype(BF16), conv_w, gate_bias]
    gathered = _gather_weights(own, name="gather_weights")

    def whole(a, li, axis):
        return jnp.concatenate([jnp.where(chip == k, own[a][li], gathered[a][k, li]) for k in range(N_CHIPS)],
                               axis=axis)

    wls = []
    for li in range(depth):
        wls.append(dict(
            norm_w=norm_w[li], w_in=_pad_w_in(whole(0, li, 1)),
            conv_w=whole(3, li, 1), conv_b=conv_b[li], dt_bias=dt_bias[li], a_log=a_log[li], d_skip=d_skip[li],
            ssm_norm_w=ssm_norm_w[li], sinks=sinks[li], f_bias=f_bias[li], gate_bias=whole(4, li, 1),
            w_proj=whole(1, li, 1),
            w_out=whole(2, li, 0)))

    loss_part, grad_x, grads, d_final = _local_step(x, loss_target, wls, final_norm_w)
    loss = lax.psum(loss_part, ("x", "y", "c"))

    c_in = w_in.shape[2]
    r_proj = w_proj.shape[2]
    r_out = w_out.shape[1]
    full_in = jnp.stack([jnp.stack([grads[li]["w_in"][:, k * c_in:(k + 1) * c_in] for li in range(depth)])
                         for k in range(N_CHIPS)])
    full_proj = jnp.stack([jnp.stack([grads[li]["w_proj"][:, k * r_proj:(k + 1) * r_proj].reshape(-1, D_MODEL)
                                      for li in range(depth)]) for k in range(N_CHIPS)])
    full_out = jnp.stack([jnp.stack([grads[li]["w_out"][k * r_out:(k + 1) * r_out] for li in range(depth)])
                          for k in range(N_CHIPS)])
    fulls = [full_in, full_proj, full_out]
    others = _pair_exchange(fulls, name="grad_pair_exchange")
    pair = [_add_slot_layer(f, o, name=f"grad_pair_add{i}") for i, (f, o) in enumerate(zip(fulls, others))]
    parts = _chip_exchange([p[1] for p in pair], name="grad_chip_exchange")
    mine = [_sum_slots(p, pr[0], name=f"grad_slot_sum{i}") for i, (p, pr) in enumerate(zip(parts, pair))]
    theirs = _pair_share(mine, name="grad_pair_share")
    core = lax.axis_index("c")
    red_in, red_proj, red_out = [jnp.stack([jnp.where(core == li, m, t) for li in range(depth)])
                                 for m, t in zip(mine, theirs)]
    grad_w_in = red_in
    grad_w_proj = red_proj.reshape(w_proj.shape)
    grad_w_out = red_out

    small_full = {
        "norm_w": jnp.stack([g["norm_w"] for g in grads]), "conv_b": jnp.stack([g["conv_b"] for g in grads]),
        "dt_bias": jnp.stack([g["dt_bias"] for g in grads]), "a_log": jnp.stack([g["a_log"] for g in grads]),
        "d_skip": jnp.stack([g["d_skip"] for g in grads]),
        "ssm_norm_w": jnp.stack([g["ssm_norm_w"] for g in grads]),
        "sinks": jnp.stack([g["sinks"] for g in grads]), "f_bias": jnp.stack([g["f_bias"] for g in grads]),
        "final_norm_w": d_final,
        "conv_w": jnp.stack([g["conv_w"] for g in grads]), "gate_bias": jnp.stack([g["gate_bias"] for g in grads])}
    shapes = [small_full[k].shape for k in _SMALL]
    summed = _unpack(_allreduce_small(_pack([small_full[k] for k in _SMALL]), name="allreduce_small"), shapes)
    gsmall = dict(zip(_SMALL, summed))
    gsmall["conv_w"] = lax.dynamic_slice_in_dim(gsmall["conv_w"], chip * conv_w.shape[2], conv_w.shape[2], axis=2)
    gsmall["gate_bias"] = lax.dynamic_slice_in_dim(gsmall["gate_bias"], chip * gate_bias.shape[2],
                                                   gate_bias.shape[2], axis=2)

    w_small = dict(norm_w=norm_w, conv_b=conv_b, dt_bias=dt_bias, a_log=a_log, d_skip=d_skip,
                   ssm_norm_w=ssm_norm_w, sinks=sinks, f_bias=f_bias, final_norm_w=final_norm_w, conv_w=conv_w,
                   gate_bias=gate_bias)
    m_small = dict(norm_w=m_norm_w, conv_b=m_conv_b, dt_bias=m_dt_bias, a_log=m_a_log, d_skip=m_d_skip,
                   ssm_norm_w=m_ssm_norm_w, sinks=m_sinks, f_bias=m_f_bias, final_norm_w=m_final_norm_w,
                   conv_w=m_conv_w, gate_bias=m_gate_bias)
    v_small = dict(norm_w=v_norm_w, conv_b=v_conv_b, dt_bias=v_dt_bias, a_log=v_a_log, d_skip=v_d_skip,
                   ssm_norm_w=v_ssm_norm_w, sinks=v_sinks, f_bias=v_f_bias, final_norm_w=v_final_norm_w,
                   conv_w=v_conv_w, gate_bias=v_gate_bias)
    sshapes = [w_small[k].shape for k in _SMALL]
    ds, ms, vs = _adamw(_pack([w_small[k] for k in _SMALL]), _pack([gsmall[k] for k in _SMALL]),
                        _pack([m_small[k] for k in _SMALL]), _pack([v_small[k] for k in _SMALL]), name="adamw_small")
    delta = dict(zip(_SMALL, _unpack(ds, sshapes)))
    new_m = dict(zip(_SMALL, _unpack(ms, sshapes)))
    new_v = dict(zip(_SMALL, _unpack(vs, sshapes)))
    grad = dict(gsmall)
    for nm, w, g, m, v in (("w_proj", w_proj, grad_w_proj, m_w_proj, v_w_proj),
                           ("w_out", w_out, grad_w_out, m_w_out, v_w_out)):
        grad[nm] = g
        delta[nm], new_m[nm], new_v[nm] = _adamw(w, g, m, v, name=f"adamw_{nm}")
    tview = lambda a: jnp.transpose(a, (0, 2, 1))
    grad["w_in"] = grad_w_in
    delta["w_in"], new_m["w_in"], new_v["w_in"] = [
        tview(a) for a in _adamw(tview(w_in), tview(grad_w_in), tview(m_w_in), tview(v_w_in), name="adamw_w_in")]

    order = ("norm_w", "w_in", "conv_w", "conv_b", "dt_bias", "a_log", "d_skip", "ssm_norm_w", "sinks", "f_bias",
             "gate_bias", "w_proj", "w_out", "final_norm_w")
    return (loss, grad_x, *[grad[k] for k in order], *[delta[k] for k in order],
            *[new_m[k] for k in order], *[new_v[k] for k in order])
```

```python
import numpy as np
import jax
import jax.numpy as jnp
from jax import lax
from jax.experimental import pallas as pl
from jax.experimental.pallas import tpu as pltpu

F32 = jnp.float32
BF16 = jnp.bfloat16
HIGHEST = lax.Precision.HIGHEST
MESH = pl.DeviceIdType.MESH

D_MODEL = 1024
HEAD_DIM = 64
N_HEADS = 16
N_GROUPS = 4
SSM_STATE = 128
CHUNK = 128
CONV_WIDTH = 4
CONV_DIM = 2048
ROPE_THETA = 10000.0
NORM_EPS = 1e-6
LANES = 128
N_CHIPS = 4
N_DEV = 8

ADAM_LR = 0.001
ADAM_B1 = 0.9
ADAM_B2 = 0.999
ADAM_EPS = 1e-08
ADAM_WD = 0.01
ADAM_STEP = 10

_REF_COLS = {}
_off = 0
for _n, _s in (("xbc", 2048), ("a_z", 1024), ("a_dt", 16), ("b_q", 1024), ("b_k", 256), ("b_v", 256),
               ("b_z", 1024), ("c_q", 1024), ("c_k", 1024), ("c_v", 1024), ("c_f", 16), ("c_z", 1024),
               ("gates", 3072)):
    _REF_COLS[_n] = (_off, _s)
    _off += _s

_PAD_ORDER = (("gates", 3072), ("xbc", 2048), ("a_z", 1024), ("b_q", 1024), ("b_z", 1024), ("c_q", 1024),
              ("c_k", 1024), ("c_v", 1024), ("c_z", 1024), ("b_k", 256), ("b_v", 256), ("a_dt", 512),
              ("c_f", 128))
_PAD_COLS = {}
_off = 0
for _n, _s in _PAD_ORDER:
    _PAD_COLS[_n] = (_off, _s)
    _off += _s
N_USED = _off
N_PAD = 13824


def _cp(sem, vmem_mb=48):
    return pltpu.CompilerParams(dimension_semantics=sem, vmem_limit_bytes=vmem_mb * 1024 * 1024)


def _dot(a, b, dims=((1,), (0,)), precision=None):
    return lax.dot_general(a, b, (dims, ((), ())), preferred_element_type=F32, precision=precision)


def _dot_nt(a, b):
    return _dot(a, b, ((1,), (1,)))


def _dot_tn(a, b):
    return _dot(a, b, ((0,), (0,)))


def _col(v, idx):
    lane = lax.broadcasted_iota(jnp.int32, v.shape, 1)
    return jnp.sum(jnp.where(lane == idx, v, 0.0), axis=1, keepdims=True)


def _row(v, idx):
    row = lax.broadcasted_iota(jnp.int32, v.shape, 0)
    return jnp.sum(jnp.where(row == idx, v, 0.0), axis=0, keepdims=True)


def _iota_col():
    return lax.broadcasted_iota(jnp.int32, (CHUNK, 1), 0)


def _iota_row():
    return lax.broadcasted_iota(jnp.int32, (1, LANES), 1)


def _sigmoid(x):
    return 1.0 / (1.0 + jnp.exp(-x))


def _softplus(x):
    return jnp.maximum(x, 0.0) + jnp.log(1.0 + jnp.exp(-jnp.abs(x)))


def _pad_w_in(w):
    parts = []
    for name, size in _PAD_ORDER:
        s0, sz = _REF_COLS[name]
        seg = w[:, s0:s0 + sz]
        if name == "a_dt":
            seg = jnp.pad(seg.reshape(-1, N_GROUPS, 4), ((0, 0), (0, 0), (0, LANES - 4))).reshape(-1, 512)
        elif name == "c_f":
            seg = jnp.pad(seg, ((0, 0), (0, LANES - 16)))
        parts.append(seg)
    parts.append(jnp.zeros((w.shape[0], N_PAD - N_USED), w.dtype))
    return jnp.concatenate(parts, axis=1)


def _unpad_w_in(wp):
    segs = {}
    for name, _ in _PAD_ORDER:
        p0, psz = _PAD_COLS[name]
        seg = wp[:, p0:p0 + psz]
        if name == "a_dt":
            seg = seg.reshape(-1, N_GROUPS, LANES)[:, :, :4].reshape(-1, 16)
        elif name == "c_f":
            seg = seg[:, :16]
        segs[name] = seg
    order = sorted(_REF_COLS, key=lambda n: _REF_COLS[n][0])
    return jnp.concatenate([segs[n] for n in order], axis=1)


def _group_lanes(v):
    return jnp.pad(v.reshape(N_GROUPS, 1, 4), ((0, 0), (0, 0), (0, LANES - 4)))


def _ungroup_lanes(v):
    return v[:, 0, :4].reshape(16)


def _mm(a, b, *, ta=False, tb=False, tm=512, tn=512, tk=512, out_dtype=F32, name):
    if ta:
        kdim, m = a.shape
    else:
        m, kdim = a.shape
    if tb:
        n, k2 = b.shape
    else:
        k2, n = b.shape
    assert kdim == k2, (a.shape, b.shape)
    tm, tn, tk = min(tm, m), min(tn, n), min(tk, kdim)
    assert m % tm == 0 and n % tn == 0 and kdim % tk == 0, (m, n, kdim, tm, tn, tk)
    nk = kdim // tk
    a_spec = (pl.BlockSpec((tk, tm), lambda i, j, k: (k, i)) if ta
              else pl.BlockSpec((tm, tk), lambda i, j, k: (i, k)))
    b_spec = (pl.BlockSpec((tn, tk), lambda i, j, k: (j, k)) if tb
              else pl.BlockSpec((tk, tn), lambda i, j, k: (k, j)))
    dims = ((0 if ta else 1,), (1 if tb else 0,))

    def body(a_ref, b_ref, o_ref, acc_ref):
        k = pl.program_id(2)
        p = _dot(a_ref[...].astype(BF16), b_ref[...].astype(BF16), dims)

        @pl.when(k == 0)
        def _():
            acc_ref[...] = p

        @pl.when(k > 0)
        def _():
            acc_ref[...] += p

        @pl.when(k == nk - 1)
        def _():
            o_ref[...] = acc_ref[...].astype(out_dtype)

    return pl.pallas_call(
        body, name=name, grid=(m // tm, n // tn, nk),
        in_specs=[a_spec, b_spec], out_specs=pl.BlockSpec((tm, tn), lambda i, j, k: (i, j)),
        out_shape=jax.ShapeDtypeStruct((m, n), out_dtype),
        scratch_shapes=[pltpu.VMEM((tm, tn), F32)],
        compiler_params=_cp(("parallel", "parallel", "arbitrary")),
    )(a, b)


def _rms_fwd(x, w, *, name, tm=512):
    t, d = x.shape

    def body(x_ref, w_ref, o_ref, ot_ref):
        xv = x_ref[...]
        r = lax.rsqrt(jnp.mean(xv * xv, axis=1, keepdims=True) + NORM_EPS)
        h = xv * r * w_ref[...]
        o_ref[...] = h.astype(BF16)
        ot_ref[...] = h.T.astype(BF16)

    return pl.pallas_call(
        body, name=name, grid=(t // tm,),
        in_specs=[pl.BlockSpec((tm, d), lambda i: (i, 0)), pl.BlockSpec((1, d), lambda i: (0, 0))],
        out_specs=[pl.BlockSpec((tm, d), lambda i: (i, 0)), pl.BlockSpec((d, tm), lambda i: (0, i))],
        out_shape=[jax.ShapeDtypeStruct((t, d), BF16), jax.ShapeDtypeStruct((d, t), BF16)],
        compiler_params=_cp(("parallel",)),
    )(x, w.reshape(1, d))


def _rms_bwd(x, w, dh, dres, *, name, tm=512):
    t, d = x.shape

    def body(x_ref, w_ref, dh_ref, dres_ref, dx_ref, dw_ref):
        xv = x_ref[...]
        r = lax.rsqrt(jnp.mean(xv * xv, axis=1, keepdims=True) + NORM_EPS)
        xhat = xv * r
        dhv = dh_ref[...]
        dxhat = dhv * w_ref[...]
        dx = r * (dxhat - xhat * jnp.mean(dxhat * xhat, axis=1, keepdims=True))
        dx_ref[...] = dres_ref[...] + dx

        @pl.when(pl.program_id(0) == 0)
        def _():
            dw_ref[...] = jnp.zeros_like(dw_ref)

        dw_ref[...] += jnp.sum(dhv * xhat, axis=0, keepdims=True)

    return pl.pallas_call(
        body, name=name, grid=(t // tm,),
        in_specs=[pl.BlockSpec((tm, d), lambda i: (i, 0)), pl.BlockSpec((1, d), lambda i: (0, 0)),
                  pl.BlockSpec((tm, d), lambda i: (i, 0)), pl.BlockSpec((tm, d), lambda i: (i, 0))],
        out_specs=[pl.BlockSpec((tm, d), lambda i: (i, 0)), pl.BlockSpec((1, d), lambda i: (0, 0))],
        out_shape=[jax.ShapeDtypeStruct((t, d), F32), jax.ShapeDtypeStruct((1, d), F32)],
        compiler_params=_cp(("arbitrary",)),
    )(x, w.reshape(1, d), dh, dres)


def _final_loss(x, w, target, *, name, tm=512):
    t, d = x.shape

    def body(x_ref, w_ref, t_ref, loss_ref, dx_ref, dw_ref):
        xv = x_ref[...]
        wv = w_ref[...]
        r = lax.rsqrt(jnp.mean(xv * xv, axis=1, keepdims=True) + NORM_EPS)
        xhat = xv * r
        err = xhat * wv - t_ref[...]
        dy = err * (1.0 / d)
        dxhat = dy * wv
        dx_ref[...] = r * (dxhat - xhat * jnp.mean(dxhat * xhat, axis=1, keepdims=True))

        @pl.when(pl.program_id(0) == 0)
        def _():
            dw_ref[...] = jnp.zeros_like(dw_ref)
            loss_ref[...] = jnp.zeros_like(loss_ref)

        dw_ref[...] += jnp.sum(dy * xhat, axis=0, keepdims=True)
        part = 0.5 * jnp.sum(jnp.mean(err * err, axis=1, keepdims=True), axis=0, keepdims=True)
        loss_ref[...] += jnp.broadcast_to(part, loss_ref.shape)

    return pl.pallas_call(
        body, name=name, grid=(t // tm,),
        in_specs=[pl.BlockSpec((tm, d), lambda i: (i, 0)), pl.BlockSpec((1, d), lambda i: (0, 0)),
                  pl.BlockSpec((tm, d), lambda i: (i, 0))],
        out_specs=[pl.BlockSpec((8, LANES), lambda i: (0, 0)), pl.BlockSpec((tm, d), lambda i: (i, 0)),
                   pl.BlockSpec((1, d), lambda i: (0, 0))],
        out_shape=[jax.ShapeDtypeStruct((8, LANES), F32), jax.ShapeDtypeStruct((t, d), F32),
                   jax.ShapeDtypeStruct((1, d), F32)],
        compiler_params=_cp(("arbitrary",)),
    )(x, w.reshape(1, d), target)


_CB = 128


def _conv_pre(u, w_ref, b_ref):
    s = u.shape[0]
    row = lax.broadcasted_iota(jnp.int32, u.shape, 0)
    pre = b_ref[...] + w_ref[CONV_WIDTH - 1:CONV_WIDTH, :] * u
    for sh in range(1, CONV_WIDTH):
        shifted = jnp.where(row >= sh, pltpu.roll(u, sh, 0), 0.0)
        pre = pre + w_ref[CONV_WIDTH - 1 - sh:CONV_WIDTH - sh, :] * shifted
    return pre


def _conv_fwd(proj3, cw, cb, *, name):
    b, s, _ = proj3.shape
    c0 = _PAD_COLS["xbc"][0] // _CB

    def body(u_ref, w_ref, b_ref, o_ref):
        pre = _conv_pre(u_ref[...].astype(F32), w_ref, b_ref)
        o_ref[...] = pre * _sigmoid(pre)

    return pl.pallas_call(
        body, name=name, grid=(b, CONV_DIM // _CB),
        in_specs=[pl.BlockSpec((None, s, _CB), lambda i, j: (i, 0, c0 + j)),
                  pl.BlockSpec((CONV_WIDTH, _CB), lambda i, j: (0, j)),
                  pl.BlockSpec((1, _CB), lambda i, j: (0, j))],
        out_specs=pl.BlockSpec((None, s, _CB), lambda i, j: (i, 0, j)),
        out_shape=jax.ShapeDtypeStruct((b, s, CONV_DIM), F32),
        compiler_params=_cp(("parallel", "parallel")),
    )(proj3, cw, cb.reshape(1, CONV_DIM))


def _conv_bwd(proj3, cw, cb, dact, *, name):
    b, s, _ = proj3.shape
    c0 = _PAD_COLS["xbc"][0] // _CB

    def body(u_ref, w_ref, b_ref, da_ref, du_ref, dwb_ref):
        u = u_ref[...].astype(F32)
        pre = _conv_pre(u, w_ref, b_ref)
        sg = _sigmoid(pre)
        dpre = da_ref[...] * (sg * (1.0 + pre * (1.0 - sg)))
        row = lax.broadcasted_iota(jnp.int32, u.shape, 0)
        du = w_ref[CONV_WIDTH - 1:CONV_WIDTH, :] * dpre
        rows = [jnp.sum(dpre * u, axis=0, keepdims=True)]
        for sh in range(1, CONV_WIDTH):
            fwd_shift = jnp.where(row < s - sh, pltpu.roll(dpre, s - sh, 0), 0.0)
            du = du + w_ref[CONV_WIDTH - 1 - sh:CONV_WIDTH - sh, :] * fwd_shift
            ush = jnp.where(row >= sh, pltpu.roll(u, sh, 0), 0.0)
            rows.append(jnp.sum(dpre * ush, axis=0, keepdims=True))
        du_ref[...] = du.astype(BF16)

        @pl.when(pl.program_id(1) == 0)
        def _():
            dwb_ref[...] = jnp.zeros_like(dwb_ref)

        for sh in range(CONV_WIDTH):
            k = CONV_WIDTH - 1 - sh
            dwb_ref[k:k + 1, :] += rows[sh]
        dwb_ref[CONV_WIDTH:CONV_WIDTH + 1, :] += jnp.sum(dpre, axis=0, keepdims=True)

    return pl.pallas_call(
        body, name=name, grid=(CONV_DIM // _CB, b),
        in_specs=[pl.BlockSpec((None, s, _CB), lambda j, i: (i, 0, c0 + j)),
                  pl.BlockSpec((CONV_WIDTH, _CB), lambda j, i: (0, j)),
                  pl.BlockSpec((1, _CB), lambda j, i: (0, j)),
                  pl.BlockSpec((None, s, _CB), lambda j, i: (i, 0, j))],
        out_specs=[pl.BlockSpec((None, s, _CB), lambda j, i: (i, 0, j)),
                   pl.BlockSpec((8, _CB), lambda j, i: (0, j))],
        out_shape=[jax.ShapeDtypeStruct((b, s, CONV_DIM), BF16), jax.ShapeDtypeStruct((8, CONV_DIM), F32)],
        compiler_params=_cp(("parallel", "arbitrary")),
    )(proj3, cw, cb.reshape(1, CONV_DIM), dact)


def _ssd_common(dt_ref, dtb_ref, alog_ref):
    row = lax.broadcasted_iota(jnp.int32, (CHUNK, CHUNK), 0)
    lane = lax.broadcasted_iota(jnp.int32, (CHUNK, CHUNK), 1)
    causal = row >= lane
    tri = causal.astype(F32)
    dtv = _softplus(dt_ref[...] + dtb_ref[...])
    a_row = -jnp.exp(alog_ref[...])
    acum = _dot(tri, dtv * a_row, precision=HIGHEST)
    return row, lane, causal, dtv, a_row, acum, acum.T


def _ssd_pair(pp, x, dtv, acum, acum_t, causal, lane, row):
    lo = lane < HEAD_DIM
    r0, r1 = 2 * pp, 2 * pp + 1
    dtp = jnp.where(lo, _col(dtv, r0), _col(dtv, r1))
    ac0, ac1 = _col(acum, r0), _col(acum, r1)
    ar0, ar1 = _row(acum_t, r0), _row(acum_t, r1)
    d0 = jnp.where(causal, jnp.exp(jnp.where(causal, ac0 - ar0, 0.0)), 0.0)
    d1 = jnp.where(causal, jnp.exp(jnp.where(causal, ac1 - ar1, 0.0)), 0.0)
    al0, al1 = _col(ar0, CHUNK - 1), _col(ar1, CHUNK - 1)
    eac = jnp.where(lo, jnp.exp(ac0), jnp.exp(ac1))
    dsp = jnp.where(lo, jnp.exp(al0 - ac0), jnp.exp(al1 - ac1))
    eal = jnp.where(_iota_col() < HEAD_DIM, jnp.exp(al0), jnp.exp(al1))
    return lo, dtp, x * dtp, d0, d1, al0, al1, eac, dsp, eal


def _ssd_fwd(proj3, gates3, xact3, dtb, alog, dsk, nw, *, name):
    b, s, _ = proj3.shape
    nc = s // CHUNK
    dt0 = 0
    z0 = _PAD_COLS["a_z"][0] // D_MODEL

    def body(xs_ref, bm_ref, cm_ref, dt_ref, z_ref, dtb_ref, alog_ref, dsk_ref, nw_ref,
             ya_ref, ypre_ref, hst_ref, h_scr):
        @pl.when(pl.program_id(1) == 0)
        def _():
            h_scr[...] = jnp.zeros_like(h_scr)

        for g in range(N_GROUPS):
            w256 = pl.ds(256 * g, 256)
            w128 = pl.ds(LANES * g, LANES)
            group(xs_ref.at[:, w256], bm_ref.at[:, w128], cm_ref.at[:, w128], dt_ref.at[:, w128],
                  z_ref.at[:, w256], dtb_ref.at[g], alog_ref.at[g], dsk_ref.at[g], nw_ref.at[g],
                  ya_ref.at[:, w256], ypre_ref.at[:, w256], hst_ref.at[g], h_scr.at[g])

    def group(xs_ref, bm_ref, cm_ref, dt_ref, z_ref, dtb_ref, alog_ref, dsk_ref, nw_ref,
              ya_ref, ypre_ref, hst_ref, h_scr):
        row, lane, causal, dtv, a_row, acum, acum_t = _ssd_common(dt_ref, dtb_ref, alog_ref)
        bb = bm_ref[...].astype(BF16)
        cb = cm_ref[...].astype(BF16)
        cbm = _dot_nt(cb, bb)
        hst_ref[...] = h_scr[...]
        dskv = dsk_ref[...]
        for pp in range(2):
            x = xs_ref[:, LANES * pp:LANES * (pp + 1)]
            lo, dtp, xd, d0, d1, al0, al1, eac, dsp, eal = _ssd_pair(pp, x, dtv, acum, acum_t, causal, lane, row)
            xdb = xd.astype(BF16)
            y = jnp.where(lo, _dot((cbm * d0).astype(BF16), xdb), _dot((cbm * d1).astype(BF16), xdb))
            h = h_scr[pp]
            y = y + eac * _dot_nt(cb, h.astype(BF16))
            h_scr[pp] = h * eal + _dot_tn((xd * dsp).astype(BF16), bb)
            dskp = jnp.where((_iota_row() < HEAD_DIM), _col(dskv, 2 * pp), _col(dskv, 2 * pp + 1))
            ypre_ref[:, LANES * pp:LANES * (pp + 1)] = y + x * dskp
        ypre = ypre_ref[...]
        z = z_ref[...].astype(F32)
        yg = ypre * (z * _sigmoid(z))
        rstd = lax.rsqrt(jnp.sum(yg * yg, axis=1, keepdims=True) * (1.0 / 256.0) + NORM_EPS)
        ya_ref[...] = (yg * rstd * nw_ref[...]).astype(BF16)

    g = N_GROUPS
    par = pl.BlockSpec((g, 1, LANES), lambda i, c: (0, 0, 0))
    wide = pl.BlockSpec((None, CHUNK, D_MODEL), lambda i, c: (i, c, 0))
    return pl.pallas_call(
        body, name=name, grid=(b, nc),
        in_specs=[wide,
                  pl.BlockSpec((None, CHUNK, 512), lambda i, c: (i, c, 2)),
                  pl.BlockSpec((None, CHUNK, 512), lambda i, c: (i, c, 3)),
                  pl.BlockSpec((None, CHUNK, 512), lambda i, c: (i, c, dt0)),
                  pl.BlockSpec((None, CHUNK, D_MODEL), lambda i, c: (i, c, z0)),
                  par, par, par,
                  pl.BlockSpec((g, 1, 256), lambda i, c: (0, 0, 0))],
        out_specs=[wide, wide,
                   pl.BlockSpec((None, None, g, 2, CHUNK, SSM_STATE), lambda i, c: (i, c, 0, 0, 0, 0))],
        out_shape=[jax.ShapeDtypeStruct((b, s, D_MODEL), BF16), jax.ShapeDtypeStruct((b, s, D_MODEL), F32),
                   jax.ShapeDtypeStruct((b, nc, g, 2, CHUNK, SSM_STATE), F32)],
        scratch_shapes=[pltpu.VMEM((g, 2, CHUNK, SSM_STATE), F32)],
        compiler_params=_cp(("parallel", "arbitrary")),
    )(xact3, xact3, xact3, gates3, proj3, dtb, alog, dsk, nw)


def _ssd_bwd(proj3, gates3, xact3, dtb, alog, dsk, nw, ypre3, hst, dya3, *, name):
    b, s, _ = proj3.shape
    nc = s // CHUNK
    dt0 = 0
    z0 = _PAD_COLS["a_z"][0] // D_MODEL

    def body(xs_ref, bm_ref, cm_ref, dt_ref, z_ref, dtb_ref, alog_ref, dsk_ref, nw_ref, ypre_ref, hst_ref,
             dya_ref, dact_ref, dz_ref, ddt_ref, ddtb_ref, dalog_ref, ddsk_ref, dnw_ref, dh_scr):
        first = jnp.logical_and(pl.program_id(0) == 0, pl.program_id(1) == 0)

        @pl.when(first)
        def _():
            ddtb_ref[...] = jnp.zeros_like(ddtb_ref)
            dalog_ref[...] = jnp.zeros_like(dalog_ref)
            ddsk_ref[...] = jnp.zeros_like(ddsk_ref)
            dnw_ref[...] = jnp.zeros_like(dnw_ref)

        @pl.when(pl.program_id(1) == 0)
        def _():
            dh_scr[...] = jnp.zeros_like(dh_scr)

        for g in range(N_GROUPS):
            w256 = pl.ds(256 * g, 256)
            w128 = pl.ds(LANES * g, LANES)
            group(xs_ref.at[:, w256], bm_ref.at[:, w128], cm_ref.at[:, w128], dt_ref.at[:, w128],
                  z_ref.at[:, w256], dtb_ref.at[g], alog_ref.at[g], dsk_ref.at[g], nw_ref.at[g],
                  ypre_ref.at[:, w256], hst_ref.at[g], dya_ref.at[:, w256],
                  dact_ref.at[:, w256], dact_ref.at[:, pl.ds(D_MODEL + LANES * g, LANES)],
                  dact_ref.at[:, pl.ds(D_MODEL + 512 + LANES * g, LANES)], dz_ref.at[:, w256], ddt_ref.at[:, w128],
                  ddtb_ref.at[g], dalog_ref.at[g], ddsk_ref.at[g], dnw_ref.at[g], dh_scr.at[g])

    def group(xs_ref, bm_ref, cm_ref, dt_ref, z_ref, dtb_ref, alog_ref, dsk_ref, nw_ref, ypre_ref, hst_ref,
              dya_ref, dxs_ref, dbm_ref, dcm_ref, dz_ref, ddt_ref, ddtb_ref, dalog_ref, ddsk_ref, dnw_ref,
              dh_scr):
        row, lane, causal, dtv, a_row, acum, acum_t = _ssd_common(dt_ref, dtb_ref, alog_ref)
        lane1 = _iota_row()
        bb = bm_ref[...].astype(BF16)
        cb = cm_ref[...].astype(BF16)
        cbm = _dot_nt(cb, bb)

        z = z_ref[...].astype(F32)
        ypre = ypre_ref[...]
        dya = dya_ref[...]
        sz = _sigmoid(z)
        silu = z * sz
        yg = ypre * silu
        rstd = lax.rsqrt(jnp.sum(yg * yg, axis=1, keepdims=True) * (1.0 / 256.0) + NORM_EPS)
        dnw_ref[...] += jnp.sum(dya * yg * rstd, axis=0, keepdims=True)
        dn = dya * nw_ref[...]
        dyg = rstd * dn - yg * (rstd * rstd * rstd * (1.0 / 256.0)) * jnp.sum(dn * yg, axis=1, keepdims=True)
        dz_ref[...] = (dyg * ypre * (sz * (1.0 + z * (1.0 - sz)))).astype(BF16)
        dy_all = dyg * silu

        dskv = dsk_ref[...]
        da_cols = jnp.zeros((CHUNK, LANES), F32)
        dxt_cols = jnp.zeros((CHUNK, LANES), F32)
        ddsk_row = jnp.zeros((1, LANES), F32)
        dcb = jnp.zeros((CHUNK, CHUNK), F32)
        dc = jnp.zeros((CHUNK, SSM_STATE), F32)
        db = jnp.zeros((CHUNK, SSM_STATE), F32)
        last = _iota_col() == CHUNK - 1
        for pp in range(2):
            r0, r1 = 2 * pp, 2 * pp + 1
            x = xs_ref[:, LANES * pp:LANES * (pp + 1)]
            dy = dy_all[:, LANES * pp:LANES * (pp + 1)]
            lo, dtp, xd, d0, d1, al0, al1, eac, dsp, eal = _ssd_pair(pp, x, dtv, acum, acum_t, causal, lane, row)
            w0, w1 = cbm * d0, cbm * d1
            w0b, w1b = w0.astype(BF16), w1.astype(BF16)
            xdb = xd.astype(BF16)
            dyb = dy.astype(BF16)
            h = hst_ref[pp]
            dhn = dh_scr[pp]
            hb = h.astype(BF16)
            dhb = dhn.astype(BF16)
            g0 = _dot_nt(jnp.where(lo, dy, 0.0).astype(BF16), xdb)
            g1 = _dot_nt(jnp.where(lo, 0.0, dy).astype(BF16), xdb)
            dcb = dcb + g0 * d0 + g1 * d1
            m0, m1 = g0 * w0, g1 * w1
            bdh = _dot_nt(bb, dhb)
            dxd = jnp.where(lo, _dot_tn(w0b, dyb), _dot_tn(w1b, dyb)) + dsp * bdh
            ch = _dot_nt(cb, hb)
            edy = eac * dy
            edyb = edy.astype(BF16)
            xds = xd * dsp
            dc = dc + _dot(edyb, hb)
            db = db + _dot(xds.astype(BF16), dhb)
            dh_scr[pp] = dhn * eal + _dot_tn(edyb, cb)
            t2 = edy * ch
            t3 = xds * bdh
            dhh = dhn * h
            s4_0 = jnp.sum(jnp.sum(jnp.where(row < HEAD_DIM, dhh, 0.0), axis=0, keepdims=True), axis=1, keepdims=True)
            s4_1 = jnp.sum(jnp.sum(dhh, axis=0, keepdims=True), axis=1, keepdims=True) - s4_0
            t23 = t2 - t3
            t23_0 = jnp.sum(jnp.where(lo, t23, 0.0), axis=1, keepdims=True)
            t23_1 = jnp.sum(t23, axis=1, keepdims=True) - t23_0
            c3 = jnp.sum(t3, axis=0, keepdims=True)
            c3_0 = jnp.sum(jnp.where(_iota_row() < HEAD_DIM, c3, 0.0), axis=1, keepdims=True)
            c3_1 = jnp.sum(c3, axis=1, keepdims=True) - c3_0
            dal0 = c3_0 + jnp.exp(al0) * s4_0
            dal1 = c3_1 + jnp.exp(al1) * s4_1
            dac0 = jnp.sum(m0 - m0.T, axis=1, keepdims=True) + t23_0 + jnp.where(last, dal0, 0.0)
            dac1 = jnp.sum(m1 - m1.T, axis=1, keepdims=True) + t23_1 + jnp.where(last, dal1, 0.0)
            da_cols = da_cols + jnp.where(lane == r0, dac0, 0.0) + jnp.where(lane == r1, dac1, 0.0)
            xx = dxd * x
            x0 = jnp.sum(jnp.where(lo, xx, 0.0), axis=1, keepdims=True)
            x1 = jnp.sum(xx, axis=1, keepdims=True) - x0
            dxt_cols = dxt_cols + jnp.where(lane == r0, x0, 0.0) + jnp.where(lane == r1, x1, 0.0)
            dskp = jnp.where((_iota_row() < HEAD_DIM), _col(dskv, r0), _col(dskv, r1))
            dxs_ref[:, LANES * pp:LANES * (pp + 1)] = dxd * dtp + dy * dskp
            yx = jnp.sum(dy * x, axis=0, keepdims=True)
            k0 = jnp.sum(jnp.where((_iota_row() < HEAD_DIM), yx, 0.0), axis=1, keepdims=True)
            k1 = jnp.sum(yx, axis=1, keepdims=True) - k0
            ddsk_row = ddsk_row + jnp.where(lane1 == r0, k0, 0.0) + jnp.where(lane1 == r1, k1, 0.0)
        dcbb = dcb.astype(BF16)
        dcm_ref[...] = dc + _dot(dcbb, bb)
        dbm_ref[...] = db + _dot_tn(dcbb, cb)
        tri_t = (row <= lane).astype(F32)
        dadt = _dot(tri_t, da_cols, precision=HIGHEST)
        ddtv = dadt * a_row + dxt_cols
        dalog_ref[...] += jnp.sum(dadt * dtv, axis=0, keepdims=True) * a_row
        ddt_raw = ddtv * _sigmoid(dt_ref[...] + dtb_ref[...])
        ddt_ref[...] = ddt_raw.astype(BF16)
        ddtb_ref[...] += jnp.sum(ddt_raw, axis=0, keepdims=True)
        ddsk_ref[...] += ddsk_row

    g = N_GROUPS
    rc = lambda c: nc - 1 - c
    par = pl.BlockSpec((g, 1, LANES), lambda i, c: (0, 0, 0))
    parw = pl.BlockSpec((g, 1, 256), lambda i, c: (0, 0, 0))
    wide = pl.BlockSpec((None, CHUNK, D_MODEL), lambda i, c: (i, rc(c), 0))
    blk512 = lambda col: pl.BlockSpec((None, CHUNK, 512), lambda i, c: (i, rc(c), col))
    return pl.pallas_call(
        body, name=name, grid=(b, nc),
        in_specs=[wide, blk512(2), blk512(3), blk512(dt0),
                  pl.BlockSpec((None, CHUNK, D_MODEL), lambda i, c: (i, rc(c), z0)),
                  par, par, par, parw,
                  wide,
                  pl.BlockSpec((None, None, g, 2, CHUNK, SSM_STATE), lambda i, c: (i, rc(c), 0, 0, 0, 0)),
                  wide],
        out_specs=[pl.BlockSpec((None, CHUNK, CONV_DIM), lambda i, c: (i, rc(c), 0)), wide, blk512(0),
                   par, par, par, parw],
        out_shape=[jax.ShapeDtypeStruct((b, s, CONV_DIM), F32), jax.ShapeDtypeStruct((b, s, D_MODEL), BF16),
                   jax.ShapeDtypeStruct((b, s, 512), BF16),
                   jax.ShapeDtypeStruct((g, 1, LANES), F32), jax.ShapeDtypeStruct((g, 1, LANES), F32),
                   jax.ShapeDtypeStruct((g, 1, LANES), F32), jax.ShapeDtypeStruct((g, 1, 256), F32)],
        scratch_shapes=[pltpu.VMEM((g, 2, CHUNK, SSM_STATE), F32)],
        compiler_params=_cp(("arbitrary", "arbitrary")),
    )(xact3, xact3, xact3, gates3, proj3, dtb, alog, dsk, nw, ypre3, hst, dya3)


_FGATE_ROWS = 512


def _fgate_fwd(gates3, fb, *, name):
    b, s, _ = gates3.shape
    rows = min(_FGATE_ROWS, s)
    f0 = _PAD_COLS["a_dt"][1] // LANES

    def body(f_ref, fb_ref, cum_ref, carry):
        @pl.when(pl.program_id(1) == 0)
        def _():
            carry[...] = jnp.zeros_like(carry)

        row = lax.broadcasted_iota(jnp.int32, (rows, rows), 0)
        lane = lax.broadcasted_iota(jnp.int32, (rows, rows), 1)
        tri = (row >= lane).astype(F32)
        lf = -_softplus(-(f_ref[...] + fb_ref[...]))
        cs = _dot(tri, lf, precision=HIGHEST) + carry[0:1, :]
        cum_ref[...] = cs
        carry[0:1, :] = _row(cs, rows - 1)

    return pl.pallas_call(
        body, name=name, grid=(b, s // rows),
        in_specs=[pl.BlockSpec((None, rows, LANES), lambda i, c: (i, c, f0)),
                  pl.BlockSpec((1, LANES), lambda i, c: (0, 0))],
        out_specs=pl.BlockSpec((None, rows, LANES), lambda i, c: (i, c, 0)),
        out_shape=jax.ShapeDtypeStruct((b, s, LANES), F32),
        scratch_shapes=[pltpu.VMEM((8, LANES), F32)],
        compiler_params=_cp(("parallel", "arbitrary")),
    )(gates3, fb)


def _fgate_bwd(gates3, fb, dcum, *, name):
    b, s, _ = gates3.shape
    rows = min(_FGATE_ROWS, s)
    nc = s // rows
    f0 = _PAD_COLS["a_dt"][1] // LANES
    npair = dcum.shape[1]

    def body(f_ref, fb_ref, dc_ref, df_ref, dfb_ref, carry):
        first = jnp.logical_and(pl.program_id(0) == 0, pl.program_id(1) == 0)

        @pl.when(first)
        def _():
            dfb_ref[...] = jnp.zeros_like(dfb_ref)

        @pl.when(pl.program_id(1) == 0)
        def _():
            carry[...] = jnp.zeros_like(carry)

        row = lax.broadcasted_iota(jnp.int32, (rows, rows), 0)
        lane = lax.broadcasted_iota(jnp.int32, (rows, rows), 1)
        tri_t = (row <= lane).astype(F32)
        dc = -jnp.sum(dc_ref[...], axis=0)
        dlf = _dot(tri_t, dc, precision=HIGHEST) + carry[0:1, :]
        carry[0:1, :] = _row(dlf, 0)
        df = dlf * _sigmoid(-(f_ref[...] + fb_ref[...]))
        df_ref[...] = df.astype(BF16)
        dfb_ref[...] += jnp.sum(df, axis=0, keepdims=True)

    return pl.pallas_call(
        body, name=name, grid=(b, nc),
        in_specs=[pl.BlockSpec((None, rows, LANES), lambda i, c: (i, nc - 1 - c, f0)),
                  pl.BlockSpec((1, LANES), lambda i, c: (0, 0)),
                  pl.BlockSpec((None, npair, rows, LANES), lambda i, c: (i, 0, nc - 1 - c, 0))],
        out_specs=[pl.BlockSpec((None, rows, LANES), lambda i, c: (i, nc - 1 - c, 0)),
                   pl.BlockSpec((1, LANES), lambda i, c: (0, 0))],
        out_shape=[jax.ShapeDtypeStruct((b, s, LANES), BF16), jax.ShapeDtypeStruct((1, LANES), F32)],
        scratch_shapes=[pltpu.VMEM((8, LANES), F32)],
        compiler_params=_cp(("arbitrary", "arbitrary")),
    )(gates3, fb, dcum)


_SCALE = HEAD_DIM ** -0.5
_NEG = -1e30


_ST_LSE, _ST_DELTA, _ST_MJ = 0, 2, 8


_SR = 40


def _ck_rep(cum):
    b, s, _ = cum.shape
    t = jnp.transpose(cum[:, :, :N_HEADS], (0, 2, 1)).reshape(b, N_HEADS // 2, 2, s, 1)
    return jnp.broadcast_to(t, (b, N_HEADS // 2, 2, s, LANES))


def _foxt_fwd(proj3, ckrep, *, name, tb):
    b, s, _ = proj3.shape
    nq = s // tb
    assert _ST_MJ + 2 * nq <= _SR
    q0 = _PAD_COLS["c_q"][0] // LANES
    k0 = _PAD_COLS["c_k"][0] // LANES
    v0 = _PAD_COLS["c_v"][0] // LANES
    z0 = _PAD_COLS["c_z"][0] // LANES
    rep = tb // LANES

    def body(q_ref, k_ref, v_ref, z_ref, ck_ref, y_ref, o_ref, st_ref):
        i = pl.program_id(2)
        lane = lax.broadcasted_iota(jnp.int32, (tb, LANES), 1)
        lo = lane < HEAD_DIM
        lo_r = lax.broadcasted_iota(jnp.int32, (LANES, tb), 0) < HEAD_DIM
        srow = lax.broadcasted_iota(jnp.int32, (_SR, tb), 0)
        q = q_ref[...].astype(F32) * _SCALE
        qms = (jnp.where(lo, q, 0.0).astype(BF16), jnp.where(lo, 0.0, q).astype(BF16))
        ones_at = (HEAD_DIM, 0)

        def block(j, carry, diagonal):
            ks = pl.ds(pl.multiple_of(j * tb, tb), tb)
            kb = k_ref[ks, :].astype(BF16)
            v = v_ref[ks, :].astype(F32)
            vts = (jnp.where(lo, v, jnp.where(lane == ones_at[0], 1.0, 0.0)).T.astype(BF16),
                   jnp.where(lo, jnp.where(lane == ones_at[1], 1.0, 0.0), v).T.astype(BF16))
            if diagonal:
                key = lax.broadcasted_iota(jnp.int32, (tb, tb), 0)
                qry = lax.broadcasted_iota(jnp.int32, (tb, tb), 1)
                mask = key <= qry
            ms, ls, acc, st = carry
            new_m, new_l, pvs, alphas = [], [], [], []
            for hh in range(2):
                sc = _dot_nt(kb, qms[hh]) - jnp.tile(ck_ref[hh, ks, :], (1, rep))
                if diagonal:
                    sc = jnp.where(mask, sc, _NEG)
                m_new = jnp.maximum(ms[hh], jnp.max(sc, axis=0, keepdims=True))
                alpha = jnp.exp(ms[hh] - m_new)
                pv = _dot(vts[hh], jnp.exp(sc - m_new).astype(BF16))
                rs = _row(pv[ones_at[hh]:ones_at[hh] + 8, :], 0)
                new_l.append(alpha * ls[hh] + rs)
                new_m.append(m_new)
                pvs.append(pv)
                alphas.append(alpha)
                st = jnp.where(srow == _ST_MJ + 2 * j + hh, m_new, st)
            acc = jnp.where(lo_r, alphas[0] * acc + pvs[0], alphas[1] * acc + pvs[1])
            return (tuple(new_m), tuple(new_l), acc, st)

        neg = jnp.full((1, tb), _NEG, F32)
        zero = jnp.zeros((1, tb), F32)
        init = ((neg, neg), (zero, zero), jnp.zeros((LANES, tb), F32), jnp.zeros((_SR, tb), F32))
        carry = lax.fori_loop(0, i, lambda j, c: block(j, c, False), init)
        ms, ls, acc, st = block(i, carry, True)
        o = (acc / jnp.where(lo_r, ls[0], ls[1])).T
        o_ref[...] = o
        st = jnp.where(srow == _ST_LSE, ms[0] + jnp.log(ls[0]), st)
        st_ref[...] = jnp.where(srow == _ST_LSE + 1, ms[1] + jnp.log(ls[1]), st)
        z = z_ref[...].astype(F32)
        y_ref[...] = (o * (z * _sigmoid(z))).astype(BF16)

    qspec = lambda c0: pl.BlockSpec((None, tb, LANES), lambda bi, p, i: (bi, i, c0 + p))
    kspec = lambda c0: pl.BlockSpec((None, s, LANES), lambda bi, p, i: (bi, 0, c0 + p))
    ospec = pl.BlockSpec((None, tb, LANES), lambda bi, p, i: (bi, i, p))
    return pl.pallas_call(
        body, name=name, grid=(b, N_HEADS // 2, nq),
        in_specs=[qspec(q0), kspec(k0), kspec(v0), qspec(z0),
                  pl.BlockSpec((None, None, 2, s, LANES), lambda bi, p, i: (bi, p, 0, 0, 0))],
        out_specs=[ospec, ospec, pl.BlockSpec((None, None, None, _SR, tb), lambda bi, p, i: (bi, p, i, 0, 0))],
        out_shape=[jax.ShapeDtypeStruct((b, s, D_MODEL), BF16), jax.ShapeDtypeStruct((b, s, D_MODEL), F32),
                   jax.ShapeDtypeStruct((b, N_HEADS // 2, nq, _SR, tb), F32)],
        compiler_params=_cp(("parallel", "parallel", "arbitrary")),
    )(proj3, proj3, proj3, proj3, ckrep)


def _foxt_prep(proj3, o3, stat, dy3, *, name, tb):
    b, s, _ = proj3.shape
    nq = s // tb
    z0 = _PAD_COLS["c_z"][0] // 256

    def body(z_ref, o_ref, fst_ref, dy_ref, dz_ref, do_ref, st_ref):
        z = z_ref[...].astype(F32)
        sz = _sigmoid(z)
        dy = dy_ref[...]
        o = o_ref[...]
        do = dy * (z * sz)
        dz_ref[...] = (dy * o * (sz * (1.0 + z * (1.0 - sz)))).astype(BF16)
        do_ref[...] = do
        doo = do.astype(BF16).astype(F32) * o
        r8 = lax.broadcasted_iota(jnp.int32, (8, LANES), 0)
        l8 = lax.broadcasted_iota(jnp.int32, (8, LANES), 1)
        pick = jnp.logical_or(jnp.logical_and(r8 == 0, l8 < HEAD_DIM),
                              jnp.logical_and(r8 == 1, l8 >= HEAD_DIM)).astype(F32)
        srow = lax.broadcasted_iota(jnp.int32, (_SR, tb), 0)
        for pp in range(2):
            d8 = _dot(pick, doo[:, LANES * pp:LANES * (pp + 1)], ((1,), (1,)), precision=HIGHEST)
            st = jnp.where(srow == _ST_DELTA, _row(d8, 0), fst_ref[pp])
            st_ref[pp] = jnp.where(srow == _ST_DELTA + 1, _row(d8, 1), st)

    ospec = pl.BlockSpec((None, tb, 256), lambda bi, p, i: (bi, i, p))
    sspec = pl.BlockSpec((None, 2, None, _SR, tb), lambda bi, p, i: (bi, p, i, 0, 0))
    return pl.pallas_call(
        body, name=name, grid=(b, N_HEADS // 4, nq),
        in_specs=[pl.BlockSpec((None, tb, 256), lambda bi, p, i: (bi, i, z0 + p)), ospec, sspec, ospec],
        out_specs=[ospec, ospec, sspec],
        out_shape=[jax.ShapeDtypeStruct((b, s, D_MODEL), BF16), jax.ShapeDtypeStruct((b, s, D_MODEL), F32),
                   jax.ShapeDtypeStruct((b, N_HEADS // 2, nq, _SR, tb), F32)],
        compiler_params=_cp(("parallel", "parallel", "parallel")),
    )(proj3, o3, stat, dy3)


def _foxt_bwd(proj3, ckrep, do3, stats, *, name, tb):
    b, s, _ = proj3.shape
    nq = s // tb
    q0 = _PAD_COLS["c_q"][0] // LANES
    k0 = _PAD_COLS["c_k"][0] // LANES
    v0 = _PAD_COLS["c_v"][0] // LANES
    rep = tb // LANES

    def body(q_ref, do_ref, st_ref, k_ref, v_ref, ck_ref, dq_ref, dk_ref, dv_ref, cs_ref):
        j = pl.program_id(2)
        lane = lax.broadcasted_iota(jnp.int32, (tb, LANES), 1)
        lo = lane < HEAD_DIM
        lo_r = lax.broadcasted_iota(jnp.int32, (LANES, tb), 0) < HEAD_DIM

        @pl.when(j == 0)
        def _():
            dq_ref[...] = jnp.zeros_like(dq_ref)

        kf = k_ref[...].astype(F32)
        kb = kf.astype(BF16)
        kt = kf.T.astype(BF16)
        vb = v_ref[...].astype(BF16)
        cks = (jnp.tile(ck_ref[0], (1, rep)), jnp.tile(ck_ref[1], (1, rep)))

        def block(i, carry, diagonal):
            qs = pl.ds(pl.multiple_of(i * tb, tb), tb)
            q = q_ref[qs, :].astype(F32) * _SCALE
            do = do_ref[qs, :]
            st = st_ref[i]
            if diagonal:
                key = lax.broadcasted_iota(jnp.int32, (tb, tb), 0)
                qry = lax.broadcasted_iota(jnp.int32, (tb, tb), 1)
                mask = key <= qry
            dk, dv, cs = carry
            new_cs, dqs = [], []
            for hh in range(2):
                sel = lo if hh == 0 else jnp.logical_not(lo)
                qm = jnp.where(sel, q, 0.0).astype(BF16)
                dom = jnp.where(sel, do, 0.0).astype(BF16)
                sc = _dot_nt(kb, qm) - cks[hh]
                if diagonal:
                    sc = jnp.where(mask, sc, _NEG)
                mj = _row(st, _ST_MJ + 2 * j + hh)
                w = jnp.exp(mj - _row(st, _ST_LSE + hh))
                ph = jnp.exp(sc - mj).astype(BF16).astype(F32) * w
                ds = ph * (_dot_nt(vb, dom) - _row(st, _ST_DELTA + hh))
                dsb = ds.astype(BF16)
                dv = dv + _dot(ph.astype(BF16), dom)
                dk = dk + _dot(dsb, qm)
                new_cs.append(cs[hh] + jnp.sum(ds, axis=1, keepdims=True))
                dqs.append(_dot(kt, dsb))
            dq_ref[i] += jnp.where(lo_r, dqs[0], dqs[1]) * _SCALE
            return (dk, dv, tuple(new_cs))

        zcol = jnp.zeros((tb, 1), F32)
        init = (jnp.zeros((tb, LANES), F32), jnp.zeros((tb, LANES), F32), (zcol, zcol))
        carry = block(j, init, True)
        dk, dv, cs = lax.fori_loop(j + 1, nq, lambda i, c: block(i, c, False), carry)
        dk_ref[...] = dk.astype(BF16)
        dv_ref[...] = dv.astype(BF16)
        p2 = 2 * pl.program_id(1)
        cs_ref[...] = jnp.where(lane == p2, cs[0], jnp.where(lane == p2 + 1, cs[1], 0.0))

    full = lambda c0: pl.BlockSpec((None, s, LANES), lambda bi, p, j: (bi, 0, c0 + p))
    kspec = lambda c0: pl.BlockSpec((None, tb, LANES), lambda bi, p, j: (bi, j, c0 + p))
    ko = pl.BlockSpec((None, tb, LANES), lambda bi, p, j: (bi, j, p))
    sall = pl.BlockSpec((None, None, nq, _SR, tb), lambda bi, p, j: (bi, p, 0, 0, 0))
    dqspec = pl.BlockSpec((None, None, nq, LANES, tb), lambda bi, p, j: (bi, p, 0, 0, 0))
    return pl.pallas_call(
        body, name=name, grid=(b, N_HEADS // 2, nq),
        in_specs=[full(q0), full(0), sall, kspec(k0), kspec(v0),
                  pl.BlockSpec((None, None, 2, tb, LANES), lambda bi, p, j: (bi, p, 0, j, 0))],
        out_specs=[dqspec, ko, ko, pl.BlockSpec((None, None, tb, LANES), lambda bi, p, j: (bi, p, j, 0))],
        out_shape=[jax.ShapeDtypeStruct((b, N_HEADS // 2, nq, LANES, tb), F32),
                   jax.ShapeDtypeStruct((b, s, D_MODEL), BF16), jax.ShapeDtypeStruct((b, s, D_MODEL), BF16),
                   jax.ShapeDtypeStruct((b, N_HEADS // 2, s, LANES), F32)],
        compiler_params=_cp(("parallel", "parallel", "arbitrary")),
    )(proj3, do3, stats, proj3, proj3, ckrep)


def _rope(x, cos, sin_signed):
    w = x.shape[1]
    lane = lax.broadcasted_iota(jnp.int32, x.shape, 1)
    first = (lane % HEAD_DIM) < (HEAD_DIM // 2)
    rot = jnp.where(first, pltpu.roll(x, w - HEAD_DIM // 2, 1), pltpu.roll(x, HEAD_DIM // 2, 1))
    return x * cos + rot * sin_signed


_QB = 8
_QROWS = _QB * CHUNK


def _swa_keys(g, kc_ref, kp_ref, vc_ref, vp_ref, cq_ref, sq_ref, cp_ref, sp_ref):
    def both_halves(x):
        x = x.astype(F32)
        lane = lax.broadcasted_iota(jnp.int32, x.shape, 1)
        keep = (lane // HEAD_DIM) == (g % 2)
        return jnp.where(keep, x, pltpu.roll(x, HEAD_DIM, 1))

    cq, sq, cpv, spv = cq_ref[...], sq_ref[...], cp_ref[...], sp_ref[...]
    kc = _rope(both_halves(kc_ref[...]), cq, sq).astype(BF16)
    kp = _rope(both_halves(kp_ref[...]), cpv, spv).astype(BF16)
    return cq, sq, cpv, spv, kc, kp, both_halves(vc_ref[...]).astype(BF16), both_halves(vp_ref[...]).astype(BF16)


def _swa_stack(pairs, lo):
    return jnp.concatenate([jnp.where(lo, pairs[0], 0.0), jnp.where(lo, 0.0, pairs[0]),
                            jnp.where(lo, pairs[1], 0.0), jnp.where(lo, 0.0, pairs[1])], axis=0).astype(BF16)


def _swa_mask4(prev_valid):
    r = lax.broadcasted_iota(jnp.int32, (4 * CHUNK, 2 * CHUNK), 0) & (CHUNK - 1)
    c = lax.broadcasted_iota(jnp.int32, (4 * CHUNK, 2 * CHUNK), 1)
    own = jnp.logical_and(c >= CHUNK, c - CHUNK <= r)
    before = jnp.logical_and(c < CHUNK, c > r)
    if prev_valid is True:
        return jnp.logical_or(own, before)
    return jnp.logical_or(own, jnp.logical_and(before, prev_valid))


def _swa_sink4(skv):
    return jnp.concatenate([jnp.broadcast_to(_col(skv, j), (CHUNK, 1)) for j in range(4)], axis=0)


def _swa_specs(order):
    def spec(shape, fn):
        return pl.BlockSpec(shape, lambda *ids: fn(*order(*ids)))

    q0 = _PAD_COLS["b_q"][0] // 256
    z0 = _PAD_COLS["b_z"][0] // 256
    k0 = _PAD_COLS["b_k"][0] // LANES
    v0 = _PAD_COLS["b_v"][0] // LANES
    prev = lambda i: jnp.maximum(_QB * i - 1, 0)
    return dict(
        kc=spec((None, _QROWS, LANES), lambda bi, g, i: (bi, i, k0 + g // 2)),
        kp=spec((None, CHUNK, LANES), lambda bi, g, i: (bi, prev(i), k0 + g // 2)),
        vc=spec((None, _QROWS, LANES), lambda bi, g, i: (bi, i, v0 + g // 2)),
        vp=spec((None, CHUNK, LANES), lambda bi, g, i: (bi, prev(i), v0 + g // 2)),
        q=spec((None, _QROWS, 256), lambda bi, g, i: (bi, i, q0 + g)),
        z=spec((None, _QROWS, 256), lambda bi, g, i: (bi, i, z0 + g)),
        blk=spec((None, _QROWS, 256), lambda bi, g, i: (bi, i, g)),
        kcur=spec((None, _QROWS, LANES), lambda bi, g, i: (bi, i, g)),
        kstep=spec((None, CHUNK, LANES), lambda bi, g, i: (bi, i, g)),
        tcur=spec((_QROWS, LANES), lambda bi, g, i: (i, 0)),
        tprev=spec((CHUNK, LANES), lambda bi, g, i: (prev(i), 0)),
        sk=spec((None, 1, LANES), lambda bi, g, i: (g, 0, 0)))


def _swa_fwd(proj3, cos, sin, sinks, *, name):
    b, s, _ = proj3.shape

    def body(q_ref, z_ref, kc_ref, kp_ref, vc_ref, vp_ref, cq_ref, sq_ref, cp_ref, sp_ref, sk_ref,
             y_ref, o_ref, lse_ref):
        i = pl.program_id(2)
        cq_all, sq_all, _, _, kc_all, kp0, vc_all, vp0 = _swa_keys(
            pl.program_id(1), kc_ref, kp_ref, vc_ref, vp_ref, cq_ref, sq_ref, cp_ref, sp_ref)
        lo = lax.broadcasted_iota(jnp.int32, (CHUNK, LANES), 1) < HEAD_DIM
        sink4 = _swa_sink4(sk_ref[...])
        for u in range(_QB):
            rs = slice(CHUNK * u, CHUNK * (u + 1))
            ps = slice(CHUNK * (u - 1), CHUNK * u)
            cq, sq = cq_all[rs], sq_all[rs]
            kp, vp = (kp0, vp0) if u == 0 else (kc_all[ps], vc_all[ps])
            kk = jnp.concatenate([kp, kc_all[rs]], axis=0)
            vv = jnp.concatenate([vp, vc_all[rs]], axis=0)
            q4 = _swa_stack([_rope(q_ref[rs, LANES * pp:LANES * (pp + 1)].astype(F32), cq, sq) * _SCALE
                             for pp in range(2)], lo)
            sc = jnp.where(_swa_mask4(True if u > 0 else i > 0), _dot_nt(q4, kk), _NEG)
            m = jnp.maximum(jnp.max(sc, axis=1, keepdims=True), sink4)
            pr = jnp.exp(sc - m)
            l = jnp.sum(pr, axis=1, keepdims=True) + jnp.exp(sink4 - m)
            o4 = _dot(pr.astype(BF16), vv) / l
            lse4 = m + jnp.log(l)
            for pp in range(2):
                ls = slice(LANES * pp, LANES * (pp + 1))
                h0 = slice(2 * CHUNK * pp, 2 * CHUNK * pp + CHUNK)
                h1 = slice(2 * CHUNK * pp + CHUNK, 2 * CHUNK * (pp + 1))
                o = jnp.where(lo, o4[h0], o4[h1])
                z = z_ref[rs, ls].astype(F32)
                o_ref[rs, ls] = o
                lse_ref[rs, ls] = jnp.where(lo, lse4[h0], lse4[h1])
                y_ref[rs, ls] = (o * (z * _sigmoid(z))).astype(BF16)

    sp = _swa_specs(lambda bi, g, i: (bi, g, i))
    return pl.pallas_call(
        body, name=name, grid=(b, N_GROUPS, s // _QROWS),
        in_specs=[sp["q"], sp["z"], sp["kc"], sp["kp"], sp["vc"], sp["vp"],
                  sp["tcur"], sp["tcur"], sp["tprev"], sp["tprev"], sp["sk"]],
        out_specs=[sp["blk"], sp["blk"], sp["blk"]],
        out_shape=[jax.ShapeDtypeStruct((b, s, D_MODEL), BF16)] + [jax.ShapeDtypeStruct((b, s, D_MODEL), F32)] * 2,
        compiler_params=_cp(("parallel", "parallel", "parallel")),
    )(proj3, proj3, proj3, proj3, proj3, proj3, cos, sin, cos, sin, sinks)


def _swa_bwd(proj3, cos, sin, sinks, o3, lse3, dy3, *, name):
    b, s, _ = proj3.shape

    def body(q_ref, z_ref, kc_ref, kp_ref, vc_ref, vp_ref, cq_ref, sq_ref, cp_ref, sp_ref, sk_ref,
             o_ref, lse_ref, dy_ref, dq_ref, dz_ref, dkc_ref, dkp_ref, dvc_ref, dvp_ref, dsk_ref):
        i = pl.program_id(2)
        first = jnp.logical_and(pl.program_id(1) == 0, i == 0)

        @pl.when(first)
        def _():
            dsk_ref[...] = jnp.zeros_like(dsk_ref)

        cq_all, sq_all, cpv, spv, kc_all, kp0, vc_all, vp0 = _swa_keys(
            pl.program_id(0), kc_ref, kp_ref, vc_ref, vp_ref, cq_ref, sq_ref, cp_ref, sp_ref)
        lo = lax.broadcasted_iota(jnp.int32, (CHUNK, LANES), 1) < HEAD_DIM
        lane1 = lax.broadcasted_iota(jnp.int32, (1, LANES), 1)
        sink4 = _swa_sink4(sk_ref[...])
        zero = jnp.zeros((CHUNK, LANES), F32)
        dks = [zero] * (_QB + 1)
        dvs = [zero] * (_QB + 1)
        dsk_row = jnp.zeros((1, LANES), F32)
        for u in range(_QB):
            rs = slice(CHUNK * u, CHUNK * (u + 1))
            ps = slice(CHUNK * (u - 1), CHUNK * u)
            cq, sq = cq_all[rs], sq_all[rs]
            kp, vp = (kp0, vp0) if u == 0 else (kc_all[ps], vc_all[ps])
            kk = jnp.concatenate([kp, kc_all[rs]], axis=0)
            vv = jnp.concatenate([vp, vc_all[rs]], axis=0)
            q4 = _swa_stack([_rope(q_ref[rs, LANES * pp:LANES * (pp + 1)].astype(F32), cq, sq) * _SCALE
                             for pp in range(2)], lo)
            dos, lses = [], []
            for pp in range(2):
                ls = slice(LANES * pp, LANES * (pp + 1))
                z = z_ref[rs, ls].astype(F32)
                sz = _sigmoid(z)
                dy = dy_ref[rs, ls]
                dos.append(dy * (z * sz))
                dz_ref[rs, ls] = (dy * o_ref[rs, ls] * (sz * (1.0 + z * (1.0 - sz)))).astype(BF16)
                lse = lse_ref[rs, ls]
                lses += [_col(lse, 0), _col(lse, HEAD_DIM)]
            do4 = _swa_stack(dos, lo)
            lse4 = jnp.concatenate(lses, axis=0)
            pr = jnp.exp(jnp.where(_swa_mask4(True if u > 0 else i > 0), _dot_nt(q4, kk), _NEG) - lse4)
            dp = _dot_nt(do4, vv)
            dl = jnp.sum(pr * dp, axis=1, keepdims=True)
            ds = (pr * (dp - dl)).astype(BF16)
            dsink = -jnp.exp(sink4 - lse4) * dl
            for j in range(4):
                dsk_row = dsk_row + jnp.where(
                    lane1 == j, jnp.sum(dsink[CHUNK * j:CHUNK * (j + 1)], axis=0, keepdims=True), 0.0)
            dq4 = _dot(ds, kk)
            dkk = _dot_tn(ds, q4)
            dvv = _dot_tn(pr.astype(BF16), do4)
            dks[u], dks[u + 1] = dks[u] + dkk[:CHUNK], dks[u + 1] + dkk[CHUNK:]
            dvs[u], dvs[u + 1] = dvs[u] + dvv[:CHUNK], dvs[u + 1] + dvv[CHUNK:]
            for pp in range(2):
                h0 = slice(2 * CHUNK * pp, 2 * CHUNK * pp + CHUNK)
                h1 = slice(2 * CHUNK * pp + CHUNK, 2 * CHUNK * (pp + 1))
                dq_ref[rs, LANES * pp:LANES * (pp + 1)] = _rope(
                    jnp.where(lo, dq4[h0], dq4[h1]) * _SCALE, cq, -sq).astype(BF16)
        fold = lambda v: v + pltpu.roll(v, HEAD_DIM, 1)
        dkp_ref[...] = fold(_rope(dks[0], cpv, -spv))
        dvp_ref[...] = fold(dvs[0])
        for u in range(_QB):
            rs = slice(CHUNK * u, CHUNK * (u + 1))
            dkc_ref[rs, :] = fold(_rope(dks[u + 1], cq_all[rs], -sq_all[rs]))
            dvc_ref[rs, :] = fold(dvs[u + 1])
        dsk_ref[...] += dsk_row

    sp = _swa_specs(lambda g, bi, i: (bi, g, i))
    kv_shape = jax.ShapeDtypeStruct((b, s, 512), F32)
    kvp_shape = jax.ShapeDtypeStruct((b, s // _QB, 512), F32)
    return pl.pallas_call(
        body, name=name, grid=(N_GROUPS, b, s // _QROWS),
        in_specs=[sp["q"], sp["z"], sp["kc"], sp["kp"], sp["vc"], sp["vp"],
                  sp["tcur"], sp["tcur"], sp["tprev"], sp["tprev"], sp["sk"], sp["blk"], sp["blk"], sp["blk"]],
        out_specs=[sp["blk"], sp["blk"], sp["kcur"], sp["kstep"], sp["kcur"], sp["kstep"], sp["sk"]],
        out_shape=[jax.ShapeDtypeStruct((b, s, D_MODEL), BF16), jax.ShapeDtypeStruct((b, s, D_MODEL), BF16),
                   kv_shape, kvp_shape, kv_shape, kvp_shape, jax.ShapeDtypeStruct((N_GROUPS, 1, LANES), F32)],
        compiler_params=_cp(("arbitrary", "arbitrary", "arbitrary")),
    )(proj3, proj3, proj3, proj3, proj3, proj3, cos, sin, cos, sin, sinks, o3, lse3, dy3)


def _swa_fold(dkc, dkp, dvc, dvp, *, name):
    b, s, _ = dkc.shape
    ns = s // _QROWS

    def body(kc_ref, kp_ref, vc_ref, vp_ref, dk_ref, dv_ref):
        has_next = pl.program_id(1) < ns - 1
        lo = lax.broadcasted_iota(jnp.int32, (_QROWS, LANES), 1) < HEAD_DIM
        row = lax.broadcasted_iota(jnp.int32, (_QROWS, 512), 0)
        last_block = jnp.logical_and(row >= _QROWS - CHUNK, has_next)
        for cur, nxt, out in ((kc_ref, kp_ref, dk_ref), (vc_ref, vp_ref, dv_ref)):
            tot = cur[...] + jnp.where(last_block, jnp.tile(nxt[...], (_QB, 1)), 0.0)
            for j in range(2):
                out[:, LANES * j:LANES * (j + 1)] = jnp.where(
                    lo, tot[:, 256 * j:256 * j + LANES], tot[:, 256 * j + LANES:256 * (j + 1)]).astype(BF16)

    cur = pl.BlockSpec((None, _QROWS, 512), lambda bi, i: (bi, i, 0))
    nxt = pl.BlockSpec((None, CHUNK, 512), lambda bi, i: (bi, jnp.minimum(i + 1, ns - 1), 0))
    out = pl.BlockSpec((None, _QROWS, 256), lambda bi, i: (bi, i, 0))
    sh = jax.ShapeDtypeStruct((b, s, 256), BF16)
    return pl.pallas_call(
        body, name=name, grid=(b, ns), in_specs=[cur, nxt, cur, nxt], out_specs=[out, out], out_shape=[sh, sh],
        compiler_params=_cp(("parallel", "parallel")),
    )(dkc, dkp, dvc, dvp)


def _branch_fwd(ys, proj, gb, wp, wo, x, *, name, tm=256):
    t = proj.shape[0]
    g0 = _PAD_COLS["gates"][0] // D_MODEL

    def body(g_ref, a_ref, b_ref, c_ref, gb_ref, wp_ref, wo_ref, x_ref, ba_ref, bb_ref, bc_ref, m_ref, xn_ref):
        acc = None
        for i, (y, br) in enumerate(((a_ref, ba_ref), (b_ref, bb_ref), (c_ref, bc_ref))):
            bri = _dot(y[...], wp_ref[i])
            br[...] = bri
            gate = _sigmoid(g_ref[:, D_MODEL * i:D_MODEL * (i + 1)].astype(F32) + gb_ref[i:i + 1, :])
            acc = gate * bri if acc is None else acc + gate * bri
        mb = acc.astype(BF16)
        m_ref[...] = mb
        xn_ref[...] = x_ref[...] + _dot(mb, wo_ref[...])

    row = pl.BlockSpec((tm, D_MODEL), lambda i: (i, 0))
    rowf = jax.ShapeDtypeStruct((t, D_MODEL), F32)
    outs = pl.pallas_call(
        body, name=name, grid=(t // tm,),
        in_specs=[pl.BlockSpec((tm, 3 * D_MODEL), lambda i: (i, g0)), row, row, row,
                  pl.BlockSpec((3, D_MODEL), lambda i: (0, 0)),
                  pl.BlockSpec((3, D_MODEL, D_MODEL), lambda i: (0, 0, 0)),
                  pl.BlockSpec((D_MODEL, D_MODEL), lambda i: (0, 0)), row],
        out_specs=[row, row, row, row, row],
        out_shape=[rowf, rowf, rowf, jax.ShapeDtypeStruct((t, D_MODEL), BF16), rowf],
        compiler_params=_cp(("parallel",)),
    )(proj, ys[0], ys[1], ys[2], gb, wp, wo, x)
    return outs[:3], outs[3], outs[4]


def _branch_bwd(dx, proj, br, gb, wp, wo, *, name, tm=256):
    t = proj.shape[0]
    g0 = _PAD_COLS["gates"][0] // D_MODEL

    def body(g_ref, a_ref, b_ref, c_ref, gb_ref, wp_ref, wo_ref, dx_ref,
             da_ref, db_ref, dc_ref, dg_ref, dgb_ref, ya_ref, yb_ref, yc_ref):
        @pl.when(pl.program_id(0) == 0)
        def _():
            dgb_ref[...] = jnp.zeros_like(dgb_ref)

        dmv = _dot_nt(dx_ref[...].astype(BF16), wo_ref[...])
        for i, (r, dr, dy) in enumerate(((a_ref, da_ref, ya_ref), (b_ref, db_ref, yb_ref), (c_ref, dc_ref, yc_ref))):
            gate = _sigmoid(g_ref[:, D_MODEL * i:D_MODEL * (i + 1)].astype(F32) + gb_ref[i:i + 1, :])
            dbr = (dmv * gate).astype(BF16)
            dr[...] = dbr
            dg = dmv * r[...] * gate * (1.0 - gate)
            dg_ref[:, D_MODEL * i:D_MODEL * (i + 1)] = dg.astype(BF16)
            dgb_ref[i:i + 1, :] += jnp.sum(dg, axis=0, keepdims=True)
            dy[...] = _dot_nt(dbr, wp_ref[i])

    row = pl.BlockSpec((tm, D_MODEL), lambda i: (i, 0))
    rowb = jax.ShapeDtypeStruct((t, D_MODEL), BF16)
    rowf = jax.ShapeDtypeStruct((t, D_MODEL), F32)
    outs = pl.pallas_call(
        body, name=name, grid=(t // tm,),
        in_specs=[pl.BlockSpec((tm, 3 * D_MODEL), lambda i: (i, g0)), row, row, row,
                  pl.BlockSpec((3, D_MODEL), lambda i: (0, 0)),
                  pl.BlockSpec((3, D_MODEL, D_MODEL), lambda i: (0, 0, 0)),
                  pl.BlockSpec((D_MODEL, D_MODEL), lambda i: (0, 0)), row],
        out_specs=[row, row, row, pl.BlockSpec((tm, 3 * D_MODEL), lambda i: (i, 0)),
                   pl.BlockSpec((8, D_MODEL), lambda i: (0, 0)), row, row, row],
        out_shape=[rowb, rowb, rowb, jax.ShapeDtypeStruct((t, 3 * D_MODEL), BF16),
                   jax.ShapeDtypeStruct((8, D_MODEL), F32), rowf, rowf, rowf],
        compiler_params=_cp(("arbitrary",)),
    )(proj, br[0], br[1], br[2], gb, wp, wo, dx)
    return outs[:3], outs[3], outs[4], outs[5:]


def _rope_tables(s):
    pos = jnp.arange(s, dtype=F32)
    inv_freq = ROPE_THETA ** (-jnp.arange(0, HEAD_DIM, 2, dtype=F32) / HEAD_DIM)
    ang = pos[:, None] * inv_freq[None, :]
    cos, sin = jnp.cos(ang), jnp.sin(ang)
    return jnp.tile(cos, (1, 4)), jnp.tile(jnp.concatenate([-sin, sin], axis=1), (1, 2))


def _layer_params(wl):
    return dict(
        dtb=_group_lanes(wl["dt_bias"]), alog=_group_lanes(wl["a_log"]), dsk=_group_lanes(wl["d_skip"]),
        nw=wl["ssm_norm_w"].reshape(N_GROUPS, 1, 256), sinks=_group_lanes(wl["sinks"]),
        fb=jnp.pad(wl["f_bias"], (0, LANES - N_HEADS)).reshape(1, LANES))


def _layer_fwd(x, wl, tabs, bsz, li, tb):
    t = x.shape[0]
    s = t // bsz
    cos, sin = tabs
    lp = _layer_params(wl)
    n = lambda k: f"l{li}_{k}"
    h, h_t = _rms_fwd(x, wl["norm_w"], name=n("rms_fwd"))
    proj = _mm(h, wl["w_in"], tm=1024, tn=1536, tk=1024, out_dtype=BF16, name=n("mm_proj"))
    proj3 = proj.reshape(bsz, s, N_PAD)
    g0, gw = _PAD_COLS["a_dt"][0], _PAD_COLS["a_dt"][1] + _PAD_COLS["c_f"][1]
    gates3 = _mm(h, wl["w_in"][:, g0:g0 + gw], tm=1024, tn=gw, tk=1024, name=n("mm_gates")).reshape(bsz, s, gw)
    xact3 = _conv_fwd(proj3, wl["conv_w"], wl["conv_b"], name=n("conv_fwd"))
    ya3, ypre3, hst = _ssd_fwd(proj3, gates3, xact3, lp["dtb"], lp["alog"], lp["dsk"], lp["nw"], name=n("ssd_fwd"))
    yb3, ob3, lseb3 = _swa_fwd(proj3, cos, sin, lp["sinks"], name=n("swa_fwd"))
    cum = _fgate_fwd(gates3, lp["fb"], name=n("fgate_fwd"))
    cum_t = _ck_rep(cum)
    yc3, oc3, statc3 = _foxt_fwd(proj3, cum_t, name=n("fox_fwd"), tb=tb)
    ys = [v.reshape(t, D_MODEL) for v in (ya3, yb3, yc3)]
    br, merged, x_new = _branch_fwd(ys, proj, wl["gate_bias"], wl["w_proj"], wl["w_out"], x, name=n("branch_fwd"))
    saved = dict(x=x, h_t=h_t, proj=proj, gates3=gates3, xact3=xact3, ypre3=ypre3, hst=hst, ob3=ob3, lseb3=lseb3,
                 cum_t=cum_t, oc3=oc3, statc3=statc3, ys=ys, br=br, merged=merged, lp=lp)
    return x_new, saved


def _layer_bwd(dx, wl, sv, tabs, bsz, li, tb):
    t = dx.shape[0]
    s = t // bsz
    cos, sin = tabs
    lp = sv["lp"]
    n = lambda k: f"l{li}_{k}"
    proj = sv["proj"]
    proj3 = proj.reshape(bsz, s, N_PAD)
    g = {}
    g["w_out"] = _mm(sv["merged"], dx, ta=True, tm=1024, tn=1024, tk=512, name=n("mm_dwout"))
    dbr, dgates, dgb, dys = _branch_bwd(dx, proj, sv["br"], wl["gate_bias"], wl["w_proj"], wl["w_out"],
                                        name=n("branch_bwd"))
    g["gate_bias"] = dgb[:3]
    g["w_proj"] = jnp.stack([_mm(sv["ys"][i], dbr[i], ta=True, tm=1024, tn=1024, tk=512, name=n(f"mm_dwproj{i}"))
                             for i in range(3)])
    dy3 = [v.reshape(bsz, s, D_MODEL) for v in dys]

    (dact, daz, dadt, ddtb, dalog, ddsk, dnw) = _ssd_bwd(
        proj3, sv["gates3"], sv["xact3"], lp["dtb"], lp["alog"], lp["dsk"], lp["nw"], sv["ypre3"], sv["hst"], dy3[0],
        name=n("ssd_bwd"))
    g["dt_bias"], g["a_log"], g["d_skip"] = _ungroup_lanes(ddtb), _ungroup_lanes(dalog), _ungroup_lanes(ddsk)
    g["ssm_norm_w"] = dnw.reshape(D_MODEL)
    dxbc, dwb = _conv_bwd(proj3, wl["conv_w"], wl["conv_b"], dact, name=n("conv_bwd"))
    g["conv_w"], g["conv_b"] = dwb[:CONV_WIDTH], dwb[CONV_WIDTH]

    dbq, dbz, dkc, dkp, dvc, dvp, dsk = _swa_bwd(proj3, cos, sin, lp["sinks"], sv["ob3"],
                                                 sv["lseb3"], dy3[1], name=n("swa_bwd"))
    g["sinks"] = _ungroup_lanes(dsk)

    dbk, dbv = _swa_fold(dkc, dkp, dvc, dvp, name=n("swa_fold"))

    dcz, do3, stats = _foxt_prep(proj3, sv["oc3"], sv["statc3"], dy3[2], name=n("fox_prep"), tb=tb)
    dqt, dck, dcv, csum = _foxt_bwd(proj3, sv["cum_t"], do3, stats, name=n("fox_bwd"), tb=tb)
    dcq = jnp.transpose(dqt, (0, 2, 4, 1, 3)).reshape(bsz, s, D_MODEL)
    dcf, dfb = _fgate_bwd(sv["gates3"], lp["fb"], csum, name=n("fgate_bwd"))
    g["f_bias"] = dfb[0, :N_HEADS]

    parts = {"gates": dgates.reshape(bsz, s, 3 * D_MODEL), "xbc": dxbc, "a_z": daz, "b_q": dbq, "b_z": dbz,
             "c_q": dcq, "c_k": dck, "c_v": dcv, "c_z": dcz, "b_k": dbk, "b_v": dbv, "a_dt": dadt, "c_f": dcf}
    dproj = jnp.concatenate([parts[name].astype(BF16) for name, _ in _PAD_ORDER]
                            + [jnp.zeros((bsz, s, N_PAD - N_USED), BF16)], axis=2).reshape(t, N_PAD)
    dh = _mm(dproj, wl["w_in"], tb=True, tm=1024, tn=1024, tk=1536, name=n("mm_dh"))
    g["w_in"] = _unpad_w_in(_mm(sv["h_t"], dproj, tm=1024, tn=768, tk=2048, name=n("mm_dwin")))
    dx_in, dnorm = _rms_bwd(sv["x"], wl["norm_w"], dh, dx, name=n("rms_bwd"))
    g["norm_w"] = dnorm[0]
    return dx_in, g


def _local_step(x, target, wls, final_norm_w, tb=1024):
    bsz, s, d = x.shape
    t = bsz * s
    tabs = _rope_tables(s)
    xc = x.reshape(t, d)
    saved = []
    for li, wl in enumerate(wls):
        xc, sv = _layer_fwd(xc, wl, tabs, bsz, li, tb)
        saved.append(sv)
    loss, dx, dfw = _final_loss(xc, final_norm_w, target.reshape(t, d), name="final_loss")
    grads = [None] * len(wls)
    for li in reversed(range(len(wls))):
        dx, grads[li] = _layer_bwd(dx, wls[li], saved[li], tabs, bsz, li, tb)
    return loss[0, 0], dx.reshape(bsz, s, d), grads, dfw[0]


_HBM = pl.BlockSpec(memory_space=pltpu.HBM)


def _chip_peers(x, y):
    return [(1 - x, y), (x, 1 - y), (1 - x, 1 - y)]


def _gather_weights(arrs, *, name):
    n = len(arrs)

    def body(*refs):
        ins, outs = refs[:n], refs[n:2 * n]
        ici_send, ici_recv, d2d_send, d2d_recv = refs[2 * n:]
        x, y, c = lax.axis_index("x"), lax.axis_index("y"), lax.axis_index("c")
        me = 2 * x + y
        peers = _chip_peers(x, y)
        sib = (x, y, 1 - c)
        sends, fwds = [], []
        for a in range(n):
            for k, (px, py) in enumerate(peers):
                cp = pltpu.make_async_remote_copy(
                    src_ref=ins[a].at[c], dst_ref=outs[a].at[me, c], send_sem=ici_send.at[a, k],
                    recv_sem=ici_recv.at[a, k], device_id=(px, py, c), device_id_type=MESH)
                cp.start()
                sends.append(cp)
        for a in range(n):
            for k, (px, py) in enumerate(peers):
                slot = 2 * px + py
                pltpu.make_async_remote_copy(
                    src_ref=ins[a].at[c], dst_ref=outs[a].at[slot, c], send_sem=ici_send.at[a, k],
                    recv_sem=ici_recv.at[a, k], device_id=(px, py, c), device_id_type=MESH).wait_recv()
                fw = pltpu.make_async_remote_copy(
                    src_ref=outs[a].at[slot, c], dst_ref=outs[a].at[slot, c], send_sem=d2d_send.at[a, k],
                    recv_sem=d2d_recv.at[a, k], device_id=sib, device_id_type=MESH)
                fw.start()
                fwds.append(fw)
        for a in range(n):
            for k, (px, py) in enumerate(peers):
                slot = 2 * px + py
                pltpu.make_async_remote_copy(
                    src_ref=outs[a].at[slot, 1 - c], dst_ref=outs[a].at[slot, 1 - c], send_sem=d2d_send.at[a, k],
                    recv_sem=d2d_recv.at[a, k], device_id=sib, device_id_type=MESH).wait_recv()
        for cp in sends + fwds:
            cp.wait_send()

    out_shape = [jax.ShapeDtypeStruct((N_CHIPS,) + a.shape, a.dtype) for a in arrs]
    return pl.pallas_call(
        body, name=name, out_shape=out_shape, in_specs=[_HBM] * n, out_specs=[_HBM] * n,
        scratch_shapes=[pltpu.SemaphoreType.DMA((n, 3)), pltpu.SemaphoreType.DMA((n, 3)),
                        pltpu.SemaphoreType.DMA((n, 3)), pltpu.SemaphoreType.DMA((n, 3))],
    )(*arrs)


def _pair_exchange(arrs, *, name):
    n = len(arrs)

    def body(*refs):
        ins, outs = refs[:n], refs[n:2 * n]
        send, recv = refs[2 * n:]
        x, y, c = lax.axis_index("x"), lax.axis_index("y"), lax.axis_index("c")
        sib = (x, y, 1 - c)
        cps = []
        for a in range(n):
            for k in range(N_CHIPS):
                cp = pltpu.make_async_remote_copy(
                    src_ref=ins[a].at[k, 1 - c], dst_ref=outs[a].at[k], send_sem=send.at[a, k],
                    recv_sem=recv.at[a, k], device_id=sib, device_id_type=MESH)
                cp.start()
                cps.append(cp)
        for cp in cps:
            cp.wait()

    out_shape = [jax.ShapeDtypeStruct((N_CHIPS,) + a.shape[2:], a.dtype) for a in arrs]
    return pl.pallas_call(
        body, name=name, out_shape=out_shape, in_specs=[_HBM] * n, out_specs=[_HBM] * n,
        scratch_shapes=[pltpu.SemaphoreType.DMA((n, N_CHIPS)), pltpu.SemaphoreType.DMA((n, N_CHIPS))],
    )(*arrs)


def _chip_exchange(arrs, *, name):
    n = len(arrs)

    def body(*refs):
        ins, outs = refs[:n], refs[n:2 * n]
        send, recv = refs[2 * n:]
        x, y, c = lax.axis_index("x"), lax.axis_index("y"), lax.axis_index("c")
        me = 2 * x + y
        peers = _chip_peers(x, y)
        cps = []
        for a in range(n):
            for k, (px, py) in enumerate(peers):
                cp = pltpu.make_async_remote_copy(
                    src_ref=ins[a].at[2 * px + py], dst_ref=outs[a].at[me], send_sem=send.at[a, k],
                    recv_sem=recv.at[a, k], device_id=(px, py, c), device_id_type=MESH)
                cp.start()
                cps.append(cp)
        for a in range(n):
            for k, (px, py) in enumerate(peers):
                pltpu.make_async_remote_copy(
                    src_ref=ins[a].at[2 * px + py], dst_ref=outs[a].at[2 * px + py], send_sem=send.at[a, k],
                    recv_sem=recv.at[a, k], device_id=(px, py, c), device_id_type=MESH).wait_recv()
        for cp in cps:
            cp.wait_send()

    out_shape = [jax.ShapeDtypeStruct(a.shape, a.dtype) for a in arrs]
    return pl.pallas_call(
        body, name=name, out_shape=out_shape, in_specs=[_HBM] * n, out_specs=[_HBM] * n,
        scratch_shapes=[pltpu.SemaphoreType.DMA((n, 3)), pltpu.SemaphoreType.DMA((n, 3))],
    )(*arrs)


def _pair_share(arrs, *, name):
    n = len(arrs)

    def body(*refs):
        ins, outs = refs[:n], refs[n:2 * n]
        send, recv = refs[2 * n:]
        x, y, c = lax.axis_index("x"), lax.axis_index("y"), lax.axis_index("c")
        sib = (x, y, 1 - c)
        cps = []
        for a in range(n):
            cp = pltpu.make_async_remote_copy(
                src_ref=ins[a], dst_ref=outs[a], send_sem=send.at[a], recv_sem=recv.at[a],
                device_id=sib, device_id_type=MESH)
            cp.start()
            cps.append(cp)
        for cp in cps:
            cp.wait()

    out_shape = [jax.ShapeDtypeStruct(a.shape, a.dtype) for a in arrs]
    return pl.pallas_call(
        body, name=name, out_shape=out_shape, in_specs=[_HBM] * n, out_specs=[_HBM] * n,
        scratch_shapes=[pltpu.SemaphoreType.DMA((n,)), pltpu.SemaphoreType.DMA((n,))],
    )(*arrs)


def _allreduce_small(buf, *, name):
    r = buf.shape[0]

    def body(in_ref, out_ref, land, send, recv):
        x, y, c = lax.axis_index("x"), lax.axis_index("y"), lax.axis_index("c")
        me = 4 * x + 2 * y + c
        land[me] = in_ref[...]
        cps = []
        for k in range(1, N_DEV):
            px, py, pc = x ^ ((k >> 2) & 1), y ^ ((k >> 1) & 1), c ^ (k & 1)
            cp = pltpu.make_async_remote_copy(
                src_ref=in_ref, dst_ref=land.at[me], send_sem=send.at[k - 1], recv_sem=recv.at[k - 1],
                device_id=(px, py, pc), device_id_type=MESH)
            cp.start()
            cps.append(cp)
        for k in range(1, N_DEV):
            px, py, pc = x ^ ((k >> 2) & 1), y ^ ((k >> 1) & 1), c ^ (k & 1)
            pltpu.make_async_remote_copy(
                src_ref=in_ref, dst_ref=land.at[4 * px + 2 * py + pc], send_sem=send.at[k - 1],
                recv_sem=recv.at[k - 1], device_id=(px, py, pc), device_id_type=MESH).wait_recv()
        for cp in cps:
            cp.wait_send()
        acc = land[0]
        for k in range(1, N_DEV):
            acc = acc + land[k]
        out_ref[...] = acc

    vm = pl.BlockSpec(memory_space=pltpu.VMEM)
    return pl.pallas_call(
        body, name=name, out_shape=jax.ShapeDtypeStruct((r, LANES), F32), in_specs=[vm], out_specs=vm,
        scratch_shapes=[pltpu.VMEM((N_DEV, r, LANES), F32), pltpu.SemaphoreType.DMA((N_DEV - 1,)),
                        pltpu.SemaphoreType.DMA((N_DEV - 1,))],
    )(buf)


def _row_tile(rows, cols, n_arrays, budget=20 * 1024 * 1024):
    best = 8 if rows % 8 == 0 else rows
    tr = 8
    while tr <= rows:
        if rows % tr == 0 and tr * cols * 4 * n_arrays * 2 <= budget:
            best = tr
        tr *= 2
    return best


def _add_slot_layer(full, other, *, name):
    _, _, r, cdim = full.shape
    tr = _row_tile(r, cdim, 4)

    def body(c_ref, a_ref, b_ref, o_ref, ob_ref):
        sm = a_ref[...] + b_ref[...]
        o_ref[...] = sm
        ob_ref[...] = sm.astype(BF16)

    c = lax.axis_index("c").astype(jnp.int32).reshape(1)
    blk = pl.BlockSpec((None, tr, cdim), lambda k, i, c_ref: (k, i, 0))
    return pl.pallas_call(
        body, name=name,
        grid_spec=pltpu.PrefetchScalarGridSpec(
            num_scalar_prefetch=1, grid=(N_CHIPS, r // tr),
            in_specs=[pl.BlockSpec((None, None, tr, cdim), lambda k, i, c_ref: (k, c_ref[0], i, 0)), blk],
            out_specs=[blk, blk]),
        out_shape=[jax.ShapeDtypeStruct((N_CHIPS, r, cdim), F32), jax.ShapeDtypeStruct((N_CHIPS, r, cdim), BF16)],
        compiler_params=_cp(("parallel", "parallel")),
    )(c, full, other)


def _sum_slots(parts, pair, *, name):
    _, r, cdim = parts.shape
    tr = _row_tile(r, cdim, 5)

    def body(me_ref, p_ref, own_ref, o_ref):
        me = me_ref[0]
        acc = None
        for k in range(N_CHIPS):
            term = jnp.where(me == k, own_ref[...], p_ref[k].astype(F32))
            acc = term if acc is None else acc + term
        o_ref[...] = acc

    me = (2 * lax.axis_index("x") + lax.axis_index("y")).astype(jnp.int32).reshape(1)
    return pl.pallas_call(
        body, name=name,
        grid_spec=pltpu.PrefetchScalarGridSpec(
            num_scalar_prefetch=1, grid=(r // tr,),
            in_specs=[pl.BlockSpec((N_CHIPS, tr, cdim), lambda i, me_ref: (0, i, 0)),
                      pl.BlockSpec((None, tr, cdim), lambda i, me_ref: (me_ref[0], i, 0))],
            out_specs=pl.BlockSpec((tr, cdim), lambda i, me_ref: (i, 0))),
        out_shape=jax.ShapeDtypeStruct((r, cdim), F32),
        compiler_params=_cp(("parallel",)),
    )(me, parts, pair)


def _assemble_w_proj(own, gathered, li, *, name):
    _, nb, r, cdim = own.shape

    def body(chip_ref, own_ref, slot_ref, o_ref):
        o_ref[...] = jnp.where(chip_ref[0] == pl.program_id(1), own_ref[...], slot_ref[...])

    chip = (2 * lax.axis_index("x") + lax.axis_index("y")).astype(jnp.int32).reshape(1)
    return pl.pallas_call(
        body, name=name,
        grid_spec=pltpu.PrefetchScalarGridSpec(
            num_scalar_prefetch=1, grid=(nb, N_CHIPS),
            in_specs=[pl.BlockSpec((None, None, r, cdim), lambda i, k, chip_ref: (li, i, 0, 0)),
                      pl.BlockSpec((None, None, None, r, cdim), lambda i, k, chip_ref: (k, li, i, 0, 0))],
            out_specs=pl.BlockSpec((None, r, cdim), lambda i, k, chip_ref: (i, k, 0))),
        out_shape=jax.ShapeDtypeStruct((nb, N_CHIPS * r, cdim), own.dtype),
        compiler_params=_cp(("parallel", "parallel")),
    )(chip, own, gathered)


def _adamw(w, g, m, v, *, name):
    lead, (r, cdim) = w.shape[:-2], w.shape[-2:]
    nl = len(lead)
    tr = _row_tile(r, cdim, 7)
    tc = cdim
    if tr < 64 < r and cdim % LANES == 0:
        tr, tc = r, LANES
    c1 = 1.0 - ADAM_B1 ** ADAM_STEP
    c2 = 1.0 - ADAM_B2 ** ADAM_STEP

    def body(w_ref, g_ref, m_ref, v_ref, d_ref, nm_ref, nv_ref):
        gv = g_ref[...]
        mn = ADAM_B1 * m_ref[...] + (1.0 - ADAM_B1) * gv
        vn = ADAM_B2 * v_ref[...] + (1.0 - ADAM_B2) * (gv * gv)
        nm_ref[...] = mn
        nv_ref[...] = vn
        d_ref[...] = -ADAM_LR * ((mn / c1) / (jnp.sqrt(vn / c2) + ADAM_EPS) + ADAM_WD * w_ref[...])

    blk = pl.BlockSpec((None,) * nl + (tr, tc), lambda *ids: ids[:nl] + (ids[nl], ids[nl + 1]))
    sh = jax.ShapeDtypeStruct(w.shape, F32)
    return pl.pallas_call(
        body, name=name, grid=lead + (r // tr, cdim // tc), in_specs=[blk] * 4, out_specs=[blk] * 3,
        out_shape=[sh] * 3, compiler_params=_cp(("parallel",) * (nl + 2)),
    )(w, g, m, v)


_SMALL = ("norm_w", "conv_b", "dt_bias", "a_log", "d_skip", "ssm_norm_w", "sinks", "f_bias", "final_norm_w",
          "conv_w", "gate_bias")


def _pack(vals):
    flat = jnp.concatenate([v.reshape(-1) for v in vals])
    rows = -(-flat.shape[0] // LANES)
    rows = -(-rows // 8) * 8
    return jnp.pad(flat, (0, rows * LANES - flat.shape[0])).reshape(rows, LANES)


def _unpack(buf, shapes):
    flat = buf.reshape(-1)
    out, off = [], 0
    for sh in shapes:
        sz = int(np.prod(sh))
        out.append(flat[off:off + sz].reshape(sh))
        off += sz
    return out


def kernel(x, norm_w, w_in, conv_w, conv_b, dt_bias, a_log, d_skip, ssm_norm_w, sinks, f_bias, gate_bias, w_proj, w_out, final_norm_w, loss_target, m_norm_w, m_w_in, m_conv_w, m_conv_b, m_dt_bias, m_a_log, m_d_skip, m_ssm_norm_w, m_sinks, m_f_bias, m_gate_bias, m_w_proj, m_w_out, m_final_norm_w, v_norm_w, v_w_in, v_conv_w, v_conv_b, v_dt_bias, v_a_log, v_d_skip, v_ssm_norm_w, v_sinks, v_f_bias, v_gate_bias, v_w_proj, v_w_out, v_final_norm_w):
    depth = w_in.shape[0]
    chip = 2 * lax.axis_index("x") + lax.axis_index("y")

    own = [w_in.astype(BF16), w_proj.astype(BF16), w_out.astype(BF16), conv_w, gate_bias]
    gathered = _gather_weights(own, name="gather_weights")

    def whole(a, li, axis):
        return jnp.concatenate([jnp.where(chip == k, own[a][li], gathered[a][k, li]) for k in range(N_CHIPS)],
                               axis=axis)

    wls = []
    for li in range(depth):
        wls.append(dict(
            norm_w=norm_w[li], w_in=_pad_w_in(whole(0, li, 1)),
            conv_w=whole(3, li, 1), conv_b=conv_b[li], dt_bias=dt_bias[li], a_log=a_log[li], d_skip=d_skip[li],
            ssm_norm_w=ssm_norm_w[li], sinks=sinks[li], f_bias=f_bias[li], gate_bias=whole(4, li, 1),
            w_proj=_assemble_w_proj(own[1], gathered[1], li, name=f"l{li}_assemble_w_proj"),
            w_out=whole(2, li, 0)))

    loss_part, grad_x, grads, d_final = _local_step(x, loss_target, wls, final_norm_w)
    loss = lax.psum(loss_part, ("x", "y", "c"))

    c_in = w_in.shape[2]
    r_proj = w_proj.shape[2]
    r_out = w_out.shape[1]
    full_in = jnp.stack([jnp.stack([grads[li]["w_in"][:, k * c_in:(k + 1) * c_in] for li in range(depth)])
                         for k in range(N_CHIPS)])
    full_proj = jnp.stack([jnp.stack([grads[li]["w_proj"][:, k * r_proj:(k + 1) * r_proj].reshape(-1, D_MODEL)
                                      for li in range(depth)]) for k in range(N_CHIPS)])
    full_out = jnp.stack([jnp.stack([grads[li]["w_out"][k * r_out:(k + 1) * r_out] for li in range(depth)])
                          for k in range(N_CHIPS)])
    fulls = [full_in, full_proj, full_out]
    others = _pair_exchange(fulls, name="grad_pair_exchange")
    pair = [_add_slot_layer(f, o, name=f"grad_pair_add{i}") for i, (f, o) in enumerate(zip(fulls, others))]
    parts = _chip_exchange([p[1] for p in pair], name="grad_chip_exchange")
    mine = [_sum_slots(p, pr[0], name=f"grad_slot_sum{i}") for i, (p, pr) in enumerate(zip(parts, pair))]
    theirs = _pair_share(mine, name="grad_pair_share")
    core = lax.axis_index("c")
    red_in, red_proj, red_out = [jnp.stack([jnp.where(core == li, m, t) for li in range(depth)])
                                 for m, t in zip(mine, theirs)]
    grad_w_in = red_in
    grad_w_proj = red_proj.reshape(w_proj.shape)
    grad_w_out = red_out

    small_full = {
        "norm_w": jnp.stack([g["norm_w"] for g in grads]), "conv_b": jnp.stack([g["conv_b"] for g in grads]),
        "dt_bias": jnp.stack([g["dt_bias"] for g in grads]), "a_log": jnp.stack([g["a_log"] for g in grads]),
        "d_skip": jnp.stack([g["d_skip"] for g in grads]),
        "ssm_norm_w": jnp.stack([g["ssm_norm_w"] for g in grads]),
        "sinks": jnp.stack([g["sinks"] for g in grads]), "f_bias": jnp.stack([g["f_bias"] for g in grads]),
        "final_norm_w": d_final,
        "conv_w": jnp.stack([g["conv_w"] for g in grads]), "gate_bias": jnp.stack([g["gate_bias"] for g in grads])}
    shapes = [small_full[k].shape for k in _SMALL]
    summed = _unpack(_allreduce_small(_pack([small_full[k] for k in _SMALL]), name="allreduce_small"), shapes)
    gsmall = dict(zip(_SMALL, summed))
    gsmall["conv_w"] = lax.dynamic_slice_in_dim(gsmall["conv_w"], chip * conv_w.shape[2], conv_w.shape[2], axis=2)
    gsmall["gate_bias"] = lax.dynamic_slice_in_dim(gsmall["gate_bias"], chip * gate_bias.shape[2],
                                                   gate_bias.shape[2], axis=2)

    w_small = dict(norm_w=norm_w, conv_b=conv_b, dt_bias=dt_bias, a_log=a_log, d_skip=d_skip,
                   ssm_norm_w=ssm_norm_w, sinks=sinks, f_bias=f_bias, final_norm_w=final_norm_w, conv_w=conv_w,
                   gate_bias=gate_bias)
    m_small = dict(norm_w=m_norm_w, conv_b=m_conv_b, dt_bias=m_dt_bias, a_log=m_a_log, d_skip=m_d_skip,
                   ssm_norm_w=m_ssm_norm_w, sinks=m_sinks, f_bias=m_f_bias, final_norm_w=m_final_norm_w,
                   conv_w=m_conv_w, gate_bias=m_gate_bias)
    v_small = dict(norm_w=v_norm_w, conv_b=v_conv_b, dt_bias=v_dt_bias, a_log=v_a_log, d_skip=v_d_skip,
                   ssm_norm_w=v_ssm_norm_w, sinks=v_sinks, f_bias=v_f_bias, final_norm_w=v_final_norm_w,
                   conv_w=v_conv_w, gate_bias=v_gate_bias)
    sshapes = [w_small[k].shape for k in _SMALL]
    ds, ms, vs = _adamw(_pack([w_small[k] for k in _SMALL]), _pack([gsmall[k] for k in _SMALL]),
                        _pack([m_small[k] for k in _SMALL]), _pack([v_small[k] for k in _SMALL]), name="adamw_small")
    delta = dict(zip(_SMALL, _unpack(ds, sshapes)))
    new_m = dict(zip(_SMALL, _unpack(ms, sshapes)))
    new_v = dict(zip(_SMALL, _unpack(vs, sshapes)))
    grad = dict(gsmall)
    for nm, w, g, m, v in (("w_proj", w_proj, grad_w_proj, m_w_proj, v_w_proj),
                           ("w_out", w_out, grad_w_out, m_w_out, v_w_out)):
        grad[nm] = g
        delta[nm], new_m[nm], new_v[nm] = _adamw(w, g, m, v, name=f"adamw_{nm}")
    tview = lambda a: jnp.transpose(a, (0, 2, 1))
    grad["w_in"] = grad_w_in
    delta["w_in"], new_m["w_in"], new_v["w_in"] = [
        tview(a) for a in _adamw(tview(w_in), tview(grad_w_in), tview(m_w_in), tview(v_w_in), name="adamw_w_in")]

    order = ("norm_w", "w_in", "conv_w", "conv_b", "dt_bias", "a_log", "d_skip", "ssm_norm_w", "sinks", "f_bias",
             "gate_bias", "w_proj", "w_out", "final_norm_w")
    return (loss, grad_x, *[grad[k] for k in order], *[delta[k] for k in order],
            *[new_m[k] for k in order], *[new_v[k] for k in order])
```

```python
import numpy as np
import jax
import jax.numpy as jnp
from jax import lax
from jax.experimental import pallas as pl
from jax.experimental.pallas import tpu as pltpu

F32 = jnp.float32
BF16 = jnp.bfloat16
HIGHEST = lax.Precision.HIGHEST
MESH = pl.DeviceIdType.MESH

D_MODEL = 1024
HEAD_DIM = 64
N_HEADS = 16
N_GROUPS = 4
SSM_STATE = 128
CHUNK = 128
CONV_WIDTH = 4
CONV_DIM = 2048
ROPE_THETA = 10000.0
NORM_EPS = 1e-6
LANES = 128
N_CHIPS = 4
N_DEV = 8

ADAM_LR = 0.001
ADAM_B1 = 0.9
ADAM_B2 = 0.999
ADAM_EPS = 1e-08
ADAM_WD = 0.01
ADAM_STEP = 10

_REF_COLS = {}
_off = 0
for _n, _s in (("xbc", 2048), ("a_z", 1024), ("a_dt", 16), ("b_q", 1024), ("b_k", 256), ("b_v", 256),
               ("b_z", 1024), ("c_q", 1024), ("c_k", 1024), ("c_v", 1024), ("c_f", 16), ("c_z", 1024),
               ("gates", 3072)):
    _REF_COLS[_n] = (_off, _s)
    _off += _s

_PAD_ORDER = (("gates", 3072), ("xbc", 2048), ("a_z", 1024), ("b_q", 1024), ("b_z", 1024), ("c_q", 1024),
              ("c_k", 1024), ("c_v", 1024), ("c_z", 1024), ("b_k", 256), ("b_v", 256), ("a_dt", 512),
              ("c_f", 128))
_PAD_COLS = {}
_off = 0
for _n, _s in _PAD_ORDER:
    _PAD_COLS[_n] = (_off, _s)
    _off += _s
N_USED = _off
N_PAD = 13824


def _cp(sem, vmem_mb=48):
    return pltpu.CompilerParams(dimension_semantics=sem, vmem_limit_bytes=vmem_mb * 1024 * 1024)


def _dot(a, b, dims=((1,), (0,)), precision=None):
    return lax.dot_general(a, b, (dims, ((), ())), preferred_element_type=F32, precision=precision)


def _dot_nt(a, b):
    return _dot(a, b, ((1,), (1,)))


def _dot_tn(a, b):
    return _dot(a, b, ((0,), (0,)))


def _col(v, idx):
    lane = lax.broadcasted_iota(jnp.int32, v.shape, 1)
    return jnp.sum(jnp.where(lane == idx, v, 0.0), axis=1, keepdims=True)


def _row(v, idx):
    row = lax.broadcasted_iota(jnp.int32, v.shape, 0)
    return jnp.sum(jnp.where(row == idx, v, 0.0), axis=0, keepdims=True)


def _iota_col():
    return lax.broadcasted_iota(jnp.int32, (CHUNK, 1), 0)


def _iota_row():
    return lax.broadcasted_iota(jnp.int32, (1, LANES), 1)


def _sigmoid(x):
    return 1.0 / (1.0 + jnp.exp(-x))


def _softplus(x):
    return jnp.maximum(x, 0.0) + jnp.log(1.0 + jnp.exp(-jnp.abs(x)))


def _pad_w_in(w):
    parts = []
    for name, size in _PAD_ORDER:
        s0, sz = _REF_COLS[name]
        seg = w[:, s0:s0 + sz]
        if name == "a_dt":
            seg = jnp.pad(seg.reshape(-1, N_GROUPS, 4), ((0, 0), (0, 0), (0, LANES - 4))).reshape(-1, 512)
        elif name == "c_f":
            seg = jnp.pad(seg, ((0, 0), (0, LANES - 16)))
        parts.append(seg)
    parts.append(jnp.zeros((w.shape[0], N_PAD - N_USED), w.dtype))
    return jnp.concatenate(parts, axis=1)


def _unpad_w_in(wp):
    segs = {}
    for name, _ in _PAD_ORDER:
        p0, psz = _PAD_COLS[name]
        seg = wp[:, p0:p0 + psz]
        if name == "a_dt":
            seg = seg.reshape(-1, N_GROUPS, LANES)[:, :, :4].reshape(-1, 16)
        elif name == "c_f":
            seg = seg[:, :16]
        segs[name] = seg
    order = sorted(_REF_COLS, key=lambda n: _REF_COLS[n][0])
    return jnp.concatenate([segs[n] for n in order], axis=1)


def _group_lanes(v):
    return jnp.pad(v.reshape(N_GROUPS, 1, 4), ((0, 0), (0, 0), (0, LANES - 4)))


def _ungroup_lanes(v):
    return v[:, 0, :4].reshape(16)


def _mm(a, b, *, ta=False, tb=False, tm=512, tn=512, tk=512, out_dtype=F32, name):
    if ta:
        kdim, m = a.shape
    else:
        m, kdim = a.shape
    if tb:
        n, k2 = b.shape
    else:
        k2, n = b.shape
    assert kdim == k2, (a.shape, b.shape)
    tm, tn, tk = min(tm, m), min(tn, n), min(tk, kdim)
    assert m % tm == 0 and n % tn == 0 and kdim % tk == 0, (m, n, kdim, tm, tn, tk)
    nk = kdim // tk
    a_spec = (pl.BlockSpec((tk, tm), lambda i, j, k: (k, i)) if ta
              else pl.BlockSpec((tm, tk), lambda i, j, k: (i, k)))
    b_spec = (pl.BlockSpec((tn, tk), lambda i, j, k: (j, k)) if tb
              else pl.BlockSpec((tk, tn), lambda i, j, k: (k, j)))
    dims = ((0 if ta else 1,), (1 if tb else 0,))

    def body(a_ref, b_ref, o_ref, acc_ref):
        k = pl.program_id(2)
        p = _dot(a_ref[...].astype(BF16), b_ref[...].astype(BF16), dims)

        @pl.when(k == 0)
        def _():
            acc_ref[...] = p

        @pl.when(k > 0)
        def _():
            acc_ref[...] += p

        @pl.when(k == nk - 1)
        def _():
            o_ref[...] = acc_ref[...].astype(out_dtype)

    return pl.pallas_call(
        body, name=name, grid=(m // tm, n // tn, nk),
        in_specs=[a_spec, b_spec], out_specs=pl.BlockSpec((tm, tn), lambda i, j, k: (i, j)),
        out_shape=jax.ShapeDtypeStruct((m, n), out_dtype),
        scratch_shapes=[pltpu.VMEM((tm, tn), F32)],
        compiler_params=_cp(("parallel", "parallel", "arbitrary")),
    )(a, b)


def _rms_fwd(x, w, *, name, tm=512):
    t, d = x.shape

    def body(x_ref, w_ref, o_ref, ot_ref):
        xv = x_ref[...]
        r = lax.rsqrt(jnp.mean(xv * xv, axis=1, keepdims=True) + NORM_EPS)
        h = xv * r * w_ref[...]
        o_ref[...] = h.astype(BF16)
        ot_ref[...] = h.T.astype(BF16)

    return pl.pallas_call(
        body, name=name, grid=(t // tm,),
        in_specs=[pl.BlockSpec((tm, d), lambda i: (i, 0)), pl.BlockSpec((1, d), lambda i: (0, 0))],
        out_specs=[pl.BlockSpec((tm, d), lambda i: (i, 0)), pl.BlockSpec((d, tm), lambda i: (0, i))],
        out_shape=[jax.ShapeDtypeStruct((t, d), BF16), jax.ShapeDtypeStruct((d, t), BF16)],
        compiler_params=_cp(("parallel",)),
    )(x, w.reshape(1, d))


def _rms_bwd(x, w, dh, dres, *, name, tm=512):
    t, d = x.shape

    def body(x_ref, w_ref, dh_ref, dres_ref, dx_ref, dw_ref):
        xv = x_ref[...]
        r = lax.rsqrt(jnp.mean(xv * xv, axis=1, keepdims=True) + NORM_EPS)
        xhat = xv * r
        dhv = dh_ref[...]
        dxhat = dhv * w_ref[...]
        dx = r * (dxhat - xhat * jnp.mean(dxhat * xhat, axis=1, keepdims=True))
        dx_ref[...] = dres_ref[...] + dx

        @pl.when(pl.program_id(0) == 0)
        def _():
            dw_ref[...] = jnp.zeros_like(dw_ref)

        dw_ref[...] += jnp.sum(dhv * xhat, axis=0, keepdims=True)

    return pl.pallas_call(
        body, name=name, grid=(t // tm,),
        in_specs=[pl.BlockSpec((tm, d), lambda i: (i, 0)), pl.BlockSpec((1, d), lambda i: (0, 0)),
                  pl.BlockSpec((tm, d), lambda i: (i, 0)), pl.BlockSpec((tm, d), lambda i: (i, 0))],
        out_specs=[pl.BlockSpec((tm, d), lambda i: (i, 0)), pl.BlockSpec((1, d), lambda i: (0, 0))],
        out_shape=[jax.ShapeDtypeStruct((t, d), F32), jax.ShapeDtypeStruct((1, d), F32)],
        compiler_params=_cp(("arbitrary",)),
    )(x, w.reshape(1, d), dh, dres)


def _final_loss(x, w, target, *, name, tm=512):
    t, d = x.shape

    def body(x_ref, w_ref, t_ref, loss_ref, dx_ref, dw_ref):
        xv = x_ref[...]
        wv = w_ref[...]
        r = lax.rsqrt(jnp.mean(xv * xv, axis=1, keepdims=True) + NORM_EPS)
        xhat = xv * r
        err = xhat * wv - t_ref[...]
        dy = err * (1.0 / d)
        dxhat = dy * wv
        dx_ref[...] = r * (dxhat - xhat * jnp.mean(dxhat * xhat, axis=1, keepdims=True))

        @pl.when(pl.program_id(0) == 0)
        def _():
            dw_ref[...] = jnp.zeros_like(dw_ref)
            loss_ref[...] = jnp.zeros_like(loss_ref)

        dw_ref[...] += jnp.sum(dy * xhat, axis=0, keepdims=True)
        part = 0.5 * jnp.sum(jnp.mean(err * err, axis=1, keepdims=True), axis=0, keepdims=True)
        loss_ref[...] += jnp.broadcast_to(part, loss_ref.shape)

    return pl.pallas_call(
        body, name=name, grid=(t // tm,),
        in_specs=[pl.BlockSpec((tm, d), lambda i: (i, 0)), pl.BlockSpec((1, d), lambda i: (0, 0)),
                  pl.BlockSpec((tm, d), lambda i: (i, 0))],
        out_specs=[pl.BlockSpec((8, LANES), lambda i: (0, 0)), pl.BlockSpec((tm, d), lambda i: (i, 0)),
                   pl.BlockSpec((1, d), lambda i: (0, 0))],
        out_shape=[jax.ShapeDtypeStruct((8, LANES), F32), jax.ShapeDtypeStruct((t, d), F32),
                   jax.ShapeDtypeStruct((1, d), F32)],
        compiler_params=_cp(("arbitrary",)),
    )(x, w.reshape(1, d), target)


_CB = 128


def _conv_pre(u, w_ref, b_ref):
    s = u.shape[0]
    row = lax.broadcasted_iota(jnp.int32, u.shape, 0)
    pre = b_ref[...] + w_ref[CONV_WIDTH - 1:CONV_WIDTH, :] * u
    for sh in range(1, CONV_WIDTH):
        shifted = jnp.where(row >= sh, pltpu.roll(u, sh, 0), 0.0)
        pre = pre + w_ref[CONV_WIDTH - 1 - sh:CONV_WIDTH - sh, :] * shifted
    return pre


def _conv_fwd(proj3, cw, cb, *, name):
    b, s, _ = proj3.shape
    c0 = _PAD_COLS["xbc"][0] // _CB

    def body(u_ref, w_ref, b_ref, o_ref):
        pre = _conv_pre(u_ref[...].astype(F32), w_ref, b_ref)
        o_ref[...] = pre * _sigmoid(pre)

    return pl.pallas_call(
        body, name=name, grid=(b, CONV_DIM // _CB),
        in_specs=[pl.BlockSpec((None, s, _CB), lambda i, j: (i, 0, c0 + j)),
                  pl.BlockSpec((CONV_WIDTH, _CB), lambda i, j: (0, j)),
                  pl.BlockSpec((1, _CB), lambda i, j: (0, j))],
        out_specs=pl.BlockSpec((None, s, _CB), lambda i, j: (i, 0, j)),
        out_shape=jax.ShapeDtypeStruct((b, s, CONV_DIM), F32),
        compiler_params=_cp(("parallel", "parallel")),
    )(proj3, cw, cb.reshape(1, CONV_DIM))


def _conv_bwd(proj3, cw, cb, dact, *, name):
    b, s, _ = proj3.shape
    c0 = _PAD_COLS["xbc"][0] // _CB

    def body(u_ref, w_ref, b_ref, da_ref, du_ref, dwb_ref):
        u = u_ref[...].astype(F32)
        pre = _conv_pre(u, w_ref, b_ref)
        sg = _sigmoid(pre)
        dpre = da_ref[...] * (sg * (1.0 + pre * (1.0 - sg)))
        row = lax.broadcasted_iota(jnp.int32, u.shape, 0)
        du = w_ref[CONV_WIDTH - 1:CONV_WIDTH, :] * dpre
        rows = [jnp.sum(dpre * u, axis=0, keepdims=True)]
        for sh in range(1, CONV_WIDTH):
            fwd_shift = jnp.where(row < s - sh, pltpu.roll(dpre, s - sh, 0), 0.0)
            du = du + w_ref[CONV_WIDTH - 1 - sh:CONV_WIDTH - sh, :] * fwd_shift
            ush = jnp.where(row >= sh, pltpu.roll(u, sh, 0), 0.0)
            rows.append(jnp.sum(dpre * ush, axis=0, keepdims=True))
        du_ref[...] = du.astype(BF16)

        @pl.when(pl.program_id(1) == 0)
        def _():
            dwb_ref[...] = jnp.zeros_like(dwb_ref)

        for sh in range(CONV_WIDTH):
            k = CONV_WIDTH - 1 - sh
            dwb_ref[k:k + 1, :] += rows[sh]
        dwb_ref[CONV_WIDTH:CONV_WIDTH + 1, :] += jnp.sum(dpre, axis=0, keepdims=True)

    return pl.pallas_call(
        body, name=name, grid=(CONV_DIM // _CB, b),
        in_specs=[pl.BlockSpec((None, s, _CB), lambda j, i: (i, 0, c0 + j)),
                  pl.BlockSpec((CONV_WIDTH, _CB), lambda j, i: (0, j)),
                  pl.BlockSpec((1, _CB), lambda j, i: (0, j)),
                  pl.BlockSpec((None, s, _CB), lambda j, i: (i, 0, j))],
        out_specs=[pl.BlockSpec((None, s, _CB), lambda j, i: (i, 0, j)),
                   pl.BlockSpec((8, _CB), lambda j, i: (0, j))],
        out_shape=[jax.ShapeDtypeStruct((b, s, CONV_DIM), BF16), jax.ShapeDtypeStruct((8, CONV_DIM), F32)],
        compiler_params=_cp(("parallel", "arbitrary")),
    )(proj3, cw, cb.reshape(1, CONV_DIM), dact)


def _ssd_common(dt_ref, dtb_ref, alog_ref):
    row = lax.broadcasted_iota(jnp.int32, (CHUNK, CHUNK), 0)
    lane = lax.broadcasted_iota(jnp.int32, (CHUNK, CHUNK), 1)
    causal = row >= lane
    tri = causal.astype(F32)
    dtv = _softplus(dt_ref[...] + dtb_ref[...])
    a_row = -jnp.exp(alog_ref[...])
    acum = _dot(tri, dtv * a_row, precision=HIGHEST)
    return row, lane, causal, dtv, a_row, acum, acum.T


def _ssd_pair(pp, x, dtv, acum, acum_t, causal, lane, row):
    lo = lane < HEAD_DIM
    r0, r1 = 2 * pp, 2 * pp + 1
    dtp = jnp.where(lo, _col(dtv, r0), _col(dtv, r1))
    ac0, ac1 = _col(acum, r0), _col(acum, r1)
    ar0, ar1 = _row(acum_t, r0), _row(acum_t, r1)
    d0 = jnp.where(causal, jnp.exp(jnp.where(causal, ac0 - ar0, 0.0)), 0.0)
    d1 = jnp.where(causal, jnp.exp(jnp.where(causal, ac1 - ar1, 0.0)), 0.0)
    al0, al1 = _col(ar0, CHUNK - 1), _col(ar1, CHUNK - 1)
    eac = jnp.where(lo, jnp.exp(ac0), jnp.exp(ac1))
    dsp = jnp.where(lo, jnp.exp(al0 - ac0), jnp.exp(al1 - ac1))
    eal = jnp.where(_iota_col() < HEAD_DIM, jnp.exp(al0), jnp.exp(al1))
    return lo, dtp, x * dtp, d0, d1, al0, al1, eac, dsp, eal


def _ssd_fwd(proj3, gates3, xact3, dtb, alog, dsk, nw, *, name):
    b, s, _ = proj3.shape
    nc = s // CHUNK
    dt0 = 0
    z0 = _PAD_COLS["a_z"][0] // D_MODEL

    def body(xs_ref, bm_ref, cm_ref, dt_ref, z_ref, dtb_ref, alog_ref, dsk_ref, nw_ref,
             ya_ref, ypre_ref, hst_ref, h_scr):
        @pl.when(pl.program_id(1) == 0)
        def _():
            h_scr[...] = jnp.zeros_like(h_scr)

        for g in range(N_GROUPS):
            w256 = pl.ds(256 * g, 256)
            w128 = pl.ds(LANES * g, LANES)
            group(xs_ref.at[:, w256], bm_ref.at[:, w128], cm_ref.at[:, w128], dt_ref.at[:, w128],
                  z_ref.at[:, w256], dtb_ref.at[g], alog_ref.at[g], dsk_ref.at[g], nw_ref.at[g],
                  ya_ref.at[:, w256], ypre_ref.at[:, w256], hst_ref.at[g], h_scr.at[g])

    def group(xs_ref, bm_ref, cm_ref, dt_ref, z_ref, dtb_ref, alog_ref, dsk_ref, nw_ref,
              ya_ref, ypre_ref, hst_ref, h_scr):
        row, lane, causal, dtv, a_row, acum, acum_t = _ssd_common(dt_ref, dtb_ref, alog_ref)
        bb = bm_ref[...].astype(BF16)
        cb = cm_ref[...].astype(BF16)
        cbm = _dot_nt(cb, bb)
        hst_ref[...] = h_scr[...]
        dskv = dsk_ref[...]
        for pp in range(2):
            x = xs_ref[:, LANES * pp:LANES * (pp + 1)]
            lo, dtp, xd, d0, d1, al0, al1, eac, dsp, eal = _ssd_pair(pp, x, dtv, acum, acum_t, causal, lane, row)
            xdb = xd.astype(BF16)
            y = jnp.where(lo, _dot((cbm * d0).astype(BF16), xdb), _dot((cbm * d1).astype(BF16), xdb))
            h = h_scr[pp]
            y = y + eac * _dot_nt(cb, h.astype(BF16))
            h_scr[pp] = h * eal + _dot_tn((xd * dsp).astype(BF16), bb)
            dskp = jnp.where((_iota_row() < HEAD_DIM), _col(dskv, 2 * pp), _col(dskv, 2 * pp + 1))
            ypre_ref[:, LANES * pp:LANES * (pp + 1)] = y + x * dskp
        ypre = ypre_ref[...]
        z = z_ref[...].astype(F32)
        yg = ypre * (z * _sigmoid(z))
        rstd = lax.rsqrt(jnp.sum(yg * yg, axis=1, keepdims=True) * (1.0 / 256.0) + NORM_EPS)
        ya_ref[...] = (yg * rstd * nw_ref[...]).astype(BF16)

    g = N_GROUPS
    par = pl.BlockSpec((g, 1, LANES), lambda i, c: (0, 0, 0))
    wide = pl.BlockSpec((None, CHUNK, D_MODEL), lambda i, c: (i, c, 0))
    return pl.pallas_call(
        body, name=name, grid=(b, nc),
        in_specs=[wide,
                  pl.BlockSpec((None, CHUNK, 512), lambda i, c: (i, c, 2)),
                  pl.BlockSpec((None, CHUNK, 512), lambda i, c: (i, c, 3)),
                  pl.BlockSpec((None, CHUNK, 512), lambda i, c: (i, c, dt0)),
                  pl.BlockSpec((None, CHUNK, D_MODEL), lambda i, c: (i, c, z0)),
                  par, par, par,
                  pl.BlockSpec((g, 1, 256), lambda i, c: (0, 0, 0))],
        out_specs=[wide, wide,
                   pl.BlockSpec((None, None, g, 2, CHUNK, SSM_STATE), lambda i, c: (i, c, 0, 0, 0, 0))],
        out_shape=[jax.ShapeDtypeStruct((b, s, D_MODEL), BF16), jax.ShapeDtypeStruct((b, s, D_MODEL), F32),
                   jax.ShapeDtypeStruct((b, nc, g, 2, CHUNK, SSM_STATE), F32)],
        scratch_shapes=[pltpu.VMEM((g, 2, CHUNK, SSM_STATE), F32)],
        compiler_params=_cp(("parallel", "arbitrary")),
    )(xact3, xact3, xact3, gates3, proj3, dtb, alog, dsk, nw)


def _ssd_bwd(proj3, gates3, xact3, dtb, alog, dsk, nw, ypre3, hst, dya3, *, name):
    b, s, _ = proj3.shape
    nc = s // CHUNK
    dt0 = 0
    z0 = _PAD_COLS["a_z"][0] // D_MODEL

    def body(xs_ref, bm_ref, cm_ref, dt_ref, z_ref, dtb_ref, alog_ref, dsk_ref, nw_ref, ypre_ref, hst_ref,
             dya_ref, dact_ref, dz_ref, ddt_ref, ddtb_ref, dalog_ref, ddsk_ref, dnw_ref, dh_scr):
        first = jnp.logical_and(pl.program_id(0) == 0, pl.program_id(1) == 0)

        @pl.when(first)
        def _():
            ddtb_ref[...] = jnp.zeros_like(ddtb_ref)
            dalog_ref[...] = jnp.zeros_like(dalog_ref)
            ddsk_ref[...] = jnp.zeros_like(ddsk_ref)
            dnw_ref[...] = jnp.zeros_like(dnw_ref)

        @pl.when(pl.program_id(1) == 0)
        def _():
            dh_scr[...] = jnp.zeros_like(dh_scr)

        for g in range(N_GROUPS):
            w256 = pl.ds(256 * g, 256)
            w128 = pl.ds(LANES * g, LANES)
            group(xs_ref.at[:, w256], bm_ref.at[:, w128], cm_ref.at[:, w128], dt_ref.at[:, w128],
                  z_ref.at[:, w256], dtb_ref.at[g], alog_ref.at[g], dsk_ref.at[g], nw_ref.at[g],
                  ypre_ref.at[:, w256], hst_ref.at[g], dya_ref.at[:, w256],
                  dact_ref.at[:, w256], dact_ref.at[:, pl.ds(D_MODEL + LANES * g, LANES)],
                  dact_ref.at[:, pl.ds(D_MODEL + 512 + LANES * g, LANES)], dz_ref.at[:, w256], ddt_ref.at[:, w128],
                  ddtb_ref.at[g], dalog_ref.at[g], ddsk_ref.at[g], dnw_ref.at[g], dh_scr.at[g])

    def group(xs_ref, bm_ref, cm_ref, dt_ref, z_ref, dtb_ref, alog_ref, dsk_ref, nw_ref, ypre_ref, hst_ref,
              dya_ref, dxs_ref, dbm_ref, dcm_ref, dz_ref, ddt_ref, ddtb_ref, dalog_ref, ddsk_ref, dnw_ref,
              dh_scr):
        row, lane, causal, dtv, a_row, acum, acum_t = _ssd_common(dt_ref, dtb_ref, alog_ref)
        lane1 = _iota_row()
        bb = bm_ref[...].astype(BF16)
        cb = cm_ref[...].astype(BF16)
        cbm = _dot_nt(cb, bb)

        z = z_ref[...].astype(F32)
        ypre = ypre_ref[...]
        dya = dya_ref[...]
        sz = _sigmoid(z)
        silu = z * sz
        yg = ypre * silu
        rstd = lax.rsqrt(jnp.sum(yg * yg, axis=1, keepdims=True) * (1.0 / 256.0) + NORM_EPS)
        dnw_ref[...] += jnp.sum(dya * yg * rstd, axis=0, keepdims=True)
        dn = dya * nw_ref[...]
        dyg = rstd * dn - yg * (rstd * rstd * rstd * (1.0 / 256.0)) * jnp.sum(dn * yg, axis=1, keepdims=True)
        dz_ref[...] = (dyg * ypre * (sz * (1.0 + z * (1.0 - sz)))).astype(BF16)
        dy_all = dyg * silu

        dskv = dsk_ref[...]
        da_cols = jnp.zeros((CHUNK, LANES), F32)
        dxt_cols = jnp.zeros((CHUNK, LANES), F32)
        ddsk_row = jnp.zeros((1, LANES), F32)
        dcb = jnp.zeros((CHUNK, CHUNK), F32)
        dc = jnp.zeros((CHUNK, SSM_STATE), F32)
        db = jnp.zeros((CHUNK, SSM_STATE), F32)
        last = _iota_col() == CHUNK - 1
        for pp in range(2):
            r0, r1 = 2 * pp, 2 * pp + 1
            x = xs_ref[:, LANES * pp:LANES * (pp + 1)]
            dy = dy_all[:, LANES * pp:LANES * (pp + 1)]
            lo, dtp, xd, d0, d1, al0, al1, eac, dsp, eal = _ssd_pair(pp, x, dtv, acum, acum_t, causal, lane, row)
            w0, w1 = cbm * d0, cbm * d1
            w0b, w1b = w0.astype(BF16), w1.astype(BF16)
            xdb = xd.astype(BF16)
            dyb = dy.astype(BF16)
            h = hst_ref[pp]
            dhn = dh_scr[pp]
            hb = h.astype(BF16)
            dhb = dhn.astype(BF16)
            g0 = _dot_nt(jnp.where(lo, dy, 0.0).astype(BF16), xdb)
            g1 = _dot_nt(jnp.where(lo, 0.0, dy).astype(BF16), xdb)
            dcb = dcb + g0 * d0 + g1 * d1
            m0, m1 = g0 * w0, g1 * w1
            bdh = _dot_nt(bb, dhb)
            dxd = jnp.where(lo, _dot_tn(w0b, dyb), _dot_tn(w1b, dyb)) + dsp * bdh
            ch = _dot_nt(cb, hb)
            edy = eac * dy
            edyb = edy.astype(BF16)
            xds = xd * dsp
            dc = dc + _dot(edyb, hb)
            db = db + _dot(xds.astype(BF16), dhb)
            dh_scr[pp] = dhn * eal + _dot_tn(edyb, cb)
            t2 = edy * ch
            t3 = xds * bdh
            dhh = dhn * h
            s4_0 = jnp.sum(jnp.sum(jnp.where(row < HEAD_DIM, dhh, 0.0), axis=0, keepdims=True), axis=1, keepdims=True)
            s4_1 = jnp.sum(jnp.sum(dhh, axis=0, keepdims=True), axis=1, keepdims=True) - s4_0
            t23 = t2 - t3
            t23_0 = jnp.sum(jnp.where(lo, t23, 0.0), axis=1, keepdims=True)
            t23_1 = jnp.sum(t23, axis=1, keepdims=True) - t23_0
            c3 = jnp.sum(t3, axis=0, keepdims=True)
            c3_0 = jnp.sum(jnp.where(_iota_row() < HEAD_DIM, c3, 0.0), axis=1, keepdims=True)
            c3_1 = jnp.sum(c3, axis=1, keepdims=True) - c3_0
            dal0 = c3_0 + jnp.exp(al0) * s4_0
            dal1 = c3_1 + jnp.exp(al1) * s4_1
            dac0 = jnp.sum(m0 - m0.T, axis=1, keepdims=True) + t23_0 + jnp.where(last, dal0, 0.0)
            dac1 = jnp.sum(m1 - m1.T, axis=1, keepdims=True) + t23_1 + jnp.where(last, dal1, 0.0)
            da_cols = da_cols + jnp.where(lane == r0, dac0, 0.0) + jnp.where(lane == r1, dac1, 0.0)
            xx = dxd * x
            x0 = jnp.sum(jnp.where(lo, xx, 0.0), axis=1, keepdims=True)
            x1 = jnp.sum(xx, axis=1, keepdims=True) - x0
            dxt_cols = dxt_cols + jnp.where(lane == r0, x0, 0.0) + jnp.where(lane == r1, x1, 0.0)
            dskp = jnp.where((_iota_row() < HEAD_DIM), _col(dskv, r0), _col(dskv, r1))
            dxs_ref[:, LANES * pp:LANES * (pp + 1)] = dxd * dtp + dy * dskp
            yx = jnp.sum(dy * x, axis=0, keepdims=True)
            k0 = jnp.sum(jnp.where((_iota_row() < HEAD_DIM), yx, 0.0), axis=1, keepdims=True)
            k1 = jnp.sum(yx, axis=1, keepdims=True) - k0
            ddsk_row = ddsk_row + jnp.where(lane1 == r0, k0, 0.0) + jnp.where(lane1 == r1, k1, 0.0)
        dcbb = dcb.astype(BF16)
        dcm_ref[...] = dc + _dot(dcbb, bb)
        dbm_ref[...] = db + _dot_tn(dcbb, cb)
        tri_t = (row <= lane).astype(F32)
        dadt = _dot(tri_t, da_cols, precision=HIGHEST)
        ddtv = dadt * a_row + dxt_cols
        dalog_ref[...] += jnp.sum(dadt * dtv, axis=0, keepdims=True) * a_row
        ddt_raw = ddtv * _sigmoid(dt_ref[...] + dtb_ref[...])
        ddt_ref[...] = ddt_raw.astype(BF16)
        ddtb_ref[...] += jnp.sum(ddt_raw, axis=0, keepdims=True)
        ddsk_ref[...] += ddsk_row

    g = N_GROUPS
    rc = lambda c: nc - 1 - c
    par = pl.BlockSpec((g, 1, LANES), lambda i, c: (0, 0, 0))
    parw = pl.BlockSpec((g, 1, 256), lambda i, c: (0, 0, 0))
    wide = pl.BlockSpec((None, CHUNK, D_MODEL), lambda i, c: (i, rc(c), 0))
    blk512 = lambda col: pl.BlockSpec((None, CHUNK, 512), lambda i, c: (i, rc(c), col))
    return pl.pallas_call(
        body, name=name, grid=(b, nc),
        in_specs=[wide, blk512(2), blk512(3), blk512(dt0),
                  pl.BlockSpec((None, CHUNK, D_MODEL), lambda i, c: (i, rc(c), z0)),
                  par, par, par, parw,
                  wide,
                  pl.BlockSpec((None, None, g, 2, CHUNK, SSM_STATE), lambda i, c: (i, rc(c), 0, 0, 0, 0)),
                  wide],
        out_specs=[pl.BlockSpec((None, CHUNK, CONV_DIM), lambda i, c: (i, rc(c), 0)), wide, blk512(0),
                   par, par, par, parw],
        out_shape=[jax.ShapeDtypeStruct((b, s, CONV_DIM), F32), jax.ShapeDtypeStruct((b, s, D_MODEL), BF16),
                   jax.ShapeDtypeStruct((b, s, 512), BF16),
                   jax.ShapeDtypeStruct((g, 1, LANES), F32), jax.ShapeDtypeStruct((g, 1, LANES), F32),
                   jax.ShapeDtypeStruct((g, 1, LANES), F32), jax.ShapeDtypeStruct((g, 1, 256), F32)],
        scratch_shapes=[pltpu.VMEM((g, 2, CHUNK, SSM_STATE), F32)],
        compiler_params=_cp(("arbitrary", "arbitrary")),
    )(xact3, xact3, xact3, gates3, proj3, dtb, alog, dsk, nw, ypre3, hst, dya3)


_FGATE_ROWS = 512


def _fgate_fwd(gates3, fb, *, name):
    b, s, _ = gates3.shape
    rows = min(_FGATE_ROWS, s)
    f0 = _PAD_COLS["a_dt"][1] // LANES

    def body(f_ref, fb_ref, cum_ref, carry):
        @pl.when(pl.program_id(1) == 0)
        def _():
            carry[...] = jnp.zeros_like(carry)

        row = lax.broadcasted_iota(jnp.int32, (rows, rows), 0)
        lane = lax.broadcasted_iota(jnp.int32, (rows, rows), 1)
        tri = (row >= lane).astype(F32)
        lf = -_softplus(-(f_ref[...] + fb_ref[...]))
        cs = _dot(tri, lf, precision=HIGHEST) + carry[0:1, :]
        cum_ref[...] = cs
        carry[0:1, :] = _row(cs, rows - 1)

    return pl.pallas_call(
        body, name=name, grid=(b, s // rows),
        in_specs=[pl.BlockSpec((None, rows, LANES), lambda i, c: (i, c, f0)),
                  pl.BlockSpec((1, LANES), lambda i, c: (0, 0))],
        out_specs=pl.BlockSpec((None, rows, LANES), lambda i, c: (i, c, 0)),
        out_shape=jax.ShapeDtypeStruct((b, s, LANES), F32),
        scratch_shapes=[pltpu.VMEM((8, LANES), F32)],
        compiler_params=_cp(("parallel", "arbitrary")),
    )(gates3, fb)


def _fgate_bwd(gates3, fb, dcum, *, name):
    b, s, _ = gates3.shape
    rows = min(_FGATE_ROWS, s)
    nc = s // rows
    f0 = _PAD_COLS["a_dt"][1] // LANES
    npair = dcum.shape[1]

    def body(f_ref, fb_ref, dc_ref, df_ref, dfb_ref, carry):
        first = jnp.logical_and(pl.program_id(0) == 0, pl.program_id(1) == 0)

        @pl.when(first)
        def _():
            dfb_ref[...] = jnp.zeros_like(dfb_ref)

        @pl.when(pl.program_id(1) == 0)
        def _():
            carry[...] = jnp.zeros_like(carry)

        row = lax.broadcasted_iota(jnp.int32, (rows, rows), 0)
        lane = lax.broadcasted_iota(jnp.int32, (rows, rows), 1)
        tri_t = (row <= lane).astype(F32)
        dc = -jnp.sum(dc_ref[...], axis=0)
        dlf = _dot(tri_t, dc, precision=HIGHEST) + carry[0:1, :]
        carry[0:1, :] = _row(dlf, 0)
        df = dlf * _sigmoid(-(f_ref[...] + fb_ref[...]))
        df_ref[...] = df.astype(BF16)
        dfb_ref[...] += jnp.sum(df, axis=0, keepdims=True)

    return pl.pallas_call(
        body, name=name, grid=(b, nc),
        in_specs=[pl.BlockSpec((None, rows, LANES), lambda i, c: (i, nc - 1 - c, f0)),
                  pl.BlockSpec((1, LANES), lambda i, c: (0, 0)),
                  pl.BlockSpec((None, npair, rows, LANES), lambda i, c: (i, 0, nc - 1 - c, 0))],
        out_specs=[pl.BlockSpec((None, rows, LANES), lambda i, c: (i, nc - 1 - c, 0)),
                   pl.BlockSpec((1, LANES), lambda i, c: (0, 0))],
        out_shape=[jax.ShapeDtypeStruct((b, s, LANES), BF16), jax.ShapeDtypeStruct((1, LANES), F32)],
        scratch_shapes=[pltpu.VMEM((8, LANES), F32)],
        compiler_params=_cp(("arbitrary", "arbitrary")),
    )(gates3, fb, dcum)


_SCALE = HEAD_DIM ** -0.5
_NEG = -1e30


_ST_LSE, _ST_DELTA, _ST_MJ = 0, 2, 8


_SR = 40


def _ck_rep(cum):
    b, s, _ = cum.shape
    t = jnp.transpose(cum[:, :, :N_HEADS], (0, 2, 1)).reshape(b, N_HEADS // 2, 2, s, 1)
    return jnp.broadcast_to(t, (b, N_HEADS // 2, 2, s, LANES))


def _foxt_fwd(proj3, ckrep, *, name, tb):
    b, s, _ = proj3.shape
    nq = s // tb
    assert _ST_MJ + 2 * nq <= _SR
    q0 = _PAD_COLS["c_q"][0] // LANES
    k0 = _PAD_COLS["c_k"][0] // LANES
    v0 = _PAD_COLS["c_v"][0] // LANES
    z0 = _PAD_COLS["c_z"][0] // LANES
    rep = tb // LANES

    def body(q_ref, k_ref, v_ref, z_ref, ck_ref, y_ref, o_ref, st_ref):
        i = pl.program_id(2)
        lane = lax.broadcasted_iota(jnp.int32, (tb, LANES), 1)
        lo = lane < HEAD_DIM
        lo_r = lax.broadcasted_iota(jnp.int32, (LANES, tb), 0) < HEAD_DIM
        srow = lax.broadcasted_iota(jnp.int32, (_SR, tb), 0)
        q = q_ref[...].astype(F32) * _SCALE
        qms = (jnp.where(lo, q, 0.0).astype(BF16), jnp.where(lo, 0.0, q).astype(BF16))
        ones_at = (HEAD_DIM, 0)

        def block(j, carry, diagonal):
            ks = pl.ds(pl.multiple_of(j * tb, tb), tb)
            kb = k_ref[ks, :].astype(BF16)
            v = v_ref[ks, :].astype(F32)
            vts = (jnp.where(lo, v, jnp.where(lane == ones_at[0], 1.0, 0.0)).T.astype(BF16),
                   jnp.where(lo, jnp.where(lane == ones_at[1], 1.0, 0.0), v).T.astype(BF16))
            if diagonal:
                key = lax.broadcasted_iota(jnp.int32, (tb, tb), 0)
                qry = lax.broadcasted_iota(jnp.int32, (tb, tb), 1)
                mask = key <= qry
            ms, ls, acc, st = carry
            new_m, new_l, pvs, alphas = [], [], [], []
            for hh in range(2):
                sc = _dot_nt(kb, qms[hh]) - jnp.tile(ck_ref[hh, ks, :], (1, rep))
                if diagonal:
                    sc = jnp.where(mask, sc, _NEG)
                m_new = jnp.maximum(ms[hh], jnp.max(sc, axis=0, keepdims=True))
                alpha = jnp.exp(ms[hh] - m_new)
                pv = _dot(vts[hh], jnp.exp(sc - m_new).astype(BF16))
                rs = _row(pv[ones_at[hh]:ones_at[hh] + 8, :], 0)
                new_l.append(alpha * ls[hh] + rs)
                new_m.append(m_new)
                pvs.append(pv)
                alphas.append(alpha)
                st = jnp.where(srow == _ST_MJ + 2 * j + hh, m_new, st)
            acc = jnp.where(lo_r, alphas[0] * acc + pvs[0], alphas[1] * acc + pvs[1])
            return (tuple(new_m), tuple(new_l), acc, st)

        neg = jnp.full((1, tb), _NEG, F32)
        zero = jnp.zeros((1, tb), F32)
        init = ((neg, neg), (zero, zero), jnp.zeros((LANES, tb), F32), jnp.zeros((_SR, tb), F32))
        carry = lax.fori_loop(0, i, lambda j, c: block(j, c, False), init)
        ms, ls, acc, st = block(i, carry, True)
        o = (acc / jnp.where(lo_r, ls[0], ls[1])).T
        o_ref[...] = o
        st = jnp.where(srow == _ST_LSE, ms[0] + jnp.log(ls[0]), st)
        st_ref[...] = jnp.where(srow == _ST_LSE + 1, ms[1] + jnp.log(ls[1]), st)
        z = z_ref[...].astype(F32)
        y_ref[...] = (o * (z * _sigmoid(z))).astype(BF16)

    qspec = lambda c0: pl.BlockSpec((None, tb, LANES), lambda bi, p, i: (bi, i, c0 + p))
    kspec = lambda c0: pl.BlockSpec((None, s, LANES), lambda bi, p, i: (bi, 0, c0 + p))
    ospec = pl.BlockSpec((None, tb, LANES), lambda bi, p, i: (bi, i, p))
    return pl.pallas_call(
        body, name=name, grid=(b, N_HEADS // 2, nq),
        in_specs=[qspec(q0), kspec(k0), kspec(v0), qspec(z0),
                  pl.BlockSpec((None, None, 2, s, LANES), lambda bi, p, i: (bi, p, 0, 0, 0))],
        out_specs=[ospec, ospec, pl.BlockSpec((None, None, None, _SR, tb), lambda bi, p, i: (bi, p, i, 0, 0))],
        out_shape=[jax.ShapeDtypeStruct((b, s, D_MODEL), BF16), jax.ShapeDtypeStruct((b, s, D_MODEL), F32),
                   jax.ShapeDtypeStruct((b, N_HEADS // 2, nq, _SR, tb), F32)],
        compiler_params=_cp(("parallel", "parallel", "arbitrary")),
    )(proj3, proj3, proj3, proj3, ckrep)


def _foxt_prep(proj3, o3, stat, dy3, *, name, tb):
    b, s, _ = proj3.shape
    nq = s // tb
    z0 = _PAD_COLS["c_z"][0] // 256

    def body(z_ref, o_ref, fst_ref, dy_ref, dz_ref, do_ref, st_ref):
        z = z_ref[...].astype(F32)
        sz = _sigmoid(z)
        dy = dy_ref[...]
        o = o_ref[...]
        do = dy * (z * sz)
        dz_ref[...] = (dy * o * (sz * (1.0 + z * (1.0 - sz)))).astype(BF16)
        do_ref[...] = do
        doo = do.astype(BF16).astype(F32) * o
        r8 = lax.broadcasted_iota(jnp.int32, (8, LANES), 0)
        l8 = lax.broadcasted_iota(jnp.int32, (8, LANES), 1)
        pick = jnp.logical_or(jnp.logical_and(r8 == 0, l8 < HEAD_DIM),
                              jnp.logical_and(r8 == 1, l8 >= HEAD_DIM)).astype(F32)
        srow = lax.broadcasted_iota(jnp.int32, (_SR, tb), 0)
        for pp in range(2):
            d8 = _dot(pick, doo[:, LANES * pp:LANES * (pp + 1)], ((1,), (1,)), precision=HIGHEST)
            st = jnp.where(srow == _ST_DELTA, _row(d8, 0), fst_ref[pp])
            st_ref[pp] = jnp.where(srow == _ST_DELTA + 1, _row(d8, 1), st)

    ospec = pl.BlockSpec((None, tb, 256), lambda bi, p, i: (bi, i, p))
    sspec = pl.BlockSpec((None, 2, None, _SR, tb), lambda bi, p, i: (bi, p, i, 0, 0))
    return pl.pallas_call(
        body, name=name, grid=(b, N_HEADS // 4, nq),
        in_specs=[pl.BlockSpec((None, tb, 256), lambda bi, p, i: (bi, i, z0 + p)), ospec, sspec, ospec],
        out_specs=[ospec, ospec, sspec],
        out_shape=[jax.ShapeDtypeStruct((b, s, D_MODEL), BF16), jax.ShapeDtypeStruct((b, s, D_MODEL), F32),
                   jax.ShapeDtypeStruct((b, N_HEADS // 2, nq, _SR, tb), F32)],
        compiler_params=_cp(("parallel", "parallel", "parallel")),
    )(proj3, o3, stat, dy3)


def _foxt_bwd(proj3, ckrep, do3, stats, *, name, tb):
    b, s, _ = proj3.shape
    nq = s // tb
    q0 = _PAD_COLS["c_q"][0] // LANES
    k0 = _PAD_COLS["c_k"][0] // LANES
    v0 = _PAD_COLS["c_v"][0] // LANES
    rep = tb // LANES

    def body(q_ref, do_ref, st_ref, k_ref, v_ref, ck_ref, dq_ref, dk_ref, dv_ref, cs_ref):
        j = pl.program_id(2)
        lane = lax.broadcasted_iota(jnp.int32, (tb, LANES), 1)
        lo = lane < HEAD_DIM
        lo_r = lax.broadcasted_iota(jnp.int32, (LANES, tb), 0) < HEAD_DIM

        @pl.when(j == 0)
        def _():
            dq_ref[...] = jnp.zeros_like(dq_ref)

        kf = k_ref[...].astype(F32)
        kb = kf.astype(BF16)
        kt = kf.T.astype(BF16)
        vb = v_ref[...].astype(BF16)
        cks = (jnp.tile(ck_ref[0], (1, rep)), jnp.tile(ck_ref[1], (1, rep)))

        hq = tb // 2

        def tile(i, half, carry, diagonal):
            nk = hq if (diagonal and half == 0) else tb
            qs = pl.ds(pl.multiple_of(i * tb + half * hq, hq), hq)
            q = q_ref[qs, :].astype(F32) * _SCALE
            do = do_ref[qs, :]
            st = st_ref[i][:, half * hq:(half + 1) * hq]
            kbs, vbs, kts = kb[:nk], vb[:nk], kt[:, :nk]
            if diagonal:
                key = lax.broadcasted_iota(jnp.int32, (nk, hq), 0)
                qry = lax.broadcasted_iota(jnp.int32, (nk, hq), 1) + half * hq
                mask = key <= qry
            dk, dv, cs = carry
            dkc = jnp.zeros((nk, LANES), F32)
            dvc = jnp.zeros((nk, LANES), F32)
            csc, dqs = [], []
            for hh in range(2):
                sel = lo[:hq] if hh == 0 else jnp.logical_not(lo[:hq])
                qm = jnp.where(sel, q, 0.0).astype(BF16)
                dom = jnp.where(sel, do, 0.0).astype(BF16)
                sc = _dot_nt(kbs, qm) - cks[hh][:nk, :hq]
                if diagonal:
                    sc = jnp.where(mask, sc, _NEG)
                mj = _row(st, _ST_MJ + 2 * j + hh)
                w = jnp.exp(mj - _row(st, _ST_LSE + hh))
                ph = jnp.exp(sc - mj).astype(BF16).astype(F32) * w
                ds = ph * (_dot_nt(vbs, dom) - _row(st, _ST_DELTA + hh))
                dsb = ds.astype(BF16)
                dvc = dvc + _dot(ph.astype(BF16), dom)
                dkc = dkc + _dot(dsb, qm)
                csc.append(jnp.sum(ds, axis=1, keepdims=True))
                dqs.append(_dot(kts, dsb))
            dq_ref[i, :, half * hq:(half + 1) * hq] += jnp.where(lo_r[:, :hq], dqs[0], dqs[1]) * _SCALE
            if nk < tb:
                grow = lambda a: jnp.concatenate([a, jnp.zeros((tb - nk,) + a.shape[1:], F32)], axis=0)
                dkc, dvc, csc = grow(dkc), grow(dvc), [grow(c) for c in csc]
            return (dk + dkc, dv + dvc, (cs[0] + csc[0], cs[1] + csc[1]))

        def block(i, carry, diagonal):
            return tile(i, 1, tile(i, 0, carry, diagonal), diagonal)

        zcol = jnp.zeros((tb, 1), F32)
        init = (jnp.zeros((tb, LANES), F32), jnp.zeros((tb, LANES), F32), (zcol, zcol))
        carry = block(j, init, True)
        dk, dv, cs = lax.fori_loop(j + 1, nq, lambda i, c: block(i, c, False), carry)
        dk_ref[...] = dk.astype(BF16)
        dv_ref[...] = dv.astype(BF16)
        p2 = 2 * pl.program_id(1)
        cs_ref[...] = jnp.where(lane == p2, cs[0], jnp.where(lane == p2 + 1, cs[1], 0.0))

    full = lambda c0: pl.BlockSpec((None, s, LANES), lambda bi, p, j: (bi, 0, c0 + p))
    kspec = lambda c0: pl.BlockSpec((None, tb, LANES), lambda bi, p, j: (bi, j, c0 + p))
    ko = pl.BlockSpec((None, tb, LANES), lambda bi, p, j: (bi, j, p))
    sall = pl.BlockSpec((None, None, nq, _SR, tb), lambda bi, p, j: (bi, p, 0, 0, 0))
    dqspec = pl.BlockSpec((None, None, nq, LANES, tb), lambda bi, p, j: (bi, p, 0, 0, 0))
    return pl.pallas_call(
        body, name=name, grid=(b, N_HEADS // 2, nq),
        in_specs=[full(q0), full(0), sall, kspec(k0), kspec(v0),
                  pl.BlockSpec((None, None, 2, tb, LANES), lambda bi, p, j: (bi, p, 0, j, 0))],
        out_specs=[dqspec, ko, ko, pl.BlockSpec((None, None, tb, LANES), lambda bi, p, j: (bi, p, j, 0))],
        out_shape=[jax.ShapeDtypeStruct((b, N_HEADS // 2, nq, LANES, tb), F32),
                   jax.ShapeDtypeStruct((b, s, D_MODEL), BF16), jax.ShapeDtypeStruct((b, s, D_MODEL), BF16),
                   jax.ShapeDtypeStruct((b, N_HEADS // 2, s, LANES), F32)],
        compiler_params=_cp(("parallel", "parallel", "arbitrary")),
    )(proj3, do3, stats, proj3, proj3, ckrep)


def _rope(x, cos, sin_signed):
    w = x.shape[1]
    lane = lax.broadcasted_iota(jnp.int32, x.shape, 1)
    first = (lane % HEAD_DIM) < (HEAD_DIM // 2)
    rot = jnp.where(first, pltpu.roll(x, w - HEAD_DIM // 2, 1), pltpu.roll(x, HEAD_DIM // 2, 1))
    return x * cos + rot * sin_signed


_QB = 8
_QROWS = _QB * CHUNK


def _swa_keys(g, kc_ref, kp_ref, vc_ref, vp_ref, cq_ref, sq_ref, cp_ref, sp_ref):
    def both_halves(x):
        x = x.astype(F32)
        lane = lax.broadcasted_iota(jnp.int32, x.shape, 1)
        keep = (lane // HEAD_DIM) == (g % 2)
        return jnp.where(keep, x, pltpu.roll(x, HEAD_DIM, 1))

    cq, sq, cpv, spv = cq_ref[...], sq_ref[...], cp_ref[...], sp_ref[...]
    kc = _rope(both_halves(kc_ref[...]), cq, sq).astype(BF16)
    kp = _rope(both_halves(kp_ref[...]), cpv, spv).astype(BF16)
    return cq, sq, cpv, spv, kc, kp, both_halves(vc_ref[...]).astype(BF16), both_halves(vp_ref[...]).astype(BF16)


def _swa_stack(pairs, lo):
    return jnp.concatenate([jnp.where(lo, pairs[0], 0.0), jnp.where(lo, 0.0, pairs[0]),
                            jnp.where(lo, pairs[1], 0.0), jnp.where(lo, 0.0, pairs[1])], axis=0).astype(BF16)


def _swa_mask4(prev_valid):
    r = lax.broadcasted_iota(jnp.int32, (4 * CHUNK, 2 * CHUNK), 0) & (CHUNK - 1)
    c = lax.broadcasted_iota(jnp.int32, (4 * CHUNK, 2 * CHUNK), 1)
    own = jnp.logical_and(c >= CHUNK, c - CHUNK <= r)
    before = jnp.logical_and(c < CHUNK, c > r)
    if prev_valid is True:
        return jnp.logical_or(own, before)
    return jnp.logical_or(own, jnp.logical_and(before, prev_valid))


def _swa_sink4(skv):
    return jnp.concatenate([jnp.broadcast_to(_col(skv, j), (CHUNK, 1)) for j in range(4)], axis=0)


def _swa_specs(order):
    def spec(shape, fn):
        return pl.BlockSpec(shape, lambda *ids: fn(*order(*ids)))

    q0 = _PAD_COLS["b_q"][0] // 256
    z0 = _PAD_COLS["b_z"][0] // 256
    k0 = _PAD_COLS["b_k"][0] // LANES
    v0 = _PAD_COLS["b_v"][0] // LANES
    prev = lambda i: jnp.maximum(_QB * i - 1, 0)
    return dict(
        kc=spec((None, _QROWS, LANES), lambda bi, g, i: (bi, i, k0 + g // 2)),
        kp=spec((None, CHUNK, LANES), lambda bi, g, i: (bi, prev(i), k0 + g // 2)),
        vc=spec((None, _QROWS, LANES), lambda bi, g, i: (bi, i, v0 + g // 2)),
        vp=spec((None, CHUNK, LANES), lambda bi, g, i: (bi, prev(i), v0 + g // 2)),
        q=spec((None, _QROWS, 256), lambda bi, g, i: (bi, i, q0 + g)),
        z=spec((None, _QROWS, 256), lambda bi, g, i: (bi, i, z0 + g)),
        blk=spec((None, _QROWS, 256), lambda bi, g, i: (bi, i, g)),
        kcur=spec((None, _QROWS, LANES), lambda bi, g, i: (bi, i, g)),
        kstep=spec((None, CHUNK, LANES), lambda bi, g, i: (bi, i, g)),
        tcur=spec((_QROWS, LANES), lambda bi, g, i: (i, 0)),
        tprev=spec((CHUNK, LANES), lambda bi, g, i: (prev(i), 0)),
        sk=spec((None, 1, LANES), lambda bi, g, i: (g, 0, 0)))


def _swa_fwd(proj3, cos, sin, sinks, *, name):
    b, s, _ = proj3.shape

    def body(q_ref, z_ref, kc_ref, kp_ref, vc_ref, vp_ref, cq_ref, sq_ref, cp_ref, sp_ref, sk_ref,
             y_ref, o_ref, lse_ref):
        i = pl.program_id(2)
        cq_all, sq_all, _, _, kc_all, kp0, vc_all, vp0 = _swa_keys(
            pl.program_id(1), kc_ref, kp_ref, vc_ref, vp_ref, cq_ref, sq_ref, cp_ref, sp_ref)
        lo = lax.broadcasted_iota(jnp.int32, (CHUNK, LANES), 1) < HEAD_DIM
        sink4 = _swa_sink4(sk_ref[...])
        for u in range(_QB):
            rs = slice(CHUNK * u, CHUNK * (u + 1))
            ps = slice(CHUNK * (u - 1), CHUNK * u)
            cq, sq = cq_all[rs], sq_all[rs]
            kp, vp = (kp0, vp0) if u == 0 else (kc_all[ps], vc_all[ps])
            kk = jnp.concatenate([kp, kc_all[rs]], axis=0)
            vv = jnp.concatenate([vp, vc_all[rs]], axis=0)
            q4 = _swa_stack([_rope(q_ref[rs, LANES * pp:LANES * (pp + 1)].astype(F32), cq, sq) * _SCALE
                             for pp in range(2)], lo)
            sc = jnp.where(_swa_mask4(True if u > 0 else i > 0), _dot_nt(q4, kk), _NEG)
            m = jnp.maximum(jnp.max(sc, axis=1, keepdims=True), sink4)
            pr = jnp.exp(sc - m)
            l = jnp.sum(pr, axis=1, keepdims=True) + jnp.exp(sink4 - m)
            o4 = _dot(pr.astype(BF16), vv) / l
            lse4 = m + jnp.log(l)
            for pp in range(2):
                ls = slice(LANES * pp, LANES * (pp + 1))
                h0 = slice(2 * CHUNK * pp, 2 * CHUNK * pp + CHUNK)
                h1 = slice(2 * CHUNK * pp + CHUNK, 2 * CHUNK * (pp + 1))
                o = jnp.where(lo, o4[h0], o4[h1])
                z = z_ref[rs, ls].astype(F32)
                o_ref[rs, ls] = o
                lse_ref[rs, ls] = jnp.where(lo, lse4[h0], lse4[h1])
                y_ref[rs, ls] = (o * (z * _sigmoid(z))).astype(BF16)

    sp = _swa_specs(lambda bi, g, i: (bi, g, i))
    return pl.pallas_call(
        body, name=name, grid=(b, N_GROUPS, s // _QROWS),
        in_specs=[sp["q"], sp["z"], sp["kc"], sp["kp"], sp["vc"], sp["vp"],
                  sp["tcur"], sp["tcur"], sp["tprev"], sp["tprev"], sp["sk"]],
        out_specs=[sp["blk"], sp["blk"], sp["blk"]],
        out_shape=[jax.ShapeDtypeStruct((b, s, D_MODEL), BF16)] + [jax.ShapeDtypeStruct((b, s, D_MODEL), F32)] * 2,
        compiler_params=_cp(("parallel", "parallel", "parallel")),
    )(proj3, proj3, proj3, proj3, proj3, proj3, cos, sin, cos, sin, sinks)


def _swa_bwd(proj3, cos, sin, sinks, o3, lse3, dy3, *, name):
    b, s, _ = proj3.shape

    def body(q_ref, z_ref, kc_ref, kp_ref, vc_ref, vp_ref, cq_ref, sq_ref, cp_ref, sp_ref, sk_ref,
             o_ref, lse_ref, dy_ref, dq_ref, dz_ref, dkc_ref, dkp_ref, dvc_ref, dvp_ref, dsk_ref):
        i = pl.program_id(2)
        first = jnp.logical_and(pl.program_id(1) == 0, i == 0)

        @pl.when(first)
        def _():
            dsk_ref[...] = jnp.zeros_like(dsk_ref)

        cq_all, sq_all, cpv, spv, kc_all, kp0, vc_all, vp0 = _swa_keys(
            pl.program_id(0), kc_ref, kp_ref, vc_ref, vp_ref, cq_ref, sq_ref, cp_ref, sp_ref)
        lo = lax.broadcasted_iota(jnp.int32, (CHUNK, LANES), 1) < HEAD_DIM
        lane1 = lax.broadcasted_iota(jnp.int32, (1, LANES), 1)
        sink4 = _swa_sink4(sk_ref[...])
        zero = jnp.zeros((CHUNK, LANES), F32)
        dks = [zero] * (_QB + 1)
        dvs = [zero] * (_QB + 1)
        dsk_row = jnp.zeros((1, LANES), F32)
        for u in range(_QB):
            rs = slice(CHUNK * u, CHUNK * (u + 1))
            ps = slice(CHUNK * (u - 1), CHUNK * u)
            cq, sq = cq_all[rs], sq_all[rs]
            kp, vp = (kp0, vp0) if u == 0 else (kc_all[ps], vc_all[ps])
            kk = jnp.concatenate([kp, kc_all[rs]], axis=0)
            vv = jnp.concatenate([vp, vc_all[rs]], axis=0)
            q4 = _swa_stack([_rope(q_ref[rs, LANES * pp:LANES * (pp + 1)].astype(F32), cq, sq) * _SCALE
                             for pp in range(2)], lo)
            dos, lses = [], []
            for pp in range(2):
                ls = slice(LANES * pp, LANES * (pp + 1))
                z = z_ref[rs, ls].astype(F32)
                sz = _sigmoid(z)
                dy = dy_ref[rs, ls]
                dos.append(dy * (z * sz))
                dz_ref[rs, ls] = (dy * o_ref[rs, ls] * (sz * (1.0 + z * (1.0 - sz)))).astype(BF16)
                lse = lse_ref[rs, ls]
                lses += [_col(lse, 0), _col(lse, HEAD_DIM)]
            do4 = _swa_stack(dos, lo)
            lse4 = jnp.concatenate(lses, axis=0)
            pr = jnp.exp(jnp.where(_swa_mask4(True if u > 0 else i > 0), _dot_nt(q4, kk), _NEG) - lse4)
            dp = _dot_nt(do4, vv)
            dl = jnp.sum(pr * dp, axis=1, keepdims=True)
            ds = (pr * (dp - dl)).astype(BF16)
            dsink = -jnp.exp(sink4 - lse4) * dl
            for j in range(4):
                dsk_row = dsk_row + jnp.where(
                    lane1 == j, jnp.sum(dsink[CHUNK * j:CHUNK * (j + 1)], axis=0, keepdims=True), 0.0)
            dq4 = _dot(ds, kk)
            dkk = _dot_tn(ds, q4)
            dvv = _dot_tn(pr.astype(BF16), do4)
            dks[u], dks[u + 1] = dks[u] + dkk[:CHUNK], dks[u + 1] + dkk[CHUNK:]
            dvs[u], dvs[u + 1] = dvs[u] + dvv[:CHUNK], dvs[u + 1] + dvv[CHUNK:]
            for pp in range(2):
                h0 = slice(2 * CHUNK * pp, 2 * CHUNK * pp + CHUNK)
                h1 = slice(2 * CHUNK * pp + CHUNK, 2 * CHUNK * (pp + 1))
                dq_ref[rs, LANES * pp:LANES * (pp + 1)] = _rope(
                    jnp.where(lo, dq4[h0], dq4[h1]) * _SCALE, cq, -sq).astype(BF16)
        fold = lambda v: v + pltpu.roll(v, HEAD_DIM, 1)
        dkp_ref[...] = fold(_rope(dks[0], cpv, -spv))
        dvp_ref[...] = fold(dvs[0])
        for u in range(_QB):
            rs = slice(CHUNK * u, CHUNK * (u + 1))
            dkc_ref[rs, :] = fold(_rope(dks[u + 1], cq_all[rs], -sq_all[rs]))
            dvc_ref[rs, :] = fold(dvs[u + 1])
        dsk_ref[...] += dsk_row

    sp = _swa_specs(lambda g, bi, i: (bi, g, i))
    kv_shape = jax.ShapeDtypeStruct((b, s, 512), F32)
    kvp_shape = jax.ShapeDtypeStruct((b, s // _QB, 512), F32)
    return pl.pallas_call(
        body, name=name, grid=(N_GROUPS, b, s // _QROWS),
        in_specs=[sp["q"], sp["z"], sp["kc"], sp["kp"], sp["vc"], sp["vp"],
                  sp["tcur"], sp["tcur"], sp["tprev"], sp["tprev"], sp["sk"], sp["blk"], sp["blk"], sp["blk"]],
        out_specs=[sp["blk"], sp["blk"], sp["kcur"], sp["kstep"], sp["kcur"], sp["kstep"], sp["sk"]],
        out_shape=[jax.ShapeDtypeStruct((b, s, D_MODEL), BF16), jax.ShapeDtypeStruct((b, s, D_MODEL), BF16),
                   kv_shape, kvp_shape, kv_shape, kvp_shape, jax.ShapeDtypeStruct((N_GROUPS, 1, LANES), F32)],
        compiler_params=_cp(("arbitrary", "arbitrary", "arbitrary")),
    )(proj3, proj3, proj3, proj3, proj3, proj3, cos, sin, cos, sin, sinks, o3, lse3, dy3)


def _swa_fold(dkc, dkp, dvc, dvp, *, name):
    b, s, _ = dkc.shape
    ns = s // _QROWS

    def body(kc_ref, kp_ref, vc_ref, vp_ref, dk_ref, dv_ref):
        has_next = pl.program_id(1) < ns - 1
        lo = lax.broadcasted_iota(jnp.int32, (_QROWS, LANES), 1) < HEAD_DIM
        row = lax.broadcasted_iota(jnp.int32, (_QROWS, 512), 0)
        last_block = jnp.logical_and(row >= _QROWS - CHUNK, has_next)
        for cur, nxt, out in ((kc_ref, kp_ref, dk_ref), (vc_ref, vp_ref, dv_ref)):
            tot = cur[...] + jnp.where(last_block, jnp.tile(nxt[...], (_QB, 1)), 0.0)
            for j in range(2):
                out[:, LANES * j:LANES * (j + 1)] = jnp.where(
                    lo, tot[:, 256 * j:256 * j + LANES], tot[:, 256 * j + LANES:256 * (j + 1)]).astype(BF16)

    cur = pl.BlockSpec((None, _QROWS, 512), lambda bi, i: (bi, i, 0))
    nxt = pl.BlockSpec((None, CHUNK, 512), lambda bi, i: (bi, jnp.minimum(i + 1, ns - 1), 0))
    out = pl.BlockSpec((None, _QROWS, 256), lambda bi, i: (bi, i, 0))
    sh = jax.ShapeDtypeStruct((b, s, 256), BF16)
    return pl.pallas_call(
        body, name=name, grid=(b, ns), in_specs=[cur, nxt, cur, nxt], out_specs=[out, out], out_shape=[sh, sh],
        compiler_params=_cp(("parallel", "parallel")),
    )(dkc, dkp, dvc, dvp)


def _branch_fwd(ys, proj, gb, wp, wo, x, *, name, tm=256):
    t = proj.shape[0]
    g0 = _PAD_COLS["gates"][0] // D_MODEL

    def body(g_ref, a_ref, b_ref, c_ref, gb_ref, wp_ref, wo_ref, x_ref, ba_ref, bb_ref, bc_ref, m_ref, xn_ref):
        acc = None
        for i, (y, br) in enumerate(((a_ref, ba_ref), (b_ref, bb_ref), (c_ref, bc_ref))):
            bri = _dot(y[...], wp_ref[i])
            br[...] = bri
            gate = _sigmoid(g_ref[:, D_MODEL * i:D_MODEL * (i + 1)].astype(F32) + gb_ref[i:i + 1, :])
            acc = gate * bri if acc is None else acc + gate * bri
        mb = acc.astype(BF16)
        m_ref[...] = mb
        xn_ref[...] = x_ref[...] + _dot(mb, wo_ref[...])

    row = pl.BlockSpec((tm, D_MODEL), lambda i: (i, 0))
    rowf = jax.ShapeDtypeStruct((t, D_MODEL), F32)
    outs = pl.pallas_call(
        body, name=name, grid=(t // tm,),
        in_specs=[pl.BlockSpec((tm, 3 * D_MODEL), lambda i: (i, g0)), row, row, row,
                  pl.BlockSpec((3, D_MODEL), lambda i: (0, 0)),
                  pl.BlockSpec((3, D_MODEL, D_MODEL), lambda i: (0, 0, 0)),
                  pl.BlockSpec((D_MODEL, D_MODEL), lambda i: (0, 0)), row],
        out_specs=[row, row, row, row, row],
        out_shape=[rowf, rowf, rowf, jax.ShapeDtypeStruct((t, D_MODEL), BF16), rowf],
        compiler_params=_cp(("parallel",)),
    )(proj, ys[0], ys[1], ys[2], gb, wp, wo, x)
    return outs[:3], outs[3], outs[4]


def _branch_bwd(dx, proj, br, gb, wp, wo, *, name, tm=256):
    t = proj.shape[0]
    g0 = _PAD_COLS["gates"][0] // D_MODEL

    def body(g_ref, a_ref, b_ref, c_ref, gb_ref, wp_ref, wo_ref, dx_ref,
             da_ref, db_ref, dc_ref, dg_ref, dgb_ref, ya_ref, yb_ref, yc_ref):
        @pl.when(pl.program_id(0) == 0)
        def _():
            dgb_ref[...] = jnp.zeros_like(dgb_ref)

        dmv = _dot_nt(dx_ref[...].astype(BF16), wo_ref[...])
        for i, (r, dr, dy) in enumerate(((a_ref, da_ref, ya_ref), (b_ref, db_ref, yb_ref), (c_ref, dc_ref, yc_ref))):
            gate = _sigmoid(g_ref[:, D_MODEL * i:D_MODEL * (i + 1)].astype(F32) + gb_ref[i:i + 1, :])
            dbr = (dmv * gate).astype(BF16)
            dr[...] = dbr
            dg = dmv * r[...] * gate * (1.0 - gate)
            dg_ref[:, D_MODEL * i:D_MODEL * (i + 1)] = dg.astype(BF16)
            dgb_ref[i:i + 1, :] += jnp.sum(dg, axis=0, keepdims=True)
            dy[...] = _dot_nt(dbr, wp_ref[i])

    row = pl.BlockSpec((tm, D_MODEL), lambda i: (i, 0))
    rowb = jax.ShapeDtypeStruct((t, D_MODEL), BF16)
    rowf = jax.ShapeDtypeStruct((t, D_MODEL), F32)
    outs = pl.pallas_call(
        body, name=name, grid=(t // tm,),
        in_specs=[pl.BlockSpec((tm, 3 * D_MODEL), lambda i: (i, g0)), row, row, row,
                  pl.BlockSpec((3, D_MODEL), lambda i: (0, 0)),
                  pl.BlockSpec((3, D_MODEL, D_MODEL), lambda i: (0, 0, 0)),
                  pl.BlockSpec((D_MODEL, D_MODEL), lambda i: (0, 0)), row],
        out_specs=[row, row, row, pl.BlockSpec((tm, 3 * D_MODEL), lambda i: (i, 0)),
                   pl.BlockSpec((8, D_MODEL), lambda i: (0, 0)), row, row, row],
        out_shape=[rowb, rowb, rowb, jax.ShapeDtypeStruct((t, 3 * D_MODEL), BF16),
                   jax.ShapeDtypeStruct((8, D_MODEL), F32), rowf, rowf, rowf],
        compiler_params=_cp(("arbitrary",)),
    )(proj, br[0], br[1], br[2], gb, wp, wo, dx)
    return outs[:3], outs[3], outs[4], outs[5:]


def _rope_tables(s):
    pos = jnp.arange(s, dtype=F32)
    inv_freq = ROPE_THETA ** (-jnp.arange(0, HEAD_DIM, 2, dtype=F32) / HEAD_DIM)
    ang = pos[:, None] * inv_freq[None, :]
    cos, sin = jnp.cos(ang), jnp.sin(ang)
    return jnp.tile(cos, (1, 4)), jnp.tile(jnp.concatenate([-sin, sin], axis=1), (1, 2))


def _layer_params(wl):
    return dict(
        dtb=_group_lanes(wl["dt_bias"]), alog=_group_lanes(wl["a_log"]), dsk=_group_lanes(wl["d_skip"]),
        nw=wl["ssm_norm_w"].reshape(N_GROUPS, 1, 256), sinks=_group_lanes(wl["sinks"]),
        fb=jnp.pad(wl["f_bias"], (0, LANES - N_HEADS)).reshape(1, LANES))


def _layer_fwd(x, wl, tabs, bsz, li, tb):
    t = x.shape[0]
    s = t // bsz
    cos, sin = tabs
    lp = _layer_params(wl)
    n = lambda k: f"l{li}_{k}"
    h, h_t = _rms_fwd(x, wl["norm_w"], name=n("rms_fwd"))
    proj = _mm(h, wl["w_in"], tm=1024, tn=1536, tk=1024, out_dtype=BF16, name=n("mm_proj"))
    proj3 = proj.reshape(bsz, s, N_PAD)
    g0, gw = _PAD_COLS["a_dt"][0], _PAD_COLS["a_dt"][1] + _PAD_COLS["c_f"][1]
    gates3 = _mm(h, wl["w_in"][:, g0:g0 + gw], tm=1024, tn=gw, tk=1024, name=n("mm_gates")).reshape(bsz, s, gw)
    xact3 = _conv_fwd(proj3, wl["conv_w"], wl["conv_b"], name=n("conv_fwd"))
    ya3, ypre3, hst = _ssd_fwd(proj3, gates3, xact3, lp["dtb"], lp["alog"], lp["dsk"], lp["nw"], name=n("ssd_fwd"))
    yb3, ob3, lseb3 = _swa_fwd(proj3, cos, sin, lp["sinks"], name=n("swa_fwd"))
    cum = _fgate_fwd(gates3, lp["fb"], name=n("fgate_fwd"))
    cum_t = _ck_rep(cum)
    yc3, oc3, statc3 = _foxt_fwd(proj3, cum_t, name=n("fox_fwd"), tb=tb)
    ys = [v.reshape(t, D_MODEL) for v in (ya3, yb3, yc3)]
    br, merged, x_new = _branch_fwd(ys, proj, wl["gate_bias"], wl["w_proj"], wl["w_out"], x, name=n("branch_fwd"))
    saved = dict(x=x, h_t=h_t, proj=proj, gates3=gates3, xact3=xact3, ypre3=ypre3, hst=hst, ob3=ob3, lseb3=lseb3,
                 cum_t=cum_t, oc3=oc3, statc3=statc3, ys=ys, br=br, merged=merged, lp=lp)
    return x_new, saved


def _layer_bwd(dx, wl, sv, tabs, bsz, li, tb):
    t = dx.shape[0]
    s = t // bsz
    cos, sin = tabs
    lp = sv["lp"]
    n = lambda k: f"l{li}_{k}"
    proj = sv["proj"]
    proj3 = proj.reshape(bsz, s, N_PAD)
    g = {}
    g["w_out"] = _mm(sv["merged"], dx, ta=True, tm=1024, tn=1024, tk=512, name=n("mm_dwout"))
    dbr, dgates, dgb, dys = _branch_bwd(dx, proj, sv["br"], wl["gate_bias"], wl["w_proj"], wl["w_out"],
                                        name=n("branch_bwd"))
    g["gate_bias"] = dgb[:3]
    g["w_proj"] = jnp.stack([_mm(sv["ys"][i], dbr[i], ta=True, tm=1024, tn=1024, tk=512, name=n(f"mm_dwproj{i}"))
                             for i in range(3)])
    dy3 = [v.reshape(bsz, s, D_MODEL) for v in dys]

    (dact, daz, dadt, ddtb, dalog, ddsk, dnw) = _ssd_bwd(
        proj3, sv["gates3"], sv["xact3"], lp["dtb"], lp["alog"], lp["dsk"], lp["nw"], sv["ypre3"], sv["hst"], dy3[0],
        name=n("ssd_bwd"))
    g["dt_bias"], g["a_log"], g["d_skip"] = _ungroup_lanes(ddtb), _ungroup_lanes(dalog), _ungroup_lanes(ddsk)
    g["ssm_norm_w"] = dnw.reshape(D_MODEL)
    dxbc, dwb = _conv_bwd(proj3, wl["conv_w"], wl["conv_b"], dact, name=n("conv_bwd"))
    g["conv_w"], g["conv_b"] = dwb[:CONV_WIDTH], dwb[CONV_WIDTH]

    dbq, dbz, dkc, dkp, dvc, dvp, dsk = _swa_bwd(proj3, cos, sin, lp["sinks"], sv["ob3"],
                                                 sv["lseb3"], dy3[1], name=n("swa_bwd"))
    g["sinks"] = _ungroup_lanes(dsk)

    dbk, dbv = _swa_fold(dkc, dkp, dvc, dvp, name=n("swa_fold"))

    dcz, do3, stats = _foxt_prep(proj3, sv["oc3"], sv["statc3"], dy3[2], name=n("fox_prep"), tb=tb)
    dqt, dck, dcv, csum = _foxt_bwd(proj3, sv["cum_t"], do3, stats, name=n("fox_bwd"), tb=tb)
    dcq = jnp.transpose(dqt, (0, 2, 4, 1, 3)).reshape(bsz, s, D_MODEL)
    dcf, dfb = _fgate_bwd(sv["gates3"], lp["fb"], csum, name=n("fgate_bwd"))
    g["f_bias"] = dfb[0, :N_HEADS]

    parts = {"gates": dgates.reshape(bsz, s, 3 * D_MODEL), "xbc": dxbc, "a_z": daz, "b_q": dbq, "b_z": dbz,
             "c_q": dcq, "c_k": dck, "c_v": dcv, "c_z": dcz, "b_k": dbk, "b_v": dbv, "a_dt": dadt, "c_f": dcf}
    dproj = jnp.concatenate([parts[name].astype(BF16) for name, _ in _PAD_ORDER]
                            + [jnp.zeros((bsz, s, N_PAD - N_USED), BF16)], axis=2).reshape(t, N_PAD)
    dh = _mm(dproj, wl["w_in"], tb=True, tm=1024, tn=1024, tk=1536, name=n("mm_dh"))
    g["w_in"] = _unpad_w_in(_mm(sv["h_t"], dproj, tm=1024, tn=768, tk=2048, name=n("mm_dwin")))
    dx_in, dnorm = _rms_bwd(sv["x"], wl["norm_w"], dh, dx, name=n("rms_bwd"))
    g["norm_w"] = dnorm[0]
    return dx_in, g


def _local_step(x, target, wls, final_norm_w, tb=1024):
    bsz, s, d = x.shape
    t = bsz * s
    tabs = _rope_tables(s)
    xc = x.reshape(t, d)
    saved = []
    for li, wl in enumerate(wls):
        xc, sv = _layer_fwd(xc, wl, tabs, bsz, li, tb)
        saved.append(sv)
    loss, dx, dfw = _final_loss(xc, final_norm_w, target.reshape(t, d), name="final_loss")
    grads = [None] * len(wls)
    for li in reversed(range(len(wls))):
        dx, grads[li] = _layer_bwd(dx, wls[li], saved[li], tabs, bsz, li, tb)
    return loss[0, 0], dx.reshape(bsz, s, d), grads, dfw[0]


_HBM = pl.BlockSpec(memory_space=pltpu.HBM)


def _chip_peers(x, y):
    return [(1 - x, y), (x, 1 - y), (1 - x, 1 - y)]


def _gather_weights(arrs, *, name):
    n = len(arrs)

    def body(*refs):
        ins, outs = refs[:n], refs[n:2 * n]
        ici_send, ici_recv, d2d_send, d2d_recv = refs[2 * n:]
        x, y, c = lax.axis_index("x"), lax.axis_index("y"), lax.axis_index("c")
        me = 2 * x + y
        peers = _chip_peers(x, y)
        sib = (x, y, 1 - c)
        sends, fwds = [], []
        for a in range(n):
            for k, (px, py) in enumerate(peers):
                cp = pltpu.make_async_remote_copy(
                    src_ref=ins[a].at[c], dst_ref=outs[a].at[me, c], send_sem=ici_send.at[a, k],
                    recv_sem=ici_recv.at[a, k], device_id=(px, py, c), device_id_type=MESH)
                cp.start()
                sends.append(cp)
        for a in range(n):
            for k, (px, py) in enumerate(peers):
                slot = 2 * px + py
                pltpu.make_async_remote_copy(
                    src_ref=ins[a].at[c], dst_ref=outs[a].at[slot, c], send_sem=ici_send.at[a, k],
                    recv_sem=ici_recv.at[a, k], device_id=(px, py, c), device_id_type=MESH).wait_recv()
                fw = pltpu.make_async_remote_copy(
                    src_ref=outs[a].at[slot, c], dst_ref=outs[a].at[slot, c], send_sem=d2d_send.at[a, k],
                    recv_sem=d2d_recv.at[a, k], device_id=sib, device_id_type=MESH)
                fw.start()
                fwds.append(fw)
        for a in range(n):
            for k, (px, py) in enumerate(peers):
                slot = 2 * px + py
                pltpu.make_async_remote_copy(
                    src_ref=outs[a].at[slot, 1 - c], dst_ref=outs[a].at[slot, 1 - c], send_sem=d2d_send.at[a, k],
                    recv_sem=d2d_recv.at[a, k], device_id=sib, device_id_type=MESH).wait_recv()
        for cp in sends + fwds:
            cp.wait_send()

    out_shape = [jax.ShapeDtypeStruct((N_CHIPS,) + a.shape, a.dtype) for a in arrs]
    return pl.pallas_call(
        body, name=name, out_shape=out_shape, in_specs=[_HBM] * n, out_specs=[_HBM] * n,
        scratch_shapes=[pltpu.SemaphoreType.DMA((n, 3)), pltpu.SemaphoreType.DMA((n, 3)),
                        pltpu.SemaphoreType.DMA((n, 3)), pltpu.SemaphoreType.DMA((n, 3))],
    )(*arrs)


def _pair_exchange(arrs, *, name):
    n = len(arrs)

    def body(*refs):
        ins, outs = refs[:n], refs[n:2 * n]
        send, recv = refs[2 * n:]
        x, y, c = lax.axis_index("x"), lax.axis_index("y"), lax.axis_index("c")
        sib = (x, y, 1 - c)
        cps = []
        for a in range(n):
            for k in range(N_CHIPS):
                cp = pltpu.make_async_remote_copy(
                    src_ref=ins[a].at[k, 1 - c], dst_ref=outs[a].at[k], send_sem=send.at[a, k],
                    recv_sem=recv.at[a, k], device_id=sib, device_id_type=MESH)
                cp.start()
                cps.append(cp)
        for cp in cps:
            cp.wait()

    out_shape = [jax.ShapeDtypeStruct((N_CHIPS,) + a.shape[2:], a.dtype) for a in arrs]
    return pl.pallas_call(
        body, name=name, out_shape=out_shape, in_specs=[_HBM] * n, out_specs=[_HBM] * n,
        scratch_shapes=[pltpu.SemaphoreType.DMA((n, N_CHIPS)), pltpu.SemaphoreType.DMA((n, N_CHIPS))],
    )(*arrs)


def _chip_exchange(arrs, *, name):
    n = len(arrs)

    def body(*refs):
        ins, outs = refs[:n], refs[n:2 * n]
        send, recv = refs[2 * n:]
        x, y, c = lax.axis_index("x"), lax.axis_index("y"), lax.axis_index("c")
        me = 2 * x + y
        peers = _chip_peers(x, y)
        cps = []
        for a in range(n):
            for k, (px, py) in enumerate(peers):
                cp = pltpu.make_async_remote_copy(
                    src_ref=ins[a].at[2 * px + py], dst_ref=outs[a].at[me], send_sem=send.at[a, k],
                    recv_sem=recv.at[a, k], device_id=(px, py, c), device_id_type=MESH)
                cp.start()
                cps.append(cp)
        for a in range(n):
            for k, (px, py) in enumerate(peers):
                pltpu.make_async_remote_copy(
                    src_ref=ins[a].at[2 * px + py], dst_ref=outs[a].at[2 * px + py], send_sem=send.at[a, k],
                    recv_sem=recv.at[a, k], device_id=(px, py, c), device_id_type=MESH).wait_recv()
        for cp in cps:
            cp.wait_send()

    out_shape = [jax.ShapeDtypeStruct(a.shape, a.dtype) for a in arrs]
    return pl.pallas_call(
        body, name=name, out_shape=out_shape, in_specs=[_HBM] * n, out_specs=[_HBM] * n,
        scratch_shapes=[pltpu.SemaphoreType.DMA((n, 3)), pltpu.SemaphoreType.DMA((n, 3))],
    )(*arrs)


def _pair_share(arrs, *, name):
    n = len(arrs)

    def body(*refs):
        ins, outs = refs[:n], refs[n:2 * n]
        send, recv = refs[2 * n:]
        x, y, c = lax.axis_index("x"), lax.axis_index("y"), lax.axis_index("c")
        sib = (x, y, 1 - c)
        cps = []
        for a in range(n):
            cp = pltpu.make_async_remote_copy(
                src_ref=ins[a], dst_ref=outs[a], send_sem=send.at[a], recv_sem=recv.at[a],
                device_id=sib, device_id_type=MESH)
            cp.start()
            cps.append(cp)
        for cp in cps:
            cp.wait()

    out_shape = [jax.ShapeDtypeStruct(a.shape, a.dtype) for a in arrs]
    return pl.pallas_call(
        body, name=name, out_shape=out_shape, in_specs=[_HBM] * n, out_specs=[_HBM] * n,
        scratch_shapes=[pltpu.SemaphoreType.DMA((n,)), pltpu.SemaphoreType.DMA((n,))],
    )(*arrs)


def _allreduce_small(buf, *, name):
    r = buf.shape[0]

    def body(in_ref, out_ref, land, send, recv):
        x, y, c = lax.axis_index("x"), lax.axis_index("y"), lax.axis_index("c")
        me = 4 * x + 2 * y + c
        land[me] = in_ref[...]
        cps = []
        for k in range(1, N_DEV):
            px, py, pc = x ^ ((k >> 2) & 1), y ^ ((k >> 1) & 1), c ^ (k & 1)
            cp = pltpu.make_async_remote_copy(
                src_ref=in_ref, dst_ref=land.at[me], send_sem=send.at[k - 1], recv_sem=recv.at[k - 1],
                device_id=(px, py, pc), device_id_type=MESH)
            cp.start()
            cps.append(cp)
        for k in range(1, N_DEV):
            px, py, pc = x ^ ((k >> 2) & 1), y ^ ((k >> 1) & 1), c ^ (k & 1)
            pltpu.make_async_remote_copy(
                src_ref=in_ref, dst_ref=land.at[4 * px + 2 * py + pc], send_sem=send.at[k - 1],
                recv_sem=recv.at[k - 1], device_id=(px, py, pc), device_id_type=MESH).wait_recv()
        for cp in cps:
            cp.wait_send()
        acc = land[0]
        for k in range(1, N_DEV):
            acc = acc + land[k]
        out_ref[...] = acc

    vm = pl.BlockSpec(memory_space=pltpu.VMEM)
    return pl.pallas_call(
        body, name=name, out_shape=jax.ShapeDtypeStruct((r, LANES), F32), in_specs=[vm], out_specs=vm,
        scratch_shapes=[pltpu.VMEM((N_DEV, r, LANES), F32), pltpu.SemaphoreType.DMA((N_DEV - 1,)),
                        pltpu.SemaphoreType.DMA((N_DEV - 1,))],
    )(buf)


def _row_tile(rows, cols, n_arrays, budget=20 * 1024 * 1024):
    best = 8 if rows % 8 == 0 else rows
    tr = 8
    while tr <= rows:
        if rows % tr == 0 and tr * cols * 4 * n_arrays * 2 <= budget:
            best = tr
        tr *= 2
    return best


def _add_slot_layer(full, other, *, name):
    _, _, r, cdim = full.shape
    tr = _row_tile(r, cdim, 4)

    def body(c_ref, a_ref, b_ref, o_ref, ob_ref):
        sm = a_ref[...] + b_ref[...]
        o_ref[...] = sm
        ob_ref[...] = sm.astype(BF16)

    c = lax.axis_index("c").astype(jnp.int32).reshape(1)
    blk = pl.BlockSpec((None, tr, cdim), lambda k, i, c_ref: (k, i, 0))
    return pl.pallas_call(
        body, name=name,
        grid_spec=pltpu.PrefetchScalarGridSpec(
            num_scalar_prefetch=1, grid=(N_CHIPS, r // tr),
            in_specs=[pl.BlockSpec((None, None, tr, cdim), lambda k, i, c_ref: (k, c_ref[0], i, 0)), blk],
            out_specs=[blk, blk]),
        out_shape=[jax.ShapeDtypeStruct((N_CHIPS, r, cdim), F32), jax.ShapeDtypeStruct((N_CHIPS, r, cdim), BF16)],
        compiler_params=_cp(("parallel", "parallel")),
    )(c, full, other)


def _sum_slots(parts, pair, *, name):
    _, r, cdim = parts.shape
    tr = _row_tile(r, cdim, 5)

    def body(me_ref, p_ref, own_ref, o_ref):
        me = me_ref[0]
        acc = None
        for k in range(N_CHIPS):
            term = jnp.where(me == k, own_ref[...], p_ref[k].astype(F32))
            acc = term if acc is None else acc + term
        o_ref[...] = acc

    me = (2 * lax.axis_index("x") + lax.axis_index("y")).astype(jnp.int32).reshape(1)
    return pl.pallas_call(
        body, name=name,
        grid_spec=pltpu.PrefetchScalarGridSpec(
            num_scalar_prefetch=1, grid=(r // tr,),
            in_specs=[pl.BlockSpec((N_CHIPS, tr, cdim), lambda i, me_ref: (0, i, 0)),
                      pl.BlockSpec((None, tr, cdim), lambda i, me_ref: (me_ref[0], i, 0))],
            out_specs=pl.BlockSpec((tr, cdim), lambda i, me_ref: (i, 0))),
        out_shape=jax.ShapeDtypeStruct((r, cdim), F32),
        compiler_params=_cp(("parallel",)),
    )(me, parts, pair)


def _assemble_w_proj(own, gathered, li, *, name):
    _, nb, r, cdim = own.shape

    def body(chip_ref, own_ref, slot_ref, o_ref):
        o_ref[...] = jnp.where(chip_ref[0] == pl.program_id(1), own_ref[...], slot_ref[...])

    chip = (2 * lax.axis_index("x") + lax.axis_index("y")).astype(jnp.int32).reshape(1)
    return pl.pallas_call(
        body, name=name,
        grid_spec=pltpu.PrefetchScalarGridSpec(
            num_scalar_prefetch=1, grid=(nb, N_CHIPS),
            in_specs=[pl.BlockSpec((None, None, r, cdim), lambda i, k, chip_ref: (li, i, 0, 0)),
                      pl.BlockSpec((None, None, None, r, cdim), lambda i, k, chip_ref: (k, li, i, 0, 0))],
            out_specs=pl.BlockSpec((None, r, cdim), lambda i, k, chip_ref: (i, k, 0))),
        out_shape=jax.ShapeDtypeStruct((nb, N_CHIPS * r, cdim), own.dtype),
        compiler_params=_cp(("parallel", "parallel")),
    )(chip, own, gathered)


def _adamw(w, g, m, v, *, name):
    lead, (r, cdim) = w.shape[:-2], w.shape[-2:]
    nl = len(lead)
    tr = _row_tile(r, cdim, 7)
    tc = cdim
    if tr < 64 < r and cdim % LANES == 0:
        tr, tc = r, LANES
    c1 = 1.0 - ADAM_B1 ** ADAM_STEP
    c2 = 1.0 - ADAM_B2 ** ADAM_STEP

    def body(w_ref, g_ref, m_ref, v_ref, d_ref, nm_ref, nv_ref):
        gv = g_ref[...]
        mn = ADAM_B1 * m_ref[...] + (1.0 - ADAM_B1) * gv
        vn = ADAM_B2 * v_ref[...] + (1.0 - ADAM_B2) * (gv * gv)
        nm_ref[...] = mn
        nv_ref[...] = vn
        d_ref[...] = -ADAM_LR * ((mn / c1) / (jnp.sqrt(vn / c2) + ADAM_EPS) + ADAM_WD * w_ref[...])

    blk = pl.BlockSpec((None,) * nl + (tr, tc), lambda *ids: ids[:nl] + (ids[nl], ids[nl + 1]))
    sh = jax.ShapeDtypeStruct(w.shape, F32)
    return pl.pallas_call(
        body, name=name, grid=lead + (r // tr, cdim // tc), in_specs=[blk] * 4, out_specs=[blk] * 3,
        out_shape=[sh] * 3, compiler_params=_cp(("parallel",) * (nl + 2)),
    )(w, g, m, v)


_SMALL = ("norm_w", "conv_b", "dt_bias", "a_log", "d_skip", "ssm_norm_w", "sinks", "f_bias", "final_norm_w",
          "conv_w", "gate_bias")


def _pack(vals):
    flat = jnp.concatenate([v.reshape(-1) for v in vals])
    rows = -(-flat.shape[0] // LANES)
    rows = -(-rows // 8) * 8
    return jnp.pad(flat, (0, rows * LANES - flat.shape[0])).reshape(rows, LANES)


def _unpack(buf, shapes):
    flat = buf.reshape(-1)
    out, off = [], 0
    for sh in shapes:
        sz = int(np.prod(sh))
        out.append(flat[off:off + sz].reshape(sh))
        off += sz
    return out


def kernel(x, norm_w, w_in, conv_w, conv_b, dt_bias, a_log, d_skip, ssm_norm_w, sinks, f_bias, gate_bias, w_proj, w_out, final_norm_w, loss_target, m_norm_w, m_w_in, m_conv_w, m_conv_b, m_dt_bias, m_a_log, m_d_skip, m_ssm_norm_w, m_sinks, m_f_bias, m_gate_bias, m_w_proj, m_w_out, m_final_norm_w, v_norm_w, v_w_in, v_conv_w, v_conv_b, v_dt_bias, v_a_log, v_d_skip, v_ssm_norm_w, v_sinks, v_f_bias, v_gate_bias, v_w_proj, v_w_out, v_final_norm_w):
    depth = w_in.shape[0]
    chip = 2 * lax.axis_index("x") + lax.axis_index("y")

    own = [w_in.astype(BF16), w_proj.astype(BF16), w_out.astype(BF16), conv_w, gate_bias]
    gathered = _gather_weights(own, name="gather_weights")

    def whole(a, li, axis):
        return jnp.concatenate([jnp.where(chip == k, own[a][li], gathered[a][k, li]) for k in range(N_CHIPS)],
                               axis=axis)

    wls = []
    for li in range(depth):
        wls.append(dict(
            norm_w=norm_w[li], w_in=_pad_w_in(whole(0, li, 1)),
            conv_w=whole(3, li, 1), conv_b=conv_b[li], dt_bias=dt_bias[li], a_log=a_log[li], d_skip=d_skip[li],
            ssm_norm_w=ssm_norm_w[li], sinks=sinks[li], f_bias=f_bias[li], gate_bias=whole(4, li, 1),
            w_proj=_assemble_w_proj(own[1], gathered[1], li, name=f"l{li}_assemble_w_proj"),
            w_out=whole(2, li, 0)))

    loss_part, grad_x, grads, d_final = _local_step(x, loss_target, wls, final_norm_w)
    loss = lax.psum(loss_part, ("x", "y", "c"))

    c_in = w_in.shape[2]
    r_proj = w_proj.shape[2]
    r_out = w_out.shape[1]
    full_in = jnp.stack([jnp.stack([grads[li]["w_in"][:, k * c_in:(k + 1) * c_in] for li in range(depth)])
                         for k in range(N_CHIPS)])
    full_proj = jnp.stack([jnp.stack([grads[li]["w_proj"][:, k * r_proj:(k + 1) * r_proj].reshape(-1, D_MODEL)
                                      for li in range(depth)]) for k in range(N_CHIPS)])
    full_out = jnp.stack([jnp.stack([grads[li]["w_out"][k * r_out:(k + 1) * r_out] for li in range(depth)])
                          for k in range(N_CHIPS)])
    fulls = [full_in, full_proj, full_out]
    others = _pair_exchange(fulls, name="grad_pair_exchange")
    pair = [_add_slot_layer(f, o, name=f"grad_pair_add{i}") for i, (f, o) in enumerate(zip(fulls, others))]
    parts = _chip_exchange([p[1] for p in pair], name="grad_chip_exchange")
    mine = [_sum_slots(p, pr[0], name=f"grad_slot_sum{i}") for i, (p, pr) in enumerate(zip(parts, pair))]
    theirs = _pair_share(mine, name="grad_pair_share")
    core = lax.axis_index("c")
    red_in, red_proj, red_out = [jnp.stack([jnp.where(core == li, m, t) for li in range(depth)])
                                 for m, t in zip(mine, theirs)]
    grad_w_in = red_in
    grad_w_proj = red_proj.reshape(w_proj.shape)
    grad_w_out = red_out

    small_full = {
        "norm_w": jnp.stack([g["norm_w"] for g in grads]), "conv_b": jnp.stack([g["conv_b"] for g in grads]),
        "dt_bias": jnp.stack([g["dt_bias"] for g in grads]), "a_log": jnp.stack([g["a_log"] for g in grads]),
        "d_skip": jnp.stack([g["d_skip"] for g in grads]),
        "ssm_norm_w": jnp.stack([g["ssm_norm_w"] for g in grads]),
        "sinks": jnp.stack([g["sinks"] for g in grads]), "f_bias": jnp.stack([g["f_bias"] for g in grads]),
        "final_norm_w": d_final,
        "conv_w": jnp.stack([g["conv_w"] for g in grads]), "gate_bias": jnp.stack([g["gate_bias"] for g in grads])}
    shapes = [small_full[k].shape for k in _SMALL]
    summed = _unpack(_allreduce_small(_pack([small_full[k] for k in _SMALL]), name="allreduce_small"), shapes)
    gsmall = dict(zip(_SMALL, summed))
    gsmall["conv_w"] = lax.dynamic_slice_in_dim(gsmall["conv_w"], chip * conv_w.shape[2], conv_w.shape[2], axis=2)
    gsmall["gate_bias"] = lax.dynamic_slice_in_dim(gsmall["gate_bias"], chip * gate_bias.shape[2],
                                                   gate_bias.shape[2], axis=2)

    w_small = dict(norm_w=norm_w, conv_b=conv_b, dt_bias=dt_bias, a_log=a_log, d_skip=d_skip,
                   ssm_norm_w=ssm_norm_w, sinks=sinks, f_bias=f_bias, final_norm_w=final_norm_w, conv_w=conv_w,
                   gate_bias=gate_bias)
    m_small = dict(norm_w=m_norm_w, conv_b=m_conv_b, dt_bias=m_dt_bias, a_log=m_a_log, d_skip=m_d_skip,
                   ssm_norm_w=m_ssm_norm_w, sinks=m_sinks, f_bias=m_f_bias, final_norm_w=m_final_norm_w,
                   conv_w=m_conv_w, gate_bias=m_gate_bias)
    v_small = dict(norm_w=v_norm_w, conv_b=v_conv_b, dt_bias=v_dt_bias, a_log=v_a_log, d_skip=v_d_skip,
                   ssm_norm_w=v_ssm_norm_w, sinks=v_sinks, f_bias=v_f_bias, final_norm_w=v_final_norm_w,
                   conv_w=v_conv_w, gate_bias=v_gate_bias)
    sshapes = [w_small[k].shape for k in _SMALL]
    ds, ms, vs = _adamw(_pack([w_small[k] for k in _SMALL]), _pack([gsmall[k] for k in _SMALL]),
                        _pack([m_small[k] for k in _SMALL]), _pack([v_small[k] for k in _SMALL]), name="adamw_small")
    delta = dict(zip(_SMALL, _unpack(ds, sshapes)))
    new_m = dict(zip(_SMALL, _unpack(ms, sshapes)))
    new_v = dict(zip(_SMALL, _unpack(vs, sshapes)))
    grad = dict(gsmall)
    for nm, w, g, m, v in (("w_proj", w_proj, grad_w_proj, m_w_proj, v_w_proj),
                           ("w_out", w_out, grad_w_out, m_w_out, v_w_out)):
        grad[nm] = g
        delta[nm], new_m[nm], new_v[nm] = _adamw(w, g, m, v, name=f"adamw_{nm}")
    tview = lambda a: jnp.transpose(a, (0, 2, 1))
    grad["w_in"] = grad_w_in
    delta["w_in"], new_m["w_in"], new_v["w_in"] = [
        tview(a) for a in _adamw(tview(w_in), tview(grad_w_in), tview(m_w_in), tview(v_w_in), name="adamw_w_in")]

    order = ("norm_w", "w_in", "conv_w", "conv_b", "dt_bias", "a_log", "d_skip", "ssm_norm_w", "sinks", "f_bias",
             "gate_bias", "w_proj", "w_out", "final_norm_w")
    return (loss, grad_x, *[grad[k] for k in order], *[delta[k] for k in order],
            *[new_m[k] for k in order], *[new_v[k] for k in order])
```

```python
import numpy as np
import jax
import jax.numpy as jnp
from jax import lax
from jax.experimental import pallas as pl
from jax.experimental.pallas import tpu as pltpu

F32 = jnp.float32
BF16 = jnp.bfloat16
HIGHEST = lax.Precision.HIGHEST
MESH = pl.DeviceIdType.MESH

D_MODEL = 1024
HEAD_DIM = 64
N_HEADS = 16
N_GROUPS = 4
SSM_STATE = 128
CHUNK = 128
CONV_WIDTH = 4
CONV_DIM = 2048
ROPE_THETA = 10000.0
NORM_EPS = 1e-6
LANES = 128
N_CHIPS = 4
N_DEV = 8

ADAM_LR = 0.001
ADAM_B1 = 0.9
ADAM_B2 = 0.999
ADAM_EPS = 1e-08
ADAM_WD = 0.01
ADAM_STEP = 10

_REF_COLS = {}
_off = 0
for _n, _s in (("xbc", 2048), ("a_z", 1024), ("a_dt", 16), ("b_q", 1024), ("b_k", 256), ("b_v", 256),
               ("b_z", 1024), ("c_q", 1024), ("c_k", 1024), ("c_v", 1024), ("c_f", 16), ("c_z", 1024),
               ("gates", 3072)):
    _REF_COLS[_n] = (_off, _s)
    _off += _s

_PAD_ORDER = (("gates", 3072), ("xbc", 2048), ("a_z", 1024), ("b_q", 1024), ("b_z", 1024), ("c_q", 1024),
              ("c_k", 1024), ("c_v", 1024), ("c_z", 1024), ("b_k", 256), ("b_v", 256), ("a_dt", 512),
              ("c_f", 128))
_PAD_COLS = {}
_off = 0
for _n, _s in _PAD_ORDER:
    _PAD_COLS[_n] = (_off, _s)
    _off += _s
N_USED = _off
N_PAD = 13824


def _cp(sem, vmem_mb=48):
    return pltpu.CompilerParams(dimension_semantics=sem, vmem_limit_bytes=vmem_mb * 1024 * 1024)


def _dot(a, b, dims=((1,), (0,)), precision=None):
    return lax.dot_general(a, b, (dims, ((), ())), preferred_element_type=F32, precision=precision)


def _dot_nt(a, b):
    return _dot(a, b, ((1,), (1,)))


def _dot_tn(a, b):
    return _dot(a, b, ((0,), (0,)))


def _col(v, idx):
    lane = lax.broadcasted_iota(jnp.int32, v.shape, 1)
    return jnp.sum(jnp.where(lane == idx, v, 0.0), axis=1, keepdims=True)


def _row(v, idx):
    row = lax.broadcasted_iota(jnp.int32, v.shape, 0)
    return jnp.sum(jnp.where(row == idx, v, 0.0), axis=0, keepdims=True)


def _iota_col():
    return lax.broadcasted_iota(jnp.int32, (CHUNK, 1), 0)


def _iota_row():
    return lax.broadcasted_iota(jnp.int32, (1, LANES), 1)


def _sigmoid(x):
    return 1.0 / (1.0 + jnp.exp(-x))


def _softplus(x):
    return jnp.maximum(x, 0.0) + jnp.log(1.0 + jnp.exp(-jnp.abs(x)))


def _pad_w_in(w):
    parts = []
    for name, size in _PAD_ORDER:
        s0, sz = _REF_COLS[name]
        seg = w[:, s0:s0 + sz]
        if name == "a_dt":
            seg = jnp.pad(seg.reshape(-1, N_GROUPS, 4), ((0, 0), (0, 0), (0, LANES - 4))).reshape(-1, 512)
        elif name == "c_f":
            seg = jnp.pad(seg, ((0, 0), (0, LANES - 16)))
        parts.append(seg)
    parts.append(jnp.zeros((w.shape[0], N_PAD - N_USED), w.dtype))
    return jnp.concatenate(parts, axis=1)


def _unpad_w_in(wp):
    segs = {}
    for name, _ in _PAD_ORDER:
        p0, psz = _PAD_COLS[name]
        seg = wp[:, p0:p0 + psz]
        if name == "a_dt":
            seg = seg.reshape(-1, N_GROUPS, LANES)[:, :, :4].reshape(-1, 16)
        elif name == "c_f":
            seg = seg[:, :16]
        segs[name] = seg
    order = sorted(_REF_COLS, key=lambda n: _REF_COLS[n][0])
    return jnp.concatenate([segs[n] for n in order], axis=1)


def _group_lanes(v):
    return jnp.pad(v.reshape(N_GROUPS, 1, 4), ((0, 0), (0, 0), (0, LANES - 4)))


def _ungroup_lanes(v):
    return v[:, 0, :4].reshape(16)


def _mm(a, b, *, ta=False, tb=False, tm=512, tn=512, tk=512, out_dtype=F32, name):
    if ta:
        kdim, m = a.shape
    else:
        m, kdim = a.shape
    if tb:
        n, k2 = b.shape
    else:
        k2, n = b.shape
    assert kdim == k2, (a.shape, b.shape)
    tm, tn, tk = min(tm, m), min(tn, n), min(tk, kdim)
    assert m % tm == 0 and n % tn == 0 and kdim % tk == 0, (m, n, kdim, tm, tn, tk)
    nk = kdim // tk
    a_spec = (pl.BlockSpec((tk, tm), lambda i, j, k: (k, i)) if ta
              else pl.BlockSpec((tm, tk), lambda i, j, k: (i, k)))
    b_spec = (pl.BlockSpec((tn, tk), lambda i, j, k: (j, k)) if tb
              else pl.BlockSpec((tk, tn), lambda i, j, k: (k, j)))
    dims = ((0 if ta else 1,), (1 if tb else 0,))

    def body(a_ref, b_ref, o_ref, acc_ref):
        k = pl.program_id(2)
        p = _dot(a_ref[...].astype(BF16), b_ref[...].astype(BF16), dims)

        @pl.when(k == 0)
        def _():
            acc_ref[...] = p

        @pl.when(k > 0)
        def _():
            acc_ref[...] += p

        @pl.when(k == nk - 1)
        def _():
            o_ref[...] = acc_ref[...].astype(out_dtype)

    return pl.pallas_call(
        body, name=name, grid=(m // tm, n // tn, nk),
        in_specs=[a_spec, b_spec], out_specs=pl.BlockSpec((tm, tn), lambda i, j, k: (i, j)),
        out_shape=jax.ShapeDtypeStruct((m, n), out_dtype),
        scratch_shapes=[pltpu.VMEM((tm, tn), F32)],
        compiler_params=_cp(("parallel", "parallel", "arbitrary")),
    )(a, b)


def _rms_fwd(x, w, *, name, tm=512):
    t, d = x.shape

    def body(x_ref, w_ref, o_ref, ot_ref):
        xv = x_ref[...]
        r = lax.rsqrt(jnp.mean(xv * xv, axis=1, keepdims=True) + NORM_EPS)
        h = xv * r * w_ref[...]
        o_ref[...] = h.astype(BF16)
        ot_ref[...] = h.T.astype(BF16)

    return pl.pallas_call(
        body, name=name, grid=(t // tm,),
        in_specs=[pl.BlockSpec((tm, d), lambda i: (i, 0)), pl.BlockSpec((1, d), lambda i: (0, 0))],
        out_specs=[pl.BlockSpec((tm, d), lambda i: (i, 0)), pl.BlockSpec((d, tm), lambda i: (0, i))],
        out_shape=[jax.ShapeDtypeStruct((t, d), BF16), jax.ShapeDtypeStruct((d, t), BF16)],
        compiler_params=_cp(("parallel",)),
    )(x, w.reshape(1, d))


def _rms_bwd(x, w, dh, dres, *, name, tm=512):
    t, d = x.shape

    def body(x_ref, w_ref, dh_ref, dres_ref, dx_ref, dw_ref):
        xv = x_ref[...]
        r = lax.rsqrt(jnp.mean(xv * xv, axis=1, keepdims=True) + NORM_EPS)
        xhat = xv * r
        dhv = dh_ref[...]
        dxhat = dhv * w_ref[...]
        dx = r * (dxhat - xhat * jnp.mean(dxhat * xhat, axis=1, keepdims=True))
        dx_ref[...] = dres_ref[...] + dx

        @pl.when(pl.program_id(0) == 0)
        def _():
            dw_ref[...] = jnp.zeros_like(dw_ref)

        dw_ref[...] += jnp.sum(dhv * xhat, axis=0, keepdims=True)

    return pl.pallas_call(
        body, name=name, grid=(t // tm,),
        in_specs=[pl.BlockSpec((tm, d), lambda i: (i, 0)), pl.BlockSpec((1, d), lambda i: (0, 0)),
                  pl.BlockSpec((tm, d), lambda i: (i, 0)), pl.BlockSpec((tm, d), lambda i: (i, 0))],
        out_specs=[pl.BlockSpec((tm, d), lambda i: (i, 0)), pl.BlockSpec((1, d), lambda i: (0, 0))],
        out_shape=[jax.ShapeDtypeStruct((t, d), F32), jax.ShapeDtypeStruct((1, d), F32)],
        compiler_params=_cp(("arbitrary",)),
    )(x, w.reshape(1, d), dh, dres)


def _final_loss(x, w, target, *, name, tm=512):
    t, d = x.shape

    def body(x_ref, w_ref, t_ref, loss_ref, dx_ref, dw_ref):
        xv = x_ref[...]
        wv = w_ref[...]
        r = lax.rsqrt(jnp.mean(xv * xv, axis=1, keepdims=True) + NORM_EPS)
        xhat = xv * r
        err = xhat * wv - t_ref[...]
        dy = err * (1.0 / d)
        dxhat = dy * wv
        dx_ref[...] = r * (dxhat - xhat * jnp.mean(dxhat * xhat, axis=1, keepdims=True))

        @pl.when(pl.program_id(0) == 0)
        def _():
            dw_ref[...] = jnp.zeros_like(dw_ref)
            loss_ref[...] = jnp.zeros_like(loss_ref)

        dw_ref[...] += jnp.sum(dy * xhat, axis=0, keepdims=True)
        part = 0.5 * jnp.sum(jnp.mean(err * err, axis=1, keepdims=True), axis=0, keepdims=True)
        loss_ref[...] += jnp.broadcast_to(part, loss_ref.shape)

    return pl.pallas_call(
        body, name=name, grid=(t // tm,),
        in_specs=[pl.BlockSpec((tm, d), lambda i: (i, 0)), pl.BlockSpec((1, d), lambda i: (0, 0)),
                  pl.BlockSpec((tm, d), lambda i: (i, 0))],
        out_specs=[pl.BlockSpec((8, LANES), lambda i: (0, 0)), pl.BlockSpec((tm, d), lambda i: (i, 0)),
                   pl.BlockSpec((1, d), lambda i: (0, 0))],
        out_shape=[jax.ShapeDtypeStruct((8, LANES), F32), jax.ShapeDtypeStruct((t, d), F32),
                   jax.ShapeDtypeStruct((1, d), F32)],
        compiler_params=_cp(("arbitrary",)),
    )(x, w.reshape(1, d), target)


_CB = 128


def _conv_pre(u, w_ref, b_ref):
    s = u.shape[0]
    row = lax.broadcasted_iota(jnp.int32, u.shape, 0)
    pre = b_ref[...] + w_ref[CONV_WIDTH - 1:CONV_WIDTH, :] * u
    for sh in range(1, CONV_WIDTH):
        shifted = jnp.where(row >= sh, pltpu.roll(u, sh, 0), 0.0)
        pre = pre + w_ref[CONV_WIDTH - 1 - sh:CONV_WIDTH - sh, :] * shifted
    return pre


def _conv_fwd(proj3, cw, cb, *, name):
    b, s, _ = proj3.shape
    c0 = _PAD_COLS["xbc"][0] // _CB

    def body(u_ref, w_ref, b_ref, o_ref):
        pre = _conv_pre(u_ref[...].astype(F32), w_ref, b_ref)
        o_ref[...] = pre * _sigmoid(pre)

    return pl.pallas_call(
        body, name=name, grid=(b, CONV_DIM // _CB),
        in_specs=[pl.BlockSpec((None, s, _CB), lambda i, j: (i, 0, c0 + j)),
                  pl.BlockSpec((CONV_WIDTH, _CB), lambda i, j: (0, j)),
                  pl.BlockSpec((1, _CB), lambda i, j: (0, j))],
        out_specs=pl.BlockSpec((None, s, _CB), lambda i, j: (i, 0, j)),
        out_shape=jax.ShapeDtypeStruct((b, s, CONV_DIM), F32),
        compiler_params=_cp(("parallel", "parallel")),
    )(proj3, cw, cb.reshape(1, CONV_DIM))


def _conv_bwd(proj3, cw, cb, dact, *, name):
    b, s, _ = proj3.shape
    c0 = _PAD_COLS["xbc"][0] // _CB

    def body(u_ref, w_ref, b_ref, da_ref, du_ref, dwb_ref):
        u = u_ref[...].astype(F32)
        pre = _conv_pre(u, w_ref, b_ref)
        sg = _sigmoid(pre)
        dpre = da_ref[...] * (sg * (1.0 + pre * (1.0 - sg)))
        row = lax.broadcasted_iota(jnp.int32, u.shape, 0)
        du = w_ref[CONV_WIDTH - 1:CONV_WIDTH, :] * dpre
        rows = [jnp.sum(dpre * u, axis=0, keepdims=True)]
        for sh in range(1, CONV_WIDTH):
            fwd_shift = jnp.where(row < s - sh, pltpu.roll(dpre, s - sh, 0), 0.0)
            du = du + w_ref[CONV_WIDTH - 1 - sh:CONV_WIDTH - sh, :] * fwd_shift
            ush = jnp.where(row >= sh, pltpu.roll(u, sh, 0), 0.0)
            rows.append(jnp.sum(dpre * ush, axis=0, keepdims=True))
        du_ref[...] = du.astype(BF16)

        @pl.when(pl.program_id(1) == 0)
        def _():
            dwb_ref[...] = jnp.zeros_like(dwb_ref)

        for sh in range(CONV_WIDTH):
            k = CONV_WIDTH - 1 - sh
            dwb_ref[k:k + 1, :] += rows[sh]
        dwb_ref[CONV_WIDTH:CONV_WIDTH + 1, :] += jnp.sum(dpre, axis=0, keepdims=True)

    return pl.pallas_call(
        body, name=name, grid=(CONV_DIM // _CB, b),
        in_specs=[pl.BlockSpec((None, s, _CB), lambda j, i: (i, 0, c0 + j)),
                  pl.BlockSpec((CONV_WIDTH, _CB), lambda j, i: (0, j)),
                  pl.BlockSpec((1, _CB), lambda j, i: (0, j)),
                  pl.BlockSpec((None, s, _CB), lambda j, i: (i, 0, j))],
        out_specs=[pl.BlockSpec((None, s, _CB), lambda j, i: (i, 0, j)),
                   pl.BlockSpec((8, _CB), lambda j, i: (0, j))],
        out_shape=[jax.ShapeDtypeStruct((b, s, CONV_DIM), BF16), jax.ShapeDtypeStruct((8, CONV_DIM), F32)],
        compiler_params=_cp(("parallel", "arbitrary")),
    )(proj3, cw, cb.reshape(1, CONV_DIM), dact)


def _ssd_common(dt_ref, dtb_ref, alog_ref):
    row = lax.broadcasted_iota(jnp.int32, (CHUNK, CHUNK), 0)
    lane = lax.broadcasted_iota(jnp.int32, (CHUNK, CHUNK), 1)
    causal = row >= lane
    tri = causal.astype(F32)
    dtv = _softplus(dt_ref[...] + dtb_ref[...])
    a_row = -jnp.exp(alog_ref[...])
    acum = _dot(tri, dtv * a_row, precision=HIGHEST)
    return row, lane, causal, dtv, a_row, acum, acum.T


def _ssd_pair(pp, x, dtv, acum, acum_t, causal, lane, row):
    lo = lane < HEAD_DIM
    r0, r1 = 2 * pp, 2 * pp + 1
    dtp = jnp.where(lo, _col(dtv, r0), _col(dtv, r1))
    ac0, ac1 = _col(acum, r0), _col(acum, r1)
    ar0, ar1 = _row(acum_t, r0), _row(acum_t, r1)
    d0 = jnp.where(causal, jnp.exp(jnp.where(causal, ac0 - ar0, 0.0)), 0.0)
    d1 = jnp.where(causal, jnp.exp(jnp.where(causal, ac1 - ar1, 0.0)), 0.0)
    al0, al1 = _col(ar0, CHUNK - 1), _col(ar1, CHUNK - 1)
    eac = jnp.where(lo, jnp.exp(ac0), jnp.exp(ac1))
    dsp = jnp.where(lo, jnp.exp(al0 - ac0), jnp.exp(al1 - ac1))
    eal = jnp.where(_iota_col() < HEAD_DIM, jnp.exp(al0), jnp.exp(al1))
    return lo, dtp, x * dtp, d0, d1, al0, al1, eac, dsp, eal


def _ssd_fwd(proj3, gates3, xact3, dtb, alog, dsk, nw, *, name):
    b, s, _ = proj3.shape
    nc = s // CHUNK
    dt0 = 0
    z0 = _PAD_COLS["a_z"][0] // D_MODEL

    def body(xs_ref, bm_ref, cm_ref, dt_ref, z_ref, dtb_ref, alog_ref, dsk_ref, nw_ref,
             ya_ref, ypre_ref, hst_ref, h_scr):
        @pl.when(pl.program_id(1) == 0)
        def _():
            h_scr[...] = jnp.zeros_like(h_scr)

        for g in range(N_GROUPS):
            w256 = pl.ds(256 * g, 256)
            w128 = pl.ds(LANES * g, LANES)
            group(xs_ref.at[:, w256], bm_ref.at[:, w128], cm_ref.at[:, w128], dt_ref.at[:, w128],
                  z_ref.at[:, w256], dtb_ref.at[g], alog_ref.at[g], dsk_ref.at[g], nw_ref.at[g],
                  ya_ref.at[:, w256], ypre_ref.at[:, w256], hst_ref.at[g], h_scr.at[g])

    def group(xs_ref, bm_ref, cm_ref, dt_ref, z_ref, dtb_ref, alog_ref, dsk_ref, nw_ref,
              ya_ref, ypre_ref, hst_ref, h_scr):
        row, lane, causal, dtv, a_row, acum, acum_t = _ssd_common(dt_ref, dtb_ref, alog_ref)
        bb = bm_ref[...].astype(BF16)
        cb = cm_ref[...].astype(BF16)
        cbm = _dot_nt(cb, bb)
        hst_ref[...] = h_scr[...]
        dskv = dsk_ref[...]
        for pp in range(2):
            x = xs_ref[:, LANES * pp:LANES * (pp + 1)]
            lo, dtp, xd, d0, d1, al0, al1, eac, dsp, eal = _ssd_pair(pp, x, dtv, acum, acum_t, causal, lane, row)
            xdb = xd.astype(BF16)
            y = jnp.where(lo, _dot((cbm * d0).astype(BF16), xdb), _dot((cbm * d1).astype(BF16), xdb))
            h = h_scr[pp]
            y = y + eac * _dot_nt(cb, h.astype(BF16))
            h_scr[pp] = h * eal + _dot_tn((xd * dsp).astype(BF16), bb)
            dskp = jnp.where((_iota_row() < HEAD_DIM), _col(dskv, 2 * pp), _col(dskv, 2 * pp + 1))
            ypre_ref[:, LANES * pp:LANES * (pp + 1)] = y + x * dskp
        ypre = ypre_ref[...]
        z = z_ref[...].astype(F32)
        yg = ypre * (z * _sigmoid(z))
        rstd = lax.rsqrt(jnp.sum(yg * yg, axis=1, keepdims=True) * (1.0 / 256.0) + NORM_EPS)
        ya_ref[...] = (yg * rstd * nw_ref[...]).astype(BF16)

    g = N_GROUPS
    par = pl.BlockSpec((g, 1, LANES), lambda i, c: (0, 0, 0))
    wide = pl.BlockSpec((None, CHUNK, D_MODEL), lambda i, c: (i, c, 0))
    return pl.pallas_call(
        body, name=name, grid=(b, nc),
        in_specs=[wide,
                  pl.BlockSpec((None, CHUNK, 512), lambda i, c: (i, c, 2)),
                  pl.BlockSpec((None, CHUNK, 512), lambda i, c: (i, c, 3)),
                  pl.BlockSpec((None, CHUNK, 512), lambda i, c: (i, c, dt0)),
                  pl.BlockSpec((None, CHUNK, D_MODEL), lambda i, c: (i, c, z0)),
                  par, par, par,
                  pl.BlockSpec((g, 1, 256), lambda i, c: (0, 0, 0))],
        out_specs=[wide, wide,
                   pl.BlockSpec((None, None, g, 2, CHUNK, SSM_STATE), lambda i, c: (i, c, 0, 0, 0, 0))],
        out_shape=[jax.ShapeDtypeStruct((b, s, D_MODEL), BF16), jax.ShapeDtypeStruct((b, s, D_MODEL), F32),
                   jax.ShapeDtypeStruct((b, nc, g, 2, CHUNK, SSM_STATE), F32)],
        scratch_shapes=[pltpu.VMEM((g, 2, CHUNK, SSM_STATE), F32)],
        compiler_params=_cp(("parallel", "arbitrary")),
    )(xact3, xact3, xact3, gates3, proj3, dtb, alog, dsk, nw)


def _ssd_bwd(proj3, gates3, xact3, dtb, alog, dsk, nw, ypre3, hst, dya3, *, name):
    b, s, _ = proj3.shape
    nc = s // CHUNK
    dt0 = 0
    z0 = _PAD_COLS["a_z"][0] // D_MODEL

    def body(xs_ref, bm_ref, cm_ref, dt_ref, z_ref, dtb_ref, alog_ref, dsk_ref, nw_ref, ypre_ref, hst_ref,
             dya_ref, dact_ref, dz_ref, ddt_ref, ddtb_ref, dalog_ref, ddsk_ref, dnw_ref, dh_scr):
        first = jnp.logical_and(pl.program_id(0) == 0, pl.program_id(1) == 0)

        @pl.when(first)
        def _():
            ddtb_ref[...] = jnp.zeros_like(ddtb_ref)
            dalog_ref[...] = jnp.zeros_like(dalog_ref)
            ddsk_ref[...] = jnp.zeros_like(ddsk_ref)
            dnw_ref[...] = jnp.zeros_like(dnw_ref)

        @pl.when(pl.program_id(1) == 0)
        def _():
            dh_scr[...] = jnp.zeros_like(dh_scr)

        for g in range(N_GROUPS):
            w256 = pl.ds(256 * g, 256)
            w128 = pl.ds(LANES * g, LANES)
            group(xs_ref.at[:, w256], bm_ref.at[:, w128], cm_ref.at[:, w128], dt_ref.at[:, w128],
                  z_ref.at[:, w256], dtb_ref.at[g], alog_ref.at[g], dsk_ref.at[g], nw_ref.at[g],
                  ypre_ref.at[:, w256], hst_ref.at[g], dya_ref.at[:, w256],
                  dact_ref.at[:, w256], dact_ref.at[:, pl.ds(D_MODEL + LANES * g, LANES)],
                  dact_ref.at[:, pl.ds(D_MODEL + 512 + LANES * g, LANES)], dz_ref.at[:, w256], ddt_ref.at[:, w128],
                  ddtb_ref.at[g], dalog_ref.at[g], ddsk_ref.at[g], dnw_ref.at[g], dh_scr.at[g])

    def group(xs_ref, bm_ref, cm_ref, dt_ref, z_ref, dtb_ref, alog_ref, dsk_ref, nw_ref, ypre_ref, hst_ref,
              dya_ref, dxs_ref, dbm_ref, dcm_ref, dz_ref, ddt_ref, ddtb_ref, dalog_ref, ddsk_ref, dnw_ref,
              dh_scr):
        row, lane, causal, dtv, a_row, acum, acum_t = _ssd_common(dt_ref, dtb_ref, alog_ref)
        lane1 = _iota_row()
        bb = bm_ref[...].astype(BF16)
        cb = cm_ref[...].astype(BF16)
        cbm = _dot_nt(cb, bb)

        z = z_ref[...].astype(F32)
        ypre = ypre_ref[...]
        dya = dya_ref[...]
        sz = _sigmoid(z)
        silu = z * sz
        yg = ypre * silu
        rstd = lax.rsqrt(jnp.sum(yg * yg, axis=1, keepdims=True) * (1.0 / 256.0) + NORM_EPS)
        dnw_ref[...] += jnp.sum(dya * yg * rstd, axis=0, keepdims=True)
        dn = dya * nw_ref[...]
        dyg = rstd * dn - yg * (rstd * rstd * rstd * (1.0 / 256.0)) * jnp.sum(dn * yg, axis=1, keepdims=True)
        dz_ref[...] = (dyg * ypre * (sz * (1.0 + z * (1.0 - sz)))).astype(BF16)
        dy_all = dyg * silu

        dskv = dsk_ref[...]
        da_cols = jnp.zeros((CHUNK, LANES), F32)
        dxt_cols = jnp.zeros((CHUNK, LANES), F32)
        ddsk_row = jnp.zeros((1, LANES), F32)
        dcb = jnp.zeros((CHUNK, CHUNK), F32)
        dc = jnp.zeros((CHUNK, SSM_STATE), F32)
        db = jnp.zeros((CHUNK, SSM_STATE), F32)
        last = _iota_col() == CHUNK - 1
        for pp in range(2):
            r0, r1 = 2 * pp, 2 * pp + 1
            x = xs_ref[:, LANES * pp:LANES * (pp + 1)]
            dy = dy_all[:, LANES * pp:LANES * (pp + 1)]
            lo, dtp, xd, d0, d1, al0, al1, eac, dsp, eal = _ssd_pair(pp, x, dtv, acum, acum_t, causal, lane, row)
            w0, w1 = cbm * d0, cbm * d1
            w0b, w1b = w0.astype(BF16), w1.astype(BF16)
            xdb = xd.astype(BF16)
            dyb = dy.astype(BF16)
            h = hst_ref[pp]
            dhn = dh_scr[pp]
            hb = h.astype(BF16)
            dhb = dhn.astype(BF16)
            g0 = _dot_nt(jnp.where(lo, dy, 0.0).astype(BF16), xdb)
            g1 = _dot_nt(jnp.where(lo, 0.0, dy).astype(BF16), xdb)
            dcb = dcb + g0 * d0 + g1 * d1
            m0, m1 = g0 * w0, g1 * w1
            bdh = _dot_nt(bb, dhb)
            dxd = jnp.where(lo, _dot_tn(w0b, dyb), _dot_tn(w1b, dyb)) + dsp * bdh
            ch = _dot_nt(cb, hb)
            edy = eac * dy
            edyb = edy.astype(BF16)
            xds = xd * dsp
            dc = dc + _dot(edyb, hb)
            db = db + _dot(xds.astype(BF16), dhb)
            dh_scr[pp] = dhn * eal + _dot_tn(edyb, cb)
            t2 = edy * ch
            t3 = xds * bdh
            dhh = dhn * h
            s4_0 = jnp.sum(jnp.sum(jnp.where(row < HEAD_DIM, dhh, 0.0), axis=0, keepdims=True), axis=1, keepdims=True)
            s4_1 = jnp.sum(jnp.sum(dhh, axis=0, keepdims=True), axis=1, keepdims=True) - s4_0
            t23 = t2 - t3
            t23_0 = jnp.sum(jnp.where(lo, t23, 0.0), axis=1, keepdims=True)
            t23_1 = jnp.sum(t23, axis=1, keepdims=True) - t23_0
            c3 = jnp.sum(t3, axis=0, keepdims=True)
            c3_0 = jnp.sum(jnp.where(_iota_row() < HEAD_DIM, c3, 0.0), axis=1, keepdims=True)
            c3_1 = jnp.sum(c3, axis=1, keepdims=True) - c3_0
            dal0 = c3_0 + jnp.exp(al0) * s4_0
            dal1 = c3_1 + jnp.exp(al1) * s4_1
            dac0 = jnp.sum(m0 - m0.T, axis=1, keepdims=True) + t23_0 + jnp.where(last, dal0, 0.0)
            dac1 = jnp.sum(m1 - m1.T, axis=1, keepdims=True) + t23_1 + jnp.where(last, dal1, 0.0)
            da_cols = da_cols + jnp.where(lane == r0, dac0, 0.0) + jnp.where(lane == r1, dac1, 0.0)
            xx = dxd * x
            x0 = jnp.sum(jnp.where(lo, xx, 0.0), axis=1, keepdims=True)
            x1 = jnp.sum(xx, axis=1, keepdims=True) - x0
            dxt_cols = dxt_cols + jnp.where(lane == r0, x0, 0.0) + jnp.where(lane == r1, x1, 0.0)
            dskp = jnp.where((_iota_row() < HEAD_DIM), _col(dskv, r0), _col(dskv, r1))
            dxs_ref[:, LANES * pp:LANES * (pp + 1)] = dxd * dtp + dy * dskp
            yx = jnp.sum(dy * x, axis=0, keepdims=True)
            k0 = jnp.sum(jnp.where((_iota_row() < HEAD_DIM), yx, 0.0), axis=1, keepdims=True)
            k1 = jnp.sum(yx, axis=1, keepdims=True) - k0
            ddsk_row = ddsk_row + jnp.where(lane1 == r0, k0, 0.0) + jnp.where(lane1 == r1, k1, 0.0)
        dcbb = dcb.astype(BF16)
        dcm_ref[...] = dc + _dot(dcbb, bb)
        dbm_ref[...] = db + _dot_tn(dcbb, cb)
        tri_t = (row <= lane).astype(F32)
        dadt = _dot(tri_t, da_cols, precision=HIGHEST)
        ddtv = dadt * a_row + dxt_cols
        dalog_ref[...] += jnp.sum(dadt * dtv, axis=0, keepdims=True) * a_row
        ddt_raw = ddtv * _sigmoid(dt_ref[...] + dtb_ref[...])
        ddt_ref[...] = ddt_raw.astype(BF16)
        ddtb_ref[...] += jnp.sum(ddt_raw, axis=0, keepdims=True)
        ddsk_ref[...] += ddsk_row

    g = N_GROUPS
    rc = lambda c: nc - 1 - c
    par = pl.BlockSpec((g, 1, LANES), lambda i, c: (0, 0, 0))
    parw = pl.BlockSpec((g, 1, 256), lambda i, c: (0, 0, 0))
    wide = pl.BlockSpec((None, CHUNK, D_MODEL), lambda i, c: (i, rc(c), 0))
    blk512 = lambda col: pl.BlockSpec((None, CHUNK, 512), lambda i, c: (i, rc(c), col))
    return pl.pallas_call(
        body, name=name, grid=(b, nc),
        in_specs=[wide, blk512(2), blk512(3), blk512(dt0),
                  pl.BlockSpec((None, CHUNK, D_MODEL), lambda i, c: (i, rc(c), z0)),
                  par, par, par, parw,
                  wide,
                  pl.BlockSpec((None, None, g, 2, CHUNK, SSM_STATE), lambda i, c: (i, rc(c), 0, 0, 0, 0)),
                  wide],
        out_specs=[pl.BlockSpec((None, CHUNK, CONV_DIM), lambda i, c: (i, rc(c), 0)), wide, blk512(0),
                   par, par, par, parw],
        out_shape=[jax.ShapeDtypeStruct((b, s, CONV_DIM), F32), jax.ShapeDtypeStruct((b, s, D_MODEL), BF16),
                   jax.ShapeDtypeStruct((b, s, 512), BF16),
                   jax.ShapeDtypeStruct((g, 1, LANES), F32), jax.ShapeDtypeStruct((g, 1, LANES), F32),
                   jax.ShapeDtypeStruct((g, 1, LANES), F32), jax.ShapeDtypeStruct((g, 1, 256), F32)],
        scratch_shapes=[pltpu.VMEM((g, 2, CHUNK, SSM_STATE), F32)],
        compiler_params=_cp(("arbitrary", "arbitrary")),
    )(xact3, xact3, xact3, gates3, proj3, dtb, alog, dsk, nw, ypre3, hst, dya3)


_FGATE_ROWS = 512


def _fgate_fwd(gates3, fb, *, name):
    b, s, _ = gates3.shape
    rows = min(_FGATE_ROWS, s)
    f0 = _PAD_COLS["a_dt"][1] // LANES

    def body(f_ref, fb_ref, cum_ref, carry):
        @pl.when(pl.program_id(1) == 0)
        def _():
            carry[...] = jnp.zeros_like(carry)

        row = lax.broadcasted_iota(jnp.int32, (rows, rows), 0)
        lane = lax.broadcasted_iota(jnp.int32, (rows, rows), 1)
        tri = (row >= lane).astype(F32)
        lf = -_softplus(-(f_ref[...] + fb_ref[...]))
        cs = _dot(tri, lf, precision=HIGHEST) + carry[0:1, :]
        cum_ref[...] = cs
        carry[0:1, :] = _row(cs, rows - 1)

    return pl.pallas_call(
        body, name=name, grid=(b, s // rows),
        in_specs=[pl.BlockSpec((None, rows, LANES), lambda i, c: (i, c, f0)),
                  pl.BlockSpec((1, LANES), lambda i, c: (0, 0))],
        out_specs=pl.BlockSpec((None, rows, LANES), lambda i, c: (i, c, 0)),
        out_shape=jax.ShapeDtypeStruct((b, s, LANES), F32),
        scratch_shapes=[pltpu.VMEM((8, LANES), F32)],
        compiler_params=_cp(("parallel", "arbitrary")),
    )(gates3, fb)


def _fgate_bwd(gates3, fb, dcum, *, name):
    b, s, _ = gates3.shape
    rows = min(_FGATE_ROWS, s)
    nc = s // rows
    f0 = _PAD_COLS["a_dt"][1] // LANES
    npair = dcum.shape[1]

    def body(f_ref, fb_ref, dc_ref, df_ref, dfb_ref, carry):
        first = jnp.logical_and(pl.program_id(0) == 0, pl.program_id(1) == 0)

        @pl.when(first)
        def _():
            dfb_ref[...] = jnp.zeros_like(dfb_ref)

        @pl.when(pl.program_id(1) == 0)
        def _():
            carry[...] = jnp.zeros_like(carry)

        row = lax.broadcasted_iota(jnp.int32, (rows, rows), 0)
        lane = lax.broadcasted_iota(jnp.int32, (rows, rows), 1)
        tri_t = (row <= lane).astype(F32)
        dc = -jnp.sum(dc_ref[...], axis=0)
        dlf = _dot(tri_t, dc, precision=HIGHEST) + carry[0:1, :]
        carry[0:1, :] = _row(dlf, 0)
        df = dlf * _sigmoid(-(f_ref[...] + fb_ref[...]))
        df_ref[...] = df.astype(BF16)
        dfb_ref[...] += jnp.sum(df, axis=0, keepdims=True)

    return pl.pallas_call(
        body, name=name, grid=(b, nc),
        in_specs=[pl.BlockSpec((None, rows, LANES), lambda i, c: (i, nc - 1 - c, f0)),
                  pl.BlockSpec((1, LANES), lambda i, c: (0, 0)),
                  pl.BlockSpec((None, npair, rows, LANES), lambda i, c: (i, 0, nc - 1 - c, 0))],
        out_specs=[pl.BlockSpec((None, rows, LANES), lambda i, c: (i, nc - 1 - c, 0)),
                   pl.BlockSpec((1, LANES), lambda i, c: (0, 0))],
        out_shape=[jax.ShapeDtypeStruct((b, s, LANES), BF16), jax.ShapeDtypeStruct((1, LANES), F32)],
        scratch_shapes=[pltpu.VMEM((8, LANES), F32)],
        compiler_params=_cp(("arbitrary", "arbitrary")),
    )(gates3, fb, dcum)


_SCALE = HEAD_DIM ** -0.5
_NEG = -1e30


_ST_LSE, _ST_DELTA, _ST_MJ = 0, 2, 8


_SR = 40


def _ck_rep(cum):
    b, s, _ = cum.shape
    t = jnp.transpose(cum[:, :, :N_HEADS], (0, 2, 1)).reshape(b, N_HEADS // 2, 2, s, 1)
    return jnp.broadcast_to(t, (b, N_HEADS // 2, 2, s, LANES))


def _foxt_fwd(proj3, ckrep, *, name, tb):
    b, s, _ = proj3.shape
    nq = s // tb
    assert _ST_MJ + 2 * nq <= _SR
    q0 = _PAD_COLS["c_q"][0] // LANES
    k0 = _PAD_COLS["c_k"][0] // LANES
    v0 = _PAD_COLS["c_v"][0] // LANES
    z0 = _PAD_COLS["c_z"][0] // LANES
    rep = tb // LANES

    def body(q_ref, k_ref, v_ref, z_ref, ck_ref, y_ref, o_ref, st_ref):
        i = pl.program_id(2)
        lane = lax.broadcasted_iota(jnp.int32, (tb, LANES), 1)
        lo = lane < HEAD_DIM
        lo_r = lax.broadcasted_iota(jnp.int32, (LANES, tb), 0) < HEAD_DIM
        srow = lax.broadcasted_iota(jnp.int32, (_SR, tb), 0)
        q = q_ref[...].astype(F32) * _SCALE
        qms = (jnp.where(lo, q, 0.0).astype(BF16), jnp.where(lo, 0.0, q).astype(BF16))
        ones_at = (HEAD_DIM, 0)

        def block(j, carry, diagonal):
            ks = pl.ds(pl.multiple_of(j * tb, tb), tb)
            kb = k_ref[ks, :].astype(BF16)
            v = v_ref[ks, :].astype(F32)
            vts = (jnp.where(lo, v, jnp.where(lane == ones_at[0], 1.0, 0.0)).T.astype(BF16),
                   jnp.where(lo, jnp.where(lane == ones_at[1], 1.0, 0.0), v).T.astype(BF16))
            if diagonal:
                key = lax.broadcasted_iota(jnp.int32, (tb, tb), 0)
                qry = lax.broadcasted_iota(jnp.int32, (tb, tb), 1)
                mask = key <= qry
            ms, ls, acc, st = carry
            new_m, new_l, pvs, alphas = [], [], [], []
            for hh in range(2):
                sc = _dot_nt(kb, qms[hh]) - jnp.tile(ck_ref[hh, ks, :], (1, rep))
                if diagonal:
                    sc = jnp.where(mask, sc, _NEG)
                m_new = jnp.maximum(ms[hh], jnp.max(sc, axis=0, keepdims=True))
                alpha = jnp.exp(ms[hh] - m_new)
                pv = _dot(vts[hh], jnp.exp(sc - m_new).astype(BF16))
                rs = _row(pv[ones_at[hh]:ones_at[hh] + 8, :], 0)
                new_l.append(alpha * ls[hh] + rs)
                new_m.append(m_new)
                pvs.append(pv)
                alphas.append(alpha)
                st = jnp.where(srow == _ST_MJ + 2 * j + hh, m_new, st)
            acc = jnp.where(lo_r, alphas[0] * acc + pvs[0], alphas[1] * acc + pvs[1])
            return (tuple(new_m), tuple(new_l), acc, st)

        neg = jnp.full((1, tb), _NEG, F32)
        zero = jnp.zeros((1, tb), F32)
        init = ((neg, neg), (zero, zero), jnp.zeros((LANES, tb), F32), jnp.zeros((_SR, tb), F32))
        carry = lax.fori_loop(0, i, lambda j, c: block(j, c, False), init)
        ms, ls, acc, st = block(i, carry, True)
        o = (acc / jnp.where(lo_r, ls[0], ls[1])).T
        o_ref[...] = o
        st = jnp.where(srow == _ST_LSE, ms[0] + jnp.log(ls[0]), st)
        st_ref[...] = jnp.where(srow == _ST_LSE + 1, ms[1] + jnp.log(ls[1]), st)
        z = z_ref[...].astype(F32)
        y_ref[...] = (o * (z * _sigmoid(z))).astype(BF16)

    qspec = lambda c0: pl.BlockSpec((None, tb, LANES), lambda bi, p, i: (bi, i, c0 + p))
    kspec = lambda c0: pl.BlockSpec((None, s, LANES), lambda bi, p, i: (bi, 0, c0 + p))
    ospec = pl.BlockSpec((None, tb, LANES), lambda bi, p, i: (bi, i, p))
    return pl.pallas_call(
        body, name=name, grid=(b, N_HEADS // 2, nq),
        in_specs=[qspec(q0), kspec(k0), kspec(v0), qspec(z0),
                  pl.BlockSpec((None, None, 2, s, LANES), lambda bi, p, i: (bi, p, 0, 0, 0))],
        out_specs=[ospec, ospec, pl.BlockSpec((None, None, None, _SR, tb), lambda bi, p, i: (bi, p, i, 0, 0))],
        out_shape=[jax.ShapeDtypeStruct((b, s, D_MODEL), BF16), jax.ShapeDtypeStruct((b, s, D_MODEL), F32),
                   jax.ShapeDtypeStruct((b, N_HEADS // 2, nq, _SR, tb), F32)],
        compiler_params=_cp(("parallel", "parallel", "arbitrary")),
    )(proj3, proj3, proj3, proj3, ckrep)


def _foxt_prep(proj3, o3, stat, dy3, *, name, tb):
    b, s, _ = proj3.shape
    nq = s // tb
    z0 = _PAD_COLS["c_z"][0] // 256

    def body(z_ref, o_ref, fst_ref, dy_ref, dz_ref, do_ref, st_ref):
        z = z_ref[...].astype(F32)
        sz = _sigmoid(z)
        dy = dy_ref[...]
        o = o_ref[...]
        do = dy * (z * sz)
        dz_ref[...] = (dy * o * (sz * (1.0 + z * (1.0 - sz)))).astype(BF16)
        do_ref[...] = do
        doo = do.astype(BF16).astype(F32) * o
        r8 = lax.broadcasted_iota(jnp.int32, (8, LANES), 0)
        l8 = lax.broadcasted_iota(jnp.int32, (8, LANES), 1)
        pick = jnp.logical_or(jnp.logical_and(r8 == 0, l8 < HEAD_DIM),
                              jnp.logical_and(r8 == 1, l8 >= HEAD_DIM)).astype(F32)
        srow = lax.broadcasted_iota(jnp.int32, (_SR, tb), 0)
        for pp in range(2):
            d8 = _dot(pick, doo[:, LANES * pp:LANES * (pp + 1)], ((1,), (1,)), precision=HIGHEST)
            st = jnp.where(srow == _ST_DELTA, _row(d8, 0), fst_ref[pp])
            st_ref[pp] = jnp.where(srow == _ST_DELTA + 1, _row(d8, 1), st)

    ospec = pl.BlockSpec((None, tb, 256), lambda bi, p, i: (bi, i, p))
    sspec = pl.BlockSpec((None, 2, None, _SR, tb), lambda bi, p, i: (bi, p, i, 0, 0))
    return pl.pallas_call(
        body, name=name, grid=(b, N_HEADS // 4, nq),
        in_specs=[pl.BlockSpec((None, tb, 256), lambda bi, p, i: (bi, i, z0 + p)), ospec, sspec, ospec],
        out_specs=[ospec, ospec, sspec],
        out_shape=[jax.ShapeDtypeStruct((b, s, D_MODEL), BF16), jax.ShapeDtypeStruct((b, s, D_MODEL), F32),
                   jax.ShapeDtypeStruct((b, N_HEADS // 2, nq, _SR, tb), F32)],
        compiler_params=_cp(("parallel", "parallel", "parallel")),
    )(proj3, o3, stat, dy3)


def _foxt_bwd(proj3, ckrep, do3, stats, *, name, tb):
    b, s, _ = proj3.shape
    nq = s // tb
    q0 = _PAD_COLS["c_q"][0] // LANES
    k0 = _PAD_COLS["c_k"][0] // LANES
    v0 = _PAD_COLS["c_v"][0] // LANES
    rep = tb // LANES

    def body(q_ref, do_ref, st_ref, k_ref, v_ref, ck_ref, dq_ref, dk_ref, dv_ref, cs_ref):
        j = pl.program_id(2)
        lane = lax.broadcasted_iota(jnp.int32, (tb, LANES), 1)
        lo = lane < HEAD_DIM
        lo_r = lax.broadcasted_iota(jnp.int32, (LANES, tb), 0) < HEAD_DIM

        @pl.when(j == 0)
        def _():
            dq_ref[...] = jnp.zeros_like(dq_ref)

        kf = k_ref[...].astype(F32)
        kb = kf.astype(BF16)
        kt = kf.T.astype(BF16)
        vb = v_ref[...].astype(BF16)
        cks = (jnp.tile(ck_ref[0], (1, rep)), jnp.tile(ck_ref[1], (1, rep)))

        def block(i, carry, diagonal):
            qs = pl.ds(pl.multiple_of(i * tb, tb), tb)
            q = q_ref[qs, :].astype(F32) * _SCALE
            do = do_ref[qs, :]
            st = st_ref[i]
            if diagonal:
                key = lax.broadcasted_iota(jnp.int32, (tb, tb), 0)
                qry = lax.broadcasted_iota(jnp.int32, (tb, tb), 1)
                mask = key <= qry
            dk, dv, cs = carry
            new_cs, dqs = [], []
            for hh in range(2):
                sel = lo if hh == 0 else jnp.logical_not(lo)
                qm = jnp.where(sel, q, 0.0).astype(BF16)
                dom = jnp.where(sel, do, 0.0).astype(BF16)
                sc = _dot_nt(kb, qm) - cks[hh]
                if diagonal:
                    sc = jnp.where(mask, sc, _NEG)
                mj = _row(st, _ST_MJ + 2 * j + hh)
                w = jnp.exp(mj - _row(st, _ST_LSE + hh))
                ph = jnp.exp(sc - mj).astype(BF16).astype(F32) * w
                ds = ph * (_dot_nt(vb, dom) - _row(st, _ST_DELTA + hh))
                dsb = ds.astype(BF16)
                dv = dv + _dot(ph.astype(BF16), dom)
                dk = dk + _dot(dsb, qm)
                new_cs.append(cs[hh] + jnp.sum(ds, axis=1, keepdims=True))
                dqs.append(_dot(kt, dsb))
            dq_ref[i] += jnp.where(lo_r, dqs[0], dqs[1]) * _SCALE
            return (dk, dv, tuple(new_cs))

        hq = tb // 2

        def diag_tile(half, carry):
            nk = hq if half == 0 else tb
            qs = pl.ds(pl.multiple_of(j * tb + half * hq, hq), hq)
            q = q_ref[qs, :].astype(F32) * _SCALE
            do = do_ref[qs, :]
            st = st_ref[j][:, half * hq:(half + 1) * hq]
            kbs, vbs, kts = kb[:nk], vb[:nk], kt[:, :nk]
            key = lax.broadcasted_iota(jnp.int32, (nk, hq), 0)
            qry = lax.broadcasted_iota(jnp.int32, (nk, hq), 1) + half * hq
            mask = key <= qry
            dk, dv, cs = carry
            dkc = jnp.zeros((nk, LANES), F32)
            dvc = jnp.zeros((nk, LANES), F32)
            csc, dqs = [], []
            for hh in range(2):
                sel = lo[:hq] if hh == 0 else jnp.logical_not(lo[:hq])
                qm = jnp.where(sel, q, 0.0).astype(BF16)
                dom = jnp.where(sel, do, 0.0).astype(BF16)
                sc = jnp.where(mask, _dot_nt(kbs, qm) - cks[hh][:nk, :hq], _NEG)
                mj = _row(st, _ST_MJ + 2 * j + hh)
                w = jnp.exp(mj - _row(st, _ST_LSE + hh))
                ph = jnp.exp(sc - mj).astype(BF16).astype(F32) * w
                ds = ph * (_dot_nt(vbs, dom) - _row(st, _ST_DELTA + hh))
                dsb = ds.astype(BF16)
                dvc = dvc + _dot(ph.astype(BF16), dom)
                dkc = dkc + _dot(dsb, qm)
                csc.append(jnp.sum(ds, axis=1, keepdims=True))
                dqs.append(_dot(kts, dsb))
            dq_ref[j, :, half * hq:(half + 1) * hq] += jnp.where(lo_r[:, :hq], dqs[0], dqs[1]) * _SCALE
            if nk < tb:
                grow = lambda a: jnp.concatenate([a, jnp.zeros((tb - nk,) + a.shape[1:], F32)], axis=0)
                dkc, dvc, csc = grow(dkc), grow(dvc), [grow(c) for c in csc]
            return (dk + dkc, dv + dvc, (cs[0] + csc[0], cs[1] + csc[1]))

        zcol = jnp.zeros((tb, 1), F32)
        init = (jnp.zeros((tb, LANES), F32), jnp.zeros((tb, LANES), F32), (zcol, zcol))
        carry = diag_tile(1, diag_tile(0, init))
        dk, dv, cs = lax.fori_loop(j + 1, nq, lambda i, c: block(i, c, False), carry)
        dk_ref[...] = dk.astype(BF16)
        dv_ref[...] = dv.astype(BF16)
        p2 = 2 * pl.program_id(1)
        cs_ref[...] = jnp.where(lane == p2, cs[0], jnp.where(lane == p2 + 1, cs[1], 0.0))

    full = lambda c0: pl.BlockSpec((None, s, LANES), lambda bi, p, j: (bi, 0, c0 + p))
    kspec = lambda c0: pl.BlockSpec((None, tb, LANES), lambda bi, p, j: (bi, j, c0 + p))
    ko = pl.BlockSpec((None, tb, LANES), lambda bi, p, j: (bi, j, p))
    sall = pl.BlockSpec((None, None, nq, _SR, tb), lambda bi, p, j: (bi, p, 0, 0, 0))
    dqspec = pl.BlockSpec((None, None, nq, LANES, tb), lambda bi, p, j: (bi, p, 0, 0, 0))
    return pl.pallas_call(
        body, name=name, grid=(b, N_HEADS // 2, nq),
        in_specs=[full(q0), full(0), sall, kspec(k0), kspec(v0),
                  pl.BlockSpec((None, None, 2, tb, LANES), lambda bi, p, j: (bi, p, 0, j, 0))],
        out_specs=[dqspec, ko, ko, pl.BlockSpec((None, None, tb, LANES), lambda bi, p, j: (bi, p, j, 0))],
        out_shape=[jax.ShapeDtypeStruct((b, N_HEADS // 2, nq, LANES, tb), F32),
                   jax.ShapeDtypeStruct((b, s, D_MODEL), BF16), jax.ShapeDtypeStruct((b, s, D_MODEL), BF16),
                   jax.ShapeDtypeStruct((b, N_HEADS // 2, s, LANES), F32)],
        compiler_params=_cp(("parallel", "parallel", "arbitrary")),
    )(proj3, do3, stats, proj3, proj3, ckrep)


def _rope(x, cos, sin_signed):
    w = x.shape[1]
    lane = lax.broadcasted_iota(jnp.int32, x.shape, 1)
    first = (lane % HEAD_DIM) < (HEAD_DIM // 2)
    rot = jnp.where(first, pltpu.roll(x, w - HEAD_DIM // 2, 1), pltpu.roll(x, HEAD_DIM // 2, 1))
    return x * cos + rot * sin_signed


_QB = 8
_QROWS = _QB * CHUNK


def _swa_keys(g, kc_ref, kp_ref, vc_ref, vp_ref, cq_ref, sq_ref, cp_ref, sp_ref):
    def both_halves(x):
        x = x.astype(F32)
        lane = lax.broadcasted_iota(jnp.int32, x.shape, 1)
        keep = (lane // HEAD_DIM) == (g % 2)
        return jnp.where(keep, x, pltpu.roll(x, HEAD_DIM, 1))

    cq, sq, cpv, spv = cq_ref[...], sq_ref[...], cp_ref[...], sp_ref[...]
    kc = _rope(both_halves(kc_ref[...]), cq, sq).astype(BF16)
    kp = _rope(both_halves(kp_ref[...]), cpv, spv).astype(BF16)
    return cq, sq, cpv, spv, kc, kp, both_halves(vc_ref[...]).astype(BF16), both_halves(vp_ref[...]).astype(BF16)


def _swa_stack(pairs, lo):
    return jnp.concatenate([jnp.where(lo, pairs[0], 0.0), jnp.where(lo, 0.0, pairs[0]),
                            jnp.where(lo, pairs[1], 0.0), jnp.where(lo, 0.0, pairs[1])], axis=0).astype(BF16)


def _swa_mask4(prev_valid):
    r = lax.broadcasted_iota(jnp.int32, (4 * CHUNK, 2 * CHUNK), 0) & (CHUNK - 1)
    c = lax.broadcasted_iota(jnp.int32, (4 * CHUNK, 2 * CHUNK), 1)
    own = jnp.logical_and(c >= CHUNK, c - CHUNK <= r)
    before = jnp.logical_and(c < CHUNK, c > r)
    if prev_valid is True:
        return jnp.logical_or(own, before)
    return jnp.logical_or(own, jnp.logical_and(before, prev_valid))


def _swa_sink4(skv):
    return jnp.concatenate([jnp.broadcast_to(_col(skv, j), (CHUNK, 1)) for j in range(4)], axis=0)


def _swa_specs(order):
    def spec(shape, fn):
        return pl.BlockSpec(shape, lambda *ids: fn(*order(*ids)))

    q0 = _PAD_COLS["b_q"][0] // 256
    z0 = _PAD_COLS["b_z"][0] // 256
    k0 = _PAD_COLS["b_k"][0] // LANES
    v0 = _PAD_COLS["b_v"][0] // LANES
    prev = lambda i: jnp.maximum(_QB * i - 1, 0)
    return dict(
        kc=spec((None, _QROWS, LANES), lambda bi, g, i: (bi, i, k0 + g // 2)),
        kp=spec((None, CHUNK, LANES), lambda bi, g, i: (bi, prev(i), k0 + g // 2)),
        vc=spec((None, _QROWS, LANES), lambda bi, g, i: (bi, i, v0 + g // 2)),
        vp=spec((None, CHUNK, LANES), lambda bi, g, i: (bi, prev(i), v0 + g // 2)),
        q=spec((None, _QROWS, 256), lambda bi, g, i: (bi, i, q0 + g)),
        z=spec((None, _QROWS, 256), lambda bi, g, i: (bi, i, z0 + g)),
        blk=spec((None, _QROWS, 256), lambda bi, g, i: (bi, i, g)),
        kcur=spec((None, _QROWS, LANES), lambda bi, g, i: (bi, i, g)),
        kstep=spec((None, CHUNK, LANES), lambda bi, g, i: (bi, i, g)),
        tcur=spec((_QROWS, LANES), lambda bi, g, i: (i, 0)),
        tprev=spec((CHUNK, LANES), lambda bi, g, i: (prev(i), 0)),
        sk=spec((None, 1, LANES), lambda bi, g, i: (g, 0, 0)))


def _swa_fwd(proj3, cos, sin, sinks, *, name):
    b, s, _ = proj3.shape

    def body(q_ref, z_ref, kc_ref, kp_ref, vc_ref, vp_ref, cq_ref, sq_ref, cp_ref, sp_ref, sk_ref,
             y_ref, o_ref, lse_ref):
        i = pl.program_id(2)
        cq_all, sq_all, _, _, kc_all, kp0, vc_all, vp0 = _swa_keys(
            pl.program_id(1), kc_ref, kp_ref, vc_ref, vp_ref, cq_ref, sq_ref, cp_ref, sp_ref)
        lo = lax.broadcasted_iota(jnp.int32, (CHUNK, LANES), 1) < HEAD_DIM
        sink4 = _swa_sink4(sk_ref[...])
        for u in range(_QB):
            rs = slice(CHUNK * u, CHUNK * (u + 1))
            ps = slice(CHUNK * (u - 1), CHUNK * u)
            cq, sq = cq_all[rs], sq_all[rs]
            kp, vp = (kp0, vp0) if u == 0 else (kc_all[ps], vc_all[ps])
            kk = jnp.concatenate([kp, kc_all[rs]], axis=0)
            vv = jnp.concatenate([vp, vc_all[rs]], axis=0)
            q4 = _swa_stack([_rope(q_ref[rs, LANES * pp:LANES * (pp + 1)].astype(F32), cq, sq) * _SCALE
                             for pp in range(2)], lo)
            sc = jnp.where(_swa_mask4(True if u > 0 else i > 0), _dot_nt(q4, kk), _NEG)
            m = jnp.maximum(jnp.max(sc, axis=1, keepdims=True), sink4)
            pr = jnp.exp(sc - m)
            l = jnp.sum(pr, axis=1, keepdims=True) + jnp.exp(sink4 - m)
            o4 = _dot(pr.astype(BF16), vv) / l
            lse4 = m + jnp.log(l)
            for pp in range(2):
                ls = slice(LANES * pp, LANES * (pp + 1))
                h0 = slice(2 * CHUNK * pp, 2 * CHUNK * pp + CHUNK)
                h1 = slice(2 * CHUNK * pp + CHUNK, 2 * CHUNK * (pp + 1))
                o = jnp.where(lo, o4[h0], o4[h1])
                z = z_ref[rs, ls].astype(F32)
                o_ref[rs, ls] = o
                lse_ref[rs, ls] = jnp.where(lo, lse4[h0], lse4[h1])
                y_ref[rs, ls] = (o * (z * _sigmoid(z))).astype(BF16)

    sp = _swa_specs(lambda bi, g, i: (bi, g, i))
    return pl.pallas_call(
        body, name=name, grid=(b, N_GROUPS, s // _QROWS),
        in_specs=[sp["q"], sp["z"], sp["kc"], sp["kp"], sp["vc"], sp["vp"],
                  sp["tcur"], sp["tcur"], sp["tprev"], sp["tprev"], sp["sk"]],
        out_specs=[sp["blk"], sp["blk"], sp["blk"]],
        out_shape=[jax.ShapeDtypeStruct((b, s, D_MODEL), BF16)] + [jax.ShapeDtypeStruct((b, s, D_MODEL), F32)] * 2,
        compiler_params=_cp(("parallel", "parallel", "parallel")),
    )(proj3, proj3, proj3, proj3, proj3, proj3, cos, sin, cos, sin, sinks)


def _swa_bwd(proj3, cos, sin, sinks, o3, lse3, dy3, *, name):
    b, s, _ = proj3.shape

    def body(q_ref, z_ref, kc_ref, kp_ref, vc_ref, vp_ref, cq_ref, sq_ref, cp_ref, sp_ref, sk_ref,
             o_ref, lse_ref, dy_ref, dq_ref, dz_ref, dkc_ref, dkp_ref, dvc_ref, dvp_ref, dsk_ref):
        i = pl.program_id(2)
        first = jnp.logical_and(pl.program_id(1) == 0, i == 0)

        @pl.when(first)
        def _():
            dsk_ref[...] = jnp.zeros_like(dsk_ref)

        cq_all, sq_all, cpv, spv, kc_all, kp0, vc_all, vp0 = _swa_keys(
            pl.program_id(0), kc_ref, kp_ref, vc_ref, vp_ref, cq_ref, sq_ref, cp_ref, sp_ref)
        lo = lax.broadcasted_iota(jnp.int32, (CHUNK, LANES), 1) < HEAD_DIM
        lane1 = lax.broadcasted_iota(jnp.int32, (1, LANES), 1)
        sink4 = _swa_sink4(sk_ref[...])
        zero = jnp.zeros((CHUNK, LANES), F32)
        dks = [zero] * (_QB + 1)
        dvs = [zero] * (_QB + 1)
        dsk_row = jnp.zeros((1, LANES), F32)
        for u in range(_QB):
            rs = slice(CHUNK * u, CHUNK * (u + 1))
            ps = slice(CHUNK * (u - 1), CHUNK * u)
            cq, sq = cq_all[rs], sq_all[rs]
            kp, vp = (kp0, vp0) if u == 0 else (kc_all[ps], vc_all[ps])
            kk = jnp.concatenate([kp, kc_all[rs]], axis=0)
            vv = jnp.concatenate([vp, vc_all[rs]], axis=0)
            q4 = _swa_stack([_rope(q_ref[rs, LANES * pp:LANES * (pp + 1)].astype(F32), cq, sq) * _SCALE
                             for pp in range(2)], lo)
            dos, lses = [], []
            for pp in range(2):
                ls = slice(LANES * pp, LANES * (pp + 1))
                z = z_ref[rs, ls].astype(F32)
                sz = _sigmoid(z)
                dy = dy_ref[rs, ls]
                dos.append(dy * (z * sz))
                dz_ref[rs, ls] = (dy * o_ref[rs, ls] * (sz * (1.0 + z * (1.0 - sz)))).astype(BF16)
                lse = lse_ref[rs, ls]
                lses += [_col(lse, 0), _col(lse, HEAD_DIM)]
            do4 = _swa_stack(dos, lo)
            lse4 = jnp.concatenate(lses, axis=0)
            pr = jnp.exp(jnp.where(_swa_mask4(True if u > 0 else i > 0), _dot_nt(q4, kk), _NEG) - lse4)
            dp = _dot_nt(do4, vv)
            dl = jnp.sum(pr * dp, axis=1, keepdims=True)
            ds = (pr * (dp - dl)).astype(BF16)
            dsink = -jnp.exp(sink4 - lse4) * dl
            for j in range(4):
                dsk_row = dsk_row + jnp.where(
                    lane1 == j, jnp.sum(dsink[CHUNK * j:CHUNK * (j + 1)], axis=0, keepdims=True), 0.0)
            dq4 = _dot(ds, kk)
            dkk = _dot_tn(ds, q4)
            dvv = _dot_tn(pr.astype(BF16), do4)
            dks[u], dks[u + 1] = dks[u] + dkk[:CHUNK], dks[u + 1] + dkk[CHUNK:]
            dvs[u], dvs[u + 1] = dvs[u] + dvv[:CHUNK], dvs[u + 1] + dvv[CHUNK:]
            for pp in range(2):
                h0 = slice(2 * CHUNK * pp, 2 * CHUNK * pp + CHUNK)
                h1 = slice(2 * CHUNK * pp + CHUNK, 2 * CHUNK * (pp + 1))
                dq_ref[rs, LANES * pp:LANES * (pp + 1)] = _rope(
                    jnp.where(lo, dq4[h0], dq4[h1]) * _SCALE, cq, -sq).astype(BF16)
        fold = lambda v: v + pltpu.roll(v, HEAD_DIM, 1)
        dkp_ref[...] = fold(_rope(dks[0], cpv, -spv))
        dvp_ref[...] = fold(dvs[0])
        for u in range(_QB):
            rs = slice(CHUNK * u, CHUNK * (u + 1))
            dkc_ref[rs, :] = fold(_rope(dks[u + 1], cq_all[rs], -sq_all[rs]))
            dvc_ref[rs, :] = fold(dvs[u + 1])
        dsk_ref[...] += dsk_row

    sp = _swa_specs(lambda g, bi, i: (bi, g, i))
    kv_shape = jax.ShapeDtypeStruct((b, s, 512), F32)
    kvp_shape = jax.ShapeDtypeStruct((b, s // _QB, 512), F32)
    return pl.pallas_call(
        body, name=name, grid=(N_GROUPS, b, s // _QROWS),
        in_specs=[sp["q"], sp["z"], sp["kc"], sp["kp"], sp["vc"], sp["vp"],
                  sp["tcur"], sp["tcur"], sp["tprev"], sp["tprev"], sp["sk"], sp["blk"], sp["blk"], sp["blk"]],
        out_specs=[sp["blk"], sp["blk"], sp["kcur"], sp["kstep"], sp["kcur"], sp["kstep"], sp["sk"]],
        out_shape=[jax.ShapeDtypeStruct((b, s, D_MODEL), BF16), jax.ShapeDtypeStruct((b, s, D_MODEL), BF16),
                   kv_shape, kvp_shape, kv_shape, kvp_shape, jax.ShapeDtypeStruct((N_GROUPS, 1, LANES), F32)],
        compiler_params=_cp(("arbitrary", "arbitrary", "arbitrary")),
    )(proj3, proj3, proj3, proj3, proj3, proj3, cos, sin, cos, sin, sinks, o3, lse3, dy3)


def _swa_fold(dkc, dkp, dvc, dvp, *, name):
    b, s, _ = dkc.shape
    ns = s // _QROWS

    def body(kc_ref, kp_ref, vc_ref, vp_ref, dk_ref, dv_ref):
        has_next = pl.program_id(1) < ns - 1
        lo = lax.broadcasted_iota(jnp.int32, (_QROWS, LANES), 1) < HEAD_DIM
        row = lax.broadcasted_iota(jnp.int32, (_QROWS, 512), 0)
        last_block = jnp.logical_and(row >= _QROWS - CHUNK, has_next)
        for cur, nxt, out in ((kc_ref, kp_ref, dk_ref), (vc_ref, vp_ref, dv_ref)):
            tot = cur[...] + jnp.where(last_block, jnp.tile(nxt[...], (_QB, 1)), 0.0)
            for j in range(2):
                out[:, LANES * j:LANES * (j + 1)] = jnp.where(
                    lo, tot[:, 256 * j:256 * j + LANES], tot[:, 256 * j + LANES:256 * (j + 1)]).astype(BF16)

    cur = pl.BlockSpec((None, _QROWS, 512), lambda bi, i: (bi, i, 0))
    nxt = pl.BlockSpec((None, CHUNK, 512), lambda bi, i: (bi, jnp.minimum(i + 1, ns - 1), 0))
    out = pl.BlockSpec((None, _QROWS, 256), lambda bi, i: (bi, i, 0))
    sh = jax.ShapeDtypeStruct((b, s, 256), BF16)
    return pl.pallas_call(
        body, name=name, grid=(b, ns), in_specs=[cur, nxt, cur, nxt], out_specs=[out, out], out_shape=[sh, sh],
        compiler_params=_cp(("parallel", "parallel")),
    )(dkc, dkp, dvc, dvp)


def _branch_fwd(ys, proj, gb, wp, wo, x, *, name, tm=256):
    t = proj.shape[0]
    g0 = _PAD_COLS["gates"][0] // D_MODEL

    def body(g_ref, a_ref, b_ref, c_ref, gb_ref, wp_ref, wo_ref, x_ref, ba_ref, bb_ref, bc_ref, m_ref, xn_ref):
        acc = None
        for i, (y, br) in enumerate(((a_ref, ba_ref), (b_ref, bb_ref), (c_ref, bc_ref))):
            bri = _dot(y[...], wp_ref[i])
            br[...] = bri
            gate = _sigmoid(g_ref[:, D_MODEL * i:D_MODEL * (i + 1)].astype(F32) + gb_ref[i:i + 1, :])
            acc = gate * bri if acc is None else acc + gate * bri
        mb = acc.astype(BF16)
        m_ref[...] = mb
        xn_ref[...] = x_ref[...] + _dot(mb, wo_ref[...])

    row = pl.BlockSpec((tm, D_MODEL), lambda i: (i, 0))
    rowf = jax.ShapeDtypeStruct((t, D_MODEL), F32)
    outs = pl.pallas_call(
        body, name=name, grid=(t // tm,),
        in_specs=[pl.BlockSpec((tm, 3 * D_MODEL), lambda i: (i, g0)), row, row, row,
                  pl.BlockSpec((3, D_MODEL), lambda i: (0, 0)),
                  pl.BlockSpec((3, D_MODEL, D_MODEL), lambda i: (0, 0, 0)),
                  pl.BlockSpec((D_MODEL, D_MODEL), lambda i: (0, 0)), row],
        out_specs=[row, row, row, row, row],
        out_shape=[rowf, rowf, rowf, jax.ShapeDtypeStruct((t, D_MODEL), BF16), rowf],
        compiler_params=_cp(("parallel",)),
    )(proj, ys[0], ys[1], ys[2], gb, wp, wo, x)
    return outs[:3], outs[3], outs[4]


def _branch_bwd(dx, proj, br, gb, wp, wo, *, name, tm=256):
    t = proj.shape[0]
    g0 = _PAD_COLS["gates"][0] // D_MODEL

    def body(g_ref, a_ref, b_ref, c_ref, gb_ref, wp_ref, wo_ref, dx_ref,
             da_ref, db_ref, dc_ref, dg_ref, dgb_ref, ya_ref, yb_ref, yc_ref):
        @pl.when(pl.program_id(0) == 0)
        def _():
            dgb_ref[...] = jnp.zeros_like(dgb_ref)

        dmv = _dot_nt(dx_ref[...].astype(BF16), wo_ref[...])
        for i, (r, dr, dy) in enumerate(((a_ref, da_ref, ya_ref), (b_ref, db_ref, yb_ref), (c_ref, dc_ref, yc_ref))):
            gate = _sigmoid(g_ref[:, D_MODEL * i:D_MODEL * (i + 1)].astype(F32) + gb_ref[i:i + 1, :])
            dbr = (dmv * gate).astype(BF16)
            dr[...] = dbr
            dg = dmv * r[...] * gate * (1.0 - gate)
            dg_ref[:, D_MODEL * i:D_MODEL * (i + 1)] = dg.astype(BF16)
            dgb_ref[i:i + 1, :] += jnp.sum(dg, axis=0, keepdims=True)
            dy[...] = _dot_nt(dbr, wp_ref[i])

    row = pl.BlockSpec((tm, D_MODEL), lambda i: (i, 0))
    rowb = jax.ShapeDtypeStruct((t, D_MODEL), BF16)
    rowf = jax.ShapeDtypeStruct((t, D_MODEL), F32)
    outs = pl.pallas_call(
        body, name=name, grid=(t // tm,),
        in_specs=[pl.BlockSpec((tm, 3 * D_MODEL), lambda i: (i, g0)), row, row, row,
                  pl.BlockSpec((3, D_MODEL), lambda i: (0, 0)),
                  pl.BlockSpec((3, D_MODEL, D_MODEL), lambda i: (0, 0, 0)),
                  pl.BlockSpec((D_MODEL, D_MODEL), lambda i: (0, 0)), row],
        out_specs=[row, row, row, pl.BlockSpec((tm, 3 * D_MODEL), lambda i: (i, 0)),
                   pl.BlockSpec((8, D_MODEL), lambda i: (0, 0)), row, row, row],
        out_shape=[rowb, rowb, rowb, jax.ShapeDtypeStruct((t, 3 * D_MODEL), BF16),
                   jax.ShapeDtypeStruct((8, D_MODEL), F32), rowf, rowf, rowf],
        compiler_params=_cp(("arbitrary",)),
    )(proj, br[0], br[1], br[2], gb, wp, wo, dx)
    return outs[:3], outs[3], outs[4], outs[5:]


def _rope_tables(s):
    pos = jnp.arange(s, dtype=F32)
    inv_freq = ROPE_THETA ** (-jnp.arange(0, HEAD_DIM, 2, dtype=F32) / HEAD_DIM)
    ang = pos[:, None] * inv_freq[None, :]
    cos, sin = jnp.cos(ang), jnp.sin(ang)
    return jnp.tile(cos, (1, 4)), jnp.tile(jnp.concatenate([-sin, sin], axis=1), (1, 2))


def _layer_params(wl):
    return dict(
        dtb=_group_lanes(wl["dt_bias"]), alog=_group_lanes(wl["a_log"]), dsk=_group_lanes(wl["d_skip"]),
        nw=wl["ssm_norm_w"].reshape(N_GROUPS, 1, 256), sinks=_group_lanes(wl["sinks"]),
        fb=jnp.pad(wl["f_bias"], (0, LANES - N_HEADS)).reshape(1, LANES))


def _layer_fwd(x, wl, tabs, bsz, li, tb):
    t = x.shape[0]
    s = t // bsz
    cos, sin = tabs
    lp = _layer_params(wl)
    n = lambda k: f"l{li}_{k}"
    h, h_t = _rms_fwd(x, wl["norm_w"], name=n("rms_fwd"))
    proj = _mm(h, wl["w_in"], tm=1024, tn=1536, tk=1024, out_dtype=BF16, name=n("mm_proj"))
    proj3 = proj.reshape(bsz, s, N_PAD)
    g0, gw = _PAD_COLS["a_dt"][0], _PAD_COLS["a_dt"][1] + _PAD_COLS["c_f"][1]
    gates3 = _mm(h, wl["w_in"][:, g0:g0 + gw], tm=1024, tn=gw, tk=1024, name=n("mm_gates")).reshape(bsz, s, gw)
    xact3 = _conv_fwd(proj3, wl["conv_w"], wl["conv_b"], name=n("conv_fwd"))
    ya3, ypre3, hst = _ssd_fwd(proj3, gates3, xact3, lp["dtb"], lp["alog"], lp["dsk"], lp["nw"], name=n("ssd_fwd"))
    yb3, ob3, lseb3 = _swa_fwd(proj3, cos, sin, lp["sinks"], name=n("swa_fwd"))
    cum = _fgate_fwd(gates3, lp["fb"], name=n("fgate_fwd"))
    cum_t = _ck_rep(cum)
    yc3, oc3, statc3 = _foxt_fwd(proj3, cum_t, name=n("fox_fwd"), tb=tb)
    ys = [v.reshape(t, D_MODEL) for v in (ya3, yb3, yc3)]
    br, merged, x_new = _branch_fwd(ys, proj, wl["gate_bias"], wl["w_proj"], wl["w_out"], x, name=n("branch_fwd"))
    saved = dict(x=x, h_t=h_t, proj=proj, gates3=gates3, xact3=xact3, ypre3=ypre3, hst=hst, ob3=ob3, lseb3=lseb3,
                 cum_t=cum_t, oc3=oc3, statc3=statc3, ys=ys, br=br, merged=merged, lp=lp)
    return x_new, saved


def _layer_bwd(dx, wl, sv, tabs, bsz, li, tb):
    t = dx.shape[0]
    s = t // bsz
    cos, sin = tabs
    lp = sv["lp"]
    n = lambda k: f"l{li}_{k}"
    proj = sv["proj"]
    proj3 = proj.reshape(bsz, s, N_PAD)
    g = {}
    g["w_out"] = _mm(sv["merged"], dx, ta=True, tm=1024, tn=1024, tk=512, name=n("mm_dwout"))
    dbr, dgates, dgb, dys = _branch_bwd(dx, proj, sv["br"], wl["gate_bias"], wl["w_proj"], wl["w_out"],
                                        name=n("branch_bwd"))
    g["gate_bias"] = dgb[:3]
    g["w_proj"] = jnp.stack([_mm(sv["ys"][i], dbr[i], ta=True, tm=1024, tn=1024, tk=512, name=n(f"mm_dwproj{i}"))
                             for i in range(3)])
    dy3 = [v.reshape(bsz, s, D_MODEL) for v in dys]

    (dact, daz, dadt, ddtb, dalog, ddsk, dnw) = _ssd_bwd(
        proj3, sv["gates3"], sv["xact3"], lp["dtb"], lp["alog"], lp["dsk"], lp["nw"], sv["ypre3"], sv["hst"], dy3[0],
        name=n("ssd_bwd"))
    g["dt_bias"], g["a_log"], g["d_skip"] = _ungroup_lanes(ddtb), _ungroup_lanes(dalog), _ungroup_lanes(ddsk)
    g["ssm_norm_w"] = dnw.reshape(D_MODEL)
    dxbc, dwb = _conv_bwd(proj3, wl["conv_w"], wl["conv_b"], dact, name=n("conv_bwd"))
    g["conv_w"], g["conv_b"] = dwb[:CONV_WIDTH], dwb[CONV_WIDTH]

    dbq, dbz, dkc, dkp, dvc, dvp, dsk = _swa_bwd(proj3, cos, sin, lp["sinks"], sv["ob3"],
                                                 sv["lseb3"], dy3[1], name=n("swa_bwd"))
    g["sinks"] = _ungroup_lanes(dsk)

    dbk, dbv = _swa_fold(dkc, dkp, dvc, dvp, name=n("swa_fold"))

    dcz, do3, stats = _foxt_prep(proj3, sv["oc3"], sv["statc3"], dy3[2], name=n("fox_prep"), tb=tb)
    dqt, dck, dcv, csum = _foxt_bwd(proj3, sv["cum_t"], do3, stats, name=n("fox_bwd"), tb=tb)
    dcq = jnp.transpose(dqt, (0, 2, 4, 1, 3)).reshape(bsz, s, D_MODEL)
    dcf, dfb = _fgate_bwd(sv["gates3"], lp["fb"], csum, name=n("fgate_bwd"))
    g["f_bias"] = dfb[0, :N_HEADS]

    parts = {"gates": dgates.reshape(bsz, s, 3 * D_MODEL), "xbc": dxbc, "a_z": daz, "b_q": dbq, "b_z": dbz,
             "c_q": dcq, "c_k": dck, "c_v": dcv, "c_z": dcz, "b_k": dbk, "b_v": dbv, "a_dt": dadt, "c_f": dcf}
    dproj = jnp.concatenate([parts[name].astype(BF16) for name, _ in _PAD_ORDER]
                            + [jnp.zeros((bsz, s, N_PAD - N_USED), BF16)], axis=2).reshape(t, N_PAD)
    dh = _mm(dproj, wl["w_in"], tb=True, tm=1024, tn=1024, tk=1536, name=n("mm_dh"))
    g["w_in"] = _unpad_w_in(_mm(sv["h_t"], dproj, tm=1024, tn=768, tk=2048, name=n("mm_dwin")))
    dx_in, dnorm = _rms_bwd(sv["x"], wl["norm_w"], dh, dx, name=n("rms_bwd"))
    g["norm_w"] = dnorm[0]
    return dx_in, g


def _local_step(x, target, wls, final_norm_w, tb=1024):
    bsz, s, d = x.shape
    t = bsz * s
    tabs = _rope_tables(s)
    xc = x.reshape(t, d)
    saved = []
    for li, wl in enumerate(wls):
        xc, sv = _layer_fwd(xc, wl, tabs, bsz, li, tb)
        saved.append(sv)
    loss, dx, dfw = _final_loss(xc, final_norm_w, target.reshape(t, d), name="final_loss")
    grads = [None] * len(wls)
    for li in reversed(range(len(wls))):
        dx, grads[li] = _layer_bwd(dx, wls[li], saved[li], tabs, bsz, li, tb)
    return loss[0, 0], dx.reshape(bsz, s, d), grads, dfw[0]


_HBM = pl.BlockSpec(memory_space=pltpu.HBM)


def _chip_peers(x, y):
    return [(1 - x, y), (x, 1 - y), (1 - x, 1 - y)]


def _gather_weights(arrs, *, name):
    n = len(arrs)

    def body(*refs):
        ins, outs = refs[:n], refs[n:2 * n]
        ici_send, ici_recv, d2d_send, d2d_recv = refs[2 * n:]
        x, y, c = lax.axis_index("x"), lax.axis_index("y"), lax.axis_index("c")
        me = 2 * x + y
        peers = _chip_peers(x, y)
        sib = (x, y, 1 - c)
        sends, fwds = [], []
        for a in range(n):
            for k, (px, py) in enumerate(peers):
                cp = pltpu.make_async_remote_copy(
                    src_ref=ins[a].at[c], dst_ref=outs[a].at[me, c], send_sem=ici_send.at[a, k],
                    recv_sem=ici_recv.at[a, k], device_id=(px, py, c), device_id_type=MESH)
                cp.start()
                sends.append(cp)
        for a in range(n):
            for k, (px, py) in enumerate(peers):
                slot = 2 * px + py
                pltpu.make_async_remote_copy(
                    src_ref=ins[a].at[c], dst_ref=outs[a].at[slot, c], send_sem=ici_send.at[a, k],
                    recv_sem=ici_recv.at[a, k], device_id=(px, py, c), device_id_type=MESH).wait_recv()
                fw = pltpu.make_async_remote_copy(
                    src_ref=outs[a].at[slot, c], dst_ref=outs[a].at[slot, c], send_sem=d2d_send.at[a, k],
                    recv_sem=d2d_recv.at[a, k], device_id=sib, device_id_type=MESH)
                fw.start()
                fwds.append(fw)
        for a in range(n):
            for k, (px, py) in enumerate(peers):
                slot = 2 * px + py
                pltpu.make_async_remote_copy(
                    src_ref=outs[a].at[slot, 1 - c], dst_ref=outs[a].at[slot, 1 - c], send_sem=d2d_send.at[a, k],
                    recv_sem=d2d_recv.at[a, k], device_id=sib, device_id_type=MESH).wait_recv()
        for cp in sends + fwds:
            cp.wait_send()

    out_shape = [jax.ShapeDtypeStruct((N_CHIPS,) + a.shape, a.dtype) for a in arrs]
    return pl.pallas_call(
        body, name=name, out_shape=out_shape, in_specs=[_HBM] * n, out_specs=[_HBM] * n,
        scratch_shapes=[pltpu.SemaphoreType.DMA((n, 3)), pltpu.SemaphoreType.DMA((n, 3)),
                        pltpu.SemaphoreType.DMA((n, 3)), pltpu.SemaphoreType.DMA((n, 3))],
    )(*arrs)


def _pair_exchange(arrs, *, name):
    n = len(arrs)

    def body(*refs):
        ins, outs = refs[:n], refs[n:2 * n]
        send, recv = refs[2 * n:]
        x, y, c = lax.axis_index("x"), lax.axis_index("y"), lax.axis_index("c")
        sib = (x, y, 1 - c)
        cps = []
        for a in range(n):
            for k in range(N_CHIPS):
                cp = pltpu.make_async_remote_copy(
                    src_ref=ins[a].at[k, 1 - c], dst_ref=outs[a].at[k], send_sem=send.at[a, k],
                    recv_sem=recv.at[a, k], device_id=sib, device_id_type=MESH)
                cp.start()
                cps.append(cp)
        for cp in cps:
            cp.wait()

    out_shape = [jax.ShapeDtypeStruct((N_CHIPS,) + a.shape[2:], a.dtype) for a in arrs]
    return pl.pallas_call(
        body, name=name, out_shape=out_shape, in_specs=[_HBM] * n, out_specs=[_HBM] * n,
        scratch_shapes=[pltpu.SemaphoreType.DMA((n, N_CHIPS)), pltpu.SemaphoreType.DMA((n, N_CHIPS))],
    )(*arrs)


def _chip_exchange(arrs, *, name):
    n = len(arrs)

    def body(*refs):
        ins, outs = refs[:n], refs[n:2 * n]
        send, recv = refs[2 * n:]
        x, y, c = lax.axis_index("x"), lax.axis_index("y"), lax.axis_index("c")
        me = 2 * x + y
        peers = _chip_peers(x, y)
        cps = []
        for a in range(n):
            for k, (px, py) in enumerate(peers):
                cp = pltpu.make_async_remote_copy(
                    src_ref=ins[a].at[2 * px + py], dst_ref=outs[a].at[me], send_sem=send.at[a, k],
                    recv_sem=recv.at[a, k], device_id=(px, py, c), device_id_type=MESH)
                cp.start()
                cps.append(cp)
        for a in range(n):
            for k, (px, py) in enumerate(peers):
                pltpu.make_async_remote_copy(
                    src_ref=ins[a].at[2 * px + py], dst_ref=outs[a].at[2 * px + py], send_sem=send.at[a, k],
                    recv_sem=recv.at[a, k], device_id=(px, py, c), device_id_type=MESH).wait_recv()
        for cp in cps:
            cp.wait_send()

    out_shape = [jax.ShapeDtypeStruct(a.shape, a.dtype) for a in arrs]
    return pl.pallas_call(
        body, name=name, out_shape=out_shape, in_specs=[_HBM] * n, out_specs=[_HBM] * n,
        scratch_shapes=[pltpu.SemaphoreType.DMA((n, 3)), pltpu.SemaphoreType.DMA((n, 3))],
    )(*arrs)


def _pair_share(arrs, *, name):
    n = len(arrs)

    def body(*refs):
        ins, outs = refs[:n], refs[n:2 * n]
        send, recv = refs[2 * n:]
        x, y, c = lax.axis_index("x"), lax.axis_index("y"), lax.axis_index("c")
        sib = (x, y, 1 - c)
        cps = []
        for a in range(n):
            cp = pltpu.make_async_remote_copy(
                src_ref=ins[a], dst_ref=outs[a], send_sem=send.at[a], recv_sem=recv.at[a],
                device_id=sib, device_id_type=MESH)
            cp.start()
            cps.append(cp)
        for cp in cps:
            cp.wait()

    out_shape = [jax.ShapeDtypeStruct(a.shape, a.dtype) for a in arrs]
    return pl.pallas_call(
        body, name=name, out_shape=out_shape, in_specs=[_HBM] * n, out_specs=[_HBM] * n,
        scratch_shapes=[pltpu.SemaphoreType.DMA((n,)), pltpu.SemaphoreType.DMA((n,))],
    )(*arrs)


def _allreduce_small(buf, *, name):
    r = buf.shape[0]

    def body(in_ref, out_ref, land, send, recv):
        x, y, c = lax.axis_index("x"), lax.axis_index("y"), lax.axis_index("c")
        me = 4 * x + 2 * y + c
        land[me] = in_ref[...]
        cps = []
        for k in range(1, N_DEV):
            px, py, pc = x ^ ((k >> 2) & 1), y ^ ((k >> 1) & 1), c ^ (k & 1)
            cp = pltpu.make_async_remote_copy(
                src_ref=in_ref, dst_ref=land.at[me], send_sem=send.at[k - 1], recv_sem=recv.at[k - 1],
                device_id=(px, py, pc), device_id_type=MESH)
            cp.start()
            cps.append(cp)
        for k in range(1, N_DEV):
            px, py, pc = x ^ ((k >> 2) & 1), y ^ ((k >> 1) & 1), c ^ (k & 1)
            pltpu.make_async_remote_copy(
                src_ref=in_ref, dst_ref=land.at[4 * px + 2 * py + pc], send_sem=send.at[k - 1],
                recv_sem=recv.at[k - 1], device_id=(px, py, pc), device_id_type=MESH).wait_recv()
        for cp in cps:
            cp.wait_send()
        acc = land[0]
        for k in range(1, N_DEV):
            acc = acc + land[k]
        out_ref[...] = acc

    vm = pl.BlockSpec(memory_space=pltpu.VMEM)
    return pl.pallas_call(
        body, name=name, out_shape=jax.ShapeDtypeStruct((r, LANES), F32), in_specs=[vm], out_specs=vm,
        scratch_shapes=[pltpu.VMEM((N_DEV, r, LANES), F32), pltpu.SemaphoreType.DMA((N_DEV - 1,)),
                        pltpu.SemaphoreType.DMA((N_DEV - 1,))],
    )(buf)


def _row_tile(rows, cols, n_arrays, budget=20 * 1024 * 1024):
    best = 8 if rows % 8 == 0 else rows
    tr = 8
    while tr <= rows:
        if rows % tr == 0 and tr * cols * 4 * n_arrays * 2 <= budget:
            best = tr
        tr *= 2
    return best


def _add_slot_layer(full, other, *, name):
    _, _, r, cdim = full.shape
    tr = _row_tile(r, cdim, 4)

    def body(c_ref, a_ref, b_ref, o_ref, ob_ref):
        sm = a_ref[...] + b_ref[...]
        o_ref[...] = sm
        ob_ref[...] = sm.astype(BF16)

    c = lax.axis_index("c").astype(jnp.int32).reshape(1)
    blk = pl.BlockSpec((None, tr, cdim), lambda k, i, c_ref: (k, i, 0))
    return pl.pallas_call(
        body, name=name,
        grid_spec=pltpu.PrefetchScalarGridSpec(
            num_scalar_prefetch=1, grid=(N_CHIPS, r // tr),
            in_specs=[pl.BlockSpec((None, None, tr, cdim), lambda k, i, c_ref: (k, c_ref[0], i, 0)), blk],
            out_specs=[blk, blk]),
        out_shape=[jax.ShapeDtypeStruct((N_CHIPS, r, cdim), F32), jax.ShapeDtypeStruct((N_CHIPS, r, cdim), BF16)],
        compiler_params=_cp(("parallel", "parallel")),
    )(c, full, other)


def _sum_slots(parts, pair, *, name):
    _, r, cdim = parts.shape
    tr = _row_tile(r, cdim, 5)

    def body(me_ref, p_ref, own_ref, o_ref):
        me = me_ref[0]
        acc = None
        for k in range(N_CHIPS):
            term = jnp.where(me == k, own_ref[...], p_ref[k].astype(F32))
            acc = term if acc is None else acc + term
        o_ref[...] = acc

    me = (2 * lax.axis_index("x") + lax.axis_index("y")).astype(jnp.int32).reshape(1)
    return pl.pallas_call(
        body, name=name,
        grid_spec=pltpu.PrefetchScalarGridSpec(
            num_scalar_prefetch=1, grid=(r // tr,),
            in_specs=[pl.BlockSpec((N_CHIPS, tr, cdim), lambda i, me_ref: (0, i, 0)),
                      pl.BlockSpec((None, tr, cdim), lambda i, me_ref: (me_ref[0], i, 0))],
            out_specs=pl.BlockSpec((tr, cdim), lambda i, me_ref: (i, 0))),
        out_shape=jax.ShapeDtypeStruct((r, cdim), F32),
        compiler_params=_cp(("parallel",)),
    )(me, parts, pair)


def _assemble_w_proj(own, gathered, li, *, name):
    _, nb, r, cdim = own.shape

    def body(chip_ref, own_ref, slot_ref, o_ref):
        o_ref[...] = jnp.where(chip_ref[0] == pl.program_id(1), own_ref[...], slot_ref[...])

    chip = (2 * lax.axis_index("x") + lax.axis_index("y")).astype(jnp.int32).reshape(1)
    return pl.pallas_call(
        body, name=name,
        grid_spec=pltpu.PrefetchScalarGridSpec(
            num_scalar_prefetch=1, grid=(nb, N_CHIPS),
            in_specs=[pl.BlockSpec((None, None, r, cdim), lambda i, k, chip_ref: (li, i, 0, 0)),
                      pl.BlockSpec((None, None, None, r, cdim), lambda i, k, chip_ref: (k, li, i, 0, 0))],
            out_specs=pl.BlockSpec((None, r, cdim), lambda i, k, chip_ref: (i, k, 0))),
        out_shape=jax.ShapeDtypeStruct((nb, N_CHIPS * r, cdim), own.dtype),
        compiler_params=_cp(("parallel", "parallel")),
    )(chip, own, gathered)


def _adamw(w, g, m, v, *, name):
    lead, (r, cdim) = w.shape[:-2], w.shape[-2:]
    nl = len(lead)
    tr = _row_tile(r, cdim, 7)
    tc = cdim
    if tr < 64 < r and cdim % LANES == 0:
        tr, tc = r, LANES
    c1 = 1.0 - ADAM_B1 ** ADAM_STEP
    c2 = 1.0 - ADAM_B2 ** ADAM_STEP

    def body(w_ref, g_ref, m_ref, v_ref, d_ref, nm_ref, nv_ref):
        gv = g_ref[...]
        mn = ADAM_B1 * m_ref[...] + (1.0 - ADAM_B1) * gv
        vn = ADAM_B2 * v_ref[...] + (1.0 - ADAM_B2) * (gv * gv)
        nm_ref[...] = mn
        nv_ref[...] = vn
        d_ref[...] = -ADAM_LR * ((mn / c1) / (jnp.sqrt(vn / c2) + ADAM_EPS) + ADAM_WD * w_ref[...])

    blk = pl.BlockSpec((None,) * nl + (tr, tc), lambda *ids: ids[:nl] + (ids[nl], ids[nl + 1]))
    sh = jax.ShapeDtypeStruct(w.shape, F32)
    return pl.pallas_call(
        body, name=name, grid=lead + (r // tr, cdim // tc), in_specs=[blk] * 4, out_specs=[blk] * 3,
        out_shape=[sh] * 3, compiler_params=_cp(("parallel",) * (nl + 2)),
    )(w, g, m, v)


_SMALL = ("norm_w", "conv_b", "dt_bias", "a_log", "d_skip", "ssm_norm_w", "sinks", "f_bias", "final_norm_w",
          "conv_w", "gate_bias")


def _pack(vals):
    flat = jnp.concatenate([v.reshape(-1) for v in vals])
    rows = -(-flat.shape[0] // LANES)
    rows = -(-rows // 8) * 8
    return jnp.pad(flat, (0, rows * LANES - flat.shape[0])).reshape(rows, LANES)


def _unpack(buf, shapes):
    flat = buf.reshape(-1)
    out, off = [], 0
    for sh in shapes:
        sz = int(np.prod(sh))
        out.append(flat[off:off + sz].reshape(sh))
        off += sz
    return out


def kernel(x, norm_w, w_in, conv_w, conv_b, dt_bias, a_log, d_skip, ssm_norm_w, sinks, f_bias, gate_bias, w_proj, w_out, final_norm_w, loss_target, m_norm_w, m_w_in, m_conv_w, m_conv_b, m_dt_bias, m_a_log, m_d_skip, m_ssm_norm_w, m_sinks, m_f_bias, m_gate_bias, m_w_proj, m_w_out, m_final_norm_w, v_norm_w, v_w_in, v_conv_w, v_conv_b, v_dt_bias, v_a_log, v_d_skip, v_ssm_norm_w, v_sinks, v_f_bias, v_gate_bias, v_w_proj, v_w_out, v_final_norm_w):
    depth = w_in.shape[0]
    chip = 2 * lax.axis_index("x") + lax.axis_index("y")

    own = [w_in.astype(BF16), w_proj.astype(BF16), w_out.astype(BF16), conv_w, gate_bias]
    gathered = _gather_weights(own, name="gather_weights")

    def whole(a, li, axis):
        return jnp.concatenate([jnp.where(chip == k, own[a][li], gathered[a][k, li]) for k in range(N_CHIPS)],
                               axis=axis)

    wls = []
    for li in range(depth):
        wls.append(dict(
            norm_w=norm_w[li], w_in=_pad_w_in(whole(0, li, 1)),
            conv_w=whole(3, li, 1), conv_b=conv_b[li], dt_bias=dt_bias[li], a_log=a_log[li], d_skip=d_skip[li],
            ssm_norm_w=ssm_norm_w[li], sinks=sinks[li], f_bias=f_bias[li], gate_bias=whole(4, li, 1),
            w_proj=_assemble_w_proj(own[1], gathered[1], li, name=f"l{li}_assemble_w_proj"),
            w_out=whole(2, li, 0)))

    loss_part, grad_x, grads, d_final = _local_step(x, loss_target, wls, final_norm_w)
    loss = lax.psum(loss_part, ("x", "y", "c"))

    c_in = w_in.shape[2]
    r_proj = w_proj.shape[2]
    r_out = w_out.shape[1]
    full_in = jnp.stack([jnp.stack([grads[li]["w_in"][:, k * c_in:(k + 1) * c_in] for li in range(depth)])
                         for k in range(N_CHIPS)])
    full_proj = jnp.stack([jnp.stack([grads[li]["w_proj"][:, k * r_proj:(k + 1) * r_proj].reshape(-1, D_MODEL)
                                      for li in range(depth)]) for k in range(N_CHIPS)])
    full_out = jnp.stack([jnp.stack([grads[li]["w_out"][k * r_out:(k + 1) * r_out] for li in range(depth)])
                          for k in range(N_CHIPS)])
    fulls = [full_in, full_proj, full_out]
    others = _pair_exchange(fulls, name="grad_pair_exchange")
    pair = [_add_slot_layer(f, o, name=f"grad_pair_add{i}") for i, (f, o) in enumerate(zip(fulls, others))]
    parts = _chip_exchange([p[1] for p in pair], name="grad_chip_exchange")
    mine = [_sum_slots(p, pr[0], name=f"grad_slot_sum{i}") for i, (p, pr) in enumerate(zip(parts, pair))]
    theirs = _pair_share(mine, name="grad_pair_share")
    core = lax.axis_index("c")
    red_in, red_proj, red_out = [jnp.stack([jnp.where(core == li, m, t) for li in range(depth)])
                                 for m, t in zip(mine, theirs)]
    grad_w_in = red_in
    grad_w_proj = red_proj.reshape(w_proj.shape)
    grad_w_out = red_out

    small_full = {
        "norm_w": jnp.stack([g["norm_w"] for g in grads]), "conv_b": jnp.stack([g["conv_b"] for g in grads]),
        "dt_bias": jnp.stack([g["dt_bias"] for g in grads]), "a_log": jnp.stack([g["a_log"] for g in grads]),
        "d_skip": jnp.stack([g["d_skip"] for g in grads]),
        "ssm_norm_w": jnp.stack([g["ssm_norm_w"] for g in grads]),
        "sinks": jnp.stack([g["sinks"] for g in grads]), "f_bias": jnp.stack([g["f_bias"] for g in grads]),
        "final_norm_w": d_final,
        "conv_w": jnp.stack([g["conv_w"] for g in grads]), "gate_bias": jnp.stack([g["gate_bias"] for g in grads])}
    shapes = [small_full[k].shape for k in _SMALL]
    summed = _unpack(_allreduce_small(_pack([small_full[k] for k in _SMALL]), name="allreduce_small"), shapes)
    gsmall = dict(zip(_SMALL, summed))
    gsmall["conv_w"] = lax.dynamic_slice_in_dim(gsmall["conv_w"], chip * conv_w.shape[2], conv_w.shape[2], axis=2)
    gsmall["gate_bias"] = lax.dynamic_slice_in_dim(gsmall["gate_bias"], chip * gate_bias.shape[2],
                                                   gate_bias.shape[2], axis=2)

    w_small = dict(norm_w=norm_w, conv_b=conv_b, dt_bias=dt_bias, a_log=a_log, d_skip=d_skip,
                   ssm_norm_w=ssm_norm_w, sinks=sinks, f_bias=f_bias, final_norm_w=final_norm_w, conv_w=conv_w,
                   gate_bias=gate_bias)
    m_small = dict(norm_w=m_norm_w, conv_b=m_conv_b, dt_bias=m_dt_bias, a_log=m_a_log, d_skip=m_d_skip,
                   ssm_norm_w=m_ssm_norm_w, sinks=m_sinks, f_bias=m_f_bias, final_norm_w=m_final_norm_w,
                   conv_w=m_conv_w, gate_bias=m_gate_bias)
    v_small = dict(norm_w=v_norm_w, conv_b=v_conv_b, dt_bias=v_dt_bias, a_log=v_a_log, d_skip=v_d_skip,
                   ssm_norm_w=v_ssm_norm_w, sinks=v_sinks, f_bias=v_f_bias, final_norm_w=v_final_norm_w,
                   conv_w=v_conv_w, gate_bias=v_gate_bias)
    sshapes = [w_small[k].shape for k in _SMALL]
    ds, ms, vs = _adamw(_pack([w_small[k] for k in _SMALL]), _pack([gsmall[k] for k in _SMALL]),
                        _pack([m_small[k] for k in _SMALL]), _pack([v_small[k] for k in _SMALL]), name="adamw_small")
    delta = dict(zip(_SMALL, _unpack(ds, sshapes)))
    new_m = dict(zip(_SMALL, _unpack(ms, sshapes)))
    new_v = dict(zip(_SMALL, _unpack(vs, sshapes)))
    grad = dict(gsmall)
    for nm, w, g, m, v in (("w_proj", w_proj, grad_w_proj, m_w_proj, v_w_proj),
                           ("w_out", w_out, grad_w_out, m_w_out, v_w_out)):
        grad[nm] = g
        delta[nm], new_m[nm], new_v[nm] = _adamw(w, g, m, v, name=f"adamw_{nm}")
    tview = lambda a: jnp.transpose(a, (0, 2, 1))
    grad["w_in"] = grad_w_in
    delta["w_in"], new_m["w_in"], new_v["w_in"] = [
        tview(a) for a in _adamw(tview(w_in), tview(grad_w_in), tview(m_w_in), tview(v_w_in), name="adamw_w_in")]

    order = ("norm_w", "w_in", "conv_w", "conv_b", "dt_bias", "a_log", "d_skip", "ssm_norm_w", "sinks", "f_bias",
             "gate_bias", "w_proj", "w_out", "final_norm_w")
    return (loss, grad_x, *[grad[k] for k in order], *[delta[k] for k in order],
            *[new_m[k] for k in order], *[new_v[k] for k in order])
```
